```python
import math
import jax, jax.numpy as jnp
from jax import lax
import numpy as np

D_MODEL = 1024
BATCH = 8
SEQ = 4096
DEPTH = 4

N_MIXERS = 3
EXPAND = 2
D_INNER = EXPAND * D_MODEL
NORM_EPS = 1e-6

GMLP_CHUNK = 128
GMLP_GROUPS = 8

S5_GROUP = 16
S5_STATE = 64
S5_GROUPS = D_INNER // S5_GROUP
S5_DT_MIN = 1e-3
S5_DT_MAX = 1e-1

MLA_HEADS = 16
MLA_NOPE = 128
MLA_ROPE = 64
MLA_V = D_INNER // MLA_HEADS
MLA_QK_DIM = MLA_NOPE + MLA_ROPE
MLA_Q_RANK = 384
MLA_KV_RANK = 128
MLA_SCALE = MLA_QK_DIM ** -0.5
ROPE_THETA = 10000.0
ATTN_QBLOCK = 128
NEG_INF = -1e30

kernel_name = "hybrid_gmlp_s5_mla_gated"


def _rmsnorm(x, g):
    xf = x.astype(jnp.float32)
    y = xf * lax.rsqrt(jnp.mean(xf * xf, axis=-1, keepdims=True) + NORM_EPS)
    return (y * g.astype(jnp.float32)).astype(x.dtype)


def _layernorm(x, g, b):
    xf = x.astype(jnp.float32)
    mu = jnp.mean(xf, axis=-1, keepdims=True)
    xc = xf - mu
    var = jnp.mean(xc * xc, axis=-1, keepdims=True)
    y = xc * lax.rsqrt(var + NORM_EPS) * g.astype(jnp.float32) + b.astype(jnp.float32)
    return y.astype(x.dtype)


def _rope(x, cos, sin):
    half = x.shape[-1] // 2
    x1 = x[..., :half].astype(jnp.float32)
    x2 = x[..., half:].astype(jnp.float32)
    return jnp.concatenate([x1 * cos - x2 * sin, x2 * cos + x1 * sin], axis=-1).astype(x.dtype)


def _gmlp_mixer(h, w_in, ln_g, ln_b, w_s, b_s, w_out):
    bsz, seq, _ = h.shape
    u, v, z = jnp.split(h @ w_in, 3, axis=-1)
    u = jax.nn.gelu(u)
    v = _layernorm(jax.nn.gelu(v), ln_g, ln_b)
    v = v.reshape(bsz, seq // GMLP_CHUNK, GMLP_CHUNK, GMLP_GROUPS, D_INNER // GMLP_GROUPS)
    causal = jnp.tril(jnp.ones((GMLP_CHUNK, GMLP_CHUNK), dtype=bool))
    w = jnp.where(causal[None], w_s, jnp.zeros((), w_s.dtype))
    s = jnp.einsum('gts,bcsgd->bctgd', w, v) + b_s.T[:, :, None]
    s = s.reshape(bsz, seq, D_INNER)
    return (u * s * jax.nn.silu(z)) @ w_out


def _s5_combine(left, right):
    a_l, b_l = left
    a_r, b_r = right
    return a_r * a_l, a_r * b_l + b_r


def _s5_mixer(h, w_in, a_re, a_im, log_step, b_re, b_im, c_re, c_im, d_skip, w_glu, b_glu, w_out):
    bsz, seq, _ = h.shape
    u, z = jnp.split(h @ w_in, 2, axis=-1)
    uf = u.astype(jnp.float32).reshape(bsz, seq, S5_GROUPS, S5_GROUP)
    lam = lax.complex(a_re.astype(jnp.float32), a_im.astype(jnp.float32))
    step = jnp.exp(log_step.astype(jnp.float32))[:, None]
    lam_bar = jnp.exp(lam * step)
    bmat = lax.complex(b_re.astype(jnp.float32), b_im.astype(jnp.float32))
    b_bar = ((lam_bar - 1.0) / lam)[..., None] * bmat
    bu = lax.complex(jnp.einsum('blgh,gph->lbgp', uf, jnp.real(b_bar)),
                     jnp.einsum('blgh,gph->lbgp', uf, jnp.imag(b_bar)))
    a_elems = jnp.broadcast_to(lam_bar, (seq, 1, S5_GROUPS, S5_STATE))
    _, xs = lax.associative_scan(_s5_combine, (a_elems, bu), axis=0)
    y = (jnp.einsum('lbgp,ghp->blgh', jnp.real(xs), c_re.astype(jnp.float32))
         - jnp.einsum('lbgp,ghp->blgh', jnp.imag(xs), c_im.astype(jnp.float32)))
    y = y + d_skip.astype(jnp.float32).reshape(S5_GROUPS, S5_GROUP) * uf
    y = jax.nn.gelu(y.reshape(bsz, seq, D_INNER)).astype(h.dtype)
    y = y * jax.nn.sigmoid(y @ w_glu + b_glu)
    return (y * jax.nn.silu(z)) @ w_out


def _mla_mixer(h, positions, w_in, q_norm_g, w_uq, kv_norm_g, w_ukv, w_out):
    bsz, seq, _ = h.shape
    c_q, c_kv, k_r, z = jnp.split(
        h @ w_in, [MLA_Q_RANK, MLA_Q_RANK + MLA_KV_RANK, MLA_Q_RANK + MLA_KV_RANK + MLA_ROPE], axis=-1)
    q = (_rmsnorm(c_q, q_norm_g) @ w_uq).reshape(bsz, seq, MLA_HEADS, MLA_QK_DIM)
    q_nope, q_rope = q[..., :MLA_NOPE], q[..., MLA_NOPE:]
    kv = (_rmsnorm(c_kv, kv_norm_g) @ w_ukv).reshape(bsz, seq, MLA_HEADS, MLA_NOPE + MLA_V)
    k_nope, v = kv[..., :MLA_NOPE], kv[..., MLA_NOPE:]
    inv_freq = ROPE_THETA ** (-jnp.arange(0, MLA_ROPE, 2, dtype=jnp.float32) / MLA_ROPE)
    ang = positions.astype(jnp.float32)[..., None] * inv_freq
    cos, sin = jnp.cos(ang), jnp.sin(ang)
    q_rope = _rope(q_rope, cos[:, :, None], sin[:, :, None])
    k_r = _rope(k_r, cos, sin)
    n_blk = seq // ATTN_QBLOCK

    def to_blocks(t):
        return t.reshape(bsz, n_blk, ATTN_QBLOCK, *t.shape[2:]).swapaxes(0, 1)

    kpos = jnp.arange(seq)

    def attend(args):
        qn_b, qr_b, blk = args
        s = (jnp.einsum('bqhd,bkhd->bhqk', qn_b, k_nope)
             + jnp.einsum('bqhd,bkd->bhqk', qr_b, k_r)).astype(jnp.float32) * MLA_SCALE
        qpos = blk * ATTN_QBLOCK + jnp.arange(ATTN_QBLOCK)
        s = jnp.where(kpos[None, :] <= qpos[:, None], s, NEG_INF)
        p = jax.nn.softmax(s, axis=-1).astype(v.dtype)
        return jnp.einsum('bhqk,bkhd->bqhd', p, v)

    o = lax.map(attend, (to_blocks(q_nope), to_blocks(q_rope), jnp.arange(n_blk)))
    o = o.swapaxes(0, 1).reshape(bsz, seq, MLA_HEADS * MLA_V)
    return (o * jax.nn.silu(z)) @ w_out


def _gain(key, n):
    return 1.0 + 0.02 * jax.random.normal(key, (n,), jnp.float32)


def _normal(key, shape, scale):
    return jax.random.normal(key, shape, jnp.float32) * scale


def _gmlp_params(key, p):
    k = jax.random.split(key, 7)
    return {
        p + 'norm_g': _gain(k[0], D_MODEL),
        p + 'w_in': _normal(k[1], (D_MODEL, 3 * D_INNER), D_MODEL ** -0.5),
        p + 'ln_g': _gain(k[2], D_INNER),
        p + 'ln_b': _normal(k[3], (D_INNER,), 0.02),
        p + 'w_s': _normal(k[4], (GMLP_GROUPS, GMLP_CHUNK, GMLP_CHUNK), GMLP_CHUNK ** -0.5),
        p + 'b_s': 1.0 + _normal(k[5], (GMLP_GROUPS, GMLP_CHUNK), 0.02),
        p + 'w_out': _normal(k[6], (D_INNER, D_MODEL), D_INNER ** -0.5),
    }


def _s5_params(key, p):
    k = jax.random.split(key, 14)
    n = jnp.arange(S5_STATE, dtype=jnp.float32)
    return {
        p + 'norm_g': _gain(k[0], D_MODEL),
        p + 'w_in': _normal(k[1], (D_MODEL, 2 * D_INNER), D_MODEL ** -0.5),
        p + 'a_re': -0.5 + _normal(k[2], (S5_GROUPS, S5_STATE), 0.01),
        p + 'a_im': math.pi * n[None, :] + _normal(k[3], (S5_GROUPS, S5_STATE), 0.01),
        p + 'log_step': jax.random.uniform(k[4], (S5_GROUPS,), jnp.float32,
                                           math.log(S5_DT_MIN), math.log(S5_DT_MAX)),
        p + 'b_re': _normal(k[5], (S5_GROUPS, S5_STATE, S5_GROUP), (2 * S5_GROUP) ** -0.5),
        p + 'b_im': _normal(k[6], (S5_GROUPS, S5_STATE, S5_GROUP), (2 * S5_GROUP) ** -0.5),
        p + 'c_re': _normal(k[7], (S5_GROUPS, S5_GROUP, S5_STATE), (2 * S5_STATE) ** -0.5),
        p + 'c_im': _normal(k[8], (S5_GROUPS, S5_GROUP, S5_STATE), (2 * S5_STATE) ** -0.5),
        p + 'd_skip': _normal(k[9], (D_INNER,), 1.0),
        p + 'w_glu': _normal(k[10], (D_INNER, D_INNER), D_INNER ** -0.5),
        p + 'b_glu': _normal(k[11], (D_INNER,), 0.02),
        p + 'w_out': _normal(k[12], (D_INNER, D_MODEL), D_INNER ** -0.5),
    }


def _mla_params(key, p):
    k = jax.random.split(key, 7)
    return {
        p + 'norm_g': _gain(k[0], D_MODEL),
        p + 'w_in': _normal(k[1], (D_MODEL, MLA_Q_RANK + MLA_KV_RANK + MLA_ROPE + D_INNER), D_MODEL ** -0.5),
        p + 'q_norm_g': _gain(k[2], MLA_Q_RANK),
        p + 'w_uq': _normal(k[3], (MLA_Q_RANK, MLA_HEADS * MLA_QK_DIM), MLA_Q_RANK ** -0.5),
        p + 'kv_norm_g': _gain(k[4], MLA_KV_RANK),
        p + 'w_ukv': _normal(k[5], (MLA_KV_RANK, MLA_HEADS * (MLA_NOPE + MLA_V)), MLA_KV_RANK ** -0.5),
        p + 'w_out': _normal(k[6], (MLA_HEADS * MLA_V, D_MODEL), D_INNER ** -0.5),
    }


def _fwd_setup_inputs(seed: int = 0) -> dict:
    key = jax.random.key(seed)
    keys = jax.random.split(key, DEPTH + 4)
    x = jax.random.normal(keys[0], (BATCH, SEQ, D_MODEL), jnp.float32)
    offset = jax.random.randint(keys[1], (BATCH, 1), 0, 1024, dtype=jnp.int32)
    positions = offset + jnp.arange(SEQ, dtype=jnp.int32)[None, :]
    inputs = {'x': x, 'positions': positions}
    makers = (_gmlp_params, _s5_params, _mla_params)
    for i in range(DEPTH):
        inputs.update(makers[i % N_MIXERS](keys[2 + i], 'l%d_' % i))
    inputs['final_norm_g'] = _gain(keys[2 + DEPTH], D_MODEL)
    return inputs


def _fwd_reference(x, positions,
              l0_norm_g, l0_w_in, l0_ln_g, l0_ln_b, l0_w_s, l0_b_s, l0_w_out,
              l1_norm_g, l1_w_in, l1_a_re, l1_a_im, l1_log_step, l1_b_re, l1_b_im, l1_c_re, l1_c_im,
              l1_d_skip, l1_w_glu, l1_b_glu, l1_w_out,
              l2_norm_g, l2_w_in, l2_q_norm_g, l2_w_uq, l2_kv_norm_g, l2_w_ukv, l2_w_out,
              l3_norm_g, l3_w_in, l3_ln_g, l3_ln_b, l3_w_s, l3_b_s, l3_w_out,
              final_norm_g):
    layer_params = (
        (l0_norm_g, (l0_w_in, l0_ln_g, l0_ln_b, l0_w_s, l0_b_s, l0_w_out)),
        (l1_norm_g, (l1_w_in, l1_a_re, l1_a_im, l1_log_step, l1_b_re, l1_b_im, l1_c_re, l1_c_im,
                     l1_d_skip, l1_w_glu, l1_b_glu, l1_w_out)),
        (l2_norm_g, (l2_w_in, l2_q_norm_g, l2_w_uq, l2_kv_norm_g, l2_w_ukv, l2_w_out)),
        (l3_norm_g, (l3_w_in, l3_ln_g, l3_ln_b, l3_w_s, l3_b_s, l3_w_out)),
    )
    h = x
    for i in range(DEPTH):
        norm_g, p = layer_params[i]
        hn = _rmsnorm(h, norm_g)
        kind = i % N_MIXERS
        if kind == 0:
            y = _gmlp_mixer(hn, *p)
        elif kind == 1:
            y = _s5_mixer(hn, *p)
        else:
            y = _mla_mixer(hn, positions, *p)
        h = h + y
    return _rmsnorm(h, final_norm_g)


import jax as _jax
import jax.numpy as _jnp

TWIN_FORMAT = 'train_step'
FWD_PARAMS = ['x', 'positions', 'l0_norm_g', 'l0_w_in', 'l0_ln_g', 'l0_ln_b', 'l0_w_s', 'l0_b_s', 'l0_w_out', 'l1_norm_g', 'l1_w_in', 'l1_a_re', 'l1_a_im', 'l1_log_step', 'l1_b_re', 'l1_b_im', 'l1_c_re', 'l1_c_im', 'l1_d_skip', 'l1_w_glu', 'l1_b_glu', 'l1_w_out', 'l2_norm_g', 'l2_w_in', 'l2_q_norm_g', 'l2_w_uq', 'l2_kv_norm_g', 'l2_w_ukv', 'l2_w_out', 'l3_norm_g', 'l3_w_in', 'l3_ln_g', 'l3_ln_b', 'l3_w_s', 'l3_b_s', 'l3_w_out', 'final_norm_g']
TWIN_WEIGHTS = ['l0_norm_g', 'l0_w_in', 'l0_ln_g', 'l0_ln_b', 'l0_w_s', 'l0_b_s', 'l0_w_out', 'l1_norm_g', 'l1_w_in', 'l1_a_re', 'l1_a_im', 'l1_log_step', 'l1_b_re', 'l1_b_im', 'l1_c_re', 'l1_c_im', 'l1_d_skip', 'l1_w_glu', 'l1_b_glu', 'l1_w_out', 'l2_norm_g', 'l2_w_in', 'l2_q_norm_g', 'l2_w_uq', 'l2_kv_norm_g', 'l2_w_ukv', 'l2_w_out', 'l3_norm_g', 'l3_w_in', 'l3_ln_g', 'l3_ln_b', 'l3_w_s', 'l3_b_s', 'l3_w_out', 'final_norm_g']
TWIN_DIFF_INPUT = 'x'
TWIN_INPUTS = ['x', 'positions', 'l0_norm_g', 'l0_w_in', 'l0_ln_g', 'l0_ln_b', 'l0_w_s', 'l0_b_s', 'l0_w_out', 'l1_norm_g', 'l1_w_in', 'l1_a_re', 'l1_a_im', 'l1_log_step', 'l1_b_re', 'l1_b_im', 'l1_c_re', 'l1_c_im', 'l1_d_skip', 'l1_w_glu', 'l1_b_glu', 'l1_w_out', 'l2_norm_g', 'l2_w_in', 'l2_q_norm_g', 'l2_w_uq', 'l2_kv_norm_g', 'l2_w_ukv', 'l2_w_out', 'l3_norm_g', 'l3_w_in', 'l3_ln_g', 'l3_ln_b', 'l3_w_s', 'l3_b_s', 'l3_w_out', 'final_norm_g', 'loss_target', 'm_l0_norm_g', 'm_l0_w_in', 'm_l0_ln_g', 'm_l0_ln_b', 'm_l0_w_s', 'm_l0_b_s', 'm_l0_w_out', 'm_l1_norm_g', 'm_l1_w_in', 'm_l1_a_re', 'm_l1_a_im', 'm_l1_log_step', 'm_l1_b_re', 'm_l1_b_im', 'm_l1_c_re', 'm_l1_c_im', 'm_l1_d_skip', 'm_l1_w_glu', 'm_l1_b_glu', 'm_l1_w_out', 'm_l2_norm_g', 'm_l2_w_in', 'm_l2_q_norm_g', 'm_l2_w_uq', 'm_l2_kv_norm_g', 'm_l2_w_ukv', 'm_l2_w_out', 'm_l3_norm_g', 'm_l3_w_in', 'm_l3_ln_g', 'm_l3_ln_b', 'm_l3_w_s', 'm_l3_b_s', 'm_l3_w_out', 'm_final_norm_g', 'v_l0_norm_g', 'v_l0_w_in', 'v_l0_ln_g', 'v_l0_ln_b', 'v_l0_w_s', 'v_l0_b_s', 'v_l0_w_out', 'v_l1_norm_g', 'v_l1_w_in', 'v_l1_a_re', 'v_l1_a_im', 'v_l1_log_step', 'v_l1_b_re', 'v_l1_b_im', 'v_l1_c_re', 'v_l1_c_im', 'v_l1_d_skip', 'v_l1_w_glu', 'v_l1_b_glu', 'v_l1_w_out', 'v_l2_norm_g', 'v_l2_w_in', 'v_l2_q_norm_g', 'v_l2_w_uq', 'v_l2_kv_norm_g', 'v_l2_w_ukv', 'v_l2_w_out', 'v_l3_norm_g', 'v_l3_w_in', 'v_l3_ln_g', 'v_l3_ln_b', 'v_l3_w_s', 'v_l3_b_s', 'v_l3_w_out', 'v_final_norm_g']
TWIN_OUTPUTS = ['loss', 'grad_x', 'grad_l0_norm_g', 'grad_l0_w_in', 'grad_l0_ln_g', 'grad_l0_ln_b', 'grad_l0_w_s', 'grad_l0_b_s', 'grad_l0_w_out', 'grad_l1_norm_g', 'grad_l1_w_in', 'grad_l1_a_re', 'grad_l1_a_im', 'grad_l1_log_step', 'grad_l1_b_re', 'grad_l1_b_im', 'grad_l1_c_re', 'grad_l1_c_im', 'grad_l1_d_skip', 'grad_l1_w_glu', 'grad_l1_b_glu', 'grad_l1_w_out', 'grad_l2_norm_g', 'grad_l2_w_in', 'grad_l2_q_norm_g', 'grad_l2_w_uq', 'grad_l2_kv_norm_g', 'grad_l2_w_ukv', 'grad_l2_w_out', 'grad_l3_norm_g', 'grad_l3_w_in', 'grad_l3_ln_g', 'grad_l3_ln_b', 'grad_l3_w_s', 'grad_l3_b_s', 'grad_l3_w_out', 'grad_final_norm_g', 'delta_l0_norm_g', 'delta_l0_w_in', 'delta_l0_ln_g', 'delta_l0_ln_b', 'delta_l0_w_s', 'delta_l0_b_s', 'delta_l0_w_out', 'delta_l1_norm_g', 'delta_l1_w_in', 'delta_l1_a_re', 'delta_l1_a_im', 'delta_l1_log_step', 'delta_l1_b_re', 'delta_l1_b_im', 'delta_l1_c_re', 'delta_l1_c_im', 'delta_l1_d_skip', 'delta_l1_w_glu', 'delta_l1_b_glu', 'delta_l1_w_out', 'delta_l2_norm_g', 'delta_l2_w_in', 'delta_l2_q_norm_g', 'delta_l2_w_uq', 'delta_l2_kv_norm_g', 'delta_l2_w_ukv', 'delta_l2_w_out', 'delta_l3_norm_g', 'delta_l3_w_in', 'delta_l3_ln_g', 'delta_l3_ln_b', 'delta_l3_w_s', 'delta_l3_b_s', 'delta_l3_w_out', 'delta_final_norm_g', 'new_m_l0_norm_g', 'new_m_l0_w_in', 'new_m_l0_ln_g', 'new_m_l0_ln_b', 'new_m_l0_w_s', 'new_m_l0_b_s', 'new_m_l0_w_out', 'new_m_l1_norm_g', 'new_m_l1_w_in', 'new_m_l1_a_re', 'new_m_l1_a_im', 'new_m_l1_log_step', 'new_m_l1_b_re', 'new_m_l1_b_im', 'new_m_l1_c_re', 'new_m_l1_c_im', 'new_m_l1_d_skip', 'new_m_l1_w_glu', 'new_m_l1_b_glu', 'new_m_l1_w_out', 'new_m_l2_norm_g', 'new_m_l2_w_in', 'new_m_l2_q_norm_g', 'new_m_l2_w_uq', 'new_m_l2_kv_norm_g', 'new_m_l2_w_ukv', 'new_m_l2_w_out', 'new_m_l3_norm_g', 'new_m_l3_w_in', 'new_m_l3_ln_g', 'new_m_l3_ln_b', 'new_m_l3_w_s', 'new_m_l3_b_s', 'new_m_l3_w_out', 'new_m_final_norm_g', 'new_v_l0_norm_g', 'new_v_l0_w_in', 'new_v_l0_ln_g', 'new_v_l0_ln_b', 'new_v_l0_w_s', 'new_v_l0_b_s', 'new_v_l0_w_out', 'new_v_l1_norm_g', 'new_v_l1_w_in', 'new_v_l1_a_re', 'new_v_l1_a_im', 'new_v_l1_log_step', 'new_v_l1_b_re', 'new_v_l1_b_im', 'new_v_l1_c_re', 'new_v_l1_c_im', 'new_v_l1_d_skip', 'new_v_l1_w_glu', 'new_v_l1_b_glu', 'new_v_l1_w_out', 'new_v_l2_norm_g', 'new_v_l2_w_in', 'new_v_l2_q_norm_g', 'new_v_l2_w_uq', 'new_v_l2_kv_norm_g', 'new_v_l2_w_ukv', 'new_v_l2_w_out', 'new_v_l3_norm_g', 'new_v_l3_w_in', 'new_v_l3_ln_g', 'new_v_l3_ln_b', 'new_v_l3_w_s', 'new_v_l3_b_s', 'new_v_l3_w_out', 'new_v_final_norm_g']
TWIN_LEAF_KINDS = {'loss': 'loss', 'grad_x': 'grad_x', 'grad_l0_norm_g': 'grad_w', 'grad_l0_w_in': 'grad_w', 'grad_l0_ln_g': 'grad_w', 'grad_l0_ln_b': 'grad_w', 'grad_l0_w_s': 'grad_w', 'grad_l0_b_s': 'grad_w', 'grad_l0_w_out': 'grad_w', 'grad_l1_norm_g': 'grad_w', 'grad_l1_w_in': 'grad_w', 'grad_l1_a_re': 'grad_w', 'grad_l1_a_im': 'grad_w', 'grad_l1_log_step': 'grad_w', 'grad_l1_b_re': 'grad_w', 'grad_l1_b_im': 'grad_w', 'grad_l1_c_re': 'grad_w', 'grad_l1_c_im': 'grad_w', 'grad_l1_d_skip': 'grad_w', 'grad_l1_w_glu': 'grad_w', 'grad_l1_b_glu': 'grad_w', 'grad_l1_w_out': 'grad_w', 'grad_l2_norm_g': 'grad_w', 'grad_l2_w_in': 'grad_w', 'grad_l2_q_norm_g': 'grad_w', 'grad_l2_w_uq': 'grad_w', 'grad_l2_kv_norm_g': 'grad_w', 'grad_l2_w_ukv': 'grad_w', 'grad_l2_w_out': 'grad_w', 'grad_l3_norm_g': 'grad_w', 'grad_l3_w_in': 'grad_w', 'grad_l3_ln_g': 'grad_w', 'grad_l3_ln_b': 'grad_w', 'grad_l3_w_s': 'grad_w', 'grad_l3_b_s': 'grad_w', 'grad_l3_w_out': 'grad_w', 'grad_final_norm_g': 'grad_w', 'delta_l0_norm_g': 'delta_w', 'delta_l0_w_in': 'delta_w', 'delta_l0_ln_g': 'delta_w', 'delta_l0_ln_b': 'delta_w', 'delta_l0_w_s': 'delta_w', 'delta_l0_b_s': 'delta_w', 'delta_l0_w_out': 'delta_w', 'delta_l1_norm_g': 'delta_w', 'delta_l1_w_in': 'delta_w', 'delta_l1_a_re': 'delta_w', 'delta_l1_a_im': 'delta_w', 'delta_l1_log_step': 'delta_w', 'delta_l1_b_re': 'delta_w', 'delta_l1_b_im': 'delta_w', 'delta_l1_c_re': 'delta_w', 'delta_l1_c_im': 'delta_w', 'delta_l1_d_skip': 'delta_w', 'delta_l1_w_glu': 'delta_w', 'delta_l1_b_glu': 'delta_w', 'delta_l1_w_out': 'delta_w', 'delta_l2_norm_g': 'delta_w', 'delta_l2_w_in': 'delta_w', 'delta_l2_q_norm_g': 'delta_w', 'delta_l2_w_uq': 'delta_w', 'delta_l2_kv_norm_g': 'delta_w', 'delta_l2_w_ukv': 'delta_w', 'delta_l2_w_out': 'delta_w', 'delta_l3_norm_g': 'delta_w', 'delta_l3_w_in': 'delta_w', 'delta_l3_ln_g': 'delta_w', 'delta_l3_ln_b': 'delta_w', 'delta_l3_w_s': 'delta_w', 'delta_l3_b_s': 'delta_w', 'delta_l3_w_out': 'delta_w', 'delta_final_norm_g': 'delta_w', 'new_m_l0_norm_g': 'new_m', 'new_m_l0_w_in': 'new_m', 'new_m_l0_ln_g': 'new_m', 'new_m_l0_ln_b': 'new_m', 'new_m_l0_w_s': 'new_m', 'new_m_l0_b_s': 'new_m', 'new_m_l0_w_out': 'new_m', 'new_m_l1_norm_g': 'new_m', 'new_m_l1_w_in': 'new_m', 'new_m_l1_a_re': 'new_m', 'new_m_l1_a_im': 'new_m', 'new_m_l1_log_step': 'new_m', 'new_m_l1_b_re': 'new_m', 'new_m_l1_b_im': 'new_m', 'new_m_l1_c_re': 'new_m', 'new_m_l1_c_im': 'new_m', 'new_m_l1_d_skip': 'new_m', 'new_m_l1_w_glu': 'new_m', 'new_m_l1_b_glu': 'new_m', 'new_m_l1_w_out': 'new_m', 'new_m_l2_norm_g': 'new_m', 'new_m_l2_w_in': 'new_m', 'new_m_l2_q_norm_g': 'new_m', 'new_m_l2_w_uq': 'new_m', 'new_m_l2_kv_norm_g': 'new_m', 'new_m_l2_w_ukv': 'new_m', 'new_m_l2_w_out': 'new_m', 'new_m_l3_norm_g': 'new_m', 'new_m_l3_w_in': 'new_m', 'new_m_l3_ln_g': 'new_m', 'new_m_l3_ln_b': 'new_m', 'new_m_l3_w_s': 'new_m', 'new_m_l3_b_s': 'new_m', 'new_m_l3_w_out': 'new_m', 'new_m_final_norm_g': 'new_m', 'new_v_l0_norm_g': 'new_v', 'new_v_l0_w_in': 'new_v', 'new_v_l0_ln_g': 'new_v', 'new_v_l0_ln_b': 'new_v', 'new_v_l0_w_s': 'new_v', 'new_v_l0_b_s': 'new_v', 'new_v_l0_w_out': 'new_v', 'new_v_l1_norm_g': 'new_v', 'new_v_l1_w_in': 'new_v', 'new_v_l1_a_re': 'new_v', 'new_v_l1_a_im': 'new_v', 'new_v_l1_log_step': 'new_v', 'new_v_l1_b_re': 'new_v', 'new_v_l1_b_im': 'new_v', 'new_v_l1_c_re': 'new_v', 'new_v_l1_c_im': 'new_v', 'new_v_l1_d_skip': 'new_v', 'new_v_l1_w_glu': 'new_v', 'new_v_l1_b_glu': 'new_v', 'new_v_l1_w_out': 'new_v', 'new_v_l2_norm_g': 'new_v', 'new_v_l2_w_in': 'new_v', 'new_v_l2_q_norm_g': 'new_v', 'new_v_l2_w_uq': 'new_v', 'new_v_l2_kv_norm_g': 'new_v', 'new_v_l2_w_ukv': 'new_v', 'new_v_l2_w_out': 'new_v', 'new_v_l3_norm_g': 'new_v', 'new_v_l3_w_in': 'new_v', 'new_v_l3_ln_g': 'new_v', 'new_v_l3_ln_b': 'new_v', 'new_v_l3_w_s': 'new_v', 'new_v_l3_b_s': 'new_v', 'new_v_l3_w_out': 'new_v', 'new_v_final_norm_g': 'new_v'}


def _forward(args):
    return _fwd_reference(*[args[k] for k in FWD_PARAMS])


def _output_shape():
    def fwd():
        inp = _fwd_setup_inputs(0)
        return _fwd_reference(*[inp[k] for k in FWD_PARAMS])
    out = _jax.eval_shape(fwd)
    return out.shape, out.dtype

N_MICROBATCH = 1
ADAM_LR = 0.001
ADAM_B1 = 0.9
ADAM_B2 = 0.999
ADAM_EPS = 1e-08
ADAM_WD = 0.01
ADAM_STEP = 10
PER_EXAMPLE_BATCH_AXIS = {'x': 0, 'positions': 0, 'loss_target': 0}
SHARED_INPUTS = []
_WEIGHT_DTYPES = {'l0_norm_g': _jnp.float32, 'l0_w_in': _jnp.float32, 'l0_ln_g': _jnp.float32, 'l0_ln_b': _jnp.float32, 'l0_w_s': _jnp.float32, 'l0_b_s': _jnp.float32, 'l0_w_out': _jnp.float32, 'l1_norm_g': _jnp.float32, 'l1_w_in': _jnp.float32, 'l1_a_re': _jnp.float32, 'l1_a_im': _jnp.float32, 'l1_log_step': _jnp.float32, 'l1_b_re': _jnp.float32, 'l1_b_im': _jnp.float32, 'l1_c_re': _jnp.float32, 'l1_c_im': _jnp.float32, 'l1_d_skip': _jnp.float32, 'l1_w_glu': _jnp.float32, 'l1_b_glu': _jnp.float32, 'l1_w_out': _jnp.float32, 'l2_norm_g': _jnp.float32, 'l2_w_in': _jnp.float32, 'l2_q_norm_g': _jnp.float32, 'l2_w_uq': _jnp.float32, 'l2_kv_norm_g': _jnp.float32, 'l2_w_ukv': _jnp.float32, 'l2_w_out': _jnp.float32, 'l3_norm_g': _jnp.float32, 'l3_w_in': _jnp.float32, 'l3_ln_g': _jnp.float32, 'l3_ln_b': _jnp.float32, 'l3_w_s': _jnp.float32, 'l3_b_s': _jnp.float32, 'l3_w_out': _jnp.float32, 'final_norm_g': _jnp.float32}
MOMENT_SCALE = {'l0_norm_g': 1.406132e-01, 'l0_w_in': 5.688298e-02, 'l0_ln_g': 3.473679e-02, 'l0_ln_b': 3.623426e-02, 'l0_w_s': 4.936780e-02, 'l0_b_s': 7.365093e-02, 'l0_w_out': 8.721552e-02, 'l1_norm_g': 5.340170e-02, 'l1_w_in': 2.651513e-02, 'l1_a_re': 1.339621e-03, 'l1_a_im': 1.344415e-03, 'l1_log_step': 9.341391e-01, 'l1_b_re': 8.620473e-04, 'l1_b_im': 8.756558e-04, 'l1_c_re': 1.746059e-03, 'l1_c_im': 1.732722e-03, 'l1_d_skip': 2.846632e-02, 'l1_w_glu': 7.314408e-03, 'l1_b_glu': 1.157139e-02, 'l1_w_out': 3.588596e-02, 'l2_norm_g': 4.364924e-02, 'l2_w_in': 2.680846e-02, 'l2_q_norm_g': 3.063353e-02, 'l2_w_uq': 1.105135e-02, 'l2_kv_norm_g': 8.783706e-02, 'l2_w_ukv': 1.445393e-02, 'l2_w_out': 2.403640e-02, 'l3_norm_g': 1.144583e-01, 'l3_w_in': 4.496384e-02, 'l3_ln_g': 2.805758e-02, 'l3_ln_b': 2.796336e-02, 'l3_w_s': 3.933874e-02, 'l3_b_s': 5.721313e-02, 'l3_w_out': 6.900865e-02, 'final_norm_g': 3.200926e+01}


def _to_microbatches(a, axis):
    t = _jnp.moveaxis(a, axis, 0)
    t = t.reshape((N_MICROBATCH, t.shape[0] // N_MICROBATCH) + t.shape[1:])
    return _jnp.moveaxis(t, 1, axis + 1)


def setup_inputs(seed: int = 0) -> dict:
    inp = _fwd_setup_inputs(seed)
    key = _jax.random.fold_in(_jax.random.key(seed), 7919)
    shape, _ = _output_shape()
    out = dict(inp)
    out["loss_target"] = _jax.random.normal(_jax.random.fold_in(key, 0), shape, _jnp.float32)
    for i, name in enumerate(TWIN_WEIGHTS):
        w = inp[name].astype(_jnp.float32)
        if MOMENT_SCALE is None:
            s = _jnp.sqrt(_jnp.mean(_jnp.square(w)) + 1e-30)
        else:
            s = MOMENT_SCALE[name]
        km, kv = _jax.random.split(_jax.random.fold_in(key, i + 1))
        out[name] = w
        out["m_" + name] = s * _jax.random.normal(km, w.shape, _jnp.float32)
        out["v_" + name] = (s * s) * _jax.random.uniform(kv, w.shape, _jnp.float32, 0.5, 1.5)
    if N_MICROBATCH > 1:
        for name, axis in PER_EXAMPLE_BATCH_AXIS.items():
            out[name] = _to_microbatches(out[name], axis)
    return {'x': out['x'], 'positions': out['positions'], 'l0_norm_g': out['l0_norm_g'], 'l0_w_in': out['l0_w_in'], 'l0_ln_g': out['l0_ln_g'], 'l0_ln_b': out['l0_ln_b'], 'l0_w_s': out['l0_w_s'], 'l0_b_s': out['l0_b_s'], 'l0_w_out': out['l0_w_out'], 'l1_norm_g': out['l1_norm_g'], 'l1_w_in': out['l1_w_in'], 'l1_a_re': out['l1_a_re'], 'l1_a_im': out['l1_a_im'], 'l1_log_step': out['l1_log_step'], 'l1_b_re': out['l1_b_re'], 'l1_b_im': out['l1_b_im'], 'l1_c_re': out['l1_c_re'], 'l1_c_im': out['l1_c_im'], 'l1_d_skip': out['l1_d_skip'], 'l1_w_glu': out['l1_w_glu'], 'l1_b_glu': out['l1_b_glu'], 'l1_w_out': out['l1_w_out'], 'l2_norm_g': out['l2_norm_g'], 'l2_w_in': out['l2_w_in'], 'l2_q_norm_g': out['l2_q_norm_g'], 'l2_w_uq': out['l2_w_uq'], 'l2_kv_norm_g': out['l2_kv_norm_g'], 'l2_w_ukv': out['l2_w_ukv'], 'l2_w_out': out['l2_w_out'], 'l3_norm_g': out['l3_norm_g'], 'l3_w_in': out['l3_w_in'], 'l3_ln_g': out['l3_ln_g'], 'l3_ln_b': out['l3_ln_b'], 'l3_w_s': out['l3_w_s'], 'l3_b_s': out['l3_b_s'], 'l3_w_out': out['l3_w_out'], 'final_norm_g': out['final_norm_g'], 'loss_target': out['loss_target'], 'm_l0_norm_g': out['m_l0_norm_g'], 'm_l0_w_in': out['m_l0_w_in'], 'm_l0_ln_g': out['m_l0_ln_g'], 'm_l0_ln_b': out['m_l0_ln_b'], 'm_l0_w_s': out['m_l0_w_s'], 'm_l0_b_s': out['m_l0_b_s'], 'm_l0_w_out': out['m_l0_w_out'], 'm_l1_norm_g': out['m_l1_norm_g'], 'm_l1_w_in': out['m_l1_w_in'], 'm_l1_a_re': out['m_l1_a_re'], 'm_l1_a_im': out['m_l1_a_im'], 'm_l1_log_step': out['m_l1_log_step'], 'm_l1_b_re': out['m_l1_b_re'], 'm_l1_b_im': out['m_l1_b_im'], 'm_l1_c_re': out['m_l1_c_re'], 'm_l1_c_im': out['m_l1_c_im'], 'm_l1_d_skip': out['m_l1_d_skip'], 'm_l1_w_glu': out['m_l1_w_glu'], 'm_l1_b_glu': out['m_l1_b_glu'], 'm_l1_w_out': out['m_l1_w_out'], 'm_l2_norm_g': out['m_l2_norm_g'], 'm_l2_w_in': out['m_l2_w_in'], 'm_l2_q_norm_g': out['m_l2_q_norm_g'], 'm_l2_w_uq': out['m_l2_w_uq'], 'm_l2_kv_norm_g': out['m_l2_kv_norm_g'], 'm_l2_w_ukv': out['m_l2_w_ukv'], 'm_l2_w_out': out['m_l2_w_out'], 'm_l3_norm_g': out['m_l3_norm_g'], 'm_l3_w_in': out['m_l3_w_in'], 'm_l3_ln_g': out['m_l3_ln_g'], 'm_l3_ln_b': out['m_l3_ln_b'], 'm_l3_w_s': out['m_l3_w_s'], 'm_l3_b_s': out['m_l3_b_s'], 'm_l3_w_out': out['m_l3_w_out'], 'm_final_norm_g': out['m_final_norm_g'], 'v_l0_norm_g': out['v_l0_norm_g'], 'v_l0_w_in': out['v_l0_w_in'], 'v_l0_ln_g': out['v_l0_ln_g'], 'v_l0_ln_b': out['v_l0_ln_b'], 'v_l0_w_s': out['v_l0_w_s'], 'v_l0_b_s': out['v_l0_b_s'], 'v_l0_w_out': out['v_l0_w_out'], 'v_l1_norm_g': out['v_l1_norm_g'], 'v_l1_w_in': out['v_l1_w_in'], 'v_l1_a_re': out['v_l1_a_re'], 'v_l1_a_im': out['v_l1_a_im'], 'v_l1_log_step': out['v_l1_log_step'], 'v_l1_b_re': out['v_l1_b_re'], 'v_l1_b_im': out['v_l1_b_im'], 'v_l1_c_re': out['v_l1_c_re'], 'v_l1_c_im': out['v_l1_c_im'], 'v_l1_d_skip': out['v_l1_d_skip'], 'v_l1_w_glu': out['v_l1_w_glu'], 'v_l1_b_glu': out['v_l1_b_glu'], 'v_l1_w_out': out['v_l1_w_out'], 'v_l2_norm_g': out['v_l2_norm_g'], 'v_l2_w_in': out['v_l2_w_in'], 'v_l2_q_norm_g': out['v_l2_q_norm_g'], 'v_l2_w_uq': out['v_l2_w_uq'], 'v_l2_kv_norm_g': out['v_l2_kv_norm_g'], 'v_l2_w_ukv': out['v_l2_w_ukv'], 'v_l2_w_out': out['v_l2_w_out'], 'v_l3_norm_g': out['v_l3_norm_g'], 'v_l3_w_in': out['v_l3_w_in'], 'v_l3_ln_g': out['v_l3_ln_g'], 'v_l3_ln_b': out['v_l3_ln_b'], 'v_l3_w_s': out['v_l3_w_s'], 'v_l3_b_s': out['v_l3_b_s'], 'v_l3_w_out': out['v_l3_w_out'], 'v_final_norm_g': out['v_final_norm_g']}


def _loss(weights, diff, rest, loss_target):
    with _jax.named_scope("forward"):
        args = {**rest, TWIN_DIFF_INPUT: diff, **{k: w.astype(_WEIGHT_DTYPES[k]) for k, w in weights.items()}}
        y = _forward(args)
    with _jax.named_scope("loss_head"):
        err = _jnp.square(y.astype(_jnp.float32) - loss_target)
        return 0.5 * _jnp.sum(_jnp.mean(err, axis=-1)) if err.ndim else 0.5 * err


def _adamw(w, g, m, v):
    m = ADAM_B1 * m + (1.0 - ADAM_B1) * g
    v = ADAM_B2 * v + (1.0 - ADAM_B2) * _jnp.square(g)
    m_hat = m / (1.0 - ADAM_B1 ** ADAM_STEP)
    v_hat = v / (1.0 - ADAM_B2 ** ADAM_STEP)
    delta = -ADAM_LR * (m_hat / (_jnp.sqrt(v_hat) + ADAM_EPS) + ADAM_WD * w)
    return delta, m, v


def reference(x, positions, l0_norm_g, l0_w_in, l0_ln_g, l0_ln_b, l0_w_s, l0_b_s, l0_w_out, l1_norm_g, l1_w_in, l1_a_re, l1_a_im, l1_log_step, l1_b_re, l1_b_im, l1_c_re, l1_c_im, l1_d_skip, l1_w_glu, l1_b_glu, l1_w_out, l2_norm_g, l2_w_in, l2_q_norm_g, l2_w_uq, l2_kv_norm_g, l2_w_ukv, l2_w_out, l3_norm_g, l3_w_in, l3_ln_g, l3_ln_b, l3_w_s, l3_b_s, l3_w_out, final_norm_g, loss_target, m_l0_norm_g, m_l0_w_in, m_l0_ln_g, m_l0_ln_b, m_l0_w_s, m_l0_b_s, m_l0_w_out, m_l1_norm_g, m_l1_w_in, m_l1_a_re, m_l1_a_im, m_l1_log_step, m_l1_b_re, m_l1_b_im, m_l1_c_re, m_l1_c_im, m_l1_d_skip, m_l1_w_glu, m_l1_b_glu, m_l1_w_out, m_l2_norm_g, m_l2_w_in, m_l2_q_norm_g, m_l2_w_uq, m_l2_kv_norm_g, m_l2_w_ukv, m_l2_w_out, m_l3_norm_g, m_l3_w_in, m_l3_ln_g, m_l3_ln_b, m_l3_w_s, m_l3_b_s, m_l3_w_out, m_final_norm_g, v_l0_norm_g, v_l0_w_in, v_l0_ln_g, v_l0_ln_b, v_l0_w_s, v_l0_b_s, v_l0_w_out, v_l1_norm_g, v_l1_w_in, v_l1_a_re, v_l1_a_im, v_l1_log_step, v_l1_b_re, v_l1_b_im, v_l1_c_re, v_l1_c_im, v_l1_d_skip, v_l1_w_glu, v_l1_b_glu, v_l1_w_out, v_l2_norm_g, v_l2_w_in, v_l2_q_norm_g, v_l2_w_uq, v_l2_kv_norm_g, v_l2_w_ukv, v_l2_w_out, v_l3_norm_g, v_l3_w_in, v_l3_ln_g, v_l3_ln_b, v_l3_w_s, v_l3_b_s, v_l3_w_out, v_final_norm_g):
    given = dict(x=x, positions=positions, l0_norm_g=l0_norm_g, l0_w_in=l0_w_in, l0_ln_g=l0_ln_g, l0_ln_b=l0_ln_b, l0_w_s=l0_w_s, l0_b_s=l0_b_s, l0_w_out=l0_w_out, l1_norm_g=l1_norm_g, l1_w_in=l1_w_in, l1_a_re=l1_a_re, l1_a_im=l1_a_im, l1_log_step=l1_log_step, l1_b_re=l1_b_re, l1_b_im=l1_b_im, l1_c_re=l1_c_re, l1_c_im=l1_c_im, l1_d_skip=l1_d_skip, l1_w_glu=l1_w_glu, l1_b_glu=l1_b_glu, l1_w_out=l1_w_out, l2_norm_g=l2_norm_g, l2_w_in=l2_w_in, l2_q_norm_g=l2_q_norm_g, l2_w_uq=l2_w_uq, l2_kv_norm_g=l2_kv_norm_g, l2_w_ukv=l2_w_ukv, l2_w_out=l2_w_out, l3_norm_g=l3_norm_g, l3_w_in=l3_w_in, l3_ln_g=l3_ln_g, l3_ln_b=l3_ln_b, l3_w_s=l3_w_s, l3_b_s=l3_b_s, l3_w_out=l3_w_out, final_norm_g=final_norm_g, loss_target=loss_target, m_l0_norm_g=m_l0_norm_g, m_l0_w_in=m_l0_w_in, m_l0_ln_g=m_l0_ln_g, m_l0_ln_b=m_l0_ln_b, m_l0_w_s=m_l0_w_s, m_l0_b_s=m_l0_b_s, m_l0_w_out=m_l0_w_out, m_l1_norm_g=m_l1_norm_g, m_l1_w_in=m_l1_w_in, m_l1_a_re=m_l1_a_re, m_l1_a_im=m_l1_a_im, m_l1_log_step=m_l1_log_step, m_l1_b_re=m_l1_b_re, m_l1_b_im=m_l1_b_im, m_l1_c_re=m_l1_c_re, m_l1_c_im=m_l1_c_im, m_l1_d_skip=m_l1_d_skip, m_l1_w_glu=m_l1_w_glu, m_l1_b_glu=m_l1_b_glu, m_l1_w_out=m_l1_w_out, m_l2_norm_g=m_l2_norm_g, m_l2_w_in=m_l2_w_in, m_l2_q_norm_g=m_l2_q_norm_g, m_l2_w_uq=m_l2_w_uq, m_l2_kv_norm_g=m_l2_kv_norm_g, m_l2_w_ukv=m_l2_w_ukv, m_l2_w_out=m_l2_w_out, m_l3_norm_g=m_l3_norm_g, m_l3_w_in=m_l3_w_in, m_l3_ln_g=m_l3_ln_g, m_l3_ln_b=m_l3_ln_b, m_l3_w_s=m_l3_w_s, m_l3_b_s=m_l3_b_s, m_l3_w_out=m_l3_w_out, m_final_norm_g=m_final_norm_g, v_l0_norm_g=v_l0_norm_g, v_l0_w_in=v_l0_w_in, v_l0_ln_g=v_l0_ln_g, v_l0_ln_b=v_l0_ln_b, v_l0_w_s=v_l0_w_s, v_l0_b_s=v_l0_b_s, v_l0_w_out=v_l0_w_out, v_l1_norm_g=v_l1_norm_g, v_l1_w_in=v_l1_w_in, v_l1_a_re=v_l1_a_re, v_l1_a_im=v_l1_a_im, v_l1_log_step=v_l1_log_step, v_l1_b_re=v_l1_b_re, v_l1_b_im=v_l1_b_im, v_l1_c_re=v_l1_c_re, v_l1_c_im=v_l1_c_im, v_l1_d_skip=v_l1_d_skip, v_l1_w_glu=v_l1_w_glu, v_l1_b_glu=v_l1_b_glu, v_l1_w_out=v_l1_w_out, v_l2_norm_g=v_l2_norm_g, v_l2_w_in=v_l2_w_in, v_l2_q_norm_g=v_l2_q_norm_g, v_l2_w_uq=v_l2_w_uq, v_l2_kv_norm_g=v_l2_kv_norm_g, v_l2_w_ukv=v_l2_w_ukv, v_l2_w_out=v_l2_w_out, v_l3_norm_g=v_l3_norm_g, v_l3_w_in=v_l3_w_in, v_l3_ln_g=v_l3_ln_g, v_l3_ln_b=v_l3_ln_b, v_l3_w_s=v_l3_w_s, v_l3_b_s=v_l3_b_s, v_l3_w_out=v_l3_w_out, v_final_norm_g=v_final_norm_g)
    weights = {n: given[n] for n in TWIN_WEIGHTS}
    shared = {n: given[n] for n in SHARED_INPUTS}
    per_example = {n: given[n] for n in ['x', 'positions']}
    grad_fn = _jax.value_and_grad(_loss, argnums=(0, 1))

    def one_microbatch(ex, loss_target):
        ex = dict(ex)
        diff = ex.pop(TWIN_DIFF_INPUT)
        return grad_fn(weights, diff, {**shared, **ex}, loss_target)

    if N_MICROBATCH == 1:
        loss, (grad_w, grad_x) = one_microbatch(per_example, given["loss_target"])
    else:
        def body(carry, xs):
            loss_sum, grad_sum = carry
            l_k, (gw_k, gx_k) = one_microbatch(xs[0], xs[1])
            with _jax.named_scope("update"):
                return (loss_sum + l_k, _jax.tree.map(_jnp.add, grad_sum, gw_k)), gx_k

        init = (_jnp.zeros((), _jnp.float32), _jax.tree.map(_jnp.zeros_like, weights))
        (loss, grad_w), grad_x = _jax.lax.scan(body, init, (per_example, given["loss_target"]))
    with _jax.named_scope("update"):
        delta_w, new_m, new_v = {}, {}, {}
        for n in TWIN_WEIGHTS:
            delta_w[n], new_m[n], new_v[n] = _adamw(weights[n], grad_w[n], given["m_" + n], given["v_" + n])
    return (loss, grad_x, *[grad_w[n] for n in TWIN_WEIGHTS], *[delta_w[n] for n in TWIN_WEIGHTS],
            *[new_m[n] for n in TWIN_WEIGHTS], *[new_v[n] for n in TWIN_WEIGHTS])
```

```python
import functools
import math

import jax
import jax.numpy as jnp
import numpy as np
from jax import lax
from jax.experimental import pallas as pl
from jax.experimental.pallas import tpu as pltpu

F32 = jnp.float32
BF16 = jnp.bfloat16
MESH = pl.DeviceIdType.MESH
VMEM_LIMIT_BYTES = 56 * 1024 * 1024
LANES = 128
PACK_W = 1024
ROW_TILE = 256

NORM_EPS = 1e-6
N_CHIPS = 4
GMLP_CHUNK = 128
GMLP_GROUPS = 8
S5_GROUPS = 128
S5_GROUP = 16
S5_STATE = 64
S5_SB = 16
S5_SEG = 8
MLA_HEADS = 16
MLA_NOPE = 128
MLA_ROPE = 64
MLA_Q_RANK = 384
MLA_KV_RANK = 128
MLA_SCALE = (MLA_NOPE + MLA_ROPE) ** -0.5
ROPE_THETA = 10000.0
NEG_INF = -1e30
ADAM_LR, ADAM_B1, ADAM_B2, ADAM_EPS, ADAM_WD, ADAM_STEP = 0.001, 0.9, 0.999, 1e-08, 0.01, 10

DN_NN = (((1,), (0,)), ((), ()))
DN_NT = (((1,), (1,)), ((), ()))
DN_TN = (((0,), (0,)), ((), ()))


def _cparams(sem):
    return pltpu.CompilerParams(dimension_semantics=sem, vmem_limit_bytes=VMEM_LIMIT_BYTES)


def _pick(n, cands=(512, 384, 256, 128)):
    for c in cands:
        if n % c == 0:
            return c
    return n


def _pick_rows(r, cap=512):
    return max(t for t in range(8, cap + 1, 8) if r % t == 0)


def _dot(a, b, dn):
    return lax.dot_general(a.astype(BF16), b.astype(BF16), dn, preferred_element_type=F32)


def _sigmoid(x):
    return 1.0 / (1.0 + jnp.exp(-x))


def _gelu(x):
    c = math.sqrt(2.0 / math.pi)
    t = jnp.tanh(c * (x + 0.044715 * x * x * x))
    return 0.5 * x * (1.0 + t)


def _gelu_grad(x):
    c = math.sqrt(2.0 / math.pi)
    t = jnp.tanh(c * (x + 0.044715 * x * x * x))
    return 0.5 * (1.0 + t) + 0.5 * x * (1.0 - t * t) * c * (1.0 + 3.0 * 0.044715 * x * x)


def _silu(z):
    return z * _sigmoid(z)


def _silu_grad(z):
    s = _sigmoid(z)
    return s * (1.0 + z * (1.0 - s))


def matmul(a, b, mode, name, out_dtype=F32, add=None):
    if mode == "nn":
        (m, k), n = a.shape, b.shape[1]
    elif mode == "nt":
        (m, k), n = a.shape, b.shape[0]
    else:
        (k, m), n = a.shape, b.shape[1]
    tm, tn, tk = _pick(m), _pick(n), _pick(k)
    nk = k // tk
    dn = {"nn": DN_NN, "nt": DN_NT, "tn": DN_TN}[mode]

    def body(*refs):
        if add is None:
            a_ref, b_ref, o_ref, acc = refs
        else:
            a_ref, b_ref, add_ref, o_ref, acc = refs
        kk = pl.program_id(2)

        @pl.when(kk == 0)
        def _():
            acc[...] = jnp.zeros_like(acc)

        acc[...] += _dot(a_ref[...], b_ref[...], dn)

        @pl.when(kk == nk - 1)
        def _():
            r = acc[...]
            if add is not None:
                r = r + add_ref[...].astype(F32)
            o_ref[...] = r.astype(out_dtype)

    a_spec = pl.BlockSpec((tk, tm), lambda i, j, q: (q, i)) if mode == "tn" else pl.BlockSpec((tm, tk), lambda i, j, q: (i, q))
    b_spec = pl.BlockSpec((tn, tk), lambda i, j, q: (j, q)) if mode == "nt" else pl.BlockSpec((tk, tn), lambda i, j, q: (q, j))
    o_spec = pl.BlockSpec((tm, tn), lambda i, j, q: (i, j))
    in_specs = [a_spec, b_spec] + ([o_spec] if add is not None else [])
    args = (a, b) + ((add,) if add is not None else ())
    return pl.pallas_call(
        body, name=name, grid=(m // tm, n // tn, nk), in_specs=in_specs, out_specs=o_spec,
        out_shape=jax.ShapeDtypeStruct((m, n), out_dtype), scratch_shapes=[pltpu.VMEM((tm, tn), F32)],
        compiler_params=_cparams(("parallel", "parallel", "arbitrary")))(*args)


def _rows(tl, w, col=0):
    return pl.BlockSpec((tl, w), lambda i: (i, col))


def _full(shape):
    nd = len(shape)
    return pl.BlockSpec(tuple(shape), lambda i: (0,) * nd)


def _rowcall(body, name, n_steps, in_specs, out_specs, out_shape, scratch=()):
    return pl.pallas_call(
        body, name=name, grid=(n_steps,), in_specs=in_specs, out_specs=out_specs, out_shape=out_shape,
        scratch_shapes=list(scratch), compiler_params=_cparams(("arbitrary",)))


def _acc(ref, val, i):
    @pl.when(i == 0)
    def _():
        ref[...] = val

    @pl.when(i != 0)
    def _():
        ref[...] += val


def rms_fwd(h, g, name):
    l, d = h.shape
    tl = ROW_TILE

    def body(h_ref, g_ref, o_ref):
        x = h_ref[...]
        r = lax.rsqrt(jnp.mean(x * x, axis=-1, keepdims=True) + NORM_EPS)
        o_ref[...] = (x * r * g_ref[...]).astype(BF16)

    return _rowcall(body, name, l // tl, [_rows(tl, d), _full((1, d))], _rows(tl, d),
                    jax.ShapeDtypeStruct((l, d), BF16))(h, g.reshape(1, d))


def rms_bwd(h, g, dhn, dh_in, name):
    l, d = h.shape
    tl = ROW_TILE

    def body(h_ref, g_ref, dhn_ref, dhi_ref, dh_ref, dg_ref):
        i = pl.program_id(0)
        x = h_ref[...]
        r = lax.rsqrt(jnp.mean(x * x, axis=-1, keepdims=True) + NORM_EPS)
        xhat = x * r
        dy = dhn_ref[...]
        dxh = dy * g_ref[...]
        dx = r * (dxh - xhat * jnp.mean(dxh * xhat, axis=-1, keepdims=True))
        dh_ref[...] = dhi_ref[...] + dx
        _acc(dg_ref, jnp.sum(dy * xhat, axis=0, keepdims=True), i)

    return _rowcall(body, name, l // tl, [_rows(tl, d), _full((1, d)), _rows(tl, d), _rows(tl, d)],
                    [_rows(tl, d), _full((1, d))],
                    [jax.ShapeDtypeStruct((l, d), F32), jax.ShapeDtypeStruct((1, d), F32)])(h, g.reshape(1, d), dhn, dh_in)


def loss_head(h, g, target):
    l, d = h.shape
    tl = ROW_TILE

    def body(h_ref, g_ref, t_ref, loss_ref, dh_ref, dg_ref):
        i = pl.program_id(0)
        x = h_ref[...]
        gg = g_ref[...]
        r = lax.rsqrt(jnp.mean(x * x, axis=-1, keepdims=True) + NORM_EPS)
        xhat = x * r
        err = xhat * gg - t_ref[...]
        part = 0.5 * jnp.sum(jnp.mean(err * err, axis=-1, keepdims=True), axis=0, keepdims=True)
        _acc(loss_ref, part, i)
        dy = err * (1.0 / d)
        dxh = dy * gg
        dh_ref[...] = r * (dxh - xhat * jnp.mean(dxh * xhat, axis=-1, keepdims=True))
        _acc(dg_ref, jnp.sum(dy * xhat, axis=0, keepdims=True), i)

    return _rowcall(body, "loss_head", l // tl, [_rows(tl, d), _full((1, d)), _rows(tl, d)],
                    [_full((1, 1)), _rows(tl, d), _full((1, d))],
                    [jax.ShapeDtypeStruct((1, 1), F32), jax.ShapeDtypeStruct((l, d), F32),
                     jax.ShapeDtypeStruct((1, d), F32)])(h, g.reshape(1, d), target)


def _gmlp_common(a_ref, lng_ref, lnb_ref):
    di = lng_ref.shape[1]
    u_pre = a_ref[:, 0:di]
    v_pre = a_ref[:, di:2 * di]
    z = a_ref[:, 2 * di:3 * di]
    vg = _gelu(v_pre)
    mu = jnp.mean(vg, axis=-1, keepdims=True)
    xc = vg - mu
    rstd = lax.rsqrt(jnp.mean(xc * xc, axis=-1, keepdims=True) + NORM_EPS)
    vhat = xc * rstd
    vn = vhat * lng_ref[...] + lnb_ref[...]
    return u_pre, v_pre, z, vhat, rstd, vn


def _tril(w):
    r = lax.broadcasted_iota(jnp.int32, w.shape, 0)
    c = lax.broadcasted_iota(jnp.int32, w.shape, 1)
    return jnp.where(c <= r, w, 0.0)


def gmlp_gate_fwd(a, ln_g, ln_b, w_s, b_s, name):
    l, w3 = a.shape
    di = w3 // 3
    dg = di // GMLP_GROUPS
    tl = GMLP_CHUNK

    def body(a_ref, lng_ref, lnb_ref, ws_ref, bs_ref, m_ref):
        u_pre, _, z, _, _, vn = _gmlp_common(a_ref, lng_ref, lnb_ref)
        gate = _gelu(u_pre) * _silu(z)
        for g in range(GMLP_GROUPS):
            sl = slice(g * dg, (g + 1) * dg)
            s = _dot(_tril(ws_ref[g]), vn[:, sl], DN_NN) + bs_ref[g]
            m_ref[:, sl] = (gate[:, sl] * s).astype(BF16)

    return _rowcall(body, name, l // tl,
                    [_rows(tl, w3), _full((1, di)), _full((1, di)), _full(w_s.shape), _full((GMLP_GROUPS, tl, 1))],
                    _rows(tl, di), jax.ShapeDtypeStruct((l, di), BF16))(
        a, ln_g.reshape(1, di), ln_b.reshape(1, di), w_s, b_s.reshape(GMLP_GROUPS, tl, 1))


def gmlp_gate_bwd(a, dm, ln_g, ln_b, w_s, b_s, name):
    l, w3 = a.shape
    di = w3 // 3
    dg = di // GMLP_GROUPS
    tl = GMLP_CHUNK

    def body(a_ref, dm_ref, lng_ref, lnb_ref, ws_ref, bs_ref, da_ref, dlg_ref, dlb_ref, dws_ref, dbs_ref, dvn_ref):
        i = pl.program_id(0)
        u_pre, v_pre, z, vhat, rstd, vn = _gmlp_common(a_ref, lng_ref, lnb_ref)
        dm_v = dm_ref[...]
        u = _gelu(u_pre)
        sz = _silu(z)
        for g in range(GMLP_GROUPS):
            sl = slice(g * dg, (g + 1) * dg)
            wt = _tril(ws_ref[g])
            vn_g = vn[:, sl]
            s = _dot(wt, vn_g, DN_NN) + bs_ref[g]
            dmg = dm_v[:, sl]
            ds = dmg * u[:, sl] * sz[:, sl]
            da_ref[:, sl] = (dmg * s * sz[:, sl] * _gelu_grad(u_pre[:, sl])).astype(BF16)
            da_ref[:, 2 * di + g * dg:2 * di + (g + 1) * dg] = (
                dmg * u[:, sl] * s * _silu_grad(z[:, sl])).astype(BF16)
            dvn_ref[:, sl] = _dot(wt, ds, DN_TN)
            dw = _tril(_dot(ds, vn_g, DN_NT))
            db = jnp.sum(ds, axis=1, keepdims=True)

            @pl.when(i == 0)
            def _():
                dws_ref[g] = dw
                dbs_ref[g] = db

            @pl.when(i != 0)
            def _():
                dws_ref[g] += dw
                dbs_ref[g] += db

        dvn = dvn_ref[...]
        dxh = dvn * lng_ref[...]
        dvg = rstd * (dxh - jnp.mean(dxh, axis=-1, keepdims=True) - vhat * jnp.mean(dxh * vhat, axis=-1, keepdims=True))
        da_ref[:, di:2 * di] = (dvg * _gelu_grad(v_pre)).astype(BF16)
        _acc(dlg_ref, jnp.sum(dvn * vhat, axis=0, keepdims=True), i)
        _acc(dlb_ref, jnp.sum(dvn, axis=0, keepdims=True), i)

    outs = _rowcall(
        body, name, l // tl,
        [_rows(tl, w3), _rows(tl, di), _full((1, di)), _full((1, di)), _full(w_s.shape), _full((GMLP_GROUPS, tl, 1))],
        [_rows(tl, w3), _full((1, di)), _full((1, di)), _full(w_s.shape), _full((GMLP_GROUPS, tl, 1))],
        [jax.ShapeDtypeStruct((l, w3), BF16), jax.ShapeDtypeStruct((1, di), F32), jax.ShapeDtypeStruct((1, di), F32),
         jax.ShapeDtypeStruct(w_s.shape, F32), jax.ShapeDtypeStruct((GMLP_GROUPS, tl, 1), F32)],
        scratch=[pltpu.VMEM((tl, di), F32)])(
        a, dm, ln_g.reshape(1, di), ln_b.reshape(1, di), w_s, b_s.reshape(GMLP_GROUPS, tl, 1))
    return outs


def gmlp_layer_fwd(h, p, wf, tag):
    hn = rms_fwd(h, p["norm_g"], tag + "_rms")
    a = matmul(hn, wf["w_in"], "nn", tag + "_mm_in")
    m = gmlp_gate_fwd(a, p["ln_g"], p["ln_b"], p["w_s"], p["b_s"], tag + "_gate")
    h_out = matmul(m, wf["w_out"], "nn", tag + "_mm_out", add=h)
    return h_out, (h, hn, a, m)


def gmlp_layer_bwd(dh_out, saved, p, wf, tag):
    h, hn, a, m = saved
    dm = matmul(dh_out, wf["w_out"], "nt", tag + "_mm_dm")
    g_w_out = matmul(m, dh_out, "tn", tag + "_mm_gwout")
    da, dlg, dlb, dws, dbs = gmlp_gate_bwd(a, dm, p["ln_g"], p["ln_b"], p["w_s"], p["b_s"], tag + "_gate_bwd")
    dhn = matmul(da, wf["w_in"], "nt", tag + "_mm_dhn")
    g_w_in = matmul(hn, da, "tn", tag + "_mm_gwin")
    dh, dng = rms_bwd(h, p["norm_g"], dhn, dh_out, tag + "_rms_bwd")
    grads = {"norm_g": dng.reshape(-1), "w_in": g_w_in, "ln_g": dlg.reshape(-1), "ln_b": dlb.reshape(-1),
             "w_s": dws, "b_s": dbs.reshape(GMLP_GROUPS, GMLP_CHUNK), "w_out": g_w_out}
    return dh, grads


def _cmul(ar, ai, br, bi):
    return ar * br - ai * bi, ar * bi + ai * br


S5_PG = 16


def _gblock(tail):
    return pl.BlockSpec((S5_PG,) + tuple(tail), lambda i: (i, 0, 0))


def s5_params_fwd(a_re, a_im, log_step, b_re, b_im):
    g, p, hh = b_re.shape

    def body(ar_ref, ai_ref, ls_ref, br_ref, bi_ref, lr_ref, li_ref, bbr_ref, bbi_ref):
        ar, ai = ar_ref[...], ai_ref[...]
        step = jnp.exp(ls_ref[...])
        mag = jnp.exp(ar * step)
        lr, li = mag * jnp.cos(ai * step), mag * jnp.sin(ai * step)
        den = 1.0 / (ar * ar + ai * ai)
        fr, fi = _cmul(lr - 1.0, li, ar * den, -ai * den)
        lr_ref[...] = lr
        li_ref[...] = li
        bbr, bbi = _cmul(fr, fi, br_ref[...], bi_ref[...])
        bbr_ref[...] = bbr
        bbi_ref[...] = bbi

    s1 = jax.ShapeDtypeStruct((g, p, 1), F32)
    s3 = jax.ShapeDtypeStruct((g, p, hh), F32)
    b1, b0, b3 = _gblock((p, 1)), _gblock((1, 1)), _gblock((p, hh))
    return pl.pallas_call(body, name="s5_params_fwd", grid=(g // S5_PG,), in_specs=[b1, b1, b0, b3, b3],
                          out_specs=[b1, b1, b3, b3], out_shape=[s1, s1, s3, s3],
                          compiler_params=_cparams(("parallel",)))(
        a_re.reshape(g, p, 1), a_im.reshape(g, p, 1), log_step.reshape(g, 1, 1), b_re, b_im)


def s5_params_bwd(a_re, a_im, log_step, b_re, b_im, dl_re, dl_im, dbb_re, dbb_im):
    g, p, hh = b_re.shape

    def body(ar_ref, ai_ref, ls_ref, br_ref, bi_ref, dlr_ref, dli_ref, dbr_ref, dbi_ref,
             gar_ref, gai_ref, gls_ref, gbr_ref, gbi_ref):
        ar, ai = ar_ref[...], ai_ref[...]
        step = jnp.exp(ls_ref[...])
        mag = jnp.exp(ar * step)
        lr, li = mag * jnp.cos(ai * step), mag * jnp.sin(ai * step)
        den = 1.0 / (ar * ar + ai * ai)
        ir, ii = ar * den, -ai * den
        fr, fi = _cmul(lr - 1.0, li, ir, ii)
        br, bi = br_ref[...], bi_ref[...]
        dbr, dbi = dbr_ref[...], dbi_ref[...]
        gbr, gbi = _cmul(fr, -fi, dbr, dbi)
        gbr_ref[...] = gbr
        gbi_ref[...] = gbi
        pr, pi = _cmul(br, -bi, dbr, dbi)
        gfr = jnp.sum(pr, axis=-1, keepdims=True)
        gfi = jnp.sum(pi, axis=-1, keepdims=True)
        t_r, t_i = _cmul(ir, -ii, gfr, gfi)
        glr, gli = dlr_ref[...] + t_r, dli_ref[...] + t_i
        c1r, c1i = _cmul(step * lr, -step * li, glr, gli)
        qr, qi = _cmul(fr, fi, ir, ii)
        c2r, c2i = _cmul(-qr, qi, gfr, gfi)
        gar_ref[...] = c1r + c2r
        gai_ref[...] = c1i + c2i
        wr, wi = _cmul(ar, ai, lr, li)
        sr, _ = _cmul(wr, -wi, glr, gli)
        gls_ref[...] = jnp.sum(sr, axis=1, keepdims=True) * step

    s1 = jax.ShapeDtypeStruct((g, p, 1), F32)
    s3 = jax.ShapeDtypeStruct((g, p, hh), F32)
    b1, b0, b3 = _gblock((p, 1)), _gblock((1, 1)), _gblock((p, hh))
    return pl.pallas_call(body, name="s5_params_bwd", grid=(g // S5_PG,),
                          in_specs=[b1, b1, b0, b3, b3, b1, b1, b3, b3], out_specs=[b1, b1, b0, b3, b3],
                          out_shape=[s1, s1, jax.ShapeDtypeStruct((g, 1, 1), F32), s3, s3],
                          compiler_params=_cparams(("parallel",)))(
        a_re.reshape(g, p, 1), a_im.reshape(g, p, 1), log_step.reshape(g, 1, 1), b_re, b_im,
        dl_re, dl_im, dbb_re, dbb_im)


def _blockdiag(t):
    sb, n, r, c = t.shape
    eye = jnp.eye(n, dtype=bool)[None, :, None, :, None]
    full = jnp.where(eye, t[:, :, :, None, :], jnp.zeros((), t.dtype))
    return full.reshape(sb, n * r, n * c)


def _blockdiag_extract(m, r, c):
    sb = m.shape[0]
    n = m.shape[1] // r
    m5 = m.reshape(sb, n, r, n, c)
    return jnp.stack([m5[:, i, :, i, :] for i in range(n)], axis=1)


S5_TB = 64


def s5_scan_fwd(a_p, lam_re, lam_im, wb_re, wb_im, wc_re, wc_im, d_skip, x0_re, x0_im, name):
    l = a_p.shape[0]
    di = d_skip.shape[1]
    rows = S5_SEG * S5_TB
    nb = l // rows
    ns = wb_re.shape[2]

    def body(u_ref, lr_ref, li_ref, wbr_ref, wbi_ref, wcr_ref, wci_ref, ds_ref, x0r_ref, x0i_ref,
             y_ref, ckr_ref, cki_ref, xer_ref, xei_ref, bur, bui, xr_s, xi_s):
        b = pl.program_id(1)

        @pl.when(b == 0)
        def _():
            xr_s[...] = x0r_ref[0]
            xi_s[...] = x0i_ref[0]

        ckr_ref[0, 0] = xr_s[...]
        cki_ref[0, 0] = xi_s[...]
        u = u_ref[...]
        bur[...] = _dot(u, wbr_ref[0], DN_NN)
        bui[...] = _dot(u, wbi_ref[0], DN_NN)
        lr = jnp.broadcast_to(lr_ref[0], (S5_SEG, ns))
        li = jnp.broadcast_to(li_ref[0], (S5_SEG, ns))

        def step(t, carry):
            xr, xi = carry
            sl = pl.ds(pl.multiple_of(t * S5_SEG, S5_SEG), S5_SEG)
            nr = lr * xr - li * xi + bur[sl, :]
            ni = lr * xi + li * xr + bui[sl, :]
            bur[sl, :] = nr
            bui[sl, :] = ni
            return nr, ni

        xr, xi = lax.fori_loop(0, S5_TB, step, (xr_s[...], xi_s[...]))
        xr_s[...] = xr
        xi_s[...] = xi
        xer_ref[0] = xr
        xei_ref[0] = xi
        y_ref[...] = _dot(bur[...], wcr_ref[0], DN_NN) - _dot(bui[...], wci_ref[0], DN_NN) + ds_ref[...] * u

    sb3 = lambda s, b: (s, 0, 0)
    st = jax.ShapeDtypeStruct
    return pl.pallas_call(
        body, name=name, grid=(S5_SB, nb),
        in_specs=[pl.BlockSpec((rows, LANES), lambda s, b: (b, s)),
                  pl.BlockSpec((1, 1, ns), sb3), pl.BlockSpec((1, 1, ns), sb3),
                  pl.BlockSpec((1, LANES, ns), sb3), pl.BlockSpec((1, LANES, ns), sb3),
                  pl.BlockSpec((1, ns, LANES), sb3), pl.BlockSpec((1, ns, LANES), sb3),
                  pl.BlockSpec((1, LANES), lambda s, b: (0, s)),
                  pl.BlockSpec((1, S5_SEG, ns), sb3), pl.BlockSpec((1, S5_SEG, ns), sb3)],
        out_specs=[pl.BlockSpec((rows, LANES), lambda s, b: (b, s)),
                   pl.BlockSpec((1, 1, S5_SEG, ns), lambda s, b: (s, b, 0, 0)),
                   pl.BlockSpec((1, 1, S5_SEG, ns), lambda s, b: (s, b, 0, 0)),
                   pl.BlockSpec((1, S5_SEG, ns), sb3), pl.BlockSpec((1, S5_SEG, ns), sb3)],
        out_shape=[st((l, di), F32), st((S5_SB, nb, S5_SEG, ns), F32), st((S5_SB, nb, S5_SEG, ns), F32),
                   st((S5_SB, S5_SEG, ns), F32), st((S5_SB, S5_SEG, ns), F32)],
        scratch_shapes=[pltpu.VMEM((rows, ns), F32), pltpu.VMEM((rows, ns), F32),
                        pltpu.VMEM((S5_SEG, ns), F32), pltpu.VMEM((S5_SEG, ns), F32)],
        compiler_params=_cparams(("parallel", "arbitrary")))(
        a_p, lam_re, lam_im, wb_re, wb_im, wc_re, wc_im, d_skip, x0_re, x0_im)


def s5_scan_bwd(a_p, dy, lam_re, lam_im, wb_re, wb_im, wc_re, wc_im, d_skip, ck_re, ck_im, a0_re, a0_im, name):
    l = a_p.shape[0]
    di = d_skip.shape[1]
    rows = S5_SEG * S5_TB
    nb = l // rows
    ns = wb_re.shape[2]

    def body(u_ref, dy_ref, lr_ref, li_ref, wbr_ref, wbi_ref, wcr_ref, wci_ref, ds_ref, ckr_ref, cki_ref,
             a0r_ref, a0i_ref,
             du_ref, dwbr_ref, dwbi_ref, dwcr_ref, dwci_ref, dds_ref, dlr_ref, dli_ref, aer_ref, aei_ref,
             xr_b, xi_b, gr_b, gi_b, ar_s, ai_s):
        b = pl.program_id(1)

        @pl.when(b == 0)
        def _():
            ar_s[...] = a0r_ref[0]
            ai_s[...] = a0i_ref[0]

        u = u_ref[...]
        dyv = dy_ref[...]
        lr = jnp.broadcast_to(lr_ref[0], (S5_SEG, ns))
        li = jnp.broadcast_to(li_ref[0], (S5_SEG, ns))
        xr_b[...] = _dot(u, wbr_ref[0], DN_NN)
        xi_b[...] = _dot(u, wbi_ref[0], DN_NN)

        def fstep(t, carry):
            xr, xi = carry
            sl = pl.ds(pl.multiple_of(t * S5_SEG, S5_SEG), S5_SEG)
            nr = lr * xr - li * xi + xr_b[sl, :]
            ni = lr * xi + li * xr + xi_b[sl, :]
            xr_b[sl, :] = nr
            xi_b[sl, :] = ni
            return nr, ni

        x0r, x0i = ckr_ref[0, 0], cki_ref[0, 0]
        lax.fori_loop(0, S5_TB, fstep, (x0r, x0i))
        dwcr = _dot(xr_b[...], dyv, DN_TN)
        dwci = -_dot(xi_b[...], dyv, DN_TN)
        gr_b[...] = _dot(dyv, wcr_ref[0], DN_NT)
        gi_b[...] = -_dot(dyv, wci_ref[0], DN_NT)

        def bstep(k, carry):
            ar, ai, dlr, dli = carry
            t = S5_TB - 1 - k
            sl = pl.ds(pl.multiple_of(t * S5_SEG, S5_SEG), S5_SEG)
            slp = pl.ds(pl.multiple_of(jnp.maximum(t - 1, 0) * S5_SEG, S5_SEG), S5_SEG)
            nr = gr_b[sl, :] + lr * ar + li * ai
            ni = gi_b[sl, :] + lr * ai - li * ar
            gr_b[sl, :] = nr
            gi_b[sl, :] = ni
            first = t == 0
            pr = jnp.where(first, x0r, xr_b[slp, :])
            pi = jnp.where(first, x0i, xi_b[slp, :])
            dlr = dlr + nr * pr + ni * pi
            dli = dli + ni * pr - nr * pi
            return nr, ni, dlr, dli

        zero = jnp.zeros((S5_SEG, ns), F32)
        ar, ai, dlr, dli = lax.fori_loop(0, S5_TB, bstep, (ar_s[...], ai_s[...], zero, zero))
        ar_s[...] = ar
        ai_s[...] = ai
        aer_ref[0] = ar
        aei_ref[0] = ai
        dsk = ds_ref[...]
        du_ref[...] = (_dot(gr_b[...], wbr_ref[0], DN_NT) + _dot(gi_b[...], wbi_ref[0], DN_NT) + dsk * dyv).astype(BF16)
        dwbr = _dot(u, gr_b[...], DN_TN)
        dwbi = _dot(u, gi_b[...], DN_TN)
        dds = jnp.sum(dyv * u, axis=0, keepdims=True)

        @pl.when(b == 0)
        def _():
            dwbr_ref[0] = dwbr
            dwbi_ref[0] = dwbi
            dwcr_ref[0] = dwcr
            dwci_ref[0] = dwci
            dds_ref[...] = dds
            dlr_ref[0] = dlr
            dli_ref[0] = dli

        @pl.when(b != 0)
        def _():
            dwbr_ref[0] += dwbr
            dwbi_ref[0] += dwbi
            dwcr_ref[0] += dwcr
            dwci_ref[0] += dwci
            dds_ref[...] += dds
            dlr_ref[0] += dlr
            dli_ref[0] += dli

    sb3 = lambda s, b: (s, 0, 0)
    rev = lambda s, b: (nb - 1 - b, s)
    st = jax.ShapeDtypeStruct
    return pl.pallas_call(
        body, name=name, grid=(S5_SB, nb),
        in_specs=[pl.BlockSpec((rows, LANES), rev), pl.BlockSpec((rows, LANES), rev),
                  pl.BlockSpec((1, 1, ns), sb3), pl.BlockSpec((1, 1, ns), sb3),
                  pl.BlockSpec((1, LANES, ns), sb3), pl.BlockSpec((1, LANES, ns), sb3),
                  pl.BlockSpec((1, ns, LANES), sb3), pl.BlockSpec((1, ns, LANES), sb3),
                  pl.BlockSpec((1, LANES), lambda s, b: (0, s)),
                  pl.BlockSpec((1, 1, S5_SEG, ns), lambda s, b: (s, nb - 1 - b, 0, 0)),
                  pl.BlockSpec((1, 1, S5_SEG, ns), lambda s, b: (s, nb - 1 - b, 0, 0)),
                  pl.BlockSpec((1, S5_SEG, ns), sb3), pl.BlockSpec((1, S5_SEG, ns), sb3)],
        out_specs=[pl.BlockSpec((rows, LANES), rev),
                   pl.BlockSpec((1, LANES, ns), sb3), pl.BlockSpec((1, LANES, ns), sb3),
                   pl.BlockSpec((1, ns, LANES), sb3), pl.BlockSpec((1, ns, LANES), sb3),
                   pl.BlockSpec((1, LANES), lambda s, b: (0, s)),
                   pl.BlockSpec((1, S5_SEG, ns), sb3), pl.BlockSpec((1, S5_SEG, ns), sb3),
                   pl.BlockSpec((1, S5_SEG, ns), sb3), pl.BlockSpec((1, S5_SEG, ns), sb3)],
        out_shape=[st((l, di), BF16), st((S5_SB, LANES, ns), F32), st((S5_SB, LANES, ns), F32),
                   st((S5_SB, ns, LANES), F32), st((S5_SB, ns, LANES), F32), st((1, di), F32),
                   st((S5_SB, S5_SEG, ns), F32), st((S5_SB, S5_SEG, ns), F32),
                   st((S5_SB, S5_SEG, ns), F32), st((S5_SB, S5_SEG, ns), F32)],
        scratch_shapes=[pltpu.VMEM((rows, ns), F32), pltpu.VMEM((rows, ns), F32),
                        pltpu.VMEM((rows, ns), F32), pltpu.VMEM((rows, ns), F32),
                        pltpu.VMEM((S5_SEG, ns), F32), pltpu.VMEM((S5_SEG, ns), F32)],
        compiler_params=_cparams(("parallel", "arbitrary")))(
        a_p, dy, lam_re, lam_im, wb_re, wb_im, wc_re, wc_im, d_skip, ck_re, ck_im, a0_re, a0_im)


def s5_carry(e_re, e_im, lam_re, lam_im, seg_len, reverse, name):
    sb, seg, ns = e_re.shape

    def body(er_ref, ei_ref, lr_ref, li_ref, cr_ref, ci_ref):
        pr, pi = lr_ref[...], li_ref[...]
        if reverse:
            pi = -pi
        for _ in range(int(math.log2(seg_len))):
            pr, pi = _cmul(pr, pi, pr, pi)
        er, ei = er_ref[...], ei_ref[...]
        row = lax.broadcasted_iota(jnp.int32, (sb, seg, ns), 1)
        cr = jnp.zeros((sb, seg, ns), F32)
        ci = jnp.zeros((sb, seg, ns), F32)
        cur_r = jnp.zeros((sb, 1, ns), F32)
        cur_i = jnp.zeros((sb, 1, ns), F32)
        order = range(seg - 2, -1, -1) if reverse else range(1, seg)
        for s in order:
            src = s + 1 if reverse else s - 1
            mr, mi = _cmul(pr, pi, cur_r, cur_i)
            cur_r = jnp.sum(jnp.where(row == src, er, 0.0), axis=1, keepdims=True) + mr
            cur_i = jnp.sum(jnp.where(row == src, ei, 0.0), axis=1, keepdims=True) + mi
            cr = jnp.where(row == s, cur_r, cr)
            ci = jnp.where(row == s, cur_i, ci)
        cr_ref[...] = cr
        ci_ref[...] = ci

    st = jax.ShapeDtypeStruct((sb, seg, ns), F32)
    return pl.pallas_call(body, name=name, out_shape=[st, st],
                          compiler_params=pltpu.CompilerParams(vmem_limit_bytes=VMEM_LIMIT_BYTES))(e_re, e_im, lam_re, lam_im)


def s5_act(y, name):
    l, d = y.shape
    tl = ROW_TILE

    def body(y_ref, o_ref):
        o_ref[...] = _gelu(y_ref[...]).astype(BF16)

    return _rowcall(body, name, l // tl, [_rows(tl, d)], _rows(tl, d), jax.ShapeDtypeStruct((l, d), BF16))(y)


def s5_gate_fwd(y, t, b_glu, a_p, name):
    l, d = y.shape
    tl = ROW_TILE

    def body(y_ref, t_ref, b_ref, z_ref, m_ref):
        yg = _gelu(y_ref[...])
        m_ref[...] = (yg * _sigmoid(t_ref[...] + b_ref[...]) * _silu(z_ref[...])).astype(BF16)

    return _rowcall(body, name, l // tl, [_rows(tl, d), _rows(tl, d), _full((1, d)), _rows(tl, d, 1)], _rows(tl, d),
                    jax.ShapeDtypeStruct((l, d), BF16))(y, t, b_glu.reshape(1, d), a_p)


def s5_gate_bwd(dm, y, t, b_glu, a_p, name):
    l, d = y.shape
    tl = ROW_TILE

    def body(dm_ref, y_ref, t_ref, b_ref, z_ref, dt_ref, dyg_ref, dz_ref, db_ref):
        i = pl.program_id(0)
        dmv = dm_ref[...]
        z = z_ref[...]
        yg = _gelu(y_ref[...])
        sg = _sigmoid(t_ref[...] + b_ref[...])
        y2 = yg * sg
        dy2 = dmv * _silu(z)
        dz_ref[...] = (dmv * y2 * _silu_grad(z)).astype(BF16)
        dyg_ref[...] = dy2 * sg
        dt = dy2 * yg * sg * (1.0 - sg)
        dt_ref[...] = dt.astype(BF16)
        _acc(db_ref, jnp.sum(dt, axis=0, keepdims=True), i)

    st = jax.ShapeDtypeStruct
    return _rowcall(body, name, l // tl, [_rows(tl, d), _rows(tl, d), _rows(tl, d), _full((1, d)), _rows(tl, d, 1)],
                    [_rows(tl, d), _rows(tl, d), _rows(tl, d), _full((1, d))],
                    [st((l, d), BF16), st((l, d), F32), st((l, d), BF16), st((1, d), F32)])(
        dm, y, t, b_glu.reshape(1, d), a_p)


def s5_act_bwd(y, dyg_a, dyg_b, name):
    l, d = y.shape
    tl = ROW_TILE

    def body(y_ref, a_ref, b_ref, o_ref):
        o_ref[...] = (a_ref[...] + b_ref[...]) * _gelu_grad(y_ref[...])

    return _rowcall(body, name, l // tl, [_rows(tl, d)] * 3, _rows(tl, d), jax.ShapeDtypeStruct((l, d), F32))(y, dyg_a, dyg_b)


def _seg_perm(t):
    l, d = t.shape
    return t.reshape(S5_SEG, l // S5_SEG, d).transpose(1, 0, 2).reshape(l, d)


def _seg_unperm(t):
    l, d = t.shape
    return t.reshape(l // S5_SEG, S5_SEG, d).transpose(1, 0, 2).reshape(l, d)


def _s5_weights(p):
    lr, li, bbr, bbi = s5_params_fwd(p["a_re"], p["a_im"], p["log_step"], p["b_re"], p["b_im"])
    ns = 8 * S5_STATE
    lam_re = lr.reshape(S5_SB, 1, ns)
    lam_im = li.reshape(S5_SB, 1, ns)
    to_bd = lambda t: _blockdiag(t.reshape(S5_SB, 8, t.shape[1], t.shape[2]))
    wb_re = to_bd(bbr.transpose(0, 2, 1)).astype(BF16)
    wb_im = to_bd(bbi.transpose(0, 2, 1)).astype(BF16)
    wc_re = to_bd(p["c_re"].transpose(0, 2, 1)).astype(BF16)
    wc_im = to_bd(p["c_im"].transpose(0, 2, 1)).astype(BF16)
    return lam_re, lam_im, wb_re, wb_im, wc_re, wc_im


def s5_layer_fwd(h, p, wf, tag):
    l = h.shape[0]
    di = p["d_skip"].shape[0]
    hn = rms_fwd(h, p["norm_g"], tag + "_rms")
    hn_p = _seg_perm(hn)
    a_p = matmul(hn_p, wf["w_in"], "nn", tag + "_mm_in")
    sw = _s5_weights(p)
    dsk = p["d_skip"].reshape(1, di)
    zeros = jnp.zeros((S5_SB, S5_SEG, 8 * S5_STATE), F32)
    _, _, _, e_re, e_im = s5_scan_fwd(a_p, *sw, dsk, zeros, zeros, tag + "_scan_ends")
    c_re, c_im = s5_carry(e_re, e_im, sw[0], sw[1], l // S5_SEG, False, tag + "_carry")
    y, ck_re, ck_im, _, _ = s5_scan_fwd(a_p, *sw, dsk, c_re, c_im, tag + "_scan")
    yg = s5_act(y, tag + "_act")
    t = matmul(yg, wf["w_glu"], "nn", tag + "_mm_glu")
    m = s5_gate_fwd(y, t, p["b_glu"], a_p, tag + "_gate")
    out_p = matmul(m, wf["w_out"], "nn", tag + "_mm_out")
    h_out = residual_add(h, _seg_unperm(out_p), tag + "_res")
    return h_out, (h, hn_p, a_p, sw, ck_re, ck_im, y, yg, t, m)


def residual_add(h, y, name):
    l, d = h.shape
    tl = ROW_TILE

    def body(h_ref, y_ref, o_ref):
        o_ref[...] = h_ref[...] + y_ref[...]

    return _rowcall(body, name, l // tl, [_rows(tl, d)] * 2, _rows(tl, d), jax.ShapeDtypeStruct((l, d), F32))(h, y)


def s5_layer_bwd(dh_out, saved, p, wf, tag):
    h, hn_p, a_p, sw, ck_re, ck_im, y, yg, t, m = saved
    l = h.shape[0]
    di = p["d_skip"].shape[0]
    dsk = p["d_skip"].reshape(1, di)
    dout_p = _seg_perm(dh_out)
    dm = matmul(dout_p, wf["w_out"], "nt", tag + "_mm_dm")
    g_w_out = matmul(m, dout_p, "tn", tag + "_mm_gwout")
    dt, dyg_a, dz, db_glu = s5_gate_bwd(dm, y, t, p["b_glu"], a_p, tag + "_gate_bwd")
    dyg_b = matmul(dt, wf["w_glu"], "nt", tag + "_mm_dyg")
    g_w_glu = matmul(yg, dt, "tn", tag + "_mm_gwglu")
    dy = s5_act_bwd(y, dyg_a, dyg_b, tag + "_act_bwd")
    zeros = jnp.zeros((S5_SB, S5_SEG, 8 * S5_STATE), F32)
    ends = s5_scan_bwd(a_p, dy, *sw, dsk, ck_re, ck_im, zeros, zeros, tag + "_scanb_ends")
    c_re, c_im = s5_carry(ends[8], ends[9], sw[0], sw[1], l // S5_SEG, True, tag + "_carry_bwd")
    du, dwbr, dwbi, dwcr, dwci, dds, dlr, dli, _, _ = s5_scan_bwd(
        a_p, dy, *sw, dsk, ck_re, ck_im, c_re, c_im, tag + "_scanb")
    da = jnp.concatenate([du, dz], axis=1)
    dhn_p = matmul(da, wf["w_in"], "nt", tag + "_mm_dhn")
    g_w_in = matmul(hn_p, da, "tn", tag + "_mm_gwin")
    dh, dng = rms_bwd(h, p["norm_g"], _seg_unperm(dhn_p), dh_out, tag + "_rms_bwd")
    ex = lambda m_, r, c: _blockdiag_extract(m_, r, c).reshape(S5_GROUPS, r, c).transpose(0, 2, 1)
    dbb_re, dbb_im = ex(dwbr, S5_GROUP, S5_STATE), ex(dwbi, S5_GROUP, S5_STATE)
    g_c_re, g_c_im = ex(dwcr, S5_STATE, S5_GROUP), ex(dwci, S5_STATE, S5_GROUP)
    dl_re = lane_sum8(dlr).reshape(S5_GROUPS, S5_STATE, 1)
    dl_im = lane_sum8(dli).reshape(S5_GROUPS, S5_STATE, 1)
    gar, gai, gls, gbr, gbi = s5_params_bwd(p["a_re"], p["a_im"], p["log_step"], p["b_re"], p["b_im"],
                                            dl_re, dl_im, dbb_re, dbb_im)
    grads = {"norm_g": dng.reshape(-1), "w_in": g_w_in, "a_re": gar.reshape(S5_GROUPS, S5_STATE),
             "a_im": gai.reshape(S5_GROUPS, S5_STATE), "log_step": gls.reshape(-1), "b_re": gbr, "b_im": gbi,
             "c_re": g_c_re, "c_im": g_c_im, "d_skip": dds.reshape(-1), "w_glu": g_w_glu,
             "b_glu": db_glu.reshape(-1), "w_out": g_w_out}
    return dh, grads


def lane_sum8(t):
    sb, seg, ns = t.shape

    def body(t_ref, o_ref):
        o_ref[...] = jnp.sum(t_ref[...], axis=1, keepdims=True)

    return pl.pallas_call(body, name="s5_seg_sum", out_shape=jax.ShapeDtypeStruct((sb, 1, ns), F32))(t)


MLA_DI = MLA_HEADS * 128
MLA_CQ0 = MLA_DI
MLA_CKV0 = MLA_CQ0 + MLA_Q_RANK
MLA_KR0 = MLA_CKV0 + MLA_KV_RANK
MLA_AW = MLA_KR0 + LANES


def _rot_half(x):
    w = x.shape[-1]
    lane = lax.broadcasted_iota(jnp.int32, x.shape, x.ndim - 1)
    return jnp.where(lane % MLA_ROPE < MLA_ROPE // 2, pltpu.roll(x, w - MLA_ROPE // 2, x.ndim - 1),
                     pltpu.roll(x, MLA_ROPE // 2, x.ndim - 1))


def rope_tables(pos):
    l = pos.shape[0]
    tl = ROW_TILE
    j = np.arange(LANES) % MLA_ROPE % (MLA_ROPE // 2)
    inv_freq = (ROPE_THETA ** (-(2.0 * j) / MLA_ROPE)).astype(np.float32).reshape(1, LANES)
    sign = np.where(np.arange(LANES) % MLA_ROPE < MLA_ROPE // 2, -1.0, 1.0).astype(np.float32).reshape(1, LANES)

    def body(p_ref, f_ref, s_ref, cos_ref, sin_ref):
        ang = p_ref[...].astype(F32) * f_ref[...]
        cos_ref[...] = jnp.cos(ang)
        sin_ref[...] = jnp.sin(ang) * s_ref[...]

    st = jax.ShapeDtypeStruct((l, LANES), F32)
    return _rowcall(body, "rope_tables", l // tl, [_rows(tl, 1), _full((1, LANES)), _full((1, LANES))],
                    [_rows(tl, LANES)] * 2, [st, st])(pos, jnp.asarray(inv_freq), jnp.asarray(sign))


def _rope(x, cos, sins):
    return x * cos + _rot_half(x) * sins


def _rope_t(dy, cos, sins):
    return dy * cos - sins * _rot_half(dy)


def _rmsn(x):
    r = lax.rsqrt(jnp.mean(x * x, axis=-1, keepdims=True) + NORM_EPS)
    return x * r, r


def mla_pre(a, q_g, kv_g, cos, sins, name):
    l = a.shape[0]
    tl = ROW_TILE

    def body(a_ref, qg_ref, kg_ref, cos_ref, sin_ref, cq_ref, ckv_ref, krs_ref):
        xq, _ = _rmsn(a_ref[:, MLA_CQ0:MLA_CKV0])
        cq_ref[...] = (xq * qg_ref[...]).astype(BF16)
        xk, _ = _rmsn(a_ref[:, MLA_CKV0:MLA_KR0])
        ckv_ref[...] = (xk * kg_ref[...]).astype(BF16)
        kr = a_ref[:, MLA_KR0:MLA_AW]
        kr2 = kr + pltpu.roll(kr, MLA_ROPE, 1)
        kr2 = _rope(kr2, cos_ref[...], sin_ref[...])
        lane = lax.broadcasted_iota(jnp.int32, kr2.shape, 1)
        krs_ref[0] = jnp.where(lane < MLA_ROPE, kr2, 0.0).astype(BF16)
        krs_ref[1] = jnp.where(lane >= MLA_ROPE, kr2, 0.0).astype(BF16)

    st = jax.ShapeDtypeStruct
    return _rowcall(body, name, l // tl,
                    [_rows(tl, MLA_AW), _full((1, MLA_Q_RANK)), _full((1, MLA_KV_RANK)), _rows(tl, LANES), _rows(tl, LANES)],
                    [_rows(tl, MLA_Q_RANK), _rows(tl, MLA_KV_RANK), pl.BlockSpec((2, tl, LANES), lambda i: (0, i, 0))],
                    [st((l, MLA_Q_RANK), BF16), st((l, MLA_KV_RANK), BF16), st((2, l, LANES), BF16)])(
        a, q_g.reshape(1, -1), kv_g.reshape(1, -1), cos, sins)


def mla_rope_q(qr, cos, sins, name):
    l, w = qr.shape
    tl = ROW_TILE

    def body(q_ref, cos_ref, sin_ref, o_ref):
        c, s = cos_ref[...], sin_ref[...]
        for p in range(w // LANES):
            sl = slice(p * LANES, (p + 1) * LANES)
            o_ref[:, sl] = _rope(q_ref[:, sl], c, s).astype(BF16)

    return _rowcall(body, name, l // tl, [_rows(tl, w), _rows(tl, LANES), _rows(tl, LANES)], _rows(tl, w),
                    jax.ShapeDtypeStruct((l, w), BF16))(qr, cos, sins)


ATT_T = 256


def _scores(qn, qr, kn, kr, qi, kj, transposed):
    if transposed:
        s = _dot(kn, qn, DN_NT) + _dot(kr, qr, DN_NT)
        kpos = kj * ATT_T + lax.broadcasted_iota(jnp.int32, s.shape, 0)
        qpos = qi * ATT_T + lax.broadcasted_iota(jnp.int32, s.shape, 1)
    else:
        s = _dot(qn, kn, DN_NT) + _dot(qr, kr, DN_NT)
        qpos = qi * ATT_T + lax.broadcasted_iota(jnp.int32, s.shape, 0)
        kpos = kj * ATT_T + lax.broadcasted_iota(jnp.int32, s.shape, 1)
    return jnp.where(kpos <= qpos, s * MLA_SCALE, NEG_INF)


def flash_fwd(qn, qr, kv, krs, name):
    l = qn.shape[0]
    nq = l // ATT_T

    def body(qn_ref, qr_ref, kn_ref, v_ref, kr_ref, o_ref, lse_ref):
        qi = pl.program_id(1)
        q_n, q_r = qn_ref[...], qr_ref[...]

        def step(j, carry):
            m, lsum, acc = carry
            sl = pl.ds(pl.multiple_of(j * ATT_T, ATT_T), ATT_T)
            s = _scores(q_n, q_r, kn_ref[sl, :], kr_ref[0, sl, :], qi, j, False)
            m_new = jnp.maximum(m, jnp.max(s, axis=-1, keepdims=True))
            alpha = jnp.exp(m - m_new)
            pr = jnp.exp(s - m_new)
            lsum = alpha * lsum + jnp.sum(pr, axis=-1, keepdims=True)
            acc = alpha * acc + _dot(pr, v_ref[sl, :], DN_NN)
            return m_new, lsum, acc

        init = (jnp.full((ATT_T, 1), NEG_INF, F32), jnp.zeros((ATT_T, 1), F32), jnp.zeros((ATT_T, LANES), F32))
        m, lsum, acc = lax.fori_loop(0, qi + 1, step, init)
        o_ref[...] = acc / lsum
        lse_ref[0] = m + jnp.log(lsum)

    st = jax.ShapeDtypeStruct
    return pl.pallas_call(
        body, name=name, grid=(MLA_HEADS, nq),
        in_specs=[pl.BlockSpec((ATT_T, LANES), lambda h, i: (i, h)),
                  pl.BlockSpec((ATT_T, LANES), lambda h, i: (i, h // 2)),
                  pl.BlockSpec((l, LANES), lambda h, i: (0, 2 * h)),
                  pl.BlockSpec((l, LANES), lambda h, i: (0, 2 * h + 1)),
                  pl.BlockSpec((1, l, LANES), lambda h, i: (h % 2, 0, 0))],
        out_specs=[pl.BlockSpec((ATT_T, LANES), lambda h, i: (i, h)),
                   pl.BlockSpec((1, ATT_T, 1), lambda h, i: (h, i, 0))],
        out_shape=[st((l, MLA_DI), F32), st((MLA_HEADS, l, 1), F32)],
        compiler_params=_cparams(("parallel", "arbitrary")))(qn, qr, kv, kv, krs)


def flash_dkv(qn, qr, kv, krs, do, lse_row, delta_row, name):
    l = qn.shape[0]
    nk = l // ATT_T

    def body(qn_ref, qr_ref, do_ref, lse_ref, dl_ref, kv_ref, kr_ref, dkv_ref, dkr_ref):
        kj = pl.program_id(1)
        lane = lax.broadcasted_iota(jnp.int32, (ATT_T, LANES), 1)
        dkr_tot = jnp.zeros((ATT_T, LANES), F32)
        for hh in range(2):
            kn = kv_ref[:, 2 * hh * LANES:(2 * hh + 1) * LANES]
            v = kv_ref[:, (2 * hh + 1) * LANES:(2 * hh + 2) * LANES]
            kr = kr_ref[hh]
            hs = slice(hh * LANES, (hh + 1) * LANES)

            def step(i, carry):
                dkn, dv, dkr = carry
                sl = pl.ds(pl.multiple_of(i * ATT_T, ATT_T), ATT_T)
                q_n, q_r, d_o = qn_ref[sl, hs], qr_ref[sl, :], do_ref[sl, hs]
                s = _scores(q_n, q_r, kn, kr, i, kj, True)
                pt = jnp.exp(s - lse_ref[hh, i])
                dv = dv + _dot(pt, d_o, DN_NN)
                dpt = _dot(v, d_o, DN_NT)
                dst = (pt * (dpt - dl_ref[hh, i]) * MLA_SCALE).astype(BF16)
                dkn = dkn + _dot(dst, q_n, DN_NN)
                dkr = dkr + _dot(dst, q_r, DN_NN)
                return dkn, dv, dkr

            z = jnp.zeros((ATT_T, LANES), F32)
            dkn, dv, dkr = lax.fori_loop(kj, nk, step, (z, z, z))
            dkv_ref[:, 2 * hh * LANES:(2 * hh + 1) * LANES] = dkn.astype(BF16)
            dkv_ref[:, (2 * hh + 1) * LANES:(2 * hh + 2) * LANES] = dv.astype(BF16)
            keep = (lane < MLA_ROPE) if hh == 0 else (lane >= MLA_ROPE)
            dkr_tot = dkr_tot + jnp.where(keep, dkr, 0.0)
        dkr_ref[0] = dkr_tot

    st = jax.ShapeDtypeStruct
    return pl.pallas_call(
        body, name=name, grid=(MLA_HEADS // 2, nk),
        in_specs=[pl.BlockSpec((l, 2 * LANES), lambda p, j: (0, p)),
                  pl.BlockSpec((l, LANES), lambda p, j: (0, p)),
                  pl.BlockSpec((l, 2 * LANES), lambda p, j: (0, p)),
                  pl.BlockSpec((2, nk, 1, ATT_T), lambda p, j: (p, 0, 0, 0)),
                  pl.BlockSpec((2, nk, 1, ATT_T), lambda p, j: (p, 0, 0, 0)),
                  pl.BlockSpec((ATT_T, 4 * LANES), lambda p, j: (j, p)),
                  pl.BlockSpec((2, ATT_T, LANES), lambda p, j: (0, j, 0))],
        out_specs=[pl.BlockSpec((ATT_T, 4 * LANES), lambda p, j: (j, p)),
                   pl.BlockSpec((1, ATT_T, LANES), lambda p, j: (p, j, 0))],
        out_shape=[st((l, 2 * MLA_DI), BF16), st((MLA_HEADS // 2, l, LANES), F32)],
        compiler_params=_cparams(("parallel", "arbitrary")))(qn, qr, do, lse_row, delta_row, kv, krs)


def flash_dq(qn, qr, kv, krs, do, lse, delta, cos, sins, name):
    l = qn.shape[0]
    nq = l // ATT_T

    def body(qn_ref, qr_ref, do_ref, lse_ref, dl_ref, kv_ref, kr_ref, cos_ref, sin_ref, dqn_ref, dqr_ref):
        qi = pl.program_id(1)
        q_r = qr_ref[...]
        dqr_tot = jnp.zeros((ATT_T, LANES), F32)
        for hh in range(2):
            hs = slice(hh * LANES, (hh + 1) * LANES)
            q_n, d_o = qn_ref[:, hs], do_ref[:, hs]
            lse_h, dl_h = lse_ref[hh], dl_ref[hh]

            def step(j, carry):
                dqn, dqr = carry
                sl = pl.ds(pl.multiple_of(j * ATT_T, ATT_T), ATT_T)
                kn = kv_ref[sl, 2 * hh * LANES:(2 * hh + 1) * LANES]
                v = kv_ref[sl, (2 * hh + 1) * LANES:(2 * hh + 2) * LANES]
                kr = kr_ref[hh, sl, :]
                s = _scores(q_n, q_r, kn, kr, qi, j, False)
                pr = jnp.exp(s - lse_h)
                dp = _dot(d_o, v, DN_NT)
                ds = (pr * (dp - dl_h) * MLA_SCALE).astype(BF16)
                return dqn + _dot(ds, kn, DN_NN), dqr + _dot(ds, kr, DN_NN)

            z = jnp.zeros((ATT_T, LANES), F32)
            dqn, dqr = lax.fori_loop(0, qi + 1, step, (z, z))
            dqn_ref[:, hs] = dqn.astype(BF16)
            dqr_tot = dqr_tot + dqr
        dqr_ref[...] = _rope_t(dqr_tot, cos_ref[...], sin_ref[...]).astype(BF16)

    st = jax.ShapeDtypeStruct
    return pl.pallas_call(
        body, name=name, grid=(MLA_HEADS // 2, nq),
        in_specs=[pl.BlockSpec((ATT_T, 2 * LANES), lambda p, i: (i, p)),
                  pl.BlockSpec((ATT_T, LANES), lambda p, i: (i, p)),
                  pl.BlockSpec((ATT_T, 2 * LANES), lambda p, i: (i, p)),
                  pl.BlockSpec((2, ATT_T, 1), lambda p, i: (p, i, 0)),
                  pl.BlockSpec((2, ATT_T, 1), lambda p, i: (p, i, 0)),
                  pl.BlockSpec((l, 4 * LANES), lambda p, i: (0, p)),
                  pl.BlockSpec((2, l, LANES), lambda p, i: (0, 0, 0)),
                  pl.BlockSpec((ATT_T, LANES), lambda p, i: (i, 0)),
                  pl.BlockSpec((ATT_T, LANES), lambda p, i: (i, 0))],
        out_specs=[pl.BlockSpec((ATT_T, 2 * LANES), lambda p, i: (i, p)),
                   pl.BlockSpec((ATT_T, LANES), lambda p, i: (i, p))],
        out_shape=[st((l, MLA_DI), BF16), st((l, MLA_HEADS * MLA_ROPE), BF16)],
        compiler_params=_cparams(("parallel", "arbitrary")))(qn, qr, do, lse, delta, kv, krs, cos, sins)


def mla_gate_fwd(o, a, name):
    l = o.shape[0]
    tl = ROW_TILE

    def body(o_ref, z_ref, m_ref):
        m_ref[...] = (o_ref[...] * _silu(z_ref[...])).astype(BF16)

    return _rowcall(body, name, l // tl, [_rows(tl, MLA_DI), _rows(tl, MLA_DI)], _rows(tl, MLA_DI),
                    jax.ShapeDtypeStruct((l, MLA_DI), BF16))(o, a)


def mla_gate_bwd(dm, o, a, name):
    l = o.shape[0]
    tl = ROW_TILE

    def body(dm_ref, o_ref, z_ref, do_ref, dz_ref, dl_ref):
        dmv, ov, z = dm_ref[...], o_ref[...], z_ref[...]
        d_o = dmv * _silu(z)
        do_ref[...] = d_o.astype(BF16)
        dz_ref[...] = (dmv * ov * _silu_grad(z)).astype(BF16)
        pr = d_o * ov
        for h in range(MLA_HEADS):
            dl_ref[h] = jnp.sum(pr[:, h * LANES:(h + 1) * LANES], axis=1, keepdims=True)

    st = jax.ShapeDtypeStruct
    return _rowcall(body, name, l // tl, [_rows(tl, MLA_DI)] * 3,
                    [_rows(tl, MLA_DI), _rows(tl, MLA_DI), pl.BlockSpec((MLA_HEADS, tl, 1), lambda i: (0, i, 0))],
                    [st((l, MLA_DI), BF16), st((l, MLA_DI), BF16), st((MLA_HEADS, l, 1), F32)])(dm, o, a)


def mla_post(a, dcqn, dckvn, dkr_pairs, dz, q_g, kv_g, cos, sins, name):
    l = a.shape[0]
    tl = ROW_TILE
    npair = MLA_HEADS // 2

    def norm_bwd(x, g, dy):
        xhat, r = _rmsn(x)
        dxh = dy * g
        return r * (dxh - xhat * jnp.mean(dxh * xhat, axis=-1, keepdims=True)), jnp.sum(dy * xhat, axis=0, keepdims=True)

    def body(a_ref, dq_ref, dk_ref, dkr_ref, dz_ref, qg_ref, kg_ref, cos_ref, sin_ref, da_ref, dqg_ref, dkg_ref):
        i = pl.program_id(0)
        da_ref[:, 0:MLA_DI] = dz_ref[...]
        dcq, dqg = norm_bwd(a_ref[:, MLA_CQ0:MLA_CKV0], qg_ref[...], dq_ref[...])
        da_ref[:, MLA_CQ0:MLA_CKV0] = dcq.astype(BF16)
        dckv, dkg = norm_bwd(a_ref[:, MLA_CKV0:MLA_KR0], kg_ref[...], dk_ref[...])
        da_ref[:, MLA_CKV0:MLA_KR0] = dckv.astype(BF16)
        dk2 = dkr_ref[0]
        for p in range(1, npair):
            dk2 = dk2 + dkr_ref[p]
        dk2 = _rope_t(dk2, cos_ref[...], sin_ref[...])
        dk2 = dk2 + pltpu.roll(dk2, MLA_ROPE, 1)
        lane = lax.broadcasted_iota(jnp.int32, dk2.shape, 1)
        da_ref[:, MLA_KR0:MLA_AW] = jnp.where(lane < MLA_ROPE, dk2, 0.0).astype(BF16)
        _acc(dqg_ref, dqg, i)
        _acc(dkg_ref, dkg, i)

    st = jax.ShapeDtypeStruct
    return _rowcall(body, name, l // tl,
                    [_rows(tl, MLA_AW), _rows(tl, MLA_Q_RANK), _rows(tl, MLA_KV_RANK),
                     pl.BlockSpec((npair, tl, LANES), lambda i: (0, i, 0)), _rows(tl, MLA_DI),
                     _full((1, MLA_Q_RANK)), _full((1, MLA_KV_RANK)), _rows(tl, LANES), _rows(tl, LANES)],
                    [_rows(tl, MLA_AW), _full((1, MLA_Q_RANK)), _full((1, MLA_KV_RANK))],
                    [st((l, MLA_AW), BF16), st((1, MLA_Q_RANK), F32), st((1, MLA_KV_RANK), F32)])(
        a, dcqn, dckvn, dkr_pairs, dz, q_g.reshape(1, -1), kv_g.reshape(1, -1), cos, sins)


def _mla_w_in_perm(w):
    r = MLA_Q_RANK + MLA_KV_RANK + MLA_ROPE
    pad = jnp.zeros(w.shape[:-1] + (MLA_AW - MLA_KR0 - MLA_ROPE,), w.dtype)
    return jnp.concatenate([w[..., r:], w[..., :r], pad], axis=-1)


def _mla_w_in_unperm(g):
    r = MLA_Q_RANK + MLA_KV_RANK + MLA_ROPE
    return jnp.concatenate([g[..., MLA_DI:MLA_DI + r], g[..., :MLA_DI]], axis=-1)


def _mla_w_uq_split(w):
    k = w.shape[0]
    w3 = w.reshape(k, MLA_HEADS, MLA_NOPE + MLA_ROPE)
    return w3[:, :, :MLA_NOPE].reshape(k, MLA_HEADS * MLA_NOPE), w3[:, :, MLA_NOPE:].reshape(k, MLA_HEADS * MLA_ROPE)


def _mla_w_uq_merge(gn, gr):
    k = gn.shape[0]
    return jnp.concatenate([gn.reshape(k, MLA_HEADS, MLA_NOPE), gr.reshape(k, MLA_HEADS, MLA_ROPE)], axis=2).reshape(k, -1)


def mla_layer_fwd(h, p, wf, cos, sins, tag):
    hn = rms_fwd(h, p["norm_g"], tag + "_rms")
    w_in = _mla_w_in_perm(wf["w_in"])
    w_uq_n, w_uq_r = _mla_w_uq_split(wf["w_uq"])
    a = matmul(hn, w_in, "nn", tag + "_mm_in")
    cqn, ckvn, krs = mla_pre(a, p["q_norm_g"], p["kv_norm_g"], cos, sins, tag + "_pre")
    qn = matmul(cqn, w_uq_n, "nn", tag + "_mm_qn", out_dtype=BF16)
    qr_raw = matmul(cqn, w_uq_r, "nn", tag + "_mm_qr")
    qr = mla_rope_q(qr_raw, cos, sins, tag + "_rope_q")
    kv = matmul(ckvn, wf["w_ukv"], "nn", tag + "_mm_kv", out_dtype=BF16)
    o, lse = flash_fwd(qn, qr, kv, krs, tag + "_flash")
    m = mla_gate_fwd(o, a, tag + "_gate")
    h_out = matmul(m, wf["w_out"], "nn", tag + "_mm_out", add=h)
    return h_out, (h, hn, a, cqn, ckvn, krs, qn, qr, kv, o, lse, m, w_in, w_uq_n, w_uq_r)


def mla_layer_bwd(dh_out, saved, p, wf, cos, sins, tag):
    h, hn, a, cqn, ckvn, krs, qn, qr, kv, o, lse, m, w_in, w_uq_n, w_uq_r = saved
    l = h.shape[0]
    dm = matmul(dh_out, wf["w_out"], "nt", tag + "_mm_dm")
    g_w_out = matmul(m, dh_out, "tn", tag + "_mm_gwout")
    do, dz, delta = mla_gate_bwd(dm, o, a, tag + "_gate_bwd")
    lse_row = lse.reshape(MLA_HEADS, l // ATT_T, 1, ATT_T)
    delta_row = delta.reshape(MLA_HEADS, l // ATT_T, 1, ATT_T)
    dkv, dkr_pairs = flash_dkv(qn, qr, kv, krs, do, lse_row, delta_row, tag + "_flash_dkv")
    dqn, dqr = flash_dq(qn, qr, kv, krs, do, lse, delta, cos, sins, tag + "_flash_dq")
    dcqn = matmul(dqn, w_uq_n, "nt", tag + "_mm_dcq_n")
    dcqn = matmul(dqr, w_uq_r, "nt", tag + "_mm_dcq_r", add=dcqn)
    g_uq_n = matmul(cqn, dqn, "tn", tag + "_mm_guq_n")
    g_uq_r = matmul(cqn, dqr, "tn", tag + "_mm_guq_r")
    dckvn = matmul(dkv, wf["w_ukv"], "nt", tag + "_mm_dckv")
    g_ukv = matmul(ckvn, dkv, "tn", tag + "_mm_gukv")
    da, dqg, dkg = mla_post(a, dcqn, dckvn, dkr_pairs, dz, p["q_norm_g"], p["kv_norm_g"], cos, sins, tag + "_post")
    dhn = matmul(da, w_in, "nt", tag + "_mm_dhn")
    g_w_in = matmul(hn, da, "tn", tag + "_mm_gwin")
    dh, dng = rms_bwd(h, p["norm_g"], dhn, dh_out, tag + "_rms_bwd")
    grads = {"norm_g": dng.reshape(-1), "w_in": _mla_w_in_unperm(g_w_in), "q_norm_g": dqg.reshape(-1),
             "w_uq": _mla_w_uq_merge(g_uq_n, g_uq_r), "kv_norm_g": dkg.reshape(-1), "w_ukv": g_ukv, "w_out": g_w_out}
    return dh, grads


ANY = pl.BlockSpec(memory_space=pl.ANY)


def _me():
    return lax.axis_index("x"), lax.axis_index("y"), lax.axis_index("c")


def _other_chips(x, y):
    return [(1 - x, y), (x, 1 - y), (1 - x, 1 - y)]


def _rcopy(src, dst, ssem, rsem, dev):
    return pltpu.make_async_remote_copy(src_ref=src, dst_ref=dst, send_sem=ssem, recv_sem=rsem,
                                        device_id=dev, device_id_type=MESH)


def _half(ref, c, hf):
    return ref.at[pl.ds(c * hf, hf), :]


def weights_allgather(wb):
    nr, w = wb.shape
    hf = nr // 2

    def body(w_ref, o_ref, ssem, rsem, lsem):
        x, y, c = _me()
        k = 2 * x + y
        chips = _other_chips(x, y)
        mine = pltpu.make_async_copy(w_ref, o_ref.at[k], lsem)
        mine.start()
        first = [_rcopy(_half(w_ref, c, hf), _half(o_ref.at[k], c, hf), ssem.at[j], rsem.at[j], (cx, cy, c))
                 for j, (cx, cy) in enumerate(chips)]
        for cp in first:
            cp.start()
        passed = []
        for j, (cx, cy) in enumerate(chips):
            region = _half(o_ref.at[2 * cx + cy], c, hf)
            _rcopy(region, region, ssem.at[j], rsem.at[j], (cx, cy, c)).wait_recv()
            fwd = _rcopy(region, region, ssem.at[3 + j], rsem.at[3 + j], (x, y, 1 - c))
            fwd.start()
            passed.append(fwd)
        for j, (cx, cy) in enumerate(chips):
            region = _half(o_ref.at[2 * cx + cy], 1 - c, hf)
            _rcopy(region, region, ssem.at[3 + j], rsem.at[3 + j], (x, y, 1 - c)).wait_recv()
        for cp in first + passed:
            cp.wait_send()
        mine.wait()

    return pl.pallas_call(
        body, name="weights_allgather", in_specs=[ANY], out_specs=ANY,
        out_shape=jax.ShapeDtypeStruct((N_CHIPS, nr, w), wb.dtype),
        scratch_shapes=[pltpu.SemaphoreType.DMA((6,)), pltpu.SemaphoreType.DMA((6,)), pltpu.SemaphoreType.DMA(())],
    )(wb)


def grads_to_sibling(p):
    _, nr, w = p.shape
    hf = nr // 2

    def body(p_ref, o_ref, ssem, rsem):
        x, y, c = _me()
        cp = _rcopy(p_ref.at[:, pl.ds((1 - c) * hf, hf), :], o_ref, ssem, rsem, (x, y, 1 - c))
        cp.start()
        cp.wait()

    return pl.pallas_call(
        body, name="grads_to_sibling", in_specs=[ANY], out_specs=ANY,
        out_shape=jax.ShapeDtypeStruct((N_CHIPS, hf, w), p.dtype),
        scratch_shapes=[pltpu.SemaphoreType.DMA(()), pltpu.SemaphoreType.DMA(())])(p)


def pair_sum(p, ra):
    _, nr, w = p.shape
    hf = nr // 2
    tr = _pick_rows(hf)
    nb = hf // tr

    def body(c_ref, p_ref, r_ref, o_ref):
        o_ref[...] = p_ref[...] + r_ref[...]

    c = lax.axis_index("c").astype(jnp.int32).reshape(1)
    return pl.pallas_call(
        body, name="pair_sum",
        grid_spec=pltpu.PrefetchScalarGridSpec(
            num_scalar_prefetch=1, grid=(N_CHIPS, nb),
            in_specs=[pl.BlockSpec((1, tr, w), lambda k, i, c_ref: (k, c_ref[0] * nb + i, 0)),
                      pl.BlockSpec((1, tr, w), lambda k, i, c_ref: (k, i, 0))],
            out_specs=pl.BlockSpec((1, tr, w), lambda k, i, c_ref: (k, i, 0))),
        out_shape=jax.ShapeDtypeStruct((N_CHIPS, hf, w), F32),
        compiler_params=_cparams(("parallel", "parallel")))(c, p, ra)


def grads_across_chips(t):
    _, hf, w = t.shape

    def body(t_ref, o_ref, ssem, rsem, lsem):
        x, y, c = _me()
        k = 2 * x + y
        chips = _other_chips(x, y)
        mine = pltpu.make_async_copy(t_ref.at[k], o_ref.at[k], lsem)
        mine.start()
        sends = [_rcopy(t_ref.at[2 * cx + cy], o_ref.at[k], ssem.at[j], rsem.at[j], (cx, cy, c))
                 for j, (cx, cy) in enumerate(chips)]
        for cp in sends:
            cp.start()
        for j, (cx, cy) in enumerate(chips):
            _rcopy(t_ref.at[k], o_ref.at[2 * cx + cy], ssem.at[j], rsem.at[j], (cx, cy, c)).wait_recv()
        for cp in sends:
            cp.wait_send()
        mine.wait()

    return pl.pallas_call(
        body, name="grads_across_chips", in_specs=[ANY], out_specs=ANY,
        out_shape=jax.ShapeDtypeStruct((N_CHIPS, hf, w), t.dtype),
        scratch_shapes=[pltpu.SemaphoreType.DMA((3,)), pltpu.SemaphoreType.DMA((3,)), pltpu.SemaphoreType.DMA(())])(t)


def chip_sum(rb):
    _, hf, w = rb.shape
    tr = _pick_rows(hf)

    def body(r_ref, o_ref):
        o_ref[...] = ((r_ref[0] + r_ref[1]) + r_ref[2]) + r_ref[3]

    return pl.pallas_call(
        body, name="chip_sum", grid=(hf // tr,),
        in_specs=[pl.BlockSpec((N_CHIPS, tr, w), lambda i: (0, i, 0))],
        out_specs=pl.BlockSpec((tr, w), lambda i: (i, 0)),
        out_shape=jax.ShapeDtypeStruct((hf, w), F32), compiler_params=_cparams(("parallel",)))(rb)


def reduced_to_sibling(f):
    hf, w = f.shape

    def body(f_ref, o_ref, ssem, rsem, lsem):
        x, y, c = _me()
        mine = pltpu.make_async_copy(f_ref, _half(o_ref, c, hf), lsem)
        mine.start()
        cp = _rcopy(f_ref, _half(o_ref, c, hf), ssem, rsem, (x, y, 1 - c))
        cp.start()
        _rcopy(f_ref, _half(o_ref, 1 - c, hf), ssem, rsem, (x, y, 1 - c)).wait_recv()
        cp.wait_send()
        mine.wait()

    return pl.pallas_call(
        body, name="reduced_to_sibling", in_specs=[ANY], out_specs=ANY,
        out_shape=jax.ShapeDtypeStruct((2 * hf, w), f.dtype),
        scratch_shapes=[pltpu.SemaphoreType.DMA(()), pltpu.SemaphoreType.DMA(()), pltpu.SemaphoreType.DMA(())])(f)


def small_allgather(g, row0, nrs):
    w = g.shape[1]

    def body(g_ref, o_ref, ssem, rsem, lsem):
        x, y, c = _me()
        k = 2 * x + y
        chips = _other_chips(x, y)
        src = g_ref.at[pl.ds(row0, nrs), :]
        mine = pltpu.make_async_copy(src, o_ref.at[k], lsem)
        mine.start()
        sends = [_rcopy(src, o_ref.at[k], ssem.at[j], rsem.at[j], (cx, cy, c)) for j, (cx, cy) in enumerate(chips)]
        for cp in sends:
            cp.start()
        for j, (cx, cy) in enumerate(chips):
            _rcopy(src, o_ref.at[2 * cx + cy], ssem.at[j], rsem.at[j], (cx, cy, c)).wait_recv()
        for cp in sends:
            cp.wait_send()
        mine.wait()

    return pl.pallas_call(
        body, name="small_allgather", in_specs=[ANY], out_specs=ANY,
        out_shape=jax.ShapeDtypeStruct((N_CHIPS, nrs, w), g.dtype),
        scratch_shapes=[pltpu.SemaphoreType.DMA((3,)), pltpu.SemaphoreType.DMA((3,)), pltpu.SemaphoreType.DMA(())])(g)


def adamw(w, g, m, v, name):
    r, wd = w.shape
    tr = _pick_rows(r)
    bc1 = 1.0 - ADAM_B1 ** ADAM_STEP
    bc2 = 1.0 - ADAM_B2 ** ADAM_STEP

    def body(w_ref, g_ref, m_ref, v_ref, d_ref, nm_ref, nv_ref):
        gv = g_ref[...]
        nm = ADAM_B1 * m_ref[...] + (1.0 - ADAM_B1) * gv
        nv = ADAM_B2 * v_ref[...] + (1.0 - ADAM_B2) * (gv * gv)
        nm_ref[...] = nm
        nv_ref[...] = nv
        d_ref[...] = -ADAM_LR * ((nm / bc1) / (jnp.sqrt(nv / bc2) + ADAM_EPS) + ADAM_WD * w_ref[...])

    spec = pl.BlockSpec((tr, wd), lambda i: (i, 0))
    st = jax.ShapeDtypeStruct((r, wd), F32)
    return pl.pallas_call(body, name=name, grid=(r // tr,), in_specs=[spec] * 4, out_specs=[spec] * 3,
                          out_shape=[st, st, st], compiler_params=_cparams(("parallel",)))(w, g, m, v)


LAYER_KINDS = ("gmlp", "s5", "mla", "gmlp")
PARAMS = {
    "gmlp": ("norm_g", "w_in", "ln_g", "ln_b", "w_s", "b_s", "w_out"),
    "s5": ("norm_g", "w_in", "a_re", "a_im", "log_step", "b_re", "b_im", "c_re", "c_im", "d_skip", "w_glu", "b_glu", "w_out"),
    "mla": ("norm_g", "w_in", "q_norm_g", "w_uq", "kv_norm_g", "w_ukv", "w_out"),
}
COL_SHARDED = ("w_in", "w_uq", "w_ukv")
ROW_SHARDED = ("w_out", "w_glu")
WEIGHT_NAMES = [("l%d_" % i) + n for i, kind in enumerate(LAYER_KINDS) for n in PARAMS[kind]] + ["final_norm_g"]


def _is_big(name):
    return name.split("_", 1)[1] in COL_SHARDED + ROW_SHARDED


BIG = [n for n in WEIGHT_NAMES if _is_big(n)]
SMALL = [n for n in WEIGHT_NAMES if not _is_big(n)]


def _pack_rows(blocks):
    return jnp.concatenate([b.reshape(-1, PACK_W) for b in blocks], axis=0)


def _shard_major(name, full):
    r, c = full.shape
    if name.split("_", 1)[1] in COL_SHARDED:
        t = full.reshape(r, N_CHIPS, c // N_CHIPS).transpose(1, 0, 2)
    else:
        t = full.reshape(N_CHIPS, r // N_CHIPS, c)
    return t.reshape(N_CHIPS, -1, PACK_W)


def _from_shard_major(name, t, block_shape):
    r, c = block_shape
    if name.split("_", 1)[1] in COL_SHARDED:
        return t.reshape(N_CHIPS, r, c).transpose(1, 0, 2).reshape(r, N_CHIPS * c)
    return t.reshape(N_CHIPS * r, c)


def _small_pack(arrs, total_padded):
    flat = jnp.concatenate([a.reshape(-1) for a in arrs])
    return jnp.pad(flat, (0, total_padded - flat.shape[0]))


def kernel(x, positions, l0_norm_g, l0_w_in, l0_ln_g, l0_ln_b, l0_w_s, l0_b_s, l0_w_out, l1_norm_g, l1_w_in, l1_a_re, l1_a_im, l1_log_step, l1_b_re, l1_b_im, l1_c_re, l1_c_im, l1_d_skip, l1_w_glu, l1_b_glu, l1_w_out, l2_norm_g, l2_w_in, l2_q_norm_g, l2_w_uq, l2_kv_norm_g, l2_w_ukv, l2_w_out, l3_norm_g, l3_w_in, l3_ln_g, l3_ln_b, l3_w_s, l3_b_s, l3_w_out, final_norm_g, loss_target, m_l0_norm_g, m_l0_w_in, m_l0_ln_g, m_l0_ln_b, m_l0_w_s, m_l0_b_s, m_l0_w_out, m_l1_norm_g, m_l1_w_in, m_l1_a_re, m_l1_a_im, m_l1_log_step, m_l1_b_re, m_l1_b_im, m_l1_c_re, m_l1_c_im, m_l1_d_skip, m_l1_w_glu, m_l1_b_glu, m_l1_w_out, m_l2_norm_g, m_l2_w_in, m_l2_q_norm_g, m_l2_w_uq, m_l2_kv_norm_g, m_l2_w_ukv, m_l2_w_out, m_l3_norm_g, m_l3_w_in, m_l3_ln_g, m_l3_ln_b, m_l3_w_s, m_l3_b_s, m_l3_w_out, m_final_norm_g, v_l0_norm_g, v_l0_w_in, v_l0_ln_g, v_l0_ln_b, v_l0_w_s, v_l0_b_s, v_l0_w_out, v_l1_norm_g, v_l1_w_in, v_l1_a_re, v_l1_a_im, v_l1_log_step, v_l1_b_re, v_l1_b_im, v_l1_c_re, v_l1_c_im, v_l1_d_skip, v_l1_w_glu, v_l1_b_glu, v_l1_w_out, v_l2_norm_g, v_l2_w_in, v_l2_q_norm_g, v_l2_w_uq, v_l2_kv_norm_g, v_l2_w_ukv, v_l2_w_out, v_l3_norm_g, v_l3_w_in, v_l3_ln_g, v_l3_ln_b, v_l3_w_s, v_l3_b_s, v_l3_w_out, v_final_norm_g):
    args = locals()
    w = {n: args[n] for n in WEIGHT_NAMES}
    mom_m = {n: args["m_" + n] for n in WEIGHT_NAMES}
    mom_v = {n: args["v_" + n] for n in WEIGHT_NAMES}
    h0 = x[0]
    target = loss_target[0]
    pos = positions.reshape(-1, 1)

    big_rows = [w[n].size // PACK_W for n in BIG]
    nrb = sum(big_rows)
    w_big = _pack_rows([w[n] for n in BIG])
    gathered = weights_allgather(jnp.pad(w_big.astype(BF16), ((0, -nrb % 32), (0, 0))))
    full = {}
    r0 = 0
    for n, nr in zip(BIG, big_rows):
        full[n] = _from_shard_major(n, gathered[:, r0:r0 + nr, :], w[n].shape)
        r0 += nr

    def layer_params(i):
        pre = "l%d_" % i
        p = {k[len(pre):]: v for k, v in w.items() if k.startswith(pre)}
        wf = {k[len(pre):]: v for k, v in full.items() if k.startswith(pre)}
        return p, wf

    cos, sins = rope_tables(pos)
    h = h0
    saved = []
    for i, kind in enumerate(LAYER_KINDS):
        p, wf = layer_params(i)
        tag = "l%d" % i
        if kind == "gmlp":
            h, s = gmlp_layer_fwd(h, p, wf, tag)
        elif kind == "s5":
            h, s = s5_layer_fwd(h, p, wf, tag)
        else:
            h, s = mla_layer_fwd(h, p, wf, cos, sins, tag)
        saved.append(s)
    loss_part, dh, g_final = loss_head(h, final_norm_g, target)
    loss = lax.psum(loss_part[0, 0], ("x", "y", "c"))

    grads = {"final_norm_g": g_final.reshape(-1)}
    for i in reversed(range(len(LAYER_KINDS))):
        kind = LAYER_KINDS[i]
        p, wf = layer_params(i)
        tag = "l%d" % i
        if kind == "gmlp":
            dh, g = gmlp_layer_bwd(dh, saved[i], p, wf, tag)
        elif kind == "s5":
            dh, g = s5_layer_bwd(dh, saved[i], p, wf, tag)
        else:
            dh, g = mla_layer_bwd(dh, saved[i], p, wf, cos, sins, tag)
        for k, val in g.items():
            grads["l%d_%s" % (i, k)] = val
    grad_x = dh[None]

    n_small = sum(w[n].size for n in SMALL)
    piece = N_CHIPS * 2 * 8 * PACK_W
    n_small_pad = -(-n_small // piece) * piece
    nrs = n_small_pad // N_CHIPS // PACK_W
    packed = jnp.concatenate(
        [_shard_major(n, grads[n]) for n in BIG]
        + [_small_pack([grads[n] for n in SMALL], n_small_pad).reshape(N_CHIPS, nrs, PACK_W)], axis=1)
    from_sibling = grads_to_sibling(packed)
    chip_part = pair_sum(packed, from_sibling)
    from_chips = grads_across_chips(chip_part)
    reduced_half = chip_sum(from_chips)
    reduced = reduced_to_sibling(reduced_half)
    small_all = small_allgather(reduced, nrb, nrs)

    g_big = reduced[:nrb]
    d_big, nm_big, nv_big = adamw(w_big, g_big, _pack_rows([mom_m[n] for n in BIG]),
                                  _pack_rows([mom_v[n] for n in BIG]), "adamw_big")
    g_small = small_all.reshape(-1, PACK_W)
    sp = lambda d: _small_pack([d[n] for n in SMALL], n_small_pad).reshape(-1, PACK_W)
    d_small, nm_small, nv_small = adamw(sp(w), g_small, sp(mom_m), sp(mom_v), "adamw_small")

    def unpack(big_buf, small_buf):
        out = {}
        r = 0
        for n, nr in zip(BIG, big_rows):
            out[n] = big_buf[r:r + nr].reshape(w[n].shape)
            r += nr
        flat = small_buf.reshape(-1)
        o = 0
        for n in SMALL:
            out[n] = flat[o:o + w[n].size].reshape(w[n].shape)
            o += w[n].size
        return out

    g_out = unpack(g_big, g_small)
    d_out = unpack(d_big, d_small)
    nm_out = unpack(nm_big, nm_small)
    nv_out = unpack(nv_big, nv_small)
    return (loss, grad_x, *[g_out[n] for n in WEIGHT_NAMES], *[d_out[n] for n in WEIGHT_NAMES],
            *[nm_out[n] for n in WEIGHT_NAMES], *[nv_out[n] for n in WEIGHT_NAMES])
```

```python
import functools
import math

import jax
import jax.numpy as jnp
import numpy as np
from jax import lax
from jax.experimental import pallas as pl
from jax.experimental.pallas import tpu as pltpu

F32 = jnp.float32
BF16 = jnp.bfloat16
MESH = pl.DeviceIdType.MESH
VMEM_LIMIT_BYTES = 56 * 1024 * 1024
LANES = 128
PACK_W = 1024
ROW_TILE = 256
MM_BLOCK_BYTES = 6 * 1024 * 1024

NORM_EPS = 1e-6
N_CHIPS = 4
GMLP_CHUNK = 128
GMLP_GROUPS = 8
S5_GROUPS = 128
S5_GROUP = 16
S5_STATE = 64
S5_SB = 16
S5_SEG = 8
MLA_HEADS = 16
MLA_NOPE = 128
MLA_ROPE = 64
MLA_Q_RANK = 384
MLA_KV_RANK = 128
MLA_SCALE = (MLA_NOPE + MLA_ROPE) ** -0.5
ROPE_THETA = 10000.0
NEG_INF = -1e30
ADAM_LR, ADAM_B1, ADAM_B2, ADAM_EPS, ADAM_WD, ADAM_STEP = 0.001, 0.9, 0.999, 1e-08, 0.01, 10

DN_NN = (((1,), (0,)), ((), ()))
DN_NT = (((1,), (1,)), ((), ()))
DN_TN = (((0,), (0,)), ((), ()))


def _cparams(sem):
    return pltpu.CompilerParams(dimension_semantics=sem, vmem_limit_bytes=VMEM_LIMIT_BYTES)


def _pick(n, cands=(512, 384, 256, 128)):
    for c in cands:
        if n % c == 0:
            return c
    return n


def _pick_rows(r, cap=512):
    return max(t for t in range(8, cap + 1, 8) if r % t == 0)


def _dot(a, b, dn):
    return lax.dot_general(a.astype(BF16), b.astype(BF16), dn, preferred_element_type=F32)


def _sigmoid(x):
    return 1.0 / (1.0 + jnp.exp(-x))


def _gelu(x):
    c = math.sqrt(2.0 / math.pi)
    t = jnp.tanh(c * (x + 0.044715 * x * x * x))
    return 0.5 * x * (1.0 + t)


def _gelu_grad(x):
    c = math.sqrt(2.0 / math.pi)
    t = jnp.tanh(c * (x + 0.044715 * x * x * x))
    return 0.5 * (1.0 + t) + 0.5 * x * (1.0 - t * t) * c * (1.0 + 3.0 * 0.044715 * x * x)


def _silu(z):
    return z * _sigmoid(z)


def _silu_grad(z):
    s = _sigmoid(z)
    return s * (1.0 + z * (1.0 - s))


def matmul(a, b, mode, name, out_dtype=F32, add=None):
    if mode == "nn":
        (m, k), n = a.shape, b.shape[1]
    elif mode == "nt":
        (m, k), n = a.shape, b.shape[0]
    else:
        (k, m), n = a.shape, b.shape[1]
    tm = _pick(m, [t for t in (1024, 512, 384, 256, 128) if t * k * a.dtype.itemsize <= MM_BLOCK_BYTES])
    tn = _pick(n, [t for t in (512, 384, 256, 128) if t * k * b.dtype.itemsize <= MM_BLOCK_BYTES])
    dn = {"nn": DN_NN, "nt": DN_NT, "tn": DN_TN}[mode]

    def body(*refs):
        if add is None:
            a_ref, b_ref, o_ref = refs
        else:
            a_ref, b_ref, add_ref, o_ref = refs
        r = _dot(a_ref[...], b_ref[...], dn)
        if add is not None:
            r = r + add_ref[...].astype(F32)
        o_ref[...] = r.astype(out_dtype)

    a_spec = pl.BlockSpec((k, tm), lambda i, j: (0, i)) if mode == "tn" else pl.BlockSpec((tm, k), lambda i, j: (i, 0))
    b_spec = pl.BlockSpec((tn, k), lambda i, j: (j, 0)) if mode == "nt" else pl.BlockSpec((k, tn), lambda i, j: (0, j))
    o_spec = pl.BlockSpec((tm, tn), lambda i, j: (i, j))
    in_specs = [a_spec, b_spec] + ([o_spec] if add is not None else [])
    args = (a, b) + ((add,) if add is not None else ())
    return pl.pallas_call(
        body, name=name, grid=(m // tm, n // tn), in_specs=in_specs, out_specs=o_spec,
        out_shape=jax.ShapeDtypeStruct((m, n), out_dtype),
        compiler_params=_cparams(("parallel", "arbitrary")))(*args)


def _rows(tl, w, col=0):
    return pl.BlockSpec((tl, w), lambda i: (i, col))


def _full(shape):
    nd = len(shape)
    return pl.BlockSpec(tuple(shape), lambda i: (0,) * nd)


def _rowcall(body, name, n_steps, in_specs, out_specs, out_shape, scratch=()):
    return pl.pallas_call(
        body, name=name, grid=(n_steps,), in_specs=in_specs, out_specs=out_specs, out_shape=out_shape,
        scratch_shapes=list(scratch), compiler_params=_cparams(("arbitrary",)))


def _acc(ref, val, i):
    @pl.when(i == 0)
    def _():
        ref[...] = val

    @pl.when(i != 0)
    def _():
        ref[...] += val


def rms_fwd(h, g, name):
    l, d = h.shape
    tl = ROW_TILE

    def body(h_ref, g_ref, o_ref):
        x = h_ref[...]
        r = lax.rsqrt(jnp.mean(x * x, axis=-1, keepdims=True) + NORM_EPS)
        o_ref[...] = (x * r * g_ref[...]).astype(BF16)

    return _rowcall(body, name, l // tl, [_rows(tl, d), _full((1, d))], _rows(tl, d),
                    jax.ShapeDtypeStruct((l, d), BF16))(h, g.reshape(1, d))


def rms_bwd(h, g, dhn, dh_in, name):
    l, d = h.shape
    tl = ROW_TILE

    def body(h_ref, g_ref, dhn_ref, dhi_ref, dh_ref, dg_ref):
        i = pl.program_id(0)
        x = h_ref[...]
        r = lax.rsqrt(jnp.mean(x * x, axis=-1, keepdims=True) + NORM_EPS)
        xhat = x * r
        dy = dhn_ref[...]
        dxh = dy * g_ref[...]
        dx = r * (dxh - xhat * jnp.mean(dxh * xhat, axis=-1, keepdims=True))
        dh_ref[...] = dhi_ref[...] + dx
        _acc(dg_ref, jnp.sum(dy * xhat, axis=0, keepdims=True), i)

    return _rowcall(body, name, l // tl, [_rows(tl, d), _full((1, d)), _rows(tl, d), _rows(tl, d)],
                    [_rows(tl, d), _full((1, d))],
                    [jax.ShapeDtypeStruct((l, d), F32), jax.ShapeDtypeStruct((1, d), F32)])(h, g.reshape(1, d), dhn, dh_in)


def loss_head(h, g, target):
    l, d = h.shape
    tl = ROW_TILE

    def body(h_ref, g_ref, t_ref, loss_ref, dh_ref, dg_ref):
        i = pl.program_id(0)
        x = h_ref[...]
        gg = g_ref[...]
        r = lax.rsqrt(jnp.mean(x * x, axis=-1, keepdims=True) + NORM_EPS)
        xhat = x * r
        err = xhat * gg - t_ref[...]
        part = 0.5 * jnp.sum(jnp.mean(err * err, axis=-1, keepdims=True), axis=0, keepdims=True)
        _acc(loss_ref, part, i)
        dy = err * (1.0 / d)
        dxh = dy * gg
        dh_ref[...] = r * (dxh - xhat * jnp.mean(dxh * xhat, axis=-1, keepdims=True))
        _acc(dg_ref, jnp.sum(dy * xhat, axis=0, keepdims=True), i)

    return _rowcall(body, "loss_head", l // tl, [_rows(tl, d), _full((1, d)), _rows(tl, d)],
                    [_full((1, 1)), _rows(tl, d), _full((1, d))],
                    [jax.ShapeDtypeStruct((1, 1), F32), jax.ShapeDtypeStruct((l, d), F32),
                     jax.ShapeDtypeStruct((1, d), F32)])(h, g.reshape(1, d), target)


def _gmlp_common(a_ref, lng_ref, lnb_ref):
    di = lng_ref.shape[1]
    u_pre = a_ref[:, 0:di]
    v_pre = a_ref[:, di:2 * di]
    z = a_ref[:, 2 * di:3 * di]
    vg = _gelu(v_pre)
    mu = jnp.mean(vg, axis=-1, keepdims=True)
    xc = vg - mu
    rstd = lax.rsqrt(jnp.mean(xc * xc, axis=-1, keepdims=True) + NORM_EPS)
    vhat = xc * rstd
    vn = vhat * lng_ref[...] + lnb_ref[...]
    return u_pre, v_pre, z, vhat, rstd, vn


def _tril(w):
    r = lax.broadcasted_iota(jnp.int32, w.shape, 0)
    c = lax.broadcasted_iota(jnp.int32, w.shape, 1)
    return jnp.where(c <= r, w, 0.0)


def gmlp_gate_fwd(a, ln_g, ln_b, w_s, b_s, name):
    l, w3 = a.shape
    di = w3 // 3
    dg = di // GMLP_GROUPS
    tl = GMLP_CHUNK

    def body(a_ref, lng_ref, lnb_ref, ws_ref, bs_ref, m_ref):
        u_pre, _, z, _, _, vn = _gmlp_common(a_ref, lng_ref, lnb_ref)
        gate = _gelu(u_pre) * _silu(z)
        for g in range(GMLP_GROUPS):
            sl = slice(g * dg, (g + 1) * dg)
            s = _dot(_tril(ws_ref[g]), vn[:, sl], DN_NN) + bs_ref[g]
            m_ref[:, sl] = (gate[:, sl] * s).astype(BF16)

    return _rowcall(body, name, l // tl,
                    [_rows(tl, w3), _full((1, di)), _full((1, di)), _full(w_s.shape), _full((GMLP_GROUPS, tl, 1))],
                    _rows(tl, di), jax.ShapeDtypeStruct((l, di), BF16))(
        a, ln_g.reshape(1, di), ln_b.reshape(1, di), w_s, b_s.reshape(GMLP_GROUPS, tl, 1))


def gmlp_gate_bwd(a, dm, ln_g, ln_b, w_s, b_s, name):
    l, w3 = a.shape
    di = w3 // 3
    dg = di // GMLP_GROUPS
    tl = GMLP_CHUNK

    def body(a_ref, dm_ref, lng_ref, lnb_ref, ws_ref, bs_ref, da_ref, dlg_ref, dlb_ref, dws_ref, dbs_ref, dvn_ref):
        i = pl.program_id(0)
        u_pre, v_pre, z, vhat, rstd, vn = _gmlp_common(a_ref, lng_ref, lnb_ref)
        dm_v = dm_ref[...]
        u = _gelu(u_pre)
        sz = _silu(z)
        for g in range(GMLP_GROUPS):
            sl = slice(g * dg, (g + 1) * dg)
            wt = _tril(ws_ref[g])
            vn_g = vn[:, sl]
            s = _dot(wt, vn_g, DN_NN) + bs_ref[g]
            dmg = dm_v[:, sl]
            ds = dmg * u[:, sl] * sz[:, sl]
            da_ref[:, sl] = (dmg * s * sz[:, sl] * _gelu_grad(u_pre[:, sl])).astype(BF16)
            da_ref[:, 2 * di + g * dg:2 * di + (g + 1) * dg] = (
                dmg * u[:, sl] * s * _silu_grad(z[:, sl])).astype(BF16)
            dvn_ref[:, sl] = _dot(wt, ds, DN_TN)
            dw = _tril(_dot(ds, vn_g, DN_NT))
            db = jnp.sum(ds, axis=1, keepdims=True)

            @pl.when(i == 0)
            def _():
                dws_ref[g] = dw
                dbs_ref[g] = db

            @pl.when(i != 0)
            def _():
                dws_ref[g] += dw
                dbs_ref[g] += db

        dvn = dvn_ref[...]
        dxh = dvn * lng_ref[...]
        dvg = rstd * (dxh - jnp.mean(dxh, axis=-1, keepdims=True) - vhat * jnp.mean(dxh * vhat, axis=-1, keepdims=True))
        da_ref[:, di:2 * di] = (dvg * _gelu_grad(v_pre)).astype(BF16)
        _acc(dlg_ref, jnp.sum(dvn * vhat, axis=0, keepdims=True), i)
        _acc(dlb_ref, jnp.sum(dvn, axis=0, keepdims=True), i)

    outs = _rowcall(
        body, name, l // tl,
        [_rows(tl, w3), _rows(tl, di), _full((1, di)), _full((1, di)), _full(w_s.shape), _full((GMLP_GROUPS, tl, 1))],
        [_rows(tl, w3), _full((1, di)), _full((1, di)), _full(w_s.shape), _full((GMLP_GROUPS, tl, 1))],
        [jax.ShapeDtypeStruct((l, w3), BF16), jax.ShapeDtypeStruct((1, di), F32), jax.ShapeDtypeStruct((1, di), F32),
         jax.ShapeDtypeStruct(w_s.shape, F32), jax.ShapeDtypeStruct((GMLP_GROUPS, tl, 1), F32)],
        scratch=[pltpu.VMEM((tl, di), F32)])(
        a, dm, ln_g.reshape(1, di), ln_b.reshape(1, di), w_s, b_s.reshape(GMLP_GROUPS, tl, 1))
    return outs


def gmlp_layer_fwd(h, p, wf, tag):
    hn = rms_fwd(h, p["norm_g"], tag + "_rms")
    a = matmul(hn, wf["w_in"], "nn", tag + "_mm_in")
    m = gmlp_gate_fwd(a, p["ln_g"], p["ln_b"], p["w_s"], p["b_s"], tag + "_gate")
    h_out = matmul(m, wf["w_out"], "nn", tag + "_mm_out", add=h)
    return h_out, (h, hn, a, m)


def gmlp_layer_bwd(dh_out, saved, p, wf, tag):
    h, hn, a, m = saved
    dm = matmul(dh_out, wf["w_out"], "nt", tag + "_mm_dm")
    g_w_out = matmul(m, dh_out, "tn", tag + "_mm_gwout")
    da, dlg, dlb, dws, dbs = gmlp_gate_bwd(a, dm, p["ln_g"], p["ln_b"], p["w_s"], p["b_s"], tag + "_gate_bwd")
    dhn = matmul(da, wf["w_in"], "nt", tag + "_mm_dhn")
    g_w_in = matmul(hn, da, "tn", tag + "_mm_gwin")
    dh, dng = rms_bwd(h, p["norm_g"], dhn, dh_out, tag + "_rms_bwd")
    grads = {"norm_g": dng.reshape(-1), "w_in": g_w_in, "ln_g": dlg.reshape(-1), "ln_b": dlb.reshape(-1),
             "w_s": dws, "b_s": dbs.reshape(GMLP_GROUPS, GMLP_CHUNK), "w_out": g_w_out}
    return dh, grads


def _cmul(ar, ai, br, bi):
    return ar * br - ai * bi, ar * bi + ai * br


S5_PG = 16


def _gblock(tail):
    return pl.BlockSpec((S5_PG,) + tuple(tail), lambda i: (i, 0, 0))


def s5_params_fwd(a_re, a_im, log_step, b_re, b_im):
    g, p, hh = b_re.shape

    def body(ar_ref, ai_ref, ls_ref, br_ref, bi_ref, lr_ref, li_ref, bbr_ref, bbi_ref):
        ar, ai = ar_ref[...], ai_ref[...]
        step = jnp.exp(ls_ref[...])
        mag = jnp.exp(ar * step)
        lr, li = mag * jnp.cos(ai * step), mag * jnp.sin(ai * step)
        den = 1.0 / (ar * ar + ai * ai)
        fr, fi = _cmul(lr - 1.0, li, ar * den, -ai * den)
        lr_ref[...] = lr
        li_ref[...] = li
        bbr, bbi = _cmul(fr, fi, br_ref[...], bi_ref[...])
        bbr_ref[...] = bbr
        bbi_ref[...] = bbi

    s1 = jax.ShapeDtypeStruct((g, p, 1), F32)
    s3 = jax.ShapeDtypeStruct((g, p, hh), F32)
    b1, b0, b3 = _gblock((p, 1)), _gblock((1, 1)), _gblock((p, hh))
    return pl.pallas_call(body, name="s5_params_fwd", grid=(g // S5_PG,), in_specs=[b1, b1, b0, b3, b3],
                          out_specs=[b1, b1, b3, b3], out_shape=[s1, s1, s3, s3],
                          compiler_params=_cparams(("parallel",)))(
        a_re.reshape(g, p, 1), a_im.reshape(g, p, 1), log_step.reshape(g, 1, 1), b_re, b_im)


def s5_params_bwd(a_re, a_im, log_step, b_re, b_im, dl_re, dl_im, dbb_re, dbb_im):
    g, p, hh = b_re.shape

    def body(ar_ref, ai_ref, ls_ref, br_ref, bi_ref, dlr_ref, dli_ref, dbr_ref, dbi_ref,
             gar_ref, gai_ref, gls_ref, gbr_ref, gbi_ref):
        ar, ai = ar_ref[...], ai_ref[...]
        step = jnp.exp(ls_ref[...])
        mag = jnp.exp(ar * step)
        lr, li = mag * jnp.cos(ai * step), mag * jnp.sin(ai * step)
        den = 1.0 / (ar * ar + ai * ai)
        ir, ii = ar * den, -ai * den
        fr, fi = _cmul(lr - 1.0, li, ir, ii)
        br, bi = br_ref[...], bi_ref[...]
        dbr, dbi = dbr_ref[...], dbi_ref[...]
        gbr, gbi = _cmul(fr, -fi, dbr, dbi)
        gbr_ref[...] = gbr
        gbi_ref[...] = gbi
        pr, pi = _cmul(br, -bi, dbr, dbi)
        gfr = jnp.sum(pr, axis=-1, keepdims=True)
        gfi = jnp.sum(pi, axis=-1, keepdims=True)
        t_r, t_i = _cmul(ir, -ii, gfr, gfi)
        glr, gli = dlr_ref[...] + t_r, dli_ref[...] + t_i
        c1r, c1i = _cmul(step * lr, -step * li, glr, gli)
        qr, qi = _cmul(fr, fi, ir, ii)
        c2r, c2i = _cmul(-qr, qi, gfr, gfi)
        gar_ref[...] = c1r + c2r
        gai_ref[...] = c1i + c2i
        wr, wi = _cmul(ar, ai, lr, li)
        sr, _ = _cmul(wr, -wi, glr, gli)
        gls_ref[...] = jnp.sum(sr, axis=1, keepdims=True) * step

    s1 = jax.ShapeDtypeStruct((g, p, 1), F32)
    s3 = jax.ShapeDtypeStruct((g, p, hh), F32)
    b1, b0, b3 = _gblock((p, 1)), _gblock((1, 1)), _gblock((p, hh))
    return pl.pallas_call(body, name="s5_params_bwd", grid=(g // S5_PG,),
                          in_specs=[b1, b1, b0, b3, b3, b1, b1, b3, b3], out_specs=[b1, b1, b0, b3, b3],
                          out_shape=[s1, s1, jax.ShapeDtypeStruct((g, 1, 1), F32), s3, s3],
                          compiler_params=_cparams(("parallel",)))(
        a_re.reshape(g, p, 1), a_im.reshape(g, p, 1), log_step.reshape(g, 1, 1), b_re, b_im,
        dl_re, dl_im, dbb_re, dbb_im)


def _blockdiag(t):
    sb, n, r, c = t.shape
    eye = jnp.eye(n, dtype=bool)[None, :, None, :, None]
    full = jnp.where(eye, t[:, :, :, None, :], jnp.zeros((), t.dtype))
    return full.reshape(sb, n * r, n * c)


def _blockdiag_extract(m, r, c):
    sb = m.shape[0]
    n = m.shape[1] // r
    m5 = m.reshape(sb, n, r, n, c)
    return jnp.stack([m5[:, i, :, i, :] for i in range(n)], axis=1)


S5_TB = 64


def s5_scan_fwd(a_p, lam_re, lam_im, wb_re, wb_im, wc_re, wc_im, d_skip, x0_re, x0_im, name):
    l = a_p.shape[0]
    di = d_skip.shape[1]
    rows = S5_SEG * S5_TB
    nb = l // rows
    ns = wb_re.shape[2]

    def body(u_ref, lr_ref, li_ref, wbr_ref, wbi_ref, wcr_ref, wci_ref, ds_ref, x0r_ref, x0i_ref,
             y_ref, ckr_ref, cki_ref, xer_ref, xei_ref, bur, bui, xr_s, xi_s):
        b = pl.program_id(1)

        @pl.when(b == 0)
        def _():
            xr_s[...] = x0r_ref[0]
            xi_s[...] = x0i_ref[0]

        ckr_ref[0, 0] = xr_s[...]
        cki_ref[0, 0] = xi_s[...]
        u = u_ref[...]
        bur[...] = _dot(u, wbr_ref[0], DN_NN)
        bui[...] = _dot(u, wbi_ref[0], DN_NN)
        lr = jnp.broadcast_to(lr_ref[0], (S5_SEG, ns))
        li = jnp.broadcast_to(li_ref[0], (S5_SEG, ns))

        def step(t, carry):
            xr, xi = carry
            sl = pl.ds(pl.multiple_of(t * S5_SEG, S5_SEG), S5_SEG)
            nr = lr * xr - li * xi + bur[sl, :]
            ni = lr * xi + li * xr + bui[sl, :]
            bur[sl, :] = nr
            bui[sl, :] = ni
            return nr, ni

        xr, xi = lax.fori_loop(0, S5_TB, step, (xr_s[...], xi_s[...]))
        xr_s[...] = xr
        xi_s[...] = xi
        xer_ref[0] = xr
        xei_ref[0] = xi
        y_ref[...] = _dot(bur[...], wcr_ref[0], DN_NN) - _dot(bui[...], wci_ref[0], DN_NN) + ds_ref[...] * u

    sb3 = lambda s, b: (s, 0, 0)
    st = jax.ShapeDtypeStruct
    return pl.pallas_call(
        body, name=name, grid=(S5_SB, nb),
        in_specs=[pl.BlockSpec((rows, LANES), lambda s, b: (b, s)),
                  pl.BlockSpec((1, 1, ns), sb3), pl.BlockSpec((1, 1, ns), sb3),
                  pl.BlockSpec((1, LANES, ns), sb3), pl.BlockSpec((1, LANES, ns), sb3),
                  pl.BlockSpec((1, ns, LANES), sb3), pl.BlockSpec((1, ns, LANES), sb3),
                  pl.BlockSpec((1, LANES), lambda s, b: (0, s)),
                  pl.BlockSpec((1, S5_SEG, ns), sb3), pl.BlockSpec((1, S5_SEG, ns), sb3)],
        out_specs=[pl.BlockSpec((rows, LANES), lambda s, b: (b, s)),
                   pl.BlockSpec((1, 1, S5_SEG, ns), lambda s, b: (s, b, 0, 0)),
                   pl.BlockSpec((1, 1, S5_SEG, ns), lambda s, b: (s, b, 0, 0)),
                   pl.BlockSpec((1, S5_SEG, ns), sb3), pl.BlockSpec((1, S5_SEG, ns), sb3)],
        out_shape=[st((l, di), F32), st((S5_SB, nb, S5_SEG, ns), F32), st((S5_SB, nb, S5_SEG, ns), F32),
                   st((S5_SB, S5_SEG, ns), F32), st((S5_SB, S5_SEG, ns), F32)],
        scratch_shapes=[pltpu.VMEM((rows, ns), F32), pltpu.VMEM((rows, ns), F32),
                        pltpu.VMEM((S5_SEG, ns), F32), pltpu.VMEM((S5_SEG, ns), F32)],
        compiler_params=_cparams(("parallel", "arbitrary")))(
        a_p, lam_re, lam_im, wb_re, wb_im, wc_re, wc_im, d_skip, x0_re, x0_im)


def s5_ends(inp, lam_re, lam_im, w_re, w_im, adjoint, name):
    l = inp.shape[0]
    rows = S5_SEG * S5_TB
    nb = l // rows
    ns = lam_re.shape[2]

    def body(i_ref, lr_ref, li_ref, wr_ref, wi_ref, er_ref, ei_ref, pr_b, pi_b, xr_s, xi_s):
        b = pl.program_id(1)

        @pl.when(b == 0)
        def _():
            xr_s[...] = jnp.zeros_like(xr_s)
            xi_s[...] = jnp.zeros_like(xi_s)

        v = i_ref[...]
        lr = jnp.broadcast_to(lr_ref[0], (S5_SEG, ns))
        li = jnp.broadcast_to(li_ref[0], (S5_SEG, ns))
        if adjoint:
            pr_b[...] = _dot(v, wr_ref[0], DN_NT)
            pi_b[...] = -_dot(v, wi_ref[0], DN_NT)
            li = -li
        else:
            pr_b[...] = _dot(v, wr_ref[0], DN_NN)
            pi_b[...] = _dot(v, wi_ref[0], DN_NN)

        def step(k, carry):
            xr, xi = carry
            t = S5_TB - 1 - k if adjoint else k
            sl = pl.ds(pl.multiple_of(t * S5_SEG, S5_SEG), S5_SEG)
            return lr * xr - li * xi + pr_b[sl, :], lr * xi + li * xr + pi_b[sl, :]

        xr, xi = lax.fori_loop(0, S5_TB, step, (xr_s[...], xi_s[...]))
        xr_s[...] = xr
        xi_s[...] = xi
        er_ref[0] = xr
        ei_ref[0] = xi

    sb3 = lambda s, b: (s, 0, 0)
    blk = (lambda s, b: (nb - 1 - b, s)) if adjoint else (lambda s, b: (b, s))
    wshape = (1, ns, LANES) if adjoint else (1, LANES, ns)
    st = jax.ShapeDtypeStruct((S5_SB, S5_SEG, ns), F32)
    return pl.pallas_call(
        body, name=name, grid=(S5_SB, nb),
        in_specs=[pl.BlockSpec((rows, LANES), blk), pl.BlockSpec((1, 1, ns), sb3), pl.BlockSpec((1, 1, ns), sb3),
                  pl.BlockSpec(wshape, sb3), pl.BlockSpec(wshape, sb3)],
        out_specs=[pl.BlockSpec((1, S5_SEG, ns), sb3), pl.BlockSpec((1, S5_SEG, ns), sb3)],
        out_shape=[st, st],
        scratch_shapes=[pltpu.VMEM((rows, ns), F32), pltpu.VMEM((rows, ns), F32),
                        pltpu.VMEM((S5_SEG, ns), F32), pltpu.VMEM((S5_SEG, ns), F32)],
        compiler_params=_cparams(("parallel", "arbitrary")))(inp, lam_re, lam_im, w_re, w_im)


def s5_scan_bwd(a_p, dy, lam_re, lam_im, wb_re, wb_im, wc_re, wc_im, d_skip, ck_re, ck_im, a0_re, a0_im, name):
    l = a_p.shape[0]
    di = d_skip.shape[1]
    rows = S5_SEG * S5_TB
    nb = l // rows
    ns = wb_re.shape[2]

    def body(u_ref, dy_ref, lr_ref, li_ref, wbr_ref, wbi_ref, wcr_ref, wci_ref, ds_ref, ckr_ref, cki_ref,
             a0r_ref, a0i_ref,
             du_ref, dwbr_ref, dwbi_ref, dwcr_ref, dwci_ref, dds_ref, dlr_ref, dli_ref, aer_ref, aei_ref,
             xr_b, xi_b, gr_b, gi_b, ar_s, ai_s):
        b = pl.program_id(1)

        @pl.when(b == 0)
        def _():
            ar_s[...] = a0r_ref[0]
            ai_s[...] = a0i_ref[0]

        u = u_ref[...]
        dyv = dy_ref[...]
        lr = jnp.broadcast_to(lr_ref[0], (S5_SEG, ns))
        li = jnp.broadcast_to(li_ref[0], (S5_SEG, ns))
        xr_b[...] = _dot(u, wbr_ref[0], DN_NN)
        xi_b[...] = _dot(u, wbi_ref[0], DN_NN)

        def fstep(t, carry):
            xr, xi = carry
            sl = pl.ds(pl.multiple_of(t * S5_SEG, S5_SEG), S5_SEG)
            nr = lr * xr - li * xi + xr_b[sl, :]
            ni = lr * xi + li * xr + xi_b[sl, :]
            xr_b[sl, :] = nr
            xi_b[sl, :] = ni
            return nr, ni

        x0r, x0i = ckr_ref[0, 0], cki_ref[0, 0]
        lax.fori_loop(0, S5_TB, fstep, (x0r, x0i))
        dwcr = _dot(xr_b[...], dyv, DN_TN)
        dwci = -_dot(xi_b[...], dyv, DN_TN)
        gr_b[...] = _dot(dyv, wcr_ref[0], DN_NT)
        gi_b[...] = -_dot(dyv, wci_ref[0], DN_NT)

        def bstep(k, carry):
            ar, ai, dlr, dli = carry
            t = S5_TB - 1 - k
            sl = pl.ds(pl.multiple_of(t * S5_SEG, S5_SEG), S5_SEG)
            slp = pl.ds(pl.multiple_of(jnp.maximum(t - 1, 0) * S5_SEG, S5_SEG), S5_SEG)
            nr = gr_b[sl, :] + lr * ar + li * ai
            ni = gi_b[sl, :] + lr * ai - li * ar
            gr_b[sl, :] = nr
            gi_b[sl, :] = ni
            first = t == 0
            pr = jnp.where(first, x0r, xr_b[slp, :])
            pi = jnp.where(first, x0i, xi_b[slp, :])
            dlr = dlr + nr * pr + ni * pi
            dli = dli + ni * pr - nr * pi
            return nr, ni, dlr, dli

        zero = jnp.zeros((S5_SEG, ns), F32)
        ar, ai, dlr, dli = lax.fori_loop(0, S5_TB, bstep, (ar_s[...], ai_s[...], zero, zero))
        ar_s[...] = ar
        ai_s[...] = ai
        aer_ref[0] = ar
        aei_ref[0] = ai
        dsk = ds_ref[...]
        du_ref[...] = (_dot(gr_b[...], wbr_ref[0], DN_NT) + _dot(gi_b[...], wbi_ref[0], DN_NT) + dsk * dyv).astype(BF16)
        dwbr = _dot(u, gr_b[...], DN_TN)
        dwbi = _dot(u, gi_b[...], DN_TN)
        dds = jnp.sum(dyv * u, axis=0, keepdims=True)

        @pl.when(b == 0)
        def _():
            dwbr_ref[0] = dwbr
            dwbi_ref[0] = dwbi
            dwcr_ref[0] = dwcr
            dwci_ref[0] = dwci
            dds_ref[...] = dds
            dlr_ref[0] = dlr
            dli_ref[0] = dli

        @pl.when(b != 0)
        def _():
            dwbr_ref[0] += dwbr
            dwbi_ref[0] += dwbi
            dwcr_ref[0] += dwcr
            dwci_ref[0] += dwci
            dds_ref[...] += dds
            dlr_ref[0] += dlr
            dli_ref[0] += dli

    sb3 = lambda s, b: (s, 0, 0)
    rev = lambda s, b: (nb - 1 - b, s)
    st = jax.ShapeDtypeStruct
    return pl.pallas_call(
        body, name=name, grid=(S5_SB, nb),
        in_specs=[pl.BlockSpec((rows, LANES), rev), pl.BlockSpec((rows, LANES), rev),
                  pl.BlockSpec((1, 1, ns), sb3), pl.BlockSpec((1, 1, ns), sb3),
                  pl.BlockSpec((1, LANES, ns), sb3), pl.BlockSpec((1, LANES, ns), sb3),
                  pl.BlockSpec((1, ns, LANES), sb3), pl.BlockSpec((1, ns, LANES), sb3),
                  pl.BlockSpec((1, LANES), lambda s, b: (0, s)),
                  pl.BlockSpec((1, 1, S5_SEG, ns), lambda s, b: (s, nb - 1 - b, 0, 0)),
                  pl.BlockSpec((1, 1, S5_SEG, ns), lambda s, b: (s, nb - 1 - b, 0, 0)),
                  pl.BlockSpec((1, S5_SEG, ns), sb3), pl.BlockSpec((1, S5_SEG, ns), sb3)],
        out_specs=[pl.BlockSpec((rows, LANES), rev),
                   pl.BlockSpec((1, LANES, ns), sb3), pl.BlockSpec((1, LANES, ns), sb3),
                   pl.BlockSpec((1, ns, LANES), sb3), pl.BlockSpec((1, ns, LANES), sb3),
                   pl.BlockSpec((1, LANES), lambda s, b: (0, s)),
                   pl.BlockSpec((1, S5_SEG, ns), sb3), pl.BlockSpec((1, S5_SEG, ns), sb3),
                   pl.BlockSpec((1, S5_SEG, ns), sb3), pl.BlockSpec((1, S5_SEG, ns), sb3)],
        out_shape=[st((l, di), BF16), st((S5_SB, LANES, ns), F32), st((S5_SB, LANES, ns), F32),
                   st((S5_SB, ns, LANES), F32), st((S5_SB, ns, LANES), F32), st((1, di), F32),
                   st((S5_SB, S5_SEG, ns), F32), st((S5_SB, S5_SEG, ns), F32),
                   st((S5_SB, S5_SEG, ns), F32), st((S5_SB, S5_SEG, ns), F32)],
        scratch_shapes=[pltpu.VMEM((rows, ns), F32), pltpu.VMEM((rows, ns), F32),
                        pltpu.VMEM((rows, ns), F32), pltpu.VMEM((rows, ns), F32),
                        pltpu.VMEM((S5_SEG, ns), F32), pltpu.VMEM((S5_SEG, ns), F32)],
        compiler_params=_cparams(("parallel", "arbitrary")))(
        a_p, dy, lam_re, lam_im, wb_re, wb_im, wc_re, wc_im, d_skip, ck_re, ck_im, a0_re, a0_im)


def s5_carry(e_re, e_im, lam_re, lam_im, seg_len, reverse, name):
    sb, seg, ns = e_re.shape

    def body(er_ref, ei_ref, lr_ref, li_ref, cr_ref, ci_ref):
        pr, pi = lr_ref[...], li_ref[...]
        if reverse:
            pi = -pi
        for _ in range(int(math.log2(seg_len))):
            pr, pi = _cmul(pr, pi, pr, pi)
        er, ei = er_ref[...], ei_ref[...]
        row = lax.broadcasted_iota(jnp.int32, (sb, seg, ns), 1)
        cr = jnp.zeros((sb, seg, ns), F32)
        ci = jnp.zeros((sb, seg, ns), F32)
        cur_r = jnp.zeros((sb, 1, ns), F32)
        cur_i = jnp.zeros((sb, 1, ns), F32)
        order = range(seg - 2, -1, -1) if reverse else range(1, seg)
        for s in order:
            src = s + 1 if reverse else s - 1
            mr, mi = _cmul(pr, pi, cur_r, cur_i)
            cur_r = jnp.sum(jnp.where(row == src, er, 0.0), axis=1, keepdims=True) + mr
            cur_i = jnp.sum(jnp.where(row == src, ei, 0.0), axis=1, keepdims=True) + mi
            cr = jnp.where(row == s, cur_r, cr)
            ci = jnp.where(row == s, cur_i, ci)
        cr_ref[...] = cr
        ci_ref[...] = ci

    st = jax.ShapeDtypeStruct((sb, seg, ns), F32)
    return pl.pallas_call(body, name=name, out_shape=[st, st],
                          compiler_params=pltpu.CompilerParams(vmem_limit_bytes=VMEM_LIMIT_BYTES))(e_re, e_im, lam_re, lam_im)


def s5_act(y, name):
    l, d = y.shape
    tl = ROW_TILE

    def body(y_ref, o_ref):
        o_ref[...] = _gelu(y_ref[...]).astype(BF16)

    return _rowcall(body, name, l // tl, [_rows(tl, d)], _rows(tl, d), jax.ShapeDtypeStruct((l, d), BF16))(y)


def s5_gate_fwd(y, t, b_glu, a_p, name):
    l, d = y.shape
    tl = ROW_TILE

    def body(y_ref, t_ref, b_ref, z_ref, m_ref):
        yg = _gelu(y_ref[...])
        m_ref[...] = (yg * _sigmoid(t_ref[...] + b_ref[...]) * _silu(z_ref[...])).astype(BF16)

    return _rowcall(body, name, l // tl, [_rows(tl, d), _rows(tl, d), _full((1, d)), _rows(tl, d, 1)], _rows(tl, d),
                    jax.ShapeDtypeStruct((l, d), BF16))(y, t, b_glu.reshape(1, d), a_p)


def s5_gate_bwd(dm, y, t, b_glu, a_p, name):
    l, d = y.shape
    tl = ROW_TILE

    def body(dm_ref, y_ref, t_ref, b_ref, z_ref, dt_ref, dyg_ref, dz_ref, db_ref):
        i = pl.program_id(0)
        dmv = dm_ref[...]
        z = z_ref[...]
        yg = _gelu(y_ref[...])
        sg = _sigmoid(t_ref[...] + b_ref[...])
        y2 = yg * sg
        dy2 = dmv * _silu(z)
        dz_ref[...] = (dmv * y2 * _silu_grad(z)).astype(BF16)
        dyg_ref[...] = dy2 * sg
        dt = dy2 * yg * sg * (1.0 - sg)
        dt_ref[...] = dt.astype(BF16)
        _acc(db_ref, jnp.sum(dt, axis=0, keepdims=True), i)

    st = jax.ShapeDtypeStruct
    return _rowcall(body, name, l // tl, [_rows(tl, d), _rows(tl, d), _rows(tl, d), _full((1, d)), _rows(tl, d, 1)],
                    [_rows(tl, d), _rows(tl, d), _rows(tl, d), _full((1, d))],
                    [st((l, d), BF16), st((l, d), F32), st((l, d), BF16), st((1, d), F32)])(
        dm, y, t, b_glu.reshape(1, d), a_p)


def s5_act_bwd(y, dyg_a, dyg_b, name):
    l, d = y.shape
    tl = ROW_TILE

    def body(y_ref, a_ref, b_ref, o_ref):
        o_ref[...] = (a_ref[...] + b_ref[...]) * _gelu_grad(y_ref[...])

    return _rowcall(body, name, l // tl, [_rows(tl, d)] * 3, _rows(tl, d), jax.ShapeDtypeStruct((l, d), F32))(y, dyg_a, dyg_b)


def _seg_perm(t):
    l, d = t.shape
    return t.reshape(S5_SEG, l // S5_SEG, d).transpose(1, 0, 2).reshape(l, d)


def _seg_unperm(t):
    l, d = t.shape
    return t.reshape(l // S5_SEG, S5_SEG, d).transpose(1, 0, 2).reshape(l, d)


def _s5_weights(p):
    lr, li, bbr, bbi = s5_params_fwd(p["a_re"], p["a_im"], p["log_step"], p["b_re"], p["b_im"])
    ns = 8 * S5_STATE
    lam_re = lr.reshape(S5_SB, 1, ns)
    lam_im = li.reshape(S5_SB, 1, ns)
    to_bd = lambda t: _blockdiag(t.reshape(S5_SB, 8, t.shape[1], t.shape[2]))
    wb_re = to_bd(bbr.transpose(0, 2, 1)).astype(BF16)
    wb_im = to_bd(bbi.transpose(0, 2, 1)).astype(BF16)
    wc_re = to_bd(p["c_re"].transpose(0, 2, 1)).astype(BF16)
    wc_im = to_bd(p["c_im"].transpose(0, 2, 1)).astype(BF16)
    return lam_re, lam_im, wb_re, wb_im, wc_re, wc_im


def s5_layer_fwd(h, p, wf, tag):
    l = h.shape[0]
    di = p["d_skip"].shape[0]
    hn = rms_fwd(h, p["norm_g"], tag + "_rms")
    hn_p = _seg_perm(hn)
    a_p = matmul(hn_p, wf["w_in"], "nn", tag + "_mm_in")
    sw = _s5_weights(p)
    dsk = p["d_skip"].reshape(1, di)
    e_re, e_im = s5_ends(a_p, sw[0], sw[1], sw[2], sw[3], False, tag + "_scan_ends")
    c_re, c_im = s5_carry(e_re, e_im, sw[0], sw[1], l // S5_SEG, False, tag + "_carry")
    y, ck_re, ck_im, _, _ = s5_scan_fwd(a_p, *sw, dsk, c_re, c_im, tag + "_scan")
    yg = s5_act(y, tag + "_act")
    t = matmul(yg, wf["w_glu"], "nn", tag + "_mm_glu")
    m = s5_gate_fwd(y, t, p["b_glu"], a_p, tag + "_gate")
    out_p = matmul(m, wf["w_out"], "nn", tag + "_mm_out")
    h_out = residual_add(h, _seg_unperm(out_p), tag + "_res")
    return h_out, (h, hn_p, a_p, sw, ck_re, ck_im, y, yg, t, m)


def residual_add(h, y, name):
    l, d = h.shape
    tl = ROW_TILE

    def body(h_ref, y_ref, o_ref):
        o_ref[...] = h_ref[...] + y_ref[...]

    return _rowcall(body, name, l // tl, [_rows(tl, d)] * 2, _rows(tl, d), jax.ShapeDtypeStruct((l, d), F32))(h, y)


def s5_layer_bwd(dh_out, saved, p, wf, tag):
    h, hn_p, a_p, sw, ck_re, ck_im, y, yg, t, m = saved
    l = h.shape[0]
    di = p["d_skip"].shape[0]
    dsk = p["d_skip"].reshape(1, di)
    dout_p = _seg_perm(dh_out)
    dm = matmul(dout_p, wf["w_out"], "nt", tag + "_mm_dm")
    g_w_out = matmul(m, dout_p, "tn", tag + "_mm_gwout")
    dt, dyg_a, dz, db_glu = s5_gate_bwd(dm, y, t, p["b_glu"], a_p, tag + "_gate_bwd")
    dyg_b = matmul(dt, wf["w_glu"], "nt", tag + "_mm_dyg")
    g_w_glu = matmul(yg, dt, "tn", tag + "_mm_gwglu")
    dy = s5_act_bwd(y, dyg_a, dyg_b, tag + "_act_bwd")
    e_re, e_im = s5_ends(dy, sw[0], sw[1], sw[4], sw[5], True, tag + "_scanb_ends")
    c_re, c_im = s5_carry(e_re, e_im, sw[0], sw[1], l // S5_SEG, True, tag + "_carry_bwd")
    du, dwbr, dwbi, dwcr, dwci, dds, dlr, dli, _, _ = s5_scan_bwd(
        a_p, dy, *sw, dsk, ck_re, ck_im, c_re, c_im, tag + "_scanb")
    da = jnp.concatenate([du, dz], axis=1)
    dhn_p = matmul(da, wf["w_in"], "nt", tag + "_mm_dhn")
    g_w_in = matmul(hn_p, da, "tn", tag + "_mm_gwin")
    dh, dng = rms_bwd(h, p["norm_g"], _seg_unperm(dhn_p), dh_out, tag + "_rms_bwd")
    ex = lambda m_, r, c: _blockdiag_extract(m_, r, c).reshape(S5_GROUPS, r, c).transpose(0, 2, 1)
    dbb_re, dbb_im = ex(dwbr, S5_GROUP, S5_STATE), ex(dwbi, S5_GROUP, S5_STATE)
    g_c_re, g_c_im = ex(dwcr, S5_STATE, S5_GROUP), ex(dwci, S5_STATE, S5_GROUP)
    dl_re = lane_sum8(dlr).reshape(S5_GROUPS, S5_STATE, 1)
    dl_im = lane_sum8(dli).reshape(S5_GROUPS, S5_STATE, 1)
    gar, gai, gls, gbr, gbi = s5_params_bwd(p["a_re"], p["a_im"], p["log_step"], p["b_re"], p["b_im"],
                                            dl_re, dl_im, dbb_re, dbb_im)
    grads = {"norm_g": dng.reshape(-1), "w_in": g_w_in, "a_re": gar.reshape(S5_GROUPS, S5_STATE),
             "a_im": gai.reshape(S5_GROUPS, S5_STATE), "log_step": gls.reshape(-1), "b_re": gbr, "b_im": gbi,
             "c_re": g_c_re, "c_im": g_c_im, "d_skip": dds.reshape(-1), "w_glu": g_w_glu,
             "b_glu": db_glu.reshape(-1), "w_out": g_w_out}
    return dh, grads


def lane_sum8(t):
    sb, seg, ns = t.shape

    def body(t_ref, o_ref):
        o_ref[...] = jnp.sum(t_ref[...], axis=1, keepdims=True)

    return pl.pallas_call(body, name="s5_seg_sum", out_shape=jax.ShapeDtypeStruct((sb, 1, ns), F32))(t)


MLA_DI = MLA_HEADS * 128
MLA_CQ0 = MLA_DI
MLA_CKV0 = MLA_CQ0 + MLA_Q_RANK
MLA_KR0 = MLA_CKV0 + MLA_KV_RANK
MLA_AW = MLA_KR0 + LANES


def _rot_half(x):
    w = x.shape[-1]
    lane = lax.broadcasted_iota(jnp.int32, x.shape, x.ndim - 1)
    return jnp.where(lane % MLA_ROPE < MLA_ROPE // 2, pltpu.roll(x, w - MLA_ROPE // 2, x.ndim - 1),
                     pltpu.roll(x, MLA_ROPE // 2, x.ndim - 1))


def rope_tables(pos):
    l = pos.shape[0]
    tl = ROW_TILE
    j = np.arange(LANES) % MLA_ROPE % (MLA_ROPE // 2)
    inv_freq = (ROPE_THETA ** (-(2.0 * j) / MLA_ROPE)).astype(np.float32).reshape(1, LANES)
    sign = np.where(np.arange(LANES) % MLA_ROPE < MLA_ROPE // 2, -1.0, 1.0).astype(np.float32).reshape(1, LANES)

    def body(p_ref, f_ref, s_ref, cos_ref, sin_ref):
        ang = p_ref[...].astype(F32) * f_ref[...]
        cos_ref[...] = jnp.cos(ang)
        sin_ref[...] = jnp.sin(ang) * s_ref[...]

    st = jax.ShapeDtypeStruct((l, LANES), F32)
    return _rowcall(body, "rope_tables", l // tl, [_rows(tl, 1), _full((1, LANES)), _full((1, LANES))],
                    [_rows(tl, LANES)] * 2, [st, st])(pos, jnp.asarray(inv_freq), jnp.asarray(sign))


def _rope(x, cos, sins):
    return x * cos + _rot_half(x) * sins


def _rope_t(dy, cos, sins):
    return dy * cos - sins * _rot_half(dy)


def _rmsn(x):
    r = lax.rsqrt(jnp.mean(x * x, axis=-1, keepdims=True) + NORM_EPS)
    return x * r, r


def mla_pre(a, q_g, kv_g, cos, sins, name):
    l = a.shape[0]
    tl = ROW_TILE

    def body(a_ref, qg_ref, kg_ref, cos_ref, sin_ref, cq_ref, ckv_ref, krs_ref):
        xq, _ = _rmsn(a_ref[:, MLA_CQ0:MLA_CKV0])
        cq_ref[...] = (xq * qg_ref[...]).astype(BF16)
        xk, _ = _rmsn(a_ref[:, MLA_CKV0:MLA_KR0])
        ckv_ref[...] = (xk * kg_ref[...]).astype(BF16)
        kr = a_ref[:, MLA_KR0:MLA_AW]
        kr2 = kr + pltpu.roll(kr, MLA_ROPE, 1)
        kr2 = _rope(kr2, cos_ref[...], sin_ref[...])
        lane = lax.broadcasted_iota(jnp.int32, kr2.shape, 1)
        krs_ref[0] = jnp.where(lane < MLA_ROPE, kr2, 0.0).astype(BF16)
        krs_ref[1] = jnp.where(lane >= MLA_ROPE, kr2, 0.0).astype(BF16)

    st = jax.ShapeDtypeStruct
    return _rowcall(body, name, l // tl,
                    [_rows(tl, MLA_AW), _full((1, MLA_Q_RANK)), _full((1, MLA_KV_RANK)), _rows(tl, LANES), _rows(tl, LANES)],
                    [_rows(tl, MLA_Q_RANK), _rows(tl, MLA_KV_RANK), pl.BlockSpec((2, tl, LANES), lambda i: (0, i, 0))],
                    [st((l, MLA_Q_RANK), BF16), st((l, MLA_KV_RANK), BF16), st((2, l, LANES), BF16)])(
        a, q_g.reshape(1, -1), kv_g.reshape(1, -1), cos, sins)


def mla_rope_q(qr, cos, sins, name):
    l, w = qr.shape
    tl = ROW_TILE

    def body(q_ref, cos_ref, sin_ref, o_ref):
        c, s = cos_ref[...], sin_ref[...]
        for p in range(w // LANES):
            sl = slice(p * LANES, (p + 1) * LANES)
            o_ref[:, sl] = _rope(q_ref[:, sl], c, s).astype(BF16)

    return _rowcall(body, name, l // tl, [_rows(tl, w), _rows(tl, LANES), _rows(tl, LANES)], _rows(tl, w),
                    jax.ShapeDtypeStruct((l, w), BF16))(qr, cos, sins)


ATT_T = 256


def _scores(qn, qr, kn, kr, diagonal, transposed):
    if transposed:
        s = (_dot(kn, qn, DN_NT) + _dot(kr, qr, DN_NT)) * MLA_SCALE
    else:
        s = (_dot(qn, kn, DN_NT) + _dot(qr, kr, DN_NT)) * MLA_SCALE
    if not diagonal:
        return s
    r = lax.broadcasted_iota(jnp.int32, s.shape, 0)
    c = lax.broadcasted_iota(jnp.int32, s.shape, 1)
    return jnp.where((r <= c) if transposed else (c <= r), s, NEG_INF)


def _fold(x):
    return x[:, :LANES], x[:, LANES:]


def flash_fwd(qn, qr, kv, krs, name):
    l = qn.shape[0]
    nq = l // ATT_T

    def body(qn_ref, qr_ref, kv_ref, kr_ref, o_ref, lse_ref, s_buf):
        qi = pl.program_id(1)
        q_r = qr_ref[...]
        q_n = [qn_ref[:, hh * LANES:(hh + 1) * LANES] for hh in range(2)]

        def block_scores(j, mx, diagonal):
            sl = pl.ds(pl.multiple_of(j * ATT_T, ATT_T), ATT_T)
            out = []
            for hh in range(2):
                s = _scores(q_n[hh], q_r, kv_ref[sl, 2 * hh * LANES:(2 * hh + 1) * LANES], kr_ref[hh, sl, :],
                            diagonal, False)
                s_buf[hh, j] = s
                lo, hi = _fold(s)
                out.append(jnp.maximum(mx[hh], jnp.maximum(lo, hi)))
            return tuple(out)

        ninf = jnp.full((ATT_T, LANES), NEG_INF, F32)
        mx = lax.fori_loop(0, qi, lambda j, c: block_scores(j, c, False), (ninf, ninf))
        mx = block_scores(qi, mx, True)
        m = [jnp.max(mx[hh], axis=-1, keepdims=True) for hh in range(2)]

        def block_pv(j, carry):
            sl = pl.ds(pl.multiple_of(j * ATT_T, ATT_T), ATT_T)
            out = []
            for hh in range(2):
                ls, acc = carry[hh]
                p = jnp.exp(s_buf[hh, j] - m[hh])
                lo, hi = _fold(p)
                out.append((ls + (lo + hi),
                            acc + _dot(p, kv_ref[sl, (2 * hh + 1) * LANES:(2 * hh + 2) * LANES], DN_NN)))
            return tuple(out)

        z = jnp.zeros((ATT_T, LANES), F32)
        res = lax.fori_loop(0, qi + 1, block_pv, ((z, z), (z, z)))
        for hh in range(2):
            lsum = jnp.sum(res[hh][0], axis=-1, keepdims=True)
            o_ref[:, hh * LANES:(hh + 1) * LANES] = res[hh][1] / lsum
            lse_ref[hh] = m[hh] + jnp.log(lsum)

    st = jax.ShapeDtypeStruct
    return pl.pallas_call(
        body, name=name, grid=(MLA_HEADS // 2, nq),
        in_specs=[pl.BlockSpec((ATT_T, 2 * LANES), lambda p, i: (i, p)),
                  pl.BlockSpec((ATT_T, LANES), lambda p, i: (i, p)),
                  pl.BlockSpec((l, 4 * LANES), lambda p, i: (0, p)),
                  pl.BlockSpec((2, l, LANES), lambda p, i: (0, 0, 0))],
        out_specs=[pl.BlockSpec((ATT_T, 2 * LANES), lambda p, i: (i, p)),
                   pl.BlockSpec((2, ATT_T, 1), lambda p, i: (p, i, 0))],
        out_shape=[st((l, MLA_DI), F32), st((MLA_HEADS, l, 1), F32)],
        scratch_shapes=[pltpu.VMEM((2, nq, ATT_T, ATT_T), F32)],
        compiler_params=_cparams(("parallel", "arbitrary")))(qn, qr, kv, krs)


def flash_dkv(qn, qr, kv, krs, do, lse_row, delta_row, name):
    l = qn.shape[0]
    nk = l // ATT_T

    def body(qn_ref, qr_ref, do_ref, lse_ref, dl_ref, kv_ref, kr_ref, dkv_ref, dkr_ref):
        kj = pl.program_id(1)
        lane = lax.broadcasted_iota(jnp.int32, (ATT_T, LANES), 1)
        kn = [kv_ref[:, 2 * hh * LANES:(2 * hh + 1) * LANES] for hh in range(2)]
        v = [kv_ref[:, (2 * hh + 1) * LANES:(2 * hh + 2) * LANES] for hh in range(2)]

        def block(i, carry, diagonal):
            sl = pl.ds(pl.multiple_of(i * ATT_T, ATT_T), ATT_T)
            q_r = qr_ref[sl, :]
            out = []
            for hh in range(2):
                dkn, dv, dkr = carry[hh]
                hs = slice(hh * LANES, (hh + 1) * LANES)
                q_n, d_o = qn_ref[sl, hs], do_ref[sl, hs]
                s = _scores(q_n, q_r, kn[hh], kr_ref[hh], diagonal, True)
                pt = jnp.exp(s - lse_ref[hh, i])
                dv = dv + _dot(pt, d_o, DN_NN)
                dpt = _dot(v[hh], d_o, DN_NT)
                dst = (pt * (dpt - dl_ref[hh, i]) * MLA_SCALE).astype(BF16)
                out.append((dkn + _dot(dst, q_n, DN_NN), dv, dkr + _dot(dst, q_r, DN_NN)))
            return tuple(out)

        z = jnp.zeros((ATT_T, LANES), F32)
        res = block(kj, ((z, z, z), (z, z, z)), True)
        res = lax.fori_loop(kj + 1, nk, lambda i, c: block(i, c, False), res)
        for hh in range(2):
            dkv_ref[:, 2 * hh * LANES:(2 * hh + 1) * LANES] = res[hh][0].astype(BF16)
            dkv_ref[:, (2 * hh + 1) * LANES:(2 * hh + 2) * LANES] = res[hh][1].astype(BF16)
        dkr_ref[0] = jnp.where(lane < MLA_ROPE, res[0][2], res[1][2])

    st = jax.ShapeDtypeStruct
    return pl.pallas_call(
        body, name=name, grid=(MLA_HEADS // 2, nk),
        in_specs=[pl.BlockSpec((l, 2 * LANES), lambda p, j: (0, p)),
                  pl.BlockSpec((l, LANES), lambda p, j: (0, p)),
                  pl.BlockSpec((l, 2 * LANES), lambda p, j: (0, p)),
                  pl.BlockSpec((2, nk, 1, ATT_T), lambda p, j: (p, 0, 0, 0)),
                  pl.BlockSpec((2, nk, 1, ATT_T), lambda p, j: (p, 0, 0, 0)),
                  pl.BlockSpec((ATT_T, 4 * LANES), lambda p, j: (j, p)),
                  pl.BlockSpec((2, ATT_T, LANES), lambda p, j: (0, j, 0))],
        out_specs=[pl.BlockSpec((ATT_T, 4 * LANES), lambda p, j: (j, p)),
                   pl.BlockSpec((1, ATT_T, LANES), lambda p, j: (p, j, 0))],
        out_shape=[st((l, 2 * MLA_DI), BF16), st((MLA_HEADS // 2, l, LANES), F32)],
        compiler_params=_cparams(("parallel", "arbitrary")))(qn, qr, do, lse_row, delta_row, kv, krs)


def flash_dq(qn, qr, kv, krs, do, lse, delta, cos, sins, name):
    l = qn.shape[0]
    nq = l // ATT_T

    def body(qn_ref, qr_ref, do_ref, lse_ref, dl_ref, kv_ref, kr_ref, cos_ref, sin_ref, dqn_ref, dqr_ref):
        qi = pl.program_id(1)
        q_r = qr_ref[...]
        q_n = [qn_ref[:, hh * LANES:(hh + 1) * LANES] for hh in range(2)]
        d_o = [do_ref[:, hh * LANES:(hh + 1) * LANES] for hh in range(2)]
        lse_h = [lse_ref[hh] for hh in range(2)]
        dl_h = [dl_ref[hh] for hh in range(2)]

        def block(j, carry, diagonal):
            sl = pl.ds(pl.multiple_of(j * ATT_T, ATT_T), ATT_T)
            dqn0, dqn1, dqr = carry
            dqn = [dqn0, dqn1]
            for hh in range(2):
                kn = kv_ref[sl, 2 * hh * LANES:(2 * hh + 1) * LANES]
                v = kv_ref[sl, (2 * hh + 1) * LANES:(2 * hh + 2) * LANES]
                kr = kr_ref[hh, sl, :]
                s = _scores(q_n[hh], q_r, kn, kr, diagonal, False)
                pr = jnp.exp(s - lse_h[hh])
                dp = _dot(d_o[hh], v, DN_NT)
                ds = (pr * (dp - dl_h[hh]) * MLA_SCALE).astype(BF16)
                dqn[hh] = dqn[hh] + _dot(ds, kn, DN_NN)
                dqr = dqr + _dot(ds, kr, DN_NN)
            return dqn[0], dqn[1], dqr

        z = jnp.zeros((ATT_T, LANES), F32)
        res = lax.fori_loop(0, qi, lambda j, c: block(j, c, False), (z, z, z))
        dqn0, dqn1, dqr = block(qi, res, True)
        dqn_ref[:, 0:LANES] = dqn0.astype(BF16)
        dqn_ref[:, LANES:2 * LANES] = dqn1.astype(BF16)
        dqr_ref[...] = _rope_t(dqr, cos_ref[...], sin_ref[...]).astype(BF16)

    st = jax.ShapeDtypeStruct
    return pl.pallas_call(
        body, name=name, grid=(MLA_HEADS // 2, nq),
        in_specs=[pl.BlockSpec((ATT_T, 2 * LANES), lambda p, i: (i, p)),
                  pl.BlockSpec((ATT_T, LANES), lambda p, i: (i, p)),
                  pl.BlockSpec((ATT_T, 2 * LANES), lambda p, i: (i, p)),
                  pl.BlockSpec((2, ATT_T, 1), lambda p, i: (p, i, 0)),
                  pl.BlockSpec((2, ATT_T, 1), lambda p, i: (p, i, 0)),
                  pl.BlockSpec((l, 4 * LANES), lambda p, i: (0, p)),
                  pl.BlockSpec((2, l, LANES), lambda p, i: (0, 0, 0)),
                  pl.BlockSpec((ATT_T, LANES), lambda p, i: (i, 0)),
                  pl.BlockSpec((ATT_T, LANES), lambda p, i: (i, 0))],
        out_specs=[pl.BlockSpec((ATT_T, 2 * LANES), lambda p, i: (i, p)),
                   pl.BlockSpec((ATT_T, LANES), lambda p, i: (i, p))],
        out_shape=[st((l, MLA_DI), BF16), st((l, MLA_HEADS * MLA_ROPE), BF16)],
        compiler_params=_cparams(("parallel", "arbitrary")))(qn, qr, do, lse, delta, kv, krs, cos, sins)


def mla_gate_fwd(o, a, name):
    l = o.shape[0]
    tl = ROW_TILE

    def body(o_ref, z_ref, m_ref):
        m_ref[...] = (o_ref[...] * _silu(z_ref[...])).astype(BF16)

    return _rowcall(body, name, l // tl, [_rows(tl, MLA_DI), _rows(tl, MLA_DI)], _rows(tl, MLA_DI),
                    jax.ShapeDtypeStruct((l, MLA_DI), BF16))(o, a)


def mla_gate_bwd(dm, o, a, name):
    l = o.shape[0]
    tl = ROW_TILE

    def body(dm_ref, o_ref, z_ref, do_ref, dz_ref, dl_ref):
        dmv, ov, z = dm_ref[...], o_ref[...], z_ref[...]
        d_o = dmv * _silu(z)
        do_ref[...] = d_o.astype(BF16)
        dz_ref[...] = (dmv * ov * _silu_grad(z)).astype(BF16)
        pr = d_o * ov
        for h in range(MLA_HEADS):
            dl_ref[h] = jnp.sum(pr[:, h * LANES:(h + 1) * LANES], axis=1, keepdims=True)

    st = jax.ShapeDtypeStruct
    return _rowcall(body, name, l // tl, [_rows(tl, MLA_DI)] * 3,
                    [_rows(tl, MLA_DI), _rows(tl, MLA_DI), pl.BlockSpec((MLA_HEADS, tl, 1), lambda i: (0, i, 0))],
                    [st((l, MLA_DI), BF16), st((l, MLA_DI), BF16), st((MLA_HEADS, l, 1), F32)])(dm, o, a)


def mla_post(a, dcqn, dckvn, dkr_pairs, dz, q_g, kv_g, cos, sins, name):
    l = a.shape[0]
    tl = ROW_TILE
    npair = MLA_HEADS // 2

    def norm_bwd(x, g, dy):
        xhat, r = _rmsn(x)
        dxh = dy * g
        return r * (dxh - xhat * jnp.mean(dxh * xhat, axis=-1, keepdims=True)), jnp.sum(dy * xhat, axis=0, keepdims=True)

    def body(a_ref, dq_ref, dk_ref, dkr_ref, dz_ref, qg_ref, kg_ref, cos_ref, sin_ref, da_ref, dqg_ref, dkg_ref):
        i = pl.program_id(0)
        da_ref[:, 0:MLA_DI] = dz_ref[...]
        dcq, dqg = norm_bwd(a_ref[:, MLA_CQ0:MLA_CKV0], qg_ref[...], dq_ref[...])
        da_ref[:, MLA_CQ0:MLA_CKV0] = dcq.astype(BF16)
        dckv, dkg = norm_bwd(a_ref[:, MLA_CKV0:MLA_KR0], kg_ref[...], dk_ref[...])
        da_ref[:, MLA_CKV0:MLA_KR0] = dckv.astype(BF16)
        dk2 = dkr_ref[0]
        for p in range(1, npair):
            dk2 = dk2 + dkr_ref[p]
        dk2 = _rope_t(dk2, cos_ref[...], sin_ref[...])
        dk2 = dk2 + pltpu.roll(dk2, MLA_ROPE, 1)
        lane = lax.broadcasted_iota(jnp.int32, dk2.shape, 1)
        da_ref[:, MLA_KR0:MLA_AW] = jnp.where(lane < MLA_ROPE, dk2, 0.0).astype(BF16)
        _acc(dqg_ref, dqg, i)
        _acc(dkg_ref, dkg, i)

    st = jax.ShapeDtypeStruct
    return _rowcall(body, name, l // tl,
                    [_rows(tl, MLA_AW), _rows(tl, MLA_Q_RANK), _rows(tl, MLA_KV_RANK),
                     pl.BlockSpec((npair, tl, LANES), lambda i: (0, i, 0)), _rows(tl, MLA_DI),
                     _full((1, MLA_Q_RANK)), _full((1, MLA_KV_RANK)), _rows(tl, LANES), _rows(tl, LANES)],
                    [_rows(tl, MLA_AW), _full((1, MLA_Q_RANK)), _full((1, MLA_KV_RANK))],
                    [st((l, MLA_AW), BF16), st((1, MLA_Q_RANK), F32), st((1, MLA_KV_RANK), F32)])(
        a, dcqn, dckvn, dkr_pairs, dz, q_g.reshape(1, -1), kv_g.reshape(1, -1), cos, sins)


def _mla_w_in_perm(w):
    r = MLA_Q_RANK + MLA_KV_RANK + MLA_ROPE
    pad = jnp.zeros(w.shape[:-1] + (MLA_AW - MLA_KR0 - MLA_ROPE,), w.dtype)
    return jnp.concatenate([w[..., r:], w[..., :r], pad], axis=-1)


def _mla_w_in_unperm(g):
    r = MLA_Q_RANK + MLA_KV_RANK + MLA_ROPE
    return jnp.concatenate([g[..., MLA_DI:MLA_DI + r], g[..., :MLA_DI]], axis=-1)


def _mla_w_uq_split(w):
    k = w.shape[0]
    w3 = w.reshape(k, MLA_HEADS, MLA_NOPE + MLA_ROPE)
    return w3[:, :, :MLA_NOPE].reshape(k, MLA_HEADS * MLA_NOPE), w3[:, :, MLA_NOPE:].reshape(k, MLA_HEADS * MLA_ROPE)


def _mla_w_uq_merge(gn, gr):
    k = gn.shape[0]
    return jnp.concatenate([gn.reshape(k, MLA_HEADS, MLA_NOPE), gr.reshape(k, MLA_HEADS, MLA_ROPE)], axis=2).reshape(k, -1)


def mla_layer_fwd(h, p, wf, cos, sins, tag):
    hn = rms_fwd(h, p["norm_g"], tag + "_rms")
    w_in = _mla_w_in_perm(wf["w_in"])
    w_uq_n, w_uq_r = _mla_w_uq_split(wf["w_uq"])
    a = matmul(hn, w_in, "nn", tag + "_mm_in")
    cqn, ckvn, krs = mla_pre(a, p["q_norm_g"], p["kv_norm_g"], cos, sins, tag + "_pre")
    qn = matmul(cqn, w_uq_n, "nn", tag + "_mm_qn", out_dtype=BF16)
    qr_raw = matmul(cqn, w_uq_r, "nn", tag + "_mm_qr")
    qr = mla_rope_q(qr_raw, cos, sins, tag + "_rope_q")
    kv = matmul(ckvn, wf["w_ukv"], "nn", tag + "_mm_kv", out_dtype=BF16)
    o, lse = flash_fwd(qn, qr, kv, krs, tag + "_flash")
    m = mla_gate_fwd(o, a, tag + "_gate")
    h_out = matmul(m, wf["w_out"], "nn", tag + "_mm_out", add=h)
    return h_out, (h, hn, a, cqn, ckvn, krs, qn, qr, kv, o, lse, m, w_in, w_uq_n, w_uq_r)


def mla_layer_bwd(dh_out, saved, p, wf, cos, sins, tag):
    h, hn, a, cqn, ckvn, krs, qn, qr, kv, o, lse, m, w_in, w_uq_n, w_uq_r = saved
    l = h.shape[0]
    dm = matmul(dh_out, wf["w_out"], "nt", tag + "_mm_dm")
    g_w_out = matmul(m, dh_out, "tn", tag + "_mm_gwout")
    do, dz, delta = mla_gate_bwd(dm, o, a, tag + "_gate_bwd")
    lse_row = lse.reshape(MLA_HEADS, l // ATT_T, 1, ATT_T)
    delta_row = delta.reshape(MLA_HEADS, l // ATT_T, 1, ATT_T)
    dkv, dkr_pairs = flash_dkv(qn, qr, kv, krs, do, lse_row, delta_row, tag + "_flash_dkv")
    dqn, dqr = flash_dq(qn, qr, kv, krs, do, lse, delta, cos, sins, tag + "_flash_dq")
    dcqn = matmul(dqn, w_uq_n, "nt", tag + "_mm_dcq_n")
    dcqn = matmul(dqr, w_uq_r, "nt", tag + "_mm_dcq_r", add=dcqn)
    g_uq_n = matmul(cqn, dqn, "tn", tag + "_mm_guq_n")
    g_uq_r = matmul(cqn, dqr, "tn", tag + "_mm_guq_r")
    dckvn = matmul(dkv, wf["w_ukv"], "nt", tag + "_mm_dckv")
    g_ukv = matmul(ckvn, dkv, "tn", tag + "_mm_gukv")
    da, dqg, dkg = mla_post(a, dcqn, dckvn, dkr_pairs, dz, p["q_norm_g"], p["kv_norm_g"], cos, sins, tag + "_post")
    dhn = matmul(da, w_in, "nt", tag + "_mm_dhn")
    g_w_in = matmul(hn, da, "tn", tag + "_mm_gwin")
    dh, dng = rms_bwd(h, p["norm_g"], dhn, dh_out, tag + "_rms_bwd")
    grads = {"norm_g": dng.reshape(-1), "w_in": _mla_w_in_unperm(g_w_in), "q_norm_g": dqg.reshape(-1),
             "w_uq": _mla_w_uq_merge(g_uq_n, g_uq_r), "kv_norm_g": dkg.reshape(-1), "w_ukv": g_ukv, "w_out": g_w_out}
    return dh, grads


ANY = pl.BlockSpec(memory_space=pl.ANY)


def _me():
    return lax.axis_index("x"), lax.axis_index("y"), lax.axis_index("c")


def _other_chips(x, y):
    return [(1 - x, y), (x, 1 - y), (1 - x, 1 - y)]


def _rcopy(src, dst, ssem, rsem, dev):
    return pltpu.make_async_remote_copy(src_ref=src, dst_ref=dst, send_sem=ssem, recv_sem=rsem,
                                        device_id=dev, device_id_type=MESH)


def _half(ref, c, hf):
    return ref.at[pl.ds(c * hf, hf), :]


def weights_allgather(wb):
    nr, w = wb.shape
    hf = nr // 2

    def body(w_ref, o_ref, ssem, rsem, lsem):
        x, y, c = _me()
        k = 2 * x + y
        chips = _other_chips(x, y)
        mine = pltpu.make_async_copy(w_ref, o_ref.at[k], lsem)
        mine.start()
        first = [_rcopy(_half(w_ref, c, hf), _half(o_ref.at[k], c, hf), ssem.at[j], rsem.at[j], (cx, cy, c))
                 for j, (cx, cy) in enumerate(chips)]
        for cp in first:
            cp.start()
        passed = []
        for j, (cx, cy) in enumerate(chips):
            region = _half(o_ref.at[2 * cx + cy], c, hf)
            _rcopy(region, region, ssem.at[j], rsem.at[j], (cx, cy, c)).wait_recv()
            fwd = _rcopy(region, region, ssem.at[3 + j], rsem.at[3 + j], (x, y, 1 - c))
            fwd.start()
            passed.append(fwd)
        for j, (cx, cy) in enumerate(chips):
            region = _half(o_ref.at[2 * cx + cy], 1 - c, hf)
            _rcopy(region, region, ssem.at[3 + j], rsem.at[3 + j], (x, y, 1 - c)).wait_recv()
        for cp in first + passed:
            cp.wait_send()
        mine.wait()

    return pl.pallas_call(
        body, name="weights_allgather", in_specs=[ANY], out_specs=ANY,
        out_shape=jax.ShapeDtypeStruct((N_CHIPS, nr, w), wb.dtype),
        scratch_shapes=[pltpu.SemaphoreType.DMA((6,)), pltpu.SemaphoreType.DMA((6,)), pltpu.SemaphoreType.DMA(())],
    )(wb)


def grads_to_sibling(p):
    _, nr, w = p.shape
    hf = nr // 2

    def body(p_ref, o_ref, ssem, rsem):
        x, y, c = _me()
        cp = _rcopy(p_ref.at[:, pl.ds((1 - c) * hf, hf), :], o_ref, ssem, rsem, (x, y, 1 - c))
        cp.start()
        cp.wait()

    return pl.pallas_call(
        body, name="grads_to_sibling", in_specs=[ANY], out_specs=ANY,
        out_shape=jax.ShapeDtypeStruct((N_CHIPS, hf, w), p.dtype),
        scratch_shapes=[pltpu.SemaphoreType.DMA(()), pltpu.SemaphoreType.DMA(())])(p)


def pair_sum(p, ra):
    _, nr, w = p.shape
    hf = nr // 2
    tr = _pick_rows(hf)
    nb = hf // tr

    def body(c_ref, p_ref, r_ref, o_ref):
        o_ref[...] = p_ref[...] + r_ref[...]

    c = lax.axis_index("c").astype(jnp.int32).reshape(1)
    return pl.pallas_call(
        body, name="pair_sum",
        grid_spec=pltpu.PrefetchScalarGridSpec(
            num_scalar_prefetch=1, grid=(N_CHIPS, nb),
            in_specs=[pl.BlockSpec((1, tr, w), lambda k, i, c_ref: (k, c_ref[0] * nb + i, 0)),
                      pl.BlockSpec((1, tr, w), lambda k, i, c_ref: (k, i, 0))],
            out_specs=pl.BlockSpec((1, tr, w), lambda k, i, c_ref: (k, i, 0))),
        out_shape=jax.ShapeDtypeStruct((N_CHIPS, hf, w), F32),
        compiler_params=_cparams(("parallel", "parallel")))(c, p, ra)


def grads_across_chips(t):
    _, hf, w = t.shape

    def body(t_ref, o_ref, ssem, rsem, lsem):
        x, y, c = _me()
        k = 2 * x + y
        chips = _other_chips(x, y)
        mine = pltpu.make_async_copy(t_ref.at[k], o_ref.at[k], lsem)
        mine.start()
        sends = [_rcopy(t_ref.at[2 * cx + cy], o_ref.at[k], ssem.at[j], rsem.at[j], (cx, cy, c))
                 for j, (cx, cy) in enumerate(chips)]
        for cp in sends:
            cp.start()
        for j, (cx, cy) in enumerate(chips):
            _rcopy(t_ref.at[k], o_ref.at[2 * cx + cy], ssem.at[j], rsem.at[j], (cx, cy, c)).wait_recv()
        for cp in sends:
            cp.wait_send()
        mine.wait()

    return pl.pallas_call(
        body, name="grads_across_chips", in_specs=[ANY], out_specs=ANY,
        out_shape=jax.ShapeDtypeStruct((N_CHIPS, hf, w), t.dtype),
        scratch_shapes=[pltpu.SemaphoreType.DMA((3,)), pltpu.SemaphoreType.DMA((3,)), pltpu.SemaphoreType.DMA(())])(t)


def chip_sum(rb):
    _, hf, w = rb.shape
    tr = _pick_rows(hf)

    def body(r_ref, o_ref):
        o_ref[...] = ((r_ref[0] + r_ref[1]) + r_ref[2]) + r_ref[3]

    return pl.pallas_call(
        body, name="chip_sum", grid=(hf // tr,),
        in_specs=[pl.BlockSpec((N_CHIPS, tr, w), lambda i: (0, i, 0))],
        out_specs=pl.BlockSpec((tr, w), lambda i: (i, 0)),
        out_shape=jax.ShapeDtypeStruct((hf, w), F32), compiler_params=_cparams(("parallel",)))(rb)


def reduced_to_sibling(f):
    hf, w = f.shape

    def body(f_ref, o_ref, ssem, rsem, lsem):
        x, y, c = _me()
        mine = pltpu.make_async_copy(f_ref, _half(o_ref, c, hf), lsem)
        mine.start()
        cp = _rcopy(f_ref, _half(o_ref, c, hf), ssem, rsem, (x, y, 1 - c))
        cp.start()
        _rcopy(f_ref, _half(o_ref, 1 - c, hf), ssem, rsem, (x, y, 1 - c)).wait_recv()
        cp.wait_send()
        mine.wait()

    return pl.pallas_call(
        body, name="reduced_to_sibling", in_specs=[ANY], out_specs=ANY,
        out_shape=jax.ShapeDtypeStruct((2 * hf, w), f.dtype),
        scratch_shapes=[pltpu.SemaphoreType.DMA(()), pltpu.SemaphoreType.DMA(()), pltpu.SemaphoreType.DMA(())])(f)


def small_allgather(g, row0, nrs):
    w = g.shape[1]

    def body(g_ref, o_ref, ssem, rsem, lsem):
        x, y, c = _me()
        k = 2 * x + y
        chips = _other_chips(x, y)
        src = g_ref.at[pl.ds(row0, nrs), :]
        mine = pltpu.make_async_copy(src, o_ref.at[k], lsem)
        mine.start()
        sends = [_rcopy(src, o_ref.at[k], ssem.at[j], rsem.at[j], (cx, cy, c)) for j, (cx, cy) in enumerate(chips)]
        for cp in sends:
            cp.start()
        for j, (cx, cy) in enumerate(chips):
            _rcopy(src, o_ref.at[2 * cx + cy], ssem.at[j], rsem.at[j], (cx, cy, c)).wait_recv()
        for cp in sends:
            cp.wait_send()
        mine.wait()

    return pl.pallas_call(
        body, name="small_allgather", in_specs=[ANY], out_specs=ANY,
        out_shape=jax.ShapeDtypeStruct((N_CHIPS, nrs, w), g.dtype),
        scratch_shapes=[pltpu.SemaphoreType.DMA((3,)), pltpu.SemaphoreType.DMA((3,)), pltpu.SemaphoreType.DMA(())])(g)


def adamw(w, g, m, v, name):
    r, wd = w.shape
    tr = _pick_rows(r)
    bc1 = 1.0 - ADAM_B1 ** ADAM_STEP
    bc2 = 1.0 - ADAM_B2 ** ADAM_STEP

    def body(w_ref, g_ref, m_ref, v_ref, d_ref, nm_ref, nv_ref):
        gv = g_ref[...]
        nm = ADAM_B1 * m_ref[...] + (1.0 - ADAM_B1) * gv
        nv = ADAM_B2 * v_ref[...] + (1.0 - ADAM_B2) * (gv * gv)
        nm_ref[...] = nm
        nv_ref[...] = nv
        d_ref[...] = -ADAM_LR * ((nm / bc1) / (jnp.sqrt(nv / bc2) + ADAM_EPS) + ADAM_WD * w_ref[...])

    spec = pl.BlockSpec((tr, wd), lambda i: (i, 0))
    st = jax.ShapeDtypeStruct((r, wd), F32)
    return pl.pallas_call(body, name=name, grid=(r // tr,), in_specs=[spec] * 4, out_specs=[spec] * 3,
                          out_shape=[st, st, st], compiler_params=_cparams(("parallel",)))(w, g, m, v)


LAYER_KINDS = ("gmlp", "s5", "mla", "gmlp")
PARAMS = {
    "gmlp": ("norm_g", "w_in", "ln_g", "ln_b", "w_s", "b_s", "w_out"),
    "s5": ("norm_g", "w_in", "a_re", "a_im", "log_step", "b_re", "b_im", "c_re", "c_im", "d_skip", "w_glu", "b_glu", "w_out"),
    "mla": ("norm_g", "w_in", "q_norm_g", "w_uq", "kv_norm_g", "w_ukv", "w_out"),
}
COL_SHARDED = ("w_in", "w_uq", "w_ukv")
ROW_SHARDED = ("w_out", "w_glu")
WEIGHT_NAMES = [("l%d_" % i) + n for i, kind in enumerate(LAYER_KINDS) for n in PARAMS[kind]] + ["final_norm_g"]


def _is_big(name):
    return name.split("_", 1)[1] in COL_SHARDED + ROW_SHARDED


BIG = [n for n in WEIGHT_NAMES if _is_big(n)]
SMALL = [n for n in WEIGHT_NAMES if not _is_big(n)]


def _pack_rows(blocks):
    return jnp.concatenate([b.reshape(-1, PACK_W) for b in blocks], axis=0)


def _shard_major(name, full):
    r, c = full.shape
    if name.split("_", 1)[1] in COL_SHARDED:
        t = full.reshape(r, N_CHIPS, c // N_CHIPS).transpose(1, 0, 2)
    else:
        t = full.reshape(N_CHIPS, r // N_CHIPS, c)
    return t.reshape(N_CHIPS, -1, PACK_W)


def _from_shard_major(name, t, block_shape):
    r, c = block_shape
    if name.split("_", 1)[1] in COL_SHARDED:
        return t.reshape(N_CHIPS, r, c).transpose(1, 0, 2).reshape(r, N_CHIPS * c)
    return t.reshape(N_CHIPS * r, c)


def _small_pack(arrs, total_padded):
    flat = jnp.concatenate([a.reshape(-1) for a in arrs])
    return jnp.pad(flat, (0, total_padded - flat.shape[0]))


def kernel(x, positions, l0_norm_g, l0_w_in, l0_ln_g, l0_ln_b, l0_w_s, l0_b_s, l0_w_out, l1_norm_g, l1_w_in, l1_a_re, l1_a_im, l1_log_step, l1_b_re, l1_b_im, l1_c_re, l1_c_im, l1_d_skip, l1_w_glu, l1_b_glu, l1_w_out, l2_norm_g, l2_w_in, l2_q_norm_g, l2_w_uq, l2_kv_norm_g, l2_w_ukv, l2_w_out, l3_norm_g, l3_w_in, l3_ln_g, l3_ln_b, l3_w_s, l3_b_s, l3_w_out, final_norm_g, loss_target, m_l0_norm_g, m_l0_w_in, m_l0_ln_g, m_l0_ln_b, m_l0_w_s, m_l0_b_s, m_l0_w_out, m_l1_norm_g, m_l1_w_in, m_l1_a_re, m_l1_a_im, m_l1_log_step, m_l1_b_re, m_l1_b_im, m_l1_c_re, m_l1_c_im, m_l1_d_skip, m_l1_w_glu, m_l1_b_glu, m_l1_w_out, m_l2_norm_g, m_l2_w_in, m_l2_q_norm_g, m_l2_w_uq, m_l2_kv_norm_g, m_l2_w_ukv, m_l2_w_out, m_l3_norm_g, m_l3_w_in, m_l3_ln_g, m_l3_ln_b, m_l3_w_s, m_l3_b_s, m_l3_w_out, m_final_norm_g, v_l0_norm_g, v_l0_w_in, v_l0_ln_g, v_l0_ln_b, v_l0_w_s, v_l0_b_s, v_l0_w_out, v_l1_norm_g, v_l1_w_in, v_l1_a_re, v_l1_a_im, v_l1_log_step, v_l1_b_re, v_l1_b_im, v_l1_c_re, v_l1_c_im, v_l1_d_skip, v_l1_w_glu, v_l1_b_glu, v_l1_w_out, v_l2_norm_g, v_l2_w_in, v_l2_q_norm_g, v_l2_w_uq, v_l2_kv_norm_g, v_l2_w_ukv, v_l2_w_out, v_l3_norm_g, v_l3_w_in, v_l3_ln_g, v_l3_ln_b, v_l3_w_s, v_l3_b_s, v_l3_w_out, v_final_norm_g):
    args = locals()
    w = {n: args[n] for n in WEIGHT_NAMES}
    mom_m = {n: args["m_" + n] for n in WEIGHT_NAMES}
    mom_v = {n: args["v_" + n] for n in WEIGHT_NAMES}
    h0 = x[0]
    target = loss_target[0]
    pos = positions.reshape(-1, 1)

    big_rows = [w[n].size // PACK_W for n in BIG]
    nrb = sum(big_rows)
    w_big = _pack_rows([w[n] for n in BIG])
    gathered = weights_allgather(jnp.pad(w_big.astype(BF16), ((0, -nrb % 32), (0, 0))))
    full = {}
    r0 = 0
    for n, nr in zip(BIG, big_rows):
        full[n] = _from_shard_major(n, gathered[:, r0:r0 + nr, :], w[n].shape)
        r0 += nr

    def layer_params(i):
        pre = "l%d_" % i
        p = {k[len(pre):]: v for k, v in w.items() if k.startswith(pre)}
        wf = {k[len(pre):]: v for k, v in full.items() if k.startswith(pre)}
        return p, wf

    cos, sins = rope_tables(pos)
    h = h0
    saved = []
    for i, kind in enumerate(LAYER_KINDS):
        p, wf = layer_params(i)
        tag = "l%d" % i
        if kind == "gmlp":
            h, s = gmlp_layer_fwd(h, p, wf, tag)
        elif kind == "s5":
            h, s = s5_layer_fwd(h, p, wf, tag)
        else:
            h, s = mla_layer_fwd(h, p, wf, cos, sins, tag)
        saved.append(s)
    loss_part, dh, g_final = loss_head(h, final_norm_g, target)
    loss = lax.psum(loss_part[0, 0], ("x", "y", "c"))

    grads = {"final_norm_g": g_final.reshape(-1)}
    for i in reversed(range(len(LAYER_KINDS))):
        kind = LAYER_KINDS[i]
        p, wf = layer_params(i)
        tag = "l%d" % i
        if kind == "gmlp":
            dh, g = gmlp_layer_bwd(dh, saved[i], p, wf, tag)
        elif kind == "s5":
            dh, g = s5_layer_bwd(dh, saved[i], p, wf, tag)
        else:
            dh, g = mla_layer_bwd(dh, saved[i], p, wf, cos, sins, tag)
        for k, val in g.items():
            grads["l%d_%s" % (i, k)] = val
    grad_x = dh[None]

    n_small = sum(w[n].size for n in SMALL)
    piece = N_CHIPS * 2 * 8 * PACK_W
    n_small_pad = -(-n_small // piece) * piece
    nrs = n_small_pad // N_CHIPS // PACK_W
    packed = jnp.concatenate(
        [_shard_major(n, grads[n]) for n in BIG]
        + [_small_pack([grads[n] for n in SMALL], n_small_pad).reshape(N_CHIPS, nrs, PACK_W)], axis=1)
    from_sibling = grads_to_sibling(packed)
    chip_part = pair_sum(packed, from_sibling)
    from_chips = grads_across_chips(chip_part)
    reduced_half = chip_sum(from_chips)
    reduced = reduced_to_sibling(reduced_half)
    small_all = small_allgather(reduced, nrb, nrs)

    g_big = reduced[:nrb]
    d_big, nm_big, nv_big = adamw(w_big, g_big, _pack_rows([mom_m[n] for n in BIG]),
                                  _pack_rows([mom_v[n] for n in BIG]), "adamw_big")
    g_small = small_all.reshape(-1, PACK_W)
    sp = lambda d: _small_pack([d[n] for n in SMALL], n_small_pad).reshape(-1, PACK_W)
    d_small, nm_small, nv_small = adamw(sp(w), g_small, sp(mom_m), sp(mom_v), "adamw_small")

    def unpack(big_buf, small_buf):
        out = {}
        r = 0
        for n, nr in zip(BIG, big_rows):
            out[n] = big_buf[r:r + nr].reshape(w[n].shape)
            r += nr
        flat = small_buf.reshape(-1)
        o = 0
        for n in SMALL:
            out[n] = flat[o:o + w[n].size].reshape(w[n].shape)
            o += w[n].size
        return out

    g_out = unpack(g_big, g_small)
    d_out = unpack(d_big, d_small)
    nm_out = unpack(nm_big, nm_small)
    nv_out = unpack(nv_big, nv_small)
    return (loss, grad_x, *[g_out[n] for n in WEIGHT_NAMES], *[d_out[n] for n in WEIGHT_NAMES],
            *[nm_out[n] for n in WEIGHT_NAMES], *[nv_out[n] for n in WEIGHT_NAMES])
```

```python
import functools
import math

import jax
import jax.numpy as jnp
import numpy as np
from jax import lax
from jax.experimental import pallas as pl
from jax.experimental.pallas import tpu as pltpu

F32 = jnp.float32
BF16 = jnp.bfloat16
MESH = pl.DeviceIdType.MESH
VMEM_LIMIT_BYTES = 56 * 1024 * 1024
LANES = 128
PACK_W = 1024
ROW_TILE = 256
MM_BLOCK_BYTES = 6 * 1024 * 1024

NORM_EPS = 1e-6
N_CHIPS = 4
GMLP_CHUNK = 128
GMLP_GROUPS = 8
S5_GROUPS = 128
S5_GROUP = 16
S5_STATE = 64
S5_SB = 16
S5_SEG = 8
MLA_HEADS = 16
MLA_NOPE = 128
MLA_ROPE = 64
MLA_Q_RANK = 384
MLA_KV_RANK = 128
MLA_SCALE = (MLA_NOPE + MLA_ROPE) ** -0.5
ROPE_THETA = 10000.0
NEG_INF = -1e30
ADAM_LR, ADAM_B1, ADAM_B2, ADAM_EPS, ADAM_WD, ADAM_STEP = 0.001, 0.9, 0.999, 1e-08, 0.01, 10

DN_NN = (((1,), (0,)), ((), ()))
DN_NT = (((1,), (1,)), ((), ()))
DN_TN = (((0,), (0,)), ((), ()))


def _cparams(sem):
    return pltpu.CompilerParams(dimension_semantics=sem, vmem_limit_bytes=VMEM_LIMIT_BYTES)


def _pick(n, cands=(512, 384, 256, 128)):
    for c in cands:
        if n % c == 0:
            return c
    return n


def _pick_rows(r, cap=512):
    return max(t for t in range(8, cap + 1, 8) if r % t == 0)


def _dot(a, b, dn):
    return lax.dot_general(a.astype(BF16), b.astype(BF16), dn, preferred_element_type=F32)


def _sigmoid(x):
    return 1.0 / (1.0 + jnp.exp(-x))


def _gelu(x):
    c = math.sqrt(2.0 / math.pi)
    t = jnp.tanh(c * (x + 0.044715 * x * x * x))
    return 0.5 * x * (1.0 + t)


def _gelu_grad(x):
    c = math.sqrt(2.0 / math.pi)
    t = jnp.tanh(c * (x + 0.044715 * x * x * x))
    return 0.5 * (1.0 + t) + 0.5 * x * (1.0 - t * t) * c * (1.0 + 3.0 * 0.044715 * x * x)


def _silu(z):
    return z * _sigmoid(z)


def _silu_grad(z):
    s = _sigmoid(z)
    return s * (1.0 + z * (1.0 - s))


def matmul(a, b, mode, name, out_dtype=F32, add=None):
    if mode == "nn":
        (m, k), n = a.shape, b.shape[1]
    elif mode == "nt":
        (m, k), n = a.shape, b.shape[0]
    else:
        (k, m), n = a.shape, b.shape[1]
    tm = _pick(m, [t for t in (1024, 512, 384, 256, 128) if t * k * a.dtype.itemsize <= MM_BLOCK_BYTES])
    tn = _pick(n, [t for t in (512, 384, 256, 128) if t * k * b.dtype.itemsize <= MM_BLOCK_BYTES])
    dn = {"nn": DN_NN, "nt": DN_NT, "tn": DN_TN}[mode]

    def body(*refs):
        if add is None:
            a_ref, b_ref, o_ref = refs
        else:
            a_ref, b_ref, add_ref, o_ref = refs
        r = _dot(a_ref[...], b_ref[...], dn)
        if add is not None:
            r = r + add_ref[...].astype(F32)
        o_ref[...] = r.astype(out_dtype)

    a_spec = pl.BlockSpec((k, tm), lambda i, j: (0, i)) if mode == "tn" else pl.BlockSpec((tm, k), lambda i, j: (i, 0))
    b_spec = pl.BlockSpec((tn, k), lambda i, j: (j, 0)) if mode == "nt" else pl.BlockSpec((k, tn), lambda i, j: (0, j))
    o_spec = pl.BlockSpec((tm, tn), lambda i, j: (i, j))
    in_specs = [a_spec, b_spec] + ([o_spec] if add is not None else [])
    args = (a, b) + ((add,) if add is not None else ())
    return pl.pallas_call(
        body, name=name, grid=(m // tm, n // tn), in_specs=in_specs, out_specs=o_spec,
        out_shape=jax.ShapeDtypeStruct((m, n), out_dtype),
        compiler_params=_cparams(("parallel", "arbitrary")))(*args)


def _rows(tl, w, col=0):
    return pl.BlockSpec((tl, w), lambda i: (i, col))


def _full(shape):
    nd = len(shape)
    return pl.BlockSpec(tuple(shape), lambda i: (0,) * nd)


def _rowcall(body, name, n_steps, in_specs, out_specs, out_shape, scratch=()):
    return pl.pallas_call(
        body, name=name, grid=(n_steps,), in_specs=in_specs, out_specs=out_specs, out_shape=out_shape,
        scratch_shapes=list(scratch), compiler_params=_cparams(("arbitrary",)))


def _acc(ref, val, i):
    @pl.when(i == 0)
    def _():
        ref[...] = val

    @pl.when(i != 0)
    def _():
        ref[...] += val


def rms_fwd(h, g, name):
    l, d = h.shape
    tl = ROW_TILE

    def body(h_ref, g_ref, o_ref):
        x = h_ref[...]
        r = lax.rsqrt(jnp.mean(x * x, axis=-1, keepdims=True) + NORM_EPS)
        o_ref[...] = (x * r * g_ref[...]).astype(BF16)

    return _rowcall(body, name, l // tl, [_rows(tl, d), _full((1, d))], _rows(tl, d),
                    jax.ShapeDtypeStruct((l, d), BF16))(h, g.reshape(1, d))


def rms_bwd(h, g, dhn, dh_in, name):
    l, d = h.shape
    tl = ROW_TILE

    def body(h_ref, g_ref, dhn_ref, dhi_ref, dh_ref, dg_ref):
        i = pl.program_id(0)
        x = h_ref[...]
        r = lax.rsqrt(jnp.mean(x * x, axis=-1, keepdims=True) + NORM_EPS)
        xhat = x * r
        dy = dhn_ref[...]
        dxh = dy * g_ref[...]
        dx = r * (dxh - xhat * jnp.mean(dxh * xhat, axis=-1, keepdims=True))
        dh_ref[...] = dhi_ref[...] + dx
        _acc(dg_ref, jnp.sum(dy * xhat, axis=0, keepdims=True), i)

    return _rowcall(body, name, l // tl, [_rows(tl, d), _full((1, d)), _rows(tl, d), _rows(tl, d)],
                    [_rows(tl, d), _full((1, d))],
                    [jax.ShapeDtypeStruct((l, d), F32), jax.ShapeDtypeStruct((1, d), F32)])(h, g.reshape(1, d), dhn, dh_in)


def loss_head(h, g, target):
    l, d = h.shape
    tl = ROW_TILE

    def body(h_ref, g_ref, t_ref, loss_ref, dh_ref, dg_ref):
        i = pl.program_id(0)
        x = h_ref[...]
        gg = g_ref[...]
        r = lax.rsqrt(jnp.mean(x * x, axis=-1, keepdims=True) + NORM_EPS)
        xhat = x * r
        err = xhat * gg - t_ref[...]
        part = 0.5 * jnp.sum(jnp.mean(err * err, axis=-1, keepdims=True), axis=0, keepdims=True)
        _acc(loss_ref, part, i)
        dy = err * (1.0 / d)
        dxh = dy * gg
        dh_ref[...] = r * (dxh - xhat * jnp.mean(dxh * xhat, axis=-1, keepdims=True))
        _acc(dg_ref, jnp.sum(dy * xhat, axis=0, keepdims=True), i)

    return _rowcall(body, "loss_head", l // tl, [_rows(tl, d), _full((1, d)), _rows(tl, d)],
                    [_full((1, 1)), _rows(tl, d), _full((1, d))],
                    [jax.ShapeDtypeStruct((1, 1), F32), jax.ShapeDtypeStruct((l, d), F32),
                     jax.ShapeDtypeStruct((1, d), F32)])(h, g.reshape(1, d), target)


def _gmlp_common(a_ref, lng_ref, lnb_ref):
    di = lng_ref.shape[1]
    u_pre = a_ref[:, 0:di]
    v_pre = a_ref[:, di:2 * di]
    z = a_ref[:, 2 * di:3 * di]
    vg = _gelu(v_pre)
    mu = jnp.mean(vg, axis=-1, keepdims=True)
    xc = vg - mu
    rstd = lax.rsqrt(jnp.mean(xc * xc, axis=-1, keepdims=True) + NORM_EPS)
    vhat = xc * rstd
    vn = vhat * lng_ref[...] + lnb_ref[...]
    return u_pre, v_pre, z, vhat, rstd, vn


def _tril(w):
    r = lax.broadcasted_iota(jnp.int32, w.shape, 0)
    c = lax.broadcasted_iota(jnp.int32, w.shape, 1)
    return jnp.where(c <= r, w, 0.0)


def gmlp_gate_fwd(a, ln_g, ln_b, w_s, b_s, name):
    l, w3 = a.shape
    di = w3 // 3
    dg = di // GMLP_GROUPS
    tl = GMLP_CHUNK

    def body(a_ref, lng_ref, lnb_ref, ws_ref, bs_ref, m_ref):
        u_pre, _, z, _, _, vn = _gmlp_common(a_ref, lng_ref, lnb_ref)
        gate = _gelu(u_pre) * _silu(z)
        for g in range(GMLP_GROUPS):
            sl = slice(g * dg, (g + 1) * dg)
            s = _dot(_tril(ws_ref[g]), vn[:, sl], DN_NN) + bs_ref[g]
            m_ref[:, sl] = (gate[:, sl] * s).astype(BF16)

    return _rowcall(body, name, l // tl,
                    [_rows(tl, w3), _full((1, di)), _full((1, di)), _full(w_s.shape), _full((GMLP_GROUPS, tl, 1))],
                    _rows(tl, di), jax.ShapeDtypeStruct((l, di), BF16))(
        a, ln_g.reshape(1, di), ln_b.reshape(1, di), w_s, b_s.reshape(GMLP_GROUPS, tl, 1))


def gmlp_gate_bwd(a, dm, ln_g, ln_b, w_s, b_s, name):
    l, w3 = a.shape
    di = w3 // 3
    dg = di // GMLP_GROUPS
    tl = GMLP_CHUNK

    def body(a_ref, dm_ref, lng_ref, lnb_ref, ws_ref, bs_ref, da_ref, dlg_ref, dlb_ref, dws_ref, dbs_ref, dvn_ref):
        i = pl.program_id(0)
        u_pre, v_pre, z, vhat, rstd, vn = _gmlp_common(a_ref, lng_ref, lnb_ref)
        dm_v = dm_ref[...]
        u = _gelu(u_pre)
        sz = _silu(z)
        for g in range(GMLP_GROUPS):
            sl = slice(g * dg, (g + 1) * dg)
            wt = _tril(ws_ref[g])
            vn_g = vn[:, sl]
            s = _dot(wt, vn_g, DN_NN) + bs_ref[g]
            dmg = dm_v[:, sl]
            ds = dmg * u[:, sl] * sz[:, sl]
            da_ref[:, sl] = (dmg * s * sz[:, sl] * _gelu_grad(u_pre[:, sl])).astype(BF16)
            da_ref[:, 2 * di + g * dg:2 * di + (g + 1) * dg] = (
                dmg * u[:, sl] * s * _silu_grad(z[:, sl])).astype(BF16)
            dvn_ref[:, sl] = _dot(wt, ds, DN_TN)
            dw = _tril(_dot(ds, vn_g, DN_NT))
            db = jnp.sum(ds, axis=1, keepdims=True)

            @pl.when(i == 0)
            def _():
                dws_ref[g] = dw
                dbs_ref[g] = db

            @pl.when(i != 0)
            def _():
                dws_ref[g] += dw
                dbs_ref[g] += db

        dvn = dvn_ref[...]
        dxh = dvn * lng_ref[...]
        dvg = rstd * (dxh - jnp.mean(dxh, axis=-1, keepdims=True) - vhat * jnp.mean(dxh * vhat, axis=-1, keepdims=True))
        da_ref[:, di:2 * di] = (dvg * _gelu_grad(v_pre)).astype(BF16)
        _acc(dlg_ref, jnp.sum(dvn * vhat, axis=0, keepdims=True), i)
        _acc(dlb_ref, jnp.sum(dvn, axis=0, keepdims=True), i)

    outs = _rowcall(
        body, name, l // tl,
        [_rows(tl, w3), _rows(tl, di), _full((1, di)), _full((1, di)), _full(w_s.shape), _full((GMLP_GROUPS, tl, 1))],
        [_rows(tl, w3), _full((1, di)), _full((1, di)), _full(w_s.shape), _full((GMLP_GROUPS, tl, 1))],
        [jax.ShapeDtypeStruct((l, w3), BF16), jax.ShapeDtypeStruct((1, di), F32), jax.ShapeDtypeStruct((1, di), F32),
         jax.ShapeDtypeStruct(w_s.shape, F32), jax.ShapeDtypeStruct((GMLP_GROUPS, tl, 1), F32)],
        scratch=[pltpu.VMEM((tl, di), F32)])(
        a, dm, ln_g.reshape(1, di), ln_b.reshape(1, di), w_s, b_s.reshape(GMLP_GROUPS, tl, 1))
    return outs


def gmlp_layer_fwd(h, p, wf, tag):
    hn = rms_fwd(h, p["norm_g"], tag + "_rms")
    a = matmul(hn, wf["w_in"], "nn", tag + "_mm_in")
    m = gmlp_gate_fwd(a, p["ln_g"], p["ln_b"], p["w_s"], p["b_s"], tag + "_gate")
    h_out = matmul(m, wf["w_out"], "nn", tag + "_mm_out", add=h)
    return h_out, (h, hn, a, m)


def gmlp_layer_bwd(dh_out, saved, p, wf, tag):
    h, hn, a, m = saved
    dm = matmul(dh_out, wf["w_out"], "nt", tag + "_mm_dm")
    g_w_out = matmul(m, dh_out, "tn", tag + "_mm_gwout")
    da, dlg, dlb, dws, dbs = gmlp_gate_bwd(a, dm, p["ln_g"], p["ln_b"], p["w_s"], p["b_s"], tag + "_gate_bwd")
    dhn = matmul(da, wf["w_in"], "nt", tag + "_mm_dhn")
    g_w_in = matmul(hn, da, "tn", tag + "_mm_gwin")
    dh, dng = rms_bwd(h, p["norm_g"], dhn, dh_out, tag + "_rms_bwd")
    grads = {"norm_g": dng.reshape(-1), "w_in": g_w_in, "ln_g": dlg.reshape(-1), "ln_b": dlb.reshape(-1),
             "w_s": dws, "b_s": dbs.reshape(GMLP_GROUPS, GMLP_CHUNK), "w_out": g_w_out}
    return dh, grads


def _cmul(ar, ai, br, bi):
    return ar * br - ai * bi, ar * bi + ai * br


S5_PG = 16


def _gblock(tail):
    return pl.BlockSpec((S5_PG,) + tuple(tail), lambda i: (i, 0, 0))


def s5_params_fwd(a_re, a_im, log_step, b_re, b_im):
    g, p, hh = b_re.shape

    def body(ar_ref, ai_ref, ls_ref, br_ref, bi_ref, lr_ref, li_ref, bbr_ref, bbi_ref):
        ar, ai = ar_ref[...], ai_ref[...]
        step = jnp.exp(ls_ref[...])
        mag = jnp.exp(ar * step)
        lr, li = mag * jnp.cos(ai * step), mag * jnp.sin(ai * step)
        den = 1.0 / (ar * ar + ai * ai)
        fr, fi = _cmul(lr - 1.0, li, ar * den, -ai * den)
        lr_ref[...] = lr
        li_ref[...] = li
        bbr, bbi = _cmul(fr, fi, br_ref[...], bi_ref[...])
        bbr_ref[...] = bbr
        bbi_ref[...] = bbi

    s1 = jax.ShapeDtypeStruct((g, p, 1), F32)
    s3 = jax.ShapeDtypeStruct((g, p, hh), F32)
    b1, b0, b3 = _gblock((p, 1)), _gblock((1, 1)), _gblock((p, hh))
    return pl.pallas_call(body, name="s5_params_fwd", grid=(g // S5_PG,), in_specs=[b1, b1, b0, b3, b3],
                          out_specs=[b1, b1, b3, b3], out_shape=[s1, s1, s3, s3],
                          compiler_params=_cparams(("parallel",)))(
        a_re.reshape(g, p, 1), a_im.reshape(g, p, 1), log_step.reshape(g, 1, 1), b_re, b_im)


def s5_params_bwd(a_re, a_im, log_step, b_re, b_im, dl_re, dl_im, dbb_re, dbb_im):
    g, p, hh = b_re.shape

    def body(ar_ref, ai_ref, ls_ref, br_ref, bi_ref, dlr_ref, dli_ref, dbr_ref, dbi_ref,
             gar_ref, gai_ref, gls_ref, gbr_ref, gbi_ref):
        ar, ai = ar_ref[...], ai_ref[...]
        step = jnp.exp(ls_ref[...])
        mag = jnp.exp(ar * step)
        lr, li = mag * jnp.cos(ai * step), mag * jnp.sin(ai * step)
        den = 1.0 / (ar * ar + ai * ai)
        ir, ii = ar * den, -ai * den
        fr, fi = _cmul(lr - 1.0, li, ir, ii)
        br, bi = br_ref[...], bi_ref[...]
        dbr, dbi = dbr_ref[...], dbi_ref[...]
        gbr, gbi = _cmul(fr, -fi, dbr, dbi)
        gbr_ref[...] = gbr
        gbi_ref[...] = gbi
        pr, pi = _cmul(br, -bi, dbr, dbi)
        gfr = jnp.sum(pr, axis=-1, keepdims=True)
        gfi = jnp.sum(pi, axis=-1, keepdims=True)
        t_r, t_i = _cmul(ir, -ii, gfr, gfi)
        glr, gli = dlr_ref[...] + t_r, dli_ref[...] + t_i
        c1r, c1i = _cmul(step * lr, -step * li, glr, gli)
        qr, qi = _cmul(fr, fi, ir, ii)
        c2r, c2i = _cmul(-qr, qi, gfr, gfi)
        gar_ref[...] = c1r + c2r
        gai_ref[...] = c1i + c2i
        wr, wi = _cmul(ar, ai, lr, li)
        sr, _ = _cmul(wr, -wi, glr, gli)
        gls_ref[...] = jnp.sum(sr, axis=1, keepdims=True) * step

    s1 = jax.ShapeDtypeStruct((g, p, 1), F32)
    s3 = jax.ShapeDtypeStruct((g, p, hh), F32)
    b1, b0, b3 = _gblock((p, 1)), _gblock((1, 1)), _gblock((p, hh))
    return pl.pallas_call(body, name="s5_params_bwd", grid=(g // S5_PG,),
                          in_specs=[b1, b1, b0, b3, b3, b1, b1, b3, b3], out_specs=[b1, b1, b0, b3, b3],
                          out_shape=[s1, s1, jax.ShapeDtypeStruct((g, 1, 1), F32), s3, s3],
                          compiler_params=_cparams(("parallel",)))(
        a_re.reshape(g, p, 1), a_im.reshape(g, p, 1), log_step.reshape(g, 1, 1), b_re, b_im,
        dl_re, dl_im, dbb_re, dbb_im)


def _blockdiag(t):
    sb, n, r, c = t.shape
    eye = jnp.eye(n, dtype=bool)[None, :, None, :, None]
    full = jnp.where(eye, t[:, :, :, None, :], jnp.zeros((), t.dtype))
    return full.reshape(sb, n * r, n * c)


def _blockdiag_extract(m, r, c):
    sb = m.shape[0]
    n = m.shape[1] // r
    m5 = m.reshape(sb, n, r, n, c)
    return jnp.stack([m5[:, i, :, i, :] for i in range(n)], axis=1)


S5_TB = 64
S5_UNROLL = 8


def s5_scan_fwd(a_p, lam_re, lam_im, wb_re, wb_im, wc_re, wc_im, d_skip, x0_re, x0_im, name):
    l = a_p.shape[0]
    di = d_skip.shape[1]
    rows = S5_SEG * S5_TB
    nb = l // rows
    ns = wb_re.shape[2]

    def body(u_ref, lr_ref, li_ref, wbr_ref, wbi_ref, wcr_ref, wci_ref, ds_ref, x0r_ref, x0i_ref,
             y_ref, ckr_ref, cki_ref, xer_ref, xei_ref, bur, bui, xr_s, xi_s):
        b = pl.program_id(1)

        @pl.when(b == 0)
        def _():
            xr_s[...] = x0r_ref[0]
            xi_s[...] = x0i_ref[0]

        ckr_ref[0, 0] = xr_s[...]
        cki_ref[0, 0] = xi_s[...]
        u = u_ref[...]
        bur[...] = _dot(u, wbr_ref[0], DN_NN)
        bui[...] = _dot(u, wbi_ref[0], DN_NN)
        lr = jnp.broadcast_to(lr_ref[0], (S5_SEG, ns))
        li = jnp.broadcast_to(li_ref[0], (S5_SEG, ns))

        def step(t, carry):
            xr, xi = carry
            sl = pl.ds(pl.multiple_of(t * S5_SEG, S5_SEG), S5_SEG)
            nr = lr * xr - li * xi + bur[sl, :]
            ni = lr * xi + li * xr + bui[sl, :]
            bur[sl, :] = nr
            bui[sl, :] = ni
            return nr, ni

        xr, xi = lax.fori_loop(0, S5_TB, step, (xr_s[...], xi_s[...]), unroll=S5_UNROLL)
        xr_s[...] = xr
        xi_s[...] = xi
        xer_ref[0] = xr
        xei_ref[0] = xi
        y_ref[...] = _dot(bur[...], wcr_ref[0], DN_NN) - _dot(bui[...], wci_ref[0], DN_NN) + ds_ref[...] * u

    sb3 = lambda s, b: (s, 0, 0)
    st = jax.ShapeDtypeStruct
    return pl.pallas_call(
        body, name=name, grid=(S5_SB, nb),
        in_specs=[pl.BlockSpec((rows, LANES), lambda s, b: (b, s)),
                  pl.BlockSpec((1, 1, ns), sb3), pl.BlockSpec((1, 1, ns), sb3),
                  pl.BlockSpec((1, LANES, ns), sb3), pl.BlockSpec((1, LANES, ns), sb3),
                  pl.BlockSpec((1, ns, LANES), sb3), pl.BlockSpec((1, ns, LANES), sb3),
                  pl.BlockSpec((1, LANES), lambda s, b: (0, s)),
                  pl.BlockSpec((1, S5_SEG, ns), sb3), pl.BlockSpec((1, S5_SEG, ns), sb3)],
        out_specs=[pl.BlockSpec((rows, LANES), lambda s, b: (b, s)),
                   pl.BlockSpec((1, 1, S5_SEG, ns), lambda s, b: (s, b, 0, 0)),
                   pl.BlockSpec((1, 1, S5_SEG, ns), lambda s, b: (s, b, 0, 0)),
                   pl.BlockSpec((1, S5_SEG, ns), sb3), pl.BlockSpec((1, S5_SEG, ns), sb3)],
        out_shape=[st((l, di), F32), st((S5_SB, nb, S5_SEG, ns), F32), st((S5_SB, nb, S5_SEG, ns), F32),
                   st((S5_SB, S5_SEG, ns), F32), st((S5_SB, S5_SEG, ns), F32)],
        scratch_shapes=[pltpu.VMEM((rows, ns), F32), pltpu.VMEM((rows, ns), F32),
                        pltpu.VMEM((S5_SEG, ns), F32), pltpu.VMEM((S5_SEG, ns), F32)],
        compiler_params=_cparams(("parallel", "arbitrary")))(
        a_p, lam_re, lam_im, wb_re, wb_im, wc_re, wc_im, d_skip, x0_re, x0_im)


def s5_ends(inp, lam_re, lam_im, w_re, w_im, adjoint, name):
    l = inp.shape[0]
    rows = S5_SEG * S5_TB
    nb = l // rows
    ns = lam_re.shape[2]

    def body(i_ref, lr_ref, li_ref, wr_ref, wi_ref, er_ref, ei_ref, pr_b, pi_b, xr_s, xi_s):
        b = pl.program_id(1)

        @pl.when(b == 0)
        def _():
            xr_s[...] = jnp.zeros_like(xr_s)
            xi_s[...] = jnp.zeros_like(xi_s)

        v = i_ref[...]
        lr = jnp.broadcast_to(lr_ref[0], (S5_SEG, ns))
        li = jnp.broadcast_to(li_ref[0], (S5_SEG, ns))
        if adjoint:
            pr_b[...] = _dot(v, wr_ref[0], DN_NT)
            pi_b[...] = -_dot(v, wi_ref[0], DN_NT)
            li = -li
        else:
            pr_b[...] = _dot(v, wr_ref[0], DN_NN)
            pi_b[...] = _dot(v, wi_ref[0], DN_NN)

        def step(k, carry):
            xr, xi = carry
            t = S5_TB - 1 - k if adjoint else k
            sl = pl.ds(pl.multiple_of(t * S5_SEG, S5_SEG), S5_SEG)
            return lr * xr - li * xi + pr_b[sl, :], lr * xi + li * xr + pi_b[sl, :]

        xr, xi = lax.fori_loop(0, S5_TB, step, (xr_s[...], xi_s[...]), unroll=S5_UNROLL)
        xr_s[...] = xr
        xi_s[...] = xi
        er_ref[0] = xr
        ei_ref[0] = xi

    sb3 = lambda s, b: (s, 0, 0)
    blk = (lambda s, b: (nb - 1 - b, s)) if adjoint else (lambda s, b: (b, s))
    wshape = (1, ns, LANES) if adjoint else (1, LANES, ns)
    st = jax.ShapeDtypeStruct((S5_SB, S5_SEG, ns), F32)
    return pl.pallas_call(
        body, name=name, grid=(S5_SB, nb),
        in_specs=[pl.BlockSpec((rows, LANES), blk), pl.BlockSpec((1, 1, ns), sb3), pl.BlockSpec((1, 1, ns), sb3),
                  pl.BlockSpec(wshape, sb3), pl.BlockSpec(wshape, sb3)],
        out_specs=[pl.BlockSpec((1, S5_SEG, ns), sb3), pl.BlockSpec((1, S5_SEG, ns), sb3)],
        out_shape=[st, st],
        scratch_shapes=[pltpu.VMEM((rows, ns), F32), pltpu.VMEM((rows, ns), F32),
                        pltpu.VMEM((S5_SEG, ns), F32), pltpu.VMEM((S5_SEG, ns), F32)],
        compiler_params=_cparams(("parallel", "arbitrary")))(inp, lam_re, lam_im, w_re, w_im)


def s5_scan_bwd(a_p, dy, lam_re, lam_im, wb_re, wb_im, wc_re, wc_im, d_skip, ck_re, ck_im, a0_re, a0_im, name):
    l = a_p.shape[0]
    di = d_skip.shape[1]
    rows = S5_SEG * S5_TB
    nb = l // rows
    ns = wb_re.shape[2]

    def body(u_ref, dy_ref, lr_ref, li_ref, wbr_ref, wbi_ref, wcr_ref, wci_ref, ds_ref, ckr_ref, cki_ref,
             a0r_ref, a0i_ref,
             du_ref, dwbr_ref, dwbi_ref, dwcr_ref, dwci_ref, dds_ref, dlr_ref, dli_ref, aer_ref, aei_ref,
             xr_b, xi_b, gr_b, gi_b, ar_s, ai_s):
        b = pl.program_id(1)

        @pl.when(b == 0)
        def _():
            ar_s[...] = a0r_ref[0]
            ai_s[...] = a0i_ref[0]

        u = u_ref[...]
        dyv = dy_ref[...]
        lr = jnp.broadcast_to(lr_ref[0], (S5_SEG, ns))
        li = jnp.broadcast_to(li_ref[0], (S5_SEG, ns))
        xr_b[...] = _dot(u, wbr_ref[0], DN_NN)
        xi_b[...] = _dot(u, wbi_ref[0], DN_NN)

        def fstep(t, carry):
            xr, xi = carry
            sl = pl.ds(pl.multiple_of(t * S5_SEG, S5_SEG), S5_SEG)
            nr = lr * xr - li * xi + xr_b[sl, :]
            ni = lr * xi + li * xr + xi_b[sl, :]
            xr_b[sl, :] = nr
            xi_b[sl, :] = ni
            return nr, ni

        x0r, x0i = ckr_ref[0, 0], cki_ref[0, 0]
        lax.fori_loop(0, S5_TB, fstep, (x0r, x0i), unroll=S5_UNROLL)
        dwcr = _dot(xr_b[...], dyv, DN_TN)
        dwci = -_dot(xi_b[...], dyv, DN_TN)
        gr_b[...] = _dot(dyv, wcr_ref[0], DN_NT)
        gi_b[...] = -_dot(dyv, wci_ref[0], DN_NT)

        def bstep(k, carry):
            ar, ai, dlr, dli = carry
            t = S5_TB - 1 - k
            sl = pl.ds(pl.multiple_of(t * S5_SEG, S5_SEG), S5_SEG)
            slp = pl.ds(pl.multiple_of(jnp.maximum(t - 1, 0) * S5_SEG, S5_SEG), S5_SEG)
            nr = gr_b[sl, :] + lr * ar + li * ai
            ni = gi_b[sl, :] + lr * ai - li * ar
            gr_b[sl, :] = nr
            gi_b[sl, :] = ni
            first = t == 0
            pr = jnp.where(first, x0r, xr_b[slp, :])
            pi = jnp.where(first, x0i, xi_b[slp, :])
            dlr = dlr + nr * pr + ni * pi
            dli = dli + ni * pr - nr * pi
            return nr, ni, dlr, dli

        zero = jnp.zeros((S5_SEG, ns), F32)
        ar, ai, dlr, dli = lax.fori_loop(0, S5_TB, bstep, (ar_s[...], ai_s[...], zero, zero), unroll=S5_UNROLL)
        ar_s[...] = ar
        ai_s[...] = ai
        aer_ref[0] = ar
        aei_ref[0] = ai
        dsk = ds_ref[...]
        du_ref[...] = (_dot(gr_b[...], wbr_ref[0], DN_NT) + _dot(gi_b[...], wbi_ref[0], DN_NT) + dsk * dyv).astype(BF16)
        dwbr = _dot(u, gr_b[...], DN_TN)
        dwbi = _dot(u, gi_b[...], DN_TN)
        dds = jnp.sum(dyv * u, axis=0, keepdims=True)

        @pl.when(b == 0)
        def _():
            dwbr_ref[0] = dwbr
            dwbi_ref[0] = dwbi
            dwcr_ref[0] = dwcr
            dwci_ref[0] = dwci
            dds_ref[...] = dds
            dlr_ref[0] = dlr
            dli_ref[0] = dli

        @pl.when(b != 0)
        def _():
            dwbr_ref[0] += dwbr
            dwbi_ref[0] += dwbi
            dwcr_ref[0] += dwcr
            dwci_ref[0] += dwci
            dds_ref[...] += dds
            dlr_ref[0] += dlr
            dli_ref[0] += dli

    sb3 = lambda s, b: (s, 0, 0)
    rev = lambda s, b: (nb - 1 - b, s)
    st = jax.ShapeDtypeStruct
    return pl.pallas_call(
        body, name=name, grid=(S5_SB, nb),
        in_specs=[pl.BlockSpec((rows, LANES), rev), pl.BlockSpec((rows, LANES), rev),
                  pl.BlockSpec((1, 1, ns), sb3), pl.BlockSpec((1, 1, ns), sb3),
                  pl.BlockSpec((1, LANES, ns), sb3), pl.BlockSpec((1, LANES, ns), sb3),
                  pl.BlockSpec((1, ns, LANES), sb3), pl.BlockSpec((1, ns, LANES), sb3),
                  pl.BlockSpec((1, LANES), lambda s, b: (0, s)),
                  pl.BlockSpec((1, 1, S5_SEG, ns), lambda s, b: (s, nb - 1 - b, 0, 0)),
                  pl.BlockSpec((1, 1, S5_SEG, ns), lambda s, b: (s, nb - 1 - b, 0, 0)),
                  pl.BlockSpec((1, S5_SEG, ns), sb3), pl.BlockSpec((1, S5_SEG, ns), sb3)],
        out_specs=[pl.BlockSpec((rows, LANES), rev),
                   pl.BlockSpec((1, LANES, ns), sb3), pl.BlockSpec((1, LANES, ns), sb3),
                   pl.BlockSpec((1, ns, LANES), sb3), pl.BlockSpec((1, ns, LANES), sb3),
                   pl.BlockSpec((1, LANES), lambda s, b: (0, s)),
                   pl.BlockSpec((1, S5_SEG, ns), sb3), pl.BlockSpec((1, S5_SEG, ns), sb3),
                   pl.BlockSpec((1, S5_SEG, ns), sb3), pl.BlockSpec((1, S5_SEG, ns), sb3)],
        out_shape=[st((l, di), BF16), st((S5_SB, LANES, ns), F32), st((S5_SB, LANES, ns), F32),
                   st((S5_SB, ns, LANES), F32), st((S5_SB, ns, LANES), F32), st((1, di), F32),
                   st((S5_SB, S5_SEG, ns), F32), st((S5_SB, S5_SEG, ns), F32),
                   st((S5_SB, S5_SEG, ns), F32), st((S5_SB, S5_SEG, ns), F32)],
        scratch_shapes=[pltpu.VMEM((rows, ns), F32), pltpu.VMEM((rows, ns), F32),
                        pltpu.VMEM((rows, ns), F32), pltpu.VMEM((rows, ns), F32),
                        pltpu.VMEM((S5_SEG, ns), F32), pltpu.VMEM((S5_SEG, ns), F32)],
        compiler_params=_cparams(("parallel", "arbitrary")))(
        a_p, dy, lam_re, lam_im, wb_re, wb_im, wc_re, wc_im, d_skip, ck_re, ck_im, a0_re, a0_im)


def s5_carry(e_re, e_im, lam_re, lam_im, seg_len, reverse, name):
    sb, seg, ns = e_re.shape

    def body(er_ref, ei_ref, lr_ref, li_ref, cr_ref, ci_ref):
        pr, pi = lr_ref[...], li_ref[...]
        if reverse:
            pi = -pi
        for _ in range(int(math.log2(seg_len))):
            pr, pi = _cmul(pr, pi, pr, pi)
        er, ei = er_ref[...], ei_ref[...]
        row = lax.broadcasted_iota(jnp.int32, (sb, seg, ns), 1)
        cr = jnp.zeros((sb, seg, ns), F32)
        ci = jnp.zeros((sb, seg, ns), F32)
        cur_r = jnp.zeros((sb, 1, ns), F32)
        cur_i = jnp.zeros((sb, 1, ns), F32)
        order = range(seg - 2, -1, -1) if reverse else range(1, seg)
        for s in order:
            src = s + 1 if reverse else s - 1
            mr, mi = _cmul(pr, pi, cur_r, cur_i)
            cur_r = jnp.sum(jnp.where(row == src, er, 0.0), axis=1, keepdims=True) + mr
            cur_i = jnp.sum(jnp.where(row == src, ei, 0.0), axis=1, keepdims=True) + mi
            cr = jnp.where(row == s, cur_r, cr)
            ci = jnp.where(row == s, cur_i, ci)
        cr_ref[...] = cr
        ci_ref[...] = ci

    st = jax.ShapeDtypeStruct((sb, seg, ns), F32)
    return pl.pallas_call(body, name=name, out_shape=[st, st],
                          compiler_params=pltpu.CompilerParams(vmem_limit_bytes=VMEM_LIMIT_BYTES))(e_re, e_im, lam_re, lam_im)


def s5_act(y, name):
    l, d = y.shape
    tl = ROW_TILE

    def body(y_ref, o_ref):
        o_ref[...] = _gelu(y_ref[...]).astype(BF16)

    return _rowcall(body, name, l // tl, [_rows(tl, d)], _rows(tl, d), jax.ShapeDtypeStruct((l, d), BF16))(y)


def s5_gate_fwd(y, t, b_glu, a_p, name):
    l, d = y.shape
    tl = ROW_TILE

    def body(y_ref, t_ref, b_ref, z_ref, m_ref):
        yg = _gelu(y_ref[...])
        m_ref[...] = (yg * _sigmoid(t_ref[...] + b_ref[...]) * _silu(z_ref[...])).astype(BF16)

    return _rowcall(body, name, l // tl, [_rows(tl, d), _rows(tl, d), _full((1, d)), _rows(tl, d, 1)], _rows(tl, d),
                    jax.ShapeDtypeStruct((l, d), BF16))(y, t, b_glu.reshape(1, d), a_p)


def s5_gate_bwd(dm, y, t, b_glu, a_p, name):
    l, d = y.shape
    tl = ROW_TILE

    def body(dm_ref, y_ref, t_ref, b_ref, z_ref, dt_ref, dyg_ref, dz_ref, db_ref):
        i = pl.program_id(0)
        dmv = dm_ref[...]
        z = z_ref[...]
        yg = _gelu(y_ref[...])
        sg = _sigmoid(t_ref[...] + b_ref[...])
        y2 = yg * sg
        dy2 = dmv * _silu(z)
        dz_ref[...] = (dmv * y2 * _silu_grad(z)).astype(BF16)
        dyg_ref[...] = dy2 * sg
        dt = dy2 * yg * sg * (1.0 - sg)
        dt_ref[...] = dt.astype(BF16)
        _acc(db_ref, jnp.sum(dt, axis=0, keepdims=True), i)

    st = jax.ShapeDtypeStruct
    return _rowcall(body, name, l // tl, [_rows(tl, d), _rows(tl, d), _rows(tl, d), _full((1, d)), _rows(tl, d, 1)],
                    [_rows(tl, d), _rows(tl, d), _rows(tl, d), _full((1, d))],
                    [st((l, d), BF16), st((l, d), F32), st((l, d), BF16), st((1, d), F32)])(
        dm, y, t, b_glu.reshape(1, d), a_p)


def s5_act_bwd(y, dyg_a, dyg_b, name):
    l, d = y.shape
    tl = ROW_TILE

    def body(y_ref, a_ref, b_ref, o_ref):
        o_ref[...] = (a_ref[...] + b_ref[...]) * _gelu_grad(y_ref[...])

    return _rowcall(body, name, l // tl, [_rows(tl, d)] * 3, _rows(tl, d), jax.ShapeDtypeStruct((l, d), F32))(y, dyg_a, dyg_b)


def _seg_perm(t):
    l, d = t.shape
    return t.reshape(S5_SEG, l // S5_SEG, d).transpose(1, 0, 2).reshape(l, d)


def _seg_unperm(t):
    l, d = t.shape
    return t.reshape(l // S5_SEG, S5_SEG, d).transpose(1, 0, 2).reshape(l, d)


def _s5_weights(p):
    lr, li, bbr, bbi = s5_params_fwd(p["a_re"], p["a_im"], p["log_step"], p["b_re"], p["b_im"])
    ns = 8 * S5_STATE
    lam_re = lr.reshape(S5_SB, 1, ns)
    lam_im = li.reshape(S5_SB, 1, ns)
    to_bd = lambda t: _blockdiag(t.reshape(S5_SB, 8, t.shape[1], t.shape[2]))
    wb_re = to_bd(bbr.transpose(0, 2, 1)).astype(BF16)
    wb_im = to_bd(bbi.transpose(0, 2, 1)).astype(BF16)
    wc_re = to_bd(p["c_re"].transpose(0, 2, 1)).astype(BF16)
    wc_im = to_bd(p["c_im"].transpose(0, 2, 1)).astype(BF16)
    return lam_re, lam_im, wb_re, wb_im, wc_re, wc_im


def s5_layer_fwd(h, p, wf, tag):
    l = h.shape[0]
    di = p["d_skip"].shape[0]
    hn = rms_fwd(h, p["norm_g"], tag + "_rms")
    hn_p = _seg_perm(hn)
    a_p = matmul(hn_p, wf["w_in"], "nn", tag + "_mm_in")
    sw = _s5_weights(p)
    dsk = p["d_skip"].reshape(1, di)
    e_re, e_im = s5_ends(a_p, sw[0], sw[1], sw[2], sw[3], False, tag + "_scan_ends")
    c_re, c_im = s5_carry(e_re, e_im, sw[0], sw[1], l // S5_SEG, False, tag + "_carry")
    y, ck_re, ck_im, _, _ = s5_scan_fwd(a_p, *sw, dsk, c_re, c_im, tag + "_scan")
    yg = s5_act(y, tag + "_act")
    t = matmul(yg, wf["w_glu"], "nn", tag + "_mm_glu")
    m = s5_gate_fwd(y, t, p["b_glu"], a_p, tag + "_gate")
    out_p = matmul(m, wf["w_out"], "nn", tag + "_mm_out")
    h_out = residual_add(h, _seg_unperm(out_p), tag + "_res")
    return h_out, (h, hn_p, a_p, sw, ck_re, ck_im, y, yg, t, m)


def residual_add(h, y, name):
    l, d = h.shape
    tl = ROW_TILE

    def body(h_ref, y_ref, o_ref):
        o_ref[...] = h_ref[...] + y_ref[...]

    return _rowcall(body, name, l // tl, [_rows(tl, d)] * 2, _rows(tl, d), jax.ShapeDtypeStruct((l, d), F32))(h, y)


def s5_layer_bwd(dh_out, saved, p, wf, tag):
    h, hn_p, a_p, sw, ck_re, ck_im, y, yg, t, m = saved
    l = h.shape[0]
    di = p["d_skip"].shape[0]
    dsk = p["d_skip"].reshape(1, di)
    dout_p = _seg_perm(dh_out)
    dm = matmul(dout_p, wf["w_out"], "nt", tag + "_mm_dm")
    g_w_out = matmul(m, dout_p, "tn", tag + "_mm_gwout")
    dt, dyg_a, dz, db_glu = s5_gate_bwd(dm, y, t, p["b_glu"], a_p, tag + "_gate_bwd")
    dyg_b = matmul(dt, wf["w_glu"], "nt", tag + "_mm_dyg")
    g_w_glu = matmul(yg, dt, "tn", tag + "_mm_gwglu")
    dy = s5_act_bwd(y, dyg_a, dyg_b, tag + "_act_bwd")
    e_re, e_im = s5_ends(dy, sw[0], sw[1], sw[4], sw[5], True, tag + "_scanb_ends")
    c_re, c_im = s5_carry(e_re, e_im, sw[0], sw[1], l // S5_SEG, True, tag + "_carry_bwd")
    du, dwbr, dwbi, dwcr, dwci, dds, dlr, dli, _, _ = s5_scan_bwd(
        a_p, dy, *sw, dsk, ck_re, ck_im, c_re, c_im, tag + "_scanb")
    da = jnp.concatenate([du, dz], axis=1)
    dhn_p = matmul(da, wf["w_in"], "nt", tag + "_mm_dhn")
    g_w_in = matmul(hn_p, da, "tn", tag + "_mm_gwin")
    dh, dng = rms_bwd(h, p["norm_g"], _seg_unperm(dhn_p), dh_out, tag + "_rms_bwd")
    ex = lambda m_, r, c: _blockdiag_extract(m_, r, c).reshape(S5_GROUPS, r, c).transpose(0, 2, 1)
    dbb_re, dbb_im = ex(dwbr, S5_GROUP, S5_STATE), ex(dwbi, S5_GROUP, S5_STATE)
    g_c_re, g_c_im = ex(dwcr, S5_STATE, S5_GROUP), ex(dwci, S5_STATE, S5_GROUP)
    dl_re = lane_sum8(dlr).reshape(S5_GROUPS, S5_STATE, 1)
    dl_im = lane_sum8(dli).reshape(S5_GROUPS, S5_STATE, 1)
    gar, gai, gls, gbr, gbi = s5_params_bwd(p["a_re"], p["a_im"], p["log_step"], p["b_re"], p["b_im"],
                                            dl_re, dl_im, dbb_re, dbb_im)
    grads = {"norm_g": dng.reshape(-1), "w_in": g_w_in, "a_re": gar.reshape(S5_GROUPS, S5_STATE),
             "a_im": gai.reshape(S5_GROUPS, S5_STATE), "log_step": gls.reshape(-1), "b_re": gbr, "b_im": gbi,
             "c_re": g_c_re, "c_im": g_c_im, "d_skip": dds.reshape(-1), "w_glu": g_w_glu,
             "b_glu": db_glu.reshape(-1), "w_out": g_w_out}
    return dh, grads


def lane_sum8(t):
    sb, seg, ns = t.shape

    def body(t_ref, o_ref):
        o_ref[...] = jnp.sum(t_ref[...], axis=1, keepdims=True)

    return pl.pallas_call(body, name="s5_seg_sum", out_shape=jax.ShapeDtypeStruct((sb, 1, ns), F32))(t)


MLA_DI = MLA_HEADS * 128
MLA_CQ0 = MLA_DI
MLA_CKV0 = MLA_CQ0 + MLA_Q_RANK
MLA_KR0 = MLA_CKV0 + MLA_KV_RANK
MLA_AW = MLA_KR0 + LANES


def _rot_half(x):
    w = x.shape[-1]
    lane = lax.broadcasted_iota(jnp.int32, x.shape, x.ndim - 1)
    return jnp.where(lane % MLA_ROPE < MLA_ROPE // 2, pltpu.roll(x, w - MLA_ROPE // 2, x.ndim - 1),
                     pltpu.roll(x, MLA_ROPE // 2, x.ndim - 1))


def rope_tables(pos):
    l = pos.shape[0]
    tl = ROW_TILE
    j = np.arange(LANES) % MLA_ROPE % (MLA_ROPE // 2)
    inv_freq = (ROPE_THETA ** (-(2.0 * j) / MLA_ROPE)).astype(np.float32).reshape(1, LANES)
    sign = np.where(np.arange(LANES) % MLA_ROPE < MLA_ROPE // 2, -1.0, 1.0).astype(np.float32).reshape(1, LANES)

    def body(p_ref, f_ref, s_ref, cos_ref, sin_ref):
        ang = p_ref[...].astype(F32) * f_ref[...]
        cos_ref[...] = jnp.cos(ang)
        sin_ref[...] = jnp.sin(ang) * s_ref[...]

    st = jax.ShapeDtypeStruct((l, LANES), F32)
    return _rowcall(body, "rope_tables", l // tl, [_rows(tl, 1), _full((1, LANES)), _full((1, LANES))],
                    [_rows(tl, LANES)] * 2, [st, st])(pos, jnp.asarray(inv_freq), jnp.asarray(sign))


def _rope(x, cos, sins):
    return x * cos + _rot_half(x) * sins


def _rope_t(dy, cos, sins):
    return dy * cos - sins * _rot_half(dy)


def _rmsn(x):
    r = lax.rsqrt(jnp.mean(x * x, axis=-1, keepdims=True) + NORM_EPS)
    return x * r, r


def mla_pre(a, q_g, kv_g, cos, sins, name):
    l = a.shape[0]
    tl = ROW_TILE

    def body(a_ref, qg_ref, kg_ref, cos_ref, sin_ref, cq_ref, ckv_ref, krs_ref):
        xq, _ = _rmsn(a_ref[:, MLA_CQ0:MLA_CKV0])
        cq_ref[...] = (xq * qg_ref[...]).astype(BF16)
        xk, _ = _rmsn(a_ref[:, MLA_CKV0:MLA_KR0])
        ckv_ref[...] = (xk * kg_ref[...]).astype(BF16)
        kr = a_ref[:, MLA_KR0:MLA_AW]
        kr2 = kr + pltpu.roll(kr, MLA_ROPE, 1)
        kr2 = _rope(kr2, cos_ref[...], sin_ref[...])
        lane = lax.broadcasted_iota(jnp.int32, kr2.shape, 1)
        krs_ref[0] = jnp.where(lane < MLA_ROPE, kr2, 0.0).astype(BF16)
        krs_ref[1] = jnp.where(lane >= MLA_ROPE, kr2, 0.0).astype(BF16)

    st = jax.ShapeDtypeStruct
    return _rowcall(body, name, l // tl,
                    [_rows(tl, MLA_AW), _full((1, MLA_Q_RANK)), _full((1, MLA_KV_RANK)), _rows(tl, LANES), _rows(tl, LANES)],
                    [_rows(tl, MLA_Q_RANK), _rows(tl, MLA_KV_RANK), pl.BlockSpec((2, tl, LANES), lambda i: (0, i, 0))],
                    [st((l, MLA_Q_RANK), BF16), st((l, MLA_KV_RANK), BF16), st((2, l, LANES), BF16)])(
        a, q_g.reshape(1, -1), kv_g.reshape(1, -1), cos, sins)


def mla_rope_q(qr, cos, sins, name):
    l, w = qr.shape
    tl = ROW_TILE

    def body(q_ref, cos_ref, sin_ref, o_ref):
        c, s = cos_ref[...], sin_ref[...]
        for p in range(w // LANES):
            sl = slice(p * LANES, (p + 1) * LANES)
            o_ref[:, sl] = _rope(q_ref[:, sl], c, s).astype(BF16)

    return _rowcall(body, name, l // tl, [_rows(tl, w), _rows(tl, LANES), _rows(tl, LANES)], _rows(tl, w),
                    jax.ShapeDtypeStruct((l, w), BF16))(qr, cos, sins)


ATT_T = 256


def _scores(qn, qr, kn, kr, diagonal, transposed):
    if transposed:
        s = (_dot(kn, qn, DN_NT) + _dot(kr, qr, DN_NT)) * MLA_SCALE
    else:
        s = (_dot(qn, kn, DN_NT) + _dot(qr, kr, DN_NT)) * MLA_SCALE
    if not diagonal:
        return s
    r = lax.broadcasted_iota(jnp.int32, s.shape, 0)
    c = lax.broadcasted_iota(jnp.int32, s.shape, 1)
    return jnp.where((r <= c) if transposed else (c <= r), s, NEG_INF)


def _fold(x):
    return x[:, :LANES], x[:, LANES:]


def flash_fwd(qn, qr, kv, krs, name):
    l = qn.shape[0]
    nq = l // ATT_T

    def body(qn_ref, qr_ref, kv_ref, kr_ref, o_ref, lse_ref, s_buf):
        qi = pl.program_id(1)
        q_r = qr_ref[...]
        q_n = [qn_ref[:, hh * LANES:(hh + 1) * LANES] for hh in range(2)]

        def block_scores(j, mx, diagonal):
            sl = pl.ds(pl.multiple_of(j * ATT_T, ATT_T), ATT_T)
            out = []
            for hh in range(2):
                s = _scores(q_n[hh], q_r, kv_ref[sl, 2 * hh * LANES:(2 * hh + 1) * LANES], kr_ref[hh, sl, :],
                            diagonal, False)
                s_buf[hh, j] = s
                lo, hi = _fold(s)
                out.append(jnp.maximum(mx[hh], jnp.maximum(lo, hi)))
            return tuple(out)

        ninf = jnp.full((ATT_T, LANES), NEG_INF, F32)
        mx = lax.fori_loop(0, qi, lambda j, c: block_scores(j, c, False), (ninf, ninf))
        mx = block_scores(qi, mx, True)
        m = [jnp.max(mx[hh], axis=-1, keepdims=True) for hh in range(2)]

        def block_pv(j, carry):
            sl = pl.ds(pl.multiple_of(j * ATT_T, ATT_T), ATT_T)
            out = []
            for hh in range(2):
                ls, acc = carry[hh]
                p = jnp.exp(s_buf[hh, j] - m[hh])
                lo, hi = _fold(p)
                out.append((ls + (lo + hi),
                            acc + _dot(p, kv_ref[sl, (2 * hh + 1) * LANES:(2 * hh + 2) * LANES], DN_NN)))
            return tuple(out)

        z = jnp.zeros((ATT_T, LANES), F32)
        res = lax.fori_loop(0, qi + 1, block_pv, ((z, z), (z, z)))
        for hh in range(2):
            lsum = jnp.sum(res[hh][0], axis=-1, keepdims=True)
            o_ref[:, hh * LANES:(hh + 1) * LANES] = res[hh][1] / lsum
            lse_ref[hh] = m[hh] + jnp.log(lsum)

    st = jax.ShapeDtypeStruct
    return pl.pallas_call(
        body, name=name, grid=(MLA_HEADS // 2, nq),
        in_specs=[pl.BlockSpec((ATT_T, 2 * LANES), lambda p, i: (i, p)),
                  pl.BlockSpec((ATT_T, LANES), lambda p, i: (i, p)),
                  pl.BlockSpec((l, 4 * LANES), lambda p, i: (0, p)),
                  pl.BlockSpec((2, l, LANES), lambda p, i: (0, 0, 0))],
        out_specs=[pl.BlockSpec((ATT_T, 2 * LANES), lambda p, i: (i, p)),
                   pl.BlockSpec((2, ATT_T, 1), lambda p, i: (p, i, 0))],
        out_shape=[st((l, MLA_DI), F32), st((MLA_HEADS, l, 1), F32)],
        scratch_shapes=[pltpu.VMEM((2, nq, ATT_T, ATT_T), F32)],
        compiler_params=_cparams(("parallel", "arbitrary")))(qn, qr, kv, krs)


def flash_dkv(qn, qr, kv, krs, do, lse_row, delta_row, name):
    l = qn.shape[0]
    nk = l // ATT_T

    def body(qn_ref, qr_ref, do_ref, lse_ref, dl_ref, kv_ref, kr_ref, dkv_ref, dkr_ref):
        kj = pl.program_id(1)
        lane = lax.broadcasted_iota(jnp.int32, (ATT_T, LANES), 1)
        kn = [kv_ref[:, 2 * hh * LANES:(2 * hh + 1) * LANES] for hh in range(2)]
        v = [kv_ref[:, (2 * hh + 1) * LANES:(2 * hh + 2) * LANES] for hh in range(2)]

        def block(i, carry, diagonal):
            sl = pl.ds(pl.multiple_of(i * ATT_T, ATT_T), ATT_T)
            q_r = qr_ref[sl, :]
            out = []
            for hh in range(2):
                dkn, dv, dkr = carry[hh]
                hs = slice(hh * LANES, (hh + 1) * LANES)
                q_n, d_o = qn_ref[sl, hs], do_ref[sl, hs]
                s = _scores(q_n, q_r, kn[hh], kr_ref[hh], diagonal, True)
                pt = jnp.exp(s - lse_ref[hh, i])
                dv = dv + _dot(pt, d_o, DN_NN)
                dpt = _dot(v[hh], d_o, DN_NT)
                dst = (pt * (dpt - dl_ref[hh, i]) * MLA_SCALE).astype(BF16)
                out.append((dkn + _dot(dst, q_n, DN_NN), dv, dkr + _dot(dst, q_r, DN_NN)))
            return tuple(out)

        z = jnp.zeros((ATT_T, LANES), F32)
        res = block(kj, ((z, z, z), (z, z, z)), True)
        res = lax.fori_loop(kj + 1, nk, lambda i, c: block(i, c, False), res)
        for hh in range(2):
            dkv_ref[:, 2 * hh * LANES:(2 * hh + 1) * LANES] = res[hh][0].astype(BF16)
            dkv_ref[:, (2 * hh + 1) * LANES:(2 * hh + 2) * LANES] = res[hh][1].astype(BF16)
        dkr_ref[0] = jnp.where(lane < MLA_ROPE, res[0][2], res[1][2])

    st = jax.ShapeDtypeStruct
    return pl.pallas_call(
        body, name=name, grid=(MLA_HEADS // 2, nk),
        in_specs=[pl.BlockSpec((l, 2 * LANES), lambda p, j: (0, p)),
                  pl.BlockSpec((l, LANES), lambda p, j: (0, p)),
                  pl.BlockSpec((l, 2 * LANES), lambda p, j: (0, p)),
                  pl.BlockSpec((2, nk, 1, ATT_T), lambda p, j: (p, 0, 0, 0)),
                  pl.BlockSpec((2, nk, 1, ATT_T), lambda p, j: (p, 0, 0, 0)),
                  pl.BlockSpec((ATT_T, 4 * LANES), lambda p, j: (j, p)),
                  pl.BlockSpec((2, ATT_T, LANES), lambda p, j: (0, j, 0))],
        out_specs=[pl.BlockSpec((ATT_T, 4 * LANES), lambda p, j: (j, p)),
                   pl.BlockSpec((1, ATT_T, LANES), lambda p, j: (p, j, 0))],
        out_shape=[st((l, 2 * MLA_DI), BF16), st((MLA_HEADS // 2, l, LANES), F32)],
        compiler_params=_cparams(("parallel", "arbitrary")))(qn, qr, do, lse_row, delta_row, kv, krs)


def flash_dq(qn, qr, kv, krs, do, lse, delta, cos, sins, name):
    l = qn.shape[0]
    nq = l // ATT_T

    def body(qn_ref, qr_ref, do_ref, lse_ref, dl_ref, kv_ref, kr_ref, cos_ref, sin_ref, dqn_ref, dqr_ref):
        qi = pl.program_id(1)
        q_r = qr_ref[...]
        q_n = [qn_ref[:, hh * LANES:(hh + 1) * LANES] for hh in range(2)]
        d_o = [do_ref[:, hh * LANES:(hh + 1) * LANES] for hh in range(2)]
        lse_h = [lse_ref[hh] for hh in range(2)]
        dl_h = [dl_ref[hh] for hh in range(2)]

        def block(j, carry, diagonal):
            sl = pl.ds(pl.multiple_of(j * ATT_T, ATT_T), ATT_T)
            dqn0, dqn1, dqr = carry
            dqn = [dqn0, dqn1]
            for hh in range(2):
                kn = kv_ref[sl, 2 * hh * LANES:(2 * hh + 1) * LANES]
                v = kv_ref[sl, (2 * hh + 1) * LANES:(2 * hh + 2) * LANES]
                kr = kr_ref[hh, sl, :]
                s = _scores(q_n[hh], q_r, kn, kr, diagonal, False)
                pr = jnp.exp(s - lse_h[hh])
                dp = _dot(d_o[hh], v, DN_NT)
                ds = (pr * (dp - dl_h[hh]) * MLA_SCALE).astype(BF16)
                dqn[hh] = dqn[hh] + _dot(ds, kn, DN_NN)
                dqr = dqr + _dot(ds, kr, DN_NN)
            return dqn[0], dqn[1], dqr

        z = jnp.zeros((ATT_T, LANES), F32)
        res = lax.fori_loop(0, qi, lambda j, c: block(j, c, False), (z, z, z))
        dqn0, dqn1, dqr = block(qi, res, True)
        dqn_ref[:, 0:LANES] = dqn0.astype(BF16)
        dqn_ref[:, LANES:2 * LANES] = dqn1.astype(BF16)
        dqr_ref[...] = _rope_t(dqr, cos_ref[...], sin_ref[...]).astype(BF16)

    st = jax.ShapeDtypeStruct
    return pl.pallas_call(
        body, name=name, grid=(MLA_HEADS // 2, nq),
        in_specs=[pl.BlockSpec((ATT_T, 2 * LANES), lambda p, i: (i, p)),
                  pl.BlockSpec((ATT_T, LANES), lambda p, i: (i, p)),
                  pl.BlockSpec((ATT_T, 2 * LANES), lambda p, i: (i, p)),
                  pl.BlockSpec((2, ATT_T, 1), lambda p, i: (p, i, 0)),
                  pl.BlockSpec((2, ATT_T, 1), lambda p, i: (p, i, 0)),
                  pl.BlockSpec((l, 4 * LANES), lambda p, i: (0, p)),
                  pl.BlockSpec((2, l, LANES), lambda p, i: (0, 0, 0)),
                  pl.BlockSpec((ATT_T, LANES), lambda p, i: (i, 0)),
                  pl.BlockSpec((ATT_T, LANES), lambda p, i: (i, 0))],
        out_specs=[pl.BlockSpec((ATT_T, 2 * LANES), lambda p, i: (i, p)),
                   pl.BlockSpec((ATT_T, LANES), lambda p, i: (i, p))],
        out_shape=[st((l, MLA_DI), BF16), st((l, MLA_HEADS * MLA_ROPE), BF16)],
        compiler_params=_cparams(("parallel", "arbitrary")))(qn, qr, do, lse, delta, kv, krs, cos, sins)


def mla_gate_fwd(o, a, name):
    l = o.shape[0]
    tl = ROW_TILE

    def body(o_ref, z_ref, m_ref):
        m_ref[...] = (o_ref[...] * _silu(z_ref[...])).astype(BF16)

    return _rowcall(body, name, l // tl, [_rows(tl, MLA_DI), _rows(tl, MLA_DI)], _rows(tl, MLA_DI),
                    jax.ShapeDtypeStruct((l, MLA_DI), BF16))(o, a)


def mla_gate_bwd(dm, o, a, name):
    l = o.shape[0]
    tl = ROW_TILE

    def body(dm_ref, o_ref, z_ref, do_ref, dz_ref, dl_ref):
        dmv, ov, z = dm_ref[...], o_ref[...], z_ref[...]
        d_o = dmv * _silu(z)
        do_ref[...] = d_o.astype(BF16)
        dz_ref[...] = (dmv * ov * _silu_grad(z)).astype(BF16)
        pr = d_o * ov
        for h in range(MLA_HEADS):
            dl_ref[h] = jnp.sum(pr[:, h * LANES:(h + 1) * LANES], axis=1, keepdims=True)

    st = jax.ShapeDtypeStruct
    return _rowcall(body, name, l // tl, [_rows(tl, MLA_DI)] * 3,
                    [_rows(tl, MLA_DI), _rows(tl, MLA_DI), pl.BlockSpec((MLA_HEADS, tl, 1), lambda i: (0, i, 0))],
                    [st((l, MLA_DI), BF16), st((l, MLA_DI), BF16), st((MLA_HEADS, l, 1), F32)])(dm, o, a)


def mla_post(a, dcqn, dckvn, dkr_pairs, dz, q_g, kv_g, cos, sins, name):
    l = a.shape[0]
    tl = ROW_TILE
    npair = MLA_HEADS // 2

    def norm_bwd(x, g, dy):
        xhat, r = _rmsn(x)
        dxh = dy * g
        return r * (dxh - xhat * jnp.mean(dxh * xhat, axis=-1, keepdims=True)), jnp.sum(dy * xhat, axis=0, keepdims=True)

    def body(a_ref, dq_ref, dk_ref, dkr_ref, dz_ref, qg_ref, kg_ref, cos_ref, sin_ref, da_ref, dqg_ref, dkg_ref):
        i = pl.program_id(0)
        da_ref[:, 0:MLA_DI] = dz_ref[...]
        dcq, dqg = norm_bwd(a_ref[:, MLA_CQ0:MLA_CKV0], qg_ref[...], dq_ref[...])
        da_ref[:, MLA_CQ0:MLA_CKV0] = dcq.astype(BF16)
        dckv, dkg = norm_bwd(a_ref[:, MLA_CKV0:MLA_KR0], kg_ref[...], dk_ref[...])
        da_ref[:, MLA_CKV0:MLA_KR0] = dckv.astype(BF16)
        dk2 = dkr_ref[0]
        for p in range(1, npair):
            dk2 = dk2 + dkr_ref[p]
        dk2 = _rope_t(dk2, cos_ref[...], sin_ref[...])
        dk2 = dk2 + pltpu.roll(dk2, MLA_ROPE, 1)
        lane = lax.broadcasted_iota(jnp.int32, dk2.shape, 1)
        da_ref[:, MLA_KR0:MLA_AW] = jnp.where(lane < MLA_ROPE, dk2, 0.0).astype(BF16)
        _acc(dqg_ref, dqg, i)
        _acc(dkg_ref, dkg, i)

    st = jax.ShapeDtypeStruct
    return _rowcall(body, name, l // tl,
                    [_rows(tl, MLA_AW), _rows(tl, MLA_Q_RANK), _rows(tl, MLA_KV_RANK),
                     pl.BlockSpec((npair, tl, LANES), lambda i: (0, i, 0)), _rows(tl, MLA_DI),
                     _full((1, MLA_Q_RANK)), _full((1, MLA_KV_RANK)), _rows(tl, LANES), _rows(tl, LANES)],
                    [_rows(tl, MLA_AW), _full((1, MLA_Q_RANK)), _full((1, MLA_KV_RANK))],
                    [st((l, MLA_AW), BF16), st((1, MLA_Q_RANK), F32), st((1, MLA_KV_RANK), F32)])(
        a, dcqn, dckvn, dkr_pairs, dz, q_g.reshape(1, -1), kv_g.reshape(1, -1), cos, sins)


def _mla_w_in_perm(w):
    r = MLA_Q_RANK + MLA_KV_RANK + MLA_ROPE
    pad = jnp.zeros(w.shape[:-1] + (MLA_AW - MLA_KR0 - MLA_ROPE,), w.dtype)
    return jnp.concatenate([w[..., r:], w[..., :r], pad], axis=-1)


def _mla_w_in_unperm(g):
    r = MLA_Q_RANK + MLA_KV_RANK + MLA_ROPE
    return jnp.concatenate([g[..., MLA_DI:MLA_DI + r], g[..., :MLA_DI]], axis=-1)


def _mla_w_uq_split(w):
    k = w.shape[0]
    w3 = w.reshape(k, MLA_HEADS, MLA_NOPE + MLA_ROPE)
    return w3[:, :, :MLA_NOPE].reshape(k, MLA_HEADS * MLA_NOPE), w3[:, :, MLA_NOPE:].reshape(k, MLA_HEADS * MLA_ROPE)


def _mla_w_uq_merge(gn, gr):
    k = gn.shape[0]
    return jnp.concatenate([gn.reshape(k, MLA_HEADS, MLA_NOPE), gr.reshape(k, MLA_HEADS, MLA_ROPE)], axis=2).reshape(k, -1)


def mla_layer_fwd(h, p, wf, cos, sins, tag):
    hn = rms_fwd(h, p["norm_g"], tag + "_rms")
    w_in = _mla_w_in_perm(wf["w_in"])
    w_uq_n, w_uq_r = _mla_w_uq_split(wf["w_uq"])
    a = matmul(hn, w_in, "nn", tag + "_mm_in")
    cqn, ckvn, krs = mla_pre(a, p["q_norm_g"], p["kv_norm_g"], cos, sins, tag + "_pre")
    qn = matmul(cqn, w_uq_n, "nn", tag + "_mm_qn", out_dtype=BF16)
    qr_raw = matmul(cqn, w_uq_r, "nn", tag + "_mm_qr")
    qr = mla_rope_q(qr_raw, cos, sins, tag + "_rope_q")
    kv = matmul(ckvn, wf["w_ukv"], "nn", tag + "_mm_kv", out_dtype=BF16)
    o, lse = flash_fwd(qn, qr, kv, krs, tag + "_flash")
    m = mla_gate_fwd(o, a, tag + "_gate")
    h_out = matmul(m, wf["w_out"], "nn", tag + "_mm_out", add=h)
    return h_out, (h, hn, a, cqn, ckvn, krs, qn, qr, kv, o, lse, m, w_in, w_uq_n, w_uq_r)


def mla_layer_bwd(dh_out, saved, p, wf, cos, sins, tag):
    h, hn, a, cqn, ckvn, krs, qn, qr, kv, o, lse, m, w_in, w_uq_n, w_uq_r = saved
    l = h.shape[0]
    dm = matmul(dh_out, wf["w_out"], "nt", tag + "_mm_dm")
    g_w_out = matmul(m, dh_out, "tn", tag + "_mm_gwout")
    do, dz, delta = mla_gate_bwd(dm, o, a, tag + "_gate_bwd")
    lse_row = lse.reshape(MLA_HEADS, l // ATT_T, 1, ATT_T)
    delta_row = delta.reshape(MLA_HEADS, l // ATT_T, 1, ATT_T)
    dkv, dkr_pairs = flash_dkv(qn, qr, kv, krs, do, lse_row, delta_row, tag + "_flash_dkv")
    dqn, dqr = flash_dq(qn, qr, kv, krs, do, lse, delta, cos, sins, tag + "_flash_dq")
    dcqn = matmul(dqn, w_uq_n, "nt", tag + "_mm_dcq_n")
    dcqn = matmul(dqr, w_uq_r, "nt", tag + "_mm_dcq_r", add=dcqn)
    g_uq_n = matmul(cqn, dqn, "tn", tag + "_mm_guq_n")
    g_uq_r = matmul(cqn, dqr, "tn", tag + "_mm_guq_r")
    dckvn = matmul(dkv, wf["w_ukv"], "nt", tag + "_mm_dckv")
    g_ukv = matmul(ckvn, dkv, "tn", tag + "_mm_gukv")
    da, dqg, dkg = mla_post(a, dcqn, dckvn, dkr_pairs, dz, p["q_norm_g"], p["kv_norm_g"], cos, sins, tag + "_post")
    dhn = matmul(da, w_in, "nt", tag + "_mm_dhn")
    g_w_in = matmul(hn, da, "tn", tag + "_mm_gwin")
    dh, dng = rms_bwd(h, p["norm_g"], dhn, dh_out, tag + "_rms_bwd")
    grads = {"norm_g": dng.reshape(-1), "w_in": _mla_w_in_unperm(g_w_in), "q_norm_g": dqg.reshape(-1),
             "w_uq": _mla_w_uq_merge(g_uq_n, g_uq_r), "kv_norm_g": dkg.reshape(-1), "w_ukv": g_ukv, "w_out": g_w_out}
    return dh, grads


ANY = pl.BlockSpec(memory_space=pl.ANY)


def _me():
    return lax.axis_index("x"), lax.axis_index("y"), lax.axis_index("c")


def _chip():
    return 2 * lax.axis_index("x") + lax.axis_index("y")


def _other_chips(x, y):
    return [(1 - x, y), (x, 1 - y), (1 - x, 1 - y)]


def _rcopy(src, dst, ssem, rsem, dev):
    return pltpu.make_async_remote_copy(src_ref=src, dst_ref=dst, send_sem=ssem, recv_sem=rsem,
                                        device_id=dev, device_id_type=MESH)


def _half(ref, c, hf):
    return ref.at[pl.ds(c * hf, hf), :]


def weights_allgather(wb):
    nr, w = wb.shape
    hf = nr // 2

    def body(w_ref, o_ref, ssem, rsem):
        x, y, c = _me()
        k = 2 * x + y
        chips = _other_chips(x, y)
        first = [_rcopy(_half(w_ref, c, hf), _half(o_ref.at[k], c, hf), ssem.at[j], rsem.at[j], (cx, cy, c))
                 for j, (cx, cy) in enumerate(chips)]
        for cp in first:
            cp.start()
        passed = []
        for j, (cx, cy) in enumerate(chips):
            region = _half(o_ref.at[2 * cx + cy], c, hf)
            _rcopy(region, region, ssem.at[j], rsem.at[j], (cx, cy, c)).wait_recv()
            fwd = _rcopy(region, region, ssem.at[3 + j], rsem.at[3 + j], (x, y, 1 - c))
            fwd.start()
            passed.append(fwd)
        for j, (cx, cy) in enumerate(chips):
            region = _half(o_ref.at[2 * cx + cy], 1 - c, hf)
            _rcopy(region, region, ssem.at[3 + j], rsem.at[3 + j], (x, y, 1 - c)).wait_recv()
        for cp in first + passed:
            cp.wait_send()

    out = pl.pallas_call(
        body, name="weights_allgather", in_specs=[ANY], out_specs=ANY,
        out_shape=jax.ShapeDtypeStruct((N_CHIPS, nr, w), wb.dtype),
        scratch_shapes=[pltpu.SemaphoreType.DMA((6,)), pltpu.SemaphoreType.DMA((6,))],
    )(wb)
    return lax.dynamic_update_slice(out, wb[None], (_chip(), 0, 0))


def grads_to_sibling(p):
    _, nr, w = p.shape
    hf = nr // 2

    def body(p_ref, o_ref, ssem, rsem):
        x, y, c = _me()
        cp = _rcopy(p_ref.at[:, pl.ds((1 - c) * hf, hf), :], o_ref, ssem, rsem, (x, y, 1 - c))
        cp.start()
        cp.wait()

    return pl.pallas_call(
        body, name="grads_to_sibling", in_specs=[ANY], out_specs=ANY,
        out_shape=jax.ShapeDtypeStruct((N_CHIPS, hf, w), p.dtype),
        scratch_shapes=[pltpu.SemaphoreType.DMA(()), pltpu.SemaphoreType.DMA(())])(p)


def pair_sum(p, ra):
    _, nr, w = p.shape
    hf = nr // 2
    tr = _pick_rows(hf)
    nb = hf // tr

    def body(c_ref, p_ref, r_ref, o_ref):
        o_ref[...] = p_ref[...] + r_ref[...]

    c = lax.axis_index("c").astype(jnp.int32).reshape(1)
    return pl.pallas_call(
        body, name="pair_sum",
        grid_spec=pltpu.PrefetchScalarGridSpec(
            num_scalar_prefetch=1, grid=(N_CHIPS, nb),
            in_specs=[pl.BlockSpec((1, tr, w), lambda k, i, c_ref: (k, c_ref[0] * nb + i, 0)),
                      pl.BlockSpec((1, tr, w), lambda k, i, c_ref: (k, i, 0))],
            out_specs=pl.BlockSpec((1, tr, w), lambda k, i, c_ref: (k, i, 0))),
        out_shape=jax.ShapeDtypeStruct((N_CHIPS, hf, w), F32),
        compiler_params=_cparams(("parallel", "parallel")))(c, p, ra)


def grads_across_chips(t):
    _, hf, w = t.shape

    def body(t_ref, o_ref, ssem, rsem):
        x, y, c = _me()
        k = 2 * x + y
        chips = _other_chips(x, y)
        sends = [_rcopy(t_ref.at[2 * cx + cy], o_ref.at[k], ssem.at[j], rsem.at[j], (cx, cy, c))
                 for j, (cx, cy) in enumerate(chips)]
        for cp in sends:
            cp.start()
        for j, (cx, cy) in enumerate(chips):
            _rcopy(t_ref.at[k], o_ref.at[2 * cx + cy], ssem.at[j], rsem.at[j], (cx, cy, c)).wait_recv()
        for cp in sends:
            cp.wait_send()

    return pl.pallas_call(
        body, name="grads_across_chips", in_specs=[ANY], out_specs=ANY,
        out_shape=jax.ShapeDtypeStruct((N_CHIPS, hf, w), t.dtype),
        scratch_shapes=[pltpu.SemaphoreType.DMA((3,)), pltpu.SemaphoreType.DMA((3,))])(t)


def chip_sum(t, rb):
    _, hf, w = rb.shape
    tr = _pick_rows(hf)
    nb = hf // tr

    def body(kc_ref, t_ref, r_ref, o_ref):
        k = kc_ref[0]
        acc = jnp.where(k == 0, t_ref[0], r_ref[0])
        for j in range(1, N_CHIPS):
            acc = acc + jnp.where(k == j, t_ref[0], r_ref[j])
        o_ref[...] = acc

    kc = jnp.stack([_chip(), lax.axis_index("c")]).astype(jnp.int32)
    return pl.pallas_call(
        body, name="chip_sum",
        grid_spec=pltpu.PrefetchScalarGridSpec(
            num_scalar_prefetch=1, grid=(nb,),
            in_specs=[pl.BlockSpec((1, tr, w), lambda i, kc_ref: (kc_ref[0], i, 0)),
                      pl.BlockSpec((N_CHIPS, tr, w), lambda i, kc_ref: (0, i, 0))],
            out_specs=pl.BlockSpec((tr, w), lambda i, kc_ref: (kc_ref[1] * nb + i, 0))),
        out_shape=jax.ShapeDtypeStruct((2 * hf, w), F32), compiler_params=_cparams(("parallel",)))(kc, t, rb)


def reduced_to_sibling(g):
    nr, w = g.shape
    hf = nr // 2

    def body(g_ref, o_ref, ssem, rsem):
        x, y, c = _me()
        cp = _rcopy(_half(o_ref, c, hf), _half(o_ref, c, hf), ssem, rsem, (x, y, 1 - c))
        cp.start()
        _rcopy(_half(o_ref, c, hf), _half(o_ref, 1 - c, hf), ssem, rsem, (x, y, 1 - c)).wait_recv()
        cp.wait_send()

    return pl.pallas_call(
        body, name="reduced_to_sibling", in_specs=[ANY], out_specs=ANY, input_output_aliases={0: 0},
        out_shape=jax.ShapeDtypeStruct((nr, w), g.dtype),
        scratch_shapes=[pltpu.SemaphoreType.DMA(()), pltpu.SemaphoreType.DMA(())])(g)


def small_allgather(g, row0, nrs):
    w = g.shape[1]

    def body(g_ref, o_ref, ssem, rsem):
        x, y, c = _me()
        k = 2 * x + y
        chips = _other_chips(x, y)
        src = g_ref.at[pl.ds(row0, nrs), :]
        sends = [_rcopy(src, o_ref.at[k], ssem.at[j], rsem.at[j], (cx, cy, c)) for j, (cx, cy) in enumerate(chips)]
        for cp in sends:
            cp.start()
        for j, (cx, cy) in enumerate(chips):
            _rcopy(src, o_ref.at[2 * cx + cy], ssem.at[j], rsem.at[j], (cx, cy, c)).wait_recv()
        for cp in sends:
            cp.wait_send()

    out = pl.pallas_call(
        body, name="small_allgather", in_specs=[ANY], out_specs=ANY,
        out_shape=jax.ShapeDtypeStruct((N_CHIPS, nrs, w), g.dtype),
        scratch_shapes=[pltpu.SemaphoreType.DMA((3,)), pltpu.SemaphoreType.DMA((3,))])(g)
    return lax.dynamic_update_slice(out, g[row0:row0 + nrs][None], (_chip(), 0, 0))


def adamw(w, g, m, v, name):
    r, wd = w.shape
    tr = _pick_rows(r)
    bc1 = 1.0 - ADAM_B1 ** ADAM_STEP
    bc2 = 1.0 - ADAM_B2 ** ADAM_STEP

    def body(w_ref, g_ref, m_ref, v_ref, d_ref, nm_ref, nv_ref):
        gv = g_ref[...]
        nm = ADAM_B1 * m_ref[...] + (1.0 - ADAM_B1) * gv
        nv = ADAM_B2 * v_ref[...] + (1.0 - ADAM_B2) * (gv * gv)
        nm_ref[...] = nm
        nv_ref[...] = nv
        d_ref[...] = -ADAM_LR * ((nm / bc1) / (jnp.sqrt(nv / bc2) + ADAM_EPS) + ADAM_WD * w_ref[...])

    spec = pl.BlockSpec((tr, wd), lambda i: (i, 0))
    st = jax.ShapeDtypeStruct((r, wd), F32)
    return pl.pallas_call(body, name=name, grid=(r // tr,), in_specs=[spec] * 4, out_specs=[spec] * 3,
                          out_shape=[st, st, st], compiler_params=_cparams(("parallel",)))(w, g, m, v)


LAYER_KINDS = ("gmlp", "s5", "mla", "gmlp")
PARAMS = {
    "gmlp": ("norm_g", "w_in", "ln_g", "ln_b", "w_s", "b_s", "w_out"),
    "s5": ("norm_g", "w_in", "a_re", "a_im", "log_step", "b_re", "b_im", "c_re", "c_im", "d_skip", "w_glu", "b_glu", "w_out"),
    "mla": ("norm_g", "w_in", "q_norm_g", "w_uq", "kv_norm_g", "w_ukv", "w_out"),
}
COL_SHARDED = ("w_in", "w_uq", "w_ukv")
ROW_SHARDED = ("w_out", "w_glu")
WEIGHT_NAMES = [("l%d_" % i) + n for i, kind in enumerate(LAYER_KINDS) for n in PARAMS[kind]] + ["final_norm_g"]


def _is_big(name):
    return name.split("_", 1)[1] in COL_SHARDED + ROW_SHARDED


BIG = [n for n in WEIGHT_NAMES if _is_big(n)]
SMALL = [n for n in WEIGHT_NAMES if not _is_big(n)]


def _pack_rows(blocks):
    return jnp.concatenate([b.reshape(-1, PACK_W) for b in blocks], axis=0)


def _shard_major(name, full):
    r, c = full.shape
    if name.split("_", 1)[1] in COL_SHARDED:
        t = full.reshape(r, N_CHIPS, c // N_CHIPS).transpose(1, 0, 2)
    else:
        t = full.reshape(N_CHIPS, r // N_CHIPS, c)
    return t.reshape(N_CHIPS, -1, PACK_W)


def _from_shard_major(name, t, block_shape):
    r, c = block_shape
    if name.split("_", 1)[1] in COL_SHARDED:
        return t.reshape(N_CHIPS, r, c).transpose(1, 0, 2).reshape(r, N_CHIPS * c)
    return t.reshape(N_CHIPS * r, c)


def _small_pack(arrs, total_padded):
    flat = jnp.concatenate([a.reshape(-1) for a in arrs])
    return jnp.pad(flat, (0, total_padded - flat.shape[0]))


def kernel(x, positions, l0_norm_g, l0_w_in, l0_ln_g, l0_ln_b, l0_w_s, l0_b_s, l0_w_out, l1_norm_g, l1_w_in, l1_a_re, l1_a_im, l1_log_step, l1_b_re, l1_b_im, l1_c_re, l1_c_im, l1_d_skip, l1_w_glu, l1_b_glu, l1_w_out, l2_norm_g, l2_w_in, l2_q_norm_g, l2_w_uq, l2_kv_norm_g, l2_w_ukv, l2_w_out, l3_norm_g, l3_w_in, l3_ln_g, l3_ln_b, l3_w_s, l3_b_s, l3_w_out, final_norm_g, loss_target, m_l0_norm_g, m_l0_w_in, m_l0_ln_g, m_l0_ln_b, m_l0_w_s, m_l0_b_s, m_l0_w_out, m_l1_norm_g, m_l1_w_in, m_l1_a_re, m_l1_a_im, m_l1_log_step, m_l1_b_re, m_l1_b_im, m_l1_c_re, m_l1_c_im, m_l1_d_skip, m_l1_w_glu, m_l1_b_glu, m_l1_w_out, m_l2_norm_g, m_l2_w_in, m_l2_q_norm_g, m_l2_w_uq, m_l2_kv_norm_g, m_l2_w_ukv, m_l2_w_out, m_l3_norm_g, m_l3_w_in, m_l3_ln_g, m_l3_ln_b, m_l3_w_s, m_l3_b_s, m_l3_w_out, m_final_norm_g, v_l0_norm_g, v_l0_w_in, v_l0_ln_g, v_l0_ln_b, v_l0_w_s, v_l0_b_s, v_l0_w_out, v_l1_norm_g, v_l1_w_in, v_l1_a_re, v_l1_a_im, v_l1_log_step, v_l1_b_re, v_l1_b_im, v_l1_c_re, v_l1_c_im, v_l1_d_skip, v_l1_w_glu, v_l1_b_glu, v_l1_w_out, v_l2_norm_g, v_l2_w_in, v_l2_q_norm_g, v_l2_w_uq, v_l2_kv_norm_g, v_l2_w_ukv, v_l2_w_out, v_l3_norm_g, v_l3_w_in, v_l3_ln_g, v_l3_ln_b, v_l3_w_s, v_l3_b_s, v_l3_w_out, v_final_norm_g):
    args = locals()
    w = {n: args[n] for n in WEIGHT_NAMES}
    mom_m = {n: args["m_" + n] for n in WEIGHT_NAMES}
    mom_v = {n: args["v_" + n] for n in WEIGHT_NAMES}
    h0 = x[0]
    target = loss_target[0]
    pos = positions.reshape(-1, 1)

    big_rows = [w[n].size // PACK_W for n in BIG]
    nrb = sum(big_rows)
    w_big = _pack_rows([w[n] for n in BIG])
    gathered = weights_allgather(jnp.pad(w_big.astype(BF16), ((0, -nrb % 32), (0, 0))))
    full = {}
    r0 = 0
    for n, nr in zip(BIG, big_rows):
        full[n] = _from_shard_major(n, gathered[:, r0:r0 + nr, :], w[n].shape)
        r0 += nr

    def layer_params(i):
        pre = "l%d_" % i
        p = {k[len(pre):]: v for k, v in w.items() if k.startswith(pre)}
        wf = {k[len(pre):]: v for k, v in full.items() if k.startswith(pre)}
        return p, wf

    cos, sins = rope_tables(pos)
    h = h0
    saved = []
    for i, kind in enumerate(LAYER_KINDS):
        p, wf = layer_params(i)
        tag = "l%d" % i
        if kind == "gmlp":
            h, s = gmlp_layer_fwd(h, p, wf, tag)
        elif kind == "s5":
            h, s = s5_layer_fwd(h, p, wf, tag)
        else:
            h, s = mla_layer_fwd(h, p, wf, cos, sins, tag)
        saved.append(s)
    loss_part, dh, g_final = loss_head(h, final_norm_g, target)
    loss = lax.psum(loss_part[0, 0], ("x", "y", "c"))

    grads = {"final_norm_g": g_final.reshape(-1)}
    for i in reversed(range(len(LAYER_KINDS))):
        kind = LAYER_KINDS[i]
        p, wf = layer_params(i)
        tag = "l%d" % i
        if kind == "gmlp":
            dh, g = gmlp_layer_bwd(dh, saved[i], p, wf, tag)
        elif kind == "s5":
            dh, g = s5_layer_bwd(dh, saved[i], p, wf, tag)
        else:
            dh, g = mla_layer_bwd(dh, saved[i], p, wf, cos, sins, tag)
        for k, val in g.items():
            grads["l%d_%s" % (i, k)] = val
    grad_x = dh[None]

    n_small = sum(w[n].size for n in SMALL)
    piece = N_CHIPS * 2 * 8 * PACK_W
    n_small_pad = -(-n_small // piece) * piece
    nrs = n_small_pad // N_CHIPS // PACK_W
    packed = jnp.concatenate(
        [_shard_major(n, grads[n]) for n in BIG]
        + [_small_pack([grads[n] for n in SMALL], n_small_pad).reshape(N_CHIPS, nrs, PACK_W)], axis=1)
    from_sibling = grads_to_sibling(packed)
    chip_part = pair_sum(packed, from_sibling)
    from_chips = grads_across_chips(chip_part)
    reduced = reduced_to_sibling(chip_sum(chip_part, from_chips))
    small_all = small_allgather(reduced, nrb, nrs)

    g_big = reduced[:nrb]
    d_big, nm_big, nv_big = adamw(w_big, g_big, _pack_rows([mom_m[n] for n in BIG]),
                                  _pack_rows([mom_v[n] for n in BIG]), "adamw_big")
    g_small = small_all.reshape(-1, PACK_W)
    sp = lambda d: _small_pack([d[n] for n in SMALL], n_small_pad).reshape(-1, PACK_W)
    d_small, nm_small, nv_small = adamw(sp(w), g_small, sp(mom_m), sp(mom_v), "adamw_small")

    def unpack(big_buf, small_buf):
        out = {}
        r = 0
        for n, nr in zip(BIG, big_rows):
            out[n] = big_buf[r:r + nr].reshape(w[n].shape)
            r += nr
        flat = small_buf.reshape(-1)
        o = 0
        for n in SMALL:
            out[n] = flat[o:o + w[n].size].reshape(w[n].shape)
            o += w[n].size
        return out

    g_out = unpack(g_big, g_small)
    d_out = unpack(d_big, d_small)
    nm_out = unpack(nm_big, nm_small)
    nv_out = unpack(nv_big, nv_small)
    return (loss, grad_x, *[g_out[n] for n in WEIGHT_NAMES], *[d_out[n] for n in WEIGHT_NAMES],
            *[nm_out[n] for n in WEIGHT_NAMES], *[nv_out[n] for n in WEIGHT_NAMES])
```

```python
import functools
import math

import jax
import jax.numpy as jnp
import numpy as np
from jax import lax
from jax.experimental import pallas as pl
from jax.experimental.pallas import tpu as pltpu

F32 = jnp.float32
BF16 = jnp.bfloat16
MESH = pl.DeviceIdType.MESH
VMEM_LIMIT_BYTES = 56 * 1024 * 1024
LANES = 128
PACK_W = 1024
ROW_TILE = 256
MM_BLOCK_BYTES = 6 * 1024 * 1024

NORM_EPS = 1e-6
N_CHIPS = 4
GMLP_CHUNK = 128
GMLP_GROUPS = 8
S5_GROUPS = 128
S5_GROUP = 16
S5_STATE = 64
S5_SB = 16
S5_SEG = 8
MLA_HEADS = 16
MLA_NOPE = 128
MLA_ROPE = 64
MLA_Q_RANK = 384
MLA_KV_RANK = 128
MLA_SCALE = (MLA_NOPE + MLA_ROPE) ** -0.5
ROPE_THETA = 10000.0
NEG_INF = -1e30
ADAM_LR, ADAM_B1, ADAM_B2, ADAM_EPS, ADAM_WD, ADAM_STEP = 0.001, 0.9, 0.999, 1e-08, 0.01, 10

DN_NN = (((1,), (0,)), ((), ()))
DN_NT = (((1,), (1,)), ((), ()))
DN_TN = (((0,), (0,)), ((), ()))


def _cparams(sem):
    return pltpu.CompilerParams(dimension_semantics=sem, vmem_limit_bytes=VMEM_LIMIT_BYTES)


def _pick(n, cands=(512, 384, 256, 128)):
    for c in cands:
        if n % c == 0:
            return c
    return n


def _pick_rows(r, cap=512):
    return max(t for t in range(8, cap + 1, 8) if r % t == 0)


def _dot(a, b, dn):
    return lax.dot_general(a.astype(BF16), b.astype(BF16), dn, preferred_element_type=F32)


def _sigmoid(x):
    return 1.0 / (1.0 + jnp.exp(-x))


def _gelu(x):
    c = math.sqrt(2.0 / math.pi)
    t = jnp.tanh(c * (x + 0.044715 * x * x * x))
    return 0.5 * x * (1.0 + t)


def _gelu_grad(x):
    c = math.sqrt(2.0 / math.pi)
    t = jnp.tanh(c * (x + 0.044715 * x * x * x))
    return 0.5 * (1.0 + t) + 0.5 * x * (1.0 - t * t) * c * (1.0 + 3.0 * 0.044715 * x * x)


def _silu(z):
    return z * _sigmoid(z)


def _silu_grad(z):
    s = _sigmoid(z)
    return s * (1.0 + z * (1.0 - s))


def matmul(a, b, mode, name, out_dtype=F32, add=None):
    if mode == "nn":
        (m, k), n = a.shape, b.shape[1]
    elif mode == "nt":
        (m, k), n = a.shape, b.shape[0]
    else:
        (k, m), n = a.shape, b.shape[1]
    tm = _pick(m, [t for t in (1024, 512, 384, 256, 128) if t * k * a.dtype.itemsize <= MM_BLOCK_BYTES])
    tn = _pick(n, [t for t in (512, 384, 256, 128) if t * k * b.dtype.itemsize <= MM_BLOCK_BYTES])
    dn = {"nn": DN_NN, "nt": DN_NT, "tn": DN_TN}[mode]

    def body(*refs):
        if add is None:
            a_ref, b_ref, o_ref = refs
        else:
            a_ref, b_ref, add_ref, o_ref = refs
        r = _dot(a_ref[...], b_ref[...], dn)
        if add is not None:
            r = r + add_ref[...].astype(F32)
        o_ref[...] = r.astype(out_dtype)

    a_spec = pl.BlockSpec((k, tm), lambda i, j: (0, i)) if mode == "tn" else pl.BlockSpec((tm, k), lambda i, j: (i, 0))
    b_spec = pl.BlockSpec((tn, k), lambda i, j: (j, 0)) if mode == "nt" else pl.BlockSpec((k, tn), lambda i, j: (0, j))
    o_spec = pl.BlockSpec((tm, tn), lambda i, j: (i, j))
    in_specs = [a_spec, b_spec] + ([o_spec] if add is not None else [])
    args = (a, b) + ((add,) if add is not None else ())
    return pl.pallas_call(
        body, name=name, grid=(m // tm, n // tn), in_specs=in_specs, out_specs=o_spec,
        out_shape=jax.ShapeDtypeStruct((m, n), out_dtype),
        compiler_params=_cparams(("parallel", "arbitrary")))(*args)


def _rows(tl, w, col=0):
    return pl.BlockSpec((tl, w), lambda i: (i, col))


def _full(shape):
    nd = len(shape)
    return pl.BlockSpec(tuple(shape), lambda i: (0,) * nd)


def _rowcall(body, name, n_steps, in_specs, out_specs, out_shape, scratch=()):
    return pl.pallas_call(
        body, name=name, grid=(n_steps,), in_specs=in_specs, out_specs=out_specs, out_shape=out_shape,
        scratch_shapes=list(scratch), compiler_params=_cparams(("arbitrary",)))


def _acc(ref, val, i):
    @pl.when(i == 0)
    def _():
        ref[...] = val

    @pl.when(i != 0)
    def _():
        ref[...] += val


def rms_fwd(h, g, name):
    l, d = h.shape
    tl = ROW_TILE

    def body(h_ref, g_ref, o_ref):
        x = h_ref[...]
        r = lax.rsqrt(jnp.mean(x * x, axis=-1, keepdims=True) + NORM_EPS)
        o_ref[...] = (x * r * g_ref[...]).astype(BF16)

    return _rowcall(body, name, l // tl, [_rows(tl, d), _full((1, d))], _rows(tl, d),
                    jax.ShapeDtypeStruct((l, d), BF16))(h, g.reshape(1, d))


def rms_bwd(h, g, dhn, dh_in, name):
    l, d = h.shape
    tl = ROW_TILE

    def body(h_ref, g_ref, dhn_ref, dhi_ref, dh_ref, dg_ref):
        i = pl.program_id(0)
        x = h_ref[...]
        r = lax.rsqrt(jnp.mean(x * x, axis=-1, keepdims=True) + NORM_EPS)
        xhat = x * r
        dy = dhn_ref[...]
        dxh = dy * g_ref[...]
        dx = r * (dxh - xhat * jnp.mean(dxh * xhat, axis=-1, keepdims=True))
        dh_ref[...] = dhi_ref[...] + dx
        _acc(dg_ref, jnp.sum(dy * xhat, axis=0, keepdims=True), i)

    return _rowcall(body, name, l // tl, [_rows(tl, d), _full((1, d)), _rows(tl, d), _rows(tl, d)],
                    [_rows(tl, d), _full((1, d))],
                    [jax.ShapeDtypeStruct((l, d), F32), jax.ShapeDtypeStruct((1, d), F32)])(h, g.reshape(1, d), dhn, dh_in)


def loss_head(h, g, target):
    l, d = h.shape
    tl = ROW_TILE

    def body(h_ref, g_ref, t_ref, loss_ref, dh_ref, dg_ref):
        i = pl.program_id(0)
        x = h_ref[...]
        gg = g_ref[...]
        r = lax.rsqrt(jnp.mean(x * x, axis=-1, keepdims=True) + NORM_EPS)
        xhat = x * r
        err = xhat * gg - t_ref[...]
        part = 0.5 * jnp.sum(jnp.mean(err * err, axis=-1, keepdims=True), axis=0, keepdims=True)
        _acc(loss_ref, part, i)
        dy = err * (1.0 / d)
        dxh = dy * gg
        dh_ref[...] = r * (dxh - xhat * jnp.mean(dxh * xhat, axis=-1, keepdims=True))
        _acc(dg_ref, jnp.sum(dy * xhat, axis=0, keepdims=True), i)

    return _rowcall(body, "loss_head", l // tl, [_rows(tl, d), _full((1, d)), _rows(tl, d)],
                    [_full((1, 1)), _rows(tl, d), _full((1, d))],
                    [jax.ShapeDtypeStruct((1, 1), F32), jax.ShapeDtypeStruct((l, d), F32),
                     jax.ShapeDtypeStruct((1, d), F32)])(h, g.reshape(1, d), target)


def _gmlp_common(a_ref, lng_ref, lnb_ref):
    di = lng_ref.shape[1]
    u_pre = a_ref[:, 0:di]
    v_pre = a_ref[:, di:2 * di]
    z = a_ref[:, 2 * di:3 * di]
    vg = _gelu(v_pre)
    mu = jnp.mean(vg, axis=-1, keepdims=True)
    xc = vg - mu
    rstd = lax.rsqrt(jnp.mean(xc * xc, axis=-1, keepdims=True) + NORM_EPS)
    vhat = xc * rstd
    vn = vhat * lng_ref[...] + lnb_ref[...]
    return u_pre, v_pre, z, vhat, rstd, vn


def _tril(w):
    r = lax.broadcasted_iota(jnp.int32, w.shape, 0)
    c = lax.broadcasted_iota(jnp.int32, w.shape, 1)
    return jnp.where(c <= r, w, 0.0)


def gmlp_gate_fwd(a, ln_g, ln_b, w_s, b_s, name):
    l, w3 = a.shape
    di = w3 // 3
    dg = di // GMLP_GROUPS
    tl = GMLP_CHUNK

    def body(a_ref, lng_ref, lnb_ref, ws_ref, bs_ref, m_ref):
        u_pre, _, z, _, _, vn = _gmlp_common(a_ref, lng_ref, lnb_ref)
        gate = _gelu(u_pre) * _silu(z)
        for g in range(GMLP_GROUPS):
            sl = slice(g * dg, (g + 1) * dg)
            s = _dot(_tril(ws_ref[g]), vn[:, sl], DN_NN) + bs_ref[g]
            m_ref[:, sl] = (gate[:, sl] * s).astype(BF16)

    return _rowcall(body, name, l // tl,
                    [_rows(tl, w3), _full((1, di)), _full((1, di)), _full(w_s.shape), _full((GMLP_GROUPS, tl, 1))],
                    _rows(tl, di), jax.ShapeDtypeStruct((l, di), BF16))(
        a, ln_g.reshape(1, di), ln_b.reshape(1, di), w_s, b_s.reshape(GMLP_GROUPS, tl, 1))


def gmlp_gate_bwd(a, dm, ln_g, ln_b, w_s, b_s, name):
    l, w3 = a.shape
    di = w3 // 3
    dg = di // GMLP_GROUPS
    tl = GMLP_CHUNK

    def body(a_ref, dm_ref, lng_ref, lnb_ref, ws_ref, bs_ref, da_ref, dlg_ref, dlb_ref, dws_ref, dbs_ref, dvn_ref):
        i = pl.program_id(0)
        u_pre, v_pre, z, vhat, rstd, vn = _gmlp_common(a_ref, lng_ref, lnb_ref)
        dm_v = dm_ref[...]
        u = _gelu(u_pre)
        sz = _silu(z)
        for g in range(GMLP_GROUPS):
            sl = slice(g * dg, (g + 1) * dg)
            wt = _tril(ws_ref[g])
            vn_g = vn[:, sl]
            s = _dot(wt, vn_g, DN_NN) + bs_ref[g]
            dmg = dm_v[:, sl]
            ds = dmg * u[:, sl] * sz[:, sl]
            da_ref[:, sl] = (dmg * s * sz[:, sl] * _gelu_grad(u_pre[:, sl])).astype(BF16)
            da_ref[:, 2 * di + g * dg:2 * di + (g + 1) * dg] = (
                dmg * u[:, sl] * s * _silu_grad(z[:, sl])).astype(BF16)
            dvn_ref[:, sl] = _dot(wt, ds, DN_TN)
            dw = _tril(_dot(ds, vn_g, DN_NT))
            db = jnp.sum(ds, axis=1, keepdims=True)

            @pl.when(i == 0)
            def _():
                dws_ref[g] = dw
                dbs_ref[g] = db

            @pl.when(i != 0)
            def _():
                dws_ref[g] += dw
                dbs_ref[g] += db

        dvn = dvn_ref[...]
        dxh = dvn * lng_ref[...]
        dvg = rstd * (dxh - jnp.mean(dxh, axis=-1, keepdims=True) - vhat * jnp.mean(dxh * vhat, axis=-1, keepdims=True))
        da_ref[:, di:2 * di] = (dvg * _gelu_grad(v_pre)).astype(BF16)
        _acc(dlg_ref, jnp.sum(dvn * vhat, axis=0, keepdims=True), i)
        _acc(dlb_ref, jnp.sum(dvn, axis=0, keepdims=True), i)

    outs = _rowcall(
        body, name, l // tl,
        [_rows(tl, w3), _rows(tl, di), _full((1, di)), _full((1, di)), _full(w_s.shape), _full((GMLP_GROUPS, tl, 1))],
        [_rows(tl, w3), _full((1, di)), _full((1, di)), _full(w_s.shape), _full((GMLP_GROUPS, tl, 1))],
        [jax.ShapeDtypeStruct((l, w3), BF16), jax.ShapeDtypeStruct((1, di), F32), jax.ShapeDtypeStruct((1, di), F32),
         jax.ShapeDtypeStruct(w_s.shape, F32), jax.ShapeDtypeStruct((GMLP_GROUPS, tl, 1), F32)],
        scratch=[pltpu.VMEM((tl, di), F32)])(
        a, dm, ln_g.reshape(1, di), ln_b.reshape(1, di), w_s, b_s.reshape(GMLP_GROUPS, tl, 1))
    return outs


def gmlp_layer_fwd(h, p, wf, tag):
    hn = rms_fwd(h, p["norm_g"], tag + "_rms")
    a = matmul(hn, wf["w_in"], "nn", tag + "_mm_in")
    m = gmlp_gate_fwd(a, p["ln_g"], p["ln_b"], p["w_s"], p["b_s"], tag + "_gate")
    h_out = matmul(m, wf["w_out"], "nn", tag + "_mm_out", add=h)
    return h_out, (h, hn, a, m)


def gmlp_layer_bwd(dh_out, saved, p, wf, tag):
    h, hn, a, m = saved
    dm = matmul(dh_out, wf["w_out"], "nt", tag + "_mm_dm")
    g_w_out = matmul(m, dh_out, "tn", tag + "_mm_gwout")
    da, dlg, dlb, dws, dbs = gmlp_gate_bwd(a, dm, p["ln_g"], p["ln_b"], p["w_s"], p["b_s"], tag + "_gate_bwd")
    dhn = matmul(da, wf["w_in"], "nt", tag + "_mm_dhn")
    g_w_in = matmul(hn, da, "tn", tag + "_mm_gwin")
    dh, dng = rms_bwd(h, p["norm_g"], dhn, dh_out, tag + "_rms_bwd")
    grads = {"norm_g": dng.reshape(-1), "w_in": g_w_in, "ln_g": dlg.reshape(-1), "ln_b": dlb.reshape(-1),
             "w_s": dws, "b_s": dbs.reshape(GMLP_GROUPS, GMLP_CHUNK), "w_out": g_w_out}
    return dh, grads


def _cmul(ar, ai, br, bi):
    return ar * br - ai * bi, ar * bi + ai * br


S5_PG = 16


def _gblock(tail):
    return pl.BlockSpec((S5_PG,) + tuple(tail), lambda i: (i, 0, 0))


def s5_params_fwd(a_re, a_im, log_step, b_re, b_im):
    g, p, hh = b_re.shape

    def body(ar_ref, ai_ref, ls_ref, br_ref, bi_ref, lr_ref, li_ref, bbr_ref, bbi_ref):
        ar, ai = ar_ref[...], ai_ref[...]
        step = jnp.exp(ls_ref[...])
        mag = jnp.exp(ar * step)
        lr, li = mag * jnp.cos(ai * step), mag * jnp.sin(ai * step)
        den = 1.0 / (ar * ar + ai * ai)
        fr, fi = _cmul(lr - 1.0, li, ar * den, -ai * den)
        lr_ref[...] = lr
        li_ref[...] = li
        bbr, bbi = _cmul(fr, fi, br_ref[...], bi_ref[...])
        bbr_ref[...] = bbr
        bbi_ref[...] = bbi

    s1 = jax.ShapeDtypeStruct((g, p, 1), F32)
    s3 = jax.ShapeDtypeStruct((g, p, hh), F32)
    b1, b0, b3 = _gblock((p, 1)), _gblock((1, 1)), _gblock((p, hh))
    return pl.pallas_call(body, name="s5_params_fwd", grid=(g // S5_PG,), in_specs=[b1, b1, b0, b3, b3],
                          out_specs=[b1, b1, b3, b3], out_shape=[s1, s1, s3, s3],
                          compiler_params=_cparams(("parallel",)))(
        a_re.reshape(g, p, 1), a_im.reshape(g, p, 1), log_step.reshape(g, 1, 1), b_re, b_im)


def s5_params_bwd(a_re, a_im, log_step, b_re, b_im, dl_re, dl_im, dbb_re, dbb_im):
    g, p, hh = b_re.shape

    def body(ar_ref, ai_ref, ls_ref, br_ref, bi_ref, dlr_ref, dli_ref, dbr_ref, dbi_ref,
             gar_ref, gai_ref, gls_ref, gbr_ref, gbi_ref):
        ar, ai = ar_ref[...], ai_ref[...]
        step = jnp.exp(ls_ref[...])
        mag = jnp.exp(ar * step)
        lr, li = mag * jnp.cos(ai * step), mag * jnp.sin(ai * step)
        den = 1.0 / (ar * ar + ai * ai)
        ir, ii = ar * den, -ai * den
        fr, fi = _cmul(lr - 1.0, li, ir, ii)
        br, bi = br_ref[...], bi_ref[...]
        dbr, dbi = dbr_ref[...], dbi_ref[...]
        gbr, gbi = _cmul(fr, -fi, dbr, dbi)
        gbr_ref[...] = gbr
        gbi_ref[...] = gbi
        pr, pi = _cmul(br, -bi, dbr, dbi)
        gfr = jnp.sum(pr, axis=-1, keepdims=True)
        gfi = jnp.sum(pi, axis=-1, keepdims=True)
        t_r, t_i = _cmul(ir, -ii, gfr, gfi)
        glr, gli = dlr_ref[...] + t_r, dli_ref[...] + t_i
        c1r, c1i = _cmul(step * lr, -step * li, glr, gli)
        qr, qi = _cmul(fr, fi, ir, ii)
        c2r, c2i = _cmul(-qr, qi, gfr, gfi)
        gar_ref[...] = c1r + c2r
        gai_ref[...] = c1i + c2i
        wr, wi = _cmul(ar, ai, lr, li)
        sr, _ = _cmul(wr, -wi, glr, gli)
        gls_ref[...] = jnp.sum(sr, axis=1, keepdims=True) * step

    s1 = jax.ShapeDtypeStruct((g, p, 1), F32)
    s3 = jax.ShapeDtypeStruct((g, p, hh), F32)
    b1, b0, b3 = _gblock((p, 1)), _gblock((1, 1)), _gblock((p, hh))
    return pl.pallas_call(body, name="s5_params_bwd", grid=(g // S5_PG,),
                          in_specs=[b1, b1, b0, b3, b3, b1, b1, b3, b3], out_specs=[b1, b1, b0, b3, b3],
                          out_shape=[s1, s1, jax.ShapeDtypeStruct((g, 1, 1), F32), s3, s3],
                          compiler_params=_cparams(("parallel",)))(
        a_re.reshape(g, p, 1), a_im.reshape(g, p, 1), log_step.reshape(g, 1, 1), b_re, b_im,
        dl_re, dl_im, dbb_re, dbb_im)


def _blockdiag(t):
    sb, n, r, c = t.shape
    eye = jnp.eye(n, dtype=bool)[None, :, None, :, None]
    full = jnp.where(eye, t[:, :, :, None, :], jnp.zeros((), t.dtype))
    return full.reshape(sb, n * r, n * c)


def _blockdiag_extract(m, r, c):
    sb = m.shape[0]
    n = m.shape[1] // r
    m5 = m.reshape(sb, n, r, n, c)
    return jnp.stack([m5[:, i, :, i, :] for i in range(n)], axis=1)


S5_TB = 64
S5_UNROLL = 8


def s5_scan_fwd(a_p, lam_re, lam_im, wb_re, wb_im, wc_re, wc_im, d_skip, x0_re, x0_im, name):
    l = a_p.shape[0]
    di = d_skip.shape[1]
    rows = S5_SEG * S5_TB
    nb = l // rows
    ns = wb_re.shape[2]

    def body(u_ref, lr_ref, li_ref, wbr_ref, wbi_ref, wcr_ref, wci_ref, ds_ref, x0r_ref, x0i_ref,
             y_ref, ckr_ref, cki_ref, xer_ref, xei_ref, bur, bui, xr_s, xi_s):
        b = pl.program_id(1)

        @pl.when(b == 0)
        def _():
            xr_s[...] = x0r_ref[0]
            xi_s[...] = x0i_ref[0]

        ckr_ref[0, 0] = xr_s[...]
        cki_ref[0, 0] = xi_s[...]
        u = u_ref[...]
        bur[...] = _dot(u, wbr_ref[0], DN_NN)
        bui[...] = _dot(u, wbi_ref[0], DN_NN)
        lr = jnp.broadcast_to(lr_ref[0], (S5_SEG, ns))
        li = jnp.broadcast_to(li_ref[0], (S5_SEG, ns))

        def step(t, carry):
            xr, xi = carry
            sl = pl.ds(pl.multiple_of(t * S5_SEG, S5_SEG), S5_SEG)
            nr = lr * xr - li * xi + bur[sl, :]
            ni = lr * xi + li * xr + bui[sl, :]
            bur[sl, :] = nr
            bui[sl, :] = ni
            return nr, ni

        xr, xi = lax.fori_loop(0, S5_TB, step, (xr_s[...], xi_s[...]), unroll=S5_UNROLL)
        xr_s[...] = xr
        xi_s[...] = xi
        xer_ref[0] = xr
        xei_ref[0] = xi
        y_ref[...] = _dot(bur[...], wcr_ref[0], DN_NN) - _dot(bui[...], wci_ref[0], DN_NN) + ds_ref[...] * u

    sb3 = lambda s, b: (s, 0, 0)
    st = jax.ShapeDtypeStruct
    return pl.pallas_call(
        body, name=name, grid=(S5_SB, nb),
        in_specs=[pl.BlockSpec((rows, LANES), lambda s, b: (b, s)),
                  pl.BlockSpec((1, 1, ns), sb3), pl.BlockSpec((1, 1, ns), sb3),
                  pl.BlockSpec((1, LANES, ns), sb3), pl.BlockSpec((1, LANES, ns), sb3),
                  pl.BlockSpec((1, ns, LANES), sb3), pl.BlockSpec((1, ns, LANES), sb3),
                  pl.BlockSpec((1, LANES), lambda s, b: (0, s)),
                  pl.BlockSpec((1, S5_SEG, ns), sb3), pl.BlockSpec((1, S5_SEG, ns), sb3)],
        out_specs=[pl.BlockSpec((rows, LANES), lambda s, b: (b, s)),
                   pl.BlockSpec((1, 1, S5_SEG, ns), lambda s, b: (s, b, 0, 0)),
                   pl.BlockSpec((1, 1, S5_SEG, ns), lambda s, b: (s, b, 0, 0)),
                   pl.BlockSpec((1, S5_SEG, ns), sb3), pl.BlockSpec((1, S5_SEG, ns), sb3)],
        out_shape=[st((l, di), F32), st((S5_SB, nb, S5_SEG, ns), F32), st((S5_SB, nb, S5_SEG, ns), F32),
                   st((S5_SB, S5_SEG, ns), F32), st((S5_SB, S5_SEG, ns), F32)],
        scratch_shapes=[pltpu.VMEM((rows, ns), F32), pltpu.VMEM((rows, ns), F32),
                        pltpu.VMEM((S5_SEG, ns), F32), pltpu.VMEM((S5_SEG, ns), F32)],
        compiler_params=_cparams(("parallel", "arbitrary")))(
        a_p, lam_re, lam_im, wb_re, wb_im, wc_re, wc_im, d_skip, x0_re, x0_im)


def s5_ends(inp, lam_re, lam_im, w_re, w_im, adjoint, name):
    l = inp.shape[0]
    rows = S5_SEG * S5_TB
    nb = l // rows
    ns = lam_re.shape[2]

    def body(i_ref, lr_ref, li_ref, wr_ref, wi_ref, er_ref, ei_ref, pr_b, pi_b, xr_s, xi_s):
        b = pl.program_id(1)

        @pl.when(b == 0)
        def _():
            xr_s[...] = jnp.zeros_like(xr_s)
            xi_s[...] = jnp.zeros_like(xi_s)

        v = i_ref[...]
        lr = jnp.broadcast_to(lr_ref[0], (S5_SEG, ns))
        li = jnp.broadcast_to(li_ref[0], (S5_SEG, ns))
        if adjoint:
            pr_b[...] = _dot(v, wr_ref[0], DN_NT)
            pi_b[...] = -_dot(v, wi_ref[0], DN_NT)
            li = -li
        else:
            pr_b[...] = _dot(v, wr_ref[0], DN_NN)
            pi_b[...] = _dot(v, wi_ref[0], DN_NN)

        def step(k, carry):
            xr, xi = carry
            t = S5_TB - 1 - k if adjoint else k
            sl = pl.ds(pl.multiple_of(t * S5_SEG, S5_SEG), S5_SEG)
            return lr * xr - li * xi + pr_b[sl, :], lr * xi + li * xr + pi_b[sl, :]

        xr, xi = lax.fori_loop(0, S5_TB, step, (xr_s[...], xi_s[...]), unroll=S5_UNROLL)
        xr_s[...] = xr
        xi_s[...] = xi
        er_ref[0] = xr
        ei_ref[0] = xi

    sb3 = lambda s, b: (s, 0, 0)
    blk = (lambda s, b: (nb - 1 - b, s)) if adjoint else (lambda s, b: (b, s))
    wshape = (1, ns, LANES) if adjoint else (1, LANES, ns)
    st = jax.ShapeDtypeStruct((S5_SB, S5_SEG, ns), F32)
    return pl.pallas_call(
        body, name=name, grid=(S5_SB, nb),
        in_specs=[pl.BlockSpec((rows, LANES), blk), pl.BlockSpec((1, 1, ns), sb3), pl.BlockSpec((1, 1, ns), sb3),
                  pl.BlockSpec(wshape, sb3), pl.BlockSpec(wshape, sb3)],
        out_specs=[pl.BlockSpec((1, S5_SEG, ns), sb3), pl.BlockSpec((1, S5_SEG, ns), sb3)],
        out_shape=[st, st],
        scratch_shapes=[pltpu.VMEM((rows, ns), F32), pltpu.VMEM((rows, ns), F32),
                        pltpu.VMEM((S5_SEG, ns), F32), pltpu.VMEM((S5_SEG, ns), F32)],
        compiler_params=_cparams(("parallel", "arbitrary")))(inp, lam_re, lam_im, w_re, w_im)


def s5_scan_bwd(a_p, dy, lam_re, lam_im, wb_re, wb_im, wc_re, wc_im, d_skip, ck_re, ck_im, a0_re, a0_im, name):
    l = a_p.shape[0]
    di = d_skip.shape[1]
    rows = S5_SEG * S5_TB
    nb = l // rows
    ns = wb_re.shape[2]

    def body(u_ref, dy_ref, lr_ref, li_ref, wbr_ref, wbi_ref, wcr_ref, wci_ref, ds_ref, ckr_ref, cki_ref,
             a0r_ref, a0i_ref,
             du_ref, dwbr_ref, dwbi_ref, dwcr_ref, dwci_ref, dds_ref, dlr_ref, dli_ref, aer_ref, aei_ref,
             xr_b, xi_b, gr_b, gi_b, ar_s, ai_s):
        b = pl.program_id(1)

        @pl.when(b == 0)
        def _():
            ar_s[...] = a0r_ref[0]
            ai_s[...] = a0i_ref[0]

        u = u_ref[...]
        dyv = dy_ref[...]
        lr = jnp.broadcast_to(lr_ref[0], (S5_SEG, ns))
        li = jnp.broadcast_to(li_ref[0], (S5_SEG, ns))
        xr_b[...] = _dot(u, wbr_ref[0], DN_NN)
        xi_b[...] = _dot(u, wbi_ref[0], DN_NN)

        def fstep(t, carry):
            xr, xi = carry
            sl = pl.ds(pl.multiple_of(t * S5_SEG, S5_SEG), S5_SEG)
            nr = lr * xr - li * xi + xr_b[sl, :]
            ni = lr * xi + li * xr + xi_b[sl, :]
            xr_b[sl, :] = nr
            xi_b[sl, :] = ni
            return nr, ni

        x0r, x0i = ckr_ref[0, 0], cki_ref[0, 0]
        lax.fori_loop(0, S5_TB, fstep, (x0r, x0i), unroll=S5_UNROLL)
        dwcr = _dot(xr_b[...], dyv, DN_TN)
        dwci = -_dot(xi_b[...], dyv, DN_TN)
        gr_b[...] = _dot(dyv, wcr_ref[0], DN_NT)
        gi_b[...] = -_dot(dyv, wci_ref[0], DN_NT)

        def bstep(k, carry):
            ar, ai, dlr, dli = carry
            t = S5_TB - 1 - k
            sl = pl.ds(pl.multiple_of(t * S5_SEG, S5_SEG), S5_SEG)
            slp = pl.ds(pl.multiple_of(jnp.maximum(t - 1, 0) * S5_SEG, S5_SEG), S5_SEG)
            nr = gr_b[sl, :] + lr * ar + li * ai
            ni = gi_b[sl, :] + lr * ai - li * ar
            gr_b[sl, :] = nr
            gi_b[sl, :] = ni
            first = t == 0
            pr = jnp.where(first, x0r, xr_b[slp, :])
            pi = jnp.where(first, x0i, xi_b[slp, :])
            dlr = dlr + nr * pr + ni * pi
            dli = dli + ni * pr - nr * pi
            return nr, ni, dlr, dli

        zero = jnp.zeros((S5_SEG, ns), F32)
        ar, ai, dlr, dli = lax.fori_loop(0, S5_TB, bstep, (ar_s[...], ai_s[...], zero, zero), unroll=S5_UNROLL)
        ar_s[...] = ar
        ai_s[...] = ai
        aer_ref[0] = ar
        aei_ref[0] = ai
        dsk = ds_ref[...]
        du_ref[...] = (_dot(gr_b[...], wbr_ref[0], DN_NT) + _dot(gi_b[...], wbi_ref[0], DN_NT) + dsk * dyv).astype(BF16)
        dwbr = _dot(u, gr_b[...], DN_TN)
        dwbi = _dot(u, gi_b[...], DN_TN)
        dds = jnp.sum(dyv * u, axis=0, keepdims=True)

        @pl.when(b == 0)
        def _():
            dwbr_ref[0] = dwbr
            dwbi_ref[0] = dwbi
            dwcr_ref[0] = dwcr
            dwci_ref[0] = dwci
            dds_ref[...] = dds
            dlr_ref[0] = dlr
            dli_ref[0] = dli

        @pl.when(b != 0)
        def _():
            dwbr_ref[0] += dwbr
            dwbi_ref[0] += dwbi
            dwcr_ref[0] += dwcr
            dwci_ref[0] += dwci
            dds_ref[...] += dds
            dlr_ref[0] += dlr
            dli_ref[0] += dli

    sb3 = lambda s, b: (s, 0, 0)
    rev = lambda s, b: (nb - 1 - b, s)
    st = jax.ShapeDtypeStruct
    return pl.pallas_call(
        body, name=name, grid=(S5_SB, nb),
        in_specs=[pl.BlockSpec((rows, LANES), rev), pl.BlockSpec((rows, LANES), rev),
                  pl.BlockSpec((1, 1, ns), sb3), pl.BlockSpec((1, 1, ns), sb3),
                  pl.BlockSpec((1, LANES, ns), sb3), pl.BlockSpec((1, LANES, ns), sb3),
                  pl.BlockSpec((1, ns, LANES), sb3), pl.BlockSpec((1, ns, LANES), sb3),
                  pl.BlockSpec((1, LANES), lambda s, b: (0, s)),
                  pl.BlockSpec((1, 1, S5_SEG, ns), lambda s, b: (s, nb - 1 - b, 0, 0)),
                  pl.BlockSpec((1, 1, S5_SEG, ns), lambda s, b: (s, nb - 1 - b, 0, 0)),
                  pl.BlockSpec((1, S5_SEG, ns), sb3), pl.BlockSpec((1, S5_SEG, ns), sb3)],
        out_specs=[pl.BlockSpec((rows, LANES), rev),
                   pl.BlockSpec((1, LANES, ns), sb3), pl.BlockSpec((1, LANES, ns), sb3),
                   pl.BlockSpec((1, ns, LANES), sb3), pl.BlockSpec((1, ns, LANES), sb3),
                   pl.BlockSpec((1, LANES), lambda s, b: (0, s)),
                   pl.BlockSpec((1, S5_SEG, ns), sb3), pl.BlockSpec((1, S5_SEG, ns), sb3),
                   pl.BlockSpec((1, S5_SEG, ns), sb3), pl.BlockSpec((1, S5_SEG, ns), sb3)],
        out_shape=[st((l, di), BF16), st((S5_SB, LANES, ns), F32), st((S5_SB, LANES, ns), F32),
                   st((S5_SB, ns, LANES), F32), st((S5_SB, ns, LANES), F32), st((1, di), F32),
                   st((S5_SB, S5_SEG, ns), F32), st((S5_SB, S5_SEG, ns), F32),
                   st((S5_SB, S5_SEG, ns), F32), st((S5_SB, S5_SEG, ns), F32)],
        scratch_shapes=[pltpu.VMEM((rows, ns), F32), pltpu.VMEM((rows, ns), F32),
                        pltpu.VMEM((rows, ns), F32), pltpu.VMEM((rows, ns), F32),
                        pltpu.VMEM((S5_SEG, ns), F32), pltpu.VMEM((S5_SEG, ns), F32)],
        compiler_params=_cparams(("parallel", "arbitrary")))(
        a_p, dy, lam_re, lam_im, wb_re, wb_im, wc_re, wc_im, d_skip, ck_re, ck_im, a0_re, a0_im)


def s5_carry(e_re, e_im, lam_re, lam_im, seg_len, reverse, name):
    sb, seg, ns = e_re.shape

    def body(er_ref, ei_ref, lr_ref, li_ref, cr_ref, ci_ref):
        pr, pi = lr_ref[...], li_ref[...]
        if reverse:
            pi = -pi
        for _ in range(int(math.log2(seg_len))):
            pr, pi = _cmul(pr, pi, pr, pi)
        er, ei = er_ref[...], ei_ref[...]
        row = lax.broadcasted_iota(jnp.int32, (sb, seg, ns), 1)
        cr = jnp.zeros((sb, seg, ns), F32)
        ci = jnp.zeros((sb, seg, ns), F32)
        cur_r = jnp.zeros((sb, 1, ns), F32)
        cur_i = jnp.zeros((sb, 1, ns), F32)
        order = range(seg - 2, -1, -1) if reverse else range(1, seg)
        for s in order:
            src = s + 1 if reverse else s - 1
            mr, mi = _cmul(pr, pi, cur_r, cur_i)
            cur_r = jnp.sum(jnp.where(row == src, er, 0.0), axis=1, keepdims=True) + mr
            cur_i = jnp.sum(jnp.where(row == src, ei, 0.0), axis=1, keepdims=True) + mi
            cr = jnp.where(row == s, cur_r, cr)
            ci = jnp.where(row == s, cur_i, ci)
        cr_ref[...] = cr
        ci_ref[...] = ci

    st = jax.ShapeDtypeStruct((sb, seg, ns), F32)
    return pl.pallas_call(body, name=name, out_shape=[st, st],
                          compiler_params=pltpu.CompilerParams(vmem_limit_bytes=VMEM_LIMIT_BYTES))(e_re, e_im, lam_re, lam_im)


def s5_act(y, name):
    l, d = y.shape
    tl = ROW_TILE

    def body(y_ref, o_ref):
        o_ref[...] = _gelu(y_ref[...]).astype(BF16)

    return _rowcall(body, name, l // tl, [_rows(tl, d)], _rows(tl, d), jax.ShapeDtypeStruct((l, d), BF16))(y)


def s5_gate_fwd(y, t, b_glu, a_p, name):
    l, d = y.shape
    tl = ROW_TILE

    def body(y_ref, t_ref, b_ref, z_ref, m_ref):
        yg = _gelu(y_ref[...])
        m_ref[...] = (yg * _sigmoid(t_ref[...] + b_ref[...]) * _silu(z_ref[...])).astype(BF16)

    return _rowcall(body, name, l // tl, [_rows(tl, d), _rows(tl, d), _full((1, d)), _rows(tl, d, 1)], _rows(tl, d),
                    jax.ShapeDtypeStruct((l, d), BF16))(y, t, b_glu.reshape(1, d), a_p)


def s5_gate_bwd(dm, y, t, b_glu, a_p, name):
    l, d = y.shape
    tl = ROW_TILE

    def body(dm_ref, y_ref, t_ref, b_ref, z_ref, dt_ref, dyg_ref, dz_ref, db_ref):
        i = pl.program_id(0)
        dmv = dm_ref[...]
        z = z_ref[...]
        yg = _gelu(y_ref[...])
        sg = _sigmoid(t_ref[...] + b_ref[...])
        y2 = yg * sg
        dy2 = dmv * _silu(z)
        dz_ref[...] = (dmv * y2 * _silu_grad(z)).astype(BF16)
        dyg_ref[...] = dy2 * sg
        dt = dy2 * yg * sg * (1.0 - sg)
        dt_ref[...] = dt.astype(BF16)
        _acc(db_ref, jnp.sum(dt, axis=0, keepdims=True), i)

    st = jax.ShapeDtypeStruct
    return _rowcall(body, name, l // tl, [_rows(tl, d), _rows(tl, d), _rows(tl, d), _full((1, d)), _rows(tl, d, 1)],
                    [_rows(tl, d), _rows(tl, d), _rows(tl, d), _full((1, d))],
                    [st((l, d), BF16), st((l, d), F32), st((l, d), BF16), st((1, d), F32)])(
        dm, y, t, b_glu.reshape(1, d), a_p)


def s5_act_bwd(y, dyg_a, dyg_b, name):
    l, d = y.shape
    tl = ROW_TILE

    def body(y_ref, a_ref, b_ref, o_ref):
        o_ref[...] = (a_ref[...] + b_ref[...]) * _gelu_grad(y_ref[...])

    return _rowcall(body, name, l // tl, [_rows(tl, d)] * 3, _rows(tl, d), jax.ShapeDtypeStruct((l, d), F32))(y, dyg_a, dyg_b)


def _seg_perm(t):
    l, d = t.shape
    return t.reshape(S5_SEG, l // S5_SEG, d).transpose(1, 0, 2).reshape(l, d)


def _seg_unperm(t):
    l, d = t.shape
    return t.reshape(l // S5_SEG, S5_SEG, d).transpose(1, 0, 2).reshape(l, d)


def _s5_weights(p):
    lr, li, bbr, bbi = s5_params_fwd(p["a_re"], p["a_im"], p["log_step"], p["b_re"], p["b_im"])
    ns = 8 * S5_STATE
    lam_re = lr.reshape(S5_SB, 1, ns)
    lam_im = li.reshape(S5_SB, 1, ns)
    to_bd = lambda t: _blockdiag(t.reshape(S5_SB, 8, t.shape[1], t.shape[2]))
    wb_re = to_bd(bbr.transpose(0, 2, 1)).astype(BF16)
    wb_im = to_bd(bbi.transpose(0, 2, 1)).astype(BF16)
    wc_re = to_bd(p["c_re"].transpose(0, 2, 1)).astype(BF16)
    wc_im = to_bd(p["c_im"].transpose(0, 2, 1)).astype(BF16)
    return lam_re, lam_im, wb_re, wb_im, wc_re, wc_im


def s5_layer_fwd(h, p, wf, tag):
    l = h.shape[0]
    di = p["d_skip"].shape[0]
    hn = rms_fwd(h, p["norm_g"], tag + "_rms")
    hn_p = _seg_perm(hn)
    a_p = matmul(hn_p, wf["w_in"], "nn", tag + "_mm_in")
    sw = _s5_weights(p)
    dsk = p["d_skip"].reshape(1, di)
    e_re, e_im = s5_ends(a_p, sw[0], sw[1], sw[2], sw[3], False, tag + "_scan_ends")
    c_re, c_im = s5_carry(e_re, e_im, sw[0], sw[1], l // S5_SEG, False, tag + "_carry")
    y, ck_re, ck_im, _, _ = s5_scan_fwd(a_p, *sw, dsk, c_re, c_im, tag + "_scan")
    yg = s5_act(y, tag + "_act")
    t = matmul(yg, wf["w_glu"], "nn", tag + "_mm_glu")
    m = s5_gate_fwd(y, t, p["b_glu"], a_p, tag + "_gate")
    out_p = matmul(m, wf["w_out"], "nn", tag + "_mm_out")
    h_out = residual_add(h, _seg_unperm(out_p), tag + "_res")
    return h_out, (h, hn_p, a_p, sw, ck_re, ck_im, y, yg, t, m)


def residual_add(h, y, name):
    l, d = h.shape
    tl = ROW_TILE

    def body(h_ref, y_ref, o_ref):
        o_ref[...] = h_ref[...] + y_ref[...]

    return _rowcall(body, name, l // tl, [_rows(tl, d)] * 2, _rows(tl, d), jax.ShapeDtypeStruct((l, d), F32))(h, y)


def s5_layer_bwd(dh_out, saved, p, wf, tag):
    h, hn_p, a_p, sw, ck_re, ck_im, y, yg, t, m = saved
    l = h.shape[0]
    di = p["d_skip"].shape[0]
    dsk = p["d_skip"].reshape(1, di)
    dout_p = _seg_perm(dh_out)
    dm = matmul(dout_p, wf["w_out"], "nt", tag + "_mm_dm")
    g_w_out = matmul(m, dout_p, "tn", tag + "_mm_gwout")
    dt, dyg_a, dz, db_glu = s5_gate_bwd(dm, y, t, p["b_glu"], a_p, tag + "_gate_bwd")
    dyg_b = matmul(dt, wf["w_glu"], "nt", tag + "_mm_dyg")
    g_w_glu = matmul(yg, dt, "tn", tag + "_mm_gwglu")
    dy = s5_act_bwd(y, dyg_a, dyg_b, tag + "_act_bwd")
    e_re, e_im = s5_ends(dy, sw[0], sw[1], sw[4], sw[5], True, tag + "_scanb_ends")
    c_re, c_im = s5_carry(e_re, e_im, sw[0], sw[1], l // S5_SEG, True, tag + "_carry_bwd")
    du, dwbr, dwbi, dwcr, dwci, dds, dlr, dli, _, _ = s5_scan_bwd(
        a_p, dy, *sw, dsk, ck_re, ck_im, c_re, c_im, tag + "_scanb")
    da = jnp.concatenate([du, dz], axis=1)
    dhn_p = matmul(da, wf["w_in"], "nt", tag + "_mm_dhn")
    g_w_in = matmul(hn_p, da, "tn", tag + "_mm_gwin")
    dh, dng = rms_bwd(h, p["norm_g"], _seg_unperm(dhn_p), dh_out, tag + "_rms_bwd")
    ex = lambda m_, r, c: _blockdiag_extract(m_, r, c).reshape(S5_GROUPS, r, c).transpose(0, 2, 1)
    dbb_re, dbb_im = ex(dwbr, S5_GROUP, S5_STATE), ex(dwbi, S5_GROUP, S5_STATE)
    g_c_re, g_c_im = ex(dwcr, S5_STATE, S5_GROUP), ex(dwci, S5_STATE, S5_GROUP)
    dl_re = lane_sum8(dlr).reshape(S5_GROUPS, S5_STATE, 1)
    dl_im = lane_sum8(dli).reshape(S5_GROUPS, S5_STATE, 1)
    gar, gai, gls, gbr, gbi = s5_params_bwd(p["a_re"], p["a_im"], p["log_step"], p["b_re"], p["b_im"],
                                            dl_re, dl_im, dbb_re, dbb_im)
    grads = {"norm_g": dng.reshape(-1), "w_in": g_w_in, "a_re": gar.reshape(S5_GROUPS, S5_STATE),
             "a_im": gai.reshape(S5_GROUPS, S5_STATE), "log_step": gls.reshape(-1), "b_re": gbr, "b_im": gbi,
             "c_re": g_c_re, "c_im": g_c_im, "d_skip": dds.reshape(-1), "w_glu": g_w_glu,
             "b_glu": db_glu.reshape(-1), "w_out": g_w_out}
    return dh, grads


def lane_sum8(t):
    sb, seg, ns = t.shape

    def body(t_ref, o_ref):
        o_ref[...] = jnp.sum(t_ref[...], axis=1, keepdims=True)

    return pl.pallas_call(body, name="s5_seg_sum", out_shape=jax.ShapeDtypeStruct((sb, 1, ns), F32))(t)


MLA_DI = MLA_HEADS * 128
MLA_CQ0 = MLA_DI
MLA_CKV0 = MLA_CQ0 + MLA_Q_RANK
MLA_KR0 = MLA_CKV0 + MLA_KV_RANK
MLA_AW = MLA_KR0 + LANES


def _rot_half(x):
    w = x.shape[-1]
    lane = lax.broadcasted_iota(jnp.int32, x.shape, x.ndim - 1)
    return jnp.where(lane % MLA_ROPE < MLA_ROPE // 2, pltpu.roll(x, w - MLA_ROPE // 2, x.ndim - 1),
                     pltpu.roll(x, MLA_ROPE // 2, x.ndim - 1))


def rope_tables(pos):
    l = pos.shape[0]
    tl = ROW_TILE
    j = np.arange(LANES) % MLA_ROPE % (MLA_ROPE // 2)
    inv_freq = (ROPE_THETA ** (-(2.0 * j) / MLA_ROPE)).astype(np.float32).reshape(1, LANES)
    sign = np.where(np.arange(LANES) % MLA_ROPE < MLA_ROPE // 2, -1.0, 1.0).astype(np.float32).reshape(1, LANES)

    def body(p_ref, f_ref, s_ref, cos_ref, sin_ref):
        ang = p_ref[...].astype(F32) * f_ref[...]
        cos_ref[...] = jnp.cos(ang)
        sin_ref[...] = jnp.sin(ang) * s_ref[...]

    st = jax.ShapeDtypeStruct((l, LANES), F32)
    return _rowcall(body, "rope_tables", l // tl, [_rows(tl, 1), _full((1, LANES)), _full((1, LANES))],
                    [_rows(tl, LANES)] * 2, [st, st])(pos, jnp.asarray(inv_freq), jnp.asarray(sign))


def _rope(x, cos, sins):
    return x * cos + _rot_half(x) * sins


def _rope_t(dy, cos, sins):
    return dy * cos - sins * _rot_half(dy)


def _rmsn(x):
    r = lax.rsqrt(jnp.mean(x * x, axis=-1, keepdims=True) + NORM_EPS)
    return x * r, r


def mla_pre(a, q_g, kv_g, cos, sins, name):
    l = a.shape[0]
    tl = ROW_TILE

    def body(a_ref, qg_ref, kg_ref, cos_ref, sin_ref, cq_ref, ckv_ref, krs_ref):
        xq, _ = _rmsn(a_ref[:, MLA_CQ0:MLA_CKV0])
        cq_ref[...] = (xq * qg_ref[...]).astype(BF16)
        xk, _ = _rmsn(a_ref[:, MLA_CKV0:MLA_KR0])
        ckv_ref[...] = (xk * kg_ref[...]).astype(BF16)
        kr = a_ref[:, MLA_KR0:MLA_AW]
        kr2 = kr + pltpu.roll(kr, MLA_ROPE, 1)
        kr2 = _rope(kr2, cos_ref[...], sin_ref[...])
        lane = lax.broadcasted_iota(jnp.int32, kr2.shape, 1)
        krs_ref[0] = jnp.where(lane < MLA_ROPE, kr2, 0.0).astype(BF16)
        krs_ref[1] = jnp.where(lane >= MLA_ROPE, kr2, 0.0).astype(BF16)

    st = jax.ShapeDtypeStruct
    return _rowcall(body, name, l // tl,
                    [_rows(tl, MLA_AW), _full((1, MLA_Q_RANK)), _full((1, MLA_KV_RANK)), _rows(tl, LANES), _rows(tl, LANES)],
                    [_rows(tl, MLA_Q_RANK), _rows(tl, MLA_KV_RANK), pl.BlockSpec((2, tl, LANES), lambda i: (0, i, 0))],
                    [st((l, MLA_Q_RANK), BF16), st((l, MLA_KV_RANK), BF16), st((2, l, LANES), BF16)])(
        a, q_g.reshape(1, -1), kv_g.reshape(1, -1), cos, sins)


def mla_rope_q(qr, cos, sins, name):
    l, w = qr.shape
    tl = ROW_TILE

    def body(q_ref, cos_ref, sin_ref, o_ref):
        c, s = cos_ref[...], sin_ref[...]
        for p in range(w // LANES):
            sl = slice(p * LANES, (p + 1) * LANES)
            o_ref[:, sl] = _rope(q_ref[:, sl], c, s).astype(BF16)

    return _rowcall(body, name, l // tl, [_rows(tl, w), _rows(tl, LANES), _rows(tl, LANES)], _rows(tl, w),
                    jax.ShapeDtypeStruct((l, w), BF16))(qr, cos, sins)


ATT_OUT = 512
ATT_IN = 256
ATT_R = ATT_OUT // ATT_IN


def _scores(qn, qr, kn, kr, mask_off, transposed):
    q2 = jnp.concatenate([qn, qr], axis=1)
    k2 = jnp.concatenate([kn, kr], axis=1)
    s = (_dot(k2, q2, DN_NT) if transposed else _dot(q2, k2, DN_NT)) * MLA_SCALE
    if mask_off is None:
        return s
    r = lax.broadcasted_iota(jnp.int32, s.shape, 0)
    c = lax.broadcasted_iota(jnp.int32, s.shape, 1)
    return jnp.where((r <= c + mask_off) if transposed else (c + mask_off <= r), s, NEG_INF)


def _fold(x):
    return x[:, :LANES], x[:, LANES:]


def flash_fwd(qn, qr, kv, krs, name):
    l = qn.shape[0]
    nq = l // ATT_OUT

    def body(qn_ref, qr_ref, kv_ref, kr_ref, o_ref, lse_ref, s_buf):
        qi = pl.program_id(1)
        q_r = qr_ref[...]
        q_n = [qn_ref[:, hh * LANES:(hh + 1) * LANES] for hh in range(2)]

        def block_scores(j, mx, mask_off):
            sl = pl.ds(pl.multiple_of(j * ATT_IN, ATT_IN), ATT_IN)
            out = []
            for hh in range(2):
                s = _scores(q_n[hh], q_r, kv_ref[sl, 2 * hh * LANES:(2 * hh + 1) * LANES], kr_ref[hh, sl, :],
                            mask_off, False)
                s_buf[hh, j] = s
                lo, hi = _fold(s)
                out.append(jnp.maximum(mx[hh], jnp.maximum(lo, hi)))
            return tuple(out)

        ninf = jnp.full((ATT_OUT, LANES), NEG_INF, F32)
        mx = lax.fori_loop(0, ATT_R * qi, lambda j, c: block_scores(j, c, None), (ninf, ninf))
        for d in range(ATT_R):
            mx = block_scores(ATT_R * qi + d, mx, d * ATT_IN)
        m = [jnp.max(mx[hh], axis=-1, keepdims=True) for hh in range(2)]

        def block_pv(j, carry):
            sl = pl.ds(pl.multiple_of(j * ATT_IN, ATT_IN), ATT_IN)
            out = []
            for hh in range(2):
                ls, acc = carry[hh]
                p = jnp.exp(s_buf[hh, j] - m[hh])
                lo, hi = _fold(p)
                out.append((ls + (lo + hi),
                            acc + _dot(p, kv_ref[sl, (2 * hh + 1) * LANES:(2 * hh + 2) * LANES], DN_NN)))
            return tuple(out)

        z = jnp.zeros((ATT_OUT, LANES), F32)
        res = lax.fori_loop(0, ATT_R * (qi + 1), block_pv, ((z, z), (z, z)))
        for hh in range(2):
            lsum = jnp.sum(res[hh][0], axis=-1, keepdims=True)
            o_ref[:, hh * LANES:(hh + 1) * LANES] = res[hh][1] / lsum
            lse_ref[hh] = m[hh] + jnp.log(lsum)

    st = jax.ShapeDtypeStruct
    return pl.pallas_call(
        body, name=name, grid=(MLA_HEADS // 2, nq),
        in_specs=[pl.BlockSpec((ATT_OUT, 2 * LANES), lambda p, i: (i, p)),
                  pl.BlockSpec((ATT_OUT, LANES), lambda p, i: (i, p)),
                  pl.BlockSpec((l, 4 * LANES), lambda p, i: (0, p)),
                  pl.BlockSpec((2, l, LANES), lambda p, i: (0, 0, 0))],
        out_specs=[pl.BlockSpec((ATT_OUT, 2 * LANES), lambda p, i: (i, p)),
                   pl.BlockSpec((2, ATT_OUT, 1), lambda p, i: (p, i, 0))],
        out_shape=[st((l, MLA_DI), F32), st((MLA_HEADS, l, 1), F32)],
        scratch_shapes=[pltpu.VMEM((2, l // ATT_IN, ATT_OUT, ATT_IN), F32)],
        compiler_params=_cparams(("parallel", "arbitrary")))(qn, qr, kv, krs)


def flash_dkv(qn, qr, kv, krs, do, lse_row, delta_row, name):
    l = qn.shape[0]
    nk = l // ATT_OUT
    nq = l // ATT_IN

    def body(qn_ref, qr_ref, do_ref, lse_ref, dl_ref, kv_ref, kr_ref, dkv_ref, dkr_ref):
        kj = pl.program_id(1)
        lane = lax.broadcasted_iota(jnp.int32, (ATT_OUT, LANES), 1)
        kn = [kv_ref[:, 2 * hh * LANES:(2 * hh + 1) * LANES] for hh in range(2)]
        v = [kv_ref[:, (2 * hh + 1) * LANES:(2 * hh + 2) * LANES] for hh in range(2)]

        def block(i, carry, mask_off):
            sl = pl.ds(pl.multiple_of(i * ATT_IN, ATT_IN), ATT_IN)
            q_r = qr_ref[sl, :]
            out = []
            for hh in range(2):
                dk2, dv = carry[hh]
                hs = slice(hh * LANES, (hh + 1) * LANES)
                q_n, d_o = qn_ref[sl, hs], do_ref[sl, hs]
                s = _scores(q_n, q_r, kn[hh], kr_ref[hh], mask_off, True)
                pt = jnp.exp(s - lse_ref[hh, i])
                dv = dv + _dot(pt, d_o, DN_NN)
                dpt = _dot(v[hh], d_o, DN_NT)
                dst = (pt * (dpt - dl_ref[hh, i]) * MLA_SCALE).astype(BF16)
                out.append((dk2 + _dot(dst, jnp.concatenate([q_n, q_r], axis=1), DN_NN), dv))
            return tuple(out)

        z = jnp.zeros((ATT_OUT, LANES), F32)
        z2 = jnp.zeros((ATT_OUT, 2 * LANES), F32)
        res = ((z2, z), (z2, z))
        for d in range(ATT_R):
            res = block(ATT_R * kj + d, res, d * ATT_IN)
        res = lax.fori_loop(ATT_R * (kj + 1), nq, lambda i, c: block(i, c, None), res)
        for hh in range(2):
            dkv_ref[:, 2 * hh * LANES:(2 * hh + 1) * LANES] = res[hh][0][:, :LANES].astype(BF16)
            dkv_ref[:, (2 * hh + 1) * LANES:(2 * hh + 2) * LANES] = res[hh][1].astype(BF16)
        dkr_ref[0] = jnp.where(lane < MLA_ROPE, res[0][0][:, LANES:], res[1][0][:, LANES:])

    st = jax.ShapeDtypeStruct
    return pl.pallas_call(
        body, name=name, grid=(MLA_HEADS // 2, nk),
        in_specs=[pl.BlockSpec((l, 2 * LANES), lambda p, j: (0, p)),
                  pl.BlockSpec((l, LANES), lambda p, j: (0, p)),
                  pl.BlockSpec((l, 2 * LANES), lambda p, j: (0, p)),
                  pl.BlockSpec((2, nq, 1, ATT_IN), lambda p, j: (p, 0, 0, 0)),
                  pl.BlockSpec((2, nq, 1, ATT_IN), lambda p, j: (p, 0, 0, 0)),
                  pl.BlockSpec((ATT_OUT, 4 * LANES), lambda p, j: (j, p)),
                  pl.BlockSpec((2, ATT_OUT, LANES), lambda p, j: (0, j, 0))],
        out_specs=[pl.BlockSpec((ATT_OUT, 4 * LANES), lambda p, j: (j, p)),
                   pl.BlockSpec((1, ATT_OUT, LANES), lambda p, j: (p, j, 0))],
        out_shape=[st((l, 2 * MLA_DI), BF16), st((MLA_HEADS // 2, l, LANES), F32)],
        compiler_params=_cparams(("parallel", "arbitrary")))(qn, qr, do, lse_row, delta_row, kv, krs)


def flash_dq(qn, qr, kv, krs, do, lse, delta, cos, sins, name):
    l = qn.shape[0]
    nq = l // ATT_OUT

    def body(qn_ref, qr_ref, do_ref, lse_ref, dl_ref, kv_ref, kr_ref, cos_ref, sin_ref, dqn_ref, dqr_ref):
        qi = pl.program_id(1)
        q_r = qr_ref[...]
        q_n = [qn_ref[:, hh * LANES:(hh + 1) * LANES] for hh in range(2)]
        d_o = [do_ref[:, hh * LANES:(hh + 1) * LANES] for hh in range(2)]
        lse_h = [lse_ref[hh] for hh in range(2)]
        dl_h = [dl_ref[hh] for hh in range(2)]

        def block(j, carry, mask_off):
            sl = pl.ds(pl.multiple_of(j * ATT_IN, ATT_IN), ATT_IN)
            dq2 = list(carry)
            for hh in range(2):
                kn = kv_ref[sl, 2 * hh * LANES:(2 * hh + 1) * LANES]
                v = kv_ref[sl, (2 * hh + 1) * LANES:(2 * hh + 2) * LANES]
                kr = kr_ref[hh, sl, :]
                s = _scores(q_n[hh], q_r, kn, kr, mask_off, False)
                pr = jnp.exp(s - lse_h[hh])
                dp = _dot(d_o[hh], v, DN_NT)
                ds = (pr * (dp - dl_h[hh]) * MLA_SCALE).astype(BF16)
                dq2[hh] = dq2[hh] + _dot(ds, jnp.concatenate([kn, kr], axis=1), DN_NN)
            return tuple(dq2)

        z2 = jnp.zeros((ATT_OUT, 2 * LANES), F32)
        res = lax.fori_loop(0, ATT_R * qi, lambda j, c: block(j, c, None), (z2, z2))
        for d in range(ATT_R):
            res = block(ATT_R * qi + d, res, d * ATT_IN)
        dqn_ref[:, 0:LANES] = res[0][:, :LANES].astype(BF16)
        dqn_ref[:, LANES:2 * LANES] = res[1][:, :LANES].astype(BF16)
        dqr = res[0][:, LANES:] + res[1][:, LANES:]
        dqr_ref[...] = _rope_t(dqr, cos_ref[...], sin_ref[...]).astype(BF16)

    st = jax.ShapeDtypeStruct
    return pl.pallas_call(
        body, name=name, grid=(MLA_HEADS // 2, nq),
        in_specs=[pl.BlockSpec((ATT_OUT, 2 * LANES), lambda p, i: (i, p)),
                  pl.BlockSpec((ATT_OUT, LANES), lambda p, i: (i, p)),
                  pl.BlockSpec((ATT_OUT, 2 * LANES), lambda p, i: (i, p)),
                  pl.BlockSpec((2, ATT_OUT, 1), lambda p, i: (p, i, 0)),
                  pl.BlockSpec((2, ATT_OUT, 1), lambda p, i: (p, i, 0)),
                  pl.BlockSpec((l, 4 * LANES), lambda p, i: (0, p)),
                  pl.BlockSpec((2, l, LANES), lambda p, i: (0, 0, 0)),
                  pl.BlockSpec((ATT_OUT, LANES), lambda p, i: (i, 0)),
                  pl.BlockSpec((ATT_OUT, LANES), lambda p, i: (i, 0))],
        out_specs=[pl.BlockSpec((ATT_OUT, 2 * LANES), lambda p, i: (i, p)),
                   pl.BlockSpec((ATT_OUT, LANES), lambda p, i: (i, p))],
        out_shape=[st((l, MLA_DI), BF16), st((l, MLA_HEADS * MLA_ROPE), BF16)],
        compiler_params=_cparams(("parallel", "arbitrary")))(qn, qr, do, lse, delta, kv, krs, cos, sins)


def mla_gate_fwd(o, a, name):
    l = o.shape[0]
    tl = ROW_TILE

    def body(o_ref, z_ref, m_ref):
        m_ref[...] = (o_ref[...] * _silu(z_ref[...])).astype(BF16)

    return _rowcall(body, name, l // tl, [_rows(tl, MLA_DI), _rows(tl, MLA_DI)], _rows(tl, MLA_DI),
                    jax.ShapeDtypeStruct((l, MLA_DI), BF16))(o, a)


def mla_gate_bwd(dm, o, a, name):
    l = o.shape[0]
    tl = ROW_TILE

    def body(dm_ref, o_ref, z_ref, do_ref, dz_ref, dl_ref):
        dmv, ov, z = dm_ref[...], o_ref[...], z_ref[...]
        d_o = dmv * _silu(z)
        do_ref[...] = d_o.astype(BF16)
        dz_ref[...] = (dmv * ov * _silu_grad(z)).astype(BF16)
        pr = d_o * ov
        for h in range(MLA_HEADS):
            dl_ref[h] = jnp.sum(pr[:, h * LANES:(h + 1) * LANES], axis=1, keepdims=True)

    st = jax.ShapeDtypeStruct
    return _rowcall(body, name, l // tl, [_rows(tl, MLA_DI)] * 3,
                    [_rows(tl, MLA_DI), _rows(tl, MLA_DI), pl.BlockSpec((MLA_HEADS, tl, 1), lambda i: (0, i, 0))],
                    [st((l, MLA_DI), BF16), st((l, MLA_DI), BF16), st((MLA_HEADS, l, 1), F32)])(dm, o, a)


def mla_post(a, dcqn, dckvn, dkr_pairs, dz, q_g, kv_g, cos, sins, name):
    l = a.shape[0]
    tl = ROW_TILE
    npair = MLA_HEADS // 2

    def norm_bwd(x, g, dy):
        xhat, r = _rmsn(x)
        dxh = dy * g
        return r * (dxh - xhat * jnp.mean(dxh * xhat, axis=-1, keepdims=True)), jnp.sum(dy * xhat, axis=0, keepdims=True)

    def body(a_ref, dq_ref, dk_ref, dkr_ref, dz_ref, qg_ref, kg_ref, cos_ref, sin_ref, da_ref, dqg_ref, dkg_ref):
        i = pl.program_id(0)
        da_ref[:, 0:MLA_DI] = dz_ref[...]
        dcq, dqg = norm_bwd(a_ref[:, MLA_CQ0:MLA_CKV0], qg_ref[...], dq_ref[...])
        da_ref[:, MLA_CQ0:MLA_CKV0] = dcq.astype(BF16)
        dckv, dkg = norm_bwd(a_ref[:, MLA_CKV0:MLA_KR0], kg_ref[...], dk_ref[...])
        da_ref[:, MLA_CKV0:MLA_KR0] = dckv.astype(BF16)
        dk2 = dkr_ref[0]
        for p in range(1, npair):
            dk2 = dk2 + dkr_ref[p]
        dk2 = _rope_t(dk2, cos_ref[...], sin_ref[...])
        dk2 = dk2 + pltpu.roll(dk2, MLA_ROPE, 1)
        lane = lax.broadcasted_iota(jnp.int32, dk2.shape, 1)
        da_ref[:, MLA_KR0:MLA_AW] = jnp.where(lane < MLA_ROPE, dk2, 0.0).astype(BF16)
        _acc(dqg_ref, dqg, i)
        _acc(dkg_ref, dkg, i)

    st = jax.ShapeDtypeStruct
    return _rowcall(body, name, l // tl,
                    [_rows(tl, MLA_AW), _rows(tl, MLA_Q_RANK), _rows(tl, MLA_KV_RANK),
                     pl.BlockSpec((npair, tl, LANES), lambda i: (0, i, 0)), _rows(tl, MLA_DI),
                     _full((1, MLA_Q_RANK)), _full((1, MLA_KV_RANK)), _rows(tl, LANES), _rows(tl, LANES)],
                    [_rows(tl, MLA_AW), _full((1, MLA_Q_RANK)), _full((1, MLA_KV_RANK))],
                    [st((l, MLA_AW), BF16), st((1, MLA_Q_RANK), F32), st((1, MLA_KV_RANK), F32)])(
        a, dcqn, dckvn, dkr_pairs, dz, q_g.reshape(1, -1), kv_g.reshape(1, -1), cos, sins)


def _mla_w_in_perm(w):
    r = MLA_Q_RANK + MLA_KV_RANK + MLA_ROPE
    pad = jnp.zeros(w.shape[:-1] + (MLA_AW - MLA_KR0 - MLA_ROPE,), w.dtype)
    return jnp.concatenate([w[..., r:], w[..., :r], pad], axis=-1)


def _mla_w_in_unperm(g):
    r = MLA_Q_RANK + MLA_KV_RANK + MLA_ROPE
    return jnp.concatenate([g[..., MLA_DI:MLA_DI + r], g[..., :MLA_DI]], axis=-1)


def _mla_w_uq_split(w):
    k = w.shape[0]
    w3 = w.reshape(k, MLA_HEADS, MLA_NOPE + MLA_ROPE)
    return w3[:, :, :MLA_NOPE].reshape(k, MLA_HEADS * MLA_NOPE), w3[:, :, MLA_NOPE:].reshape(k, MLA_HEADS * MLA_ROPE)


def _mla_w_uq_merge(gn, gr):
    k = gn.shape[0]
    return jnp.concatenate([gn.reshape(k, MLA_HEADS, MLA_NOPE), gr.reshape(k, MLA_HEADS, MLA_ROPE)], axis=2).reshape(k, -1)


def mla_layer_fwd(h, p, wf, cos, sins, tag):
    hn = rms_fwd(h, p["norm_g"], tag + "_rms")
    w_in = _mla_w_in_perm(wf["w_in"])
    w_uq_n, w_uq_r = _mla_w_uq_split(wf["w_uq"])
    a = matmul(hn, w_in, "nn", tag + "_mm_in")
    cqn, ckvn, krs = mla_pre(a, p["q_norm_g"], p["kv_norm_g"], cos, sins, tag + "_pre")
    qn = matmul(cqn, w_uq_n, "nn", tag + "_mm_qn", out_dtype=BF16)
    qr_raw = matmul(cqn, w_uq_r, "nn", tag + "_mm_qr")
    qr = mla_rope_q(qr_raw, cos, sins, tag + "_rope_q")
    kv = matmul(ckvn, wf["w_ukv"], "nn", tag + "_mm_kv", out_dtype=BF16)
    o, lse = flash_fwd(qn, qr, kv, krs, tag + "_flash")
    m = mla_gate_fwd(o, a, tag + "_gate")
    h_out = matmul(m, wf["w_out"], "nn", tag + "_mm_out", add=h)
    return h_out, (h, hn, a, cqn, ckvn, krs, qn, qr, kv, o, lse, m, w_in, w_uq_n, w_uq_r)


def mla_layer_bwd(dh_out, saved, p, wf, cos, sins, tag):
    h, hn, a, cqn, ckvn, krs, qn, qr, kv, o, lse, m, w_in, w_uq_n, w_uq_r = saved
    l = h.shape[0]
    dm = matmul(dh_out, wf["w_out"], "nt", tag + "_mm_dm")
    g_w_out = matmul(m, dh_out, "tn", tag + "_mm_gwout")
    do, dz, delta = mla_gate_bwd(dm, o, a, tag + "_gate_bwd")
    lse_row = lse.reshape(MLA_HEADS, l // ATT_IN, 1, ATT_IN)
    delta_row = delta.reshape(MLA_HEADS, l // ATT_IN, 1, ATT_IN)
    dkv, dkr_pairs = flash_dkv(qn, qr, kv, krs, do, lse_row, delta_row, tag + "_flash_dkv")
    dqn, dqr = flash_dq(qn, qr, kv, krs, do, lse, delta, cos, sins, tag + "_flash_dq")
    dcqn = matmul(dqn, w_uq_n, "nt", tag + "_mm_dcq_n")
    dcqn = matmul(dqr, w_uq_r, "nt", tag + "_mm_dcq_r", add=dcqn)
    g_uq_n = matmul(cqn, dqn, "tn", tag + "_mm_guq_n")
    g_uq_r = matmul(cqn, dqr, "tn", tag + "_mm_guq_r")
    dckvn = matmul(dkv, wf["w_ukv"], "nt", tag + "_mm_dckv")
    g_ukv = matmul(ckvn, dkv, "tn", tag + "_mm_gukv")
    da, dqg, dkg = mla_post(a, dcqn, dckvn, dkr_pairs, dz, p["q_norm_g"], p["kv_norm_g"], cos, sins, tag + "_post")
    dhn = matmul(da, w_in, "nt", tag + "_mm_dhn")
    g_w_in = matmul(hn, da, "tn", tag + "_mm_gwin")
    dh, dng = rms_bwd(h, p["norm_g"], dhn, dh_out, tag + "_rms_bwd")
    grads = {"norm_g": dng.reshape(-1), "w_in": _mla_w_in_unperm(g_w_in), "q_norm_g": dqg.reshape(-1),
             "w_uq": _mla_w_uq_merge(g_uq_n, g_uq_r), "kv_norm_g": dkg.reshape(-1), "w_ukv": g_ukv, "w_out": g_w_out}
    return dh, grads


ANY = pl.BlockSpec(memory_space=pl.ANY)


def _me():
    return lax.axis_index("x"), lax.axis_index("y"), lax.axis_index("c")


def _chip():
    return 2 * lax.axis_index("x") + lax.axis_index("y")


def _other_chips(x, y):
    return [(1 - x, y), (x, 1 - y), (1 - x, 1 - y)]


def _rcopy(src, dst, ssem, rsem, dev):
    return pltpu.make_async_remote_copy(src_ref=src, dst_ref=dst, send_sem=ssem, recv_sem=rsem,
                                        device_id=dev, device_id_type=MESH)


def _half(ref, c, hf):
    return ref.at[pl.ds(c * hf, hf), :]


def weights_allgather(wb):
    nr, w = wb.shape
    hf = nr // 2

    def body(w_ref, o_ref, ssem, rsem):
        x, y, c = _me()
        k = 2 * x + y
        chips = _other_chips(x, y)
        first = [_rcopy(_half(w_ref, c, hf), _half(o_ref.at[k], c, hf), ssem.at[j], rsem.at[j], (cx, cy, c))
                 for j, (cx, cy) in enumerate(chips)]
        for cp in first:
            cp.start()
        passed = []
        for j, (cx, cy) in enumerate(chips):
            region = _half(o_ref.at[2 * cx + cy], c, hf)
            _rcopy(region, region, ssem.at[j], rsem.at[j], (cx, cy, c)).wait_recv()
            fwd = _rcopy(region, region, ssem.at[3 + j], rsem.at[3 + j], (x, y, 1 - c))
            fwd.start()
            passed.append(fwd)
        for j, (cx, cy) in enumerate(chips):
            region = _half(o_ref.at[2 * cx + cy], 1 - c, hf)
            _rcopy(region, region, ssem.at[3 + j], rsem.at[3 + j], (x, y, 1 - c)).wait_recv()
        for cp in first + passed:
            cp.wait_send()

    out = pl.pallas_call(
        body, name="weights_allgather", in_specs=[ANY], out_specs=ANY,
        out_shape=jax.ShapeDtypeStruct((N_CHIPS, nr, w), wb.dtype),
        scratch_shapes=[pltpu.SemaphoreType.DMA((6,)), pltpu.SemaphoreType.DMA((6,))],
    )(wb)
    return lax.dynamic_update_slice(out, wb[None], (_chip(), 0, 0))


def grads_to_sibling(p):
    _, nr, w = p.shape
    hf = nr // 2

    def body(p_ref, o_ref, ssem, rsem):
        x, y, c = _me()
        cp = _rcopy(p_ref.at[:, pl.ds((1 - c) * hf, hf), :], o_ref, ssem, rsem, (x, y, 1 - c))
        cp.start()
        cp.wait()

    return pl.pallas_call(
        body, name="grads_to_sibling", in_specs=[ANY], out_specs=ANY,
        out_shape=jax.ShapeDtypeStruct((N_CHIPS, hf, w), p.dtype),
        scratch_shapes=[pltpu.SemaphoreType.DMA(()), pltpu.SemaphoreType.DMA(())])(p)


def pair_sum(p, ra):
    _, nr, w = p.shape
    hf = nr // 2
    tr = _pick_rows(hf)
    nb = hf // tr

    def body(c_ref, p_ref, r_ref, o_ref):
        o_ref[...] = p_ref[...] + r_ref[...]

    c = lax.axis_index("c").astype(jnp.int32).reshape(1)
    return pl.pallas_call(
        body, name="pair_sum",
        grid_spec=pltpu.PrefetchScalarGridSpec(
            num_scalar_prefetch=1, grid=(N_CHIPS, nb),
            in_specs=[pl.BlockSpec((1, tr, w), lambda k, i, c_ref: (k, c_ref[0] * nb + i, 0)),
                      pl.BlockSpec((1, tr, w), lambda k, i, c_ref: (k, i, 0))],
            out_specs=pl.BlockSpec((1, tr, w), lambda k, i, c_ref: (k, i, 0))),
        out_shape=jax.ShapeDtypeStruct((N_CHIPS, hf, w), F32),
        compiler_params=_cparams(("parallel", "parallel")))(c, p, ra)


def grads_across_chips(t):
    _, hf, w = t.shape

    def body(t_ref, o_ref, ssem, rsem):
        x, y, c = _me()
        k = 2 * x + y
        chips = _other_chips(x, y)
        sends = [_rcopy(t_ref.at[2 * cx + cy], o_ref.at[k], ssem.at[j], rsem.at[j], (cx, cy, c))
                 for j, (cx, cy) in enumerate(chips)]
        for cp in sends:
            cp.start()
        for j, (cx, cy) in enumerate(chips):
            _rcopy(t_ref.at[k], o_ref.at[2 * cx + cy], ssem.at[j], rsem.at[j], (cx, cy, c)).wait_recv()
        for cp in sends:
            cp.wait_send()

    return pl.pallas_call(
        body, name="grads_across_chips", in_specs=[ANY], out_specs=ANY,
        out_shape=jax.ShapeDtypeStruct((N_CHIPS, hf, w), t.dtype),
        scratch_shapes=[pltpu.SemaphoreType.DMA((3,)), pltpu.SemaphoreType.DMA((3,))])(t)


def chip_sum(t, rb):
    _, hf, w = rb.shape
    tr = _pick_rows(hf)
    nb = hf // tr

    def body(kc_ref, t_ref, r_ref, o_ref):
        k = kc_ref[0]
        acc = jnp.where(k == 0, t_ref[0], r_ref[0])
        for j in range(1, N_CHIPS):
            acc = acc + jnp.where(k == j, t_ref[0], r_ref[j])
        o_ref[...] = acc

    kc = jnp.stack([_chip(), lax.axis_index("c")]).astype(jnp.int32)
    return pl.pallas_call(
        body, name="chip_sum",
        grid_spec=pltpu.PrefetchScalarGridSpec(
            num_scalar_prefetch=1, grid=(nb,),
            in_specs=[pl.BlockSpec((1, tr, w), lambda i, kc_ref: (kc_ref[0], i, 0)),
                      pl.BlockSpec((N_CHIPS, tr, w), lambda i, kc_ref: (0, i, 0))],
            out_specs=pl.BlockSpec((tr, w), lambda i, kc_ref: (kc_ref[1] * nb + i, 0))),
        out_shape=jax.ShapeDtypeStruct((2 * hf, w), F32), compiler_params=_cparams(("parallel",)))(kc, t, rb)


def reduced_to_sibling(g):
    nr, w = g.shape
    hf = nr // 2

    def body(g_ref, o_ref, ssem, rsem):
        x, y, c = _me()
        cp = _rcopy(_half(o_ref, c, hf), _half(o_ref, c, hf), ssem, rsem, (x, y, 1 - c))
        cp.start()
        _rcopy(_half(o_ref, c, hf), _half(o_ref, 1 - c, hf), ssem, rsem, (x, y, 1 - c)).wait_recv()
        cp.wait_send()

    return pl.pallas_call(
        body, name="reduced_to_sibling", in_specs=[ANY], out_specs=ANY, input_output_aliases={0: 0},
        out_shape=jax.ShapeDtypeStruct((nr, w), g.dtype),
        scratch_shapes=[pltpu.SemaphoreType.DMA(()), pltpu.SemaphoreType.DMA(())])(g)


def small_allgather(g, row0, nrs):
    w = g.shape[1]

    def body(g_ref, o_ref, ssem, rsem):
        x, y, c = _me()
        k = 2 * x + y
        chips = _other_chips(x, y)
        src = g_ref.at[pl.ds(row0, nrs), :]
        sends = [_rcopy(src, o_ref.at[k], ssem.at[j], rsem.at[j], (cx, cy, c)) for j, (cx, cy) in enumerate(chips)]
        for cp in sends:
            cp.start()
        for j, (cx, cy) in enumerate(chips):
            _rcopy(src, o_ref.at[2 * cx + cy], ssem.at[j], rsem.at[j], (cx, cy, c)).wait_recv()
        for cp in sends:
            cp.wait_send()

    out = pl.pallas_call(
        body, name="small_allgather", in_specs=[ANY], out_specs=ANY,
        out_shape=jax.ShapeDtypeStruct((N_CHIPS, nrs, w), g.dtype),
        scratch_shapes=[pltpu.SemaphoreType.DMA((3,)), pltpu.SemaphoreType.DMA((3,))])(g)
    return lax.dynamic_update_slice(out, g[row0:row0 + nrs][None], (_chip(), 0, 0))


def adamw(w, g, m, v, name):
    r, wd = w.shape
    tr = _pick_rows(r)
    bc1 = 1.0 - ADAM_B1 ** ADAM_STEP
    bc2 = 1.0 - ADAM_B2 ** ADAM_STEP

    def body(w_ref, g_ref, m_ref, v_ref, d_ref, nm_ref, nv_ref):
        gv = g_ref[...]
        nm = ADAM_B1 * m_ref[...] + (1.0 - ADAM_B1) * gv
        nv = ADAM_B2 * v_ref[...] + (1.0 - ADAM_B2) * (gv * gv)
        nm_ref[...] = nm
        nv_ref[...] = nv
        d_ref[...] = -ADAM_LR * ((nm / bc1) / (jnp.sqrt(nv / bc2) + ADAM_EPS) + ADAM_WD * w_ref[...])

    spec = pl.BlockSpec((tr, wd), lambda i: (i, 0))
    st = jax.ShapeDtypeStruct((r, wd), F32)
    return pl.pallas_call(body, name=name, grid=(r // tr,), in_specs=[spec] * 4, out_specs=[spec] * 3,
                          out_shape=[st, st, st], compiler_params=_cparams(("parallel",)))(w, g, m, v)


LAYER_KINDS = ("gmlp", "s5", "mla", "gmlp")
PARAMS = {
    "gmlp": ("norm_g", "w_in", "ln_g", "ln_b", "w_s", "b_s", "w_out"),
    "s5": ("norm_g", "w_in", "a_re", "a_im", "log_step", "b_re", "b_im", "c_re", "c_im", "d_skip", "w_glu", "b_glu", "w_out"),
    "mla": ("norm_g", "w_in", "q_norm_g", "w_uq", "kv_norm_g", "w_ukv", "w_out"),
}
COL_SHARDED = ("w_in", "w_uq", "w_ukv")
ROW_SHARDED = ("w_out", "w_glu")
WEIGHT_NAMES = [("l%d_" % i) + n for i, kind in enumerate(LAYER_KINDS) for n in PARAMS[kind]] + ["final_norm_g"]


def _is_big(name):
    return name.split("_", 1)[1] in COL_SHARDED + ROW_SHARDED


BIG = [n for n in WEIGHT_NAMES if _is_big(n)]
SMALL = [n for n in WEIGHT_NAMES if not _is_big(n)]


def _pack_rows(blocks):
    return jnp.concatenate([b.reshape(-1, PACK_W) for b in blocks], axis=0)


def _shard_major(name, full):
    r, c = full.shape
    if name.split("_", 1)[1] in COL_SHARDED:
        t = full.reshape(r, N_CHIPS, c // N_CHIPS).transpose(1, 0, 2)
    else:
        t = full.reshape(N_CHIPS, r // N_CHIPS, c)
    return t.reshape(N_CHIPS, -1, PACK_W)


def _from_shard_major(name, t, block_shape):
    r, c = block_shape
    if name.split("_", 1)[1] in COL_SHARDED:
        return t.reshape(N_CHIPS, r, c).transpose(1, 0, 2).reshape(r, N_CHIPS * c)
    return t.reshape(N_CHIPS * r, c)


def _small_pack(arrs, total_padded):
    flat = jnp.concatenate([a.reshape(-1) for a in arrs])
    return jnp.pad(flat, (0, total_padded - flat.shape[0]))


def kernel(x, positions, l0_norm_g, l0_w_in, l0_ln_g, l0_ln_b, l0_w_s, l0_b_s, l0_w_out, l1_norm_g, l1_w_in, l1_a_re, l1_a_im, l1_log_step, l1_b_re, l1_b_im, l1_c_re, l1_c_im, l1_d_skip, l1_w_glu, l1_b_glu, l1_w_out, l2_norm_g, l2_w_in, l2_q_norm_g, l2_w_uq, l2_kv_norm_g, l2_w_ukv, l2_w_out, l3_norm_g, l3_w_in, l3_ln_g, l3_ln_b, l3_w_s, l3_b_s, l3_w_out, final_norm_g, loss_target, m_l0_norm_g, m_l0_w_in, m_l0_ln_g, m_l0_ln_b, m_l0_w_s, m_l0_b_s, m_l0_w_out, m_l1_norm_g, m_l1_w_in, m_l1_a_re, m_l1_a_im, m_l1_log_step, m_l1_b_re, m_l1_b_im, m_l1_c_re, m_l1_c_im, m_l1_d_skip, m_l1_w_glu, m_l1_b_glu, m_l1_w_out, m_l2_norm_g, m_l2_w_in, m_l2_q_norm_g, m_l2_w_uq, m_l2_kv_norm_g, m_l2_w_ukv, m_l2_w_out, m_l3_norm_g, m_l3_w_in, m_l3_ln_g, m_l3_ln_b, m_l3_w_s, m_l3_b_s, m_l3_w_out, m_final_norm_g, v_l0_norm_g, v_l0_w_in, v_l0_ln_g, v_l0_ln_b, v_l0_w_s, v_l0_b_s, v_l0_w_out, v_l1_norm_g, v_l1_w_in, v_l1_a_re, v_l1_a_im, v_l1_log_step, v_l1_b_re, v_l1_b_im, v_l1_c_re, v_l1_c_im, v_l1_d_skip, v_l1_w_glu, v_l1_b_glu, v_l1_w_out, v_l2_norm_g, v_l2_w_in, v_l2_q_norm_g, v_l2_w_uq, v_l2_kv_norm_g, v_l2_w_ukv, v_l2_w_out, v_l3_norm_g, v_l3_w_in, v_l3_ln_g, v_l3_ln_b, v_l3_w_s, v_l3_b_s, v_l3_w_out, v_final_norm_g):
    args = locals()
    w = {n: args[n] for n in WEIGHT_NAMES}
    mom_m = {n: args["m_" + n] for n in WEIGHT_NAMES}
    mom_v = {n: args["v_" + n] for n in WEIGHT_NAMES}
    h0 = x[0]
    target = loss_target[0]
    pos = positions.reshape(-1, 1)

    big_rows = [w[n].size // PACK_W for n in BIG]
    nrb = sum(big_rows)
    w_big = _pack_rows([w[n] for n in BIG])
    gathered = weights_allgather(jnp.pad(w_big.astype(BF16), ((0, -nrb % 32), (0, 0))))
    full = {}
    r0 = 0
    for n, nr in zip(BIG, big_rows):
        full[n] = _from_shard_major(n, gathered[:, r0:r0 + nr, :], w[n].shape)
        r0 += nr

    def layer_params(i):
        pre = "l%d_" % i
        p = {k[len(pre):]: v for k, v in w.items() if k.startswith(pre)}
        wf = {k[len(pre):]: v for k, v in full.items() if k.startswith(pre)}
        return p, wf

    cos, sins = rope_tables(pos)
    h = h0
    saved = []
    for i, kind in enumerate(LAYER_KINDS):
        p, wf = layer_params(i)
        tag = "l%d" % i
        if kind == "gmlp":
            h, s = gmlp_layer_fwd(h, p, wf, tag)
        elif kind == "s5":
            h, s = s5_layer_fwd(h, p, wf, tag)
        else:
            h, s = mla_layer_fwd(h, p, wf, cos, sins, tag)
        saved.append(s)
    loss_part, dh, g_final = loss_head(h, final_norm_g, target)
    loss = lax.psum(loss_part[0, 0], ("x", "y", "c"))

    grads = {"final_norm_g": g_final.reshape(-1)}
    for i in reversed(range(len(LAYER_KINDS))):
        kind = LAYER_KINDS[i]
        p, wf = layer_params(i)
        tag = "l%d" % i
        if kind == "gmlp":
            dh, g = gmlp_layer_bwd(dh, saved[i], p, wf, tag)
        elif kind == "s5":
            dh, g = s5_layer_bwd(dh, saved[i], p, wf, tag)
        else:
            dh, g = mla_layer_bwd(dh, saved[i], p, wf, cos, sins, tag)
        for k, val in g.items():
            grads["l%d_%s" % (i, k)] = val
    grad_x = dh[None]

    n_small = sum(w[n].size for n in SMALL)
    piece = N_CHIPS * 2 * 8 * PACK_W
    n_small_pad = -(-n_small // piece) * piece
    nrs = n_small_pad // N_CHIPS // PACK_W
    packed = jnp.concatenate(
        [_shard_major(n, grads[n]) for n in BIG]
        + [_small_pack([grads[n] for n in SMALL], n_small_pad).reshape(N_CHIPS, nrs, PACK_W)], axis=1)
    from_sibling = grads_to_sibling(packed)
    chip_part = pair_sum(packed, from_sibling)
    from_chips = grads_across_chips(chip_part)
    reduced = reduced_to_sibling(chip_sum(chip_part, from_chips))
    small_all = small_allgather(reduced, nrb, nrs)

    g_big = reduced[:nrb]
    d_big, nm_big, nv_big = adamw(w_big, g_big, _pack_rows([mom_m[n] for n in BIG]),
                                  _pack_rows([mom_v[n] for n in BIG]), "adamw_big")
    g_small = small_all.reshape(-1, PACK_W)
    sp = lambda d: _small_pack([d[n] for n in SMALL], n_small_pad).reshape(-1, PACK_W)
    d_small, nm_small, nv_small = adamw(sp(w), g_small, sp(mom_m), sp(mom_v), "adamw_small")

    def unpack(big_buf, small_buf):
        out = {}
        r = 0
        for n, nr in zip(BIG, big_rows):
            out[n] = big_buf[r:r + nr].reshape(w[n].shape)
            r += nr
        flat = small_buf.reshape(-1)
        o = 0
        for n in SMALL:
            out[n] = flat[o:o + w[n].size].reshape(w[n].shape)
            o += w[n].size
        return out

    g_out = unpack(g_big, g_small)
    d_out = unpack(d_big, d_small)
    nm_out = unpack(nm_big, nm_small)
    nv_out = unpack(nv_big, nv_small)
    return (loss, grad_x, *[g_out[n] for n in WEIGHT_NAMES], *[d_out[n] for n in WEIGHT_NAMES],
            *[nm_out[n] for n in WEIGHT_NAMES], *[nv_out[n] for n in WEIGHT_NAMES])
```

```python
import functools
import math

import jax
import jax.numpy as jnp
import numpy as np
from jax import lax
from jax.experimental import pallas as pl
from jax.experimental.pallas import tpu as pltpu

F32 = jnp.float32
BF16 = jnp.bfloat16
MESH = pl.DeviceIdType.MESH
VMEM_LIMIT_BYTES = 56 * 1024 * 1024
LANES = 128
PACK_W = 1024
PACK_ROW_ALIGN = 256
ROW_TILE = 256
ADAMW_BLOCK_BYTES = 1024 * 1024
MM_BLOCK_BYTES = 6 * 1024 * 1024

NORM_EPS = 1e-6
N_CHIPS = 4
GMLP_CHUNK = 128
GMLP_GROUPS = 8
S5_GROUPS = 128
S5_GROUP = 16
S5_STATE = 64
S5_SB = 16
S5_SEG = 8
MLA_HEADS = 16
MLA_NOPE = 128
MLA_ROPE = 64
MLA_Q_RANK = 384
MLA_KV_RANK = 128
MLA_SCALE = (MLA_NOPE + MLA_ROPE) ** -0.5
ROPE_THETA = 10000.0
NEG_INF = -1e30
ADAM_LR, ADAM_B1, ADAM_B2, ADAM_EPS, ADAM_WD, ADAM_STEP = 0.001, 0.9, 0.999, 1e-08, 0.01, 10

DN_NN = (((1,), (0,)), ((), ()))
DN_NT = (((1,), (1,)), ((), ()))
DN_TN = (((0,), (0,)), ((), ()))


def _cparams(sem):
    return pltpu.CompilerParams(dimension_semantics=sem, vmem_limit_bytes=VMEM_LIMIT_BYTES)


def _pick(n, cands=(512, 384, 256, 128)):
    for c in cands:
        if n % c == 0:
            return c
    return n


def _pick_rows(r, cap=512, mult=16):
    return max(t for t in range(mult, cap + 1, mult) if r % t == 0)


def _dot(a, b, dn):
    return lax.dot_general(a.astype(BF16), b.astype(BF16), dn, preferred_element_type=F32)


def _sigmoid(x):
    return 1.0 / (1.0 + jnp.exp(-x))


def _gelu(x):
    c = math.sqrt(2.0 / math.pi)
    t = jnp.tanh(c * (x + 0.044715 * x * x * x))
    return 0.5 * x * (1.0 + t)


def _gelu_grad(x):
    c = math.sqrt(2.0 / math.pi)
    t = jnp.tanh(c * (x + 0.044715 * x * x * x))
    return 0.5 * (1.0 + t) + 0.5 * x * (1.0 - t * t) * c * (1.0 + 3.0 * 0.044715 * x * x)


def _silu(z):
    return z * _sigmoid(z)


def _silu_grad(z):
    s = _sigmoid(z)
    return s * (1.0 + z * (1.0 - s))


def matmul(a, b, mode, name, out_dtype=F32, add=None):
    if mode == "nn":
        (m, k), n = a.shape, b.shape[1]
    elif mode == "nt":
        (m, k), n = a.shape, b.shape[0]
    else:
        (k, m), n = a.shape, b.shape[1]
    tm = _pick(m, [t for t in (1024, 512, 384, 256, 128) if t * k * a.dtype.itemsize <= MM_BLOCK_BYTES])
    tn = _pick(n, [t for t in (512, 384, 256, 128) if t * k * b.dtype.itemsize <= MM_BLOCK_BYTES])
    dn = {"nn": DN_NN, "nt": DN_NT, "tn": DN_TN}[mode]

    def body(*refs):
        if add is None:
            a_ref, b_ref, o_ref = refs
        else:
            a_ref, b_ref, add_ref, o_ref = refs
        r = _dot(a_ref[...], b_ref[...], dn)
        if add is not None:
            r = r + add_ref[...].astype(F32)
        o_ref[...] = r.astype(out_dtype)

    a_spec = pl.BlockSpec((k, tm), lambda i, j: (0, i)) if mode == "tn" else pl.BlockSpec((tm, k), lambda i, j: (i, 0))
    b_spec = pl.BlockSpec((tn, k), lambda i, j: (j, 0)) if mode == "nt" else pl.BlockSpec((k, tn), lambda i, j: (0, j))
    o_spec = pl.BlockSpec((tm, tn), lambda i, j: (i, j))
    in_specs = [a_spec, b_spec] + ([o_spec] if add is not None else [])
    args = (a, b) + ((add,) if add is not None else ())
    return pl.pallas_call(
        body, name=name, grid=(m // tm, n // tn), in_specs=in_specs, out_specs=o_spec,
        out_shape=jax.ShapeDtypeStruct((m, n), out_dtype),
        compiler_params=_cparams(("parallel", "arbitrary")))(*args)


def _rows(tl, w, col=0):
    return pl.BlockSpec((tl, w), lambda i: (i, col))


def _full(shape):
    nd = len(shape)
    return pl.BlockSpec(tuple(shape), lambda i: (0,) * nd)


def _rowcall(body, name, n_steps, in_specs, out_specs, out_shape, scratch=()):
    return pl.pallas_call(
        body, name=name, grid=(n_steps,), in_specs=in_specs, out_specs=out_specs, out_shape=out_shape,
        scratch_shapes=list(scratch), compiler_params=_cparams(("arbitrary",)))


def _acc(ref, val, i):
    @pl.when(i == 0)
    def _():
        ref[...] = val

    @pl.when(i != 0)
    def _():
        ref[...] += val


def rms_fwd(h, g, name):
    l, d = h.shape
    tl = ROW_TILE

    def body(h_ref, g_ref, o_ref):
        x = h_ref[...]
        r = lax.rsqrt(jnp.mean(x * x, axis=-1, keepdims=True) + NORM_EPS)
        o_ref[...] = (x * r * g_ref[...]).astype(BF16)

    return _rowcall(body, name, l // tl, [_rows(tl, d), _full((1, d))], _rows(tl, d),
                    jax.ShapeDtypeStruct((l, d), BF16))(h, g.reshape(1, d))


def rms_bwd(h, g, dhn, dh_in, name):
    l, d = h.shape
    tl = ROW_TILE

    def body(h_ref, g_ref, dhn_ref, dhi_ref, dh_ref, dg_ref):
        i = pl.program_id(0)
        x = h_ref[...]
        r = lax.rsqrt(jnp.mean(x * x, axis=-1, keepdims=True) + NORM_EPS)
        xhat = x * r
        dy = dhn_ref[...]
        dxh = dy * g_ref[...]
        dx = r * (dxh - xhat * jnp.mean(dxh * xhat, axis=-1, keepdims=True))
        dh_ref[...] = dhi_ref[...] + dx
        _acc(dg_ref, jnp.sum(dy * xhat, axis=0, keepdims=True), i)

    return _rowcall(body, name, l // tl, [_rows(tl, d), _full((1, d)), _rows(tl, d), _rows(tl, d)],
                    [_rows(tl, d), _full((1, d))],
                    [jax.ShapeDtypeStruct((l, d), F32), jax.ShapeDtypeStruct((1, d), F32)])(h, g.reshape(1, d), dhn, dh_in)


def loss_head(h, g, target):
    l, d = h.shape
    tl = ROW_TILE

    def body(h_ref, g_ref, t_ref, loss_ref, dh_ref, dg_ref):
        i = pl.program_id(0)
        x = h_ref[...]
        gg = g_ref[...]
        r = lax.rsqrt(jnp.mean(x * x, axis=-1, keepdims=True) + NORM_EPS)
        xhat = x * r
        err = xhat * gg - t_ref[...]
        part = 0.5 * jnp.sum(jnp.mean(err * err, axis=-1, keepdims=True), axis=0, keepdims=True)
        _acc(loss_ref, part, i)
        dy = err * (1.0 / d)
        dxh = dy * gg
        dh_ref[...] = r * (dxh - xhat * jnp.mean(dxh * xhat, axis=-1, keepdims=True))
        _acc(dg_ref, jnp.sum(dy * xhat, axis=0, keepdims=True), i)

    return _rowcall(body, "loss_head", l // tl, [_rows(tl, d), _full((1, d)), _rows(tl, d)],
                    [_full((1, 1)), _rows(tl, d), _full((1, d))],
                    [jax.ShapeDtypeStruct((1, 1), F32), jax.ShapeDtypeStruct((l, d), F32),
                     jax.ShapeDtypeStruct((1, d), F32)])(h, g.reshape(1, d), target)


def _gmlp_common(a_ref, lng_ref, lnb_ref):
    di = lng_ref.shape[1]
    u_pre = a_ref[:, 0:di]
    v_pre = a_ref[:, di:2 * di]
    z = a_ref[:, 2 * di:3 * di]
    vg = _gelu(v_pre)
    mu = jnp.mean(vg, axis=-1, keepdims=True)
    xc = vg - mu
    rstd = lax.rsqrt(jnp.mean(xc * xc, axis=-1, keepdims=True) + NORM_EPS)
    vhat = xc * rstd
    vn = vhat * lng_ref[...] + lnb_ref[...]
    return u_pre, v_pre, z, vhat, rstd, vn


def _tril(w):
    r = lax.broadcasted_iota(jnp.int32, w.shape, 0)
    c = lax.broadcasted_iota(jnp.int32, w.shape, 1)
    return jnp.where(c <= r, w, 0.0)


def gmlp_gate_fwd(a, ln_g, ln_b, w_s, b_s, name):
    l, w3 = a.shape
    di = w3 // 3
    dg = di // GMLP_GROUPS
    tl = GMLP_CHUNK

    def body(a_ref, lng_ref, lnb_ref, ws_ref, bs_ref, m_ref):
        u_pre, _, z, _, _, vn = _gmlp_common(a_ref, lng_ref, lnb_ref)
        gate = _gelu(u_pre) * _silu(z)
        for g in range(GMLP_GROUPS):
            sl = slice(g * dg, (g + 1) * dg)
            s = _dot(_tril(ws_ref[g]), vn[:, sl], DN_NN) + bs_ref[g]
            m_ref[:, sl] = (gate[:, sl] * s).astype(BF16)

    return _rowcall(body, name, l // tl,
                    [_rows(tl, w3), _full((1, di)), _full((1, di)), _full(w_s.shape), _full((GMLP_GROUPS, tl, 1))],
                    _rows(tl, di), jax.ShapeDtypeStruct((l, di), BF16))(
        a, ln_g.reshape(1, di), ln_b.reshape(1, di), w_s, b_s.reshape(GMLP_GROUPS, tl, 1))


def gmlp_gate_bwd(a, dm, ln_g, ln_b, w_s, b_s, name):
    l, w3 = a.shape
    di = w3 // 3
    dg = di // GMLP_GROUPS
    tl = GMLP_CHUNK

    def body(a_ref, dm_ref, lng_ref, lnb_ref, ws_ref, bs_ref, da_ref, dlg_ref, dlb_ref, dws_ref, dbs_ref, dvn_ref):
        i = pl.program_id(0)
        u_pre, v_pre, z, vhat, rstd, vn = _gmlp_common(a_ref, lng_ref, lnb_ref)
        dm_v = dm_ref[...]
        u = _gelu(u_pre)
        sz = _silu(z)
        for g in range(GMLP_GROUPS):
            sl = slice(g * dg, (g + 1) * dg)
            wt = _tril(ws_ref[g])
            vn_g = vn[:, sl]
            s = _dot(wt, vn_g, DN_NN) + bs_ref[g]
            dmg = dm_v[:, sl]
            ds = dmg * u[:, sl] * sz[:, sl]
            da_ref[:, sl] = (dmg * s * sz[:, sl] * _gelu_grad(u_pre[:, sl])).astype(BF16)
            da_ref[:, 2 * di + g * dg:2 * di + (g + 1) * dg] = (
                dmg * u[:, sl] * s * _silu_grad(z[:, sl])).astype(BF16)
            dvn_ref[:, sl] = _dot(wt, ds, DN_TN)
            dw = _tril(_dot(ds, vn_g, DN_NT))
            db = jnp.sum(ds, axis=1, keepdims=True)

            @pl.when(i == 0)
            def _():
                dws_ref[g] = dw
                dbs_ref[g] = db

            @pl.when(i != 0)
            def _():
                dws_ref[g] += dw
                dbs_ref[g] += db

        dvn = dvn_ref[...]
        dxh = dvn * lng_ref[...]
        dvg = rstd * (dxh - jnp.mean(dxh, axis=-1, keepdims=True) - vhat * jnp.mean(dxh * vhat, axis=-1, keepdims=True))
        da_ref[:, di:2 * di] = (dvg * _gelu_grad(v_pre)).astype(BF16)
        _acc(dlg_ref, jnp.sum(dvn * vhat, axis=0, keepdims=True), i)
        _acc(dlb_ref, jnp.sum(dvn, axis=0, keepdims=True), i)

    outs = _rowcall(
        body, name, l // tl,
        [_rows(tl, w3), _rows(tl, di), _full((1, di)), _full((1, di)), _full(w_s.shape), _full((GMLP_GROUPS, tl, 1))],
        [_rows(tl, w3), _full((1, di)), _full((1, di)), _full(w_s.shape), _full((GMLP_GROUPS, tl, 1))],
        [jax.ShapeDtypeStruct((l, w3), BF16), jax.ShapeDtypeStruct((1, di), F32), jax.ShapeDtypeStruct((1, di), F32),
         jax.ShapeDtypeStruct(w_s.shape, F32), jax.ShapeDtypeStruct((GMLP_GROUPS, tl, 1), F32)],
        scratch=[pltpu.VMEM((tl, di), F32)])(
        a, dm, ln_g.reshape(1, di), ln_b.reshape(1, di), w_s, b_s.reshape(GMLP_GROUPS, tl, 1))
    return outs


def gmlp_layer_fwd(h, p, wf, tag):
    hn = rms_fwd(h, p["norm_g"], tag + "_rms")
    a = matmul(hn, wf["w_in"], "nn", tag + "_mm_in")
    m = gmlp_gate_fwd(a, p["ln_g"], p["ln_b"], p["w_s"], p["b_s"], tag + "_gate")
    h_out = matmul(m, wf["w_out"], "nn", tag + "_mm_out", add=h)
    return h_out, (h, hn, a, m)


def gmlp_layer_bwd(dh_out, saved, p, wf, tag):
    h, hn, a, m = saved
    dm = matmul(dh_out, wf["w_out"], "nt", tag + "_mm_dm")
    g_w_out = matmul(m, dh_out, "tn", tag + "_mm_gwout")
    da, dlg, dlb, dws, dbs = gmlp_gate_bwd(a, dm, p["ln_g"], p["ln_b"], p["w_s"], p["b_s"], tag + "_gate_bwd")
    dhn = matmul(da, wf["w_in"], "nt", tag + "_mm_dhn")
    g_w_in = matmul(hn, da, "tn", tag + "_mm_gwin")
    dh, dng = rms_bwd(h, p["norm_g"], dhn, dh_out, tag + "_rms_bwd")
    grads = {"norm_g": dng.reshape(-1), "w_in": g_w_in, "ln_g": dlg.reshape(-1), "ln_b": dlb.reshape(-1),
             "w_s": dws, "b_s": dbs.reshape(GMLP_GROUPS, GMLP_CHUNK), "w_out": g_w_out}
    return dh, grads


def _cmul(ar, ai, br, bi):
    return ar * br - ai * bi, ar * bi + ai * br


S5_PG = 16


def _gblock(tail):
    return pl.BlockSpec((S5_PG,) + tuple(tail), lambda i: (i, 0, 0))


def s5_params_fwd(a_re, a_im, log_step, b_re, b_im):
    g, p, hh = b_re.shape

    def body(ar_ref, ai_ref, ls_ref, br_ref, bi_ref, lr_ref, li_ref, bbr_ref, bbi_ref):
        ar, ai = ar_ref[...], ai_ref[...]
        step = jnp.exp(ls_ref[...])
        mag = jnp.exp(ar * step)
        lr, li = mag * jnp.cos(ai * step), mag * jnp.sin(ai * step)
        den = 1.0 / (ar * ar + ai * ai)
        fr, fi = _cmul(lr - 1.0, li, ar * den, -ai * den)
        lr_ref[...] = lr
        li_ref[...] = li
        bbr, bbi = _cmul(fr, fi, br_ref[...], bi_ref[...])
        bbr_ref[...] = bbr
        bbi_ref[...] = bbi

    s1 = jax.ShapeDtypeStruct((g, p, 1), F32)
    s3 = jax.ShapeDtypeStruct((g, p, hh), F32)
    b1, b0, b3 = _gblock((p, 1)), _gblock((1, 1)), _gblock((p, hh))
    return pl.pallas_call(body, name="s5_params_fwd", grid=(g // S5_PG,), in_specs=[b1, b1, b0, b3, b3],
                          out_specs=[b1, b1, b3, b3], out_shape=[s1, s1, s3, s3],
                          compiler_params=_cparams(("parallel",)))(
        a_re.reshape(g, p, 1), a_im.reshape(g, p, 1), log_step.reshape(g, 1, 1), b_re, b_im)


def s5_params_bwd(a_re, a_im, log_step, b_re, b_im, dl_re, dl_im, dbb_re, dbb_im):
    g, p, hh = b_re.shape

    def body(ar_ref, ai_ref, ls_ref, br_ref, bi_ref, dlr_ref, dli_ref, dbr_ref, dbi_ref,
             gar_ref, gai_ref, gls_ref, gbr_ref, gbi_ref):
        ar, ai = ar_ref[...], ai_ref[...]
        step = jnp.exp(ls_ref[...])
        mag = jnp.exp(ar * step)
        lr, li = mag * jnp.cos(ai * step), mag * jnp.sin(ai * step)
        den = 1.0 / (ar * ar + ai * ai)
        ir, ii = ar * den, -ai * den
        fr, fi = _cmul(lr - 1.0, li, ir, ii)
        br, bi = br_ref[...], bi_ref[...]
        dbr, dbi = dbr_ref[...], dbi_ref[...]
        gbr, gbi = _cmul(fr, -fi, dbr, dbi)
        gbr_ref[...] = gbr
        gbi_ref[...] = gbi
        pr, pi = _cmul(br, -bi, dbr, dbi)
        gfr = jnp.sum(pr, axis=-1, keepdims=True)
        gfi = jnp.sum(pi, axis=-1, keepdims=True)
        t_r, t_i = _cmul(ir, -ii, gfr, gfi)
        glr, gli = dlr_ref[...] + t_r, dli_ref[...] + t_i
        c1r, c1i = _cmul(step * lr, -step * li, glr, gli)
        qr, qi = _cmul(fr, fi, ir, ii)
        c2r, c2i = _cmul(-qr, qi, gfr, gfi)
        gar_ref[...] = c1r + c2r
        gai_ref[...] = c1i + c2i
        wr, wi = _cmul(ar, ai, lr, li)
        sr, _ = _cmul(wr, -wi, glr, gli)
        gls_ref[...] = jnp.sum(sr, axis=1, keepdims=True) * step

    s1 = jax.ShapeDtypeStruct((g, p, 1), F32)
    s3 = jax.ShapeDtypeStruct((g, p, hh), F32)
    b1, b0, b3 = _gblock((p, 1)), _gblock((1, 1)), _gblock((p, hh))
    return pl.pallas_call(body, name="s5_params_bwd", grid=(g // S5_PG,),
                          in_specs=[b1, b1, b0, b3, b3, b1, b1, b3, b3], out_specs=[b1, b1, b0, b3, b3],
                          out_shape=[s1, s1, jax.ShapeDtypeStruct((g, 1, 1), F32), s3, s3],
                          compiler_params=_cparams(("parallel",)))(
        a_re.reshape(g, p, 1), a_im.reshape(g, p, 1), log_step.reshape(g, 1, 1), b_re, b_im,
        dl_re, dl_im, dbb_re, dbb_im)


def _blockdiag(t):
    sb, n, r, c = t.shape
    eye = jnp.eye(n, dtype=bool)[None, :, None, :, None]
    full = jnp.where(eye, t[:, :, :, None, :], jnp.zeros((), t.dtype))
    return full.reshape(sb, n * r, n * c)


def _blockdiag_extract(m, r, c):
    sb = m.shape[0]
    n = m.shape[1] // r
    m5 = m.reshape(sb, n, r, n, c)
    return jnp.stack([m5[:, i, :, i, :] for i in range(n)], axis=1)


S5_TB = 64
S5_UNROLL = 8


def s5_scan_fwd(a_p, lam_re, lam_im, wb_re, wb_im, wc_re, wc_im, d_skip, x0_re, x0_im, name):
    l = a_p.shape[0]
    di = d_skip.shape[1]
    rows = S5_SEG * S5_TB
    nb = l // rows
    ns = wb_re.shape[2]

    def body(u_ref, lr_ref, li_ref, wbr_ref, wbi_ref, wcr_ref, wci_ref, ds_ref, x0r_ref, x0i_ref,
             y_ref, ckr_ref, cki_ref, xer_ref, xei_ref, bur, bui, xr_s, xi_s):
        b = pl.program_id(1)

        @pl.when(b == 0)
        def _():
            xr_s[...] = x0r_ref[0]
            xi_s[...] = x0i_ref[0]

        ckr_ref[0, 0] = xr_s[...]
        cki_ref[0, 0] = xi_s[...]
        u = u_ref[...]
        bur[...] = _dot(u, wbr_ref[0], DN_NN)
        bui[...] = _dot(u, wbi_ref[0], DN_NN)
        lr = jnp.broadcast_to(lr_ref[0], (S5_SEG, ns))
        li = jnp.broadcast_to(li_ref[0], (S5_SEG, ns))

        def step(t, carry):
            xr, xi = carry
            sl = pl.ds(pl.multiple_of(t * S5_SEG, S5_SEG), S5_SEG)
            nr = lr * xr - li * xi + bur[sl, :]
            ni = lr * xi + li * xr + bui[sl, :]
            bur[sl, :] = nr
            bui[sl, :] = ni
            return nr, ni

        xr, xi = lax.fori_loop(0, S5_TB, step, (xr_s[...], xi_s[...]), unroll=S5_UNROLL)
        xr_s[...] = xr
        xi_s[...] = xi
        xer_ref[0] = xr
        xei_ref[0] = xi
        y_ref[...] = _dot(bur[...], wcr_ref[0], DN_NN) - _dot(bui[...], wci_ref[0], DN_NN) + ds_ref[...] * u

    sb3 = lambda s, b: (s, 0, 0)
    st = jax.ShapeDtypeStruct
    return pl.pallas_call(
        body, name=name, grid=(S5_SB, nb),
        in_specs=[pl.BlockSpec((rows, LANES), lambda s, b: (b, s)),
                  pl.BlockSpec((1, 1, ns), sb3), pl.BlockSpec((1, 1, ns), sb3),
                  pl.BlockSpec((1, LANES, ns), sb3), pl.BlockSpec((1, LANES, ns), sb3),
                  pl.BlockSpec((1, ns, LANES), sb3), pl.BlockSpec((1, ns, LANES), sb3),
                  pl.BlockSpec((1, LANES), lambda s, b: (0, s)),
                  pl.BlockSpec((1, S5_SEG, ns), sb3), pl.BlockSpec((1, S5_SEG, ns), sb3)],
        out_specs=[pl.BlockSpec((rows, LANES), lambda s, b: (b, s)),
                   pl.BlockSpec((1, 1, S5_SEG, ns), lambda s, b: (s, b, 0, 0)),
                   pl.BlockSpec((1, 1, S5_SEG, ns), lambda s, b: (s, b, 0, 0)),
                   pl.BlockSpec((1, S5_SEG, ns), sb3), pl.BlockSpec((1, S5_SEG, ns), sb3)],
        out_shape=[st((l, di), F32), st((S5_SB, nb, S5_SEG, ns), F32), st((S5_SB, nb, S5_SEG, ns), F32),
                   st((S5_SB, S5_SEG, ns), F32), st((S5_SB, S5_SEG, ns), F32)],
        scratch_shapes=[pltpu.VMEM((rows, ns), F32), pltpu.VMEM((rows, ns), F32),
                        pltpu.VMEM((S5_SEG, ns), F32), pltpu.VMEM((S5_SEG, ns), F32)],
        compiler_params=_cparams(("parallel", "arbitrary")))(
        a_p, lam_re, lam_im, wb_re, wb_im, wc_re, wc_im, d_skip, x0_re, x0_im)


def s5_ends(inp, lam_re, lam_im, w_re, w_im, adjoint, name):
    l = inp.shape[0]
    rows = S5_SEG * S5_TB
    nb = l // rows
    ns = lam_re.shape[2]

    def body(i_ref, lr_ref, li_ref, wr_ref, wi_ref, er_ref, ei_ref, pr_b, pi_b, xr_s, xi_s):
        b = pl.program_id(1)

        @pl.when(b == 0)
        def _():
            xr_s[...] = jnp.zeros_like(xr_s)
            xi_s[...] = jnp.zeros_like(xi_s)

        v = i_ref[...]
        lr = jnp.broadcast_to(lr_ref[0], (S5_SEG, ns))
        li = jnp.broadcast_to(li_ref[0], (S5_SEG, ns))
        if adjoint:
            pr_b[...] = _dot(v, wr_ref[0], DN_NT)
            pi_b[...] = -_dot(v, wi_ref[0], DN_NT)
            li = -li
        else:
            pr_b[...] = _dot(v, wr_ref[0], DN_NN)
            pi_b[...] = _dot(v, wi_ref[0], DN_NN)

        def step(k, carry):
            xr, xi = carry
            t = S5_TB - 1 - k if adjoint else k
            sl = pl.ds(pl.multiple_of(t * S5_SEG, S5_SEG), S5_SEG)
            return lr * xr - li * xi + pr_b[sl, :], lr * xi + li * xr + pi_b[sl, :]

        xr, xi = lax.fori_loop(0, S5_TB, step, (xr_s[...], xi_s[...]), unroll=S5_UNROLL)
        xr_s[...] = xr
        xi_s[...] = xi
        er_ref[0] = xr
        ei_ref[0] = xi

    sb3 = lambda s, b: (s, 0, 0)
    blk = (lambda s, b: (nb - 1 - b, s)) if adjoint else (lambda s, b: (b, s))
    wshape = (1, ns, LANES) if adjoint else (1, LANES, ns)
    st = jax.ShapeDtypeStruct((S5_SB, S5_SEG, ns), F32)
    return pl.pallas_call(
        body, name=name, grid=(S5_SB, nb),
        in_specs=[pl.BlockSpec((rows, LANES), blk), pl.BlockSpec((1, 1, ns), sb3), pl.BlockSpec((1, 1, ns), sb3),
                  pl.BlockSpec(wshape, sb3), pl.BlockSpec(wshape, sb3)],
        out_specs=[pl.BlockSpec((1, S5_SEG, ns), sb3), pl.BlockSpec((1, S5_SEG, ns), sb3)],
        out_shape=[st, st],
        scratch_shapes=[pltpu.VMEM((rows, ns), F32), pltpu.VMEM((rows, ns), F32),
                        pltpu.VMEM((S5_SEG, ns), F32), pltpu.VMEM((S5_SEG, ns), F32)],
        compiler_params=_cparams(("parallel", "arbitrary")))(inp, lam_re, lam_im, w_re, w_im)


def s5_scan_bwd(a_p, dy, lam_re, lam_im, wb_re, wb_im, wc_re, wc_im, d_skip, ck_re, ck_im, a0_re, a0_im, name):
    l = a_p.shape[0]
    di = d_skip.shape[1]
    rows = S5_SEG * S5_TB
    nb = l // rows
    ns = wb_re.shape[2]

    def body(u_ref, dy_ref, lr_ref, li_ref, wbr_ref, wbi_ref, wcr_ref, wci_ref, ds_ref, ckr_ref, cki_ref,
             a0r_ref, a0i_ref,
             du_ref, dwbr_ref, dwbi_ref, dwcr_ref, dwci_ref, dds_ref, dlr_ref, dli_ref, aer_ref, aei_ref,
             xr_b, xi_b, gr_b, gi_b, ar_s, ai_s):
        b = pl.program_id(1)

        @pl.when(b == 0)
        def _():
            ar_s[...] = a0r_ref[0]
            ai_s[...] = a0i_ref[0]

        u = u_ref[...]
        dyv = dy_ref[...]
        lr = jnp.broadcast_to(lr_ref[0], (S5_SEG, ns))
        li = jnp.broadcast_to(li_ref[0], (S5_SEG, ns))
        xr_b[...] = _dot(u, wbr_ref[0], DN_NN)
        xi_b[...] = _dot(u, wbi_ref[0], DN_NN)

        def fstep(t, carry):
            xr, xi = carry
            sl = pl.ds(pl.multiple_of(t * S5_SEG, S5_SEG), S5_SEG)
            nr = lr * xr - li * xi + xr_b[sl, :]
            ni = lr * xi + li * xr + xi_b[sl, :]
            xr_b[sl, :] = nr
            xi_b[sl, :] = ni
            return nr, ni

        x0r, x0i = ckr_ref[0, 0], cki_ref[0, 0]
        lax.fori_loop(0, S5_TB, fstep, (x0r, x0i), unroll=S5_UNROLL)
        dwcr = _dot(xr_b[...], dyv, DN_TN)
        dwci = -_dot(xi_b[...], dyv, DN_TN)
        gr_b[...] = _dot(dyv, wcr_ref[0], DN_NT)
        gi_b[...] = -_dot(dyv, wci_ref[0], DN_NT)

        def bstep(k, carry):
            ar, ai, dlr, dli = carry
            t = S5_TB - 1 - k
            sl = pl.ds(pl.multiple_of(t * S5_SEG, S5_SEG), S5_SEG)
            slp = pl.ds(pl.multiple_of(jnp.maximum(t - 1, 0) * S5_SEG, S5_SEG), S5_SEG)
            nr = gr_b[sl, :] + lr * ar + li * ai
            ni = gi_b[sl, :] + lr * ai - li * ar
            gr_b[sl, :] = nr
            gi_b[sl, :] = ni
            first = t == 0
            pr = jnp.where(first, x0r, xr_b[slp, :])
            pi = jnp.where(first, x0i, xi_b[slp, :])
            dlr = dlr + nr * pr + ni * pi
            dli = dli + ni * pr - nr * pi
            return nr, ni, dlr, dli

        zero = jnp.zeros((S5_SEG, ns), F32)
        ar, ai, dlr, dli = lax.fori_loop(0, S5_TB, bstep, (ar_s[...], ai_s[...], zero, zero), unroll=S5_UNROLL)
        ar_s[...] = ar
        ai_s[...] = ai
        aer_ref[0] = ar
        aei_ref[0] = ai
        dsk = ds_ref[...]
        du_ref[...] = (_dot(gr_b[...], wbr_ref[0], DN_NT) + _dot(gi_b[...], wbi_ref[0], DN_NT) + dsk * dyv).astype(BF16)
        dwbr = _dot(u, gr_b[...], DN_TN)
        dwbi = _dot(u, gi_b[...], DN_TN)
        dds = jnp.sum(dyv * u, axis=0, keepdims=True)

        @pl.when(b == 0)
        def _():
            dwbr_ref[0] = dwbr
            dwbi_ref[0] = dwbi
            dwcr_ref[0] = dwcr
            dwci_ref[0] = dwci
            dds_ref[...] = dds
            dlr_ref[0] = dlr
            dli_ref[0] = dli

        @pl.when(b != 0)
        def _():
            dwbr_ref[0] += dwbr
            dwbi_ref[0] += dwbi
            dwcr_ref[0] += dwcr
            dwci_ref[0] += dwci
            dds_ref[...] += dds
            dlr_ref[0] += dlr
            dli_ref[0] += dli

    sb3 = lambda s, b: (s, 0, 0)
    rev = lambda s, b: (nb - 1 - b, s)
    st = jax.ShapeDtypeStruct
    return pl.pallas_call(
        body, name=name, grid=(S5_SB, nb),
        in_specs=[pl.BlockSpec((rows, LANES), rev), pl.BlockSpec((rows, LANES), rev),
                  pl.BlockSpec((1, 1, ns), sb3), pl.BlockSpec((1, 1, ns), sb3),
                  pl.BlockSpec((1, LANES, ns), sb3), pl.BlockSpec((1, LANES, ns), sb3),
                  pl.BlockSpec((1, ns, LANES), sb3), pl.BlockSpec((1, ns, LANES), sb3),
                  pl.BlockSpec((1, LANES), lambda s, b: (0, s)),
                  pl.BlockSpec((1, 1, S5_SEG, ns), lambda s, b: (s, nb - 1 - b, 0, 0)),
                  pl.BlockSpec((1, 1, S5_SEG, ns), lambda s, b: (s, nb - 1 - b, 0, 0)),
                  pl.BlockSpec((1, S5_SEG, ns), sb3), pl.BlockSpec((1, S5_SEG, ns), sb3)],
        out_specs=[pl.BlockSpec((rows, LANES), rev),
                   pl.BlockSpec((1, LANES, ns), sb3), pl.BlockSpec((1, LANES, ns), sb3),
                   pl.BlockSpec((1, ns, LANES), sb3), pl.BlockSpec((1, ns, LANES), sb3),
                   pl.BlockSpec((1, LANES), lambda s, b: (0, s)),
                   pl.BlockSpec((1, S5_SEG, ns), sb3), pl.BlockSpec((1, S5_SEG, ns), sb3),
                   pl.BlockSpec((1, S5_SEG, ns), sb3), pl.BlockSpec((1, S5_SEG, ns), sb3)],
        out_shape=[st((l, di), BF16), st((S5_SB, LANES, ns), F32), st((S5_SB, LANES, ns), F32),
                   st((S5_SB, ns, LANES), F32), st((S5_SB, ns, LANES), F32), st((1, di), F32),
                   st((S5_SB, S5_SEG, ns), F32), st((S5_SB, S5_SEG, ns), F32),
                   st((S5_SB, S5_SEG, ns), F32), st((S5_SB, S5_SEG, ns), F32)],
        scratch_shapes=[pltpu.VMEM((rows, ns), F32), pltpu.VMEM((rows, ns), F32),
                        pltpu.VMEM((rows, ns), F32), pltpu.VMEM((rows, ns), F32),
                        pltpu.VMEM((S5_SEG, ns), F32), pltpu.VMEM((S5_SEG, ns), F32)],
        compiler_params=_cparams(("parallel", "arbitrary")))(
        a_p, dy, lam_re, lam_im, wb_re, wb_im, wc_re, wc_im, d_skip, ck_re, ck_im, a0_re, a0_im)


def s5_carry(e_re, e_im, lam_re, lam_im, seg_len, reverse, name):
    sb, seg, ns = e_re.shape

    def body(er_ref, ei_ref, lr_ref, li_ref, cr_ref, ci_ref):
        pr, pi = lr_ref[...], li_ref[...]
        if reverse:
            pi = -pi
        for _ in range(int(math.log2(seg_len))):
            pr, pi = _cmul(pr, pi, pr, pi)
        er, ei = er_ref[...], ei_ref[...]
        row = lax.broadcasted_iota(jnp.int32, (sb, seg, ns), 1)
        cr = jnp.zeros((sb, seg, ns), F32)
        ci = jnp.zeros((sb, seg, ns), F32)
        cur_r = jnp.zeros((sb, 1, ns), F32)
        cur_i = jnp.zeros((sb, 1, ns), F32)
        order = range(seg - 2, -1, -1) if reverse else range(1, seg)
        for s in order:
            src = s + 1 if reverse else s - 1
            mr, mi = _cmul(pr, pi, cur_r, cur_i)
            cur_r = jnp.sum(jnp.where(row == src, er, 0.0), axis=1, keepdims=True) + mr
            cur_i = jnp.sum(jnp.where(row == src, ei, 0.0), axis=1, keepdims=True) + mi
            cr = jnp.where(row == s, cur_r, cr)
            ci = jnp.where(row == s, cur_i, ci)
        cr_ref[...] = cr
        ci_ref[...] = ci

    st = jax.ShapeDtypeStruct((sb, seg, ns), F32)
    return pl.pallas_call(body, name=name, out_shape=[st, st],
                          compiler_params=pltpu.CompilerParams(vmem_limit_bytes=VMEM_LIMIT_BYTES))(e_re, e_im, lam_re, lam_im)


def s5_act(y, name):
    l, d = y.shape
    tl = ROW_TILE

    def body(y_ref, o_ref):
        o_ref[...] = _gelu(y_ref[...]).astype(BF16)

    return _rowcall(body, name, l // tl, [_rows(tl, d)], _rows(tl, d), jax.ShapeDtypeStruct((l, d), BF16))(y)


def s5_gate_fwd(y, t, b_glu, a_p, name):
    l, d = y.shape
    tl = ROW_TILE

    def body(y_ref, t_ref, b_ref, z_ref, m_ref):
        yg = _gelu(y_ref[...])
        m_ref[...] = (yg * _sigmoid(t_ref[...] + b_ref[...]) * _silu(z_ref[...])).astype(BF16)

    return _rowcall(body, name, l // tl, [_rows(tl, d), _rows(tl, d), _full((1, d)), _rows(tl, d, 1)], _rows(tl, d),
                    jax.ShapeDtypeStruct((l, d), BF16))(y, t, b_glu.reshape(1, d), a_p)


def s5_gate_bwd(dm, y, t, b_glu, a_p, name):
    l, d = y.shape
    tl = ROW_TILE

    def body(dm_ref, y_ref, t_ref, b_ref, z_ref, dt_ref, dyg_ref, dz_ref, db_ref):
        i = pl.program_id(0)
        dmv = dm_ref[...]
        z = z_ref[...]
        yg = _gelu(y_ref[...])
        sg = _sigmoid(t_ref[...] + b_ref[...])
        y2 = yg * sg
        dy2 = dmv * _silu(z)
        dz_ref[...] = (dmv * y2 * _silu_grad(z)).astype(BF16)
        dyg_ref[...] = dy2 * sg
        dt = dy2 * yg * sg * (1.0 - sg)
        dt_ref[...] = dt.astype(BF16)
        _acc(db_ref, jnp.sum(dt, axis=0, keepdims=True), i)

    st = jax.ShapeDtypeStruct
    return _rowcall(body, name, l // tl, [_rows(tl, d), _rows(tl, d), _rows(tl, d), _full((1, d)), _rows(tl, d, 1)],
                    [_rows(tl, d), _rows(tl, d), _rows(tl, d), _full((1, d))],
                    [st((l, d), BF16), st((l, d), F32), st((l, d), BF16), st((1, d), F32)])(
        dm, y, t, b_glu.reshape(1, d), a_p)


def s5_act_bwd(y, dyg_a, dyg_b, name):
    l, d = y.shape
    tl = ROW_TILE

    def body(y_ref, a_ref, b_ref, o_ref):
        o_ref[...] = (a_ref[...] + b_ref[...]) * _gelu_grad(y_ref[...])

    return _rowcall(body, name, l // tl, [_rows(tl, d)] * 3, _rows(tl, d), jax.ShapeDtypeStruct((l, d), F32))(y, dyg_a, dyg_b)


def _seg_perm(t):
    l, d = t.shape
    return t.reshape(S5_SEG, l // S5_SEG, d).transpose(1, 0, 2).reshape(l, d)


def _seg_unperm(t):
    l, d = t.shape
    return t.reshape(l // S5_SEG, S5_SEG, d).transpose(1, 0, 2).reshape(l, d)


def _s5_weights(p):
    lr, li, bbr, bbi = s5_params_fwd(p["a_re"], p["a_im"], p["log_step"], p["b_re"], p["b_im"])
    ns = 8 * S5_STATE
    lam_re = lr.reshape(S5_SB, 1, ns)
    lam_im = li.reshape(S5_SB, 1, ns)
    to_bd = lambda t: _blockdiag(t.reshape(S5_SB, 8, t.shape[1], t.shape[2]))
    wb_re = to_bd(bbr.transpose(0, 2, 1)).astype(BF16)
    wb_im = to_bd(bbi.transpose(0, 2, 1)).astype(BF16)
    wc_re = to_bd(p["c_re"].transpose(0, 2, 1)).astype(BF16)
    wc_im = to_bd(p["c_im"].transpose(0, 2, 1)).astype(BF16)
    return lam_re, lam_im, wb_re, wb_im, wc_re, wc_im


def s5_layer_fwd(h, p, wf, tag):
    l = h.shape[0]
    di = p["d_skip"].shape[0]
    hn = rms_fwd(h, p["norm_g"], tag + "_rms")
    hn_p = _seg_perm(hn)
    a_p = matmul(hn_p, wf["w_in"], "nn", tag + "_mm_in")
    sw = _s5_weights(p)
    dsk = p["d_skip"].reshape(1, di)
    e_re, e_im = s5_ends(a_p, sw[0], sw[1], sw[2], sw[3], False, tag + "_scan_ends")
    c_re, c_im = s5_carry(e_re, e_im, sw[0], sw[1], l // S5_SEG, False, tag + "_carry")
    y, ck_re, ck_im, _, _ = s5_scan_fwd(a_p, *sw, dsk, c_re, c_im, tag + "_scan")
    yg = s5_act(y, tag + "_act")
    t = matmul(yg, wf["w_glu"], "nn", tag + "_mm_glu")
    m = s5_gate_fwd(y, t, p["b_glu"], a_p, tag + "_gate")
    out_p = matmul(m, wf["w_out"], "nn", tag + "_mm_out")
    h_out = residual_add(h, _seg_unperm(out_p), tag + "_res")
    return h_out, (h, hn_p, a_p, sw, ck_re, ck_im, y, yg, t, m)


def residual_add(h, y, name):
    l, d = h.shape
    tl = ROW_TILE

    def body(h_ref, y_ref, o_ref):
        o_ref[...] = h_ref[...] + y_ref[...]

    return _rowcall(body, name, l // tl, [_rows(tl, d)] * 2, _rows(tl, d), jax.ShapeDtypeStruct((l, d), F32))(h, y)


def s5_layer_bwd(dh_out, saved, p, wf, tag):
    h, hn_p, a_p, sw, ck_re, ck_im, y, yg, t, m = saved
    l = h.shape[0]
    di = p["d_skip"].shape[0]
    dsk = p["d_skip"].reshape(1, di)
    dout_p = _seg_perm(dh_out)
    dm = matmul(dout_p, wf["w_out"], "nt", tag + "_mm_dm")
    g_w_out = matmul(m, dout_p, "tn", tag + "_mm_gwout")
    dt, dyg_a, dz, db_glu = s5_gate_bwd(dm, y, t, p["b_glu"], a_p, tag + "_gate_bwd")
    dyg_b = matmul(dt, wf["w_glu"], "nt", tag + "_mm_dyg")
    g_w_glu = matmul(yg, dt, "tn", tag + "_mm_gwglu")
    dy = s5_act_bwd(y, dyg_a, dyg_b, tag + "_act_bwd")
    e_re, e_im = s5_ends(dy, sw[0], sw[1], sw[4], sw[5], True, tag + "_scanb_ends")
    c_re, c_im = s5_carry(e_re, e_im, sw[0], sw[1], l // S5_SEG, True, tag + "_carry_bwd")
    du, dwbr, dwbi, dwcr, dwci, dds, dlr, dli, _, _ = s5_scan_bwd(
        a_p, dy, *sw, dsk, ck_re, ck_im, c_re, c_im, tag + "_scanb")
    da = jnp.concatenate([du, dz], axis=1)
    dhn_p = matmul(da, wf["w_in"], "nt", tag + "_mm_dhn")
    g_w_in = matmul(hn_p, da, "tn", tag + "_mm_gwin")
    dh, dng = rms_bwd(h, p["norm_g"], _seg_unperm(dhn_p), dh_out, tag + "_rms_bwd")
    ex = lambda m_, r, c: _blockdiag_extract(m_, r, c).reshape(S5_GROUPS, r, c).transpose(0, 2, 1)
    dbb_re, dbb_im = ex(dwbr, S5_GROUP, S5_STATE), ex(dwbi, S5_GROUP, S5_STATE)
    g_c_re, g_c_im = ex(dwcr, S5_STATE, S5_GROUP), ex(dwci, S5_STATE, S5_GROUP)
    dl_re = lane_sum8(dlr).reshape(S5_GROUPS, S5_STATE, 1)
    dl_im = lane_sum8(dli).reshape(S5_GROUPS, S5_STATE, 1)
    gar, gai, gls, gbr, gbi = s5_params_bwd(p["a_re"], p["a_im"], p["log_step"], p["b_re"], p["b_im"],
                                            dl_re, dl_im, dbb_re, dbb_im)
    grads = {"norm_g": dng.reshape(-1), "w_in": g_w_in, "a_re": gar.reshape(S5_GROUPS, S5_STATE),
             "a_im": gai.reshape(S5_GROUPS, S5_STATE), "log_step": gls.reshape(-1), "b_re": gbr, "b_im": gbi,
             "c_re": g_c_re, "c_im": g_c_im, "d_skip": dds.reshape(-1), "w_glu": g_w_glu,
             "b_glu": db_glu.reshape(-1), "w_out": g_w_out}
    return dh, grads


def lane_sum8(t):
    sb, seg, ns = t.shape

    def body(t_ref, o_ref):
        o_ref[...] = jnp.sum(t_ref[...], axis=1, keepdims=True)

    return pl.pallas_call(body, name="s5_seg_sum", out_shape=jax.ShapeDtypeStruct((sb, 1, ns), F32))(t)


MLA_DI = MLA_HEADS * 128
MLA_CQ0 = MLA_DI
MLA_CKV0 = MLA_CQ0 + MLA_Q_RANK
MLA_KR0 = MLA_CKV0 + MLA_KV_RANK
MLA_AW = MLA_KR0 + LANES


def _rot_half(x):
    w = x.shape[-1]
    lane = lax.broadcasted_iota(jnp.int32, x.shape, x.ndim - 1)
    return jnp.where(lane % MLA_ROPE < MLA_ROPE // 2, pltpu.roll(x, w - MLA_ROPE // 2, x.ndim - 1),
                     pltpu.roll(x, MLA_ROPE // 2, x.ndim - 1))


def rope_tables(pos):
    l = pos.shape[0]
    tl = ROW_TILE
    j = np.arange(LANES) % MLA_ROPE % (MLA_ROPE // 2)
    inv_freq = (ROPE_THETA ** (-(2.0 * j) / MLA_ROPE)).astype(np.float32).reshape(1, LANES)
    sign = np.where(np.arange(LANES) % MLA_ROPE < MLA_ROPE // 2, -1.0, 1.0).astype(np.float32).reshape(1, LANES)

    def body(p_ref, f_ref, s_ref, cos_ref, sin_ref):
        ang = p_ref[...].astype(F32) * f_ref[...]
        cos_ref[...] = jnp.cos(ang)
        sin_ref[...] = jnp.sin(ang) * s_ref[...]

    st = jax.ShapeDtypeStruct((l, LANES), F32)
    return _rowcall(body, "rope_tables", l // tl, [_rows(tl, 1), _full((1, LANES)), _full((1, LANES))],
                    [_rows(tl, LANES)] * 2, [st, st])(pos, jnp.asarray(inv_freq), jnp.asarray(sign))


def _rope(x, cos, sins):
    return x * cos + _rot_half(x) * sins


def _rope_t(dy, cos, sins):
    return dy * cos - sins * _rot_half(dy)


def _rmsn(x):
    r = lax.rsqrt(jnp.mean(x * x, axis=-1, keepdims=True) + NORM_EPS)
    return x * r, r


def mla_pre(a, q_g, kv_g, cos, sins, name):
    l = a.shape[0]
    tl = ROW_TILE

    def body(a_ref, qg_ref, kg_ref, cos_ref, sin_ref, cq_ref, ckv_ref, krs_ref):
        xq, _ = _rmsn(a_ref[:, MLA_CQ0:MLA_CKV0])
        cq_ref[...] = (xq * qg_ref[...]).astype(BF16)
        xk, _ = _rmsn(a_ref[:, MLA_CKV0:MLA_KR0])
        ckv_ref[...] = (xk * kg_ref[...]).astype(BF16)
        kr = a_ref[:, MLA_KR0:MLA_AW]
        kr2 = kr + pltpu.roll(kr, MLA_ROPE, 1)
        kr2 = _rope(kr2, cos_ref[...], sin_ref[...])
        lane = lax.broadcasted_iota(jnp.int32, kr2.shape, 1)
        krs_ref[0] = jnp.where(lane < MLA_ROPE, kr2, 0.0).astype(BF16)
        krs_ref[1] = jnp.where(lane >= MLA_ROPE, kr2, 0.0).astype(BF16)

    st = jax.ShapeDtypeStruct
    return _rowcall(body, name, l // tl,
                    [_rows(tl, MLA_AW), _full((1, MLA_Q_RANK)), _full((1, MLA_KV_RANK)), _rows(tl, LANES), _rows(tl, LANES)],
                    [_rows(tl, MLA_Q_RANK), _rows(tl, MLA_KV_RANK), pl.BlockSpec((2, tl, LANES), lambda i: (0, i, 0))],
                    [st((l, MLA_Q_RANK), BF16), st((l, MLA_KV_RANK), BF16), st((2, l, LANES), BF16)])(
        a, q_g.reshape(1, -1), kv_g.reshape(1, -1), cos, sins)


def mla_rope_q(qr, cos, sins, name):
    l, w = qr.shape
    tl = ROW_TILE

    def body(q_ref, cos_ref, sin_ref, o_ref):
        c, s = cos_ref[...], sin_ref[...]
        for p in range(w // LANES):
            sl = slice(p * LANES, (p + 1) * LANES)
            o_ref[:, sl] = _rope(q_ref[:, sl], c, s).astype(BF16)

    return _rowcall(body, name, l // tl, [_rows(tl, w), _rows(tl, LANES), _rows(tl, LANES)], _rows(tl, w),
                    jax.ShapeDtypeStruct((l, w), BF16))(qr, cos, sins)


ATT_OUT = 512
ATT_IN = 256
ATT_R = ATT_OUT // ATT_IN


def _scores(qn, qr, kn, kr, mask_off, transposed):
    q2 = jnp.concatenate([qn, qr], axis=1)
    k2 = jnp.concatenate([kn, kr], axis=1)
    s = (_dot(k2, q2, DN_NT) if transposed else _dot(q2, k2, DN_NT)) * MLA_SCALE
    if mask_off is None:
        return s
    r = lax.broadcasted_iota(jnp.int32, s.shape, 0)
    c = lax.broadcasted_iota(jnp.int32, s.shape, 1)
    return jnp.where((r <= c + mask_off) if transposed else (c + mask_off <= r), s, NEG_INF)


def _fold(x):
    return x[:, :LANES], x[:, LANES:]


def flash_fwd(qn, qr, kv, krs, name):
    l = qn.shape[0]
    nq = l // ATT_OUT

    def body(qn_ref, qr_ref, kv_ref, kr_ref, o_ref, lse_ref, s_buf):
        qi = pl.program_id(1)
        q_r = qr_ref[...]
        q_n = [qn_ref[:, hh * LANES:(hh + 1) * LANES] for hh in range(2)]

        def block_scores(j, mx, mask_off):
            sl = pl.ds(pl.multiple_of(j * ATT_IN, ATT_IN), ATT_IN)
            out = []
            for hh in range(2):
                s = _scores(q_n[hh], q_r, kv_ref[sl, 2 * hh * LANES:(2 * hh + 1) * LANES], kr_ref[hh, sl, :],
                            mask_off, False)
                s_buf[hh, j] = s
                lo, hi = _fold(s)
                out.append(jnp.maximum(mx[hh], jnp.maximum(lo, hi)))
            return tuple(out)

        ninf = jnp.full((ATT_OUT, LANES), NEG_INF, F32)
        mx = lax.fori_loop(0, ATT_R * qi, lambda j, c: block_scores(j, c, None), (ninf, ninf))
        for d in range(ATT_R):
            mx = block_scores(ATT_R * qi + d, mx, d * ATT_IN)
        m = [jnp.max(mx[hh], axis=-1, keepdims=True) for hh in range(2)]

        def block_pv(j, carry):
            sl = pl.ds(pl.multiple_of(j * ATT_IN, ATT_IN), ATT_IN)
            out = []
            for hh in range(2):
                ls, acc = carry[hh]
                p = jnp.exp(s_buf[hh, j] - m[hh])
                lo, hi = _fold(p)
                out.append((ls + (lo + hi),
                            acc + _dot(p, kv_ref[sl, (2 * hh + 1) * LANES:(2 * hh + 2) * LANES], DN_NN)))
            return tuple(out)

        z = jnp.zeros((ATT_OUT, LANES), F32)
        res = lax.fori_loop(0, ATT_R * (qi + 1), block_pv, ((z, z), (z, z)))
        for hh in range(2):
            lsum = jnp.sum(res[hh][0], axis=-1, keepdims=True)
            o_ref[:, hh * LANES:(hh + 1) * LANES] = res[hh][1] / lsum
            lse_ref[hh] = m[hh] + jnp.log(lsum)

    st = jax.ShapeDtypeStruct
    return pl.pallas_call(
        body, name=name, grid=(MLA_HEADS // 2, nq),
        in_specs=[pl.BlockSpec((ATT_OUT, 2 * LANES), lambda p, i: (i, p)),
                  pl.BlockSpec((ATT_OUT, LANES), lambda p, i: (i, p)),
                  pl.BlockSpec((l, 4 * LANES), lambda p, i: (0, p)),
                  pl.BlockSpec((2, l, LANES), lambda p, i: (0, 0, 0))],
        out_specs=[pl.BlockSpec((ATT_OUT, 2 * LANES), lambda p, i: (i, p)),
                   pl.BlockSpec((2, ATT_OUT, 1), lambda p, i: (p, i, 0))],
        out_shape=[st((l, MLA_DI), F32), st((MLA_HEADS, l, 1), F32)],
        scratch_shapes=[pltpu.VMEM((2, l // ATT_IN, ATT_OUT, ATT_IN), F32)],
        compiler_params=_cparams(("parallel", "arbitrary")))(qn, qr, kv, krs)


def flash_dkv(qn, qr, kv, krs, do, lse_row, delta_row, name):
    l = qn.shape[0]
    nk = l // ATT_OUT
    nq = l // ATT_IN

    def body(qn_ref, qr_ref, do_ref, lse_ref, dl_ref, kv_ref, kr_ref, dkv_ref, dkr_ref):
        kj = pl.program_id(1)
        lane = lax.broadcasted_iota(jnp.int32, (ATT_OUT, LANES), 1)
        kn = [kv_ref[:, 2 * hh * LANES:(2 * hh + 1) * LANES] for hh in range(2)]
        v = [kv_ref[:, (2 * hh + 1) * LANES:(2 * hh + 2) * LANES] for hh in range(2)]

        def block(i, carry, mask_off):
            sl = pl.ds(pl.multiple_of(i * ATT_IN, ATT_IN), ATT_IN)
            q_r = qr_ref[sl, :]
            out = []
            for hh in range(2):
                dk2, dv = carry[hh]
                hs = slice(hh * LANES, (hh + 1) * LANES)
                q_n, d_o = qn_ref[sl, hs], do_ref[sl, hs]
                s = _scores(q_n, q_r, kn[hh], kr_ref[hh], mask_off, True)
                pt = jnp.exp(s - lse_ref[hh, i])
                dv = dv + _dot(pt, d_o, DN_NN)
                dpt = _dot(v[hh], d_o, DN_NT)
                dst = (pt * (dpt - dl_ref[hh, i]) * MLA_SCALE).astype(BF16)
                out.append((dk2 + _dot(dst, jnp.concatenate([q_n, q_r], axis=1), DN_NN), dv))
            return tuple(out)

        z = jnp.zeros((ATT_OUT, LANES), F32)
        z2 = jnp.zeros((ATT_OUT, 2 * LANES), F32)
        res = ((z2, z), (z2, z))
        for d in range(ATT_R):
            res = block(ATT_R * kj + d, res, d * ATT_IN)
        res = lax.fori_loop(ATT_R * (kj + 1), nq, lambda i, c: block(i, c, None), res)
        for hh in range(2):
            dkv_ref[:, 2 * hh * LANES:(2 * hh + 1) * LANES] = res[hh][0][:, :LANES].astype(BF16)
            dkv_ref[:, (2 * hh + 1) * LANES:(2 * hh + 2) * LANES] = res[hh][1].astype(BF16)
        dkr_ref[0] = jnp.where(lane < MLA_ROPE, res[0][0][:, LANES:], res[1][0][:, LANES:])

    st = jax.ShapeDtypeStruct
    return pl.pallas_call(
        body, name=name, grid=(MLA_HEADS // 2, nk),
        in_specs=[pl.BlockSpec((l, 2 * LANES), lambda p, j: (0, p)),
                  pl.BlockSpec((l, LANES), lambda p, j: (0, p)),
                  pl.BlockSpec((l, 2 * LANES), lambda p, j: (0, p)),
                  pl.BlockSpec((2, nq, 1, ATT_IN), lambda p, j: (p, 0, 0, 0)),
                  pl.BlockSpec((2, nq, 1, ATT_IN), lambda p, j: (p, 0, 0, 0)),
                  pl.BlockSpec((ATT_OUT, 4 * LANES), lambda p, j: (j, p)),
                  pl.BlockSpec((2, ATT_OUT, LANES), lambda p, j: (0, j, 0))],
        out_specs=[pl.BlockSpec((ATT_OUT, 4 * LANES), lambda p, j: (j, p)),
                   pl.BlockSpec((1, ATT_OUT, LANES), lambda p, j: (p, j, 0))],
        out_shape=[st((l, 2 * MLA_DI), BF16), st((MLA_HEADS // 2, l, LANES), F32)],
        compiler_params=_cparams(("parallel", "arbitrary")))(qn, qr, do, lse_row, delta_row, kv, krs)


def flash_dq(qn, qr, kv, krs, do, lse, delta, cos, sins, name):
    l = qn.shape[0]
    nq = l // ATT_OUT

    def body(qn_ref, qr_ref, do_ref, lse_ref, dl_ref, kv_ref, kr_ref, cos_ref, sin_ref, dqn_ref, dqr_ref):
        qi = pl.program_id(1)
        q_r = qr_ref[...]
        q_n = [qn_ref[:, hh * LANES:(hh + 1) * LANES] for hh in range(2)]
        d_o = [do_ref[:, hh * LANES:(hh + 1) * LANES] for hh in range(2)]
        lse_h = [lse_ref[hh] for hh in range(2)]
        dl_h = [dl_ref[hh] for hh in range(2)]

        def block(j, carry, mask_off):
            sl = pl.ds(pl.multiple_of(j * ATT_IN, ATT_IN), ATT_IN)
            dq2 = list(carry)
            for hh in range(2):
                kn = kv_ref[sl, 2 * hh * LANES:(2 * hh + 1) * LANES]
                v = kv_ref[sl, (2 * hh + 1) * LANES:(2 * hh + 2) * LANES]
                kr = kr_ref[hh, sl, :]
                s = _scores(q_n[hh], q_r, kn, kr, mask_off, False)
                pr = jnp.exp(s - lse_h[hh])
                dp = _dot(d_o[hh], v, DN_NT)
                ds = (pr * (dp - dl_h[hh]) * MLA_SCALE).astype(BF16)
                dq2[hh] = dq2[hh] + _dot(ds, jnp.concatenate([kn, kr], axis=1), DN_NN)
            return tuple(dq2)

        z2 = jnp.zeros((ATT_OUT, 2 * LANES), F32)
        res = lax.fori_loop(0, ATT_R * qi, lambda j, c: block(j, c, None), (z2, z2))
        for d in range(ATT_R):
            res = block(ATT_R * qi + d, res, d * ATT_IN)
        dqn_ref[:, 0:LANES] = res[0][:, :LANES].astype(BF16)
        dqn_ref[:, LANES:2 * LANES] = res[1][:, :LANES].astype(BF16)
        dqr = res[0][:, LANES:] + res[1][:, LANES:]
        dqr_ref[...] = _rope_t(dqr, cos_ref[...], sin_ref[...]).astype(BF16)

    st = jax.ShapeDtypeStruct
    return pl.pallas_call(
        body, name=name, grid=(MLA_HEADS // 2, nq),
        in_specs=[pl.BlockSpec((ATT_OUT, 2 * LANES), lambda p, i: (i, p)),
                  pl.BlockSpec((ATT_OUT, LANES), lambda p, i: (i, p)),
                  pl.BlockSpec((ATT_OUT, 2 * LANES), lambda p, i: (i, p)),
                  pl.BlockSpec((2, ATT_OUT, 1), lambda p, i: (p, i, 0)),
                  pl.BlockSpec((2, ATT_OUT, 1), lambda p, i: (p, i, 0)),
                  pl.BlockSpec((l, 4 * LANES), lambda p, i: (0, p)),
                  pl.BlockSpec((2, l, LANES), lambda p, i: (0, 0, 0)),
                  pl.BlockSpec((ATT_OUT, LANES), lambda p, i: (i, 0)),
                  pl.BlockSpec((ATT_OUT, LANES), lambda p, i: (i, 0))],
        out_specs=[pl.BlockSpec((ATT_OUT, 2 * LANES), lambda p, i: (i, p)),
                   pl.BlockSpec((ATT_OUT, LANES), lambda p, i: (i, p))],
        out_shape=[st((l, MLA_DI), BF16), st((l, MLA_HEADS * MLA_ROPE), BF16)],
        compiler_params=_cparams(("parallel", "arbitrary")))(qn, qr, do, lse, delta, kv, krs, cos, sins)


def mla_gate_fwd(o, a, name):
    l = o.shape[0]
    tl = ROW_TILE

    def body(o_ref, z_ref, m_ref):
        m_ref[...] = (o_ref[...] * _silu(z_ref[...])).astype(BF16)

    return _rowcall(body, name, l // tl, [_rows(tl, MLA_DI), _rows(tl, MLA_DI)], _rows(tl, MLA_DI),
                    jax.ShapeDtypeStruct((l, MLA_DI), BF16))(o, a)


def mla_gate_bwd(dm, o, a, name):
    l = o.shape[0]
    tl = ROW_TILE

    def body(dm_ref, o_ref, z_ref, do_ref, dz_ref, dl_ref):
        dmv, ov, z = dm_ref[...], o_ref[...], z_ref[...]
        d_o = dmv * _silu(z)
        do_ref[...] = d_o.astype(BF16)
        dz_ref[...] = (dmv * ov * _silu_grad(z)).astype(BF16)
        pr = d_o * ov
        for h in range(MLA_HEADS):
            dl_ref[h] = jnp.sum(pr[:, h * LANES:(h + 1) * LANES], axis=1, keepdims=True)

    st = jax.ShapeDtypeStruct
    return _rowcall(body, name, l // tl, [_rows(tl, MLA_DI)] * 3,
                    [_rows(tl, MLA_DI), _rows(tl, MLA_DI), pl.BlockSpec((MLA_HEADS, tl, 1), lambda i: (0, i, 0))],
                    [st((l, MLA_DI), BF16), st((l, MLA_DI), BF16), st((MLA_HEADS, l, 1), F32)])(dm, o, a)


def mla_post(a, dcqn, dckvn, dkr_pairs, dz, q_g, kv_g, cos, sins, name):
    l = a.shape[0]
    tl = ROW_TILE
    npair = MLA_HEADS // 2

    def norm_bwd(x, g, dy):
        xhat, r = _rmsn(x)
        dxh = dy * g
        return r * (dxh - xhat * jnp.mean(dxh * xhat, axis=-1, keepdims=True)), jnp.sum(dy * xhat, axis=0, keepdims=True)

    def body(a_ref, dq_ref, dk_ref, dkr_ref, dz_ref, qg_ref, kg_ref, cos_ref, sin_ref, da_ref, dqg_ref, dkg_ref):
        i = pl.program_id(0)
        da_ref[:, 0:MLA_DI] = dz_ref[...]
        dcq, dqg = norm_bwd(a_ref[:, MLA_CQ0:MLA_CKV0], qg_ref[...], dq_ref[...])
        da_ref[:, MLA_CQ0:MLA_CKV0] = dcq.astype(BF16)
        dckv, dkg = norm_bwd(a_ref[:, MLA_CKV0:MLA_KR0], kg_ref[...], dk_ref[...])
        da_ref[:, MLA_CKV0:MLA_KR0] = dckv.astype(BF16)
        dk2 = dkr_ref[0]
        for p in range(1, npair):
            dk2 = dk2 + dkr_ref[p]
        dk2 = _rope_t(dk2, cos_ref[...], sin_ref[...])
        dk2 = dk2 + pltpu.roll(dk2, MLA_ROPE, 1)
        lane = lax.broadcasted_iota(jnp.int32, dk2.shape, 1)
        da_ref[:, MLA_KR0:MLA_AW] = jnp.where(lane < MLA_ROPE, dk2, 0.0).astype(BF16)
        _acc(dqg_ref, dqg, i)
        _acc(dkg_ref, dkg, i)

    st = jax.ShapeDtypeStruct
    return _rowcall(body, name, l // tl,
                    [_rows(tl, MLA_AW), _rows(tl, MLA_Q_RANK), _rows(tl, MLA_KV_RANK),
                     pl.BlockSpec((npair, tl, LANES), lambda i: (0, i, 0)), _rows(tl, MLA_DI),
                     _full((1, MLA_Q_RANK)), _full((1, MLA_KV_RANK)), _rows(tl, LANES), _rows(tl, LANES)],
                    [_rows(tl, MLA_AW), _full((1, MLA_Q_RANK)), _full((1, MLA_KV_RANK))],
                    [st((l, MLA_AW), BF16), st((1, MLA_Q_RANK), F32), st((1, MLA_KV_RANK), F32)])(
        a, dcqn, dckvn, dkr_pairs, dz, q_g.reshape(1, -1), kv_g.reshape(1, -1), cos, sins)


def _mla_w_in_perm(w):
    r = MLA_Q_RANK + MLA_KV_RANK + MLA_ROPE
    pad = jnp.zeros(w.shape[:-1] + (MLA_AW - MLA_KR0 - MLA_ROPE,), w.dtype)
    return jnp.concatenate([w[..., r:], w[..., :r], pad], axis=-1)


def _mla_w_in_unperm(g):
    r = MLA_Q_RANK + MLA_KV_RANK + MLA_ROPE
    return jnp.concatenate([g[..., MLA_DI:MLA_DI + r], g[..., :MLA_DI]], axis=-1)


def _mla_w_uq_split(w):
    k = w.shape[0]
    w3 = w.reshape(k, MLA_HEADS, MLA_NOPE + MLA_ROPE)
    return w3[:, :, :MLA_NOPE].reshape(k, MLA_HEADS * MLA_NOPE), w3[:, :, MLA_NOPE:].reshape(k, MLA_HEADS * MLA_ROPE)


def _mla_w_uq_merge(gn, gr):
    k = gn.shape[0]
    return jnp.concatenate([gn.reshape(k, MLA_HEADS, MLA_NOPE), gr.reshape(k, MLA_HEADS, MLA_ROPE)], axis=2).reshape(k, -1)


def mla_layer_fwd(h, p, wf, cos, sins, tag):
    hn = rms_fwd(h, p["norm_g"], tag + "_rms")
    w_in = _mla_w_in_perm(wf["w_in"])
    w_uq_n, w_uq_r = _mla_w_uq_split(wf["w_uq"])
    a = matmul(hn, w_in, "nn", tag + "_mm_in")
    cqn, ckvn, krs = mla_pre(a, p["q_norm_g"], p["kv_norm_g"], cos, sins, tag + "_pre")
    qn = matmul(cqn, w_uq_n, "nn", tag + "_mm_qn", out_dtype=BF16)
    qr_raw = matmul(cqn, w_uq_r, "nn", tag + "_mm_qr")
    qr = mla_rope_q(qr_raw, cos, sins, tag + "_rope_q")
    kv = matmul(ckvn, wf["w_ukv"], "nn", tag + "_mm_kv", out_dtype=BF16)
    o, lse = flash_fwd(qn, qr, kv, krs, tag + "_flash")
    m = mla_gate_fwd(o, a, tag + "_gate")
    h_out = matmul(m, wf["w_out"], "nn", tag + "_mm_out", add=h)
    return h_out, (h, hn, a, cqn, ckvn, krs, qn, qr, kv, o, lse, m, w_in, w_uq_n, w_uq_r)


def mla_layer_bwd(dh_out, saved, p, wf, cos, sins, tag):
    h, hn, a, cqn, ckvn, krs, qn, qr, kv, o, lse, m, w_in, w_uq_n, w_uq_r = saved
    l = h.shape[0]
    dm = matmul(dh_out, wf["w_out"], "nt", tag + "_mm_dm")
    g_w_out = matmul(m, dh_out, "tn", tag + "_mm_gwout")
    do, dz, delta = mla_gate_bwd(dm, o, a, tag + "_gate_bwd")
    lse_row = lse.reshape(MLA_HEADS, l // ATT_IN, 1, ATT_IN)
    delta_row = delta.reshape(MLA_HEADS, l // ATT_IN, 1, ATT_IN)
    dkv, dkr_pairs = flash_dkv(qn, qr, kv, krs, do, lse_row, delta_row, tag + "_flash_dkv")
    dqn, dqr = flash_dq(qn, qr, kv, krs, do, lse, delta, cos, sins, tag + "_flash_dq")
    dcqn = matmul(dqn, w_uq_n, "nt", tag + "_mm_dcq_n")
    dcqn = matmul(dqr, w_uq_r, "nt", tag + "_mm_dcq_r", add=dcqn)
    g_uq_n = matmul(cqn, dqn, "tn", tag + "_mm_guq_n")
    g_uq_r = matmul(cqn, dqr, "tn", tag + "_mm_guq_r")
    dckvn = matmul(dkv, wf["w_ukv"], "nt", tag + "_mm_dckv")
    g_ukv = matmul(ckvn, dkv, "tn", tag + "_mm_gukv")
    da, dqg, dkg = mla_post(a, dcqn, dckvn, dkr_pairs, dz, p["q_norm_g"], p["kv_norm_g"], cos, sins, tag + "_post")
    dhn = matmul(da, w_in, "nt", tag + "_mm_dhn")
    g_w_in = matmul(hn, da, "tn", tag + "_mm_gwin")
    dh, dng = rms_bwd(h, p["norm_g"], dhn, dh_out, tag + "_rms_bwd")
    grads = {"norm_g": dng.reshape(-1), "w_in": _mla_w_in_unperm(g_w_in), "q_norm_g": dqg.reshape(-1),
             "w_uq": _mla_w_uq_merge(g_uq_n, g_uq_r), "kv_norm_g": dkg.reshape(-1), "w_ukv": g_ukv, "w_out": g_w_out}
    return dh, grads


ANY = pl.BlockSpec(memory_space=pl.ANY)


def _me():
    return lax.axis_index("x"), lax.axis_index("y"), lax.axis_index("c")


def _chip():
    return 2 * lax.axis_index("x") + lax.axis_index("y")


def _other_chips(x, y):
    return [(1 - x, y), (x, 1 - y), (1 - x, 1 - y)]


def _rcopy(src, dst, ssem, rsem, dev):
    return pltpu.make_async_remote_copy(src_ref=src, dst_ref=dst, send_sem=ssem, recv_sem=rsem,
                                        device_id=dev, device_id_type=MESH)


def _half(ref, c, hf):
    return ref.at[pl.ds(c * hf, hf), :]


def weights_allgather(wb):
    nr, w = wb.shape
    hf = nr // 2

    def body(w_ref, o_ref, ssem, rsem):
        x, y, c = _me()
        k = 2 * x + y
        chips = _other_chips(x, y)
        first = [_rcopy(_half(w_ref, c, hf), _half(o_ref.at[k], c, hf), ssem.at[j], rsem.at[j], (cx, cy, c))
                 for j, (cx, cy) in enumerate(chips)]
        for cp in first:
            cp.start()
        passed = []
        for j, (cx, cy) in enumerate(chips):
            region = _half(o_ref.at[2 * cx + cy], c, hf)
            _rcopy(region, region, ssem.at[j], rsem.at[j], (cx, cy, c)).wait_recv()
            fwd = _rcopy(region, region, ssem.at[3 + j], rsem.at[3 + j], (x, y, 1 - c))
            fwd.start()
            passed.append(fwd)
        for j, (cx, cy) in enumerate(chips):
            region = _half(o_ref.at[2 * cx + cy], 1 - c, hf)
            _rcopy(region, region, ssem.at[3 + j], rsem.at[3 + j], (x, y, 1 - c)).wait_recv()
        for cp in first + passed:
            cp.wait_send()

    out = pl.pallas_call(
        body, name="weights_allgather", in_specs=[ANY], out_specs=ANY,
        out_shape=jax.ShapeDtypeStruct((N_CHIPS, nr, w), wb.dtype),
        scratch_shapes=[pltpu.SemaphoreType.DMA((6,)), pltpu.SemaphoreType.DMA((6,))],
    )(wb)
    return lax.dynamic_update_slice(out, wb[None], (_chip(), 0, 0))


def grads_to_sibling(ps):
    n = len(ps)

    def body(*refs):
        p_refs, o_refs, ssem, rsem = refs[:n], refs[n:2 * n], refs[2 * n], refs[2 * n + 1]
        x, y, c = _me()
        cps = []
        for a in range(n):
            hf = ps[a].shape[1] // 2
            cps.append(_rcopy(p_refs[a].at[:, pl.ds((1 - c) * hf, hf), :], o_refs[a], ssem.at[a], rsem.at[a],
                              (x, y, 1 - c)))
        for cp in cps:
            cp.start()
        for cp in cps:
            cp.wait()

    return pl.pallas_call(
        body, name="grads_to_sibling", in_specs=[ANY] * n, out_specs=[ANY] * n,
        out_shape=[jax.ShapeDtypeStruct((N_CHIPS, p.shape[1] // 2, p.shape[2]), p.dtype) for p in ps],
        scratch_shapes=[pltpu.SemaphoreType.DMA((n,)), pltpu.SemaphoreType.DMA((n,))])(*ps)


def pair_sum(p, ra, out_dtype, name):
    _, nr, w = p.shape
    hf = nr // 2
    tr = _pick_rows(hf)
    nb = hf // tr

    def body(c_ref, p_ref, r_ref, o_ref):
        o_ref[...] = (p_ref[...] + r_ref[...]).astype(out_dtype)

    c = lax.axis_index("c").astype(jnp.int32).reshape(1)
    return pl.pallas_call(
        body, name=name,
        grid_spec=pltpu.PrefetchScalarGridSpec(
            num_scalar_prefetch=1, grid=(N_CHIPS, nb),
            in_specs=[pl.BlockSpec((1, tr, w), lambda k, i, c_ref: (k, c_ref[0] * nb + i, 0)),
                      pl.BlockSpec((1, tr, w), lambda k, i, c_ref: (k, i, 0))],
            out_specs=pl.BlockSpec((1, tr, w), lambda k, i, c_ref: (k, i, 0))),
        out_shape=jax.ShapeDtypeStruct((N_CHIPS, hf, w), out_dtype),
        compiler_params=_cparams(("parallel", "parallel")))(c, p, ra)


def grads_across_chips(ts):
    n = len(ts)

    def body(*refs):
        t_refs, o_refs, ssem, rsem = refs[:n], refs[n:2 * n], refs[2 * n], refs[2 * n + 1]
        x, y, c = _me()
        k = 2 * x + y
        chips = _other_chips(x, y)
        sends = [_rcopy(t_refs[a].at[2 * cx + cy], o_refs[a].at[k], ssem.at[3 * a + j], rsem.at[3 * a + j], (cx, cy, c))
                 for a in range(n) for j, (cx, cy) in enumerate(chips)]
        for cp in sends:
            cp.start()
        for a in range(n):
            for j, (cx, cy) in enumerate(chips):
                _rcopy(t_refs[a].at[k], o_refs[a].at[2 * cx + cy], ssem.at[3 * a + j], rsem.at[3 * a + j],
                       (cx, cy, c)).wait_recv()
        for cp in sends:
            cp.wait_send()

    return pl.pallas_call(
        body, name="grads_across_chips", in_specs=[ANY] * n, out_specs=[ANY] * n,
        out_shape=[jax.ShapeDtypeStruct(t.shape, t.dtype) for t in ts],
        scratch_shapes=[pltpu.SemaphoreType.DMA((3 * n,)), pltpu.SemaphoreType.DMA((3 * n,))])(*ts)


def chip_sum(t, rb, name):
    _, hf, w = rb.shape
    tr = _pick_rows(hf)
    nb = hf // tr

    def body(kc_ref, t_ref, r_ref, o_ref):
        k = kc_ref[0]
        acc = jnp.where(k == 0, t_ref[0], r_ref[0]).astype(F32)
        for j in range(1, N_CHIPS):
            acc = acc + jnp.where(k == j, t_ref[0], r_ref[j]).astype(F32)
        o_ref[...] = acc

    kc = jnp.stack([_chip(), lax.axis_index("c")]).astype(jnp.int32)
    return pl.pallas_call(
        body, name=name,
        grid_spec=pltpu.PrefetchScalarGridSpec(
            num_scalar_prefetch=1, grid=(nb,),
            in_specs=[pl.BlockSpec((1, tr, w), lambda i, kc_ref: (kc_ref[0], i, 0)),
                      pl.BlockSpec((N_CHIPS, tr, w), lambda i, kc_ref: (0, i, 0))],
            out_specs=pl.BlockSpec((tr, w), lambda i, kc_ref: (kc_ref[1] * nb + i, 0))),
        out_shape=jax.ShapeDtypeStruct((2 * hf, w), F32), compiler_params=_cparams(("parallel",)))(kc, t, rb)


def reduced_to_sibling(gs):
    n = len(gs)

    def body(*refs):
        o_refs, ssem, rsem = refs[n:2 * n], refs[2 * n], refs[2 * n + 1]
        x, y, c = _me()
        cps = []
        for a in range(n):
            hf = gs[a].shape[0] // 2
            cps.append(_rcopy(_half(o_refs[a], c, hf), _half(o_refs[a], c, hf), ssem.at[a], rsem.at[a], (x, y, 1 - c)))
        for cp in cps:
            cp.start()
        for a in range(n):
            hf = gs[a].shape[0] // 2
            _rcopy(_half(o_refs[a], c, hf), _half(o_refs[a], 1 - c, hf), ssem.at[a], rsem.at[a],
                   (x, y, 1 - c)).wait_recv()
        for cp in cps:
            cp.wait_send()

    return pl.pallas_call(
        body, name="reduced_to_sibling", in_specs=[ANY] * n, out_specs=[ANY] * n,
        input_output_aliases={a: a for a in range(n)},
        out_shape=[jax.ShapeDtypeStruct(g.shape, g.dtype) for g in gs],
        scratch_shapes=[pltpu.SemaphoreType.DMA((n,)), pltpu.SemaphoreType.DMA((n,))])(*gs)


def small_allgather(g, row0, nrs):
    w = g.shape[1]

    def body(g_ref, o_ref, ssem, rsem):
        x, y, c = _me()
        k = 2 * x + y
        chips = _other_chips(x, y)
        src = g_ref.at[pl.ds(row0, nrs), :]
        sends = [_rcopy(src, o_ref.at[k], ssem.at[j], rsem.at[j], (cx, cy, c)) for j, (cx, cy) in enumerate(chips)]
        for cp in sends:
            cp.start()
        for j, (cx, cy) in enumerate(chips):
            _rcopy(src, o_ref.at[2 * cx + cy], ssem.at[j], rsem.at[j], (cx, cy, c)).wait_recv()
        for cp in sends:
            cp.wait_send()

    out = pl.pallas_call(
        body, name="small_allgather", in_specs=[ANY], out_specs=ANY,
        out_shape=jax.ShapeDtypeStruct((N_CHIPS, nrs, w), g.dtype),
        scratch_shapes=[pltpu.SemaphoreType.DMA((3,)), pltpu.SemaphoreType.DMA((3,))])(g)
    return lax.dynamic_update_slice(out, g[row0:row0 + nrs][None], (_chip(), 0, 0))


def adamw(w, g, m, v, name):
    r, wd = w.shape
    tr = _pick_rows(r, cap=max(16, ADAMW_BLOCK_BYTES // (4 * wd)))
    bc1 = 1.0 - ADAM_B1 ** ADAM_STEP
    bc2 = 1.0 - ADAM_B2 ** ADAM_STEP

    def body(w_ref, g_ref, m_ref, v_ref, d_ref, nm_ref, nv_ref):
        gv = g_ref[...]
        nm = ADAM_B1 * m_ref[...] + (1.0 - ADAM_B1) * gv
        nv = ADAM_B2 * v_ref[...] + (1.0 - ADAM_B2) * (gv * gv)
        nm_ref[...] = nm
        nv_ref[...] = nv
        d_ref[...] = -ADAM_LR * ((nm / bc1) / (jnp.sqrt(nv / bc2) + ADAM_EPS) + ADAM_WD * w_ref[...])

    spec = pl.BlockSpec((tr, wd), lambda i: (i, 0))
    st = jax.ShapeDtypeStruct((r, wd), F32)
    return pl.pallas_call(body, name=name, grid=(r // tr,), in_specs=[spec] * 4, out_specs=[spec] * 3,
                          out_shape=[st, st, st], compiler_params=_cparams(("parallel",)))(w, g, m, v)


LAYER_KINDS = ("gmlp", "s5", "mla", "gmlp")
PARAMS = {
    "gmlp": ("norm_g", "w_in", "ln_g", "ln_b", "w_s", "b_s", "w_out"),
    "s5": ("norm_g", "w_in", "a_re", "a_im", "log_step", "b_re", "b_im", "c_re", "c_im", "d_skip", "w_glu", "b_glu", "w_out"),
    "mla": ("norm_g", "w_in", "q_norm_g", "w_uq", "kv_norm_g", "w_ukv", "w_out"),
}
COL_SHARDED = ("w_in", "w_uq", "w_ukv")
ROW_SHARDED = ("w_out", "w_glu")
WEIGHT_NAMES = [("l%d_" % i) + n for i, kind in enumerate(LAYER_KINDS) for n in PARAMS[kind]] + ["final_norm_g"]


def _is_big(name):
    return name.split("_", 1)[1] in COL_SHARDED + ROW_SHARDED


BIG = [n for n in WEIGHT_NAMES if _is_big(n)]
SMALL = [n for n in WEIGHT_NAMES if not _is_big(n)]


def _pack_rows(blocks):
    return jnp.concatenate([b.reshape(-1, PACK_W) for b in blocks], axis=0)


def _shard_major(name, full):
    r, c = full.shape
    if name.split("_", 1)[1] in COL_SHARDED:
        t = full.reshape(r, N_CHIPS, c // N_CHIPS).transpose(1, 0, 2)
    else:
        t = full.reshape(N_CHIPS, r // N_CHIPS, c)
    return t.reshape(N_CHIPS, -1, PACK_W)


def _from_shard_major(name, t, block_shape):
    r, c = block_shape
    if name.split("_", 1)[1] in COL_SHARDED:
        return t.reshape(N_CHIPS, r, c).transpose(1, 0, 2).reshape(r, N_CHIPS * c)
    return t.reshape(N_CHIPS * r, c)


def _small_pack(arrs, total_padded):
    flat = jnp.concatenate([a.reshape(-1) for a in arrs])
    return jnp.pad(flat, (0, total_padded - flat.shape[0]))


def kernel(x, positions, l0_norm_g, l0_w_in, l0_ln_g, l0_ln_b, l0_w_s, l0_b_s, l0_w_out, l1_norm_g, l1_w_in, l1_a_re, l1_a_im, l1_log_step, l1_b_re, l1_b_im, l1_c_re, l1_c_im, l1_d_skip, l1_w_glu, l1_b_glu, l1_w_out, l2_norm_g, l2_w_in, l2_q_norm_g, l2_w_uq, l2_kv_norm_g, l2_w_ukv, l2_w_out, l3_norm_g, l3_w_in, l3_ln_g, l3_ln_b, l3_w_s, l3_b_s, l3_w_out, final_norm_g, loss_target, m_l0_norm_g, m_l0_w_in, m_l0_ln_g, m_l0_ln_b, m_l0_w_s, m_l0_b_s, m_l0_w_out, m_l1_norm_g, m_l1_w_in, m_l1_a_re, m_l1_a_im, m_l1_log_step, m_l1_b_re, m_l1_b_im, m_l1_c_re, m_l1_c_im, m_l1_d_skip, m_l1_w_glu, m_l1_b_glu, m_l1_w_out, m_l2_norm_g, m_l2_w_in, m_l2_q_norm_g, m_l2_w_uq, m_l2_kv_norm_g, m_l2_w_ukv, m_l2_w_out, m_l3_norm_g, m_l3_w_in, m_l3_ln_g, m_l3_ln_b, m_l3_w_s, m_l3_b_s, m_l3_w_out, m_final_norm_g, v_l0_norm_g, v_l0_w_in, v_l0_ln_g, v_l0_ln_b, v_l0_w_s, v_l0_b_s, v_l0_w_out, v_l1_norm_g, v_l1_w_in, v_l1_a_re, v_l1_a_im, v_l1_log_step, v_l1_b_re, v_l1_b_im, v_l1_c_re, v_l1_c_im, v_l1_d_skip, v_l1_w_glu, v_l1_b_glu, v_l1_w_out, v_l2_norm_g, v_l2_w_in, v_l2_q_norm_g, v_l2_w_uq, v_l2_kv_norm_g, v_l2_w_ukv, v_l2_w_out, v_l3_norm_g, v_l3_w_in, v_l3_ln_g, v_l3_ln_b, v_l3_w_s, v_l3_b_s, v_l3_w_out, v_final_norm_g):
    args = locals()
    w = {n: args[n] for n in WEIGHT_NAMES}
    mom_m = {n: args["m_" + n] for n in WEIGHT_NAMES}
    mom_v = {n: args["v_" + n] for n in WEIGHT_NAMES}
    h0 = x[0]
    target = loss_target[0]
    pos = positions.reshape(-1, 1)

    big_rows = [w[n].size // PACK_W for n in BIG]
    nrb = sum(big_rows)
    nrb_pad = -(-nrb // PACK_ROW_ALIGN) * PACK_ROW_ALIGN
    gathered = weights_allgather(_pack_rows([w[n].astype(BF16) for n in BIG]
                                            + [jnp.zeros((nrb_pad - nrb, PACK_W), BF16)]))
    full = {}
    r0 = 0
    for n, nr in zip(BIG, big_rows):
        full[n] = _from_shard_major(n, gathered[:, r0:r0 + nr, :], w[n].shape)
        r0 += nr

    def layer_params(i):
        pre = "l%d_" % i
        p = {k[len(pre):]: v for k, v in w.items() if k.startswith(pre)}
        wf = {k[len(pre):]: v for k, v in full.items() if k.startswith(pre)}
        return p, wf

    cos, sins = rope_tables(pos)
    h = h0
    saved = []
    for i, kind in enumerate(LAYER_KINDS):
        p, wf = layer_params(i)
        tag = "l%d" % i
        if kind == "gmlp":
            h, s = gmlp_layer_fwd(h, p, wf, tag)
        elif kind == "s5":
            h, s = s5_layer_fwd(h, p, wf, tag)
        else:
            h, s = mla_layer_fwd(h, p, wf, cos, sins, tag)
        saved.append(s)
    loss_part, dh, g_final = loss_head(h, final_norm_g, target)
    loss = lax.psum(loss_part[0, 0], ("x", "y", "c"))

    grads = {"final_norm_g": g_final.reshape(-1)}
    for i in reversed(range(len(LAYER_KINDS))):
        kind = LAYER_KINDS[i]
        p, wf = layer_params(i)
        tag = "l%d" % i
        if kind == "gmlp":
            dh, g = gmlp_layer_bwd(dh, saved[i], p, wf, tag)
        elif kind == "s5":
            dh, g = s5_layer_bwd(dh, saved[i], p, wf, tag)
        else:
            dh, g = mla_layer_bwd(dh, saved[i], p, wf, cos, sins, tag)
        for k, val in g.items():
            grads["l%d_%s" % (i, k)] = val
    grad_x = dh[None]

    n_small = sum(w[n].size for n in SMALL)
    piece = N_CHIPS * 2 * 16 * PACK_W
    n_small_pad = -(-n_small // piece) * piece
    nrs = n_small_pad // N_CHIPS // PACK_W
    p_big = jnp.concatenate([_shard_major(n, grads[n]) for n in BIG]
                            + [jnp.zeros((N_CHIPS, nrb_pad - nrb, PACK_W), F32)], axis=1)
    p_small = _small_pack([grads[n] for n in SMALL], n_small_pad).reshape(N_CHIPS, nrs, PACK_W)
    sib_big, sib_small = grads_to_sibling([p_big, p_small])
    t_big = pair_sum(p_big, sib_big, BF16, "pair_sum_big")
    t_small = pair_sum(p_small, sib_small, F32, "pair_sum_small")
    rb_big, rb_small = grads_across_chips([t_big, t_small])
    g_big, red_small = reduced_to_sibling([chip_sum(t_big, rb_big, "chip_sum_big"),
                                           chip_sum(t_small, rb_small, "chip_sum_small")])
    small_all = small_allgather(red_small, 0, nrs)

    g_out, d_out, nm_out, nv_out = {}, {}, {}, {}
    r0 = 0
    for n, nr in zip(BIG, big_rows):
        g_out[n] = g_big[r0:r0 + nr].reshape(w[n].shape)
        d_out[n], nm_out[n], nv_out[n] = adamw(w[n], g_out[n], mom_m[n], mom_v[n], "adamw_" + n)
        r0 += nr
    g_small = small_all.reshape(-1, PACK_W)
    sp = lambda d: _small_pack([d[n] for n in SMALL], n_small_pad).reshape(-1, PACK_W)
    d_small, nm_small, nv_small = adamw(sp(w), g_small, sp(mom_m), sp(mom_v), "adamw_small")
    for buf, out in ((g_small, g_out), (d_small, d_out), (nm_small, nm_out), (nv_small, nv_out)):
        flat = buf.reshape(-1)
        o = 0
        for n in SMALL:
            out[n] = flat[o:o + w[n].size].reshape(w[n].shape)
            o += w[n].size
    return (loss, grad_x, *[g_out[n] for n in WEIGHT_NAMES], *[d_out[n] for n in WEIGHT_NAMES],
            *[nm_out[n] for n in WEIGHT_NAMES], *[nv_out[n] for n in WEIGHT_NAMES])
```

```python
import functools
import math

import jax
import jax.numpy as jnp
import numpy as np
from jax import lax
from jax.experimental import pallas as pl
from jax.experimental.pallas import tpu as pltpu

F32 = jnp.float32
BF16 = jnp.bfloat16
MESH = pl.DeviceIdType.MESH
VMEM_LIMIT_BYTES = 56 * 1024 * 1024
LANES = 128
PACK_W = 1024
PACK_ROW_ALIGN = 256
ROW_TILE = 256
ADAMW_BLOCK_BYTES = 1024 * 1024
MM_BLOCK_BYTES = 6 * 1024 * 1024

NORM_EPS = 1e-6
N_CHIPS = 4
GMLP_CHUNK = 128
GMLP_GROUPS = 8
S5_GROUPS = 128
S5_GROUP = 16
S5_STATE = 64
S5_SB = 16
S5_SEG = 8
MLA_HEADS = 16
MLA_NOPE = 128
MLA_ROPE = 64
MLA_Q_RANK = 384
MLA_KV_RANK = 128
MLA_SCALE = (MLA_NOPE + MLA_ROPE) ** -0.5
ROPE_THETA = 10000.0
NEG_INF = -1e30
ADAM_LR, ADAM_B1, ADAM_B2, ADAM_EPS, ADAM_WD, ADAM_STEP = 0.001, 0.9, 0.999, 1e-08, 0.01, 10

DN_NN = (((1,), (0,)), ((), ()))
DN_NT = (((1,), (1,)), ((), ()))
DN_TN = (((0,), (0,)), ((), ()))


def _cparams(sem):
    return pltpu.CompilerParams(dimension_semantics=sem, vmem_limit_bytes=VMEM_LIMIT_BYTES)


def _pick(n, cands=(512, 384, 256, 128)):
    for c in cands:
        if n % c == 0:
            return c
    return n


def _pick_rows(r, cap=512, mult=16):
    return max(t for t in range(mult, cap + 1, mult) if r % t == 0)


def _dot(a, b, dn):
    return lax.dot_general(a.astype(BF16), b.astype(BF16), dn, preferred_element_type=F32)


def _sigmoid(x):
    return 1.0 / (1.0 + jnp.exp(-x))


def _gelu(x):
    c = math.sqrt(2.0 / math.pi)
    t = jnp.tanh(c * (x + 0.044715 * x * x * x))
    return 0.5 * x * (1.0 + t)


def _gelu_grad(x):
    c = math.sqrt(2.0 / math.pi)
    t = jnp.tanh(c * (x + 0.044715 * x * x * x))
    return 0.5 * (1.0 + t) + 0.5 * x * (1.0 - t * t) * c * (1.0 + 3.0 * 0.044715 * x * x)


def _gelu_both(x):
    c = math.sqrt(2.0 / math.pi)
    t = jnp.tanh(c * (x + 0.044715 * x * x * x))
    return 0.5 * x * (1.0 + t), 0.5 * (1.0 + t) + 0.5 * x * (1.0 - t * t) * c * (1.0 + 3.0 * 0.044715 * x * x)


def _silu_both(z):
    s = _sigmoid(z)
    return z * s, s * (1.0 + z * (1.0 - s))


def _silu(z):
    return z * _sigmoid(z)


def _silu_grad(z):
    s = _sigmoid(z)
    return s * (1.0 + z * (1.0 - s))


def matmul(a, b, mode, name, out_dtype=F32, add=None):
    if mode == "nn":
        (m, k), n = a.shape, b.shape[1]
    elif mode == "nt":
        (m, k), n = a.shape, b.shape[0]
    else:
        (k, m), n = a.shape, b.shape[1]
    tm = _pick(m, [t for t in (1024, 512, 384, 256, 128) if t * k * a.dtype.itemsize <= MM_BLOCK_BYTES])
    tn = _pick(n, [t for t in (512, 384, 256, 128) if t * k * b.dtype.itemsize <= MM_BLOCK_BYTES])
    dn = {"nn": DN_NN, "nt": DN_NT, "tn": DN_TN}[mode]

    def body(*refs):
        if add is None:
            a_ref, b_ref, o_ref = refs
        else:
            a_ref, b_ref, add_ref, o_ref = refs
        r = _dot(a_ref[...], b_ref[...], dn)
        if add is not None:
            r = r + add_ref[...].astype(F32)
        o_ref[...] = r.astype(out_dtype)

    a_spec = pl.BlockSpec((k, tm), lambda i, j: (0, i)) if mode == "tn" else pl.BlockSpec((tm, k), lambda i, j: (i, 0))
    b_spec = pl.BlockSpec((tn, k), lambda i, j: (j, 0)) if mode == "nt" else pl.BlockSpec((k, tn), lambda i, j: (0, j))
    o_spec = pl.BlockSpec((tm, tn), lambda i, j: (i, j))
    in_specs = [a_spec, b_spec] + ([o_spec] if add is not None else [])
    args = (a, b) + ((add,) if add is not None else ())
    return pl.pallas_call(
        body, name=name, grid=(m // tm, n // tn), in_specs=in_specs, out_specs=o_spec,
        out_shape=jax.ShapeDtypeStruct((m, n), out_dtype),
        compiler_params=_cparams(("parallel", "arbitrary")))(*args)


def _rows(tl, w, col=0):
    return pl.BlockSpec((tl, w), lambda i: (i, col))


def _full(shape):
    nd = len(shape)
    return pl.BlockSpec(tuple(shape), lambda i: (0,) * nd)


def _rowcall(body, name, n_steps, in_specs, out_specs, out_shape, scratch=()):
    return pl.pallas_call(
        body, name=name, grid=(n_steps,), in_specs=in_specs, out_specs=out_specs, out_shape=out_shape,
        scratch_shapes=list(scratch), compiler_params=_cparams(("arbitrary",)))


def _acc(ref, val, i):
    @pl.when(i == 0)
    def _():
        ref[...] = val

    @pl.when(i != 0)
    def _():
        ref[...] += val


def rms_fwd(h, g, name):
    l, d = h.shape
    tl = ROW_TILE

    def body(h_ref, g_ref, o_ref):
        x = h_ref[...]
        r = lax.rsqrt(jnp.mean(x * x, axis=-1, keepdims=True) + NORM_EPS)
        o_ref[...] = (x * r * g_ref[...]).astype(BF16)

    return _rowcall(body, name, l // tl, [_rows(tl, d), _full((1, d))], _rows(tl, d),
                    jax.ShapeDtypeStruct((l, d), BF16))(h, g.reshape(1, d))


def rms_bwd(h, g, dhn, dh_in, name):
    l, d = h.shape
    tl = ROW_TILE

    def body(h_ref, g_ref, dhn_ref, dhi_ref, dh_ref, dg_ref):
        i = pl.program_id(0)
        x = h_ref[...]
        r = lax.rsqrt(jnp.mean(x * x, axis=-1, keepdims=True) + NORM_EPS)
        xhat = x * r
        dy = dhn_ref[...]
        dxh = dy * g_ref[...]
        dx = r * (dxh - xhat * jnp.mean(dxh * xhat, axis=-1, keepdims=True))
        dh_ref[...] = dhi_ref[...] + dx
        _acc(dg_ref, jnp.sum(dy * xhat, axis=0, keepdims=True), i)

    return _rowcall(body, name, l // tl, [_rows(tl, d), _full((1, d)), _rows(tl, d), _rows(tl, d)],
                    [_rows(tl, d), _full((1, d))],
                    [jax.ShapeDtypeStruct((l, d), F32), jax.ShapeDtypeStruct((1, d), F32)])(h, g.reshape(1, d), dhn, dh_in)


def loss_head(h, g, target):
    l, d = h.shape
    tl = ROW_TILE

    def body(h_ref, g_ref, t_ref, loss_ref, dh_ref, dg_ref):
        i = pl.program_id(0)
        x = h_ref[...]
        gg = g_ref[...]
        r = lax.rsqrt(jnp.mean(x * x, axis=-1, keepdims=True) + NORM_EPS)
        xhat = x * r
        err = xhat * gg - t_ref[...]
        part = 0.5 * jnp.sum(jnp.mean(err * err, axis=-1, keepdims=True), axis=0, keepdims=True)
        _acc(loss_ref, part, i)
        dy = err * (1.0 / d)
        dxh = dy * gg
        dh_ref[...] = r * (dxh - xhat * jnp.mean(dxh * xhat, axis=-1, keepdims=True))
        _acc(dg_ref, jnp.sum(dy * xhat, axis=0, keepdims=True), i)

    return _rowcall(body, "loss_head", l // tl, [_rows(tl, d), _full((1, d)), _rows(tl, d)],
                    [_full((1, 1)), _rows(tl, d), _full((1, d))],
                    [jax.ShapeDtypeStruct((1, 1), F32), jax.ShapeDtypeStruct((l, d), F32),
                     jax.ShapeDtypeStruct((1, d), F32)])(h, g.reshape(1, d), target)


def _gmlp_common(a_ref, lng_ref, lnb_ref):
    di = lng_ref.shape[1]
    u_pre = a_ref[:, 0:di]
    v_pre = a_ref[:, di:2 * di]
    z = a_ref[:, 2 * di:3 * di]
    vg = _gelu(v_pre)
    mu = jnp.mean(vg, axis=-1, keepdims=True)
    xc = vg - mu
    rstd = lax.rsqrt(jnp.mean(xc * xc, axis=-1, keepdims=True) + NORM_EPS)
    vhat = xc * rstd
    vn = vhat * lng_ref[...] + lnb_ref[...]
    return u_pre, v_pre, z, vhat, rstd, vn


def _tril(w):
    r = lax.broadcasted_iota(jnp.int32, w.shape, 0)
    c = lax.broadcasted_iota(jnp.int32, w.shape, 1)
    return jnp.where(c <= r, w, 0.0)


def gmlp_gate_fwd(a, ln_g, ln_b, w_s, b_s, name):
    l, w3 = a.shape
    di = w3 // 3
    dg = di // GMLP_GROUPS
    tl = GMLP_CHUNK

    def body(a_ref, lng_ref, lnb_ref, ws_ref, bs_ref, m_ref):
        u_pre, _, z, _, _, vn = _gmlp_common(a_ref, lng_ref, lnb_ref)
        gate = _gelu(u_pre) * _silu(z)
        for g in range(GMLP_GROUPS):
            sl = slice(g * dg, (g + 1) * dg)
            s = _dot(_tril(ws_ref[g]), vn[:, sl], DN_NN) + bs_ref[g]
            m_ref[:, sl] = (gate[:, sl] * s).astype(BF16)

    return _rowcall(body, name, l // tl,
                    [_rows(tl, w3), _full((1, di)), _full((1, di)), _full(w_s.shape), _full((GMLP_GROUPS, tl, 1))],
                    _rows(tl, di), jax.ShapeDtypeStruct((l, di), BF16))(
        a, ln_g.reshape(1, di), ln_b.reshape(1, di), w_s, b_s.reshape(GMLP_GROUPS, tl, 1))


def gmlp_gate_bwd(a, dm, ln_g, ln_b, w_s, b_s, name):
    l, w3 = a.shape
    di = w3 // 3
    dg = di // GMLP_GROUPS
    tl = GMLP_CHUNK

    def body(a_ref, dm_ref, lng_ref, lnb_ref, ws_ref, bs_ref, da_ref, dlg_ref, dlb_ref, dws_ref, dbs_ref,
             dvn_ref, vh_ref, gv_ref):
        i = pl.program_id(0)
        vg, gv = _gelu_both(a_ref[:, di:2 * di])
        gv_ref[...] = gv
        xc = vg - jnp.mean(vg, axis=-1, keepdims=True)
        rstd = lax.rsqrt(jnp.mean(xc * xc, axis=-1, keepdims=True) + NORM_EPS)
        vh_ref[...] = xc * rstd
        for g in range(GMLP_GROUPS):
            sl = slice(g * dg, (g + 1) * dg)
            wt = _tril(ws_ref[g])
            vn_g = vh_ref[:, sl] * lng_ref[:, sl] + lnb_ref[:, sl]
            s = _dot(wt, vn_g, DN_NN) + bs_ref[g]
            dmg = dm_ref[:, sl]
            u, gu = _gelu_both(a_ref[:, sl])
            sz, gz = _silu_both(a_ref[:, 2 * di + g * dg:2 * di + (g + 1) * dg])
            ds = dmg * u * sz
            da_ref[:, sl] = (dmg * s * sz * gu).astype(BF16)
            da_ref[:, 2 * di + g * dg:2 * di + (g + 1) * dg] = (dmg * u * s * gz).astype(BF16)
            dvn_ref[:, sl] = _dot(wt, ds, DN_TN)
            dw = _tril(_dot(ds, vn_g, DN_NT))
            db = jnp.sum(ds, axis=1, keepdims=True)

            @pl.when(i == 0)
            def _():
                dws_ref[g] = dw
                dbs_ref[g] = db

            @pl.when(i != 0)
            def _():
                dws_ref[g] += dw
                dbs_ref[g] += db

        dvn = dvn_ref[...]
        vhat = vh_ref[...]
        dxh = dvn * lng_ref[...]
        dvg = rstd * (dxh - jnp.mean(dxh, axis=-1, keepdims=True) - vhat * jnp.mean(dxh * vhat, axis=-1, keepdims=True))
        da_ref[:, di:2 * di] = (dvg * gv_ref[...]).astype(BF16)
        _acc(dlg_ref, jnp.sum(dvn * vhat, axis=0, keepdims=True), i)
        _acc(dlb_ref, jnp.sum(dvn, axis=0, keepdims=True), i)

    outs = _rowcall(
        body, name, l // tl,
        [_rows(tl, w3), _rows(tl, di), _full((1, di)), _full((1, di)), _full(w_s.shape), _full((GMLP_GROUPS, tl, 1))],
        [_rows(tl, w3), _full((1, di)), _full((1, di)), _full(w_s.shape), _full((GMLP_GROUPS, tl, 1))],
        [jax.ShapeDtypeStruct((l, w3), BF16), jax.ShapeDtypeStruct((1, di), F32), jax.ShapeDtypeStruct((1, di), F32),
         jax.ShapeDtypeStruct(w_s.shape, F32), jax.ShapeDtypeStruct((GMLP_GROUPS, tl, 1), F32)],
        scratch=[pltpu.VMEM((tl, di), F32)] * 3)(
        a, dm, ln_g.reshape(1, di), ln_b.reshape(1, di), w_s, b_s.reshape(GMLP_GROUPS, tl, 1))
    return outs


def gmlp_layer_fwd(h, p, wf, tag):
    hn = rms_fwd(h, p["norm_g"], tag + "_rms")
    a = matmul(hn, wf["w_in"], "nn", tag + "_mm_in")
    m = gmlp_gate_fwd(a, p["ln_g"], p["ln_b"], p["w_s"], p["b_s"], tag + "_gate")
    h_out = matmul(m, wf["w_out"], "nn", tag + "_mm_out", add=h)
    return h_out, (h, hn, a, m)


def gmlp_layer_bwd(dh_out, saved, p, wf, tag):
    h, hn, a, m = saved
    dm = matmul(dh_out, wf["w_out"], "nt", tag + "_mm_dm")
    g_w_out = matmul(m, dh_out, "tn", tag + "_mm_gwout")
    da, dlg, dlb, dws, dbs = gmlp_gate_bwd(a, dm, p["ln_g"], p["ln_b"], p["w_s"], p["b_s"], tag + "_gate_bwd")
    dhn = matmul(da, wf["w_in"], "nt", tag + "_mm_dhn")
    g_w_in = matmul(hn, da, "tn", tag + "_mm_gwin")
    dh, dng = rms_bwd(h, p["norm_g"], dhn, dh_out, tag + "_rms_bwd")
    grads = {"norm_g": dng.reshape(-1), "w_in": g_w_in, "ln_g": dlg.reshape(-1), "ln_b": dlb.reshape(-1),
             "w_s": dws, "b_s": dbs.reshape(GMLP_GROUPS, GMLP_CHUNK), "w_out": g_w_out}
    return dh, grads


def _cmul(ar, ai, br, bi):
    return ar * br - ai * bi, ar * bi + ai * br


S5_PG = 16


def _gblock(tail):
    return pl.BlockSpec((S5_PG,) + tuple(tail), lambda i: (i, 0, 0))


def s5_params_fwd(a_re, a_im, log_step, b_re, b_im):
    g, p, hh = b_re.shape

    def body(ar_ref, ai_ref, ls_ref, br_ref, bi_ref, lr_ref, li_ref, bbr_ref, bbi_ref):
        ar, ai = ar_ref[...], ai_ref[...]
        step = jnp.exp(ls_ref[...])
        mag = jnp.exp(ar * step)
        lr, li = mag * jnp.cos(ai * step), mag * jnp.sin(ai * step)
        den = 1.0 / (ar * ar + ai * ai)
        fr, fi = _cmul(lr - 1.0, li, ar * den, -ai * den)
        lr_ref[...] = lr
        li_ref[...] = li
        bbr, bbi = _cmul(fr, fi, br_ref[...], bi_ref[...])
        bbr_ref[...] = bbr
        bbi_ref[...] = bbi

    s1 = jax.ShapeDtypeStruct((g, p, 1), F32)
    s3 = jax.ShapeDtypeStruct((g, p, hh), F32)
    b1, b0, b3 = _gblock((p, 1)), _gblock((1, 1)), _gblock((p, hh))
    return pl.pallas_call(body, name="s5_params_fwd", grid=(g // S5_PG,), in_specs=[b1, b1, b0, b3, b3],
                          out_specs=[b1, b1, b3, b3], out_shape=[s1, s1, s3, s3],
                          compiler_params=_cparams(("parallel",)))(
        a_re.reshape(g, p, 1), a_im.reshape(g, p, 1), log_step.reshape(g, 1, 1), b_re, b_im)


def s5_params_bwd(a_re, a_im, log_step, b_re, b_im, dl_re, dl_im, dbb_re, dbb_im):
    g, p, hh = b_re.shape

    def body(ar_ref, ai_ref, ls_ref, br_ref, bi_ref, dlr_ref, dli_ref, dbr_ref, dbi_ref,
             gar_ref, gai_ref, gls_ref, gbr_ref, gbi_ref):
        ar, ai = ar_ref[...], ai_ref[...]
        step = jnp.exp(ls_ref[...])
        mag = jnp.exp(ar * step)
        lr, li = mag * jnp.cos(ai * step), mag * jnp.sin(ai * step)
        den = 1.0 / (ar * ar + ai * ai)
        ir, ii = ar * den, -ai * den
        fr, fi = _cmul(lr - 1.0, li, ir, ii)
        br, bi = br_ref[...], bi_ref[...]
        dbr, dbi = dbr_ref[...], dbi_ref[...]
        gbr, gbi = _cmul(fr, -fi, dbr, dbi)
        gbr_ref[...] = gbr
        gbi_ref[...] = gbi
        pr, pi = _cmul(br, -bi, dbr, dbi)
        gfr = jnp.sum(pr, axis=-1, keepdims=True)
        gfi = jnp.sum(pi, axis=-1, keepdims=True)
        t_r, t_i = _cmul(ir, -ii, gfr, gfi)
        glr, gli = dlr_ref[...] + t_r, dli_ref[...] + t_i
        c1r, c1i = _cmul(step * lr, -step * li, glr, gli)
        qr, qi = _cmul(fr, fi, ir, ii)
        c2r, c2i = _cmul(-qr, qi, gfr, gfi)
        gar_ref[...] = c1r + c2r
        gai_ref[...] = c1i + c2i
        wr, wi = _cmul(ar, ai, lr, li)
        sr, _ = _cmul(wr, -wi, glr, gli)
        gls_ref[...] = jnp.sum(sr, axis=1, keepdims=True) * step

    s1 = jax.ShapeDtypeStruct((g, p, 1), F32)
    s3 = jax.ShapeDtypeStruct((g, p, hh), F32)
    b1, b0, b3 = _gblock((p, 1)), _gblock((1, 1)), _gblock((p, hh))
    return pl.pallas_call(body, name="s5_params_bwd", grid=(g // S5_PG,),
                          in_specs=[b1, b1, b0, b3, b3, b1, b1, b3, b3], out_specs=[b1, b1, b0, b3, b3],
                          out_shape=[s1, s1, jax.ShapeDtypeStruct((g, 1, 1), F32), s3, s3],
                          compiler_params=_cparams(("parallel",)))(
        a_re.reshape(g, p, 1), a_im.reshape(g, p, 1), log_step.reshape(g, 1, 1), b_re, b_im,
        dl_re, dl_im, dbb_re, dbb_im)


def _blockdiag(t):
    sb, n, r, c = t.shape
    eye = jnp.eye(n, dtype=bool)[None, :, None, :, None]
    full = jnp.where(eye, t[:, :, :, None, :], jnp.zeros((), t.dtype))
    return full.reshape(sb, n * r, n * c)


def _blockdiag_extract(m, r, c):
    sb = m.shape[0]
    n = m.shape[1] // r
    m5 = m.reshape(sb, n, r, n, c)
    return jnp.stack([m5[:, i, :, i, :] for i in range(n)], axis=1)


S5_TB = 64
S5_UNROLL = 8


def s5_scan_fwd(a_p, lam_re, lam_im, wb_re, wb_im, wc_re, wc_im, d_skip, x0_re, x0_im, name):
    l = a_p.shape[0]
    di = d_skip.shape[1]
    rows = S5_SEG * S5_TB
    nb = l // rows
    ns = wb_re.shape[2]

    def body(u_ref, lr_ref, li_ref, wbr_ref, wbi_ref, wcr_ref, wci_ref, ds_ref, x0r_ref, x0i_ref,
             y_ref, ckr_ref, cki_ref, xer_ref, xei_ref, bur, bui, xr_s, xi_s):
        b = pl.program_id(1)

        @pl.when(b == 0)
        def _():
            xr_s[...] = x0r_ref[0]
            xi_s[...] = x0i_ref[0]

        ckr_ref[0, 0] = xr_s[...]
        cki_ref[0, 0] = xi_s[...]
        u = u_ref[...]
        bur[...] = _dot(u, wbr_ref[0], DN_NN)
        bui[...] = _dot(u, wbi_ref[0], DN_NN)
        lr = jnp.broadcast_to(lr_ref[0], (S5_SEG, ns))
        li = jnp.broadcast_to(li_ref[0], (S5_SEG, ns))

        def step(t, carry):
            xr, xi = carry
            sl = pl.ds(pl.multiple_of(t * S5_SEG, S5_SEG), S5_SEG)
            nr = lr * xr - li * xi + bur[sl, :]
            ni = lr * xi + li * xr + bui[sl, :]
            bur[sl, :] = nr
            bui[sl, :] = ni
            return nr, ni

        xr, xi = lax.fori_loop(0, S5_TB, step, (xr_s[...], xi_s[...]), unroll=S5_UNROLL)
        xr_s[...] = xr
        xi_s[...] = xi
        xer_ref[0] = xr
        xei_ref[0] = xi
        y_ref[...] = _dot(bur[...], wcr_ref[0], DN_NN) - _dot(bui[...], wci_ref[0], DN_NN) + ds_ref[...] * u

    sb3 = lambda s, b: (s, 0, 0)
    st = jax.ShapeDtypeStruct
    return pl.pallas_call(
        body, name=name, grid=(S5_SB, nb),
        in_specs=[pl.BlockSpec((rows, LANES), lambda s, b: (b, s)),
                  pl.BlockSpec((1, 1, ns), sb3), pl.BlockSpec((1, 1, ns), sb3),
                  pl.BlockSpec((1, LANES, ns), sb3), pl.BlockSpec((1, LANES, ns), sb3),
                  pl.BlockSpec((1, ns, LANES), sb3), pl.BlockSpec((1, ns, LANES), sb3),
                  pl.BlockSpec((1, LANES), lambda s, b: (0, s)),
                  pl.BlockSpec((1, S5_SEG, ns), sb3), pl.BlockSpec((1, S5_SEG, ns), sb3)],
        out_specs=[pl.BlockSpec((rows, LANES), lambda s, b: (b, s)),
                   pl.BlockSpec((1, 1, S5_SEG, ns), lambda s, b: (s, b, 0, 0)),
                   pl.BlockSpec((1, 1, S5_SEG, ns), lambda s, b: (s, b, 0, 0)),
                   pl.BlockSpec((1, S5_SEG, ns), sb3), pl.BlockSpec((1, S5_SEG, ns), sb3)],
        out_shape=[st((l, di), F32), st((S5_SB, nb, S5_SEG, ns), F32), st((S5_SB, nb, S5_SEG, ns), F32),
                   st((S5_SB, S5_SEG, ns), F32), st((S5_SB, S5_SEG, ns), F32)],
        scratch_shapes=[pltpu.VMEM((rows, ns), F32), pltpu.VMEM((rows, ns), F32),
                        pltpu.VMEM((S5_SEG, ns), F32), pltpu.VMEM((S5_SEG, ns), F32)],
        compiler_params=_cparams(("parallel", "arbitrary")))(
        a_p, lam_re, lam_im, wb_re, wb_im, wc_re, wc_im, d_skip, x0_re, x0_im)


def s5_ends(inp, lam_re, lam_im, w_re, w_im, adjoint, name):
    l = inp.shape[0]
    rows = S5_SEG * S5_TB
    nb = l // rows
    ns = lam_re.shape[2]

    def body(i_ref, lr_ref, li_ref, wr_ref, wi_ref, er_ref, ei_ref, pr_b, pi_b, xr_s, xi_s):
        b = pl.program_id(1)

        @pl.when(b == 0)
        def _():
            xr_s[...] = jnp.zeros_like(xr_s)
            xi_s[...] = jnp.zeros_like(xi_s)

        v = i_ref[...]
        lr = jnp.broadcast_to(lr_ref[0], (S5_SEG, ns))
        li = jnp.broadcast_to(li_ref[0], (S5_SEG, ns))
        if adjoint:
            pr_b[...] = _dot(v, wr_ref[0], DN_NT)
            pi_b[...] = -_dot(v, wi_ref[0], DN_NT)
            li = -li
        else:
            pr_b[...] = _dot(v, wr_ref[0], DN_NN)
            pi_b[...] = _dot(v, wi_ref[0], DN_NN)

        def step(k, carry):
            xr, xi = carry
            t = S5_TB - 1 - k if adjoint else k
            sl = pl.ds(pl.multiple_of(t * S5_SEG, S5_SEG), S5_SEG)
            return lr * xr - li * xi + pr_b[sl, :], lr * xi + li * xr + pi_b[sl, :]

        xr, xi = lax.fori_loop(0, S5_TB, step, (xr_s[...], xi_s[...]), unroll=S5_UNROLL)
        xr_s[...] = xr
        xi_s[...] = xi
        er_ref[0] = xr
        ei_ref[0] = xi

    sb3 = lambda s, b: (s, 0, 0)
    blk = (lambda s, b: (nb - 1 - b, s)) if adjoint else (lambda s, b: (b, s))
    wshape = (1, ns, LANES) if adjoint else (1, LANES, ns)
    st = jax.ShapeDtypeStruct((S5_SB, S5_SEG, ns), F32)
    return pl.pallas_call(
        body, name=name, grid=(S5_SB, nb),
        in_specs=[pl.BlockSpec((rows, LANES), blk), pl.BlockSpec((1, 1, ns), sb3), pl.BlockSpec((1, 1, ns), sb3),
                  pl.BlockSpec(wshape, sb3), pl.BlockSpec(wshape, sb3)],
        out_specs=[pl.BlockSpec((1, S5_SEG, ns), sb3), pl.BlockSpec((1, S5_SEG, ns), sb3)],
        out_shape=[st, st],
        scratch_shapes=[pltpu.VMEM((rows, ns), F32), pltpu.VMEM((rows, ns), F32),
                        pltpu.VMEM((S5_SEG, ns), F32), pltpu.VMEM((S5_SEG, ns), F32)],
        compiler_params=_cparams(("parallel", "arbitrary")))(inp, lam_re, lam_im, w_re, w_im)


def s5_scan_bwd(a_p, dy, lam_re, lam_im, wb_re, wb_im, wc_re, wc_im, d_skip, ck_re, ck_im, a0_re, a0_im, name):
    l = a_p.shape[0]
    di = d_skip.shape[1]
    rows = S5_SEG * S5_TB
    nb = l // rows
    ns = wb_re.shape[2]

    def body(u_ref, dy_ref, lr_ref, li_ref, wbr_ref, wbi_ref, wcr_ref, wci_ref, ds_ref, ckr_ref, cki_ref,
             a0r_ref, a0i_ref,
             du_ref, dwbr_ref, dwbi_ref, dwcr_ref, dwci_ref, dds_ref, dlr_ref, dli_ref, aer_ref, aei_ref,
             xr_b, xi_b, gr_b, gi_b, ar_s, ai_s):
        b = pl.program_id(1)

        @pl.when(b == 0)
        def _():
            ar_s[...] = a0r_ref[0]
            ai_s[...] = a0i_ref[0]

        u = u_ref[...]
        dyv = dy_ref[...]
        lr = jnp.broadcast_to(lr_ref[0], (S5_SEG, ns))
        li = jnp.broadcast_to(li_ref[0], (S5_SEG, ns))
        xr_b[...] = _dot(u, wbr_ref[0], DN_NN)
        xi_b[...] = _dot(u, wbi_ref[0], DN_NN)

        def fstep(t, carry):
            xr, xi = carry
            sl = pl.ds(pl.multiple_of(t * S5_SEG, S5_SEG), S5_SEG)
            nr = lr * xr - li * xi + xr_b[sl, :]
            ni = lr * xi + li * xr + xi_b[sl, :]
            xr_b[sl, :] = nr
            xi_b[sl, :] = ni
            return nr, ni

        x0r, x0i = ckr_ref[0, 0], cki_ref[0, 0]
        lax.fori_loop(0, S5_TB, fstep, (x0r, x0i), unroll=S5_UNROLL)
        dwcr = _dot(xr_b[...], dyv, DN_TN)
        dwci = -_dot(xi_b[...], dyv, DN_TN)
        gr_b[...] = _dot(dyv, wcr_ref[0], DN_NT)
        gi_b[...] = -_dot(dyv, wci_ref[0], DN_NT)

        def bstep(k, carry):
            ar, ai, dlr, dli = carry
            t = S5_TB - 1 - k
            sl = pl.ds(pl.multiple_of(t * S5_SEG, S5_SEG), S5_SEG)
            slp = pl.ds(pl.multiple_of(jnp.maximum(t - 1, 0) * S5_SEG, S5_SEG), S5_SEG)
            nr = gr_b[sl, :] + lr * ar + li * ai
            ni = gi_b[sl, :] + lr * ai - li * ar
            gr_b[sl, :] = nr
            gi_b[sl, :] = ni
            first = t == 0
            pr = jnp.where(first, x0r, xr_b[slp, :])
            pi = jnp.where(first, x0i, xi_b[slp, :])
            dlr = dlr + nr * pr + ni * pi
            dli = dli + ni * pr - nr * pi
            return nr, ni, dlr, dli

        zero = jnp.zeros((S5_SEG, ns), F32)
        ar, ai, dlr, dli = lax.fori_loop(0, S5_TB, bstep, (ar_s[...], ai_s[...], zero, zero), unroll=S5_UNROLL)
        ar_s[...] = ar
        ai_s[...] = ai
        aer_ref[0] = ar
        aei_ref[0] = ai
        dsk = ds_ref[...]
        du_ref[...] = (_dot(gr_b[...], wbr_ref[0], DN_NT) + _dot(gi_b[...], wbi_ref[0], DN_NT) + dsk * dyv).astype(BF16)
        dwbr = _dot(u, gr_b[...], DN_TN)
        dwbi = _dot(u, gi_b[...], DN_TN)
        dds = jnp.sum(dyv * u, axis=0, keepdims=True)

        @pl.when(b == 0)
        def _():
            dwbr_ref[0] = dwbr
            dwbi_ref[0] = dwbi
            dwcr_ref[0] = dwcr
            dwci_ref[0] = dwci
            dds_ref[...] = dds
            dlr_ref[0] = dlr
            dli_ref[0] = dli

        @pl.when(b != 0)
        def _():
            dwbr_ref[0] += dwbr
            dwbi_ref[0] += dwbi
            dwcr_ref[0] += dwcr
            dwci_ref[0] += dwci
            dds_ref[...] += dds
            dlr_ref[0] += dlr
            dli_ref[0] += dli

    sb3 = lambda s, b: (s, 0, 0)
    rev = lambda s, b: (nb - 1 - b, s)
    st = jax.ShapeDtypeStruct
    return pl.pallas_call(
        body, name=name, grid=(S5_SB, nb),
        in_specs=[pl.BlockSpec((rows, LANES), rev), pl.BlockSpec((rows, LANES), rev),
                  pl.BlockSpec((1, 1, ns), sb3), pl.BlockSpec((1, 1, ns), sb3),
                  pl.BlockSpec((1, LANES, ns), sb3), pl.BlockSpec((1, LANES, ns), sb3),
                  pl.BlockSpec((1, ns, LANES), sb3), pl.BlockSpec((1, ns, LANES), sb3),
                  pl.BlockSpec((1, LANES), lambda s, b: (0, s)),
                  pl.BlockSpec((1, 1, S5_SEG, ns), lambda s, b: (s, nb - 1 - b, 0, 0)),
                  pl.BlockSpec((1, 1, S5_SEG, ns), lambda s, b: (s, nb - 1 - b, 0, 0)),
                  pl.BlockSpec((1, S5_SEG, ns), sb3), pl.BlockSpec((1, S5_SEG, ns), sb3)],
        out_specs=[pl.BlockSpec((rows, LANES), rev),
                   pl.BlockSpec((1, LANES, ns), sb3), pl.BlockSpec((1, LANES, ns), sb3),
                   pl.BlockSpec((1, ns, LANES), sb3), pl.BlockSpec((1, ns, LANES), sb3),
                   pl.BlockSpec((1, LANES), lambda s, b: (0, s)),
                   pl.BlockSpec((1, S5_SEG, ns), sb3), pl.BlockSpec((1, S5_SEG, ns), sb3),
                   pl.BlockSpec((1, S5_SEG, ns), sb3), pl.BlockSpec((1, S5_SEG, ns), sb3)],
        out_shape=[st((l, di), BF16), st((S5_SB, LANES, ns), F32), st((S5_SB, LANES, ns), F32),
                   st((S5_SB, ns, LANES), F32), st((S5_SB, ns, LANES), F32), st((1, di), F32),
                   st((S5_SB, S5_SEG, ns), F32), st((S5_SB, S5_SEG, ns), F32),
                   st((S5_SB, S5_SEG, ns), F32), st((S5_SB, S5_SEG, ns), F32)],
        scratch_shapes=[pltpu.VMEM((rows, ns), F32), pltpu.VMEM((rows, ns), F32),
                        pltpu.VMEM((rows, ns), F32), pltpu.VMEM((rows, ns), F32),
                        pltpu.VMEM((S5_SEG, ns), F32), pltpu.VMEM((S5_SEG, ns), F32)],
        compiler_params=_cparams(("parallel", "arbitrary")))(
        a_p, dy, lam_re, lam_im, wb_re, wb_im, wc_re, wc_im, d_skip, ck_re, ck_im, a0_re, a0_im)


def s5_carry(e_re, e_im, lam_re, lam_im, seg_len, reverse, name):
    sb, seg, ns = e_re.shape

    def body(er_ref, ei_ref, lr_ref, li_ref, cr_ref, ci_ref):
        pr, pi = lr_ref[...], li_ref[...]
        if reverse:
            pi = -pi
        for _ in range(int(math.log2(seg_len))):
            pr, pi = _cmul(pr, pi, pr, pi)
        er, ei = er_ref[...], ei_ref[...]
        row = lax.broadcasted_iota(jnp.int32, (sb, seg, ns), 1)
        cr = jnp.zeros((sb, seg, ns), F32)
        ci = jnp.zeros((sb, seg, ns), F32)
        cur_r = jnp.zeros((sb, 1, ns), F32)
        cur_i = jnp.zeros((sb, 1, ns), F32)
        order = range(seg - 2, -1, -1) if reverse else range(1, seg)
        for s in order:
            src = s + 1 if reverse else s - 1
            mr, mi = _cmul(pr, pi, cur_r, cur_i)
            cur_r = jnp.sum(jnp.where(row == src, er, 0.0), axis=1, keepdims=True) + mr
            cur_i = jnp.sum(jnp.where(row == src, ei, 0.0), axis=1, keepdims=True) + mi
            cr = jnp.where(row == s, cur_r, cr)
            ci = jnp.where(row == s, cur_i, ci)
        cr_ref[...] = cr
        ci_ref[...] = ci

    st = jax.ShapeDtypeStruct((sb, seg, ns), F32)
    return pl.pallas_call(body, name=name, out_shape=[st, st],
                          compiler_params=pltpu.CompilerParams(vmem_limit_bytes=VMEM_LIMIT_BYTES))(e_re, e_im, lam_re, lam_im)


def s5_act(y, name):
    l, d = y.shape
    tl = ROW_TILE

    def body(y_ref, o_ref):
        o_ref[...] = _gelu(y_ref[...]).astype(BF16)

    return _rowcall(body, name, l // tl, [_rows(tl, d)], _rows(tl, d), jax.ShapeDtypeStruct((l, d), BF16))(y)


def s5_gate_fwd(y, t, b_glu, a_p, name):
    l, d = y.shape
    tl = ROW_TILE

    def body(y_ref, t_ref, b_ref, z_ref, m_ref):
        yg = _gelu(y_ref[...])
        m_ref[...] = (yg * _sigmoid(t_ref[...] + b_ref[...]) * _silu(z_ref[...])).astype(BF16)

    return _rowcall(body, name, l // tl, [_rows(tl, d), _rows(tl, d), _full((1, d)), _rows(tl, d, 1)], _rows(tl, d),
                    jax.ShapeDtypeStruct((l, d), BF16))(y, t, b_glu.reshape(1, d), a_p)


def s5_gate_bwd(dm, y, t, b_glu, a_p, name):
    l, d = y.shape
    tl = ROW_TILE

    def body(dm_ref, y_ref, t_ref, b_ref, z_ref, dt_ref, dyg_ref, dz_ref, db_ref):
        i = pl.program_id(0)
        dmv = dm_ref[...]
        z = z_ref[...]
        yg = _gelu(y_ref[...])
        sg = _sigmoid(t_ref[...] + b_ref[...])
        y2 = yg * sg
        sz, gz = _silu_both(z)
        dy2 = dmv * sz
        dz_ref[...] = (dmv * y2 * gz).astype(BF16)
        dyg_ref[...] = dy2 * sg
        dt = dy2 * yg * sg * (1.0 - sg)
        dt_ref[...] = dt.astype(BF16)
        _acc(db_ref, jnp.sum(dt, axis=0, keepdims=True), i)

    st = jax.ShapeDtypeStruct
    return _rowcall(body, name, l // tl, [_rows(tl, d), _rows(tl, d), _rows(tl, d), _full((1, d)), _rows(tl, d, 1)],
                    [_rows(tl, d), _rows(tl, d), _rows(tl, d), _full((1, d))],
                    [st((l, d), BF16), st((l, d), F32), st((l, d), BF16), st((1, d), F32)])(
        dm, y, t, b_glu.reshape(1, d), a_p)


def s5_act_bwd(y, dyg_a, dyg_b, name):
    l, d = y.shape
    tl = ROW_TILE

    def body(y_ref, a_ref, b_ref, o_ref):
        o_ref[...] = (a_ref[...] + b_ref[...]) * _gelu_grad(y_ref[...])

    return _rowcall(body, name, l // tl, [_rows(tl, d)] * 3, _rows(tl, d), jax.ShapeDtypeStruct((l, d), F32))(y, dyg_a, dyg_b)


def _seg_perm(t):
    l, d = t.shape
    return t.reshape(S5_SEG, l // S5_SEG, d).transpose(1, 0, 2).reshape(l, d)


def _seg_unperm(t):
    l, d = t.shape
    return t.reshape(l // S5_SEG, S5_SEG, d).transpose(1, 0, 2).reshape(l, d)


def _s5_weights(p):
    lr, li, bbr, bbi = s5_params_fwd(p["a_re"], p["a_im"], p["log_step"], p["b_re"], p["b_im"])
    ns = 8 * S5_STATE
    lam_re = lr.reshape(S5_SB, 1, ns)
    lam_im = li.reshape(S5_SB, 1, ns)
    to_bd = lambda t: _blockdiag(t.reshape(S5_SB, 8, t.shape[1], t.shape[2]))
    wb_re = to_bd(bbr.transpose(0, 2, 1)).astype(BF16)
    wb_im = to_bd(bbi.transpose(0, 2, 1)).astype(BF16)
    wc_re = to_bd(p["c_re"].transpose(0, 2, 1)).astype(BF16)
    wc_im = to_bd(p["c_im"].transpose(0, 2, 1)).astype(BF16)
    return lam_re, lam_im, wb_re, wb_im, wc_re, wc_im


def s5_layer_fwd(h, p, wf, tag):
    l = h.shape[0]
    di = p["d_skip"].shape[0]
    hn = rms_fwd(h, p["norm_g"], tag + "_rms")
    hn_p = _seg_perm(hn)
    a_p = matmul(hn_p, wf["w_in"], "nn", tag + "_mm_in")
    sw = _s5_weights(p)
    dsk = p["d_skip"].reshape(1, di)
    e_re, e_im = s5_ends(a_p, sw[0], sw[1], sw[2], sw[3], False, tag + "_scan_ends")
    c_re, c_im = s5_carry(e_re, e_im, sw[0], sw[1], l // S5_SEG, False, tag + "_carry")
    y, ck_re, ck_im, _, _ = s5_scan_fwd(a_p, *sw, dsk, c_re, c_im, tag + "_scan")
    yg = s5_act(y, tag + "_act")
    t = matmul(yg, wf["w_glu"], "nn", tag + "_mm_glu")
    m = s5_gate_fwd(y, t, p["b_glu"], a_p, tag + "_gate")
    out_p = matmul(m, wf["w_out"], "nn", tag + "_mm_out")
    h_out = residual_add(h, _seg_unperm(out_p), tag + "_res")
    return h_out, (h, hn_p, a_p, sw, ck_re, ck_im, y, yg, t, m)


def residual_add(h, y, name):
    l, d = h.shape
    tl = ROW_TILE

    def body(h_ref, y_ref, o_ref):
        o_ref[...] = h_ref[...] + y_ref[...]

    return _rowcall(body, name, l // tl, [_rows(tl, d)] * 2, _rows(tl, d), jax.ShapeDtypeStruct((l, d), F32))(h, y)


def s5_layer_bwd(dh_out, saved, p, wf, tag):
    h, hn_p, a_p, sw, ck_re, ck_im, y, yg, t, m = saved
    l = h.shape[0]
    di = p["d_skip"].shape[0]
    dsk = p["d_skip"].reshape(1, di)
    dout_p = _seg_perm(dh_out)
    dm = matmul(dout_p, wf["w_out"], "nt", tag + "_mm_dm")
    g_w_out = matmul(m, dout_p, "tn", tag + "_mm_gwout")
    dt, dyg_a, dz, db_glu = s5_gate_bwd(dm, y, t, p["b_glu"], a_p, tag + "_gate_bwd")
    dyg_b = matmul(dt, wf["w_glu"], "nt", tag + "_mm_dyg")
    g_w_glu = matmul(yg, dt, "tn", tag + "_mm_gwglu")
    dy = s5_act_bwd(y, dyg_a, dyg_b, tag + "_act_bwd")
    e_re, e_im = s5_ends(dy, sw[0], sw[1], sw[4], sw[5], True, tag + "_scanb_ends")
    c_re, c_im = s5_carry(e_re, e_im, sw[0], sw[1], l // S5_SEG, True, tag + "_carry_bwd")
    du, dwbr, dwbi, dwcr, dwci, dds, dlr, dli, _, _ = s5_scan_bwd(
        a_p, dy, *sw, dsk, ck_re, ck_im, c_re, c_im, tag + "_scanb")
    da = jnp.concatenate([du, dz], axis=1)
    dhn_p = matmul(da, wf["w_in"], "nt", tag + "_mm_dhn")
    g_w_in = matmul(hn_p, da, "tn", tag + "_mm_gwin")
    dh, dng = rms_bwd(h, p["norm_g"], _seg_unperm(dhn_p), dh_out, tag + "_rms_bwd")
    ex = lambda m_, r, c: _blockdiag_extract(m_, r, c).reshape(S5_GROUPS, r, c).transpose(0, 2, 1)
    dbb_re, dbb_im = ex(dwbr, S5_GROUP, S5_STATE), ex(dwbi, S5_GROUP, S5_STATE)
    g_c_re, g_c_im = ex(dwcr, S5_STATE, S5_GROUP), ex(dwci, S5_STATE, S5_GROUP)
    dl_re = lane_sum8(dlr).reshape(S5_GROUPS, S5_STATE, 1)
    dl_im = lane_sum8(dli).reshape(S5_GROUPS, S5_STATE, 1)
    gar, gai, gls, gbr, gbi = s5_params_bwd(p["a_re"], p["a_im"], p["log_step"], p["b_re"], p["b_im"],
                                            dl_re, dl_im, dbb_re, dbb_im)
    grads = {"norm_g": dng.reshape(-1), "w_in": g_w_in, "a_re": gar.reshape(S5_GROUPS, S5_STATE),
             "a_im": gai.reshape(S5_GROUPS, S5_STATE), "log_step": gls.reshape(-1), "b_re": gbr, "b_im": gbi,
             "c_re": g_c_re, "c_im": g_c_im, "d_skip": dds.reshape(-1), "w_glu": g_w_glu,
             "b_glu": db_glu.reshape(-1), "w_out": g_w_out}
    return dh, grads


def lane_sum8(t):
    sb, seg, ns = t.shape

    def body(t_ref, o_ref):
        o_ref[...] = jnp.sum(t_ref[...], axis=1, keepdims=True)

    return pl.pallas_call(body, name="s5_seg_sum", out_shape=jax.ShapeDtypeStruct((sb, 1, ns), F32))(t)


MLA_DI = MLA_HEADS * 128
MLA_CQ0 = MLA_DI
MLA_CKV0 = MLA_CQ0 + MLA_Q_RANK
MLA_KR0 = MLA_CKV0 + MLA_KV_RANK
MLA_AW = MLA_KR0 + LANES


def _rot_half(x):
    w = x.shape[-1]
    lane = lax.broadcasted_iota(jnp.int32, x.shape, x.ndim - 1)
    return jnp.where(lane % MLA_ROPE < MLA_ROPE // 2, pltpu.roll(x, w - MLA_ROPE // 2, x.ndim - 1),
                     pltpu.roll(x, MLA_ROPE // 2, x.ndim - 1))


def rope_tables(pos):
    l = pos.shape[0]
    tl = ROW_TILE
    j = np.arange(LANES) % MLA_ROPE % (MLA_ROPE // 2)
    inv_freq = (ROPE_THETA ** (-(2.0 * j) / MLA_ROPE)).astype(np.float32).reshape(1, LANES)
    sign = np.where(np.arange(LANES) % MLA_ROPE < MLA_ROPE // 2, -1.0, 1.0).astype(np.float32).reshape(1, LANES)

    def body(p_ref, f_ref, s_ref, cos_ref, sin_ref):
        ang = p_ref[...].astype(F32) * f_ref[...]
        cos_ref[...] = jnp.cos(ang)
        sin_ref[...] = jnp.sin(ang) * s_ref[...]

    st = jax.ShapeDtypeStruct((l, LANES), F32)
    return _rowcall(body, "rope_tables", l // tl, [_rows(tl, 1), _full((1, LANES)), _full((1, LANES))],
                    [_rows(tl, LANES)] * 2, [st, st])(pos, jnp.asarray(inv_freq), jnp.asarray(sign))


def _rope(x, cos, sins):
    return x * cos + _rot_half(x) * sins


def _rope_t(dy, cos, sins):
    return dy * cos - sins * _rot_half(dy)


def _rmsn(x):
    r = lax.rsqrt(jnp.mean(x * x, axis=-1, keepdims=True) + NORM_EPS)
    return x * r, r


def mla_pre(a, q_g, kv_g, cos, sins, name):
    l = a.shape[0]
    tl = ROW_TILE

    def body(a_ref, qg_ref, kg_ref, cos_ref, sin_ref, cq_ref, ckv_ref, krs_ref):
        xq, _ = _rmsn(a_ref[:, MLA_CQ0:MLA_CKV0])
        cq_ref[...] = (xq * qg_ref[...]).astype(BF16)
        xk, _ = _rmsn(a_ref[:, MLA_CKV0:MLA_KR0])
        ckv_ref[...] = (xk * kg_ref[...]).astype(BF16)
        kr = a_ref[:, MLA_KR0:MLA_AW]
        kr2 = kr + pltpu.roll(kr, MLA_ROPE, 1)
        kr2 = _rope(kr2, cos_ref[...], sin_ref[...])
        lane = lax.broadcasted_iota(jnp.int32, kr2.shape, 1)
        krs_ref[0] = jnp.where(lane < MLA_ROPE, kr2, 0.0).astype(BF16)
        krs_ref[1] = jnp.where(lane >= MLA_ROPE, kr2, 0.0).astype(BF16)

    st = jax.ShapeDtypeStruct
    return _rowcall(body, name, l // tl,
                    [_rows(tl, MLA_AW), _full((1, MLA_Q_RANK)), _full((1, MLA_KV_RANK)), _rows(tl, LANES), _rows(tl, LANES)],
                    [_rows(tl, MLA_Q_RANK), _rows(tl, MLA_KV_RANK), pl.BlockSpec((2, tl, LANES), lambda i: (0, i, 0))],
                    [st((l, MLA_Q_RANK), BF16), st((l, MLA_KV_RANK), BF16), st((2, l, LANES), BF16)])(
        a, q_g.reshape(1, -1), kv_g.reshape(1, -1), cos, sins)


def mla_rope_q(qr, cos, sins, name):
    l, w = qr.shape
    tl = ROW_TILE

    def body(q_ref, cos_ref, sin_ref, o_ref):
        c, s = cos_ref[...], sin_ref[...]
        for p in range(w // LANES):
            sl = slice(p * LANES, (p + 1) * LANES)
            o_ref[:, sl] = _rope(q_ref[:, sl], c, s).astype(BF16)

    return _rowcall(body, name, l // tl, [_rows(tl, w), _rows(tl, LANES), _rows(tl, LANES)], _rows(tl, w),
                    jax.ShapeDtypeStruct((l, w), BF16))(qr, cos, sins)


ATT_OUT = 512
ATT_IN = 256
ATT_R = ATT_OUT // ATT_IN


def _scores(qn, qr, kn, kr, mask_off, transposed):
    q2 = jnp.concatenate([qn, qr], axis=1)
    k2 = jnp.concatenate([kn, kr], axis=1)
    s = (_dot(k2, q2, DN_NT) if transposed else _dot(q2, k2, DN_NT)) * MLA_SCALE
    if mask_off is None:
        return s
    r = lax.broadcasted_iota(jnp.int32, s.shape, 0)
    c = lax.broadcasted_iota(jnp.int32, s.shape, 1)
    return jnp.where((r <= c + mask_off) if transposed else (c + mask_off <= r), s, NEG_INF)


def _fold(x):
    return x[:, :LANES], x[:, LANES:]


def flash_fwd(qn, qr, kv, krs, name):
    l = qn.shape[0]
    nq = l // ATT_OUT

    def body(qn_ref, qr_ref, kv_ref, kr_ref, o_ref, lse_ref, s_buf):
        qi = pl.program_id(1)
        q_r = qr_ref[...]
        q_n = [qn_ref[:, hh * LANES:(hh + 1) * LANES] for hh in range(2)]

        def block_scores(j, mx, mask_off):
            sl = pl.ds(pl.multiple_of(j * ATT_IN, ATT_IN), ATT_IN)
            out = []
            for hh in range(2):
                s = _scores(q_n[hh], q_r, kv_ref[sl, 2 * hh * LANES:(2 * hh + 1) * LANES], kr_ref[hh, sl, :],
                            mask_off, False)
                s_buf[hh, j] = s
                lo, hi = _fold(s)
                out.append(jnp.maximum(mx[hh], jnp.maximum(lo, hi)))
            return tuple(out)

        ninf = jnp.full((ATT_OUT, LANES), NEG_INF, F32)
        mx = lax.fori_loop(0, ATT_R * qi, lambda j, c: block_scores(j, c, None), (ninf, ninf))
        for d in range(ATT_R):
            mx = block_scores(ATT_R * qi + d, mx, d * ATT_IN)
        m = [jnp.max(mx[hh], axis=-1, keepdims=True) for hh in range(2)]

        def block_pv(j, carry):
            sl = pl.ds(pl.multiple_of(j * ATT_IN, ATT_IN), ATT_IN)
            out = []
            for hh in range(2):
                ls, acc = carry[hh]
                p = jnp.exp(s_buf[hh, j] - m[hh])
                lo, hi = _fold(p)
                out.append((ls + (lo + hi),
                            acc + _dot(p, kv_ref[sl, (2 * hh + 1) * LANES:(2 * hh + 2) * LANES], DN_NN)))
            return tuple(out)

        z = jnp.zeros((ATT_OUT, LANES), F32)
        res = lax.fori_loop(0, ATT_R * (qi + 1), block_pv, ((z, z), (z, z)))
        for hh in range(2):
            lsum = jnp.sum(res[hh][0], axis=-1, keepdims=True)
            o_ref[:, hh * LANES:(hh + 1) * LANES] = res[hh][1] / lsum
            lse_ref[hh] = m[hh] + jnp.log(lsum)

    st = jax.ShapeDtypeStruct
    return pl.pallas_call(
        body, name=name, grid=(MLA_HEADS // 2, nq),
        in_specs=[pl.BlockSpec((ATT_OUT, 2 * LANES), lambda p, i: (i, p)),
                  pl.BlockSpec((ATT_OUT, LANES), lambda p, i: (i, p)),
                  pl.BlockSpec((l, 4 * LANES), lambda p, i: (0, p)),
                  pl.BlockSpec((2, l, LANES), lambda p, i: (0, 0, 0))],
        out_specs=[pl.BlockSpec((ATT_OUT, 2 * LANES), lambda p, i: (i, p)),
                   pl.BlockSpec((2, ATT_OUT, 1), lambda p, i: (p, i, 0))],
        out_shape=[st((l, MLA_DI), F32), st((MLA_HEADS, l, 1), F32)],
        scratch_shapes=[pltpu.VMEM((2, l // ATT_IN, ATT_OUT, ATT_IN), F32)],
        compiler_params=_cparams(("parallel", "arbitrary")))(qn, qr, kv, krs)


def flash_dkv(qn, qr, kv, krs, do, lse_row, delta_row, name):
    l = qn.shape[0]
    nk = l // ATT_OUT
    nq = l // ATT_IN

    def body(qn_ref, qr_ref, do_ref, lse_ref, dl_ref, kv_ref, kr_ref, dkv_ref, dkr_ref):
        kj = pl.program_id(1)
        lane = lax.broadcasted_iota(jnp.int32, (ATT_OUT, LANES), 1)
        kn = [kv_ref[:, 2 * hh * LANES:(2 * hh + 1) * LANES] for hh in range(2)]
        v = [kv_ref[:, (2 * hh + 1) * LANES:(2 * hh + 2) * LANES] for hh in range(2)]

        def block(i, carry, mask_off):
            sl = pl.ds(pl.multiple_of(i * ATT_IN, ATT_IN), ATT_IN)
            q_r = qr_ref[sl, :]
            out = []
            for hh in range(2):
                dk2, dv = carry[hh]
                hs = slice(hh * LANES, (hh + 1) * LANES)
                q_n, d_o = qn_ref[sl, hs], do_ref[sl, hs]
                s = _scores(q_n, q_r, kn[hh], kr_ref[hh], mask_off, True)
                pt = jnp.exp(s - lse_ref[hh, i])
                dv = dv + _dot(pt, d_o, DN_NN)
                dpt = _dot(v[hh], d_o, DN_NT)
                dst = (pt * (dpt - dl_ref[hh, i]) * MLA_SCALE).astype(BF16)
                out.append((dk2 + _dot(dst, jnp.concatenate([q_n, q_r], axis=1), DN_NN), dv))
            return tuple(out)

        z = jnp.zeros((ATT_OUT, LANES), F32)
        z2 = jnp.zeros((ATT_OUT, 2 * LANES), F32)
        res = ((z2, z), (z2, z))
        for d in range(ATT_R):
            res = block(ATT_R * kj + d, res, d * ATT_IN)
        res = lax.fori_loop(ATT_R * (kj + 1), nq, lambda i, c: block(i, c, None), res)
        for hh in range(2):
            dkv_ref[:, 2 * hh * LANES:(2 * hh + 1) * LANES] = res[hh][0][:, :LANES].astype(BF16)
            dkv_ref[:, (2 * hh + 1) * LANES:(2 * hh + 2) * LANES] = res[hh][1].astype(BF16)
        dkr_ref[0] = jnp.where(lane < MLA_ROPE, res[0][0][:, LANES:], res[1][0][:, LANES:])

    st = jax.ShapeDtypeStruct
    return pl.pallas_call(
        body, name=name, grid=(MLA_HEADS // 2, nk),
        in_specs=[pl.BlockSpec((l, 2 * LANES), lambda p, j: (0, p)),
                  pl.BlockSpec((l, LANES), lambda p, j: (0, p)),
                  pl.BlockSpec((l, 2 * LANES), lambda p, j: (0, p)),
                  pl.BlockSpec((2, nq, 1, ATT_IN), lambda p, j: (p, 0, 0, 0)),
                  pl.BlockSpec((2, nq, 1, ATT_IN), lambda p, j: (p, 0, 0, 0)),
                  pl.BlockSpec((ATT_OUT, 4 * LANES), lambda p, j: (j, p)),
                  pl.BlockSpec((2, ATT_OUT, LANES), lambda p, j: (0, j, 0))],
        out_specs=[pl.BlockSpec((ATT_OUT, 4 * LANES), lambda p, j: (j, p)),
                   pl.BlockSpec((1, ATT_OUT, LANES), lambda p, j: (p, j, 0))],
        out_shape=[st((l, 2 * MLA_DI), BF16), st((MLA_HEADS // 2, l, LANES), F32)],
        compiler_params=_cparams(("parallel", "arbitrary")))(qn, qr, do, lse_row, delta_row, kv, krs)


def flash_dq(qn, qr, kv, krs, do, lse, delta, cos, sins, name):
    l = qn.shape[0]
    nq = l // ATT_OUT

    def body(qn_ref, qr_ref, do_ref, lse_ref, dl_ref, kv_ref, kr_ref, cos_ref, sin_ref, dqn_ref, dqr_ref):
        qi = pl.program_id(1)
        q_r = qr_ref[...]
        q_n = [qn_ref[:, hh * LANES:(hh + 1) * LANES] for hh in range(2)]
        d_o = [do_ref[:, hh * LANES:(hh + 1) * LANES] for hh in range(2)]
        lse_h = [lse_ref[hh] for hh in range(2)]
        dl_h = [dl_ref[hh] for hh in range(2)]

        def block(j, carry, mask_off):
            sl = pl.ds(pl.multiple_of(j * ATT_IN, ATT_IN), ATT_IN)
            dq2 = list(carry)
            for hh in range(2):
                kn = kv_ref[sl, 2 * hh * LANES:(2 * hh + 1) * LANES]
                v = kv_ref[sl, (2 * hh + 1) * LANES:(2 * hh + 2) * LANES]
                kr = kr_ref[hh, sl, :]
                s = _scores(q_n[hh], q_r, kn, kr, mask_off, False)
                pr = jnp.exp(s - lse_h[hh])
                dp = _dot(d_o[hh], v, DN_NT)
                ds = (pr * (dp - dl_h[hh]) * MLA_SCALE).astype(BF16)
                dq2[hh] = dq2[hh] + _dot(ds, jnp.concatenate([kn, kr], axis=1), DN_NN)
            return tuple(dq2)

        z2 = jnp.zeros((ATT_OUT, 2 * LANES), F32)
        res = lax.fori_loop(0, ATT_R * qi, lambda j, c: block(j, c, None), (z2, z2))
        for d in range(ATT_R):
            res = block(ATT_R * qi + d, res, d * ATT_IN)
        dqn_ref[:, 0:LANES] = res[0][:, :LANES].astype(BF16)
        dqn_ref[:, LANES:2 * LANES] = res[1][:, :LANES].astype(BF16)
        dqr = res[0][:, LANES:] + res[1][:, LANES:]
        dqr_ref[...] = _rope_t(dqr, cos_ref[...], sin_ref[...]).astype(BF16)

    st = jax.ShapeDtypeStruct
    return pl.pallas_call(
        body, name=name, grid=(MLA_HEADS // 2, nq),
        in_specs=[pl.BlockSpec((ATT_OUT, 2 * LANES), lambda p, i: (i, p)),
                  pl.BlockSpec((ATT_OUT, LANES), lambda p, i: (i, p)),
                  pl.BlockSpec((ATT_OUT, 2 * LANES), lambda p, i: (i, p)),
                  pl.BlockSpec((2, ATT_OUT, 1), lambda p, i: (p, i, 0)),
                  pl.BlockSpec((2, ATT_OUT, 1), lambda p, i: (p, i, 0)),
                  pl.BlockSpec((l, 4 * LANES), lambda p, i: (0, p)),
                  pl.BlockSpec((2, l, LANES), lambda p, i: (0, 0, 0)),
                  pl.BlockSpec((ATT_OUT, LANES), lambda p, i: (i, 0)),
                  pl.BlockSpec((ATT_OUT, LANES), lambda p, i: (i, 0))],
        out_specs=[pl.BlockSpec((ATT_OUT, 2 * LANES), lambda p, i: (i, p)),
                   pl.BlockSpec((ATT_OUT, LANES), lambda p, i: (i, p))],
        out_shape=[st((l, MLA_DI), BF16), st((l, MLA_HEADS * MLA_ROPE), BF16)],
        compiler_params=_cparams(("parallel", "arbitrary")))(qn, qr, do, lse, delta, kv, krs, cos, sins)


def mla_gate_fwd(o, a, name):
    l = o.shape[0]
    tl = ROW_TILE

    def body(o_ref, z_ref, m_ref):
        m_ref[...] = (o_ref[...] * _silu(z_ref[...])).astype(BF16)

    return _rowcall(body, name, l // tl, [_rows(tl, MLA_DI), _rows(tl, MLA_DI)], _rows(tl, MLA_DI),
                    jax.ShapeDtypeStruct((l, MLA_DI), BF16))(o, a)


def mla_gate_bwd(dm, o, a, name):
    l = o.shape[0]
    tl = ROW_TILE

    def body(dm_ref, o_ref, z_ref, do_ref, dz_ref, dl_ref):
        dmv, ov, z = dm_ref[...], o_ref[...], z_ref[...]
        sz, gz = _silu_both(z)
        d_o = dmv * sz
        do_ref[...] = d_o.astype(BF16)
        dz_ref[...] = (dmv * ov * gz).astype(BF16)
        pr = d_o * ov
        for h in range(MLA_HEADS):
            dl_ref[h] = jnp.sum(pr[:, h * LANES:(h + 1) * LANES], axis=1, keepdims=True)

    st = jax.ShapeDtypeStruct
    return _rowcall(body, name, l // tl, [_rows(tl, MLA_DI)] * 3,
                    [_rows(tl, MLA_DI), _rows(tl, MLA_DI), pl.BlockSpec((MLA_HEADS, tl, 1), lambda i: (0, i, 0))],
                    [st((l, MLA_DI), BF16), st((l, MLA_DI), BF16), st((MLA_HEADS, l, 1), F32)])(dm, o, a)


def mla_post(a, dcqn, dckvn, dkr_pairs, dz, q_g, kv_g, cos, sins, name):
    l = a.shape[0]
    tl = ROW_TILE
    npair = MLA_HEADS // 2

    def norm_bwd(x, g, dy):
        xhat, r = _rmsn(x)
        dxh = dy * g
        return r * (dxh - xhat * jnp.mean(dxh * xhat, axis=-1, keepdims=True)), jnp.sum(dy * xhat, axis=0, keepdims=True)

    def body(a_ref, dq_ref, dk_ref, dkr_ref, dz_ref, qg_ref, kg_ref, cos_ref, sin_ref, da_ref, dqg_ref, dkg_ref):
        i = pl.program_id(0)
        da_ref[:, 0:MLA_DI] = dz_ref[...]
        dcq, dqg = norm_bwd(a_ref[:, MLA_CQ0:MLA_CKV0], qg_ref[...], dq_ref[...])
        da_ref[:, MLA_CQ0:MLA_CKV0] = dcq.astype(BF16)
        dckv, dkg = norm_bwd(a_ref[:, MLA_CKV0:MLA_KR0], kg_ref[...], dk_ref[...])
        da_ref[:, MLA_CKV0:MLA_KR0] = dckv.astype(BF16)
        dk2 = dkr_ref[0]
        for p in range(1, npair):
            dk2 = dk2 + dkr_ref[p]
        dk2 = _rope_t(dk2, cos_ref[...], sin_ref[...])
        dk2 = dk2 + pltpu.roll(dk2, MLA_ROPE, 1)
        lane = lax.broadcasted_iota(jnp.int32, dk2.shape, 1)
        da_ref[:, MLA_KR0:MLA_AW] = jnp.where(lane < MLA_ROPE, dk2, 0.0).astype(BF16)
        _acc(dqg_ref, dqg, i)
        _acc(dkg_ref, dkg, i)

    st = jax.ShapeDtypeStruct
    return _rowcall(body, name, l // tl,
                    [_rows(tl, MLA_AW), _rows(tl, MLA_Q_RANK), _rows(tl, MLA_KV_RANK),
                     pl.BlockSpec((npair, tl, LANES), lambda i: (0, i, 0)), _rows(tl, MLA_DI),
                     _full((1, MLA_Q_RANK)), _full((1, MLA_KV_RANK)), _rows(tl, LANES), _rows(tl, LANES)],
                    [_rows(tl, MLA_AW), _full((1, MLA_Q_RANK)), _full((1, MLA_KV_RANK))],
                    [st((l, MLA_AW), BF16), st((1, MLA_Q_RANK), F32), st((1, MLA_KV_RANK), F32)])(
        a, dcqn, dckvn, dkr_pairs, dz, q_g.reshape(1, -1), kv_g.reshape(1, -1), cos, sins)


def _mla_w_in_perm(w):
    r = MLA_Q_RANK + MLA_KV_RANK + MLA_ROPE
    pad = jnp.zeros(w.shape[:-1] + (MLA_AW - MLA_KR0 - MLA_ROPE,), w.dtype)
    return jnp.concatenate([w[..., r:], w[..., :r], pad], axis=-1)


def _mla_w_in_unperm(g):
    r = MLA_Q_RANK + MLA_KV_RANK + MLA_ROPE
    return jnp.concatenate([g[..., MLA_DI:MLA_DI + r], g[..., :MLA_DI]], axis=-1)


def _mla_w_uq_split(w):
    k = w.shape[0]
    w3 = w.reshape(k, MLA_HEADS, MLA_NOPE + MLA_ROPE)
    return w3[:, :, :MLA_NOPE].reshape(k, MLA_HEADS * MLA_NOPE), w3[:, :, MLA_NOPE:].reshape(k, MLA_HEADS * MLA_ROPE)


def _mla_w_uq_merge(gn, gr):
    k = gn.shape[0]
    return jnp.concatenate([gn.reshape(k, MLA_HEADS, MLA_NOPE), gr.reshape(k, MLA_HEADS, MLA_ROPE)], axis=2).reshape(k, -1)


def mla_layer_fwd(h, p, wf, cos, sins, tag):
    hn = rms_fwd(h, p["norm_g"], tag + "_rms")
    w_in = _mla_w_in_perm(wf["w_in"])
    w_uq_n, w_uq_r = _mla_w_uq_split(wf["w_uq"])
    a = matmul(hn, w_in, "nn", tag + "_mm_in")
    cqn, ckvn, krs = mla_pre(a, p["q_norm_g"], p["kv_norm_g"], cos, sins, tag + "_pre")
    qn = matmul(cqn, w_uq_n, "nn", tag + "_mm_qn", out_dtype=BF16)
    qr_raw = matmul(cqn, w_uq_r, "nn", tag + "_mm_qr")
    qr = mla_rope_q(qr_raw, cos, sins, tag + "_rope_q")
    kv = matmul(ckvn, wf["w_ukv"], "nn", tag + "_mm_kv", out_dtype=BF16)
    o, lse = flash_fwd(qn, qr, kv, krs, tag + "_flash")
    m = mla_gate_fwd(o, a, tag + "_gate")
    h_out = matmul(m, wf["w_out"], "nn", tag + "_mm_out", add=h)
    return h_out, (h, hn, a, cqn, ckvn, krs, qn, qr, kv, o, lse, m, w_in, w_uq_n, w_uq_r)


def mla_layer_bwd(dh_out, saved, p, wf, cos, sins, tag):
    h, hn, a, cqn, ckvn, krs, qn, qr, kv, o, lse, m, w_in, w_uq_n, w_uq_r = saved
    l = h.shape[0]
    dm = matmul(dh_out, wf["w_out"], "nt", tag + "_mm_dm")
    g_w_out = matmul(m, dh_out, "tn", tag + "_mm_gwout")
    do, dz, delta = mla_gate_bwd(dm, o, a, tag + "_gate_bwd")
    lse_row = lse.reshape(MLA_HEADS, l // ATT_IN, 1, ATT_IN)
    delta_row = delta.reshape(MLA_HEADS, l // ATT_IN, 1, ATT_IN)
    dkv, dkr_pairs = flash_dkv(qn, qr, kv, krs, do, lse_row, delta_row, tag + "_flash_dkv")
    dqn, dqr = flash_dq(qn, qr, kv, krs, do, lse, delta, cos, sins, tag + "_flash_dq")
    dcqn = matmul(dqn, w_uq_n, "nt", tag + "_mm_dcq_n")
    dcqn = matmul(dqr, w_uq_r, "nt", tag + "_mm_dcq_r", add=dcqn)
    g_uq_n = matmul(cqn, dqn, "tn", tag + "_mm_guq_n")
    g_uq_r = matmul(cqn, dqr, "tn", tag + "_mm_guq_r")
    dckvn = matmul(dkv, wf["w_ukv"], "nt", tag + "_mm_dckv")
    g_ukv = matmul(ckvn, dkv, "tn", tag + "_mm_gukv")
    da, dqg, dkg = mla_post(a, dcqn, dckvn, dkr_pairs, dz, p["q_norm_g"], p["kv_norm_g"], cos, sins, tag + "_post")
    dhn = matmul(da, w_in, "nt", tag + "_mm_dhn")
    g_w_in = matmul(hn, da, "tn", tag + "_mm_gwin")
    dh, dng = rms_bwd(h, p["norm_g"], dhn, dh_out, tag + "_rms_bwd")
    grads = {"norm_g": dng.reshape(-1), "w_in": _mla_w_in_unperm(g_w_in), "q_norm_g": dqg.reshape(-1),
             "w_uq": _mla_w_uq_merge(g_uq_n, g_uq_r), "kv_norm_g": dkg.reshape(-1), "w_ukv": g_ukv, "w_out": g_w_out}
    return dh, grads


ANY = pl.BlockSpec(memory_space=pl.ANY)


def _me():
    return lax.axis_index("x"), lax.axis_index("y"), lax.axis_index("c")


def _chip():
    return 2 * lax.axis_index("x") + lax.axis_index("y")


def _other_chips(x, y):
    return [(1 - x, y), (x, 1 - y), (1 - x, 1 - y)]


def _rcopy(src, dst, ssem, rsem, dev):
    return pltpu.make_async_remote_copy(src_ref=src, dst_ref=dst, send_sem=ssem, recv_sem=rsem,
                                        device_id=dev, device_id_type=MESH)


def _half(ref, c, hf):
    return ref.at[pl.ds(c * hf, hf), :]


def weights_allgather(wb):
    nr, w = wb.shape
    hf = nr // 2

    def body(w_ref, o_ref, ssem, rsem):
        x, y, c = _me()
        k = 2 * x + y
        chips = _other_chips(x, y)
        first = [_rcopy(_half(w_ref, c, hf), _half(o_ref.at[k], c, hf), ssem.at[j], rsem.at[j], (cx, cy, c))
                 for j, (cx, cy) in enumerate(chips)]
        for cp in first:
            cp.start()
        passed = []
        for j, (cx, cy) in enumerate(chips):
            region = _half(o_ref.at[2 * cx + cy], c, hf)
            _rcopy(region, region, ssem.at[j], rsem.at[j], (cx, cy, c)).wait_recv()
            fwd = _rcopy(region, region, ssem.at[3 + j], rsem.at[3 + j], (x, y, 1 - c))
            fwd.start()
            passed.append(fwd)
        for j, (cx, cy) in enumerate(chips):
            region = _half(o_ref.at[2 * cx + cy], 1 - c, hf)
            _rcopy(region, region, ssem.at[3 + j], rsem.at[3 + j], (x, y, 1 - c)).wait_recv()
        for cp in first + passed:
            cp.wait_send()

    out = pl.pallas_call(
        body, name="weights_allgather", in_specs=[ANY], out_specs=ANY,
        out_shape=jax.ShapeDtypeStruct((N_CHIPS, nr, w), wb.dtype),
        scratch_shapes=[pltpu.SemaphoreType.DMA((6,)), pltpu.SemaphoreType.DMA((6,))],
    )(wb)
    return lax.dynamic_update_slice(out, wb[None], (_chip(), 0, 0))


HBM = pl.BlockSpec(memory_space=pltpu.HBM)
SEM = pl.BlockSpec(memory_space=pltpu.SEMAPHORE)
SPLIT_EFFECT = pltpu.SideEffectType.DATAFLOW_SIDE_EFFECTING


def gather_start(wb, after, name):
    nr, w = wb.shape
    hf = nr // 2

    def body(w_ref, land_ref, after_ref, ssem, rsem, w_thru, land_thru, token):
        x, y, c = _me()
        k = 2 * x + y
        for j, (cx, cy) in enumerate(_other_chips(x, y)):
            _rcopy(_half(w_ref, c, hf), _half(land_ref.at[k], c, hf), ssem.at[j], rsem.at[j], (cx, cy, c)).start()
        token[...] = jnp.zeros_like(token)

    land = lax.empty((N_CHIPS, nr, w), wb.dtype)
    return pl.pallas_call(
        body, name=name,
        out_shape=(pltpu.SemaphoreType.DMA((3,)), pltpu.SemaphoreType.DMA((3,)), pltpu.HBM(wb.shape, wb.dtype),
                   pltpu.HBM(land.shape, land.dtype), jax.ShapeDtypeStruct((8, LANES), F32)),
        in_specs=(HBM, HBM, ANY), out_specs=(SEM, SEM, HBM, HBM, pl.BlockSpec(memory_space=pltpu.VMEM)),
        input_output_aliases={0: 2, 1: 3},
        compiler_params=pltpu.CompilerParams(has_side_effects=SPLIT_EFFECT))(
        pltpu.with_memory_space_constraint(wb, pltpu.HBM), pltpu.with_memory_space_constraint(land, pltpu.HBM), after)


def gather_wait(ssem, rsem, w_thru, land_thru, after, name):
    nr, w = w_thru.shape
    hf = nr // 2

    def body(w_ref, land_ref, ssem_ref, rsem_ref, after_ref, w_dead, got_ref):
        x, y, c = _me()
        for j, (cx, cy) in enumerate(_other_chips(x, y)):
            cp = _rcopy(_half(w_ref, c, hf), _half(land_ref.at[2 * cx + cy], c, hf), ssem_ref.at[j], rsem_ref.at[j],
                        (cx, cy, c))
            cp.wait_send()
            cp.wait_recv()

    return pl.pallas_call(
        body, name=name, out_shape=(pltpu.HBM(w_thru.shape, w_thru.dtype), pltpu.HBM(land_thru.shape, land_thru.dtype)),
        in_specs=(HBM, HBM, SEM, SEM, ANY), out_specs=(HBM, HBM), input_output_aliases={0: 0, 1: 1},
        compiler_params=pltpu.CompilerParams(has_side_effects=SPLIT_EFFECT))(w_thru, land_thru, ssem, rsem, after)[1]


def gather_handover(land, wb, name):
    _, nr, w = land.shape
    hf = nr // 2

    def body(l_ref, o_ref, ssem, rsem):
        x, y, c = _me()
        chips = _other_chips(x, y)
        sends = []
        for j, (cx, cy) in enumerate(chips):
            region = _half(o_ref.at[2 * cx + cy], c, hf)
            sends.append(_rcopy(region, region, ssem.at[j], rsem.at[j], (x, y, 1 - c)))
            sends[-1].start()
        for j, (cx, cy) in enumerate(chips):
            region = _half(o_ref.at[2 * cx + cy], 1 - c, hf)
            _rcopy(region, region, ssem.at[j], rsem.at[j], (x, y, 1 - c)).wait_recv()
        for cp in sends:
            cp.wait_send()

    out = pl.pallas_call(
        body, name=name, in_specs=[ANY], out_specs=ANY, input_output_aliases={0: 0},
        out_shape=jax.ShapeDtypeStruct(land.shape, land.dtype),
        scratch_shapes=[pltpu.SemaphoreType.DMA((3,)), pltpu.SemaphoreType.DMA((3,))])(land)
    return lax.dynamic_update_slice(out, wb[None], (_chip(), 0, 0))


def grads_to_sibling(ps):
    n = len(ps)

    def body(*refs):
        p_refs, o_refs, ssem, rsem = refs[:n], refs[n:2 * n], refs[2 * n], refs[2 * n + 1]
        x, y, c = _me()
        cps = []
        for a in range(n):
            hf = ps[a].shape[1] // 2
            cps.append(_rcopy(p_refs[a].at[:, pl.ds((1 - c) * hf, hf), :], o_refs[a], ssem.at[a], rsem.at[a],
                              (x, y, 1 - c)))
        for cp in cps:
            cp.start()
        for cp in cps:
            cp.wait()

    return pl.pallas_call(
        body, name="grads_to_sibling", in_specs=[ANY] * n, out_specs=[ANY] * n,
        out_shape=[jax.ShapeDtypeStruct((N_CHIPS, p.shape[1] // 2, p.shape[2]), p.dtype) for p in ps],
        scratch_shapes=[pltpu.SemaphoreType.DMA((n,)), pltpu.SemaphoreType.DMA((n,))])(*ps)


def pair_sum(p, ra, out_dtype, name):
    _, nr, w = p.shape
    hf = nr // 2
    tr = _pick_rows(hf)
    nb = hf // tr

    def body(c_ref, p_ref, r_ref, o_ref):
        o_ref[...] = (p_ref[...] + r_ref[...]).astype(out_dtype)

    c = lax.axis_index("c").astype(jnp.int32).reshape(1)
    return pl.pallas_call(
        body, name=name,
        grid_spec=pltpu.PrefetchScalarGridSpec(
            num_scalar_prefetch=1, grid=(N_CHIPS, nb),
            in_specs=[pl.BlockSpec((1, tr, w), lambda k, i, c_ref: (k, c_ref[0] * nb + i, 0)),
                      pl.BlockSpec((1, tr, w), lambda k, i, c_ref: (k, i, 0))],
            out_specs=pl.BlockSpec((1, tr, w), lambda k, i, c_ref: (k, i, 0))),
        out_shape=jax.ShapeDtypeStruct((N_CHIPS, hf, w), out_dtype),
        compiler_params=_cparams(("parallel", "parallel")))(c, p, ra)


def grads_across_chips(ts):
    n = len(ts)

    def body(*refs):
        t_refs, o_refs, ssem, rsem = refs[:n], refs[n:2 * n], refs[2 * n], refs[2 * n + 1]
        x, y, c = _me()
        k = 2 * x + y
        chips = _other_chips(x, y)
        sends = [_rcopy(t_refs[a].at[2 * cx + cy], o_refs[a].at[k], ssem.at[3 * a + j], rsem.at[3 * a + j], (cx, cy, c))
                 for a in range(n) for j, (cx, cy) in enumerate(chips)]
        for cp in sends:
            cp.start()
        for a in range(n):
            for j, (cx, cy) in enumerate(chips):
                _rcopy(t_refs[a].at[k], o_refs[a].at[2 * cx + cy], ssem.at[3 * a + j], rsem.at[3 * a + j],
                       (cx, cy, c)).wait_recv()
        for cp in sends:
            cp.wait_send()

    return pl.pallas_call(
        body, name="grads_across_chips", in_specs=[ANY] * n, out_specs=[ANY] * n,
        out_shape=[jax.ShapeDtypeStruct(t.shape, t.dtype) for t in ts],
        scratch_shapes=[pltpu.SemaphoreType.DMA((3 * n,)), pltpu.SemaphoreType.DMA((3 * n,))])(*ts)


def chip_sum(t, rb, name):
    _, hf, w = rb.shape
    tr = _pick_rows(hf)
    nb = hf // tr

    def body(kc_ref, t_ref, r_ref, o_ref):
        k = kc_ref[0]
        acc = jnp.where(k == 0, t_ref[0], r_ref[0]).astype(F32)
        for j in range(1, N_CHIPS):
            acc = acc + jnp.where(k == j, t_ref[0], r_ref[j]).astype(F32)
        o_ref[...] = acc

    kc = jnp.stack([_chip(), lax.axis_index("c")]).astype(jnp.int32)
    return pl.pallas_call(
        body, name=name,
        grid_spec=pltpu.PrefetchScalarGridSpec(
            num_scalar_prefetch=1, grid=(nb,),
            in_specs=[pl.BlockSpec((1, tr, w), lambda i, kc_ref: (kc_ref[0], i, 0)),
                      pl.BlockSpec((N_CHIPS, tr, w), lambda i, kc_ref: (0, i, 0))],
            out_specs=pl.BlockSpec((tr, w), lambda i, kc_ref: (kc_ref[1] * nb + i, 0))),
        out_shape=jax.ShapeDtypeStruct((2 * hf, w), F32), compiler_params=_cparams(("parallel",)))(kc, t, rb)


def reduced_to_sibling(gs):
    n = len(gs)

    def body(*refs):
        o_refs, ssem, rsem = refs[n:2 * n], refs[2 * n], refs[2 * n + 1]
        x, y, c = _me()
        cps = []
        for a in range(n):
            hf = gs[a].shape[0] // 2
            cps.append(_rcopy(_half(o_refs[a], c, hf), _half(o_refs[a], c, hf), ssem.at[a], rsem.at[a], (x, y, 1 - c)))
        for cp in cps:
            cp.start()
        for a in range(n):
            hf = gs[a].shape[0] // 2
            _rcopy(_half(o_refs[a], c, hf), _half(o_refs[a], 1 - c, hf), ssem.at[a], rsem.at[a],
                   (x, y, 1 - c)).wait_recv()
        for cp in cps:
            cp.wait_send()

    return pl.pallas_call(
        body, name="reduced_to_sibling", in_specs=[ANY] * n, out_specs=[ANY] * n,
        input_output_aliases={a: a for a in range(n)},
        out_shape=[jax.ShapeDtypeStruct(g.shape, g.dtype) for g in gs],
        scratch_shapes=[pltpu.SemaphoreType.DMA((n,)), pltpu.SemaphoreType.DMA((n,))])(*gs)


def small_allgather(g, row0, nrs):
    w = g.shape[1]

    def body(g_ref, o_ref, ssem, rsem):
        x, y, c = _me()
        k = 2 * x + y
        chips = _other_chips(x, y)
        src = g_ref.at[pl.ds(row0, nrs), :]
        sends = [_rcopy(src, o_ref.at[k], ssem.at[j], rsem.at[j], (cx, cy, c)) for j, (cx, cy) in enumerate(chips)]
        for cp in sends:
            cp.start()
        for j, (cx, cy) in enumerate(chips):
            _rcopy(src, o_ref.at[2 * cx + cy], ssem.at[j], rsem.at[j], (cx, cy, c)).wait_recv()
        for cp in sends:
            cp.wait_send()

    out = pl.pallas_call(
        body, name="small_allgather", in_specs=[ANY], out_specs=ANY,
        out_shape=jax.ShapeDtypeStruct((N_CHIPS, nrs, w), g.dtype),
        scratch_shapes=[pltpu.SemaphoreType.DMA((3,)), pltpu.SemaphoreType.DMA((3,))])(g)
    return lax.dynamic_update_slice(out, g[row0:row0 + nrs][None], (_chip(), 0, 0))


def adamw(w, g, m, v, name):
    r, wd = w.shape
    tr = _pick_rows(r, cap=max(16, ADAMW_BLOCK_BYTES // (4 * wd)))
    bc1 = 1.0 - ADAM_B1 ** ADAM_STEP
    bc2 = 1.0 - ADAM_B2 ** ADAM_STEP

    def body(w_ref, g_ref, m_ref, v_ref, d_ref, nm_ref, nv_ref):
        gv = g_ref[...]
        nm = ADAM_B1 * m_ref[...] + (1.0 - ADAM_B1) * gv
        nv = ADAM_B2 * v_ref[...] + (1.0 - ADAM_B2) * (gv * gv)
        nm_ref[...] = nm
        nv_ref[...] = nv
        d_ref[...] = -ADAM_LR * ((nm / bc1) / (jnp.sqrt(nv / bc2) + ADAM_EPS) + ADAM_WD * w_ref[...])

    spec = pl.BlockSpec((tr, wd), lambda i: (i, 0))
    st = jax.ShapeDtypeStruct((r, wd), F32)
    return pl.pallas_call(body, name=name, grid=(r // tr,), in_specs=[spec] * 4, out_specs=[spec] * 3,
                          out_shape=[st, st, st], compiler_params=_cparams(("parallel",)))(w, g, m, v)


LAYER_KINDS = ("gmlp", "s5", "mla", "gmlp")
PARAMS = {
    "gmlp": ("norm_g", "w_in", "ln_g", "ln_b", "w_s", "b_s", "w_out"),
    "s5": ("norm_g", "w_in", "a_re", "a_im", "log_step", "b_re", "b_im", "c_re", "c_im", "d_skip", "w_glu", "b_glu", "w_out"),
    "mla": ("norm_g", "w_in", "q_norm_g", "w_uq", "kv_norm_g", "w_ukv", "w_out"),
}
COL_SHARDED = ("w_in", "w_uq", "w_ukv")
ROW_SHARDED = ("w_out", "w_glu")
WEIGHT_NAMES = [("l%d_" % i) + n for i, kind in enumerate(LAYER_KINDS) for n in PARAMS[kind]] + ["final_norm_g"]


def _is_big(name):
    return name.split("_", 1)[1] in COL_SHARDED + ROW_SHARDED


BIG = [n for n in WEIGHT_NAMES if _is_big(n)]
SMALL = [n for n in WEIGHT_NAMES if not _is_big(n)]


def _pack_rows(blocks):
    return jnp.concatenate([b.reshape(-1, PACK_W) for b in blocks], axis=0)


def _shard_major(name, full):
    r, c = full.shape
    if name.split("_", 1)[1] in COL_SHARDED:
        t = full.reshape(r, N_CHIPS, c // N_CHIPS).transpose(1, 0, 2)
    else:
        t = full.reshape(N_CHIPS, r // N_CHIPS, c)
    return t.reshape(N_CHIPS, -1, PACK_W)


def _from_shard_major(name, t, block_shape):
    r, c = block_shape
    if name.split("_", 1)[1] in COL_SHARDED:
        return t.reshape(N_CHIPS, r, c).transpose(1, 0, 2).reshape(r, N_CHIPS * c)
    return t.reshape(N_CHIPS * r, c)


def _small_pack(arrs, total_padded):
    flat = jnp.concatenate([a.reshape(-1) for a in arrs])
    return jnp.pad(flat, (0, total_padded - flat.shape[0]))


def kernel(x, positions, l0_norm_g, l0_w_in, l0_ln_g, l0_ln_b, l0_w_s, l0_b_s, l0_w_out, l1_norm_g, l1_w_in, l1_a_re, l1_a_im, l1_log_step, l1_b_re, l1_b_im, l1_c_re, l1_c_im, l1_d_skip, l1_w_glu, l1_b_glu, l1_w_out, l2_norm_g, l2_w_in, l2_q_norm_g, l2_w_uq, l2_kv_norm_g, l2_w_ukv, l2_w_out, l3_norm_g, l3_w_in, l3_ln_g, l3_ln_b, l3_w_s, l3_b_s, l3_w_out, final_norm_g, loss_target, m_l0_norm_g, m_l0_w_in, m_l0_ln_g, m_l0_ln_b, m_l0_w_s, m_l0_b_s, m_l0_w_out, m_l1_norm_g, m_l1_w_in, m_l1_a_re, m_l1_a_im, m_l1_log_step, m_l1_b_re, m_l1_b_im, m_l1_c_re, m_l1_c_im, m_l1_d_skip, m_l1_w_glu, m_l1_b_glu, m_l1_w_out, m_l2_norm_g, m_l2_w_in, m_l2_q_norm_g, m_l2_w_uq, m_l2_kv_norm_g, m_l2_w_ukv, m_l2_w_out, m_l3_norm_g, m_l3_w_in, m_l3_ln_g, m_l3_ln_b, m_l3_w_s, m_l3_b_s, m_l3_w_out, m_final_norm_g, v_l0_norm_g, v_l0_w_in, v_l0_ln_g, v_l0_ln_b, v_l0_w_s, v_l0_b_s, v_l0_w_out, v_l1_norm_g, v_l1_w_in, v_l1_a_re, v_l1_a_im, v_l1_log_step, v_l1_b_re, v_l1_b_im, v_l1_c_re, v_l1_c_im, v_l1_d_skip, v_l1_w_glu, v_l1_b_glu, v_l1_w_out, v_l2_norm_g, v_l2_w_in, v_l2_q_norm_g, v_l2_w_uq, v_l2_kv_norm_g, v_l2_w_ukv, v_l2_w_out, v_l3_norm_g, v_l3_w_in, v_l3_ln_g, v_l3_ln_b, v_l3_w_s, v_l3_b_s, v_l3_w_out, v_final_norm_g):
    args = locals()
    w = {n: args[n] for n in WEIGHT_NAMES}
    mom_m = {n: args["m_" + n] for n in WEIGHT_NAMES}
    mom_v = {n: args["v_" + n] for n in WEIGHT_NAMES}
    h0 = x[0]
    target = loss_target[0]
    pos = positions.reshape(-1, 1)

    big_rows = [w[n].size // PACK_W for n in BIG]
    nrb = sum(big_rows)
    nrb_pad = -(-nrb // PACK_ROW_ALIGN) * PACK_ROW_ALIGN
    full = {}

    def pack_unit(layers):
        names = [n for n in BIG if int(n[1]) in layers]
        rows = [w[n].size // PACK_W for n in names]
        pad = -sum(rows) % PACK_ROW_ALIGN
        return names, rows, _pack_rows([w[n].astype(BF16) for n in names] + [jnp.zeros((pad, PACK_W), BF16)])

    def unpack_unit(names, rows, gathered):
        r0 = 0
        for n, nr in zip(names, rows):
            full[n] = _from_shard_major(n, gathered[:, r0:r0 + nr, :], w[n].shape)
            r0 += nr

    unit0, unit1, unit2 = pack_unit((0,)), pack_unit((1,)), pack_unit((2, 3))
    unpack_unit(unit0[0], unit0[1], weights_allgather(unit0[2]))
    flight = gather_start(unit1[2], unit0[2], "gather_l1_start")
    wp = dict(w)
    wp["l0_norm_g"] = w["l0_norm_g"] + flight[4][0, 0]

    def layer_params(i):
        pre = "l%d_" % i
        p = {k[len(pre):]: v for k, v in wp.items() if k.startswith(pre)}
        wf = {k[len(pre):]: v for k, v in full.items() if k.startswith(pre)}
        return p, wf

    cos, sins = rope_tables(pos)
    h = h0
    saved = []
    for i, kind in enumerate(LAYER_KINDS):
        if i == 1:
            land = gather_wait(*flight[:4], h, "gather_l1_wait")
            got = gather_handover(land, unit1[2], "gather_l1_handover")
            unpack_unit(unit1[0], unit1[1], got)
            flight = gather_start(unit2[2], got, "gather_l23_start")
            wp["l1_norm_g"] = w["l1_norm_g"] + flight[4][0, 0]
        if i == 2:
            land = gather_wait(*flight[:4], h, "gather_l23_wait")
            unpack_unit(unit2[0], unit2[1], gather_handover(land, unit2[2], "gather_l23_handover"))
        p, wf = layer_params(i)
        tag = "l%d" % i
        if kind == "gmlp":
            h, s = gmlp_layer_fwd(h, p, wf, tag)
        elif kind == "s5":
            h, s = s5_layer_fwd(h, p, wf, tag)
        else:
            h, s = mla_layer_fwd(h, p, wf, cos, sins, tag)
        saved.append(s)
    loss_part, dh, g_final = loss_head(h, final_norm_g, target)
    loss = lax.psum(loss_part[0, 0], ("x", "y", "c"))

    grads = {"final_norm_g": g_final.reshape(-1)}
    for i in reversed(range(len(LAYER_KINDS))):
        kind = LAYER_KINDS[i]
        p, wf = layer_params(i)
        tag = "l%d" % i
        if kind == "gmlp":
            dh, g = gmlp_layer_bwd(dh, saved[i], p, wf, tag)
        elif kind == "s5":
            dh, g = s5_layer_bwd(dh, saved[i], p, wf, tag)
        else:
            dh, g = mla_layer_bwd(dh, saved[i], p, wf, cos, sins, tag)
        for k, val in g.items():
            grads["l%d_%s" % (i, k)] = val
    grad_x = dh[None]

    n_small = sum(w[n].size for n in SMALL)
    piece = N_CHIPS * 2 * 16 * PACK_W
    n_small_pad = -(-n_small // piece) * piece
    nrs = n_small_pad // N_CHIPS // PACK_W
    p_big = jnp.concatenate([_shard_major(n, grads[n]) for n in BIG]
                            + [jnp.zeros((N_CHIPS, nrb_pad - nrb, PACK_W), F32)], axis=1)
    p_small = _small_pack([grads[n] for n in SMALL], n_small_pad).reshape(N_CHIPS, nrs, PACK_W)
    sib_big, sib_small = grads_to_sibling([p_big, p_small])
    t_big = pair_sum(p_big, sib_big, BF16, "pair_sum_big")
    t_small = pair_sum(p_small, sib_small, F32, "pair_sum_small")
    rb_big, rb_small = grads_across_chips([t_big, t_small])
    g_big, red_small = reduced_to_sibling([chip_sum(t_big, rb_big, "chip_sum_big"),
                                           chip_sum(t_small, rb_small, "chip_sum_small")])
    small_all = small_allgather(red_small, 0, nrs)

    g_out, d_out, nm_out, nv_out = {}, {}, {}, {}
    r0 = 0
    for n, nr in zip(BIG, big_rows):
        g_out[n] = g_big[r0:r0 + nr].reshape(w[n].shape)
        d_out[n], nm_out[n], nv_out[n] = adamw(w[n], g_out[n], mom_m[n], mom_v[n], "adamw_" + n)
        r0 += nr
    g_small = small_all.reshape(-1, PACK_W)
    sp = lambda d: _small_pack([d[n] for n in SMALL], n_small_pad).reshape(-1, PACK_W)
    d_small, nm_small, nv_small = adamw(sp(w), g_small, sp(mom_m), sp(mom_v), "adamw_small")
    for buf, out in ((g_small, g_out), (d_small, d_out), (nm_small, nm_out), (nv_small, nv_out)):
        flat = buf.reshape(-1)
        o = 0
        for n in SMALL:
            out[n] = flat[o:o + w[n].size].reshape(w[n].shape)
            o += w[n].size
    return (loss, grad_x, *[g_out[n] for n in WEIGHT_NAMES], *[d_out[n] for n in WEIGHT_NAMES],
            *[nm_out[n] for n in WEIGHT_NAMES], *[nv_out[n] for n in WEIGHT_NAMES])
```

```python
import functools
import math

import jax
import jax.numpy as jnp
import numpy as np
from jax import lax
from jax.experimental import pallas as pl
from jax.experimental.pallas import tpu as pltpu

F32 = jnp.float32
BF16 = jnp.bfloat16
MESH = pl.DeviceIdType.MESH
VMEM_LIMIT_BYTES = 56 * 1024 * 1024
LANES = 128
PACK_W = 1024
PACK_ROW_ALIGN = 256
ROW_TILE = 256
ADAMW_BLOCK_BYTES = 1024 * 1024
MM_BLOCK_BYTES = 6 * 1024 * 1024

NORM_EPS = 1e-6
N_CHIPS = 4
GMLP_CHUNK = 128
GMLP_GROUPS = 8
S5_GROUPS = 128
S5_GROUP = 16
S5_STATE = 64
S5_SB = 16
S5_SEG = 8
MLA_HEADS = 16
MLA_NOPE = 128
MLA_ROPE = 64
MLA_Q_RANK = 384
MLA_KV_RANK = 128
MLA_SCALE = (MLA_NOPE + MLA_ROPE) ** -0.5
ROPE_THETA = 10000.0
NEG_INF = -1e30
ADAM_LR, ADAM_B1, ADAM_B2, ADAM_EPS, ADAM_WD, ADAM_STEP = 0.001, 0.9, 0.999, 1e-08, 0.01, 10

DN_NN = (((1,), (0,)), ((), ()))
DN_NT = (((1,), (1,)), ((), ()))
DN_TN = (((0,), (0,)), ((), ()))


def _cparams(sem):
    return pltpu.CompilerParams(dimension_semantics=sem, vmem_limit_bytes=VMEM_LIMIT_BYTES)


def _pick(n, cands=(512, 384, 256, 128)):
    for c in cands:
        if n % c == 0:
            return c
    return n


def _pick_rows(r, cap=512, mult=16):
    return max(t for t in range(mult, cap + 1, mult) if r % t == 0)


def _dot(a, b, dn):
    return lax.dot_general(a.astype(BF16), b.astype(BF16), dn, preferred_element_type=F32)


def _sigmoid(x):
    return 1.0 / (1.0 + jnp.exp(-x))


def _gelu(x):
    c = math.sqrt(2.0 / math.pi)
    t = jnp.tanh(c * (x + 0.044715 * x * x * x))
    return 0.5 * x * (1.0 + t)


def _gelu_grad(x):
    c = math.sqrt(2.0 / math.pi)
    t = jnp.tanh(c * (x + 0.044715 * x * x * x))
    return 0.5 * (1.0 + t) + 0.5 * x * (1.0 - t * t) * c * (1.0 + 3.0 * 0.044715 * x * x)


def _gelu_both(x):
    c = math.sqrt(2.0 / math.pi)
    t = jnp.tanh(c * (x + 0.044715 * x * x * x))
    return 0.5 * x * (1.0 + t), 0.5 * (1.0 + t) + 0.5 * x * (1.0 - t * t) * c * (1.0 + 3.0 * 0.044715 * x * x)


def _silu_both(z):
    s = _sigmoid(z)
    return z * s, s * (1.0 + z * (1.0 - s))


def _silu(z):
    return z * _sigmoid(z)


def _silu_grad(z):
    s = _sigmoid(z)
    return s * (1.0 + z * (1.0 - s))


def matmul(a, b, mode, name, out_dtype=F32, add=None):
    if mode == "nn":
        (m, k), n = a.shape, b.shape[1]
    elif mode == "nt":
        (m, k), n = a.shape, b.shape[0]
    else:
        (k, m), n = a.shape, b.shape[1]
    tm = _pick(m, [t for t in (1024, 512, 384, 256, 128) if t * k * a.dtype.itemsize <= MM_BLOCK_BYTES])
    tn = _pick(n, [t for t in (512, 384, 256, 128) if t * k * b.dtype.itemsize <= MM_BLOCK_BYTES])
    dn = {"nn": DN_NN, "nt": DN_NT, "tn": DN_TN}[mode]

    def body(*refs):
        if add is None:
            a_ref, b_ref, o_ref = refs
        else:
            a_ref, b_ref, add_ref, o_ref = refs
        r = _dot(a_ref[...], b_ref[...], dn)
        if add is not None:
            r = r + add_ref[...].astype(F32)
        o_ref[...] = r.astype(out_dtype)

    a_spec = pl.BlockSpec((k, tm), lambda i, j: (0, i)) if mode == "tn" else pl.BlockSpec((tm, k), lambda i, j: (i, 0))
    b_spec = pl.BlockSpec((tn, k), lambda i, j: (j, 0)) if mode == "nt" else pl.BlockSpec((k, tn), lambda i, j: (0, j))
    o_spec = pl.BlockSpec((tm, tn), lambda i, j: (i, j))
    in_specs = [a_spec, b_spec] + ([o_spec] if add is not None else [])
    args = (a, b) + ((add,) if add is not None else ())
    return pl.pallas_call(
        body, name=name, grid=(m // tm, n // tn), in_specs=in_specs, out_specs=o_spec,
        out_shape=jax.ShapeDtypeStruct((m, n), out_dtype),
        compiler_params=_cparams(("parallel", "arbitrary")))(*args)


def _rows(tl, w, col=0):
    return pl.BlockSpec((tl, w), lambda i: (i, col))


def _full(shape):
    nd = len(shape)
    return pl.BlockSpec(tuple(shape), lambda i: (0,) * nd)


def _rowcall(body, name, n_steps, in_specs, out_specs, out_shape, scratch=()):
    return pl.pallas_call(
        body, name=name, grid=(n_steps,), in_specs=in_specs, out_specs=out_specs, out_shape=out_shape,
        scratch_shapes=list(scratch), compiler_params=_cparams(("arbitrary",)))


def _acc(ref, val, i):
    @pl.when(i == 0)
    def _():
        ref[...] = val

    @pl.when(i != 0)
    def _():
        ref[...] += val


def rms_fwd(h, g, name):
    l, d = h.shape
    tl = ROW_TILE

    def body(h_ref, g_ref, o_ref):
        x = h_ref[...]
        r = lax.rsqrt(jnp.mean(x * x, axis=-1, keepdims=True) + NORM_EPS)
        o_ref[...] = (x * r * g_ref[...]).astype(BF16)

    return _rowcall(body, name, l // tl, [_rows(tl, d), _full((1, d))], _rows(tl, d),
                    jax.ShapeDtypeStruct((l, d), BF16))(h, g.reshape(1, d))


def rms_bwd(h, g, dhn, dh_in, name):
    l, d = h.shape
    tl = ROW_TILE

    def body(h_ref, g_ref, dhn_ref, dhi_ref, dh_ref, dg_ref):
        i = pl.program_id(0)
        x = h_ref[...]
        r = lax.rsqrt(jnp.mean(x * x, axis=-1, keepdims=True) + NORM_EPS)
        xhat = x * r
        dy = dhn_ref[...]
        dxh = dy * g_ref[...]
        dx = r * (dxh - xhat * jnp.mean(dxh * xhat, axis=-1, keepdims=True))
        dh_ref[...] = dhi_ref[...] + dx
        _acc(dg_ref, jnp.sum(dy * xhat, axis=0, keepdims=True), i)

    return _rowcall(body, name, l // tl, [_rows(tl, d), _full((1, d)), _rows(tl, d), _rows(tl, d)],
                    [_rows(tl, d), _full((1, d))],
                    [jax.ShapeDtypeStruct((l, d), F32), jax.ShapeDtypeStruct((1, d), F32)])(h, g.reshape(1, d), dhn, dh_in)


def loss_head(h, g, target):
    l, d = h.shape
    tl = ROW_TILE

    def body(h_ref, g_ref, t_ref, loss_ref, dh_ref, dg_ref):
        i = pl.program_id(0)
        x = h_ref[...]
        gg = g_ref[...]
        r = lax.rsqrt(jnp.mean(x * x, axis=-1, keepdims=True) + NORM_EPS)
        xhat = x * r
        err = xhat * gg - t_ref[...]
        part = 0.5 * jnp.sum(jnp.mean(err * err, axis=-1, keepdims=True), axis=0, keepdims=True)
        _acc(loss_ref, part, i)
        dy = err * (1.0 / d)
        dxh = dy * gg
        dh_ref[...] = r * (dxh - xhat * jnp.mean(dxh * xhat, axis=-1, keepdims=True))
        _acc(dg_ref, jnp.sum(dy * xhat, axis=0, keepdims=True), i)

    return _rowcall(body, "loss_head", l // tl, [_rows(tl, d), _full((1, d)), _rows(tl, d)],
                    [_full((1, 1)), _rows(tl, d), _full((1, d))],
                    [jax.ShapeDtypeStruct((1, 1), F32), jax.ShapeDtypeStruct((l, d), F32),
                     jax.ShapeDtypeStruct((1, d), F32)])(h, g.reshape(1, d), target)


def _gmlp_common(a_ref, lng_ref, lnb_ref):
    di = lng_ref.shape[1]
    u_pre = a_ref[:, 0:di]
    v_pre = a_ref[:, di:2 * di]
    z = a_ref[:, 2 * di:3 * di]
    vg = _gelu(v_pre)
    mu = jnp.mean(vg, axis=-1, keepdims=True)
    xc = vg - mu
    rstd = lax.rsqrt(jnp.mean(xc * xc, axis=-1, keepdims=True) + NORM_EPS)
    vhat = xc * rstd
    vn = vhat * lng_ref[...] + lnb_ref[...]
    return u_pre, v_pre, z, vhat, rstd, vn


def _tril(w):
    r = lax.broadcasted_iota(jnp.int32, w.shape, 0)
    c = lax.broadcasted_iota(jnp.int32, w.shape, 1)
    return jnp.where(c <= r, w, 0.0)


def gmlp_gate_fwd(a, ln_g, ln_b, w_s, b_s, name):
    l, w3 = a.shape
    di = w3 // 3
    dg = di // GMLP_GROUPS
    tl = GMLP_CHUNK

    def body(a_ref, lng_ref, lnb_ref, ws_ref, bs_ref, m_ref):
        u_pre, _, z, _, _, vn = _gmlp_common(a_ref, lng_ref, lnb_ref)
        gate = _gelu(u_pre) * _silu(z)
        for g in range(GMLP_GROUPS):
            sl = slice(g * dg, (g + 1) * dg)
            s = _dot(_tril(ws_ref[g]), vn[:, sl], DN_NN) + bs_ref[g]
            m_ref[:, sl] = (gate[:, sl] * s).astype(BF16)

    return _rowcall(body, name, l // tl,
                    [_rows(tl, w3), _full((1, di)), _full((1, di)), _full(w_s.shape), _full((GMLP_GROUPS, tl, 1))],
                    _rows(tl, di), jax.ShapeDtypeStruct((l, di), BF16))(
        a, ln_g.reshape(1, di), ln_b.reshape(1, di), w_s, b_s.reshape(GMLP_GROUPS, tl, 1))


def gmlp_gate_bwd(a, dm, ln_g, ln_b, w_s, b_s, name):
    l, w3 = a.shape
    di = w3 // 3
    dg = di // GMLP_GROUPS
    tl = GMLP_CHUNK

    def body(a_ref, dm_ref, lng_ref, lnb_ref, ws_ref, bs_ref, da_ref, dlg_ref, dlb_ref, dws_ref, dbs_ref,
             dvn_ref, vh_ref, gv_ref):
        i = pl.program_id(0)
        vg, gv = _gelu_both(a_ref[:, di:2 * di])
        gv_ref[...] = gv
        xc = vg - jnp.mean(vg, axis=-1, keepdims=True)
        rstd = lax.rsqrt(jnp.mean(xc * xc, axis=-1, keepdims=True) + NORM_EPS)
        vh_ref[...] = xc * rstd
        for g in range(GMLP_GROUPS):
            sl = slice(g * dg, (g + 1) * dg)
            wt = _tril(ws_ref[g])
            vn_g = vh_ref[:, sl] * lng_ref[:, sl] + lnb_ref[:, sl]
            s = _dot(wt, vn_g, DN_NN) + bs_ref[g]
            dmg = dm_ref[:, sl]
            u, gu = _gelu_both(a_ref[:, sl])
            sz, gz = _silu_both(a_ref[:, 2 * di + g * dg:2 * di + (g + 1) * dg])
            ds = dmg * u * sz
            da_ref[:, sl] = (dmg * s * sz * gu).astype(BF16)
            da_ref[:, 2 * di + g * dg:2 * di + (g + 1) * dg] = (dmg * u * s * gz).astype(BF16)
            dvn_ref[:, sl] = _dot(wt, ds, DN_TN)
            dw = _tril(_dot(ds, vn_g, DN_NT))
            db = jnp.sum(ds, axis=1, keepdims=True)

            @pl.when(i == 0)
            def _():
                dws_ref[g] = dw
                dbs_ref[g] = db

            @pl.when(i != 0)
            def _():
                dws_ref[g] += dw
                dbs_ref[g] += db

        dvn = dvn_ref[...]
        vhat = vh_ref[...]
        dxh = dvn * lng_ref[...]
        dvg = rstd * (dxh - jnp.mean(dxh, axis=-1, keepdims=True) - vhat * jnp.mean(dxh * vhat, axis=-1, keepdims=True))
        da_ref[:, di:2 * di] = (dvg * gv_ref[...]).astype(BF16)
        _acc(dlg_ref, jnp.sum(dvn * vhat, axis=0, keepdims=True), i)
        _acc(dlb_ref, jnp.sum(dvn, axis=0, keepdims=True), i)

    outs = _rowcall(
        body, name, l // tl,
        [_rows(tl, w3), _rows(tl, di), _full((1, di)), _full((1, di)), _full(w_s.shape), _full((GMLP_GROUPS, tl, 1))],
        [_rows(tl, w3), _full((1, di)), _full((1, di)), _full(w_s.shape), _full((GMLP_GROUPS, tl, 1))],
        [jax.ShapeDtypeStruct((l, w3), BF16), jax.ShapeDtypeStruct((1, di), F32), jax.ShapeDtypeStruct((1, di), F32),
         jax.ShapeDtypeStruct(w_s.shape, F32), jax.ShapeDtypeStruct((GMLP_GROUPS, tl, 1), F32)],
        scratch=[pltpu.VMEM((tl, di), F32)] * 3)(
        a, dm, ln_g.reshape(1, di), ln_b.reshape(1, di), w_s, b_s.reshape(GMLP_GROUPS, tl, 1))
    return outs


def gmlp_layer_fwd(h, p, wf, tag):
    hn = rms_fwd(h, p["norm_g"], tag + "_rms")
    a = matmul(hn, wf["w_in"], "nn", tag + "_mm_in")
    m = gmlp_gate_fwd(a, p["ln_g"], p["ln_b"], p["w_s"], p["b_s"], tag + "_gate")
    h_out = matmul(m, wf["w_out"], "nn", tag + "_mm_out", add=h)
    return h_out, (h, hn, a, m)


def gmlp_layer_bwd(dh_out, saved, p, wf, tag, on_big_grads):
    h, hn, a, m = saved
    dm = matmul(dh_out, wf["w_out"], "nt", tag + "_mm_dm")
    g_w_out = matmul(m, dh_out, "tn", tag + "_mm_gwout")
    da, dlg, dlb, dws, dbs = gmlp_gate_bwd(a, dm, p["ln_g"], p["ln_b"], p["w_s"], p["b_s"], tag + "_gate_bwd")
    dhn = matmul(da, wf["w_in"], "nt", tag + "_mm_dhn")
    g_w_in = matmul(hn, da, "tn", tag + "_mm_gwin")
    zero = on_big_grads({"w_in": g_w_in, "w_out": g_w_out})
    dh, dng = rms_bwd(h, p["norm_g"] + zero, dhn, dh_out, tag + "_rms_bwd")
    grads = {"norm_g": dng.reshape(-1), "w_in": g_w_in, "ln_g": dlg.reshape(-1), "ln_b": dlb.reshape(-1),
             "w_s": dws, "b_s": dbs.reshape(GMLP_GROUPS, GMLP_CHUNK), "w_out": g_w_out}
    return dh, grads


def _cmul(ar, ai, br, bi):
    return ar * br - ai * bi, ar * bi + ai * br


S5_PG = 16


def _gblock(tail):
    return pl.BlockSpec((S5_PG,) + tuple(tail), lambda i: (i, 0, 0))


def s5_params_fwd(a_re, a_im, log_step, b_re, b_im):
    g, p, hh = b_re.shape

    def body(ar_ref, ai_ref, ls_ref, br_ref, bi_ref, lr_ref, li_ref, bbr_ref, bbi_ref):
        ar, ai = ar_ref[...], ai_ref[...]
        step = jnp.exp(ls_ref[...])
        mag = jnp.exp(ar * step)
        lr, li = mag * jnp.cos(ai * step), mag * jnp.sin(ai * step)
        den = 1.0 / (ar * ar + ai * ai)
        fr, fi = _cmul(lr - 1.0, li, ar * den, -ai * den)
        lr_ref[...] = lr
        li_ref[...] = li
        bbr, bbi = _cmul(fr, fi, br_ref[...], bi_ref[...])
        bbr_ref[...] = bbr
        bbi_ref[...] = bbi

    s1 = jax.ShapeDtypeStruct((g, p, 1), F32)
    s3 = jax.ShapeDtypeStruct((g, p, hh), F32)
    b1, b0, b3 = _gblock((p, 1)), _gblock((1, 1)), _gblock((p, hh))
    return pl.pallas_call(body, name="s5_params_fwd", grid=(g // S5_PG,), in_specs=[b1, b1, b0, b3, b3],
                          out_specs=[b1, b1, b3, b3], out_shape=[s1, s1, s3, s3],
                          compiler_params=_cparams(("parallel",)))(
        a_re.reshape(g, p, 1), a_im.reshape(g, p, 1), log_step.reshape(g, 1, 1), b_re, b_im)


def s5_params_bwd(a_re, a_im, log_step, b_re, b_im, dl_re, dl_im, dbb_re, dbb_im):
    g, p, hh = b_re.shape

    def body(ar_ref, ai_ref, ls_ref, br_ref, bi_ref, dlr_ref, dli_ref, dbr_ref, dbi_ref,
             gar_ref, gai_ref, gls_ref, gbr_ref, gbi_ref):
        ar, ai = ar_ref[...], ai_ref[...]
        step = jnp.exp(ls_ref[...])
        mag = jnp.exp(ar * step)
        lr, li = mag * jnp.cos(ai * step), mag * jnp.sin(ai * step)
        den = 1.0 / (ar * ar + ai * ai)
        ir, ii = ar * den, -ai * den
        fr, fi = _cmul(lr - 1.0, li, ir, ii)
        br, bi = br_ref[...], bi_ref[...]
        dbr, dbi = dbr_ref[...], dbi_ref[...]
        gbr, gbi = _cmul(fr, -fi, dbr, dbi)
        gbr_ref[...] = gbr
        gbi_ref[...] = gbi
        pr, pi = _cmul(br, -bi, dbr, dbi)
        gfr = jnp.sum(pr, axis=-1, keepdims=True)
        gfi = jnp.sum(pi, axis=-1, keepdims=True)
        t_r, t_i = _cmul(ir, -ii, gfr, gfi)
        glr, gli = dlr_ref[...] + t_r, dli_ref[...] + t_i
        c1r, c1i = _cmul(step * lr, -step * li, glr, gli)
        qr, qi = _cmul(fr, fi, ir, ii)
        c2r, c2i = _cmul(-qr, qi, gfr, gfi)
        gar_ref[...] = c1r + c2r
        gai_ref[...] = c1i + c2i
        wr, wi = _cmul(ar, ai, lr, li)
        sr, _ = _cmul(wr, -wi, glr, gli)
        gls_ref[...] = jnp.sum(sr, axis=1, keepdims=True) * step

    s1 = jax.ShapeDtypeStruct((g, p, 1), F32)
    s3 = jax.ShapeDtypeStruct((g, p, hh), F32)
    b1, b0, b3 = _gblock((p, 1)), _gblock((1, 1)), _gblock((p, hh))
    return pl.pallas_call(body, name="s5_params_bwd", grid=(g // S5_PG,),
                          in_specs=[b1, b1, b0, b3, b3, b1, b1, b3, b3], out_specs=[b1, b1, b0, b3, b3],
                          out_shape=[s1, s1, jax.ShapeDtypeStruct((g, 1, 1), F32), s3, s3],
                          compiler_params=_cparams(("parallel",)))(
        a_re.reshape(g, p, 1), a_im.reshape(g, p, 1), log_step.reshape(g, 1, 1), b_re, b_im,
        dl_re, dl_im, dbb_re, dbb_im)


def _blockdiag(t):
    sb, n, r, c = t.shape
    eye = jnp.eye(n, dtype=bool)[None, :, None, :, None]
    full = jnp.where(eye, t[:, :, :, None, :], jnp.zeros((), t.dtype))
    return full.reshape(sb, n * r, n * c)


def _blockdiag_extract(m, r, c):
    sb = m.shape[0]
    n = m.shape[1] // r
    m5 = m.reshape(sb, n, r, n, c)
    return jnp.stack([m5[:, i, :, i, :] for i in range(n)], axis=1)


S5_TB = 64
S5_UNROLL = 8


def s5_scan_fwd(a_p, lam_re, lam_im, wb_re, wb_im, wc_re, wc_im, d_skip, x0_re, x0_im, name):
    l = a_p.shape[0]
    di = d_skip.shape[1]
    rows = S5_SEG * S5_TB
    nb = l // rows
    ns = wb_re.shape[2]

    def body(u_ref, lr_ref, li_ref, wbr_ref, wbi_ref, wcr_ref, wci_ref, ds_ref, x0r_ref, x0i_ref,
             y_ref, ckr_ref, cki_ref, xer_ref, xei_ref, bur, bui, xr_s, xi_s):
        b = pl.program_id(1)

        @pl.when(b == 0)
        def _():
            xr_s[...] = x0r_ref[0]
            xi_s[...] = x0i_ref[0]

        ckr_ref[0, 0] = xr_s[...]
        cki_ref[0, 0] = xi_s[...]
        u = u_ref[...]
        bur[...] = _dot(u, wbr_ref[0], DN_NN)
        bui[...] = _dot(u, wbi_ref[0], DN_NN)
        lr = jnp.broadcast_to(lr_ref[0], (S5_SEG, ns))
        li = jnp.broadcast_to(li_ref[0], (S5_SEG, ns))

        def step(t, carry):
            xr, xi = carry
            sl = pl.ds(pl.multiple_of(t * S5_SEG, S5_SEG), S5_SEG)
            nr = lr * xr - li * xi + bur[sl, :]
            ni = lr * xi + li * xr + bui[sl, :]
            bur[sl, :] = nr
            bui[sl, :] = ni
            return nr, ni

        xr, xi = lax.fori_loop(0, S5_TB, step, (xr_s[...], xi_s[...]), unroll=S5_UNROLL)
        xr_s[...] = xr
        xi_s[...] = xi
        xer_ref[0] = xr
        xei_ref[0] = xi
        y_ref[...] = _dot(bur[...], wcr_ref[0], DN_NN) - _dot(bui[...], wci_ref[0], DN_NN) + ds_ref[...] * u

    sb3 = lambda s, b: (s, 0, 0)
    st = jax.ShapeDtypeStruct
    return pl.pallas_call(
        body, name=name, grid=(S5_SB, nb),
        in_specs=[pl.BlockSpec((rows, LANES), lambda s, b: (b, s)),
                  pl.BlockSpec((1, 1, ns), sb3), pl.BlockSpec((1, 1, ns), sb3),
                  pl.BlockSpec((1, LANES, ns), sb3), pl.BlockSpec((1, LANES, ns), sb3),
                  pl.BlockSpec((1, ns, LANES), sb3), pl.BlockSpec((1, ns, LANES), sb3),
                  pl.BlockSpec((1, LANES), lambda s, b: (0, s)),
                  pl.BlockSpec((1, S5_SEG, ns), sb3), pl.BlockSpec((1, S5_SEG, ns), sb3)],
        out_specs=[pl.BlockSpec((rows, LANES), lambda s, b: (b, s)),
                   pl.BlockSpec((1, 1, S5_SEG, ns), lambda s, b: (s, b, 0, 0)),
                   pl.BlockSpec((1, 1, S5_SEG, ns), lambda s, b: (s, b, 0, 0)),
                   pl.BlockSpec((1, S5_SEG, ns), sb3), pl.BlockSpec((1, S5_SEG, ns), sb3)],
        out_shape=[st((l, di), F32), st((S5_SB, nb, S5_SEG, ns), F32), st((S5_SB, nb, S5_SEG, ns), F32),
                   st((S5_SB, S5_SEG, ns), F32), st((S5_SB, S5_SEG, ns), F32)],
        scratch_shapes=[pltpu.VMEM((rows, ns), F32), pltpu.VMEM((rows, ns), F32),
                        pltpu.VMEM((S5_SEG, ns), F32), pltpu.VMEM((S5_SEG, ns), F32)],
        compiler_params=_cparams(("parallel", "arbitrary")))(
        a_p, lam_re, lam_im, wb_re, wb_im, wc_re, wc_im, d_skip, x0_re, x0_im)


def s5_ends(inp, lam_re, lam_im, w_re, w_im, adjoint, name):
    l = inp.shape[0]
    rows = S5_SEG * S5_TB
    nb = l // rows
    ns = lam_re.shape[2]

    def body(i_ref, lr_ref, li_ref, wr_ref, wi_ref, er_ref, ei_ref, pr_b, pi_b, xr_s, xi_s):
        b = pl.program_id(1)

        @pl.when(b == 0)
        def _():
            xr_s[...] = jnp.zeros_like(xr_s)
            xi_s[...] = jnp.zeros_like(xi_s)

        v = i_ref[...]
        lr = jnp.broadcast_to(lr_ref[0], (S5_SEG, ns))
        li = jnp.broadcast_to(li_ref[0], (S5_SEG, ns))
        if adjoint:
            pr_b[...] = _dot(v, wr_ref[0], DN_NT)
            pi_b[...] = -_dot(v, wi_ref[0], DN_NT)
            li = -li
        else:
            pr_b[...] = _dot(v, wr_ref[0], DN_NN)
            pi_b[...] = _dot(v, wi_ref[0], DN_NN)

        def step(k, carry):
            xr, xi = carry
            t = S5_TB - 1 - k if adjoint else k
            sl = pl.ds(pl.multiple_of(t * S5_SEG, S5_SEG), S5_SEG)
            return lr * xr - li * xi + pr_b[sl, :], lr * xi + li * xr + pi_b[sl, :]

        xr, xi = lax.fori_loop(0, S5_TB, step, (xr_s[...], xi_s[...]), unroll=S5_UNROLL)
        xr_s[...] = xr
        xi_s[...] = xi
        er_ref[0] = xr
        ei_ref[0] = xi

    sb3 = lambda s, b: (s, 0, 0)
    blk = (lambda s, b: (nb - 1 - b, s)) if adjoint else (lambda s, b: (b, s))
    wshape = (1, ns, LANES) if adjoint else (1, LANES, ns)
    st = jax.ShapeDtypeStruct((S5_SB, S5_SEG, ns), F32)
    return pl.pallas_call(
        body, name=name, grid=(S5_SB, nb),
        in_specs=[pl.BlockSpec((rows, LANES), blk), pl.BlockSpec((1, 1, ns), sb3), pl.BlockSpec((1, 1, ns), sb3),
                  pl.BlockSpec(wshape, sb3), pl.BlockSpec(wshape, sb3)],
        out_specs=[pl.BlockSpec((1, S5_SEG, ns), sb3), pl.BlockSpec((1, S5_SEG, ns), sb3)],
        out_shape=[st, st],
        scratch_shapes=[pltpu.VMEM((rows, ns), F32), pltpu.VMEM((rows, ns), F32),
                        pltpu.VMEM((S5_SEG, ns), F32), pltpu.VMEM((S5_SEG, ns), F32)],
        compiler_params=_cparams(("parallel", "arbitrary")))(inp, lam_re, lam_im, w_re, w_im)


def s5_scan_bwd(a_p, dy, lam_re, lam_im, wb_re, wb_im, wc_re, wc_im, d_skip, ck_re, ck_im, a0_re, a0_im, name):
    l = a_p.shape[0]
    di = d_skip.shape[1]
    rows = S5_SEG * S5_TB
    nb = l // rows
    ns = wb_re.shape[2]

    def body(u_ref, dy_ref, lr_ref, li_ref, wbr_ref, wbi_ref, wcr_ref, wci_ref, ds_ref, ckr_ref, cki_ref,
             a0r_ref, a0i_ref,
             du_ref, dwbr_ref, dwbi_ref, dwcr_ref, dwci_ref, dds_ref, dlr_ref, dli_ref, aer_ref, aei_ref,
             xr_b, xi_b, gr_b, gi_b, ar_s, ai_s):
        b = pl.program_id(1)

        @pl.when(b == 0)
        def _():
            ar_s[...] = a0r_ref[0]
            ai_s[...] = a0i_ref[0]

        u = u_ref[...]
        dyv = dy_ref[...]
        lr = jnp.broadcast_to(lr_ref[0], (S5_SEG, ns))
        li = jnp.broadcast_to(li_ref[0], (S5_SEG, ns))
        xr_b[...] = _dot(u, wbr_ref[0], DN_NN)
        xi_b[...] = _dot(u, wbi_ref[0], DN_NN)

        def fstep(t, carry):
            xr, xi = carry
            sl = pl.ds(pl.multiple_of(t * S5_SEG, S5_SEG), S5_SEG)
            nr = lr * xr - li * xi + xr_b[sl, :]
            ni = lr * xi + li * xr + xi_b[sl, :]
            xr_b[sl, :] = nr
            xi_b[sl, :] = ni
            return nr, ni

        x0r, x0i = ckr_ref[0, 0], cki_ref[0, 0]
        lax.fori_loop(0, S5_TB, fstep, (x0r, x0i), unroll=S5_UNROLL)
        dwcr = _dot(xr_b[...], dyv, DN_TN)
        dwci = -_dot(xi_b[...], dyv, DN_TN)
        gr_b[...] = _dot(dyv, wcr_ref[0], DN_NT)
        gi_b[...] = -_dot(dyv, wci_ref[0], DN_NT)

        def bstep(k, carry):
            ar, ai, dlr, dli = carry
            t = S5_TB - 1 - k
            sl = pl.ds(pl.multiple_of(t * S5_SEG, S5_SEG), S5_SEG)
            slp = pl.ds(pl.multiple_of(jnp.maximum(t - 1, 0) * S5_SEG, S5_SEG), S5_SEG)
            nr = gr_b[sl, :] + lr * ar + li * ai
            ni = gi_b[sl, :] + lr * ai - li * ar
            gr_b[sl, :] = nr
            gi_b[sl, :] = ni
            first = t == 0
            pr = jnp.where(first, x0r, xr_b[slp, :])
            pi = jnp.where(first, x0i, xi_b[slp, :])
            dlr = dlr + nr * pr + ni * pi
            dli = dli + ni * pr - nr * pi
            return nr, ni, dlr, dli

        zero = jnp.zeros((S5_SEG, ns), F32)
        ar, ai, dlr, dli = lax.fori_loop(0, S5_TB, bstep, (ar_s[...], ai_s[...], zero, zero), unroll=S5_UNROLL)
        ar_s[...] = ar
        ai_s[...] = ai
        aer_ref[0] = ar
        aei_ref[0] = ai
        dsk = ds_ref[...]
        du_ref[...] = (_dot(gr_b[...], wbr_ref[0], DN_NT) + _dot(gi_b[...], wbi_ref[0], DN_NT) + dsk * dyv).astype(BF16)
        dwbr = _dot(u, gr_b[...], DN_TN)
        dwbi = _dot(u, gi_b[...], DN_TN)
        dds = jnp.sum(dyv * u, axis=0, keepdims=True)

        @pl.when(b == 0)
        def _():
            dwbr_ref[0] = dwbr
            dwbi_ref[0] = dwbi
            dwcr_ref[0] = dwcr
            dwci_ref[0] = dwci
            dds_ref[...] = dds
            dlr_ref[0] = dlr
            dli_ref[0] = dli

        @pl.when(b != 0)
        def _():
            dwbr_ref[0] += dwbr
            dwbi_ref[0] += dwbi
            dwcr_ref[0] += dwcr
            dwci_ref[0] += dwci
            dds_ref[...] += dds
            dlr_ref[0] += dlr
            dli_ref[0] += dli

    sb3 = lambda s, b: (s, 0, 0)
    rev = lambda s, b: (nb - 1 - b, s)
    st = jax.ShapeDtypeStruct
    return pl.pallas_call(
        body, name=name, grid=(S5_SB, nb),
        in_specs=[pl.BlockSpec((rows, LANES), rev), pl.BlockSpec((rows, LANES), rev),
                  pl.BlockSpec((1, 1, ns), sb3), pl.BlockSpec((1, 1, ns), sb3),
                  pl.BlockSpec((1, LANES, ns), sb3), pl.BlockSpec((1, LANES, ns), sb3),
                  pl.BlockSpec((1, ns, LANES), sb3), pl.BlockSpec((1, ns, LANES), sb3),
                  pl.BlockSpec((1, LANES), lambda s, b: (0, s)),
                  pl.BlockSpec((1, 1, S5_SEG, ns), lambda s, b: (s, nb - 1 - b, 0, 0)),
                  pl.BlockSpec((1, 1, S5_SEG, ns), lambda s, b: (s, nb - 1 - b, 0, 0)),
                  pl.BlockSpec((1, S5_SEG, ns), sb3), pl.BlockSpec((1, S5_SEG, ns), sb3)],
        out_specs=[pl.BlockSpec((rows, LANES), rev),
                   pl.BlockSpec((1, LANES, ns), sb3), pl.BlockSpec((1, LANES, ns), sb3),
                   pl.BlockSpec((1, ns, LANES), sb3), pl.BlockSpec((1, ns, LANES), sb3),
                   pl.BlockSpec((1, LANES), lambda s, b: (0, s)),
                   pl.BlockSpec((1, S5_SEG, ns), sb3), pl.BlockSpec((1, S5_SEG, ns), sb3),
                   pl.BlockSpec((1, S5_SEG, ns), sb3), pl.BlockSpec((1, S5_SEG, ns), sb3)],
        out_shape=[st((l, di), BF16), st((S5_SB, LANES, ns), F32), st((S5_SB, LANES, ns), F32),
                   st((S5_SB, ns, LANES), F32), st((S5_SB, ns, LANES), F32), st((1, di), F32),
                   st((S5_SB, S5_SEG, ns), F32), st((S5_SB, S5_SEG, ns), F32),
                   st((S5_SB, S5_SEG, ns), F32), st((S5_SB, S5_SEG, ns), F32)],
        scratch_shapes=[pltpu.VMEM((rows, ns), F32), pltpu.VMEM((rows, ns), F32),
                        pltpu.VMEM((rows, ns), F32), pltpu.VMEM((rows, ns), F32),
                        pltpu.VMEM((S5_SEG, ns), F32), pltpu.VMEM((S5_SEG, ns), F32)],
        compiler_params=_cparams(("parallel", "arbitrary")))(
        a_p, dy, lam_re, lam_im, wb_re, wb_im, wc_re, wc_im, d_skip, ck_re, ck_im, a0_re, a0_im)


def s5_carry(e_re, e_im, lam_re, lam_im, seg_len, reverse, name):
    sb, seg, ns = e_re.shape

    def body(er_ref, ei_ref, lr_ref, li_ref, cr_ref, ci_ref):
        pr, pi = lr_ref[...], li_ref[...]
        if reverse:
            pi = -pi
        for _ in range(int(math.log2(seg_len))):
            pr, pi = _cmul(pr, pi, pr, pi)
        er, ei = er_ref[...], ei_ref[...]
        row = lax.broadcasted_iota(jnp.int32, (sb, seg, ns), 1)
        cr = jnp.zeros((sb, seg, ns), F32)
        ci = jnp.zeros((sb, seg, ns), F32)
        cur_r = jnp.zeros((sb, 1, ns), F32)
        cur_i = jnp.zeros((sb, 1, ns), F32)
        order = range(seg - 2, -1, -1) if reverse else range(1, seg)
        for s in order:
            src = s + 1 if reverse else s - 1
            mr, mi = _cmul(pr, pi, cur_r, cur_i)
            cur_r = jnp.sum(jnp.where(row == src, er, 0.0), axis=1, keepdims=True) + mr
            cur_i = jnp.sum(jnp.where(row == src, ei, 0.0), axis=1, keepdims=True) + mi
            cr = jnp.where(row == s, cur_r, cr)
            ci = jnp.where(row == s, cur_i, ci)
        cr_ref[...] = cr
        ci_ref[...] = ci

    st = jax.ShapeDtypeStruct((sb, seg, ns), F32)
    return pl.pallas_call(body, name=name, out_shape=[st, st],
                          compiler_params=pltpu.CompilerParams(vmem_limit_bytes=VMEM_LIMIT_BYTES))(e_re, e_im, lam_re, lam_im)


def s5_act(y, name):
    l, d = y.shape
    tl = ROW_TILE

    def body(y_ref, o_ref):
        o_ref[...] = _gelu(y_ref[...]).astype(BF16)

    return _rowcall(body, name, l // tl, [_rows(tl, d)], _rows(tl, d), jax.ShapeDtypeStruct((l, d), BF16))(y)


def s5_gate_fwd(y, t, b_glu, a_p, name):
    l, d = y.shape
    tl = ROW_TILE

    def body(y_ref, t_ref, b_ref, z_ref, m_ref):
        yg = _gelu(y_ref[...])
        m_ref[...] = (yg * _sigmoid(t_ref[...] + b_ref[...]) * _silu(z_ref[...])).astype(BF16)

    return _rowcall(body, name, l // tl, [_rows(tl, d), _rows(tl, d), _full((1, d)), _rows(tl, d, 1)], _rows(tl, d),
                    jax.ShapeDtypeStruct((l, d), BF16))(y, t, b_glu.reshape(1, d), a_p)


def s5_gate_bwd(dm, y, t, b_glu, a_p, name):
    l, d = y.shape
    tl = ROW_TILE

    def body(dm_ref, y_ref, t_ref, b_ref, z_ref, dt_ref, dyg_ref, dz_ref, db_ref):
        i = pl.program_id(0)
        dmv = dm_ref[...]
        z = z_ref[...]
        yg = _gelu(y_ref[...])
        sg = _sigmoid(t_ref[...] + b_ref[...])
        y2 = yg * sg
        sz, gz = _silu_both(z)
        dy2 = dmv * sz
        dz_ref[...] = (dmv * y2 * gz).astype(BF16)
        dyg_ref[...] = dy2 * sg
        dt = dy2 * yg * sg * (1.0 - sg)
        dt_ref[...] = dt.astype(BF16)
        _acc(db_ref, jnp.sum(dt, axis=0, keepdims=True), i)

    st = jax.ShapeDtypeStruct
    return _rowcall(body, name, l // tl, [_rows(tl, d), _rows(tl, d), _rows(tl, d), _full((1, d)), _rows(tl, d, 1)],
                    [_rows(tl, d), _rows(tl, d), _rows(tl, d), _full((1, d))],
                    [st((l, d), BF16), st((l, d), F32), st((l, d), BF16), st((1, d), F32)])(
        dm, y, t, b_glu.reshape(1, d), a_p)


def s5_act_bwd(y, dyg_a, dyg_b, name):
    l, d = y.shape
    tl = ROW_TILE

    def body(y_ref, a_ref, b_ref, o_ref):
        o_ref[...] = (a_ref[...] + b_ref[...]) * _gelu_grad(y_ref[...])

    return _rowcall(body, name, l // tl, [_rows(tl, d)] * 3, _rows(tl, d), jax.ShapeDtypeStruct((l, d), F32))(y, dyg_a, dyg_b)


def _seg_perm(t):
    l, d = t.shape
    return t.reshape(S5_SEG, l // S5_SEG, d).transpose(1, 0, 2).reshape(l, d)


def _seg_unperm(t):
    l, d = t.shape
    return t.reshape(l // S5_SEG, S5_SEG, d).transpose(1, 0, 2).reshape(l, d)


def _s5_weights(p):
    lr, li, bbr, bbi = s5_params_fwd(p["a_re"], p["a_im"], p["log_step"], p["b_re"], p["b_im"])
    ns = 8 * S5_STATE
    lam_re = lr.reshape(S5_SB, 1, ns)
    lam_im = li.reshape(S5_SB, 1, ns)
    to_bd = lambda t: _blockdiag(t.reshape(S5_SB, 8, t.shape[1], t.shape[2]))
    wb_re = to_bd(bbr.transpose(0, 2, 1)).astype(BF16)
    wb_im = to_bd(bbi.transpose(0, 2, 1)).astype(BF16)
    wc_re = to_bd(p["c_re"].transpose(0, 2, 1)).astype(BF16)
    wc_im = to_bd(p["c_im"].transpose(0, 2, 1)).astype(BF16)
    return lam_re, lam_im, wb_re, wb_im, wc_re, wc_im


def s5_layer_fwd(h, p, wf, tag):
    l = h.shape[0]
    di = p["d_skip"].shape[0]
    hn = rms_fwd(h, p["norm_g"], tag + "_rms")
    hn_p = _seg_perm(hn)
    a_p = matmul(hn_p, wf["w_in"], "nn", tag + "_mm_in")
    sw = _s5_weights(p)
    dsk = p["d_skip"].reshape(1, di)
    e_re, e_im = s5_ends(a_p, sw[0], sw[1], sw[2], sw[3], False, tag + "_scan_ends")
    c_re, c_im = s5_carry(e_re, e_im, sw[0], sw[1], l // S5_SEG, False, tag + "_carry")
    y, ck_re, ck_im, _, _ = s5_scan_fwd(a_p, *sw, dsk, c_re, c_im, tag + "_scan")
    yg = s5_act(y, tag + "_act")
    t = matmul(yg, wf["w_glu"], "nn", tag + "_mm_glu")
    m = s5_gate_fwd(y, t, p["b_glu"], a_p, tag + "_gate")
    out_p = matmul(m, wf["w_out"], "nn", tag + "_mm_out")
    h_out = residual_add(h, _seg_unperm(out_p), tag + "_res")
    return h_out, (h, hn_p, a_p, sw, ck_re, ck_im, y, yg, t, m)


def residual_add(h, y, name):
    l, d = h.shape
    tl = ROW_TILE

    def body(h_ref, y_ref, o_ref):
        o_ref[...] = h_ref[...] + y_ref[...]

    return _rowcall(body, name, l // tl, [_rows(tl, d)] * 2, _rows(tl, d), jax.ShapeDtypeStruct((l, d), F32))(h, y)


def s5_layer_bwd(dh_out, saved, p, wf, tag, on_big_grads):
    h, hn_p, a_p, sw, ck_re, ck_im, y, yg, t, m = saved
    l = h.shape[0]
    di = p["d_skip"].shape[0]
    dsk = p["d_skip"].reshape(1, di)
    dout_p = _seg_perm(dh_out)
    dm = matmul(dout_p, wf["w_out"], "nt", tag + "_mm_dm")
    g_w_out = matmul(m, dout_p, "tn", tag + "_mm_gwout")
    dt, dyg_a, dz, db_glu = s5_gate_bwd(dm, y, t, p["b_glu"], a_p, tag + "_gate_bwd")
    dyg_b = matmul(dt, wf["w_glu"], "nt", tag + "_mm_dyg")
    g_w_glu = matmul(yg, dt, "tn", tag + "_mm_gwglu")
    dy = s5_act_bwd(y, dyg_a, dyg_b, tag + "_act_bwd")
    e_re, e_im = s5_ends(dy, sw[0], sw[1], sw[4], sw[5], True, tag + "_scanb_ends")
    c_re, c_im = s5_carry(e_re, e_im, sw[0], sw[1], l // S5_SEG, True, tag + "_carry_bwd")
    du, dwbr, dwbi, dwcr, dwci, dds, dlr, dli, _, _ = s5_scan_bwd(
        a_p, dy, *sw, dsk, ck_re, ck_im, c_re, c_im, tag + "_scanb")
    da = jnp.concatenate([du, dz], axis=1)
    dhn_p = matmul(da, wf["w_in"], "nt", tag + "_mm_dhn")
    g_w_in = matmul(hn_p, da, "tn", tag + "_mm_gwin")
    zero = on_big_grads({"w_in": g_w_in, "w_glu": g_w_glu, "w_out": g_w_out})
    dh, dng = rms_bwd(h, p["norm_g"] + zero, _seg_unperm(dhn_p), dh_out, tag + "_rms_bwd")
    ex = lambda m_, r, c: _blockdiag_extract(m_, r, c).reshape(S5_GROUPS, r, c).transpose(0, 2, 1)
    dbb_re, dbb_im = ex(dwbr, S5_GROUP, S5_STATE), ex(dwbi, S5_GROUP, S5_STATE)
    g_c_re, g_c_im = ex(dwcr, S5_STATE, S5_GROUP), ex(dwci, S5_STATE, S5_GROUP)
    dl_re = lane_sum8(dlr).reshape(S5_GROUPS, S5_STATE, 1)
    dl_im = lane_sum8(dli).reshape(S5_GROUPS, S5_STATE, 1)
    gar, gai, gls, gbr, gbi = s5_params_bwd(p["a_re"], p["a_im"], p["log_step"], p["b_re"], p["b_im"],
                                            dl_re, dl_im, dbb_re, dbb_im)
    grads = {"norm_g": dng.reshape(-1), "w_in": g_w_in, "a_re": gar.reshape(S5_GROUPS, S5_STATE),
             "a_im": gai.reshape(S5_GROUPS, S5_STATE), "log_step": gls.reshape(-1), "b_re": gbr, "b_im": gbi,
             "c_re": g_c_re, "c_im": g_c_im, "d_skip": dds.reshape(-1), "w_glu": g_w_glu,
             "b_glu": db_glu.reshape(-1), "w_out": g_w_out}
    return dh, grads


def lane_sum8(t):
    sb, seg, ns = t.shape

    def body(t_ref, o_ref):
        o_ref[...] = jnp.sum(t_ref[...], axis=1, keepdims=True)

    return pl.pallas_call(body, name="s5_seg_sum", out_shape=jax.ShapeDtypeStruct((sb, 1, ns), F32))(t)


MLA_DI = MLA_HEADS * 128
MLA_CQ0 = MLA_DI
MLA_CKV0 = MLA_CQ0 + MLA_Q_RANK
MLA_KR0 = MLA_CKV0 + MLA_KV_RANK
MLA_AW = MLA_KR0 + LANES


def _rot_half(x):
    w = x.shape[-1]
    lane = lax.broadcasted_iota(jnp.int32, x.shape, x.ndim - 1)
    return jnp.where(lane % MLA_ROPE < MLA_ROPE // 2, pltpu.roll(x, w - MLA_ROPE // 2, x.ndim - 1),
                     pltpu.roll(x, MLA_ROPE // 2, x.ndim - 1))


def rope_tables(pos):
    l = pos.shape[0]
    tl = ROW_TILE
    j = np.arange(LANES) % MLA_ROPE % (MLA_ROPE // 2)
    inv_freq = (ROPE_THETA ** (-(2.0 * j) / MLA_ROPE)).astype(np.float32).reshape(1, LANES)
    sign = np.where(np.arange(LANES) % MLA_ROPE < MLA_ROPE // 2, -1.0, 1.0).astype(np.float32).reshape(1, LANES)

    def body(p_ref, f_ref, s_ref, cos_ref, sin_ref):
        ang = p_ref[...].astype(F32) * f_ref[...]
        cos_ref[...] = jnp.cos(ang)
        sin_ref[...] = jnp.sin(ang) * s_ref[...]

    st = jax.ShapeDtypeStruct((l, LANES), F32)
    return _rowcall(body, "rope_tables", l // tl, [_rows(tl, 1), _full((1, LANES)), _full((1, LANES))],
                    [_rows(tl, LANES)] * 2, [st, st])(pos, jnp.asarray(inv_freq), jnp.asarray(sign))


def _rope(x, cos, sins):
    return x * cos + _rot_half(x) * sins


def _rope_t(dy, cos, sins):
    return dy * cos - sins * _rot_half(dy)


def _rmsn(x):
    r = lax.rsqrt(jnp.mean(x * x, axis=-1, keepdims=True) + NORM_EPS)
    return x * r, r


def mla_pre(a, q_g, kv_g, cos, sins, name):
    l = a.shape[0]
    tl = ROW_TILE

    def body(a_ref, qg_ref, kg_ref, cos_ref, sin_ref, cq_ref, ckv_ref, krs_ref):
        xq, _ = _rmsn(a_ref[:, MLA_CQ0:MLA_CKV0])
        cq_ref[...] = (xq * qg_ref[...]).astype(BF16)
        xk, _ = _rmsn(a_ref[:, MLA_CKV0:MLA_KR0])
        ckv_ref[...] = (xk * kg_ref[...]).astype(BF16)
        kr = a_ref[:, MLA_KR0:MLA_AW]
        kr2 = kr + pltpu.roll(kr, MLA_ROPE, 1)
        kr2 = _rope(kr2, cos_ref[...], sin_ref[...])
        lane = lax.broadcasted_iota(jnp.int32, kr2.shape, 1)
        krs_ref[0] = jnp.where(lane < MLA_ROPE, kr2, 0.0).astype(BF16)
        krs_ref[1] = jnp.where(lane >= MLA_ROPE, kr2, 0.0).astype(BF16)

    st = jax.ShapeDtypeStruct
    return _rowcall(body, name, l // tl,
                    [_rows(tl, MLA_AW), _full((1, MLA_Q_RANK)), _full((1, MLA_KV_RANK)), _rows(tl, LANES), _rows(tl, LANES)],
                    [_rows(tl, MLA_Q_RANK), _rows(tl, MLA_KV_RANK), pl.BlockSpec((2, tl, LANES), lambda i: (0, i, 0))],
                    [st((l, MLA_Q_RANK), BF16), st((l, MLA_KV_RANK), BF16), st((2, l, LANES), BF16)])(
        a, q_g.reshape(1, -1), kv_g.reshape(1, -1), cos, sins)


def mla_rope_q(qr, cos, sins, name):
    l, w = qr.shape
    tl = ROW_TILE

    def body(q_ref, cos_ref, sin_ref, o_ref):
        c, s = cos_ref[...], sin_ref[...]
        for p in range(w // LANES):
            sl = slice(p * LANES, (p + 1) * LANES)
            o_ref[:, sl] = _rope(q_ref[:, sl], c, s).astype(BF16)

    return _rowcall(body, name, l // tl, [_rows(tl, w), _rows(tl, LANES), _rows(tl, LANES)], _rows(tl, w),
                    jax.ShapeDtypeStruct((l, w), BF16))(qr, cos, sins)


ATT_OUT = 512
ATT_IN = 256
ATT_R = ATT_OUT // ATT_IN


def _scores(qn, qr, kn, kr, mask_off, transposed):
    q2 = jnp.concatenate([qn, qr], axis=1)
    k2 = jnp.concatenate([kn, kr], axis=1)
    s = (_dot(k2, q2, DN_NT) if transposed else _dot(q2, k2, DN_NT)) * MLA_SCALE
    if mask_off is None:
        return s
    r = lax.broadcasted_iota(jnp.int32, s.shape, 0)
    c = lax.broadcasted_iota(jnp.int32, s.shape, 1)
    return jnp.where((r <= c + mask_off) if transposed else (c + mask_off <= r), s, NEG_INF)


def _fold(x):
    return x[:, :LANES], x[:, LANES:]


def flash_fwd(qn, qr, kv, krs, name):
    l = qn.shape[0]
    nq = l // ATT_OUT

    def body(qn_ref, qr_ref, kv_ref, kr_ref, o_ref, lse_ref, s_buf):
        qi = pl.program_id(1)
        q_r = qr_ref[...]
        q_n = [qn_ref[:, hh * LANES:(hh + 1) * LANES] for hh in range(2)]

        def block_scores(j, mx, mask_off):
            sl = pl.ds(pl.multiple_of(j * ATT_IN, ATT_IN), ATT_IN)
            out = []
            for hh in range(2):
                s = _scores(q_n[hh], q_r, kv_ref[sl, 2 * hh * LANES:(2 * hh + 1) * LANES], kr_ref[hh, sl, :],
                            mask_off, False)
                s_buf[hh, j] = s
                lo, hi = _fold(s)
                out.append(jnp.maximum(mx[hh], jnp.maximum(lo, hi)))
            return tuple(out)

        ninf = jnp.full((ATT_OUT, LANES), NEG_INF, F32)
        mx = lax.fori_loop(0, ATT_R * qi, lambda j, c: block_scores(j, c, None), (ninf, ninf))
        for d in range(ATT_R):
            mx = block_scores(ATT_R * qi + d, mx, d * ATT_IN)
        m = [jnp.max(mx[hh], axis=-1, keepdims=True) for hh in range(2)]

        def block_pv(j, carry):
            sl = pl.ds(pl.multiple_of(j * ATT_IN, ATT_IN), ATT_IN)
            out = []
            for hh in range(2):
                ls, acc = carry[hh]
                p = jnp.exp(s_buf[hh, j] - m[hh])
                lo, hi = _fold(p)
                out.append((ls + (lo + hi),
                            acc + _dot(p, kv_ref[sl, (2 * hh + 1) * LANES:(2 * hh + 2) * LANES], DN_NN)))
            return tuple(out)

        z = jnp.zeros((ATT_OUT, LANES), F32)
        res = lax.fori_loop(0, ATT_R * (qi + 1), block_pv, ((z, z), (z, z)))
        for hh in range(2):
            lsum = jnp.sum(res[hh][0], axis=-1, keepdims=True)
            o_ref[:, hh * LANES:(hh + 1) * LANES] = res[hh][1] / lsum
            lse_ref[hh] = m[hh] + jnp.log(lsum)

    st = jax.ShapeDtypeStruct
    return pl.pallas_call(
        body, name=name, grid=(MLA_HEADS // 2, nq),
        in_specs=[pl.BlockSpec((ATT_OUT, 2 * LANES), lambda p, i: (i, p)),
                  pl.BlockSpec((ATT_OUT, LANES), lambda p, i: (i, p)),
                  pl.BlockSpec((l, 4 * LANES), lambda p, i: (0, p)),
                  pl.BlockSpec((2, l, LANES), lambda p, i: (0, 0, 0))],
        out_specs=[pl.BlockSpec((ATT_OUT, 2 * LANES), lambda p, i: (i, p)),
                   pl.BlockSpec((2, ATT_OUT, 1), lambda p, i: (p, i, 0))],
        out_shape=[st((l, MLA_DI), F32), st((MLA_HEADS, l, 1), F32)],
        scratch_shapes=[pltpu.VMEM((2, l // ATT_IN, ATT_OUT, ATT_IN), F32)],
        compiler_params=_cparams(("parallel", "arbitrary")))(qn, qr, kv, krs)


def flash_dkv(qn, qr, kv, krs, do, lse_row, delta_row, name):
    l = qn.shape[0]
    nk = l // ATT_OUT
    nq = l // ATT_IN

    def body(qn_ref, qr_ref, do_ref, lse_ref, dl_ref, kv_ref, kr_ref, dkv_ref, dkr_ref):
        kj = pl.program_id(1)
        lane = lax.broadcasted_iota(jnp.int32, (ATT_OUT, LANES), 1)
        kn = [kv_ref[:, 2 * hh * LANES:(2 * hh + 1) * LANES] for hh in range(2)]
        v = [kv_ref[:, (2 * hh + 1) * LANES:(2 * hh + 2) * LANES] for hh in range(2)]

        def block(i, carry, mask_off):
            sl = pl.ds(pl.multiple_of(i * ATT_IN, ATT_IN), ATT_IN)
            q_r = qr_ref[sl, :]
            out = []
            for hh in range(2):
                dk2, dv = carry[hh]
                hs = slice(hh * LANES, (hh + 1) * LANES)
                q_n, d_o = qn_ref[sl, hs], do_ref[sl, hs]
                s = _scores(q_n, q_r, kn[hh], kr_ref[hh], mask_off, True)
                pt = jnp.exp(s - lse_ref[hh, i])
                dv = dv + _dot(pt, d_o, DN_NN)
                dpt = _dot(v[hh], d_o, DN_NT)
                dst = (pt * (dpt - dl_ref[hh, i]) * MLA_SCALE).astype(BF16)
                out.append((dk2 + _dot(dst, jnp.concatenate([q_n, q_r], axis=1), DN_NN), dv))
            return tuple(out)

        z = jnp.zeros((ATT_OUT, LANES), F32)
        z2 = jnp.zeros((ATT_OUT, 2 * LANES), F32)
        res = ((z2, z), (z2, z))
        for d in range(ATT_R):
            res = block(ATT_R * kj + d, res, d * ATT_IN)
        res = lax.fori_loop(ATT_R * (kj + 1), nq, lambda i, c: block(i, c, None), res)
        for hh in range(2):
            dkv_ref[:, 2 * hh * LANES:(2 * hh + 1) * LANES] = res[hh][0][:, :LANES].astype(BF16)
            dkv_ref[:, (2 * hh + 1) * LANES:(2 * hh + 2) * LANES] = res[hh][1].astype(BF16)
        dkr_ref[0] = jnp.where(lane < MLA_ROPE, res[0][0][:, LANES:], res[1][0][:, LANES:])

    st = jax.ShapeDtypeStruct
    return pl.pallas_call(
        body, name=name, grid=(MLA_HEADS // 2, nk),
        in_specs=[pl.BlockSpec((l, 2 * LANES), lambda p, j: (0, p)),
                  pl.BlockSpec((l, LANES), lambda p, j: (0, p)),
                  pl.BlockSpec((l, 2 * LANES), lambda p, j: (0, p)),
                  pl.BlockSpec((2, nq, 1, ATT_IN), lambda p, j: (p, 0, 0, 0)),
                  pl.BlockSpec((2, nq, 1, ATT_IN), lambda p, j: (p, 0, 0, 0)),
                  pl.BlockSpec((ATT_OUT, 4 * LANES), lambda p, j: (j, p)),
                  pl.BlockSpec((2, ATT_OUT, LANES), lambda p, j: (0, j, 0))],
        out_specs=[pl.BlockSpec((ATT_OUT, 4 * LANES), lambda p, j: (j, p)),
                   pl.BlockSpec((1, ATT_OUT, LANES), lambda p, j: (p, j, 0))],
        out_shape=[st((l, 2 * MLA_DI), BF16), st((MLA_HEADS // 2, l, LANES), F32)],
        compiler_params=_cparams(("parallel", "arbitrary")))(qn, qr, do, lse_row, delta_row, kv, krs)


def flash_dq(qn, qr, kv, krs, do, lse, delta, cos, sins, name):
    l = qn.shape[0]
    nq = l // ATT_OUT

    def body(qn_ref, qr_ref, do_ref, lse_ref, dl_ref, kv_ref, kr_ref, cos_ref, sin_ref, dqn_ref, dqr_ref):
        qi = pl.program_id(1)
        q_r = qr_ref[...]
        q_n = [qn_ref[:, hh * LANES:(hh + 1) * LANES] for hh in range(2)]
        d_o = [do_ref[:, hh * LANES:(hh + 1) * LANES] for hh in range(2)]
        lse_h = [lse_ref[hh] for hh in range(2)]
        dl_h = [dl_ref[hh] for hh in range(2)]

        def block(j, carry, mask_off):
            sl = pl.ds(pl.multiple_of(j * ATT_IN, ATT_IN), ATT_IN)
            dq2 = list(carry)
            for hh in range(2):
                kn = kv_ref[sl, 2 * hh * LANES:(2 * hh + 1) * LANES]
                v = kv_ref[sl, (2 * hh + 1) * LANES:(2 * hh + 2) * LANES]
                kr = kr_ref[hh, sl, :]
                s = _scores(q_n[hh], q_r, kn, kr, mask_off, False)
                pr = jnp.exp(s - lse_h[hh])
                dp = _dot(d_o[hh], v, DN_NT)
                ds = (pr * (dp - dl_h[hh]) * MLA_SCALE).astype(BF16)
                dq2[hh] = dq2[hh] + _dot(ds, jnp.concatenate([kn, kr], axis=1), DN_NN)
            return tuple(dq2)

        z2 = jnp.zeros((ATT_OUT, 2 * LANES), F32)
        res = lax.fori_loop(0, ATT_R * qi, lambda j, c: block(j, c, None), (z2, z2))
        for d in range(ATT_R):
            res = block(ATT_R * qi + d, res, d * ATT_IN)
        dqn_ref[:, 0:LANES] = res[0][:, :LANES].astype(BF16)
        dqn_ref[:, LANES:2 * LANES] = res[1][:, :LANES].astype(BF16)
        dqr = res[0][:, LANES:] + res[1][:, LANES:]
        dqr_ref[...] = _rope_t(dqr, cos_ref[...], sin_ref[...]).astype(BF16)

    st = jax.ShapeDtypeStruct
    return pl.pallas_call(
        body, name=name, grid=(MLA_HEADS // 2, nq),
        in_specs=[pl.BlockSpec((ATT_OUT, 2 * LANES), lambda p, i: (i, p)),
                  pl.BlockSpec((ATT_OUT, LANES), lambda p, i: (i, p)),
                  pl.BlockSpec((ATT_OUT, 2 * LANES), lambda p, i: (i, p)),
                  pl.BlockSpec((2, ATT_OUT, 1), lambda p, i: (p, i, 0)),
                  pl.BlockSpec((2, ATT_OUT, 1), lambda p, i: (p, i, 0)),
                  pl.BlockSpec((l, 4 * LANES), lambda p, i: (0, p)),
                  pl.BlockSpec((2, l, LANES), lambda p, i: (0, 0, 0)),
                  pl.BlockSpec((ATT_OUT, LANES), lambda p, i: (i, 0)),
                  pl.BlockSpec((ATT_OUT, LANES), lambda p, i: (i, 0))],
        out_specs=[pl.BlockSpec((ATT_OUT, 2 * LANES), lambda p, i: (i, p)),
                   pl.BlockSpec((ATT_OUT, LANES), lambda p, i: (i, p))],
        out_shape=[st((l, MLA_DI), BF16), st((l, MLA_HEADS * MLA_ROPE), BF16)],
        compiler_params=_cparams(("parallel", "arbitrary")))(qn, qr, do, lse, delta, kv, krs, cos, sins)


def mla_gate_fwd(o, a, name):
    l = o.shape[0]
    tl = ROW_TILE

    def body(o_ref, z_ref, m_ref):
        m_ref[...] = (o_ref[...] * _silu(z_ref[...])).astype(BF16)

    return _rowcall(body, name, l // tl, [_rows(tl, MLA_DI), _rows(tl, MLA_DI)], _rows(tl, MLA_DI),
                    jax.ShapeDtypeStruct((l, MLA_DI), BF16))(o, a)


def mla_gate_bwd(dm, o, a, name):
    l = o.shape[0]
    tl = ROW_TILE

    def body(dm_ref, o_ref, z_ref, do_ref, dz_ref, dl_ref):
        dmv, ov, z = dm_ref[...], o_ref[...], z_ref[...]
        sz, gz = _silu_both(z)
        d_o = dmv * sz
        do_ref[...] = d_o.astype(BF16)
        dz_ref[...] = (dmv * ov * gz).astype(BF16)
        pr = d_o * ov
        for h in range(MLA_HEADS):
            dl_ref[h] = jnp.sum(pr[:, h * LANES:(h + 1) * LANES], axis=1, keepdims=True)

    st = jax.ShapeDtypeStruct
    return _rowcall(body, name, l // tl, [_rows(tl, MLA_DI)] * 3,
                    [_rows(tl, MLA_DI), _rows(tl, MLA_DI), pl.BlockSpec((MLA_HEADS, tl, 1), lambda i: (0, i, 0))],
                    [st((l, MLA_DI), BF16), st((l, MLA_DI), BF16), st((MLA_HEADS, l, 1), F32)])(dm, o, a)


def mla_post(a, dcqn, dckvn, dkr_pairs, dz, q_g, kv_g, cos, sins, name):
    l = a.shape[0]
    tl = ROW_TILE
    npair = MLA_HEADS // 2

    def norm_bwd(x, g, dy):
        xhat, r = _rmsn(x)
        dxh = dy * g
        return r * (dxh - xhat * jnp.mean(dxh * xhat, axis=-1, keepdims=True)), jnp.sum(dy * xhat, axis=0, keepdims=True)

    def body(a_ref, dq_ref, dk_ref, dkr_ref, dz_ref, qg_ref, kg_ref, cos_ref, sin_ref, da_ref, dqg_ref, dkg_ref):
        i = pl.program_id(0)
        da_ref[:, 0:MLA_DI] = dz_ref[...]
        dcq, dqg = norm_bwd(a_ref[:, MLA_CQ0:MLA_CKV0], qg_ref[...], dq_ref[...])
        da_ref[:, MLA_CQ0:MLA_CKV0] = dcq.astype(BF16)
        dckv, dkg = norm_bwd(a_ref[:, MLA_CKV0:MLA_KR0], kg_ref[...], dk_ref[...])
        da_ref[:, MLA_CKV0:MLA_KR0] = dckv.astype(BF16)
        dk2 = dkr_ref[0]
        for p in range(1, npair):
            dk2 = dk2 + dkr_ref[p]
        dk2 = _rope_t(dk2, cos_ref[...], sin_ref[...])
        dk2 = dk2 + pltpu.roll(dk2, MLA_ROPE, 1)
        lane = lax.broadcasted_iota(jnp.int32, dk2.shape, 1)
        da_ref[:, MLA_KR0:MLA_AW] = jnp.where(lane < MLA_ROPE, dk2, 0.0).astype(BF16)
        _acc(dqg_ref, dqg, i)
        _acc(dkg_ref, dkg, i)

    st = jax.ShapeDtypeStruct
    return _rowcall(body, name, l // tl,
                    [_rows(tl, MLA_AW), _rows(tl, MLA_Q_RANK), _rows(tl, MLA_KV_RANK),
                     pl.BlockSpec((npair, tl, LANES), lambda i: (0, i, 0)), _rows(tl, MLA_DI),
                     _full((1, MLA_Q_RANK)), _full((1, MLA_KV_RANK)), _rows(tl, LANES), _rows(tl, LANES)],
                    [_rows(tl, MLA_AW), _full((1, MLA_Q_RANK)), _full((1, MLA_KV_RANK))],
                    [st((l, MLA_AW), BF16), st((1, MLA_Q_RANK), F32), st((1, MLA_KV_RANK), F32)])(
        a, dcqn, dckvn, dkr_pairs, dz, q_g.reshape(1, -1), kv_g.reshape(1, -1), cos, sins)


def _mla_w_in_perm(w):
    r = MLA_Q_RANK + MLA_KV_RANK + MLA_ROPE
    pad = jnp.zeros(w.shape[:-1] + (MLA_AW - MLA_KR0 - MLA_ROPE,), w.dtype)
    return jnp.concatenate([w[..., r:], w[..., :r], pad], axis=-1)


def _mla_w_in_unperm(g):
    r = MLA_Q_RANK + MLA_KV_RANK + MLA_ROPE
    return jnp.concatenate([g[..., MLA_DI:MLA_DI + r], g[..., :MLA_DI]], axis=-1)


def _mla_w_uq_split(w):
    k = w.shape[0]
    w3 = w.reshape(k, MLA_HEADS, MLA_NOPE + MLA_ROPE)
    return w3[:, :, :MLA_NOPE].reshape(k, MLA_HEADS * MLA_NOPE), w3[:, :, MLA_NOPE:].reshape(k, MLA_HEADS * MLA_ROPE)


def _mla_w_uq_merge(gn, gr):
    k = gn.shape[0]
    return jnp.concatenate([gn.reshape(k, MLA_HEADS, MLA_NOPE), gr.reshape(k, MLA_HEADS, MLA_ROPE)], axis=2).reshape(k, -1)


def mla_layer_fwd(h, p, wf, cos, sins, tag):
    hn = rms_fwd(h, p["norm_g"], tag + "_rms")
    w_in = _mla_w_in_perm(wf["w_in"])
    w_uq_n, w_uq_r = _mla_w_uq_split(wf["w_uq"])
    a = matmul(hn, w_in, "nn", tag + "_mm_in")
    cqn, ckvn, krs = mla_pre(a, p["q_norm_g"], p["kv_norm_g"], cos, sins, tag + "_pre")
    qn = matmul(cqn, w_uq_n, "nn", tag + "_mm_qn", out_dtype=BF16)
    qr_raw = matmul(cqn, w_uq_r, "nn", tag + "_mm_qr")
    qr = mla_rope_q(qr_raw, cos, sins, tag + "_rope_q")
    kv = matmul(ckvn, wf["w_ukv"], "nn", tag + "_mm_kv", out_dtype=BF16)
    o, lse = flash_fwd(qn, qr, kv, krs, tag + "_flash")
    m = mla_gate_fwd(o, a, tag + "_gate")
    h_out = matmul(m, wf["w_out"], "nn", tag + "_mm_out", add=h)
    return h_out, (h, hn, a, cqn, ckvn, krs, qn, qr, kv, o, lse, m, w_in, w_uq_n, w_uq_r)


def mla_layer_bwd(dh_out, saved, p, wf, cos, sins, tag, on_big_grads):
    h, hn, a, cqn, ckvn, krs, qn, qr, kv, o, lse, m, w_in, w_uq_n, w_uq_r = saved
    l = h.shape[0]
    dm = matmul(dh_out, wf["w_out"], "nt", tag + "_mm_dm")
    g_w_out = matmul(m, dh_out, "tn", tag + "_mm_gwout")
    do, dz, delta = mla_gate_bwd(dm, o, a, tag + "_gate_bwd")
    lse_row = lse.reshape(MLA_HEADS, l // ATT_IN, 1, ATT_IN)
    delta_row = delta.reshape(MLA_HEADS, l // ATT_IN, 1, ATT_IN)
    dkv, dkr_pairs = flash_dkv(qn, qr, kv, krs, do, lse_row, delta_row, tag + "_flash_dkv")
    dqn, dqr = flash_dq(qn, qr, kv, krs, do, lse, delta, cos, sins, tag + "_flash_dq")
    dcqn = matmul(dqn, w_uq_n, "nt", tag + "_mm_dcq_n")
    dcqn = matmul(dqr, w_uq_r, "nt", tag + "_mm_dcq_r", add=dcqn)
    g_uq_n = matmul(cqn, dqn, "tn", tag + "_mm_guq_n")
    g_uq_r = matmul(cqn, dqr, "tn", tag + "_mm_guq_r")
    dckvn = matmul(dkv, wf["w_ukv"], "nt", tag + "_mm_dckv")
    g_ukv = matmul(ckvn, dkv, "tn", tag + "_mm_gukv")
    da, dqg, dkg = mla_post(a, dcqn, dckvn, dkr_pairs, dz, p["q_norm_g"], p["kv_norm_g"], cos, sins, tag + "_post")
    dhn = matmul(da, w_in, "nt", tag + "_mm_dhn")
    g_w_in = matmul(hn, da, "tn", tag + "_mm_gwin")
    big = {"w_in": _mla_w_in_unperm(g_w_in), "w_uq": _mla_w_uq_merge(g_uq_n, g_uq_r), "w_ukv": g_ukv, "w_out": g_w_out}
    zero = on_big_grads(big)
    dh, dng = rms_bwd(h, p["norm_g"] + zero, dhn, dh_out, tag + "_rms_bwd")
    grads = {"norm_g": dng.reshape(-1), "q_norm_g": dqg.reshape(-1), "kv_norm_g": dkg.reshape(-1), **big}
    return dh, grads


ANY = pl.BlockSpec(memory_space=pl.ANY)


def _me():
    return lax.axis_index("x"), lax.axis_index("y"), lax.axis_index("c")


def _chip():
    return 2 * lax.axis_index("x") + lax.axis_index("y")


def _other_chips(x, y):
    return [(1 - x, y), (x, 1 - y), (1 - x, 1 - y)]


def _rcopy(src, dst, ssem, rsem, dev):
    return pltpu.make_async_remote_copy(src_ref=src, dst_ref=dst, send_sem=ssem, recv_sem=rsem,
                                        device_id=dev, device_id_type=MESH)


def _half(ref, c, hf):
    return ref.at[pl.ds(c * hf, hf), :]


def weights_allgather(wb):
    nr, w = wb.shape
    hf = nr // 2

    def body(w_ref, o_ref, ssem, rsem):
        x, y, c = _me()
        k = 2 * x + y
        chips = _other_chips(x, y)
        first = [_rcopy(_half(w_ref, c, hf), _half(o_ref.at[k], c, hf), ssem.at[j], rsem.at[j], (cx, cy, c))
                 for j, (cx, cy) in enumerate(chips)]
        for cp in first:
            cp.start()
        passed = []
        for j, (cx, cy) in enumerate(chips):
            region = _half(o_ref.at[2 * cx + cy], c, hf)
            _rcopy(region, region, ssem.at[j], rsem.at[j], (cx, cy, c)).wait_recv()
            fwd = _rcopy(region, region, ssem.at[3 + j], rsem.at[3 + j], (x, y, 1 - c))
            fwd.start()
            passed.append(fwd)
        for j, (cx, cy) in enumerate(chips):
            region = _half(o_ref.at[2 * cx + cy], 1 - c, hf)
            _rcopy(region, region, ssem.at[3 + j], rsem.at[3 + j], (x, y, 1 - c)).wait_recv()
        for cp in first + passed:
            cp.wait_send()

    out = pl.pallas_call(
        body, name="weights_allgather", in_specs=[ANY], out_specs=ANY,
        out_shape=jax.ShapeDtypeStruct((N_CHIPS, nr, w), wb.dtype),
        scratch_shapes=[pltpu.SemaphoreType.DMA((6,)), pltpu.SemaphoreType.DMA((6,))],
    )(wb)
    return lax.dynamic_update_slice(out, wb[None], (_chip(), 0, 0))


HBM = pl.BlockSpec(memory_space=pltpu.HBM)
SEM = pl.BlockSpec(memory_space=pltpu.SEMAPHORE)
SPLIT_EFFECT = pltpu.SideEffectType.DATAFLOW_SIDE_EFFECTING


def gather_start(wb, after, name):
    nr, w = wb.shape
    hf = nr // 2

    def body(w_ref, land_ref, after_ref, ssem, rsem, w_thru, land_thru, token):
        x, y, c = _me()
        k = 2 * x + y
        for j, (cx, cy) in enumerate(_other_chips(x, y)):
            _rcopy(_half(w_ref, c, hf), _half(land_ref.at[k], c, hf), ssem.at[j], rsem.at[j], (cx, cy, c)).start()
        token[...] = jnp.zeros_like(token)

    land = lax.empty((N_CHIPS, nr, w), wb.dtype)
    return pl.pallas_call(
        body, name=name,
        out_shape=(pltpu.SemaphoreType.DMA((3,)), pltpu.SemaphoreType.DMA((3,)), pltpu.HBM(wb.shape, wb.dtype),
                   pltpu.HBM(land.shape, land.dtype), jax.ShapeDtypeStruct((8, LANES), F32)),
        in_specs=(HBM, HBM, ANY), out_specs=(SEM, SEM, HBM, HBM, pl.BlockSpec(memory_space=pltpu.VMEM)),
        input_output_aliases={0: 2, 1: 3},
        compiler_params=pltpu.CompilerParams(has_side_effects=SPLIT_EFFECT))(
        pltpu.with_memory_space_constraint(wb, pltpu.HBM), pltpu.with_memory_space_constraint(land, pltpu.HBM), after)


def gather_wait(ssem, rsem, w_thru, land_thru, after, name):
    nr, w = w_thru.shape
    hf = nr // 2

    def body(w_ref, land_ref, ssem_ref, rsem_ref, after_ref, w_dead, got_ref):
        x, y, c = _me()
        for j, (cx, cy) in enumerate(_other_chips(x, y)):
            cp = _rcopy(_half(w_ref, c, hf), _half(land_ref.at[2 * cx + cy], c, hf), ssem_ref.at[j], rsem_ref.at[j],
                        (cx, cy, c))
            cp.wait_send()
            cp.wait_recv()

    return pl.pallas_call(
        body, name=name, out_shape=(pltpu.HBM(w_thru.shape, w_thru.dtype), pltpu.HBM(land_thru.shape, land_thru.dtype)),
        in_specs=(HBM, HBM, SEM, SEM, ANY), out_specs=(HBM, HBM), input_output_aliases={0: 0, 1: 1},
        compiler_params=pltpu.CompilerParams(has_side_effects=SPLIT_EFFECT))(w_thru, land_thru, ssem, rsem, after)[1]


def gather_handover(land, wb, name):
    _, nr, w = land.shape
    hf = nr // 2

    def body(l_ref, o_ref, ssem, rsem):
        x, y, c = _me()
        chips = _other_chips(x, y)
        sends = []
        for j, (cx, cy) in enumerate(chips):
            region = _half(o_ref.at[2 * cx + cy], c, hf)
            sends.append(_rcopy(region, region, ssem.at[j], rsem.at[j], (x, y, 1 - c)))
            sends[-1].start()
        for j, (cx, cy) in enumerate(chips):
            region = _half(o_ref.at[2 * cx + cy], 1 - c, hf)
            _rcopy(region, region, ssem.at[j], rsem.at[j], (x, y, 1 - c)).wait_recv()
        for cp in sends:
            cp.wait_send()

    out = pl.pallas_call(
        body, name=name, in_specs=[ANY], out_specs=ANY, input_output_aliases={0: 0},
        out_shape=jax.ShapeDtypeStruct(land.shape, land.dtype),
        scratch_shapes=[pltpu.SemaphoreType.DMA((3,)), pltpu.SemaphoreType.DMA((3,))])(land)
    return lax.dynamic_update_slice(out, wb[None], (_chip(), 0, 0))


def reduce_start(t, after, name):
    def body(t_ref, land_ref, after_ref, ssem, rsem, t_thru, land_thru, token):
        x, y, c = _me()
        k = 2 * x + y
        for j, (cx, cy) in enumerate(_other_chips(x, y)):
            _rcopy(t_ref.at[2 * cx + cy], land_ref.at[k], ssem.at[j], rsem.at[j], (cx, cy, c)).start()
        token[...] = jnp.zeros_like(token)

    land = lax.empty(t.shape, t.dtype)
    return pl.pallas_call(
        body, name=name,
        out_shape=(pltpu.SemaphoreType.DMA((3,)), pltpu.SemaphoreType.DMA((3,)), pltpu.HBM(t.shape, t.dtype),
                   pltpu.HBM(t.shape, t.dtype), jax.ShapeDtypeStruct((8, LANES), F32)),
        in_specs=(HBM, HBM, ANY), out_specs=(SEM, SEM, HBM, HBM, pl.BlockSpec(memory_space=pltpu.VMEM)),
        input_output_aliases={0: 2, 1: 3},
        compiler_params=pltpu.CompilerParams(has_side_effects=SPLIT_EFFECT))(
        pltpu.with_memory_space_constraint(t, pltpu.HBM), pltpu.with_memory_space_constraint(land, pltpu.HBM), after)


def reduce_wait(ssem, rsem, t_thru, land_thru, after, name):
    def body(t_ref, land_ref, ssem_ref, rsem_ref, after_ref, t_out, got_ref):
        x, y, c = _me()
        k = 2 * x + y
        for j, (cx, cy) in enumerate(_other_chips(x, y)):
            cp = _rcopy(t_ref.at[k], land_ref.at[2 * cx + cy], ssem_ref.at[j], rsem_ref.at[j], (cx, cy, c))
            cp.wait_send()
            cp.wait_recv()

    return pl.pallas_call(
        body, name=name, out_shape=(pltpu.HBM(t_thru.shape, t_thru.dtype), pltpu.HBM(land_thru.shape, land_thru.dtype)),
        in_specs=(HBM, HBM, SEM, SEM, ANY), out_specs=(HBM, HBM), input_output_aliases={0: 0, 1: 1},
        compiler_params=pltpu.CompilerParams(has_side_effects=SPLIT_EFFECT))(t_thru, land_thru, ssem, rsem, after)


def grads_to_sibling(ps, name="grads_to_sibling"):
    n = len(ps)

    def body(*refs):
        p_refs, o_refs, ssem, rsem = refs[:n], refs[n:2 * n], refs[2 * n], refs[2 * n + 1]
        x, y, c = _me()
        cps = []
        for a in range(n):
            hf = ps[a].shape[1] // 2
            cps.append(_rcopy(p_refs[a].at[:, pl.ds((1 - c) * hf, hf), :], o_refs[a], ssem.at[a], rsem.at[a],
                              (x, y, 1 - c)))
        for cp in cps:
            cp.start()
        for cp in cps:
            cp.wait()

    return pl.pallas_call(
        body, name=name, in_specs=[ANY] * n, out_specs=[ANY] * n,
        out_shape=[jax.ShapeDtypeStruct((N_CHIPS, p.shape[1] // 2, p.shape[2]), p.dtype) for p in ps],
        scratch_shapes=[pltpu.SemaphoreType.DMA((n,)), pltpu.SemaphoreType.DMA((n,))])(*ps)


def pair_sum(p, ra, out_dtype, name):
    _, nr, w = p.shape
    hf = nr // 2
    tr = _pick_rows(hf)
    nb = hf // tr

    def body(c_ref, p_ref, r_ref, o_ref):
        o_ref[...] = (p_ref[...] + r_ref[...]).astype(out_dtype)

    c = lax.axis_index("c").astype(jnp.int32).reshape(1)
    return pl.pallas_call(
        body, name=name,
        grid_spec=pltpu.PrefetchScalarGridSpec(
            num_scalar_prefetch=1, grid=(N_CHIPS, nb),
            in_specs=[pl.BlockSpec((1, tr, w), lambda k, i, c_ref: (k, c_ref[0] * nb + i, 0)),
                      pl.BlockSpec((1, tr, w), lambda k, i, c_ref: (k, i, 0))],
            out_specs=pl.BlockSpec((1, tr, w), lambda k, i, c_ref: (k, i, 0))),
        out_shape=jax.ShapeDtypeStruct((N_CHIPS, hf, w), out_dtype),
        compiler_params=_cparams(("parallel", "parallel")))(c, p, ra)


def grads_across_chips(ts):
    n = len(ts)

    def body(*refs):
        t_refs, o_refs, ssem, rsem = refs[:n], refs[n:2 * n], refs[2 * n], refs[2 * n + 1]
        x, y, c = _me()
        k = 2 * x + y
        chips = _other_chips(x, y)
        sends = [_rcopy(t_refs[a].at[2 * cx + cy], o_refs[a].at[k], ssem.at[3 * a + j], rsem.at[3 * a + j], (cx, cy, c))
                 for a in range(n) for j, (cx, cy) in enumerate(chips)]
        for cp in sends:
            cp.start()
        for a in range(n):
            for j, (cx, cy) in enumerate(chips):
                _rcopy(t_refs[a].at[k], o_refs[a].at[2 * cx + cy], ssem.at[3 * a + j], rsem.at[3 * a + j],
                       (cx, cy, c)).wait_recv()
        for cp in sends:
            cp.wait_send()

    return pl.pallas_call(
        body, name="grads_across_chips", in_specs=[ANY] * n, out_specs=[ANY] * n,
        out_shape=[jax.ShapeDtypeStruct(t.shape, t.dtype) for t in ts],
        scratch_shapes=[pltpu.SemaphoreType.DMA((3 * n,)), pltpu.SemaphoreType.DMA((3 * n,))])(*ts)


def chip_sum(t, rb, name):
    _, hf, w = rb.shape
    tr = _pick_rows(hf)
    nb = hf // tr

    def body(kc_ref, t_ref, r_ref, o_ref):
        k = kc_ref[0]
        acc = jnp.where(k == 0, t_ref[0], r_ref[0]).astype(F32)
        for j in range(1, N_CHIPS):
            acc = acc + jnp.where(k == j, t_ref[0], r_ref[j]).astype(F32)
        o_ref[...] = acc

    kc = jnp.stack([_chip(), lax.axis_index("c")]).astype(jnp.int32)
    return pl.pallas_call(
        body, name=name,
        grid_spec=pltpu.PrefetchScalarGridSpec(
            num_scalar_prefetch=1, grid=(nb,),
            in_specs=[pl.BlockSpec((1, tr, w), lambda i, kc_ref: (kc_ref[0], i, 0)),
                      pl.BlockSpec((N_CHIPS, tr, w), lambda i, kc_ref: (0, i, 0))],
            out_specs=pl.BlockSpec((tr, w), lambda i, kc_ref: (kc_ref[1] * nb + i, 0))),
        out_shape=jax.ShapeDtypeStruct((2 * hf, w), F32), compiler_params=_cparams(("parallel",)))(kc, t, rb)


def reduced_to_sibling(gs):
    n = len(gs)

    def body(*refs):
        o_refs, ssem, rsem = refs[n:2 * n], refs[2 * n], refs[2 * n + 1]
        x, y, c = _me()
        cps = []
        for a in range(n):
            hf = gs[a].shape[0] // 2
            cps.append(_rcopy(_half(o_refs[a], c, hf), _half(o_refs[a], c, hf), ssem.at[a], rsem.at[a], (x, y, 1 - c)))
        for cp in cps:
            cp.start()
        for a in range(n):
            hf = gs[a].shape[0] // 2
            _rcopy(_half(o_refs[a], c, hf), _half(o_refs[a], 1 - c, hf), ssem.at[a], rsem.at[a],
                   (x, y, 1 - c)).wait_recv()
        for cp in cps:
            cp.wait_send()

    return pl.pallas_call(
        body, name="reduced_to_sibling", in_specs=[ANY] * n, out_specs=[ANY] * n,
        input_output_aliases={a: a for a in range(n)},
        out_shape=[jax.ShapeDtypeStruct(g.shape, g.dtype) for g in gs],
        scratch_shapes=[pltpu.SemaphoreType.DMA((n,)), pltpu.SemaphoreType.DMA((n,))])(*gs)


def small_allgather(g, row0, nrs):
    w = g.shape[1]

    def body(g_ref, o_ref, ssem, rsem):
        x, y, c = _me()
        k = 2 * x + y
        chips = _other_chips(x, y)
        src = g_ref.at[pl.ds(row0, nrs), :]
        sends = [_rcopy(src, o_ref.at[k], ssem.at[j], rsem.at[j], (cx, cy, c)) for j, (cx, cy) in enumerate(chips)]
        for cp in sends:
            cp.start()
        for j, (cx, cy) in enumerate(chips):
            _rcopy(src, o_ref.at[2 * cx + cy], ssem.at[j], rsem.at[j], (cx, cy, c)).wait_recv()
        for cp in sends:
            cp.wait_send()

    out = pl.pallas_call(
        body, name="small_allgather", in_specs=[ANY], out_specs=ANY,
        out_shape=jax.ShapeDtypeStruct((N_CHIPS, nrs, w), g.dtype),
        scratch_shapes=[pltpu.SemaphoreType.DMA((3,)), pltpu.SemaphoreType.DMA((3,))])(g)
    return lax.dynamic_update_slice(out, g[row0:row0 + nrs][None], (_chip(), 0, 0))


def adamw(w, g, m, v, name):
    r, wd = w.shape
    tr = _pick_rows(r, cap=max(16, ADAMW_BLOCK_BYTES // (4 * wd)))
    bc1 = 1.0 - ADAM_B1 ** ADAM_STEP
    bc2 = 1.0 - ADAM_B2 ** ADAM_STEP

    def body(w_ref, g_ref, m_ref, v_ref, d_ref, nm_ref, nv_ref):
        gv = g_ref[...]
        nm = ADAM_B1 * m_ref[...] + (1.0 - ADAM_B1) * gv
        nv = ADAM_B2 * v_ref[...] + (1.0 - ADAM_B2) * (gv * gv)
        nm_ref[...] = nm
        nv_ref[...] = nv
        d_ref[...] = -ADAM_LR * ((nm / bc1) / (jnp.sqrt(nv / bc2) + ADAM_EPS) + ADAM_WD * w_ref[...])

    spec = pl.BlockSpec((tr, wd), lambda i: (i, 0))
    st = jax.ShapeDtypeStruct((r, wd), F32)
    return pl.pallas_call(body, name=name, grid=(r // tr,), in_specs=[spec] * 4, out_specs=[spec] * 3,
                          out_shape=[st, st, st], compiler_params=_cparams(("parallel",)))(w, g, m, v)


LAYER_KINDS = ("gmlp", "s5", "mla", "gmlp")
PARAMS = {
    "gmlp": ("norm_g", "w_in", "ln_g", "ln_b", "w_s", "b_s", "w_out"),
    "s5": ("norm_g", "w_in", "a_re", "a_im", "log_step", "b_re", "b_im", "c_re", "c_im", "d_skip", "w_glu", "b_glu", "w_out"),
    "mla": ("norm_g", "w_in", "q_norm_g", "w_uq", "kv_norm_g", "w_ukv", "w_out"),
}
COL_SHARDED = ("w_in", "w_uq", "w_ukv")
ROW_SHARDED = ("w_out", "w_glu")
WEIGHT_NAMES = [("l%d_" % i) + n for i, kind in enumerate(LAYER_KINDS) for n in PARAMS[kind]] + ["final_norm_g"]


def _is_big(name):
    return name.split("_", 1)[1] in COL_SHARDED + ROW_SHARDED


BIG = [n for n in WEIGHT_NAMES if _is_big(n)]
SMALL = [n for n in WEIGHT_NAMES if not _is_big(n)]


def _pack_rows(blocks):
    return jnp.concatenate([b.reshape(-1, PACK_W) for b in blocks], axis=0)


def _shard_major(name, full):
    r, c = full.shape
    if name.split("_", 1)[1] in COL_SHARDED:
        t = full.reshape(r, N_CHIPS, c // N_CHIPS).transpose(1, 0, 2)
    else:
        t = full.reshape(N_CHIPS, r // N_CHIPS, c)
    return t.reshape(N_CHIPS, -1, PACK_W)


def _from_shard_major(name, t, block_shape):
    r, c = block_shape
    if name.split("_", 1)[1] in COL_SHARDED:
        return t.reshape(N_CHIPS, r, c).transpose(1, 0, 2).reshape(r, N_CHIPS * c)
    return t.reshape(N_CHIPS * r, c)


def _small_pack(arrs, total_padded):
    flat = jnp.concatenate([a.reshape(-1) for a in arrs])
    return jnp.pad(flat, (0, total_padded - flat.shape[0]))


def kernel(x, positions, l0_norm_g, l0_w_in, l0_ln_g, l0_ln_b, l0_w_s, l0_b_s, l0_w_out, l1_norm_g, l1_w_in, l1_a_re, l1_a_im, l1_log_step, l1_b_re, l1_b_im, l1_c_re, l1_c_im, l1_d_skip, l1_w_glu, l1_b_glu, l1_w_out, l2_norm_g, l2_w_in, l2_q_norm_g, l2_w_uq, l2_kv_norm_g, l2_w_ukv, l2_w_out, l3_norm_g, l3_w_in, l3_ln_g, l3_ln_b, l3_w_s, l3_b_s, l3_w_out, final_norm_g, loss_target, m_l0_norm_g, m_l0_w_in, m_l0_ln_g, m_l0_ln_b, m_l0_w_s, m_l0_b_s, m_l0_w_out, m_l1_norm_g, m_l1_w_in, m_l1_a_re, m_l1_a_im, m_l1_log_step, m_l1_b_re, m_l1_b_im, m_l1_c_re, m_l1_c_im, m_l1_d_skip, m_l1_w_glu, m_l1_b_glu, m_l1_w_out, m_l2_norm_g, m_l2_w_in, m_l2_q_norm_g, m_l2_w_uq, m_l2_kv_norm_g, m_l2_w_ukv, m_l2_w_out, m_l3_norm_g, m_l3_w_in, m_l3_ln_g, m_l3_ln_b, m_l3_w_s, m_l3_b_s, m_l3_w_out, m_final_norm_g, v_l0_norm_g, v_l0_w_in, v_l0_ln_g, v_l0_ln_b, v_l0_w_s, v_l0_b_s, v_l0_w_out, v_l1_norm_g, v_l1_w_in, v_l1_a_re, v_l1_a_im, v_l1_log_step, v_l1_b_re, v_l1_b_im, v_l1_c_re, v_l1_c_im, v_l1_d_skip, v_l1_w_glu, v_l1_b_glu, v_l1_w_out, v_l2_norm_g, v_l2_w_in, v_l2_q_norm_g, v_l2_w_uq, v_l2_kv_norm_g, v_l2_w_ukv, v_l2_w_out, v_l3_norm_g, v_l3_w_in, v_l3_ln_g, v_l3_ln_b, v_l3_w_s, v_l3_b_s, v_l3_w_out, v_final_norm_g):
    args = locals()
    w = {n: args[n] for n in WEIGHT_NAMES}
    mom_m = {n: args["m_" + n] for n in WEIGHT_NAMES}
    mom_v = {n: args["v_" + n] for n in WEIGHT_NAMES}
    h0 = x[0]
    target = loss_target[0]
    pos = positions.reshape(-1, 1)

    big_rows = [w[n].size // PACK_W for n in BIG]
    nrb = sum(big_rows)
    nrb_pad = -(-nrb // PACK_ROW_ALIGN) * PACK_ROW_ALIGN
    full = {}

    def pack_unit(layers):
        names = [n for n in BIG if int(n[1]) in layers]
        rows = [w[n].size // PACK_W for n in names]
        pad = -sum(rows) % PACK_ROW_ALIGN
        return names, rows, _pack_rows([w[n].astype(BF16) for n in names] + [jnp.zeros((pad, PACK_W), BF16)])

    def unpack_unit(names, rows, gathered):
        r0 = 0
        for n, nr in zip(names, rows):
            full[n] = _from_shard_major(n, gathered[:, r0:r0 + nr, :], w[n].shape)
            r0 += nr

    unit0, unit1, unit2 = pack_unit((0,)), pack_unit((1,)), pack_unit((2, 3))
    unpack_unit(unit0[0], unit0[1], weights_allgather(unit0[2]))
    flight = gather_start(unit1[2], unit0[2], "gather_l1_start")
    wp = dict(w)
    wp["l0_norm_g"] = w["l0_norm_g"] + flight[4][0, 0]

    def layer_params(i):
        pre = "l%d_" % i
        p = {k[len(pre):]: v for k, v in wp.items() if k.startswith(pre)}
        wf = {k[len(pre):]: v for k, v in full.items() if k.startswith(pre)}
        return p, wf

    cos, sins = rope_tables(pos)
    h = h0
    saved = []
    for i, kind in enumerate(LAYER_KINDS):
        if i == 1:
            land = gather_wait(*flight[:4], h, "gather_l1_wait")
            got = gather_handover(land, unit1[2], "gather_l1_handover")
            unpack_unit(unit1[0], unit1[1], got)
            flight = gather_start(unit2[2], got, "gather_l23_start")
            wp["l1_norm_g"] = w["l1_norm_g"] + flight[4][0, 0]
        if i == 2:
            land = gather_wait(*flight[:4], h, "gather_l23_wait")
            unpack_unit(unit2[0], unit2[1], gather_handover(land, unit2[2], "gather_l23_handover"))
        p, wf = layer_params(i)
        tag = "l%d" % i
        if kind == "gmlp":
            h, s = gmlp_layer_fwd(h, p, wf, tag)
        elif kind == "s5":
            h, s = s5_layer_fwd(h, p, wf, tag)
        else:
            h, s = mla_layer_fwd(h, p, wf, cos, sins, tag)
        saved.append(s)
    loss_part, dh, g_final = loss_head(h, final_norm_g, target)
    loss = lax.psum(loss_part[0, 0], ("x", "y", "c"))

    grads = {"final_norm_g": g_final.reshape(-1)}
    flights = {}

    def send_big_grads(i, big):
        names = [n for n in BIG if int(n[1]) == i]
        rows = [w[n].size // PACK_W for n in names]
        pad = -sum(rows) % PACK_ROW_ALIGN
        p_i = jnp.concatenate([_shard_major(n, big[n[3:]]) for n in names]
                              + [jnp.zeros((N_CHIPS, pad, PACK_W), F32)], axis=1)
        sib, = grads_to_sibling([p_i], "grads_to_sibling_l%d" % i)
        t_i = pair_sum(p_i, sib, BF16, "pair_sum_l%d" % i)
        flights[i] = (names, rows, reduce_start(t_i, sib, "reduce_l%d_start" % i))
        return flights[i][2][4][0, 0]

    for i in reversed(range(len(LAYER_KINDS))):
        kind = LAYER_KINDS[i]
        p, wf = layer_params(i)
        tag = "l%d" % i
        hook = functools.partial(send_big_grads, i)
        if kind == "gmlp":
            dh, g = gmlp_layer_bwd(dh, saved[i], p, wf, tag, hook)
        elif kind == "s5":
            dh, g = s5_layer_bwd(dh, saved[i], p, wf, tag, hook)
        else:
            dh, g = mla_layer_bwd(dh, saved[i], p, wf, cos, sins, tag, hook)
        for k, val in g.items():
            grads["l%d_%s" % (i, k)] = val
    grad_x = dh[None]

    n_small = sum(w[n].size for n in SMALL)
    piece = N_CHIPS * 2 * 16 * PACK_W
    n_small_pad = -(-n_small // piece) * piece
    nrs = n_small_pad // N_CHIPS // PACK_W
    p_small = _small_pack([grads[n] for n in SMALL], n_small_pad).reshape(N_CHIPS, nrs, PACK_W)
    sib_small, = grads_to_sibling([p_small], "grads_to_sibling_small")
    t_small = pair_sum(p_small, sib_small, F32, "pair_sum_small")
    rb_small, = grads_across_chips([t_small])
    halves = [chip_sum(t_small, rb_small, "chip_sum_small")]

    after = halves[0]
    for i in reversed(range(len(LAYER_KINDS))):
        t_i, rb_i = reduce_wait(*flights[i][2][:4], after, "reduce_l%d_wait" % i)
        halves.append(chip_sum(t_i, rb_i, "chip_sum_l%d" % i))
        after = halves[-1]
    reduced = reduced_to_sibling(halves)
    small_all = small_allgather(reduced[0], 0, nrs)

    g_out, d_out, nm_out, nv_out = {}, {}, {}, {}
    for i, g_i in zip(reversed(range(len(LAYER_KINDS))), reduced[1:]):
        r0 = 0
        for n, nr in zip(flights[i][0], flights[i][1]):
            g_out[n] = g_i[r0:r0 + nr].reshape(w[n].shape)
            d_out[n], nm_out[n], nv_out[n] = adamw(w[n], g_out[n], mom_m[n], mom_v[n], "adamw_" + n)
            r0 += nr
    g_small = small_all.reshape(-1, PACK_W)
    sp = lambda d: _small_pack([d[n] for n in SMALL], n_small_pad).reshape(-1, PACK_W)
    d_small, nm_small, nv_small = adamw(sp(w), g_small, sp(mom_m), sp(mom_v), "adamw_small")
    for buf, out in ((g_small, g_out), (d_small, d_out), (nm_small, nm_out), (nv_small, nv_out)):
        flat = buf.reshape(-1)
        o = 0
        for n in SMALL:
            out[n] = flat[o:o + w[n].size].reshape(w[n].shape)
            o += w[n].size
    return (loss, grad_x, *[g_out[n] for n in WEIGHT_NAMES], *[d_out[n] for n in WEIGHT_NAMES],
            *[nm_out[n] for n in WEIGHT_NAMES], *[nv_out[n] for n in WEIGHT_NAMES])
```

```python
import functools
import math

import jax
import jax.numpy as jnp
import numpy as np
from jax import lax
from jax.experimental import pallas as pl
from jax.experimental.pallas import tpu as pltpu

F32 = jnp.float32
BF16 = jnp.bfloat16
MESH = pl.DeviceIdType.MESH
VMEM_LIMIT_BYTES = 56 * 1024 * 1024
LANES = 128
PACK_W = 1024
CHUNK_W = 256
PACK_ROW_ALIGN = 256
ROW_TILE = 256
ADAMW_BLOCK_BYTES = 1024 * 1024
MM_BLOCK_BYTES = 6 * 1024 * 1024

NORM_EPS = 1e-6
N_CHIPS = 4
GMLP_CHUNK = 128
GMLP_GROUPS = 8
S5_GROUPS = 128
S5_GROUP = 16
S5_STATE = 64
S5_SB = 16
S5_SEG = 8
MLA_HEADS = 16
MLA_NOPE = 128
MLA_ROPE = 64
MLA_Q_RANK = 384
MLA_KV_RANK = 128
MLA_SCALE = (MLA_NOPE + MLA_ROPE) ** -0.5
ROPE_THETA = 10000.0
NEG_INF = -1e30
ADAM_LR, ADAM_B1, ADAM_B2, ADAM_EPS, ADAM_WD, ADAM_STEP = 0.001, 0.9, 0.999, 1e-08, 0.01, 10

DN_NN = (((1,), (0,)), ((), ()))
DN_NT = (((1,), (1,)), ((), ()))
DN_TN = (((0,), (0,)), ((), ()))


def _cparams(sem):
    return pltpu.CompilerParams(dimension_semantics=sem, vmem_limit_bytes=VMEM_LIMIT_BYTES)


def _pick(n, cands=(512, 384, 256, 128)):
    for c in cands:
        if n % c == 0:
            return c
    return n


def _pick_rows(r, cap=512, mult=16):
    return max(t for t in range(mult, cap + 1, mult) if r % t == 0)


def _dot(a, b, dn):
    return lax.dot_general(a.astype(BF16), b.astype(BF16), dn, preferred_element_type=F32)


def _sigmoid(x):
    return 1.0 / (1.0 + jnp.exp(-x))


def _gelu(x):
    c = math.sqrt(2.0 / math.pi)
    t = jnp.tanh(c * (x + 0.044715 * x * x * x))
    return 0.5 * x * (1.0 + t)


def _gelu_grad(x):
    c = math.sqrt(2.0 / math.pi)
    t = jnp.tanh(c * (x + 0.044715 * x * x * x))
    return 0.5 * (1.0 + t) + 0.5 * x * (1.0 - t * t) * c * (1.0 + 3.0 * 0.044715 * x * x)


def _gelu_both(x):
    c = math.sqrt(2.0 / math.pi)
    t = jnp.tanh(c * (x + 0.044715 * x * x * x))
    return 0.5 * x * (1.0 + t), 0.5 * (1.0 + t) + 0.5 * x * (1.0 - t * t) * c * (1.0 + 3.0 * 0.044715 * x * x)


def _silu_both(z):
    s = _sigmoid(z)
    return z * s, s * (1.0 + z * (1.0 - s))


def _silu(z):
    return z * _sigmoid(z)


def _silu_grad(z):
    s = _sigmoid(z)
    return s * (1.0 + z * (1.0 - s))


def matmul(a, b, mode, name, out_dtype=F32, add=None):
    if mode == "nn":
        (m, k), n = a.shape, b.shape[1]
    elif mode == "nt":
        (m, k), n = a.shape, b.shape[0]
    else:
        (k, m), n = a.shape, b.shape[1]
    tm = _pick(m, [t for t in (1024, 512, 384, 256, 128) if t * k * a.dtype.itemsize <= MM_BLOCK_BYTES])
    tn = _pick(n, [t for t in (512, 384, 256, 128) if t * k * b.dtype.itemsize <= MM_BLOCK_BYTES])
    dn = {"nn": DN_NN, "nt": DN_NT, "tn": DN_TN}[mode]

    def body(*refs):
        if add is None:
            a_ref, b_ref, o_ref = refs
        else:
            a_ref, b_ref, add_ref, o_ref = refs
        r = _dot(a_ref[...], b_ref[...], dn)
        if add is not None:
            r = r + add_ref[...].astype(F32)
        o_ref[...] = r.astype(out_dtype)

    a_spec = pl.BlockSpec((k, tm), lambda i, j: (0, i)) if mode == "tn" else pl.BlockSpec((tm, k), lambda i, j: (i, 0))
    b_spec = pl.BlockSpec((tn, k), lambda i, j: (j, 0)) if mode == "nt" else pl.BlockSpec((k, tn), lambda i, j: (0, j))
    o_spec = pl.BlockSpec((tm, tn), lambda i, j: (i, j))
    in_specs = [a_spec, b_spec] + ([o_spec] if add is not None else [])
    args = (a, b) + ((add,) if add is not None else ())
    return pl.pallas_call(
        body, name=name, grid=(m // tm, n // tn), in_specs=in_specs, out_specs=o_spec,
        out_shape=jax.ShapeDtypeStruct((m, n), out_dtype),
        compiler_params=_cparams(("parallel", "arbitrary")))(*args)


def _tile_rows(r, r0, cands=(512, 384, 256, 128)):
    return next(t for t in cands if r % t == 0 and r0 % t == 0)


def matmul_tn_packed(a, b, buf, r0, col_sharded, name):
    k, m = a.shape
    n = b.shape[1]
    if col_sharded:
        chunks = n // N_CHIPS // CHUNK_W
        tm = _tile_rows(m, r0)
        o_map = lambda i, j: (j // chunks, (r0 + (j % chunks) * m) // tm + i, 0)
    else:
        rs = m // N_CHIPS
        tm = _tile_rows(rs, r0)
        per = rs // tm
        o_map = lambda i, j: (i // per, (r0 + j * rs) // tm + i % per, 0)

    def body(a_ref, b_ref, buf_ref, o_ref):
        o_ref[0] = _dot(a_ref[...], b_ref[...], DN_TN)

    return pl.pallas_call(
        body, name=name, grid=(m // tm, n // CHUNK_W),
        in_specs=[pl.BlockSpec((k, tm), lambda i, j: (0, i)), pl.BlockSpec((k, CHUNK_W), lambda i, j: (0, j)),
                  pl.BlockSpec(memory_space=pl.ANY)],
        out_specs=pl.BlockSpec((1, tm, CHUNK_W), o_map), out_shape=jax.ShapeDtypeStruct(buf.shape, buf.dtype),
        input_output_aliases={2: 0}, compiler_params=_cparams(("parallel", "arbitrary")))(a, b, buf)


def _rows(tl, w, col=0):
    return pl.BlockSpec((tl, w), lambda i: (i, col))


def _full(shape):
    nd = len(shape)
    return pl.BlockSpec(tuple(shape), lambda i: (0,) * nd)


def _rowcall(body, name, n_steps, in_specs, out_specs, out_shape, scratch=()):
    return pl.pallas_call(
        body, name=name, grid=(n_steps,), in_specs=in_specs, out_specs=out_specs, out_shape=out_shape,
        scratch_shapes=list(scratch), compiler_params=_cparams(("arbitrary",)))


def _acc(ref, val, i):
    @pl.when(i == 0)
    def _():
        ref[...] = val

    @pl.when(i != 0)
    def _():
        ref[...] += val


def rms_fwd(h, g, name):
    l, d = h.shape
    tl = ROW_TILE

    def body(h_ref, g_ref, o_ref):
        x = h_ref[...]
        r = lax.rsqrt(jnp.mean(x * x, axis=-1, keepdims=True) + NORM_EPS)
        o_ref[...] = (x * r * g_ref[...]).astype(BF16)

    return _rowcall(body, name, l // tl, [_rows(tl, d), _full((1, d))], _rows(tl, d),
                    jax.ShapeDtypeStruct((l, d), BF16))(h, g.reshape(1, d))


def rms_bwd(h, g, dhn, dh_in, name):
    l, d = h.shape
    tl = ROW_TILE

    def body(h_ref, g_ref, dhn_ref, dhi_ref, dh_ref, dg_ref):
        i = pl.program_id(0)
        x = h_ref[...]
        r = lax.rsqrt(jnp.mean(x * x, axis=-1, keepdims=True) + NORM_EPS)
        xhat = x * r
        dy = dhn_ref[...]
        dxh = dy * g_ref[...]
        dx = r * (dxh - xhat * jnp.mean(dxh * xhat, axis=-1, keepdims=True))
        dh_ref[...] = dhi_ref[...] + dx
        _acc(dg_ref, jnp.sum(dy * xhat, axis=0, keepdims=True), i)

    return _rowcall(body, name, l // tl, [_rows(tl, d), _full((1, d)), _rows(tl, d), _rows(tl, d)],
                    [_rows(tl, d), _full((1, d))],
                    [jax.ShapeDtypeStruct((l, d), F32), jax.ShapeDtypeStruct((1, d), F32)])(h, g.reshape(1, d), dhn, dh_in)


def loss_head(h, g, target):
    l, d = h.shape
    tl = ROW_TILE

    def body(h_ref, g_ref, t_ref, loss_ref, dh_ref, dg_ref):
        i = pl.program_id(0)
        x = h_ref[...]
        gg = g_ref[...]
        r = lax.rsqrt(jnp.mean(x * x, axis=-1, keepdims=True) + NORM_EPS)
        xhat = x * r
        err = xhat * gg - t_ref[...]
        part = 0.5 * jnp.sum(jnp.mean(err * err, axis=-1, keepdims=True), axis=0, keepdims=True)
        _acc(loss_ref, part, i)
        dy = err * (1.0 / d)
        dxh = dy * gg
        dh_ref[...] = r * (dxh - xhat * jnp.mean(dxh * xhat, axis=-1, keepdims=True))
        _acc(dg_ref, jnp.sum(dy * xhat, axis=0, keepdims=True), i)

    return _rowcall(body, "loss_head", l // tl, [_rows(tl, d), _full((1, d)), _rows(tl, d)],
                    [_full((1, 1)), _rows(tl, d), _full((1, d))],
                    [jax.ShapeDtypeStruct((1, 1), F32), jax.ShapeDtypeStruct((l, d), F32),
                     jax.ShapeDtypeStruct((1, d), F32)])(h, g.reshape(1, d), target)


def _gmlp_common(a_ref, lng_ref, lnb_ref):
    di = lng_ref.shape[1]
    u_pre = a_ref[:, 0:di]
    v_pre = a_ref[:, di:2 * di]
    z = a_ref[:, 2 * di:3 * di]
    vg = _gelu(v_pre)
    mu = jnp.mean(vg, axis=-1, keepdims=True)
    xc = vg - mu
    rstd = lax.rsqrt(jnp.mean(xc * xc, axis=-1, keepdims=True) + NORM_EPS)
    vhat = xc * rstd
    vn = vhat * lng_ref[...] + lnb_ref[...]
    return u_pre, v_pre, z, vhat, rstd, vn


def _tril(w):
    r = lax.broadcasted_iota(jnp.int32, w.shape, 0)
    c = lax.broadcasted_iota(jnp.int32, w.shape, 1)
    return jnp.where(c <= r, w, 0.0)


def gmlp_gate_fwd(a, ln_g, ln_b, w_s, b_s, name):
    l, w3 = a.shape
    di = w3 // 3
    dg = di // GMLP_GROUPS
    tl = GMLP_CHUNK

    def body(a_ref, lng_ref, lnb_ref, ws_ref, bs_ref, m_ref):
        u_pre, _, z, _, _, vn = _gmlp_common(a_ref, lng_ref, lnb_ref)
        gate = _gelu(u_pre) * _silu(z)
        for g in range(GMLP_GROUPS):
            sl = slice(g * dg, (g + 1) * dg)
            s = _dot(_tril(ws_ref[g]), vn[:, sl], DN_NN) + bs_ref[g]
            m_ref[:, sl] = (gate[:, sl] * s).astype(BF16)

    return _rowcall(body, name, l // tl,
                    [_rows(tl, w3), _full((1, di)), _full((1, di)), _full(w_s.shape), _full((GMLP_GROUPS, tl, 1))],
                    _rows(tl, di), jax.ShapeDtypeStruct((l, di), BF16))(
        a, ln_g.reshape(1, di), ln_b.reshape(1, di), w_s, b_s.reshape(GMLP_GROUPS, tl, 1))


def gmlp_gate_bwd(a, dm, ln_g, ln_b, w_s, b_s, name):
    l, w3 = a.shape
    di = w3 // 3
    dg = di // GMLP_GROUPS
    tl = GMLP_CHUNK

    def body(a_ref, dm_ref, lng_ref, lnb_ref, ws_ref, bs_ref, da_ref, dlg_ref, dlb_ref, dws_ref, dbs_ref,
             dvn_ref, vh_ref, gv_ref):
        i = pl.program_id(0)
        vg, gv = _gelu_both(a_ref[:, di:2 * di])
        gv_ref[...] = gv
        xc = vg - jnp.mean(vg, axis=-1, keepdims=True)
        rstd = lax.rsqrt(jnp.mean(xc * xc, axis=-1, keepdims=True) + NORM_EPS)
        vh_ref[...] = xc * rstd
        for g in range(GMLP_GROUPS):
            sl = slice(g * dg, (g + 1) * dg)
            wt = _tril(ws_ref[g])
            vn_g = vh_ref[:, sl] * lng_ref[:, sl] + lnb_ref[:, sl]
            s = _dot(wt, vn_g, DN_NN) + bs_ref[g]
            dmg = dm_ref[:, sl]
            u, gu = _gelu_both(a_ref[:, sl])
            sz, gz = _silu_both(a_ref[:, 2 * di + g * dg:2 * di + (g + 1) * dg])
            ds = dmg * u * sz
            da_ref[:, sl] = (dmg * s * sz * gu).astype(BF16)
            da_ref[:, 2 * di + g * dg:2 * di + (g + 1) * dg] = (dmg * u * s * gz).astype(BF16)
            dvn_ref[:, sl] = _dot(wt, ds, DN_TN)
            dw = _tril(_dot(ds, vn_g, DN_NT))
            db = jnp.sum(ds, axis=1, keepdims=True)

            @pl.when(i == 0)
            def _():
                dws_ref[g] = dw
                dbs_ref[g] = db

            @pl.when(i != 0)
            def _():
                dws_ref[g] += dw
                dbs_ref[g] += db

        dvn = dvn_ref[...]
        vhat = vh_ref[...]
        dxh = dvn * lng_ref[...]
        dvg = rstd * (dxh - jnp.mean(dxh, axis=-1, keepdims=True) - vhat * jnp.mean(dxh * vhat, axis=-1, keepdims=True))
        da_ref[:, di:2 * di] = (dvg * gv_ref[...]).astype(BF16)
        _acc(dlg_ref, jnp.sum(dvn * vhat, axis=0, keepdims=True), i)
        _acc(dlb_ref, jnp.sum(dvn, axis=0, keepdims=True), i)

    outs = _rowcall(
        body, name, l // tl,
        [_rows(tl, w3), _rows(tl, di), _full((1, di)), _full((1, di)), _full(w_s.shape), _full((GMLP_GROUPS, tl, 1))],
        [_rows(tl, w3), _full((1, di)), _full((1, di)), _full(w_s.shape), _full((GMLP_GROUPS, tl, 1))],
        [jax.ShapeDtypeStruct((l, w3), BF16), jax.ShapeDtypeStruct((1, di), F32), jax.ShapeDtypeStruct((1, di), F32),
         jax.ShapeDtypeStruct(w_s.shape, F32), jax.ShapeDtypeStruct((GMLP_GROUPS, tl, 1), F32)],
        scratch=[pltpu.VMEM((tl, di), F32)] * 3)(
        a, dm, ln_g.reshape(1, di), ln_b.reshape(1, di), w_s, b_s.reshape(GMLP_GROUPS, tl, 1))
    return outs


def gmlp_layer_fwd(h, p, wf, tag):
    hn = rms_fwd(h, p["norm_g"], tag + "_rms")
    a = matmul(hn, wf["w_in"], "nn", tag + "_mm_in")
    m = gmlp_gate_fwd(a, p["ln_g"], p["ln_b"], p["w_s"], p["b_s"], tag + "_gate")
    h_out = matmul(m, wf["w_out"], "nn", tag + "_mm_out", add=h)
    return h_out, (h, hn, a, m)


def gmlp_layer_bwd(dh_out, saved, p, wf, tag, sink):
    h, hn, a, m = saved
    dm = matmul(dh_out, wf["w_out"], "nt", tag + "_mm_dm")
    sink.mm("w_out", m, dh_out, tag + "_mm_gwout")
    da, dlg, dlb, dws, dbs = gmlp_gate_bwd(a, dm, p["ln_g"], p["ln_b"], p["w_s"], p["b_s"], tag + "_gate_bwd")
    dhn = matmul(da, wf["w_in"], "nt", tag + "_mm_dhn")
    sink.mm("w_in", hn, da, tag + "_mm_gwin")
    zero = sink.send()
    dh, dng = rms_bwd(h, p["norm_g"] + zero, dhn, dh_out, tag + "_rms_bwd")
    grads = {"norm_g": dng.reshape(-1), "ln_g": dlg.reshape(-1), "ln_b": dlb.reshape(-1),
             "w_s": dws, "b_s": dbs.reshape(GMLP_GROUPS, GMLP_CHUNK)}
    return dh, grads


def _cmul(ar, ai, br, bi):
    return ar * br - ai * bi, ar * bi + ai * br


S5_PG = 16


def _gblock(tail):
    return pl.BlockSpec((S5_PG,) + tuple(tail), lambda i: (i, 0, 0))


def s5_params_fwd(a_re, a_im, log_step, b_re, b_im):
    g, p, hh = b_re.shape

    def body(ar_ref, ai_ref, ls_ref, br_ref, bi_ref, lr_ref, li_ref, bbr_ref, bbi_ref):
        ar, ai = ar_ref[...], ai_ref[...]
        step = jnp.exp(ls_ref[...])
        mag = jnp.exp(ar * step)
        lr, li = mag * jnp.cos(ai * step), mag * jnp.sin(ai * step)
        den = 1.0 / (ar * ar + ai * ai)
        fr, fi = _cmul(lr - 1.0, li, ar * den, -ai * den)
        lr_ref[...] = lr
        li_ref[...] = li
        bbr, bbi = _cmul(fr, fi, br_ref[...], bi_ref[...])
        bbr_ref[...] = bbr
        bbi_ref[...] = bbi

    s1 = jax.ShapeDtypeStruct((g, p, 1), F32)
    s3 = jax.ShapeDtypeStruct((g, p, hh), F32)
    b1, b0, b3 = _gblock((p, 1)), _gblock((1, 1)), _gblock((p, hh))
    return pl.pallas_call(body, name="s5_params_fwd", grid=(g // S5_PG,), in_specs=[b1, b1, b0, b3, b3],
                          out_specs=[b1, b1, b3, b3], out_shape=[s1, s1, s3, s3],
                          compiler_params=_cparams(("parallel",)))(
        a_re.reshape(g, p, 1), a_im.reshape(g, p, 1), log_step.reshape(g, 1, 1), b_re, b_im)


def s5_params_bwd(a_re, a_im, log_step, b_re, b_im, dl_re, dl_im, dbb_re, dbb_im):
    g, p, hh = b_re.shape

    def body(ar_ref, ai_ref, ls_ref, br_ref, bi_ref, dlr_ref, dli_ref, dbr_ref, dbi_ref,
             gar_ref, gai_ref, gls_ref, gbr_ref, gbi_ref):
        ar, ai = ar_ref[...], ai_ref[...]
        step = jnp.exp(ls_ref[...])
        mag = jnp.exp(ar * step)
        lr, li = mag * jnp.cos(ai * step), mag * jnp.sin(ai * step)
        den = 1.0 / (ar * ar + ai * ai)
        ir, ii = ar * den, -ai * den
        fr, fi = _cmul(lr - 1.0, li, ir, ii)
        br, bi = br_ref[...], bi_ref[...]
        dbr, dbi = dbr_ref[...], dbi_ref[...]
        gbr, gbi = _cmul(fr, -fi, dbr, dbi)
        gbr_ref[...] = gbr
        gbi_ref[...] = gbi
        pr, pi = _cmul(br, -bi, dbr, dbi)
        gfr = jnp.sum(pr, axis=-1, keepdims=True)
        gfi = jnp.sum(pi, axis=-1, keepdims=True)
        t_r, t_i = _cmul(ir, -ii, gfr, gfi)
        glr, gli = dlr_ref[...] + t_r, dli_ref[...] + t_i
        c1r, c1i = _cmul(step * lr, -step * li, glr, gli)
        qr, qi = _cmul(fr, fi, ir, ii)
        c2r, c2i = _cmul(-qr, qi, gfr, gfi)
        gar_ref[...] = c1r + c2r
        gai_ref[...] = c1i + c2i
        wr, wi = _cmul(ar, ai, lr, li)
        sr, _ = _cmul(wr, -wi, glr, gli)
        gls_ref[...] = jnp.sum(sr, axis=1, keepdims=True) * step

    s1 = jax.ShapeDtypeStruct((g, p, 1), F32)
    s3 = jax.ShapeDtypeStruct((g, p, hh), F32)
    b1, b0, b3 = _gblock((p, 1)), _gblock((1, 1)), _gblock((p, hh))
    return pl.pallas_call(body, name="s5_params_bwd", grid=(g // S5_PG,),
                          in_specs=[b1, b1, b0, b3, b3, b1, b1, b3, b3], out_specs=[b1, b1, b0, b3, b3],
                          out_shape=[s1, s1, jax.ShapeDtypeStruct((g, 1, 1), F32), s3, s3],
                          compiler_params=_cparams(("parallel",)))(
        a_re.reshape(g, p, 1), a_im.reshape(g, p, 1), log_step.reshape(g, 1, 1), b_re, b_im,
        dl_re, dl_im, dbb_re, dbb_im)


def _blockdiag(t):
    sb, n, r, c = t.shape
    eye = jnp.eye(n, dtype=bool)[None, :, None, :, None]
    full = jnp.where(eye, t[:, :, :, None, :], jnp.zeros((), t.dtype))
    return full.reshape(sb, n * r, n * c)


def _blockdiag_extract(m, r, c):
    sb = m.shape[0]
    n = m.shape[1] // r
    m5 = m.reshape(sb, n, r, n, c)
    return jnp.stack([m5[:, i, :, i, :] for i in range(n)], axis=1)


S5_TB = 64
S5_UNROLL = 8


def s5_scan_fwd(a_p, lam_re, lam_im, wb_re, wb_im, wc_re, wc_im, d_skip, x0_re, x0_im, name):
    l = a_p.shape[0]
    di = d_skip.shape[1]
    rows = S5_SEG * S5_TB
    nb = l // rows
    ns = wb_re.shape[2]

    def body(u_ref, lr_ref, li_ref, wbr_ref, wbi_ref, wcr_ref, wci_ref, ds_ref, x0r_ref, x0i_ref,
             y_ref, ckr_ref, cki_ref, xer_ref, xei_ref, bur, bui, xr_s, xi_s):
        b = pl.program_id(1)

        @pl.when(b == 0)
        def _():
            xr_s[...] = x0r_ref[0]
            xi_s[...] = x0i_ref[0]

        ckr_ref[0, 0] = xr_s[...]
        cki_ref[0, 0] = xi_s[...]
        u = u_ref[...]
        bur[...] = _dot(u, wbr_ref[0], DN_NN)
        bui[...] = _dot(u, wbi_ref[0], DN_NN)
        lr = jnp.broadcast_to(lr_ref[0], (S5_SEG, ns))
        li = jnp.broadcast_to(li_ref[0], (S5_SEG, ns))

        def step(t, carry):
            xr, xi = carry
            sl = pl.ds(pl.multiple_of(t * S5_SEG, S5_SEG), S5_SEG)
            nr = lr * xr - li * xi + bur[sl, :]
            ni = lr * xi + li * xr + bui[sl, :]
            bur[sl, :] = nr
            bui[sl, :] = ni
            return nr, ni

        xr, xi = lax.fori_loop(0, S5_TB, step, (xr_s[...], xi_s[...]), unroll=S5_UNROLL)
        xr_s[...] = xr
        xi_s[...] = xi
        xer_ref[0] = xr
        xei_ref[0] = xi
        y_ref[...] = _dot(bur[...], wcr_ref[0], DN_NN) - _dot(bui[...], wci_ref[0], DN_NN) + ds_ref[...] * u

    sb3 = lambda s, b: (s, 0, 0)
    st = jax.ShapeDtypeStruct
    return pl.pallas_call(
        body, name=name, grid=(S5_SB, nb),
        in_specs=[pl.BlockSpec((rows, LANES), lambda s, b: (b, s)),
                  pl.BlockSpec((1, 1, ns), sb3), pl.BlockSpec((1, 1, ns), sb3),
                  pl.BlockSpec((1, LANES, ns), sb3), pl.BlockSpec((1, LANES, ns), sb3),
                  pl.BlockSpec((1, ns, LANES), sb3), pl.BlockSpec((1, ns, LANES), sb3),
                  pl.BlockSpec((1, LANES), lambda s, b: (0, s)),
                  pl.BlockSpec((1, S5_SEG, ns), sb3), pl.BlockSpec((1, S5_SEG, ns), sb3)],
        out_specs=[pl.BlockSpec((rows, LANES), lambda s, b: (b, s)),
                   pl.BlockSpec((1, 1, S5_SEG, ns), lambda s, b: (s, b, 0, 0)),
                   pl.BlockSpec((1, 1, S5_SEG, ns), lambda s, b: (s, b, 0, 0)),
                   pl.BlockSpec((1, S5_SEG, ns), sb3), pl.BlockSpec((1, S5_SEG, ns), sb3)],
        out_shape=[st((l, di), F32), st((S5_SB, nb, S5_SEG, ns), F32), st((S5_SB, nb, S5_SEG, ns), F32),
                   st((S5_SB, S5_SEG, ns), F32), st((S5_SB, S5_SEG, ns), F32)],
        scratch_shapes=[pltpu.VMEM((rows, ns), F32), pltpu.VMEM((rows, ns), F32),
                        pltpu.VMEM((S5_SEG, ns), F32), pltpu.VMEM((S5_SEG, ns), F32)],
        compiler_params=_cparams(("parallel", "arbitrary")))(
        a_p, lam_re, lam_im, wb_re, wb_im, wc_re, wc_im, d_skip, x0_re, x0_im)


def s5_ends(inp, lam_re, lam_im, w_re, w_im, adjoint, name):
    l = inp.shape[0]
    rows = S5_SEG * S5_TB
    nb = l // rows
    ns = lam_re.shape[2]

    def body(i_ref, lr_ref, li_ref, wr_ref, wi_ref, er_ref, ei_ref, pr_b, pi_b, xr_s, xi_s):
        b = pl.program_id(1)

        @pl.when(b == 0)
        def _():
            xr_s[...] = jnp.zeros_like(xr_s)
            xi_s[...] = jnp.zeros_like(xi_s)

        v = i_ref[...]
        lr = jnp.broadcast_to(lr_ref[0], (S5_SEG, ns))
        li = jnp.broadcast_to(li_ref[0], (S5_SEG, ns))
        if adjoint:
            pr_b[...] = _dot(v, wr_ref[0], DN_NT)
            pi_b[...] = -_dot(v, wi_ref[0], DN_NT)
            li = -li
        else:
            pr_b[...] = _dot(v, wr_ref[0], DN_NN)
            pi_b[...] = _dot(v, wi_ref[0], DN_NN)

        def step(k, carry):
            xr, xi = carry
            t = S5_TB - 1 - k if adjoint else k
            sl = pl.ds(pl.multiple_of(t * S5_SEG, S5_SEG), S5_SEG)
            return lr * xr - li * xi + pr_b[sl, :], lr * xi + li * xr + pi_b[sl, :]

        xr, xi = lax.fori_loop(0, S5_TB, step, (xr_s[...], xi_s[...]), unroll=S5_UNROLL)
        xr_s[...] = xr
        xi_s[...] = xi
        er_ref[0] = xr
        ei_ref[0] = xi

    sb3 = lambda s, b: (s, 0, 0)
    blk = (lambda s, b: (nb - 1 - b, s)) if adjoint else (lambda s, b: (b, s))
    wshape = (1, ns, LANES) if adjoint else (1, LANES, ns)
    st = jax.ShapeDtypeStruct((S5_SB, S5_SEG, ns), F32)
    return pl.pallas_call(
        body, name=name, grid=(S5_SB, nb),
        in_specs=[pl.BlockSpec((rows, LANES), blk), pl.BlockSpec((1, 1, ns), sb3), pl.BlockSpec((1, 1, ns), sb3),
                  pl.BlockSpec(wshape, sb3), pl.BlockSpec(wshape, sb3)],
        out_specs=[pl.BlockSpec((1, S5_SEG, ns), sb3), pl.BlockSpec((1, S5_SEG, ns), sb3)],
        out_shape=[st, st],
        scratch_shapes=[pltpu.VMEM((rows, ns), F32), pltpu.VMEM((rows, ns), F32),
                        pltpu.VMEM((S5_SEG, ns), F32), pltpu.VMEM((S5_SEG, ns), F32)],
        compiler_params=_cparams(("parallel", "arbitrary")))(inp, lam_re, lam_im, w_re, w_im)


def s5_scan_bwd(a_p, dy, lam_re, lam_im, wb_re, wb_im, wc_re, wc_im, d_skip, ck_re, ck_im, a0_re, a0_im, name):
    l = a_p.shape[0]
    di = d_skip.shape[1]
    rows = S5_SEG * S5_TB
    nb = l // rows
    ns = wb_re.shape[2]

    def body(u_ref, dy_ref, lr_ref, li_ref, wbr_ref, wbi_ref, wcr_ref, wci_ref, ds_ref, ckr_ref, cki_ref,
             a0r_ref, a0i_ref,
             du_ref, dwbr_ref, dwbi_ref, dwcr_ref, dwci_ref, dds_ref, dlr_ref, dli_ref, aer_ref, aei_ref,
             xr_b, xi_b, gr_b, gi_b, ar_s, ai_s):
        b = pl.program_id(1)

        @pl.when(b == 0)
        def _():
            ar_s[...] = a0r_ref[0]
            ai_s[...] = a0i_ref[0]

        u = u_ref[...]
        dyv = dy_ref[...]
        lr = jnp.broadcast_to(lr_ref[0], (S5_SEG, ns))
        li = jnp.broadcast_to(li_ref[0], (S5_SEG, ns))
        xr_b[...] = _dot(u, wbr_ref[0], DN_NN)
        xi_b[...] = _dot(u, wbi_ref[0], DN_NN)

        def fstep(t, carry):
            xr, xi = carry
            sl = pl.ds(pl.multiple_of(t * S5_SEG, S5_SEG), S5_SEG)
            nr = lr * xr - li * xi + xr_b[sl, :]
            ni = lr * xi + li * xr + xi_b[sl, :]
            xr_b[sl, :] = nr
            xi_b[sl, :] = ni
            return nr, ni

        x0r, x0i = ckr_ref[0, 0], cki_ref[0, 0]
        lax.fori_loop(0, S5_TB, fstep, (x0r, x0i), unroll=S5_UNROLL)
        dwcr = _dot(xr_b[...], dyv, DN_TN)
        dwci = -_dot(xi_b[...], dyv, DN_TN)
        gr_b[...] = _dot(dyv, wcr_ref[0], DN_NT)
        gi_b[...] = -_dot(dyv, wci_ref[0], DN_NT)

        def bstep(k, carry):
            ar, ai, dlr, dli = carry
            t = S5_TB - 1 - k
            sl = pl.ds(pl.multiple_of(t * S5_SEG, S5_SEG), S5_SEG)
            slp = pl.ds(pl.multiple_of(jnp.maximum(t - 1, 0) * S5_SEG, S5_SEG), S5_SEG)
            nr = gr_b[sl, :] + lr * ar + li * ai
            ni = gi_b[sl, :] + lr * ai - li * ar
            gr_b[sl, :] = nr
            gi_b[sl, :] = ni
            first = t == 0
            pr = jnp.where(first, x0r, xr_b[slp, :])
            pi = jnp.where(first, x0i, xi_b[slp, :])
            dlr = dlr + nr * pr + ni * pi
            dli = dli + ni * pr - nr * pi
            return nr, ni, dlr, dli

        zero = jnp.zeros((S5_SEG, ns), F32)
        ar, ai, dlr, dli = lax.fori_loop(0, S5_TB, bstep, (ar_s[...], ai_s[...], zero, zero), unroll=S5_UNROLL)
        ar_s[...] = ar
        ai_s[...] = ai
        aer_ref[0] = ar
        aei_ref[0] = ai
        dsk = ds_ref[...]
        du_ref[...] = (_dot(gr_b[...], wbr_ref[0], DN_NT) + _dot(gi_b[...], wbi_ref[0], DN_NT) + dsk * dyv).astype(BF16)
        dwbr = _dot(u, gr_b[...], DN_TN)
        dwbi = _dot(u, gi_b[...], DN_TN)
        dds = jnp.sum(dyv * u, axis=0, keepdims=True)

        @pl.when(b == 0)
        def _():
            dwbr_ref[0] = dwbr
            dwbi_ref[0] = dwbi
            dwcr_ref[0] = dwcr
            dwci_ref[0] = dwci
            dds_ref[...] = dds
            dlr_ref[0] = dlr
            dli_ref[0] = dli

        @pl.when(b != 0)
        def _():
            dwbr_ref[0] += dwbr
            dwbi_ref[0] += dwbi
            dwcr_ref[0] += dwcr
            dwci_ref[0] += dwci
            dds_ref[...] += dds
            dlr_ref[0] += dlr
            dli_ref[0] += dli

    sb3 = lambda s, b: (s, 0, 0)
    rev = lambda s, b: (nb - 1 - b, s)
    st = jax.ShapeDtypeStruct
    return pl.pallas_call(
        body, name=name, grid=(S5_SB, nb),
        in_specs=[pl.BlockSpec((rows, LANES), rev), pl.BlockSpec((rows, LANES), rev),
                  pl.BlockSpec((1, 1, ns), sb3), pl.BlockSpec((1, 1, ns), sb3),
                  pl.BlockSpec((1, LANES, ns), sb3), pl.BlockSpec((1, LANES, ns), sb3),
                  pl.BlockSpec((1, ns, LANES), sb3), pl.BlockSpec((1, ns, LANES), sb3),
                  pl.BlockSpec((1, LANES), lambda s, b: (0, s)),
                  pl.BlockSpec((1, 1, S5_SEG, ns), lambda s, b: (s, nb - 1 - b, 0, 0)),
                  pl.BlockSpec((1, 1, S5_SEG, ns), lambda s, b: (s, nb - 1 - b, 0, 0)),
                  pl.BlockSpec((1, S5_SEG, ns), sb3), pl.BlockSpec((1, S5_SEG, ns), sb3)],
        out_specs=[pl.BlockSpec((rows, LANES), rev),
                   pl.BlockSpec((1, LANES, ns), sb3), pl.BlockSpec((1, LANES, ns), sb3),
                   pl.BlockSpec((1, ns, LANES), sb3), pl.BlockSpec((1, ns, LANES), sb3),
                   pl.BlockSpec((1, LANES), lambda s, b: (0, s)),
                   pl.BlockSpec((1, S5_SEG, ns), sb3), pl.BlockSpec((1, S5_SEG, ns), sb3),
                   pl.BlockSpec((1, S5_SEG, ns), sb3), pl.BlockSpec((1, S5_SEG, ns), sb3)],
        out_shape=[st((l, di), BF16), st((S5_SB, LANES, ns), F32), st((S5_SB, LANES, ns), F32),
                   st((S5_SB, ns, LANES), F32), st((S5_SB, ns, LANES), F32), st((1, di), F32),
                   st((S5_SB, S5_SEG, ns), F32), st((S5_SB, S5_SEG, ns), F32),
                   st((S5_SB, S5_SEG, ns), F32), st((S5_SB, S5_SEG, ns), F32)],
        scratch_shapes=[pltpu.VMEM((rows, ns), F32), pltpu.VMEM((rows, ns), F32),
                        pltpu.VMEM((rows, ns), F32), pltpu.VMEM((rows, ns), F32),
                        pltpu.VMEM((S5_SEG, ns), F32), pltpu.VMEM((S5_SEG, ns), F32)],
        compiler_params=_cparams(("parallel", "arbitrary")))(
        a_p, dy, lam_re, lam_im, wb_re, wb_im, wc_re, wc_im, d_skip, ck_re, ck_im, a0_re, a0_im)


def s5_carry(e_re, e_im, lam_re, lam_im, seg_len, reverse, name):
    sb, seg, ns = e_re.shape

    def body(er_ref, ei_ref, lr_ref, li_ref, cr_ref, ci_ref):
        pr, pi = lr_ref[...], li_ref[...]
        if reverse:
            pi = -pi
        for _ in range(int(math.log2(seg_len))):
            pr, pi = _cmul(pr, pi, pr, pi)
        er, ei = er_ref[...], ei_ref[...]
        row = lax.broadcasted_iota(jnp.int32, (sb, seg, ns), 1)
        cr = jnp.zeros((sb, seg, ns), F32)
        ci = jnp.zeros((sb, seg, ns), F32)
        cur_r = jnp.zeros((sb, 1, ns), F32)
        cur_i = jnp.zeros((sb, 1, ns), F32)
        order = range(seg - 2, -1, -1) if reverse else range(1, seg)
        for s in order:
            src = s + 1 if reverse else s - 1
            mr, mi = _cmul(pr, pi, cur_r, cur_i)
            cur_r = jnp.sum(jnp.where(row == src, er, 0.0), axis=1, keepdims=True) + mr
            cur_i = jnp.sum(jnp.where(row == src, ei, 0.0), axis=1, keepdims=True) + mi
            cr = jnp.where(row == s, cur_r, cr)
            ci = jnp.where(row == s, cur_i, ci)
        cr_ref[...] = cr
        ci_ref[...] = ci

    st = jax.ShapeDtypeStruct((sb, seg, ns), F32)
    return pl.pallas_call(body, name=name, out_shape=[st, st],
                          compiler_params=pltpu.CompilerParams(vmem_limit_bytes=VMEM_LIMIT_BYTES))(e_re, e_im, lam_re, lam_im)


def s5_act(y, name):
    l, d = y.shape
    tl = ROW_TILE

    def body(y_ref, o_ref):
        o_ref[...] = _gelu(y_ref[...]).astype(BF16)

    return _rowcall(body, name, l // tl, [_rows(tl, d)], _rows(tl, d), jax.ShapeDtypeStruct((l, d), BF16))(y)


def s5_gate_fwd(y, t, b_glu, a_p, name):
    l, d = y.shape
    tl = ROW_TILE

    def body(y_ref, t_ref, b_ref, z_ref, m_ref):
        yg = _gelu(y_ref[...])
        m_ref[...] = (yg * _sigmoid(t_ref[...] + b_ref[...]) * _silu(z_ref[...])).astype(BF16)

    return _rowcall(body, name, l // tl, [_rows(tl, d), _rows(tl, d), _full((1, d)), _rows(tl, d, 1)], _rows(tl, d),
                    jax.ShapeDtypeStruct((l, d), BF16))(y, t, b_glu.reshape(1, d), a_p)


def s5_gate_bwd(dm, y, t, b_glu, a_p, name):
    l, d = y.shape
    tl = ROW_TILE

    def body(dm_ref, y_ref, t_ref, b_ref, z_ref, dt_ref, dyg_ref, dz_ref, db_ref):
        i = pl.program_id(0)
        dmv = dm_ref[...]
        z = z_ref[...]
        yg = _gelu(y_ref[...])
        sg = _sigmoid(t_ref[...] + b_ref[...])
        y2 = yg * sg
        sz, gz = _silu_both(z)
        dy2 = dmv * sz
        dz_ref[...] = (dmv * y2 * gz).astype(BF16)
        dyg_ref[...] = dy2 * sg
        dt = dy2 * yg * sg * (1.0 - sg)
        dt_ref[...] = dt.astype(BF16)
        _acc(db_ref, jnp.sum(dt, axis=0, keepdims=True), i)

    st = jax.ShapeDtypeStruct
    return _rowcall(body, name, l // tl, [_rows(tl, d), _rows(tl, d), _rows(tl, d), _full((1, d)), _rows(tl, d, 1)],
                    [_rows(tl, d), _rows(tl, d), _rows(tl, d), _full((1, d))],
                    [st((l, d), BF16), st((l, d), F32), st((l, d), BF16), st((1, d), F32)])(
        dm, y, t, b_glu.reshape(1, d), a_p)


def s5_act_bwd(y, dyg_a, dyg_b, name):
    l, d = y.shape
    tl = ROW_TILE

    def body(y_ref, a_ref, b_ref, o_ref):
        o_ref[...] = (a_ref[...] + b_ref[...]) * _gelu_grad(y_ref[...])

    return _rowcall(body, name, l // tl, [_rows(tl, d)] * 3, _rows(tl, d), jax.ShapeDtypeStruct((l, d), F32))(y, dyg_a, dyg_b)


def _seg_perm(t):
    l, d = t.shape
    return t.reshape(S5_SEG, l // S5_SEG, d).transpose(1, 0, 2).reshape(l, d)


def _seg_unperm(t):
    l, d = t.shape
    return t.reshape(l // S5_SEG, S5_SEG, d).transpose(1, 0, 2).reshape(l, d)


def _s5_weights(p):
    lr, li, bbr, bbi = s5_params_fwd(p["a_re"], p["a_im"], p["log_step"], p["b_re"], p["b_im"])
    ns = 8 * S5_STATE
    lam_re = lr.reshape(S5_SB, 1, ns)
    lam_im = li.reshape(S5_SB, 1, ns)
    to_bd = lambda t: _blockdiag(t.reshape(S5_SB, 8, t.shape[1], t.shape[2]))
    wb_re = to_bd(bbr.transpose(0, 2, 1)).astype(BF16)
    wb_im = to_bd(bbi.transpose(0, 2, 1)).astype(BF16)
    wc_re = to_bd(p["c_re"].transpose(0, 2, 1)).astype(BF16)
    wc_im = to_bd(p["c_im"].transpose(0, 2, 1)).astype(BF16)
    return lam_re, lam_im, wb_re, wb_im, wc_re, wc_im


def s5_layer_fwd(h, p, wf, tag):
    l = h.shape[0]
    di = p["d_skip"].shape[0]
    hn = rms_fwd(h, p["norm_g"], tag + "_rms")
    hn_p = _seg_perm(hn)
    a_p = matmul(hn_p, wf["w_in"], "nn", tag + "_mm_in")
    sw = _s5_weights(p)
    dsk = p["d_skip"].reshape(1, di)
    e_re, e_im = s5_ends(a_p, sw[0], sw[1], sw[2], sw[3], False, tag + "_scan_ends")
    c_re, c_im = s5_carry(e_re, e_im, sw[0], sw[1], l // S5_SEG, False, tag + "_carry")
    y, ck_re, ck_im, _, _ = s5_scan_fwd(a_p, *sw, dsk, c_re, c_im, tag + "_scan")
    yg = s5_act(y, tag + "_act")
    t = matmul(yg, wf["w_glu"], "nn", tag + "_mm_glu")
    m = s5_gate_fwd(y, t, p["b_glu"], a_p, tag + "_gate")
    out_p = matmul(m, wf["w_out"], "nn", tag + "_mm_out")
    h_out = residual_add(h, _seg_unperm(out_p), tag + "_res")
    return h_out, (h, hn_p, a_p, sw, ck_re, ck_im, y, yg, t, m)


def residual_add(h, y, name):
    l, d = h.shape
    tl = ROW_TILE

    def body(h_ref, y_ref, o_ref):
        o_ref[...] = h_ref[...] + y_ref[...]

    return _rowcall(body, name, l // tl, [_rows(tl, d)] * 2, _rows(tl, d), jax.ShapeDtypeStruct((l, d), F32))(h, y)


def s5_layer_bwd(dh_out, saved, p, wf, tag, sink):
    h, hn_p, a_p, sw, ck_re, ck_im, y, yg, t, m = saved
    l = h.shape[0]
    di = p["d_skip"].shape[0]
    dsk = p["d_skip"].reshape(1, di)
    dout_p = _seg_perm(dh_out)
    dm = matmul(dout_p, wf["w_out"], "nt", tag + "_mm_dm")
    sink.mm("w_out", m, dout_p, tag + "_mm_gwout")
    dt, dyg_a, dz, db_glu = s5_gate_bwd(dm, y, t, p["b_glu"], a_p, tag + "_gate_bwd")
    dyg_b = matmul(dt, wf["w_glu"], "nt", tag + "_mm_dyg")
    sink.mm("w_glu", yg, dt, tag + "_mm_gwglu")
    dy = s5_act_bwd(y, dyg_a, dyg_b, tag + "_act_bwd")
    e_re, e_im = s5_ends(dy, sw[0], sw[1], sw[4], sw[5], True, tag + "_scanb_ends")
    c_re, c_im = s5_carry(e_re, e_im, sw[0], sw[1], l // S5_SEG, True, tag + "_carry_bwd")
    du, dwbr, dwbi, dwcr, dwci, dds, dlr, dli, _, _ = s5_scan_bwd(
        a_p, dy, *sw, dsk, ck_re, ck_im, c_re, c_im, tag + "_scanb")
    da = jnp.concatenate([du, dz], axis=1)
    dhn_p = matmul(da, wf["w_in"], "nt", tag + "_mm_dhn")
    sink.mm("w_in", hn_p, da, tag + "_mm_gwin")
    zero = sink.send()
    dh, dng = rms_bwd(h, p["norm_g"] + zero, _seg_unperm(dhn_p), dh_out, tag + "_rms_bwd")
    ex = lambda m_, r, c: _blockdiag_extract(m_, r, c).reshape(S5_GROUPS, r, c).transpose(0, 2, 1)
    dbb_re, dbb_im = ex(dwbr, S5_GROUP, S5_STATE), ex(dwbi, S5_GROUP, S5_STATE)
    g_c_re, g_c_im = ex(dwcr, S5_STATE, S5_GROUP), ex(dwci, S5_STATE, S5_GROUP)
    dl_re = lane_sum8(dlr).reshape(S5_GROUPS, S5_STATE, 1)
    dl_im = lane_sum8(dli).reshape(S5_GROUPS, S5_STATE, 1)
    gar, gai, gls, gbr, gbi = s5_params_bwd(p["a_re"], p["a_im"], p["log_step"], p["b_re"], p["b_im"],
                                            dl_re, dl_im, dbb_re, dbb_im)
    grads = {"norm_g": dng.reshape(-1), "a_re": gar.reshape(S5_GROUPS, S5_STATE),
             "a_im": gai.reshape(S5_GROUPS, S5_STATE), "log_step": gls.reshape(-1), "b_re": gbr, "b_im": gbi,
             "c_re": g_c_re, "c_im": g_c_im, "d_skip": dds.reshape(-1), "b_glu": db_glu.reshape(-1)}
    return dh, grads


def lane_sum8(t):
    sb, seg, ns = t.shape

    def body(t_ref, o_ref):
        o_ref[...] = jnp.sum(t_ref[...], axis=1, keepdims=True)

    return pl.pallas_call(body, name="s5_seg_sum", out_shape=jax.ShapeDtypeStruct((sb, 1, ns), F32))(t)


MLA_DI = MLA_HEADS * 128
MLA_CQ0 = MLA_DI
MLA_CKV0 = MLA_CQ0 + MLA_Q_RANK
MLA_KR0 = MLA_CKV0 + MLA_KV_RANK
MLA_AW = MLA_KR0 + LANES


def _rot_half(x):
    w = x.shape[-1]
    lane = lax.broadcasted_iota(jnp.int32, x.shape, x.ndim - 1)
    return jnp.where(lane % MLA_ROPE < MLA_ROPE // 2, pltpu.roll(x, w - MLA_ROPE // 2, x.ndim - 1),
                     pltpu.roll(x, MLA_ROPE // 2, x.ndim - 1))


def rope_tables(pos):
    l = pos.shape[0]
    tl = ROW_TILE
    j = np.arange(LANES) % MLA_ROPE % (MLA_ROPE // 2)
    inv_freq = (ROPE_THETA ** (-(2.0 * j) / MLA_ROPE)).astype(np.float32).reshape(1, LANES)
    sign = np.where(np.arange(LANES) % MLA_ROPE < MLA_ROPE // 2, -1.0, 1.0).astype(np.float32).reshape(1, LANES)

    def body(p_ref, f_ref, s_ref, cos_ref, sin_ref):
        ang = p_ref[...].astype(F32) * f_ref[...]
        cos_ref[...] = jnp.cos(ang)
        sin_ref[...] = jnp.sin(ang) * s_ref[...]

    st = jax.ShapeDtypeStruct((l, LANES), F32)
    return _rowcall(body, "rope_tables", l // tl, [_rows(tl, 1), _full((1, LANES)), _full((1, LANES))],
                    [_rows(tl, LANES)] * 2, [st, st])(pos, jnp.asarray(inv_freq), jnp.asarray(sign))


def _rope(x, cos, sins):
    return x * cos + _rot_half(x) * sins


def _rope_t(dy, cos, sins):
    return dy * cos - sins * _rot_half(dy)


def _rmsn(x):
    r = lax.rsqrt(jnp.mean(x * x, axis=-1, keepdims=True) + NORM_EPS)
    return x * r, r


def mla_pre(a, q_g, kv_g, cos, sins, name):
    l = a.shape[0]
    tl = ROW_TILE

    def body(a_ref, qg_ref, kg_ref, cos_ref, sin_ref, cq_ref, ckv_ref, krs_ref):
        xq, _ = _rmsn(a_ref[:, MLA_CQ0:MLA_CKV0])
        cq_ref[...] = (xq * qg_ref[...]).astype(BF16)
        xk, _ = _rmsn(a_ref[:, MLA_CKV0:MLA_KR0])
        ckv_ref[...] = (xk * kg_ref[...]).astype(BF16)
        kr = a_ref[:, MLA_KR0:MLA_AW]
        kr2 = kr + pltpu.roll(kr, MLA_ROPE, 1)
        kr2 = _rope(kr2, cos_ref[...], sin_ref[...])
        lane = lax.broadcasted_iota(jnp.int32, kr2.shape, 1)
        krs_ref[0] = jnp.where(lane < MLA_ROPE, kr2, 0.0).astype(BF16)
        krs_ref[1] = jnp.where(lane >= MLA_ROPE, kr2, 0.0).astype(BF16)

    st = jax.ShapeDtypeStruct
    return _rowcall(body, name, l // tl,
                    [_rows(tl, MLA_AW), _full((1, MLA_Q_RANK)), _full((1, MLA_KV_RANK)), _rows(tl, LANES), _rows(tl, LANES)],
                    [_rows(tl, MLA_Q_RANK), _rows(tl, MLA_KV_RANK), pl.BlockSpec((2, tl, LANES), lambda i: (0, i, 0))],
                    [st((l, MLA_Q_RANK), BF16), st((l, MLA_KV_RANK), BF16), st((2, l, LANES), BF16)])(
        a, q_g.reshape(1, -1), kv_g.reshape(1, -1), cos, sins)


def mla_rope_q(qr, cos, sins, name):
    l, w = qr.shape
    tl = ROW_TILE

    def body(q_ref, cos_ref, sin_ref, o_ref):
        c, s = cos_ref[...], sin_ref[...]
        for p in range(w // LANES):
            sl = slice(p * LANES, (p + 1) * LANES)
            o_ref[:, sl] = _rope(q_ref[:, sl], c, s).astype(BF16)

    return _rowcall(body, name, l // tl, [_rows(tl, w), _rows(tl, LANES), _rows(tl, LANES)], _rows(tl, w),
                    jax.ShapeDtypeStruct((l, w), BF16))(qr, cos, sins)


ATT_OUT = 512
ATT_IN = 256
ATT_R = ATT_OUT // ATT_IN


def _scores(qn, qr, kn, kr, mask_off, transposed):
    q2 = jnp.concatenate([qn, qr], axis=1)
    k2 = jnp.concatenate([kn, kr], axis=1)
    s = (_dot(k2, q2, DN_NT) if transposed else _dot(q2, k2, DN_NT)) * MLA_SCALE
    if mask_off is None:
        return s
    r = lax.broadcasted_iota(jnp.int32, s.shape, 0)
    c = lax.broadcasted_iota(jnp.int32, s.shape, 1)
    return jnp.where((r <= c + mask_off) if transposed else (c + mask_off <= r), s, NEG_INF)


def _fold(x):
    return x[:, :LANES], x[:, LANES:]


def flash_fwd(qn, qr, kv, krs, name):
    l = qn.shape[0]
    nq = l // ATT_OUT

    def body(qn_ref, qr_ref, kv_ref, kr_ref, o_ref, lse_ref, s_buf):
        qi = pl.program_id(1)
        q_r = qr_ref[...]
        q_n = [qn_ref[:, hh * LANES:(hh + 1) * LANES] for hh in range(2)]

        def block_scores(j, mx, mask_off):
            sl = pl.ds(pl.multiple_of(j * ATT_IN, ATT_IN), ATT_IN)
            out = []
            for hh in range(2):
                s = _scores(q_n[hh], q_r, kv_ref[sl, 2 * hh * LANES:(2 * hh + 1) * LANES], kr_ref[hh, sl, :],
                            mask_off, False)
                s_buf[hh, j] = s
                lo, hi = _fold(s)
                out.append(jnp.maximum(mx[hh], jnp.maximum(lo, hi)))
            return tuple(out)

        ninf = jnp.full((ATT_OUT, LANES), NEG_INF, F32)
        mx = lax.fori_loop(0, ATT_R * qi, lambda j, c: block_scores(j, c, None), (ninf, ninf))
        for d in range(ATT_R):
            mx = block_scores(ATT_R * qi + d, mx, d * ATT_IN)
        m = [jnp.max(mx[hh], axis=-1, keepdims=True) for hh in range(2)]

        def block_pv(j, carry):
            sl = pl.ds(pl.multiple_of(j * ATT_IN, ATT_IN), ATT_IN)
            out = []
            for hh in range(2):
                ls, acc = carry[hh]
                p = jnp.exp(s_buf[hh, j] - m[hh])
                lo, hi = _fold(p)
                out.append((ls + (lo + hi),
                            acc + _dot(p, kv_ref[sl, (2 * hh + 1) * LANES:(2 * hh + 2) * LANES], DN_NN)))
            return tuple(out)

        z = jnp.zeros((ATT_OUT, LANES), F32)
        res = lax.fori_loop(0, ATT_R * (qi + 1), block_pv, ((z, z), (z, z)))
        for hh in range(2):
            lsum = jnp.sum(res[hh][0], axis=-1, keepdims=True)
            o_ref[:, hh * LANES:(hh + 1) * LANES] = res[hh][1] / lsum
            lse_ref[hh] = m[hh] + jnp.log(lsum)

    st = jax.ShapeDtypeStruct
    return pl.pallas_call(
        body, name=name, grid=(MLA_HEADS // 2, nq),
        in_specs=[pl.BlockSpec((ATT_OUT, 2 * LANES), lambda p, i: (i, p)),
                  pl.BlockSpec((ATT_OUT, LANES), lambda p, i: (i, p)),
                  pl.BlockSpec((l, 4 * LANES), lambda p, i: (0, p)),
                  pl.BlockSpec((2, l, LANES), lambda p, i: (0, 0, 0))],
        out_specs=[pl.BlockSpec((ATT_OUT, 2 * LANES), lambda p, i: (i, p)),
                   pl.BlockSpec((2, ATT_OUT, 1), lambda p, i: (p, i, 0))],
        out_shape=[st((l, MLA_DI), F32), st((MLA_HEADS, l, 1), F32)],
        scratch_shapes=[pltpu.VMEM((2, l // ATT_IN, ATT_OUT, ATT_IN), F32)],
        compiler_params=_cparams(("parallel", "arbitrary")))(qn, qr, kv, krs)


def flash_dkv(qn, qr, kv, krs, do, lse_row, delta_row, name):
    l = qn.shape[0]
    nk = l // ATT_OUT
    nq = l // ATT_IN

    def body(qn_ref, qr_ref, do_ref, lse_ref, dl_ref, kv_ref, kr_ref, dkv_ref, dkr_ref):
        kj = pl.program_id(1)
        lane = lax.broadcasted_iota(jnp.int32, (ATT_OUT, LANES), 1)
        kn = [kv_ref[:, 2 * hh * LANES:(2 * hh + 1) * LANES] for hh in range(2)]
        v = [kv_ref[:, (2 * hh + 1) * LANES:(2 * hh + 2) * LANES] for hh in range(2)]

        def block(i, carry, mask_off):
            sl = pl.ds(pl.multiple_of(i * ATT_IN, ATT_IN), ATT_IN)
            q_r = qr_ref[sl, :]
            out = []
            for hh in range(2):
                dk2, dv = carry[hh]
                hs = slice(hh * LANES, (hh + 1) * LANES)
                q_n, d_o = qn_ref[sl, hs], do_ref[sl, hs]
                s = _scores(q_n, q_r, kn[hh], kr_ref[hh], mask_off, True)
                pt = jnp.exp(s - lse_ref[hh, i])
                dv = dv + _dot(pt, d_o, DN_NN)
                dpt = _dot(v[hh], d_o, DN_NT)
                dst = (pt * (dpt - dl_ref[hh, i]) * MLA_SCALE).astype(BF16)
                out.append((dk2 + _dot(dst, jnp.concatenate([q_n, q_r], axis=1), DN_NN), dv))
            return tuple(out)

        z = jnp.zeros((ATT_OUT, LANES), F32)
        z2 = jnp.zeros((ATT_OUT, 2 * LANES), F32)
        res = ((z2, z), (z2, z))
        for d in range(ATT_R):
            res = block(ATT_R * kj + d, res, d * ATT_IN)
        res = lax.fori_loop(ATT_R * (kj + 1), nq, lambda i, c: block(i, c, None), res)
        for hh in range(2):
            dkv_ref[:, 2 * hh * LANES:(2 * hh + 1) * LANES] = res[hh][0][:, :LANES].astype(BF16)
            dkv_ref[:, (2 * hh + 1) * LANES:(2 * hh + 2) * LANES] = res[hh][1].astype(BF16)
        dkr_ref[0] = jnp.where(lane < MLA_ROPE, res[0][0][:, LANES:], res[1][0][:, LANES:])

    st = jax.ShapeDtypeStruct
    return pl.pallas_call(
        body, name=name, grid=(MLA_HEADS // 2, nk),
        in_specs=[pl.BlockSpec((l, 2 * LANES), lambda p, j: (0, p)),
                  pl.BlockSpec((l, LANES), lambda p, j: (0, p)),
                  pl.BlockSpec((l, 2 * LANES), lambda p, j: (0, p)),
                  pl.BlockSpec((2, nq, 1, ATT_IN), lambda p, j: (p, 0, 0, 0)),
                  pl.BlockSpec((2, nq, 1, ATT_IN), lambda p, j: (p, 0, 0, 0)),
                  pl.BlockSpec((ATT_OUT, 4 * LANES), lambda p, j: (j, p)),
                  pl.BlockSpec((2, ATT_OUT, LANES), lambda p, j: (0, j, 0))],
        out_specs=[pl.BlockSpec((ATT_OUT, 4 * LANES), lambda p, j: (j, p)),
                   pl.BlockSpec((1, ATT_OUT, LANES), lambda p, j: (p, j, 0))],
        out_shape=[st((l, 2 * MLA_DI), BF16), st((MLA_HEADS // 2, l, LANES), F32)],
        compiler_params=_cparams(("parallel", "arbitrary")))(qn, qr, do, lse_row, delta_row, kv, krs)


def flash_dq(qn, qr, kv, krs, do, lse, delta, cos, sins, name):
    l = qn.shape[0]
    nq = l // ATT_OUT

    def body(qn_ref, qr_ref, do_ref, lse_ref, dl_ref, kv_ref, kr_ref, cos_ref, sin_ref, dqn_ref, dqr_ref):
        qi = pl.program_id(1)
        q_r = qr_ref[...]
        q_n = [qn_ref[:, hh * LANES:(hh + 1) * LANES] for hh in range(2)]
        d_o = [do_ref[:, hh * LANES:(hh + 1) * LANES] for hh in range(2)]
        lse_h = [lse_ref[hh] for hh in range(2)]
        dl_h = [dl_ref[hh] for hh in range(2)]

        def block(j, carry, mask_off):
            sl = pl.ds(pl.multiple_of(j * ATT_IN, ATT_IN), ATT_IN)
            dq2 = list(carry)
            for hh in range(2):
                kn = kv_ref[sl, 2 * hh * LANES:(2 * hh + 1) * LANES]
                v = kv_ref[sl, (2 * hh + 1) * LANES:(2 * hh + 2) * LANES]
                kr = kr_ref[hh, sl, :]
                s = _scores(q_n[hh], q_r, kn, kr, mask_off, False)
                pr = jnp.exp(s - lse_h[hh])
                dp = _dot(d_o[hh], v, DN_NT)
                ds = (pr * (dp - dl_h[hh]) * MLA_SCALE).astype(BF16)
                dq2[hh] = dq2[hh] + _dot(ds, jnp.concatenate([kn, kr], axis=1), DN_NN)
            return tuple(dq2)

        z2 = jnp.zeros((ATT_OUT, 2 * LANES), F32)
        res = lax.fori_loop(0, ATT_R * qi, lambda j, c: block(j, c, None), (z2, z2))
        for d in range(ATT_R):
            res = block(ATT_R * qi + d, res, d * ATT_IN)
        dqn_ref[:, 0:LANES] = res[0][:, :LANES].astype(BF16)
        dqn_ref[:, LANES:2 * LANES] = res[1][:, :LANES].astype(BF16)
        dqr = res[0][:, LANES:] + res[1][:, LANES:]
        dqr_ref[...] = _rope_t(dqr, cos_ref[...], sin_ref[...]).astype(BF16)

    st = jax.ShapeDtypeStruct
    return pl.pallas_call(
        body, name=name, grid=(MLA_HEADS // 2, nq),
        in_specs=[pl.BlockSpec((ATT_OUT, 2 * LANES), lambda p, i: (i, p)),
                  pl.BlockSpec((ATT_OUT, LANES), lambda p, i: (i, p)),
                  pl.BlockSpec((ATT_OUT, 2 * LANES), lambda p, i: (i, p)),
                  pl.BlockSpec((2, ATT_OUT, 1), lambda p, i: (p, i, 0)),
                  pl.BlockSpec((2, ATT_OUT, 1), lambda p, i: (p, i, 0)),
                  pl.BlockSpec((l, 4 * LANES), lambda p, i: (0, p)),
                  pl.BlockSpec((2, l, LANES), lambda p, i: (0, 0, 0)),
                  pl.BlockSpec((ATT_OUT, LANES), lambda p, i: (i, 0)),
                  pl.BlockSpec((ATT_OUT, LANES), lambda p, i: (i, 0))],
        out_specs=[pl.BlockSpec((ATT_OUT, 2 * LANES), lambda p, i: (i, p)),
                   pl.BlockSpec((ATT_OUT, LANES), lambda p, i: (i, p))],
        out_shape=[st((l, MLA_DI), BF16), st((l, MLA_HEADS * MLA_ROPE), BF16)],
        compiler_params=_cparams(("parallel", "arbitrary")))(qn, qr, do, lse, delta, kv, krs, cos, sins)


def mla_gate_fwd(o, a, name):
    l = o.shape[0]
    tl = ROW_TILE

    def body(o_ref, z_ref, m_ref):
        m_ref[...] = (o_ref[...] * _silu(z_ref[...])).astype(BF16)

    return _rowcall(body, name, l // tl, [_rows(tl, MLA_DI), _rows(tl, MLA_DI)], _rows(tl, MLA_DI),
                    jax.ShapeDtypeStruct((l, MLA_DI), BF16))(o, a)


def mla_gate_bwd(dm, o, a, name):
    l = o.shape[0]
    tl = ROW_TILE

    def body(dm_ref, o_ref, z_ref, do_ref, dz_ref, dl_ref):
        dmv, ov, z = dm_ref[...], o_ref[...], z_ref[...]
        sz, gz = _silu_both(z)
        d_o = dmv * sz
        do_ref[...] = d_o.astype(BF16)
        dz_ref[...] = (dmv * ov * gz).astype(BF16)
        pr = d_o * ov
        for h in range(MLA_HEADS):
            dl_ref[h] = jnp.sum(pr[:, h * LANES:(h + 1) * LANES], axis=1, keepdims=True)

    st = jax.ShapeDtypeStruct
    return _rowcall(body, name, l // tl, [_rows(tl, MLA_DI)] * 3,
                    [_rows(tl, MLA_DI), _rows(tl, MLA_DI), pl.BlockSpec((MLA_HEADS, tl, 1), lambda i: (0, i, 0))],
                    [st((l, MLA_DI), BF16), st((l, MLA_DI), BF16), st((MLA_HEADS, l, 1), F32)])(dm, o, a)


def mla_post(a, dcqn, dckvn, dkr_pairs, dz, q_g, kv_g, cos, sins, name):
    l = a.shape[0]
    tl = ROW_TILE
    npair = MLA_HEADS // 2

    def norm_bwd(x, g, dy):
        xhat, r = _rmsn(x)
        dxh = dy * g
        return r * (dxh - xhat * jnp.mean(dxh * xhat, axis=-1, keepdims=True)), jnp.sum(dy * xhat, axis=0, keepdims=True)

    def body(a_ref, dq_ref, dk_ref, dkr_ref, dz_ref, qg_ref, kg_ref, cos_ref, sin_ref, da_ref, dqg_ref, dkg_ref):
        i = pl.program_id(0)
        da_ref[:, 0:MLA_DI] = dz_ref[...]
        dcq, dqg = norm_bwd(a_ref[:, MLA_CQ0:MLA_CKV0], qg_ref[...], dq_ref[...])
        da_ref[:, MLA_CQ0:MLA_CKV0] = dcq.astype(BF16)
        dckv, dkg = norm_bwd(a_ref[:, MLA_CKV0:MLA_KR0], kg_ref[...], dk_ref[...])
        da_ref[:, MLA_CKV0:MLA_KR0] = dckv.astype(BF16)
        dk2 = dkr_ref[0]
        for p in range(1, npair):
            dk2 = dk2 + dkr_ref[p]
        dk2 = _rope_t(dk2, cos_ref[...], sin_ref[...])
        dk2 = dk2 + pltpu.roll(dk2, MLA_ROPE, 1)
        lane = lax.broadcasted_iota(jnp.int32, dk2.shape, 1)
        da_ref[:, MLA_KR0:MLA_AW] = jnp.where(lane < MLA_ROPE, dk2, 0.0).astype(BF16)
        _acc(dqg_ref, dqg, i)
        _acc(dkg_ref, dkg, i)

    st = jax.ShapeDtypeStruct
    return _rowcall(body, name, l // tl,
                    [_rows(tl, MLA_AW), _rows(tl, MLA_Q_RANK), _rows(tl, MLA_KV_RANK),
                     pl.BlockSpec((npair, tl, LANES), lambda i: (0, i, 0)), _rows(tl, MLA_DI),
                     _full((1, MLA_Q_RANK)), _full((1, MLA_KV_RANK)), _rows(tl, LANES), _rows(tl, LANES)],
                    [_rows(tl, MLA_AW), _full((1, MLA_Q_RANK)), _full((1, MLA_KV_RANK))],
                    [st((l, MLA_AW), BF16), st((1, MLA_Q_RANK), F32), st((1, MLA_KV_RANK), F32)])(
        a, dcqn, dckvn, dkr_pairs, dz, q_g.reshape(1, -1), kv_g.reshape(1, -1), cos, sins)


def _mla_w_in_perm(w):
    r = MLA_Q_RANK + MLA_KV_RANK + MLA_ROPE
    pad = jnp.zeros(w.shape[:-1] + (MLA_AW - MLA_KR0 - MLA_ROPE,), w.dtype)
    return jnp.concatenate([w[..., r:], w[..., :r], pad], axis=-1)


def _mla_w_in_unperm(g):
    r = MLA_Q_RANK + MLA_KV_RANK + MLA_ROPE
    return jnp.concatenate([g[..., MLA_DI:MLA_DI + r], g[..., :MLA_DI]], axis=-1)


def _mla_w_uq_split(w):
    k = w.shape[0]
    w3 = w.reshape(k, MLA_HEADS, MLA_NOPE + MLA_ROPE)
    return w3[:, :, :MLA_NOPE].reshape(k, MLA_HEADS * MLA_NOPE), w3[:, :, MLA_NOPE:].reshape(k, MLA_HEADS * MLA_ROPE)


def _mla_w_uq_merge(gn, gr):
    k = gn.shape[0]
    return jnp.concatenate([gn.reshape(k, MLA_HEADS, MLA_NOPE), gr.reshape(k, MLA_HEADS, MLA_ROPE)], axis=2).reshape(k, -1)


def mla_layer_fwd(h, p, wf, cos, sins, tag):
    hn = rms_fwd(h, p["norm_g"], tag + "_rms")
    w_in = _mla_w_in_perm(wf["w_in"])
    w_uq_n, w_uq_r = _mla_w_uq_split(wf["w_uq"])
    a = matmul(hn, w_in, "nn", tag + "_mm_in")
    cqn, ckvn, krs = mla_pre(a, p["q_norm_g"], p["kv_norm_g"], cos, sins, tag + "_pre")
    qn = matmul(cqn, w_uq_n, "nn", tag + "_mm_qn", out_dtype=BF16)
    qr_raw = matmul(cqn, w_uq_r, "nn", tag + "_mm_qr")
    qr = mla_rope_q(qr_raw, cos, sins, tag + "_rope_q")
    kv = matmul(ckvn, wf["w_ukv"], "nn", tag + "_mm_kv", out_dtype=BF16)
    o, lse = flash_fwd(qn, qr, kv, krs, tag + "_flash")
    m = mla_gate_fwd(o, a, tag + "_gate")
    h_out = matmul(m, wf["w_out"], "nn", tag + "_mm_out", add=h)
    return h_out, (h, hn, a, cqn, ckvn, krs, qn, qr, kv, o, lse, m, w_in, w_uq_n, w_uq_r)


def mla_layer_bwd(dh_out, saved, p, wf, cos, sins, tag, sink):
    h, hn, a, cqn, ckvn, krs, qn, qr, kv, o, lse, m, w_in, w_uq_n, w_uq_r = saved
    l = h.shape[0]
    dm = matmul(dh_out, wf["w_out"], "nt", tag + "_mm_dm")
    sink.mm("w_out", m, dh_out, tag + "_mm_gwout")
    do, dz, delta = mla_gate_bwd(dm, o, a, tag + "_gate_bwd")
    lse_row = lse.reshape(MLA_HEADS, l // ATT_IN, 1, ATT_IN)
    delta_row = delta.reshape(MLA_HEADS, l // ATT_IN, 1, ATT_IN)
    dkv, dkr_pairs = flash_dkv(qn, qr, kv, krs, do, lse_row, delta_row, tag + "_flash_dkv")
    dqn, dqr = flash_dq(qn, qr, kv, krs, do, lse, delta, cos, sins, tag + "_flash_dq")
    dcqn = matmul(dqn, w_uq_n, "nt", tag + "_mm_dcq_n")
    dcqn = matmul(dqr, w_uq_r, "nt", tag + "_mm_dcq_r", add=dcqn)
    g_uq_n = matmul(cqn, dqn, "tn", tag + "_mm_guq_n")
    g_uq_r = matmul(cqn, dqr, "tn", tag + "_mm_guq_r")
    dckvn = matmul(dkv, wf["w_ukv"], "nt", tag + "_mm_dckv")
    sink.mm("w_ukv", ckvn, dkv, tag + "_mm_gukv")
    da, dqg, dkg = mla_post(a, dcqn, dckvn, dkr_pairs, dz, p["q_norm_g"], p["kv_norm_g"], cos, sins, tag + "_post")
    dhn = matmul(da, w_in, "nt", tag + "_mm_dhn")
    g_w_in = matmul(hn, da, "tn", tag + "_mm_gwin")
    sink.put("w_uq", _mla_w_uq_merge(g_uq_n, g_uq_r))
    sink.put("w_in", _mla_w_in_unperm(g_w_in))
    zero = sink.send()
    dh, dng = rms_bwd(h, p["norm_g"] + zero, dhn, dh_out, tag + "_rms_bwd")
    grads = {"norm_g": dng.reshape(-1), "q_norm_g": dqg.reshape(-1), "kv_norm_g": dkg.reshape(-1)}
    return dh, grads


ANY = pl.BlockSpec(memory_space=pl.ANY)


def _me():
    return lax.axis_index("x"), lax.axis_index("y"), lax.axis_index("c")


def _chip():
    return 2 * lax.axis_index("x") + lax.axis_index("y")


def _other_chips(x, y):
    return [(1 - x, y), (x, 1 - y), (1 - x, 1 - y)]


def _rcopy(src, dst, ssem, rsem, dev):
    return pltpu.make_async_remote_copy(src_ref=src, dst_ref=dst, send_sem=ssem, recv_sem=rsem,
                                        device_id=dev, device_id_type=MESH)


def _half(ref, c, hf):
    return ref.at[pl.ds(c * hf, hf), :]


def weights_allgather(wb):
    nr, w = wb.shape
    hf = nr // 2

    def body(w_ref, o_ref, ssem, rsem):
        x, y, c = _me()
        k = 2 * x + y
        chips = _other_chips(x, y)
        first = [_rcopy(_half(w_ref, c, hf), _half(o_ref.at[k], c, hf), ssem.at[j], rsem.at[j], (cx, cy, c))
                 for j, (cx, cy) in enumerate(chips)]
        for cp in first:
            cp.start()
        passed = []
        for j, (cx, cy) in enumerate(chips):
            region = _half(o_ref.at[2 * cx + cy], c, hf)
            _rcopy(region, region, ssem.at[j], rsem.at[j], (cx, cy, c)).wait_recv()
            fwd = _rcopy(region, region, ssem.at[3 + j], rsem.at[3 + j], (x, y, 1 - c))
            fwd.start()
            passed.append(fwd)
        for j, (cx, cy) in enumerate(chips):
            region = _half(o_ref.at[2 * cx + cy], 1 - c, hf)
            _rcopy(region, region, ssem.at[3 + j], rsem.at[3 + j], (x, y, 1 - c)).wait_recv()
        for cp in first + passed:
            cp.wait_send()

    out = pl.pallas_call(
        body, name="weights_allgather", in_specs=[ANY], out_specs=ANY,
        out_shape=jax.ShapeDtypeStruct((N_CHIPS, nr, w), wb.dtype),
        scratch_shapes=[pltpu.SemaphoreType.DMA((6,)), pltpu.SemaphoreType.DMA((6,))],
    )(wb)
    return lax.dynamic_update_slice(out, wb[None], (_chip(), 0, 0))


HBM = pl.BlockSpec(memory_space=pltpu.HBM)
SEM = pl.BlockSpec(memory_space=pltpu.SEMAPHORE)
SPLIT_EFFECT = pltpu.SideEffectType.DATAFLOW_SIDE_EFFECTING


def gather_start(wb, after, name):
    nr, w = wb.shape
    hf = nr // 2

    def body(w_ref, land_ref, after_ref, ssem, rsem, w_thru, land_thru, token):
        x, y, c = _me()
        k = 2 * x + y
        for j, (cx, cy) in enumerate(_other_chips(x, y)):
            _rcopy(_half(w_ref, c, hf), _half(land_ref.at[k], c, hf), ssem.at[j], rsem.at[j], (cx, cy, c)).start()
        token[...] = jnp.zeros_like(token)

    land = lax.empty((N_CHIPS, nr, w), wb.dtype)
    return pl.pallas_call(
        body, name=name,
        out_shape=(pltpu.SemaphoreType.DMA((3,)), pltpu.SemaphoreType.DMA((3,)), pltpu.HBM(wb.shape, wb.dtype),
                   pltpu.HBM(land.shape, land.dtype), jax.ShapeDtypeStruct((8, LANES), F32)),
        in_specs=(HBM, HBM, ANY), out_specs=(SEM, SEM, HBM, HBM, pl.BlockSpec(memory_space=pltpu.VMEM)),
        input_output_aliases={0: 2, 1: 3},
        compiler_params=pltpu.CompilerParams(has_side_effects=SPLIT_EFFECT))(
        pltpu.with_memory_space_constraint(wb, pltpu.HBM), pltpu.with_memory_space_constraint(land, pltpu.HBM), after)


def gather_wait(ssem, rsem, w_thru, land_thru, after, name):
    nr, w = w_thru.shape
    hf = nr // 2

    def body(w_ref, land_ref, ssem_ref, rsem_ref, after_ref, w_dead, got_ref):
        x, y, c = _me()
        for j, (cx, cy) in enumerate(_other_chips(x, y)):
            cp = _rcopy(_half(w_ref, c, hf), _half(land_ref.at[2 * cx + cy], c, hf), ssem_ref.at[j], rsem_ref.at[j],
                        (cx, cy, c))
            cp.wait_send()
            cp.wait_recv()

    return pl.pallas_call(
        body, name=name, out_shape=(pltpu.HBM(w_thru.shape, w_thru.dtype), pltpu.HBM(land_thru.shape, land_thru.dtype)),
        in_specs=(HBM, HBM, SEM, SEM, ANY), out_specs=(HBM, HBM), input_output_aliases={0: 0, 1: 1},
        compiler_params=pltpu.CompilerParams(has_side_effects=SPLIT_EFFECT))(w_thru, land_thru, ssem, rsem, after)[1]


def gather_handover(land, wb, name):
    _, nr, w = land.shape
    hf = nr // 2

    def body(l_ref, o_ref, ssem, rsem):
        x, y, c = _me()
        chips = _other_chips(x, y)
        sends = []
        for j, (cx, cy) in enumerate(chips):
            region = _half(o_ref.at[2 * cx + cy], c, hf)
            sends.append(_rcopy(region, region, ssem.at[j], rsem.at[j], (x, y, 1 - c)))
            sends[-1].start()
        for j, (cx, cy) in enumerate(chips):
            region = _half(o_ref.at[2 * cx + cy], 1 - c, hf)
            _rcopy(region, region, ssem.at[j], rsem.at[j], (x, y, 1 - c)).wait_recv()
        for cp in sends:
            cp.wait_send()

    out = pl.pallas_call(
        body, name=name, in_specs=[ANY], out_specs=ANY, input_output_aliases={0: 0},
        out_shape=jax.ShapeDtypeStruct(land.shape, land.dtype),
        scratch_shapes=[pltpu.SemaphoreType.DMA((3,)), pltpu.SemaphoreType.DMA((3,))])(land)
    return lax.dynamic_update_slice(out, wb[None], (_chip(), 0, 0))


def reduce_start(t, after, name):
    def body(t_ref, land_ref, after_ref, ssem, rsem, t_thru, land_thru, token):
        x, y, c = _me()
        k = 2 * x + y
        for j, (cx, cy) in enumerate(_other_chips(x, y)):
            _rcopy(t_ref.at[2 * cx + cy], land_ref.at[k], ssem.at[j], rsem.at[j], (cx, cy, c)).start()
        token[...] = jnp.zeros_like(token)

    land = lax.empty(t.shape, t.dtype)
    return pl.pallas_call(
        body, name=name,
        out_shape=(pltpu.SemaphoreType.DMA((3,)), pltpu.SemaphoreType.DMA((3,)), pltpu.HBM(t.shape, t.dtype),
                   pltpu.HBM(t.shape, t.dtype), jax.ShapeDtypeStruct((8, LANES), F32)),
        in_specs=(HBM, HBM, ANY), out_specs=(SEM, SEM, HBM, HBM, pl.BlockSpec(memory_space=pltpu.VMEM)),
        input_output_aliases={0: 2, 1: 3},
        compiler_params=pltpu.CompilerParams(has_side_effects=SPLIT_EFFECT))(
        pltpu.with_memory_space_constraint(t, pltpu.HBM), pltpu.with_memory_space_constraint(land, pltpu.HBM), after)


def reduce_wait(ssem, rsem, t_thru, land_thru, after, name):
    def body(t_ref, land_ref, ssem_ref, rsem_ref, after_ref, t_out, got_ref):
        x, y, c = _me()
        k = 2 * x + y
        for j, (cx, cy) in enumerate(_other_chips(x, y)):
            cp = _rcopy(t_ref.at[k], land_ref.at[2 * cx + cy], ssem_ref.at[j], rsem_ref.at[j], (cx, cy, c))
            cp.wait_send()
            cp.wait_recv()

    return pl.pallas_call(
        body, name=name, out_shape=(pltpu.HBM(t_thru.shape, t_thru.dtype), pltpu.HBM(land_thru.shape, land_thru.dtype)),
        in_specs=(HBM, HBM, SEM, SEM, ANY), out_specs=(HBM, HBM), input_output_aliases={0: 0, 1: 1},
        compiler_params=pltpu.CompilerParams(has_side_effects=SPLIT_EFFECT))(t_thru, land_thru, ssem, rsem, after)


def grads_to_sibling(ps, name="grads_to_sibling"):
    n = len(ps)

    def body(*refs):
        p_refs, o_refs, ssem, rsem = refs[:n], refs[n:2 * n], refs[2 * n], refs[2 * n + 1]
        x, y, c = _me()
        cps = []
        for a in range(n):
            hf = ps[a].shape[1] // 2
            cps.append(_rcopy(p_refs[a].at[:, pl.ds((1 - c) * hf, hf), :], o_refs[a], ssem.at[a], rsem.at[a],
                              (x, y, 1 - c)))
        for cp in cps:
            cp.start()
        for cp in cps:
            cp.wait()

    return pl.pallas_call(
        body, name=name, in_specs=[ANY] * n, out_specs=[ANY] * n,
        out_shape=[jax.ShapeDtypeStruct((N_CHIPS, p.shape[1] // 2, p.shape[2]), p.dtype) for p in ps],
        scratch_shapes=[pltpu.SemaphoreType.DMA((n,)), pltpu.SemaphoreType.DMA((n,))])(*ps)


def pair_sum(p, ra, out_dtype, name):
    _, nr, w = p.shape
    hf = nr // 2
    tr = _pick_rows(hf)
    nb = hf // tr

    def body(c_ref, p_ref, r_ref, o_ref):
        o_ref[...] = (p_ref[...] + r_ref[...]).astype(out_dtype)

    c = lax.axis_index("c").astype(jnp.int32).reshape(1)
    return pl.pallas_call(
        body, name=name,
        grid_spec=pltpu.PrefetchScalarGridSpec(
            num_scalar_prefetch=1, grid=(N_CHIPS, nb),
            in_specs=[pl.BlockSpec((1, tr, w), lambda k, i, c_ref: (k, c_ref[0] * nb + i, 0)),
                      pl.BlockSpec((1, tr, w), lambda k, i, c_ref: (k, i, 0))],
            out_specs=pl.BlockSpec((1, tr, w), lambda k, i, c_ref: (k, i, 0))),
        out_shape=jax.ShapeDtypeStruct((N_CHIPS, hf, w), out_dtype),
        compiler_params=_cparams(("parallel", "parallel")))(c, p, ra)


def grads_across_chips(ts):
    n = len(ts)

    def body(*refs):
        t_refs, o_refs, ssem, rsem = refs[:n], refs[n:2 * n], refs[2 * n], refs[2 * n + 1]
        x, y, c = _me()
        k = 2 * x + y
        chips = _other_chips(x, y)
        sends = [_rcopy(t_refs[a].at[2 * cx + cy], o_refs[a].at[k], ssem.at[3 * a + j], rsem.at[3 * a + j], (cx, cy, c))
                 for a in range(n) for j, (cx, cy) in enumerate(chips)]
        for cp in sends:
            cp.start()
        for a in range(n):
            for j, (cx, cy) in enumerate(chips):
                _rcopy(t_refs[a].at[k], o_refs[a].at[2 * cx + cy], ssem.at[3 * a + j], rsem.at[3 * a + j],
                       (cx, cy, c)).wait_recv()
        for cp in sends:
            cp.wait_send()

    return pl.pallas_call(
        body, name="grads_across_chips", in_specs=[ANY] * n, out_specs=[ANY] * n,
        out_shape=[jax.ShapeDtypeStruct(t.shape, t.dtype) for t in ts],
        scratch_shapes=[pltpu.SemaphoreType.DMA((3 * n,)), pltpu.SemaphoreType.DMA((3 * n,))])(*ts)


def chip_sum(t, rb, name):
    _, hf, w = rb.shape
    tr = _pick_rows(hf)
    nb = hf // tr

    def body(kc_ref, t_ref, r_ref, o_ref):
        k = kc_ref[0]
        acc = jnp.where(k == 0, t_ref[0], r_ref[0]).astype(F32)
        for j in range(1, N_CHIPS):
            acc = acc + jnp.where(k == j, t_ref[0], r_ref[j]).astype(F32)
        o_ref[...] = acc

    kc = jnp.stack([_chip(), lax.axis_index("c")]).astype(jnp.int32)
    return pl.pallas_call(
        body, name=name,
        grid_spec=pltpu.PrefetchScalarGridSpec(
            num_scalar_prefetch=1, grid=(nb,),
            in_specs=[pl.BlockSpec((1, tr, w), lambda i, kc_ref: (kc_ref[0], i, 0)),
                      pl.BlockSpec((N_CHIPS, tr, w), lambda i, kc_ref: (0, i, 0))],
            out_specs=pl.BlockSpec((tr, w), lambda i, kc_ref: (kc_ref[1] * nb + i, 0))),
        out_shape=jax.ShapeDtypeStruct((2 * hf, w), F32), compiler_params=_cparams(("parallel",)))(kc, t, rb)


def reduced_to_sibling(gs):
    n = len(gs)

    def body(*refs):
        o_refs, ssem, rsem = refs[n:2 * n], refs[2 * n], refs[2 * n + 1]
        x, y, c = _me()
        cps = []
        for a in range(n):
            hf = gs[a].shape[0] // 2
            cps.append(_rcopy(_half(o_refs[a], c, hf), _half(o_refs[a], c, hf), ssem.at[a], rsem.at[a], (x, y, 1 - c)))
        for cp in cps:
            cp.start()
        for a in range(n):
            hf = gs[a].shape[0] // 2
            _rcopy(_half(o_refs[a], c, hf), _half(o_refs[a], 1 - c, hf), ssem.at[a], rsem.at[a],
                   (x, y, 1 - c)).wait_recv()
        for cp in cps:
            cp.wait_send()

    return pl.pallas_call(
        body, name="reduced_to_sibling", in_specs=[ANY] * n, out_specs=[ANY] * n,
        input_output_aliases={a: a for a in range(n)},
        out_shape=[jax.ShapeDtypeStruct(g.shape, g.dtype) for g in gs],
        scratch_shapes=[pltpu.SemaphoreType.DMA((n,)), pltpu.SemaphoreType.DMA((n,))])(*gs)


def small_allgather(g, row0, nrs):
    w = g.shape[1]

    def body(g_ref, o_ref, ssem, rsem):
        x, y, c = _me()
        k = 2 * x + y
        chips = _other_chips(x, y)
        src = g_ref.at[pl.ds(row0, nrs), :]
        sends = [_rcopy(src, o_ref.at[k], ssem.at[j], rsem.at[j], (cx, cy, c)) for j, (cx, cy) in enumerate(chips)]
        for cp in sends:
            cp.start()
        for j, (cx, cy) in enumerate(chips):
            _rcopy(src, o_ref.at[2 * cx + cy], ssem.at[j], rsem.at[j], (cx, cy, c)).wait_recv()
        for cp in sends:
            cp.wait_send()

    out = pl.pallas_call(
        body, name="small_allgather", in_specs=[ANY], out_specs=ANY,
        out_shape=jax.ShapeDtypeStruct((N_CHIPS, nrs, w), g.dtype),
        scratch_shapes=[pltpu.SemaphoreType.DMA((3,)), pltpu.SemaphoreType.DMA((3,))])(g)
    return lax.dynamic_update_slice(out, g[row0:row0 + nrs][None], (_chip(), 0, 0))


def _adamw_step(w_ref, g_ref, m_ref, v_ref, d_ref, nm_ref, nv_ref):
    bc1 = 1.0 - ADAM_B1 ** ADAM_STEP
    bc2 = 1.0 - ADAM_B2 ** ADAM_STEP
    gv = g_ref[...]
    nm = ADAM_B1 * m_ref[...] + (1.0 - ADAM_B1) * gv
    nv = ADAM_B2 * v_ref[...] + (1.0 - ADAM_B2) * (gv * gv)
    nm_ref[...] = nm
    nv_ref[...] = nv
    d_ref[...] = -ADAM_LR * ((nm / bc1) / (jnp.sqrt(nv / bc2) + ADAM_EPS) + ADAM_WD * w_ref[...])


def adamw_packed(w, g_buf, r0, m, v, name):
    r, c = w.shape
    tr = _tile_rows(r, r0)

    def body(w_ref, g_ref, m_ref, v_ref, go_ref, d_ref, nm_ref, nv_ref):
        go_ref[...] = g_ref[...]
        _adamw_step(w_ref, g_ref, m_ref, v_ref, d_ref, nm_ref, nv_ref)

    own = pl.BlockSpec((tr, CHUNK_W), lambda i, j: (i, j))
    packed = pl.BlockSpec((tr, CHUNK_W), lambda i, j: ((r0 + j * r) // tr + i, 0))
    st = jax.ShapeDtypeStruct((r, c), F32)
    return pl.pallas_call(body, name=name, grid=(r // tr, c // CHUNK_W), in_specs=[own, packed, own, own],
                          out_specs=[own] * 4, out_shape=[st] * 4,
                          compiler_params=_cparams(("parallel", "parallel")))(w, g_buf, m, v)


def adamw(w, g, m, v, name):
    r, wd = w.shape
    tr = _pick_rows(r, cap=max(16, ADAMW_BLOCK_BYTES // (4 * wd)))
    body = functools.partial(_adamw_step)

    spec = pl.BlockSpec((tr, wd), lambda i: (i, 0))
    st = jax.ShapeDtypeStruct((r, wd), F32)
    return pl.pallas_call(body, name=name, grid=(r // tr,), in_specs=[spec] * 4, out_specs=[spec] * 3,
                          out_shape=[st, st, st], compiler_params=_cparams(("parallel",)))(w, g, m, v)


LAYER_KINDS = ("gmlp", "s5", "mla", "gmlp")
PARAMS = {
    "gmlp": ("norm_g", "w_in", "ln_g", "ln_b", "w_s", "b_s", "w_out"),
    "s5": ("norm_g", "w_in", "a_re", "a_im", "log_step", "b_re", "b_im", "c_re", "c_im", "d_skip", "w_glu", "b_glu", "w_out"),
    "mla": ("norm_g", "w_in", "q_norm_g", "w_uq", "kv_norm_g", "w_ukv", "w_out"),
}
COL_SHARDED = ("w_in", "w_uq", "w_ukv")
ROW_SHARDED = ("w_out", "w_glu")
WEIGHT_NAMES = [("l%d_" % i) + n for i, kind in enumerate(LAYER_KINDS) for n in PARAMS[kind]] + ["final_norm_g"]


def _is_big(name):
    return name.split("_", 1)[1] in COL_SHARDED + ROW_SHARDED


BIG = [n for n in WEIGHT_NAMES if _is_big(n)]
SMALL = [n for n in WEIGHT_NAMES if not _is_big(n)]


def _pack_rows(blocks):
    return jnp.concatenate([b.reshape(-1, PACK_W) for b in blocks], axis=0)


def _shard_major(wn, full, width):
    r, c = full.shape
    if wn in COL_SHARDED:
        t = full.reshape(r, N_CHIPS, c // N_CHIPS).transpose(1, 0, 2)
    else:
        t = full.reshape(N_CHIPS, r // N_CHIPS, c)
    return t.reshape(N_CHIPS, -1, width)


def _from_shard_major(name, t, block_shape):
    r, c = block_shape
    if name.split("_", 1)[1] in COL_SHARDED:
        return t.reshape(N_CHIPS, r, c).transpose(1, 0, 2).reshape(r, N_CHIPS * c)
    return t.reshape(N_CHIPS * r, c)


class BigGradSink:
    ORDER = ("w_out", "w_glu", "w_ukv", "w_uq", "w_in")
    ROW_MAJOR = {2: ("w_uq", "w_in")}

    def __init__(self, layer, block_shapes):
        self.layer = layer
        self.regions = {}
        r0 = 0
        for wn in self.ORDER:
            if wn in block_shapes:
                shape = block_shapes[wn]
                self.regions[wn] = (r0, shape, wn not in self.ROW_MAJOR.get(layer, ()))
                r0 += shape[0] * shape[1] // CHUNK_W
        self.buf = lax.empty((N_CHIPS, r0, CHUNK_W), F32)
        self.flight = None

    def mm(self, wn, a, b, name):
        r0, _, direct = self.regions[wn]
        assert direct
        self.buf = matmul_tn_packed(a, b, self.buf, r0, wn in COL_SHARDED, name)

    def put(self, wn, full):
        r0, _, direct = self.regions[wn]
        assert not direct
        piece = _shard_major(wn, full, CHUNK_W)
        self.buf = lax.dynamic_update_slice(self.buf, piece, (0, r0, 0))

    def send(self):
        i = self.layer
        sib, = grads_to_sibling([self.buf], "grads_to_sibling_l%d" % i)
        t = pair_sum(self.buf, sib, BF16, "pair_sum_l%d" % i)
        self.flight = reduce_start(t, sib, "reduce_l%d_start" % i)
        return self.flight[4][0, 0]


def _small_pack(arrs, total_padded):
    flat = jnp.concatenate([a.reshape(-1) for a in arrs])
    return jnp.pad(flat, (0, total_padded - flat.shape[0]))


def kernel(x, positions, l0_norm_g, l0_w_in, l0_ln_g, l0_ln_b, l0_w_s, l0_b_s, l0_w_out, l1_norm_g, l1_w_in, l1_a_re, l1_a_im, l1_log_step, l1_b_re, l1_b_im, l1_c_re, l1_c_im, l1_d_skip, l1_w_glu, l1_b_glu, l1_w_out, l2_norm_g, l2_w_in, l2_q_norm_g, l2_w_uq, l2_kv_norm_g, l2_w_ukv, l2_w_out, l3_norm_g, l3_w_in, l3_ln_g, l3_ln_b, l3_w_s, l3_b_s, l3_w_out, final_norm_g, loss_target, m_l0_norm_g, m_l0_w_in, m_l0_ln_g, m_l0_ln_b, m_l0_w_s, m_l0_b_s, m_l0_w_out, m_l1_norm_g, m_l1_w_in, m_l1_a_re, m_l1_a_im, m_l1_log_step, m_l1_b_re, m_l1_b_im, m_l1_c_re, m_l1_c_im, m_l1_d_skip, m_l1_w_glu, m_l1_b_glu, m_l1_w_out, m_l2_norm_g, m_l2_w_in, m_l2_q_norm_g, m_l2_w_uq, m_l2_kv_norm_g, m_l2_w_ukv, m_l2_w_out, m_l3_norm_g, m_l3_w_in, m_l3_ln_g, m_l3_ln_b, m_l3_w_s, m_l3_b_s, m_l3_w_out, m_final_norm_g, v_l0_norm_g, v_l0_w_in, v_l0_ln_g, v_l0_ln_b, v_l0_w_s, v_l0_b_s, v_l0_w_out, v_l1_norm_g, v_l1_w_in, v_l1_a_re, v_l1_a_im, v_l1_log_step, v_l1_b_re, v_l1_b_im, v_l1_c_re, v_l1_c_im, v_l1_d_skip, v_l1_w_glu, v_l1_b_glu, v_l1_w_out, v_l2_norm_g, v_l2_w_in, v_l2_q_norm_g, v_l2_w_uq, v_l2_kv_norm_g, v_l2_w_ukv, v_l2_w_out, v_l3_norm_g, v_l3_w_in, v_l3_ln_g, v_l3_ln_b, v_l3_w_s, v_l3_b_s, v_l3_w_out, v_final_norm_g):
    args = locals()
    w = {n: args[n] for n in WEIGHT_NAMES}
    mom_m = {n: args["m_" + n] for n in WEIGHT_NAMES}
    mom_v = {n: args["v_" + n] for n in WEIGHT_NAMES}
    h0 = x[0]
    target = loss_target[0]
    pos = positions.reshape(-1, 1)

    big_rows = [w[n].size // PACK_W for n in BIG]
    nrb = sum(big_rows)
    nrb_pad = -(-nrb // PACK_ROW_ALIGN) * PACK_ROW_ALIGN
    full = {}

    def pack_unit(layers):
        names = [n for n in BIG if int(n[1]) in layers]
        rows = [w[n].size // PACK_W for n in names]
        pad = -sum(rows) % PACK_ROW_ALIGN
        return names, rows, _pack_rows([w[n].astype(BF16) for n in names] + [jnp.zeros((pad, PACK_W), BF16)])

    def unpack_unit(names, rows, gathered):
        r0 = 0
        for n, nr in zip(names, rows):
            full[n] = _from_shard_major(n, gathered[:, r0:r0 + nr, :], w[n].shape)
            r0 += nr

    unit0, unit1, unit2 = pack_unit((0,)), pack_unit((1,)), pack_unit((2, 3))
    unpack_unit(unit0[0], unit0[1], weights_allgather(unit0[2]))
    flight = gather_start(unit1[2], unit0[2], "gather_l1_start")
    wp = dict(w)
    wp["l0_norm_g"] = w["l0_norm_g"] + flight[4][0, 0]

    def layer_params(i):
        pre = "l%d_" % i
        p = {k[len(pre):]: v for k, v in wp.items() if k.startswith(pre)}
        wf = {k[len(pre):]: v for k, v in full.items() if k.startswith(pre)}
        return p, wf

    cos, sins = rope_tables(pos)
    h = h0
    saved = []
    for i, kind in enumerate(LAYER_KINDS):
        if i == 1:
            land = gather_wait(*flight[:4], h, "gather_l1_wait")
            got = gather_handover(land, unit1[2], "gather_l1_handover")
            unpack_unit(unit1[0], unit1[1], got)
            flight = gather_start(unit2[2], got, "gather_l23_start")
            wp["l1_norm_g"] = w["l1_norm_g"] + flight[4][0, 0]
        if i == 2:
            land = gather_wait(*flight[:4], h, "gather_l23_wait")
            unpack_unit(unit2[0], unit2[1], gather_handover(land, unit2[2], "gather_l23_handover"))
        p, wf = layer_params(i)
        tag = "l%d" % i
        if kind == "gmlp":
            h, s = gmlp_layer_fwd(h, p, wf, tag)
        elif kind == "s5":
            h, s = s5_layer_fwd(h, p, wf, tag)
        else:
            h, s = mla_layer_fwd(h, p, wf, cos, sins, tag)
        saved.append(s)
    loss_part, dh, g_final = loss_head(h, final_norm_g, target)

    grads = {"final_norm_g": g_final.reshape(-1)}
    sinks = {}

    for i in reversed(range(len(LAYER_KINDS))):
        kind = LAYER_KINDS[i]
        p, wf = layer_params(i)
        tag = "l%d" % i
        sink = sinks[i] = BigGradSink(i, {n[3:]: w[n].shape for n in BIG if int(n[1]) == i})
        if kind == "gmlp":
            dh, g = gmlp_layer_bwd(dh, saved[i], p, wf, tag, sink)
        elif kind == "s5":
            dh, g = s5_layer_bwd(dh, saved[i], p, wf, tag, sink)
        else:
            dh, g = mla_layer_bwd(dh, saved[i], p, wf, cos, sins, tag, sink)
        for k, val in g.items():
            grads["l%d_%s" % (i, k)] = val
    grad_x = dh[None]

    n_small = sum(w[n].size for n in SMALL)
    piece = N_CHIPS * 2 * 16 * PACK_W
    n_small_pad = -(-n_small // piece) * piece
    nrs = n_small_pad // N_CHIPS // PACK_W
    p_small = _small_pack([grads[n] for n in SMALL], n_small_pad).reshape(N_CHIPS, nrs, PACK_W)
    sib_small, = grads_to_sibling([p_small], "grads_to_sibling_small")
    t_small = pair_sum(p_small, sib_small, F32, "pair_sum_small")
    rb_small, = grads_across_chips([t_small])
    halves = [chip_sum(t_small, rb_small, "chip_sum_small")]

    after = halves[0]
    for i in reversed(range(len(LAYER_KINDS))):
        t_i, rb_i = reduce_wait(*sinks[i].flight[:4], after, "reduce_l%d_wait" % i)
        halves.append(chip_sum(t_i, rb_i, "chip_sum_l%d" % i))
        after = halves[-1]
    reduced = reduced_to_sibling(halves)
    small_all = small_allgather(reduced[0], 0, nrs)

    g_out, d_out, nm_out, nv_out = {}, {}, {}, {}
    for i, g_i in zip(reversed(range(len(LAYER_KINDS))), reduced[1:]):
        for wn, (r0, shape, direct) in sinks[i].regions.items():
            n = "l%d_%s" % (i, wn)
            if direct:
                g_out[n], d_out[n], nm_out[n], nv_out[n] = adamw_packed(w[n], g_i, r0, mom_m[n], mom_v[n], "adamw_" + n)
            else:
                g_out[n] = g_i[r0:r0 + shape[0] * shape[1] // CHUNK_W].reshape(shape)
                d_out[n], nm_out[n], nv_out[n] = adamw(w[n], g_out[n], mom_m[n], mom_v[n], "adamw_" + n)
    g_small = small_all.reshape(-1, PACK_W)
    sp = lambda d: _small_pack([d[n] for n in SMALL], n_small_pad).reshape(-1, PACK_W)
    d_small, nm_small, nv_small = adamw(sp(w), g_small, sp(mom_m), sp(mom_v), "adamw_small")
    for buf, out in ((g_small, g_out), (d_small, d_out), (nm_small, nm_out), (nv_small, nv_out)):
        flat = buf.reshape(-1)
        o = 0
        for n in SMALL:
            out[n] = flat[o:o + w[n].size].reshape(w[n].shape)
            o += w[n].size
    loss = lax.psum(loss_part[0, 0], ("x", "y", "c"))
    return (loss, grad_x, *[g_out[n] for n in WEIGHT_NAMES], *[d_out[n] for n in WEIGHT_NAMES],
            *[nm_out[n] for n in WEIGHT_NAMES], *[nv_out[n] for n in WEIGHT_NAMES])
```

```python
import functools
import math

import jax
import jax.numpy as jnp
import numpy as np
from jax import lax
from jax.experimental import pallas as pl
from jax.experimental.pallas import tpu as pltpu

F32 = jnp.float32
BF16 = jnp.bfloat16
MESH = pl.DeviceIdType.MESH
VMEM_LIMIT_BYTES = 56 * 1024 * 1024
LANES = 128
PACK_W = 1024
CHUNK_W = 256
PACK_ROW_ALIGN = 256
ROW_TILE = 256
ADAMW_BLOCK_BYTES = 1024 * 1024
MM_BLOCK_BYTES = 6 * 1024 * 1024

NORM_EPS = 1e-6
N_CHIPS = 4
GMLP_CHUNK = 128
GMLP_GROUPS = 8
S5_GROUPS = 128
S5_GROUP = 16
S5_STATE = 64
S5_SB = 16
S5_SEG = 8
MLA_HEADS = 16
MLA_NOPE = 128
MLA_ROPE = 64
MLA_Q_RANK = 384
MLA_KV_RANK = 128
MLA_SCALE = (MLA_NOPE + MLA_ROPE) ** -0.5
ROPE_THETA = 10000.0
NEG_INF = -1e30
ADAM_LR, ADAM_B1, ADAM_B2, ADAM_EPS, ADAM_WD, ADAM_STEP = 0.001, 0.9, 0.999, 1e-08, 0.01, 10

DN_NN = (((1,), (0,)), ((), ()))
DN_NT = (((1,), (1,)), ((), ()))
DN_TN = (((0,), (0,)), ((), ()))


def _cparams(sem):
    return pltpu.CompilerParams(dimension_semantics=sem, vmem_limit_bytes=VMEM_LIMIT_BYTES)


def _pick(n, cands=(512, 384, 256, 128)):
    for c in cands:
        if n % c == 0:
            return c
    return n


def _pick_rows(r, cap=512, mult=16):
    return max(t for t in range(mult, cap + 1, mult) if r % t == 0)


def _dot(a, b, dn):
    return lax.dot_general(a.astype(BF16), b.astype(BF16), dn, preferred_element_type=F32)


def _sigmoid(x):
    return 1.0 / (1.0 + jnp.exp(-x))


def _gelu(x):
    c = math.sqrt(2.0 / math.pi)
    t = jnp.tanh(c * (x + 0.044715 * x * x * x))
    return 0.5 * x * (1.0 + t)


def _gelu_grad(x):
    c = math.sqrt(2.0 / math.pi)
    t = jnp.tanh(c * (x + 0.044715 * x * x * x))
    return 0.5 * (1.0 + t) + 0.5 * x * (1.0 - t * t) * c * (1.0 + 3.0 * 0.044715 * x * x)


def _gelu_both(x):
    c = math.sqrt(2.0 / math.pi)
    t = jnp.tanh(c * (x + 0.044715 * x * x * x))
    return 0.5 * x * (1.0 + t), 0.5 * (1.0 + t) + 0.5 * x * (1.0 - t * t) * c * (1.0 + 3.0 * 0.044715 * x * x)


def _silu_both(z):
    s = _sigmoid(z)
    return z * s, s * (1.0 + z * (1.0 - s))


def _silu(z):
    return z * _sigmoid(z)


def _silu_grad(z):
    s = _sigmoid(z)
    return s * (1.0 + z * (1.0 - s))


def matmul(a, b, mode, name, out_dtype=F32, add=None):
    if mode == "nn":
        (m, k), n = a.shape, b.shape[1]
    elif mode == "nt":
        (m, k), n = a.shape, b.shape[0]
    else:
        (k, m), n = a.shape, b.shape[1]
    tm = _pick(m, [t for t in (1024, 512, 384, 256, 128) if t * k * a.dtype.itemsize <= MM_BLOCK_BYTES])
    tn = _pick(n, [t for t in (512, 384, 256, 128) if t * k * b.dtype.itemsize <= MM_BLOCK_BYTES])
    dn = {"nn": DN_NN, "nt": DN_NT, "tn": DN_TN}[mode]

    def body(*refs):
        if add is None:
            a_ref, b_ref, o_ref = refs
        else:
            a_ref, b_ref, add_ref, o_ref = refs
        r = _dot(a_ref[...], b_ref[...], dn)
        if add is not None:
            r = r + add_ref[...].astype(F32)
        o_ref[...] = r.astype(out_dtype)

    a_spec = pl.BlockSpec((k, tm), lambda i, j: (0, i)) if mode == "tn" else pl.BlockSpec((tm, k), lambda i, j: (i, 0))
    b_spec = pl.BlockSpec((tn, k), lambda i, j: (j, 0)) if mode == "nt" else pl.BlockSpec((k, tn), lambda i, j: (0, j))
    o_spec = pl.BlockSpec((tm, tn), lambda i, j: (i, j))
    in_specs = [a_spec, b_spec] + ([o_spec] if add is not None else [])
    args = (a, b) + ((add,) if add is not None else ())
    return pl.pallas_call(
        body, name=name, grid=(m // tm, n // tn), in_specs=in_specs, out_specs=o_spec,
        out_shape=jax.ShapeDtypeStruct((m, n), out_dtype),
        compiler_params=_cparams(("parallel", "arbitrary")))(*args)


def _tile_rows(r, r0, cands=(512, 384, 256, 128)):
    return next(t for t in cands if r % t == 0 and r0 % t == 0)


def matmul_tn_packed(a, b, buf, r0, col_sharded, name):
    k, m = a.shape
    n = b.shape[1]
    if col_sharded:
        chunks = n // N_CHIPS // CHUNK_W
        tm = _tile_rows(m, r0)
        o_map = lambda i, j: (j // chunks, (r0 + (j % chunks) * m) // tm + i, 0)
    else:
        rs = m // N_CHIPS
        tm = _tile_rows(rs, r0)
        per = rs // tm
        o_map = lambda i, j: (i // per, (r0 + j * rs) // tm + i % per, 0)

    def body(a_ref, b_ref, buf_ref, o_ref):
        o_ref[0] = _dot(a_ref[...], b_ref[...], DN_TN)

    return pl.pallas_call(
        body, name=name, grid=(m // tm, n // CHUNK_W),
        in_specs=[pl.BlockSpec((k, tm), lambda i, j: (0, i)), pl.BlockSpec((k, CHUNK_W), lambda i, j: (0, j)),
                  pl.BlockSpec(memory_space=pl.ANY)],
        out_specs=pl.BlockSpec((1, tm, CHUNK_W), o_map), out_shape=jax.ShapeDtypeStruct(buf.shape, buf.dtype),
        input_output_aliases={2: 0}, compiler_params=_cparams(("parallel", "arbitrary")))(a, b, buf)


def _rows(tl, w, col=0):
    return pl.BlockSpec((tl, w), lambda i: (i, col))


def _full(shape):
    nd = len(shape)
    return pl.BlockSpec(tuple(shape), lambda i: (0,) * nd)


def _rowcall(body, name, n_steps, in_specs, out_specs, out_shape, scratch=()):
    return pl.pallas_call(
        body, name=name, grid=(n_steps,), in_specs=in_specs, out_specs=out_specs, out_shape=out_shape,
        scratch_shapes=list(scratch), compiler_params=_cparams(("arbitrary",)))


def _acc(ref, val, i):
    @pl.when(i == 0)
    def _():
        ref[...] = val

    @pl.when(i != 0)
    def _():
        ref[...] += val


def rms_fwd(h, g, name):
    l, d = h.shape
    tl = ROW_TILE

    def body(h_ref, g_ref, o_ref):
        x = h_ref[...]
        r = lax.rsqrt(jnp.mean(x * x, axis=-1, keepdims=True) + NORM_EPS)
        o_ref[...] = (x * r * g_ref[...]).astype(BF16)

    return _rowcall(body, name, l // tl, [_rows(tl, d), _full((1, d))], _rows(tl, d),
                    jax.ShapeDtypeStruct((l, d), BF16))(h, g.reshape(1, d))


def rms_bwd(h, g, dhn, dh_in, name):
    l, d = h.shape
    tl = ROW_TILE

    def body(h_ref, g_ref, dhn_ref, dhi_ref, dh_ref, dg_ref):
        i = pl.program_id(0)
        x = h_ref[...]
        r = lax.rsqrt(jnp.mean(x * x, axis=-1, keepdims=True) + NORM_EPS)
        xhat = x * r
        dy = dhn_ref[...]
        dxh = dy * g_ref[...]
        dx = r * (dxh - xhat * jnp.mean(dxh * xhat, axis=-1, keepdims=True))
        dh_ref[...] = dhi_ref[...] + dx
        _acc(dg_ref, jnp.sum(dy * xhat, axis=0, keepdims=True), i)

    return _rowcall(body, name, l // tl, [_rows(tl, d), _full((1, d)), _rows(tl, d), _rows(tl, d)],
                    [_rows(tl, d), _full((1, d))],
                    [jax.ShapeDtypeStruct((l, d), F32), jax.ShapeDtypeStruct((1, d), F32)])(h, g.reshape(1, d), dhn, dh_in)


def loss_head(h, g, target):
    l, d = h.shape
    tl = ROW_TILE

    def body(h_ref, g_ref, t_ref, loss_ref, dh_ref, dg_ref):
        i = pl.program_id(0)
        x = h_ref[...]
        gg = g_ref[...]
        r = lax.rsqrt(jnp.mean(x * x, axis=-1, keepdims=True) + NORM_EPS)
        xhat = x * r
        err = xhat * gg - t_ref[...]
        part = 0.5 * jnp.sum(jnp.mean(err * err, axis=-1, keepdims=True), axis=0, keepdims=True)
        _acc(loss_ref, part, i)
        dy = err * (1.0 / d)
        dxh = dy * gg
        dh_ref[...] = r * (dxh - xhat * jnp.mean(dxh * xhat, axis=-1, keepdims=True))
        _acc(dg_ref, jnp.sum(dy * xhat, axis=0, keepdims=True), i)

    return _rowcall(body, "loss_head", l // tl, [_rows(tl, d), _full((1, d)), _rows(tl, d)],
                    [_full((1, 1)), _rows(tl, d), _full((1, d))],
                    [jax.ShapeDtypeStruct((1, 1), F32), jax.ShapeDtypeStruct((l, d), F32),
                     jax.ShapeDtypeStruct((1, d), F32)])(h, g.reshape(1, d), target)


def _gmlp_common(a_ref, lng_ref, lnb_ref):
    di = lng_ref.shape[1]
    u_pre = a_ref[:, 0:di]
    v_pre = a_ref[:, di:2 * di]
    z = a_ref[:, 2 * di:3 * di]
    vg = _gelu(v_pre)
    mu = jnp.mean(vg, axis=-1, keepdims=True)
    xc = vg - mu
    rstd = lax.rsqrt(jnp.mean(xc * xc, axis=-1, keepdims=True) + NORM_EPS)
    vhat = xc * rstd
    vn = vhat * lng_ref[...] + lnb_ref[...]
    return u_pre, v_pre, z, vhat, rstd, vn


def _tril(w):
    r = lax.broadcasted_iota(jnp.int32, w.shape, 0)
    c = lax.broadcasted_iota(jnp.int32, w.shape, 1)
    return jnp.where(c <= r, w, 0.0)


def gmlp_gate_fwd(a, ln_g, ln_b, w_s, b_s, name):
    l, w3 = a.shape
    di = w3 // 3
    dg = di // GMLP_GROUPS
    tl = GMLP_CHUNK

    def body(a_ref, lng_ref, lnb_ref, ws_ref, bs_ref, m_ref):
        u_pre, _, z, _, _, vn = _gmlp_common(a_ref, lng_ref, lnb_ref)
        gate = _gelu(u_pre) * _silu(z)
        for g in range(GMLP_GROUPS):
            sl = slice(g * dg, (g + 1) * dg)
            s = _dot(_tril(ws_ref[g]), vn[:, sl], DN_NN) + bs_ref[g]
            m_ref[:, sl] = (gate[:, sl] * s).astype(BF16)

    return _rowcall(body, name, l // tl,
                    [_rows(tl, w3), _full((1, di)), _full((1, di)), _full(w_s.shape), _full((GMLP_GROUPS, tl, 1))],
                    _rows(tl, di), jax.ShapeDtypeStruct((l, di), BF16))(
        a, ln_g.reshape(1, di), ln_b.reshape(1, di), w_s, b_s.reshape(GMLP_GROUPS, tl, 1))


def gmlp_gate_bwd(a, dm, ln_g, ln_b, w_s, b_s, name):
    l, w3 = a.shape
    di = w3 // 3
    dg = di // GMLP_GROUPS
    tl = GMLP_CHUNK

    def body(a_ref, dm_ref, lng_ref, lnb_ref, ws_ref, bs_ref, da_ref, dlg_ref, dlb_ref, dws_ref, dbs_ref,
             dvn_ref, vh_ref, gv_ref):
        i = pl.program_id(0)
        vg, gv = _gelu_both(a_ref[:, di:2 * di])
        gv_ref[...] = gv
        xc = vg - jnp.mean(vg, axis=-1, keepdims=True)
        rstd = lax.rsqrt(jnp.mean(xc * xc, axis=-1, keepdims=True) + NORM_EPS)
        vh_ref[...] = xc * rstd
        for g in range(GMLP_GROUPS):
            sl = slice(g * dg, (g + 1) * dg)
            wt = _tril(ws_ref[g])
            vn_g = vh_ref[:, sl] * lng_ref[:, sl] + lnb_ref[:, sl]
            s = _dot(wt, vn_g, DN_NN) + bs_ref[g]
            dmg = dm_ref[:, sl]
            u, gu = _gelu_both(a_ref[:, sl])
            sz, gz = _silu_both(a_ref[:, 2 * di + g * dg:2 * di + (g + 1) * dg])
            ds = dmg * u * sz
            da_ref[:, sl] = (dmg * s * sz * gu).astype(BF16)
            da_ref[:, 2 * di + g * dg:2 * di + (g + 1) * dg] = (dmg * u * s * gz).astype(BF16)
            dvn_ref[:, sl] = _dot(wt, ds, DN_TN)
            dw = _tril(_dot(ds, vn_g, DN_NT))
            db = jnp.sum(ds, axis=1, keepdims=True)

            @pl.when(i == 0)
            def _():
                dws_ref[g] = dw
                dbs_ref[g] = db

            @pl.when(i != 0)
            def _():
                dws_ref[g] += dw
                dbs_ref[g] += db

        dvn = dvn_ref[...]
        vhat = vh_ref[...]
        dxh = dvn * lng_ref[...]
        dvg = rstd * (dxh - jnp.mean(dxh, axis=-1, keepdims=True) - vhat * jnp.mean(dxh * vhat, axis=-1, keepdims=True))
        da_ref[:, di:2 * di] = (dvg * gv_ref[...]).astype(BF16)
        _acc(dlg_ref, jnp.sum(dvn * vhat, axis=0, keepdims=True), i)
        _acc(dlb_ref, jnp.sum(dvn, axis=0, keepdims=True), i)

    outs = _rowcall(
        body, name, l // tl,
        [_rows(tl, w3), _rows(tl, di), _full((1, di)), _full((1, di)), _full(w_s.shape), _full((GMLP_GROUPS, tl, 1))],
        [_rows(tl, w3), _full((1, di)), _full((1, di)), _full(w_s.shape), _full((GMLP_GROUPS, tl, 1))],
        [jax.ShapeDtypeStruct((l, w3), BF16), jax.ShapeDtypeStruct((1, di), F32), jax.ShapeDtypeStruct((1, di), F32),
         jax.ShapeDtypeStruct(w_s.shape, F32), jax.ShapeDtypeStruct((GMLP_GROUPS, tl, 1), F32)],
        scratch=[pltpu.VMEM((tl, di), F32)] * 3)(
        a, dm, ln_g.reshape(1, di), ln_b.reshape(1, di), w_s, b_s.reshape(GMLP_GROUPS, tl, 1))
    return outs


def gmlp_layer_fwd(h, p, wf, tag):
    hn = rms_fwd(h, p["norm_g"], tag + "_rms")
    a = matmul(hn, wf["w_in"], "nn", tag + "_mm_in")
    m = gmlp_gate_fwd(a, p["ln_g"], p["ln_b"], p["w_s"], p["b_s"], tag + "_gate")
    h_out = matmul(m, wf["w_out"], "nn", tag + "_mm_out", add=h)
    return h_out, (h, hn, a, m)


def gmlp_layer_bwd(dh_out, saved, p, wf, tag, sink):
    h, hn, a, m = saved
    dm = matmul(dh_out, wf["w_out"], "nt", tag + "_mm_dm")
    sink.mm("w_out", m, dh_out, tag + "_mm_gwout")
    da, dlg, dlb, dws, dbs = gmlp_gate_bwd(a, dm, p["ln_g"], p["ln_b"], p["w_s"], p["b_s"], tag + "_gate_bwd")
    dhn = matmul(da, wf["w_in"], "nt", tag + "_mm_dhn")
    sink.mm("w_in", hn, da, tag + "_mm_gwin")
    zero = sink.send()
    dh, dng = rms_bwd(h, p["norm_g"] + zero, dhn, dh_out, tag + "_rms_bwd")
    grads = {"norm_g": dng.reshape(-1), "ln_g": dlg.reshape(-1), "ln_b": dlb.reshape(-1),
             "w_s": dws, "b_s": dbs.reshape(GMLP_GROUPS, GMLP_CHUNK)}
    return dh, grads


def _cmul(ar, ai, br, bi):
    return ar * br - ai * bi, ar * bi + ai * br


S5_PG = 16


def _gblock(tail):
    return pl.BlockSpec((S5_PG,) + tuple(tail), lambda i: (i, 0, 0))


def s5_params_fwd(a_re, a_im, log_step, b_re, b_im):
    g, p, hh = b_re.shape

    def body(ar_ref, ai_ref, ls_ref, br_ref, bi_ref, lr_ref, li_ref, bbr_ref, bbi_ref):
        ar, ai = ar_ref[...], ai_ref[...]
        step = jnp.exp(ls_ref[...])
        mag = jnp.exp(ar * step)
        lr, li = mag * jnp.cos(ai * step), mag * jnp.sin(ai * step)
        den = 1.0 / (ar * ar + ai * ai)
        fr, fi = _cmul(lr - 1.0, li, ar * den, -ai * den)
        lr_ref[...] = lr
        li_ref[...] = li
        bbr, bbi = _cmul(fr, fi, br_ref[...], bi_ref[...])
        bbr_ref[...] = bbr
        bbi_ref[...] = bbi

    s1 = jax.ShapeDtypeStruct((g, p, 1), F32)
    s3 = jax.ShapeDtypeStruct((g, p, hh), F32)
    b1, b0, b3 = _gblock((p, 1)), _gblock((1, 1)), _gblock((p, hh))
    return pl.pallas_call(body, name="s5_params_fwd", grid=(g // S5_PG,), in_specs=[b1, b1, b0, b3, b3],
                          out_specs=[b1, b1, b3, b3], out_shape=[s1, s1, s3, s3],
                          compiler_params=_cparams(("parallel",)))(
        a_re.reshape(g, p, 1), a_im.reshape(g, p, 1), log_step.reshape(g, 1, 1), b_re, b_im)


def s5_params_bwd(a_re, a_im, log_step, b_re, b_im, dl_re, dl_im, dbb_re, dbb_im):
    g, p, hh = b_re.shape

    def body(ar_ref, ai_ref, ls_ref, br_ref, bi_ref, dlr_ref, dli_ref, dbr_ref, dbi_ref,
             gar_ref, gai_ref, gls_ref, gbr_ref, gbi_ref):
        ar, ai = ar_ref[...], ai_ref[...]
        step = jnp.exp(ls_ref[...])
        mag = jnp.exp(ar * step)
        lr, li = mag * jnp.cos(ai * step), mag * jnp.sin(ai * step)
        den = 1.0 / (ar * ar + ai * ai)
        ir, ii = ar * den, -ai * den
        fr, fi = _cmul(lr - 1.0, li, ir, ii)
        br, bi = br_ref[...], bi_ref[...]
        dbr, dbi = dbr_ref[...], dbi_ref[...]
        gbr, gbi = _cmul(fr, -fi, dbr, dbi)
        gbr_ref[...] = gbr
        gbi_ref[...] = gbi
        pr, pi = _cmul(br, -bi, dbr, dbi)
        gfr = jnp.sum(pr, axis=-1, keepdims=True)
        gfi = jnp.sum(pi, axis=-1, keepdims=True)
        t_r, t_i = _cmul(ir, -ii, gfr, gfi)
        glr, gli = dlr_ref[...] + t_r, dli_ref[...] + t_i
        c1r, c1i = _cmul(step * lr, -step * li, glr, gli)
        qr, qi = _cmul(fr, fi, ir, ii)
        c2r, c2i = _cmul(-qr, qi, gfr, gfi)
        gar_ref[...] = c1r + c2r
        gai_ref[...] = c1i + c2i
        wr, wi = _cmul(ar, ai, lr, li)
        sr, _ = _cmul(wr, -wi, glr, gli)
        gls_ref[...] = jnp.sum(sr, axis=1, keepdims=True) * step

    s1 = jax.ShapeDtypeStruct((g, p, 1), F32)
    s3 = jax.ShapeDtypeStruct((g, p, hh), F32)
    b1, b0, b3 = _gblock((p, 1)), _gblock((1, 1)), _gblock((p, hh))
    return pl.pallas_call(body, name="s5_params_bwd", grid=(g // S5_PG,),
                          in_specs=[b1, b1, b0, b3, b3, b1, b1, b3, b3], out_specs=[b1, b1, b0, b3, b3],
                          out_shape=[s1, s1, jax.ShapeDtypeStruct((g, 1, 1), F32), s3, s3],
                          compiler_params=_cparams(("parallel",)))(
        a_re.reshape(g, p, 1), a_im.reshape(g, p, 1), log_step.reshape(g, 1, 1), b_re, b_im,
        dl_re, dl_im, dbb_re, dbb_im)


def _blockdiag(t):
    sb, n, r, c = t.shape
    eye = jnp.eye(n, dtype=bool)[None, :, None, :, None]
    full = jnp.where(eye, t[:, :, :, None, :], jnp.zeros((), t.dtype))
    return full.reshape(sb, n * r, n * c)


def _blockdiag_extract(m, r, c):
    sb = m.shape[0]
    n = m.shape[1] // r
    m5 = m.reshape(sb, n, r, n, c)
    return jnp.stack([m5[:, i, :, i, :] for i in range(n)], axis=1)


S5_TB = 64
S5_UNROLL = 8


def s5_scan_fwd(a_p, lam_re, lam_im, wb_re, wb_im, wc_re, wc_im, d_skip, x0_re, x0_im, name):
    l = a_p.shape[0]
    di = d_skip.shape[1]
    rows = S5_SEG * S5_TB
    nb = l // rows
    ns = wb_re.shape[2]

    def body(u_ref, lr_ref, li_ref, wbr_ref, wbi_ref, wcr_ref, wci_ref, ds_ref, x0r_ref, x0i_ref,
             y_ref, ckr_ref, cki_ref, xer_ref, xei_ref, bur, bui, xr_s, xi_s):
        b = pl.program_id(1)

        @pl.when(b == 0)
        def _():
            xr_s[...] = x0r_ref[0]
            xi_s[...] = x0i_ref[0]

        ckr_ref[0, 0] = xr_s[...]
        cki_ref[0, 0] = xi_s[...]
        u = u_ref[...]
        bur[...] = _dot(u, wbr_ref[0], DN_NN)
        bui[...] = _dot(u, wbi_ref[0], DN_NN)
        lr = jnp.broadcast_to(lr_ref[0], (S5_SEG, ns))
        li = jnp.broadcast_to(li_ref[0], (S5_SEG, ns))

        def step(t, carry):
            xr, xi = carry
            sl = pl.ds(pl.multiple_of(t * S5_SEG, S5_SEG), S5_SEG)
            nr = lr * xr - li * xi + bur[sl, :]
            ni = lr * xi + li * xr + bui[sl, :]
            bur[sl, :] = nr
            bui[sl, :] = ni
            return nr, ni

        xr, xi = lax.fori_loop(0, S5_TB, step, (xr_s[...], xi_s[...]), unroll=S5_UNROLL)
        xr_s[...] = xr
        xi_s[...] = xi
        xer_ref[0] = xr
        xei_ref[0] = xi
        y_ref[...] = _dot(bur[...], wcr_ref[0], DN_NN) - _dot(bui[...], wci_ref[0], DN_NN) + ds_ref[...] * u

    sb3 = lambda s, b: (s, 0, 0)
    st = jax.ShapeDtypeStruct
    return pl.pallas_call(
        body, name=name, grid=(S5_SB, nb),
        in_specs=[pl.BlockSpec((rows, LANES), lambda s, b: (b, s)),
                  pl.BlockSpec((1, 1, ns), sb3), pl.BlockSpec((1, 1, ns), sb3),
                  pl.BlockSpec((1, LANES, ns), sb3), pl.BlockSpec((1, LANES, ns), sb3),
                  pl.BlockSpec((1, ns, LANES), sb3), pl.BlockSpec((1, ns, LANES), sb3),
                  pl.BlockSpec((1, LANES), lambda s, b: (0, s)),
                  pl.BlockSpec((1, S5_SEG, ns), sb3), pl.BlockSpec((1, S5_SEG, ns), sb3)],
        out_specs=[pl.BlockSpec((rows, LANES), lambda s, b: (b, s)),
                   pl.BlockSpec((1, 1, S5_SEG, ns), lambda s, b: (s, b, 0, 0)),
                   pl.BlockSpec((1, 1, S5_SEG, ns), lambda s, b: (s, b, 0, 0)),
                   pl.BlockSpec((1, S5_SEG, ns), sb3), pl.BlockSpec((1, S5_SEG, ns), sb3)],
        out_shape=[st((l, di), F32), st((S5_SB, nb, S5_SEG, ns), F32), st((S5_SB, nb, S5_SEG, ns), F32),
                   st((S5_SB, S5_SEG, ns), F32), st((S5_SB, S5_SEG, ns), F32)],
        scratch_shapes=[pltpu.VMEM((rows, ns), F32), pltpu.VMEM((rows, ns), F32),
                        pltpu.VMEM((S5_SEG, ns), F32), pltpu.VMEM((S5_SEG, ns), F32)],
        compiler_params=_cparams(("parallel", "arbitrary")))(
        a_p, lam_re, lam_im, wb_re, wb_im, wc_re, wc_im, d_skip, x0_re, x0_im)


def s5_ends(inp, lam_re, lam_im, w_re, w_im, adjoint, name):
    l = inp.shape[0]
    rows = S5_SEG * S5_TB
    nb = l // rows
    ns = lam_re.shape[2]

    def body(i_ref, lr_ref, li_ref, wr_ref, wi_ref, er_ref, ei_ref, pr_b, pi_b, xr_s, xi_s):
        b = pl.program_id(1)

        @pl.when(b == 0)
        def _():
            xr_s[...] = jnp.zeros_like(xr_s)
            xi_s[...] = jnp.zeros_like(xi_s)

        v = i_ref[...]
        lr = jnp.broadcast_to(lr_ref[0], (S5_SEG, ns))
        li = jnp.broadcast_to(li_ref[0], (S5_SEG, ns))
        if adjoint:
            pr_b[...] = _dot(v, wr_ref[0], DN_NT)
            pi_b[...] = -_dot(v, wi_ref[0], DN_NT)
            li = -li
        else:
            pr_b[...] = _dot(v, wr_ref[0], DN_NN)
            pi_b[...] = _dot(v, wi_ref[0], DN_NN)

        def step(k, carry):
            xr, xi = carry
            t = S5_TB - 1 - k if adjoint else k
            sl = pl.ds(pl.multiple_of(t * S5_SEG, S5_SEG), S5_SEG)
            return lr * xr - li * xi + pr_b[sl, :], lr * xi + li * xr + pi_b[sl, :]

        xr, xi = lax.fori_loop(0, S5_TB, step, (xr_s[...], xi_s[...]), unroll=S5_UNROLL)
        xr_s[...] = xr
        xi_s[...] = xi
        er_ref[0] = xr
        ei_ref[0] = xi

    sb3 = lambda s, b: (s, 0, 0)
    blk = (lambda s, b: (nb - 1 - b, s)) if adjoint else (lambda s, b: (b, s))
    wshape = (1, ns, LANES) if adjoint else (1, LANES, ns)
    st = jax.ShapeDtypeStruct((S5_SB, S5_SEG, ns), F32)
    return pl.pallas_call(
        body, name=name, grid=(S5_SB, nb),
        in_specs=[pl.BlockSpec((rows, LANES), blk), pl.BlockSpec((1, 1, ns), sb3), pl.BlockSpec((1, 1, ns), sb3),
                  pl.BlockSpec(wshape, sb3), pl.BlockSpec(wshape, sb3)],
        out_specs=[pl.BlockSpec((1, S5_SEG, ns), sb3), pl.BlockSpec((1, S5_SEG, ns), sb3)],
        out_shape=[st, st],
        scratch_shapes=[pltpu.VMEM((rows, ns), F32), pltpu.VMEM((rows, ns), F32),
                        pltpu.VMEM((S5_SEG, ns), F32), pltpu.VMEM((S5_SEG, ns), F32)],
        compiler_params=_cparams(("parallel", "arbitrary")))(inp, lam_re, lam_im, w_re, w_im)


def s5_scan_bwd(a_p, dy, lam_re, lam_im, wb_re, wb_im, wc_re, wc_im, d_skip, ck_re, ck_im, a0_re, a0_im, name):
    l = a_p.shape[0]
    di = d_skip.shape[1]
    rows = S5_SEG * S5_TB
    nb = l // rows
    ns = wb_re.shape[2]

    def body(u_ref, dy_ref, lr_ref, li_ref, wbr_ref, wbi_ref, wcr_ref, wci_ref, ds_ref, ckr_ref, cki_ref,
             a0r_ref, a0i_ref,
             du_ref, dwbr_ref, dwbi_ref, dwcr_ref, dwci_ref, dds_ref, dlr_ref, dli_ref, aer_ref, aei_ref,
             xr_b, xi_b, gr_b, gi_b, ar_s, ai_s):
        b = pl.program_id(1)

        @pl.when(b == 0)
        def _():
            ar_s[...] = a0r_ref[0]
            ai_s[...] = a0i_ref[0]

        u = u_ref[...]
        dyv = dy_ref[...]
        lr = jnp.broadcast_to(lr_ref[0], (S5_SEG, ns))
        li = jnp.broadcast_to(li_ref[0], (S5_SEG, ns))
        xr_b[...] = _dot(u, wbr_ref[0], DN_NN)
        xi_b[...] = _dot(u, wbi_ref[0], DN_NN)

        def fstep(t, carry):
            xr, xi = carry
            sl = pl.ds(pl.multiple_of(t * S5_SEG, S5_SEG), S5_SEG)
            nr = lr * xr - li * xi + xr_b[sl, :]
            ni = lr * xi + li * xr + xi_b[sl, :]
            xr_b[sl, :] = nr
            xi_b[sl, :] = ni
            return nr, ni

        x0r, x0i = ckr_ref[0, 0], cki_ref[0, 0]
        lax.fori_loop(0, S5_TB, fstep, (x0r, x0i), unroll=S5_UNROLL)
        dwcr = _dot(xr_b[...], dyv, DN_TN)
        dwci = -_dot(xi_b[...], dyv, DN_TN)
        gr_b[...] = _dot(dyv, wcr_ref[0], DN_NT)
        gi_b[...] = -_dot(dyv, wci_ref[0], DN_NT)

        def bstep(k, carry):
            ar, ai, dlr, dli = carry
            t = S5_TB - 1 - k
            sl = pl.ds(pl.multiple_of(t * S5_SEG, S5_SEG), S5_SEG)
            slp = pl.ds(pl.multiple_of(jnp.maximum(t - 1, 0) * S5_SEG, S5_SEG), S5_SEG)
            nr = gr_b[sl, :] + lr * ar + li * ai
            ni = gi_b[sl, :] + lr * ai - li * ar
            gr_b[sl, :] = nr
            gi_b[sl, :] = ni
            first = t == 0
            pr = jnp.where(first, x0r, xr_b[slp, :])
            pi = jnp.where(first, x0i, xi_b[slp, :])
            dlr = dlr + nr * pr + ni * pi
            dli = dli + ni * pr - nr * pi
            return nr, ni, dlr, dli

        zero = jnp.zeros((S5_SEG, ns), F32)
        ar, ai, dlr, dli = lax.fori_loop(0, S5_TB, bstep, (ar_s[...], ai_s[...], zero, zero), unroll=S5_UNROLL)
        ar_s[...] = ar
        ai_s[...] = ai
        aer_ref[0] = ar
        aei_ref[0] = ai
        dsk = ds_ref[...]
        du_ref[...] = (_dot(gr_b[...], wbr_ref[0], DN_NT) + _dot(gi_b[...], wbi_ref[0], DN_NT) + dsk * dyv).astype(BF16)
        dwbr = _dot(u, gr_b[...], DN_TN)
        dwbi = _dot(u, gi_b[...], DN_TN)
        dds = jnp.sum(dyv * u, axis=0, keepdims=True)

        @pl.when(b == 0)
        def _():
            dwbr_ref[0] = dwbr
            dwbi_ref[0] = dwbi
            dwcr_ref[0] = dwcr
            dwci_ref[0] = dwci
            dds_ref[...] = dds
            dlr_ref[0] = dlr
            dli_ref[0] = dli

        @pl.when(b != 0)
        def _():
            dwbr_ref[0] += dwbr
            dwbi_ref[0] += dwbi
            dwcr_ref[0] += dwcr
            dwci_ref[0] += dwci
            dds_ref[...] += dds
            dlr_ref[0] += dlr
            dli_ref[0] += dli

    sb3 = lambda s, b: (s, 0, 0)
    rev = lambda s, b: (nb - 1 - b, s)
    st = jax.ShapeDtypeStruct
    return pl.pallas_call(
        body, name=name, grid=(S5_SB, nb),
        in_specs=[pl.BlockSpec((rows, LANES), rev), pl.BlockSpec((rows, LANES), rev),
                  pl.BlockSpec((1, 1, ns), sb3), pl.BlockSpec((1, 1, ns), sb3),
                  pl.BlockSpec((1, LANES, ns), sb3), pl.BlockSpec((1, LANES, ns), sb3),
                  pl.BlockSpec((1, ns, LANES), sb3), pl.BlockSpec((1, ns, LANES), sb3),
                  pl.BlockSpec((1, LANES), lambda s, b: (0, s)),
                  pl.BlockSpec((1, 1, S5_SEG, ns), lambda s, b: (s, nb - 1 - b, 0, 0)),
                  pl.BlockSpec((1, 1, S5_SEG, ns), lambda s, b: (s, nb - 1 - b, 0, 0)),
                  pl.BlockSpec((1, S5_SEG, ns), sb3), pl.BlockSpec((1, S5_SEG, ns), sb3)],
        out_specs=[pl.BlockSpec((rows, LANES), rev),
                   pl.BlockSpec((1, LANES, ns), sb3), pl.BlockSpec((1, LANES, ns), sb3),
                   pl.BlockSpec((1, ns, LANES), sb3), pl.BlockSpec((1, ns, LANES), sb3),
                   pl.BlockSpec((1, LANES), lambda s, b: (0, s)),
                   pl.BlockSpec((1, S5_SEG, ns), sb3), pl.BlockSpec((1, S5_SEG, ns), sb3),
                   pl.BlockSpec((1, S5_SEG, ns), sb3), pl.BlockSpec((1, S5_SEG, ns), sb3)],
        out_shape=[st((l, di), BF16), st((S5_SB, LANES, ns), F32), st((S5_SB, LANES, ns), F32),
                   st((S5_SB, ns, LANES), F32), st((S5_SB, ns, LANES), F32), st((1, di), F32),
                   st((S5_SB, S5_SEG, ns), F32), st((S5_SB, S5_SEG, ns), F32),
                   st((S5_SB, S5_SEG, ns), F32), st((S5_SB, S5_SEG, ns), F32)],
        scratch_shapes=[pltpu.VMEM((rows, ns), F32), pltpu.VMEM((rows, ns), F32),
                        pltpu.VMEM((rows, ns), F32), pltpu.VMEM((rows, ns), F32),
                        pltpu.VMEM((S5_SEG, ns), F32), pltpu.VMEM((S5_SEG, ns), F32)],
        compiler_params=_cparams(("parallel", "arbitrary")))(
        a_p, dy, lam_re, lam_im, wb_re, wb_im, wc_re, wc_im, d_skip, ck_re, ck_im, a0_re, a0_im)


def s5_carry(e_re, e_im, lam_re, lam_im, seg_len, reverse, name):
    sb, seg, ns = e_re.shape

    def body(er_ref, ei_ref, lr_ref, li_ref, cr_ref, ci_ref):
        pr, pi = lr_ref[...], li_ref[...]
        if reverse:
            pi = -pi
        for _ in range(int(math.log2(seg_len))):
            pr, pi = _cmul(pr, pi, pr, pi)
        er, ei = er_ref[...], ei_ref[...]
        row = lax.broadcasted_iota(jnp.int32, (sb, seg, ns), 1)
        cr = jnp.zeros((sb, seg, ns), F32)
        ci = jnp.zeros((sb, seg, ns), F32)
        cur_r = jnp.zeros((sb, 1, ns), F32)
        cur_i = jnp.zeros((sb, 1, ns), F32)
        order = range(seg - 2, -1, -1) if reverse else range(1, seg)
        for s in order:
            src = s + 1 if reverse else s - 1
            mr, mi = _cmul(pr, pi, cur_r, cur_i)
            cur_r = jnp.sum(jnp.where(row == src, er, 0.0), axis=1, keepdims=True) + mr
            cur_i = jnp.sum(jnp.where(row == src, ei, 0.0), axis=1, keepdims=True) + mi
            cr = jnp.where(row == s, cur_r, cr)
            ci = jnp.where(row == s, cur_i, ci)
        cr_ref[...] = cr
        ci_ref[...] = ci

    st = jax.ShapeDtypeStruct((sb, seg, ns), F32)
    return pl.pallas_call(body, name=name, out_shape=[st, st],
                          compiler_params=pltpu.CompilerParams(vmem_limit_bytes=VMEM_LIMIT_BYTES))(e_re, e_im, lam_re, lam_im)


def s5_act(y, name):
    l, d = y.shape
    tl = ROW_TILE

    def body(y_ref, o_ref):
        o_ref[...] = _gelu(y_ref[...]).astype(BF16)

    return _rowcall(body, name, l // tl, [_rows(tl, d)], _rows(tl, d), jax.ShapeDtypeStruct((l, d), BF16))(y)


def s5_gate_fwd(y, t, b_glu, a_p, name):
    l, d = y.shape
    tl = ROW_TILE

    def body(y_ref, t_ref, b_ref, z_ref, m_ref):
        yg = _gelu(y_ref[...])
        m_ref[...] = (yg * _sigmoid(t_ref[...] + b_ref[...]) * _silu(z_ref[...])).astype(BF16)

    return _rowcall(body, name, l // tl, [_rows(tl, d), _rows(tl, d), _full((1, d)), _rows(tl, d, 1)], _rows(tl, d),
                    jax.ShapeDtypeStruct((l, d), BF16))(y, t, b_glu.reshape(1, d), a_p)


def s5_gate_bwd(dm, y, t, b_glu, a_p, name):
    l, d = y.shape
    tl = ROW_TILE

    def body(dm_ref, y_ref, t_ref, b_ref, z_ref, dt_ref, dyg_ref, dz_ref, db_ref):
        i = pl.program_id(0)
        dmv = dm_ref[...]
        z = z_ref[...]
        yg = _gelu(y_ref[...])
        sg = _sigmoid(t_ref[...] + b_ref[...])
        y2 = yg * sg
        sz, gz = _silu_both(z)
        dy2 = dmv * sz
        dz_ref[...] = (dmv * y2 * gz).astype(BF16)
        dyg_ref[...] = dy2 * sg
        dt = dy2 * yg * sg * (1.0 - sg)
        dt_ref[...] = dt.astype(BF16)
        _acc(db_ref, jnp.sum(dt, axis=0, keepdims=True), i)

    st = jax.ShapeDtypeStruct
    return _rowcall(body, name, l // tl, [_rows(tl, d), _rows(tl, d), _rows(tl, d), _full((1, d)), _rows(tl, d, 1)],
                    [_rows(tl, d), _rows(tl, d), _rows(tl, d), _full((1, d))],
                    [st((l, d), BF16), st((l, d), F32), st((l, d), BF16), st((1, d), F32)])(
        dm, y, t, b_glu.reshape(1, d), a_p)


def s5_act_bwd(y, dyg_a, dyg_b, name):
    l, d = y.shape
    tl = ROW_TILE

    def body(y_ref, a_ref, b_ref, o_ref):
        o_ref[...] = (a_ref[...] + b_ref[...]) * _gelu_grad(y_ref[...])

    return _rowcall(body, name, l // tl, [_rows(tl, d)] * 3, _rows(tl, d), jax.ShapeDtypeStruct((l, d), F32))(y, dyg_a, dyg_b)


def _seg_perm(t):
    l, d = t.shape
    return t.reshape(S5_SEG, l // S5_SEG, d).transpose(1, 0, 2).reshape(l, d)


def _seg_unperm(t):
    l, d = t.shape
    return t.reshape(l // S5_SEG, S5_SEG, d).transpose(1, 0, 2).reshape(l, d)


def _s5_weights(p):
    lr, li, bbr, bbi = s5_params_fwd(p["a_re"], p["a_im"], p["log_step"], p["b_re"], p["b_im"])
    ns = 8 * S5_STATE
    lam_re = lr.reshape(S5_SB, 1, ns)
    lam_im = li.reshape(S5_SB, 1, ns)
    to_bd = lambda t: _blockdiag(t.reshape(S5_SB, 8, t.shape[1], t.shape[2]))
    wb_re = to_bd(bbr.transpose(0, 2, 1)).astype(BF16)
    wb_im = to_bd(bbi.transpose(0, 2, 1)).astype(BF16)
    wc_re = to_bd(p["c_re"].transpose(0, 2, 1)).astype(BF16)
    wc_im = to_bd(p["c_im"].transpose(0, 2, 1)).astype(BF16)
    return lam_re, lam_im, wb_re, wb_im, wc_re, wc_im


def s5_layer_fwd(h, p, wf, tag):
    l = h.shape[0]
    di = p["d_skip"].shape[0]
    hn = rms_fwd(h, p["norm_g"], tag + "_rms")
    hn_p = _seg_perm(hn)
    a_p = matmul(hn_p, wf["w_in"], "nn", tag + "_mm_in")
    sw = _s5_weights(p)
    dsk = p["d_skip"].reshape(1, di)
    e_re, e_im = s5_ends(a_p, sw[0], sw[1], sw[2], sw[3], False, tag + "_scan_ends")
    c_re, c_im = s5_carry(e_re, e_im, sw[0], sw[1], l // S5_SEG, False, tag + "_carry")
    y, ck_re, ck_im, _, _ = s5_scan_fwd(a_p, *sw, dsk, c_re, c_im, tag + "_scan")
    yg = s5_act(y, tag + "_act")
    t = matmul(yg, wf["w_glu"], "nn", tag + "_mm_glu")
    m = s5_gate_fwd(y, t, p["b_glu"], a_p, tag + "_gate")
    out_p = matmul(m, wf["w_out"], "nn", tag + "_mm_out")
    h_out = residual_add(h, _seg_unperm(out_p), tag + "_res")
    return h_out, (h, hn_p, a_p, sw, ck_re, ck_im, y, yg, t, m)


def residual_add(h, y, name):
    l, d = h.shape
    tl = ROW_TILE

    def body(h_ref, y_ref, o_ref):
        o_ref[...] = h_ref[...] + y_ref[...]

    return _rowcall(body, name, l // tl, [_rows(tl, d)] * 2, _rows(tl, d), jax.ShapeDtypeStruct((l, d), F32))(h, y)


def s5_layer_bwd(dh_out, saved, p, wf, tag, sink):
    h, hn_p, a_p, sw, ck_re, ck_im, y, yg, t, m = saved
    l = h.shape[0]
    di = p["d_skip"].shape[0]
    dsk = p["d_skip"].reshape(1, di)
    dout_p = _seg_perm(dh_out)
    dm = matmul(dout_p, wf["w_out"], "nt", tag + "_mm_dm")
    sink.mm("w_out", m, dout_p, tag + "_mm_gwout")
    dt, dyg_a, dz, db_glu = s5_gate_bwd(dm, y, t, p["b_glu"], a_p, tag + "_gate_bwd")
    dyg_b = matmul(dt, wf["w_glu"], "nt", tag + "_mm_dyg")
    sink.mm("w_glu", yg, dt, tag + "_mm_gwglu")
    dy = s5_act_bwd(y, dyg_a, dyg_b, tag + "_act_bwd")
    e_re, e_im = s5_ends(dy, sw[0], sw[1], sw[4], sw[5], True, tag + "_scanb_ends")
    c_re, c_im = s5_carry(e_re, e_im, sw[0], sw[1], l // S5_SEG, True, tag + "_carry_bwd")
    du, dwbr, dwbi, dwcr, dwci, dds, dlr, dli, _, _ = s5_scan_bwd(
        a_p, dy, *sw, dsk, ck_re, ck_im, c_re, c_im, tag + "_scanb")
    da = jnp.concatenate([du, dz], axis=1)
    dhn_p = matmul(da, wf["w_in"], "nt", tag + "_mm_dhn")
    sink.mm("w_in", hn_p, da, tag + "_mm_gwin")
    zero = sink.send()
    dh, dng = rms_bwd(h, p["norm_g"] + zero, _seg_unperm(dhn_p), dh_out, tag + "_rms_bwd")
    ex = lambda m_, r, c: _blockdiag_extract(m_, r, c).reshape(S5_GROUPS, r, c).transpose(0, 2, 1)
    dbb_re, dbb_im = ex(dwbr, S5_GROUP, S5_STATE), ex(dwbi, S5_GROUP, S5_STATE)
    g_c_re, g_c_im = ex(dwcr, S5_STATE, S5_GROUP), ex(dwci, S5_STATE, S5_GROUP)
    dl_re = lane_sum8(dlr).reshape(S5_GROUPS, S5_STATE, 1)
    dl_im = lane_sum8(dli).reshape(S5_GROUPS, S5_STATE, 1)
    gar, gai, gls, gbr, gbi = s5_params_bwd(p["a_re"], p["a_im"], p["log_step"], p["b_re"], p["b_im"],
                                            dl_re, dl_im, dbb_re, dbb_im)
    grads = {"norm_g": dng.reshape(-1), "a_re": gar.reshape(S5_GROUPS, S5_STATE),
             "a_im": gai.reshape(S5_GROUPS, S5_STATE), "log_step": gls.reshape(-1), "b_re": gbr, "b_im": gbi,
             "c_re": g_c_re, "c_im": g_c_im, "d_skip": dds.reshape(-1), "b_glu": db_glu.reshape(-1)}
    return dh, grads


def lane_sum8(t):
    sb, seg, ns = t.shape

    def body(t_ref, o_ref):
        o_ref[...] = jnp.sum(t_ref[...], axis=1, keepdims=True)

    return pl.pallas_call(body, name="s5_seg_sum", out_shape=jax.ShapeDtypeStruct((sb, 1, ns), F32))(t)


MLA_DI = MLA_HEADS * 128
MLA_CQ0 = MLA_DI
MLA_CKV0 = MLA_CQ0 + MLA_Q_RANK
MLA_KR0 = MLA_CKV0 + MLA_KV_RANK
MLA_AW = MLA_KR0 + LANES


def _rot_half(x):
    w = x.shape[-1]
    lane = lax.broadcasted_iota(jnp.int32, x.shape, x.ndim - 1)
    return jnp.where(lane % MLA_ROPE < MLA_ROPE // 2, pltpu.roll(x, w - MLA_ROPE // 2, x.ndim - 1),
                     pltpu.roll(x, MLA_ROPE // 2, x.ndim - 1))


def rope_tables(pos):
    l = pos.shape[0]
    tl = ROW_TILE
    j = np.arange(LANES) % MLA_ROPE % (MLA_ROPE // 2)
    inv_freq = (ROPE_THETA ** (-(2.0 * j) / MLA_ROPE)).astype(np.float32).reshape(1, LANES)
    sign = np.where(np.arange(LANES) % MLA_ROPE < MLA_ROPE // 2, -1.0, 1.0).astype(np.float32).reshape(1, LANES)

    def body(p_ref, f_ref, s_ref, cos_ref, sin_ref):
        ang = p_ref[...].astype(F32) * f_ref[...]
        cos_ref[...] = jnp.cos(ang)
        sin_ref[...] = jnp.sin(ang) * s_ref[...]

    st = jax.ShapeDtypeStruct((l, LANES), F32)
    return _rowcall(body, "rope_tables", l // tl, [_rows(tl, 1), _full((1, LANES)), _full((1, LANES))],
                    [_rows(tl, LANES)] * 2, [st, st])(pos, jnp.asarray(inv_freq), jnp.asarray(sign))


def _rope(x, cos, sins):
    return x * cos + _rot_half(x) * sins


def _rope_t(dy, cos, sins):
    return dy * cos - sins * _rot_half(dy)


def _rmsn(x):
    r = lax.rsqrt(jnp.mean(x * x, axis=-1, keepdims=True) + NORM_EPS)
    return x * r, r


def mla_pre(a, q_g, kv_g, cos, sins, name):
    l = a.shape[0]
    tl = ROW_TILE

    def body(a_ref, qg_ref, kg_ref, cos_ref, sin_ref, cq_ref, ckv_ref, krs_ref):
        xq, _ = _rmsn(a_ref[:, MLA_CQ0:MLA_CKV0])
        cq_ref[...] = (xq * qg_ref[...]).astype(BF16)
        xk, _ = _rmsn(a_ref[:, MLA_CKV0:MLA_KR0])
        ckv_ref[...] = (xk * kg_ref[...]).astype(BF16)
        kr = a_ref[:, MLA_KR0:MLA_AW]
        kr2 = kr + pltpu.roll(kr, MLA_ROPE, 1)
        kr2 = _rope(kr2, cos_ref[...], sin_ref[...])
        lane = lax.broadcasted_iota(jnp.int32, kr2.shape, 1)
        krs_ref[0] = jnp.where(lane < MLA_ROPE, kr2, 0.0).astype(BF16)
        krs_ref[1] = jnp.where(lane >= MLA_ROPE, kr2, 0.0).astype(BF16)

    st = jax.ShapeDtypeStruct
    return _rowcall(body, name, l // tl,
                    [_rows(tl, MLA_AW), _full((1, MLA_Q_RANK)), _full((1, MLA_KV_RANK)), _rows(tl, LANES), _rows(tl, LANES)],
                    [_rows(tl, MLA_Q_RANK), _rows(tl, MLA_KV_RANK), pl.BlockSpec((2, tl, LANES), lambda i: (0, i, 0))],
                    [st((l, MLA_Q_RANK), BF16), st((l, MLA_KV_RANK), BF16), st((2, l, LANES), BF16)])(
        a, q_g.reshape(1, -1), kv_g.reshape(1, -1), cos, sins)


def mla_rope_q(qr, cos, sins, name):
    l, w = qr.shape
    tl = ROW_TILE

    def body(q_ref, cos_ref, sin_ref, o_ref):
        c, s = cos_ref[...], sin_ref[...]
        for p in range(w // LANES):
            sl = slice(p * LANES, (p + 1) * LANES)
            o_ref[:, sl] = _rope(q_ref[:, sl], c, s).astype(BF16)

    return _rowcall(body, name, l // tl, [_rows(tl, w), _rows(tl, LANES), _rows(tl, LANES)], _rows(tl, w),
                    jax.ShapeDtypeStruct((l, w), BF16))(qr, cos, sins)


ATT_OUT = 512
ATT_IN = 512
ATT_R = ATT_OUT // ATT_IN


def _scores(qn, qr, kn, kr, mask_off, transposed):
    q2 = jnp.concatenate([qn, qr], axis=1)
    k2 = jnp.concatenate([kn, kr], axis=1)
    s = (_dot(k2, q2, DN_NT) if transposed else _dot(q2, k2, DN_NT)) * MLA_SCALE
    if mask_off is None:
        return s
    r = lax.broadcasted_iota(jnp.int32, s.shape, 0)
    c = lax.broadcasted_iota(jnp.int32, s.shape, 1)
    return jnp.where((r <= c + mask_off) if transposed else (c + mask_off <= r), s, NEG_INF)


def _fold(x, op):
    out = x[:, :LANES]
    for t in range(1, x.shape[1] // LANES):
        out = op(out, x[:, t * LANES:(t + 1) * LANES])
    return out


def flash_fwd(qn, qr, kv, krs, name):
    l = qn.shape[0]
    nq = l // ATT_OUT

    def body(qn_ref, qr_ref, kv_ref, kr_ref, o_ref, lse_ref, s_buf):
        qi = pl.program_id(1)
        q_r = qr_ref[...]
        q_n = [qn_ref[:, hh * LANES:(hh + 1) * LANES] for hh in range(2)]

        def block_scores(j, mx, mask_off):
            sl = pl.ds(pl.multiple_of(j * ATT_IN, ATT_IN), ATT_IN)
            out = []
            for hh in range(2):
                s = _scores(q_n[hh], q_r, kv_ref[sl, 2 * hh * LANES:(2 * hh + 1) * LANES], kr_ref[hh, sl, :],
                            mask_off, False)
                s_buf[hh, j] = s
                out.append(jnp.maximum(mx[hh], _fold(s, jnp.maximum)))
            return tuple(out)

        ninf = jnp.full((ATT_OUT, LANES), NEG_INF, F32)
        mx = lax.fori_loop(0, ATT_R * qi, lambda j, c: block_scores(j, c, None), (ninf, ninf))
        for d in range(ATT_R):
            mx = block_scores(ATT_R * qi + d, mx, d * ATT_IN)
        m = [jnp.max(mx[hh], axis=-1, keepdims=True) for hh in range(2)]

        def block_pv(j, carry):
            sl = pl.ds(pl.multiple_of(j * ATT_IN, ATT_IN), ATT_IN)
            out = []
            for hh in range(2):
                ls, acc = carry[hh]
                p = jnp.exp(s_buf[hh, j] - m[hh])
                out.append((ls + _fold(p, jnp.add),
                            acc + _dot(p, kv_ref[sl, (2 * hh + 1) * LANES:(2 * hh + 2) * LANES], DN_NN)))
            return tuple(out)

        z = jnp.zeros((ATT_OUT, LANES), F32)
        res = lax.fori_loop(0, ATT_R * (qi + 1), block_pv, ((z, z), (z, z)))
        for hh in range(2):
            lsum = jnp.sum(res[hh][0], axis=-1, keepdims=True)
            o_ref[:, hh * LANES:(hh + 1) * LANES] = res[hh][1] / lsum
            lse_ref[hh] = m[hh] + jnp.log(lsum)

    st = jax.ShapeDtypeStruct
    return pl.pallas_call(
        body, name=name, grid=(MLA_HEADS // 2, nq),
        in_specs=[pl.BlockSpec((ATT_OUT, 2 * LANES), lambda p, i: (i, p)),
                  pl.BlockSpec((ATT_OUT, LANES), lambda p, i: (i, p)),
                  pl.BlockSpec((l, 4 * LANES), lambda p, i: (0, p)),
                  pl.BlockSpec((2, l, LANES), lambda p, i: (0, 0, 0))],
        out_specs=[pl.BlockSpec((ATT_OUT, 2 * LANES), lambda p, i: (i, p)),
                   pl.BlockSpec((2, ATT_OUT, 1), lambda p, i: (p, i, 0))],
        out_shape=[st((l, MLA_DI), F32), st((MLA_HEADS, l, 1), F32)],
        scratch_shapes=[pltpu.VMEM((2, l // ATT_IN, ATT_OUT, ATT_IN), F32)],
        compiler_params=_cparams(("parallel", "arbitrary")))(qn, qr, kv, krs)


def flash_dkv(qn, qr, kv, krs, do, lse_row, delta_row, name):
    l = qn.shape[0]
    nk = l // ATT_OUT
    nq = l // ATT_IN

    def body(qn_ref, qr_ref, do_ref, lse_ref, dl_ref, kv_ref, kr_ref, dkv_ref, dkr_ref):
        kj = pl.program_id(1)
        lane = lax.broadcasted_iota(jnp.int32, (ATT_OUT, LANES), 1)
        kn = [kv_ref[:, 2 * hh * LANES:(2 * hh + 1) * LANES] for hh in range(2)]
        v = [kv_ref[:, (2 * hh + 1) * LANES:(2 * hh + 2) * LANES] for hh in range(2)]

        def block(i, carry, mask_off):
            sl = pl.ds(pl.multiple_of(i * ATT_IN, ATT_IN), ATT_IN)
            q_r = qr_ref[sl, :]
            out = []
            for hh in range(2):
                dk2, dv = carry[hh]
                hs = slice(hh * LANES, (hh + 1) * LANES)
                q_n, d_o = qn_ref[sl, hs], do_ref[sl, hs]
                s = _scores(q_n, q_r, kn[hh], kr_ref[hh], mask_off, True)
                pt = jnp.exp(s - lse_ref[hh, i])
                dv = dv + _dot(pt, d_o, DN_NN)
                dpt = _dot(v[hh], d_o, DN_NT)
                dst = (pt * (dpt - dl_ref[hh, i]) * MLA_SCALE).astype(BF16)
                out.append((dk2 + _dot(dst, jnp.concatenate([q_n, q_r], axis=1), DN_NN), dv))
            return tuple(out)

        z = jnp.zeros((ATT_OUT, LANES), F32)
        z2 = jnp.zeros((ATT_OUT, 2 * LANES), F32)
        res = ((z2, z), (z2, z))
        for d in range(ATT_R):
            res = block(ATT_R * kj + d, res, d * ATT_IN)
        res = lax.fori_loop(ATT_R * (kj + 1), nq, lambda i, c: block(i, c, None), res)
        for hh in range(2):
            dkv_ref[:, 2 * hh * LANES:(2 * hh + 1) * LANES] = res[hh][0][:, :LANES].astype(BF16)
            dkv_ref[:, (2 * hh + 1) * LANES:(2 * hh + 2) * LANES] = res[hh][1].astype(BF16)
        dkr_ref[0] = jnp.where(lane < MLA_ROPE, res[0][0][:, LANES:], res[1][0][:, LANES:])

    st = jax.ShapeDtypeStruct
    return pl.pallas_call(
        body, name=name, grid=(MLA_HEADS // 2, nk),
        in_specs=[pl.BlockSpec((l, 2 * LANES), lambda p, j: (0, p)),
                  pl.BlockSpec((l, LANES), lambda p, j: (0, p)),
                  pl.BlockSpec((l, 2 * LANES), lambda p, j: (0, p)),
                  pl.BlockSpec((2, nq, 1, ATT_IN), lambda p, j: (p, 0, 0, 0)),
                  pl.BlockSpec((2, nq, 1, ATT_IN), lambda p, j: (p, 0, 0, 0)),
                  pl.BlockSpec((ATT_OUT, 4 * LANES), lambda p, j: (j, p)),
                  pl.BlockSpec((2, ATT_OUT, LANES), lambda p, j: (0, j, 0))],
        out_specs=[pl.BlockSpec((ATT_OUT, 4 * LANES), lambda p, j: (j, p)),
                   pl.BlockSpec((1, ATT_OUT, LANES), lambda p, j: (p, j, 0))],
        out_shape=[st((l, 2 * MLA_DI), BF16), st((MLA_HEADS // 2, l, LANES), F32)],
        compiler_params=_cparams(("parallel", "arbitrary")))(qn, qr, do, lse_row, delta_row, kv, krs)


def flash_dq(qn, qr, kv, krs, do, lse, delta, cos, sins, name):
    l = qn.shape[0]
    nq = l // ATT_OUT

    def body(qn_ref, qr_ref, do_ref, lse_ref, dl_ref, kv_ref, kr_ref, cos_ref, sin_ref, dqn_ref, dqr_ref):
        qi = pl.program_id(1)
        q_r = qr_ref[...]
        q_n = [qn_ref[:, hh * LANES:(hh + 1) * LANES] for hh in range(2)]
        d_o = [do_ref[:, hh * LANES:(hh + 1) * LANES] for hh in range(2)]
        lse_h = [lse_ref[hh] for hh in range(2)]
        dl_h = [dl_ref[hh] for hh in range(2)]

        def block(j, carry, mask_off):
            sl = pl.ds(pl.multiple_of(j * ATT_IN, ATT_IN), ATT_IN)
            dq2 = list(carry)
            for hh in range(2):
                kn = kv_ref[sl, 2 * hh * LANES:(2 * hh + 1) * LANES]
                v = kv_ref[sl, (2 * hh + 1) * LANES:(2 * hh + 2) * LANES]
                kr = kr_ref[hh, sl, :]
                s = _scores(q_n[hh], q_r, kn, kr, mask_off, False)
                pr = jnp.exp(s - lse_h[hh])
                dp = _dot(d_o[hh], v, DN_NT)
                ds = (pr * (dp - dl_h[hh]) * MLA_SCALE).astype(BF16)
                dq2[hh] = dq2[hh] + _dot(ds, jnp.concatenate([kn, kr], axis=1), DN_NN)
            return tuple(dq2)

        z2 = jnp.zeros((ATT_OUT, 2 * LANES), F32)
        res = lax.fori_loop(0, ATT_R * qi, lambda j, c: block(j, c, None), (z2, z2))
        for d in range(ATT_R):
            res = block(ATT_R * qi + d, res, d * ATT_IN)
        dqn_ref[:, 0:LANES] = res[0][:, :LANES].astype(BF16)
        dqn_ref[:, LANES:2 * LANES] = res[1][:, :LANES].astype(BF16)
        dqr = res[0][:, LANES:] + res[1][:, LANES:]
        dqr_ref[...] = _rope_t(dqr, cos_ref[...], sin_ref[...]).astype(BF16)

    st = jax.ShapeDtypeStruct
    return pl.pallas_call(
        body, name=name, grid=(MLA_HEADS // 2, nq),
        in_specs=[pl.BlockSpec((ATT_OUT, 2 * LANES), lambda p, i: (i, p)),
                  pl.BlockSpec((ATT_OUT, LANES), lambda p, i: (i, p)),
                  pl.BlockSpec((ATT_OUT, 2 * LANES), lambda p, i: (i, p)),
                  pl.BlockSpec((2, ATT_OUT, 1), lambda p, i: (p, i, 0)),
                  pl.BlockSpec((2, ATT_OUT, 1), lambda p, i: (p, i, 0)),
                  pl.BlockSpec((l, 4 * LANES), lambda p, i: (0, p)),
                  pl.BlockSpec((2, l, LANES), lambda p, i: (0, 0, 0)),
                  pl.BlockSpec((ATT_OUT, LANES), lambda p, i: (i, 0)),
                  pl.BlockSpec((ATT_OUT, LANES), lambda p, i: (i, 0))],
        out_specs=[pl.BlockSpec((ATT_OUT, 2 * LANES), lambda p, i: (i, p)),
                   pl.BlockSpec((ATT_OUT, LANES), lambda p, i: (i, p))],
        out_shape=[st((l, MLA_DI), BF16), st((l, MLA_HEADS * MLA_ROPE), BF16)],
        compiler_params=_cparams(("parallel", "arbitrary")))(qn, qr, do, lse, delta, kv, krs, cos, sins)


def mla_gate_fwd(o, a, name):
    l = o.shape[0]
    tl = ROW_TILE

    def body(o_ref, z_ref, m_ref):
        m_ref[...] = (o_ref[...] * _silu(z_ref[...])).astype(BF16)

    return _rowcall(body, name, l // tl, [_rows(tl, MLA_DI), _rows(tl, MLA_DI)], _rows(tl, MLA_DI),
                    jax.ShapeDtypeStruct((l, MLA_DI), BF16))(o, a)


def mla_gate_bwd(dm, o, a, name):
    l = o.shape[0]
    tl = ROW_TILE

    def body(dm_ref, o_ref, z_ref, do_ref, dz_ref, dl_ref):
        dmv, ov, z = dm_ref[...], o_ref[...], z_ref[...]
        sz, gz = _silu_both(z)
        d_o = dmv * sz
        do_ref[...] = d_o.astype(BF16)
        dz_ref[...] = (dmv * ov * gz).astype(BF16)
        pr = d_o * ov
        for h in range(MLA_HEADS):
            dl_ref[h] = jnp.sum(pr[:, h * LANES:(h + 1) * LANES], axis=1, keepdims=True)

    st = jax.ShapeDtypeStruct
    return _rowcall(body, name, l // tl, [_rows(tl, MLA_DI)] * 3,
                    [_rows(tl, MLA_DI), _rows(tl, MLA_DI), pl.BlockSpec((MLA_HEADS, tl, 1), lambda i: (0, i, 0))],
                    [st((l, MLA_DI), BF16), st((l, MLA_DI), BF16), st((MLA_HEADS, l, 1), F32)])(dm, o, a)


def mla_post(a, dcqn, dckvn, dkr_pairs, dz, q_g, kv_g, cos, sins, name):
    l = a.shape[0]
    tl = ROW_TILE
    npair = MLA_HEADS // 2

    def norm_bwd(x, g, dy):
        xhat, r = _rmsn(x)
        dxh = dy * g
        return r * (dxh - xhat * jnp.mean(dxh * xhat, axis=-1, keepdims=True)), jnp.sum(dy * xhat, axis=0, keepdims=True)

    def body(a_ref, dq_ref, dk_ref, dkr_ref, dz_ref, qg_ref, kg_ref, cos_ref, sin_ref, da_ref, dqg_ref, dkg_ref):
        i = pl.program_id(0)
        da_ref[:, 0:MLA_DI] = dz_ref[...]
        dcq, dqg = norm_bwd(a_ref[:, MLA_CQ0:MLA_CKV0], qg_ref[...], dq_ref[...])
        da_ref[:, MLA_CQ0:MLA_CKV0] = dcq.astype(BF16)
        dckv, dkg = norm_bwd(a_ref[:, MLA_CKV0:MLA_KR0], kg_ref[...], dk_ref[...])
        da_ref[:, MLA_CKV0:MLA_KR0] = dckv.astype(BF16)
        dk2 = dkr_ref[0]
        for p in range(1, npair):
            dk2 = dk2 + dkr_ref[p]
        dk2 = _rope_t(dk2, cos_ref[...], sin_ref[...])
        dk2 = dk2 + pltpu.roll(dk2, MLA_ROPE, 1)
        lane = lax.broadcasted_iota(jnp.int32, dk2.shape, 1)
        da_ref[:, MLA_KR0:MLA_AW] = jnp.where(lane < MLA_ROPE, dk2, 0.0).astype(BF16)
        _acc(dqg_ref, dqg, i)
        _acc(dkg_ref, dkg, i)

    st = jax.ShapeDtypeStruct
    return _rowcall(body, name, l // tl,
                    [_rows(tl, MLA_AW), _rows(tl, MLA_Q_RANK), _rows(tl, MLA_KV_RANK),
                     pl.BlockSpec((npair, tl, LANES), lambda i: (0, i, 0)), _rows(tl, MLA_DI),
                     _full((1, MLA_Q_RANK)), _full((1, MLA_KV_RANK)), _rows(tl, LANES), _rows(tl, LANES)],
                    [_rows(tl, MLA_AW), _full((1, MLA_Q_RANK)), _full((1, MLA_KV_RANK))],
                    [st((l, MLA_AW), BF16), st((1, MLA_Q_RANK), F32), st((1, MLA_KV_RANK), F32)])(
        a, dcqn, dckvn, dkr_pairs, dz, q_g.reshape(1, -1), kv_g.reshape(1, -1), cos, sins)


def _mla_w_in_perm(w):
    r = MLA_Q_RANK + MLA_KV_RANK + MLA_ROPE
    pad = jnp.zeros(w.shape[:-1] + (MLA_AW - MLA_KR0 - MLA_ROPE,), w.dtype)
    return jnp.concatenate([w[..., r:], w[..., :r], pad], axis=-1)


def _mla_w_in_unperm(g):
    r = MLA_Q_RANK + MLA_KV_RANK + MLA_ROPE
    return jnp.concatenate([g[..., MLA_DI:MLA_DI + r], g[..., :MLA_DI]], axis=-1)


def _mla_w_uq_split(w):
    k = w.shape[0]
    w3 = w.reshape(k, MLA_HEADS, MLA_NOPE + MLA_ROPE)
    return w3[:, :, :MLA_NOPE].reshape(k, MLA_HEADS * MLA_NOPE), w3[:, :, MLA_NOPE:].reshape(k, MLA_HEADS * MLA_ROPE)


def _mla_w_uq_merge(gn, gr):
    k = gn.shape[0]
    return jnp.concatenate([gn.reshape(k, MLA_HEADS, MLA_NOPE), gr.reshape(k, MLA_HEADS, MLA_ROPE)], axis=2).reshape(k, -1)


def mla_layer_fwd(h, p, wf, cos, sins, tag):
    hn = rms_fwd(h, p["norm_g"], tag + "_rms")
    w_in = _mla_w_in_perm(wf["w_in"])
    w_uq_n, w_uq_r = _mla_w_uq_split(wf["w_uq"])
    a = matmul(hn, w_in, "nn", tag + "_mm_in")
    cqn, ckvn, krs = mla_pre(a, p["q_norm_g"], p["kv_norm_g"], cos, sins, tag + "_pre")
    qn = matmul(cqn, w_uq_n, "nn", tag + "_mm_qn", out_dtype=BF16)
    qr_raw = matmul(cqn, w_uq_r, "nn", tag + "_mm_qr")
    qr = mla_rope_q(qr_raw, cos, sins, tag + "_rope_q")
    kv = matmul(ckvn, wf["w_ukv"], "nn", tag + "_mm_kv", out_dtype=BF16)
    o, lse = flash_fwd(qn, qr, kv, krs, tag + "_flash")
    m = mla_gate_fwd(o, a, tag + "_gate")
    h_out = matmul(m, wf["w_out"], "nn", tag + "_mm_out", add=h)
    return h_out, (h, hn, a, cqn, ckvn, krs, qn, qr, kv, o, lse, m, w_in, w_uq_n, w_uq_r)


def mla_layer_bwd(dh_out, saved, p, wf, cos, sins, tag, sink):
    h, hn, a, cqn, ckvn, krs, qn, qr, kv, o, lse, m, w_in, w_uq_n, w_uq_r = saved
    l = h.shape[0]
    dm = matmul(dh_out, wf["w_out"], "nt", tag + "_mm_dm")
    sink.mm("w_out", m, dh_out, tag + "_mm_gwout")
    do, dz, delta = mla_gate_bwd(dm, o, a, tag + "_gate_bwd")
    lse_row = lse.reshape(MLA_HEADS, l // ATT_IN, 1, ATT_IN)
    delta_row = delta.reshape(MLA_HEADS, l // ATT_IN, 1, ATT_IN)
    dkv, dkr_pairs = flash_dkv(qn, qr, kv, krs, do, lse_row, delta_row, tag + "_flash_dkv")
    dqn, dqr = flash_dq(qn, qr, kv, krs, do, lse, delta, cos, sins, tag + "_flash_dq")
    dcqn = matmul(dqn, w_uq_n, "nt", tag + "_mm_dcq_n")
    dcqn = matmul(dqr, w_uq_r, "nt", tag + "_mm_dcq_r", add=dcqn)
    g_uq_n = matmul(cqn, dqn, "tn", tag + "_mm_guq_n")
    g_uq_r = matmul(cqn, dqr, "tn", tag + "_mm_guq_r")
    dckvn = matmul(dkv, wf["w_ukv"], "nt", tag + "_mm_dckv")
    sink.mm("w_ukv", ckvn, dkv, tag + "_mm_gukv")
    da, dqg, dkg = mla_post(a, dcqn, dckvn, dkr_pairs, dz, p["q_norm_g"], p["kv_norm_g"], cos, sins, tag + "_post")
    dhn = matmul(da, w_in, "nt", tag + "_mm_dhn")
    g_w_in = matmul(hn, da, "tn", tag + "_mm_gwin")
    sink.put("w_uq", _mla_w_uq_merge(g_uq_n, g_uq_r))
    sink.put("w_in", _mla_w_in_unperm(g_w_in))
    zero = sink.send()
    dh, dng = rms_bwd(h, p["norm_g"] + zero, dhn, dh_out, tag + "_rms_bwd")
    grads = {"norm_g": dng.reshape(-1), "q_norm_g": dqg.reshape(-1), "kv_norm_g": dkg.reshape(-1)}
    return dh, grads


ANY = pl.BlockSpec(memory_space=pl.ANY)


def _me():
    return lax.axis_index("x"), lax.axis_index("y"), lax.axis_index("c")


def _chip():
    return 2 * lax.axis_index("x") + lax.axis_index("y")


def _other_chips(x, y):
    return [(1 - x, y), (x, 1 - y), (1 - x, 1 - y)]


def _rcopy(src, dst, ssem, rsem, dev):
    return pltpu.make_async_remote_copy(src_ref=src, dst_ref=dst, send_sem=ssem, recv_sem=rsem,
                                        device_id=dev, device_id_type=MESH)


def _half(ref, c, hf):
    return ref.at[pl.ds(c * hf, hf), :]


def weights_allgather(wb):
    nr, w = wb.shape
    hf = nr // 2

    def body(w_ref, o_ref, ssem, rsem):
        x, y, c = _me()
        k = 2 * x + y
        chips = _other_chips(x, y)
        first = [_rcopy(_half(w_ref, c, hf), _half(o_ref.at[k], c, hf), ssem.at[j], rsem.at[j], (cx, cy, c))
                 for j, (cx, cy) in enumerate(chips)]
        for cp in first:
            cp.start()
        passed = []
        for j, (cx, cy) in enumerate(chips):
            region = _half(o_ref.at[2 * cx + cy], c, hf)
            _rcopy(region, region, ssem.at[j], rsem.at[j], (cx, cy, c)).wait_recv()
            fwd = _rcopy(region, region, ssem.at[3 + j], rsem.at[3 + j], (x, y, 1 - c))
            fwd.start()
            passed.append(fwd)
        for j, (cx, cy) in enumerate(chips):
            region = _half(o_ref.at[2 * cx + cy], 1 - c, hf)
            _rcopy(region, region, ssem.at[3 + j], rsem.at[3 + j], (x, y, 1 - c)).wait_recv()
        for cp in first + passed:
            cp.wait_send()

    out = pl.pallas_call(
        body, name="weights_allgather", in_specs=[ANY], out_specs=ANY,
        out_shape=jax.ShapeDtypeStruct((N_CHIPS, nr, w), wb.dtype),
        scratch_shapes=[pltpu.SemaphoreType.DMA((6,)), pltpu.SemaphoreType.DMA((6,))],
    )(wb)
    return lax.dynamic_update_slice(out, wb[None], (_chip(), 0, 0))


HBM = pl.BlockSpec(memory_space=pltpu.HBM)
SEM = pl.BlockSpec(memory_space=pltpu.SEMAPHORE)
SPLIT_EFFECT = pltpu.SideEffectType.DATAFLOW_SIDE_EFFECTING


def gather_start(wb, after, name):
    nr, w = wb.shape
    hf = nr // 2

    def body(w_ref, land_ref, after_ref, ssem, rsem, w_thru, land_thru, token):
        x, y, c = _me()
        k = 2 * x + y
        for j, (cx, cy) in enumerate(_other_chips(x, y)):
            _rcopy(_half(w_ref, c, hf), _half(land_ref.at[k], c, hf), ssem.at[j], rsem.at[j], (cx, cy, c)).start()
        token[...] = jnp.zeros_like(token)

    land = lax.empty((N_CHIPS, nr, w), wb.dtype)
    return pl.pallas_call(
        body, name=name,
        out_shape=(pltpu.SemaphoreType.DMA((3,)), pltpu.SemaphoreType.DMA((3,)), pltpu.HBM(wb.shape, wb.dtype),
                   pltpu.HBM(land.shape, land.dtype), jax.ShapeDtypeStruct((8, LANES), F32)),
        in_specs=(HBM, HBM, ANY), out_specs=(SEM, SEM, HBM, HBM, pl.BlockSpec(memory_space=pltpu.VMEM)),
        input_output_aliases={0: 2, 1: 3},
        compiler_params=pltpu.CompilerParams(has_side_effects=SPLIT_EFFECT))(
        pltpu.with_memory_space_constraint(wb, pltpu.HBM), pltpu.with_memory_space_constraint(land, pltpu.HBM), after)


def gather_wait(ssem, rsem, w_thru, land_thru, after, name):
    nr, w = w_thru.shape
    hf = nr // 2

    def body(w_ref, land_ref, ssem_ref, rsem_ref, after_ref, w_dead, got_ref):
        x, y, c = _me()
        for j, (cx, cy) in enumerate(_other_chips(x, y)):
            cp = _rcopy(_half(w_ref, c, hf), _half(land_ref.at[2 * cx + cy], c, hf), ssem_ref.at[j], rsem_ref.at[j],
                        (cx, cy, c))
            cp.wait_send()
            cp.wait_recv()

    return pl.pallas_call(
        body, name=name, out_shape=(pltpu.HBM(w_thru.shape, w_thru.dtype), pltpu.HBM(land_thru.shape, land_thru.dtype)),
        in_specs=(HBM, HBM, SEM, SEM, ANY), out_specs=(HBM, HBM), input_output_aliases={0: 0, 1: 1},
        compiler_params=pltpu.CompilerParams(has_side_effects=SPLIT_EFFECT))(w_thru, land_thru, ssem, rsem, after)[1]


def gather_handover(land, wb, name):
    _, nr, w = land.shape
    hf = nr // 2

    def body(l_ref, o_ref, ssem, rsem):
        x, y, c = _me()
        chips = _other_chips(x, y)
        sends = []
        for j, (cx, cy) in enumerate(chips):
            region = _half(o_ref.at[2 * cx + cy], c, hf)
            sends.append(_rcopy(region, region, ssem.at[j], rsem.at[j], (x, y, 1 - c)))
            sends[-1].start()
        for j, (cx, cy) in enumerate(chips):
            region = _half(o_ref.at[2 * cx + cy], 1 - c, hf)
            _rcopy(region, region, ssem.at[j], rsem.at[j], (x, y, 1 - c)).wait_recv()
        for cp in sends:
            cp.wait_send()

    out = pl.pallas_call(
        body, name=name, in_specs=[ANY], out_specs=ANY, input_output_aliases={0: 0},
        out_shape=jax.ShapeDtypeStruct(land.shape, land.dtype),
        scratch_shapes=[pltpu.SemaphoreType.DMA((3,)), pltpu.SemaphoreType.DMA((3,))])(land)
    return lax.dynamic_update_slice(out, wb[None], (_chip(), 0, 0))


def reduce_start(t, after, name):
    def body(t_ref, land_ref, after_ref, ssem, rsem, t_thru, land_thru, token):
        x, y, c = _me()
        k = 2 * x + y
        for j, (cx, cy) in enumerate(_other_chips(x, y)):
            _rcopy(t_ref.at[2 * cx + cy], land_ref.at[k], ssem.at[j], rsem.at[j], (cx, cy, c)).start()
        token[...] = jnp.zeros_like(token)

    land = lax.empty(t.shape, t.dtype)
    return pl.pallas_call(
        body, name=name,
        out_shape=(pltpu.SemaphoreType.DMA((3,)), pltpu.SemaphoreType.DMA((3,)), pltpu.HBM(t.shape, t.dtype),
                   pltpu.HBM(t.shape, t.dtype), jax.ShapeDtypeStruct((8, LANES), F32)),
        in_specs=(HBM, HBM, ANY), out_specs=(SEM, SEM, HBM, HBM, pl.BlockSpec(memory_space=pltpu.VMEM)),
        input_output_aliases={0: 2, 1: 3},
        compiler_params=pltpu.CompilerParams(has_side_effects=SPLIT_EFFECT))(
        pltpu.with_memory_space_constraint(t, pltpu.HBM), pltpu.with_memory_space_constraint(land, pltpu.HBM), after)


def reduce_wait(ssem, rsem, t_thru, land_thru, after, name):
    def body(t_ref, land_ref, ssem_ref, rsem_ref, after_ref, t_out, got_ref):
        x, y, c = _me()
        k = 2 * x + y
        for j, (cx, cy) in enumerate(_other_chips(x, y)):
            cp = _rcopy(t_ref.at[k], land_ref.at[2 * cx + cy], ssem_ref.at[j], rsem_ref.at[j], (cx, cy, c))
            cp.wait_send()
            cp.wait_recv()

    return pl.pallas_call(
        body, name=name, out_shape=(pltpu.HBM(t_thru.shape, t_thru.dtype), pltpu.HBM(land_thru.shape, land_thru.dtype)),
        in_specs=(HBM, HBM, SEM, SEM, ANY), out_specs=(HBM, HBM), input_output_aliases={0: 0, 1: 1},
        compiler_params=pltpu.CompilerParams(has_side_effects=SPLIT_EFFECT))(t_thru, land_thru, ssem, rsem, after)


def grads_to_sibling(ps, name="grads_to_sibling"):
    n = len(ps)

    def body(*refs):
        p_refs, o_refs, ssem, rsem = refs[:n], refs[n:2 * n], refs[2 * n], refs[2 * n + 1]
        x, y, c = _me()
        cps = []
        for a in range(n):
            hf = ps[a].shape[1] // 2
            cps.append(_rcopy(p_refs[a].at[:, pl.ds((1 - c) * hf, hf), :], o_refs[a], ssem.at[a], rsem.at[a],
                              (x, y, 1 - c)))
        for cp in cps:
            cp.start()
        for cp in cps:
            cp.wait()

    return pl.pallas_call(
        body, name=name, in_specs=[ANY] * n, out_specs=[ANY] * n,
        out_shape=[jax.ShapeDtypeStruct((N_CHIPS, p.shape[1] // 2, p.shape[2]), p.dtype) for p in ps],
        scratch_shapes=[pltpu.SemaphoreType.DMA((n,)), pltpu.SemaphoreType.DMA((n,))])(*ps)


def pair_sum(p, ra, out_dtype, name):
    _, nr, w = p.shape
    hf = nr // 2
    tr = _pick_rows(hf)
    nb = hf // tr

    def body(c_ref, p_ref, r_ref, o_ref):
        o_ref[...] = (p_ref[...] + r_ref[...]).astype(out_dtype)

    c = lax.axis_index("c").astype(jnp.int32).reshape(1)
    return pl.pallas_call(
        body, name=name,
        grid_spec=pltpu.PrefetchScalarGridSpec(
            num_scalar_prefetch=1, grid=(N_CHIPS, nb),
            in_specs=[pl.BlockSpec((1, tr, w), lambda k, i, c_ref: (k, c_ref[0] * nb + i, 0)),
                      pl.BlockSpec((1, tr, w), lambda k, i, c_ref: (k, i, 0))],
            out_specs=pl.BlockSpec((1, tr, w), lambda k, i, c_ref: (k, i, 0))),
        out_shape=jax.ShapeDtypeStruct((N_CHIPS, hf, w), out_dtype),
        compiler_params=_cparams(("parallel", "parallel")))(c, p, ra)


def grads_across_chips(ts):
    n = len(ts)

    def body(*refs):
        t_refs, o_refs, ssem, rsem = refs[:n], refs[n:2 * n], refs[2 * n], refs[2 * n + 1]
        x, y, c = _me()
        k = 2 * x + y
        chips = _other_chips(x, y)
        sends = [_rcopy(t_refs[a].at[2 * cx + cy], o_refs[a].at[k], ssem.at[3 * a + j], rsem.at[3 * a + j], (cx, cy, c))
                 for a in range(n) for j, (cx, cy) in enumerate(chips)]
        for cp in sends:
            cp.start()
        for a in range(n):
            for j, (cx, cy) in enumerate(chips):
                _rcopy(t_refs[a].at[k], o_refs[a].at[2 * cx + cy], ssem.at[3 * a + j], rsem.at[3 * a + j],
                       (cx, cy, c)).wait_recv()
        for cp in sends:
            cp.wait_send()

    return pl.pallas_call(
        body, name="grads_across_chips", in_specs=[ANY] * n, out_specs=[ANY] * n,
        out_shape=[jax.ShapeDtypeStruct(t.shape, t.dtype) for t in ts],
        scratch_shapes=[pltpu.SemaphoreType.DMA((3 * n,)), pltpu.SemaphoreType.DMA((3 * n,))])(*ts)


def chip_sum(t, rb, name):
    _, hf, w = rb.shape
    tr = _pick_rows(hf)
    nb = hf // tr

    def body(kc_ref, t_ref, r_ref, o_ref):
        k = kc_ref[0]
        acc = jnp.where(k == 0, t_ref[0], r_ref[0]).astype(F32)
        for j in range(1, N_CHIPS):
            acc = acc + jnp.where(k == j, t_ref[0], r_ref[j]).astype(F32)
        o_ref[...] = acc

    kc = jnp.stack([_chip(), lax.axis_index("c")]).astype(jnp.int32)
    return pl.pallas_call(
        body, name=name,
        grid_spec=pltpu.PrefetchScalarGridSpec(
            num_scalar_prefetch=1, grid=(nb,),
            in_specs=[pl.BlockSpec((1, tr, w), lambda i, kc_ref: (kc_ref[0], i, 0)),
                      pl.BlockSpec((N_CHIPS, tr, w), lambda i, kc_ref: (0, i, 0))],
            out_specs=pl.BlockSpec((tr, w), lambda i, kc_ref: (kc_ref[1] * nb + i, 0))),
        out_shape=jax.ShapeDtypeStruct((2 * hf, w), F32), compiler_params=_cparams(("parallel",)))(kc, t, rb)


def reduced_to_sibling(gs):
    n = len(gs)

    def body(*refs):
        o_refs, ssem, rsem = refs[n:2 * n], refs[2 * n], refs[2 * n + 1]
        x, y, c = _me()
        cps = []
        for a in range(n):
            hf = gs[a].shape[0] // 2
            cps.append(_rcopy(_half(o_refs[a], c, hf), _half(o_refs[a], c, hf), ssem.at[a], rsem.at[a], (x, y, 1 - c)))
        for cp in cps:
            cp.start()
        for a in range(n):
            hf = gs[a].shape[0] // 2
            _rcopy(_half(o_refs[a], c, hf), _half(o_refs[a], 1 - c, hf), ssem.at[a], rsem.at[a],
                   (x, y, 1 - c)).wait_recv()
        for cp in cps:
            cp.wait_send()

    return pl.pallas_call(
        body, name="reduced_to_sibling", in_specs=[ANY] * n, out_specs=[ANY] * n,
        input_output_aliases={a: a for a in range(n)},
        out_shape=[jax.ShapeDtypeStruct(g.shape, g.dtype) for g in gs],
        scratch_shapes=[pltpu.SemaphoreType.DMA((n,)), pltpu.SemaphoreType.DMA((n,))])(*gs)


def small_allgather(g, row0, nrs):
    w = g.shape[1]

    def body(g_ref, o_ref, ssem, rsem):
        x, y, c = _me()
        k = 2 * x + y
        chips = _other_chips(x, y)
        src = g_ref.at[pl.ds(row0, nrs), :]
        sends = [_rcopy(src, o_ref.at[k], ssem.at[j], rsem.at[j], (cx, cy, c)) for j, (cx, cy) in enumerate(chips)]
        for cp in sends:
            cp.start()
        for j, (cx, cy) in enumerate(chips):
            _rcopy(src, o_ref.at[2 * cx + cy], ssem.at[j], rsem.at[j], (cx, cy, c)).wait_recv()
        for cp in sends:
            cp.wait_send()

    out = pl.pallas_call(
        body, name="small_allgather", in_specs=[ANY], out_specs=ANY,
        out_shape=jax.ShapeDtypeStruct((N_CHIPS, nrs, w), g.dtype),
        scratch_shapes=[pltpu.SemaphoreType.DMA((3,)), pltpu.SemaphoreType.DMA((3,))])(g)
    return lax.dynamic_update_slice(out, g[row0:row0 + nrs][None], (_chip(), 0, 0))


def _adamw_step(w_ref, g_ref, m_ref, v_ref, d_ref, nm_ref, nv_ref):
    bc1 = 1.0 - ADAM_B1 ** ADAM_STEP
    bc2 = 1.0 - ADAM_B2 ** ADAM_STEP
    gv = g_ref[...]
    nm = ADAM_B1 * m_ref[...] + (1.0 - ADAM_B1) * gv
    nv = ADAM_B2 * v_ref[...] + (1.0 - ADAM_B2) * (gv * gv)
    nm_ref[...] = nm
    nv_ref[...] = nv
    d_ref[...] = -ADAM_LR * ((nm / bc1) / (jnp.sqrt(nv / bc2) + ADAM_EPS) + ADAM_WD * w_ref[...])


def adamw_packed(w, g_buf, r0, m, v, name):
    r, c = w.shape
    tr = _tile_rows(r, r0)

    def body(w_ref, g_ref, m_ref, v_ref, go_ref, d_ref, nm_ref, nv_ref):
        go_ref[...] = g_ref[...]
        _adamw_step(w_ref, g_ref, m_ref, v_ref, d_ref, nm_ref, nv_ref)

    own = pl.BlockSpec((tr, CHUNK_W), lambda i, j: (i, j))
    packed = pl.BlockSpec((tr, CHUNK_W), lambda i, j: ((r0 + j * r) // tr + i, 0))
    st = jax.ShapeDtypeStruct((r, c), F32)
    return pl.pallas_call(body, name=name, grid=(r // tr, c // CHUNK_W), in_specs=[own, packed, own, own],
                          out_specs=[own] * 4, out_shape=[st] * 4,
                          compiler_params=_cparams(("parallel", "parallel")))(w, g_buf, m, v)


def adamw(w, g, m, v, name):
    r, wd = w.shape
    tr = _pick_rows(r, cap=max(16, ADAMW_BLOCK_BYTES // (4 * wd)))
    body = functools.partial(_adamw_step)

    spec = pl.BlockSpec((tr, wd), lambda i: (i, 0))
    st = jax.ShapeDtypeStruct((r, wd), F32)
    return pl.pallas_call(body, name=name, grid=(r // tr,), in_specs=[spec] * 4, out_specs=[spec] * 3,
                          out_shape=[st, st, st], compiler_params=_cparams(("parallel",)))(w, g, m, v)


LAYER_KINDS = ("gmlp", "s5", "mla", "gmlp")
PARAMS = {
    "gmlp": ("norm_g", "w_in", "ln_g", "ln_b", "w_s", "b_s", "w_out"),
    "s5": ("norm_g", "w_in", "a_re", "a_im", "log_step", "b_re", "b_im", "c_re", "c_im", "d_skip", "w_glu", "b_glu", "w_out"),
    "mla": ("norm_g", "w_in", "q_norm_g", "w_uq", "kv_norm_g", "w_ukv", "w_out"),
}
COL_SHARDED = ("w_in", "w_uq", "w_ukv")
ROW_SHARDED = ("w_out", "w_glu")
WEIGHT_NAMES = [("l%d_" % i) + n for i, kind in enumerate(LAYER_KINDS) for n in PARAMS[kind]] + ["final_norm_g"]


def _is_big(name):
    return name.split("_", 1)[1] in COL_SHARDED + ROW_SHARDED


BIG = [n for n in WEIGHT_NAMES if _is_big(n)]
SMALL = [n for n in WEIGHT_NAMES if not _is_big(n)]


def _pack_rows(blocks):
    return jnp.concatenate([b.reshape(-1, PACK_W) for b in blocks], axis=0)


def _shard_major(wn, full, width):
    r, c = full.shape
    if wn in COL_SHARDED:
        t = full.reshape(r, N_CHIPS, c // N_CHIPS).transpose(1, 0, 2)
    else:
        t = full.reshape(N_CHIPS, r // N_CHIPS, c)
    return t.reshape(N_CHIPS, -1, width)


def _from_shard_major(name, t, block_shape):
    r, c = block_shape
    if name.split("_", 1)[1] in COL_SHARDED:
        return t.reshape(N_CHIPS, r, c).transpose(1, 0, 2).reshape(r, N_CHIPS * c)
    return t.reshape(N_CHIPS * r, c)


class BigGradSink:
    ORDER = ("w_out", "w_glu", "w_ukv", "w_uq", "w_in")
    ROW_MAJOR = {2: ("w_uq", "w_in")}

    def __init__(self, layer, block_shapes):
        self.layer = layer
        self.regions = {}
        r0 = 0
        for wn in self.ORDER:
            if wn in block_shapes:
                shape = block_shapes[wn]
                self.regions[wn] = (r0, shape, wn not in self.ROW_MAJOR.get(layer, ()))
                r0 += shape[0] * shape[1] // CHUNK_W
        self.buf = lax.empty((N_CHIPS, r0, CHUNK_W), F32)
        self.flight = None

    def mm(self, wn, a, b, name):
        r0, _, direct = self.regions[wn]
        assert direct
        self.buf = matmul_tn_packed(a, b, self.buf, r0, wn in COL_SHARDED, name)

    def put(self, wn, full):
        r0, _, direct = self.regions[wn]
        assert not direct
        piece = _shard_major(wn, full, CHUNK_W)
        self.buf = lax.dynamic_update_slice(self.buf, piece, (0, r0, 0))

    def send(self):
        i = self.layer
        sib, = grads_to_sibling([self.buf], "grads_to_sibling_l%d" % i)
        t = pair_sum(self.buf, sib, BF16, "pair_sum_l%d" % i)
        self.flight = reduce_start(t, sib, "reduce_l%d_start" % i)
        return self.flight[4][0, 0]


def _small_pack(arrs, total_padded):
    flat = jnp.concatenate([a.reshape(-1) for a in arrs])
    return jnp.pad(flat, (0, total_padded - flat.shape[0]))


def kernel(x, positions, l0_norm_g, l0_w_in, l0_ln_g, l0_ln_b, l0_w_s, l0_b_s, l0_w_out, l1_norm_g, l1_w_in, l1_a_re, l1_a_im, l1_log_step, l1_b_re, l1_b_im, l1_c_re, l1_c_im, l1_d_skip, l1_w_glu, l1_b_glu, l1_w_out, l2_norm_g, l2_w_in, l2_q_norm_g, l2_w_uq, l2_kv_norm_g, l2_w_ukv, l2_w_out, l3_norm_g, l3_w_in, l3_ln_g, l3_ln_b, l3_w_s, l3_b_s, l3_w_out, final_norm_g, loss_target, m_l0_norm_g, m_l0_w_in, m_l0_ln_g, m_l0_ln_b, m_l0_w_s, m_l0_b_s, m_l0_w_out, m_l1_norm_g, m_l1_w_in, m_l1_a_re, m_l1_a_im, m_l1_log_step, m_l1_b_re, m_l1_b_im, m_l1_c_re, m_l1_c_im, m_l1_d_skip, m_l1_w_glu, m_l1_b_glu, m_l1_w_out, m_l2_norm_g, m_l2_w_in, m_l2_q_norm_g, m_l2_w_uq, m_l2_kv_norm_g, m_l2_w_ukv, m_l2_w_out, m_l3_norm_g, m_l3_w_in, m_l3_ln_g, m_l3_ln_b, m_l3_w_s, m_l3_b_s, m_l3_w_out, m_final_norm_g, v_l0_norm_g, v_l0_w_in, v_l0_ln_g, v_l0_ln_b, v_l0_w_s, v_l0_b_s, v_l0_w_out, v_l1_norm_g, v_l1_w_in, v_l1_a_re, v_l1_a_im, v_l1_log_step, v_l1_b_re, v_l1_b_im, v_l1_c_re, v_l1_c_im, v_l1_d_skip, v_l1_w_glu, v_l1_b_glu, v_l1_w_out, v_l2_norm_g, v_l2_w_in, v_l2_q_norm_g, v_l2_w_uq, v_l2_kv_norm_g, v_l2_w_ukv, v_l2_w_out, v_l3_norm_g, v_l3_w_in, v_l3_ln_g, v_l3_ln_b, v_l3_w_s, v_l3_b_s, v_l3_w_out, v_final_norm_g):
    args = locals()
    w = {n: args[n] for n in WEIGHT_NAMES}
    mom_m = {n: args["m_" + n] for n in WEIGHT_NAMES}
    mom_v = {n: args["v_" + n] for n in WEIGHT_NAMES}
    h0 = x[0]
    target = loss_target[0]
    pos = positions.reshape(-1, 1)

    big_rows = [w[n].size // PACK_W for n in BIG]
    nrb = sum(big_rows)
    nrb_pad = -(-nrb // PACK_ROW_ALIGN) * PACK_ROW_ALIGN
    full = {}

    def pack_unit(layers):
        names = [n for n in BIG if int(n[1]) in layers]
        rows = [w[n].size // PACK_W for n in names]
        pad = -sum(rows) % PACK_ROW_ALIGN
        return names, rows, _pack_rows([w[n].astype(BF16) for n in names] + [jnp.zeros((pad, PACK_W), BF16)])

    def unpack_unit(names, rows, gathered):
        r0 = 0
        for n, nr in zip(names, rows):
            full[n] = _from_shard_major(n, gathered[:, r0:r0 + nr, :], w[n].shape)
            r0 += nr

    unit0, unit1, unit2 = pack_unit((0,)), pack_unit((1,)), pack_unit((2, 3))
    unpack_unit(unit0[0], unit0[1], weights_allgather(unit0[2]))
    flight = gather_start(unit1[2], unit0[2], "gather_l1_start")
    wp = dict(w)
    wp["l0_norm_g"] = w["l0_norm_g"] + flight[4][0, 0]

    def layer_params(i):
        pre = "l%d_" % i
        p = {k[len(pre):]: v for k, v in wp.items() if k.startswith(pre)}
        wf = {k[len(pre):]: v for k, v in full.items() if k.startswith(pre)}
        return p, wf

    cos, sins = rope_tables(pos)
    h = h0
    saved = []
    for i, kind in enumerate(LAYER_KINDS):
        if i == 1:
            land = gather_wait(*flight[:4], h, "gather_l1_wait")
            got = gather_handover(land, unit1[2], "gather_l1_handover")
            unpack_unit(unit1[0], unit1[1], got)
            flight = gather_start(unit2[2], got, "gather_l23_start")
            wp["l1_norm_g"] = w["l1_norm_g"] + flight[4][0, 0]
        if i == 2:
            land = gather_wait(*flight[:4], h, "gather_l23_wait")
            unpack_unit(unit2[0], unit2[1], gather_handover(land, unit2[2], "gather_l23_handover"))
        p, wf = layer_params(i)
        tag = "l%d" % i
        if kind == "gmlp":
            h, s = gmlp_layer_fwd(h, p, wf, tag)
        elif kind == "s5":
            h, s = s5_layer_fwd(h, p, wf, tag)
        else:
            h, s = mla_layer_fwd(h, p, wf, cos, sins, tag)
        saved.append(s)
    loss_part, dh, g_final = loss_head(h, final_norm_g, target)

    grads = {"final_norm_g": g_final.reshape(-1)}
    sinks = {}

    for i in reversed(range(len(LAYER_KINDS))):
        kind = LAYER_KINDS[i]
        p, wf = layer_params(i)
        tag = "l%d" % i
        sink = sinks[i] = BigGradSink(i, {n[3:]: w[n].shape for n in BIG if int(n[1]) == i})
        if kind == "gmlp":
            dh, g = gmlp_layer_bwd(dh, saved[i], p, wf, tag, sink)
        elif kind == "s5":
            dh, g = s5_layer_bwd(dh, saved[i], p, wf, tag, sink)
        else:
            dh, g = mla_layer_bwd(dh, saved[i], p, wf, cos, sins, tag, sink)
        for k, val in g.items():
            grads["l%d_%s" % (i, k)] = val
    grad_x = dh[None]

    n_small = sum(w[n].size for n in SMALL)
    piece = N_CHIPS * 2 * 16 * PACK_W
    n_small_pad = -(-(n_small + 1) // piece) * piece
    nrs = n_small_pad // N_CHIPS // PACK_W
    p_small = _small_pack([grads[n] for n in SMALL] + [loss_part], n_small_pad).reshape(N_CHIPS, nrs, PACK_W)
    sib_small, = grads_to_sibling([p_small], "grads_to_sibling_small")
    t_small = pair_sum(p_small, sib_small, F32, "pair_sum_small")
    rb_small, = grads_across_chips([t_small])
    halves = [chip_sum(t_small, rb_small, "chip_sum_small")]

    after = halves[0]
    for i in reversed(range(len(LAYER_KINDS))):
        t_i, rb_i = reduce_wait(*sinks[i].flight[:4], after, "reduce_l%d_wait" % i)
        halves.append(chip_sum(t_i, rb_i, "chip_sum_l%d" % i))
        after = halves[-1]
    reduced = reduced_to_sibling(halves)
    small_all = small_allgather(reduced[0], 0, nrs)

    g_out, d_out, nm_out, nv_out = {}, {}, {}, {}
    for i, g_i in zip(reversed(range(len(LAYER_KINDS))), reduced[1:]):
        for wn, (r0, shape, direct) in sinks[i].regions.items():
            n = "l%d_%s" % (i, wn)
            if direct:
                g_out[n], d_out[n], nm_out[n], nv_out[n] = adamw_packed(w[n], g_i, r0, mom_m[n], mom_v[n], "adamw_" + n)
            else:
                g_out[n] = g_i[r0:r0 + shape[0] * shape[1] // CHUNK_W].reshape(shape)
                d_out[n], nm_out[n], nv_out[n] = adamw(w[n], g_out[n], mom_m[n], mom_v[n], "adamw_" + n)
    g_small = small_all.reshape(-1, PACK_W)
    sp = lambda d: _small_pack([d[n] for n in SMALL], n_small_pad).reshape(-1, PACK_W)
    d_small, nm_small, nv_small = adamw(sp(w), g_small, sp(mom_m), sp(mom_v), "adamw_small")
    for buf, out in ((g_small, g_out), (d_small, d_out), (nm_small, nm_out), (nv_small, nv_out)):
        flat = buf.reshape(-1)
        o = 0
        for n in SMALL:
            out[n] = flat[o:o + w[n].size].reshape(w[n].shape)
            o += w[n].size
    loss = g_small.reshape(-1)[n_small]
    return (loss, grad_x, *[g_out[n] for n in WEIGHT_NAMES], *[d_out[n] for n in WEIGHT_NAMES],
            *[nm_out[n] for n in WEIGHT_NAMES], *[nv_out[n] for n in WEIGHT_NAMES])
```

```python
import functools
import math

import jax
import jax.numpy as jnp
import numpy as np
from jax import lax
from jax.experimental import pallas as pl
from jax.experimental.pallas import tpu as pltpu

F32 = jnp.float32
BF16 = jnp.bfloat16
MESH = pl.DeviceIdType.MESH
VMEM_LIMIT_BYTES = 56 * 1024 * 1024
LANES = 128
PACK_W = 1024
CHUNK_W = 256
PACK_ROW_ALIGN = 256
ROW_TILE = 256
ADAMW_BLOCK_BYTES = 1024 * 1024
MM_BLOCK_BYTES = 6 * 1024 * 1024

NORM_EPS = 1e-6
N_CHIPS = 4
GMLP_CHUNK = 128
GMLP_GROUPS = 8
S5_GROUPS = 128
S5_GROUP = 16
S5_STATE = 64
S5_SB = 16
S5_SEG = 8
MLA_HEADS = 16
MLA_NOPE = 128
MLA_ROPE = 64
MLA_Q_RANK = 384
MLA_KV_RANK = 128
MLA_SCALE = (MLA_NOPE + MLA_ROPE) ** -0.5
ROPE_THETA = 10000.0
NEG_INF = -1e30
ADAM_LR, ADAM_B1, ADAM_B2, ADAM_EPS, ADAM_WD, ADAM_STEP = 0.001, 0.9, 0.999, 1e-08, 0.01, 10

DN_NN = (((1,), (0,)), ((), ()))
DN_NT = (((1,), (1,)), ((), ()))
DN_TN = (((0,), (0,)), ((), ()))


def _cparams(sem):
    return pltpu.CompilerParams(dimension_semantics=sem, vmem_limit_bytes=VMEM_LIMIT_BYTES)


def _pick(n, cands=(512, 384, 256, 128)):
    for c in cands:
        if n % c == 0:
            return c
    return n


def _pick_rows(r, cap=512, mult=16):
    return max(t for t in range(mult, cap + 1, mult) if r % t == 0)


def _dot(a, b, dn):
    return lax.dot_general(a.astype(BF16), b.astype(BF16), dn, preferred_element_type=F32)


def _sigmoid(x):
    return 0.5 + 0.5 * jnp.tanh(0.5 * x)


def _gelu(x):
    c = math.sqrt(2.0 / math.pi)
    t = jnp.tanh(c * (x + 0.044715 * x * x * x))
    return 0.5 * x * (1.0 + t)


def _gelu_grad(x):
    c = math.sqrt(2.0 / math.pi)
    t = jnp.tanh(c * (x + 0.044715 * x * x * x))
    return 0.5 * (1.0 + t) + 0.5 * x * (1.0 - t * t) * c * (1.0 + 3.0 * 0.044715 * x * x)


def _gelu_both(x):
    c = math.sqrt(2.0 / math.pi)
    t = jnp.tanh(c * (x + 0.044715 * x * x * x))
    return 0.5 * x * (1.0 + t), 0.5 * (1.0 + t) + 0.5 * x * (1.0 - t * t) * c * (1.0 + 3.0 * 0.044715 * x * x)


def _silu_both(z):
    s = _sigmoid(z)
    return z * s, s * (1.0 + z * (1.0 - s))


def _silu(z):
    return z * _sigmoid(z)


def _silu_grad(z):
    s = _sigmoid(z)
    return s * (1.0 + z * (1.0 - s))


def matmul(a, b, mode, name, out_dtype=F32, add=None):
    if mode == "nn":
        (m, k), n = a.shape, b.shape[1]
    elif mode == "nt":
        (m, k), n = a.shape, b.shape[0]
    else:
        (k, m), n = a.shape, b.shape[1]
    tm = _pick(m, [t for t in (1024, 512, 384, 256, 128) if t * k * a.dtype.itemsize <= MM_BLOCK_BYTES])
    tn = _pick(n, [t for t in (512, 384, 256, 128) if t * k * b.dtype.itemsize <= MM_BLOCK_BYTES])
    dn = {"nn": DN_NN, "nt": DN_NT, "tn": DN_TN}[mode]

    def body(*refs):
        if add is None:
            a_ref, b_ref, o_ref = refs
        else:
            a_ref, b_ref, add_ref, o_ref = refs
        r = _dot(a_ref[...], b_ref[...], dn)
        if add is not None:
            r = r + add_ref[...].astype(F32)
        o_ref[...] = r.astype(out_dtype)

    a_spec = pl.BlockSpec((k, tm), lambda i, j: (0, i)) if mode == "tn" else pl.BlockSpec((tm, k), lambda i, j: (i, 0))
    b_spec = pl.BlockSpec((tn, k), lambda i, j: (j, 0)) if mode == "nt" else pl.BlockSpec((k, tn), lambda i, j: (0, j))
    o_spec = pl.BlockSpec((tm, tn), lambda i, j: (i, j))
    in_specs = [a_spec, b_spec] + ([o_spec] if add is not None else [])
    args = (a, b) + ((add,) if add is not None else ())
    return pl.pallas_call(
        body, name=name, grid=(m // tm, n // tn), in_specs=in_specs, out_specs=o_spec,
        out_shape=jax.ShapeDtypeStruct((m, n), out_dtype),
        compiler_params=_cparams(("parallel", "arbitrary")))(*args)


def _tile_rows(r, r0, cands=(512, 384, 256, 128)):
    return next(t for t in cands if r % t == 0 and r0 % t == 0)


def matmul_tn_packed(a, b, buf, r0, col_sharded, name):
    k, m = a.shape
    n = b.shape[1]
    if col_sharded:
        chunks = n // N_CHIPS // CHUNK_W
        tm = _tile_rows(m, r0)
        o_map = lambda i, j: (j // chunks, (r0 + (j % chunks) * m) // tm + i, 0)
    else:
        rs = m // N_CHIPS
        tm = _tile_rows(rs, r0)
        per = rs // tm
        o_map = lambda i, j: (i // per, (r0 + j * rs) // tm + i % per, 0)

    def body(a_ref, b_ref, buf_ref, o_ref):
        o_ref[0] = _dot(a_ref[...], b_ref[...], DN_TN)

    return pl.pallas_call(
        body, name=name, grid=(m // tm, n // CHUNK_W),
        in_specs=[pl.BlockSpec((k, tm), lambda i, j: (0, i)), pl.BlockSpec((k, CHUNK_W), lambda i, j: (0, j)),
                  pl.BlockSpec(memory_space=pl.ANY)],
        out_specs=pl.BlockSpec((1, tm, CHUNK_W), o_map), out_shape=jax.ShapeDtypeStruct(buf.shape, buf.dtype),
        input_output_aliases={2: 0}, compiler_params=_cparams(("parallel", "arbitrary")))(a, b, buf)


def _rows(tl, w, col=0):
    return pl.BlockSpec((tl, w), lambda i: (i, col))


def _full(shape):
    nd = len(shape)
    return pl.BlockSpec(tuple(shape), lambda i: (0,) * nd)


def _rowcall(body, name, n_steps, in_specs, out_specs, out_shape, scratch=()):
    return pl.pallas_call(
        body, name=name, grid=(n_steps,), in_specs=in_specs, out_specs=out_specs, out_shape=out_shape,
        scratch_shapes=list(scratch), compiler_params=_cparams(("arbitrary",)))


def _acc(ref, val, i):
    @pl.when(i == 0)
    def _():
        ref[...] = val

    @pl.when(i != 0)
    def _():
        ref[...] += val


def rms_fwd(h, g, name):
    l, d = h.shape
    tl = ROW_TILE

    def body(h_ref, g_ref, o_ref):
        x = h_ref[...]
        r = lax.rsqrt(jnp.mean(x * x, axis=-1, keepdims=True) + NORM_EPS)
        o_ref[...] = (x * r * g_ref[...]).astype(BF16)

    return _rowcall(body, name, l // tl, [_rows(tl, d), _full((1, d))], _rows(tl, d),
                    jax.ShapeDtypeStruct((l, d), BF16))(h, g.reshape(1, d))


def rms_bwd(h, g, dhn, dh_in, name):
    l, d = h.shape
    tl = ROW_TILE

    def body(h_ref, g_ref, dhn_ref, dhi_ref, dh_ref, dg_ref):
        i = pl.program_id(0)
        x = h_ref[...]
        r = lax.rsqrt(jnp.mean(x * x, axis=-1, keepdims=True) + NORM_EPS)
        xhat = x * r
        dy = dhn_ref[...]
        dxh = dy * g_ref[...]
        dx = r * (dxh - xhat * jnp.mean(dxh * xhat, axis=-1, keepdims=True))
        dh_ref[...] = dhi_ref[...] + dx
        _acc(dg_ref, jnp.sum(dy * xhat, axis=0, keepdims=True), i)

    return _rowcall(body, name, l // tl, [_rows(tl, d), _full((1, d)), _rows(tl, d), _rows(tl, d)],
                    [_rows(tl, d), _full((1, d))],
                    [jax.ShapeDtypeStruct((l, d), F32), jax.ShapeDtypeStruct((1, d), F32)])(h, g.reshape(1, d), dhn, dh_in)


def loss_head(h, g, target):
    l, d = h.shape
    tl = ROW_TILE

    def body(h_ref, g_ref, t_ref, loss_ref, dh_ref, dg_ref):
        i = pl.program_id(0)
        x = h_ref[...]
        gg = g_ref[...]
        r = lax.rsqrt(jnp.mean(x * x, axis=-1, keepdims=True) + NORM_EPS)
        xhat = x * r
        err = xhat * gg - t_ref[...]
        part = 0.5 * jnp.sum(jnp.mean(err * err, axis=-1, keepdims=True), axis=0, keepdims=True)
        _acc(loss_ref, part, i)
        dy = err * (1.0 / d)
        dxh = dy * gg
        dh_ref[...] = r * (dxh - xhat * jnp.mean(dxh * xhat, axis=-1, keepdims=True))
        _acc(dg_ref, jnp.sum(dy * xhat, axis=0, keepdims=True), i)

    return _rowcall(body, "loss_head", l // tl, [_rows(tl, d), _full((1, d)), _rows(tl, d)],
                    [_full((1, 1)), _rows(tl, d), _full((1, d))],
                    [jax.ShapeDtypeStruct((1, 1), F32), jax.ShapeDtypeStruct((l, d), F32),
                     jax.ShapeDtypeStruct((1, d), F32)])(h, g.reshape(1, d), target)


def _gmlp_common(a_ref, lng_ref, lnb_ref):
    di = lng_ref.shape[1]
    u_pre = a_ref[:, 0:di]
    v_pre = a_ref[:, di:2 * di]
    z = a_ref[:, 2 * di:3 * di]
    vg = _gelu(v_pre)
    mu = jnp.mean(vg, axis=-1, keepdims=True)
    xc = vg - mu
    rstd = lax.rsqrt(jnp.mean(xc * xc, axis=-1, keepdims=True) + NORM_EPS)
    vhat = xc * rstd
    vn = vhat * lng_ref[...] + lnb_ref[...]
    return u_pre, v_pre, z, vhat, rstd, vn


def _tril(w):
    r = lax.broadcasted_iota(jnp.int32, w.shape, 0)
    c = lax.broadcasted_iota(jnp.int32, w.shape, 1)
    return jnp.where(c <= r, w, 0.0)


def gmlp_gate_fwd(a, ln_g, ln_b, w_s, b_s, name):
    l, w3 = a.shape
    di = w3 // 3
    dg = di // GMLP_GROUPS
    tl = GMLP_CHUNK

    def body(a_ref, lng_ref, lnb_ref, ws_ref, bs_ref, m_ref):
        u_pre, _, z, _, _, vn = _gmlp_common(a_ref, lng_ref, lnb_ref)
        gate = _gelu(u_pre) * _silu(z)
        for g in range(GMLP_GROUPS):
            sl = slice(g * dg, (g + 1) * dg)
            s = _dot(_tril(ws_ref[g]), vn[:, sl], DN_NN) + bs_ref[g]
            m_ref[:, sl] = (gate[:, sl] * s).astype(BF16)

    return _rowcall(body, name, l // tl,
                    [_rows(tl, w3), _full((1, di)), _full((1, di)), _full(w_s.shape), _full((GMLP_GROUPS, tl, 1))],
                    _rows(tl, di), jax.ShapeDtypeStruct((l, di), BF16))(
        a, ln_g.reshape(1, di), ln_b.reshape(1, di), w_s, b_s.reshape(GMLP_GROUPS, tl, 1))


def gmlp_gate_bwd(a, dm, ln_g, ln_b, w_s, b_s, name):
    l, w3 = a.shape
    di = w3 // 3
    dg = di // GMLP_GROUPS
    tl = GMLP_CHUNK

    def body(a_ref, dm_ref, lng_ref, lnb_ref, ws_ref, bs_ref, da_ref, dlg_ref, dlb_ref, dws_ref, dbs_ref,
             dvn_ref, vh_ref, gv_ref):
        i = pl.program_id(0)
        vg, gv = _gelu_both(a_ref[:, di:2 * di])
        gv_ref[...] = gv
        xc = vg - jnp.mean(vg, axis=-1, keepdims=True)
        rstd = lax.rsqrt(jnp.mean(xc * xc, axis=-1, keepdims=True) + NORM_EPS)
        vh_ref[...] = xc * rstd
        for g in range(GMLP_GROUPS):
            sl = slice(g * dg, (g + 1) * dg)
            wt = _tril(ws_ref[g])
            vn_g = vh_ref[:, sl] * lng_ref[:, sl] + lnb_ref[:, sl]
            s = _dot(wt, vn_g, DN_NN) + bs_ref[g]
            dmg = dm_ref[:, sl]
            u, gu = _gelu_both(a_ref[:, sl])
            sz, gz = _silu_both(a_ref[:, 2 * di + g * dg:2 * di + (g + 1) * dg])
            ds = dmg * u * sz
            da_ref[:, sl] = (dmg * s * sz * gu).astype(BF16)
            da_ref[:, 2 * di + g * dg:2 * di + (g + 1) * dg] = (dmg * u * s * gz).astype(BF16)
            dvn_ref[:, sl] = _dot(wt, ds, DN_TN)
            dw = _tril(_dot(ds, vn_g, DN_NT))
            db = jnp.sum(ds, axis=1, keepdims=True)

            @pl.when(i == 0)
            def _():
                dws_ref[g] = dw
                dbs_ref[g] = db

            @pl.when(i != 0)
            def _():
                dws_ref[g] += dw
                dbs_ref[g] += db

        dvn = dvn_ref[...]
        vhat = vh_ref[...]
        dxh = dvn * lng_ref[...]
        dvg = rstd * (dxh - jnp.mean(dxh, axis=-1, keepdims=True) - vhat * jnp.mean(dxh * vhat, axis=-1, keepdims=True))
        da_ref[:, di:2 * di] = (dvg * gv_ref[...]).astype(BF16)
        _acc(dlg_ref, jnp.sum(dvn * vhat, axis=0, keepdims=True), i)
        _acc(dlb_ref, jnp.sum(dvn, axis=0, keepdims=True), i)

    outs = _rowcall(
        body, name, l // tl,
        [_rows(tl, w3), _rows(tl, di), _full((1, di)), _full((1, di)), _full(w_s.shape), _full((GMLP_GROUPS, tl, 1))],
        [_rows(tl, w3), _full((1, di)), _full((1, di)), _full(w_s.shape), _full((GMLP_GROUPS, tl, 1))],
        [jax.ShapeDtypeStruct((l, w3), BF16), jax.ShapeDtypeStruct((1, di), F32), jax.ShapeDtypeStruct((1, di), F32),
         jax.ShapeDtypeStruct(w_s.shape, F32), jax.ShapeDtypeStruct((GMLP_GROUPS, tl, 1), F32)],
        scratch=[pltpu.VMEM((tl, di), F32)] * 3)(
        a, dm, ln_g.reshape(1, di), ln_b.reshape(1, di), w_s, b_s.reshape(GMLP_GROUPS, tl, 1))
    return outs


def gmlp_layer_fwd(h, p, wf, tag):
    hn = rms_fwd(h, p["norm_g"], tag + "_rms")
    a = matmul(hn, wf["w_in"], "nn", tag + "_mm_in")
    m = gmlp_gate_fwd(a, p["ln_g"], p["ln_b"], p["w_s"], p["b_s"], tag + "_gate")
    h_out = matmul(m, wf["w_out"], "nn", tag + "_mm_out", add=h)
    return h_out, (h, hn, a, m)


def gmlp_layer_bwd(dh_out, saved, p, wf, tag, sink):
    h, hn, a, m = saved
    dm = matmul(dh_out, wf["w_out"], "nt", tag + "_mm_dm")
    sink.mm("w_out", m, dh_out, tag + "_mm_gwout")
    da, dlg, dlb, dws, dbs = gmlp_gate_bwd(a, dm, p["ln_g"], p["ln_b"], p["w_s"], p["b_s"], tag + "_gate_bwd")
    dhn = matmul(da, wf["w_in"], "nt", tag + "_mm_dhn")
    sink.mm("w_in", hn, da, tag + "_mm_gwin")
    zero = sink.send()
    dh, dng = rms_bwd(h, p["norm_g"] + zero, dhn, dh_out, tag + "_rms_bwd")
    grads = {"norm_g": dng.reshape(-1), "ln_g": dlg.reshape(-1), "ln_b": dlb.reshape(-1),
             "w_s": dws, "b_s": dbs.reshape(GMLP_GROUPS, GMLP_CHUNK)}
    return dh, grads


def _cmul(ar, ai, br, bi):
    return ar * br - ai * bi, ar * bi + ai * br


S5_PG = 16


def _gblock(tail):
    return pl.BlockSpec((S5_PG,) + tuple(tail), lambda i: (i, 0, 0))


def s5_params_fwd(a_re, a_im, log_step, b_re, b_im):
    g, p, hh = b_re.shape

    def body(ar_ref, ai_ref, ls_ref, br_ref, bi_ref, lr_ref, li_ref, bbr_ref, bbi_ref):
        ar, ai = ar_ref[...], ai_ref[...]
        step = jnp.exp(ls_ref[...])
        mag = jnp.exp(ar * step)
        lr, li = mag * jnp.cos(ai * step), mag * jnp.sin(ai * step)
        den = 1.0 / (ar * ar + ai * ai)
        fr, fi = _cmul(lr - 1.0, li, ar * den, -ai * den)
        lr_ref[...] = lr
        li_ref[...] = li
        bbr, bbi = _cmul(fr, fi, br_ref[...], bi_ref[...])
        bbr_ref[...] = bbr
        bbi_ref[...] = bbi

    s1 = jax.ShapeDtypeStruct((g, p, 1), F32)
    s3 = jax.ShapeDtypeStruct((g, p, hh), F32)
    b1, b0, b3 = _gblock((p, 1)), _gblock((1, 1)), _gblock((p, hh))
    return pl.pallas_call(body, name="s5_params_fwd", grid=(g // S5_PG,), in_specs=[b1, b1, b0, b3, b3],
                          out_specs=[b1, b1, b3, b3], out_shape=[s1, s1, s3, s3],
                          compiler_params=_cparams(("parallel",)))(
        a_re.reshape(g, p, 1), a_im.reshape(g, p, 1), log_step.reshape(g, 1, 1), b_re, b_im)


def s5_params_bwd(a_re, a_im, log_step, b_re, b_im, dl_re, dl_im, dbb_re, dbb_im):
    g, p, hh = b_re.shape

    def body(ar_ref, ai_ref, ls_ref, br_ref, bi_ref, dlr_ref, dli_ref, dbr_ref, dbi_ref,
             gar_ref, gai_ref, gls_ref, gbr_ref, gbi_ref):
        ar, ai = ar_ref[...], ai_ref[...]
        step = jnp.exp(ls_ref[...])
        mag = jnp.exp(ar * step)
        lr, li = mag * jnp.cos(ai * step), mag * jnp.sin(ai * step)
        den = 1.0 / (ar * ar + ai * ai)
        ir, ii = ar * den, -ai * den
        fr, fi = _cmul(lr - 1.0, li, ir, ii)
        br, bi = br_ref[...], bi_ref[...]
        dbr, dbi = dbr_ref[...], dbi_ref[...]
        gbr, gbi = _cmul(fr, -fi, dbr, dbi)
        gbr_ref[...] = gbr
        gbi_ref[...] = gbi
        pr, pi = _cmul(br, -bi, dbr, dbi)
        gfr = jnp.sum(pr, axis=-1, keepdims=True)
        gfi = jnp.sum(pi, axis=-1, keepdims=True)
        t_r, t_i = _cmul(ir, -ii, gfr, gfi)
        glr, gli = dlr_ref[...] + t_r, dli_ref[...] + t_i
        c1r, c1i = _cmul(step * lr, -step * li, glr, gli)
        qr, qi = _cmul(fr, fi, ir, ii)
        c2r, c2i = _cmul(-qr, qi, gfr, gfi)
        gar_ref[...] = c1r + c2r
        gai_ref[...] = c1i + c2i
        wr, wi = _cmul(ar, ai, lr, li)
        sr, _ = _cmul(wr, -wi, glr, gli)
        gls_ref[...] = jnp.sum(sr, axis=1, keepdims=True) * step

    s1 = jax.ShapeDtypeStruct((g, p, 1), F32)
    s3 = jax.ShapeDtypeStruct((g, p, hh), F32)
    b1, b0, b3 = _gblock((p, 1)), _gblock((1, 1)), _gblock((p, hh))
    return pl.pallas_call(body, name="s5_params_bwd", grid=(g // S5_PG,),
                          in_specs=[b1, b1, b0, b3, b3, b1, b1, b3, b3], out_specs=[b1, b1, b0, b3, b3],
                          out_shape=[s1, s1, jax.ShapeDtypeStruct((g, 1, 1), F32), s3, s3],
                          compiler_params=_cparams(("parallel",)))(
        a_re.reshape(g, p, 1), a_im.reshape(g, p, 1), log_step.reshape(g, 1, 1), b_re, b_im,
        dl_re, dl_im, dbb_re, dbb_im)


def _blockdiag(t):
    sb, n, r, c = t.shape
    eye = jnp.eye(n, dtype=bool)[None, :, None, :, None]
    full = jnp.where(eye, t[:, :, :, None, :], jnp.zeros((), t.dtype))
    return full.reshape(sb, n * r, n * c)


def _blockdiag_extract(m, r, c):
    sb = m.shape[0]
    n = m.shape[1] // r
    m5 = m.reshape(sb, n, r, n, c)
    return jnp.stack([m5[:, i, :, i, :] for i in range(n)], axis=1)


S5_TB = 64
S5_UNROLL = 8


def s5_scan_fwd(a_p, lam_re, lam_im, wb_re, wb_im, wc_re, wc_im, d_skip, x0_re, x0_im, name):
    l = a_p.shape[0]
    di = d_skip.shape[1]
    rows = S5_SEG * S5_TB
    nb = l // rows
    ns = wb_re.shape[2]

    def body(u_ref, lr_ref, li_ref, wbr_ref, wbi_ref, wcr_ref, wci_ref, ds_ref, x0r_ref, x0i_ref,
             y_ref, ckr_ref, cki_ref, xer_ref, xei_ref, bur, bui, xr_s, xi_s):
        b = pl.program_id(1)

        @pl.when(b == 0)
        def _():
            xr_s[...] = x0r_ref[0]
            xi_s[...] = x0i_ref[0]

        ckr_ref[0, 0] = xr_s[...]
        cki_ref[0, 0] = xi_s[...]
        u = u_ref[...]
        bur[...] = _dot(u, wbr_ref[0], DN_NN)
        bui[...] = _dot(u, wbi_ref[0], DN_NN)
        lr = jnp.broadcast_to(lr_ref[0], (S5_SEG, ns))
        li = jnp.broadcast_to(li_ref[0], (S5_SEG, ns))

        def step(t, carry):
            xr, xi = carry
            sl = pl.ds(pl.multiple_of(t * S5_SEG, S5_SEG), S5_SEG)
            nr = lr * xr - li * xi + bur[sl, :]
            ni = lr * xi + li * xr + bui[sl, :]
            bur[sl, :] = nr
            bui[sl, :] = ni
            return nr, ni

        xr, xi = lax.fori_loop(0, S5_TB, step, (xr_s[...], xi_s[...]), unroll=S5_UNROLL)
        xr_s[...] = xr
        xi_s[...] = xi
        xer_ref[0] = xr
        xei_ref[0] = xi
        y_ref[...] = _dot(bur[...], wcr_ref[0], DN_NN) - _dot(bui[...], wci_ref[0], DN_NN) + ds_ref[...] * u

    sb3 = lambda s, b: (s, 0, 0)
    st = jax.ShapeDtypeStruct
    return pl.pallas_call(
        body, name=name, grid=(S5_SB, nb),
        in_specs=[pl.BlockSpec((rows, LANES), lambda s, b: (b, s)),
                  pl.BlockSpec((1, 1, ns), sb3), pl.BlockSpec((1, 1, ns), sb3),
                  pl.BlockSpec((1, LANES, ns), sb3), pl.BlockSpec((1, LANES, ns), sb3),
                  pl.BlockSpec((1, ns, LANES), sb3), pl.BlockSpec((1, ns, LANES), sb3),
                  pl.BlockSpec((1, LANES), lambda s, b: (0, s)),
                  pl.BlockSpec((1, S5_SEG, ns), sb3), pl.BlockSpec((1, S5_SEG, ns), sb3)],
        out_specs=[pl.BlockSpec((rows, LANES), lambda s, b: (b, s)),
                   pl.BlockSpec((1, 1, S5_SEG, ns), lambda s, b: (s, b, 0, 0)),
                   pl.BlockSpec((1, 1, S5_SEG, ns), lambda s, b: (s, b, 0, 0)),
                   pl.BlockSpec((1, S5_SEG, ns), sb3), pl.BlockSpec((1, S5_SEG, ns), sb3)],
        out_shape=[st((l, di), F32), st((S5_SB, nb, S5_SEG, ns), F32), st((S5_SB, nb, S5_SEG, ns), F32),
                   st((S5_SB, S5_SEG, ns), F32), st((S5_SB, S5_SEG, ns), F32)],
        scratch_shapes=[pltpu.VMEM((rows, ns), F32), pltpu.VMEM((rows, ns), F32),
                        pltpu.VMEM((S5_SEG, ns), F32), pltpu.VMEM((S5_SEG, ns), F32)],
        compiler_params=_cparams(("parallel", "arbitrary")))(
        a_p, lam_re, lam_im, wb_re, wb_im, wc_re, wc_im, d_skip, x0_re, x0_im)


def s5_ends(inp, lam_re, lam_im, w_re, w_im, adjoint, name):
    l = inp.shape[0]
    rows = S5_SEG * S5_TB
    nb = l // rows
    ns = lam_re.shape[2]

    def body(i_ref, lr_ref, li_ref, wr_ref, wi_ref, er_ref, ei_ref, pr_b, pi_b, xr_s, xi_s):
        b = pl.program_id(1)

        @pl.when(b == 0)
        def _():
            xr_s[...] = jnp.zeros_like(xr_s)
            xi_s[...] = jnp.zeros_like(xi_s)

        v = i_ref[...]
        lr = jnp.broadcast_to(lr_ref[0], (S5_SEG, ns))
        li = jnp.broadcast_to(li_ref[0], (S5_SEG, ns))
        if adjoint:
            pr_b[...] = _dot(v, wr_ref[0], DN_NT)
            pi_b[...] = -_dot(v, wi_ref[0], DN_NT)
            li = -li
        else:
            pr_b[...] = _dot(v, wr_ref[0], DN_NN)
            pi_b[...] = _dot(v, wi_ref[0], DN_NN)

        def step(k, carry):
            xr, xi = carry
            t = S5_TB - 1 - k if adjoint else k
            sl = pl.ds(pl.multiple_of(t * S5_SEG, S5_SEG), S5_SEG)
            return lr * xr - li * xi + pr_b[sl, :], lr * xi + li * xr + pi_b[sl, :]

        xr, xi = lax.fori_loop(0, S5_TB, step, (xr_s[...], xi_s[...]), unroll=S5_UNROLL)
        xr_s[...] = xr
        xi_s[...] = xi
        er_ref[0] = xr
        ei_ref[0] = xi

    sb3 = lambda s, b: (s, 0, 0)
    blk = (lambda s, b: (nb - 1 - b, s)) if adjoint else (lambda s, b: (b, s))
    wshape = (1, ns, LANES) if adjoint else (1, LANES, ns)
    st = jax.ShapeDtypeStruct((S5_SB, S5_SEG, ns), F32)
    return pl.pallas_call(
        body, name=name, grid=(S5_SB, nb),
        in_specs=[pl.BlockSpec((rows, LANES), blk), pl.BlockSpec((1, 1, ns), sb3), pl.BlockSpec((1, 1, ns), sb3),
                  pl.BlockSpec(wshape, sb3), pl.BlockSpec(wshape, sb3)],
        out_specs=[pl.BlockSpec((1, S5_SEG, ns), sb3), pl.BlockSpec((1, S5_SEG, ns), sb3)],
        out_shape=[st, st],
        scratch_shapes=[pltpu.VMEM((rows, ns), F32), pltpu.VMEM((rows, ns), F32),
                        pltpu.VMEM((S5_SEG, ns), F32), pltpu.VMEM((S5_SEG, ns), F32)],
        compiler_params=_cparams(("parallel", "arbitrary")))(inp, lam_re, lam_im, w_re, w_im)


def s5_scan_bwd(a_p, dy, lam_re, lam_im, wb_re, wb_im, wc_re, wc_im, d_skip, ck_re, ck_im, a0_re, a0_im, name):
    l = a_p.shape[0]
    di = d_skip.shape[1]
    rows = S5_SEG * S5_TB
    nb = l // rows
    ns = wb_re.shape[2]

    def body(u_ref, dy_ref, lr_ref, li_ref, wbr_ref, wbi_ref, wcr_ref, wci_ref, ds_ref, ckr_ref, cki_ref,
             a0r_ref, a0i_ref,
             du_ref, dwbr_ref, dwbi_ref, dwcr_ref, dwci_ref, dds_ref, dlr_ref, dli_ref, aer_ref, aei_ref,
             xr_b, xi_b, gr_b, gi_b, ar_s, ai_s):
        b = pl.program_id(1)

        @pl.when(b == 0)
        def _():
            ar_s[...] = a0r_ref[0]
            ai_s[...] = a0i_ref[0]

        u = u_ref[...]
        dyv = dy_ref[...]
        lr = jnp.broadcast_to(lr_ref[0], (S5_SEG, ns))
        li = jnp.broadcast_to(li_ref[0], (S5_SEG, ns))
        xr_b[...] = _dot(u, wbr_ref[0], DN_NN)
        xi_b[...] = _dot(u, wbi_ref[0], DN_NN)

        def fstep(t, carry):
            xr, xi = carry
            sl = pl.ds(pl.multiple_of(t * S5_SEG, S5_SEG), S5_SEG)
            nr = lr * xr - li * xi + xr_b[sl, :]
            ni = lr * xi + li * xr + xi_b[sl, :]
            xr_b[sl, :] = nr
            xi_b[sl, :] = ni
            return nr, ni

        x0r, x0i = ckr_ref[0, 0], cki_ref[0, 0]
        lax.fori_loop(0, S5_TB, fstep, (x0r, x0i), unroll=S5_UNROLL)
        dwcr = _dot(xr_b[...], dyv, DN_TN)
        dwci = -_dot(xi_b[...], dyv, DN_TN)
        gr_b[...] = _dot(dyv, wcr_ref[0], DN_NT)
        gi_b[...] = -_dot(dyv, wci_ref[0], DN_NT)

        def bstep(k, carry):
            ar, ai, dlr, dli = carry
            t = S5_TB - 1 - k
            sl = pl.ds(pl.multiple_of(t * S5_SEG, S5_SEG), S5_SEG)
            slp = pl.ds(pl.multiple_of(jnp.maximum(t - 1, 0) * S5_SEG, S5_SEG), S5_SEG)
            nr = gr_b[sl, :] + lr * ar + li * ai
            ni = gi_b[sl, :] + lr * ai - li * ar
            gr_b[sl, :] = nr
            gi_b[sl, :] = ni
            first = t == 0
            pr = jnp.where(first, x0r, xr_b[slp, :])
            pi = jnp.where(first, x0i, xi_b[slp, :])
            dlr = dlr + nr * pr + ni * pi
            dli = dli + ni * pr - nr * pi
            return nr, ni, dlr, dli

        zero = jnp.zeros((S5_SEG, ns), F32)
        ar, ai, dlr, dli = lax.fori_loop(0, S5_TB, bstep, (ar_s[...], ai_s[...], zero, zero), unroll=S5_UNROLL)
        ar_s[...] = ar
        ai_s[...] = ai
        aer_ref[0] = ar
        aei_ref[0] = ai
        dsk = ds_ref[...]
        du_ref[...] = (_dot(gr_b[...], wbr_ref[0], DN_NT) + _dot(gi_b[...], wbi_ref[0], DN_NT) + dsk * dyv).astype(BF16)
        dwbr = _dot(u, gr_b[...], DN_TN)
        dwbi = _dot(u, gi_b[...], DN_TN)
        dds = jnp.sum(dyv * u, axis=0, keepdims=True)

        @pl.when(b == 0)
        def _():
            dwbr_ref[0] = dwbr
            dwbi_ref[0] = dwbi
            dwcr_ref[0] = dwcr
            dwci_ref[0] = dwci
            dds_ref[...] = dds
            dlr_ref[0] = dlr
            dli_ref[0] = dli

        @pl.when(b != 0)
        def _():
            dwbr_ref[0] += dwbr
            dwbi_ref[0] += dwbi
            dwcr_ref[0] += dwcr
            dwci_ref[0] += dwci
            dds_ref[...] += dds
            dlr_ref[0] += dlr
            dli_ref[0] += dli

    sb3 = lambda s, b: (s, 0, 0)
    rev = lambda s, b: (nb - 1 - b, s)
    st = jax.ShapeDtypeStruct
    return pl.pallas_call(
        body, name=name, grid=(S5_SB, nb),
        in_specs=[pl.BlockSpec((rows, LANES), rev), pl.BlockSpec((rows, LANES), rev),
                  pl.BlockSpec((1, 1, ns), sb3), pl.BlockSpec((1, 1, ns), sb3),
                  pl.BlockSpec((1, LANES, ns), sb3), pl.BlockSpec((1, LANES, ns), sb3),
                  pl.BlockSpec((1, ns, LANES), sb3), pl.BlockSpec((1, ns, LANES), sb3),
                  pl.BlockSpec((1, LANES), lambda s, b: (0, s)),
                  pl.BlockSpec((1, 1, S5_SEG, ns), lambda s, b: (s, nb - 1 - b, 0, 0)),
                  pl.BlockSpec((1, 1, S5_SEG, ns), lambda s, b: (s, nb - 1 - b, 0, 0)),
                  pl.BlockSpec((1, S5_SEG, ns), sb3), pl.BlockSpec((1, S5_SEG, ns), sb3)],
        out_specs=[pl.BlockSpec((rows, LANES), rev),
                   pl.BlockSpec((1, LANES, ns), sb3), pl.BlockSpec((1, LANES, ns), sb3),
                   pl.BlockSpec((1, ns, LANES), sb3), pl.BlockSpec((1, ns, LANES), sb3),
                   pl.BlockSpec((1, LANES), lambda s, b: (0, s)),
                   pl.BlockSpec((1, S5_SEG, ns), sb3), pl.BlockSpec((1, S5_SEG, ns), sb3),
                   pl.BlockSpec((1, S5_SEG, ns), sb3), pl.BlockSpec((1, S5_SEG, ns), sb3)],
        out_shape=[st((l, di), BF16), st((S5_SB, LANES, ns), F32), st((S5_SB, LANES, ns), F32),
                   st((S5_SB, ns, LANES), F32), st((S5_SB, ns, LANES), F32), st((1, di), F32),
                   st((S5_SB, S5_SEG, ns), F32), st((S5_SB, S5_SEG, ns), F32),
                   st((S5_SB, S5_SEG, ns), F32), st((S5_SB, S5_SEG, ns), F32)],
        scratch_shapes=[pltpu.VMEM((rows, ns), F32), pltpu.VMEM((rows, ns), F32),
                        pltpu.VMEM((rows, ns), F32), pltpu.VMEM((rows, ns), F32),
                        pltpu.VMEM((S5_SEG, ns), F32), pltpu.VMEM((S5_SEG, ns), F32)],
        compiler_params=_cparams(("parallel", "arbitrary")))(
        a_p, dy, lam_re, lam_im, wb_re, wb_im, wc_re, wc_im, d_skip, ck_re, ck_im, a0_re, a0_im)


def s5_carry(e_re, e_im, lam_re, lam_im, seg_len, reverse, name):
    sb, seg, ns = e_re.shape

    def body(er_ref, ei_ref, lr_ref, li_ref, cr_ref, ci_ref):
        pr, pi = lr_ref[...], li_ref[...]
        if reverse:
            pi = -pi
        for _ in range(int(math.log2(seg_len))):
            pr, pi = _cmul(pr, pi, pr, pi)
        er, ei = er_ref[...], ei_ref[...]
        row = lax.broadcasted_iota(jnp.int32, (sb, seg, ns), 1)
        cr = jnp.zeros((sb, seg, ns), F32)
        ci = jnp.zeros((sb, seg, ns), F32)
        cur_r = jnp.zeros((sb, 1, ns), F32)
        cur_i = jnp.zeros((sb, 1, ns), F32)
        order = range(seg - 2, -1, -1) if reverse else range(1, seg)
        for s in order:
            src = s + 1 if reverse else s - 1
            mr, mi = _cmul(pr, pi, cur_r, cur_i)
            cur_r = jnp.sum(jnp.where(row == src, er, 0.0), axis=1, keepdims=True) + mr
            cur_i = jnp.sum(jnp.where(row == src, ei, 0.0), axis=1, keepdims=True) + mi
            cr = jnp.where(row == s, cur_r, cr)
            ci = jnp.where(row == s, cur_i, ci)
        cr_ref[...] = cr
        ci_ref[...] = ci

    st = jax.ShapeDtypeStruct((sb, seg, ns), F32)
    return pl.pallas_call(body, name=name, out_shape=[st, st],
                          compiler_params=pltpu.CompilerParams(vmem_limit_bytes=VMEM_LIMIT_BYTES))(e_re, e_im, lam_re, lam_im)


def _lam_power(pr, pi, n):
    for _ in range(int(math.log2(n))):
        pr, pi = _cmul(pr, pi, pr, pi)
    return pr, pi


def _segment_entries(er, ei, pr, pi, reverse):
    seg, ns = er.shape
    row = lax.broadcasted_iota(jnp.int32, (seg, ns), 0)
    cr = jnp.zeros((seg, ns), F32)
    ci = jnp.zeros((seg, ns), F32)
    cur_r = jnp.zeros((1, ns), F32)
    cur_i = jnp.zeros((1, ns), F32)
    for s in (range(seg - 2, -1, -1) if reverse else range(1, seg)):
        src = s + 1 if reverse else s - 1
        mr, mi = _cmul(pr, pi, cur_r, cur_i)
        cur_r = jnp.sum(jnp.where(row == src, er, 0.0), axis=0, keepdims=True) + mr
        cur_i = jnp.sum(jnp.where(row == src, ei, 0.0), axis=0, keepdims=True) + mi
        cr = jnp.where(row == s, cur_r, cr)
        ci = jnp.where(row == s, cur_i, ci)
    return cr, ci


def s5_scan_fused_fwd(a_p, lam_re, lam_im, wb_re, wb_im, wc_re, wc_im, d_skip, name):
    l = a_p.shape[0]
    di = d_skip.shape[1]
    rows = S5_SEG * S5_TB
    nb = l // rows
    ns = wb_re.shape[2]

    def body(u_ref, lr_ref, li_ref, wbr_ref, wbi_ref, wcr_ref, wci_ref, ds_ref, y_ref, ckr_ref, cki_ref, bur, bui):
        lr = jnp.broadcast_to(lr_ref[0], (S5_SEG, ns))
        li = jnp.broadcast_to(li_ref[0], (S5_SEG, ns))

        def scan_block(b, carry, keep):
            def step(t, c):
                xr, xi = c
                sl = pl.ds(pl.multiple_of(b * rows + t * S5_SEG, S5_SEG), S5_SEG)
                nr = lr * xr - li * xi + bur[sl, :]
                ni = lr * xi + li * xr + bui[sl, :]
                if keep:
                    bur[sl, :] = nr
                    bui[sl, :] = ni
                return nr, ni

            return lax.fori_loop(0, S5_TB, step, carry, unroll=S5_UNROLL)

        def project(b, carry):
            rs = pl.ds(pl.multiple_of(b * rows, rows), rows)
            u = u_ref[rs, :]
            bur[rs, :] = _dot(u, wbr_ref[0], DN_NN)
            bui[rs, :] = _dot(u, wbi_ref[0], DN_NN)
            return scan_block(b, carry, False)

        zero = jnp.zeros((S5_SEG, ns), F32)
        er, ei = lax.fori_loop(0, nb, project, (zero, zero))
        pr, pi = _lam_power(lr_ref[0], li_ref[0], l // S5_SEG)
        entry = _segment_entries(er, ei, pr, pi, False)

        def emit(b, carry):
            ckr_ref[0, b] = carry[0]
            cki_ref[0, b] = carry[1]
            carry = scan_block(b, carry, True)
            rs = pl.ds(pl.multiple_of(b * rows, rows), rows)
            y_ref[rs, :] = (_dot(bur[rs, :], wcr_ref[0], DN_NN) - _dot(bui[rs, :], wci_ref[0], DN_NN)
                            + ds_ref[...] * u_ref[rs, :])
            return carry

        lax.fori_loop(0, nb, emit, entry)

    sb3 = lambda s: (s, 0, 0)
    st = jax.ShapeDtypeStruct
    return pl.pallas_call(
        body, name=name, grid=(S5_SB,),
        in_specs=[pl.BlockSpec((l, LANES), lambda s: (0, s)),
                  pl.BlockSpec((1, 1, ns), sb3), pl.BlockSpec((1, 1, ns), sb3),
                  pl.BlockSpec((1, LANES, ns), sb3), pl.BlockSpec((1, LANES, ns), sb3),
                  pl.BlockSpec((1, ns, LANES), sb3), pl.BlockSpec((1, ns, LANES), sb3),
                  pl.BlockSpec((1, LANES), lambda s: (0, s))],
        out_specs=[pl.BlockSpec((l, LANES), lambda s: (0, s)),
                   pl.BlockSpec((1, nb, S5_SEG, ns), lambda s: (s, 0, 0, 0)),
                   pl.BlockSpec((1, nb, S5_SEG, ns), lambda s: (s, 0, 0, 0))],
        out_shape=[st((l, di), F32), st((S5_SB, nb, S5_SEG, ns), F32), st((S5_SB, nb, S5_SEG, ns), F32)],
        scratch_shapes=[pltpu.VMEM((l, ns), F32), pltpu.VMEM((l, ns), F32)],
        compiler_params=_cparams(("parallel",)))(a_p, lam_re, lam_im, wb_re, wb_im, wc_re, wc_im, d_skip)


def s5_scan_fused_bwd(a_p, dy, lam_re, lam_im, wb_re, wb_im, wc_re, wc_im, d_skip, ck_re, ck_im, name):
    l = a_p.shape[0]
    di = d_skip.shape[1]
    rows = S5_SEG * S5_TB
    nb = l // rows
    ns = wb_re.shape[2]

    def body(u_ref, dy_ref, lr_ref, li_ref, wbr_ref, wbi_ref, wcr_ref, wci_ref, ds_ref, ckr_ref, cki_ref,
             du_ref, dwbr_ref, dwbi_ref, dwcr_ref, dwci_ref, dds_ref, dlr_ref, dli_ref, gr, gi, xr_b, xi_b):
        lr = jnp.broadcast_to(lr_ref[0], (S5_SEG, ns))
        li = jnp.broadcast_to(li_ref[0], (S5_SEG, ns))

        def back_project(k, carry):
            b = nb - 1 - k
            rs = pl.ds(pl.multiple_of(b * rows, rows), rows)
            dyv = dy_ref[rs, :]
            gr[rs, :] = _dot(dyv, wcr_ref[0], DN_NT)
            gi[rs, :] = -_dot(dyv, wci_ref[0], DN_NT)

            def step(kk, c):
                ar, ai = c
                sl = pl.ds(pl.multiple_of(b * rows + (S5_TB - 1 - kk) * S5_SEG, S5_SEG), S5_SEG)
                return gr[sl, :] + lr * ar + li * ai, gi[sl, :] + lr * ai - li * ar

            return lax.fori_loop(0, S5_TB, step, carry, unroll=S5_UNROLL)

        zero = jnp.zeros((S5_SEG, ns), F32)
        er, ei = lax.fori_loop(0, nb, back_project, (zero, zero))
        pr, pi = _lam_power(lr_ref[0], -li_ref[0], l // S5_SEG)
        a0r, a0i = _segment_entries(er, ei, pr, pi, True)

        dwbr_ref[...] = jnp.zeros_like(dwbr_ref)
        dwbi_ref[...] = jnp.zeros_like(dwbi_ref)
        dwcr_ref[...] = jnp.zeros_like(dwcr_ref)
        dwci_ref[...] = jnp.zeros_like(dwci_ref)
        dds_ref[...] = jnp.zeros_like(dds_ref)

        def block(k, carry):
            b = nb - 1 - k
            rs = pl.ds(pl.multiple_of(b * rows, rows), rows)
            u = u_ref[rs, :]
            dyv = dy_ref[rs, :]
            xr_b[...] = _dot(u, wbr_ref[0], DN_NN)
            xi_b[...] = _dot(u, wbi_ref[0], DN_NN)
            x0r, x0i = ckr_ref[0, b], cki_ref[0, b]

            def fstep(t, c):
                xr, xi = c
                sl = pl.ds(pl.multiple_of(t * S5_SEG, S5_SEG), S5_SEG)
                nr = lr * xr - li * xi + xr_b[sl, :]
                ni = lr * xi + li * xr + xi_b[sl, :]
                xr_b[sl, :] = nr
                xi_b[sl, :] = ni
                return nr, ni

            lax.fori_loop(0, S5_TB, fstep, (x0r, x0i), unroll=S5_UNROLL)
            dwcr_ref[0] += _dot(xr_b[...], dyv, DN_TN)
            dwci_ref[0] -= _dot(xi_b[...], dyv, DN_TN)

            def bstep(kk, c):
                ar, ai, dlr, dli = c
                t = S5_TB - 1 - kk
                sl = pl.ds(pl.multiple_of(b * rows + t * S5_SEG, S5_SEG), S5_SEG)
                slp = pl.ds(pl.multiple_of(jnp.maximum(t - 1, 0) * S5_SEG, S5_SEG), S5_SEG)
                nr = gr[sl, :] + lr * ar + li * ai
                ni = gi[sl, :] + lr * ai - li * ar
                gr[sl, :] = nr
                gi[sl, :] = ni
                first = t == 0
                p_r = jnp.where(first, x0r, xr_b[slp, :])
                p_i = jnp.where(first, x0i, xi_b[slp, :])
                return nr, ni, dlr + nr * p_r + ni * p_i, dli + ni * p_r - nr * p_i

            carry = lax.fori_loop(0, S5_TB, bstep, carry, unroll=S5_UNROLL)
            a_r, a_i = gr[rs, :], gi[rs, :]
            du_ref[rs, :] = (_dot(a_r, wbr_ref[0], DN_NT) + _dot(a_i, wbi_ref[0], DN_NT) + ds_ref[...] * dyv).astype(BF16)
            dwbr_ref[0] += _dot(u, a_r, DN_TN)
            dwbi_ref[0] += _dot(u, a_i, DN_TN)
            dds_ref[...] += jnp.sum(dyv * u, axis=0, keepdims=True)
            return carry

        _, _, dlr, dli = lax.fori_loop(0, nb, block, (a0r, a0i, zero, zero))
        dlr_ref[0] = dlr
        dli_ref[0] = dli

    sb3 = lambda s: (s, 0, 0)
    seq = pl.BlockSpec((l, LANES), lambda s: (0, s))
    ck = pl.BlockSpec((1, nb, S5_SEG, ns), lambda s: (s, 0, 0, 0))
    st = jax.ShapeDtypeStruct
    return pl.pallas_call(
        body, name=name, grid=(S5_SB,),
        in_specs=[seq, seq, pl.BlockSpec((1, 1, ns), sb3), pl.BlockSpec((1, 1, ns), sb3),
                  pl.BlockSpec((1, LANES, ns), sb3), pl.BlockSpec((1, LANES, ns), sb3),
                  pl.BlockSpec((1, ns, LANES), sb3), pl.BlockSpec((1, ns, LANES), sb3),
                  pl.BlockSpec((1, LANES), lambda s: (0, s)), ck, ck],
        out_specs=[seq, pl.BlockSpec((1, LANES, ns), sb3), pl.BlockSpec((1, LANES, ns), sb3),
                   pl.BlockSpec((1, ns, LANES), sb3), pl.BlockSpec((1, ns, LANES), sb3),
                   pl.BlockSpec((1, LANES), lambda s: (0, s)),
                   pl.BlockSpec((1, S5_SEG, ns), sb3), pl.BlockSpec((1, S5_SEG, ns), sb3)],
        out_shape=[st((l, di), BF16), st((S5_SB, LANES, ns), F32), st((S5_SB, LANES, ns), F32),
                   st((S5_SB, ns, LANES), F32), st((S5_SB, ns, LANES), F32), st((1, di), F32),
                   st((S5_SB, S5_SEG, ns), F32), st((S5_SB, S5_SEG, ns), F32)],
        scratch_shapes=[pltpu.VMEM((l, ns), F32), pltpu.VMEM((l, ns), F32),
                        pltpu.VMEM((rows, ns), F32), pltpu.VMEM((rows, ns), F32)],
        compiler_params=_cparams(("parallel",)))(
        a_p, dy, lam_re, lam_im, wb_re, wb_im, wc_re, wc_im, d_skip, ck_re, ck_im)


def s5_act(y, name):
    l, d = y.shape
    tl = ROW_TILE

    def body(y_ref, o_ref):
        o_ref[...] = _gelu(y_ref[...]).astype(BF16)

    return _rowcall(body, name, l // tl, [_rows(tl, d)], _rows(tl, d), jax.ShapeDtypeStruct((l, d), BF16))(y)


def s5_gate_fwd(y, t, b_glu, a_p, name):
    l, d = y.shape
    tl = ROW_TILE

    def body(y_ref, t_ref, b_ref, z_ref, m_ref):
        yg = _gelu(y_ref[...])
        m_ref[...] = (yg * _sigmoid(t_ref[...] + b_ref[...]) * _silu(z_ref[...])).astype(BF16)

    return _rowcall(body, name, l // tl, [_rows(tl, d), _rows(tl, d), _full((1, d)), _rows(tl, d, 1)], _rows(tl, d),
                    jax.ShapeDtypeStruct((l, d), BF16))(y, t, b_glu.reshape(1, d), a_p)


def s5_gate_bwd(dm, y, t, b_glu, a_p, name):
    l, d = y.shape
    tl = ROW_TILE

    def body(dm_ref, y_ref, t_ref, b_ref, z_ref, dt_ref, dyg_ref, dz_ref, db_ref):
        i = pl.program_id(0)
        dmv = dm_ref[...]
        z = z_ref[...]
        yg = _gelu(y_ref[...])
        sg = _sigmoid(t_ref[...] + b_ref[...])
        y2 = yg * sg
        sz, gz = _silu_both(z)
        dy2 = dmv * sz
        dz_ref[...] = (dmv * y2 * gz).astype(BF16)
        dyg_ref[...] = dy2 * sg
        dt = dy2 * yg * sg * (1.0 - sg)
        dt_ref[...] = dt.astype(BF16)
        _acc(db_ref, jnp.sum(dt, axis=0, keepdims=True), i)

    st = jax.ShapeDtypeStruct
    return _rowcall(body, name, l // tl, [_rows(tl, d), _rows(tl, d), _rows(tl, d), _full((1, d)), _rows(tl, d, 1)],
                    [_rows(tl, d), _rows(tl, d), _rows(tl, d), _full((1, d))],
                    [st((l, d), BF16), st((l, d), F32), st((l, d), BF16), st((1, d), F32)])(
        dm, y, t, b_glu.reshape(1, d), a_p)


def s5_act_bwd(y, dyg_a, dyg_b, name):
    l, d = y.shape
    tl = ROW_TILE

    def body(y_ref, a_ref, b_ref, o_ref):
        o_ref[...] = (a_ref[...] + b_ref[...]) * _gelu_grad(y_ref[...])

    return _rowcall(body, name, l // tl, [_rows(tl, d)] * 3, _rows(tl, d), jax.ShapeDtypeStruct((l, d), F32))(y, dyg_a, dyg_b)


def _seg_perm(t):
    l, d = t.shape
    return t.reshape(S5_SEG, l // S5_SEG, d).transpose(1, 0, 2).reshape(l, d)


def _seg_unperm(t):
    l, d = t.shape
    return t.reshape(l // S5_SEG, S5_SEG, d).transpose(1, 0, 2).reshape(l, d)


def _s5_weights(p):
    lr, li, bbr, bbi = s5_params_fwd(p["a_re"], p["a_im"], p["log_step"], p["b_re"], p["b_im"])
    ns = 8 * S5_STATE
    lam_re = lr.reshape(S5_SB, 1, ns)
    lam_im = li.reshape(S5_SB, 1, ns)
    to_bd = lambda t: _blockdiag(t.reshape(S5_SB, 8, t.shape[1], t.shape[2]))
    wb_re = to_bd(bbr.transpose(0, 2, 1)).astype(BF16)
    wb_im = to_bd(bbi.transpose(0, 2, 1)).astype(BF16)
    wc_re = to_bd(p["c_re"].transpose(0, 2, 1)).astype(BF16)
    wc_im = to_bd(p["c_im"].transpose(0, 2, 1)).astype(BF16)
    return lam_re, lam_im, wb_re, wb_im, wc_re, wc_im


def s5_layer_fwd(h, p, wf, tag):
    l = h.shape[0]
    di = p["d_skip"].shape[0]
    hn = rms_fwd(h, p["norm_g"], tag + "_rms")
    hn_p = _seg_perm(hn)
    a_p = matmul(hn_p, wf["w_in"], "nn", tag + "_mm_in")
    sw = _s5_weights(p)
    dsk = p["d_skip"].reshape(1, di)
    y, ck_re, ck_im = s5_scan_fused_fwd(a_p, *sw, dsk, tag + "_scan")
    yg = s5_act(y, tag + "_act")
    t = matmul(yg, wf["w_glu"], "nn", tag + "_mm_glu")
    m = s5_gate_fwd(y, t, p["b_glu"], a_p, tag + "_gate")
    out_p = matmul(m, wf["w_out"], "nn", tag + "_mm_out")
    h_out = residual_add(h, _seg_unperm(out_p), tag + "_res")
    return h_out, (h, hn_p, a_p, sw, ck_re, ck_im, y, yg, t, m)


def residual_add(h, y, name):
    l, d = h.shape
    tl = ROW_TILE

    def body(h_ref, y_ref, o_ref):
        o_ref[...] = h_ref[...] + y_ref[...]

    return _rowcall(body, name, l // tl, [_rows(tl, d)] * 2, _rows(tl, d), jax.ShapeDtypeStruct((l, d), F32))(h, y)


def s5_layer_bwd(dh_out, saved, p, wf, tag, sink):
    h, hn_p, a_p, sw, ck_re, ck_im, y, yg, t, m = saved
    l = h.shape[0]
    di = p["d_skip"].shape[0]
    dsk = p["d_skip"].reshape(1, di)
    dout_p = _seg_perm(dh_out)
    dm = matmul(dout_p, wf["w_out"], "nt", tag + "_mm_dm")
    sink.mm("w_out", m, dout_p, tag + "_mm_gwout")
    dt, dyg_a, dz, db_glu = s5_gate_bwd(dm, y, t, p["b_glu"], a_p, tag + "_gate_bwd")
    dyg_b = matmul(dt, wf["w_glu"], "nt", tag + "_mm_dyg")
    sink.mm("w_glu", yg, dt, tag + "_mm_gwglu")
    dy = s5_act_bwd(y, dyg_a, dyg_b, tag + "_act_bwd")
    du, dwbr, dwbi, dwcr, dwci, dds, dlr, dli = s5_scan_fused_bwd(a_p, dy, *sw, dsk, ck_re, ck_im, tag + "_scanb")
    da = jnp.concatenate([du, dz], axis=1)
    dhn_p = matmul(da, wf["w_in"], "nt", tag + "_mm_dhn")
    sink.mm("w_in", hn_p, da, tag + "_mm_gwin")
    zero = sink.send()
    dh, dng = rms_bwd(h, p["norm_g"] + zero, _seg_unperm(dhn_p), dh_out, tag + "_rms_bwd")
    ex = lambda m_, r, c: _blockdiag_extract(m_, r, c).reshape(S5_GROUPS, r, c).transpose(0, 2, 1)
    dbb_re, dbb_im = ex(dwbr, S5_GROUP, S5_STATE), ex(dwbi, S5_GROUP, S5_STATE)
    g_c_re, g_c_im = ex(dwcr, S5_STATE, S5_GROUP), ex(dwci, S5_STATE, S5_GROUP)
    dl_re = lane_sum8(dlr).reshape(S5_GROUPS, S5_STATE, 1)
    dl_im = lane_sum8(dli).reshape(S5_GROUPS, S5_STATE, 1)
    gar, gai, gls, gbr, gbi = s5_params_bwd(p["a_re"], p["a_im"], p["log_step"], p["b_re"], p["b_im"],
                                            dl_re, dl_im, dbb_re, dbb_im)
    grads = {"norm_g": dng.reshape(-1), "a_re": gar.reshape(S5_GROUPS, S5_STATE),
             "a_im": gai.reshape(S5_GROUPS, S5_STATE), "log_step": gls.reshape(-1), "b_re": gbr, "b_im": gbi,
             "c_re": g_c_re, "c_im": g_c_im, "d_skip": dds.reshape(-1), "b_glu": db_glu.reshape(-1)}
    return dh, grads


def lane_sum8(t):
    sb, seg, ns = t.shape

    def body(t_ref, o_ref):
        o_ref[...] = jnp.sum(t_ref[...], axis=1, keepdims=True)

    return pl.pallas_call(body, name="s5_seg_sum", out_shape=jax.ShapeDtypeStruct((sb, 1, ns), F32))(t)


MLA_DI = MLA_HEADS * 128
MLA_CQ0 = MLA_DI
MLA_CKV0 = MLA_CQ0 + MLA_Q_RANK
MLA_KR0 = MLA_CKV0 + MLA_KV_RANK
MLA_AW = MLA_KR0 + LANES


def _rot_half(x):
    w = x.shape[-1]
    lane = lax.broadcasted_iota(jnp.int32, x.shape, x.ndim - 1)
    return jnp.where(lane % MLA_ROPE < MLA_ROPE // 2, pltpu.roll(x, w - MLA_ROPE // 2, x.ndim - 1),
                     pltpu.roll(x, MLA_ROPE // 2, x.ndim - 1))


def rope_tables(pos):
    l = pos.shape[0]
    tl = ROW_TILE
    j = np.arange(LANES) % MLA_ROPE % (MLA_ROPE // 2)
    inv_freq = (ROPE_THETA ** (-(2.0 * j) / MLA_ROPE)).astype(np.float32).reshape(1, LANES)
    sign = np.where(np.arange(LANES) % MLA_ROPE < MLA_ROPE // 2, -1.0, 1.0).astype(np.float32).reshape(1, LANES)

    def body(p_ref, f_ref, s_ref, cos_ref, sin_ref):
        ang = p_ref[...].astype(F32) * f_ref[...]
        cos_ref[...] = jnp.cos(ang)
        sin_ref[...] = jnp.sin(ang) * s_ref[...]

    st = jax.ShapeDtypeStruct((l, LANES), F32)
    return _rowcall(body, "rope_tables", l // tl, [_rows(tl, 1), _full((1, LANES)), _full((1, LANES))],
                    [_rows(tl, LANES)] * 2, [st, st])(pos, jnp.asarray(inv_freq), jnp.asarray(sign))


def _rope(x, cos, sins):
    return x * cos + _rot_half(x) * sins


def _rope_t(dy, cos, sins):
    return dy * cos - sins * _rot_half(dy)


def _rmsn(x):
    r = lax.rsqrt(jnp.mean(x * x, axis=-1, keepdims=True) + NORM_EPS)
    return x * r, r


def mla_pre(a, q_g, kv_g, cos, sins, name):
    l = a.shape[0]
    tl = ROW_TILE

    def body(a_ref, qg_ref, kg_ref, cos_ref, sin_ref, cq_ref, ckv_ref, krs_ref):
        xq, _ = _rmsn(a_ref[:, MLA_CQ0:MLA_CKV0])
        cq_ref[...] = (xq * qg_ref[...]).astype(BF16)
        xk, _ = _rmsn(a_ref[:, MLA_CKV0:MLA_KR0])
        ckv_ref[...] = (xk * kg_ref[...]).astype(BF16)
        kr = a_ref[:, MLA_KR0:MLA_AW]
        kr2 = kr + pltpu.roll(kr, MLA_ROPE, 1)
        kr2 = _rope(kr2, cos_ref[...], sin_ref[...])
        lane = lax.broadcasted_iota(jnp.int32, kr2.shape, 1)
        krs_ref[0] = jnp.where(lane < MLA_ROPE, kr2, 0.0).astype(BF16)
        krs_ref[1] = jnp.where(lane >= MLA_ROPE, kr2, 0.0).astype(BF16)

    st = jax.ShapeDtypeStruct
    return _rowcall(body, name, l // tl,
                    [_rows(tl, MLA_AW), _full((1, MLA_Q_RANK)), _full((1, MLA_KV_RANK)), _rows(tl, LANES), _rows(tl, LANES)],
                    [_rows(tl, MLA_Q_RANK), _rows(tl, MLA_KV_RANK), pl.BlockSpec((2, tl, LANES), lambda i: (0, i, 0))],
                    [st((l, MLA_Q_RANK), BF16), st((l, MLA_KV_RANK), BF16), st((2, l, LANES), BF16)])(
        a, q_g.reshape(1, -1), kv_g.reshape(1, -1), cos, sins)


def mla_rope_q(qr, cos, sins, name):
    l, w = qr.shape
    tl = ROW_TILE

    def body(q_ref, cos_ref, sin_ref, o_ref):
        c, s = cos_ref[...], sin_ref[...]
        for p in range(w // LANES):
            sl = slice(p * LANES, (p + 1) * LANES)
            o_ref[:, sl] = _rope(q_ref[:, sl], c, s).astype(BF16)

    return _rowcall(body, name, l // tl, [_rows(tl, w), _rows(tl, LANES), _rows(tl, LANES)], _rows(tl, w),
                    jax.ShapeDtypeStruct((l, w), BF16))(qr, cos, sins)


ATT_OUT = 512
ATT_IN = 512
ATT_R = ATT_OUT // ATT_IN


def _scores(qn, qr, kn, kr, mask_off, transposed):
    q2 = jnp.concatenate([qn, qr], axis=1)
    k2 = jnp.concatenate([kn, kr], axis=1)
    s = (_dot(k2, q2, DN_NT) if transposed else _dot(q2, k2, DN_NT)) * MLA_SCALE
    if mask_off is None:
        return s
    r = lax.broadcasted_iota(jnp.int32, s.shape, 0)
    c = lax.broadcasted_iota(jnp.int32, s.shape, 1)
    return jnp.where((r <= c + mask_off) if transposed else (c + mask_off <= r), s, NEG_INF)


def _fold(x, op):
    out = x[:, :LANES]
    for t in range(1, x.shape[1] // LANES):
        out = op(out, x[:, t * LANES:(t + 1) * LANES])
    return out


def flash_fwd(qn, qr, kv, krs, name):
    l = qn.shape[0]
    nq = l // ATT_OUT

    def body(qn_ref, qr_ref, kv_ref, kr_ref, o_ref, lse_ref, s_buf):
        qi = pl.program_id(1)
        q_r = qr_ref[...]
        q_n = [qn_ref[:, hh * LANES:(hh + 1) * LANES] for hh in range(2)]

        def block_scores(j, mx, mask_off):
            sl = pl.ds(pl.multiple_of(j * ATT_IN, ATT_IN), ATT_IN)
            out = []
            for hh in range(2):
                s = _scores(q_n[hh], q_r, kv_ref[sl, 2 * hh * LANES:(2 * hh + 1) * LANES], kr_ref[hh, sl, :],
                            mask_off, False)
                s_buf[hh, j] = s
                out.append(jnp.maximum(mx[hh], _fold(s, jnp.maximum)))
            return tuple(out)

        ninf = jnp.full((ATT_OUT, LANES), NEG_INF, F32)
        mx = lax.fori_loop(0, ATT_R * qi, lambda j, c: block_scores(j, c, None), (ninf, ninf))
        for d in range(ATT_R):
            mx = block_scores(ATT_R * qi + d, mx, d * ATT_IN)
        m = [jnp.max(mx[hh], axis=-1, keepdims=True) for hh in range(2)]

        def block_pv(j, carry):
            sl = pl.ds(pl.multiple_of(j * ATT_IN, ATT_IN), ATT_IN)
            out = []
            for hh in range(2):
                ls, acc = carry[hh]
                p = jnp.exp(s_buf[hh, j] - m[hh])
                out.append((ls + _fold(p, jnp.add),
                            acc + _dot(p, kv_ref[sl, (2 * hh + 1) * LANES:(2 * hh + 2) * LANES], DN_NN)))
            return tuple(out)

        z = jnp.zeros((ATT_OUT, LANES), F32)
        res = lax.fori_loop(0, ATT_R * (qi + 1), block_pv, ((z, z), (z, z)))
        for hh in range(2):
            lsum = jnp.sum(res[hh][0], axis=-1, keepdims=True)
            o_ref[:, hh * LANES:(hh + 1) * LANES] = res[hh][1] / lsum
            lse_ref[hh] = m[hh] + jnp.log(lsum)

    st = jax.ShapeDtypeStruct
    return pl.pallas_call(
        body, name=name, grid=(MLA_HEADS // 2, nq),
        in_specs=[pl.BlockSpec((ATT_OUT, 2 * LANES), lambda p, i: (i, p)),
                  pl.BlockSpec((ATT_OUT, LANES), lambda p, i: (i, p)),
                  pl.BlockSpec((l, 4 * LANES), lambda p, i: (0, p)),
                  pl.BlockSpec((2, l, LANES), lambda p, i: (0, 0, 0))],
        out_specs=[pl.BlockSpec((ATT_OUT, 2 * LANES), lambda p, i: (i, p)),
                   pl.BlockSpec((2, ATT_OUT, 1), lambda p, i: (p, i, 0))],
        out_shape=[st((l, MLA_DI), F32), st((MLA_HEADS, l, 1), F32)],
        scratch_shapes=[pltpu.VMEM((2, l // ATT_IN, ATT_OUT, ATT_IN), F32)],
        compiler_params=_cparams(("parallel", "arbitrary")))(qn, qr, kv, krs)


def flash_dkv(qn, qr, kv, krs, do, lse_row, delta_row, name):
    l = qn.shape[0]
    nk = l // ATT_OUT
    nq = l // ATT_IN

    def body(qn_ref, qr_ref, do_ref, lse_ref, dl_ref, kv_ref, kr_ref, dkv_ref, dkr_ref):
        kj = pl.program_id(1)
        lane = lax.broadcasted_iota(jnp.int32, (ATT_OUT, LANES), 1)
        kn = [kv_ref[:, 2 * hh * LANES:(2 * hh + 1) * LANES] for hh in range(2)]
        v = [kv_ref[:, (2 * hh + 1) * LANES:(2 * hh + 2) * LANES] for hh in range(2)]

        def block(i, carry, mask_off):
            sl = pl.ds(pl.multiple_of(i * ATT_IN, ATT_IN), ATT_IN)
            q_r = qr_ref[sl, :]
            out = []
            for hh in range(2):
                dk2, dv = carry[hh]
                hs = slice(hh * LANES, (hh + 1) * LANES)
                q_n, d_o = qn_ref[sl, hs], do_ref[sl, hs]
                s = _scores(q_n, q_r, kn[hh], kr_ref[hh], mask_off, True)
                pt = jnp.exp(s - lse_ref[hh, i])
                dv = dv + _dot(pt, d_o, DN_NN)
                dpt = _dot(v[hh], d_o, DN_NT)
                dst = (pt * (dpt - dl_ref[hh, i]) * MLA_SCALE).astype(BF16)
                out.append((dk2 + _dot(dst, jnp.concatenate([q_n, q_r], axis=1), DN_NN), dv))
            return tuple(out)

        z = jnp.zeros((ATT_OUT, LANES), F32)
        z2 = jnp.zeros((ATT_OUT, 2 * LANES), F32)
        res = ((z2, z), (z2, z))
        for d in range(ATT_R):
            res = block(ATT_R * kj + d, res, d * ATT_IN)
        res = lax.fori_loop(ATT_R * (kj + 1), nq, lambda i, c: block(i, c, None), res)
        for hh in range(2):
            dkv_ref[:, 2 * hh * LANES:(2 * hh + 1) * LANES] = res[hh][0][:, :LANES].astype(BF16)
            dkv_ref[:, (2 * hh + 1) * LANES:(2 * hh + 2) * LANES] = res[hh][1].astype(BF16)
        dkr_ref[0] = jnp.where(lane < MLA_ROPE, res[0][0][:, LANES:], res[1][0][:, LANES:])

    st = jax.ShapeDtypeStruct
    return pl.pallas_call(
        body, name=name, grid=(MLA_HEADS // 2, nk),
        in_specs=[pl.BlockSpec((l, 2 * LANES), lambda p, j: (0, p)),
                  pl.BlockSpec((l, LANES), lambda p, j: (0, p)),
                  pl.BlockSpec((l, 2 * LANES), lambda p, j: (0, p)),
                  pl.BlockSpec((2, nq, 1, ATT_IN), lambda p, j: (p, 0, 0, 0)),
                  pl.BlockSpec((2, nq, 1, ATT_IN), lambda p, j: (p, 0, 0, 0)),
                  pl.BlockSpec((ATT_OUT, 4 * LANES), lambda p, j: (j, p)),
                  pl.BlockSpec((2, ATT_OUT, LANES), lambda p, j: (0, j, 0))],
        out_specs=[pl.BlockSpec((ATT_OUT, 4 * LANES), lambda p, j: (j, p)),
                   pl.BlockSpec((1, ATT_OUT, LANES), lambda p, j: (p, j, 0))],
        out_shape=[st((l, 2 * MLA_DI), BF16), st((MLA_HEADS // 2, l, LANES), F32)],
        compiler_params=_cparams(("parallel", "arbitrary")))(qn, qr, do, lse_row, delta_row, kv, krs)


def flash_dq(qn, qr, kv, krs, do, lse, delta, cos, sins, name):
    l = qn.shape[0]
    nq = l // ATT_OUT

    def body(qn_ref, qr_ref, do_ref, lse_ref, dl_ref, kv_ref, kr_ref, cos_ref, sin_ref, dqn_ref, dqr_ref):
        qi = pl.program_id(1)
        q_r = qr_ref[...]
        q_n = [qn_ref[:, hh * LANES:(hh + 1) * LANES] for hh in range(2)]
        d_o = [do_ref[:, hh * LANES:(hh + 1) * LANES] for hh in range(2)]
        lse_h = [lse_ref[hh] for hh in range(2)]
        dl_h = [dl_ref[hh] for hh in range(2)]

        def block(j, carry, mask_off):
            sl = pl.ds(pl.multiple_of(j * ATT_IN, ATT_IN), ATT_IN)
            dq2 = list(carry)
            for hh in range(2):
                kn = kv_ref[sl, 2 * hh * LANES:(2 * hh + 1) * LANES]
                v = kv_ref[sl, (2 * hh + 1) * LANES:(2 * hh + 2) * LANES]
                kr = kr_ref[hh, sl, :]
                s = _scores(q_n[hh], q_r, kn, kr, mask_off, False)
                pr = jnp.exp(s - lse_h[hh])
                dp = _dot(d_o[hh], v, DN_NT)
                ds = (pr * (dp - dl_h[hh]) * MLA_SCALE).astype(BF16)
                dq2[hh] = dq2[hh] + _dot(ds, jnp.concatenate([kn, kr], axis=1), DN_NN)
            return tuple(dq2)

        z2 = jnp.zeros((ATT_OUT, 2 * LANES), F32)
        res = lax.fori_loop(0, ATT_R * qi, lambda j, c: block(j, c, None), (z2, z2))
        for d in range(ATT_R):
            res = block(ATT_R * qi + d, res, d * ATT_IN)
        dqn_ref[:, 0:LANES] = res[0][:, :LANES].astype(BF16)
        dqn_ref[:, LANES:2 * LANES] = res[1][:, :LANES].astype(BF16)
        dqr = res[0][:, LANES:] + res[1][:, LANES:]
        dqr_ref[...] = _rope_t(dqr, cos_ref[...], sin_ref[...]).astype(BF16)

    st = jax.ShapeDtypeStruct
    return pl.pallas_call(
        body, name=name, grid=(MLA_HEADS // 2, nq),
        in_specs=[pl.BlockSpec((ATT_OUT, 2 * LANES), lambda p, i: (i, p)),
                  pl.BlockSpec((ATT_OUT, LANES), lambda p, i: (i, p)),
                  pl.BlockSpec((ATT_OUT, 2 * LANES), lambda p, i: (i, p)),
                  pl.BlockSpec((2, ATT_OUT, 1), lambda p, i: (p, i, 0)),
                  pl.BlockSpec((2, ATT_OUT, 1), lambda p, i: (p, i, 0)),
                  pl.BlockSpec((l, 4 * LANES), lambda p, i: (0, p)),
                  pl.BlockSpec((2, l, LANES), lambda p, i: (0, 0, 0)),
                  pl.BlockSpec((ATT_OUT, LANES), lambda p, i: (i, 0)),
                  pl.BlockSpec((ATT_OUT, LANES), lambda p, i: (i, 0))],
        out_specs=[pl.BlockSpec((ATT_OUT, 2 * LANES), lambda p, i: (i, p)),
                   pl.BlockSpec((ATT_OUT, LANES), lambda p, i: (i, p))],
        out_shape=[st((l, MLA_DI), BF16), st((l, MLA_HEADS * MLA_ROPE), BF16)],
        compiler_params=_cparams(("parallel", "arbitrary")))(qn, qr, do, lse, delta, kv, krs, cos, sins)


def mla_gate_fwd(o, a, name):
    l = o.shape[0]
    tl = ROW_TILE

    def body(o_ref, z_ref, m_ref):
        m_ref[...] = (o_ref[...] * _silu(z_ref[...])).astype(BF16)

    return _rowcall(body, name, l // tl, [_rows(tl, MLA_DI), _rows(tl, MLA_DI)], _rows(tl, MLA_DI),
                    jax.ShapeDtypeStruct((l, MLA_DI), BF16))(o, a)


def mla_gate_bwd(dm, o, a, name):
    l = o.shape[0]
    tl = ROW_TILE

    def body(dm_ref, o_ref, z_ref, do_ref, dz_ref, dl_ref):
        dmv, ov, z = dm_ref[...], o_ref[...], z_ref[...]
        sz, gz = _silu_both(z)
        d_o = dmv * sz
        do_ref[...] = d_o.astype(BF16)
        dz_ref[...] = (dmv * ov * gz).astype(BF16)
        pr = d_o * ov
        for h in range(MLA_HEADS):
            dl_ref[h] = jnp.sum(pr[:, h * LANES:(h + 1) * LANES], axis=1, keepdims=True)

    st = jax.ShapeDtypeStruct
    return _rowcall(body, name, l // tl, [_rows(tl, MLA_DI)] * 3,
                    [_rows(tl, MLA_DI), _rows(tl, MLA_DI), pl.BlockSpec((MLA_HEADS, tl, 1), lambda i: (0, i, 0))],
                    [st((l, MLA_DI), BF16), st((l, MLA_DI), BF16), st((MLA_HEADS, l, 1), F32)])(dm, o, a)


def mla_post(a, dcqn, dckvn, dkr_pairs, dz, q_g, kv_g, cos, sins, name):
    l = a.shape[0]
    tl = ROW_TILE
    npair = MLA_HEADS // 2

    def norm_bwd(x, g, dy):
        xhat, r = _rmsn(x)
        dxh = dy * g
        return r * (dxh - xhat * jnp.mean(dxh * xhat, axis=-1, keepdims=True)), jnp.sum(dy * xhat, axis=0, keepdims=True)

    def body(a_ref, dq_ref, dk_ref, dkr_ref, dz_ref, qg_ref, kg_ref, cos_ref, sin_ref, da_ref, dqg_ref, dkg_ref):
        i = pl.program_id(0)
        da_ref[:, 0:MLA_DI] = dz_ref[...]
        dcq, dqg = norm_bwd(a_ref[:, MLA_CQ0:MLA_CKV0], qg_ref[...], dq_ref[...])
        da_ref[:, MLA_CQ0:MLA_CKV0] = dcq.astype(BF16)
        dckv, dkg = norm_bwd(a_ref[:, MLA_CKV0:MLA_KR0], kg_ref[...], dk_ref[...])
        da_ref[:, MLA_CKV0:MLA_KR0] = dckv.astype(BF16)
        dk2 = dkr_ref[0]
        for p in range(1, npair):
            dk2 = dk2 + dkr_ref[p]
        dk2 = _rope_t(dk2, cos_ref[...], sin_ref[...])
        dk2 = dk2 + pltpu.roll(dk2, MLA_ROPE, 1)
        lane = lax.broadcasted_iota(jnp.int32, dk2.shape, 1)
        da_ref[:, MLA_KR0:MLA_AW] = jnp.where(lane < MLA_ROPE, dk2, 0.0).astype(BF16)
        _acc(dqg_ref, dqg, i)
        _acc(dkg_ref, dkg, i)

    st = jax.ShapeDtypeStruct
    return _rowcall(body, name, l // tl,
                    [_rows(tl, MLA_AW), _rows(tl, MLA_Q_RANK), _rows(tl, MLA_KV_RANK),
                     pl.BlockSpec((npair, tl, LANES), lambda i: (0, i, 0)), _rows(tl, MLA_DI),
                     _full((1, MLA_Q_RANK)), _full((1, MLA_KV_RANK)), _rows(tl, LANES), _rows(tl, LANES)],
                    [_rows(tl, MLA_AW), _full((1, MLA_Q_RANK)), _full((1, MLA_KV_RANK))],
                    [st((l, MLA_AW), BF16), st((1, MLA_Q_RANK), F32), st((1, MLA_KV_RANK), F32)])(
        a, dcqn, dckvn, dkr_pairs, dz, q_g.reshape(1, -1), kv_g.reshape(1, -1), cos, sins)


def _mla_w_in_perm(w):
    r = MLA_Q_RANK + MLA_KV_RANK + MLA_ROPE
    pad = jnp.zeros(w.shape[:-1] + (MLA_AW - MLA_KR0 - MLA_ROPE,), w.dtype)
    return jnp.concatenate([w[..., r:], w[..., :r], pad], axis=-1)


def _mla_w_in_unperm(g):
    r = MLA_Q_RANK + MLA_KV_RANK + MLA_ROPE
    return jnp.concatenate([g[..., MLA_DI:MLA_DI + r], g[..., :MLA_DI]], axis=-1)


def _mla_w_uq_split(w):
    k = w.shape[0]
    w3 = w.reshape(k, MLA_HEADS, MLA_NOPE + MLA_ROPE)
    return w3[:, :, :MLA_NOPE].reshape(k, MLA_HEADS * MLA_NOPE), w3[:, :, MLA_NOPE:].reshape(k, MLA_HEADS * MLA_ROPE)


def _mla_w_uq_merge(gn, gr):
    k = gn.shape[0]
    return jnp.concatenate([gn.reshape(k, MLA_HEADS, MLA_NOPE), gr.reshape(k, MLA_HEADS, MLA_ROPE)], axis=2).reshape(k, -1)


def mla_layer_fwd(h, p, wf, cos, sins, tag):
    hn = rms_fwd(h, p["norm_g"], tag + "_rms")
    w_in = _mla_w_in_perm(wf["w_in"])
    w_uq_n, w_uq_r = _mla_w_uq_split(wf["w_uq"])
    a = matmul(hn, w_in, "nn", tag + "_mm_in")
    cqn, ckvn, krs = mla_pre(a, p["q_norm_g"], p["kv_norm_g"], cos, sins, tag + "_pre")
    qn = matmul(cqn, w_uq_n, "nn", tag + "_mm_qn", out_dtype=BF16)
    qr_raw = matmul(cqn, w_uq_r, "nn", tag + "_mm_qr")
    qr = mla_rope_q(qr_raw, cos, sins, tag + "_rope_q")
    kv = matmul(ckvn, wf["w_ukv"], "nn", tag + "_mm_kv", out_dtype=BF16)
    o, lse = flash_fwd(qn, qr, kv, krs, tag + "_flash")
    m = mla_gate_fwd(o, a, tag + "_gate")
    h_out = matmul(m, wf["w_out"], "nn", tag + "_mm_out", add=h)
    return h_out, (h, hn, a, cqn, ckvn, krs, qn, qr, kv, o, lse, m, w_in, w_uq_n, w_uq_r)


def mla_layer_bwd(dh_out, saved, p, wf, cos, sins, tag, sink):
    h, hn, a, cqn, ckvn, krs, qn, qr, kv, o, lse, m, w_in, w_uq_n, w_uq_r = saved
    l = h.shape[0]
    dm = matmul(dh_out, wf["w_out"], "nt", tag + "_mm_dm")
    sink.mm("w_out", m, dh_out, tag + "_mm_gwout")
    do, dz, delta = mla_gate_bwd(dm, o, a, tag + "_gate_bwd")
    lse_row = lse.reshape(MLA_HEADS, l // ATT_IN, 1, ATT_IN)
    delta_row = delta.reshape(MLA_HEADS, l // ATT_IN, 1, ATT_IN)
    dkv, dkr_pairs = flash_dkv(qn, qr, kv, krs, do, lse_row, delta_row, tag + "_flash_dkv")
    dqn, dqr = flash_dq(qn, qr, kv, krs, do, lse, delta, cos, sins, tag + "_flash_dq")
    dcqn = matmul(dqn, w_uq_n, "nt", tag + "_mm_dcq_n")
    dcqn = matmul(dqr, w_uq_r, "nt", tag + "_mm_dcq_r", add=dcqn)
    g_uq_n = matmul(cqn, dqn, "tn", tag + "_mm_guq_n")
    g_uq_r = matmul(cqn, dqr, "tn", tag + "_mm_guq_r")
    dckvn = matmul(dkv, wf["w_ukv"], "nt", tag + "_mm_dckv")
    sink.mm("w_ukv", ckvn, dkv, tag + "_mm_gukv")
    da, dqg, dkg = mla_post(a, dcqn, dckvn, dkr_pairs, dz, p["q_norm_g"], p["kv_norm_g"], cos, sins, tag + "_post")
    dhn = matmul(da, w_in, "nt", tag + "_mm_dhn")
    g_w_in = matmul(hn, da, "tn", tag + "_mm_gwin")
    sink.put("w_uq", _mla_w_uq_merge(g_uq_n, g_uq_r))
    sink.put("w_in", _mla_w_in_unperm(g_w_in))
    zero = sink.send()
    dh, dng = rms_bwd(h, p["norm_g"] + zero, dhn, dh_out, tag + "_rms_bwd")
    grads = {"norm_g": dng.reshape(-1), "q_norm_g": dqg.reshape(-1), "kv_norm_g": dkg.reshape(-1)}
    return dh, grads


ANY = pl.BlockSpec(memory_space=pl.ANY)


def _me():
    return lax.axis_index("x"), lax.axis_index("y"), lax.axis_index("c")


def _chip():
    return 2 * lax.axis_index("x") + lax.axis_index("y")


def _other_chips(x, y):
    return [(1 - x, y), (x, 1 - y), (1 - x, 1 - y)]


def _rcopy(src, dst, ssem, rsem, dev):
    return pltpu.make_async_remote_copy(src_ref=src, dst_ref=dst, send_sem=ssem, recv_sem=rsem,
                                        device_id=dev, device_id_type=MESH)


def _half(ref, c, hf):
    return ref.at[pl.ds(c * hf, hf), :]


def weights_allgather(wb):
    nr, w = wb.shape
    hf = nr // 2

    def body(w_ref, o_ref, ssem, rsem):
        x, y, c = _me()
        k = 2 * x + y
        chips = _other_chips(x, y)
        first = [_rcopy(_half(w_ref, c, hf), _half(o_ref.at[k], c, hf), ssem.at[j], rsem.at[j], (cx, cy, c))
                 for j, (cx, cy) in enumerate(chips)]
        for cp in first:
            cp.start()
        passed = []
        for j, (cx, cy) in enumerate(chips):
            region = _half(o_ref.at[2 * cx + cy], c, hf)
            _rcopy(region, region, ssem.at[j], rsem.at[j], (cx, cy, c)).wait_recv()
            fwd = _rcopy(region, region, ssem.at[3 + j], rsem.at[3 + j], (x, y, 1 - c))
            fwd.start()
            passed.append(fwd)
        for j, (cx, cy) in enumerate(chips):
            region = _half(o_ref.at[2 * cx + cy], 1 - c, hf)
            _rcopy(region, region, ssem.at[3 + j], rsem.at[3 + j], (x, y, 1 - c)).wait_recv()
        for cp in first + passed:
            cp.wait_send()

    out = pl.pallas_call(
        body, name="weights_allgather", in_specs=[ANY], out_specs=ANY,
        out_shape=jax.ShapeDtypeStruct((N_CHIPS, nr, w), wb.dtype),
        scratch_shapes=[pltpu.SemaphoreType.DMA((6,)), pltpu.SemaphoreType.DMA((6,))],
    )(wb)
    return lax.dynamic_update_slice(out, wb[None], (_chip(), 0, 0))


HBM = pl.BlockSpec(memory_space=pltpu.HBM)
SEM = pl.BlockSpec(memory_space=pltpu.SEMAPHORE)
SPLIT_EFFECT = pltpu.SideEffectType.DATAFLOW_SIDE_EFFECTING


def gather_start(wb, after, name):
    nr, w = wb.shape
    hf = nr // 2

    def body(w_ref, land_ref, after_ref, ssem, rsem, w_thru, land_thru, token):
        x, y, c = _me()
        k = 2 * x + y
        for j, (cx, cy) in enumerate(_other_chips(x, y)):
            _rcopy(_half(w_ref, c, hf), _half(land_ref.at[k], c, hf), ssem.at[j], rsem.at[j], (cx, cy, c)).start()
        token[...] = jnp.zeros_like(token)

    land = lax.empty((N_CHIPS, nr, w), wb.dtype)
    return pl.pallas_call(
        body, name=name,
        out_shape=(pltpu.SemaphoreType.DMA((3,)), pltpu.SemaphoreType.DMA((3,)), pltpu.HBM(wb.shape, wb.dtype),
                   pltpu.HBM(land.shape, land.dtype), jax.ShapeDtypeStruct((8, LANES), F32)),
        in_specs=(HBM, HBM, ANY), out_specs=(SEM, SEM, HBM, HBM, pl.BlockSpec(memory_space=pltpu.VMEM)),
        input_output_aliases={0: 2, 1: 3},
        compiler_params=pltpu.CompilerParams(has_side_effects=SPLIT_EFFECT))(
        pltpu.with_memory_space_constraint(wb, pltpu.HBM), pltpu.with_memory_space_constraint(land, pltpu.HBM), after)


def gather_wait(ssem, rsem, w_thru, land_thru, after, name):
    nr, w = w_thru.shape
    hf = nr // 2

    def body(w_ref, land_ref, ssem_ref, rsem_ref, after_ref, w_dead, got_ref):
        x, y, c = _me()
        for j, (cx, cy) in enumerate(_other_chips(x, y)):
            cp = _rcopy(_half(w_ref, c, hf), _half(land_ref.at[2 * cx + cy], c, hf), ssem_ref.at[j], rsem_ref.at[j],
                        (cx, cy, c))
            cp.wait_send()
            cp.wait_recv()

    return pl.pallas_call(
        body, name=name, out_shape=(pltpu.HBM(w_thru.shape, w_thru.dtype), pltpu.HBM(land_thru.shape, land_thru.dtype)),
        in_specs=(HBM, HBM, SEM, SEM, ANY), out_specs=(HBM, HBM), input_output_aliases={0: 0, 1: 1},
        compiler_params=pltpu.CompilerParams(has_side_effects=SPLIT_EFFECT))(w_thru, land_thru, ssem, rsem, after)[1]


def gather_handover(land, wb, name):
    _, nr, w = land.shape
    hf = nr // 2

    def body(l_ref, o_ref, ssem, rsem):
        x, y, c = _me()
        chips = _other_chips(x, y)
        sends = []
        for j, (cx, cy) in enumerate(chips):
            region = _half(o_ref.at[2 * cx + cy], c, hf)
            sends.append(_rcopy(region, region, ssem.at[j], rsem.at[j], (x, y, 1 - c)))
            sends[-1].start()
        for j, (cx, cy) in enumerate(chips):
            region = _half(o_ref.at[2 * cx + cy], 1 - c, hf)
            _rcopy(region, region, ssem.at[j], rsem.at[j], (x, y, 1 - c)).wait_recv()
        for cp in sends:
            cp.wait_send()

    out = pl.pallas_call(
        body, name=name, in_specs=[ANY], out_specs=ANY, input_output_aliases={0: 0},
        out_shape=jax.ShapeDtypeStruct(land.shape, land.dtype),
        scratch_shapes=[pltpu.SemaphoreType.DMA((3,)), pltpu.SemaphoreType.DMA((3,))])(land)
    return lax.dynamic_update_slice(out, wb[None], (_chip(), 0, 0))


def reduce_start(t, after, name):
    def body(t_ref, land_ref, after_ref, ssem, rsem, t_thru, land_thru, token):
        x, y, c = _me()
        k = 2 * x + y
        for j, (cx, cy) in enumerate(_other_chips(x, y)):
            _rcopy(t_ref.at[2 * cx + cy], land_ref.at[k], ssem.at[j], rsem.at[j], (cx, cy, c)).start()
        token[...] = jnp.zeros_like(token)

    land = lax.empty(t.shape, t.dtype)
    return pl.pallas_call(
        body, name=name,
        out_shape=(pltpu.SemaphoreType.DMA((3,)), pltpu.SemaphoreType.DMA((3,)), pltpu.HBM(t.shape, t.dtype),
                   pltpu.HBM(t.shape, t.dtype), jax.ShapeDtypeStruct((8, LANES), F32)),
        in_specs=(HBM, HBM, ANY), out_specs=(SEM, SEM, HBM, HBM, pl.BlockSpec(memory_space=pltpu.VMEM)),
        input_output_aliases={0: 2, 1: 3},
        compiler_params=pltpu.CompilerParams(has_side_effects=SPLIT_EFFECT))(
        pltpu.with_memory_space_constraint(t, pltpu.HBM), pltpu.with_memory_space_constraint(land, pltpu.HBM), after)


def reduce_wait(ssem, rsem, t_thru, land_thru, after, name):
    def body(t_ref, land_ref, ssem_ref, rsem_ref, after_ref, t_out, got_ref):
        x, y, c = _me()
        k = 2 * x + y
        for j, (cx, cy) in enumerate(_other_chips(x, y)):
            cp = _rcopy(t_ref.at[k], land_ref.at[2 * cx + cy], ssem_ref.at[j], rsem_ref.at[j], (cx, cy, c))
            cp.wait_send()
            cp.wait_recv()

    return pl.pallas_call(
        body, name=name, out_shape=(pltpu.HBM(t_thru.shape, t_thru.dtype), pltpu.HBM(land_thru.shape, land_thru.dtype)),
        in_specs=(HBM, HBM, SEM, SEM, ANY), out_specs=(HBM, HBM), input_output_aliases={0: 0, 1: 1},
        compiler_params=pltpu.CompilerParams(has_side_effects=SPLIT_EFFECT))(t_thru, land_thru, ssem, rsem, after)


def grads_to_sibling(ps, name="grads_to_sibling"):
    n = len(ps)

    def body(*refs):
        p_refs, o_refs, ssem, rsem = refs[:n], refs[n:2 * n], refs[2 * n], refs[2 * n + 1]
        x, y, c = _me()
        cps = []
        for a in range(n):
            hf = ps[a].shape[1] // 2
            cps.append(_rcopy(p_refs[a].at[:, pl.ds((1 - c) * hf, hf), :], o_refs[a], ssem.at[a], rsem.at[a],
                              (x, y, 1 - c)))
        for cp in cps:
            cp.start()
        for cp in cps:
            cp.wait()

    return pl.pallas_call(
        body, name=name, in_specs=[ANY] * n, out_specs=[ANY] * n,
        out_shape=[jax.ShapeDtypeStruct((N_CHIPS, p.shape[1] // 2, p.shape[2]), p.dtype) for p in ps],
        scratch_shapes=[pltpu.SemaphoreType.DMA((n,)), pltpu.SemaphoreType.DMA((n,))])(*ps)


def pair_sum(p, ra, out_dtype, name):
    _, nr, w = p.shape
    hf = nr // 2
    tr = _pick_rows(hf)
    nb = hf // tr

    def body(c_ref, p_ref, r_ref, o_ref):
        o_ref[...] = (p_ref[...] + r_ref[...]).astype(out_dtype)

    c = lax.axis_index("c").astype(jnp.int32).reshape(1)
    return pl.pallas_call(
        body, name=name,
        grid_spec=pltpu.PrefetchScalarGridSpec(
            num_scalar_prefetch=1, grid=(N_CHIPS, nb),
            in_specs=[pl.BlockSpec((1, tr, w), lambda k, i, c_ref: (k, c_ref[0] * nb + i, 0)),
                      pl.BlockSpec((1, tr, w), lambda k, i, c_ref: (k, i, 0))],
            out_specs=pl.BlockSpec((1, tr, w), lambda k, i, c_ref: (k, i, 0))),
        out_shape=jax.ShapeDtypeStruct((N_CHIPS, hf, w), out_dtype),
        compiler_params=_cparams(("parallel", "parallel")))(c, p, ra)


def grads_across_chips(ts):
    n = len(ts)

    def body(*refs):
        t_refs, o_refs, ssem, rsem = refs[:n], refs[n:2 * n], refs[2 * n], refs[2 * n + 1]
        x, y, c = _me()
        k = 2 * x + y
        chips = _other_chips(x, y)
        sends = [_rcopy(t_refs[a].at[2 * cx + cy], o_refs[a].at[k], ssem.at[3 * a + j], rsem.at[3 * a + j], (cx, cy, c))
                 for a in range(n) for j, (cx, cy) in enumerate(chips)]
        for cp in sends:
            cp.start()
        for a in range(n):
            for j, (cx, cy) in enumerate(chips):
                _rcopy(t_refs[a].at[k], o_refs[a].at[2 * cx + cy], ssem.at[3 * a + j], rsem.at[3 * a + j],
                       (cx, cy, c)).wait_recv()
        for cp in sends:
            cp.wait_send()

    return pl.pallas_call(
        body, name="grads_across_chips", in_specs=[ANY] * n, out_specs=[ANY] * n,
        out_shape=[jax.ShapeDtypeStruct(t.shape, t.dtype) for t in ts],
        scratch_shapes=[pltpu.SemaphoreType.DMA((3 * n,)), pltpu.SemaphoreType.DMA((3 * n,))])(*ts)


def chip_sum(t, rb, name):
    _, hf, w = rb.shape
    tr = _pick_rows(hf)
    nb = hf // tr

    def body(kc_ref, t_ref, r_ref, o_ref):
        k = kc_ref[0]
        acc = jnp.where(k == 0, t_ref[0], r_ref[0]).astype(F32)
        for j in range(1, N_CHIPS):
            acc = acc + jnp.where(k == j, t_ref[0], r_ref[j]).astype(F32)
        o_ref[...] = acc

    kc = jnp.stack([_chip(), lax.axis_index("c")]).astype(jnp.int32)
    return pl.pallas_call(
        body, name=name,
        grid_spec=pltpu.PrefetchScalarGridSpec(
            num_scalar_prefetch=1, grid=(nb,),
            in_specs=[pl.BlockSpec((1, tr, w), lambda i, kc_ref: (kc_ref[0], i, 0)),
                      pl.BlockSpec((N_CHIPS, tr, w), lambda i, kc_ref: (0, i, 0))],
            out_specs=pl.BlockSpec((tr, w), lambda i, kc_ref: (kc_ref[1] * nb + i, 0))),
        out_shape=jax.ShapeDtypeStruct((2 * hf, w), F32), compiler_params=_cparams(("parallel",)))(kc, t, rb)


def reduced_to_sibling(gs):
    n = len(gs)

    def body(*refs):
        o_refs, ssem, rsem = refs[n:2 * n], refs[2 * n], refs[2 * n + 1]
        x, y, c = _me()
        cps = []
        for a in range(n):
            hf = gs[a].shape[0] // 2
            cps.append(_rcopy(_half(o_refs[a], c, hf), _half(o_refs[a], c, hf), ssem.at[a], rsem.at[a], (x, y, 1 - c)))
        for cp in cps:
            cp.start()
        for a in range(n):
            hf = gs[a].shape[0] // 2
            _rcopy(_half(o_refs[a], c, hf), _half(o_refs[a], 1 - c, hf), ssem.at[a], rsem.at[a],
                   (x, y, 1 - c)).wait_recv()
        for cp in cps:
            cp.wait_send()

    return pl.pallas_call(
        body, name="reduced_to_sibling", in_specs=[ANY] * n, out_specs=[ANY] * n,
        input_output_aliases={a: a for a in range(n)},
        out_shape=[jax.ShapeDtypeStruct(g.shape, g.dtype) for g in gs],
        scratch_shapes=[pltpu.SemaphoreType.DMA((n,)), pltpu.SemaphoreType.DMA((n,))])(*gs)


def small_allgather(g, row0, nrs):
    w = g.shape[1]

    def body(g_ref, o_ref, ssem, rsem):
        x, y, c = _me()
        k = 2 * x + y
        chips = _other_chips(x, y)
        src = g_ref.at[pl.ds(row0, nrs), :]
        sends = [_rcopy(src, o_ref.at[k], ssem.at[j], rsem.at[j], (cx, cy, c)) for j, (cx, cy) in enumerate(chips)]
        for cp in sends:
            cp.start()
        for j, (cx, cy) in enumerate(chips):
            _rcopy(src, o_ref.at[2 * cx + cy], ssem.at[j], rsem.at[j], (cx, cy, c)).wait_recv()
        for cp in sends:
            cp.wait_send()

    out = pl.pallas_call(
        body, name="small_allgather", in_specs=[ANY], out_specs=ANY,
        out_shape=jax.ShapeDtypeStruct((N_CHIPS, nrs, w), g.dtype),
        scratch_shapes=[pltpu.SemaphoreType.DMA((3,)), pltpu.SemaphoreType.DMA((3,))])(g)
    return lax.dynamic_update_slice(out, g[row0:row0 + nrs][None], (_chip(), 0, 0))


def _adamw_step(w_ref, g_ref, m_ref, v_ref, d_ref, nm_ref, nv_ref):
    bc1 = 1.0 - ADAM_B1 ** ADAM_STEP
    bc2 = 1.0 - ADAM_B2 ** ADAM_STEP
    gv = g_ref[...]
    nm = ADAM_B1 * m_ref[...] + (1.0 - ADAM_B1) * gv
    nv = ADAM_B2 * v_ref[...] + (1.0 - ADAM_B2) * (gv * gv)
    nm_ref[...] = nm
    nv_ref[...] = nv
    d_ref[...] = -ADAM_LR * ((nm / bc1) / (jnp.sqrt(nv / bc2) + ADAM_EPS) + ADAM_WD * w_ref[...])


def adamw_packed(w, g_buf, r0, m, v, name):
    r, c = w.shape
    tr = _tile_rows(r, r0)

    def body(w_ref, g_ref, m_ref, v_ref, go_ref, d_ref, nm_ref, nv_ref):
        go_ref[...] = g_ref[...]
        _adamw_step(w_ref, g_ref, m_ref, v_ref, d_ref, nm_ref, nv_ref)

    own = pl.BlockSpec((tr, CHUNK_W), lambda i, j: (i, j))
    packed = pl.BlockSpec((tr, CHUNK_W), lambda i, j: ((r0 + j * r) // tr + i, 0))
    st = jax.ShapeDtypeStruct((r, c), F32)
    return pl.pallas_call(body, name=name, grid=(r // tr, c // CHUNK_W), in_specs=[own, packed, own, own],
                          out_specs=[own] * 4, out_shape=[st] * 4,
                          compiler_params=_cparams(("parallel", "parallel")))(w, g_buf, m, v)


def adamw(w, g, m, v, name):
    r, wd = w.shape
    tr = _pick_rows(r, cap=max(16, ADAMW_BLOCK_BYTES // (4 * wd)))
    body = functools.partial(_adamw_step)

    spec = pl.BlockSpec((tr, wd), lambda i: (i, 0))
    st = jax.ShapeDtypeStruct((r, wd), F32)
    return pl.pallas_call(body, name=name, grid=(r // tr,), in_specs=[spec] * 4, out_specs=[spec] * 3,
                          out_shape=[st, st, st], compiler_params=_cparams(("parallel",)))(w, g, m, v)


LAYER_KINDS = ("gmlp", "s5", "mla", "gmlp")
PARAMS = {
    "gmlp": ("norm_g", "w_in", "ln_g", "ln_b", "w_s", "b_s", "w_out"),
    "s5": ("norm_g", "w_in", "a_re", "a_im", "log_step", "b_re", "b_im", "c_re", "c_im", "d_skip", "w_glu", "b_glu", "w_out"),
    "mla": ("norm_g", "w_in", "q_norm_g", "w_uq", "kv_norm_g", "w_ukv", "w_out"),
}
COL_SHARDED = ("w_in", "w_uq", "w_ukv")
ROW_SHARDED = ("w_out", "w_glu")
WEIGHT_NAMES = [("l%d_" % i) + n for i, kind in enumerate(LAYER_KINDS) for n in PARAMS[kind]] + ["final_norm_g"]


def _is_big(name):
    return name.split("_", 1)[1] in COL_SHARDED + ROW_SHARDED


BIG = [n for n in WEIGHT_NAMES if _is_big(n)]
SMALL = [n for n in WEIGHT_NAMES if not _is_big(n)]


def _pack_rows(blocks):
    return jnp.concatenate([b.reshape(-1, PACK_W) for b in blocks], axis=0)


def _shard_major(wn, full, width):
    r, c = full.shape
    if wn in COL_SHARDED:
        t = full.reshape(r, N_CHIPS, c // N_CHIPS).transpose(1, 0, 2)
    else:
        t = full.reshape(N_CHIPS, r // N_CHIPS, c)
    return t.reshape(N_CHIPS, -1, width)


def _from_shard_major(name, t, block_shape):
    r, c = block_shape
    if name.split("_", 1)[1] in COL_SHARDED:
        return t.reshape(N_CHIPS, r, c).transpose(1, 0, 2).reshape(r, N_CHIPS * c)
    return t.reshape(N_CHIPS * r, c)


class BigGradSink:
    ORDER = ("w_out", "w_glu", "w_ukv", "w_uq", "w_in")
    ROW_MAJOR = {2: ("w_uq", "w_in")}

    def __init__(self, layer, block_shapes):
        self.layer = layer
        self.regions = {}
        r0 = 0
        for wn in self.ORDER:
            if wn in block_shapes:
                shape = block_shapes[wn]
                self.regions[wn] = (r0, shape, wn not in self.ROW_MAJOR.get(layer, ()))
                r0 += shape[0] * shape[1] // CHUNK_W
        self.buf = lax.empty((N_CHIPS, r0, CHUNK_W), F32)
        self.flight = None

    def mm(self, wn, a, b, name):
        r0, _, direct = self.regions[wn]
        assert direct
        self.buf = matmul_tn_packed(a, b, self.buf, r0, wn in COL_SHARDED, name)

    def put(self, wn, full):
        r0, _, direct = self.regions[wn]
        assert not direct
        piece = _shard_major(wn, full, CHUNK_W)
        self.buf = lax.dynamic_update_slice(self.buf, piece, (0, r0, 0))

    def send(self):
        i = self.layer
        sib, = grads_to_sibling([self.buf], "grads_to_sibling_l%d" % i)
        t = pair_sum(self.buf, sib, BF16, "pair_sum_l%d" % i)
        self.flight = reduce_start(t, sib, "reduce_l%d_start" % i)
        return self.flight[4][0, 0]


def _small_pack(arrs, total_padded):
    flat = jnp.concatenate([a.reshape(-1) for a in arrs])
    return jnp.pad(flat, (0, total_padded - flat.shape[0]))


def kernel(x, positions, l0_norm_g, l0_w_in, l0_ln_g, l0_ln_b, l0_w_s, l0_b_s, l0_w_out, l1_norm_g, l1_w_in, l1_a_re, l1_a_im, l1_log_step, l1_b_re, l1_b_im, l1_c_re, l1_c_im, l1_d_skip, l1_w_glu, l1_b_glu, l1_w_out, l2_norm_g, l2_w_in, l2_q_norm_g, l2_w_uq, l2_kv_norm_g, l2_w_ukv, l2_w_out, l3_norm_g, l3_w_in, l3_ln_g, l3_ln_b, l3_w_s, l3_b_s, l3_w_out, final_norm_g, loss_target, m_l0_norm_g, m_l0_w_in, m_l0_ln_g, m_l0_ln_b, m_l0_w_s, m_l0_b_s, m_l0_w_out, m_l1_norm_g, m_l1_w_in, m_l1_a_re, m_l1_a_im, m_l1_log_step, m_l1_b_re, m_l1_b_im, m_l1_c_re, m_l1_c_im, m_l1_d_skip, m_l1_w_glu, m_l1_b_glu, m_l1_w_out, m_l2_norm_g, m_l2_w_in, m_l2_q_norm_g, m_l2_w_uq, m_l2_kv_norm_g, m_l2_w_ukv, m_l2_w_out, m_l3_norm_g, m_l3_w_in, m_l3_ln_g, m_l3_ln_b, m_l3_w_s, m_l3_b_s, m_l3_w_out, m_final_norm_g, v_l0_norm_g, v_l0_w_in, v_l0_ln_g, v_l0_ln_b, v_l0_w_s, v_l0_b_s, v_l0_w_out, v_l1_norm_g, v_l1_w_in, v_l1_a_re, v_l1_a_im, v_l1_log_step, v_l1_b_re, v_l1_b_im, v_l1_c_re, v_l1_c_im, v_l1_d_skip, v_l1_w_glu, v_l1_b_glu, v_l1_w_out, v_l2_norm_g, v_l2_w_in, v_l2_q_norm_g, v_l2_w_uq, v_l2_kv_norm_g, v_l2_w_ukv, v_l2_w_out, v_l3_norm_g, v_l3_w_in, v_l3_ln_g, v_l3_ln_b, v_l3_w_s, v_l3_b_s, v_l3_w_out, v_final_norm_g):
    args = locals()
    w = {n: args[n] for n in WEIGHT_NAMES}
    mom_m = {n: args["m_" + n] for n in WEIGHT_NAMES}
    mom_v = {n: args["v_" + n] for n in WEIGHT_NAMES}
    h0 = x[0]
    target = loss_target[0]
    pos = positions.reshape(-1, 1)

    big_rows = [w[n].size // PACK_W for n in BIG]
    nrb = sum(big_rows)
    nrb_pad = -(-nrb // PACK_ROW_ALIGN) * PACK_ROW_ALIGN
    full = {}

    def pack_unit(layers):
        names = [n for n in BIG if int(n[1]) in layers]
        rows = [w[n].size // PACK_W for n in names]
        pad = -sum(rows) % PACK_ROW_ALIGN
        return names, rows, _pack_rows([w[n].astype(BF16) for n in names] + [jnp.zeros((pad, PACK_W), BF16)])

    def unpack_unit(names, rows, gathered):
        r0 = 0
        for n, nr in zip(names, rows):
            full[n] = _from_shard_major(n, gathered[:, r0:r0 + nr, :], w[n].shape)
            r0 += nr

    unit0, unit1, unit2 = pack_unit((0,)), pack_unit((1,)), pack_unit((2, 3))
    unpack_unit(unit0[0], unit0[1], weights_allgather(unit0[2]))
    flight = gather_start(unit1[2], unit0[2], "gather_l1_start")
    wp = dict(w)
    wp["l0_norm_g"] = w["l0_norm_g"] + flight[4][0, 0]

    def layer_params(i):
        pre = "l%d_" % i
        p = {k[len(pre):]: v for k, v in wp.items() if k.startswith(pre)}
        wf = {k[len(pre):]: v for k, v in full.items() if k.startswith(pre)}
        return p, wf

    cos, sins = rope_tables(pos)
    h = h0
    saved = []
    for i, kind in enumerate(LAYER_KINDS):
        if i == 1:
            land = gather_wait(*flight[:4], h, "gather_l1_wait")
            got = gather_handover(land, unit1[2], "gather_l1_handover")
            unpack_unit(unit1[0], unit1[1], got)
            flight = gather_start(unit2[2], got, "gather_l23_start")
            wp["l1_norm_g"] = w["l1_norm_g"] + flight[4][0, 0]
        if i == 2:
            land = gather_wait(*flight[:4], h, "gather_l23_wait")
            unpack_unit(unit2[0], unit2[1], gather_handover(land, unit2[2], "gather_l23_handover"))
        p, wf = layer_params(i)
        tag = "l%d" % i
        if kind == "gmlp":
            h, s = gmlp_layer_fwd(h, p, wf, tag)
        elif kind == "s5":
            h, s = s5_layer_fwd(h, p, wf, tag)
        else:
            h, s = mla_layer_fwd(h, p, wf, cos, sins, tag)
        saved.append(s)
    loss_part, dh, g_final = loss_head(h, final_norm_g, target)

    grads = {"final_norm_g": g_final.reshape(-1)}
    sinks = {}

    for i in reversed(range(len(LAYER_KINDS))):
        kind = LAYER_KINDS[i]
        p, wf = layer_params(i)
        tag = "l%d" % i
        sink = sinks[i] = BigGradSink(i, {n[3:]: w[n].shape for n in BIG if int(n[1]) == i})
        if kind == "gmlp":
            dh, g = gmlp_layer_bwd(dh, saved[i], p, wf, tag, sink)
        elif kind == "s5":
            dh, g = s5_layer_bwd(dh, saved[i], p, wf, tag, sink)
        else:
            dh, g = mla_layer_bwd(dh, saved[i], p, wf, cos, sins, tag, sink)
        for k, val in g.items():
            grads["l%d_%s" % (i, k)] = val
    grad_x = dh[None]

    n_small = sum(w[n].size for n in SMALL)
    piece = N_CHIPS * 2 * 16 * PACK_W
    n_small_pad = -(-(n_small + 1) // piece) * piece
    nrs = n_small_pad // N_CHIPS // PACK_W
    p_small = _small_pack([grads[n] for n in SMALL] + [loss_part], n_small_pad).reshape(N_CHIPS, nrs, PACK_W)
    sib_small, = grads_to_sibling([p_small], "grads_to_sibling_small")
    t_small = pair_sum(p_small, sib_small, F32, "pair_sum_small")
    rb_small, = grads_across_chips([t_small])
    halves = [chip_sum(t_small, rb_small, "chip_sum_small")]

    after = halves[0]
    for i in reversed(range(len(LAYER_KINDS))):
        t_i, rb_i = reduce_wait(*sinks[i].flight[:4], after, "reduce_l%d_wait" % i)
        halves.append(chip_sum(t_i, rb_i, "chip_sum_l%d" % i))
        after = halves[-1]
    reduced = reduced_to_sibling(halves)
    small_all = small_allgather(reduced[0], 0, nrs)

    g_out, d_out, nm_out, nv_out = {}, {}, {}, {}
    for i, g_i in zip(reversed(range(len(LAYER_KINDS))), reduced[1:]):
        for wn, (r0, shape, direct) in sinks[i].regions.items():
            n = "l%d_%s" % (i, wn)
            if direct:
                g_out[n], d_out[n], nm_out[n], nv_out[n] = adamw_packed(w[n], g_i, r0, mom_m[n], mom_v[n], "adamw_" + n)
            else:
                g_out[n] = g_i[r0:r0 + shape[0] * shape[1] // CHUNK_W].reshape(shape)
                d_out[n], nm_out[n], nv_out[n] = adamw(w[n], g_out[n], mom_m[n], mom_v[n], "adamw_" + n)
    g_small = small_all.reshape(-1, PACK_W)
    sp = lambda d: _small_pack([d[n] for n in SMALL], n_small_pad).reshape(-1, PACK_W)
    d_small, nm_small, nv_small = adamw(sp(w), g_small, sp(mom_m), sp(mom_v), "adamw_small")
    for buf, out in ((g_small, g_out), (d_small, d_out), (nm_small, nm_out), (nv_small, nv_out)):
        flat = buf.reshape(-1)
        o = 0
        for n in SMALL:
            out[n] = flat[o:o + w[n].size].reshape(w[n].shape)
            o += w[n].size
    loss = g_small.reshape(-1)[n_small]
    return (loss, grad_x, *[g_out[n] for n in WEIGHT_NAMES], *[d_out[n] for n in WEIGHT_NAMES],
            *[nm_out[n] for n in WEIGHT_NAMES], *[nv_out[n] for n in WEIGHT_NAMES])
```

```python
import functools
import math

import jax
import jax.numpy as jnp
import numpy as np
from jax import lax
from jax.experimental import pallas as pl
from jax.experimental.pallas import tpu as pltpu

F32 = jnp.float32
BF16 = jnp.bfloat16
MESH = pl.DeviceIdType.MESH
VMEM_LIMIT_BYTES = 56 * 1024 * 1024
LANES = 128
PACK_W = 1024
CHUNK_W = 256
PACK_ROW_ALIGN = 256
ROW_TILE = 256
ADAMW_BLOCK_BYTES = 1024 * 1024
MM_BLOCK_BYTES = 6 * 1024 * 1024

NORM_EPS = 1e-6
N_CHIPS = 4
GMLP_CHUNK = 128
GMLP_GROUPS = 8
S5_GROUPS = 128
S5_GROUP = 16
S5_STATE = 64
S5_SB = 16
S5_SEG = 8
MLA_HEADS = 16
MLA_NOPE = 128
MLA_ROPE = 64
MLA_Q_RANK = 384
MLA_KV_RANK = 128
MLA_SCALE = (MLA_NOPE + MLA_ROPE) ** -0.5
ROPE_THETA = 10000.0
NEG_INF = -1e30
ADAM_LR, ADAM_B1, ADAM_B2, ADAM_EPS, ADAM_WD, ADAM_STEP = 0.001, 0.9, 0.999, 1e-08, 0.01, 10

DN_NN = (((1,), (0,)), ((), ()))
DN_NT = (((1,), (1,)), ((), ()))
DN_TN = (((0,), (0,)), ((), ()))


def _cparams(sem):
    return pltpu.CompilerParams(dimension_semantics=sem, vmem_limit_bytes=VMEM_LIMIT_BYTES)


def _pick(n, cands=(512, 384, 256, 128)):
    for c in cands:
        if n % c == 0:
            return c
    return n


def _pick_rows(r, cap=512, mult=16):
    return max(t for t in range(mult, cap + 1, mult) if r % t == 0)


def _dot(a, b, dn):
    return lax.dot_general(a.astype(BF16), b.astype(BF16), dn, preferred_element_type=F32)


def _sigmoid(x):
    return 0.5 + 0.5 * jnp.tanh(0.5 * x)


def _gelu(x):
    c = math.sqrt(2.0 / math.pi)
    t = jnp.tanh(c * (x + 0.044715 * x * x * x))
    return 0.5 * x * (1.0 + t)


def _gelu_grad(x):
    c = math.sqrt(2.0 / math.pi)
    t = jnp.tanh(c * (x + 0.044715 * x * x * x))
    return 0.5 * (1.0 + t) + 0.5 * x * (1.0 - t * t) * c * (1.0 + 3.0 * 0.044715 * x * x)


def _gelu_both(x):
    c = math.sqrt(2.0 / math.pi)
    t = jnp.tanh(c * (x + 0.044715 * x * x * x))
    return 0.5 * x * (1.0 + t), 0.5 * (1.0 + t) + 0.5 * x * (1.0 - t * t) * c * (1.0 + 3.0 * 0.044715 * x * x)


def _silu_both(z):
    s = _sigmoid(z)
    return z * s, s * (1.0 + z * (1.0 - s))


def _silu(z):
    return z * _sigmoid(z)


def matmul(a, b, mode, name, out_dtype=F32, add=None):
    if mode == "nn":
        (m, k), n = a.shape, b.shape[1]
    elif mode == "nt":
        (m, k), n = a.shape, b.shape[0]
    else:
        (k, m), n = a.shape, b.shape[1]
    tm = _pick(m, [t for t in (1024, 512, 384, 256, 128) if t * k * a.dtype.itemsize <= MM_BLOCK_BYTES])
    tn = _pick(n, [t for t in (512, 384, 256, 128) if t * k * b.dtype.itemsize <= MM_BLOCK_BYTES])
    dn = {"nn": DN_NN, "nt": DN_NT, "tn": DN_TN}[mode]

    def body(*refs):
        if add is None:
            a_ref, b_ref, o_ref = refs
        else:
            a_ref, b_ref, add_ref, o_ref = refs
        r = _dot(a_ref[...], b_ref[...], dn)
        if add is not None:
            r = r + add_ref[...].astype(F32)
        o_ref[...] = r.astype(out_dtype)

    a_spec = pl.BlockSpec((k, tm), lambda i, j: (0, i)) if mode == "tn" else pl.BlockSpec((tm, k), lambda i, j: (i, 0))
    b_spec = pl.BlockSpec((tn, k), lambda i, j: (j, 0)) if mode == "nt" else pl.BlockSpec((k, tn), lambda i, j: (0, j))
    o_spec = pl.BlockSpec((tm, tn), lambda i, j: (i, j))
    in_specs = [a_spec, b_spec] + ([o_spec] if add is not None else [])
    args = (a, b) + ((add,) if add is not None else ())
    return pl.pallas_call(
        body, name=name, grid=(m // tm, n // tn), in_specs=in_specs, out_specs=o_spec,
        out_shape=jax.ShapeDtypeStruct((m, n), out_dtype),
        compiler_params=_cparams(("parallel", "arbitrary")))(*args)


def _tile_rows(r, r0, cands=(512, 384, 256, 128)):
    return next(t for t in cands if r % t == 0 and r0 % t == 0)


def matmul_tn_packed(a, b, buf, r0, col_sharded, name):
    k, m = a.shape
    n = b.shape[1]
    if col_sharded:
        chunks = n // N_CHIPS // CHUNK_W
        tm = _tile_rows(m, r0, (1024, 512, 384, 256, 128))
        o_map = lambda i, j: (j // chunks, (r0 + (j % chunks) * m) // tm + i, 0)
    else:
        rs = m // N_CHIPS
        tm = _tile_rows(rs, r0)
        per = rs // tm
        o_map = lambda i, j: (i // per, (r0 + j * rs) // tm + i % per, 0)

    def body(a_ref, b_ref, buf_ref, o_ref):
        o_ref[0] = _dot(a_ref[...], b_ref[...], DN_TN)

    return pl.pallas_call(
        body, name=name, grid=(m // tm, n // CHUNK_W),
        in_specs=[pl.BlockSpec((k, tm), lambda i, j: (0, i)), pl.BlockSpec((k, CHUNK_W), lambda i, j: (0, j)),
                  pl.BlockSpec(memory_space=pl.ANY)],
        out_specs=pl.BlockSpec((1, tm, CHUNK_W), o_map), out_shape=jax.ShapeDtypeStruct(buf.shape, buf.dtype),
        input_output_aliases={2: 0}, compiler_params=_cparams(("parallel", "arbitrary")))(a, b, buf)


def _rows(tl, w, col=0):
    return pl.BlockSpec((tl, w), lambda i: (i, col))


def _full(shape):
    nd = len(shape)
    return pl.BlockSpec(tuple(shape), lambda i: (0,) * nd)


def _rowcall(body, name, n_steps, in_specs, out_specs, out_shape, scratch=()):
    return pl.pallas_call(
        body, name=name, grid=(n_steps,), in_specs=in_specs, out_specs=out_specs, out_shape=out_shape,
        scratch_shapes=list(scratch), compiler_params=_cparams(("arbitrary",)))


def _acc(ref, val, i):
    @pl.when(i == 0)
    def _():
        ref[...] = val

    @pl.when(i != 0)
    def _():
        ref[...] += val


def rms_fwd(h, g, name):
    l, d = h.shape
    tl = ROW_TILE

    def body(h_ref, g_ref, o_ref):
        x = h_ref[...]
        r = lax.rsqrt(jnp.mean(x * x, axis=-1, keepdims=True) + NORM_EPS)
        o_ref[...] = (x * r * g_ref[...]).astype(BF16)

    return _rowcall(body, name, l // tl, [_rows(tl, d), _full((1, d))], _rows(tl, d),
                    jax.ShapeDtypeStruct((l, d), BF16))(h, g.reshape(1, d))


def rms_bwd(h, g, dhn, dh_in, name):
    l, d = h.shape
    tl = ROW_TILE

    def body(h_ref, g_ref, dhn_ref, dhi_ref, dh_ref, dg_ref):
        i = pl.program_id(0)
        x = h_ref[...]
        r = lax.rsqrt(jnp.mean(x * x, axis=-1, keepdims=True) + NORM_EPS)
        xhat = x * r
        dy = dhn_ref[...]
        dxh = dy * g_ref[...]
        dx = r * (dxh - xhat * jnp.mean(dxh * xhat, axis=-1, keepdims=True))
        dh_ref[...] = dhi_ref[...] + dx
        _acc(dg_ref, jnp.sum(dy * xhat, axis=0, keepdims=True), i)

    return _rowcall(body, name, l // tl, [_rows(tl, d), _full((1, d)), _rows(tl, d), _rows(tl, d)],
                    [_rows(tl, d), _full((1, d))],
                    [jax.ShapeDtypeStruct((l, d), F32), jax.ShapeDtypeStruct((1, d), F32)])(h, g.reshape(1, d), dhn, dh_in)


def loss_head(h, g, target):
    l, d = h.shape
    tl = ROW_TILE

    def body(h_ref, g_ref, t_ref, loss_ref, dh_ref, dg_ref):
        i = pl.program_id(0)
        x = h_ref[...]
        gg = g_ref[...]
        r = lax.rsqrt(jnp.mean(x * x, axis=-1, keepdims=True) + NORM_EPS)
        xhat = x * r
        err = xhat * gg - t_ref[...]
        part = 0.5 * jnp.sum(jnp.mean(err * err, axis=-1, keepdims=True), axis=0, keepdims=True)
        _acc(loss_ref, part, i)
        dy = err * (1.0 / d)
        dxh = dy * gg
        dh_ref[...] = r * (dxh - xhat * jnp.mean(dxh * xhat, axis=-1, keepdims=True))
        _acc(dg_ref, jnp.sum(dy * xhat, axis=0, keepdims=True), i)

    return _rowcall(body, "loss_head", l // tl, [_rows(tl, d), _full((1, d)), _rows(tl, d)],
                    [_full((1, 1)), _rows(tl, d), _full((1, d))],
                    [jax.ShapeDtypeStruct((1, 1), F32), jax.ShapeDtypeStruct((l, d), F32),
                     jax.ShapeDtypeStruct((1, d), F32)])(h, g.reshape(1, d), target)


def _gmlp_common(a_ref, lng_ref, lnb_ref):
    di = lng_ref.shape[1]
    u_pre = a_ref[:, 0:di]
    v_pre = a_ref[:, di:2 * di]
    z = a_ref[:, 2 * di:3 * di]
    vg = _gelu(v_pre)
    mu = jnp.mean(vg, axis=-1, keepdims=True)
    xc = vg - mu
    rstd = lax.rsqrt(jnp.mean(xc * xc, axis=-1, keepdims=True) + NORM_EPS)
    vhat = xc * rstd
    vn = vhat * lng_ref[...] + lnb_ref[...]
    return u_pre, v_pre, z, vhat, rstd, vn


def _tril(w):
    r = lax.broadcasted_iota(jnp.int32, w.shape, 0)
    c = lax.broadcasted_iota(jnp.int32, w.shape, 1)
    return jnp.where(c <= r, w, 0.0)


def gmlp_gate_fwd(a, ln_g, ln_b, w_s, b_s, name):
    l, w3 = a.shape
    di = w3 // 3
    dg = di // GMLP_GROUPS
    tl = GMLP_CHUNK

    def body(a_ref, lng_ref, lnb_ref, ws_ref, bs_ref, m_ref):
        u_pre, _, z, _, _, vn = _gmlp_common(a_ref, lng_ref, lnb_ref)
        gate = _gelu(u_pre) * _silu(z)
        for g in range(GMLP_GROUPS):
            sl = slice(g * dg, (g + 1) * dg)
            s = _dot(_tril(ws_ref[g]), vn[:, sl], DN_NN) + bs_ref[g]
            m_ref[:, sl] = (gate[:, sl] * s).astype(BF16)

    return _rowcall(body, name, l // tl,
                    [_rows(tl, w3), _full((1, di)), _full((1, di)), _full(w_s.shape), _full((GMLP_GROUPS, tl, 1))],
                    _rows(tl, di), jax.ShapeDtypeStruct((l, di), BF16))(
        a, ln_g.reshape(1, di), ln_b.reshape(1, di), w_s, b_s.reshape(GMLP_GROUPS, tl, 1))


def gmlp_gate_bwd(a, dm, ln_g, ln_b, w_s, b_s, name):
    l, w3 = a.shape
    di = w3 // 3
    dg = di // GMLP_GROUPS
    tl = GMLP_CHUNK

    def body(a_ref, dm_ref, lng_ref, lnb_ref, ws_ref, bs_ref, da_ref, dlg_ref, dlb_ref, dws_ref, dbs_ref,
             dvn_ref, vh_ref, gv_ref):
        i = pl.program_id(0)
        vg, gv = _gelu_both(a_ref[:, di:2 * di])
        gv_ref[...] = gv
        xc = vg - jnp.mean(vg, axis=-1, keepdims=True)
        rstd = lax.rsqrt(jnp.mean(xc * xc, axis=-1, keepdims=True) + NORM_EPS)
        vh_ref[...] = xc * rstd
        for g in range(GMLP_GROUPS):
            sl = slice(g * dg, (g + 1) * dg)
            wt = _tril(ws_ref[g])
            vn_g = vh_ref[:, sl] * lng_ref[:, sl] + lnb_ref[:, sl]
            s = _dot(wt, vn_g, DN_NN) + bs_ref[g]
            dmg = dm_ref[:, sl]
            u, gu = _gelu_both(a_ref[:, sl])
            sz, gz = _silu_both(a_ref[:, 2 * di + g * dg:2 * di + (g + 1) * dg])
            ds = dmg * u * sz
            da_ref[:, sl] = (dmg * s * sz * gu).astype(BF16)
            da_ref[:, 2 * di + g * dg:2 * di + (g + 1) * dg] = (dmg * u * s * gz).astype(BF16)
            dvn_ref[:, sl] = _dot(wt, ds, DN_TN)
            dw = _tril(_dot(ds, vn_g, DN_NT))
            db = jnp.sum(ds, axis=1, keepdims=True)

            @pl.when(i == 0)
            def _():
                dws_ref[g] = dw
                dbs_ref[g] = db

            @pl.when(i != 0)
            def _():
                dws_ref[g] += dw
                dbs_ref[g] += db

        dvn = dvn_ref[...]
        vhat = vh_ref[...]
        dxh = dvn * lng_ref[...]
        dvg = rstd * (dxh - jnp.mean(dxh, axis=-1, keepdims=True) - vhat * jnp.mean(dxh * vhat, axis=-1, keepdims=True))
        da_ref[:, di:2 * di] = (dvg * gv_ref[...]).astype(BF16)
        _acc(dlg_ref, jnp.sum(dvn * vhat, axis=0, keepdims=True), i)
        _acc(dlb_ref, jnp.sum(dvn, axis=0, keepdims=True), i)

    outs = _rowcall(
        body, name, l // tl,
        [_rows(tl, w3), _rows(tl, di), _full((1, di)), _full((1, di)), _full(w_s.shape), _full((GMLP_GROUPS, tl, 1))],
        [_rows(tl, w3), _full((1, di)), _full((1, di)), _full(w_s.shape), _full((GMLP_GROUPS, tl, 1))],
        [jax.ShapeDtypeStruct((l, w3), BF16), jax.ShapeDtypeStruct((1, di), F32), jax.ShapeDtypeStruct((1, di), F32),
         jax.ShapeDtypeStruct(w_s.shape, F32), jax.ShapeDtypeStruct((GMLP_GROUPS, tl, 1), F32)],
        scratch=[pltpu.VMEM((tl, di), F32)] * 3)(
        a, dm, ln_g.reshape(1, di), ln_b.reshape(1, di), w_s, b_s.reshape(GMLP_GROUPS, tl, 1))
    return outs


def gmlp_layer_fwd(h, p, wf, tag):
    hn = rms_fwd(h, p["norm_g"], tag + "_rms")
    a = matmul(hn, wf["w_in"], "nn", tag + "_mm_in")
    m = gmlp_gate_fwd(a, p["ln_g"], p["ln_b"], p["w_s"], p["b_s"], tag + "_gate")
    h_out = matmul(m, wf["w_out"], "nn", tag + "_mm_out", add=h)
    return h_out, (h, hn, a, m)


def gmlp_layer_bwd(dh_out, saved, p, wf, tag, sink):
    h, hn, a, m = saved
    dm = matmul(dh_out, wf["w_out"], "nt", tag + "_mm_dm")
    sink.mm("w_out", m, dh_out, tag + "_mm_gwout")
    da, dlg, dlb, dws, dbs = gmlp_gate_bwd(a, dm, p["ln_g"], p["ln_b"], p["w_s"], p["b_s"], tag + "_gate_bwd")
    dhn = matmul(da, wf["w_in"], "nt", tag + "_mm_dhn")
    sink.mm("w_in", hn, da, tag + "_mm_gwin")
    zero = sink.send()
    dh, dng = rms_bwd(h, p["norm_g"] + zero, dhn, dh_out, tag + "_rms_bwd")
    grads = {"norm_g": dng.reshape(-1), "ln_g": dlg.reshape(-1), "ln_b": dlb.reshape(-1),
             "w_s": dws, "b_s": dbs.reshape(GMLP_GROUPS, GMLP_CHUNK)}
    return dh, grads


def _cmul(ar, ai, br, bi):
    return ar * br - ai * bi, ar * bi + ai * br


S5_PG = 16


def _gblock(tail):
    return pl.BlockSpec((S5_PG,) + tuple(tail), lambda i: (i, 0, 0))


def s5_params_fwd(a_re, a_im, log_step, b_re, b_im):
    g, p, hh = b_re.shape

    def body(ar_ref, ai_ref, ls_ref, br_ref, bi_ref, lr_ref, li_ref, bbr_ref, bbi_ref):
        ar, ai = ar_ref[...], ai_ref[...]
        step = jnp.exp(ls_ref[...])
        mag = jnp.exp(ar * step)
        lr, li = mag * jnp.cos(ai * step), mag * jnp.sin(ai * step)
        den = 1.0 / (ar * ar + ai * ai)
        fr, fi = _cmul(lr - 1.0, li, ar * den, -ai * den)
        lr_ref[...] = lr
        li_ref[...] = li
        bbr, bbi = _cmul(fr, fi, br_ref[...], bi_ref[...])
        bbr_ref[...] = bbr
        bbi_ref[...] = bbi

    s1 = jax.ShapeDtypeStruct((g, p, 1), F32)
    s3 = jax.ShapeDtypeStruct((g, p, hh), F32)
    b1, b0, b3 = _gblock((p, 1)), _gblock((1, 1)), _gblock((p, hh))
    return pl.pallas_call(body, name="s5_params_fwd", grid=(g // S5_PG,), in_specs=[b1, b1, b0, b3, b3],
                          out_specs=[b1, b1, b3, b3], out_shape=[s1, s1, s3, s3],
                          compiler_params=_cparams(("parallel",)))(
        a_re.reshape(g, p, 1), a_im.reshape(g, p, 1), log_step.reshape(g, 1, 1), b_re, b_im)


def s5_params_bwd(a_re, a_im, log_step, b_re, b_im, dl_re, dl_im, dbb_re, dbb_im):
    g, p, hh = b_re.shape

    def body(ar_ref, ai_ref, ls_ref, br_ref, bi_ref, dlr_ref, dli_ref, dbr_ref, dbi_ref,
             gar_ref, gai_ref, gls_ref, gbr_ref, gbi_ref):
        ar, ai = ar_ref[...], ai_ref[...]
        step = jnp.exp(ls_ref[...])
        mag = jnp.exp(ar * step)
        lr, li = mag * jnp.cos(ai * step), mag * jnp.sin(ai * step)
        den = 1.0 / (ar * ar + ai * ai)
        ir, ii = ar * den, -ai * den
        fr, fi = _cmul(lr - 1.0, li, ir, ii)
        br, bi = br_ref[...], bi_ref[...]
        dbr, dbi = dbr_ref[...], dbi_ref[...]
        gbr, gbi = _cmul(fr, -fi, dbr, dbi)
        gbr_ref[...] = gbr
        gbi_ref[...] = gbi
        pr, pi = _cmul(br, -bi, dbr, dbi)
        gfr = jnp.sum(pr, axis=-1, keepdims=True)
        gfi = jnp.sum(pi, axis=-1, keepdims=True)
        t_r, t_i = _cmul(ir, -ii, gfr, gfi)
        glr, gli = dlr_ref[...] + t_r, dli_ref[...] + t_i
        c1r, c1i = _cmul(step * lr, -step * li, glr, gli)
        qr, qi = _cmul(fr, fi, ir, ii)
        c2r, c2i = _cmul(-qr, qi, gfr, gfi)
        gar_ref[...] = c1r + c2r
        gai_ref[...] = c1i + c2i
        wr, wi = _cmul(ar, ai, lr, li)
        sr, _ = _cmul(wr, -wi, glr, gli)
        gls_ref[...] = jnp.sum(sr, axis=1, keepdims=True) * step

    s1 = jax.ShapeDtypeStruct((g, p, 1), F32)
    s3 = jax.ShapeDtypeStruct((g, p, hh), F32)
    b1, b0, b3 = _gblock((p, 1)), _gblock((1, 1)), _gblock((p, hh))
    return pl.pallas_call(body, name="s5_params_bwd", grid=(g // S5_PG,),
                          in_specs=[b1, b1, b0, b3, b3, b1, b1, b3, b3], out_specs=[b1, b1, b0, b3, b3],
                          out_shape=[s1, s1, jax.ShapeDtypeStruct((g, 1, 1), F32), s3, s3],
                          compiler_params=_cparams(("parallel",)))(
        a_re.reshape(g, p, 1), a_im.reshape(g, p, 1), log_step.reshape(g, 1, 1), b_re, b_im,
        dl_re, dl_im, dbb_re, dbb_im)


def _blockdiag(t):
    sb, n, r, c = t.shape
    eye = jnp.eye(n, dtype=bool)[None, :, None, :, None]
    full = jnp.where(eye, t[:, :, :, None, :], jnp.zeros((), t.dtype))
    return full.reshape(sb, n * r, n * c)


def _blockdiag_extract(m, r, c):
    sb = m.shape[0]
    n = m.shape[1] // r
    m5 = m.reshape(sb, n, r, n, c)
    return jnp.stack([m5[:, i, :, i, :] for i in range(n)], axis=1)


S5_TB = 64
S5_UNROLL = 8


def _lam_power(pr, pi, n):
    for _ in range(int(math.log2(n))):
        pr, pi = _cmul(pr, pi, pr, pi)
    return pr, pi


def _segment_entries(er, ei, pr, pi, reverse):
    seg, ns = er.shape
    row = lax.broadcasted_iota(jnp.int32, (seg, ns), 0)
    cr = jnp.zeros((seg, ns), F32)
    ci = jnp.zeros((seg, ns), F32)
    cur_r = jnp.zeros((1, ns), F32)
    cur_i = jnp.zeros((1, ns), F32)
    for s in (range(seg - 2, -1, -1) if reverse else range(1, seg)):
        src = s + 1 if reverse else s - 1
        mr, mi = _cmul(pr, pi, cur_r, cur_i)
        cur_r = jnp.sum(jnp.where(row == src, er, 0.0), axis=0, keepdims=True) + mr
        cur_i = jnp.sum(jnp.where(row == src, ei, 0.0), axis=0, keepdims=True) + mi
        cr = jnp.where(row == s, cur_r, cr)
        ci = jnp.where(row == s, cur_i, ci)
    return cr, ci


def s5_scan_fused_fwd(a_p, lam_re, lam_im, wb_re, wb_im, wc_re, wc_im, d_skip, name):
    l = a_p.shape[0]
    di = d_skip.shape[1]
    rows = S5_SEG * S5_TB
    nb = l // rows
    ns = wb_re.shape[2]

    def body(u_ref, lr_ref, li_ref, wbr_ref, wbi_ref, wcr_ref, wci_ref, ds_ref, y_ref, ckr_ref, cki_ref, bur, bui):
        lr = jnp.broadcast_to(lr_ref[0], (S5_SEG, ns))
        li = jnp.broadcast_to(li_ref[0], (S5_SEG, ns))

        def scan_block(b, carry, keep):
            def step(t, c):
                xr, xi = c
                sl = pl.ds(pl.multiple_of(b * rows + t * S5_SEG, S5_SEG), S5_SEG)
                nr = lr * xr - li * xi + bur[sl, :]
                ni = lr * xi + li * xr + bui[sl, :]
                if keep:
                    bur[sl, :] = nr
                    bui[sl, :] = ni
                return nr, ni

            return lax.fori_loop(0, S5_TB, step, carry, unroll=S5_UNROLL)

        def project(b, carry):
            rs = pl.ds(pl.multiple_of(b * rows, rows), rows)
            u = u_ref[rs, :]
            bur[rs, :] = _dot(u, wbr_ref[0], DN_NN)
            bui[rs, :] = _dot(u, wbi_ref[0], DN_NN)
            return scan_block(b, carry, False)

        zero = jnp.zeros((S5_SEG, ns), F32)
        er, ei = lax.fori_loop(0, nb, project, (zero, zero))
        pr, pi = _lam_power(lr_ref[0], li_ref[0], l // S5_SEG)
        entry = _segment_entries(er, ei, pr, pi, False)

        def emit(b, carry):
            ckr_ref[0, b] = carry[0]
            cki_ref[0, b] = carry[1]
            carry = scan_block(b, carry, True)
            rs = pl.ds(pl.multiple_of(b * rows, rows), rows)
            y_ref[rs, :] = (_dot(bur[rs, :], wcr_ref[0], DN_NN) - _dot(bui[rs, :], wci_ref[0], DN_NN)
                            + ds_ref[...] * u_ref[rs, :])
            return carry

        lax.fori_loop(0, nb, emit, entry)

    sb3 = lambda s: (s, 0, 0)
    st = jax.ShapeDtypeStruct
    return pl.pallas_call(
        body, name=name, grid=(S5_SB,),
        in_specs=[pl.BlockSpec((l, LANES), lambda s: (0, s)),
                  pl.BlockSpec((1, 1, ns), sb3), pl.BlockSpec((1, 1, ns), sb3),
                  pl.BlockSpec((1, LANES, ns), sb3), pl.BlockSpec((1, LANES, ns), sb3),
                  pl.BlockSpec((1, ns, LANES), sb3), pl.BlockSpec((1, ns, LANES), sb3),
                  pl.BlockSpec((1, LANES), lambda s: (0, s))],
        out_specs=[pl.BlockSpec((l, LANES), lambda s: (0, s)),
                   pl.BlockSpec((1, nb, S5_SEG, ns), lambda s: (s, 0, 0, 0)),
                   pl.BlockSpec((1, nb, S5_SEG, ns), lambda s: (s, 0, 0, 0))],
        out_shape=[st((l, di), F32), st((S5_SB, nb, S5_SEG, ns), F32), st((S5_SB, nb, S5_SEG, ns), F32)],
        scratch_shapes=[pltpu.VMEM((l, ns), F32), pltpu.VMEM((l, ns), F32)],
        compiler_params=_cparams(("parallel",)))(a_p, lam_re, lam_im, wb_re, wb_im, wc_re, wc_im, d_skip)


def s5_scan_fused_bwd(a_p, dy, lam_re, lam_im, wb_re, wb_im, wc_re, wc_im, d_skip, ck_re, ck_im, name):
    l = a_p.shape[0]
    di = d_skip.shape[1]
    rows = S5_SEG * S5_TB
    nb = l // rows
    ns = wb_re.shape[2]

    def body(u_ref, dy_ref, lr_ref, li_ref, wbr_ref, wbi_ref, wcr_ref, wci_ref, ds_ref, ckr_ref, cki_ref,
             du_ref, dwbr_ref, dwbi_ref, dwcr_ref, dwci_ref, dds_ref, dlr_ref, dli_ref, gr, gi, xr_b, xi_b):
        lr = jnp.broadcast_to(lr_ref[0], (S5_SEG, ns))
        li = jnp.broadcast_to(li_ref[0], (S5_SEG, ns))

        def back_project(k, carry):
            b = nb - 1 - k
            rs = pl.ds(pl.multiple_of(b * rows, rows), rows)
            dyv = dy_ref[rs, :]
            gr[rs, :] = _dot(dyv, wcr_ref[0], DN_NT)
            gi[rs, :] = -_dot(dyv, wci_ref[0], DN_NT)

            def step(kk, c):
                ar, ai = c
                sl = pl.ds(pl.multiple_of(b * rows + (S5_TB - 1 - kk) * S5_SEG, S5_SEG), S5_SEG)
                return gr[sl, :] + lr * ar + li * ai, gi[sl, :] + lr * ai - li * ar

            return lax.fori_loop(0, S5_TB, step, carry, unroll=S5_UNROLL)

        zero = jnp.zeros((S5_SEG, ns), F32)
        er, ei = lax.fori_loop(0, nb, back_project, (zero, zero))
        pr, pi = _lam_power(lr_ref[0], -li_ref[0], l // S5_SEG)
        a0r, a0i = _segment_entries(er, ei, pr, pi, True)

        dwbr_ref[...] = jnp.zeros_like(dwbr_ref)
        dwbi_ref[...] = jnp.zeros_like(dwbi_ref)
        dwcr_ref[...] = jnp.zeros_like(dwcr_ref)
        dwci_ref[...] = jnp.zeros_like(dwci_ref)
        dds_ref[...] = jnp.zeros_like(dds_ref)

        def block(k, carry):
            b = nb - 1 - k
            rs = pl.ds(pl.multiple_of(b * rows, rows), rows)
            u = u_ref[rs, :]
            dyv = dy_ref[rs, :]
            body_rows = pl.ds(S5_SEG, rows)
            x0r, x0i = ckr_ref[0, b], cki_ref[0, b]
            xr_b[0:S5_SEG, :] = x0r
            xi_b[0:S5_SEG, :] = x0i
            xr_b[body_rows, :] = _dot(u, wbr_ref[0], DN_NN)
            xi_b[body_rows, :] = _dot(u, wbi_ref[0], DN_NN)

            def fstep(t, c):
                xr, xi = c
                sl = pl.ds(pl.multiple_of((t + 1) * S5_SEG, S5_SEG), S5_SEG)
                nr = lr * xr - li * xi + xr_b[sl, :]
                ni = lr * xi + li * xr + xi_b[sl, :]
                xr_b[sl, :] = nr
                xi_b[sl, :] = ni
                return nr, ni

            lax.fori_loop(0, S5_TB, fstep, (x0r, x0i), unroll=S5_UNROLL)
            dwcr_ref[0] += _dot(xr_b[body_rows, :], dyv, DN_TN)
            dwci_ref[0] -= _dot(xi_b[body_rows, :], dyv, DN_TN)

            def bstep(kk, c):
                ar, ai = c
                sl = pl.ds(pl.multiple_of(b * rows + (S5_TB - 1 - kk) * S5_SEG, S5_SEG), S5_SEG)
                nr = gr[sl, :] + lr * ar + li * ai
                ni = gi[sl, :] + lr * ai - li * ar
                gr[sl, :] = nr
                gi[sl, :] = ni
                return nr, ni

            ar, ai = lax.fori_loop(0, S5_TB, bstep, carry[:2], unroll=S5_UNROLL)
            a_r, a_i = gr[rs, :], gi[rs, :]
            p_r, p_i = xr_b[0:rows, :], xi_b[0:rows, :]
            per_seg = lambda v: jnp.sum(v.reshape(S5_TB, S5_SEG, ns), axis=0)
            carry = (ar, ai, carry[2] + per_seg(a_r * p_r + a_i * p_i), carry[3] + per_seg(a_i * p_r - a_r * p_i))
            du_ref[rs, :] = (_dot(a_r, wbr_ref[0], DN_NT) + _dot(a_i, wbi_ref[0], DN_NT) + ds_ref[...] * dyv).astype(BF16)
            dwbr_ref[0] += _dot(u, a_r, DN_TN)
            dwbi_ref[0] += _dot(u, a_i, DN_TN)
            dds_ref[...] += jnp.sum(dyv * u, axis=0, keepdims=True)
            return carry

        _, _, dlr, dli = lax.fori_loop(0, nb, block, (a0r, a0i, zero, zero))
        dlr_ref[0] = dlr
        dli_ref[0] = dli

    sb3 = lambda s: (s, 0, 0)
    seq = pl.BlockSpec((l, LANES), lambda s: (0, s))
    ck = pl.BlockSpec((1, nb, S5_SEG, ns), lambda s: (s, 0, 0, 0))
    st = jax.ShapeDtypeStruct
    return pl.pallas_call(
        body, name=name, grid=(S5_SB,),
        in_specs=[seq, seq, pl.BlockSpec((1, 1, ns), sb3), pl.BlockSpec((1, 1, ns), sb3),
                  pl.BlockSpec((1, LANES, ns), sb3), pl.BlockSpec((1, LANES, ns), sb3),
                  pl.BlockSpec((1, ns, LANES), sb3), pl.BlockSpec((1, ns, LANES), sb3),
                  pl.BlockSpec((1, LANES), lambda s: (0, s)), ck, ck],
        out_specs=[seq, pl.BlockSpec((1, LANES, ns), sb3), pl.BlockSpec((1, LANES, ns), sb3),
                   pl.BlockSpec((1, ns, LANES), sb3), pl.BlockSpec((1, ns, LANES), sb3),
                   pl.BlockSpec((1, LANES), lambda s: (0, s)),
                   pl.BlockSpec((1, S5_SEG, ns), sb3), pl.BlockSpec((1, S5_SEG, ns), sb3)],
        out_shape=[st((l, di), BF16), st((S5_SB, LANES, ns), F32), st((S5_SB, LANES, ns), F32),
                   st((S5_SB, ns, LANES), F32), st((S5_SB, ns, LANES), F32), st((1, di), F32),
                   st((S5_SB, S5_SEG, ns), F32), st((S5_SB, S5_SEG, ns), F32)],
        scratch_shapes=[pltpu.VMEM((l, ns), F32), pltpu.VMEM((l, ns), F32),
                        pltpu.VMEM((rows + S5_SEG, ns), F32), pltpu.VMEM((rows + S5_SEG, ns), F32)],
        compiler_params=_cparams(("parallel",)))(
        a_p, dy, lam_re, lam_im, wb_re, wb_im, wc_re, wc_im, d_skip, ck_re, ck_im)


def s5_act(y, name):
    l, d = y.shape
    tl = ROW_TILE

    def body(y_ref, o_ref):
        o_ref[...] = _gelu(y_ref[...]).astype(BF16)

    return _rowcall(body, name, l // tl, [_rows(tl, d)], _rows(tl, d), jax.ShapeDtypeStruct((l, d), BF16))(y)


def s5_gate_fwd(y, t, b_glu, a_p, name):
    l, d = y.shape
    tl = ROW_TILE

    def body(y_ref, t_ref, b_ref, z_ref, m_ref):
        yg = _gelu(y_ref[...])
        m_ref[...] = (yg * _sigmoid(t_ref[...] + b_ref[...]) * _silu(z_ref[...])).astype(BF16)

    return _rowcall(body, name, l // tl, [_rows(tl, d), _rows(tl, d), _full((1, d)), _rows(tl, d, 1)], _rows(tl, d),
                    jax.ShapeDtypeStruct((l, d), BF16))(y, t, b_glu.reshape(1, d), a_p)


def s5_gate_bwd(dm, y, t, b_glu, a_p, name):
    l, d = y.shape
    tl = ROW_TILE

    def body(dm_ref, y_ref, t_ref, b_ref, z_ref, dt_ref, dyg_ref, dz_ref, db_ref):
        i = pl.program_id(0)
        dmv = dm_ref[...]
        z = z_ref[...]
        yg = _gelu(y_ref[...])
        sg = _sigmoid(t_ref[...] + b_ref[...])
        y2 = yg * sg
        sz, gz = _silu_both(z)
        dy2 = dmv * sz
        dz_ref[...] = (dmv * y2 * gz).astype(BF16)
        dyg_ref[...] = dy2 * sg
        dt = dy2 * yg * sg * (1.0 - sg)
        dt_ref[...] = dt.astype(BF16)
        _acc(db_ref, jnp.sum(dt, axis=0, keepdims=True), i)

    st = jax.ShapeDtypeStruct
    return _rowcall(body, name, l // tl, [_rows(tl, d), _rows(tl, d), _rows(tl, d), _full((1, d)), _rows(tl, d, 1)],
                    [_rows(tl, d), _rows(tl, d), _rows(tl, d), _full((1, d))],
                    [st((l, d), BF16), st((l, d), F32), st((l, d), BF16), st((1, d), F32)])(
        dm, y, t, b_glu.reshape(1, d), a_p)


def s5_act_bwd(y, dyg_a, dyg_b, name):
    l, d = y.shape
    tl = ROW_TILE

    def body(y_ref, a_ref, b_ref, o_ref):
        o_ref[...] = (a_ref[...] + b_ref[...]) * _gelu_grad(y_ref[...])

    return _rowcall(body, name, l // tl, [_rows(tl, d)] * 3, _rows(tl, d), jax.ShapeDtypeStruct((l, d), F32))(y, dyg_a, dyg_b)


def _seg_perm(t):
    l, d = t.shape
    return t.reshape(S5_SEG, l // S5_SEG, d).transpose(1, 0, 2).reshape(l, d)


def _seg_unperm(t):
    l, d = t.shape
    return t.reshape(l // S5_SEG, S5_SEG, d).transpose(1, 0, 2).reshape(l, d)


def _s5_weights(p):
    lr, li, bbr, bbi = s5_params_fwd(p["a_re"], p["a_im"], p["log_step"], p["b_re"], p["b_im"])
    ns = 8 * S5_STATE
    lam_re = lr.reshape(S5_SB, 1, ns)
    lam_im = li.reshape(S5_SB, 1, ns)
    to_bd = lambda t: _blockdiag(t.reshape(S5_SB, 8, t.shape[1], t.shape[2]))
    wb_re = to_bd(bbr.transpose(0, 2, 1)).astype(BF16)
    wb_im = to_bd(bbi.transpose(0, 2, 1)).astype(BF16)
    wc_re = to_bd(p["c_re"].transpose(0, 2, 1)).astype(BF16)
    wc_im = to_bd(p["c_im"].transpose(0, 2, 1)).astype(BF16)
    return lam_re, lam_im, wb_re, wb_im, wc_re, wc_im


def s5_layer_fwd(h, p, wf, tag):
    l = h.shape[0]
    di = p["d_skip"].shape[0]
    hn = rms_fwd(h, p["norm_g"], tag + "_rms")
    hn_p = _seg_perm(hn)
    a_p = matmul(hn_p, wf["w_in"], "nn", tag + "_mm_in")
    sw = _s5_weights(p)
    dsk = p["d_skip"].reshape(1, di)
    y, ck_re, ck_im = s5_scan_fused_fwd(a_p, *sw, dsk, tag + "_scan")
    yg = s5_act(y, tag + "_act")
    t = matmul(yg, wf["w_glu"], "nn", tag + "_mm_glu")
    m = s5_gate_fwd(y, t, p["b_glu"], a_p, tag + "_gate")
    out_p = matmul(m, wf["w_out"], "nn", tag + "_mm_out")
    h_out = residual_add(h, _seg_unperm(out_p), tag + "_res")
    return h_out, (h, hn_p, a_p, sw, ck_re, ck_im, y, yg, t, m)


def residual_add(h, y, name):
    l, d = h.shape
    tl = ROW_TILE

    def body(h_ref, y_ref, o_ref):
        o_ref[...] = h_ref[...] + y_ref[...]

    return _rowcall(body, name, l // tl, [_rows(tl, d)] * 2, _rows(tl, d), jax.ShapeDtypeStruct((l, d), F32))(h, y)


def s5_layer_bwd(dh_out, saved, p, wf, tag, sink):
    h, hn_p, a_p, sw, ck_re, ck_im, y, yg, t, m = saved
    l = h.shape[0]
    di = p["d_skip"].shape[0]
    dsk = p["d_skip"].reshape(1, di)
    dout_p = _seg_perm(dh_out)
    dm = matmul(dout_p, wf["w_out"], "nt", tag + "_mm_dm")
    sink.mm("w_out", m, dout_p, tag + "_mm_gwout")
    dt, dyg_a, dz, db_glu = s5_gate_bwd(dm, y, t, p["b_glu"], a_p, tag + "_gate_bwd")
    dyg_b = matmul(dt, wf["w_glu"], "nt", tag + "_mm_dyg")
    sink.mm("w_glu", yg, dt, tag + "_mm_gwglu")
    dy = s5_act_bwd(y, dyg_a, dyg_b, tag + "_act_bwd")
    du, dwbr, dwbi, dwcr, dwci, dds, dlr, dli = s5_scan_fused_bwd(a_p, dy, *sw, dsk, ck_re, ck_im, tag + "_scanb")
    da = jnp.concatenate([du, dz], axis=1)
    dhn_p = matmul(da, wf["w_in"], "nt", tag + "_mm_dhn")
    sink.mm("w_in", hn_p, da, tag + "_mm_gwin")
    zero = sink.send()
    dh, dng = rms_bwd(h, p["norm_g"] + zero, _seg_unperm(dhn_p), dh_out, tag + "_rms_bwd")
    ex = lambda m_, r, c: _blockdiag_extract(m_, r, c).reshape(S5_GROUPS, r, c).transpose(0, 2, 1)
    dbb_re, dbb_im = ex(dwbr, S5_GROUP, S5_STATE), ex(dwbi, S5_GROUP, S5_STATE)
    g_c_re, g_c_im = ex(dwcr, S5_STATE, S5_GROUP), ex(dwci, S5_STATE, S5_GROUP)
    dl_re = lane_sum8(dlr).reshape(S5_GROUPS, S5_STATE, 1)
    dl_im = lane_sum8(dli).reshape(S5_GROUPS, S5_STATE, 1)
    gar, gai, gls, gbr, gbi = s5_params_bwd(p["a_re"], p["a_im"], p["log_step"], p["b_re"], p["b_im"],
                                            dl_re, dl_im, dbb_re, dbb_im)
    grads = {"norm_g": dng.reshape(-1), "a_re": gar.reshape(S5_GROUPS, S5_STATE),
             "a_im": gai.reshape(S5_GROUPS, S5_STATE), "log_step": gls.reshape(-1), "b_re": gbr, "b_im": gbi,
             "c_re": g_c_re, "c_im": g_c_im, "d_skip": dds.reshape(-1), "b_glu": db_glu.reshape(-1)}
    return dh, grads


def lane_sum8(t):
    sb, seg, ns = t.shape

    def body(t_ref, o_ref):
        o_ref[...] = jnp.sum(t_ref[...], axis=1, keepdims=True)

    return pl.pallas_call(body, name="s5_seg_sum", out_shape=jax.ShapeDtypeStruct((sb, 1, ns), F32))(t)


MLA_DI = MLA_HEADS * 128
MLA_CQ0 = MLA_DI
MLA_CKV0 = MLA_CQ0 + MLA_Q_RANK
MLA_KR0 = MLA_CKV0 + MLA_KV_RANK
MLA_AW = MLA_KR0 + LANES


def _rot_half(x):
    w = x.shape[-1]
    lane = lax.broadcasted_iota(jnp.int32, x.shape, x.ndim - 1)
    return jnp.where(lane % MLA_ROPE < MLA_ROPE // 2, pltpu.roll(x, w - MLA_ROPE // 2, x.ndim - 1),
                     pltpu.roll(x, MLA_ROPE // 2, x.ndim - 1))


def rope_tables(pos):
    l = pos.shape[0]
    tl = ROW_TILE
    j = np.arange(LANES) % MLA_ROPE % (MLA_ROPE // 2)
    inv_freq = (ROPE_THETA ** (-(2.0 * j) / MLA_ROPE)).astype(np.float32).reshape(1, LANES)
    sign = np.where(np.arange(LANES) % MLA_ROPE < MLA_ROPE // 2, -1.0, 1.0).astype(np.float32).reshape(1, LANES)

    def body(p_ref, f_ref, s_ref, cos_ref, sin_ref):
        ang = p_ref[...].astype(F32) * f_ref[...]
        cos_ref[...] = jnp.cos(ang)
        sin_ref[...] = jnp.sin(ang) * s_ref[...]

    st = jax.ShapeDtypeStruct((l, LANES), F32)
    return _rowcall(body, "rope_tables", l // tl, [_rows(tl, 1), _full((1, LANES)), _full((1, LANES))],
                    [_rows(tl, LANES)] * 2, [st, st])(pos, jnp.asarray(inv_freq), jnp.asarray(sign))


def _rope(x, cos, sins):
    return x * cos + _rot_half(x) * sins


def _rope_t(dy, cos, sins):
    return dy * cos - sins * _rot_half(dy)


def _rmsn(x):
    r = lax.rsqrt(jnp.mean(x * x, axis=-1, keepdims=True) + NORM_EPS)
    return x * r, r


def mla_pre(a, q_g, kv_g, cos, sins, name):
    l = a.shape[0]
    tl = ROW_TILE

    def body(a_ref, qg_ref, kg_ref, cos_ref, sin_ref, cq_ref, ckv_ref, krs_ref):
        xq, _ = _rmsn(a_ref[:, MLA_CQ0:MLA_CKV0])
        cq_ref[...] = (xq * qg_ref[...]).astype(BF16)
        xk, _ = _rmsn(a_ref[:, MLA_CKV0:MLA_KR0])
        ckv_ref[...] = (xk * kg_ref[...]).astype(BF16)
        kr = a_ref[:, MLA_KR0:MLA_AW]
        kr2 = kr + pltpu.roll(kr, MLA_ROPE, 1)
        kr2 = _rope(kr2, cos_ref[...], sin_ref[...])
        lane = lax.broadcasted_iota(jnp.int32, kr2.shape, 1)
        krs_ref[0] = jnp.where(lane < MLA_ROPE, kr2, 0.0).astype(BF16)
        krs_ref[1] = jnp.where(lane >= MLA_ROPE, kr2, 0.0).astype(BF16)

    st = jax.ShapeDtypeStruct
    return _rowcall(body, name, l // tl,
                    [_rows(tl, MLA_AW), _full((1, MLA_Q_RANK)), _full((1, MLA_KV_RANK)), _rows(tl, LANES), _rows(tl, LANES)],
                    [_rows(tl, MLA_Q_RANK), _rows(tl, MLA_KV_RANK), pl.BlockSpec((2, tl, LANES), lambda i: (0, i, 0))],
                    [st((l, MLA_Q_RANK), BF16), st((l, MLA_KV_RANK), BF16), st((2, l, LANES), BF16)])(
        a, q_g.reshape(1, -1), kv_g.reshape(1, -1), cos, sins)


def mla_rope_q(qr, cos, sins, name):
    l, w = qr.shape
    tl = ROW_TILE

    def body(q_ref, cos_ref, sin_ref, o_ref):
        c, s = cos_ref[...], sin_ref[...]
        for p in range(w // LANES):
            sl = slice(p * LANES, (p + 1) * LANES)
            o_ref[:, sl] = _rope(q_ref[:, sl], c, s).astype(BF16)

    return _rowcall(body, name, l // tl, [_rows(tl, w), _rows(tl, LANES), _rows(tl, LANES)], _rows(tl, w),
                    jax.ShapeDtypeStruct((l, w), BF16))(qr, cos, sins)


ATT_OUT = 512
ATT_IN = 512
ATT_R = ATT_OUT // ATT_IN


def _scores(qn, qr, kn, kr, mask_off, transposed):
    q2 = jnp.concatenate([qn, qr], axis=1)
    k2 = jnp.concatenate([kn, kr], axis=1)
    s = (_dot(k2, q2, DN_NT) if transposed else _dot(q2, k2, DN_NT)) * MLA_SCALE
    if mask_off is None:
        return s
    r = lax.broadcasted_iota(jnp.int32, s.shape, 0)
    c = lax.broadcasted_iota(jnp.int32, s.shape, 1)
    return jnp.where((r <= c + mask_off) if transposed else (c + mask_off <= r), s, NEG_INF)


def _fold(x, op):
    out = x[:, :LANES]
    for t in range(1, x.shape[1] // LANES):
        out = op(out, x[:, t * LANES:(t + 1) * LANES])
    return out


def flash_fwd(qn, qr, kv, krs, name):
    l = qn.shape[0]
    nq = l // ATT_OUT

    def body(qn_ref, qr_ref, kv_ref, kr_ref, o_ref, lse_ref, s_buf):
        qi = pl.program_id(1)
        q_r = qr_ref[...]
        q_n = [qn_ref[:, hh * LANES:(hh + 1) * LANES] for hh in range(2)]

        def block_scores(j, mx, mask_off):
            sl = pl.ds(pl.multiple_of(j * ATT_IN, ATT_IN), ATT_IN)
            out = []
            for hh in range(2):
                s = _scores(q_n[hh], q_r, kv_ref[sl, 2 * hh * LANES:(2 * hh + 1) * LANES], kr_ref[hh, sl, :],
                            mask_off, False)
                s_buf[hh, j] = s
                out.append(jnp.maximum(mx[hh], _fold(s, jnp.maximum)))
            return tuple(out)

        ninf = jnp.full((ATT_OUT, LANES), NEG_INF, F32)
        mx = lax.fori_loop(0, ATT_R * qi, lambda j, c: block_scores(j, c, None), (ninf, ninf))
        for d in range(ATT_R):
            mx = block_scores(ATT_R * qi + d, mx, d * ATT_IN)
        m = [jnp.max(mx[hh], axis=-1, keepdims=True) for hh in range(2)]

        def block_pv(j, carry):
            sl = pl.ds(pl.multiple_of(j * ATT_IN, ATT_IN), ATT_IN)
            out = []
            for hh in range(2):
                ls, acc = carry[hh]
                p = jnp.exp(s_buf[hh, j] - m[hh])
                out.append((ls + _fold(p, jnp.add),
                            acc + _dot(p, kv_ref[sl, (2 * hh + 1) * LANES:(2 * hh + 2) * LANES], DN_NN)))
            return tuple(out)

        z = jnp.zeros((ATT_OUT, LANES), F32)
        res = lax.fori_loop(0, ATT_R * (qi + 1), block_pv, ((z, z), (z, z)))
        for hh in range(2):
            lsum = jnp.sum(res[hh][0], axis=-1, keepdims=True)
            o_ref[:, hh * LANES:(hh + 1) * LANES] = res[hh][1] / lsum
            lse_ref[hh] = m[hh] + jnp.log(lsum)

    st = jax.ShapeDtypeStruct
    return pl.pallas_call(
        body, name=name, grid=(MLA_HEADS // 2, nq),
        in_specs=[pl.BlockSpec((ATT_OUT, 2 * LANES), lambda p, i: (i, p)),
                  pl.BlockSpec((ATT_OUT, LANES), lambda p, i: (i, p)),
                  pl.BlockSpec((l, 4 * LANES), lambda p, i: (0, p)),
                  pl.BlockSpec((2, l, LANES), lambda p, i: (0, 0, 0))],
        out_specs=[pl.BlockSpec((ATT_OUT, 2 * LANES), lambda p, i: (i, p)),
                   pl.BlockSpec((2, ATT_OUT, 1), lambda p, i: (p, i, 0))],
        out_shape=[st((l, MLA_DI), F32), st((MLA_HEADS, l, 1), F32)],
        scratch_shapes=[pltpu.VMEM((2, l // ATT_IN, ATT_OUT, ATT_IN), F32)],
        compiler_params=_cparams(("parallel", "arbitrary")))(qn, qr, kv, krs)


def flash_dkv(qn, qr, kv, krs, do, lse_row, delta_row, name):
    l = qn.shape[0]
    nk = l // ATT_OUT
    nq = l // ATT_IN

    def body(qn_ref, qr_ref, do_ref, lse_ref, dl_ref, kv_ref, kr_ref, dkv_ref, dkr_ref):
        kj = pl.program_id(1)
        lane = lax.broadcasted_iota(jnp.int32, (ATT_OUT, LANES), 1)
        kn = [kv_ref[:, 2 * hh * LANES:(2 * hh + 1) * LANES] for hh in range(2)]
        v = [kv_ref[:, (2 * hh + 1) * LANES:(2 * hh + 2) * LANES] for hh in range(2)]

        def block(i, carry, mask_off):
            sl = pl.ds(pl.multiple_of(i * ATT_IN, ATT_IN), ATT_IN)
            q_r = qr_ref[sl, :]
            out = []
            for hh in range(2):
                dk2, dv = carry[hh]
                hs = slice(hh * LANES, (hh + 1) * LANES)
                q_n, d_o = qn_ref[sl, hs], do_ref[sl, hs]
                s = _scores(q_n, q_r, kn[hh], kr_ref[hh], mask_off, True)
                pt = jnp.exp(s - lse_ref[hh, i])
                dv = dv + _dot(pt, d_o, DN_NN)
                dpt = _dot(v[hh], d_o, DN_NT)
                dst = (pt * (dpt - dl_ref[hh, i]) * MLA_SCALE).astype(BF16)
                out.append((dk2 + _dot(dst, jnp.concatenate([q_n, q_r], axis=1), DN_NN), dv))
            return tuple(out)

        z = jnp.zeros((ATT_OUT, LANES), F32)
        z2 = jnp.zeros((ATT_OUT, 2 * LANES), F32)
        res = ((z2, z), (z2, z))
        for d in range(ATT_R):
            res = block(ATT_R * kj + d, res, d * ATT_IN)
        res = lax.fori_loop(ATT_R * (kj + 1), nq, lambda i, c: block(i, c, None), res)
        for hh in range(2):
            dkv_ref[:, 2 * hh * LANES:(2 * hh + 1) * LANES] = res[hh][0][:, :LANES].astype(BF16)
            dkv_ref[:, (2 * hh + 1) * LANES:(2 * hh + 2) * LANES] = res[hh][1].astype(BF16)
        dkr_ref[0] = jnp.where(lane < MLA_ROPE, res[0][0][:, LANES:], res[1][0][:, LANES:])

    st = jax.ShapeDtypeStruct
    return pl.pallas_call(
        body, name=name, grid=(MLA_HEADS // 2, nk),
        in_specs=[pl.BlockSpec((l, 2 * LANES), lambda p, j: (0, p)),
                  pl.BlockSpec((l, LANES), lambda p, j: (0, p)),
                  pl.BlockSpec((l, 2 * LANES), lambda p, j: (0, p)),
                  pl.BlockSpec((2, nq, 1, ATT_IN), lambda p, j: (p, 0, 0, 0)),
                  pl.BlockSpec((2, nq, 1, ATT_IN), lambda p, j: (p, 0, 0, 0)),
                  pl.BlockSpec((ATT_OUT, 4 * LANES), lambda p, j: (j, p)),
                  pl.BlockSpec((2, ATT_OUT, LANES), lambda p, j: (0, j, 0))],
        out_specs=[pl.BlockSpec((ATT_OUT, 4 * LANES), lambda p, j: (j, p)),
                   pl.BlockSpec((1, ATT_OUT, LANES), lambda p, j: (p, j, 0))],
        out_shape=[st((l, 2 * MLA_DI), BF16), st((MLA_HEADS // 2, l, LANES), F32)],
        compiler_params=_cparams(("parallel", "arbitrary")))(qn, qr, do, lse_row, delta_row, kv, krs)


def flash_dq(qn, qr, kv, krs, do, lse, delta, cos, sins, name):
    l = qn.shape[0]
    nq = l // ATT_OUT

    def body(qn_ref, qr_ref, do_ref, lse_ref, dl_ref, kv_ref, kr_ref, cos_ref, sin_ref, dqn_ref, dqr_ref):
        qi = pl.program_id(1)
        q_r = qr_ref[...]
        q_n = [qn_ref[:, hh * LANES:(hh + 1) * LANES] for hh in range(2)]
        d_o = [do_ref[:, hh * LANES:(hh + 1) * LANES] for hh in range(2)]
        lse_h = [lse_ref[hh] for hh in range(2)]
        dl_h = [dl_ref[hh] for hh in range(2)]

        def block(j, carry, mask_off):
            sl = pl.ds(pl.multiple_of(j * ATT_IN, ATT_IN), ATT_IN)
            dq2 = list(carry)
            for hh in range(2):
                kn = kv_ref[sl, 2 * hh * LANES:(2 * hh + 1) * LANES]
                v = kv_ref[sl, (2 * hh + 1) * LANES:(2 * hh + 2) * LANES]
                kr = kr_ref[hh, sl, :]
                s = _scores(q_n[hh], q_r, kn, kr, mask_off, False)
                pr = jnp.exp(s - lse_h[hh])
                dp = _dot(d_o[hh], v, DN_NT)
                ds = (pr * (dp - dl_h[hh]) * MLA_SCALE).astype(BF16)
                dq2[hh] = dq2[hh] + _dot(ds, jnp.concatenate([kn, kr], axis=1), DN_NN)
            return tuple(dq2)

        z2 = jnp.zeros((ATT_OUT, 2 * LANES), F32)
        res = lax.fori_loop(0, ATT_R * qi, lambda j, c: block(j, c, None), (z2, z2))
        for d in range(ATT_R):
            res = block(ATT_R * qi + d, res, d * ATT_IN)
        dqn_ref[:, 0:LANES] = res[0][:, :LANES].astype(BF16)
        dqn_ref[:, LANES:2 * LANES] = res[1][:, :LANES].astype(BF16)
        dqr = res[0][:, LANES:] + res[1][:, LANES:]
        dqr_ref[...] = _rope_t(dqr, cos_ref[...], sin_ref[...]).astype(BF16)

    st = jax.ShapeDtypeStruct
    return pl.pallas_call(
        body, name=name, grid=(MLA_HEADS // 2, nq),
        in_specs=[pl.BlockSpec((ATT_OUT, 2 * LANES), lambda p, i: (i, p)),
                  pl.BlockSpec((ATT_OUT, LANES), lambda p, i: (i, p)),
                  pl.BlockSpec((ATT_OUT, 2 * LANES), lambda p, i: (i, p)),
                  pl.BlockSpec((2, ATT_OUT, 1), lambda p, i: (p, i, 0)),
                  pl.BlockSpec((2, ATT_OUT, 1), lambda p, i: (p, i, 0)),
                  pl.BlockSpec((l, 4 * LANES), lambda p, i: (0, p)),
                  pl.BlockSpec((2, l, LANES), lambda p, i: (0, 0, 0)),
                  pl.BlockSpec((ATT_OUT, LANES), lambda p, i: (i, 0)),
                  pl.BlockSpec((ATT_OUT, LANES), lambda p, i: (i, 0))],
        out_specs=[pl.BlockSpec((ATT_OUT, 2 * LANES), lambda p, i: (i, p)),
                   pl.BlockSpec((ATT_OUT, LANES), lambda p, i: (i, p))],
        out_shape=[st((l, MLA_DI), BF16), st((l, MLA_HEADS * MLA_ROPE), BF16)],
        compiler_params=_cparams(("parallel", "arbitrary")))(qn, qr, do, lse, delta, kv, krs, cos, sins)


def mla_gate_fwd(o, a, name):
    l = o.shape[0]
    tl = ROW_TILE

    def body(o_ref, z_ref, m_ref):
        m_ref[...] = (o_ref[...] * _silu(z_ref[...])).astype(BF16)

    return _rowcall(body, name, l // tl, [_rows(tl, MLA_DI), _rows(tl, MLA_DI)], _rows(tl, MLA_DI),
                    jax.ShapeDtypeStruct((l, MLA_DI), BF16))(o, a)


def mla_gate_bwd(dm, o, a, name):
    l = o.shape[0]
    tl = ROW_TILE

    def body(dm_ref, o_ref, z_ref, do_ref, dz_ref, dl_ref):
        dmv, ov, z = dm_ref[...], o_ref[...], z_ref[...]
        sz, gz = _silu_both(z)
        d_o = dmv * sz
        do_ref[...] = d_o.astype(BF16)
        dz_ref[...] = (dmv * ov * gz).astype(BF16)
        pr = d_o * ov
        for h in range(MLA_HEADS):
            dl_ref[h] = jnp.sum(pr[:, h * LANES:(h + 1) * LANES], axis=1, keepdims=True)

    st = jax.ShapeDtypeStruct
    return _rowcall(body, name, l // tl, [_rows(tl, MLA_DI)] * 3,
                    [_rows(tl, MLA_DI), _rows(tl, MLA_DI), pl.BlockSpec((MLA_HEADS, tl, 1), lambda i: (0, i, 0))],
                    [st((l, MLA_DI), BF16), st((l, MLA_DI), BF16), st((MLA_HEADS, l, 1), F32)])(dm, o, a)


def mla_post(a, dcqn, dckvn, dkr_pairs, dz, q_g, kv_g, cos, sins, name):
    l = a.shape[0]
    tl = ROW_TILE
    npair = MLA_HEADS // 2

    def norm_bwd(x, g, dy):
        xhat, r = _rmsn(x)
        dxh = dy * g
        return r * (dxh - xhat * jnp.mean(dxh * xhat, axis=-1, keepdims=True)), jnp.sum(dy * xhat, axis=0, keepdims=True)

    def body(a_ref, dq_ref, dk_ref, dkr_ref, dz_ref, qg_ref, kg_ref, cos_ref, sin_ref, da_ref, dqg_ref, dkg_ref):
        i = pl.program_id(0)
        da_ref[:, 0:MLA_DI] = dz_ref[...]
        dcq, dqg = norm_bwd(a_ref[:, MLA_CQ0:MLA_CKV0], qg_ref[...], dq_ref[...])
        da_ref[:, MLA_CQ0:MLA_CKV0] = dcq.astype(BF16)
        dckv, dkg = norm_bwd(a_ref[:, MLA_CKV0:MLA_KR0], kg_ref[...], dk_ref[...])
        da_ref[:, MLA_CKV0:MLA_KR0] = dckv.astype(BF16)
        dk2 = dkr_ref[0]
        for p in range(1, npair):
            dk2 = dk2 + dkr_ref[p]
        dk2 = _rope_t(dk2, cos_ref[...], sin_ref[...])
        dk2 = dk2 + pltpu.roll(dk2, MLA_ROPE, 1)
        lane = lax.broadcasted_iota(jnp.int32, dk2.shape, 1)
        da_ref[:, MLA_KR0:MLA_AW] = jnp.where(lane < MLA_ROPE, dk2, 0.0).astype(BF16)
        _acc(dqg_ref, dqg, i)
        _acc(dkg_ref, dkg, i)

    st = jax.ShapeDtypeStruct
    return _rowcall(body, name, l // tl,
                    [_rows(tl, MLA_AW), _rows(tl, MLA_Q_RANK), _rows(tl, MLA_KV_RANK),
                     pl.BlockSpec((npair, tl, LANES), lambda i: (0, i, 0)), _rows(tl, MLA_DI),
                     _full((1, MLA_Q_RANK)), _full((1, MLA_KV_RANK)), _rows(tl, LANES), _rows(tl, LANES)],
                    [_rows(tl, MLA_AW), _full((1, MLA_Q_RANK)), _full((1, MLA_KV_RANK))],
                    [st((l, MLA_AW), BF16), st((1, MLA_Q_RANK), F32), st((1, MLA_KV_RANK), F32)])(
        a, dcqn, dckvn, dkr_pairs, dz, q_g.reshape(1, -1), kv_g.reshape(1, -1), cos, sins)


def _mla_w_in_perm(w):
    r = MLA_Q_RANK + MLA_KV_RANK + MLA_ROPE
    pad = jnp.zeros(w.shape[:-1] + (MLA_AW - MLA_KR0 - MLA_ROPE,), w.dtype)
    return jnp.concatenate([w[..., r:], w[..., :r], pad], axis=-1)


def _mla_w_in_unperm(g):
    r = MLA_Q_RANK + MLA_KV_RANK + MLA_ROPE
    return jnp.concatenate([g[..., MLA_DI:MLA_DI + r], g[..., :MLA_DI]], axis=-1)


def _mla_w_uq_split(w):
    k = w.shape[0]
    w3 = w.reshape(k, MLA_HEADS, MLA_NOPE + MLA_ROPE)
    return w3[:, :, :MLA_NOPE].reshape(k, MLA_HEADS * MLA_NOPE), w3[:, :, MLA_NOPE:].reshape(k, MLA_HEADS * MLA_ROPE)


def _mla_w_uq_merge(gn, gr):
    k = gn.shape[0]
    return jnp.concatenate([gn.reshape(k, MLA_HEADS, MLA_NOPE), gr.reshape(k, MLA_HEADS, MLA_ROPE)], axis=2).reshape(k, -1)


def mla_layer_fwd(h, p, wf, cos, sins, tag):
    hn = rms_fwd(h, p["norm_g"], tag + "_rms")
    w_in = _mla_w_in_perm(wf["w_in"])
    w_uq_n, w_uq_r = _mla_w_uq_split(wf["w_uq"])
    a = matmul(hn, w_in, "nn", tag + "_mm_in")
    cqn, ckvn, krs = mla_pre(a, p["q_norm_g"], p["kv_norm_g"], cos, sins, tag + "_pre")
    qn = matmul(cqn, w_uq_n, "nn", tag + "_mm_qn", out_dtype=BF16)
    qr_raw = matmul(cqn, w_uq_r, "nn", tag + "_mm_qr")
    qr = mla_rope_q(qr_raw, cos, sins, tag + "_rope_q")
    kv = matmul(ckvn, wf["w_ukv"], "nn", tag + "_mm_kv", out_dtype=BF16)
    o, lse = flash_fwd(qn, qr, kv, krs, tag + "_flash")
    m = mla_gate_fwd(o, a, tag + "_gate")
    h_out = matmul(m, wf["w_out"], "nn", tag + "_mm_out", add=h)
    return h_out, (h, hn, a, cqn, ckvn, krs, qn, qr, kv, o, lse, m, w_in, w_uq_n, w_uq_r)


def mla_layer_bwd(dh_out, saved, p, wf, cos, sins, tag, sink):
    h, hn, a, cqn, ckvn, krs, qn, qr, kv, o, lse, m, w_in, w_uq_n, w_uq_r = saved
    l = h.shape[0]
    dm = matmul(dh_out, wf["w_out"], "nt", tag + "_mm_dm")
    sink.mm("w_out", m, dh_out, tag + "_mm_gwout")
    do, dz, delta = mla_gate_bwd(dm, o, a, tag + "_gate_bwd")
    lse_row = lse.reshape(MLA_HEADS, l // ATT_IN, 1, ATT_IN)
    delta_row = delta.reshape(MLA_HEADS, l // ATT_IN, 1, ATT_IN)
    dkv, dkr_pairs = flash_dkv(qn, qr, kv, krs, do, lse_row, delta_row, tag + "_flash_dkv")
    dqn, dqr = flash_dq(qn, qr, kv, krs, do, lse, delta, cos, sins, tag + "_flash_dq")
    dcqn = matmul(dqn, w_uq_n, "nt", tag + "_mm_dcq_n")
    dcqn = matmul(dqr, w_uq_r, "nt", tag + "_mm_dcq_r", add=dcqn)
    g_uq_n = matmul(cqn, dqn, "tn", tag + "_mm_guq_n")
    g_uq_r = matmul(cqn, dqr, "tn", tag + "_mm_guq_r")
    dckvn = matmul(dkv, wf["w_ukv"], "nt", tag + "_mm_dckv")
    sink.mm("w_ukv", ckvn, dkv, tag + "_mm_gukv")
    da, dqg, dkg = mla_post(a, dcqn, dckvn, dkr_pairs, dz, p["q_norm_g"], p["kv_norm_g"], cos, sins, tag + "_post")
    dhn = matmul(da, w_in, "nt", tag + "_mm_dhn")
    g_w_in = matmul(hn, da, "tn", tag + "_mm_gwin")
    sink.put("w_uq", _mla_w_uq_merge(g_uq_n, g_uq_r))
    sink.put("w_in", _mla_w_in_unperm(g_w_in))
    zero = sink.send()
    dh, dng = rms_bwd(h, p["norm_g"] + zero, dhn, dh_out, tag + "_rms_bwd")
    grads = {"norm_g": dng.reshape(-1), "q_norm_g": dqg.reshape(-1), "kv_norm_g": dkg.reshape(-1)}
    return dh, grads


ANY = pl.BlockSpec(memory_space=pl.ANY)


def _me():
    return lax.axis_index("x"), lax.axis_index("y"), lax.axis_index("c")


def _chip():
    return 2 * lax.axis_index("x") + lax.axis_index("y")


def _other_chips(x, y):
    return [(1 - x, y), (x, 1 - y), (1 - x, 1 - y)]


def _rcopy(src, dst, ssem, rsem, dev):
    return pltpu.make_async_remote_copy(src_ref=src, dst_ref=dst, send_sem=ssem, recv_sem=rsem,
                                        device_id=dev, device_id_type=MESH)


def _half(ref, c, hf):
    return ref.at[pl.ds(c * hf, hf), :]


def weights_allgather(wb):
    nr, w = wb.shape
    hf = nr // 2

    def body(w_ref, o_ref, ssem, rsem):
        x, y, c = _me()
        k = 2 * x + y
        chips = _other_chips(x, y)
        first = [_rcopy(_half(w_ref, c, hf), _half(o_ref.at[k], c, hf), ssem.at[j], rsem.at[j], (cx, cy, c))
                 for j, (cx, cy) in enumerate(chips)]
        for cp in first:
            cp.start()
        passed = []
        for j, (cx, cy) in enumerate(chips):
            region = _half(o_ref.at[2 * cx + cy], c, hf)
            _rcopy(region, region, ssem.at[j], rsem.at[j], (cx, cy, c)).wait_recv()
            fwd = _rcopy(region, region, ssem.at[3 + j], rsem.at[3 + j], (x, y, 1 - c))
            fwd.start()
            passed.append(fwd)
        for j, (cx, cy) in enumerate(chips):
            region = _half(o_ref.at[2 * cx + cy], 1 - c, hf)
            _rcopy(region, region, ssem.at[3 + j], rsem.at[3 + j], (x, y, 1 - c)).wait_recv()
        for cp in first + passed:
            cp.wait_send()

    out = pl.pallas_call(
        body, name="weights_allgather", in_specs=[ANY], out_specs=ANY,
        out_shape=jax.ShapeDtypeStruct((N_CHIPS, nr, w), wb.dtype),
        scratch_shapes=[pltpu.SemaphoreType.DMA((6,)), pltpu.SemaphoreType.DMA((6,))],
    )(wb)
    return lax.dynamic_update_slice(out, wb[None], (_chip(), 0, 0))


HBM = pl.BlockSpec(memory_space=pltpu.HBM)
SEM = pl.BlockSpec(memory_space=pltpu.SEMAPHORE)
SPLIT_EFFECT = pltpu.SideEffectType.DATAFLOW_SIDE_EFFECTING


def gather_start(wb, after, name):
    nr, w = wb.shape
    hf = nr // 2

    def body(w_ref, land_ref, after_ref, ssem, rsem, w_thru, land_thru, token):
        x, y, c = _me()
        k = 2 * x + y
        for j, (cx, cy) in enumerate(_other_chips(x, y)):
            _rcopy(_half(w_ref, c, hf), _half(land_ref.at[k], c, hf), ssem.at[j], rsem.at[j], (cx, cy, c)).start()
        token[...] = jnp.zeros_like(token)

    land = lax.empty((N_CHIPS, nr, w), wb.dtype)
    return pl.pallas_call(
        body, name=name,
        out_shape=(pltpu.SemaphoreType.DMA((3,)), pltpu.SemaphoreType.DMA((3,)), pltpu.HBM(wb.shape, wb.dtype),
                   pltpu.HBM(land.shape, land.dtype), jax.ShapeDtypeStruct((8, LANES), F32)),
        in_specs=(HBM, HBM, ANY), out_specs=(SEM, SEM, HBM, HBM, pl.BlockSpec(memory_space=pltpu.VMEM)),
        input_output_aliases={0: 2, 1: 3},
        compiler_params=pltpu.CompilerParams(has_side_effects=SPLIT_EFFECT))(
        pltpu.with_memory_space_constraint(wb, pltpu.HBM), pltpu.with_memory_space_constraint(land, pltpu.HBM), after)


def gather_wait(ssem, rsem, w_thru, land_thru, after, name):
    nr, w = w_thru.shape
    hf = nr // 2

    def body(w_ref, land_ref, ssem_ref, rsem_ref, after_ref, w_dead, got_ref):
        x, y, c = _me()
        for j, (cx, cy) in enumerate(_other_chips(x, y)):
            cp = _rcopy(_half(w_ref, c, hf), _half(land_ref.at[2 * cx + cy], c, hf), ssem_ref.at[j], rsem_ref.at[j],
                        (cx, cy, c))
            cp.wait_send()
            cp.wait_recv()

    return pl.pallas_call(
        body, name=name, out_shape=(pltpu.HBM(w_thru.shape, w_thru.dtype), pltpu.HBM(land_thru.shape, land_thru.dtype)),
        in_specs=(HBM, HBM, SEM, SEM, ANY), out_specs=(HBM, HBM), input_output_aliases={0: 0, 1: 1},
        compiler_params=pltpu.CompilerParams(has_side_effects=SPLIT_EFFECT))(w_thru, land_thru, ssem, rsem, after)[1]


def gather_handover(land, wb, name):
    _, nr, w = land.shape
    hf = nr // 2

    def body(l_ref, o_ref, ssem, rsem):
        x, y, c = _me()
        chips = _other_chips(x, y)
        sends = []
        for j, (cx, cy) in enumerate(chips):
            region = _half(o_ref.at[2 * cx + cy], c, hf)
            sends.append(_rcopy(region, region, ssem.at[j], rsem.at[j], (x, y, 1 - c)))
            sends[-1].start()
        for j, (cx, cy) in enumerate(chips):
            region = _half(o_ref.at[2 * cx + cy], 1 - c, hf)
            _rcopy(region, region, ssem.at[j], rsem.at[j], (x, y, 1 - c)).wait_recv()
        for cp in sends:
            cp.wait_send()

    out = pl.pallas_call(
        body, name=name, in_specs=[ANY], out_specs=ANY, input_output_aliases={0: 0},
        out_shape=jax.ShapeDtypeStruct(land.shape, land.dtype),
        scratch_shapes=[pltpu.SemaphoreType.DMA((3,)), pltpu.SemaphoreType.DMA((3,))])(land)
    return lax.dynamic_update_slice(out, wb[None], (_chip(), 0, 0))


def reduce_start(t, after, name):
    def body(t_ref, land_ref, after_ref, ssem, rsem, t_thru, land_thru, token):
        x, y, c = _me()
        k = 2 * x + y
        for j, (cx, cy) in enumerate(_other_chips(x, y)):
            _rcopy(t_ref.at[2 * cx + cy], land_ref.at[k], ssem.at[j], rsem.at[j], (cx, cy, c)).start()
        token[...] = jnp.zeros_like(token)

    land = lax.empty(t.shape, t.dtype)
    return pl.pallas_call(
        body, name=name,
        out_shape=(pltpu.SemaphoreType.DMA((3,)), pltpu.SemaphoreType.DMA((3,)), pltpu.HBM(t.shape, t.dtype),
                   pltpu.HBM(t.shape, t.dtype), jax.ShapeDtypeStruct((8, LANES), F32)),
        in_specs=(HBM, HBM, ANY), out_specs=(SEM, SEM, HBM, HBM, pl.BlockSpec(memory_space=pltpu.VMEM)),
        input_output_aliases={0: 2, 1: 3},
        compiler_params=pltpu.CompilerParams(has_side_effects=SPLIT_EFFECT))(
        pltpu.with_memory_space_constraint(t, pltpu.HBM), pltpu.with_memory_space_constraint(land, pltpu.HBM), after)


def reduce_wait(ssem, rsem, t_thru, land_thru, after, name):
    def body(t_ref, land_ref, ssem_ref, rsem_ref, after_ref, t_out, got_ref):
        x, y, c = _me()
        k = 2 * x + y
        for j, (cx, cy) in enumerate(_other_chips(x, y)):
            cp = _rcopy(t_ref.at[k], land_ref.at[2 * cx + cy], ssem_ref.at[j], rsem_ref.at[j], (cx, cy, c))
            cp.wait_send()
            cp.wait_recv()

    return pl.pallas_call(
        body, name=name, out_shape=(pltpu.HBM(t_thru.shape, t_thru.dtype), pltpu.HBM(land_thru.shape, land_thru.dtype)),
        in_specs=(HBM, HBM, SEM, SEM, ANY), out_specs=(HBM, HBM), input_output_aliases={0: 0, 1: 1},
        compiler_params=pltpu.CompilerParams(has_side_effects=SPLIT_EFFECT))(t_thru, land_thru, ssem, rsem, after)


def grads_to_sibling(ps, name="grads_to_sibling"):
    n = len(ps)

    def body(*refs):
        p_refs, o_refs, ssem, rsem = refs[:n], refs[n:2 * n], refs[2 * n], refs[2 * n + 1]
        x, y, c = _me()
        cps = []
        for a in range(n):
            hf = ps[a].shape[1] // 2
            cps.append(_rcopy(p_refs[a].at[:, pl.ds((1 - c) * hf, hf), :], o_refs[a], ssem.at[a], rsem.at[a],
                              (x, y, 1 - c)))
        for cp in cps:
            cp.start()
        for cp in cps:
            cp.wait()

    return pl.pallas_call(
        body, name=name, in_specs=[ANY] * n, out_specs=[ANY] * n,
        out_shape=[jax.ShapeDtypeStruct((N_CHIPS, p.shape[1] // 2, p.shape[2]), p.dtype) for p in ps],
        scratch_shapes=[pltpu.SemaphoreType.DMA((n,)), pltpu.SemaphoreType.DMA((n,))])(*ps)


def pair_sum(p, ra, out_dtype, name):
    _, nr, w = p.shape
    hf = nr // 2
    tr = _pick_rows(hf)
    nb = hf // tr

    def body(c_ref, p_ref, r_ref, o_ref):
        o_ref[...] = (p_ref[...] + r_ref[...]).astype(out_dtype)

    c = lax.axis_index("c").astype(jnp.int32).reshape(1)
    return pl.pallas_call(
        body, name=name,
        grid_spec=pltpu.PrefetchScalarGridSpec(
            num_scalar_prefetch=1, grid=(N_CHIPS, nb),
            in_specs=[pl.BlockSpec((1, tr, w), lambda k, i, c_ref: (k, c_ref[0] * nb + i, 0)),
                      pl.BlockSpec((1, tr, w), lambda k, i, c_ref: (k, i, 0))],
            out_specs=pl.BlockSpec((1, tr, w), lambda k, i, c_ref: (k, i, 0))),
        out_shape=jax.ShapeDtypeStruct((N_CHIPS, hf, w), out_dtype),
        compiler_params=_cparams(("parallel", "parallel")))(c, p, ra)


def grads_across_chips(ts):
    n = len(ts)

    def body(*refs):
        t_refs, o_refs, ssem, rsem = refs[:n], refs[n:2 * n], refs[2 * n], refs[2 * n + 1]
        x, y, c = _me()
        k = 2 * x + y
        chips = _other_chips(x, y)
        sends = [_rcopy(t_refs[a].at[2 * cx + cy], o_refs[a].at[k], ssem.at[3 * a + j], rsem.at[3 * a + j], (cx, cy, c))
                 for a in range(n) for j, (cx, cy) in enumerate(chips)]
        for cp in sends:
            cp.start()
        for a in range(n):
            for j, (cx, cy) in enumerate(chips):
                _rcopy(t_refs[a].at[k], o_refs[a].at[2 * cx + cy], ssem.at[3 * a + j], rsem.at[3 * a + j],
                       (cx, cy, c)).wait_recv()
        for cp in sends:
            cp.wait_send()

    return pl.pallas_call(
        body, name="grads_across_chips", in_specs=[ANY] * n, out_specs=[ANY] * n,
        out_shape=[jax.ShapeDtypeStruct(t.shape, t.dtype) for t in ts],
        scratch_shapes=[pltpu.SemaphoreType.DMA((3 * n,)), pltpu.SemaphoreType.DMA((3 * n,))])(*ts)


def chip_sum(t, rb, name):
    _, hf, w = rb.shape
    tr = _pick_rows(hf)
    nb = hf // tr

    def body(kc_ref, t_ref, r_ref, o_ref):
        k = kc_ref[0]
        acc = jnp.where(k == 0, t_ref[0], r_ref[0]).astype(F32)
        for j in range(1, N_CHIPS):
            acc = acc + jnp.where(k == j, t_ref[0], r_ref[j]).astype(F32)
        o_ref[...] = acc

    kc = jnp.stack([_chip(), lax.axis_index("c")]).astype(jnp.int32)
    return pl.pallas_call(
        body, name=name,
        grid_spec=pltpu.PrefetchScalarGridSpec(
            num_scalar_prefetch=1, grid=(nb,),
            in_specs=[pl.BlockSpec((1, tr, w), lambda i, kc_ref: (kc_ref[0], i, 0)),
                      pl.BlockSpec((N_CHIPS, tr, w), lambda i, kc_ref: (0, i, 0))],
            out_specs=pl.BlockSpec((tr, w), lambda i, kc_ref: (kc_ref[1] * nb + i, 0))),
        out_shape=jax.ShapeDtypeStruct((2 * hf, w), F32), compiler_params=_cparams(("parallel",)))(kc, t, rb)


def reduced_to_sibling(gs):
    n = len(gs)

    def body(*refs):
        o_refs, ssem, rsem = refs[n:2 * n], refs[2 * n], refs[2 * n + 1]
        x, y, c = _me()
        cps = []
        for a in range(n):
            hf = gs[a].shape[0] // 2
            cps.append(_rcopy(_half(o_refs[a], c, hf), _half(o_refs[a], c, hf), ssem.at[a], rsem.at[a], (x, y, 1 - c)))
        for cp in cps:
            cp.start()
        for a in range(n):
            hf = gs[a].shape[0] // 2
            _rcopy(_half(o_refs[a], c, hf), _half(o_refs[a], 1 - c, hf), ssem.at[a], rsem.at[a],
                   (x, y, 1 - c)).wait_recv()
        for cp in cps:
            cp.wait_send()

    return pl.pallas_call(
        body, name="reduced_to_sibling", in_specs=[ANY] * n, out_specs=[ANY] * n,
        input_output_aliases={a: a for a in range(n)},
        out_shape=[jax.ShapeDtypeStruct(g.shape, g.dtype) for g in gs],
        scratch_shapes=[pltpu.SemaphoreType.DMA((n,)), pltpu.SemaphoreType.DMA((n,))])(*gs)


def small_allgather(g, row0, nrs):
    w = g.shape[1]

    def body(g_ref, o_ref, ssem, rsem):
        x, y, c = _me()
        k = 2 * x + y
        chips = _other_chips(x, y)
        src = g_ref.at[pl.ds(row0, nrs), :]
        sends = [_rcopy(src, o_ref.at[k], ssem.at[j], rsem.at[j], (cx, cy, c)) for j, (cx, cy) in enumerate(chips)]
        for cp in sends:
            cp.start()
        for j, (cx, cy) in enumerate(chips):
            _rcopy(src, o_ref.at[2 * cx + cy], ssem.at[j], rsem.at[j], (cx, cy, c)).wait_recv()
        for cp in sends:
            cp.wait_send()

    out = pl.pallas_call(
        body, name="small_allgather", in_specs=[ANY], out_specs=ANY,
        out_shape=jax.ShapeDtypeStruct((N_CHIPS, nrs, w), g.dtype),
        scratch_shapes=[pltpu.SemaphoreType.DMA((3,)), pltpu.SemaphoreType.DMA((3,))])(g)
    return lax.dynamic_update_slice(out, g[row0:row0 + nrs][None], (_chip(), 0, 0))


def _adamw_step(w_ref, g_ref, m_ref, v_ref, d_ref, nm_ref, nv_ref):
    bc1 = 1.0 - ADAM_B1 ** ADAM_STEP
    bc2 = 1.0 - ADAM_B2 ** ADAM_STEP
    gv = g_ref[...]
    nm = ADAM_B1 * m_ref[...] + (1.0 - ADAM_B1) * gv
    nv = ADAM_B2 * v_ref[...] + (1.0 - ADAM_B2) * (gv * gv)
    nm_ref[...] = nm
    nv_ref[...] = nv
    d_ref[...] = -ADAM_LR * ((nm / bc1) / (jnp.sqrt(nv / bc2) + ADAM_EPS) + ADAM_WD * w_ref[...])


def adamw_packed(w, g_buf, r0, m, v, name):
    r, c = w.shape
    tr = _tile_rows(r, r0)

    def body(w_ref, g_ref, m_ref, v_ref, go_ref, d_ref, nm_ref, nv_ref):
        go_ref[...] = g_ref[...]
        _adamw_step(w_ref, g_ref, m_ref, v_ref, d_ref, nm_ref, nv_ref)

    own = pl.BlockSpec((tr, CHUNK_W), lambda i, j: (i, j))
    packed = pl.BlockSpec((tr, CHUNK_W), lambda i, j: ((r0 + j * r) // tr + i, 0))
    st = jax.ShapeDtypeStruct((r, c), F32)
    return pl.pallas_call(body, name=name, grid=(r // tr, c // CHUNK_W), in_specs=[own, packed, own, own],
                          out_specs=[own] * 4, out_shape=[st] * 4,
                          compiler_params=_cparams(("parallel", "parallel")))(w, g_buf, m, v)


def adamw(w, g, m, v, name):
    r, wd = w.shape
    tr = _pick_rows(r, cap=max(16, ADAMW_BLOCK_BYTES // (4 * wd)))
    body = functools.partial(_adamw_step)

    spec = pl.BlockSpec((tr, wd), lambda i: (i, 0))
    st = jax.ShapeDtypeStruct((r, wd), F32)
    return pl.pallas_call(body, name=name, grid=(r // tr,), in_specs=[spec] * 4, out_specs=[spec] * 3,
                          out_shape=[st, st, st], compiler_params=_cparams(("parallel",)))(w, g, m, v)


LAYER_KINDS = ("gmlp", "s5", "mla", "gmlp")
PARAMS = {
    "gmlp": ("norm_g", "w_in", "ln_g", "ln_b", "w_s", "b_s", "w_out"),
    "s5": ("norm_g", "w_in", "a_re", "a_im", "log_step", "b_re", "b_im", "c_re", "c_im", "d_skip", "w_glu", "b_glu", "w_out"),
    "mla": ("norm_g", "w_in", "q_norm_g", "w_uq", "kv_norm_g", "w_ukv", "w_out"),
}
COL_SHARDED = ("w_in", "w_uq", "w_ukv")
ROW_SHARDED = ("w_out", "w_glu")
WEIGHT_NAMES = [("l%d_" % i) + n for i, kind in enumerate(LAYER_KINDS) for n in PARAMS[kind]] + ["final_norm_g"]


def _is_big(name):
    return name.split("_", 1)[1] in COL_SHARDED + ROW_SHARDED


BIG = [n for n in WEIGHT_NAMES if _is_big(n)]
SMALL = [n for n in WEIGHT_NAMES if not _is_big(n)]


def _pack_rows(blocks):
    return jnp.concatenate([b.reshape(-1, PACK_W) for b in blocks], axis=0)


def _shard_major(wn, full, width):
    r, c = full.shape
    if wn in COL_SHARDED:
        t = full.reshape(r, N_CHIPS, c // N_CHIPS).transpose(1, 0, 2)
    else:
        t = full.reshape(N_CHIPS, r // N_CHIPS, c)
    return t.reshape(N_CHIPS, -1, width)


def _from_shard_major(name, t, block_shape):
    r, c = block_shape
    if name.split("_", 1)[1] in COL_SHARDED:
        return t.reshape(N_CHIPS, r, c).transpose(1, 0, 2).reshape(r, N_CHIPS * c)
    return t.reshape(N_CHIPS * r, c)


class BigGradSink:
    ORDER = ("w_out", "w_glu", "w_ukv", "w_uq", "w_in")
    ROW_MAJOR = {2: ("w_uq", "w_in")}

    def __init__(self, layer, block_shapes):
        self.layer = layer
        self.regions = {}
        r0 = 0
        for wn in self.ORDER:
            if wn in block_shapes:
                shape = block_shapes[wn]
                self.regions[wn] = (r0, shape, wn not in self.ROW_MAJOR.get(layer, ()))
                r0 += shape[0] * shape[1] // CHUNK_W
        self.buf = lax.empty((N_CHIPS, r0, CHUNK_W), F32)
        self.flight = None

    def mm(self, wn, a, b, name):
        r0, _, direct = self.regions[wn]
        assert direct
        self.buf = matmul_tn_packed(a, b, self.buf, r0, wn in COL_SHARDED, name)

    def put(self, wn, full):
        r0, _, direct = self.regions[wn]
        assert not direct
        piece = _shard_major(wn, full, CHUNK_W)
        self.buf = lax.dynamic_update_slice(self.buf, piece, (0, r0, 0))

    def send(self):
        i = self.layer
        sib, = grads_to_sibling([self.buf], "grads_to_sibling_l%d" % i)
        t = pair_sum(self.buf, sib, BF16, "pair_sum_l%d" % i)
        self.flight = reduce_start(t, sib, "reduce_l%d_start" % i)
        return self.flight[4][0, 0]


def _small_pack(arrs, total_padded):
    flat = jnp.concatenate([a.reshape(-1) for a in arrs])
    return jnp.pad(flat, (0, total_padded - flat.shape[0]))


def kernel(x, positions, l0_norm_g, l0_w_in, l0_ln_g, l0_ln_b, l0_w_s, l0_b_s, l0_w_out, l1_norm_g, l1_w_in, l1_a_re, l1_a_im, l1_log_step, l1_b_re, l1_b_im, l1_c_re, l1_c_im, l1_d_skip, l1_w_glu, l1_b_glu, l1_w_out, l2_norm_g, l2_w_in, l2_q_norm_g, l2_w_uq, l2_kv_norm_g, l2_w_ukv, l2_w_out, l3_norm_g, l3_w_in, l3_ln_g, l3_ln_b, l3_w_s, l3_b_s, l3_w_out, final_norm_g, loss_target, m_l0_norm_g, m_l0_w_in, m_l0_ln_g, m_l0_ln_b, m_l0_w_s, m_l0_b_s, m_l0_w_out, m_l1_norm_g, m_l1_w_in, m_l1_a_re, m_l1_a_im, m_l1_log_step, m_l1_b_re, m_l1_b_im, m_l1_c_re, m_l1_c_im, m_l1_d_skip, m_l1_w_glu, m_l1_b_glu, m_l1_w_out, m_l2_norm_g, m_l2_w_in, m_l2_q_norm_g, m_l2_w_uq, m_l2_kv_norm_g, m_l2_w_ukv, m_l2_w_out, m_l3_norm_g, m_l3_w_in, m_l3_ln_g, m_l3_ln_b, m_l3_w_s, m_l3_b_s, m_l3_w_out, m_final_norm_g, v_l0_norm_g, v_l0_w_in, v_l0_ln_g, v_l0_ln_b, v_l0_w_s, v_l0_b_s, v_l0_w_out, v_l1_norm_g, v_l1_w_in, v_l1_a_re, v_l1_a_im, v_l1_log_step, v_l1_b_re, v_l1_b_im, v_l1_c_re, v_l1_c_im, v_l1_d_skip, v_l1_w_glu, v_l1_b_glu, v_l1_w_out, v_l2_norm_g, v_l2_w_in, v_l2_q_norm_g, v_l2_w_uq, v_l2_kv_norm_g, v_l2_w_ukv, v_l2_w_out, v_l3_norm_g, v_l3_w_in, v_l3_ln_g, v_l3_ln_b, v_l3_w_s, v_l3_b_s, v_l3_w_out, v_final_norm_g):
    args = locals()
    w = {n: args[n] for n in WEIGHT_NAMES}
    mom_m = {n: args["m_" + n] for n in WEIGHT_NAMES}
    mom_v = {n: args["v_" + n] for n in WEIGHT_NAMES}
    h0 = x[0]
    target = loss_target[0]
    pos = positions.reshape(-1, 1)

    full = {}

    def pack_unit(layers):
        names = [n for n in BIG if int(n[1]) in layers]
        rows = [w[n].size // PACK_W for n in names]
        pad = -sum(rows) % PACK_ROW_ALIGN
        return names, rows, _pack_rows([w[n].astype(BF16) for n in names] + [jnp.zeros((pad, PACK_W), BF16)])

    def unpack_unit(names, rows, gathered):
        r0 = 0
        for n, nr in zip(names, rows):
            full[n] = _from_shard_major(n, gathered[:, r0:r0 + nr, :], w[n].shape)
            r0 += nr

    unit0, unit1, unit2 = pack_unit((0,)), pack_unit((1,)), pack_unit((2, 3))
    unpack_unit(unit0[0], unit0[1], weights_allgather(unit0[2]))
    flight = gather_start(unit1[2], unit0[2], "gather_l1_start")
    wp = dict(w)
    wp["l0_norm_g"] = w["l0_norm_g"] + flight[4][0, 0]

    def layer_params(i):
        pre = "l%d_" % i
        p = {k[len(pre):]: v for k, v in wp.items() if k.startswith(pre)}
        wf = {k[len(pre):]: v for k, v in full.items() if k.startswith(pre)}
        return p, wf

    cos, sins = rope_tables(pos)
    h = h0
    saved = []
    for i, kind in enumerate(LAYER_KINDS):
        if i == 1:
            land = gather_wait(*flight[:4], h, "gather_l1_wait")
            got = gather_handover(land, unit1[2], "gather_l1_handover")
            unpack_unit(unit1[0], unit1[1], got)
            flight = gather_start(unit2[2], got, "gather_l23_start")
            wp["l1_norm_g"] = w["l1_norm_g"] + flight[4][0, 0]
        if i == 2:
            land = gather_wait(*flight[:4], h, "gather_l23_wait")
            unpack_unit(unit2[0], unit2[1], gather_handover(land, unit2[2], "gather_l23_handover"))
        p, wf = layer_params(i)
        tag = "l%d" % i
        if kind == "gmlp":
            h, s = gmlp_layer_fwd(h, p, wf, tag)
        elif kind == "s5":
            h, s = s5_layer_fwd(h, p, wf, tag)
        else:
            h, s = mla_layer_fwd(h, p, wf, cos, sins, tag)
        saved.append(s)
    loss_part, dh, g_final = loss_head(h, final_norm_g, target)

    grads = {"final_norm_g": g_final.reshape(-1)}
    sinks = {}

    for i in reversed(range(len(LAYER_KINDS))):
        kind = LAYER_KINDS[i]
        p, wf = layer_params(i)
        tag = "l%d" % i
        sink = sinks[i] = BigGradSink(i, {n[3:]: w[n].shape for n in BIG if int(n[1]) == i})
        if kind == "gmlp":
            dh, g = gmlp_layer_bwd(dh, saved[i], p, wf, tag, sink)
        elif kind == "s5":
            dh, g = s5_layer_bwd(dh, saved[i], p, wf, tag, sink)
        else:
            dh, g = mla_layer_bwd(dh, saved[i], p, wf, cos, sins, tag, sink)
        for k, val in g.items():
            grads["l%d_%s" % (i, k)] = val
    grad_x = dh[None]

    n_small = sum(w[n].size for n in SMALL)
    piece = N_CHIPS * 2 * 16 * PACK_W
    n_small_pad = -(-(n_small + 1) // piece) * piece
    nrs = n_small_pad // N_CHIPS // PACK_W
    p_small = _small_pack([grads[n] for n in SMALL] + [loss_part], n_small_pad).reshape(N_CHIPS, nrs, PACK_W)
    sib_small, = grads_to_sibling([p_small], "grads_to_sibling_small")
    t_small = pair_sum(p_small, sib_small, F32, "pair_sum_small")
    rb_small, = grads_across_chips([t_small])
    halves = [chip_sum(t_small, rb_small, "chip_sum_small")]

    after = halves[0]
    for i in reversed(range(len(LAYER_KINDS))):
        t_i, rb_i = reduce_wait(*sinks[i].flight[:4], after, "reduce_l%d_wait" % i)
        halves.append(chip_sum(t_i, rb_i, "chip_sum_l%d" % i))
        after = halves[-1]
    reduced = reduced_to_sibling(halves)
    small_all = small_allgather(reduced[0], 0, nrs)

    g_out, d_out, nm_out, nv_out = {}, {}, {}, {}
    for i, g_i in zip(reversed(range(len(LAYER_KINDS))), reduced[1:]):
        for wn, (r0, shape, direct) in sinks[i].regions.items():
            n = "l%d_%s" % (i, wn)
            if direct:
                g_out[n], d_out[n], nm_out[n], nv_out[n] = adamw_packed(w[n], g_i, r0, mom_m[n], mom_v[n], "adamw_" + n)
            else:
                g_out[n] = g_i[r0:r0 + shape[0] * shape[1] // CHUNK_W].reshape(shape)
                d_out[n], nm_out[n], nv_out[n] = adamw(w[n], g_out[n], mom_m[n], mom_v[n], "adamw_" + n)
    g_small = small_all.reshape(-1, PACK_W)
    sp = lambda d: _small_pack([d[n] for n in SMALL], n_small_pad).reshape(-1, PACK_W)
    d_small, nm_small, nv_small = adamw(sp(w), g_small, sp(mom_m), sp(mom_v), "adamw_small")
    for buf, out in ((g_small, g_out), (d_small, d_out), (nm_small, nm_out), (nv_small, nv_out)):
        flat = buf.reshape(-1)
        o = 0
        for n in SMALL:
            out[n] = flat[o:o + w[n].size].reshape(w[n].shape)
            o += w[n].size
    loss = g_small.reshape(-1)[n_small]
    return (loss, grad_x, *[g_out[n] for n in WEIGHT_NAMES], *[d_out[n] for n in WEIGHT_NAMES],
            *[nm_out[n] for n in WEIGHT_NAMES], *[nv_out[n] for n in WEIGHT_NAMES])
```

```python
import functools
import math

import jax
import jax.numpy as jnp
import numpy as np
from jax import lax
from jax.experimental import pallas as pl
from jax.experimental.pallas import tpu as pltpu

F32 = jnp.float32
BF16 = jnp.bfloat16
MESH = pl.DeviceIdType.MESH
VMEM_LIMIT_BYTES = 56 * 1024 * 1024
LANES = 128
PACK_W = 1024
CHUNK_W = 256
PACK_ROW_ALIGN = 256
ROW_TILE = 256
ADAMW_BLOCK_BYTES = 1024 * 1024
MM_BLOCK_BYTES = 6 * 1024 * 1024

NORM_EPS = 1e-6
N_CHIPS = 4
GMLP_CHUNK = 128
GMLP_GROUPS = 8
S5_GROUPS = 128
S5_GROUP = 16
S5_STATE = 64
S5_SB = 16
S5_SEG = 8
MLA_HEADS = 16
MLA_NOPE = 128
MLA_ROPE = 64
MLA_Q_RANK = 384
MLA_KV_RANK = 128
MLA_SCALE = (MLA_NOPE + MLA_ROPE) ** -0.5
ROPE_THETA = 10000.0
NEG_INF = -1e30
ADAM_LR, ADAM_B1, ADAM_B2, ADAM_EPS, ADAM_WD, ADAM_STEP = 0.001, 0.9, 0.999, 1e-08, 0.01, 10

DN_NN = (((1,), (0,)), ((), ()))
DN_NT = (((1,), (1,)), ((), ()))
DN_TN = (((0,), (0,)), ((), ()))


def _cparams(sem):
    return pltpu.CompilerParams(dimension_semantics=sem, vmem_limit_bytes=VMEM_LIMIT_BYTES)


def _pick(n, cands=(512, 384, 256, 128)):
    for c in cands:
        if n % c == 0:
            return c
    return n


def _pick_rows(r, cap=512, mult=16):
    return max(t for t in range(mult, cap + 1, mult) if r % t == 0)


def _dot(a, b, dn):
    return lax.dot_general(a.astype(BF16), b.astype(BF16), dn, preferred_element_type=F32)


def _sigmoid(x):
    return 0.5 + 0.5 * jnp.tanh(0.5 * x)


def _gelu(x):
    c = math.sqrt(2.0 / math.pi)
    t = jnp.tanh(c * (x + 0.044715 * x * x * x))
    return 0.5 * x * (1.0 + t)


def _gelu_grad(x):
    c = math.sqrt(2.0 / math.pi)
    t = jnp.tanh(c * (x + 0.044715 * x * x * x))
    return 0.5 * (1.0 + t) + 0.5 * x * (1.0 - t * t) * c * (1.0 + 3.0 * 0.044715 * x * x)


def _gelu_both(x):
    c = math.sqrt(2.0 / math.pi)
    t = jnp.tanh(c * (x + 0.044715 * x * x * x))
    return 0.5 * x * (1.0 + t), 0.5 * (1.0 + t) + 0.5 * x * (1.0 - t * t) * c * (1.0 + 3.0 * 0.044715 * x * x)


def _silu_both(z):
    s = _sigmoid(z)
    return z * s, s * (1.0 + z * (1.0 - s))


def _silu(z):
    return z * _sigmoid(z)


def matmul(a, b, mode, name, out_dtype=F32, add=None):
    if mode == "nn":
        (m, k), n = a.shape, b.shape[1]
    elif mode == "nt":
        (m, k), n = a.shape, b.shape[0]
    else:
        (k, m), n = a.shape, b.shape[1]
    tm = _pick(m, [t for t in (1024, 512, 384, 256, 128) if t * k * a.dtype.itemsize <= MM_BLOCK_BYTES])
    tn = _pick(n, [t for t in (512, 384, 256, 128) if t * k * b.dtype.itemsize <= MM_BLOCK_BYTES])
    dn = {"nn": DN_NN, "nt": DN_NT, "tn": DN_TN}[mode]

    def body(*refs):
        if add is None:
            a_ref, b_ref, o_ref = refs
        else:
            a_ref, b_ref, add_ref, o_ref = refs
        r = _dot(a_ref[...], b_ref[...], dn)
        if add is not None:
            r = r + add_ref[...].astype(F32)
        o_ref[...] = r.astype(out_dtype)

    a_spec = pl.BlockSpec((k, tm), lambda i, j: (0, i)) if mode == "tn" else pl.BlockSpec((tm, k), lambda i, j: (i, 0))
    b_spec = pl.BlockSpec((tn, k), lambda i, j: (j, 0)) if mode == "nt" else pl.BlockSpec((k, tn), lambda i, j: (0, j))
    o_spec = pl.BlockSpec((tm, tn), lambda i, j: (i, j))
    in_specs = [a_spec, b_spec] + ([o_spec] if add is not None else [])
    args = (a, b) + ((add,) if add is not None else ())
    return pl.pallas_call(
        body, name=name, grid=(m // tm, n // tn), in_specs=in_specs, out_specs=o_spec,
        out_shape=jax.ShapeDtypeStruct((m, n), out_dtype),
        compiler_params=_cparams(("parallel", "arbitrary")))(*args)


def _tile_rows(r, r0, cands=(512, 384, 256, 128)):
    return next(t for t in cands if r % t == 0 and r0 % t == 0)


def matmul_tn_packed(a, b, buf, r0, col_sharded, name):
    k, m = a.shape
    n = b.shape[1]
    if col_sharded:
        chunks = n // N_CHIPS // CHUNK_W
        tm = _tile_rows(m, r0, (1024, 512, 384, 256, 128))
        o_map = lambda i, j: (j // chunks, (r0 + (j % chunks) * m) // tm + i, 0)
    else:
        rs = m // N_CHIPS
        tm = _tile_rows(rs, r0)
        per = rs // tm
        o_map = lambda i, j: (i // per, (r0 + j * rs) // tm + i % per, 0)

    def body(a_ref, b_ref, buf_ref, o_ref):
        o_ref[0] = _dot(a_ref[...], b_ref[...], DN_TN)

    return pl.pallas_call(
        body, name=name, grid=(m // tm, n // CHUNK_W),
        in_specs=[pl.BlockSpec((k, tm), lambda i, j: (0, i)), pl.BlockSpec((k, CHUNK_W), lambda i, j: (0, j)),
                  pl.BlockSpec(memory_space=pl.ANY)],
        out_specs=pl.BlockSpec((1, tm, CHUNK_W), o_map), out_shape=jax.ShapeDtypeStruct(buf.shape, buf.dtype),
        input_output_aliases={2: 0}, compiler_params=_cparams(("parallel", "arbitrary")))(a, b, buf)


def _rows(tl, w, col=0):
    return pl.BlockSpec((tl, w), lambda i: (i, col))


def _full(shape):
    nd = len(shape)
    return pl.BlockSpec(tuple(shape), lambda i: (0,) * nd)


def _rowcall(body, name, n_steps, in_specs, out_specs, out_shape, scratch=()):
    return pl.pallas_call(
        body, name=name, grid=(n_steps,), in_specs=in_specs, out_specs=out_specs, out_shape=out_shape,
        scratch_shapes=list(scratch), compiler_params=_cparams(("arbitrary",)))


def _acc(ref, val, i):
    @pl.when(i == 0)
    def _():
        ref[...] = val

    @pl.when(i != 0)
    def _():
        ref[...] += val


def rms_fwd(h, g, name):
    l, d = h.shape
    tl = ROW_TILE

    def body(h_ref, g_ref, o_ref):
        x = h_ref[...]
        r = lax.rsqrt(jnp.mean(x * x, axis=-1, keepdims=True) + NORM_EPS)
        o_ref[...] = (x * r * g_ref[...]).astype(BF16)

    return _rowcall(body, name, l // tl, [_rows(tl, d), _full((1, d))], _rows(tl, d),
                    jax.ShapeDtypeStruct((l, d), BF16))(h, g.reshape(1, d))


def rms_bwd(h, g, dhn, dh_in, name):
    l, d = h.shape
    tl = ROW_TILE

    def body(h_ref, g_ref, dhn_ref, dhi_ref, dh_ref, dg_ref):
        i = pl.program_id(0)
        x = h_ref[...]
        r = lax.rsqrt(jnp.mean(x * x, axis=-1, keepdims=True) + NORM_EPS)
        xhat = x * r
        dy = dhn_ref[...]
        dxh = dy * g_ref[...]
        dx = r * (dxh - xhat * jnp.mean(dxh * xhat, axis=-1, keepdims=True))
        dh_ref[...] = dhi_ref[...] + dx
        _acc(dg_ref, jnp.sum(dy * xhat, axis=0, keepdims=True), i)

    return _rowcall(body, name, l // tl, [_rows(tl, d), _full((1, d)), _rows(tl, d), _rows(tl, d)],
                    [_rows(tl, d), _full((1, d))],
                    [jax.ShapeDtypeStruct((l, d), F32), jax.ShapeDtypeStruct((1, d), F32)])(h, g.reshape(1, d), dhn, dh_in)


def loss_head(h, g, target):
    l, d = h.shape
    tl = ROW_TILE

    def body(h_ref, g_ref, t_ref, loss_ref, dh_ref, dg_ref):
        i = pl.program_id(0)
        x = h_ref[...]
        gg = g_ref[...]
        r = lax.rsqrt(jnp.mean(x * x, axis=-1, keepdims=True) + NORM_EPS)
        xhat = x * r
        err = xhat * gg - t_ref[...]
        part = 0.5 * jnp.sum(jnp.mean(err * err, axis=-1, keepdims=True), axis=0, keepdims=True)
        _acc(loss_ref, part, i)
        dy = err * (1.0 / d)
        dxh = dy * gg
        dh_ref[...] = r * (dxh - xhat * jnp.mean(dxh * xhat, axis=-1, keepdims=True))
        _acc(dg_ref, jnp.sum(dy * xhat, axis=0, keepdims=True), i)

    return _rowcall(body, "loss_head", l // tl, [_rows(tl, d), _full((1, d)), _rows(tl, d)],
                    [_full((1, 1)), _rows(tl, d), _full((1, d))],
                    [jax.ShapeDtypeStruct((1, 1), F32), jax.ShapeDtypeStruct((l, d), F32),
                     jax.ShapeDtypeStruct((1, d), F32)])(h, g.reshape(1, d), target)


def _gmlp_common(a_ref, lng_ref, lnb_ref):
    di = lng_ref.shape[1]
    u_pre = a_ref[:, 0:di]
    v_pre = a_ref[:, di:2 * di]
    z = a_ref[:, 2 * di:3 * di]
    vg = _gelu(v_pre)
    mu = jnp.mean(vg, axis=-1, keepdims=True)
    xc = vg - mu
    rstd = lax.rsqrt(jnp.mean(xc * xc, axis=-1, keepdims=True) + NORM_EPS)
    vhat = xc * rstd
    vn = vhat * lng_ref[...] + lnb_ref[...]
    return u_pre, v_pre, z, vhat, rstd, vn


def _tril(w):
    r = lax.broadcasted_iota(jnp.int32, w.shape, 0)
    c = lax.broadcasted_iota(jnp.int32, w.shape, 1)
    return jnp.where(c <= r, w, 0.0)


def gmlp_gate_fwd(a, ln_g, ln_b, w_s, b_s, name):
    l, w3 = a.shape
    di = w3 // 3
    dg = di // GMLP_GROUPS
    tl = GMLP_CHUNK

    def body(a_ref, lng_ref, lnb_ref, ws_ref, bs_ref, m_ref):
        u_pre, _, z, _, _, vn = _gmlp_common(a_ref, lng_ref, lnb_ref)
        gate = _gelu(u_pre) * _silu(z)
        for g in range(GMLP_GROUPS):
            sl = slice(g * dg, (g + 1) * dg)
            s = _dot(_tril(ws_ref[g]), vn[:, sl], DN_NN) + bs_ref[g]
            m_ref[:, sl] = (gate[:, sl] * s).astype(BF16)

    return _rowcall(body, name, l // tl,
                    [_rows(tl, w3), _full((1, di)), _full((1, di)), _full(w_s.shape), _full((GMLP_GROUPS, tl, 1))],
                    _rows(tl, di), jax.ShapeDtypeStruct((l, di), BF16))(
        a, ln_g.reshape(1, di), ln_b.reshape(1, di), w_s, b_s.reshape(GMLP_GROUPS, tl, 1))


def gmlp_gate_bwd(a, dm, ln_g, ln_b, w_s, b_s, name):
    l, w3 = a.shape
    di = w3 // 3
    dg = di // GMLP_GROUPS
    tl = GMLP_CHUNK

    def body(a_ref, dm_ref, lng_ref, lnb_ref, ws_ref, bs_ref, da_ref, dlg_ref, dlb_ref, dws_ref, dbs_ref,
             dvn_ref, vh_ref, gv_ref):
        i = pl.program_id(0)
        vg, gv = _gelu_both(a_ref[:, di:2 * di])
        gv_ref[...] = gv
        xc = vg - jnp.mean(vg, axis=-1, keepdims=True)
        rstd = lax.rsqrt(jnp.mean(xc * xc, axis=-1, keepdims=True) + NORM_EPS)
        vh_ref[...] = xc * rstd
        for g in range(GMLP_GROUPS):
            sl = slice(g * dg, (g + 1) * dg)
            wt = _tril(ws_ref[g])
            vn_g = vh_ref[:, sl] * lng_ref[:, sl] + lnb_ref[:, sl]
            s = _dot(wt, vn_g, DN_NN) + bs_ref[g]
            dmg = dm_ref[:, sl]
            u, gu = _gelu_both(a_ref[:, sl])
            sz, gz = _silu_both(a_ref[:, 2 * di + g * dg:2 * di + (g + 1) * dg])
            ds = dmg * u * sz
            da_ref[:, sl] = (dmg * s * sz * gu).astype(BF16)
            da_ref[:, 2 * di + g * dg:2 * di + (g + 1) * dg] = (dmg * u * s * gz).astype(BF16)
            dvn_ref[:, sl] = _dot(wt, ds, DN_TN)
            dw = _tril(_dot(ds, vn_g, DN_NT))
            db = jnp.sum(ds, axis=1, keepdims=True)

            @pl.when(i == 0)
            def _():
                dws_ref[g] = dw
                dbs_ref[g] = db

            @pl.when(i != 0)
            def _():
                dws_ref[g] += dw
                dbs_ref[g] += db

        dvn = dvn_ref[...]
        vhat = vh_ref[...]
        dxh = dvn * lng_ref[...]
        dvg = rstd * (dxh - jnp.mean(dxh, axis=-1, keepdims=True) - vhat * jnp.mean(dxh * vhat, axis=-1, keepdims=True))
        da_ref[:, di:2 * di] = (dvg * gv_ref[...]).astype(BF16)
        _acc(dlg_ref, jnp.sum(dvn * vhat, axis=0, keepdims=True), i)
        _acc(dlb_ref, jnp.sum(dvn, axis=0, keepdims=True), i)

    outs = _rowcall(
        body, name, l // tl,
        [_rows(tl, w3), _rows(tl, di), _full((1, di)), _full((1, di)), _full(w_s.shape), _full((GMLP_GROUPS, tl, 1))],
        [_rows(tl, w3), _full((1, di)), _full((1, di)), _full(w_s.shape), _full((GMLP_GROUPS, tl, 1))],
        [jax.ShapeDtypeStruct((l, w3), BF16), jax.ShapeDtypeStruct((1, di), F32), jax.ShapeDtypeStruct((1, di), F32),
         jax.ShapeDtypeStruct(w_s.shape, F32), jax.ShapeDtypeStruct((GMLP_GROUPS, tl, 1), F32)],
        scratch=[pltpu.VMEM((tl, di), F32)] * 3)(
        a, dm, ln_g.reshape(1, di), ln_b.reshape(1, di), w_s, b_s.reshape(GMLP_GROUPS, tl, 1))
    return outs


def gmlp_layer_fwd(h, p, wf, tag):
    hn = rms_fwd(h, p["norm_g"], tag + "_rms")
    a = matmul(hn, wf["w_in"], "nn", tag + "_mm_in")
    m = gmlp_gate_fwd(a, p["ln_g"], p["ln_b"], p["w_s"], p["b_s"], tag + "_gate")
    h_out = matmul(m, wf["w_out"], "nn", tag + "_mm_out", add=h)
    return h_out, (h, hn, a, m)


def gmlp_layer_bwd(dh_out, saved, p, wf, tag, sink):
    h, hn, a, m = saved
    dm = matmul(dh_out, wf["w_out"], "nt", tag + "_mm_dm")
    sink.mm("w_out", m, dh_out, tag + "_mm_gwout")
    da, dlg, dlb, dws, dbs = gmlp_gate_bwd(a, dm, p["ln_g"], p["ln_b"], p["w_s"], p["b_s"], tag + "_gate_bwd")
    dhn = matmul(da, wf["w_in"], "nt", tag + "_mm_dhn")
    sink.mm("w_in", hn, da, tag + "_mm_gwin")
    zero = sink.send()
    dh, dng = rms_bwd(h, p["norm_g"] + zero, dhn, dh_out, tag + "_rms_bwd")
    grads = {"norm_g": dng.reshape(-1), "ln_g": dlg.reshape(-1), "ln_b": dlb.reshape(-1),
             "w_s": dws, "b_s": dbs.reshape(GMLP_GROUPS, GMLP_CHUNK)}
    return dh, grads


def _cmul(ar, ai, br, bi):
    return ar * br - ai * bi, ar * bi + ai * br


S5_PG = 16


def _gblock(tail):
    return pl.BlockSpec((S5_PG,) + tuple(tail), lambda i: (i, 0, 0))


def s5_params_fwd(a_re, a_im, log_step, b_re, b_im):
    g, p, hh = b_re.shape

    def body(ar_ref, ai_ref, ls_ref, br_ref, bi_ref, lr_ref, li_ref, bbr_ref, bbi_ref):
        ar, ai = ar_ref[...], ai_ref[...]
        step = jnp.exp(ls_ref[...])
        mag = jnp.exp(ar * step)
        lr, li = mag * jnp.cos(ai * step), mag * jnp.sin(ai * step)
        den = 1.0 / (ar * ar + ai * ai)
        fr, fi = _cmul(lr - 1.0, li, ar * den, -ai * den)
        lr_ref[...] = lr
        li_ref[...] = li
        bbr, bbi = _cmul(fr, fi, br_ref[...], bi_ref[...])
        bbr_ref[...] = bbr
        bbi_ref[...] = bbi

    s1 = jax.ShapeDtypeStruct((g, p, 1), F32)
    s3 = jax.ShapeDtypeStruct((g, p, hh), F32)
    b1, b0, b3 = _gblock((p, 1)), _gblock((1, 1)), _gblock((p, hh))
    return pl.pallas_call(body, name="s5_params_fwd", grid=(g // S5_PG,), in_specs=[b1, b1, b0, b3, b3],
                          out_specs=[b1, b1, b3, b3], out_shape=[s1, s1, s3, s3],
                          compiler_params=_cparams(("parallel",)))(
        a_re.reshape(g, p, 1), a_im.reshape(g, p, 1), log_step.reshape(g, 1, 1), b_re, b_im)


def s5_params_bwd(a_re, a_im, log_step, b_re, b_im, dl_re, dl_im, dbb_re, dbb_im):
    g, p, hh = b_re.shape

    def body(ar_ref, ai_ref, ls_ref, br_ref, bi_ref, dlr_ref, dli_ref, dbr_ref, dbi_ref,
             gar_ref, gai_ref, gls_ref, gbr_ref, gbi_ref):
        ar, ai = ar_ref[...], ai_ref[...]
        step = jnp.exp(ls_ref[...])
        mag = jnp.exp(ar * step)
        lr, li = mag * jnp.cos(ai * step), mag * jnp.sin(ai * step)
        den = 1.0 / (ar * ar + ai * ai)
        ir, ii = ar * den, -ai * den
        fr, fi = _cmul(lr - 1.0, li, ir, ii)
        br, bi = br_ref[...], bi_ref[...]
        dbr, dbi = dbr_ref[...], dbi_ref[...]
        gbr, gbi = _cmul(fr, -fi, dbr, dbi)
        gbr_ref[...] = gbr
        gbi_ref[...] = gbi
        pr, pi = _cmul(br, -bi, dbr, dbi)
        gfr = jnp.sum(pr, axis=-1, keepdims=True)
        gfi = jnp.sum(pi, axis=-1, keepdims=True)
        t_r, t_i = _cmul(ir, -ii, gfr, gfi)
        glr, gli = dlr_ref[...] + t_r, dli_ref[...] + t_i
        c1r, c1i = _cmul(step * lr, -step * li, glr, gli)
        qr, qi = _cmul(fr, fi, ir, ii)
        c2r, c2i = _cmul(-qr, qi, gfr, gfi)
        gar_ref[...] = c1r + c2r
        gai_ref[...] = c1i + c2i
        wr, wi = _cmul(ar, ai, lr, li)
        sr, _ = _cmul(wr, -wi, glr, gli)
        gls_ref[...] = jnp.sum(sr, axis=1, keepdims=True) * step

    s1 = jax.ShapeDtypeStruct((g, p, 1), F32)
    s3 = jax.ShapeDtypeStruct((g, p, hh), F32)
    b1, b0, b3 = _gblock((p, 1)), _gblock((1, 1)), _gblock((p, hh))
    return pl.pallas_call(body, name="s5_params_bwd", grid=(g // S5_PG,),
                          in_specs=[b1, b1, b0, b3, b3, b1, b1, b3, b3], out_specs=[b1, b1, b0, b3, b3],
                          out_shape=[s1, s1, jax.ShapeDtypeStruct((g, 1, 1), F32), s3, s3],
                          compiler_params=_cparams(("parallel",)))(
        a_re.reshape(g, p, 1), a_im.reshape(g, p, 1), log_step.reshape(g, 1, 1), b_re, b_im,
        dl_re, dl_im, dbb_re, dbb_im)


def _blockdiag(t):
    sb, n, r, c = t.shape
    eye = jnp.eye(n, dtype=bool)[None, :, None, :, None]
    full = jnp.where(eye, t[:, :, :, None, :], jnp.zeros((), t.dtype))
    return full.reshape(sb, n * r, n * c)


def _blockdiag_extract(m, r, c):
    sb = m.shape[0]
    n = m.shape[1] // r
    m5 = m.reshape(sb, n, r, n, c)
    return jnp.stack([m5[:, i, :, i, :] for i in range(n)], axis=1)


S5_TB = 64
S5_UNROLL = 8


def _lam_power(pr, pi, n):
    for _ in range(int(math.log2(n))):
        pr, pi = _cmul(pr, pi, pr, pi)
    return pr, pi


def _segment_entries(er, ei, pr, pi, reverse):
    seg, ns = er.shape
    row = lax.broadcasted_iota(jnp.int32, (seg, ns), 0)
    cr = jnp.zeros((seg, ns), F32)
    ci = jnp.zeros((seg, ns), F32)
    cur_r = jnp.zeros((1, ns), F32)
    cur_i = jnp.zeros((1, ns), F32)
    for s in (range(seg - 2, -1, -1) if reverse else range(1, seg)):
        src = s + 1 if reverse else s - 1
        mr, mi = _cmul(pr, pi, cur_r, cur_i)
        cur_r = jnp.sum(jnp.where(row == src, er, 0.0), axis=0, keepdims=True) + mr
        cur_i = jnp.sum(jnp.where(row == src, ei, 0.0), axis=0, keepdims=True) + mi
        cr = jnp.where(row == s, cur_r, cr)
        ci = jnp.where(row == s, cur_i, ci)
    return cr, ci


def s5_scan_fused_fwd(a_p, lam_re, lam_im, wb_re, wb_im, wc_re, wc_im, d_skip, name):
    l = a_p.shape[0]
    di = d_skip.shape[1]
    rows = S5_SEG * S5_TB
    nb = l // rows
    ns = wb_re.shape[2]

    def body(u_ref, lr_ref, li_ref, wbr_ref, wbi_ref, wcr_ref, wci_ref, ds_ref, y_ref, ckr_ref, cki_ref, bur, bui):
        lr = jnp.broadcast_to(lr_ref[0], (S5_SEG, ns))
        li = jnp.broadcast_to(li_ref[0], (S5_SEG, ns))

        def scan_block(b, carry, keep):
            def step(t, c):
                xr, xi = c
                sl = pl.ds(pl.multiple_of(b * rows + t * S5_SEG, S5_SEG), S5_SEG)
                nr = lr * xr - li * xi + bur[sl, :]
                ni = lr * xi + li * xr + bui[sl, :]
                if keep:
                    bur[sl, :] = nr
                    bui[sl, :] = ni
                return nr, ni

            return lax.fori_loop(0, S5_TB, step, carry, unroll=S5_UNROLL)

        def project(b, carry):
            rs = pl.ds(pl.multiple_of(b * rows, rows), rows)
            u = u_ref[rs, :]
            bur[rs, :] = _dot(u, wbr_ref[0], DN_NN)
            bui[rs, :] = _dot(u, wbi_ref[0], DN_NN)
            return scan_block(b, carry, False)

        zero = jnp.zeros((S5_SEG, ns), F32)
        er, ei = lax.fori_loop(0, nb, project, (zero, zero))
        pr, pi = _lam_power(lr_ref[0], li_ref[0], l // S5_SEG)
        entry = _segment_entries(er, ei, pr, pi, False)

        def emit(b, carry):
            ckr_ref[0, b] = carry[0]
            cki_ref[0, b] = carry[1]
            carry = scan_block(b, carry, True)
            rs = pl.ds(pl.multiple_of(b * rows, rows), rows)
            y_ref[rs, :] = (_dot(bur[rs, :], wcr_ref[0], DN_NN) - _dot(bui[rs, :], wci_ref[0], DN_NN)
                            + ds_ref[...] * u_ref[rs, :])
            return carry

        lax.fori_loop(0, nb, emit, entry)

    sb3 = lambda s: (s, 0, 0)
    st = jax.ShapeDtypeStruct
    return pl.pallas_call(
        body, name=name, grid=(S5_SB,),
        in_specs=[pl.BlockSpec((l, LANES), lambda s: (0, s)),
                  pl.BlockSpec((1, 1, ns), sb3), pl.BlockSpec((1, 1, ns), sb3),
                  pl.BlockSpec((1, LANES, ns), sb3), pl.BlockSpec((1, LANES, ns), sb3),
                  pl.BlockSpec((1, ns, LANES), sb3), pl.BlockSpec((1, ns, LANES), sb3),
                  pl.BlockSpec((1, LANES), lambda s: (0, s))],
        out_specs=[pl.BlockSpec((l, LANES), lambda s: (0, s)),
                   pl.BlockSpec((1, nb, S5_SEG, ns), lambda s: (s, 0, 0, 0)),
                   pl.BlockSpec((1, nb, S5_SEG, ns), lambda s: (s, 0, 0, 0))],
        out_shape=[st((l, di), F32), st((S5_SB, nb, S5_SEG, ns), F32), st((S5_SB, nb, S5_SEG, ns), F32)],
        scratch_shapes=[pltpu.VMEM((l, ns), F32), pltpu.VMEM((l, ns), F32)],
        compiler_params=_cparams(("parallel",)))(a_p, lam_re, lam_im, wb_re, wb_im, wc_re, wc_im, d_skip)


def s5_scan_fused_bwd(a_p, dy, lam_re, lam_im, wb_re, wb_im, wc_re, wc_im, d_skip, ck_re, ck_im, name):
    l = a_p.shape[0]
    di = d_skip.shape[1]
    rows = S5_SEG * S5_TB
    nb = l // rows
    ns = wb_re.shape[2]

    def body(u_ref, dy_ref, lr_ref, li_ref, wbr_ref, wbi_ref, wcr_ref, wci_ref, ds_ref, ckr_ref, cki_ref,
             du_ref, dwbr_ref, dwbi_ref, dwcr_ref, dwci_ref, dds_ref, dlr_ref, dli_ref, gr, gi, xr_b, xi_b):
        lr = jnp.broadcast_to(lr_ref[0], (S5_SEG, ns))
        li = jnp.broadcast_to(li_ref[0], (S5_SEG, ns))

        def back_project(k, carry):
            b = nb - 1 - k
            rs = pl.ds(pl.multiple_of(b * rows, rows), rows)
            dyv = dy_ref[rs, :]
            gr[rs, :] = _dot(dyv, wcr_ref[0], DN_NT)
            gi[rs, :] = -_dot(dyv, wci_ref[0], DN_NT)

            def step(kk, c):
                ar, ai = c
                sl = pl.ds(pl.multiple_of(b * rows + (S5_TB - 1 - kk) * S5_SEG, S5_SEG), S5_SEG)
                return gr[sl, :] + lr * ar + li * ai, gi[sl, :] + lr * ai - li * ar

            return lax.fori_loop(0, S5_TB, step, carry, unroll=S5_UNROLL)

        zero = jnp.zeros((S5_SEG, ns), F32)
        er, ei = lax.fori_loop(0, nb, back_project, (zero, zero))
        pr, pi = _lam_power(lr_ref[0], -li_ref[0], l // S5_SEG)
        a0r, a0i = _segment_entries(er, ei, pr, pi, True)

        dwbr_ref[...] = jnp.zeros_like(dwbr_ref)
        dwbi_ref[...] = jnp.zeros_like(dwbi_ref)
        dwcr_ref[...] = jnp.zeros_like(dwcr_ref)
        dwci_ref[...] = jnp.zeros_like(dwci_ref)
        dds_ref[...] = jnp.zeros_like(dds_ref)

        def block(k, carry):
            b = nb - 1 - k
            rs = pl.ds(pl.multiple_of(b * rows, rows), rows)
            u = u_ref[rs, :]
            dyv = dy_ref[rs, :]
            body_rows = pl.ds(S5_SEG, rows)
            x0r, x0i = ckr_ref[0, b], cki_ref[0, b]
            xr_b[0:S5_SEG, :] = x0r
            xi_b[0:S5_SEG, :] = x0i
            xr_b[body_rows, :] = _dot(u, wbr_ref[0], DN_NN)
            xi_b[body_rows, :] = _dot(u, wbi_ref[0], DN_NN)

            def fstep(t, c):
                xr, xi = c
                sl = pl.ds(pl.multiple_of((t + 1) * S5_SEG, S5_SEG), S5_SEG)
                nr = lr * xr - li * xi + xr_b[sl, :]
                ni = lr * xi + li * xr + xi_b[sl, :]
                xr_b[sl, :] = nr
                xi_b[sl, :] = ni
                return nr, ni

            lax.fori_loop(0, S5_TB, fstep, (x0r, x0i), unroll=S5_UNROLL)
            dwcr_ref[0] += _dot(xr_b[body_rows, :], dyv, DN_TN)
            dwci_ref[0] -= _dot(xi_b[body_rows, :], dyv, DN_TN)

            def bstep(kk, c):
                ar, ai = c
                sl = pl.ds(pl.multiple_of(b * rows + (S5_TB - 1 - kk) * S5_SEG, S5_SEG), S5_SEG)
                nr = gr[sl, :] + lr * ar + li * ai
                ni = gi[sl, :] + lr * ai - li * ar
                gr[sl, :] = nr
                gi[sl, :] = ni
                return nr, ni

            ar, ai = lax.fori_loop(0, S5_TB, bstep, carry[:2], unroll=S5_UNROLL)
            a_r, a_i = gr[rs, :], gi[rs, :]
            p_r, p_i = xr_b[0:rows, :], xi_b[0:rows, :]
            per_seg = lambda v: jnp.sum(v.reshape(S5_TB, S5_SEG, ns), axis=0)
            carry = (ar, ai, carry[2] + per_seg(a_r * p_r + a_i * p_i), carry[3] + per_seg(a_i * p_r - a_r * p_i))
            du_ref[rs, :] = (_dot(a_r, wbr_ref[0], DN_NT) + _dot(a_i, wbi_ref[0], DN_NT) + ds_ref[...] * dyv).astype(BF16)
            dwbr_ref[0] += _dot(u, a_r, DN_TN)
            dwbi_ref[0] += _dot(u, a_i, DN_TN)
            dds_ref[...] += jnp.sum(dyv * u, axis=0, keepdims=True)
            return carry

        _, _, dlr, dli = lax.fori_loop(0, nb, block, (a0r, a0i, zero, zero))
        dlr_ref[0] = dlr
        dli_ref[0] = dli

    sb3 = lambda s: (s, 0, 0)
    seq = pl.BlockSpec((l, LANES), lambda s: (0, s))
    ck = pl.BlockSpec((1, nb, S5_SEG, ns), lambda s: (s, 0, 0, 0))
    st = jax.ShapeDtypeStruct
    return pl.pallas_call(
        body, name=name, grid=(S5_SB,),
        in_specs=[seq, seq, pl.BlockSpec((1, 1, ns), sb3), pl.BlockSpec((1, 1, ns), sb3),
                  pl.BlockSpec((1, LANES, ns), sb3), pl.BlockSpec((1, LANES, ns), sb3),
                  pl.BlockSpec((1, ns, LANES), sb3), pl.BlockSpec((1, ns, LANES), sb3),
                  pl.BlockSpec((1, LANES), lambda s: (0, s)), ck, ck],
        out_specs=[seq, pl.BlockSpec((1, LANES, ns), sb3), pl.BlockSpec((1, LANES, ns), sb3),
                   pl.BlockSpec((1, ns, LANES), sb3), pl.BlockSpec((1, ns, LANES), sb3),
                   pl.BlockSpec((1, LANES), lambda s: (0, s)),
                   pl.BlockSpec((1, S5_SEG, ns), sb3), pl.BlockSpec((1, S5_SEG, ns), sb3)],
        out_shape=[st((l, di), BF16), st((S5_SB, LANES, ns), F32), st((S5_SB, LANES, ns), F32),
                   st((S5_SB, ns, LANES), F32), st((S5_SB, ns, LANES), F32), st((1, di), F32),
                   st((S5_SB, S5_SEG, ns), F32), st((S5_SB, S5_SEG, ns), F32)],
        scratch_shapes=[pltpu.VMEM((l, ns), F32), pltpu.VMEM((l, ns), F32),
                        pltpu.VMEM((rows + S5_SEG, ns), F32), pltpu.VMEM((rows + S5_SEG, ns), F32)],
        compiler_params=_cparams(("parallel",)))(
        a_p, dy, lam_re, lam_im, wb_re, wb_im, wc_re, wc_im, d_skip, ck_re, ck_im)


def s5_act(y, name):
    l, d = y.shape
    tl = ROW_TILE

    def body(y_ref, o_ref):
        o_ref[...] = _gelu(y_ref[...]).astype(BF16)

    return _rowcall(body, name, l // tl, [_rows(tl, d)], _rows(tl, d), jax.ShapeDtypeStruct((l, d), BF16))(y)


def s5_gate_fwd(y, t, b_glu, a_p, name):
    l, d = y.shape
    tl = ROW_TILE

    def body(y_ref, t_ref, b_ref, z_ref, m_ref):
        yg = _gelu(y_ref[...])
        m_ref[...] = (yg * _sigmoid(t_ref[...] + b_ref[...]) * _silu(z_ref[...])).astype(BF16)

    return _rowcall(body, name, l // tl, [_rows(tl, d), _rows(tl, d), _full((1, d)), _rows(tl, d, 1)], _rows(tl, d),
                    jax.ShapeDtypeStruct((l, d), BF16))(y, t, b_glu.reshape(1, d), a_p)


def s5_gate_bwd(dm, y, t, b_glu, a_p, name):
    l, d = y.shape
    tl = ROW_TILE

    def body(dm_ref, y_ref, t_ref, b_ref, z_ref, dt_ref, dyg_ref, dz_ref, db_ref):
        i = pl.program_id(0)
        dmv = dm_ref[...]
        z = z_ref[...]
        yg = _gelu(y_ref[...])
        sg = _sigmoid(t_ref[...] + b_ref[...])
        y2 = yg * sg
        sz, gz = _silu_both(z)
        dy2 = dmv * sz
        dz_ref[...] = (dmv * y2 * gz).astype(BF16)
        dyg_ref[...] = dy2 * sg
        dt = dy2 * yg * sg * (1.0 - sg)
        dt_ref[...] = dt.astype(BF16)
        _acc(db_ref, jnp.sum(dt, axis=0, keepdims=True), i)

    st = jax.ShapeDtypeStruct
    return _rowcall(body, name, l // tl, [_rows(tl, d), _rows(tl, d), _rows(tl, d), _full((1, d)), _rows(tl, d, 1)],
                    [_rows(tl, d), _rows(tl, d), _rows(tl, d), _full((1, d))],
                    [st((l, d), BF16), st((l, d), F32), st((l, d), BF16), st((1, d), F32)])(
        dm, y, t, b_glu.reshape(1, d), a_p)


def s5_act_bwd(y, dyg_a, dyg_b, name):
    l, d = y.shape
    tl = ROW_TILE

    def body(y_ref, a_ref, b_ref, o_ref):
        o_ref[...] = (a_ref[...] + b_ref[...]) * _gelu_grad(y_ref[...])

    return _rowcall(body, name, l // tl, [_rows(tl, d)] * 3, _rows(tl, d), jax.ShapeDtypeStruct((l, d), F32))(y, dyg_a, dyg_b)


def _seg_perm(t):
    l, d = t.shape
    return t.reshape(S5_SEG, l // S5_SEG, d).transpose(1, 0, 2).reshape(l, d)


def _seg_unperm(t):
    l, d = t.shape
    return t.reshape(l // S5_SEG, S5_SEG, d).transpose(1, 0, 2).reshape(l, d)


def _s5_weights(p):
    lr, li, bbr, bbi = s5_params_fwd(p["a_re"], p["a_im"], p["log_step"], p["b_re"], p["b_im"])
    ns = 8 * S5_STATE
    lam_re = lr.reshape(S5_SB, 1, ns)
    lam_im = li.reshape(S5_SB, 1, ns)
    to_bd = lambda t: _blockdiag(t.reshape(S5_SB, 8, t.shape[1], t.shape[2]))
    wb_re = to_bd(bbr.transpose(0, 2, 1)).astype(BF16)
    wb_im = to_bd(bbi.transpose(0, 2, 1)).astype(BF16)
    wc_re = to_bd(p["c_re"].transpose(0, 2, 1)).astype(BF16)
    wc_im = to_bd(p["c_im"].transpose(0, 2, 1)).astype(BF16)
    return lam_re, lam_im, wb_re, wb_im, wc_re, wc_im


def s5_layer_fwd(h, p, wf, sw, tag):
    l = h.shape[0]
    di = p["d_skip"].shape[0]
    hn = rms_fwd(h, p["norm_g"], tag + "_rms")
    hn_p = _seg_perm(hn)
    a_p = matmul(hn_p, wf["w_in"], "nn", tag + "_mm_in")
    dsk = p["d_skip"].reshape(1, di)
    y, ck_re, ck_im = s5_scan_fused_fwd(a_p, *sw, dsk, tag + "_scan")
    yg = s5_act(y, tag + "_act")
    t = matmul(yg, wf["w_glu"], "nn", tag + "_mm_glu")
    m = s5_gate_fwd(y, t, p["b_glu"], a_p, tag + "_gate")
    out_p = matmul(m, wf["w_out"], "nn", tag + "_mm_out")
    h_out = residual_add(h, _seg_unperm(out_p), tag + "_res")
    return h_out, (h, hn_p, a_p, sw, ck_re, ck_im, y, yg, t, m)


def residual_add(h, y, name):
    l, d = h.shape
    tl = ROW_TILE

    def body(h_ref, y_ref, o_ref):
        o_ref[...] = h_ref[...] + y_ref[...]

    return _rowcall(body, name, l // tl, [_rows(tl, d)] * 2, _rows(tl, d), jax.ShapeDtypeStruct((l, d), F32))(h, y)


def s5_layer_bwd(dh_out, saved, p, wf, tag, sink):
    h, hn_p, a_p, sw, ck_re, ck_im, y, yg, t, m = saved
    l = h.shape[0]
    di = p["d_skip"].shape[0]
    dsk = p["d_skip"].reshape(1, di)
    dout_p = _seg_perm(dh_out)
    dm = matmul(dout_p, wf["w_out"], "nt", tag + "_mm_dm")
    sink.mm("w_out", m, dout_p, tag + "_mm_gwout")
    dt, dyg_a, dz, db_glu = s5_gate_bwd(dm, y, t, p["b_glu"], a_p, tag + "_gate_bwd")
    dyg_b = matmul(dt, wf["w_glu"], "nt", tag + "_mm_dyg")
    sink.mm("w_glu", yg, dt, tag + "_mm_gwglu")
    dy = s5_act_bwd(y, dyg_a, dyg_b, tag + "_act_bwd")
    du, dwbr, dwbi, dwcr, dwci, dds, dlr, dli = s5_scan_fused_bwd(a_p, dy, *sw, dsk, ck_re, ck_im, tag + "_scanb")
    da = jnp.concatenate([du, dz], axis=1)
    dhn_p = matmul(da, wf["w_in"], "nt", tag + "_mm_dhn")
    sink.mm("w_in", hn_p, da, tag + "_mm_gwin")
    zero = sink.send()
    dh, dng = rms_bwd(h, p["norm_g"] + zero, _seg_unperm(dhn_p), dh_out, tag + "_rms_bwd")
    ex = lambda m_, r, c: _blockdiag_extract(m_, r, c).reshape(S5_GROUPS, r, c).transpose(0, 2, 1)
    dbb_re, dbb_im = ex(dwbr, S5_GROUP, S5_STATE), ex(dwbi, S5_GROUP, S5_STATE)
    g_c_re, g_c_im = ex(dwcr, S5_STATE, S5_GROUP), ex(dwci, S5_STATE, S5_GROUP)
    dl_re = lane_sum8(dlr).reshape(S5_GROUPS, S5_STATE, 1)
    dl_im = lane_sum8(dli).reshape(S5_GROUPS, S5_STATE, 1)
    gar, gai, gls, gbr, gbi = s5_params_bwd(p["a_re"], p["a_im"], p["log_step"], p["b_re"], p["b_im"],
                                            dl_re, dl_im, dbb_re, dbb_im)
    grads = {"norm_g": dng.reshape(-1), "a_re": gar.reshape(S5_GROUPS, S5_STATE),
             "a_im": gai.reshape(S5_GROUPS, S5_STATE), "log_step": gls.reshape(-1), "b_re": gbr, "b_im": gbi,
             "c_re": g_c_re, "c_im": g_c_im, "d_skip": dds.reshape(-1), "b_glu": db_glu.reshape(-1)}
    return dh, grads


def lane_sum8(t):
    sb, seg, ns = t.shape

    def body(t_ref, o_ref):
        o_ref[...] = jnp.sum(t_ref[...], axis=1, keepdims=True)

    return pl.pallas_call(body, name="s5_seg_sum", out_shape=jax.ShapeDtypeStruct((sb, 1, ns), F32))(t)


MLA_DI = MLA_HEADS * 128
MLA_CQ0 = MLA_DI
MLA_CKV0 = MLA_CQ0 + MLA_Q_RANK
MLA_KR0 = MLA_CKV0 + MLA_KV_RANK
MLA_AW = MLA_KR0 + LANES


def _rot_half(x):
    w = x.shape[-1]
    lane = lax.broadcasted_iota(jnp.int32, x.shape, x.ndim - 1)
    return jnp.where(lane % MLA_ROPE < MLA_ROPE // 2, pltpu.roll(x, w - MLA_ROPE // 2, x.ndim - 1),
                     pltpu.roll(x, MLA_ROPE // 2, x.ndim - 1))


def rope_tables(pos, zero):
    l = pos.shape[0]
    tl = ROW_TILE
    j = np.arange(LANES) % MLA_ROPE % (MLA_ROPE // 2)
    inv_freq = (ROPE_THETA ** (-(2.0 * j) / MLA_ROPE)).astype(np.float32).reshape(1, LANES)
    sign = np.where(np.arange(LANES) % MLA_ROPE < MLA_ROPE // 2, -1.0, 1.0).astype(np.float32).reshape(1, LANES)

    def body(p_ref, f_ref, s_ref, cos_ref, sin_ref):
        ang = p_ref[...].astype(F32) * f_ref[...]
        cos_ref[...] = jnp.cos(ang)
        sin_ref[...] = jnp.sin(ang) * s_ref[...]

    st = jax.ShapeDtypeStruct((l, LANES), F32)
    return _rowcall(body, "rope_tables", l // tl, [_rows(tl, 1), _full((1, LANES)), _full((1, LANES))],
                    [_rows(tl, LANES)] * 2, [st, st])(pos, jnp.asarray(inv_freq), jnp.asarray(sign) + zero)


def _rope(x, cos, sins):
    return x * cos + _rot_half(x) * sins


def _rope_t(dy, cos, sins):
    return dy * cos - sins * _rot_half(dy)


def _rmsn(x):
    r = lax.rsqrt(jnp.mean(x * x, axis=-1, keepdims=True) + NORM_EPS)
    return x * r, r


def mla_pre(a, q_g, kv_g, cos, sins, name):
    l = a.shape[0]
    tl = ROW_TILE

    def body(a_ref, qg_ref, kg_ref, cos_ref, sin_ref, cq_ref, ckv_ref, krs_ref):
        xq, _ = _rmsn(a_ref[:, MLA_CQ0:MLA_CKV0])
        cq_ref[...] = (xq * qg_ref[...]).astype(BF16)
        xk, _ = _rmsn(a_ref[:, MLA_CKV0:MLA_KR0])
        ckv_ref[...] = (xk * kg_ref[...]).astype(BF16)
        kr = a_ref[:, MLA_KR0:MLA_AW]
        kr2 = kr + pltpu.roll(kr, MLA_ROPE, 1)
        kr2 = _rope(kr2, cos_ref[...], sin_ref[...])
        lane = lax.broadcasted_iota(jnp.int32, kr2.shape, 1)
        krs_ref[0] = jnp.where(lane < MLA_ROPE, kr2, 0.0).astype(BF16)
        krs_ref[1] = jnp.where(lane >= MLA_ROPE, kr2, 0.0).astype(BF16)

    st = jax.ShapeDtypeStruct
    return _rowcall(body, name, l // tl,
                    [_rows(tl, MLA_AW), _full((1, MLA_Q_RANK)), _full((1, MLA_KV_RANK)), _rows(tl, LANES), _rows(tl, LANES)],
                    [_rows(tl, MLA_Q_RANK), _rows(tl, MLA_KV_RANK), pl.BlockSpec((2, tl, LANES), lambda i: (0, i, 0))],
                    [st((l, MLA_Q_RANK), BF16), st((l, MLA_KV_RANK), BF16), st((2, l, LANES), BF16)])(
        a, q_g.reshape(1, -1), kv_g.reshape(1, -1), cos, sins)


def mla_rope_q(qr, cos, sins, name):
    l, w = qr.shape
    tl = ROW_TILE

    def body(q_ref, cos_ref, sin_ref, o_ref):
        c, s = cos_ref[...], sin_ref[...]
        for p in range(w // LANES):
            sl = slice(p * LANES, (p + 1) * LANES)
            o_ref[:, sl] = _rope(q_ref[:, sl], c, s).astype(BF16)

    return _rowcall(body, name, l // tl, [_rows(tl, w), _rows(tl, LANES), _rows(tl, LANES)], _rows(tl, w),
                    jax.ShapeDtypeStruct((l, w), BF16))(qr, cos, sins)


ATT_OUT = 512
ATT_IN = 512
ATT_R = ATT_OUT // ATT_IN


def _scores(qn, qr, kn, kr, mask_off, transposed):
    q2 = jnp.concatenate([qn, qr], axis=1)
    k2 = jnp.concatenate([kn, kr], axis=1)
    s = (_dot(k2, q2, DN_NT) if transposed else _dot(q2, k2, DN_NT)) * MLA_SCALE
    if mask_off is None:
        return s
    r = lax.broadcasted_iota(jnp.int32, s.shape, 0)
    c = lax.broadcasted_iota(jnp.int32, s.shape, 1)
    return jnp.where((r <= c + mask_off) if transposed else (c + mask_off <= r), s, NEG_INF)


def _fold(x, op):
    out = x[:, :LANES]
    for t in range(1, x.shape[1] // LANES):
        out = op(out, x[:, t * LANES:(t + 1) * LANES])
    return out


def flash_fwd(qn, qr, kv, krs, name):
    l = qn.shape[0]
    nq = l // ATT_OUT

    def body(qn_ref, qr_ref, kv_ref, kr_ref, o_ref, lse_ref, s_buf):
        qi = pl.program_id(1)
        q_r = qr_ref[...]
        q_n = [qn_ref[:, hh * LANES:(hh + 1) * LANES] for hh in range(2)]

        def block_scores(j, mx, mask_off):
            sl = pl.ds(pl.multiple_of(j * ATT_IN, ATT_IN), ATT_IN)
            out = []
            for hh in range(2):
                s = _scores(q_n[hh], q_r, kv_ref[sl, 2 * hh * LANES:(2 * hh + 1) * LANES], kr_ref[hh, sl, :],
                            mask_off, False)
                s_buf[hh, j] = s
                out.append(jnp.maximum(mx[hh], _fold(s, jnp.maximum)))
            return tuple(out)

        ninf = jnp.full((ATT_OUT, LANES), NEG_INF, F32)
        mx = lax.fori_loop(0, ATT_R * qi, lambda j, c: block_scores(j, c, None), (ninf, ninf))
        for d in range(ATT_R):
            mx = block_scores(ATT_R * qi + d, mx, d * ATT_IN)
        m = [jnp.max(mx[hh], axis=-1, keepdims=True) for hh in range(2)]

        def block_pv(j, carry):
            sl = pl.ds(pl.multiple_of(j * ATT_IN, ATT_IN), ATT_IN)
            out = []
            for hh in range(2):
                ls, acc = carry[hh]
                p = jnp.exp(s_buf[hh, j] - m[hh])
                out.append((ls + _fold(p, jnp.add),
                            acc + _dot(p, kv_ref[sl, (2 * hh + 1) * LANES:(2 * hh + 2) * LANES], DN_NN)))
            return tuple(out)

        z = jnp.zeros((ATT_OUT, LANES), F32)
        res = lax.fori_loop(0, ATT_R * (qi + 1), block_pv, ((z, z), (z, z)))
        for hh in range(2):
            lsum = jnp.sum(res[hh][0], axis=-1, keepdims=True)
            o_ref[:, hh * LANES:(hh + 1) * LANES] = res[hh][1] / lsum
            lse_ref[hh] = m[hh] + jnp.log(lsum)

    st = jax.ShapeDtypeStruct
    return pl.pallas_call(
        body, name=name, grid=(MLA_HEADS // 2, nq),
        in_specs=[pl.BlockSpec((ATT_OUT, 2 * LANES), lambda p, i: (i, p)),
                  pl.BlockSpec((ATT_OUT, LANES), lambda p, i: (i, p)),
                  pl.BlockSpec((l, 4 * LANES), lambda p, i: (0, p)),
                  pl.BlockSpec((2, l, LANES), lambda p, i: (0, 0, 0))],
        out_specs=[pl.BlockSpec((ATT_OUT, 2 * LANES), lambda p, i: (i, p)),
                   pl.BlockSpec((2, ATT_OUT, 1), lambda p, i: (p, i, 0))],
        out_shape=[st((l, MLA_DI), F32), st((MLA_HEADS, l, 1), F32)],
        scratch_shapes=[pltpu.VMEM((2, l // ATT_IN, ATT_OUT, ATT_IN), F32)],
        compiler_params=_cparams(("parallel", "arbitrary")))(qn, qr, kv, krs)


def flash_dkv(qn, qr, kv, krs, do, lse_row, delta_row, name):
    l = qn.shape[0]
    nk = l // ATT_OUT
    nq = l // ATT_IN

    def body(qn_ref, qr_ref, do_ref, lse_ref, dl_ref, kv_ref, kr_ref, dkv_ref, dkr_ref):
        kj = pl.program_id(1)
        lane = lax.broadcasted_iota(jnp.int32, (ATT_OUT, LANES), 1)
        kn = [kv_ref[:, 2 * hh * LANES:(2 * hh + 1) * LANES] for hh in range(2)]
        v = [kv_ref[:, (2 * hh + 1) * LANES:(2 * hh + 2) * LANES] for hh in range(2)]

        def block(i, carry, mask_off):
            sl = pl.ds(pl.multiple_of(i * ATT_IN, ATT_IN), ATT_IN)
            q_r = qr_ref[sl, :]
            out = []
            for hh in range(2):
                dk2, dv = carry[hh]
                hs = slice(hh * LANES, (hh + 1) * LANES)
                q_n, d_o = qn_ref[sl, hs], do_ref[sl, hs]
                s = _scores(q_n, q_r, kn[hh], kr_ref[hh], mask_off, True)
                pt = jnp.exp(s - lse_ref[hh, i])
                dv = dv + _dot(pt, d_o, DN_NN)
                dpt = _dot(v[hh], d_o, DN_NT)
                dst = (pt * (dpt - dl_ref[hh, i]) * MLA_SCALE).astype(BF16)
                out.append((dk2 + _dot(dst, jnp.concatenate([q_n, q_r], axis=1), DN_NN), dv))
            return tuple(out)

        z = jnp.zeros((ATT_OUT, LANES), F32)
        z2 = jnp.zeros((ATT_OUT, 2 * LANES), F32)
        res = ((z2, z), (z2, z))
        for d in range(ATT_R):
            res = block(ATT_R * kj + d, res, d * ATT_IN)
        res = lax.fori_loop(ATT_R * (kj + 1), nq, lambda i, c: block(i, c, None), res)
        for hh in range(2):
            dkv_ref[:, 2 * hh * LANES:(2 * hh + 1) * LANES] = res[hh][0][:, :LANES].astype(BF16)
            dkv_ref[:, (2 * hh + 1) * LANES:(2 * hh + 2) * LANES] = res[hh][1].astype(BF16)
        dkr_ref[0] = jnp.where(lane < MLA_ROPE, res[0][0][:, LANES:], res[1][0][:, LANES:])

    st = jax.ShapeDtypeStruct
    return pl.pallas_call(
        body, name=name, grid=(MLA_HEADS // 2, nk),
        in_specs=[pl.BlockSpec((l, 2 * LANES), lambda p, j: (0, p)),
                  pl.BlockSpec((l, LANES), lambda p, j: (0, p)),
                  pl.BlockSpec((l, 2 * LANES), lambda p, j: (0, p)),
                  pl.BlockSpec((2, nq, 1, ATT_IN), lambda p, j: (p, 0, 0, 0)),
                  pl.BlockSpec((2, nq, 1, ATT_IN), lambda p, j: (p, 0, 0, 0)),
                  pl.BlockSpec((ATT_OUT, 4 * LANES), lambda p, j: (j, p)),
                  pl.BlockSpec((2, ATT_OUT, LANES), lambda p, j: (0, j, 0))],
        out_specs=[pl.BlockSpec((ATT_OUT, 4 * LANES), lambda p, j: (j, p)),
                   pl.BlockSpec((1, ATT_OUT, LANES), lambda p, j: (p, j, 0))],
        out_shape=[st((l, 2 * MLA_DI), BF16), st((MLA_HEADS // 2, l, LANES), F32)],
        compiler_params=_cparams(("parallel", "arbitrary")))(qn, qr, do, lse_row, delta_row, kv, krs)


def flash_dq(qn, qr, kv, krs, do, lse, delta, cos, sins, name):
    l = qn.shape[0]
    nq = l // ATT_OUT

    def body(qn_ref, qr_ref, do_ref, lse_ref, dl_ref, kv_ref, kr_ref, cos_ref, sin_ref, dqn_ref, dqr_ref):
        qi = pl.program_id(1)
        q_r = qr_ref[...]
        q_n = [qn_ref[:, hh * LANES:(hh + 1) * LANES] for hh in range(2)]
        d_o = [do_ref[:, hh * LANES:(hh + 1) * LANES] for hh in range(2)]
        lse_h = [lse_ref[hh] for hh in range(2)]
        dl_h = [dl_ref[hh] for hh in range(2)]

        def block(j, carry, mask_off):
            sl = pl.ds(pl.multiple_of(j * ATT_IN, ATT_IN), ATT_IN)
            dq2 = list(carry)
            for hh in range(2):
                kn = kv_ref[sl, 2 * hh * LANES:(2 * hh + 1) * LANES]
                v = kv_ref[sl, (2 * hh + 1) * LANES:(2 * hh + 2) * LANES]
                kr = kr_ref[hh, sl, :]
                s = _scores(q_n[hh], q_r, kn, kr, mask_off, False)
                pr = jnp.exp(s - lse_h[hh])
                dp = _dot(d_o[hh], v, DN_NT)
                ds = (pr * (dp - dl_h[hh]) * MLA_SCALE).astype(BF16)
                dq2[hh] = dq2[hh] + _dot(ds, jnp.concatenate([kn, kr], axis=1), DN_NN)
            return tuple(dq2)

        z2 = jnp.zeros((ATT_OUT, 2 * LANES), F32)
        res = lax.fori_loop(0, ATT_R * qi, lambda j, c: block(j, c, None), (z2, z2))
        for d in range(ATT_R):
            res = block(ATT_R * qi + d, res, d * ATT_IN)
        dqn_ref[:, 0:LANES] = res[0][:, :LANES].astype(BF16)
        dqn_ref[:, LANES:2 * LANES] = res[1][:, :LANES].astype(BF16)
        dqr = res[0][:, LANES:] + res[1][:, LANES:]
        dqr_ref[...] = _rope_t(dqr, cos_ref[...], sin_ref[...]).astype(BF16)

    st = jax.ShapeDtypeStruct
    return pl.pallas_call(
        body, name=name, grid=(MLA_HEADS // 2, nq),
        in_specs=[pl.BlockSpec((ATT_OUT, 2 * LANES), lambda p, i: (i, p)),
                  pl.BlockSpec((ATT_OUT, LANES), lambda p, i: (i, p)),
                  pl.BlockSpec((ATT_OUT, 2 * LANES), lambda p, i: (i, p)),
                  pl.BlockSpec((2, ATT_OUT, 1), lambda p, i: (p, i, 0)),
                  pl.BlockSpec((2, ATT_OUT, 1), lambda p, i: (p, i, 0)),
                  pl.BlockSpec((l, 4 * LANES), lambda p, i: (0, p)),
                  pl.BlockSpec((2, l, LANES), lambda p, i: (0, 0, 0)),
                  pl.BlockSpec((ATT_OUT, LANES), lambda p, i: (i, 0)),
                  pl.BlockSpec((ATT_OUT, LANES), lambda p, i: (i, 0))],
        out_specs=[pl.BlockSpec((ATT_OUT, 2 * LANES), lambda p, i: (i, p)),
                   pl.BlockSpec((ATT_OUT, LANES), lambda p, i: (i, p))],
        out_shape=[st((l, MLA_DI), BF16), st((l, MLA_HEADS * MLA_ROPE), BF16)],
        compiler_params=_cparams(("parallel", "arbitrary")))(qn, qr, do, lse, delta, kv, krs, cos, sins)


def mla_gate_fwd(o, a, name):
    l = o.shape[0]
    tl = ROW_TILE

    def body(o_ref, z_ref, m_ref):
        m_ref[...] = (o_ref[...] * _silu(z_ref[...])).astype(BF16)

    return _rowcall(body, name, l // tl, [_rows(tl, MLA_DI), _rows(tl, MLA_DI)], _rows(tl, MLA_DI),
                    jax.ShapeDtypeStruct((l, MLA_DI), BF16))(o, a)


def mla_gate_bwd(dm, o, a, name):
    l = o.shape[0]
    tl = ROW_TILE

    def body(dm_ref, o_ref, z_ref, do_ref, dz_ref, dl_ref):
        dmv, ov, z = dm_ref[...], o_ref[...], z_ref[...]
        sz, gz = _silu_both(z)
        d_o = dmv * sz
        do_ref[...] = d_o.astype(BF16)
        dz_ref[...] = (dmv * ov * gz).astype(BF16)
        pr = d_o * ov
        for h in range(MLA_HEADS):
            dl_ref[h] = jnp.sum(pr[:, h * LANES:(h + 1) * LANES], axis=1, keepdims=True)

    st = jax.ShapeDtypeStruct
    return _rowcall(body, name, l // tl, [_rows(tl, MLA_DI)] * 3,
                    [_rows(tl, MLA_DI), _rows(tl, MLA_DI), pl.BlockSpec((MLA_HEADS, tl, 1), lambda i: (0, i, 0))],
                    [st((l, MLA_DI), BF16), st((l, MLA_DI), BF16), st((MLA_HEADS, l, 1), F32)])(dm, o, a)


def mla_post(a, dcqn, dckvn, dkr_pairs, dz, q_g, kv_g, cos, sins, name):
    l = a.shape[0]
    tl = ROW_TILE
    npair = MLA_HEADS // 2

    def norm_bwd(x, g, dy):
        xhat, r = _rmsn(x)
        dxh = dy * g
        return r * (dxh - xhat * jnp.mean(dxh * xhat, axis=-1, keepdims=True)), jnp.sum(dy * xhat, axis=0, keepdims=True)

    def body(a_ref, dq_ref, dk_ref, dkr_ref, dz_ref, qg_ref, kg_ref, cos_ref, sin_ref, da_ref, dqg_ref, dkg_ref):
        i = pl.program_id(0)
        da_ref[:, 0:MLA_DI] = dz_ref[...]
        dcq, dqg = norm_bwd(a_ref[:, MLA_CQ0:MLA_CKV0], qg_ref[...], dq_ref[...])
        da_ref[:, MLA_CQ0:MLA_CKV0] = dcq.astype(BF16)
        dckv, dkg = norm_bwd(a_ref[:, MLA_CKV0:MLA_KR0], kg_ref[...], dk_ref[...])
        da_ref[:, MLA_CKV0:MLA_KR0] = dckv.astype(BF16)
        dk2 = dkr_ref[0]
        for p in range(1, npair):
            dk2 = dk2 + dkr_ref[p]
        dk2 = _rope_t(dk2, cos_ref[...], sin_ref[...])
        dk2 = dk2 + pltpu.roll(dk2, MLA_ROPE, 1)
        lane = lax.broadcasted_iota(jnp.int32, dk2.shape, 1)
        da_ref[:, MLA_KR0:MLA_AW] = jnp.where(lane < MLA_ROPE, dk2, 0.0).astype(BF16)
        _acc(dqg_ref, dqg, i)
        _acc(dkg_ref, dkg, i)

    st = jax.ShapeDtypeStruct
    return _rowcall(body, name, l // tl,
                    [_rows(tl, MLA_AW), _rows(tl, MLA_Q_RANK), _rows(tl, MLA_KV_RANK),
                     pl.BlockSpec((npair, tl, LANES), lambda i: (0, i, 0)), _rows(tl, MLA_DI),
                     _full((1, MLA_Q_RANK)), _full((1, MLA_KV_RANK)), _rows(tl, LANES), _rows(tl, LANES)],
                    [_rows(tl, MLA_AW), _full((1, MLA_Q_RANK)), _full((1, MLA_KV_RANK))],
                    [st((l, MLA_AW), BF16), st((1, MLA_Q_RANK), F32), st((1, MLA_KV_RANK), F32)])(
        a, dcqn, dckvn, dkr_pairs, dz, q_g.reshape(1, -1), kv_g.reshape(1, -1), cos, sins)


def _mla_w_in_perm(w):
    r = MLA_Q_RANK + MLA_KV_RANK + MLA_ROPE
    pad = jnp.zeros(w.shape[:-1] + (MLA_AW - MLA_KR0 - MLA_ROPE,), w.dtype)
    return jnp.concatenate([w[..., r:], w[..., :r], pad], axis=-1)


def _mla_w_in_unperm(g):
    r = MLA_Q_RANK + MLA_KV_RANK + MLA_ROPE
    return jnp.concatenate([g[..., MLA_DI:MLA_DI + r], g[..., :MLA_DI]], axis=-1)


def _mla_w_uq_split(w):
    k = w.shape[0]
    w3 = w.reshape(k, MLA_HEADS, MLA_NOPE + MLA_ROPE)
    return w3[:, :, :MLA_NOPE].reshape(k, MLA_HEADS * MLA_NOPE), w3[:, :, MLA_NOPE:].reshape(k, MLA_HEADS * MLA_ROPE)


def _mla_w_uq_merge(gn, gr):
    k = gn.shape[0]
    return jnp.concatenate([gn.reshape(k, MLA_HEADS, MLA_NOPE), gr.reshape(k, MLA_HEADS, MLA_ROPE)], axis=2).reshape(k, -1)


def mla_layer_fwd(h, p, wf, cos, sins, tag):
    hn = rms_fwd(h, p["norm_g"], tag + "_rms")
    w_in = _mla_w_in_perm(wf["w_in"])
    w_uq_n, w_uq_r = _mla_w_uq_split(wf["w_uq"])
    a = matmul(hn, w_in, "nn", tag + "_mm_in")
    cqn, ckvn, krs = mla_pre(a, p["q_norm_g"], p["kv_norm_g"], cos, sins, tag + "_pre")
    qn = matmul(cqn, w_uq_n, "nn", tag + "_mm_qn", out_dtype=BF16)
    qr_raw = matmul(cqn, w_uq_r, "nn", tag + "_mm_qr")
    qr = mla_rope_q(qr_raw, cos, sins, tag + "_rope_q")
    kv = matmul(ckvn, wf["w_ukv"], "nn", tag + "_mm_kv", out_dtype=BF16)
    o, lse = flash_fwd(qn, qr, kv, krs, tag + "_flash")
    m = mla_gate_fwd(o, a, tag + "_gate")
    h_out = matmul(m, wf["w_out"], "nn", tag + "_mm_out", add=h)
    return h_out, (h, hn, a, cqn, ckvn, krs, qn, qr, kv, o, lse, m, w_in, w_uq_n, w_uq_r)


def mla_layer_bwd(dh_out, saved, p, wf, cos, sins, tag, sink):
    h, hn, a, cqn, ckvn, krs, qn, qr, kv, o, lse, m, w_in, w_uq_n, w_uq_r = saved
    l = h.shape[0]
    dm = matmul(dh_out, wf["w_out"], "nt", tag + "_mm_dm")
    sink.mm("w_out", m, dh_out, tag + "_mm_gwout")
    do, dz, delta = mla_gate_bwd(dm, o, a, tag + "_gate_bwd")
    lse_row = lse.reshape(MLA_HEADS, l // ATT_IN, 1, ATT_IN)
    delta_row = delta.reshape(MLA_HEADS, l // ATT_IN, 1, ATT_IN)
    dkv, dkr_pairs = flash_dkv(qn, qr, kv, krs, do, lse_row, delta_row, tag + "_flash_dkv")
    dqn, dqr = flash_dq(qn, qr, kv, krs, do, lse, delta, cos, sins, tag + "_flash_dq")
    dcqn = matmul(dqn, w_uq_n, "nt", tag + "_mm_dcq_n")
    dcqn = matmul(dqr, w_uq_r, "nt", tag + "_mm_dcq_r", add=dcqn)
    g_uq_n = matmul(cqn, dqn, "tn", tag + "_mm_guq_n")
    g_uq_r = matmul(cqn, dqr, "tn", tag + "_mm_guq_r")
    dckvn = matmul(dkv, wf["w_ukv"], "nt", tag + "_mm_dckv")
    sink.mm("w_ukv", ckvn, dkv, tag + "_mm_gukv")
    da, dqg, dkg = mla_post(a, dcqn, dckvn, dkr_pairs, dz, p["q_norm_g"], p["kv_norm_g"], cos, sins, tag + "_post")
    dhn = matmul(da, w_in, "nt", tag + "_mm_dhn")
    g_w_in = matmul(hn, da, "tn", tag + "_mm_gwin")
    sink.put("w_uq", _mla_w_uq_merge(g_uq_n, g_uq_r))
    sink.put("w_in", _mla_w_in_unperm(g_w_in))
    zero = sink.send()
    dh, dng = rms_bwd(h, p["norm_g"] + zero, dhn, dh_out, tag + "_rms_bwd")
    grads = {"norm_g": dng.reshape(-1), "q_norm_g": dqg.reshape(-1), "kv_norm_g": dkg.reshape(-1)}
    return dh, grads


ANY = pl.BlockSpec(memory_space=pl.ANY)


def _me():
    return lax.axis_index("x"), lax.axis_index("y"), lax.axis_index("c")


def _chip():
    return 2 * lax.axis_index("x") + lax.axis_index("y")


def _other_chips(x, y):
    return [(1 - x, y), (x, 1 - y), (1 - x, 1 - y)]


def _rcopy(src, dst, ssem, rsem, dev):
    return pltpu.make_async_remote_copy(src_ref=src, dst_ref=dst, send_sem=ssem, recv_sem=rsem,
                                        device_id=dev, device_id_type=MESH)


def _half(ref, c, hf):
    return ref.at[pl.ds(c * hf, hf), :]


HBM = pl.BlockSpec(memory_space=pltpu.HBM)
SEM = pl.BlockSpec(memory_space=pltpu.SEMAPHORE)
SPLIT_EFFECT = pltpu.SideEffectType.DATAFLOW_SIDE_EFFECTING


def gather_start(wb, after, name):
    nr, w = wb.shape
    hf = nr // 2

    def body(w_ref, land_ref, after_ref, ssem, rsem, w_thru, land_thru, token):
        x, y, c = _me()
        k = 2 * x + y
        for j, (cx, cy) in enumerate(_other_chips(x, y)):
            _rcopy(_half(w_ref, c, hf), _half(land_ref.at[k], c, hf), ssem.at[j], rsem.at[j], (cx, cy, c)).start()
        token[...] = jnp.zeros_like(token)

    land = lax.empty((N_CHIPS, nr, w), wb.dtype)
    return pl.pallas_call(
        body, name=name,
        out_shape=(pltpu.SemaphoreType.DMA((3,)), pltpu.SemaphoreType.DMA((3,)), pltpu.HBM(wb.shape, wb.dtype),
                   pltpu.HBM(land.shape, land.dtype), jax.ShapeDtypeStruct((8, LANES), F32)),
        in_specs=(HBM, HBM, ANY), out_specs=(SEM, SEM, HBM, HBM, pl.BlockSpec(memory_space=pltpu.VMEM)),
        input_output_aliases={0: 2, 1: 3},
        compiler_params=pltpu.CompilerParams(has_side_effects=SPLIT_EFFECT))(
        pltpu.with_memory_space_constraint(wb, pltpu.HBM), pltpu.with_memory_space_constraint(land, pltpu.HBM), after)


def gather_wait(ssem, rsem, w_thru, land_thru, after, name):
    nr, w = w_thru.shape
    hf = nr // 2

    def body(w_ref, land_ref, ssem_ref, rsem_ref, after_ref, w_dead, got_ref):
        x, y, c = _me()
        for j, (cx, cy) in enumerate(_other_chips(x, y)):
            cp = _rcopy(_half(w_ref, c, hf), _half(land_ref.at[2 * cx + cy], c, hf), ssem_ref.at[j], rsem_ref.at[j],
                        (cx, cy, c))
            cp.wait_send()
            cp.wait_recv()

    return pl.pallas_call(
        body, name=name, out_shape=(pltpu.HBM(w_thru.shape, w_thru.dtype), pltpu.HBM(land_thru.shape, land_thru.dtype)),
        in_specs=(HBM, HBM, SEM, SEM, ANY), out_specs=(HBM, HBM), input_output_aliases={0: 0, 1: 1},
        compiler_params=pltpu.CompilerParams(has_side_effects=SPLIT_EFFECT))(w_thru, land_thru, ssem, rsem, after)[1]


def gather_handover(land, wb, name):
    _, nr, w = land.shape
    hf = nr // 2

    def body(l_ref, o_ref, ssem, rsem):
        x, y, c = _me()
        chips = _other_chips(x, y)
        sends = []
        for j, (cx, cy) in enumerate(chips):
            region = _half(o_ref.at[2 * cx + cy], c, hf)
            sends.append(_rcopy(region, region, ssem.at[j], rsem.at[j], (x, y, 1 - c)))
            sends[-1].start()
        for j, (cx, cy) in enumerate(chips):
            region = _half(o_ref.at[2 * cx + cy], 1 - c, hf)
            _rcopy(region, region, ssem.at[j], rsem.at[j], (x, y, 1 - c)).wait_recv()
        for cp in sends:
            cp.wait_send()

    out = pl.pallas_call(
        body, name=name, in_specs=[ANY], out_specs=ANY, input_output_aliases={0: 0},
        out_shape=jax.ShapeDtypeStruct(land.shape, land.dtype),
        scratch_shapes=[pltpu.SemaphoreType.DMA((3,)), pltpu.SemaphoreType.DMA((3,))])(land)
    return lax.dynamic_update_slice(out, wb[None], (_chip(), 0, 0))


def reduce_start(t, after, name):
    def body(t_ref, land_ref, after_ref, ssem, rsem, t_thru, land_thru, token):
        x, y, c = _me()
        k = 2 * x + y
        for j, (cx, cy) in enumerate(_other_chips(x, y)):
            _rcopy(t_ref.at[2 * cx + cy], land_ref.at[k], ssem.at[j], rsem.at[j], (cx, cy, c)).start()
        token[...] = jnp.zeros_like(token)

    land = lax.empty(t.shape, t.dtype)
    return pl.pallas_call(
        body, name=name,
        out_shape=(pltpu.SemaphoreType.DMA((3,)), pltpu.SemaphoreType.DMA((3,)), pltpu.HBM(t.shape, t.dtype),
                   pltpu.HBM(t.shape, t.dtype), jax.ShapeDtypeStruct((8, LANES), F32)),
        in_specs=(HBM, HBM, ANY), out_specs=(SEM, SEM, HBM, HBM, pl.BlockSpec(memory_space=pltpu.VMEM)),
        input_output_aliases={0: 2, 1: 3},
        compiler_params=pltpu.CompilerParams(has_side_effects=SPLIT_EFFECT))(
        pltpu.with_memory_space_constraint(t, pltpu.HBM), pltpu.with_memory_space_constraint(land, pltpu.HBM), after)


def bcast_start(g, after, name):
    def body(g_ref, land_ref, after_ref, ssem, rsem, g_thru, land_thru, token):
        x, y, c = _me()
        k = 2 * x + y
        for j, (cx, cy) in enumerate(_other_chips(x, y)):
            _rcopy(g_ref, land_ref.at[k], ssem.at[j], rsem.at[j], (cx, cy, c)).start()
        token[...] = jnp.zeros_like(token)

    land = lax.empty((N_CHIPS,) + g.shape, g.dtype)
    return pl.pallas_call(
        body, name=name,
        out_shape=(pltpu.SemaphoreType.DMA((3,)), pltpu.SemaphoreType.DMA((3,)), pltpu.HBM(g.shape, g.dtype),
                   pltpu.HBM(land.shape, land.dtype), jax.ShapeDtypeStruct((8, LANES), F32)),
        in_specs=(HBM, HBM, ANY), out_specs=(SEM, SEM, HBM, HBM, pl.BlockSpec(memory_space=pltpu.VMEM)),
        input_output_aliases={0: 2, 1: 3},
        compiler_params=pltpu.CompilerParams(has_side_effects=SPLIT_EFFECT))(
        pltpu.with_memory_space_constraint(g, pltpu.HBM), pltpu.with_memory_space_constraint(land, pltpu.HBM), after)


def bcast_wait(ssem, rsem, g_thru, land_thru, after, name):
    def body(g_ref, land_ref, ssem_ref, rsem_ref, after_ref, g_out, got_ref):
        x, y, c = _me()
        for j, (cx, cy) in enumerate(_other_chips(x, y)):
            cp = _rcopy(g_ref, land_ref.at[2 * cx + cy], ssem_ref.at[j], rsem_ref.at[j], (cx, cy, c))
            cp.wait_send()
            cp.wait_recv()

    g, land = pl.pallas_call(
        body, name=name, out_shape=(pltpu.HBM(g_thru.shape, g_thru.dtype), pltpu.HBM(land_thru.shape, land_thru.dtype)),
        in_specs=(HBM, HBM, SEM, SEM, ANY), out_specs=(HBM, HBM), input_output_aliases={0: 0, 1: 1},
        compiler_params=pltpu.CompilerParams(has_side_effects=SPLIT_EFFECT))(g_thru, land_thru, ssem, rsem, after)
    return lax.dynamic_update_slice(land, g[None], (_chip(), 0, 0))


def reduce_wait(ssem, rsem, t_thru, land_thru, after, name):
    def body(t_ref, land_ref, ssem_ref, rsem_ref, after_ref, t_out, got_ref):
        x, y, c = _me()
        k = 2 * x + y
        for j, (cx, cy) in enumerate(_other_chips(x, y)):
            cp = _rcopy(t_ref.at[k], land_ref.at[2 * cx + cy], ssem_ref.at[j], rsem_ref.at[j], (cx, cy, c))
            cp.wait_send()
            cp.wait_recv()

    return pl.pallas_call(
        body, name=name, out_shape=(pltpu.HBM(t_thru.shape, t_thru.dtype), pltpu.HBM(land_thru.shape, land_thru.dtype)),
        in_specs=(HBM, HBM, SEM, SEM, ANY), out_specs=(HBM, HBM), input_output_aliases={0: 0, 1: 1},
        compiler_params=pltpu.CompilerParams(has_side_effects=SPLIT_EFFECT))(t_thru, land_thru, ssem, rsem, after)


def grads_to_sibling(ps, name="grads_to_sibling"):
    n = len(ps)

    def body(*refs):
        p_refs, o_refs, ssem, rsem = refs[:n], refs[n:2 * n], refs[2 * n], refs[2 * n + 1]
        x, y, c = _me()
        cps = []
        for a in range(n):
            hf = ps[a].shape[1] // 2
            cps.append(_rcopy(p_refs[a].at[:, pl.ds((1 - c) * hf, hf), :], o_refs[a], ssem.at[a], rsem.at[a],
                              (x, y, 1 - c)))
        for cp in cps:
            cp.start()
        for cp in cps:
            cp.wait()

    return pl.pallas_call(
        body, name=name, in_specs=[ANY] * n, out_specs=[ANY] * n,
        out_shape=[jax.ShapeDtypeStruct((N_CHIPS, p.shape[1] // 2, p.shape[2]), p.dtype) for p in ps],
        scratch_shapes=[pltpu.SemaphoreType.DMA((n,)), pltpu.SemaphoreType.DMA((n,))])(*ps)


def pair_sum(p, ra, out_dtype, name):
    _, nr, w = p.shape
    hf = nr // 2
    tr = _pick_rows(hf)
    nb = hf // tr

    def body(c_ref, p_ref, r_ref, o_ref):
        o_ref[...] = (p_ref[...] + r_ref[...]).astype(out_dtype)

    c = lax.axis_index("c").astype(jnp.int32).reshape(1)
    return pl.pallas_call(
        body, name=name,
        grid_spec=pltpu.PrefetchScalarGridSpec(
            num_scalar_prefetch=1, grid=(N_CHIPS, nb),
            in_specs=[pl.BlockSpec((1, tr, w), lambda k, i, c_ref: (k, c_ref[0] * nb + i, 0)),
                      pl.BlockSpec((1, tr, w), lambda k, i, c_ref: (k, i, 0))],
            out_specs=pl.BlockSpec((1, tr, w), lambda k, i, c_ref: (k, i, 0))),
        out_shape=jax.ShapeDtypeStruct((N_CHIPS, hf, w), out_dtype),
        compiler_params=_cparams(("parallel", "parallel")))(c, p, ra)


def grads_across_chips(ts):
    n = len(ts)

    def body(*refs):
        t_refs, o_refs, ssem, rsem = refs[:n], refs[n:2 * n], refs[2 * n], refs[2 * n + 1]
        x, y, c = _me()
        k = 2 * x + y
        chips = _other_chips(x, y)
        sends = [_rcopy(t_refs[a].at[2 * cx + cy], o_refs[a].at[k], ssem.at[3 * a + j], rsem.at[3 * a + j], (cx, cy, c))
                 for a in range(n) for j, (cx, cy) in enumerate(chips)]
        for cp in sends:
            cp.start()
        for a in range(n):
            for j, (cx, cy) in enumerate(chips):
                _rcopy(t_refs[a].at[k], o_refs[a].at[2 * cx + cy], ssem.at[3 * a + j], rsem.at[3 * a + j],
                       (cx, cy, c)).wait_recv()
        for cp in sends:
            cp.wait_send()

    return pl.pallas_call(
        body, name="grads_across_chips", in_specs=[ANY] * n, out_specs=[ANY] * n,
        out_shape=[jax.ShapeDtypeStruct(t.shape, t.dtype) for t in ts],
        scratch_shapes=[pltpu.SemaphoreType.DMA((3 * n,)), pltpu.SemaphoreType.DMA((3 * n,))])(*ts)


def chip_sum(t, rb, name):
    _, hf, w = rb.shape
    tr = _pick_rows(hf)
    nb = hf // tr

    def body(kc_ref, t_ref, r_ref, o_ref):
        k = kc_ref[0]
        acc = jnp.where(k == 0, t_ref[0], r_ref[0]).astype(F32)
        for j in range(1, N_CHIPS):
            acc = acc + jnp.where(k == j, t_ref[0], r_ref[j]).astype(F32)
        o_ref[...] = acc

    kc = jnp.stack([_chip(), lax.axis_index("c")]).astype(jnp.int32)
    return pl.pallas_call(
        body, name=name,
        grid_spec=pltpu.PrefetchScalarGridSpec(
            num_scalar_prefetch=1, grid=(nb,),
            in_specs=[pl.BlockSpec((1, tr, w), lambda i, kc_ref: (kc_ref[0], i, 0)),
                      pl.BlockSpec((N_CHIPS, tr, w), lambda i, kc_ref: (0, i, 0))],
            out_specs=pl.BlockSpec((tr, w), lambda i, kc_ref: (kc_ref[1] * nb + i, 0))),
        out_shape=jax.ShapeDtypeStruct((2 * hf, w), F32), compiler_params=_cparams(("parallel",)))(kc, t, rb)


def reduced_to_sibling(gs):
    n = len(gs)

    def body(*refs):
        o_refs, ssem, rsem = refs[n:2 * n], refs[2 * n], refs[2 * n + 1]
        x, y, c = _me()
        cps = []
        for a in range(n):
            hf = gs[a].shape[0] // 2
            cps.append(_rcopy(_half(o_refs[a], c, hf), _half(o_refs[a], c, hf), ssem.at[a], rsem.at[a], (x, y, 1 - c)))
        for cp in cps:
            cp.start()
        for a in range(n):
            hf = gs[a].shape[0] // 2
            _rcopy(_half(o_refs[a], c, hf), _half(o_refs[a], 1 - c, hf), ssem.at[a], rsem.at[a],
                   (x, y, 1 - c)).wait_recv()
        for cp in cps:
            cp.wait_send()

    return pl.pallas_call(
        body, name="reduced_to_sibling", in_specs=[ANY] * n, out_specs=[ANY] * n,
        input_output_aliases={a: a for a in range(n)},
        out_shape=[jax.ShapeDtypeStruct(g.shape, g.dtype) for g in gs],
        scratch_shapes=[pltpu.SemaphoreType.DMA((n,)), pltpu.SemaphoreType.DMA((n,))])(*gs)


def _adamw_step(w_ref, g_ref, m_ref, v_ref, d_ref, nm_ref, nv_ref):
    bc1 = 1.0 - ADAM_B1 ** ADAM_STEP
    bc2 = 1.0 - ADAM_B2 ** ADAM_STEP
    gv = g_ref[...]
    nm = ADAM_B1 * m_ref[...] + (1.0 - ADAM_B1) * gv
    nv = ADAM_B2 * v_ref[...] + (1.0 - ADAM_B2) * (gv * gv)
    nm_ref[...] = nm
    nv_ref[...] = nv
    d_ref[...] = -ADAM_LR * ((nm / bc1) / (jnp.sqrt(nv / bc2) + ADAM_EPS) + ADAM_WD * w_ref[...])


def adamw_packed(w, g_buf, r0, m, v, name):
    r, c = w.shape
    tr = _tile_rows(r, r0)

    def body(w_ref, g_ref, m_ref, v_ref, go_ref, d_ref, nm_ref, nv_ref):
        go_ref[...] = g_ref[...]
        _adamw_step(w_ref, g_ref, m_ref, v_ref, d_ref, nm_ref, nv_ref)

    own = pl.BlockSpec((tr, CHUNK_W), lambda i, j: (i, j))
    packed = pl.BlockSpec((tr, CHUNK_W), lambda i, j: ((r0 + j * r) // tr + i, 0))
    st = jax.ShapeDtypeStruct((r, c), F32)
    return pl.pallas_call(body, name=name, grid=(r // tr, c // CHUNK_W), in_specs=[own, packed, own, own],
                          out_specs=[own] * 4, out_shape=[st] * 4,
                          compiler_params=_cparams(("parallel", "parallel")))(w, g_buf, m, v)


def adamw(w, g, m, v, name):
    r, wd = w.shape
    tr = _pick_rows(r, cap=max(16, ADAMW_BLOCK_BYTES // (4 * wd)))
    body = functools.partial(_adamw_step)

    spec = pl.BlockSpec((tr, wd), lambda i: (i, 0))
    st = jax.ShapeDtypeStruct((r, wd), F32)
    return pl.pallas_call(body, name=name, grid=(r // tr,), in_specs=[spec] * 4, out_specs=[spec] * 3,
                          out_shape=[st, st, st], compiler_params=_cparams(("parallel",)))(w, g, m, v)


LAYER_KINDS = ("gmlp", "s5", "mla", "gmlp")
PARAMS = {
    "gmlp": ("norm_g", "w_in", "ln_g", "ln_b", "w_s", "b_s", "w_out"),
    "s5": ("norm_g", "w_in", "a_re", "a_im", "log_step", "b_re", "b_im", "c_re", "c_im", "d_skip", "w_glu", "b_glu", "w_out"),
    "mla": ("norm_g", "w_in", "q_norm_g", "w_uq", "kv_norm_g", "w_ukv", "w_out"),
}
COL_SHARDED = ("w_in", "w_uq", "w_ukv")
ROW_SHARDED = ("w_out", "w_glu")
WEIGHT_NAMES = [("l%d_" % i) + n for i, kind in enumerate(LAYER_KINDS) for n in PARAMS[kind]] + ["final_norm_g"]


def _is_big(name):
    return name.split("_", 1)[1] in COL_SHARDED + ROW_SHARDED


BIG = [n for n in WEIGHT_NAMES if _is_big(n)]
SMALL = [n for n in WEIGHT_NAMES if not _is_big(n)]


def _pack_rows(blocks):
    return jnp.concatenate([b.reshape(-1, PACK_W) for b in blocks], axis=0)


def _shard_major(wn, full, width):
    r, c = full.shape
    if wn in COL_SHARDED:
        t = full.reshape(r, N_CHIPS, c // N_CHIPS).transpose(1, 0, 2)
    else:
        t = full.reshape(N_CHIPS, r // N_CHIPS, c)
    return t.reshape(N_CHIPS, -1, width)


def _from_shard_major(name, t, block_shape):
    r, c = block_shape
    if name.split("_", 1)[1] in COL_SHARDED:
        return t.reshape(N_CHIPS, r, c).transpose(1, 0, 2).reshape(r, N_CHIPS * c)
    return t.reshape(N_CHIPS * r, c)


class BigGradSink:
    ORDER = ("w_out", "w_glu", "w_ukv", "w_uq", "w_in")
    ROW_MAJOR = {2: ("w_uq", "w_in")}

    def __init__(self, layer, block_shapes):
        self.layer = layer
        self.regions = {}
        r0 = 0
        for wn in self.ORDER:
            if wn in block_shapes:
                shape = block_shapes[wn]
                self.regions[wn] = (r0, shape, wn not in self.ROW_MAJOR.get(layer, ()))
                r0 += shape[0] * shape[1] // CHUNK_W
        self.buf = lax.empty((N_CHIPS, r0, CHUNK_W), F32)
        self.flight = None

    def mm(self, wn, a, b, name):
        r0, _, direct = self.regions[wn]
        assert direct
        self.buf = matmul_tn_packed(a, b, self.buf, r0, wn in COL_SHARDED, name)

    def put(self, wn, full):
        r0, _, direct = self.regions[wn]
        assert not direct
        piece = _shard_major(wn, full, CHUNK_W)
        self.buf = lax.dynamic_update_slice(self.buf, piece, (0, r0, 0))

    def send(self):
        i = self.layer
        sib, = grads_to_sibling([self.buf], "grads_to_sibling_l%d" % i)
        t = pair_sum(self.buf, sib, BF16, "pair_sum_l%d" % i)
        self.flight = reduce_start(t, sib, "reduce_l%d_start" % i)
        return self.flight[4][0, 0]


def _small_pack(arrs, total_padded):
    flat = jnp.concatenate([a.reshape(-1) for a in arrs])
    return jnp.pad(flat, (0, total_padded - flat.shape[0]))


def kernel(x, positions, l0_norm_g, l0_w_in, l0_ln_g, l0_ln_b, l0_w_s, l0_b_s, l0_w_out, l1_norm_g, l1_w_in, l1_a_re, l1_a_im, l1_log_step, l1_b_re, l1_b_im, l1_c_re, l1_c_im, l1_d_skip, l1_w_glu, l1_b_glu, l1_w_out, l2_norm_g, l2_w_in, l2_q_norm_g, l2_w_uq, l2_kv_norm_g, l2_w_ukv, l2_w_out, l3_norm_g, l3_w_in, l3_ln_g, l3_ln_b, l3_w_s, l3_b_s, l3_w_out, final_norm_g, loss_target, m_l0_norm_g, m_l0_w_in, m_l0_ln_g, m_l0_ln_b, m_l0_w_s, m_l0_b_s, m_l0_w_out, m_l1_norm_g, m_l1_w_in, m_l1_a_re, m_l1_a_im, m_l1_log_step, m_l1_b_re, m_l1_b_im, m_l1_c_re, m_l1_c_im, m_l1_d_skip, m_l1_w_glu, m_l1_b_glu, m_l1_w_out, m_l2_norm_g, m_l2_w_in, m_l2_q_norm_g, m_l2_w_uq, m_l2_kv_norm_g, m_l2_w_ukv, m_l2_w_out, m_l3_norm_g, m_l3_w_in, m_l3_ln_g, m_l3_ln_b, m_l3_w_s, m_l3_b_s, m_l3_w_out, m_final_norm_g, v_l0_norm_g, v_l0_w_in, v_l0_ln_g, v_l0_ln_b, v_l0_w_s, v_l0_b_s, v_l0_w_out, v_l1_norm_g, v_l1_w_in, v_l1_a_re, v_l1_a_im, v_l1_log_step, v_l1_b_re, v_l1_b_im, v_l1_c_re, v_l1_c_im, v_l1_d_skip, v_l1_w_glu, v_l1_b_glu, v_l1_w_out, v_l2_norm_g, v_l2_w_in, v_l2_q_norm_g, v_l2_w_uq, v_l2_kv_norm_g, v_l2_w_ukv, v_l2_w_out, v_l3_norm_g, v_l3_w_in, v_l3_ln_g, v_l3_ln_b, v_l3_w_s, v_l3_b_s, v_l3_w_out, v_final_norm_g):
    args = locals()
    w = {n: args[n] for n in WEIGHT_NAMES}
    mom_m = {n: args["m_" + n] for n in WEIGHT_NAMES}
    mom_v = {n: args["v_" + n] for n in WEIGHT_NAMES}
    h0 = x[0]
    target = loss_target[0]
    pos = positions.reshape(-1, 1)

    full = {}

    def pack_unit(layers):
        names = [n for n in BIG if int(n[1]) in layers]
        rows = [w[n].size // PACK_W for n in names]
        pad = -sum(rows) % PACK_ROW_ALIGN
        return names, rows, _pack_rows([w[n].astype(BF16) for n in names] + [jnp.zeros((pad, PACK_W), BF16)])

    def unpack_unit(names, rows, gathered):
        r0 = 0
        for n, nr in zip(names, rows):
            full[n] = _from_shard_major(n, gathered[:, r0:r0 + nr, :], w[n].shape)
            r0 += nr

    unit0, unit1, unit2 = pack_unit((0,)), pack_unit((1,)), pack_unit((2, 3))
    wp = dict(w)

    def layer_params(i):
        pre = "l%d_" % i
        p = {k[len(pre):]: v for k, v in wp.items() if k.startswith(pre)}
        wf = {k[len(pre):]: v for k, v in full.items() if k.startswith(pre)}
        return p, wf

    flight = gather_start(unit0[2], unit1[2], "gather_l0_start")
    cos, sins = rope_tables(pos, flight[4][0, 0])
    wp["l1_a_re"] = w["l1_a_re"] + flight[4][0, 0]
    s5_weights = _s5_weights(layer_params(1)[0])
    land = gather_wait(*flight[:4], s5_weights[2], "gather_l0_wait")
    got = gather_handover(land, unit0[2], "gather_l0_handover")
    unpack_unit(unit0[0], unit0[1], got)
    flight = gather_start(unit1[2], got, "gather_l1_start")
    wp["l0_norm_g"] = w["l0_norm_g"] + flight[4][0, 0]

    h = h0
    saved = []
    for i, kind in enumerate(LAYER_KINDS):
        if i == 1:
            land = gather_wait(*flight[:4], h, "gather_l1_wait")
            got = gather_handover(land, unit1[2], "gather_l1_handover")
            unpack_unit(unit1[0], unit1[1], got)
            flight = gather_start(unit2[2], got, "gather_l23_start")
            wp["l1_norm_g"] = w["l1_norm_g"] + flight[4][0, 0]
        if i == 2:
            land = gather_wait(*flight[:4], h, "gather_l23_wait")
            unpack_unit(unit2[0], unit2[1], gather_handover(land, unit2[2], "gather_l23_handover"))
        p, wf = layer_params(i)
        tag = "l%d" % i
        if kind == "gmlp":
            h, s = gmlp_layer_fwd(h, p, wf, tag)
        elif kind == "s5":
            h, s = s5_layer_fwd(h, p, wf, s5_weights, tag)
        else:
            h, s = mla_layer_fwd(h, p, wf, cos, sins, tag)
        saved.append(s)
    loss_part, dh, g_final = loss_head(h, final_norm_g, target)

    grads = {"final_norm_g": g_final.reshape(-1)}
    sinks = {}

    for i in reversed(range(len(LAYER_KINDS))):
        kind = LAYER_KINDS[i]
        p, wf = layer_params(i)
        tag = "l%d" % i
        sink = sinks[i] = BigGradSink(i, {n[3:]: w[n].shape for n in BIG if int(n[1]) == i})
        if kind == "gmlp":
            dh, g = gmlp_layer_bwd(dh, saved[i], p, wf, tag, sink)
        elif kind == "s5":
            dh, g = s5_layer_bwd(dh, saved[i], p, wf, tag, sink)
        else:
            dh, g = mla_layer_bwd(dh, saved[i], p, wf, cos, sins, tag, sink)
        for k, val in g.items():
            grads["l%d_%s" % (i, k)] = val
    grad_x = dh[None]

    n_small = sum(w[n].size for n in SMALL)
    piece = N_CHIPS * 2 * 16 * PACK_W
    n_small_pad = -(-(n_small + 1) // piece) * piece
    nrs = n_small_pad // N_CHIPS // PACK_W
    p_small = _small_pack([grads[n] for n in SMALL] + [loss_part], n_small_pad).reshape(N_CHIPS, nrs, PACK_W)
    sib_small, = grads_to_sibling([p_small], "grads_to_sibling_small")
    t_small = pair_sum(p_small, sib_small, F32, "pair_sum_small")
    rb_small, = grads_across_chips([t_small])
    halves = [chip_sum(t_small, rb_small, "chip_sum_small")]

    after = halves[0]
    for i in reversed(range(len(LAYER_KINDS))):
        t_i, rb_i = reduce_wait(*sinks[i].flight[:4], after, "reduce_l%d_wait" % i)
        halves.append(chip_sum(t_i, rb_i, "chip_sum_l%d" % i))
        after = halves[-1]
    reduced = reduced_to_sibling(halves)
    small_flight = bcast_start(reduced[0], reduced[1], "small_allgather_start")

    g_out, d_out, nm_out, nv_out = {}, {}, {}, {}
    for i, g_i in zip(reversed(range(len(LAYER_KINDS))), reduced[1:]):
        for wn, (r0, shape, direct) in sinks[i].regions.items():
            n = "l%d_%s" % (i, wn)
            if direct:
                g_out[n], d_out[n], nm_out[n], nv_out[n] = adamw_packed(w[n], g_i, r0, mom_m[n], mom_v[n], "adamw_" + n)
            else:
                g_out[n] = g_i[r0:r0 + shape[0] * shape[1] // CHUNK_W].reshape(shape)
                d_out[n], nm_out[n], nv_out[n] = adamw(w[n], g_out[n], mom_m[n], mom_v[n], "adamw_" + n)
    small_all = bcast_wait(*small_flight[:4], nv_out["l0_w_in"], "small_allgather_wait")
    g_small = small_all.reshape(-1, PACK_W)
    sp = lambda d: _small_pack([d[n] for n in SMALL], n_small_pad).reshape(-1, PACK_W)
    d_small, nm_small, nv_small = adamw(sp(w), g_small, sp(mom_m), sp(mom_v), "adamw_small")
    for buf, out in ((g_small, g_out), (d_small, d_out), (nm_small, nm_out), (nv_small, nv_out)):
        flat = buf.reshape(-1)
        o = 0
        for n in SMALL:
            out[n] = flat[o:o + w[n].size].reshape(w[n].shape)
            o += w[n].size
    loss = g_small.reshape(-1)[n_small]
    return (loss, grad_x, *[g_out[n] for n in WEIGHT_NAMES], *[d_out[n] for n in WEIGHT_NAMES],
            *[nm_out[n] for n in WEIGHT_NAMES], *[nv_out[n] for n in WEIGHT_NAMES])
```

```python
import functools
import math

import jax
import jax.numpy as jnp
import numpy as np
from jax import lax
from jax.experimental import pallas as pl
from jax.experimental.pallas import tpu as pltpu

F32 = jnp.float32
BF16 = jnp.bfloat16
MESH = pl.DeviceIdType.MESH
VMEM_LIMIT_BYTES = 56 * 1024 * 1024
LANES = 128
PACK_W = 1024
CHUNK_W = 256
PACK_ROW_ALIGN = 256
ROW_TILE = 256
SUM_BLOCK_BYTES = 1024 * 1024
ADAMW_BLOCK_BYTES = 1024 * 1024
MM_BLOCK_BYTES = 6 * 1024 * 1024

NORM_EPS = 1e-6
N_CHIPS = 4
GMLP_CHUNK = 128
GMLP_GROUPS = 8
S5_GROUPS = 128
S5_GROUP = 16
S5_STATE = 64
S5_SB = 16
S5_SEG = 8
MLA_HEADS = 16
MLA_NOPE = 128
MLA_ROPE = 64
MLA_Q_RANK = 384
MLA_KV_RANK = 128
MLA_SCALE = (MLA_NOPE + MLA_ROPE) ** -0.5
ROPE_THETA = 10000.0
NEG_INF = -1e30
ADAM_LR, ADAM_B1, ADAM_B2, ADAM_EPS, ADAM_WD, ADAM_STEP = 0.001, 0.9, 0.999, 1e-08, 0.01, 10

DN_NN = (((1,), (0,)), ((), ()))
DN_NT = (((1,), (1,)), ((), ()))
DN_TN = (((0,), (0,)), ((), ()))


def _cparams(sem):
    return pltpu.CompilerParams(dimension_semantics=sem, vmem_limit_bytes=VMEM_LIMIT_BYTES)


def _pick(n, cands=(512, 384, 256, 128)):
    for c in cands:
        if n % c == 0:
            return c
    return n


def _pick_rows(r, cap=512, mult=16):
    return max(t for t in range(mult, cap + 1, mult) if r % t == 0)


def _dot(a, b, dn):
    return lax.dot_general(a.astype(BF16), b.astype(BF16), dn, preferred_element_type=F32)


def _sigmoid(x):
    return 0.5 + 0.5 * jnp.tanh(0.5 * x)


def _gelu(x):
    c = math.sqrt(2.0 / math.pi)
    t = jnp.tanh(c * (x + 0.044715 * x * x * x))
    return 0.5 * x * (1.0 + t)


def _gelu_grad(x):
    c = math.sqrt(2.0 / math.pi)
    t = jnp.tanh(c * (x + 0.044715 * x * x * x))
    return 0.5 * (1.0 + t) + 0.5 * x * (1.0 - t * t) * c * (1.0 + 3.0 * 0.044715 * x * x)


def _gelu_both(x):
    c = math.sqrt(2.0 / math.pi)
    t = jnp.tanh(c * (x + 0.044715 * x * x * x))
    return 0.5 * x * (1.0 + t), 0.5 * (1.0 + t) + 0.5 * x * (1.0 - t * t) * c * (1.0 + 3.0 * 0.044715 * x * x)


def _silu_both(z):
    s = _sigmoid(z)
    return z * s, s * (1.0 + z * (1.0 - s))


def _silu(z):
    return z * _sigmoid(z)


def matmul(a, b, mode, name, out_dtype=F32, add=None):
    if mode == "nn":
        (m, k), n = a.shape, b.shape[1]
    elif mode == "nt":
        (m, k), n = a.shape, b.shape[0]
    else:
        (k, m), n = a.shape, b.shape[1]
    tm = _pick(m, [t for t in (2048, 1024, 512, 384, 256, 128) if t * k * a.dtype.itemsize <= MM_BLOCK_BYTES])
    tn = _pick(n, [t for t in (512, 384, 256, 128) if t * k * b.dtype.itemsize <= MM_BLOCK_BYTES])
    dn = {"nn": DN_NN, "nt": DN_NT, "tn": DN_TN}[mode]

    def body(*refs):
        if add is None:
            a_ref, b_ref, o_ref = refs
        else:
            a_ref, b_ref, add_ref, o_ref = refs
        r = _dot(a_ref[...], b_ref[...], dn)
        if add is not None:
            r = r + add_ref[...].astype(F32)
        o_ref[...] = r.astype(out_dtype)

    a_spec = pl.BlockSpec((k, tm), lambda i, j: (0, i)) if mode == "tn" else pl.BlockSpec((tm, k), lambda i, j: (i, 0))
    b_spec = pl.BlockSpec((tn, k), lambda i, j: (j, 0)) if mode == "nt" else pl.BlockSpec((k, tn), lambda i, j: (0, j))
    o_spec = pl.BlockSpec((tm, tn), lambda i, j: (i, j))
    in_specs = [a_spec, b_spec] + ([o_spec] if add is not None else [])
    args = (a, b) + ((add,) if add is not None else ())
    return pl.pallas_call(
        body, name=name, grid=(m // tm, n // tn), in_specs=in_specs, out_specs=o_spec,
        out_shape=jax.ShapeDtypeStruct((m, n), out_dtype),
        compiler_params=_cparams(("parallel", "arbitrary")))(*args)


def _tile_rows(r, r0, cands=(512, 384, 256, 128)):
    return next(t for t in cands if r % t == 0 and r0 % t == 0)


def matmul_tn_packed(a, b, buf, r0, col_sharded, name):
    k, m = a.shape
    n = b.shape[1]
    if col_sharded:
        chunks = n // N_CHIPS // CHUNK_W
        tm = _tile_rows(m, r0, (1024, 512, 384, 256, 128))
        o_map = lambda i, j: (j // chunks, (r0 + (j % chunks) * m) // tm + i, 0)
    else:
        rs = m // N_CHIPS
        tm = _tile_rows(rs, r0)
        per = rs // tm
        o_map = lambda i, j: (i // per, (r0 + j * rs) // tm + i % per, 0)

    def body(a_ref, b_ref, buf_ref, o_ref):
        o_ref[0] = _dot(a_ref[...], b_ref[...], DN_TN)

    return pl.pallas_call(
        body, name=name, grid=(m // tm, n // CHUNK_W),
        in_specs=[pl.BlockSpec((k, tm), lambda i, j: (0, i)), pl.BlockSpec((k, CHUNK_W), lambda i, j: (0, j)),
                  pl.BlockSpec(memory_space=pl.ANY)],
        out_specs=pl.BlockSpec((1, tm, CHUNK_W), o_map), out_shape=jax.ShapeDtypeStruct(buf.shape, buf.dtype),
        input_output_aliases={2: 0}, compiler_params=_cparams(("parallel", "arbitrary")))(a, b, buf)


def _rows(tl, w, col=0):
    return pl.BlockSpec((tl, w), lambda i: (i, col))


def _full(shape):
    nd = len(shape)
    return pl.BlockSpec(tuple(shape), lambda i: (0,) * nd)


def _rowcall(body, name, n_steps, in_specs, out_specs, out_shape, scratch=()):
    return pl.pallas_call(
        body, name=name, grid=(n_steps,), in_specs=in_specs, out_specs=out_specs, out_shape=out_shape,
        scratch_shapes=list(scratch), compiler_params=_cparams(("arbitrary",)))


def _acc(ref, val, i):
    @pl.when(i == 0)
    def _():
        ref[...] = val

    @pl.when(i != 0)
    def _():
        ref[...] += val


def rms_fwd(h, g, name):
    l, d = h.shape
    tl = ROW_TILE

    def body(h_ref, g_ref, o_ref):
        x = h_ref[...]
        r = lax.rsqrt(jnp.mean(x * x, axis=-1, keepdims=True) + NORM_EPS)
        o_ref[...] = (x * r * g_ref[...]).astype(BF16)

    return _rowcall(body, name, l // tl, [_rows(tl, d), _full((1, d))], _rows(tl, d),
                    jax.ShapeDtypeStruct((l, d), BF16))(h, g.reshape(1, d))


def rms_bwd(h, g, dhn, dh_in, name):
    l, d = h.shape
    tl = ROW_TILE

    def body(h_ref, g_ref, dhn_ref, dhi_ref, dh_ref, dg_ref):
        i = pl.program_id(0)
        x = h_ref[...]
        r = lax.rsqrt(jnp.mean(x * x, axis=-1, keepdims=True) + NORM_EPS)
        xhat = x * r
        dy = dhn_ref[...]
        dxh = dy * g_ref[...]
        dx = r * (dxh - xhat * jnp.mean(dxh * xhat, axis=-1, keepdims=True))
        dh_ref[...] = dhi_ref[...] + dx
        _acc(dg_ref, jnp.sum(dy * xhat, axis=0, keepdims=True), i)

    return _rowcall(body, name, l // tl, [_rows(tl, d), _full((1, d)), _rows(tl, d), _rows(tl, d)],
                    [_rows(tl, d), _full((1, d))],
                    [jax.ShapeDtypeStruct((l, d), F32), jax.ShapeDtypeStruct((1, d), F32)])(h, g.reshape(1, d), dhn, dh_in)


def loss_head(h, g, target):
    l, d = h.shape
    tl = ROW_TILE

    def body(h_ref, g_ref, t_ref, loss_ref, dh_ref, dg_ref):
        i = pl.program_id(0)
        x = h_ref[...]
        gg = g_ref[...]
        r = lax.rsqrt(jnp.mean(x * x, axis=-1, keepdims=True) + NORM_EPS)
        xhat = x * r
        err = xhat * gg - t_ref[...]
        part = 0.5 * jnp.sum(jnp.mean(err * err, axis=-1, keepdims=True), axis=0, keepdims=True)
        _acc(loss_ref, part, i)
        dy = err * (1.0 / d)
        dxh = dy * gg
        dh_ref[...] = r * (dxh - xhat * jnp.mean(dxh * xhat, axis=-1, keepdims=True))
        _acc(dg_ref, jnp.sum(dy * xhat, axis=0, keepdims=True), i)

    return _rowcall(body, "loss_head", l // tl, [_rows(tl, d), _full((1, d)), _rows(tl, d)],
                    [_full((1, 1)), _rows(tl, d), _full((1, d))],
                    [jax.ShapeDtypeStruct((1, 1), F32), jax.ShapeDtypeStruct((l, d), F32),
                     jax.ShapeDtypeStruct((1, d), F32)])(h, g.reshape(1, d), target)


def _gmlp_common(a_ref, lng_ref, lnb_ref):
    di = lng_ref.shape[1]
    u_pre = a_ref[:, 0:di]
    v_pre = a_ref[:, di:2 * di]
    z = a_ref[:, 2 * di:3 * di]
    vg = _gelu(v_pre)
    mu = jnp.mean(vg, axis=-1, keepdims=True)
    xc = vg - mu
    rstd = lax.rsqrt(jnp.mean(xc * xc, axis=-1, keepdims=True) + NORM_EPS)
    vhat = xc * rstd
    vn = vhat * lng_ref[...] + lnb_ref[...]
    return u_pre, v_pre, z, vhat, rstd, vn


def _tril(w):
    r = lax.broadcasted_iota(jnp.int32, w.shape, 0)
    c = lax.broadcasted_iota(jnp.int32, w.shape, 1)
    return jnp.where(c <= r, w, 0.0)


def gmlp_gate_fwd(a, ln_g, ln_b, w_s, b_s, name):
    l, w3 = a.shape
    di = w3 // 3
    dg = di // GMLP_GROUPS
    tl = GMLP_CHUNK

    def body(a_ref, lng_ref, lnb_ref, ws_ref, bs_ref, m_ref):
        u_pre, _, z, _, _, vn = _gmlp_common(a_ref, lng_ref, lnb_ref)
        gate = _gelu(u_pre) * _silu(z)
        for g in range(GMLP_GROUPS):
            sl = slice(g * dg, (g + 1) * dg)
            s = _dot(_tril(ws_ref[g]), vn[:, sl], DN_NN) + bs_ref[g]
            m_ref[:, sl] = (gate[:, sl] * s).astype(BF16)

    return _rowcall(body, name, l // tl,
                    [_rows(tl, w3), _full((1, di)), _full((1, di)), _full(w_s.shape), _full((GMLP_GROUPS, tl, 1))],
                    _rows(tl, di), jax.ShapeDtypeStruct((l, di), BF16))(
        a, ln_g.reshape(1, di), ln_b.reshape(1, di), w_s, b_s.reshape(GMLP_GROUPS, tl, 1))


def gmlp_gate_bwd(a, dm, ln_g, ln_b, w_s, b_s, name):
    l, w3 = a.shape
    di = w3 // 3
    dg = di // GMLP_GROUPS
    tl = GMLP_CHUNK

    def body(a_ref, dm_ref, lng_ref, lnb_ref, ws_ref, bs_ref, da_ref, dlg_ref, dlb_ref, dws_ref, dbs_ref,
             dvn_ref, vh_ref, gv_ref):
        i = pl.program_id(0)
        vg, gv = _gelu_both(a_ref[:, di:2 * di])
        gv_ref[...] = gv
        xc = vg - jnp.mean(vg, axis=-1, keepdims=True)
        rstd = lax.rsqrt(jnp.mean(xc * xc, axis=-1, keepdims=True) + NORM_EPS)
        vh_ref[...] = xc * rstd
        for g in range(GMLP_GROUPS):
            sl = slice(g * dg, (g + 1) * dg)
            wt = _tril(ws_ref[g])
            vn_g = vh_ref[:, sl] * lng_ref[:, sl] + lnb_ref[:, sl]
            s = _dot(wt, vn_g, DN_NN) + bs_ref[g]
            dmg = dm_ref[:, sl]
            u, gu = _gelu_both(a_ref[:, sl])
            sz, gz = _silu_both(a_ref[:, 2 * di + g * dg:2 * di + (g + 1) * dg])
            ds = dmg * u * sz
            da_ref[:, sl] = (dmg * s * sz * gu).astype(BF16)
            da_ref[:, 2 * di + g * dg:2 * di + (g + 1) * dg] = (dmg * u * s * gz).astype(BF16)
            dvn_ref[:, sl] = _dot(wt, ds, DN_TN)
            dw = _tril(_dot(ds, vn_g, DN_NT))
            db = jnp.sum(ds, axis=1, keepdims=True)

            @pl.when(i == 0)
            def _():
                dws_ref[g] = dw
                dbs_ref[g] = db

            @pl.when(i != 0)
            def _():
                dws_ref[g] += dw
                dbs_ref[g] += db

        dvn = dvn_ref[...]
        vhat = vh_ref[...]
        dxh = dvn * lng_ref[...]
        dvg = rstd * (dxh - jnp.mean(dxh, axis=-1, keepdims=True) - vhat * jnp.mean(dxh * vhat, axis=-1, keepdims=True))
        da_ref[:, di:2 * di] = (dvg * gv_ref[...]).astype(BF16)
        _acc(dlg_ref, jnp.sum(dvn * vhat, axis=0, keepdims=True), i)
        _acc(dlb_ref, jnp.sum(dvn, axis=0, keepdims=True), i)

    outs = _rowcall(
        body, name, l // tl,
        [_rows(tl, w3), _rows(tl, di), _full((1, di)), _full((1, di)), _full(w_s.shape), _full((GMLP_GROUPS, tl, 1))],
        [_rows(tl, w3), _full((1, di)), _full((1, di)), _full(w_s.shape), _full((GMLP_GROUPS, tl, 1))],
        [jax.ShapeDtypeStruct((l, w3), BF16), jax.ShapeDtypeStruct((1, di), F32), jax.ShapeDtypeStruct((1, di), F32),
         jax.ShapeDtypeStruct(w_s.shape, F32), jax.ShapeDtypeStruct((GMLP_GROUPS, tl, 1), F32)],
        scratch=[pltpu.VMEM((tl, di), F32)] * 3)(
        a, dm, ln_g.reshape(1, di), ln_b.reshape(1, di), w_s, b_s.reshape(GMLP_GROUPS, tl, 1))
    return outs


def gmlp_layer_fwd(h, p, wf, tag):
    hn = rms_fwd(h, p["norm_g"], tag + "_rms")
    a = matmul(hn, wf["w_in"], "nn", tag + "_mm_in")
    m = gmlp_gate_fwd(a, p["ln_g"], p["ln_b"], p["w_s"], p["b_s"], tag + "_gate")
    h_out = matmul(m, wf["w_out"], "nn", tag + "_mm_out", add=h)
    return h_out, (h, hn, a, m)


def gmlp_layer_bwd(dh_out, saved, p, wf, tag, sink):
    h, hn, a, m = saved
    dm = matmul(dh_out, wf["w_out"], "nt", tag + "_mm_dm")
    sink.mm("w_out", m, dh_out, tag + "_mm_gwout")
    da, dlg, dlb, dws, dbs = gmlp_gate_bwd(a, dm, p["ln_g"], p["ln_b"], p["w_s"], p["b_s"], tag + "_gate_bwd")
    dhn = matmul(da, wf["w_in"], "nt", tag + "_mm_dhn")
    sink.mm("w_in", hn, da, tag + "_mm_gwin")
    zero = sink.send()
    dh, dng = rms_bwd(h, p["norm_g"] + zero, dhn, dh_out, tag + "_rms_bwd")
    grads = {"norm_g": dng.reshape(-1), "ln_g": dlg.reshape(-1), "ln_b": dlb.reshape(-1),
             "w_s": dws, "b_s": dbs.reshape(GMLP_GROUPS, GMLP_CHUNK)}
    return dh, grads


def _cmul(ar, ai, br, bi):
    return ar * br - ai * bi, ar * bi + ai * br


S5_PG = 16


def _gblock(tail):
    return pl.BlockSpec((S5_PG,) + tuple(tail), lambda i: (i, 0, 0))


def s5_params_fwd(a_re, a_im, log_step, b_re, b_im):
    g, p, hh = b_re.shape

    def body(ar_ref, ai_ref, ls_ref, br_ref, bi_ref, lr_ref, li_ref, bbr_ref, bbi_ref):
        ar, ai = ar_ref[...], ai_ref[...]
        step = jnp.exp(ls_ref[...])
        mag = jnp.exp(ar * step)
        lr, li = mag * jnp.cos(ai * step), mag * jnp.sin(ai * step)
        den = 1.0 / (ar * ar + ai * ai)
        fr, fi = _cmul(lr - 1.0, li, ar * den, -ai * den)
        lr_ref[...] = lr
        li_ref[...] = li
        bbr, bbi = _cmul(fr, fi, br_ref[...], bi_ref[...])
        bbr_ref[...] = bbr
        bbi_ref[...] = bbi

    s1 = jax.ShapeDtypeStruct((g, p, 1), F32)
    s3 = jax.ShapeDtypeStruct((g, p, hh), F32)
    b1, b0, b3 = _gblock((p, 1)), _gblock((1, 1)), _gblock((p, hh))
    return pl.pallas_call(body, name="s5_params_fwd", grid=(g // S5_PG,), in_specs=[b1, b1, b0, b3, b3],
                          out_specs=[b1, b1, b3, b3], out_shape=[s1, s1, s3, s3],
                          compiler_params=_cparams(("parallel",)))(
        a_re.reshape(g, p, 1), a_im.reshape(g, p, 1), log_step.reshape(g, 1, 1), b_re, b_im)


def s5_params_bwd(a_re, a_im, log_step, b_re, b_im, dl_re, dl_im, dbb_re, dbb_im):
    g, p, hh = b_re.shape

    def body(ar_ref, ai_ref, ls_ref, br_ref, bi_ref, dlr_ref, dli_ref, dbr_ref, dbi_ref,
             gar_ref, gai_ref, gls_ref, gbr_ref, gbi_ref):
        ar, ai = ar_ref[...], ai_ref[...]
        step = jnp.exp(ls_ref[...])
        mag = jnp.exp(ar * step)
        lr, li = mag * jnp.cos(ai * step), mag * jnp.sin(ai * step)
        den = 1.0 / (ar * ar + ai * ai)
        ir, ii = ar * den, -ai * den
        fr, fi = _cmul(lr - 1.0, li, ir, ii)
        br, bi = br_ref[...], bi_ref[...]
        dbr, dbi = dbr_ref[...], dbi_ref[...]
        gbr, gbi = _cmul(fr, -fi, dbr, dbi)
        gbr_ref[...] = gbr
        gbi_ref[...] = gbi
        pr, pi = _cmul(br, -bi, dbr, dbi)
        gfr = jnp.sum(pr, axis=-1, keepdims=True)
        gfi = jnp.sum(pi, axis=-1, keepdims=True)
        t_r, t_i = _cmul(ir, -ii, gfr, gfi)
        glr, gli = dlr_ref[...] + t_r, dli_ref[...] + t_i
        c1r, c1i = _cmul(step * lr, -step * li, glr, gli)
        qr, qi = _cmul(fr, fi, ir, ii)
        c2r, c2i = _cmul(-qr, qi, gfr, gfi)
        gar_ref[...] = c1r + c2r
        gai_ref[...] = c1i + c2i
        wr, wi = _cmul(ar, ai, lr, li)
        sr, _ = _cmul(wr, -wi, glr, gli)
        gls_ref[...] = jnp.sum(sr, axis=1, keepdims=True) * step

    s1 = jax.ShapeDtypeStruct((g, p, 1), F32)
    s3 = jax.ShapeDtypeStruct((g, p, hh), F32)
    b1, b0, b3 = _gblock((p, 1)), _gblock((1, 1)), _gblock((p, hh))
    return pl.pallas_call(body, name="s5_params_bwd", grid=(g // S5_PG,),
                          in_specs=[b1, b1, b0, b3, b3, b1, b1, b3, b3], out_specs=[b1, b1, b0, b3, b3],
                          out_shape=[s1, s1, jax.ShapeDtypeStruct((g, 1, 1), F32), s3, s3],
                          compiler_params=_cparams(("parallel",)))(
        a_re.reshape(g, p, 1), a_im.reshape(g, p, 1), log_step.reshape(g, 1, 1), b_re, b_im,
        dl_re, dl_im, dbb_re, dbb_im)


def _blockdiag(t):
    sb, n, r, c = t.shape
    eye = jnp.eye(n, dtype=bool)[None, :, None, :, None]
    full = jnp.where(eye, t[:, :, :, None, :], jnp.zeros((), t.dtype))
    return full.reshape(sb, n * r, n * c)


def _blockdiag_extract(m, r, c):
    sb = m.shape[0]
    n = m.shape[1] // r
    m5 = m.reshape(sb, n, r, n, c)
    return jnp.stack([m5[:, i, :, i, :] for i in range(n)], axis=1)


S5_TB = 64
S5_UNROLL = 8


def _lam_power(pr, pi, n):
    for _ in range(int(math.log2(n))):
        pr, pi = _cmul(pr, pi, pr, pi)
    return pr, pi


def _segment_entries(er, ei, pr, pi, reverse):
    seg, ns = er.shape
    row = lax.broadcasted_iota(jnp.int32, (seg, ns), 0)
    cr = jnp.zeros((seg, ns), F32)
    ci = jnp.zeros((seg, ns), F32)
    cur_r = jnp.zeros((1, ns), F32)
    cur_i = jnp.zeros((1, ns), F32)
    for s in (range(seg - 2, -1, -1) if reverse else range(1, seg)):
        src = s + 1 if reverse else s - 1
        mr, mi = _cmul(pr, pi, cur_r, cur_i)
        cur_r = jnp.sum(jnp.where(row == src, er, 0.0), axis=0, keepdims=True) + mr
        cur_i = jnp.sum(jnp.where(row == src, ei, 0.0), axis=0, keepdims=True) + mi
        cr = jnp.where(row == s, cur_r, cr)
        ci = jnp.where(row == s, cur_i, ci)
    return cr, ci


def s5_scan_fused_fwd(a_p, lam_re, lam_im, wb_re, wb_im, wc_re, wc_im, d_skip, name):
    l = a_p.shape[0]
    di = d_skip.shape[1]
    rows = S5_SEG * S5_TB
    nb = l // rows
    ns = wb_re.shape[2]

    def body(u_ref, lr_ref, li_ref, wbr_ref, wbi_ref, wcr_ref, wci_ref, ds_ref, y_ref, ckr_ref, cki_ref, bur, bui):
        lr = jnp.broadcast_to(lr_ref[0], (S5_SEG, ns))
        li = jnp.broadcast_to(li_ref[0], (S5_SEG, ns))

        def scan_block(b, carry, keep):
            def step(t, c):
                xr, xi = c
                sl = pl.ds(pl.multiple_of(b * rows + t * S5_SEG, S5_SEG), S5_SEG)
                nr = lr * xr - li * xi + bur[sl, :]
                ni = lr * xi + li * xr + bui[sl, :]
                if keep:
                    bur[sl, :] = nr
                    bui[sl, :] = ni
                return nr, ni

            return lax.fori_loop(0, S5_TB, step, carry, unroll=S5_UNROLL)

        def project(b, carry):
            rs = pl.ds(pl.multiple_of(b * rows, rows), rows)
            u = u_ref[rs, :]
            bur[rs, :] = _dot(u, wbr_ref[0], DN_NN)
            bui[rs, :] = _dot(u, wbi_ref[0], DN_NN)
            return scan_block(b, carry, False)

        zero = jnp.zeros((S5_SEG, ns), F32)
        er, ei = lax.fori_loop(0, nb, project, (zero, zero))
        pr, pi = _lam_power(lr_ref[0], li_ref[0], l // S5_SEG)
        entry = _segment_entries(er, ei, pr, pi, False)

        def emit(b, carry):
            ckr_ref[0, b] = carry[0]
            cki_ref[0, b] = carry[1]
            carry = scan_block(b, carry, True)
            rs = pl.ds(pl.multiple_of(b * rows, rows), rows)
            y_ref[rs, :] = (_dot(bur[rs, :], wcr_ref[0], DN_NN) - _dot(bui[rs, :], wci_ref[0], DN_NN)
                            + ds_ref[...] * u_ref[rs, :])
            return carry

        lax.fori_loop(0, nb, emit, entry)

    sb3 = lambda s: (s, 0, 0)
    st = jax.ShapeDtypeStruct
    return pl.pallas_call(
        body, name=name, grid=(S5_SB,),
        in_specs=[pl.BlockSpec((l, LANES), lambda s: (0, s)),
                  pl.BlockSpec((1, 1, ns), sb3), pl.BlockSpec((1, 1, ns), sb3),
                  pl.BlockSpec((1, LANES, ns), sb3), pl.BlockSpec((1, LANES, ns), sb3),
                  pl.BlockSpec((1, ns, LANES), sb3), pl.BlockSpec((1, ns, LANES), sb3),
                  pl.BlockSpec((1, LANES), lambda s: (0, s))],
        out_specs=[pl.BlockSpec((l, LANES), lambda s: (0, s)),
                   pl.BlockSpec((1, nb, S5_SEG, ns), lambda s: (s, 0, 0, 0)),
                   pl.BlockSpec((1, nb, S5_SEG, ns), lambda s: (s, 0, 0, 0))],
        out_shape=[st((l, di), F32), st((S5_SB, nb, S5_SEG, ns), F32), st((S5_SB, nb, S5_SEG, ns), F32)],
        scratch_shapes=[pltpu.VMEM((l, ns), F32), pltpu.VMEM((l, ns), F32)],
        compiler_params=_cparams(("parallel",)))(a_p, lam_re, lam_im, wb_re, wb_im, wc_re, wc_im, d_skip)


def s5_scan_fused_bwd(a_p, dy, lam_re, lam_im, wb_re, wb_im, wc_re, wc_im, d_skip, ck_re, ck_im, name):
    l = a_p.shape[0]
    di = d_skip.shape[1]
    rows = S5_SEG * S5_TB
    nb = l // rows
    ns = wb_re.shape[2]

    def body(u_ref, dy_ref, lr_ref, li_ref, wbr_ref, wbi_ref, wcr_ref, wci_ref, ds_ref, ckr_ref, cki_ref,
             du_ref, dwbr_ref, dwbi_ref, dwcr_ref, dwci_ref, dds_ref, dlr_ref, dli_ref, gr, gi, xr_b, xi_b):
        lr = jnp.broadcast_to(lr_ref[0], (S5_SEG, ns))
        li = jnp.broadcast_to(li_ref[0], (S5_SEG, ns))

        def back_project(k, carry):
            b = nb - 1 - k
            rs = pl.ds(pl.multiple_of(b * rows, rows), rows)
            dyv = dy_ref[rs, :]
            gr[rs, :] = _dot(dyv, wcr_ref[0], DN_NT)
            gi[rs, :] = -_dot(dyv, wci_ref[0], DN_NT)

            def step(kk, c):
                ar, ai = c
                sl = pl.ds(pl.multiple_of(b * rows + (S5_TB - 1 - kk) * S5_SEG, S5_SEG), S5_SEG)
                return gr[sl, :] + lr * ar + li * ai, gi[sl, :] + lr * ai - li * ar

            return lax.fori_loop(0, S5_TB, step, carry, unroll=S5_UNROLL)

        zero = jnp.zeros((S5_SEG, ns), F32)
        er, ei = lax.fori_loop(0, nb, back_project, (zero, zero))
        pr, pi = _lam_power(lr_ref[0], -li_ref[0], l // S5_SEG)
        a0r, a0i = _segment_entries(er, ei, pr, pi, True)

        dwbr_ref[...] = jnp.zeros_like(dwbr_ref)
        dwbi_ref[...] = jnp.zeros_like(dwbi_ref)
        dwcr_ref[...] = jnp.zeros_like(dwcr_ref)
        dwci_ref[...] = jnp.zeros_like(dwci_ref)
        dds_ref[...] = jnp.zeros_like(dds_ref)

        def block(k, carry):
            b = nb - 1 - k
            rs = pl.ds(pl.multiple_of(b * rows, rows), rows)
            u = u_ref[rs, :]
            dyv = dy_ref[rs, :]
            body_rows = pl.ds(S5_SEG, rows)
            x0r, x0i = ckr_ref[0, b], cki_ref[0, b]
            xr_b[0:S5_SEG, :] = x0r
            xi_b[0:S5_SEG, :] = x0i
            xr_b[body_rows, :] = _dot(u, wbr_ref[0], DN_NN)
            xi_b[body_rows, :] = _dot(u, wbi_ref[0], DN_NN)

            def fstep(t, c):
                xr, xi = c
                sl = pl.ds(pl.multiple_of((t + 1) * S5_SEG, S5_SEG), S5_SEG)
                nr = lr * xr - li * xi + xr_b[sl, :]
                ni = lr * xi + li * xr + xi_b[sl, :]
                xr_b[sl, :] = nr
                xi_b[sl, :] = ni
                return nr, ni

            lax.fori_loop(0, S5_TB, fstep, (x0r, x0i), unroll=S5_UNROLL)
            dwcr_ref[0] += _dot(xr_b[body_rows, :], dyv, DN_TN)
            dwci_ref[0] -= _dot(xi_b[body_rows, :], dyv, DN_TN)

            def bstep(kk, c):
                ar, ai = c
                sl = pl.ds(pl.multiple_of(b * rows + (S5_TB - 1 - kk) * S5_SEG, S5_SEG), S5_SEG)
                nr = gr[sl, :] + lr * ar + li * ai
                ni = gi[sl, :] + lr * ai - li * ar
                gr[sl, :] = nr
                gi[sl, :] = ni
                return nr, ni

            ar, ai = lax.fori_loop(0, S5_TB, bstep, carry[:2], unroll=S5_UNROLL)
            a_r, a_i = gr[rs, :], gi[rs, :]
            p_r, p_i = xr_b[0:rows, :], xi_b[0:rows, :]
            per_seg = lambda v: jnp.sum(v.reshape(S5_TB, S5_SEG, ns), axis=0)
            carry = (ar, ai, carry[2] + per_seg(a_r * p_r + a_i * p_i), carry[3] + per_seg(a_i * p_r - a_r * p_i))
            du_ref[rs, :] = (_dot(a_r, wbr_ref[0], DN_NT) + _dot(a_i, wbi_ref[0], DN_NT) + ds_ref[...] * dyv).astype(BF16)
            dwbr_ref[0] += _dot(u, a_r, DN_TN)
            dwbi_ref[0] += _dot(u, a_i, DN_TN)
            dds_ref[...] += jnp.sum(dyv * u, axis=0, keepdims=True)
            return carry

        _, _, dlr, dli = lax.fori_loop(0, nb, block, (a0r, a0i, zero, zero))
        dlr_ref[0] = dlr
        dli_ref[0] = dli

    sb3 = lambda s: (s, 0, 0)
    seq = pl.BlockSpec((l, LANES), lambda s: (0, s))
    ck = pl.BlockSpec((1, nb, S5_SEG, ns), lambda s: (s, 0, 0, 0))
    st = jax.ShapeDtypeStruct
    return pl.pallas_call(
        body, name=name, grid=(S5_SB,),
        in_specs=[seq, seq, pl.BlockSpec((1, 1, ns), sb3), pl.BlockSpec((1, 1, ns), sb3),
                  pl.BlockSpec((1, LANES, ns), sb3), pl.BlockSpec((1, LANES, ns), sb3),
                  pl.BlockSpec((1, ns, LANES), sb3), pl.BlockSpec((1, ns, LANES), sb3),
                  pl.BlockSpec((1, LANES), lambda s: (0, s)), ck, ck],
        out_specs=[seq, pl.BlockSpec((1, LANES, ns), sb3), pl.BlockSpec((1, LANES, ns), sb3),
                   pl.BlockSpec((1, ns, LANES), sb3), pl.BlockSpec((1, ns, LANES), sb3),
                   pl.BlockSpec((1, LANES), lambda s: (0, s)),
                   pl.BlockSpec((1, S5_SEG, ns), sb3), pl.BlockSpec((1, S5_SEG, ns), sb3)],
        out_shape=[st((l, di), BF16), st((S5_SB, LANES, ns), F32), st((S5_SB, LANES, ns), F32),
                   st((S5_SB, ns, LANES), F32), st((S5_SB, ns, LANES), F32), st((1, di), F32),
                   st((S5_SB, S5_SEG, ns), F32), st((S5_SB, S5_SEG, ns), F32)],
        scratch_shapes=[pltpu.VMEM((l, ns), F32), pltpu.VMEM((l, ns), F32),
                        pltpu.VMEM((rows + S5_SEG, ns), F32), pltpu.VMEM((rows + S5_SEG, ns), F32)],
        compiler_params=_cparams(("parallel",)))(
        a_p, dy, lam_re, lam_im, wb_re, wb_im, wc_re, wc_im, d_skip, ck_re, ck_im)


def s5_act(y, name):
    l, d = y.shape
    tl = ROW_TILE

    def body(y_ref, o_ref):
        o_ref[...] = _gelu(y_ref[...]).astype(BF16)

    return _rowcall(body, name, l // tl, [_rows(tl, d)], _rows(tl, d), jax.ShapeDtypeStruct((l, d), BF16))(y)


def s5_gate_fwd(y, t, b_glu, a_p, name):
    l, d = y.shape
    tl = ROW_TILE

    def body(y_ref, t_ref, b_ref, z_ref, m_ref):
        yg = _gelu(y_ref[...])
        m_ref[...] = (yg * _sigmoid(t_ref[...] + b_ref[...]) * _silu(z_ref[...])).astype(BF16)

    return _rowcall(body, name, l // tl, [_rows(tl, d), _rows(tl, d), _full((1, d)), _rows(tl, d, 1)], _rows(tl, d),
                    jax.ShapeDtypeStruct((l, d), BF16))(y, t, b_glu.reshape(1, d), a_p)


def s5_gate_bwd(dm, y, t, b_glu, a_p, name):
    l, d = y.shape
    tl = ROW_TILE

    def body(dm_ref, y_ref, t_ref, b_ref, z_ref, dt_ref, dyg_ref, dz_ref, db_ref):
        i = pl.program_id(0)
        dmv = dm_ref[...]
        z = z_ref[...]
        yg = _gelu(y_ref[...])
        sg = _sigmoid(t_ref[...] + b_ref[...])
        y2 = yg * sg
        sz, gz = _silu_both(z)
        dy2 = dmv * sz
        dz_ref[...] = (dmv * y2 * gz).astype(BF16)
        dyg_ref[...] = dy2 * sg
        dt = dy2 * yg * sg * (1.0 - sg)
        dt_ref[...] = dt.astype(BF16)
        _acc(db_ref, jnp.sum(dt, axis=0, keepdims=True), i)

    st = jax.ShapeDtypeStruct
    return _rowcall(body, name, l // tl, [_rows(tl, d), _rows(tl, d), _rows(tl, d), _full((1, d)), _rows(tl, d, 1)],
                    [_rows(tl, d), _rows(tl, d), _rows(tl, d), _full((1, d))],
                    [st((l, d), BF16), st((l, d), F32), st((l, d), BF16), st((1, d), F32)])(
        dm, y, t, b_glu.reshape(1, d), a_p)


def s5_act_bwd(y, dyg_a, dyg_b, name):
    l, d = y.shape
    tl = ROW_TILE

    def body(y_ref, a_ref, b_ref, o_ref):
        o_ref[...] = (a_ref[...] + b_ref[...]) * _gelu_grad(y_ref[...])

    return _rowcall(body, name, l // tl, [_rows(tl, d)] * 3, _rows(tl, d), jax.ShapeDtypeStruct((l, d), F32))(y, dyg_a, dyg_b)


def _seg_perm(t):
    l, d = t.shape
    return t.reshape(S5_SEG, l // S5_SEG, d).transpose(1, 0, 2).reshape(l, d)


def _seg_unperm(t):
    l, d = t.shape
    return t.reshape(l // S5_SEG, S5_SEG, d).transpose(1, 0, 2).reshape(l, d)


def _s5_weights(p):
    lr, li, bbr, bbi = s5_params_fwd(p["a_re"], p["a_im"], p["log_step"], p["b_re"], p["b_im"])
    ns = 8 * S5_STATE
    lam_re = lr.reshape(S5_SB, 1, ns)
    lam_im = li.reshape(S5_SB, 1, ns)
    to_bd = lambda t: _blockdiag(t.reshape(S5_SB, 8, t.shape[1], t.shape[2]))
    wb_re = to_bd(bbr.transpose(0, 2, 1)).astype(BF16)
    wb_im = to_bd(bbi.transpose(0, 2, 1)).astype(BF16)
    wc_re = to_bd(p["c_re"].transpose(0, 2, 1)).astype(BF16)
    wc_im = to_bd(p["c_im"].transpose(0, 2, 1)).astype(BF16)
    return lam_re, lam_im, wb_re, wb_im, wc_re, wc_im


def s5_layer_fwd(h, p, wf, sw, tag):
    l = h.shape[0]
    di = p["d_skip"].shape[0]
    hn = rms_fwd(h, p["norm_g"], tag + "_rms")
    hn_p = _seg_perm(hn)
    a_p = matmul(hn_p, wf["w_in"], "nn", tag + "_mm_in")
    dsk = p["d_skip"].reshape(1, di)
    y, ck_re, ck_im = s5_scan_fused_fwd(a_p, *sw, dsk, tag + "_scan")
    yg = s5_act(y, tag + "_act")
    t = matmul(yg, wf["w_glu"], "nn", tag + "_mm_glu")
    m = s5_gate_fwd(y, t, p["b_glu"], a_p, tag + "_gate")
    out_p = matmul(m, wf["w_out"], "nn", tag + "_mm_out")
    h_out = residual_add(h, _seg_unperm(out_p), tag + "_res")
    return h_out, (h, hn_p, a_p, sw, ck_re, ck_im, y, yg, t, m)


def residual_add(h, y, name):
    l, d = h.shape
    tl = ROW_TILE

    def body(h_ref, y_ref, o_ref):
        o_ref[...] = h_ref[...] + y_ref[...]

    return _rowcall(body, name, l // tl, [_rows(tl, d)] * 2, _rows(tl, d), jax.ShapeDtypeStruct((l, d), F32))(h, y)


def s5_layer_bwd(dh_out, saved, p, wf, tag, sink):
    h, hn_p, a_p, sw, ck_re, ck_im, y, yg, t, m = saved
    l = h.shape[0]
    di = p["d_skip"].shape[0]
    dsk = p["d_skip"].reshape(1, di)
    dout_p = _seg_perm(dh_out)
    dm = matmul(dout_p, wf["w_out"], "nt", tag + "_mm_dm")
    sink.mm("w_out", m, dout_p, tag + "_mm_gwout")
    dt, dyg_a, dz, db_glu = s5_gate_bwd(dm, y, t, p["b_glu"], a_p, tag + "_gate_bwd")
    dyg_b = matmul(dt, wf["w_glu"], "nt", tag + "_mm_dyg")
    sink.mm("w_glu", yg, dt, tag + "_mm_gwglu")
    dy = s5_act_bwd(y, dyg_a, dyg_b, tag + "_act_bwd")
    du, dwbr, dwbi, dwcr, dwci, dds, dlr, dli = s5_scan_fused_bwd(a_p, dy, *sw, dsk, ck_re, ck_im, tag + "_scanb")
    da = jnp.concatenate([du, dz], axis=1)
    dhn_p = matmul(da, wf["w_in"], "nt", tag + "_mm_dhn")
    sink.mm("w_in", hn_p, da, tag + "_mm_gwin")
    zero = sink.send()
    dh, dng = rms_bwd(h, p["norm_g"] + zero, _seg_unperm(dhn_p), dh_out, tag + "_rms_bwd")
    ex = lambda m_, r, c: _blockdiag_extract(m_, r, c).reshape(S5_GROUPS, r, c).transpose(0, 2, 1)
    dbb_re, dbb_im = ex(dwbr, S5_GROUP, S5_STATE), ex(dwbi, S5_GROUP, S5_STATE)
    g_c_re, g_c_im = ex(dwcr, S5_STATE, S5_GROUP), ex(dwci, S5_STATE, S5_GROUP)
    dl_re = lane_sum8(dlr).reshape(S5_GROUPS, S5_STATE, 1)
    dl_im = lane_sum8(dli).reshape(S5_GROUPS, S5_STATE, 1)
    gar, gai, gls, gbr, gbi = s5_params_bwd(p["a_re"], p["a_im"], p["log_step"], p["b_re"], p["b_im"],
                                            dl_re, dl_im, dbb_re, dbb_im)
    grads = {"norm_g": dng.reshape(-1), "a_re": gar.reshape(S5_GROUPS, S5_STATE),
             "a_im": gai.reshape(S5_GROUPS, S5_STATE), "log_step": gls.reshape(-1), "b_re": gbr, "b_im": gbi,
             "c_re": g_c_re, "c_im": g_c_im, "d_skip": dds.reshape(-1), "b_glu": db_glu.reshape(-1)}
    return dh, grads


def lane_sum8(t):
    sb, seg, ns = t.shape

    def body(t_ref, o_ref):
        o_ref[...] = jnp.sum(t_ref[...], axis=1, keepdims=True)

    return pl.pallas_call(body, name="s5_seg_sum", out_shape=jax.ShapeDtypeStruct((sb, 1, ns), F32))(t)


MLA_DI = MLA_HEADS * 128
MLA_CQ0 = MLA_DI
MLA_CKV0 = MLA_CQ0 + MLA_Q_RANK
MLA_KR0 = MLA_CKV0 + MLA_KV_RANK
MLA_AW = MLA_KR0 + LANES


def _rot_half(x):
    w = x.shape[-1]
    lane = lax.broadcasted_iota(jnp.int32, x.shape, x.ndim - 1)
    return jnp.where(lane % MLA_ROPE < MLA_ROPE // 2, pltpu.roll(x, w - MLA_ROPE // 2, x.ndim - 1),
                     pltpu.roll(x, MLA_ROPE // 2, x.ndim - 1))


def rope_tables(pos, zero):
    l = pos.shape[0]
    tl = ROW_TILE
    j = np.arange(LANES) % MLA_ROPE % (MLA_ROPE // 2)
    inv_freq = (ROPE_THETA ** (-(2.0 * j) / MLA_ROPE)).astype(np.float32).reshape(1, LANES)
    sign = np.where(np.arange(LANES) % MLA_ROPE < MLA_ROPE // 2, -1.0, 1.0).astype(np.float32).reshape(1, LANES)

    def body(p_ref, f_ref, s_ref, cos_ref, sin_ref):
        ang = p_ref[...].astype(F32) * f_ref[...]
        cos_ref[...] = jnp.cos(ang)
        sin_ref[...] = jnp.sin(ang) * s_ref[...]

    st = jax.ShapeDtypeStruct((l, LANES), F32)
    return _rowcall(body, "rope_tables", l // tl, [_rows(tl, 1), _full((1, LANES)), _full((1, LANES))],
                    [_rows(tl, LANES)] * 2, [st, st])(pos, jnp.asarray(inv_freq), jnp.asarray(sign) + zero)


def _rope(x, cos, sins):
    return x * cos + _rot_half(x) * sins


def _rope_t(dy, cos, sins):
    return dy * cos - sins * _rot_half(dy)


def _rmsn(x):
    r = lax.rsqrt(jnp.mean(x * x, axis=-1, keepdims=True) + NORM_EPS)
    return x * r, r


def mla_pre(a, q_g, kv_g, cos, sins, name):
    l = a.shape[0]
    tl = ROW_TILE

    def body(a_ref, qg_ref, kg_ref, cos_ref, sin_ref, cq_ref, ckv_ref, krs_ref):
        xq, _ = _rmsn(a_ref[:, MLA_CQ0:MLA_CKV0])
        cq_ref[...] = (xq * qg_ref[...]).astype(BF16)
        xk, _ = _rmsn(a_ref[:, MLA_CKV0:MLA_KR0])
        ckv_ref[...] = (xk * kg_ref[...]).astype(BF16)
        kr = a_ref[:, MLA_KR0:MLA_AW]
        kr2 = kr + pltpu.roll(kr, MLA_ROPE, 1)
        kr2 = _rope(kr2, cos_ref[...], sin_ref[...])
        lane = lax.broadcasted_iota(jnp.int32, kr2.shape, 1)
        krs_ref[0] = jnp.where(lane < MLA_ROPE, kr2, 0.0).astype(BF16)
        krs_ref[1] = jnp.where(lane >= MLA_ROPE, kr2, 0.0).astype(BF16)

    st = jax.ShapeDtypeStruct
    return _rowcall(body, name, l // tl,
                    [_rows(tl, MLA_AW), _full((1, MLA_Q_RANK)), _full((1, MLA_KV_RANK)), _rows(tl, LANES), _rows(tl, LANES)],
                    [_rows(tl, MLA_Q_RANK), _rows(tl, MLA_KV_RANK), pl.BlockSpec((2, tl, LANES), lambda i: (0, i, 0))],
                    [st((l, MLA_Q_RANK), BF16), st((l, MLA_KV_RANK), BF16), st((2, l, LANES), BF16)])(
        a, q_g.reshape(1, -1), kv_g.reshape(1, -1), cos, sins)


def mla_rope_q(qr, cos, sins, name):
    l, w = qr.shape
    tl = ROW_TILE

    def body(q_ref, cos_ref, sin_ref, o_ref):
        c, s = cos_ref[...], sin_ref[...]
        for p in range(w // LANES):
            sl = slice(p * LANES, (p + 1) * LANES)
            o_ref[:, sl] = _rope(q_ref[:, sl], c, s).astype(BF16)

    return _rowcall(body, name, l // tl, [_rows(tl, w), _rows(tl, LANES), _rows(tl, LANES)], _rows(tl, w),
                    jax.ShapeDtypeStruct((l, w), BF16))(qr, cos, sins)


ATT_OUT = 512
ATT_IN = 512
ATT_R = ATT_OUT // ATT_IN


def _scores(qn, qr, kn, kr, mask_off, transposed):
    q2 = jnp.concatenate([qn, qr], axis=1)
    k2 = jnp.concatenate([kn, kr], axis=1)
    s = (_dot(k2, q2, DN_NT) if transposed else _dot(q2, k2, DN_NT)) * MLA_SCALE
    if mask_off is None:
        return s
    r = lax.broadcasted_iota(jnp.int32, s.shape, 0)
    c = lax.broadcasted_iota(jnp.int32, s.shape, 1)
    return jnp.where((r <= c + mask_off) if transposed else (c + mask_off <= r), s, NEG_INF)


def _fold(x, op):
    out = x[:, :LANES]
    for t in range(1, x.shape[1] // LANES):
        out = op(out, x[:, t * LANES:(t + 1) * LANES])
    return out


def flash_fwd(qn, qr, kv, krs, name):
    l = qn.shape[0]
    nq = l // ATT_OUT

    def body(qn_ref, qr_ref, kv_ref, kr_ref, o_ref, lse_ref, s_buf):
        qi = pl.program_id(1)
        q_r = qr_ref[...]
        q_n = [qn_ref[:, hh * LANES:(hh + 1) * LANES] for hh in range(2)]

        def block_scores(j, mx, mask_off):
            sl = pl.ds(pl.multiple_of(j * ATT_IN, ATT_IN), ATT_IN)
            out = []
            for hh in range(2):
                s = _scores(q_n[hh], q_r, kv_ref[sl, 2 * hh * LANES:(2 * hh + 1) * LANES], kr_ref[hh, sl, :],
                            mask_off, False)
                s_buf[hh, j] = s
                out.append(jnp.maximum(mx[hh], _fold(s, jnp.maximum)))
            return tuple(out)

        ninf = jnp.full((ATT_OUT, LANES), NEG_INF, F32)
        mx = lax.fori_loop(0, ATT_R * qi, lambda j, c: block_scores(j, c, None), (ninf, ninf))
        for d in range(ATT_R):
            mx = block_scores(ATT_R * qi + d, mx, d * ATT_IN)
        m = [jnp.max(mx[hh], axis=-1, keepdims=True) for hh in range(2)]

        def block_pv(j, carry):
            sl = pl.ds(pl.multiple_of(j * ATT_IN, ATT_IN), ATT_IN)
            out = []
            for hh in range(2):
                ls, acc = carry[hh]
                p = jnp.exp(s_buf[hh, j] - m[hh])
                out.append((ls + _fold(p, jnp.add),
                            acc + _dot(p, kv_ref[sl, (2 * hh + 1) * LANES:(2 * hh + 2) * LANES], DN_NN)))
            return tuple(out)

        z = jnp.zeros((ATT_OUT, LANES), F32)
        res = lax.fori_loop(0, ATT_R * (qi + 1), block_pv, ((z, z), (z, z)))
        for hh in range(2):
            lsum = jnp.sum(res[hh][0], axis=-1, keepdims=True)
            o_ref[:, hh * LANES:(hh + 1) * LANES] = res[hh][1] / lsum
            lse_ref[hh] = m[hh] + jnp.log(lsum)

    st = jax.ShapeDtypeStruct
    return pl.pallas_call(
        body, name=name, grid=(MLA_HEADS // 2, nq),
        in_specs=[pl.BlockSpec((ATT_OUT, 2 * LANES), lambda p, i: (i, p)),
                  pl.BlockSpec((ATT_OUT, LANES), lambda p, i: (i, p)),
                  pl.BlockSpec((l, 4 * LANES), lambda p, i: (0, p)),
                  pl.BlockSpec((2, l, LANES), lambda p, i: (0, 0, 0))],
        out_specs=[pl.BlockSpec((ATT_OUT, 2 * LANES), lambda p, i: (i, p)),
                   pl.BlockSpec((2, ATT_OUT, 1), lambda p, i: (p, i, 0))],
        out_shape=[st((l, MLA_DI), F32), st((MLA_HEADS, l, 1), F32)],
        scratch_shapes=[pltpu.VMEM((2, l // ATT_IN, ATT_OUT, ATT_IN), F32)],
        compiler_params=_cparams(("parallel", "arbitrary")))(qn, qr, kv, krs)


def flash_dkv(qn, qr, kv, krs, do, lse_row, delta_row, name):
    l = qn.shape[0]
    nk = l // ATT_OUT
    nq = l // ATT_IN

    def body(qn_ref, qr_ref, do_ref, lse_ref, dl_ref, kv_ref, kr_ref, dkv_ref, dkr_ref):
        kj = pl.program_id(1)
        lane = lax.broadcasted_iota(jnp.int32, (ATT_OUT, LANES), 1)
        kn = [kv_ref[:, 2 * hh * LANES:(2 * hh + 1) * LANES] for hh in range(2)]
        v = [kv_ref[:, (2 * hh + 1) * LANES:(2 * hh + 2) * LANES] for hh in range(2)]

        def block(i, carry, mask_off):
            sl = pl.ds(pl.multiple_of(i * ATT_IN, ATT_IN), ATT_IN)
            q_r = qr_ref[sl, :]
            out = []
            for hh in range(2):
                dk2, dv = carry[hh]
                hs = slice(hh * LANES, (hh + 1) * LANES)
                q_n, d_o = qn_ref[sl, hs], do_ref[sl, hs]
                s = _scores(q_n, q_r, kn[hh], kr_ref[hh], mask_off, True)
                pt = jnp.exp(s - lse_ref[hh, i])
                dv = dv + _dot(pt, d_o, DN_NN)
                dpt = _dot(v[hh], d_o, DN_NT)
                dst = (pt * (dpt - dl_ref[hh, i]) * MLA_SCALE).astype(BF16)
                out.append((dk2 + _dot(dst, jnp.concatenate([q_n, q_r], axis=1), DN_NN), dv))
            return tuple(out)

        z = jnp.zeros((ATT_OUT, LANES), F32)
        z2 = jnp.zeros((ATT_OUT, 2 * LANES), F32)
        res = ((z2, z), (z2, z))
        for d in range(ATT_R):
            res = block(ATT_R * kj + d, res, d * ATT_IN)
        res = lax.fori_loop(ATT_R * (kj + 1), nq, lambda i, c: block(i, c, None), res)
        for hh in range(2):
            dkv_ref[:, 2 * hh * LANES:(2 * hh + 1) * LANES] = res[hh][0][:, :LANES].astype(BF16)
            dkv_ref[:, (2 * hh + 1) * LANES:(2 * hh + 2) * LANES] = res[hh][1].astype(BF16)
        dkr_ref[0] = jnp.where(lane < MLA_ROPE, res[0][0][:, LANES:], res[1][0][:, LANES:])

    st = jax.ShapeDtypeStruct
    return pl.pallas_call(
        body, name=name, grid=(MLA_HEADS // 2, nk),
        in_specs=[pl.BlockSpec((l, 2 * LANES), lambda p, j: (0, p)),
                  pl.BlockSpec((l, LANES), lambda p, j: (0, p)),
                  pl.BlockSpec((l, 2 * LANES), lambda p, j: (0, p)),
                  pl.BlockSpec((2, nq, 1, ATT_IN), lambda p, j: (p, 0, 0, 0)),
                  pl.BlockSpec((2, nq, 1, ATT_IN), lambda p, j: (p, 0, 0, 0)),
                  pl.BlockSpec((ATT_OUT, 4 * LANES), lambda p, j: (j, p)),
                  pl.BlockSpec((2, ATT_OUT, LANES), lambda p, j: (0, j, 0))],
        out_specs=[pl.BlockSpec((ATT_OUT, 4 * LANES), lambda p, j: (j, p)),
                   pl.BlockSpec((1, ATT_OUT, LANES), lambda p, j: (p, j, 0))],
        out_shape=[st((l, 2 * MLA_DI), BF16), st((MLA_HEADS // 2, l, LANES), F32)],
        compiler_params=_cparams(("parallel", "arbitrary")))(qn, qr, do, lse_row, delta_row, kv, krs)


def flash_dq(qn, qr, kv, krs, do, lse, delta, cos, sins, name):
    l = qn.shape[0]
    nq = l // ATT_OUT

    def body(qn_ref, qr_ref, do_ref, lse_ref, dl_ref, kv_ref, kr_ref, cos_ref, sin_ref, dqn_ref, dqr_ref):
        qi = pl.program_id(1)
        q_r = qr_ref[...]
        q_n = [qn_ref[:, hh * LANES:(hh + 1) * LANES] for hh in range(2)]
        d_o = [do_ref[:, hh * LANES:(hh + 1) * LANES] for hh in range(2)]
        lse_h = [lse_ref[hh] for hh in range(2)]
        dl_h = [dl_ref[hh] for hh in range(2)]

        def block(j, carry, mask_off):
            sl = pl.ds(pl.multiple_of(j * ATT_IN, ATT_IN), ATT_IN)
            dq2 = list(carry)
            for hh in range(2):
                kn = kv_ref[sl, 2 * hh * LANES:(2 * hh + 1) * LANES]
                v = kv_ref[sl, (2 * hh + 1) * LANES:(2 * hh + 2) * LANES]
                kr = kr_ref[hh, sl, :]
                s = _scores(q_n[hh], q_r, kn, kr, mask_off, False)
                pr = jnp.exp(s - lse_h[hh])
                dp = _dot(d_o[hh], v, DN_NT)
                ds = (pr * (dp - dl_h[hh]) * MLA_SCALE).astype(BF16)
                dq2[hh] = dq2[hh] + _dot(ds, jnp.concatenate([kn, kr], axis=1), DN_NN)
            return tuple(dq2)

        z2 = jnp.zeros((ATT_OUT, 2 * LANES), F32)
        res = lax.fori_loop(0, ATT_R * qi, lambda j, c: block(j, c, None), (z2, z2))
        for d in range(ATT_R):
            res = block(ATT_R * qi + d, res, d * ATT_IN)
        dqn_ref[:, 0:LANES] = res[0][:, :LANES].astype(BF16)
        dqn_ref[:, LANES:2 * LANES] = res[1][:, :LANES].astype(BF16)
        dqr = res[0][:, LANES:] + res[1][:, LANES:]
        dqr_ref[...] = _rope_t(dqr, cos_ref[...], sin_ref[...]).astype(BF16)

    st = jax.ShapeDtypeStruct
    return pl.pallas_call(
        body, name=name, grid=(MLA_HEADS // 2, nq),
        in_specs=[pl.BlockSpec((ATT_OUT, 2 * LANES), lambda p, i: (i, p)),
                  pl.BlockSpec((ATT_OUT, LANES), lambda p, i: (i, p)),
                  pl.BlockSpec((ATT_OUT, 2 * LANES), lambda p, i: (i, p)),
                  pl.BlockSpec((2, ATT_OUT, 1), lambda p, i: (p, i, 0)),
                  pl.BlockSpec((2, ATT_OUT, 1), lambda p, i: (p, i, 0)),
                  pl.BlockSpec((l, 4 * LANES), lambda p, i: (0, p)),
                  pl.BlockSpec((2, l, LANES), lambda p, i: (0, 0, 0)),
                  pl.BlockSpec((ATT_OUT, LANES), lambda p, i: (i, 0)),
                  pl.BlockSpec((ATT_OUT, LANES), lambda p, i: (i, 0))],
        out_specs=[pl.BlockSpec((ATT_OUT, 2 * LANES), lambda p, i: (i, p)),
                   pl.BlockSpec((ATT_OUT, LANES), lambda p, i: (i, p))],
        out_shape=[st((l, MLA_DI), BF16), st((l, MLA_HEADS * MLA_ROPE), BF16)],
        compiler_params=_cparams(("parallel", "arbitrary")))(qn, qr, do, lse, delta, kv, krs, cos, sins)


def mla_gate_fwd(o, a, name):
    l = o.shape[0]
    tl = ROW_TILE

    def body(o_ref, z_ref, m_ref):
        m_ref[...] = (o_ref[...] * _silu(z_ref[...])).astype(BF16)

    return _rowcall(body, name, l // tl, [_rows(tl, MLA_DI), _rows(tl, MLA_DI)], _rows(tl, MLA_DI),
                    jax.ShapeDtypeStruct((l, MLA_DI), BF16))(o, a)


def mla_gate_bwd(dm, o, a, name):
    l = o.shape[0]
    tl = ROW_TILE

    def body(dm_ref, o_ref, z_ref, do_ref, dz_ref, dl_ref):
        dmv, ov, z = dm_ref[...], o_ref[...], z_ref[...]
        sz, gz = _silu_both(z)
        d_o = dmv * sz
        do_ref[...] = d_o.astype(BF16)
        dz_ref[...] = (dmv * ov * gz).astype(BF16)
        pr = d_o * ov
        for h in range(MLA_HEADS):
            dl_ref[h] = jnp.sum(pr[:, h * LANES:(h + 1) * LANES], axis=1, keepdims=True)

    st = jax.ShapeDtypeStruct
    return _rowcall(body, name, l // tl, [_rows(tl, MLA_DI)] * 3,
                    [_rows(tl, MLA_DI), _rows(tl, MLA_DI), pl.BlockSpec((MLA_HEADS, tl, 1), lambda i: (0, i, 0))],
                    [st((l, MLA_DI), BF16), st((l, MLA_DI), BF16), st((MLA_HEADS, l, 1), F32)])(dm, o, a)


def mla_post(a, dcqn, dckvn, dkr_pairs, dz, q_g, kv_g, cos, sins, name):
    l = a.shape[0]
    tl = ROW_TILE
    npair = MLA_HEADS // 2

    def norm_bwd(x, g, dy):
        xhat, r = _rmsn(x)
        dxh = dy * g
        return r * (dxh - xhat * jnp.mean(dxh * xhat, axis=-1, keepdims=True)), jnp.sum(dy * xhat, axis=0, keepdims=True)

    def body(a_ref, dq_ref, dk_ref, dkr_ref, dz_ref, qg_ref, kg_ref, cos_ref, sin_ref, da_ref, dqg_ref, dkg_ref):
        i = pl.program_id(0)
        da_ref[:, 0:MLA_DI] = dz_ref[...]
        dcq, dqg = norm_bwd(a_ref[:, MLA_CQ0:MLA_CKV0], qg_ref[...], dq_ref[...])
        da_ref[:, MLA_CQ0:MLA_CKV0] = dcq.astype(BF16)
        dckv, dkg = norm_bwd(a_ref[:, MLA_CKV0:MLA_KR0], kg_ref[...], dk_ref[...])
        da_ref[:, MLA_CKV0:MLA_KR0] = dckv.astype(BF16)
        dk2 = dkr_ref[0]
        for p in range(1, npair):
            dk2 = dk2 + dkr_ref[p]
        dk2 = _rope_t(dk2, cos_ref[...], sin_ref[...])
        dk2 = dk2 + pltpu.roll(dk2, MLA_ROPE, 1)
        lane = lax.broadcasted_iota(jnp.int32, dk2.shape, 1)
        da_ref[:, MLA_KR0:MLA_AW] = jnp.where(lane < MLA_ROPE, dk2, 0.0).astype(BF16)
        _acc(dqg_ref, dqg, i)
        _acc(dkg_ref, dkg, i)

    st = jax.ShapeDtypeStruct
    return _rowcall(body, name, l // tl,
                    [_rows(tl, MLA_AW), _rows(tl, MLA_Q_RANK), _rows(tl, MLA_KV_RANK),
                     pl.BlockSpec((npair, tl, LANES), lambda i: (0, i, 0)), _rows(tl, MLA_DI),
                     _full((1, MLA_Q_RANK)), _full((1, MLA_KV_RANK)), _rows(tl, LANES), _rows(tl, LANES)],
                    [_rows(tl, MLA_AW), _full((1, MLA_Q_RANK)), _full((1, MLA_KV_RANK))],
                    [st((l, MLA_AW), BF16), st((1, MLA_Q_RANK), F32), st((1, MLA_KV_RANK), F32)])(
        a, dcqn, dckvn, dkr_pairs, dz, q_g.reshape(1, -1), kv_g.reshape(1, -1), cos, sins)


def _mla_w_in_perm(w):
    r = MLA_Q_RANK + MLA_KV_RANK + MLA_ROPE
    pad = jnp.zeros(w.shape[:-1] + (MLA_AW - MLA_KR0 - MLA_ROPE,), w.dtype)
    return jnp.concatenate([w[..., r:], w[..., :r], pad], axis=-1)


def _mla_w_in_unperm(g):
    r = MLA_Q_RANK + MLA_KV_RANK + MLA_ROPE
    return jnp.concatenate([g[..., MLA_DI:MLA_DI + r], g[..., :MLA_DI]], axis=-1)


def _mla_w_uq_split(w):
    k = w.shape[0]
    w3 = w.reshape(k, MLA_HEADS, MLA_NOPE + MLA_ROPE)
    return w3[:, :, :MLA_NOPE].reshape(k, MLA_HEADS * MLA_NOPE), w3[:, :, MLA_NOPE:].reshape(k, MLA_HEADS * MLA_ROPE)


def _mla_w_uq_merge(gn, gr):
    k = gn.shape[0]
    return jnp.concatenate([gn.reshape(k, MLA_HEADS, MLA_NOPE), gr.reshape(k, MLA_HEADS, MLA_ROPE)], axis=2).reshape(k, -1)


def mla_layer_fwd(h, p, wf, cos, sins, tag):
    hn = rms_fwd(h, p["norm_g"], tag + "_rms")
    w_in = _mla_w_in_perm(wf["w_in"])
    w_uq_n, w_uq_r = _mla_w_uq_split(wf["w_uq"])
    a = matmul(hn, w_in, "nn", tag + "_mm_in")
    cqn, ckvn, krs = mla_pre(a, p["q_norm_g"], p["kv_norm_g"], cos, sins, tag + "_pre")
    qn = matmul(cqn, w_uq_n, "nn", tag + "_mm_qn", out_dtype=BF16)
    qr_raw = matmul(cqn, w_uq_r, "nn", tag + "_mm_qr")
    qr = mla_rope_q(qr_raw, cos, sins, tag + "_rope_q")
    kv = matmul(ckvn, wf["w_ukv"], "nn", tag + "_mm_kv", out_dtype=BF16)
    o, lse = flash_fwd(qn, qr, kv, krs, tag + "_flash")
    m = mla_gate_fwd(o, a, tag + "_gate")
    h_out = matmul(m, wf["w_out"], "nn", tag + "_mm_out", add=h)
    return h_out, (h, hn, a, cqn, ckvn, krs, qn, qr, kv, o, lse, m, w_in, w_uq_n, w_uq_r)


def mla_layer_bwd(dh_out, saved, p, wf, cos, sins, tag, sink):
    h, hn, a, cqn, ckvn, krs, qn, qr, kv, o, lse, m, w_in, w_uq_n, w_uq_r = saved
    l = h.shape[0]
    dm = matmul(dh_out, wf["w_out"], "nt", tag + "_mm_dm")
    sink.mm("w_out", m, dh_out, tag + "_mm_gwout")
    do, dz, delta = mla_gate_bwd(dm, o, a, tag + "_gate_bwd")
    lse_row = lse.reshape(MLA_HEADS, l // ATT_IN, 1, ATT_IN)
    delta_row = delta.reshape(MLA_HEADS, l // ATT_IN, 1, ATT_IN)
    dkv, dkr_pairs = flash_dkv(qn, qr, kv, krs, do, lse_row, delta_row, tag + "_flash_dkv")
    dqn, dqr = flash_dq(qn, qr, kv, krs, do, lse, delta, cos, sins, tag + "_flash_dq")
    dcqn = matmul(dqn, w_uq_n, "nt", tag + "_mm_dcq_n")
    dcqn = matmul(dqr, w_uq_r, "nt", tag + "_mm_dcq_r", add=dcqn)
    g_uq_n = matmul(cqn, dqn, "tn", tag + "_mm_guq_n")
    g_uq_r = matmul(cqn, dqr, "tn", tag + "_mm_guq_r")
    dckvn = matmul(dkv, wf["w_ukv"], "nt", tag + "_mm_dckv")
    sink.mm("w_ukv", ckvn, dkv, tag + "_mm_gukv")
    da, dqg, dkg = mla_post(a, dcqn, dckvn, dkr_pairs, dz, p["q_norm_g"], p["kv_norm_g"], cos, sins, tag + "_post")
    dhn = matmul(da, w_in, "nt", tag + "_mm_dhn")
    g_w_in = matmul(hn, da, "tn", tag + "_mm_gwin")
    sink.put("w_uq", _mla_w_uq_merge(g_uq_n, g_uq_r))
    sink.put("w_in", _mla_w_in_unperm(g_w_in))
    zero = sink.send()
    dh, dng = rms_bwd(h, p["norm_g"] + zero, dhn, dh_out, tag + "_rms_bwd")
    grads = {"norm_g": dng.reshape(-1), "q_norm_g": dqg.reshape(-1), "kv_norm_g": dkg.reshape(-1)}
    return dh, grads


ANY = pl.BlockSpec(memory_space=pl.ANY)


def _me():
    return lax.axis_index("x"), lax.axis_index("y"), lax.axis_index("c")


def _chip():
    return 2 * lax.axis_index("x") + lax.axis_index("y")


def _other_chips(x, y):
    return [(1 - x, y), (x, 1 - y), (1 - x, 1 - y)]


def _rcopy(src, dst, ssem, rsem, dev):
    return pltpu.make_async_remote_copy(src_ref=src, dst_ref=dst, send_sem=ssem, recv_sem=rsem,
                                        device_id=dev, device_id_type=MESH)


def _half(ref, c, hf):
    return ref.at[pl.ds(c * hf, hf), :]


HBM = pl.BlockSpec(memory_space=pltpu.HBM)
SEM = pl.BlockSpec(memory_space=pltpu.SEMAPHORE)
SPLIT_EFFECT = pltpu.SideEffectType.DATAFLOW_SIDE_EFFECTING


def gather_start(wb, after, name):
    nr, w = wb.shape
    hf = nr // 2

    def body(w_ref, land_ref, after_ref, ssem, rsem, w_thru, land_thru, token):
        x, y, c = _me()
        k = 2 * x + y
        for j, (cx, cy) in enumerate(_other_chips(x, y)):
            _rcopy(_half(w_ref, c, hf), _half(land_ref.at[k], c, hf), ssem.at[j], rsem.at[j], (cx, cy, c)).start()
        token[...] = jnp.zeros_like(token)

    land = lax.empty((N_CHIPS, nr, w), wb.dtype)
    return pl.pallas_call(
        body, name=name,
        out_shape=(pltpu.SemaphoreType.DMA((3,)), pltpu.SemaphoreType.DMA((3,)), pltpu.HBM(wb.shape, wb.dtype),
                   pltpu.HBM(land.shape, land.dtype), jax.ShapeDtypeStruct((8, LANES), F32)),
        in_specs=(HBM, HBM, ANY), out_specs=(SEM, SEM, HBM, HBM, pl.BlockSpec(memory_space=pltpu.VMEM)),
        input_output_aliases={0: 2, 1: 3},
        compiler_params=pltpu.CompilerParams(has_side_effects=SPLIT_EFFECT))(
        pltpu.with_memory_space_constraint(wb, pltpu.HBM), pltpu.with_memory_space_constraint(land, pltpu.HBM), after)


def gather_wait(ssem, rsem, w_thru, land_thru, after, name):
    nr, w = w_thru.shape
    hf = nr // 2

    def body(w_ref, land_ref, ssem_ref, rsem_ref, after_ref, w_dead, got_ref):
        x, y, c = _me()
        for j, (cx, cy) in enumerate(_other_chips(x, y)):
            cp = _rcopy(_half(w_ref, c, hf), _half(land_ref.at[2 * cx + cy], c, hf), ssem_ref.at[j], rsem_ref.at[j],
                        (cx, cy, c))
            cp.wait_send()
            cp.wait_recv()

    return pl.pallas_call(
        body, name=name, out_shape=(pltpu.HBM(w_thru.shape, w_thru.dtype), pltpu.HBM(land_thru.shape, land_thru.dtype)),
        in_specs=(HBM, HBM, SEM, SEM, ANY), out_specs=(HBM, HBM), input_output_aliases={0: 0, 1: 1},
        compiler_params=pltpu.CompilerParams(has_side_effects=SPLIT_EFFECT))(w_thru, land_thru, ssem, rsem, after)[1]


def gather_handover(land, wb, name):
    _, nr, w = land.shape
    hf = nr // 2

    def body(l_ref, o_ref, ssem, rsem):
        x, y, c = _me()
        chips = _other_chips(x, y)
        sends = []
        for j, (cx, cy) in enumerate(chips):
            region = _half(o_ref.at[2 * cx + cy], c, hf)
            sends.append(_rcopy(region, region, ssem.at[j], rsem.at[j], (x, y, 1 - c)))
            sends[-1].start()
        for j, (cx, cy) in enumerate(chips):
            region = _half(o_ref.at[2 * cx + cy], 1 - c, hf)
            _rcopy(region, region, ssem.at[j], rsem.at[j], (x, y, 1 - c)).wait_recv()
        for cp in sends:
            cp.wait_send()

    out = pl.pallas_call(
        body, name=name, in_specs=[ANY], out_specs=ANY, input_output_aliases={0: 0},
        out_shape=jax.ShapeDtypeStruct(land.shape, land.dtype),
        scratch_shapes=[pltpu.SemaphoreType.DMA((3,)), pltpu.SemaphoreType.DMA((3,))])(land)
    return lax.dynamic_update_slice(out, wb[None], (_chip(), 0, 0))


def reduce_start(t, after, name):
    def body(t_ref, land_ref, after_ref, ssem, rsem, t_thru, land_thru, token):
        x, y, c = _me()
        k = 2 * x + y
        for j, (cx, cy) in enumerate(_other_chips(x, y)):
            _rcopy(t_ref.at[2 * cx + cy], land_ref.at[k], ssem.at[j], rsem.at[j], (cx, cy, c)).start()
        token[...] = jnp.zeros_like(token)

    land = lax.empty(t.shape, t.dtype)
    return pl.pallas_call(
        body, name=name,
        out_shape=(pltpu.SemaphoreType.DMA((3,)), pltpu.SemaphoreType.DMA((3,)), pltpu.HBM(t.shape, t.dtype),
                   pltpu.HBM(t.shape, t.dtype), jax.ShapeDtypeStruct((8, LANES), F32)),
        in_specs=(HBM, HBM, ANY), out_specs=(SEM, SEM, HBM, HBM, pl.BlockSpec(memory_space=pltpu.VMEM)),
        input_output_aliases={0: 2, 1: 3},
        compiler_params=pltpu.CompilerParams(has_side_effects=SPLIT_EFFECT))(
        pltpu.with_memory_space_constraint(t, pltpu.HBM), pltpu.with_memory_space_constraint(land, pltpu.HBM), after)


def bcast_start(g, after, name):
    def body(g_ref, land_ref, after_ref, ssem, rsem, g_thru, land_thru, token):
        x, y, c = _me()
        k = 2 * x + y
        for j, (cx, cy) in enumerate(_other_chips(x, y)):
            _rcopy(g_ref, land_ref.at[k], ssem.at[j], rsem.at[j], (cx, cy, c)).start()
        token[...] = jnp.zeros_like(token)

    land = lax.empty((N_CHIPS,) + g.shape, g.dtype)
    return pl.pallas_call(
        body, name=name,
        out_shape=(pltpu.SemaphoreType.DMA((3,)), pltpu.SemaphoreType.DMA((3,)), pltpu.HBM(g.shape, g.dtype),
                   pltpu.HBM(land.shape, land.dtype), jax.ShapeDtypeStruct((8, LANES), F32)),
        in_specs=(HBM, HBM, ANY), out_specs=(SEM, SEM, HBM, HBM, pl.BlockSpec(memory_space=pltpu.VMEM)),
        input_output_aliases={0: 2, 1: 3},
        compiler_params=pltpu.CompilerParams(has_side_effects=SPLIT_EFFECT))(
        pltpu.with_memory_space_constraint(g, pltpu.HBM), pltpu.with_memory_space_constraint(land, pltpu.HBM), after)


def bcast_wait(ssem, rsem, g_thru, land_thru, after, name):
    def body(g_ref, land_ref, ssem_ref, rsem_ref, after_ref, g_out, got_ref):
        x, y, c = _me()
        for j, (cx, cy) in enumerate(_other_chips(x, y)):
            cp = _rcopy(g_ref, land_ref.at[2 * cx + cy], ssem_ref.at[j], rsem_ref.at[j], (cx, cy, c))
            cp.wait_send()
            cp.wait_recv()

    g, land = pl.pallas_call(
        body, name=name, out_shape=(pltpu.HBM(g_thru.shape, g_thru.dtype), pltpu.HBM(land_thru.shape, land_thru.dtype)),
        in_specs=(HBM, HBM, SEM, SEM, ANY), out_specs=(HBM, HBM), input_output_aliases={0: 0, 1: 1},
        compiler_params=pltpu.CompilerParams(has_side_effects=SPLIT_EFFECT))(g_thru, land_thru, ssem, rsem, after)
    return lax.dynamic_update_slice(land, g[None], (_chip(), 0, 0))


def reduce_wait(ssem, rsem, t_thru, land_thru, after, name):
    def body(t_ref, land_ref, ssem_ref, rsem_ref, after_ref, t_out, got_ref):
        x, y, c = _me()
        k = 2 * x + y
        for j, (cx, cy) in enumerate(_other_chips(x, y)):
            cp = _rcopy(t_ref.at[k], land_ref.at[2 * cx + cy], ssem_ref.at[j], rsem_ref.at[j], (cx, cy, c))
            cp.wait_send()
            cp.wait_recv()

    return pl.pallas_call(
        body, name=name, out_shape=(pltpu.HBM(t_thru.shape, t_thru.dtype), pltpu.HBM(land_thru.shape, land_thru.dtype)),
        in_specs=(HBM, HBM, SEM, SEM, ANY), out_specs=(HBM, HBM), input_output_aliases={0: 0, 1: 1},
        compiler_params=pltpu.CompilerParams(has_side_effects=SPLIT_EFFECT))(t_thru, land_thru, ssem, rsem, after)


def grads_to_sibling(ps, name="grads_to_sibling"):
    n = len(ps)

    def body(*refs):
        p_refs, o_refs, ssem, rsem = refs[:n], refs[n:2 * n], refs[2 * n], refs[2 * n + 1]
        x, y, c = _me()
        cps = []
        for a in range(n):
            hf = ps[a].shape[1] // 2
            cps.append(_rcopy(p_refs[a].at[:, pl.ds((1 - c) * hf, hf), :], o_refs[a], ssem.at[a], rsem.at[a],
                              (x, y, 1 - c)))
        for cp in cps:
            cp.start()
        for cp in cps:
            cp.wait()

    return pl.pallas_call(
        body, name=name, in_specs=[ANY] * n, out_specs=[ANY] * n,
        out_shape=[jax.ShapeDtypeStruct((N_CHIPS, p.shape[1] // 2, p.shape[2]), p.dtype) for p in ps],
        scratch_shapes=[pltpu.SemaphoreType.DMA((n,)), pltpu.SemaphoreType.DMA((n,))])(*ps)


def pair_sum(p, ra, out_dtype, name):
    _, nr, w = p.shape
    hf = nr // 2
    tr = _pick_rows(hf, cap=max(512, SUM_BLOCK_BYTES // (4 * w)))
    nb = hf // tr

    def body(c_ref, p_ref, r_ref, o_ref):
        o_ref[...] = (p_ref[...] + r_ref[...]).astype(out_dtype)

    c = lax.axis_index("c").astype(jnp.int32).reshape(1)
    return pl.pallas_call(
        body, name=name,
        grid_spec=pltpu.PrefetchScalarGridSpec(
            num_scalar_prefetch=1, grid=(N_CHIPS, nb),
            in_specs=[pl.BlockSpec((1, tr, w), lambda k, i, c_ref: (k, c_ref[0] * nb + i, 0)),
                      pl.BlockSpec((1, tr, w), lambda k, i, c_ref: (k, i, 0))],
            out_specs=pl.BlockSpec((1, tr, w), lambda k, i, c_ref: (k, i, 0))),
        out_shape=jax.ShapeDtypeStruct((N_CHIPS, hf, w), out_dtype),
        compiler_params=_cparams(("parallel", "parallel")))(c, p, ra)


def grads_across_chips(ts):
    n = len(ts)

    def body(*refs):
        t_refs, o_refs, ssem, rsem = refs[:n], refs[n:2 * n], refs[2 * n], refs[2 * n + 1]
        x, y, c = _me()
        k = 2 * x + y
        chips = _other_chips(x, y)
        sends = [_rcopy(t_refs[a].at[2 * cx + cy], o_refs[a].at[k], ssem.at[3 * a + j], rsem.at[3 * a + j], (cx, cy, c))
                 for a in range(n) for j, (cx, cy) in enumerate(chips)]
        for cp in sends:
            cp.start()
        for a in range(n):
            for j, (cx, cy) in enumerate(chips):
                _rcopy(t_refs[a].at[k], o_refs[a].at[2 * cx + cy], ssem.at[3 * a + j], rsem.at[3 * a + j],
                       (cx, cy, c)).wait_recv()
        for cp in sends:
            cp.wait_send()

    return pl.pallas_call(
        body, name="grads_across_chips", in_specs=[ANY] * n, out_specs=[ANY] * n,
        out_shape=[jax.ShapeDtypeStruct(t.shape, t.dtype) for t in ts],
        scratch_shapes=[pltpu.SemaphoreType.DMA((3 * n,)), pltpu.SemaphoreType.DMA((3 * n,))])(*ts)


def chip_sum(t, rb, name):
    _, hf, w = rb.shape
    tr = _pick_rows(hf, cap=max(512, SUM_BLOCK_BYTES // (4 * w)))
    nb = hf // tr

    def body(kc_ref, t_ref, r_ref, o_ref):
        k = kc_ref[0]
        acc = jnp.where(k == 0, t_ref[0], r_ref[0]).astype(F32)
        for j in range(1, N_CHIPS):
            acc = acc + jnp.where(k == j, t_ref[0], r_ref[j]).astype(F32)
        o_ref[...] = acc

    kc = jnp.stack([_chip(), lax.axis_index("c")]).astype(jnp.int32)
    return pl.pallas_call(
        body, name=name,
        grid_spec=pltpu.PrefetchScalarGridSpec(
            num_scalar_prefetch=1, grid=(nb,),
            in_specs=[pl.BlockSpec((1, tr, w), lambda i, kc_ref: (kc_ref[0], i, 0)),
                      pl.BlockSpec((N_CHIPS, tr, w), lambda i, kc_ref: (0, i, 0))],
            out_specs=pl.BlockSpec((tr, w), lambda i, kc_ref: (kc_ref[1] * nb + i, 0))),
        out_shape=jax.ShapeDtypeStruct((2 * hf, w), F32), compiler_params=_cparams(("parallel",)))(kc, t, rb)


def reduced_to_sibling(gs):
    n = len(gs)

    def body(*refs):
        o_refs, ssem, rsem = refs[n:2 * n], refs[2 * n], refs[2 * n + 1]
        x, y, c = _me()
        cps = []
        for a in range(n):
            hf = gs[a].shape[0] // 2
            cps.append(_rcopy(_half(o_refs[a], c, hf), _half(o_refs[a], c, hf), ssem.at[a], rsem.at[a], (x, y, 1 - c)))
        for cp in cps:
            cp.start()
        for a in range(n):
            hf = gs[a].shape[0] // 2
            _rcopy(_half(o_refs[a], c, hf), _half(o_refs[a], 1 - c, hf), ssem.at[a], rsem.at[a],
                   (x, y, 1 - c)).wait_recv()
        for cp in cps:
            cp.wait_send()

    return pl.pallas_call(
        body, name="reduced_to_sibling", in_specs=[ANY] * n, out_specs=[ANY] * n,
        input_output_aliases={a: a for a in range(n)},
        out_shape=[jax.ShapeDtypeStruct(g.shape, g.dtype) for g in gs],
        scratch_shapes=[pltpu.SemaphoreType.DMA((n,)), pltpu.SemaphoreType.DMA((n,))])(*gs)


def _adamw_step(w_ref, g_ref, m_ref, v_ref, d_ref, nm_ref, nv_ref):
    bc1 = 1.0 - ADAM_B1 ** ADAM_STEP
    bc2 = 1.0 - ADAM_B2 ** ADAM_STEP
    gv = g_ref[...]
    nm = ADAM_B1 * m_ref[...] + (1.0 - ADAM_B1) * gv
    nv = ADAM_B2 * v_ref[...] + (1.0 - ADAM_B2) * (gv * gv)
    nm_ref[...] = nm
    nv_ref[...] = nv
    d_ref[...] = -ADAM_LR * ((nm / bc1) / (jnp.sqrt(nv / bc2) + ADAM_EPS) + ADAM_WD * w_ref[...])


def adamw_packed(w, g_buf, r0, m, v, name):
    r, c = w.shape
    tr = _tile_rows(r, r0)

    def body(w_ref, g_ref, m_ref, v_ref, go_ref, d_ref, nm_ref, nv_ref):
        go_ref[...] = g_ref[...]
        _adamw_step(w_ref, g_ref, m_ref, v_ref, d_ref, nm_ref, nv_ref)

    own = pl.BlockSpec((tr, CHUNK_W), lambda i, j: (i, j))
    packed = pl.BlockSpec((tr, CHUNK_W), lambda i, j: ((r0 + j * r) // tr + i, 0))
    st = jax.ShapeDtypeStruct((r, c), F32)
    return pl.pallas_call(body, name=name, grid=(r // tr, c // CHUNK_W), in_specs=[own, packed, own, own],
                          out_specs=[own] * 4, out_shape=[st] * 4,
                          compiler_params=_cparams(("parallel", "parallel")))(w, g_buf, m, v)


def adamw(w, g, m, v, name):
    r, wd = w.shape
    tr = _pick_rows(r, cap=max(16, ADAMW_BLOCK_BYTES // (4 * wd)))
    body = functools.partial(_adamw_step)

    spec = pl.BlockSpec((tr, wd), lambda i: (i, 0))
    st = jax.ShapeDtypeStruct((r, wd), F32)
    return pl.pallas_call(body, name=name, grid=(r // tr,), in_specs=[spec] * 4, out_specs=[spec] * 3,
                          out_shape=[st, st, st], compiler_params=_cparams(("parallel",)))(w, g, m, v)


LAYER_KINDS = ("gmlp", "s5", "mla", "gmlp")
PARAMS = {
    "gmlp": ("norm_g", "w_in", "ln_g", "ln_b", "w_s", "b_s", "w_out"),
    "s5": ("norm_g", "w_in", "a_re", "a_im", "log_step", "b_re", "b_im", "c_re", "c_im", "d_skip", "w_glu", "b_glu", "w_out"),
    "mla": ("norm_g", "w_in", "q_norm_g", "w_uq", "kv_norm_g", "w_ukv", "w_out"),
}
COL_SHARDED = ("w_in", "w_uq", "w_ukv")
ROW_SHARDED = ("w_out", "w_glu")
WEIGHT_NAMES = [("l%d_" % i) + n for i, kind in enumerate(LAYER_KINDS) for n in PARAMS[kind]] + ["final_norm_g"]


def _is_big(name):
    return name.split("_", 1)[1] in COL_SHARDED + ROW_SHARDED


BIG = [n for n in WEIGHT_NAMES if _is_big(n)]
SMALL = [n for n in WEIGHT_NAMES if not _is_big(n)]


def _pack_rows(blocks):
    return jnp.concatenate([b.reshape(-1, PACK_W) for b in blocks], axis=0)


def _shard_major(wn, full, width):
    r, c = full.shape
    if wn in COL_SHARDED:
        t = full.reshape(r, N_CHIPS, c // N_CHIPS).transpose(1, 0, 2)
    else:
        t = full.reshape(N_CHIPS, r // N_CHIPS, c)
    return t.reshape(N_CHIPS, -1, width)


def _from_shard_major(name, t, block_shape):
    r, c = block_shape
    if name.split("_", 1)[1] in COL_SHARDED:
        return t.reshape(N_CHIPS, r, c).transpose(1, 0, 2).reshape(r, N_CHIPS * c)
    return t.reshape(N_CHIPS * r, c)


class BigGradSink:
    ORDER = ("w_out", "w_glu", "w_ukv", "w_uq", "w_in")
    ROW_MAJOR = {2: ("w_uq", "w_in")}

    def __init__(self, layer, block_shapes):
        self.layer = layer
        self.regions = {}
        r0 = 0
        for wn in self.ORDER:
            if wn in block_shapes:
                shape = block_shapes[wn]
                self.regions[wn] = (r0, shape, wn not in self.ROW_MAJOR.get(layer, ()))
                r0 += shape[0] * shape[1] // CHUNK_W
        self.buf = lax.empty((N_CHIPS, r0, CHUNK_W), F32)
        self.flight = None

    def mm(self, wn, a, b, name):
        r0, _, direct = self.regions[wn]
        assert direct
        self.buf = matmul_tn_packed(a, b, self.buf, r0, wn in COL_SHARDED, name)

    def put(self, wn, full):
        r0, _, direct = self.regions[wn]
        assert not direct
        piece = _shard_major(wn, full, CHUNK_W)
        self.buf = lax.dynamic_update_slice(self.buf, piece, (0, r0, 0))

    def send(self):
        i = self.layer
        sib, = grads_to_sibling([self.buf], "grads_to_sibling_l%d" % i)
        t = pair_sum(self.buf, sib, BF16, "pair_sum_l%d" % i)
        self.flight = reduce_start(t, sib, "reduce_l%d_start" % i)
        return self.flight[4][0, 0]


def _small_pack(arrs, total_padded):
    flat = jnp.concatenate([a.reshape(-1) for a in arrs])
    return jnp.pad(flat, (0, total_padded - flat.shape[0]))


def kernel(x, positions, l0_norm_g, l0_w_in, l0_ln_g, l0_ln_b, l0_w_s, l0_b_s, l0_w_out, l1_norm_g, l1_w_in, l1_a_re, l1_a_im, l1_log_step, l1_b_re, l1_b_im, l1_c_re, l1_c_im, l1_d_skip, l1_w_glu, l1_b_glu, l1_w_out, l2_norm_g, l2_w_in, l2_q_norm_g, l2_w_uq, l2_kv_norm_g, l2_w_ukv, l2_w_out, l3_norm_g, l3_w_in, l3_ln_g, l3_ln_b, l3_w_s, l3_b_s, l3_w_out, final_norm_g, loss_target, m_l0_norm_g, m_l0_w_in, m_l0_ln_g, m_l0_ln_b, m_l0_w_s, m_l0_b_s, m_l0_w_out, m_l1_norm_g, m_l1_w_in, m_l1_a_re, m_l1_a_im, m_l1_log_step, m_l1_b_re, m_l1_b_im, m_l1_c_re, m_l1_c_im, m_l1_d_skip, m_l1_w_glu, m_l1_b_glu, m_l1_w_out, m_l2_norm_g, m_l2_w_in, m_l2_q_norm_g, m_l2_w_uq, m_l2_kv_norm_g, m_l2_w_ukv, m_l2_w_out, m_l3_norm_g, m_l3_w_in, m_l3_ln_g, m_l3_ln_b, m_l3_w_s, m_l3_b_s, m_l3_w_out, m_final_norm_g, v_l0_norm_g, v_l0_w_in, v_l0_ln_g, v_l0_ln_b, v_l0_w_s, v_l0_b_s, v_l0_w_out, v_l1_norm_g, v_l1_w_in, v_l1_a_re, v_l1_a_im, v_l1_log_step, v_l1_b_re, v_l1_b_im, v_l1_c_re, v_l1_c_im, v_l1_d_skip, v_l1_w_glu, v_l1_b_glu, v_l1_w_out, v_l2_norm_g, v_l2_w_in, v_l2_q_norm_g, v_l2_w_uq, v_l2_kv_norm_g, v_l2_w_ukv, v_l2_w_out, v_l3_norm_g, v_l3_w_in, v_l3_ln_g, v_l3_ln_b, v_l3_w_s, v_l3_b_s, v_l3_w_out, v_final_norm_g):
    args = locals()
    w = {n: args[n] for n in WEIGHT_NAMES}
    mom_m = {n: args["m_" + n] for n in WEIGHT_NAMES}
    mom_v = {n: args["v_" + n] for n in WEIGHT_NAMES}
    h0 = x[0]
    target = loss_target[0]
    pos = positions.reshape(-1, 1)

    full = {}

    def pack_unit(layers):
        names = [n for n in BIG if int(n[1]) in layers]
        rows = [w[n].size // PACK_W for n in names]
        pad = -sum(rows) % PACK_ROW_ALIGN
        return names, rows, _pack_rows([w[n].astype(BF16) for n in names] + [jnp.zeros((pad, PACK_W), BF16)])

    def unpack_unit(names, rows, gathered):
        r0 = 0
        for n, nr in zip(names, rows):
            full[n] = _from_shard_major(n, gathered[:, r0:r0 + nr, :], w[n].shape)
            r0 += nr

    unit0, unit1, unit2 = pack_unit((0,)), pack_unit((1,)), pack_unit((2, 3))
    wp = dict(w)

    def layer_params(i):
        pre = "l%d_" % i
        p = {k[len(pre):]: v for k, v in wp.items() if k.startswith(pre)}
        wf = {k[len(pre):]: v for k, v in full.items() if k.startswith(pre)}
        return p, wf

    flight = gather_start(unit0[2], unit1[2], "gather_l0_start")
    cos, sins = rope_tables(pos, flight[4][0, 0])
    wp["l1_a_re"] = w["l1_a_re"] + flight[4][0, 0]
    s5_weights = _s5_weights(layer_params(1)[0])
    land = gather_wait(*flight[:4], s5_weights[2], "gather_l0_wait")
    got = gather_handover(land, unit0[2], "gather_l0_handover")
    unpack_unit(unit0[0], unit0[1], got)
    flight = gather_start(unit1[2], got, "gather_l1_start")
    wp["l0_norm_g"] = w["l0_norm_g"] + flight[4][0, 0]

    h = h0
    saved = []
    for i, kind in enumerate(LAYER_KINDS):
        if i == 1:
            land = gather_wait(*flight[:4], h, "gather_l1_wait")
            got = gather_handover(land, unit1[2], "gather_l1_handover")
            unpack_unit(unit1[0], unit1[1], got)
            flight = gather_start(unit2[2], got, "gather_l23_start")
            wp["l1_norm_g"] = w["l1_norm_g"] + flight[4][0, 0]
        if i == 2:
            land = gather_wait(*flight[:4], h, "gather_l23_wait")
            unpack_unit(unit2[0], unit2[1], gather_handover(land, unit2[2], "gather_l23_handover"))
        p, wf = layer_params(i)
        tag = "l%d" % i
        if kind == "gmlp":
            h, s = gmlp_layer_fwd(h, p, wf, tag)
        elif kind == "s5":
            h, s = s5_layer_fwd(h, p, wf, s5_weights, tag)
        else:
            h, s = mla_layer_fwd(h, p, wf, cos, sins, tag)
        saved.append(s)
    loss_part, dh, g_final = loss_head(h, final_norm_g, target)

    grads = {"final_norm_g": g_final.reshape(-1)}
    sinks = {}

    for i in reversed(range(len(LAYER_KINDS))):
        kind = LAYER_KINDS[i]
        p, wf = layer_params(i)
        tag = "l%d" % i
        sink = sinks[i] = BigGradSink(i, {n[3:]: w[n].shape for n in BIG if int(n[1]) == i})
        if kind == "gmlp":
            dh, g = gmlp_layer_bwd(dh, saved[i], p, wf, tag, sink)
        elif kind == "s5":
            dh, g = s5_layer_bwd(dh, saved[i], p, wf, tag, sink)
        else:
            dh, g = mla_layer_bwd(dh, saved[i], p, wf, cos, sins, tag, sink)
        for k, val in g.items():
            grads["l%d_%s" % (i, k)] = val
    grad_x = dh[None]

    n_small = sum(w[n].size for n in SMALL)
    piece = N_CHIPS * 2 * 16 * PACK_W
    n_small_pad = -(-(n_small + 1) // piece) * piece
    nrs = n_small_pad // N_CHIPS // PACK_W
    p_small = _small_pack([grads[n] for n in SMALL] + [loss_part], n_small_pad).reshape(N_CHIPS, nrs, PACK_W)
    sib_small, = grads_to_sibling([p_small], "grads_to_sibling_small")
    t_small = pair_sum(p_small, sib_small, F32, "pair_sum_small")
    rb_small, = grads_across_chips([t_small])
    halves = [chip_sum(t_small, rb_small, "chip_sum_small")]

    after = halves[0]
    for i in reversed(range(len(LAYER_KINDS))):
        t_i, rb_i = reduce_wait(*sinks[i].flight[:4], after, "reduce_l%d_wait" % i)
        halves.append(chip_sum(t_i, rb_i, "chip_sum_l%d" % i))
        after = halves[-1]
    reduced = reduced_to_sibling(halves)
    small_flight = bcast_start(reduced[0], reduced[1], "small_allgather_start")

    g_out, d_out, nm_out, nv_out = {}, {}, {}, {}
    for i, g_i in zip(reversed(range(len(LAYER_KINDS))), reduced[1:]):
        for wn, (r0, shape, direct) in sinks[i].regions.items():
            n = "l%d_%s" % (i, wn)
            if direct:
                g_out[n], d_out[n], nm_out[n], nv_out[n] = adamw_packed(w[n], g_i, r0, mom_m[n], mom_v[n], "adamw_" + n)
            else:
                g_out[n] = g_i[r0:r0 + shape[0] * shape[1] // CHUNK_W].reshape(shape)
                d_out[n], nm_out[n], nv_out[n] = adamw(w[n], g_out[n], mom_m[n], mom_v[n], "adamw_" + n)
    small_all = bcast_wait(*small_flight[:4], nv_out["l0_w_in"], "small_allgather_wait")
    g_small = small_all.reshape(-1, PACK_W)
    sp = lambda d: _small_pack([d[n] for n in SMALL], n_small_pad).reshape(-1, PACK_W)
    d_small, nm_small, nv_small = adamw(sp(w), g_small, sp(mom_m), sp(mom_v), "adamw_small")
    for buf, out in ((g_small, g_out), (d_small, d_out), (nm_small, nm_out), (nv_small, nv_out)):
        flat = buf.reshape(-1)
        o = 0
        for n in SMALL:
            out[n] = flat[o:o + w[n].size].reshape(w[n].shape)
            o += w[n].size
    loss = g_small.reshape(-1)[n_small]
    return (loss, grad_x, *[g_out[n] for n in WEIGHT_NAMES], *[d_out[n] for n in WEIGHT_NAMES],
            *[nm_out[n] for n in WEIGHT_NAMES], *[nv_out[n] for n in WEIGHT_NAMES])
```

```python
import functools
import math

import jax
import jax.numpy as jnp
import numpy as np
from jax import lax
from jax.experimental import pallas as pl
from jax.experimental.pallas import tpu as pltpu

F32 = jnp.float32
BF16 = jnp.bfloat16
MESH = pl.DeviceIdType.MESH
VMEM_LIMIT_BYTES = 56 * 1024 * 1024
LANES = 128
PACK_W = 1024
CHUNK_W = 256
PACK_ROW_ALIGN = 256
ROW_TILE = 256
ROW_TILE_NARROW = 512
SUM_BLOCK_BYTES = 1024 * 1024
ADAMW_BLOCK_BYTES = 1024 * 1024
MM_BLOCK_BYTES = 6 * 1024 * 1024

NORM_EPS = 1e-6
N_CHIPS = 4
GMLP_CHUNK = 128
GMLP_GROUPS = 8
S5_GROUPS = 128
S5_GROUP = 16
S5_STATE = 64
S5_SB = 16
S5_SEG = 8
MLA_HEADS = 16
MLA_NOPE = 128
MLA_ROPE = 64
MLA_Q_RANK = 384
MLA_KV_RANK = 128
MLA_SCALE = (MLA_NOPE + MLA_ROPE) ** -0.5
ROPE_THETA = 10000.0
NEG_INF = -1e30
ADAM_LR, ADAM_B1, ADAM_B2, ADAM_EPS, ADAM_WD, ADAM_STEP = 0.001, 0.9, 0.999, 1e-08, 0.01, 10

DN_NN = (((1,), (0,)), ((), ()))
DN_NT = (((1,), (1,)), ((), ()))
DN_TN = (((0,), (0,)), ((), ()))


def _cparams(sem):
    return pltpu.CompilerParams(dimension_semantics=sem, vmem_limit_bytes=VMEM_LIMIT_BYTES)


def _pick(n, cands=(512, 384, 256, 128)):
    for c in cands:
        if n % c == 0:
            return c
    return n


def _pick_rows(r, cap=512, mult=16):
    return max(t for t in range(mult, cap + 1, mult) if r % t == 0)


def _dot(a, b, dn):
    return lax.dot_general(a.astype(BF16), b.astype(BF16), dn, preferred_element_type=F32)


def _sigmoid(x):
    return 0.5 + 0.5 * jnp.tanh(0.5 * x)


def _gelu(x):
    c = math.sqrt(2.0 / math.pi)
    t = jnp.tanh(c * (x + 0.044715 * x * x * x))
    return 0.5 * x * (1.0 + t)


def _gelu_grad(x):
    c = math.sqrt(2.0 / math.pi)
    t = jnp.tanh(c * (x + 0.044715 * x * x * x))
    return 0.5 * (1.0 + t) + 0.5 * x * (1.0 - t * t) * c * (1.0 + 3.0 * 0.044715 * x * x)


def _gelu_both(x):
    c = math.sqrt(2.0 / math.pi)
    t = jnp.tanh(c * (x + 0.044715 * x * x * x))
    return 0.5 * x * (1.0 + t), 0.5 * (1.0 + t) + 0.5 * x * (1.0 - t * t) * c * (1.0 + 3.0 * 0.044715 * x * x)


def _silu_both(z):
    s = _sigmoid(z)
    return z * s, s * (1.0 + z * (1.0 - s))


def _silu(z):
    return z * _sigmoid(z)


def matmul(a, b, mode, name, out_dtype=F32, add=None):
    if mode == "nn":
        (m, k), n = a.shape, b.shape[1]
    elif mode == "nt":
        (m, k), n = a.shape, b.shape[0]
    else:
        (k, m), n = a.shape, b.shape[1]
    tm = _pick(m, [t for t in (2048, 1024, 512, 384, 256, 128) if t * k * a.dtype.itemsize <= MM_BLOCK_BYTES])
    tn = _pick(n, [t for t in (512, 384, 256, 128) if t * k * b.dtype.itemsize <= MM_BLOCK_BYTES])
    dn = {"nn": DN_NN, "nt": DN_NT, "tn": DN_TN}[mode]

    def body(*refs):
        if add is None:
            a_ref, b_ref, o_ref = refs
        else:
            a_ref, b_ref, add_ref, o_ref = refs
        r = _dot(a_ref[...], b_ref[...], dn)
        if add is not None:
            r = r + add_ref[...].astype(F32)
        o_ref[...] = r.astype(out_dtype)

    a_spec = pl.BlockSpec((k, tm), lambda i, j: (0, i)) if mode == "tn" else pl.BlockSpec((tm, k), lambda i, j: (i, 0))
    b_spec = pl.BlockSpec((tn, k), lambda i, j: (j, 0)) if mode == "nt" else pl.BlockSpec((k, tn), lambda i, j: (0, j))
    o_spec = pl.BlockSpec((tm, tn), lambda i, j: (i, j))
    in_specs = [a_spec, b_spec] + ([o_spec] if add is not None else [])
    args = (a, b) + ((add,) if add is not None else ())
    return pl.pallas_call(
        body, name=name, grid=(m // tm, n // tn), in_specs=in_specs, out_specs=o_spec,
        out_shape=jax.ShapeDtypeStruct((m, n), out_dtype),
        compiler_params=_cparams(("parallel", "arbitrary")))(*args)


def _tile_rows(r, r0, cands=(512, 384, 256, 128)):
    return next(t for t in cands if r % t == 0 and r0 % t == 0)


def matmul_tn_packed(a, b, buf, r0, col_sharded, name):
    k, m = a.shape
    n = b.shape[1]
    if col_sharded:
        chunks = n // N_CHIPS // CHUNK_W
        tm = _tile_rows(m, r0, (1024, 512, 384, 256, 128))
        o_map = lambda i, j: (j // chunks, (r0 + (j % chunks) * m) // tm + i, 0)
    else:
        rs = m // N_CHIPS
        tm = _tile_rows(rs, r0)
        per = rs // tm
        o_map = lambda i, j: (i // per, (r0 + j * rs) // tm + i % per, 0)

    def body(a_ref, b_ref, buf_ref, o_ref):
        o_ref[0] = _dot(a_ref[...], b_ref[...], DN_TN)

    return pl.pallas_call(
        body, name=name, grid=(m // tm, n // CHUNK_W),
        in_specs=[pl.BlockSpec((k, tm), lambda i, j: (0, i)), pl.BlockSpec((k, CHUNK_W), lambda i, j: (0, j)),
                  pl.BlockSpec(memory_space=pl.ANY)],
        out_specs=pl.BlockSpec((1, tm, CHUNK_W), o_map), out_shape=jax.ShapeDtypeStruct(buf.shape, buf.dtype),
        input_output_aliases={2: 0}, compiler_params=_cparams(("parallel", "arbitrary")))(a, b, buf)


def _rows(tl, w, col=0):
    return pl.BlockSpec((tl, w), lambda i: (i, col))


def _full(shape):
    nd = len(shape)
    return pl.BlockSpec(tuple(shape), lambda i: (0,) * nd)


def _rowcall(body, name, n_steps, in_specs, out_specs, out_shape, scratch=()):
    return pl.pallas_call(
        body, name=name, grid=(n_steps,), in_specs=in_specs, out_specs=out_specs, out_shape=out_shape,
        scratch_shapes=list(scratch), compiler_params=_cparams(("arbitrary",)))


def _acc(ref, val, i):
    @pl.when(i == 0)
    def _():
        ref[...] = val

    @pl.when(i != 0)
    def _():
        ref[...] += val


def rms_fwd(h, g, name):
    l, d = h.shape
    tl = ROW_TILE_NARROW

    def body(h_ref, g_ref, o_ref):
        x = h_ref[...]
        r = lax.rsqrt(jnp.mean(x * x, axis=-1, keepdims=True) + NORM_EPS)
        o_ref[...] = (x * r * g_ref[...]).astype(BF16)

    return _rowcall(body, name, l // tl, [_rows(tl, d), _full((1, d))], _rows(tl, d),
                    jax.ShapeDtypeStruct((l, d), BF16))(h, g.reshape(1, d))


def rms_bwd(h, g, dhn, dh_in, name):
    l, d = h.shape
    tl = ROW_TILE_NARROW

    def body(h_ref, g_ref, dhn_ref, dhi_ref, dh_ref, dg_ref):
        i = pl.program_id(0)
        x = h_ref[...]
        r = lax.rsqrt(jnp.mean(x * x, axis=-1, keepdims=True) + NORM_EPS)
        xhat = x * r
        dy = dhn_ref[...]
        dxh = dy * g_ref[...]
        dx = r * (dxh - xhat * jnp.mean(dxh * xhat, axis=-1, keepdims=True))
        dh_ref[...] = dhi_ref[...] + dx
        _acc(dg_ref, jnp.sum(dy * xhat, axis=0, keepdims=True), i)

    return _rowcall(body, name, l // tl, [_rows(tl, d), _full((1, d)), _rows(tl, d), _rows(tl, d)],
                    [_rows(tl, d), _full((1, d))],
                    [jax.ShapeDtypeStruct((l, d), F32), jax.ShapeDtypeStruct((1, d), F32)])(h, g.reshape(1, d), dhn, dh_in)


def loss_head(h, g, target):
    l, d = h.shape
    tl = ROW_TILE_NARROW

    def body(h_ref, g_ref, t_ref, loss_ref, dh_ref, dg_ref):
        i = pl.program_id(0)
        x = h_ref[...]
        gg = g_ref[...]
        r = lax.rsqrt(jnp.mean(x * x, axis=-1, keepdims=True) + NORM_EPS)
        xhat = x * r
        err = xhat * gg - t_ref[...]
        part = 0.5 * jnp.sum(jnp.mean(err * err, axis=-1, keepdims=True), axis=0, keepdims=True)
        _acc(loss_ref, part, i)
        dy = err * (1.0 / d)
        dxh = dy * gg
        dh_ref[...] = r * (dxh - xhat * jnp.mean(dxh * xhat, axis=-1, keepdims=True))
        _acc(dg_ref, jnp.sum(dy * xhat, axis=0, keepdims=True), i)

    return _rowcall(body, "loss_head", l // tl, [_rows(tl, d), _full((1, d)), _rows(tl, d)],
                    [_full((1, 1)), _rows(tl, d), _full((1, d))],
                    [jax.ShapeDtypeStruct((1, 1), F32), jax.ShapeDtypeStruct((l, d), F32),
                     jax.ShapeDtypeStruct((1, d), F32)])(h, g.reshape(1, d), target)


def _gmlp_common(a_ref, lng_ref, lnb_ref):
    di = lng_ref.shape[1]
    u_pre = a_ref[:, 0:di]
    v_pre = a_ref[:, di:2 * di]
    z = a_ref[:, 2 * di:3 * di]
    vg = _gelu(v_pre)
    mu = jnp.mean(vg, axis=-1, keepdims=True)
    xc = vg - mu
    rstd = lax.rsqrt(jnp.mean(xc * xc, axis=-1, keepdims=True) + NORM_EPS)
    vhat = xc * rstd
    vn = vhat * lng_ref[...] + lnb_ref[...]
    return u_pre, v_pre, z, vhat, rstd, vn


def _tril(w):
    r = lax.broadcasted_iota(jnp.int32, w.shape, 0)
    c = lax.broadcasted_iota(jnp.int32, w.shape, 1)
    return jnp.where(c <= r, w, 0.0)


def gmlp_gate_fwd(a, ln_g, ln_b, w_s, b_s, name):
    l, w3 = a.shape
    di = w3 // 3
    dg = di // GMLP_GROUPS
    tl = GMLP_CHUNK

    def body(a_ref, lng_ref, lnb_ref, ws_ref, bs_ref, m_ref):
        u_pre, _, z, _, _, vn = _gmlp_common(a_ref, lng_ref, lnb_ref)
        gate = _gelu(u_pre) * _silu(z)
        for g in range(GMLP_GROUPS):
            sl = slice(g * dg, (g + 1) * dg)
            s = _dot(_tril(ws_ref[g]), vn[:, sl], DN_NN) + bs_ref[g]
            m_ref[:, sl] = (gate[:, sl] * s).astype(BF16)

    return _rowcall(body, name, l // tl,
                    [_rows(tl, w3), _full((1, di)), _full((1, di)), _full(w_s.shape), _full((GMLP_GROUPS, tl, 1))],
                    _rows(tl, di), jax.ShapeDtypeStruct((l, di), BF16))(
        a, ln_g.reshape(1, di), ln_b.reshape(1, di), w_s, b_s.reshape(GMLP_GROUPS, tl, 1))


def gmlp_gate_bwd(a, dm, ln_g, ln_b, w_s, b_s, name):
    l, w3 = a.shape
    di = w3 // 3
    dg = di // GMLP_GROUPS
    tl = GMLP_CHUNK

    def body(a_ref, dm_ref, lng_ref, lnb_ref, ws_ref, bs_ref, da_ref, dlg_ref, dlb_ref, dws_ref, dbs_ref,
             dvn_ref, vh_ref, gv_ref):
        i = pl.program_id(0)
        vg, gv = _gelu_both(a_ref[:, di:2 * di])
        gv_ref[...] = gv
        xc = vg - jnp.mean(vg, axis=-1, keepdims=True)
        rstd = lax.rsqrt(jnp.mean(xc * xc, axis=-1, keepdims=True) + NORM_EPS)
        vh_ref[...] = xc * rstd
        for g in range(GMLP_GROUPS):
            sl = slice(g * dg, (g + 1) * dg)
            wt = _tril(ws_ref[g])
            vn_g = vh_ref[:, sl] * lng_ref[:, sl] + lnb_ref[:, sl]
            s = _dot(wt, vn_g, DN_NN) + bs_ref[g]
            dmg = dm_ref[:, sl]
            u, gu = _gelu_both(a_ref[:, sl])
            sz, gz = _silu_both(a_ref[:, 2 * di + g * dg:2 * di + (g + 1) * dg])
            ds = dmg * u * sz
            da_ref[:, sl] = (dmg * s * sz * gu).astype(BF16)
            da_ref[:, 2 * di + g * dg:2 * di + (g + 1) * dg] = (dmg * u * s * gz).astype(BF16)
            dvn_ref[:, sl] = _dot(wt, ds, DN_TN)
            dw = _tril(_dot(ds, vn_g, DN_NT))
            db = jnp.sum(ds, axis=1, keepdims=True)

            @pl.when(i == 0)
            def _():
                dws_ref[g] = dw
                dbs_ref[g] = db

            @pl.when(i != 0)
            def _():
                dws_ref[g] += dw
                dbs_ref[g] += db

        dvn = dvn_ref[...]
        vhat = vh_ref[...]
        dxh = dvn * lng_ref[...]
        dvg = rstd * (dxh - jnp.mean(dxh, axis=-1, keepdims=True) - vhat * jnp.mean(dxh * vhat, axis=-1, keepdims=True))
        da_ref[:, di:2 * di] = (dvg * gv_ref[...]).astype(BF16)
        _acc(dlg_ref, jnp.sum(dvn * vhat, axis=0, keepdims=True), i)
        _acc(dlb_ref, jnp.sum(dvn, axis=0, keepdims=True), i)

    outs = _rowcall(
        body, name, l // tl,
        [_rows(tl, w3), _rows(tl, di), _full((1, di)), _full((1, di)), _full(w_s.shape), _full((GMLP_GROUPS, tl, 1))],
        [_rows(tl, w3), _full((1, di)), _full((1, di)), _full(w_s.shape), _full((GMLP_GROUPS, tl, 1))],
        [jax.ShapeDtypeStruct((l, w3), BF16), jax.ShapeDtypeStruct((1, di), F32), jax.ShapeDtypeStruct((1, di), F32),
         jax.ShapeDtypeStruct(w_s.shape, F32), jax.ShapeDtypeStruct((GMLP_GROUPS, tl, 1), F32)],
        scratch=[pltpu.VMEM((tl, di), F32)] * 3)(
        a, dm, ln_g.reshape(1, di), ln_b.reshape(1, di), w_s, b_s.reshape(GMLP_GROUPS, tl, 1))
    return outs


def gmlp_layer_fwd(h, p, wf, tag):
    hn = rms_fwd(h, p["norm_g"], tag + "_rms")
    a = matmul(hn, wf["w_in"], "nn", tag + "_mm_in")
    m = gmlp_gate_fwd(a, p["ln_g"], p["ln_b"], p["w_s"], p["b_s"], tag + "_gate")
    h_out = matmul(m, wf["w_out"], "nn", tag + "_mm_out", add=h)
    return h_out, (h, hn, a, m)


def gmlp_layer_bwd(dh_out, saved, p, wf, tag, sink):
    h, hn, a, m = saved
    dm = matmul(dh_out, wf["w_out"], "nt", tag + "_mm_dm")
    sink.mm("w_out", m, dh_out, tag + "_mm_gwout")
    da, dlg, dlb, dws, dbs = gmlp_gate_bwd(a, dm, p["ln_g"], p["ln_b"], p["w_s"], p["b_s"], tag + "_gate_bwd")
    dhn = matmul(da, wf["w_in"], "nt", tag + "_mm_dhn")
    sink.mm("w_in", hn, da, tag + "_mm_gwin")
    zero = sink.send()
    dh, dng = rms_bwd(h, p["norm_g"] + zero, dhn, dh_out, tag + "_rms_bwd")
    grads = {"norm_g": dng.reshape(-1), "ln_g": dlg.reshape(-1), "ln_b": dlb.reshape(-1),
             "w_s": dws, "b_s": dbs.reshape(GMLP_GROUPS, GMLP_CHUNK)}
    return dh, grads


def _cmul(ar, ai, br, bi):
    return ar * br - ai * bi, ar * bi + ai * br


S5_PG = 16


def _gblock(tail):
    return pl.BlockSpec((S5_PG,) + tuple(tail), lambda i: (i, 0, 0))


def s5_params_fwd(a_re, a_im, log_step, b_re, b_im):
    g, p, hh = b_re.shape

    def body(ar_ref, ai_ref, ls_ref, br_ref, bi_ref, lr_ref, li_ref, bbr_ref, bbi_ref):
        ar, ai = ar_ref[...], ai_ref[...]
        step = jnp.exp(ls_ref[...])
        mag = jnp.exp(ar * step)
        lr, li = mag * jnp.cos(ai * step), mag * jnp.sin(ai * step)
        den = 1.0 / (ar * ar + ai * ai)
        fr, fi = _cmul(lr - 1.0, li, ar * den, -ai * den)
        lr_ref[...] = lr
        li_ref[...] = li
        bbr, bbi = _cmul(fr, fi, br_ref[...], bi_ref[...])
        bbr_ref[...] = bbr
        bbi_ref[...] = bbi

    s1 = jax.ShapeDtypeStruct((g, p, 1), F32)
    s3 = jax.ShapeDtypeStruct((g, p, hh), F32)
    b1, b0, b3 = _gblock((p, 1)), _gblock((1, 1)), _gblock((p, hh))
    return pl.pallas_call(body, name="s5_params_fwd", grid=(g // S5_PG,), in_specs=[b1, b1, b0, b3, b3],
                          out_specs=[b1, b1, b3, b3], out_shape=[s1, s1, s3, s3],
                          compiler_params=_cparams(("parallel",)))(
        a_re.reshape(g, p, 1), a_im.reshape(g, p, 1), log_step.reshape(g, 1, 1), b_re, b_im)


def s5_params_bwd(a_re, a_im, log_step, b_re, b_im, dl_re, dl_im, dbb_re, dbb_im):
    g, p, hh = b_re.shape

    def body(ar_ref, ai_ref, ls_ref, br_ref, bi_ref, dlr_ref, dli_ref, dbr_ref, dbi_ref,
             gar_ref, gai_ref, gls_ref, gbr_ref, gbi_ref):
        ar, ai = ar_ref[...], ai_ref[...]
        step = jnp.exp(ls_ref[...])
        mag = jnp.exp(ar * step)
        lr, li = mag * jnp.cos(ai * step), mag * jnp.sin(ai * step)
        den = 1.0 / (ar * ar + ai * ai)
        ir, ii = ar * den, -ai * den
        fr, fi = _cmul(lr - 1.0, li, ir, ii)
        br, bi = br_ref[...], bi_ref[...]
        dbr, dbi = dbr_ref[...], dbi_ref[...]
        gbr, gbi = _cmul(fr, -fi, dbr, dbi)
        gbr_ref[...] = gbr
        gbi_ref[...] = gbi
        pr, pi = _cmul(br, -bi, dbr, dbi)
        gfr = jnp.sum(pr, axis=-1, keepdims=True)
        gfi = jnp.sum(pi, axis=-1, keepdims=True)
        t_r, t_i = _cmul(ir, -ii, gfr, gfi)
        glr, gli = dlr_ref[...] + t_r, dli_ref[...] + t_i
        c1r, c1i = _cmul(step * lr, -step * li, glr, gli)
        qr, qi = _cmul(fr, fi, ir, ii)
        c2r, c2i = _cmul(-qr, qi, gfr, gfi)
        gar_ref[...] = c1r + c2r
        gai_ref[...] = c1i + c2i
        wr, wi = _cmul(ar, ai, lr, li)
        sr, _ = _cmul(wr, -wi, glr, gli)
        gls_ref[...] = jnp.sum(sr, axis=1, keepdims=True) * step

    s1 = jax.ShapeDtypeStruct((g, p, 1), F32)
    s3 = jax.ShapeDtypeStruct((g, p, hh), F32)
    b1, b0, b3 = _gblock((p, 1)), _gblock((1, 1)), _gblock((p, hh))
    return pl.pallas_call(body, name="s5_params_bwd", grid=(g // S5_PG,),
                          in_specs=[b1, b1, b0, b3, b3, b1, b1, b3, b3], out_specs=[b1, b1, b0, b3, b3],
                          out_shape=[s1, s1, jax.ShapeDtypeStruct((g, 1, 1), F32), s3, s3],
                          compiler_params=_cparams(("parallel",)))(
        a_re.reshape(g, p, 1), a_im.reshape(g, p, 1), log_step.reshape(g, 1, 1), b_re, b_im,
        dl_re, dl_im, dbb_re, dbb_im)


def _blockdiag(t):
    sb, n, r, c = t.shape
    eye = jnp.eye(n, dtype=bool)[None, :, None, :, None]
    full = jnp.where(eye, t[:, :, :, None, :], jnp.zeros((), t.dtype))
    return full.reshape(sb, n * r, n * c)


def _blockdiag_extract(m, r, c):
    sb = m.shape[0]
    n = m.shape[1] // r
    m5 = m.reshape(sb, n, r, n, c)
    return jnp.stack([m5[:, i, :, i, :] for i in range(n)], axis=1)


S5_TB = 128
S5_UNROLL = 8


def _lam_power(pr, pi, n):
    for _ in range(int(math.log2(n))):
        pr, pi = _cmul(pr, pi, pr, pi)
    return pr, pi


def _segment_entries(er, ei, pr, pi, reverse):
    seg, ns = er.shape
    row = lax.broadcasted_iota(jnp.int32, (seg, ns), 0)
    cr = jnp.zeros((seg, ns), F32)
    ci = jnp.zeros((seg, ns), F32)
    cur_r = jnp.zeros((1, ns), F32)
    cur_i = jnp.zeros((1, ns), F32)
    for s in (range(seg - 2, -1, -1) if reverse else range(1, seg)):
        src = s + 1 if reverse else s - 1
        mr, mi = _cmul(pr, pi, cur_r, cur_i)
        cur_r = jnp.sum(jnp.where(row == src, er, 0.0), axis=0, keepdims=True) + mr
        cur_i = jnp.sum(jnp.where(row == src, ei, 0.0), axis=0, keepdims=True) + mi
        cr = jnp.where(row == s, cur_r, cr)
        ci = jnp.where(row == s, cur_i, ci)
    return cr, ci


def s5_scan_fused_fwd(a_p, lam_re, lam_im, wb_re, wb_im, wc_re, wc_im, d_skip, name):
    l = a_p.shape[0]
    di = d_skip.shape[1]
    rows = S5_SEG * S5_TB
    nb = l // rows
    ns = wb_re.shape[2]

    def body(u_ref, lr_ref, li_ref, wbr_ref, wbi_ref, wcr_ref, wci_ref, ds_ref, y_ref, ckr_ref, cki_ref, bur, bui):
        lr = jnp.broadcast_to(lr_ref[0], (S5_SEG, ns))
        li = jnp.broadcast_to(li_ref[0], (S5_SEG, ns))

        def scan_block(b, carry, keep):
            def step(t, c):
                xr, xi = c
                sl = pl.ds(pl.multiple_of(b * rows + t * S5_SEG, S5_SEG), S5_SEG)
                nr = lr * xr - li * xi + bur[sl, :]
                ni = lr * xi + li * xr + bui[sl, :]
                if keep:
                    bur[sl, :] = nr
                    bui[sl, :] = ni
                return nr, ni

            return lax.fori_loop(0, S5_TB, step, carry, unroll=S5_UNROLL)

        def project(b, carry):
            rs = pl.ds(pl.multiple_of(b * rows, rows), rows)
            u = u_ref[rs, :]
            bur[rs, :] = _dot(u, wbr_ref[0], DN_NN)
            bui[rs, :] = _dot(u, wbi_ref[0], DN_NN)
            return scan_block(b, carry, False)

        zero = jnp.zeros((S5_SEG, ns), F32)
        er, ei = lax.fori_loop(0, nb, project, (zero, zero))
        pr, pi = _lam_power(lr_ref[0], li_ref[0], l // S5_SEG)
        entry = _segment_entries(er, ei, pr, pi, False)

        def emit(b, carry):
            ckr_ref[0, b] = carry[0]
            cki_ref[0, b] = carry[1]
            carry = scan_block(b, carry, True)
            rs = pl.ds(pl.multiple_of(b * rows, rows), rows)
            y_ref[rs, :] = (_dot(bur[rs, :], wcr_ref[0], DN_NN) - _dot(bui[rs, :], wci_ref[0], DN_NN)
                            + ds_ref[...] * u_ref[rs, :])
            return carry

        lax.fori_loop(0, nb, emit, entry)

    sb3 = lambda s: (s, 0, 0)
    st = jax.ShapeDtypeStruct
    return pl.pallas_call(
        body, name=name, grid=(S5_SB,),
        in_specs=[pl.BlockSpec((l, LANES), lambda s: (0, s)),
                  pl.BlockSpec((1, 1, ns), sb3), pl.BlockSpec((1, 1, ns), sb3),
                  pl.BlockSpec((1, LANES, ns), sb3), pl.BlockSpec((1, LANES, ns), sb3),
                  pl.BlockSpec((1, ns, LANES), sb3), pl.BlockSpec((1, ns, LANES), sb3),
                  pl.BlockSpec((1, LANES), lambda s: (0, s))],
        out_specs=[pl.BlockSpec((l, LANES), lambda s: (0, s)),
                   pl.BlockSpec((1, nb, S5_SEG, ns), lambda s: (s, 0, 0, 0)),
                   pl.BlockSpec((1, nb, S5_SEG, ns), lambda s: (s, 0, 0, 0))],
        out_shape=[st((l, di), F32), st((S5_SB, nb, S5_SEG, ns), F32), st((S5_SB, nb, S5_SEG, ns), F32)],
        scratch_shapes=[pltpu.VMEM((l, ns), F32), pltpu.VMEM((l, ns), F32)],
        compiler_params=_cparams(("parallel",)))(a_p, lam_re, lam_im, wb_re, wb_im, wc_re, wc_im, d_skip)


def s5_scan_fused_bwd(a_p, dy, lam_re, lam_im, wb_re, wb_im, wc_re, wc_im, d_skip, ck_re, ck_im, name):
    l = a_p.shape[0]
    di = d_skip.shape[1]
    rows = S5_SEG * S5_TB
    nb = l // rows
    ns = wb_re.shape[2]

    def body(u_ref, dy_ref, lr_ref, li_ref, wbr_ref, wbi_ref, wcr_ref, wci_ref, ds_ref, ckr_ref, cki_ref,
             du_ref, dwbr_ref, dwbi_ref, dwcr_ref, dwci_ref, dds_ref, dlr_ref, dli_ref, gr, gi, xr_b, xi_b):
        lr = jnp.broadcast_to(lr_ref[0], (S5_SEG, ns))
        li = jnp.broadcast_to(li_ref[0], (S5_SEG, ns))

        def back_project(k, carry):
            b = nb - 1 - k
            rs = pl.ds(pl.multiple_of(b * rows, rows), rows)
            dyv = dy_ref[rs, :]
            gr[rs, :] = _dot(dyv, wcr_ref[0], DN_NT)
            gi[rs, :] = -_dot(dyv, wci_ref[0], DN_NT)

            def step(kk, c):
                ar, ai = c
                sl = pl.ds(pl.multiple_of(b * rows + (S5_TB - 1 - kk) * S5_SEG, S5_SEG), S5_SEG)
                return gr[sl, :] + lr * ar + li * ai, gi[sl, :] + lr * ai - li * ar

            return lax.fori_loop(0, S5_TB, step, carry, unroll=S5_UNROLL)

        zero = jnp.zeros((S5_SEG, ns), F32)
        er, ei = lax.fori_loop(0, nb, back_project, (zero, zero))
        pr, pi = _lam_power(lr_ref[0], -li_ref[0], l // S5_SEG)
        a0r, a0i = _segment_entries(er, ei, pr, pi, True)

        dwbr_ref[...] = jnp.zeros_like(dwbr_ref)
        dwbi_ref[...] = jnp.zeros_like(dwbi_ref)
        dwcr_ref[...] = jnp.zeros_like(dwcr_ref)
        dwci_ref[...] = jnp.zeros_like(dwci_ref)
        dds_ref[...] = jnp.zeros_like(dds_ref)

        def block(k, carry):
            b = nb - 1 - k
            rs = pl.ds(pl.multiple_of(b * rows, rows), rows)
            u = u_ref[rs, :]
            dyv = dy_ref[rs, :]
            body_rows = pl.ds(S5_SEG, rows)
            x0r, x0i = ckr_ref[0, b], cki_ref[0, b]
            xr_b[0:S5_SEG, :] = x0r
            xi_b[0:S5_SEG, :] = x0i
            xr_b[body_rows, :] = _dot(u, wbr_ref[0], DN_NN)
            xi_b[body_rows, :] = _dot(u, wbi_ref[0], DN_NN)

            def fstep(t, c):
                xr, xi = c
                sl = pl.ds(pl.multiple_of((t + 1) * S5_SEG, S5_SEG), S5_SEG)
                nr = lr * xr - li * xi + xr_b[sl, :]
                ni = lr * xi + li * xr + xi_b[sl, :]
                xr_b[sl, :] = nr
                xi_b[sl, :] = ni
                return nr, ni

            lax.fori_loop(0, S5_TB, fstep, (x0r, x0i), unroll=S5_UNROLL)
            dwcr_ref[0] += _dot(xr_b[body_rows, :], dyv, DN_TN)
            dwci_ref[0] -= _dot(xi_b[body_rows, :], dyv, DN_TN)

            def bstep(kk, c):
                ar, ai = c
                sl = pl.ds(pl.multiple_of(b * rows + (S5_TB - 1 - kk) * S5_SEG, S5_SEG), S5_SEG)
                nr = gr[sl, :] + lr * ar + li * ai
                ni = gi[sl, :] + lr * ai - li * ar
                gr[sl, :] = nr
                gi[sl, :] = ni
                return nr, ni

            ar, ai = lax.fori_loop(0, S5_TB, bstep, carry[:2], unroll=S5_UNROLL)
            a_r, a_i = gr[rs, :], gi[rs, :]
            p_r, p_i = xr_b[0:rows, :], xi_b[0:rows, :]
            per_seg = lambda v: jnp.sum(v.reshape(S5_TB, S5_SEG, ns), axis=0)
            carry = (ar, ai, carry[2] + per_seg(a_r * p_r + a_i * p_i), carry[3] + per_seg(a_i * p_r - a_r * p_i))
            du_ref[rs, :] = (_dot(a_r, wbr_ref[0], DN_NT) + _dot(a_i, wbi_ref[0], DN_NT) + ds_ref[...] * dyv).astype(BF16)
            dwbr_ref[0] += _dot(u, a_r, DN_TN)
            dwbi_ref[0] += _dot(u, a_i, DN_TN)
            dds_ref[...] += jnp.sum(dyv * u, axis=0, keepdims=True)
            return carry

        _, _, dlr, dli = lax.fori_loop(0, nb, block, (a0r, a0i, zero, zero))
        dlr_ref[0] = dlr
        dli_ref[0] = dli

    sb3 = lambda s: (s, 0, 0)
    seq = pl.BlockSpec((l, LANES), lambda s: (0, s))
    ck = pl.BlockSpec((1, nb, S5_SEG, ns), lambda s: (s, 0, 0, 0))
    st = jax.ShapeDtypeStruct
    return pl.pallas_call(
        body, name=name, grid=(S5_SB,),
        in_specs=[seq, seq, pl.BlockSpec((1, 1, ns), sb3), pl.BlockSpec((1, 1, ns), sb3),
                  pl.BlockSpec((1, LANES, ns), sb3), pl.BlockSpec((1, LANES, ns), sb3),
                  pl.BlockSpec((1, ns, LANES), sb3), pl.BlockSpec((1, ns, LANES), sb3),
                  pl.BlockSpec((1, LANES), lambda s: (0, s)), ck, ck],
        out_specs=[seq, pl.BlockSpec((1, LANES, ns), sb3), pl.BlockSpec((1, LANES, ns), sb3),
                   pl.BlockSpec((1, ns, LANES), sb3), pl.BlockSpec((1, ns, LANES), sb3),
                   pl.BlockSpec((1, LANES), lambda s: (0, s)),
                   pl.BlockSpec((1, S5_SEG, ns), sb3), pl.BlockSpec((1, S5_SEG, ns), sb3)],
        out_shape=[st((l, di), BF16), st((S5_SB, LANES, ns), F32), st((S5_SB, LANES, ns), F32),
                   st((S5_SB, ns, LANES), F32), st((S5_SB, ns, LANES), F32), st((1, di), F32),
                   st((S5_SB, S5_SEG, ns), F32), st((S5_SB, S5_SEG, ns), F32)],
        scratch_shapes=[pltpu.VMEM((l, ns), F32), pltpu.VMEM((l, ns), F32),
                        pltpu.VMEM((rows + S5_SEG, ns), F32), pltpu.VMEM((rows + S5_SEG, ns), F32)],
        compiler_params=_cparams(("parallel",)))(
        a_p, dy, lam_re, lam_im, wb_re, wb_im, wc_re, wc_im, d_skip, ck_re, ck_im)


def s5_act(y, name):
    l, d = y.shape
    tl = ROW_TILE

    def body(y_ref, o_ref):
        o_ref[...] = _gelu(y_ref[...]).astype(BF16)

    return _rowcall(body, name, l // tl, [_rows(tl, d)], _rows(tl, d), jax.ShapeDtypeStruct((l, d), BF16))(y)


def s5_gate_fwd(y, t, b_glu, a_p, name):
    l, d = y.shape
    tl = ROW_TILE

    def body(y_ref, t_ref, b_ref, z_ref, m_ref):
        yg = _gelu(y_ref[...])
        m_ref[...] = (yg * _sigmoid(t_ref[...] + b_ref[...]) * _silu(z_ref[...])).astype(BF16)

    return _rowcall(body, name, l // tl, [_rows(tl, d), _rows(tl, d), _full((1, d)), _rows(tl, d, 1)], _rows(tl, d),
                    jax.ShapeDtypeStruct((l, d), BF16))(y, t, b_glu.reshape(1, d), a_p)


def s5_gate_bwd(dm, y, t, b_glu, a_p, name):
    l, d = y.shape
    tl = ROW_TILE

    def body(dm_ref, y_ref, t_ref, b_ref, z_ref, dt_ref, dyg_ref, dz_ref, db_ref):
        i = pl.program_id(0)
        dmv = dm_ref[...]
        z = z_ref[...]
        yg = _gelu(y_ref[...])
        sg = _sigmoid(t_ref[...] + b_ref[...])
        y2 = yg * sg
        sz, gz = _silu_both(z)
        dy2 = dmv * sz
        dz_ref[...] = (dmv * y2 * gz).astype(BF16)
        dyg_ref[...] = dy2 * sg
        dt = dy2 * yg * sg * (1.0 - sg)
        dt_ref[...] = dt.astype(BF16)
        _acc(db_ref, jnp.sum(dt, axis=0, keepdims=True), i)

    st = jax.ShapeDtypeStruct
    return _rowcall(body, name, l // tl, [_rows(tl, d), _rows(tl, d), _rows(tl, d), _full((1, d)), _rows(tl, d, 1)],
                    [_rows(tl, d), _rows(tl, d), _rows(tl, d), _full((1, d))],
                    [st((l, d), BF16), st((l, d), F32), st((l, d), BF16), st((1, d), F32)])(
        dm, y, t, b_glu.reshape(1, d), a_p)


def s5_act_bwd(y, dyg_a, dyg_b, name):
    l, d = y.shape
    tl = ROW_TILE

    def body(y_ref, a_ref, b_ref, o_ref):
        o_ref[...] = (a_ref[...] + b_ref[...]) * _gelu_grad(y_ref[...])

    return _rowcall(body, name, l // tl, [_rows(tl, d)] * 3, _rows(tl, d), jax.ShapeDtypeStruct((l, d), F32))(y, dyg_a, dyg_b)


def _seg_perm(t):
    l, d = t.shape
    return t.reshape(S5_SEG, l // S5_SEG, d).transpose(1, 0, 2).reshape(l, d)


def _seg_unperm(t):
    l, d = t.shape
    return t.reshape(l // S5_SEG, S5_SEG, d).transpose(1, 0, 2).reshape(l, d)


def _s5_weights(p):
    lr, li, bbr, bbi = s5_params_fwd(p["a_re"], p["a_im"], p["log_step"], p["b_re"], p["b_im"])
    ns = 8 * S5_STATE
    lam_re = lr.reshape(S5_SB, 1, ns)
    lam_im = li.reshape(S5_SB, 1, ns)
    to_bd = lambda t: _blockdiag(t.reshape(S5_SB, 8, t.shape[1], t.shape[2]))
    wb_re = to_bd(bbr.transpose(0, 2, 1)).astype(BF16)
    wb_im = to_bd(bbi.transpose(0, 2, 1)).astype(BF16)
    wc_re = to_bd(p["c_re"].transpose(0, 2, 1)).astype(BF16)
    wc_im = to_bd(p["c_im"].transpose(0, 2, 1)).astype(BF16)
    return lam_re, lam_im, wb_re, wb_im, wc_re, wc_im


def s5_layer_fwd(h, p, wf, sw, tag):
    l = h.shape[0]
    di = p["d_skip"].shape[0]
    hn = rms_fwd(h, p["norm_g"], tag + "_rms")
    hn_p = _seg_perm(hn)
    a_p = matmul(hn_p, wf["w_in"], "nn", tag + "_mm_in")
    dsk = p["d_skip"].reshape(1, di)
    y, ck_re, ck_im = s5_scan_fused_fwd(a_p, *sw, dsk, tag + "_scan")
    yg = s5_act(y, tag + "_act")
    t = matmul(yg, wf["w_glu"], "nn", tag + "_mm_glu")
    m = s5_gate_fwd(y, t, p["b_glu"], a_p, tag + "_gate")
    out_p = matmul(m, wf["w_out"], "nn", tag + "_mm_out")
    h_out = residual_add(h, _seg_unperm(out_p), tag + "_res")
    return h_out, (h, hn_p, a_p, sw, ck_re, ck_im, y, yg, t, m)


def residual_add(h, y, name):
    l, d = h.shape
    tl = ROW_TILE_NARROW

    def body(h_ref, y_ref, o_ref):
        o_ref[...] = h_ref[...] + y_ref[...]

    return _rowcall(body, name, l // tl, [_rows(tl, d)] * 2, _rows(tl, d), jax.ShapeDtypeStruct((l, d), F32))(h, y)


def s5_layer_bwd(dh_out, saved, p, wf, tag, sink):
    h, hn_p, a_p, sw, ck_re, ck_im, y, yg, t, m = saved
    l = h.shape[0]
    di = p["d_skip"].shape[0]
    dsk = p["d_skip"].reshape(1, di)
    dout_p = _seg_perm(dh_out)
    dm = matmul(dout_p, wf["w_out"], "nt", tag + "_mm_dm")
    sink.mm("w_out", m, dout_p, tag + "_mm_gwout")
    dt, dyg_a, dz, db_glu = s5_gate_bwd(dm, y, t, p["b_glu"], a_p, tag + "_gate_bwd")
    dyg_b = matmul(dt, wf["w_glu"], "nt", tag + "_mm_dyg")
    sink.mm("w_glu", yg, dt, tag + "_mm_gwglu")
    dy = s5_act_bwd(y, dyg_a, dyg_b, tag + "_act_bwd")
    du, dwbr, dwbi, dwcr, dwci, dds, dlr, dli = s5_scan_fused_bwd(a_p, dy, *sw, dsk, ck_re, ck_im, tag + "_scanb")
    da = jnp.concatenate([du, dz], axis=1)
    dhn_p = matmul(da, wf["w_in"], "nt", tag + "_mm_dhn")
    sink.mm("w_in", hn_p, da, tag + "_mm_gwin")
    zero = sink.send()
    dh, dng = rms_bwd(h, p["norm_g"] + zero, _seg_unperm(dhn_p), dh_out, tag + "_rms_bwd")
    ex = lambda m_, r, c: _blockdiag_extract(m_, r, c).reshape(S5_GROUPS, r, c).transpose(0, 2, 1)
    dbb_re, dbb_im = ex(dwbr, S5_GROUP, S5_STATE), ex(dwbi, S5_GROUP, S5_STATE)
    g_c_re, g_c_im = ex(dwcr, S5_STATE, S5_GROUP), ex(dwci, S5_STATE, S5_GROUP)
    dl_re = lane_sum8(dlr).reshape(S5_GROUPS, S5_STATE, 1)
    dl_im = lane_sum8(dli).reshape(S5_GROUPS, S5_STATE, 1)
    gar, gai, gls, gbr, gbi = s5_params_bwd(p["a_re"], p["a_im"], p["log_step"], p["b_re"], p["b_im"],
                                            dl_re, dl_im, dbb_re, dbb_im)
    grads = {"norm_g": dng.reshape(-1), "a_re": gar.reshape(S5_GROUPS, S5_STATE),
             "a_im": gai.reshape(S5_GROUPS, S5_STATE), "log_step": gls.reshape(-1), "b_re": gbr, "b_im": gbi,
             "c_re": g_c_re, "c_im": g_c_im, "d_skip": dds.reshape(-1), "b_glu": db_glu.reshape(-1)}
    return dh, grads


def lane_sum8(t):
    sb, seg, ns = t.shape

    def body(t_ref, o_ref):
        o_ref[...] = jnp.sum(t_ref[...], axis=1, keepdims=True)

    return pl.pallas_call(body, name="s5_seg_sum", out_shape=jax.ShapeDtypeStruct((sb, 1, ns), F32))(t)


MLA_DI = MLA_HEADS * 128
MLA_CQ0 = MLA_DI
MLA_CKV0 = MLA_CQ0 + MLA_Q_RANK
MLA_KR0 = MLA_CKV0 + MLA_KV_RANK
MLA_AW = MLA_KR0 + LANES


def _rot_half(x):
    w = x.shape[-1]
    lane = lax.broadcasted_iota(jnp.int32, x.shape, x.ndim - 1)
    return jnp.where(lane % MLA_ROPE < MLA_ROPE // 2, pltpu.roll(x, w - MLA_ROPE // 2, x.ndim - 1),
                     pltpu.roll(x, MLA_ROPE // 2, x.ndim - 1))


def rope_tables(pos, zero):
    l = pos.shape[0]
    tl = ROW_TILE
    j = np.arange(LANES) % MLA_ROPE % (MLA_ROPE // 2)
    inv_freq = (ROPE_THETA ** (-(2.0 * j) / MLA_ROPE)).astype(np.float32).reshape(1, LANES)
    sign = np.where(np.arange(LANES) % MLA_ROPE < MLA_ROPE // 2, -1.0, 1.0).astype(np.float32).reshape(1, LANES)

    def body(p_ref, f_ref, s_ref, cos_ref, sin_ref):
        ang = p_ref[...].astype(F32) * f_ref[...]
        cos_ref[...] = jnp.cos(ang)
        sin_ref[...] = jnp.sin(ang) * s_ref[...]

    st = jax.ShapeDtypeStruct((l, LANES), F32)
    return _rowcall(body, "rope_tables", l // tl, [_rows(tl, 1), _full((1, LANES)), _full((1, LANES))],
                    [_rows(tl, LANES)] * 2, [st, st])(pos, jnp.asarray(inv_freq), jnp.asarray(sign) + zero)


def _rope(x, cos, sins):
    return x * cos + _rot_half(x) * sins


def _rope_t(dy, cos, sins):
    return dy * cos - sins * _rot_half(dy)


def _rmsn(x):
    r = lax.rsqrt(jnp.mean(x * x, axis=-1, keepdims=True) + NORM_EPS)
    return x * r, r


def mla_pre(a, q_g, kv_g, cos, sins, name):
    l = a.shape[0]
    tl = ROW_TILE

    def body(a_ref, qg_ref, kg_ref, cos_ref, sin_ref, cq_ref, ckv_ref, krs_ref):
        xq, _ = _rmsn(a_ref[:, MLA_CQ0:MLA_CKV0])
        cq_ref[...] = (xq * qg_ref[...]).astype(BF16)
        xk, _ = _rmsn(a_ref[:, MLA_CKV0:MLA_KR0])
        ckv_ref[...] = (xk * kg_ref[...]).astype(BF16)
        kr = a_ref[:, MLA_KR0:MLA_AW]
        kr2 = kr + pltpu.roll(kr, MLA_ROPE, 1)
        kr2 = _rope(kr2, cos_ref[...], sin_ref[...])
        lane = lax.broadcasted_iota(jnp.int32, kr2.shape, 1)
        krs_ref[0] = jnp.where(lane < MLA_ROPE, kr2, 0.0).astype(BF16)
        krs_ref[1] = jnp.where(lane >= MLA_ROPE, kr2, 0.0).astype(BF16)

    st = jax.ShapeDtypeStruct
    return _rowcall(body, name, l // tl,
                    [_rows(tl, MLA_AW), _full((1, MLA_Q_RANK)), _full((1, MLA_KV_RANK)), _rows(tl, LANES), _rows(tl, LANES)],
                    [_rows(tl, MLA_Q_RANK), _rows(tl, MLA_KV_RANK), pl.BlockSpec((2, tl, LANES), lambda i: (0, i, 0))],
                    [st((l, MLA_Q_RANK), BF16), st((l, MLA_KV_RANK), BF16), st((2, l, LANES), BF16)])(
        a, q_g.reshape(1, -1), kv_g.reshape(1, -1), cos, sins)


def mla_rope_q(qr, cos, sins, name):
    l, w = qr.shape
    tl = ROW_TILE

    def body(q_ref, cos_ref, sin_ref, o_ref):
        c, s = cos_ref[...], sin_ref[...]
        for p in range(w // LANES):
            sl = slice(p * LANES, (p + 1) * LANES)
            o_ref[:, sl] = _rope(q_ref[:, sl], c, s).astype(BF16)

    return _rowcall(body, name, l // tl, [_rows(tl, w), _rows(tl, LANES), _rows(tl, LANES)], _rows(tl, w),
                    jax.ShapeDtypeStruct((l, w), BF16))(qr, cos, sins)


ATT_OUT = 512
ATT_IN = 512
ATT_R = ATT_OUT // ATT_IN


def _scores(qn, qr, kn, kr, mask_off, transposed):
    q2 = jnp.concatenate([qn, qr], axis=1)
    k2 = jnp.concatenate([kn, kr], axis=1)
    s = (_dot(k2, q2, DN_NT) if transposed else _dot(q2, k2, DN_NT)) * MLA_SCALE
    if mask_off is None:
        return s
    r = lax.broadcasted_iota(jnp.int32, s.shape, 0)
    c = lax.broadcasted_iota(jnp.int32, s.shape, 1)
    return jnp.where((r <= c + mask_off) if transposed else (c + mask_off <= r), s, NEG_INF)


def _fold(x, op):
    out = x[:, :LANES]
    for t in range(1, x.shape[1] // LANES):
        out = op(out, x[:, t * LANES:(t + 1) * LANES])
    return out


def flash_fwd(qn, qr, kv, krs, name):
    l = qn.shape[0]
    nq = l // ATT_OUT

    def body(qn_ref, qr_ref, kv_ref, kr_ref, o_ref, lse_ref, s_buf):
        qi = pl.program_id(1)
        q_r = qr_ref[...]
        q_n = [qn_ref[:, hh * LANES:(hh + 1) * LANES] for hh in range(2)]

        def block_scores(j, mx, mask_off):
            sl = pl.ds(pl.multiple_of(j * ATT_IN, ATT_IN), ATT_IN)
            out = []
            for hh in range(2):
                s = _scores(q_n[hh], q_r, kv_ref[sl, 2 * hh * LANES:(2 * hh + 1) * LANES], kr_ref[hh, sl, :],
                            mask_off, False)
                s_buf[hh, j] = s
                out.append(jnp.maximum(mx[hh], _fold(s, jnp.maximum)))
            return tuple(out)

        ninf = jnp.full((ATT_OUT, LANES), NEG_INF, F32)
        mx = lax.fori_loop(0, ATT_R * qi, lambda j, c: block_scores(j, c, None), (ninf, ninf))
        for d in range(ATT_R):
            mx = block_scores(ATT_R * qi + d, mx, d * ATT_IN)
        m = [jnp.max(mx[hh], axis=-1, keepdims=True) for hh in range(2)]

        def block_pv(j, carry):
            sl = pl.ds(pl.multiple_of(j * ATT_IN, ATT_IN), ATT_IN)
            out = []
            for hh in range(2):
                ls, acc = carry[hh]
                p = jnp.exp(s_buf[hh, j] - m[hh])
                out.append((ls + _fold(p, jnp.add),
                            acc + _dot(p, kv_ref[sl, (2 * hh + 1) * LANES:(2 * hh + 2) * LANES], DN_NN)))
            return tuple(out)

        z = jnp.zeros((ATT_OUT, LANES), F32)
        res = lax.fori_loop(0, ATT_R * (qi + 1), block_pv, ((z, z), (z, z)))
        for hh in range(2):
            lsum = jnp.sum(res[hh][0], axis=-1, keepdims=True)
            o_ref[:, hh * LANES:(hh + 1) * LANES] = res[hh][1] / lsum
            lse_ref[hh] = m[hh] + jnp.log(lsum)

    st = jax.ShapeDtypeStruct
    return pl.pallas_call(
        body, name=name, grid=(MLA_HEADS // 2, nq),
        in_specs=[pl.BlockSpec((ATT_OUT, 2 * LANES), lambda p, i: (i, p)),
                  pl.BlockSpec((ATT_OUT, LANES), lambda p, i: (i, p)),
                  pl.BlockSpec((l, 4 * LANES), lambda p, i: (0, p)),
                  pl.BlockSpec((2, l, LANES), lambda p, i: (0, 0, 0))],
        out_specs=[pl.BlockSpec((ATT_OUT, 2 * LANES), lambda p, i: (i, p)),
                   pl.BlockSpec((2, ATT_OUT, 1), lambda p, i: (p, i, 0))],
        out_shape=[st((l, MLA_DI), F32), st((MLA_HEADS, l, 1), F32)],
        scratch_shapes=[pltpu.VMEM((2, l // ATT_IN, ATT_OUT, ATT_IN), F32)],
        compiler_params=_cparams(("parallel", "arbitrary")))(qn, qr, kv, krs)


def flash_dkv(qn, qr, kv, krs, do, lse_row, delta_row, name):
    l = qn.shape[0]
    nk = l // ATT_OUT
    nq = l // ATT_IN

    def body(qn_ref, qr_ref, do_ref, lse_ref, dl_ref, kv_ref, kr_ref, dkv_ref, dkr_ref):
        kj = pl.program_id(1)
        lane = lax.broadcasted_iota(jnp.int32, (ATT_OUT, LANES), 1)
        kn = [kv_ref[:, 2 * hh * LANES:(2 * hh + 1) * LANES] for hh in range(2)]
        v = [kv_ref[:, (2 * hh + 1) * LANES:(2 * hh + 2) * LANES] for hh in range(2)]

        def block(i, carry, mask_off):
            sl = pl.ds(pl.multiple_of(i * ATT_IN, ATT_IN), ATT_IN)
            q_r = qr_ref[sl, :]
            out = []
            for hh in range(2):
                dk2, dv = carry[hh]
                hs = slice(hh * LANES, (hh + 1) * LANES)
                q_n, d_o = qn_ref[sl, hs], do_ref[sl, hs]
                s = _scores(q_n, q_r, kn[hh], kr_ref[hh], mask_off, True)
                pt = jnp.exp(s - lse_ref[hh, i])
                dv = dv + _dot(pt, d_o, DN_NN)
                dpt = _dot(v[hh], d_o, DN_NT)
                dst = (pt * (dpt - dl_ref[hh, i]) * MLA_SCALE).astype(BF16)
                out.append((dk2 + _dot(dst, jnp.concatenate([q_n, q_r], axis=1), DN_NN), dv))
            return tuple(out)

        z = jnp.zeros((ATT_OUT, LANES), F32)
        z2 = jnp.zeros((ATT_OUT, 2 * LANES), F32)
        res = ((z2, z), (z2, z))
        for d in range(ATT_R):
            res = block(ATT_R * kj + d, res, d * ATT_IN)
        res = lax.fori_loop(ATT_R * (kj + 1), nq, lambda i, c: block(i, c, None), res)
        for hh in range(2):
            dkv_ref[:, 2 * hh * LANES:(2 * hh + 1) * LANES] = res[hh][0][:, :LANES].astype(BF16)
            dkv_ref[:, (2 * hh + 1) * LANES:(2 * hh + 2) * LANES] = res[hh][1].astype(BF16)
        dkr_ref[0] = jnp.where(lane < MLA_ROPE, res[0][0][:, LANES:], res[1][0][:, LANES:])

    st = jax.ShapeDtypeStruct
    return pl.pallas_call(
        body, name=name, grid=(MLA_HEADS // 2, nk),
        in_specs=[pl.BlockSpec((l, 2 * LANES), lambda p, j: (0, p)),
                  pl.BlockSpec((l, LANES), lambda p, j: (0, p)),
                  pl.BlockSpec((l, 2 * LANES), lambda p, j: (0, p)),
                  pl.BlockSpec((2, nq, 1, ATT_IN), lambda p, j: (p, 0, 0, 0)),
                  pl.BlockSpec((2, nq, 1, ATT_IN), lambda p, j: (p, 0, 0, 0)),
                  pl.BlockSpec((ATT_OUT, 4 * LANES), lambda p, j: (j, p)),
                  pl.BlockSpec((2, ATT_OUT, LANES), lambda p, j: (0, j, 0))],
        out_specs=[pl.BlockSpec((ATT_OUT, 4 * LANES), lambda p, j: (j, p)),
                   pl.BlockSpec((1, ATT_OUT, LANES), lambda p, j: (p, j, 0))],
        out_shape=[st((l, 2 * MLA_DI), BF16), st((MLA_HEADS // 2, l, LANES), F32)],
        compiler_params=_cparams(("parallel", "arbitrary")))(qn, qr, do, lse_row, delta_row, kv, krs)


def flash_dq(qn, qr, kv, krs, do, lse, delta, cos, sins, name):
    l = qn.shape[0]
    nq = l // ATT_OUT

    def body(qn_ref, qr_ref, do_ref, lse_ref, dl_ref, kv_ref, kr_ref, cos_ref, sin_ref, dqn_ref, dqr_ref):
        qi = pl.program_id(1)
        q_r = qr_ref[...]
        q_n = [qn_ref[:, hh * LANES:(hh + 1) * LANES] for hh in range(2)]
        d_o = [do_ref[:, hh * LANES:(hh + 1) * LANES] for hh in range(2)]
        lse_h = [lse_ref[hh] for hh in range(2)]
        dl_h = [dl_ref[hh] for hh in range(2)]

        def block(j, carry, mask_off):
            sl = pl.ds(pl.multiple_of(j * ATT_IN, ATT_IN), ATT_IN)
            dq2 = list(carry)
            for hh in range(2):
                kn = kv_ref[sl, 2 * hh * LANES:(2 * hh + 1) * LANES]
                v = kv_ref[sl, (2 * hh + 1) * LANES:(2 * hh + 2) * LANES]
                kr = kr_ref[hh, sl, :]
                s = _scores(q_n[hh], q_r, kn, kr, mask_off, False)
                pr = jnp.exp(s - lse_h[hh])
                dp = _dot(d_o[hh], v, DN_NT)
                ds = (pr * (dp - dl_h[hh]) * MLA_SCALE).astype(BF16)
                dq2[hh] = dq2[hh] + _dot(ds, jnp.concatenate([kn, kr], axis=1), DN_NN)
            return tuple(dq2)

        z2 = jnp.zeros((ATT_OUT, 2 * LANES), F32)
        res = lax.fori_loop(0, ATT_R * qi, lambda j, c: block(j, c, None), (z2, z2))
        for d in range(ATT_R):
            res = block(ATT_R * qi + d, res, d * ATT_IN)
        dqn_ref[:, 0:LANES] = res[0][:, :LANES].astype(BF16)
        dqn_ref[:, LANES:2 * LANES] = res[1][:, :LANES].astype(BF16)
        dqr = res[0][:, LANES:] + res[1][:, LANES:]
        dqr_ref[...] = _rope_t(dqr, cos_ref[...], sin_ref[...]).astype(BF16)

    st = jax.ShapeDtypeStruct
    return pl.pallas_call(
        body, name=name, grid=(MLA_HEADS // 2, nq),
        in_specs=[pl.BlockSpec((ATT_OUT, 2 * LANES), lambda p, i: (i, p)),
                  pl.BlockSpec((ATT_OUT, LANES), lambda p, i: (i, p)),
                  pl.BlockSpec((ATT_OUT, 2 * LANES), lambda p, i: (i, p)),
                  pl.BlockSpec((2, ATT_OUT, 1), lambda p, i: (p, i, 0)),
                  pl.BlockSpec((2, ATT_OUT, 1), lambda p, i: (p, i, 0)),
                  pl.BlockSpec((l, 4 * LANES), lambda p, i: (0, p)),
                  pl.BlockSpec((2, l, LANES), lambda p, i: (0, 0, 0)),
                  pl.BlockSpec((ATT_OUT, LANES), lambda p, i: (i, 0)),
                  pl.BlockSpec((ATT_OUT, LANES), lambda p, i: (i, 0))],
        out_specs=[pl.BlockSpec((ATT_OUT, 2 * LANES), lambda p, i: (i, p)),
                   pl.BlockSpec((ATT_OUT, LANES), lambda p, i: (i, p))],
        out_shape=[st((l, MLA_DI), BF16), st((l, MLA_HEADS * MLA_ROPE), BF16)],
        compiler_params=_cparams(("parallel", "arbitrary")))(qn, qr, do, lse, delta, kv, krs, cos, sins)


def mla_gate_fwd(o, a, name):
    l = o.shape[0]
    tl = ROW_TILE

    def body(o_ref, z_ref, m_ref):
        m_ref[...] = (o_ref[...] * _silu(z_ref[...])).astype(BF16)

    return _rowcall(body, name, l // tl, [_rows(tl, MLA_DI), _rows(tl, MLA_DI)], _rows(tl, MLA_DI),
                    jax.ShapeDtypeStruct((l, MLA_DI), BF16))(o, a)


def mla_gate_bwd(dm, o, a, name):
    l = o.shape[0]
    tl = ROW_TILE

    def body(dm_ref, o_ref, z_ref, do_ref, dz_ref, dl_ref):
        dmv, ov, z = dm_ref[...], o_ref[...], z_ref[...]
        sz, gz = _silu_both(z)
        d_o = dmv * sz
        do_ref[...] = d_o.astype(BF16)
        dz_ref[...] = (dmv * ov * gz).astype(BF16)
        pr = d_o * ov
        for h in range(MLA_HEADS):
            dl_ref[h] = jnp.sum(pr[:, h * LANES:(h + 1) * LANES], axis=1, keepdims=True)

    st = jax.ShapeDtypeStruct
    return _rowcall(body, name, l // tl, [_rows(tl, MLA_DI)] * 3,
                    [_rows(tl, MLA_DI), _rows(tl, MLA_DI), pl.BlockSpec((MLA_HEADS, tl, 1), lambda i: (0, i, 0))],
                    [st((l, MLA_DI), BF16), st((l, MLA_DI), BF16), st((MLA_HEADS, l, 1), F32)])(dm, o, a)


def mla_post(a, dcqn, dckvn, dkr_pairs, dz, q_g, kv_g, cos, sins, name):
    l = a.shape[0]
    tl = ROW_TILE
    npair = MLA_HEADS // 2

    def norm_bwd(x, g, dy):
        xhat, r = _rmsn(x)
        dxh = dy * g
        return r * (dxh - xhat * jnp.mean(dxh * xhat, axis=-1, keepdims=True)), jnp.sum(dy * xhat, axis=0, keepdims=True)

    def body(a_ref, dq_ref, dk_ref, dkr_ref, dz_ref, qg_ref, kg_ref, cos_ref, sin_ref, da_ref, dqg_ref, dkg_ref):
        i = pl.program_id(0)
        da_ref[:, 0:MLA_DI] = dz_ref[...]
        dcq, dqg = norm_bwd(a_ref[:, MLA_CQ0:MLA_CKV0], qg_ref[...], dq_ref[...])
        da_ref[:, MLA_CQ0:MLA_CKV0] = dcq.astype(BF16)
        dckv, dkg = norm_bwd(a_ref[:, MLA_CKV0:MLA_KR0], kg_ref[...], dk_ref[...])
        da_ref[:, MLA_CKV0:MLA_KR0] = dckv.astype(BF16)
        dk2 = dkr_ref[0]
        for p in range(1, npair):
            dk2 = dk2 + dkr_ref[p]
        dk2 = _rope_t(dk2, cos_ref[...], sin_ref[...])
        dk2 = dk2 + pltpu.roll(dk2, MLA_ROPE, 1)
        lane = lax.broadcasted_iota(jnp.int32, dk2.shape, 1)
        da_ref[:, MLA_KR0:MLA_AW] = jnp.where(lane < MLA_ROPE, dk2, 0.0).astype(BF16)
        _acc(dqg_ref, dqg, i)
        _acc(dkg_ref, dkg, i)

    st = jax.ShapeDtypeStruct
    return _rowcall(body, name, l // tl,
                    [_rows(tl, MLA_AW), _rows(tl, MLA_Q_RANK), _rows(tl, MLA_KV_RANK),
                     pl.BlockSpec((npair, tl, LANES), lambda i: (0, i, 0)), _rows(tl, MLA_DI),
                     _full((1, MLA_Q_RANK)), _full((1, MLA_KV_RANK)), _rows(tl, LANES), _rows(tl, LANES)],
                    [_rows(tl, MLA_AW), _full((1, MLA_Q_RANK)), _full((1, MLA_KV_RANK))],
                    [st((l, MLA_AW), BF16), st((1, MLA_Q_RANK), F32), st((1, MLA_KV_RANK), F32)])(
        a, dcqn, dckvn, dkr_pairs, dz, q_g.reshape(1, -1), kv_g.reshape(1, -1), cos, sins)


def _mla_w_in_perm(w):
    r = MLA_Q_RANK + MLA_KV_RANK + MLA_ROPE
    pad = jnp.zeros(w.shape[:-1] + (MLA_AW - MLA_KR0 - MLA_ROPE,), w.dtype)
    return jnp.concatenate([w[..., r:], w[..., :r], pad], axis=-1)


def _mla_w_in_unperm(g):
    r = MLA_Q_RANK + MLA_KV_RANK + MLA_ROPE
    return jnp.concatenate([g[..., MLA_DI:MLA_DI + r], g[..., :MLA_DI]], axis=-1)


def _mla_w_uq_split(w):
    k = w.shape[0]
    w3 = w.reshape(k, MLA_HEADS, MLA_NOPE + MLA_ROPE)
    return w3[:, :, :MLA_NOPE].reshape(k, MLA_HEADS * MLA_NOPE), w3[:, :, MLA_NOPE:].reshape(k, MLA_HEADS * MLA_ROPE)


def _mla_w_uq_merge(gn, gr):
    k = gn.shape[0]
    return jnp.concatenate([gn.reshape(k, MLA_HEADS, MLA_NOPE), gr.reshape(k, MLA_HEADS, MLA_ROPE)], axis=2).reshape(k, -1)


def mla_layer_fwd(h, p, wf, cos, sins, tag):
    hn = rms_fwd(h, p["norm_g"], tag + "_rms")
    w_in = _mla_w_in_perm(wf["w_in"])
    w_uq_n, w_uq_r = _mla_w_uq_split(wf["w_uq"])
    a = matmul(hn, w_in, "nn", tag + "_mm_in")
    cqn, ckvn, krs = mla_pre(a, p["q_norm_g"], p["kv_norm_g"], cos, sins, tag + "_pre")
    qn = matmul(cqn, w_uq_n, "nn", tag + "_mm_qn", out_dtype=BF16)
    qr_raw = matmul(cqn, w_uq_r, "nn", tag + "_mm_qr")
    qr = mla_rope_q(qr_raw, cos, sins, tag + "_rope_q")
    kv = matmul(ckvn, wf["w_ukv"], "nn", tag + "_mm_kv", out_dtype=BF16)
    o, lse = flash_fwd(qn, qr, kv, krs, tag + "_flash")
    m = mla_gate_fwd(o, a, tag + "_gate")
    h_out = matmul(m, wf["w_out"], "nn", tag + "_mm_out", add=h)
    return h_out, (h, hn, a, cqn, ckvn, krs, qn, qr, kv, o, lse, m, w_in, w_uq_n, w_uq_r)


def mla_layer_bwd(dh_out, saved, p, wf, cos, sins, tag, sink):
    h, hn, a, cqn, ckvn, krs, qn, qr, kv, o, lse, m, w_in, w_uq_n, w_uq_r = saved
    l = h.shape[0]
    dm = matmul(dh_out, wf["w_out"], "nt", tag + "_mm_dm")
    sink.mm("w_out", m, dh_out, tag + "_mm_gwout")
    do, dz, delta = mla_gate_bwd(dm, o, a, tag + "_gate_bwd")
    lse_row = lse.reshape(MLA_HEADS, l // ATT_IN, 1, ATT_IN)
    delta_row = delta.reshape(MLA_HEADS, l // ATT_IN, 1, ATT_IN)
    dkv, dkr_pairs = flash_dkv(qn, qr, kv, krs, do, lse_row, delta_row, tag + "_flash_dkv")
    dqn, dqr = flash_dq(qn, qr, kv, krs, do, lse, delta, cos, sins, tag + "_flash_dq")
    dcqn = matmul(dqn, w_uq_n, "nt", tag + "_mm_dcq_n")
    dcqn = matmul(dqr, w_uq_r, "nt", tag + "_mm_dcq_r", add=dcqn)
    g_uq_n = matmul(cqn, dqn, "tn", tag + "_mm_guq_n")
    g_uq_r = matmul(cqn, dqr, "tn", tag + "_mm_guq_r")
    dckvn = matmul(dkv, wf["w_ukv"], "nt", tag + "_mm_dckv")
    sink.mm("w_ukv", ckvn, dkv, tag + "_mm_gukv")
    da, dqg, dkg = mla_post(a, dcqn, dckvn, dkr_pairs, dz, p["q_norm_g"], p["kv_norm_g"], cos, sins, tag + "_post")
    dhn = matmul(da, w_in, "nt", tag + "_mm_dhn")
    g_w_in = matmul(hn, da, "tn", tag + "_mm_gwin")
    sink.put("w_uq", _mla_w_uq_merge(g_uq_n, g_uq_r))
    sink.put("w_in", _mla_w_in_unperm(g_w_in))
    zero = sink.send()
    dh, dng = rms_bwd(h, p["norm_g"] + zero, dhn, dh_out, tag + "_rms_bwd")
    grads = {"norm_g": dng.reshape(-1), "q_norm_g": dqg.reshape(-1), "kv_norm_g": dkg.reshape(-1)}
    return dh, grads


ANY = pl.BlockSpec(memory_space=pl.ANY)


def _me():
    return lax.axis_index("x"), lax.axis_index("y"), lax.axis_index("c")


def _chip():
    return 2 * lax.axis_index("x") + lax.axis_index("y")


def _other_chips(x, y):
    return [(1 - x, y), (x, 1 - y), (1 - x, 1 - y)]


def _rcopy(src, dst, ssem, rsem, dev):
    return pltpu.make_async_remote_copy(src_ref=src, dst_ref=dst, send_sem=ssem, recv_sem=rsem,
                                        device_id=dev, device_id_type=MESH)


def _half(ref, c, hf):
    return ref.at[pl.ds(c * hf, hf), :]


HBM = pl.BlockSpec(memory_space=pltpu.HBM)
SEM = pl.BlockSpec(memory_space=pltpu.SEMAPHORE)
SPLIT_EFFECT = pltpu.SideEffectType.DATAFLOW_SIDE_EFFECTING


def gather_start(wb, after, name):
    nr, w = wb.shape
    hf = nr // 2

    def body(w_ref, land_ref, after_ref, ssem, rsem, w_thru, land_thru, token):
        x, y, c = _me()
        k = 2 * x + y
        for j, (cx, cy) in enumerate(_other_chips(x, y)):
            _rcopy(_half(w_ref, c, hf), _half(land_ref.at[k], c, hf), ssem.at[j], rsem.at[j], (cx, cy, c)).start()
        token[...] = jnp.zeros_like(token)

    land = lax.empty((N_CHIPS, nr, w), wb.dtype)
    return pl.pallas_call(
        body, name=name,
        out_shape=(pltpu.SemaphoreType.DMA((3,)), pltpu.SemaphoreType.DMA((3,)), pltpu.HBM(wb.shape, wb.dtype),
                   pltpu.HBM(land.shape, land.dtype), jax.ShapeDtypeStruct((8, LANES), F32)),
        in_specs=(HBM, HBM, ANY), out_specs=(SEM, SEM, HBM, HBM, pl.BlockSpec(memory_space=pltpu.VMEM)),
        input_output_aliases={0: 2, 1: 3},
        compiler_params=pltpu.CompilerParams(has_side_effects=SPLIT_EFFECT))(
        pltpu.with_memory_space_constraint(wb, pltpu.HBM), pltpu.with_memory_space_constraint(land, pltpu.HBM), after)


def gather_wait(ssem, rsem, w_thru, land_thru, after, name):
    nr, w = w_thru.shape
    hf = nr // 2

    def body(w_ref, land_ref, ssem_ref, rsem_ref, after_ref, w_dead, got_ref):
        x, y, c = _me()
        for j, (cx, cy) in enumerate(_other_chips(x, y)):
            cp = _rcopy(_half(w_ref, c, hf), _half(land_ref.at[2 * cx + cy], c, hf), ssem_ref.at[j], rsem_ref.at[j],
                        (cx, cy, c))
            cp.wait_send()
            cp.wait_recv()

    return pl.pallas_call(
        body, name=name, out_shape=(pltpu.HBM(w_thru.shape, w_thru.dtype), pltpu.HBM(land_thru.shape, land_thru.dtype)),
        in_specs=(HBM, HBM, SEM, SEM, ANY), out_specs=(HBM, HBM), input_output_aliases={0: 0, 1: 1},
        compiler_params=pltpu.CompilerParams(has_side_effects=SPLIT_EFFECT))(w_thru, land_thru, ssem, rsem, after)[1]


def gather_handover(land, wb, name):
    _, nr, w = land.shape
    hf = nr // 2

    def body(l_ref, o_ref, ssem, rsem):
        x, y, c = _me()
        chips = _other_chips(x, y)
        sends = []
        for j, (cx, cy) in enumerate(chips):
            region = _half(o_ref.at[2 * cx + cy], c, hf)
            sends.append(_rcopy(region, region, ssem.at[j], rsem.at[j], (x, y, 1 - c)))
            sends[-1].start()
        for j, (cx, cy) in enumerate(chips):
            region = _half(o_ref.at[2 * cx + cy], 1 - c, hf)
            _rcopy(region, region, ssem.at[j], rsem.at[j], (x, y, 1 - c)).wait_recv()
        for cp in sends:
            cp.wait_send()

    out = pl.pallas_call(
        body, name=name, in_specs=[ANY], out_specs=ANY, input_output_aliases={0: 0},
        out_shape=jax.ShapeDtypeStruct(land.shape, land.dtype),
        scratch_shapes=[pltpu.SemaphoreType.DMA((3,)), pltpu.SemaphoreType.DMA((3,))])(land)
    return lax.dynamic_update_slice(out, wb[None], (_chip(), 0, 0))


def reduce_start(t, after, name):
    def body(t_ref, land_ref, after_ref, ssem, rsem, t_thru, land_thru, token):
        x, y, c = _me()
        k = 2 * x + y
        for j, (cx, cy) in enumerate(_other_chips(x, y)):
            _rcopy(t_ref.at[2 * cx + cy], land_ref.at[k], ssem.at[j], rsem.at[j], (cx, cy, c)).start()
        token[...] = jnp.zeros_like(token)

    land = lax.empty(t.shape, t.dtype)
    return pl.pallas_call(
        body, name=name,
        out_shape=(pltpu.SemaphoreType.DMA((3,)), pltpu.SemaphoreType.DMA((3,)), pltpu.HBM(t.shape, t.dtype),
                   pltpu.HBM(t.shape, t.dtype), jax.ShapeDtypeStruct((8, LANES), F32)),
        in_specs=(HBM, HBM, ANY), out_specs=(SEM, SEM, HBM, HBM, pl.BlockSpec(memory_space=pltpu.VMEM)),
        input_output_aliases={0: 2, 1: 3},
        compiler_params=pltpu.CompilerParams(has_side_effects=SPLIT_EFFECT))(
        pltpu.with_memory_space_constraint(t, pltpu.HBM), pltpu.with_memory_space_constraint(land, pltpu.HBM), after)


def bcast_start(g, after, name):
    def body(g_ref, land_ref, after_ref, ssem, rsem, g_thru, land_thru, token):
        x, y, c = _me()
        k = 2 * x + y
        for j, (cx, cy) in enumerate(_other_chips(x, y)):
            _rcopy(g_ref, land_ref.at[k], ssem.at[j], rsem.at[j], (cx, cy, c)).start()
        token[...] = jnp.zeros_like(token)

    land = lax.empty((N_CHIPS,) + g.shape, g.dtype)
    return pl.pallas_call(
        body, name=name,
        out_shape=(pltpu.SemaphoreType.DMA((3,)), pltpu.SemaphoreType.DMA((3,)), pltpu.HBM(g.shape, g.dtype),
                   pltpu.HBM(land.shape, land.dtype), jax.ShapeDtypeStruct((8, LANES), F32)),
        in_specs=(HBM, HBM, ANY), out_specs=(SEM, SEM, HBM, HBM, pl.BlockSpec(memory_space=pltpu.VMEM)),
        input_output_aliases={0: 2, 1: 3},
        compiler_params=pltpu.CompilerParams(has_side_effects=SPLIT_EFFECT))(
        pltpu.with_memory_space_constraint(g, pltpu.HBM), pltpu.with_memory_space_constraint(land, pltpu.HBM), after)


def bcast_wait(ssem, rsem, g_thru, land_thru, after, name):
    def body(g_ref, land_ref, ssem_ref, rsem_ref, after_ref, g_out, got_ref):
        x, y, c = _me()
        for j, (cx, cy) in enumerate(_other_chips(x, y)):
            cp = _rcopy(g_ref, land_ref.at[2 * cx + cy], ssem_ref.at[j], rsem_ref.at[j], (cx, cy, c))
            cp.wait_send()
            cp.wait_recv()

    g, land = pl.pallas_call(
        body, name=name, out_shape=(pltpu.HBM(g_thru.shape, g_thru.dtype), pltpu.HBM(land_thru.shape, land_thru.dtype)),
        in_specs=(HBM, HBM, SEM, SEM, ANY), out_specs=(HBM, HBM), input_output_aliases={0: 0, 1: 1},
        compiler_params=pltpu.CompilerParams(has_side_effects=SPLIT_EFFECT))(g_thru, land_thru, ssem, rsem, after)
    return lax.dynamic_update_slice(land, g[None], (_chip(), 0, 0))


def reduce_wait(ssem, rsem, t_thru, land_thru, after, name):
    def body(t_ref, land_ref, ssem_ref, rsem_ref, after_ref, t_out, got_ref):
        x, y, c = _me()
        k = 2 * x + y
        for j, (cx, cy) in enumerate(_other_chips(x, y)):
            cp = _rcopy(t_ref.at[k], land_ref.at[2 * cx + cy], ssem_ref.at[j], rsem_ref.at[j], (cx, cy, c))
            cp.wait_send()
            cp.wait_recv()

    return pl.pallas_call(
        body, name=name, out_shape=(pltpu.HBM(t_thru.shape, t_thru.dtype), pltpu.HBM(land_thru.shape, land_thru.dtype)),
        in_specs=(HBM, HBM, SEM, SEM, ANY), out_specs=(HBM, HBM), input_output_aliases={0: 0, 1: 1},
        compiler_params=pltpu.CompilerParams(has_side_effects=SPLIT_EFFECT))(t_thru, land_thru, ssem, rsem, after)


def grads_to_sibling(ps, name="grads_to_sibling"):
    n = len(ps)

    def body(*refs):
        p_refs, o_refs, ssem, rsem = refs[:n], refs[n:2 * n], refs[2 * n], refs[2 * n + 1]
        x, y, c = _me()
        cps = []
        for a in range(n):
            hf = ps[a].shape[1] // 2
            cps.append(_rcopy(p_refs[a].at[:, pl.ds((1 - c) * hf, hf), :], o_refs[a], ssem.at[a], rsem.at[a],
                              (x, y, 1 - c)))
        for cp in cps:
            cp.start()
        for cp in cps:
            cp.wait()

    return pl.pallas_call(
        body, name=name, in_specs=[ANY] * n, out_specs=[ANY] * n,
        out_shape=[jax.ShapeDtypeStruct((N_CHIPS, p.shape[1] // 2, p.shape[2]), p.dtype) for p in ps],
        scratch_shapes=[pltpu.SemaphoreType.DMA((n,)), pltpu.SemaphoreType.DMA((n,))])(*ps)


def pair_sum(p, ra, out_dtype, name):
    _, nr, w = p.shape
    hf = nr // 2
    tr = _pick_rows(hf, cap=max(512, SUM_BLOCK_BYTES // (4 * w)))
    nb = hf // tr

    def body(c_ref, p_ref, r_ref, o_ref):
        o_ref[...] = (p_ref[...] + r_ref[...]).astype(out_dtype)

    c = lax.axis_index("c").astype(jnp.int32).reshape(1)
    return pl.pallas_call(
        body, name=name,
        grid_spec=pltpu.PrefetchScalarGridSpec(
            num_scalar_prefetch=1, grid=(N_CHIPS, nb),
            in_specs=[pl.BlockSpec((1, tr, w), lambda k, i, c_ref: (k, c_ref[0] * nb + i, 0)),
                      pl.BlockSpec((1, tr, w), lambda k, i, c_ref: (k, i, 0))],
            out_specs=pl.BlockSpec((1, tr, w), lambda k, i, c_ref: (k, i, 0))),
        out_shape=jax.ShapeDtypeStruct((N_CHIPS, hf, w), out_dtype),
        compiler_params=_cparams(("parallel", "parallel")))(c, p, ra)


def grads_across_chips(ts):
    n = len(ts)

    def body(*refs):
        t_refs, o_refs, ssem, rsem = refs[:n], refs[n:2 * n], refs[2 * n], refs[2 * n + 1]
        x, y, c = _me()
        k = 2 * x + y
        chips = _other_chips(x, y)
        sends = [_rcopy(t_refs[a].at[2 * cx + cy], o_refs[a].at[k], ssem.at[3 * a + j], rsem.at[3 * a + j], (cx, cy, c))
                 for a in range(n) for j, (cx, cy) in enumerate(chips)]
        for cp in sends:
            cp.start()
        for a in range(n):
            for j, (cx, cy) in enumerate(chips):
                _rcopy(t_refs[a].at[k], o_refs[a].at[2 * cx + cy], ssem.at[3 * a + j], rsem.at[3 * a + j],
                       (cx, cy, c)).wait_recv()
        for cp in sends:
            cp.wait_send()

    return pl.pallas_call(
        body, name="grads_across_chips", in_specs=[ANY] * n, out_specs=[ANY] * n,
        out_shape=[jax.ShapeDtypeStruct(t.shape, t.dtype) for t in ts],
        scratch_shapes=[pltpu.SemaphoreType.DMA((3 * n,)), pltpu.SemaphoreType.DMA((3 * n,))])(*ts)


def chip_sum(t, rb, name):
    _, hf, w = rb.shape
    tr = _pick_rows(hf, cap=max(512, SUM_BLOCK_BYTES // (4 * w)))
    nb = hf // tr

    def body(kc_ref, t_ref, r_ref, o_ref):
        k = kc_ref[0]
        acc = jnp.where(k == 0, t_ref[0], r_ref[0]).astype(F32)
        for j in range(1, N_CHIPS):
            acc = acc + jnp.where(k == j, t_ref[0], r_ref[j]).astype(F32)
        o_ref[...] = acc

    kc = jnp.stack([_chip(), lax.axis_index("c")]).astype(jnp.int32)
    return pl.pallas_call(
        body, name=name,
        grid_spec=pltpu.PrefetchScalarGridSpec(
            num_scalar_prefetch=1, grid=(nb,),
            in_specs=[pl.BlockSpec((1, tr, w), lambda i, kc_ref: (kc_ref[0], i, 0)),
                      pl.BlockSpec((N_CHIPS, tr, w), lambda i, kc_ref: (0, i, 0))],
            out_specs=pl.BlockSpec((tr, w), lambda i, kc_ref: (kc_ref[1] * nb + i, 0))),
        out_shape=jax.ShapeDtypeStruct((2 * hf, w), F32), compiler_params=_cparams(("parallel",)))(kc, t, rb)


def reduced_to_sibling(gs):
    n = len(gs)

    def body(*refs):
        o_refs, ssem, rsem = refs[n:2 * n], refs[2 * n], refs[2 * n + 1]
        x, y, c = _me()
        cps = []
        for a in range(n):
            hf = gs[a].shape[0] // 2
            cps.append(_rcopy(_half(o_refs[a], c, hf), _half(o_refs[a], c, hf), ssem.at[a], rsem.at[a], (x, y, 1 - c)))
        for cp in cps:
            cp.start()
        for a in range(n):
            hf = gs[a].shape[0] // 2
            _rcopy(_half(o_refs[a], c, hf), _half(o_refs[a], 1 - c, hf), ssem.at[a], rsem.at[a],
                   (x, y, 1 - c)).wait_recv()
        for cp in cps:
            cp.wait_send()

    return pl.pallas_call(
        body, name="reduced_to_sibling", in_specs=[ANY] * n, out_specs=[ANY] * n,
        input_output_aliases={a: a for a in range(n)},
        out_shape=[jax.ShapeDtypeStruct(g.shape, g.dtype) for g in gs],
        scratch_shapes=[pltpu.SemaphoreType.DMA((n,)), pltpu.SemaphoreType.DMA((n,))])(*gs)


def _adamw_step(w_ref, g_ref, m_ref, v_ref, d_ref, nm_ref, nv_ref):
    bc1 = 1.0 - ADAM_B1 ** ADAM_STEP
    bc2 = 1.0 - ADAM_B2 ** ADAM_STEP
    gv = g_ref[...]
    nm = ADAM_B1 * m_ref[...] + (1.0 - ADAM_B1) * gv
    nv = ADAM_B2 * v_ref[...] + (1.0 - ADAM_B2) * (gv * gv)
    nm_ref[...] = nm
    nv_ref[...] = nv
    d_ref[...] = -ADAM_LR * ((nm / bc1) / (jnp.sqrt(nv / bc2) + ADAM_EPS) + ADAM_WD * w_ref[...])


def adamw_packed(w, g_buf, r0, m, v, name):
    r, c = w.shape
    tr = _tile_rows(r, r0)

    def body(w_ref, g_ref, m_ref, v_ref, go_ref, d_ref, nm_ref, nv_ref):
        go_ref[...] = g_ref[...]
        _adamw_step(w_ref, g_ref, m_ref, v_ref, d_ref, nm_ref, nv_ref)

    own = pl.BlockSpec((tr, CHUNK_W), lambda i, j: (i, j))
    packed = pl.BlockSpec((tr, CHUNK_W), lambda i, j: ((r0 + j * r) // tr + i, 0))
    st = jax.ShapeDtypeStruct((r, c), F32)
    return pl.pallas_call(body, name=name, grid=(r // tr, c // CHUNK_W), in_specs=[own, packed, own, own],
                          out_specs=[own] * 4, out_shape=[st] * 4,
                          compiler_params=_cparams(("parallel", "parallel")))(w, g_buf, m, v)


def adamw(w, g, m, v, name):
    r, wd = w.shape
    tr = _pick_rows(r, cap=max(16, ADAMW_BLOCK_BYTES // (4 * wd)))
    body = functools.partial(_adamw_step)

    spec = pl.BlockSpec((tr, wd), lambda i: (i, 0))
    st = jax.ShapeDtypeStruct((r, wd), F32)
    return pl.pallas_call(body, name=name, grid=(r // tr,), in_specs=[spec] * 4, out_specs=[spec] * 3,
                          out_shape=[st, st, st], compiler_params=_cparams(("parallel",)))(w, g, m, v)


LAYER_KINDS = ("gmlp", "s5", "mla", "gmlp")
PARAMS = {
    "gmlp": ("norm_g", "w_in", "ln_g", "ln_b", "w_s", "b_s", "w_out"),
    "s5": ("norm_g", "w_in", "a_re", "a_im", "log_step", "b_re", "b_im", "c_re", "c_im", "d_skip", "w_glu", "b_glu", "w_out"),
    "mla": ("norm_g", "w_in", "q_norm_g", "w_uq", "kv_norm_g", "w_ukv", "w_out"),
}
COL_SHARDED = ("w_in", "w_uq", "w_ukv")
ROW_SHARDED = ("w_out", "w_glu")
WEIGHT_NAMES = [("l%d_" % i) + n for i, kind in enumerate(LAYER_KINDS) for n in PARAMS[kind]] + ["final_norm_g"]


def _is_big(name):
    return name.split("_", 1)[1] in COL_SHARDED + ROW_SHARDED


BIG = [n for n in WEIGHT_NAMES if _is_big(n)]
SMALL = [n for n in WEIGHT_NAMES if not _is_big(n)]


def _pack_rows(blocks):
    return jnp.concatenate([b.reshape(-1, PACK_W) for b in blocks], axis=0)


def _shard_major(wn, full, width):
    r, c = full.shape
    if wn in COL_SHARDED:
        t = full.reshape(r, N_CHIPS, c // N_CHIPS).transpose(1, 0, 2)
    else:
        t = full.reshape(N_CHIPS, r // N_CHIPS, c)
    return t.reshape(N_CHIPS, -1, width)


def _from_shard_major(name, t, block_shape):
    r, c = block_shape
    if name.split("_", 1)[1] in COL_SHARDED:
        return t.reshape(N_CHIPS, r, c).transpose(1, 0, 2).reshape(r, N_CHIPS * c)
    return t.reshape(N_CHIPS * r, c)


class BigGradSink:
    ORDER = ("w_out", "w_glu", "w_ukv", "w_uq", "w_in")
    ROW_MAJOR = {2: ("w_uq", "w_in")}

    def __init__(self, layer, block_shapes):
        self.layer = layer
        self.regions = {}
        r0 = 0
        for wn in self.ORDER:
            if wn in block_shapes:
                shape = block_shapes[wn]
                self.regions[wn] = (r0, shape, wn not in self.ROW_MAJOR.get(layer, ()))
                r0 += shape[0] * shape[1] // CHUNK_W
        self.buf = lax.empty((N_CHIPS, r0, CHUNK_W), F32)
        self.flight = None

    def mm(self, wn, a, b, name):
        r0, _, direct = self.regions[wn]
        assert direct
        self.buf = matmul_tn_packed(a, b, self.buf, r0, wn in COL_SHARDED, name)

    def put(self, wn, full):
        r0, _, direct = self.regions[wn]
        assert not direct
        piece = _shard_major(wn, full, CHUNK_W)
        self.buf = lax.dynamic_update_slice(self.buf, piece, (0, r0, 0))

    def send(self):
        i = self.layer
        sib, = grads_to_sibling([self.buf], "grads_to_sibling_l%d" % i)
        t = pair_sum(self.buf, sib, BF16, "pair_sum_l%d" % i)
        self.flight = reduce_start(t, sib, "reduce_l%d_start" % i)
        return self.flight[4][0, 0]


def _small_pack(arrs, total_padded):
    flat = jnp.concatenate([a.reshape(-1) for a in arrs])
    return jnp.pad(flat, (0, total_padded - flat.shape[0]))


def kernel(x, positions, l0_norm_g, l0_w_in, l0_ln_g, l0_ln_b, l0_w_s, l0_b_s, l0_w_out, l1_norm_g, l1_w_in, l1_a_re, l1_a_im, l1_log_step, l1_b_re, l1_b_im, l1_c_re, l1_c_im, l1_d_skip, l1_w_glu, l1_b_glu, l1_w_out, l2_norm_g, l2_w_in, l2_q_norm_g, l2_w_uq, l2_kv_norm_g, l2_w_ukv, l2_w_out, l3_norm_g, l3_w_in, l3_ln_g, l3_ln_b, l3_w_s, l3_b_s, l3_w_out, final_norm_g, loss_target, m_l0_norm_g, m_l0_w_in, m_l0_ln_g, m_l0_ln_b, m_l0_w_s, m_l0_b_s, m_l0_w_out, m_l1_norm_g, m_l1_w_in, m_l1_a_re, m_l1_a_im, m_l1_log_step, m_l1_b_re, m_l1_b_im, m_l1_c_re, m_l1_c_im, m_l1_d_skip, m_l1_w_glu, m_l1_b_glu, m_l1_w_out, m_l2_norm_g, m_l2_w_in, m_l2_q_norm_g, m_l2_w_uq, m_l2_kv_norm_g, m_l2_w_ukv, m_l2_w_out, m_l3_norm_g, m_l3_w_in, m_l3_ln_g, m_l3_ln_b, m_l3_w_s, m_l3_b_s, m_l3_w_out, m_final_norm_g, v_l0_norm_g, v_l0_w_in, v_l0_ln_g, v_l0_ln_b, v_l0_w_s, v_l0_b_s, v_l0_w_out, v_l1_norm_g, v_l1_w_in, v_l1_a_re, v_l1_a_im, v_l1_log_step, v_l1_b_re, v_l1_b_im, v_l1_c_re, v_l1_c_im, v_l1_d_skip, v_l1_w_glu, v_l1_b_glu, v_l1_w_out, v_l2_norm_g, v_l2_w_in, v_l2_q_norm_g, v_l2_w_uq, v_l2_kv_norm_g, v_l2_w_ukv, v_l2_w_out, v_l3_norm_g, v_l3_w_in, v_l3_ln_g, v_l3_ln_b, v_l3_w_s, v_l3_b_s, v_l3_w_out, v_final_norm_g):
    args = locals()
    w = {n: args[n] for n in WEIGHT_NAMES}
    mom_m = {n: args["m_" + n] for n in WEIGHT_NAMES}
    mom_v = {n: args["v_" + n] for n in WEIGHT_NAMES}
    h0 = x[0]
    target = loss_target[0]
    pos = positions.reshape(-1, 1)

    full = {}

    def pack_unit(layers):
        names = [n for n in BIG if int(n[1]) in layers]
        rows = [w[n].size // PACK_W for n in names]
        pad = -sum(rows) % PACK_ROW_ALIGN
        return names, rows, _pack_rows([w[n].astype(BF16) for n in names] + [jnp.zeros((pad, PACK_W), BF16)])

    def unpack_unit(names, rows, gathered):
        r0 = 0
        for n, nr in zip(names, rows):
            full[n] = _from_shard_major(n, gathered[:, r0:r0 + nr, :], w[n].shape)
            r0 += nr

    unit0, unit1, unit2 = pack_unit((0,)), pack_unit((1,)), pack_unit((2, 3))
    wp = dict(w)

    def layer_params(i):
        pre = "l%d_" % i
        p = {k[len(pre):]: v for k, v in wp.items() if k.startswith(pre)}
        wf = {k[len(pre):]: v for k, v in full.items() if k.startswith(pre)}
        return p, wf

    flight = gather_start(unit0[2], unit1[2], "gather_l0_start")
    cos, sins = rope_tables(pos, flight[4][0, 0])
    wp["l1_a_re"] = w["l1_a_re"] + flight[4][0, 0]
    s5_weights = _s5_weights(layer_params(1)[0])
    land = gather_wait(*flight[:4], s5_weights[2], "gather_l0_wait")
    got = gather_handover(land, unit0[2], "gather_l0_handover")
    unpack_unit(unit0[0], unit0[1], got)
    flight = gather_start(unit1[2], got, "gather_l1_start")
    wp["l0_norm_g"] = w["l0_norm_g"] + flight[4][0, 0]

    h = h0
    saved = []
    for i, kind in enumerate(LAYER_KINDS):
        if i == 1:
            land = gather_wait(*flight[:4], h, "gather_l1_wait")
            got = gather_handover(land, unit1[2], "gather_l1_handover")
            unpack_unit(unit1[0], unit1[1], got)
            flight = gather_start(unit2[2], got, "gather_l23_start")
            wp["l1_norm_g"] = w["l1_norm_g"] + flight[4][0, 0]
        if i == 2:
            land = gather_wait(*flight[:4], h, "gather_l23_wait")
            unpack_unit(unit2[0], unit2[1], gather_handover(land, unit2[2], "gather_l23_handover"))
        p, wf = layer_params(i)
        tag = "l%d" % i
        if kind == "gmlp":
            h, s = gmlp_layer_fwd(h, p, wf, tag)
        elif kind == "s5":
            h, s = s5_layer_fwd(h, p, wf, s5_weights, tag)
        else:
            h, s = mla_layer_fwd(h, p, wf, cos, sins, tag)
        saved.append(s)
    loss_part, dh, g_final = loss_head(h, final_norm_g, target)

    grads = {"final_norm_g": g_final.reshape(-1)}
    sinks = {}

    for i in reversed(range(len(LAYER_KINDS))):
        kind = LAYER_KINDS[i]
        p, wf = layer_params(i)
        tag = "l%d" % i
        sink = sinks[i] = BigGradSink(i, {n[3:]: w[n].shape for n in BIG if int(n[1]) == i})
        if kind == "gmlp":
            dh, g = gmlp_layer_bwd(dh, saved[i], p, wf, tag, sink)
        elif kind == "s5":
            dh, g = s5_layer_bwd(dh, saved[i], p, wf, tag, sink)
        else:
            dh, g = mla_layer_bwd(dh, saved[i], p, wf, cos, sins, tag, sink)
        for k, val in g.items():
            grads["l%d_%s" % (i, k)] = val
    grad_x = dh[None]

    n_small = sum(w[n].size for n in SMALL)
    piece = N_CHIPS * 2 * 16 * PACK_W
    n_small_pad = -(-(n_small + 1) // piece) * piece
    nrs = n_small_pad // N_CHIPS // PACK_W
    p_small = _small_pack([grads[n] for n in SMALL] + [loss_part], n_small_pad).reshape(N_CHIPS, nrs, PACK_W)
    sib_small, = grads_to_sibling([p_small], "grads_to_sibling_small")
    t_small = pair_sum(p_small, sib_small, F32, "pair_sum_small")
    rb_small, = grads_across_chips([t_small])
    halves = [chip_sum(t_small, rb_small, "chip_sum_small")]

    after = halves[0]
    for i in reversed(range(len(LAYER_KINDS))):
        t_i, rb_i = reduce_wait(*sinks[i].flight[:4], after, "reduce_l%d_wait" % i)
        halves.append(chip_sum(t_i, rb_i, "chip_sum_l%d" % i))
        after = halves[-1]
    reduced = reduced_to_sibling(halves)
    small_flight = bcast_start(reduced[0], reduced[1], "small_allgather_start")

    g_out, d_out, nm_out, nv_out = {}, {}, {}, {}
    for i, g_i in zip(reversed(range(len(LAYER_KINDS))), reduced[1:]):
        for wn, (r0, shape, direct) in sinks[i].regions.items():
            n = "l%d_%s" % (i, wn)
            if direct:
                g_out[n], d_out[n], nm_out[n], nv_out[n] = adamw_packed(w[n], g_i, r0, mom_m[n], mom_v[n], "adamw_" + n)
            else:
                g_out[n] = g_i[r0:r0 + shape[0] * shape[1] // CHUNK_W].reshape(shape)
                d_out[n], nm_out[n], nv_out[n] = adamw(w[n], g_out[n], mom_m[n], mom_v[n], "adamw_" + n)
    small_all = bcast_wait(*small_flight[:4], nv_out["l0_w_in"], "small_allgather_wait")
    g_small = small_all.reshape(-1, PACK_W)
    sp = lambda d: _small_pack([d[n] for n in SMALL], n_small_pad).reshape(-1, PACK_W)
    d_small, nm_small, nv_small = adamw(sp(w), g_small, sp(mom_m), sp(mom_v), "adamw_small")
    for buf, out in ((g_small, g_out), (d_small, d_out), (nm_small, nm_out), (nv_small, nv_out)):
        flat = buf.reshape(-1)
        o = 0
        for n in SMALL:
            out[n] = flat[o:o + w[n].size].reshape(w[n].shape)
            o += w[n].size
    loss = g_small.reshape(-1)[n_small]
    return (loss, grad_x, *[g_out[n] for n in WEIGHT_NAMES], *[d_out[n] for n in WEIGHT_NAMES],
            *[nm_out[n] for n in WEIGHT_NAMES], *[nv_out[n] for n in WEIGHT_NAMES])
```

```python
import functools
import math

import jax
import jax.numpy as jnp
import numpy as np
from jax import lax
from jax.experimental import pallas as pl
from jax.experimental.pallas import tpu as pltpu

F32 = jnp.float32
BF16 = jnp.bfloat16
MESH = pl.DeviceIdType.MESH
VMEM_LIMIT_BYTES = 56 * 1024 * 1024
LANES = 128
PACK_W = 1024
CHUNK_W = 256
PACK_ROW_ALIGN = 256
ROW_TILE = 256
ROW_TILE_NARROW = 512
SUM_BLOCK_BYTES = 1024 * 1024
ADAMW_BLOCK_BYTES = 1024 * 1024
MM_BLOCK_BYTES = 12 * 1024 * 1024

NORM_EPS = 1e-6
N_CHIPS = 4
GMLP_CHUNK = 128
GMLP_GROUPS = 8
S5_GROUPS = 128
S5_GROUP = 16
S5_STATE = 64
S5_SB = 16
S5_SEG = 8
MLA_HEADS = 16
MLA_NOPE = 128
MLA_ROPE = 64
MLA_Q_RANK = 384
MLA_KV_RANK = 128
MLA_SCALE = (MLA_NOPE + MLA_ROPE) ** -0.5
ROPE_THETA = 10000.0
NEG_INF = -1e30
ADAM_LR, ADAM_B1, ADAM_B2, ADAM_EPS, ADAM_WD, ADAM_STEP = 0.001, 0.9, 0.999, 1e-08, 0.01, 10

DN_NN = (((1,), (0,)), ((), ()))
DN_NT = (((1,), (1,)), ((), ()))
DN_TN = (((0,), (0,)), ((), ()))


def _cparams(sem):
    return pltpu.CompilerParams(dimension_semantics=sem, vmem_limit_bytes=VMEM_LIMIT_BYTES)


def _pick(n, cands=(512, 384, 256, 128)):
    for c in cands:
        if n % c == 0:
            return c
    return n


def _pick_rows(r, cap=512, mult=16):
    return max(t for t in range(mult, cap + 1, mult) if r % t == 0)


def _dot(a, b, dn):
    return lax.dot_general(a.astype(BF16), b.astype(BF16), dn, preferred_element_type=F32)


def _sigmoid(x):
    return 0.5 + 0.5 * jnp.tanh(0.5 * x)


def _gelu(x):
    c = math.sqrt(2.0 / math.pi)
    t = jnp.tanh(c * (x + 0.044715 * x * x * x))
    return 0.5 * x * (1.0 + t)


def _gelu_grad(x):
    c = math.sqrt(2.0 / math.pi)
    t = jnp.tanh(c * (x + 0.044715 * x * x * x))
    return 0.5 * (1.0 + t) + 0.5 * x * (1.0 - t * t) * c * (1.0 + 3.0 * 0.044715 * x * x)


def _gelu_both(x):
    c = math.sqrt(2.0 / math.pi)
    t = jnp.tanh(c * (x + 0.044715 * x * x * x))
    return 0.5 * x * (1.0 + t), 0.5 * (1.0 + t) + 0.5 * x * (1.0 - t * t) * c * (1.0 + 3.0 * 0.044715 * x * x)


def _silu_both(z):
    s = _sigmoid(z)
    return z * s, s * (1.0 + z * (1.0 - s))


def _silu(z):
    return z * _sigmoid(z)


def matmul(a, b, mode, name, out_dtype=F32, add=None):
    if mode == "nn":
        (m, k), n = a.shape, b.shape[1]
    elif mode == "nt":
        (m, k), n = a.shape, b.shape[0]
    else:
        (k, m), n = a.shape, b.shape[1]
    tm = _pick(m, [t for t in (2048, 1024, 512, 384, 256, 128) if t * k * a.dtype.itemsize <= MM_BLOCK_BYTES])
    tn = _pick(n, [t for t in (512, 384, 256, 128) if t * k * b.dtype.itemsize <= MM_BLOCK_BYTES])
    dn = {"nn": DN_NN, "nt": DN_NT, "tn": DN_TN}[mode]

    def body(*refs):
        if add is None:
            a_ref, b_ref, o_ref = refs
        else:
            a_ref, b_ref, add_ref, o_ref = refs
        r = _dot(a_ref[...], b_ref[...], dn)
        if add is not None:
            r = r + add_ref[...].astype(F32)
        o_ref[...] = r.astype(out_dtype)

    a_spec = pl.BlockSpec((k, tm), lambda i, j: (0, i)) if mode == "tn" else pl.BlockSpec((tm, k), lambda i, j: (i, 0))
    b_spec = pl.BlockSpec((tn, k), lambda i, j: (j, 0)) if mode == "nt" else pl.BlockSpec((k, tn), lambda i, j: (0, j))
    o_spec = pl.BlockSpec((tm, tn), lambda i, j: (i, j))
    in_specs = [a_spec, b_spec] + ([o_spec] if add is not None else [])
    args = (a, b) + ((add,) if add is not None else ())
    return pl.pallas_call(
        body, name=name, grid=(m // tm, n // tn), in_specs=in_specs, out_specs=o_spec,
        out_shape=jax.ShapeDtypeStruct((m, n), out_dtype),
        compiler_params=_cparams(("parallel", "arbitrary")))(*args)


def _tile_rows(r, r0, cands=(512, 384, 256, 128)):
    return next(t for t in cands if r % t == 0 and r0 % t == 0)


def matmul_tn_packed(a, b, buf, r0, col_sharded, name):
    k, m = a.shape
    n = b.shape[1]
    if col_sharded:
        chunks = n // N_CHIPS // CHUNK_W
        tm = _tile_rows(m, r0, (1024, 512, 384, 256, 128))
        o_map = lambda i, j: (j // chunks, (r0 + (j % chunks) * m) // tm + i, 0)
    else:
        rs = m // N_CHIPS
        tm = _tile_rows(rs, r0)
        per = rs // tm
        o_map = lambda i, j: (i // per, (r0 + j * rs) // tm + i % per, 0)

    def body(a_ref, b_ref, buf_ref, o_ref):
        o_ref[0] = _dot(a_ref[...], b_ref[...], DN_TN)

    return pl.pallas_call(
        body, name=name, grid=(m // tm, n // CHUNK_W),
        in_specs=[pl.BlockSpec((k, tm), lambda i, j: (0, i)), pl.BlockSpec((k, CHUNK_W), lambda i, j: (0, j)),
                  pl.BlockSpec(memory_space=pl.ANY)],
        out_specs=pl.BlockSpec((1, tm, CHUNK_W), o_map), out_shape=jax.ShapeDtypeStruct(buf.shape, buf.dtype),
        input_output_aliases={2: 0}, compiler_params=_cparams(("parallel", "arbitrary")))(a, b, buf)


def _rows(tl, w, col=0):
    return pl.BlockSpec((tl, w), lambda i: (i, col))


def _full(shape):
    nd = len(shape)
    return pl.BlockSpec(tuple(shape), lambda i: (0,) * nd)


def _rowcall(body, name, n_steps, in_specs, out_specs, out_shape, scratch=()):
    return pl.pallas_call(
        body, name=name, grid=(n_steps,), in_specs=in_specs, out_specs=out_specs, out_shape=out_shape,
        scratch_shapes=list(scratch), compiler_params=_cparams(("arbitrary",)))


def _acc(ref, val, i):
    @pl.when(i == 0)
    def _():
        ref[...] = val

    @pl.when(i != 0)
    def _():
        ref[...] += val


def rms_fwd(h, g, name):
    l, d = h.shape
    tl = ROW_TILE_NARROW

    def body(h_ref, g_ref, o_ref):
        x = h_ref[...]
        r = lax.rsqrt(jnp.mean(x * x, axis=-1, keepdims=True) + NORM_EPS)
        o_ref[...] = (x * r * g_ref[...]).astype(BF16)

    return _rowcall(body, name, l // tl, [_rows(tl, d), _full((1, d))], _rows(tl, d),
                    jax.ShapeDtypeStruct((l, d), BF16))(h, g.reshape(1, d))


def rms_bwd(h, g, dhn, dh_in, name):
    l, d = h.shape
    tl = ROW_TILE_NARROW

    def body(h_ref, g_ref, dhn_ref, dhi_ref, dh_ref, dg_ref):
        i = pl.program_id(0)
        x = h_ref[...]
        r = lax.rsqrt(jnp.mean(x * x, axis=-1, keepdims=True) + NORM_EPS)
        xhat = x * r
        dy = dhn_ref[...]
        dxh = dy * g_ref[...]
        dx = r * (dxh - xhat * jnp.mean(dxh * xhat, axis=-1, keepdims=True))
        dh_ref[...] = dhi_ref[...] + dx
        _acc(dg_ref, jnp.sum(dy * xhat, axis=0, keepdims=True), i)

    return _rowcall(body, name, l // tl, [_rows(tl, d), _full((1, d)), _rows(tl, d), _rows(tl, d)],
                    [_rows(tl, d), _full((1, d))],
                    [jax.ShapeDtypeStruct((l, d), F32), jax.ShapeDtypeStruct((1, d), F32)])(h, g.reshape(1, d), dhn, dh_in)


def loss_head(h, g, target):
    l, d = h.shape
    tl = ROW_TILE_NARROW

    def body(h_ref, g_ref, t_ref, loss_ref, dh_ref, dg_ref):
        i = pl.program_id(0)
        x = h_ref[...]
        gg = g_ref[...]
        r = lax.rsqrt(jnp.mean(x * x, axis=-1, keepdims=True) + NORM_EPS)
        xhat = x * r
        err = xhat * gg - t_ref[...]
        part = 0.5 * jnp.sum(jnp.mean(err * err, axis=-1, keepdims=True), axis=0, keepdims=True)
        _acc(loss_ref, part, i)
        dy = err * (1.0 / d)
        dxh = dy * gg
        dh_ref[...] = r * (dxh - xhat * jnp.mean(dxh * xhat, axis=-1, keepdims=True))
        _acc(dg_ref, jnp.sum(dy * xhat, axis=0, keepdims=True), i)

    return _rowcall(body, "loss_head", l // tl, [_rows(tl, d), _full((1, d)), _rows(tl, d)],
                    [_full((1, 1)), _rows(tl, d), _full((1, d))],
                    [jax.ShapeDtypeStruct((1, 1), F32), jax.ShapeDtypeStruct((l, d), F32),
                     jax.ShapeDtypeStruct((1, d), F32)])(h, g.reshape(1, d), target)


def _gmlp_common(a_ref, lng_ref, lnb_ref):
    di = lng_ref.shape[1]
    u_pre = a_ref[:, 0:di]
    v_pre = a_ref[:, di:2 * di]
    z = a_ref[:, 2 * di:3 * di]
    vg = _gelu(v_pre)
    mu = jnp.mean(vg, axis=-1, keepdims=True)
    xc = vg - mu
    rstd = lax.rsqrt(jnp.mean(xc * xc, axis=-1, keepdims=True) + NORM_EPS)
    vhat = xc * rstd
    vn = vhat * lng_ref[...] + lnb_ref[...]
    return u_pre, v_pre, z, vhat, rstd, vn


def _tril(w):
    r = lax.broadcasted_iota(jnp.int32, w.shape, 0)
    c = lax.broadcasted_iota(jnp.int32, w.shape, 1)
    return jnp.where(c <= r, w, 0.0)


def gmlp_gate_fwd(a, ln_g, ln_b, w_s, b_s, name):
    l, w3 = a.shape
    di = w3 // 3
    dg = di // GMLP_GROUPS
    tl = GMLP_CHUNK

    def body(a_ref, lng_ref, lnb_ref, ws_ref, bs_ref, m_ref):
        u_pre, _, z, _, _, vn = _gmlp_common(a_ref, lng_ref, lnb_ref)
        gate = _gelu(u_pre) * _silu(z)
        for g in range(GMLP_GROUPS):
            sl = slice(g * dg, (g + 1) * dg)
            s = _dot(_tril(ws_ref[g]), vn[:, sl], DN_NN) + bs_ref[g]
            m_ref[:, sl] = (gate[:, sl] * s).astype(BF16)

    return _rowcall(body, name, l // tl,
                    [_rows(tl, w3), _full((1, di)), _full((1, di)), _full(w_s.shape), _full((GMLP_GROUPS, tl, 1))],
                    _rows(tl, di), jax.ShapeDtypeStruct((l, di), BF16))(
        a, ln_g.reshape(1, di), ln_b.reshape(1, di), w_s, b_s.reshape(GMLP_GROUPS, tl, 1))


def gmlp_gate_bwd(a, dm, ln_g, ln_b, w_s, b_s, name):
    l, w3 = a.shape
    di = w3 // 3
    dg = di // GMLP_GROUPS
    tl = GMLP_CHUNK

    def body(a_ref, dm_ref, lng_ref, lnb_ref, ws_ref, bs_ref, da_ref, dlg_ref, dlb_ref, dws_ref, dbs_ref,
             dvn_ref, vh_ref, gv_ref):
        i = pl.program_id(0)
        vg, gv = _gelu_both(a_ref[:, di:2 * di])
        gv_ref[...] = gv
        xc = vg - jnp.mean(vg, axis=-1, keepdims=True)
        rstd = lax.rsqrt(jnp.mean(xc * xc, axis=-1, keepdims=True) + NORM_EPS)
        vh_ref[...] = xc * rstd
        for g in range(GMLP_GROUPS):
            sl = slice(g * dg, (g + 1) * dg)
            wt = _tril(ws_ref[g])
            vn_g = vh_ref[:, sl] * lng_ref[:, sl] + lnb_ref[:, sl]
            s = _dot(wt, vn_g, DN_NN) + bs_ref[g]
            dmg = dm_ref[:, sl]
            u, gu = _gelu_both(a_ref[:, sl])
            sz, gz = _silu_both(a_ref[:, 2 * di + g * dg:2 * di + (g + 1) * dg])
            ds = dmg * u * sz
            da_ref[:, sl] = (dmg * s * sz * gu).astype(BF16)
            da_ref[:, 2 * di + g * dg:2 * di + (g + 1) * dg] = (dmg * u * s * gz).astype(BF16)
            dvn_ref[:, sl] = _dot(wt, ds, DN_TN)
            dw = _tril(_dot(ds, vn_g, DN_NT))
            db = jnp.sum(ds, axis=1, keepdims=True)

            @pl.when(i == 0)
            def _():
                dws_ref[g] = dw
                dbs_ref[g] = db

            @pl.when(i != 0)
            def _():
                dws_ref[g] += dw
                dbs_ref[g] += db

        dvn = dvn_ref[...]
        vhat = vh_ref[...]
        dxh = dvn * lng_ref[...]
        dvg = rstd * (dxh - jnp.mean(dxh, axis=-1, keepdims=True) - vhat * jnp.mean(dxh * vhat, axis=-1, keepdims=True))
        da_ref[:, di:2 * di] = (dvg * gv_ref[...]).astype(BF16)
        _acc(dlg_ref, jnp.sum(dvn * vhat, axis=0, keepdims=True), i)
        _acc(dlb_ref, jnp.sum(dvn, axis=0, keepdims=True), i)

    outs = _rowcall(
        body, name, l // tl,
        [_rows(tl, w3), _rows(tl, di), _full((1, di)), _full((1, di)), _full(w_s.shape), _full((GMLP_GROUPS, tl, 1))],
        [_rows(tl, w3), _full((1, di)), _full((1, di)), _full(w_s.shape), _full((GMLP_GROUPS, tl, 1))],
        [jax.ShapeDtypeStruct((l, w3), BF16), jax.ShapeDtypeStruct((1, di), F32), jax.ShapeDtypeStruct((1, di), F32),
         jax.ShapeDtypeStruct(w_s.shape, F32), jax.ShapeDtypeStruct((GMLP_GROUPS, tl, 1), F32)],
        scratch=[pltpu.VMEM((tl, di), F32)] * 3)(
        a, dm, ln_g.reshape(1, di), ln_b.reshape(1, di), w_s, b_s.reshape(GMLP_GROUPS, tl, 1))
    return outs


def gmlp_layer_fwd(h, p, wf, tag):
    hn = rms_fwd(h, p["norm_g"], tag + "_rms")
    a = matmul(hn, wf["w_in"], "nn", tag + "_mm_in")
    m = gmlp_gate_fwd(a, p["ln_g"], p["ln_b"], p["w_s"], p["b_s"], tag + "_gate")
    h_out = matmul(m, wf["w_out"], "nn", tag + "_mm_out", add=h)
    return h_out, (h, hn, a, m)


def gmlp_layer_bwd(dh_out, saved, p, wf, tag, sink):
    h, hn, a, m = saved
    dm = matmul(dh_out, wf["w_out"], "nt", tag + "_mm_dm")
    sink.mm("w_out", m, dh_out, tag + "_mm_gwout")
    da, dlg, dlb, dws, dbs = gmlp_gate_bwd(a, dm, p["ln_g"], p["ln_b"], p["w_s"], p["b_s"], tag + "_gate_bwd")
    dhn = matmul(da, wf["w_in"], "nt", tag + "_mm_dhn")
    sink.mm("w_in", hn, da, tag + "_mm_gwin")
    zero = sink.send()
    dh, dng = rms_bwd(h, p["norm_g"] + zero, dhn, dh_out, tag + "_rms_bwd")
    grads = {"norm_g": dng.reshape(-1), "ln_g": dlg.reshape(-1), "ln_b": dlb.reshape(-1),
             "w_s": dws, "b_s": dbs.reshape(GMLP_GROUPS, GMLP_CHUNK)}
    return dh, grads


def _cmul(ar, ai, br, bi):
    return ar * br - ai * bi, ar * bi + ai * br


S5_PG = 16


def _gblock(tail):
    return pl.BlockSpec((S5_PG,) + tuple(tail), lambda i: (i, 0, 0))


def s5_params_fwd(a_re, a_im, log_step, b_re, b_im):
    g, p, hh = b_re.shape

    def body(ar_ref, ai_ref, ls_ref, br_ref, bi_ref, lr_ref, li_ref, bbr_ref, bbi_ref):
        ar, ai = ar_ref[...], ai_ref[...]
        step = jnp.exp(ls_ref[...])
        mag = jnp.exp(ar * step)
        lr, li = mag * jnp.cos(ai * step), mag * jnp.sin(ai * step)
        den = 1.0 / (ar * ar + ai * ai)
        fr, fi = _cmul(lr - 1.0, li, ar * den, -ai * den)
        lr_ref[...] = lr
        li_ref[...] = li
        bbr, bbi = _cmul(fr, fi, br_ref[...], bi_ref[...])
        bbr_ref[...] = bbr
        bbi_ref[...] = bbi

    s1 = jax.ShapeDtypeStruct((g, p, 1), F32)
    s3 = jax.ShapeDtypeStruct((g, p, hh), F32)
    b1, b0, b3 = _gblock((p, 1)), _gblock((1, 1)), _gblock((p, hh))
    return pl.pallas_call(body, name="s5_params_fwd", grid=(g // S5_PG,), in_specs=[b1, b1, b0, b3, b3],
                          out_specs=[b1, b1, b3, b3], out_shape=[s1, s1, s3, s3],
                          compiler_params=_cparams(("parallel",)))(
        a_re.reshape(g, p, 1), a_im.reshape(g, p, 1), log_step.reshape(g, 1, 1), b_re, b_im)


def s5_params_bwd(a_re, a_im, log_step, b_re, b_im, dl_re, dl_im, dbb_re, dbb_im):
    g, p, hh = b_re.shape

    def body(ar_ref, ai_ref, ls_ref, br_ref, bi_ref, dlr_ref, dli_ref, dbr_ref, dbi_ref,
             gar_ref, gai_ref, gls_ref, gbr_ref, gbi_ref):
        ar, ai = ar_ref[...], ai_ref[...]
        step = jnp.exp(ls_ref[...])
        mag = jnp.exp(ar * step)
        lr, li = mag * jnp.cos(ai * step), mag * jnp.sin(ai * step)
        den = 1.0 / (ar * ar + ai * ai)
        ir, ii = ar * den, -ai * den
        fr, fi = _cmul(lr - 1.0, li, ir, ii)
        br, bi = br_ref[...], bi_ref[...]
        dbr, dbi = dbr_ref[...], dbi_ref[...]
        gbr, gbi = _cmul(fr, -fi, dbr, dbi)
        gbr_ref[...] = gbr
        gbi_ref[...] = gbi
        pr, pi = _cmul(br, -bi, dbr, dbi)
        gfr = jnp.sum(pr, axis=-1, keepdims=True)
        gfi = jnp.sum(pi, axis=-1, keepdims=True)
        t_r, t_i = _cmul(ir, -ii, gfr, gfi)
        glr, gli = dlr_ref[...] + t_r, dli_ref[...] + t_i
        c1r, c1i = _cmul(step * lr, -step * li, glr, gli)
        qr, qi = _cmul(fr, fi, ir, ii)
        c2r, c2i = _cmul(-qr, qi, gfr, gfi)
        gar_ref[...] = c1r + c2r
        gai_ref[...] = c1i + c2i
        wr, wi = _cmul(ar, ai, lr, li)
        sr, _ = _cmul(wr, -wi, glr, gli)
        gls_ref[...] = jnp.sum(sr, axis=1, keepdims=True) * step

    s1 = jax.ShapeDtypeStruct((g, p, 1), F32)
    s3 = jax.ShapeDtypeStruct((g, p, hh), F32)
    b1, b0, b3 = _gblock((p, 1)), _gblock((1, 1)), _gblock((p, hh))
    return pl.pallas_call(body, name="s5_params_bwd", grid=(g // S5_PG,),
                          in_specs=[b1, b1, b0, b3, b3, b1, b1, b3, b3], out_specs=[b1, b1, b0, b3, b3],
                          out_shape=[s1, s1, jax.ShapeDtypeStruct((g, 1, 1), F32), s3, s3],
                          compiler_params=_cparams(("parallel",)))(
        a_re.reshape(g, p, 1), a_im.reshape(g, p, 1), log_step.reshape(g, 1, 1), b_re, b_im,
        dl_re, dl_im, dbb_re, dbb_im)


def _blockdiag(t):
    sb, n, r, c = t.shape
    eye = jnp.eye(n, dtype=bool)[None, :, None, :, None]
    full = jnp.where(eye, t[:, :, :, None, :], jnp.zeros((), t.dtype))
    return full.reshape(sb, n * r, n * c)


def _blockdiag_extract(m, r, c):
    sb = m.shape[0]
    n = m.shape[1] // r
    m5 = m.reshape(sb, n, r, n, c)
    return jnp.stack([m5[:, i, :, i, :] for i in range(n)], axis=1)


S5_TB = 256
S5_UNROLL = 8


def _lam_power(pr, pi, n):
    for _ in range(int(math.log2(n))):
        pr, pi = _cmul(pr, pi, pr, pi)
    return pr, pi


def _segment_entries(er, ei, pr, pi, reverse):
    seg, ns = er.shape
    row = lax.broadcasted_iota(jnp.int32, (seg, ns), 0)
    cr = jnp.zeros((seg, ns), F32)
    ci = jnp.zeros((seg, ns), F32)
    cur_r = jnp.zeros((1, ns), F32)
    cur_i = jnp.zeros((1, ns), F32)
    for s in (range(seg - 2, -1, -1) if reverse else range(1, seg)):
        src = s + 1 if reverse else s - 1
        mr, mi = _cmul(pr, pi, cur_r, cur_i)
        cur_r = jnp.sum(jnp.where(row == src, er, 0.0), axis=0, keepdims=True) + mr
        cur_i = jnp.sum(jnp.where(row == src, ei, 0.0), axis=0, keepdims=True) + mi
        cr = jnp.where(row == s, cur_r, cr)
        ci = jnp.where(row == s, cur_i, ci)
    return cr, ci


def s5_scan_fused_fwd(a_p, lam_re, lam_im, wb_re, wb_im, wc_re, wc_im, d_skip, name):
    l = a_p.shape[0]
    di = d_skip.shape[1]
    rows = S5_SEG * S5_TB
    nb = l // rows
    ns = wb_re.shape[2]

    def body(u_ref, lr_ref, li_ref, wbr_ref, wbi_ref, wcr_ref, wci_ref, ds_ref, y_ref, ckr_ref, cki_ref, bur, bui):
        lr = jnp.broadcast_to(lr_ref[0], (S5_SEG, ns))
        li = jnp.broadcast_to(li_ref[0], (S5_SEG, ns))

        def scan_block(b, carry, keep):
            def step(t, c):
                xr, xi = c
                sl = pl.ds(pl.multiple_of(b * rows + t * S5_SEG, S5_SEG), S5_SEG)
                nr = lr * xr - li * xi + bur[sl, :]
                ni = lr * xi + li * xr + bui[sl, :]
                if keep:
                    bur[sl, :] = nr
                    bui[sl, :] = ni
                return nr, ni

            return lax.fori_loop(0, S5_TB, step, carry, unroll=S5_UNROLL)

        def project(b, carry):
            rs = pl.ds(pl.multiple_of(b * rows, rows), rows)
            u = u_ref[rs, :]
            bur[rs, :] = _dot(u, wbr_ref[0], DN_NN)
            bui[rs, :] = _dot(u, wbi_ref[0], DN_NN)
            return scan_block(b, carry, False)

        zero = jnp.zeros((S5_SEG, ns), F32)
        er, ei = lax.fori_loop(0, nb, project, (zero, zero))
        pr, pi = _lam_power(lr_ref[0], li_ref[0], l // S5_SEG)
        entry = _segment_entries(er, ei, pr, pi, False)

        def emit(b, carry):
            ckr_ref[0, b] = carry[0]
            cki_ref[0, b] = carry[1]
            carry = scan_block(b, carry, True)
            rs = pl.ds(pl.multiple_of(b * rows, rows), rows)
            y_ref[rs, :] = (_dot(bur[rs, :], wcr_ref[0], DN_NN) - _dot(bui[rs, :], wci_ref[0], DN_NN)
                            + ds_ref[...] * u_ref[rs, :])
            return carry

        lax.fori_loop(0, nb, emit, entry)

    sb3 = lambda s: (s, 0, 0)
    st = jax.ShapeDtypeStruct
    return pl.pallas_call(
        body, name=name, grid=(S5_SB,),
        in_specs=[pl.BlockSpec((l, LANES), lambda s: (0, s)),
                  pl.BlockSpec((1, 1, ns), sb3), pl.BlockSpec((1, 1, ns), sb3),
                  pl.BlockSpec((1, LANES, ns), sb3), pl.BlockSpec((1, LANES, ns), sb3),
                  pl.BlockSpec((1, ns, LANES), sb3), pl.BlockSpec((1, ns, LANES), sb3),
                  pl.BlockSpec((1, LANES), lambda s: (0, s))],
        out_specs=[pl.BlockSpec((l, LANES), lambda s: (0, s)),
                   pl.BlockSpec((1, nb, S5_SEG, ns), lambda s: (s, 0, 0, 0)),
                   pl.BlockSpec((1, nb, S5_SEG, ns), lambda s: (s, 0, 0, 0))],
        out_shape=[st((l, di), F32), st((S5_SB, nb, S5_SEG, ns), F32), st((S5_SB, nb, S5_SEG, ns), F32)],
        scratch_shapes=[pltpu.VMEM((l, ns), F32), pltpu.VMEM((l, ns), F32)],
        compiler_params=_cparams(("parallel",)))(a_p, lam_re, lam_im, wb_re, wb_im, wc_re, wc_im, d_skip)


def s5_scan_fused_bwd(a_p, dy, lam_re, lam_im, wb_re, wb_im, wc_re, wc_im, d_skip, ck_re, ck_im, name):
    l = a_p.shape[0]
    di = d_skip.shape[1]
    rows = S5_SEG * S5_TB
    nb = l // rows
    ns = wb_re.shape[2]

    def body(u_ref, dy_ref, lr_ref, li_ref, wbr_ref, wbi_ref, wcr_ref, wci_ref, ds_ref, ckr_ref, cki_ref,
             du_ref, dwbr_ref, dwbi_ref, dwcr_ref, dwci_ref, dds_ref, dlr_ref, dli_ref, gr, gi, xr_b, xi_b):
        lr = jnp.broadcast_to(lr_ref[0], (S5_SEG, ns))
        li = jnp.broadcast_to(li_ref[0], (S5_SEG, ns))

        def back_project(k, carry):
            b = nb - 1 - k
            rs = pl.ds(pl.multiple_of(b * rows, rows), rows)
            dyv = dy_ref[rs, :]
            gr[rs, :] = _dot(dyv, wcr_ref[0], DN_NT)
            gi[rs, :] = -_dot(dyv, wci_ref[0], DN_NT)

            def step(kk, c):
                ar, ai = c
                sl = pl.ds(pl.multiple_of(b * rows + (S5_TB - 1 - kk) * S5_SEG, S5_SEG), S5_SEG)
                return gr[sl, :] + lr * ar + li * ai, gi[sl, :] + lr * ai - li * ar

            return lax.fori_loop(0, S5_TB, step, carry, unroll=S5_UNROLL)

        zero = jnp.zeros((S5_SEG, ns), F32)
        er, ei = lax.fori_loop(0, nb, back_project, (zero, zero))
        pr, pi = _lam_power(lr_ref[0], -li_ref[0], l // S5_SEG)
        a0r, a0i = _segment_entries(er, ei, pr, pi, True)

        dwbr_ref[...] = jnp.zeros_like(dwbr_ref)
        dwbi_ref[...] = jnp.zeros_like(dwbi_ref)
        dwcr_ref[...] = jnp.zeros_like(dwcr_ref)
        dwci_ref[...] = jnp.zeros_like(dwci_ref)
        dds_ref[...] = jnp.zeros_like(dds_ref)

        def block(k, carry):
            b = nb - 1 - k
            rs = pl.ds(pl.multiple_of(b * rows, rows), rows)
            u = u_ref[rs, :]
            dyv = dy_ref[rs, :]
            body_rows = pl.ds(S5_SEG, rows)
            x0r, x0i = ckr_ref[0, b], cki_ref[0, b]
            xr_b[0:S5_SEG, :] = x0r
            xi_b[0:S5_SEG, :] = x0i
            xr_b[body_rows, :] = _dot(u, wbr_ref[0], DN_NN)
            xi_b[body_rows, :] = _dot(u, wbi_ref[0], DN_NN)

            def fstep(t, c):
                xr, xi = c
                sl = pl.ds(pl.multiple_of((t + 1) * S5_SEG, S5_SEG), S5_SEG)
                nr = lr * xr - li * xi + xr_b[sl, :]
                ni = lr * xi + li * xr + xi_b[sl, :]
                xr_b[sl, :] = nr
                xi_b[sl, :] = ni
                return nr, ni

            lax.fori_loop(0, S5_TB, fstep, (x0r, x0i), unroll=S5_UNROLL)
            dwcr_ref[0] += _dot(xr_b[body_rows, :], dyv, DN_TN)
            dwci_ref[0] -= _dot(xi_b[body_rows, :], dyv, DN_TN)

            def bstep(kk, c):
                ar, ai = c
                sl = pl.ds(pl.multiple_of(b * rows + (S5_TB - 1 - kk) * S5_SEG, S5_SEG), S5_SEG)
                nr = gr[sl, :] + lr * ar + li * ai
                ni = gi[sl, :] + lr * ai - li * ar
                gr[sl, :] = nr
                gi[sl, :] = ni
                return nr, ni

            ar, ai = lax.fori_loop(0, S5_TB, bstep, carry[:2], unroll=S5_UNROLL)
            a_r, a_i = gr[rs, :], gi[rs, :]
            p_r, p_i = xr_b[0:rows, :], xi_b[0:rows, :]
            per_seg = lambda v: jnp.sum(v.reshape(S5_TB, S5_SEG, ns), axis=0)
            carry = (ar, ai, carry[2] + per_seg(a_r * p_r + a_i * p_i), carry[3] + per_seg(a_i * p_r - a_r * p_i))
            du_ref[rs, :] = (_dot(a_r, wbr_ref[0], DN_NT) + _dot(a_i, wbi_ref[0], DN_NT) + ds_ref[...] * dyv).astype(BF16)
            dwbr_ref[0] += _dot(u, a_r, DN_TN)
            dwbi_ref[0] += _dot(u, a_i, DN_TN)
            dds_ref[...] += jnp.sum(dyv * u, axis=0, keepdims=True)
            return carry

        _, _, dlr, dli = lax.fori_loop(0, nb, block, (a0r, a0i, zero, zero))
        dlr_ref[0] = dlr
        dli_ref[0] = dli

    sb3 = lambda s: (s, 0, 0)
    seq = pl.BlockSpec((l, LANES), lambda s: (0, s))
    ck = pl.BlockSpec((1, nb, S5_SEG, ns), lambda s: (s, 0, 0, 0))
    st = jax.ShapeDtypeStruct
    return pl.pallas_call(
        body, name=name, grid=(S5_SB,),
        in_specs=[seq, seq, pl.BlockSpec((1, 1, ns), sb3), pl.BlockSpec((1, 1, ns), sb3),
                  pl.BlockSpec((1, LANES, ns), sb3), pl.BlockSpec((1, LANES, ns), sb3),
                  pl.BlockSpec((1, ns, LANES), sb3), pl.BlockSpec((1, ns, LANES), sb3),
                  pl.BlockSpec((1, LANES), lambda s: (0, s)), ck, ck],
        out_specs=[seq, pl.BlockSpec((1, LANES, ns), sb3), pl.BlockSpec((1, LANES, ns), sb3),
                   pl.BlockSpec((1, ns, LANES), sb3), pl.BlockSpec((1, ns, LANES), sb3),
                   pl.BlockSpec((1, LANES), lambda s: (0, s)),
                   pl.BlockSpec((1, S5_SEG, ns), sb3), pl.BlockSpec((1, S5_SEG, ns), sb3)],
        out_shape=[st((l, di), BF16), st((S5_SB, LANES, ns), F32), st((S5_SB, LANES, ns), F32),
                   st((S5_SB, ns, LANES), F32), st((S5_SB, ns, LANES), F32), st((1, di), F32),
                   st((S5_SB, S5_SEG, ns), F32), st((S5_SB, S5_SEG, ns), F32)],
        scratch_shapes=[pltpu.VMEM((l, ns), F32), pltpu.VMEM((l, ns), F32),
                        pltpu.VMEM((rows + S5_SEG, ns), F32), pltpu.VMEM((rows + S5_SEG, ns), F32)],
        compiler_params=_cparams(("parallel",)))(
        a_p, dy, lam_re, lam_im, wb_re, wb_im, wc_re, wc_im, d_skip, ck_re, ck_im)


def s5_act(y, name):
    l, d = y.shape
    tl = ROW_TILE

    def body(y_ref, o_ref):
        o_ref[...] = _gelu(y_ref[...]).astype(BF16)

    return _rowcall(body, name, l // tl, [_rows(tl, d)], _rows(tl, d), jax.ShapeDtypeStruct((l, d), BF16))(y)


def s5_gate_fwd(y, t, b_glu, a_p, name):
    l, d = y.shape
    tl = ROW_TILE

    def body(y_ref, t_ref, b_ref, z_ref, m_ref):
        yg = _gelu(y_ref[...])
        m_ref[...] = (yg * _sigmoid(t_ref[...] + b_ref[...]) * _silu(z_ref[...])).astype(BF16)

    return _rowcall(body, name, l // tl, [_rows(tl, d), _rows(tl, d), _full((1, d)), _rows(tl, d, 1)], _rows(tl, d),
                    jax.ShapeDtypeStruct((l, d), BF16))(y, t, b_glu.reshape(1, d), a_p)


def s5_gate_bwd(dm, y, t, b_glu, a_p, name):
    l, d = y.shape
    tl = ROW_TILE

    def body(dm_ref, y_ref, t_ref, b_ref, z_ref, dt_ref, dyg_ref, dz_ref, db_ref):
        i = pl.program_id(0)
        dmv = dm_ref[...]
        z = z_ref[...]
        yg = _gelu(y_ref[...])
        sg = _sigmoid(t_ref[...] + b_ref[...])
        y2 = yg * sg
        sz, gz = _silu_both(z)
        dy2 = dmv * sz
        dz_ref[...] = (dmv * y2 * gz).astype(BF16)
        dyg_ref[...] = dy2 * sg
        dt = dy2 * yg * sg * (1.0 - sg)
        dt_ref[...] = dt.astype(BF16)
        _acc(db_ref, jnp.sum(dt, axis=0, keepdims=True), i)

    st = jax.ShapeDtypeStruct
    return _rowcall(body, name, l // tl, [_rows(tl, d), _rows(tl, d), _rows(tl, d), _full((1, d)), _rows(tl, d, 1)],
                    [_rows(tl, d), _rows(tl, d), _rows(tl, d), _full((1, d))],
                    [st((l, d), BF16), st((l, d), F32), st((l, d), BF16), st((1, d), F32)])(
        dm, y, t, b_glu.reshape(1, d), a_p)


def s5_act_bwd(y, dyg_a, dyg_b, name):
    l, d = y.shape
    tl = ROW_TILE

    def body(y_ref, a_ref, b_ref, o_ref):
        o_ref[...] = (a_ref[...] + b_ref[...]) * _gelu_grad(y_ref[...])

    return _rowcall(body, name, l // tl, [_rows(tl, d)] * 3, _rows(tl, d), jax.ShapeDtypeStruct((l, d), F32))(y, dyg_a, dyg_b)


def _seg_perm(t):
    l, d = t.shape
    return t.reshape(S5_SEG, l // S5_SEG, d).transpose(1, 0, 2).reshape(l, d)


def _seg_unperm(t):
    l, d = t.shape
    return t.reshape(l // S5_SEG, S5_SEG, d).transpose(1, 0, 2).reshape(l, d)


def _s5_weights(p):
    lr, li, bbr, bbi = s5_params_fwd(p["a_re"], p["a_im"], p["log_step"], p["b_re"], p["b_im"])
    ns = 8 * S5_STATE
    lam_re = lr.reshape(S5_SB, 1, ns)
    lam_im = li.reshape(S5_SB, 1, ns)
    to_bd = lambda t: _blockdiag(t.reshape(S5_SB, 8, t.shape[1], t.shape[2]))
    wb_re = to_bd(bbr.transpose(0, 2, 1)).astype(BF16)
    wb_im = to_bd(bbi.transpose(0, 2, 1)).astype(BF16)
    wc_re = to_bd(p["c_re"].transpose(0, 2, 1)).astype(BF16)
    wc_im = to_bd(p["c_im"].transpose(0, 2, 1)).astype(BF16)
    return lam_re, lam_im, wb_re, wb_im, wc_re, wc_im


def s5_layer_fwd(h, p, wf, sw, tag):
    l = h.shape[0]
    di = p["d_skip"].shape[0]
    hn = rms_fwd(h, p["norm_g"], tag + "_rms")
    hn_p = _seg_perm(hn)
    a_p = matmul(hn_p, wf["w_in"], "nn", tag + "_mm_in")
    dsk = p["d_skip"].reshape(1, di)
    y, ck_re, ck_im = s5_scan_fused_fwd(a_p, *sw, dsk, tag + "_scan")
    yg = s5_act(y, tag + "_act")
    t = matmul(yg, wf["w_glu"], "nn", tag + "_mm_glu")
    m = s5_gate_fwd(y, t, p["b_glu"], a_p, tag + "_gate")
    out_p = matmul(m, wf["w_out"], "nn", tag + "_mm_out")
    h_out = residual_add(h, _seg_unperm(out_p), tag + "_res")
    return h_out, (h, hn_p, a_p, sw, ck_re, ck_im, y, yg, t, m)


def residual_add(h, y, name):
    l, d = h.shape
    tl = ROW_TILE_NARROW

    def body(h_ref, y_ref, o_ref):
        o_ref[...] = h_ref[...] + y_ref[...]

    return _rowcall(body, name, l // tl, [_rows(tl, d)] * 2, _rows(tl, d), jax.ShapeDtypeStruct((l, d), F32))(h, y)


def s5_layer_bwd(dh_out, saved, p, wf, tag, sink):
    h, hn_p, a_p, sw, ck_re, ck_im, y, yg, t, m = saved
    l = h.shape[0]
    di = p["d_skip"].shape[0]
    dsk = p["d_skip"].reshape(1, di)
    dout_p = _seg_perm(dh_out)
    dm = matmul(dout_p, wf["w_out"], "nt", tag + "_mm_dm")
    sink.mm("w_out", m, dout_p, tag + "_mm_gwout")
    dt, dyg_a, dz, db_glu = s5_gate_bwd(dm, y, t, p["b_glu"], a_p, tag + "_gate_bwd")
    dyg_b = matmul(dt, wf["w_glu"], "nt", tag + "_mm_dyg")
    sink.mm("w_glu", yg, dt, tag + "_mm_gwglu")
    dy = s5_act_bwd(y, dyg_a, dyg_b, tag + "_act_bwd")
    du, dwbr, dwbi, dwcr, dwci, dds, dlr, dli = s5_scan_fused_bwd(a_p, dy, *sw, dsk, ck_re, ck_im, tag + "_scanb")
    da = jnp.concatenate([du, dz], axis=1)
    dhn_p = matmul(da, wf["w_in"], "nt", tag + "_mm_dhn")
    sink.mm("w_in", hn_p, da, tag + "_mm_gwin")
    zero = sink.send()
    dh, dng = rms_bwd(h, p["norm_g"] + zero, _seg_unperm(dhn_p), dh_out, tag + "_rms_bwd")
    ex = lambda m_, r, c: _blockdiag_extract(m_, r, c).reshape(S5_GROUPS, r, c).transpose(0, 2, 1)
    dbb_re, dbb_im = ex(dwbr, S5_GROUP, S5_STATE), ex(dwbi, S5_GROUP, S5_STATE)
    g_c_re, g_c_im = ex(dwcr, S5_STATE, S5_GROUP), ex(dwci, S5_STATE, S5_GROUP)
    dl_re = lane_sum8(dlr).reshape(S5_GROUPS, S5_STATE, 1)
    dl_im = lane_sum8(dli).reshape(S5_GROUPS, S5_STATE, 1)
    gar, gai, gls, gbr, gbi = s5_params_bwd(p["a_re"], p["a_im"], p["log_step"], p["b_re"], p["b_im"],
                                            dl_re, dl_im, dbb_re, dbb_im)
    grads = {"norm_g": dng.reshape(-1), "a_re": gar.reshape(S5_GROUPS, S5_STATE),
             "a_im": gai.reshape(S5_GROUPS, S5_STATE), "log_step": gls.reshape(-1), "b_re": gbr, "b_im": gbi,
             "c_re": g_c_re, "c_im": g_c_im, "d_skip": dds.reshape(-1), "b_glu": db_glu.reshape(-1)}
    return dh, grads


def lane_sum8(t):
    sb, seg, ns = t.shape

    def body(t_ref, o_ref):
        o_ref[...] = jnp.sum(t_ref[...], axis=1, keepdims=True)

    return pl.pallas_call(body, name="s5_seg_sum", out_shape=jax.ShapeDtypeStruct((sb, 1, ns), F32))(t)


MLA_DI = MLA_HEADS * 128
MLA_CQ0 = MLA_DI
MLA_CKV0 = MLA_CQ0 + MLA_Q_RANK
MLA_KR0 = MLA_CKV0 + MLA_KV_RANK
MLA_AW = MLA_KR0 + LANES


def _rot_half(x):
    w = x.shape[-1]
    lane = lax.broadcasted_iota(jnp.int32, x.shape, x.ndim - 1)
    return jnp.where(lane % MLA_ROPE < MLA_ROPE // 2, pltpu.roll(x, w - MLA_ROPE // 2, x.ndim - 1),
                     pltpu.roll(x, MLA_ROPE // 2, x.ndim - 1))


def rope_tables(pos, zero):
    l = pos.shape[0]
    tl = ROW_TILE
    j = np.arange(LANES) % MLA_ROPE % (MLA_ROPE // 2)
    inv_freq = (ROPE_THETA ** (-(2.0 * j) / MLA_ROPE)).astype(np.float32).reshape(1, LANES)
    sign = np.where(np.arange(LANES) % MLA_ROPE < MLA_ROPE // 2, -1.0, 1.0).astype(np.float32).reshape(1, LANES)

    def body(p_ref, f_ref, s_ref, cos_ref, sin_ref):
        ang = p_ref[...].astype(F32) * f_ref[...]
        cos_ref[...] = jnp.cos(ang)
        sin_ref[...] = jnp.sin(ang) * s_ref[...]

    st = jax.ShapeDtypeStruct((l, LANES), F32)
    return _rowcall(body, "rope_tables", l // tl, [_rows(tl, 1), _full((1, LANES)), _full((1, LANES))],
                    [_rows(tl, LANES)] * 2, [st, st])(pos, jnp.asarray(inv_freq), jnp.asarray(sign) + zero)


def _rope(x, cos, sins):
    return x * cos + _rot_half(x) * sins


def _rope_t(dy, cos, sins):
    return dy * cos - sins * _rot_half(dy)


def _rmsn(x):
    r = lax.rsqrt(jnp.mean(x * x, axis=-1, keepdims=True) + NORM_EPS)
    return x * r, r


def mla_pre(a, q_g, kv_g, cos, sins, name):
    l = a.shape[0]
    tl = ROW_TILE

    def body(a_ref, qg_ref, kg_ref, cos_ref, sin_ref, cq_ref, ckv_ref, krs_ref):
        xq, _ = _rmsn(a_ref[:, MLA_CQ0:MLA_CKV0])
        cq_ref[...] = (xq * qg_ref[...]).astype(BF16)
        xk, _ = _rmsn(a_ref[:, MLA_CKV0:MLA_KR0])
        ckv_ref[...] = (xk * kg_ref[...]).astype(BF16)
        kr = a_ref[:, MLA_KR0:MLA_AW]
        kr2 = kr + pltpu.roll(kr, MLA_ROPE, 1)
        kr2 = _rope(kr2, cos_ref[...], sin_ref[...])
        lane = lax.broadcasted_iota(jnp.int32, kr2.shape, 1)
        krs_ref[0] = jnp.where(lane < MLA_ROPE, kr2, 0.0).astype(BF16)
        krs_ref[1] = jnp.where(lane >= MLA_ROPE, kr2, 0.0).astype(BF16)

    st = jax.ShapeDtypeStruct
    return _rowcall(body, name, l // tl,
                    [_rows(tl, MLA_AW), _full((1, MLA_Q_RANK)), _full((1, MLA_KV_RANK)), _rows(tl, LANES), _rows(tl, LANES)],
                    [_rows(tl, MLA_Q_RANK), _rows(tl, MLA_KV_RANK), pl.BlockSpec((2, tl, LANES), lambda i: (0, i, 0))],
                    [st((l, MLA_Q_RANK), BF16), st((l, MLA_KV_RANK), BF16), st((2, l, LANES), BF16)])(
        a, q_g.reshape(1, -1), kv_g.reshape(1, -1), cos, sins)


def mla_rope_q(qr, cos, sins, name):
    l, w = qr.shape
    tl = ROW_TILE

    def body(q_ref, cos_ref, sin_ref, o_ref):
        c, s = cos_ref[...], sin_ref[...]
        for p in range(w // LANES):
            sl = slice(p * LANES, (p + 1) * LANES)
            o_ref[:, sl] = _rope(q_ref[:, sl], c, s).astype(BF16)

    return _rowcall(body, name, l // tl, [_rows(tl, w), _rows(tl, LANES), _rows(tl, LANES)], _rows(tl, w),
                    jax.ShapeDtypeStruct((l, w), BF16))(qr, cos, sins)


ATT_OUT = 512
ATT_IN = 512
ATT_R = ATT_OUT // ATT_IN


def _scores(qn, qr, kn, kr, mask_off, transposed):
    q2 = jnp.concatenate([qn, qr], axis=1)
    k2 = jnp.concatenate([kn, kr], axis=1)
    s = (_dot(k2, q2, DN_NT) if transposed else _dot(q2, k2, DN_NT)) * MLA_SCALE
    if mask_off is None:
        return s
    r = lax.broadcasted_iota(jnp.int32, s.shape, 0)
    c = lax.broadcasted_iota(jnp.int32, s.shape, 1)
    return jnp.where((r <= c + mask_off) if transposed else (c + mask_off <= r), s, NEG_INF)


def _fold(x, op):
    out = x[:, :LANES]
    for t in range(1, x.shape[1] // LANES):
        out = op(out, x[:, t * LANES:(t + 1) * LANES])
    return out


def flash_fwd(qn, qr, kv, krs, name):
    l = qn.shape[0]
    nq = l // ATT_OUT

    def body(qn_ref, qr_ref, kv_ref, kr_ref, o_ref, lse_ref, s_buf):
        qi = pl.program_id(1)
        q_r = qr_ref[...]
        q_n = [qn_ref[:, hh * LANES:(hh + 1) * LANES] for hh in range(2)]

        def block_scores(j, mx, mask_off):
            sl = pl.ds(pl.multiple_of(j * ATT_IN, ATT_IN), ATT_IN)
            out = []
            for hh in range(2):
                s = _scores(q_n[hh], q_r, kv_ref[sl, 2 * hh * LANES:(2 * hh + 1) * LANES], kr_ref[hh, sl, :],
                            mask_off, False)
                s_buf[hh, j] = s
                out.append(jnp.maximum(mx[hh], _fold(s, jnp.maximum)))
            return tuple(out)

        ninf = jnp.full((ATT_OUT, LANES), NEG_INF, F32)
        mx = lax.fori_loop(0, ATT_R * qi, lambda j, c: block_scores(j, c, None), (ninf, ninf))
        for d in range(ATT_R):
            mx = block_scores(ATT_R * qi + d, mx, d * ATT_IN)
        m = [jnp.max(mx[hh], axis=-1, keepdims=True) for hh in range(2)]

        def block_pv(j, carry):
            sl = pl.ds(pl.multiple_of(j * ATT_IN, ATT_IN), ATT_IN)
            out = []
            for hh in range(2):
                ls, acc = carry[hh]
                p = jnp.exp(s_buf[hh, j] - m[hh])
                out.append((ls + _fold(p, jnp.add),
                            acc + _dot(p, kv_ref[sl, (2 * hh + 1) * LANES:(2 * hh + 2) * LANES], DN_NN)))
            return tuple(out)

        z = jnp.zeros((ATT_OUT, LANES), F32)
        res = lax.fori_loop(0, ATT_R * (qi + 1), block_pv, ((z, z), (z, z)))
        for hh in range(2):
            lsum = jnp.sum(res[hh][0], axis=-1, keepdims=True)
            o_ref[:, hh * LANES:(hh + 1) * LANES] = res[hh][1] / lsum
            lse_ref[hh] = m[hh] + jnp.log(lsum)

    st = jax.ShapeDtypeStruct
    return pl.pallas_call(
        body, name=name, grid=(MLA_HEADS // 2, nq),
        in_specs=[pl.BlockSpec((ATT_OUT, 2 * LANES), lambda p, i: (i, p)),
                  pl.BlockSpec((ATT_OUT, LANES), lambda p, i: (i, p)),
                  pl.BlockSpec((l, 4 * LANES), lambda p, i: (0, p)),
                  pl.BlockSpec((2, l, LANES), lambda p, i: (0, 0, 0))],
        out_specs=[pl.BlockSpec((ATT_OUT, 2 * LANES), lambda p, i: (i, p)),
                   pl.BlockSpec((2, ATT_OUT, 1), lambda p, i: (p, i, 0))],
        out_shape=[st((l, MLA_DI), F32), st((MLA_HEADS, l, 1), F32)],
        scratch_shapes=[pltpu.VMEM((2, l // ATT_IN, ATT_OUT, ATT_IN), F32)],
        compiler_params=_cparams(("parallel", "arbitrary")))(qn, qr, kv, krs)


def flash_dkv(qn, qr, kv, krs, do, lse_row, delta_row, name):
    l = qn.shape[0]
    nk = l // ATT_OUT
    nq = l // ATT_IN

    def body(qn_ref, qr_ref, do_ref, lse_ref, dl_ref, kv_ref, kr_ref, dkv_ref, dkr_ref):
        kj = pl.program_id(1)
        lane = lax.broadcasted_iota(jnp.int32, (ATT_OUT, LANES), 1)
        kn = [kv_ref[:, 2 * hh * LANES:(2 * hh + 1) * LANES] for hh in range(2)]
        v = [kv_ref[:, (2 * hh + 1) * LANES:(2 * hh + 2) * LANES] for hh in range(2)]

        def block(i, carry, mask_off):
            sl = pl.ds(pl.multiple_of(i * ATT_IN, ATT_IN), ATT_IN)
            q_r = qr_ref[sl, :]
            out = []
            for hh in range(2):
                dk2, dv = carry[hh]
                hs = slice(hh * LANES, (hh + 1) * LANES)
                q_n, d_o = qn_ref[sl, hs], do_ref[sl, hs]
                s = _scores(q_n, q_r, kn[hh], kr_ref[hh], mask_off, True)
                pt = jnp.exp(s - lse_ref[hh, i])
                dv = dv + _dot(pt, d_o, DN_NN)
                dpt = _dot(v[hh], d_o, DN_NT)
                dst = (pt * (dpt - dl_ref[hh, i]) * MLA_SCALE).astype(BF16)
                out.append((dk2 + _dot(dst, jnp.concatenate([q_n, q_r], axis=1), DN_NN), dv))
            return tuple(out)

        z = jnp.zeros((ATT_OUT, LANES), F32)
        z2 = jnp.zeros((ATT_OUT, 2 * LANES), F32)
        res = ((z2, z), (z2, z))
        for d in range(ATT_R):
            res = block(ATT_R * kj + d, res, d * ATT_IN)
        res = lax.fori_loop(ATT_R * (kj + 1), nq, lambda i, c: block(i, c, None), res)
        for hh in range(2):
            dkv_ref[:, 2 * hh * LANES:(2 * hh + 1) * LANES] = res[hh][0][:, :LANES].astype(BF16)
            dkv_ref[:, (2 * hh + 1) * LANES:(2 * hh + 2) * LANES] = res[hh][1].astype(BF16)
        dkr_ref[0] = jnp.where(lane < MLA_ROPE, res[0][0][:, LANES:], res[1][0][:, LANES:])

    st = jax.ShapeDtypeStruct
    return pl.pallas_call(
        body, name=name, grid=(MLA_HEADS // 2, nk),
        in_specs=[pl.BlockSpec((l, 2 * LANES), lambda p, j: (0, p)),
                  pl.BlockSpec((l, LANES), lambda p, j: (0, p)),
                  pl.BlockSpec((l, 2 * LANES), lambda p, j: (0, p)),
                  pl.BlockSpec((2, nq, 1, ATT_IN), lambda p, j: (p, 0, 0, 0)),
                  pl.BlockSpec((2, nq, 1, ATT_IN), lambda p, j: (p, 0, 0, 0)),
                  pl.BlockSpec((ATT_OUT, 4 * LANES), lambda p, j: (j, p)),
                  pl.BlockSpec((2, ATT_OUT, LANES), lambda p, j: (0, j, 0))],
        out_specs=[pl.BlockSpec((ATT_OUT, 4 * LANES), lambda p, j: (j, p)),
                   pl.BlockSpec((1, ATT_OUT, LANES), lambda p, j: (p, j, 0))],
        out_shape=[st((l, 2 * MLA_DI), BF16), st((MLA_HEADS // 2, l, LANES), F32)],
        compiler_params=_cparams(("parallel", "arbitrary")))(qn, qr, do, lse_row, delta_row, kv, krs)


def flash_dq(qn, qr, kv, krs, do, lse, delta, cos, sins, name):
    l = qn.shape[0]
    nq = l // ATT_OUT

    def body(qn_ref, qr_ref, do_ref, lse_ref, dl_ref, kv_ref, kr_ref, cos_ref, sin_ref, dqn_ref, dqr_ref):
        qi = pl.program_id(1)
        q_r = qr_ref[...]
        q_n = [qn_ref[:, hh * LANES:(hh + 1) * LANES] for hh in range(2)]
        d_o = [do_ref[:, hh * LANES:(hh + 1) * LANES] for hh in range(2)]
        lse_h = [lse_ref[hh] for hh in range(2)]
        dl_h = [dl_ref[hh] for hh in range(2)]

        def block(j, carry, mask_off):
            sl = pl.ds(pl.multiple_of(j * ATT_IN, ATT_IN), ATT_IN)
            dq2 = list(carry)
            for hh in range(2):
                kn = kv_ref[sl, 2 * hh * LANES:(2 * hh + 1) * LANES]
                v = kv_ref[sl, (2 * hh + 1) * LANES:(2 * hh + 2) * LANES]
                kr = kr_ref[hh, sl, :]
                s = _scores(q_n[hh], q_r, kn, kr, mask_off, False)
                pr = jnp.exp(s - lse_h[hh])
                dp = _dot(d_o[hh], v, DN_NT)
                ds = (pr * (dp - dl_h[hh]) * MLA_SCALE).astype(BF16)
                dq2[hh] = dq2[hh] + _dot(ds, jnp.concatenate([kn, kr], axis=1), DN_NN)
            return tuple(dq2)

        z2 = jnp.zeros((ATT_OUT, 2 * LANES), F32)
        res = lax.fori_loop(0, ATT_R * qi, lambda j, c: block(j, c, None), (z2, z2))
        for d in range(ATT_R):
            res = block(ATT_R * qi + d, res, d * ATT_IN)
        dqn_ref[:, 0:LANES] = res[0][:, :LANES].astype(BF16)
        dqn_ref[:, LANES:2 * LANES] = res[1][:, :LANES].astype(BF16)
        dqr = res[0][:, LANES:] + res[1][:, LANES:]
        dqr_ref[...] = _rope_t(dqr, cos_ref[...], sin_ref[...]).astype(BF16)

    st = jax.ShapeDtypeStruct
    return pl.pallas_call(
        body, name=name, grid=(MLA_HEADS // 2, nq),
        in_specs=[pl.BlockSpec((ATT_OUT, 2 * LANES), lambda p, i: (i, p)),
                  pl.BlockSpec((ATT_OUT, LANES), lambda p, i: (i, p)),
                  pl.BlockSpec((ATT_OUT, 2 * LANES), lambda p, i: (i, p)),
                  pl.BlockSpec((2, ATT_OUT, 1), lambda p, i: (p, i, 0)),
                  pl.BlockSpec((2, ATT_OUT, 1), lambda p, i: (p, i, 0)),
                  pl.BlockSpec((l, 4 * LANES), lambda p, i: (0, p)),
                  pl.BlockSpec((2, l, LANES), lambda p, i: (0, 0, 0)),
                  pl.BlockSpec((ATT_OUT, LANES), lambda p, i: (i, 0)),
                  pl.BlockSpec((ATT_OUT, LANES), lambda p, i: (i, 0))],
        out_specs=[pl.BlockSpec((ATT_OUT, 2 * LANES), lambda p, i: (i, p)),
                   pl.BlockSpec((ATT_OUT, LANES), lambda p, i: (i, p))],
        out_shape=[st((l, MLA_DI), BF16), st((l, MLA_HEADS * MLA_ROPE), BF16)],
        compiler_params=_cparams(("parallel", "arbitrary")))(qn, qr, do, lse, delta, kv, krs, cos, sins)


def mla_gate_fwd(o, a, name):
    l = o.shape[0]
    tl = ROW_TILE

    def body(o_ref, z_ref, m_ref):
        m_ref[...] = (o_ref[...] * _silu(z_ref[...])).astype(BF16)

    return _rowcall(body, name, l // tl, [_rows(tl, MLA_DI), _rows(tl, MLA_DI)], _rows(tl, MLA_DI),
                    jax.ShapeDtypeStruct((l, MLA_DI), BF16))(o, a)


def mla_gate_bwd(dm, o, a, name):
    l = o.shape[0]
    tl = ROW_TILE

    def body(dm_ref, o_ref, z_ref, do_ref, dz_ref, dl_ref):
        dmv, ov, z = dm_ref[...], o_ref[...], z_ref[...]
        sz, gz = _silu_both(z)
        d_o = dmv * sz
        do_ref[...] = d_o.astype(BF16)
        dz_ref[...] = (dmv * ov * gz).astype(BF16)
        pr = d_o * ov
        for h in range(MLA_HEADS):
            dl_ref[h] = jnp.sum(pr[:, h * LANES:(h + 1) * LANES], axis=1, keepdims=True)

    st = jax.ShapeDtypeStruct
    return _rowcall(body, name, l // tl, [_rows(tl, MLA_DI)] * 3,
                    [_rows(tl, MLA_DI), _rows(tl, MLA_DI), pl.BlockSpec((MLA_HEADS, tl, 1), lambda i: (0, i, 0))],
                    [st((l, MLA_DI), BF16), st((l, MLA_DI), BF16), st((MLA_HEADS, l, 1), F32)])(dm, o, a)


def mla_post(a, dcqn, dckvn, dkr_pairs, dz, q_g, kv_g, cos, sins, name):
    l = a.shape[0]
    tl = ROW_TILE
    npair = MLA_HEADS // 2

    def norm_bwd(x, g, dy):
        xhat, r = _rmsn(x)
        dxh = dy * g
        return r * (dxh - xhat * jnp.mean(dxh * xhat, axis=-1, keepdims=True)), jnp.sum(dy * xhat, axis=0, keepdims=True)

    def body(a_ref, dq_ref, dk_ref, dkr_ref, dz_ref, qg_ref, kg_ref, cos_ref, sin_ref, da_ref, dqg_ref, dkg_ref):
        i = pl.program_id(0)
        da_ref[:, 0:MLA_DI] = dz_ref[...]
        dcq, dqg = norm_bwd(a_ref[:, MLA_CQ0:MLA_CKV0], qg_ref[...], dq_ref[...])
        da_ref[:, MLA_CQ0:MLA_CKV0] = dcq.astype(BF16)
        dckv, dkg = norm_bwd(a_ref[:, MLA_CKV0:MLA_KR0], kg_ref[...], dk_ref[...])
        da_ref[:, MLA_CKV0:MLA_KR0] = dckv.astype(BF16)
        dk2 = dkr_ref[0]
        for p in range(1, npair):
            dk2 = dk2 + dkr_ref[p]
        dk2 = _rope_t(dk2, cos_ref[...], sin_ref[...])
        dk2 = dk2 + pltpu.roll(dk2, MLA_ROPE, 1)
        lane = lax.broadcasted_iota(jnp.int32, dk2.shape, 1)
        da_ref[:, MLA_KR0:MLA_AW] = jnp.where(lane < MLA_ROPE, dk2, 0.0).astype(BF16)
        _acc(dqg_ref, dqg, i)
        _acc(dkg_ref, dkg, i)

    st = jax.ShapeDtypeStruct
    return _rowcall(body, name, l // tl,
                    [_rows(tl, MLA_AW), _rows(tl, MLA_Q_RANK), _rows(tl, MLA_KV_RANK),
                     pl.BlockSpec((npair, tl, LANES), lambda i: (0, i, 0)), _rows(tl, MLA_DI),
                     _full((1, MLA_Q_RANK)), _full((1, MLA_KV_RANK)), _rows(tl, LANES), _rows(tl, LANES)],
                    [_rows(tl, MLA_AW), _full((1, MLA_Q_RANK)), _full((1, MLA_KV_RANK))],
                    [st((l, MLA_AW), BF16), st((1, MLA_Q_RANK), F32), st((1, MLA_KV_RANK), F32)])(
        a, dcqn, dckvn, dkr_pairs, dz, q_g.reshape(1, -1), kv_g.reshape(1, -1), cos, sins)


def _mla_w_in_perm(w):
    r = MLA_Q_RANK + MLA_KV_RANK + MLA_ROPE
    pad = jnp.zeros(w.shape[:-1] + (MLA_AW - MLA_KR0 - MLA_ROPE,), w.dtype)
    return jnp.concatenate([w[..., r:], w[..., :r], pad], axis=-1)


def _mla_w_in_unperm(g):
    r = MLA_Q_RANK + MLA_KV_RANK + MLA_ROPE
    return jnp.concatenate([g[..., MLA_DI:MLA_DI + r], g[..., :MLA_DI]], axis=-1)


def _mla_w_uq_split(w):
    k = w.shape[0]
    w3 = w.reshape(k, MLA_HEADS, MLA_NOPE + MLA_ROPE)
    return w3[:, :, :MLA_NOPE].reshape(k, MLA_HEADS * MLA_NOPE), w3[:, :, MLA_NOPE:].reshape(k, MLA_HEADS * MLA_ROPE)


def _mla_w_uq_merge(gn, gr):
    k = gn.shape[0]
    return jnp.concatenate([gn.reshape(k, MLA_HEADS, MLA_NOPE), gr.reshape(k, MLA_HEADS, MLA_ROPE)], axis=2).reshape(k, -1)


def mla_layer_fwd(h, p, wf, cos, sins, tag):
    hn = rms_fwd(h, p["norm_g"], tag + "_rms")
    w_in = _mla_w_in_perm(wf["w_in"])
    w_uq_n, w_uq_r = _mla_w_uq_split(wf["w_uq"])
    a = matmul(hn, w_in, "nn", tag + "_mm_in")
    cqn, ckvn, krs = mla_pre(a, p["q_norm_g"], p["kv_norm_g"], cos, sins, tag + "_pre")
    qn = matmul(cqn, w_uq_n, "nn", tag + "_mm_qn", out_dtype=BF16)
    qr_raw = matmul(cqn, w_uq_r, "nn", tag + "_mm_qr")
    qr = mla_rope_q(qr_raw, cos, sins, tag + "_rope_q")
    kv = matmul(ckvn, wf["w_ukv"], "nn", tag + "_mm_kv", out_dtype=BF16)
    o, lse = flash_fwd(qn, qr, kv, krs, tag + "_flash")
    m = mla_gate_fwd(o, a, tag + "_gate")
    h_out = matmul(m, wf["w_out"], "nn", tag + "_mm_out", add=h)
    return h_out, (h, hn, a, cqn, ckvn, krs, qn, qr, kv, o, lse, m, w_in, w_uq_n, w_uq_r)


def mla_layer_bwd(dh_out, saved, p, wf, cos, sins, tag, sink):
    h, hn, a, cqn, ckvn, krs, qn, qr, kv, o, lse, m, w_in, w_uq_n, w_uq_r = saved
    l = h.shape[0]
    dm = matmul(dh_out, wf["w_out"], "nt", tag + "_mm_dm")
    sink.mm("w_out", m, dh_out, tag + "_mm_gwout")
    do, dz, delta = mla_gate_bwd(dm, o, a, tag + "_gate_bwd")
    lse_row = lse.reshape(MLA_HEADS, l // ATT_IN, 1, ATT_IN)
    delta_row = delta.reshape(MLA_HEADS, l // ATT_IN, 1, ATT_IN)
    dkv, dkr_pairs = flash_dkv(qn, qr, kv, krs, do, lse_row, delta_row, tag + "_flash_dkv")
    dqn, dqr = flash_dq(qn, qr, kv, krs, do, lse, delta, cos, sins, tag + "_flash_dq")
    dcqn = matmul(dqn, w_uq_n, "nt", tag + "_mm_dcq_n")
    dcqn = matmul(dqr, w_uq_r, "nt", tag + "_mm_dcq_r", add=dcqn)
    g_uq_n = matmul(cqn, dqn, "tn", tag + "_mm_guq_n")
    g_uq_r = matmul(cqn, dqr, "tn", tag + "_mm_guq_r")
    dckvn = matmul(dkv, wf["w_ukv"], "nt", tag + "_mm_dckv")
    sink.mm("w_ukv", ckvn, dkv, tag + "_mm_gukv")
    da, dqg, dkg = mla_post(a, dcqn, dckvn, dkr_pairs, dz, p["q_norm_g"], p["kv_norm_g"], cos, sins, tag + "_post")
    dhn = matmul(da, w_in, "nt", tag + "_mm_dhn")
    g_w_in = matmul(hn, da, "tn", tag + "_mm_gwin")
    sink.put("w_uq", _mla_w_uq_merge(g_uq_n, g_uq_r))
    sink.put("w_in", _mla_w_in_unperm(g_w_in))
    zero = sink.send()
    dh, dng = rms_bwd(h, p["norm_g"] + zero, dhn, dh_out, tag + "_rms_bwd")
    grads = {"norm_g": dng.reshape(-1), "q_norm_g": dqg.reshape(-1), "kv_norm_g": dkg.reshape(-1)}
    return dh, grads


ANY = pl.BlockSpec(memory_space=pl.ANY)


def _me():
    return lax.axis_index("x"), lax.axis_index("y"), lax.axis_index("c")


def _chip():
    return 2 * lax.axis_index("x") + lax.axis_index("y")


def _other_chips(x, y):
    return [(1 - x, y), (x, 1 - y), (1 - x, 1 - y)]


def _rcopy(src, dst, ssem, rsem, dev):
    return pltpu.make_async_remote_copy(src_ref=src, dst_ref=dst, send_sem=ssem, recv_sem=rsem,
                                        device_id=dev, device_id_type=MESH)


def _half(ref, c, hf):
    return ref.at[pl.ds(c * hf, hf), :]


HBM = pl.BlockSpec(memory_space=pltpu.HBM)
SEM = pl.BlockSpec(memory_space=pltpu.SEMAPHORE)
SPLIT_EFFECT = pltpu.SideEffectType.DATAFLOW_SIDE_EFFECTING


def gather_start(wb, after, name):
    nr, w = wb.shape
    hf = nr // 2

    def body(w_ref, land_ref, after_ref, ssem, rsem, w_thru, land_thru, token):
        x, y, c = _me()
        k = 2 * x + y
        for j, (cx, cy) in enumerate(_other_chips(x, y)):
            _rcopy(_half(w_ref, c, hf), _half(land_ref.at[k], c, hf), ssem.at[j], rsem.at[j], (cx, cy, c)).start()
        token[...] = jnp.zeros_like(token)

    land = lax.empty((N_CHIPS, nr, w), wb.dtype)
    return pl.pallas_call(
        body, name=name,
        out_shape=(pltpu.SemaphoreType.DMA((3,)), pltpu.SemaphoreType.DMA((3,)), pltpu.HBM(wb.shape, wb.dtype),
                   pltpu.HBM(land.shape, land.dtype), jax.ShapeDtypeStruct((8, LANES), F32)),
        in_specs=(HBM, HBM, ANY), out_specs=(SEM, SEM, HBM, HBM, pl.BlockSpec(memory_space=pltpu.VMEM)),
        input_output_aliases={0: 2, 1: 3},
        compiler_params=pltpu.CompilerParams(has_side_effects=SPLIT_EFFECT))(
        pltpu.with_memory_space_constraint(wb, pltpu.HBM), pltpu.with_memory_space_constraint(land, pltpu.HBM), after)


def gather_wait(ssem, rsem, w_thru, land_thru, after, name):
    nr, w = w_thru.shape
    hf = nr // 2

    def body(w_ref, land_ref, ssem_ref, rsem_ref, after_ref, w_dead, got_ref):
        x, y, c = _me()
        for j, (cx, cy) in enumerate(_other_chips(x, y)):
            cp = _rcopy(_half(w_ref, c, hf), _half(land_ref.at[2 * cx + cy], c, hf), ssem_ref.at[j], rsem_ref.at[j],
                        (cx, cy, c))
            cp.wait_send()
            cp.wait_recv()

    return pl.pallas_call(
        body, name=name, out_shape=(pltpu.HBM(w_thru.shape, w_thru.dtype), pltpu.HBM(land_thru.shape, land_thru.dtype)),
        in_specs=(HBM, HBM, SEM, SEM, ANY), out_specs=(HBM, HBM), input_output_aliases={0: 0, 1: 1},
        compiler_params=pltpu.CompilerParams(has_side_effects=SPLIT_EFFECT))(w_thru, land_thru, ssem, rsem, after)[1]


def gather_handover(land, wb, name):
    _, nr, w = land.shape
    hf = nr // 2

    def body(l_ref, o_ref, ssem, rsem):
        x, y, c = _me()
        chips = _other_chips(x, y)
        sends = []
        for j, (cx, cy) in enumerate(chips):
            region = _half(o_ref.at[2 * cx + cy], c, hf)
            sends.append(_rcopy(region, region, ssem.at[j], rsem.at[j], (x, y, 1 - c)))
            sends[-1].start()
        for j, (cx, cy) in enumerate(chips):
            region = _half(o_ref.at[2 * cx + cy], 1 - c, hf)
            _rcopy(region, region, ssem.at[j], rsem.at[j], (x, y, 1 - c)).wait_recv()
        for cp in sends:
            cp.wait_send()

    out = pl.pallas_call(
        body, name=name, in_specs=[ANY], out_specs=ANY, input_output_aliases={0: 0},
        out_shape=jax.ShapeDtypeStruct(land.shape, land.dtype),
        scratch_shapes=[pltpu.SemaphoreType.DMA((3,)), pltpu.SemaphoreType.DMA((3,))])(land)
    return lax.dynamic_update_slice(out, wb[None], (_chip(), 0, 0))


def reduce_start(t, after, name):
    def body(t_ref, land_ref, after_ref, ssem, rsem, t_thru, land_thru, token):
        x, y, c = _me()
        k = 2 * x + y
        for j, (cx, cy) in enumerate(_other_chips(x, y)):
            _rcopy(t_ref.at[2 * cx + cy], land_ref.at[k], ssem.at[j], rsem.at[j], (cx, cy, c)).start()
        token[...] = jnp.zeros_like(token)

    land = lax.empty(t.shape, t.dtype)
    return pl.pallas_call(
        body, name=name,
        out_shape=(pltpu.SemaphoreType.DMA((3,)), pltpu.SemaphoreType.DMA((3,)), pltpu.HBM(t.shape, t.dtype),
                   pltpu.HBM(t.shape, t.dtype), jax.ShapeDtypeStruct((8, LANES), F32)),
        in_specs=(HBM, HBM, ANY), out_specs=(SEM, SEM, HBM, HBM, pl.BlockSpec(memory_space=pltpu.VMEM)),
        input_output_aliases={0: 2, 1: 3},
        compiler_params=pltpu.CompilerParams(has_side_effects=SPLIT_EFFECT))(
        pltpu.with_memory_space_constraint(t, pltpu.HBM), pltpu.with_memory_space_constraint(land, pltpu.HBM), after)


def bcast_start(g, after, name):
    def body(g_ref, land_ref, after_ref, ssem, rsem, g_thru, land_thru, token):
        x, y, c = _me()
        k = 2 * x + y
        for j, (cx, cy) in enumerate(_other_chips(x, y)):
            _rcopy(g_ref, land_ref.at[k], ssem.at[j], rsem.at[j], (cx, cy, c)).start()
        token[...] = jnp.zeros_like(token)

    land = lax.empty((N_CHIPS,) + g.shape, g.dtype)
    return pl.pallas_call(
        body, name=name,
        out_shape=(pltpu.SemaphoreType.DMA((3,)), pltpu.SemaphoreType.DMA((3,)), pltpu.HBM(g.shape, g.dtype),
                   pltpu.HBM(land.shape, land.dtype), jax.ShapeDtypeStruct((8, LANES), F32)),
        in_specs=(HBM, HBM, ANY), out_specs=(SEM, SEM, HBM, HBM, pl.BlockSpec(memory_space=pltpu.VMEM)),
        input_output_aliases={0: 2, 1: 3},
        compiler_params=pltpu.CompilerParams(has_side_effects=SPLIT_EFFECT))(
        pltpu.with_memory_space_constraint(g, pltpu.HBM), pltpu.with_memory_space_constraint(land, pltpu.HBM), after)


def bcast_wait(ssem, rsem, g_thru, land_thru, after, name):
    def body(g_ref, land_ref, ssem_ref, rsem_ref, after_ref, g_out, got_ref):
        x, y, c = _me()
        for j, (cx, cy) in enumerate(_other_chips(x, y)):
            cp = _rcopy(g_ref, land_ref.at[2 * cx + cy], ssem_ref.at[j], rsem_ref.at[j], (cx, cy, c))
            cp.wait_send()
            cp.wait_recv()

    g, land = pl.pallas_call(
        body, name=name, out_shape=(pltpu.HBM(g_thru.shape, g_thru.dtype), pltpu.HBM(land_thru.shape, land_thru.dtype)),
        in_specs=(HBM, HBM, SEM, SEM, ANY), out_specs=(HBM, HBM), input_output_aliases={0: 0, 1: 1},
        compiler_params=pltpu.CompilerParams(has_side_effects=SPLIT_EFFECT))(g_thru, land_thru, ssem, rsem, after)
    return lax.dynamic_update_slice(land, g[None], (_chip(), 0, 0))


def reduce_wait(ssem, rsem, t_thru, land_thru, after, name):
    def body(t_ref, land_ref, ssem_ref, rsem_ref, after_ref, t_out, got_ref):
        x, y, c = _me()
        k = 2 * x + y
        for j, (cx, cy) in enumerate(_other_chips(x, y)):
            cp = _rcopy(t_ref.at[k], land_ref.at[2 * cx + cy], ssem_ref.at[j], rsem_ref.at[j], (cx, cy, c))
            cp.wait_send()
            cp.wait_recv()

    return pl.pallas_call(
        body, name=name, out_shape=(pltpu.HBM(t_thru.shape, t_thru.dtype), pltpu.HBM(land_thru.shape, land_thru.dtype)),
        in_specs=(HBM, HBM, SEM, SEM, ANY), out_specs=(HBM, HBM), input_output_aliases={0: 0, 1: 1},
        compiler_params=pltpu.CompilerParams(has_side_effects=SPLIT_EFFECT))(t_thru, land_thru, ssem, rsem, after)


def grads_to_sibling(ps, name="grads_to_sibling"):
    n = len(ps)

    def body(*refs):
        p_refs, o_refs, ssem, rsem = refs[:n], refs[n:2 * n], refs[2 * n], refs[2 * n + 1]
        x, y, c = _me()
        cps = []
        for a in range(n):
            hf = ps[a].shape[1] // 2
            cps.append(_rcopy(p_refs[a].at[:, pl.ds((1 - c) * hf, hf), :], o_refs[a], ssem.at[a], rsem.at[a],
                              (x, y, 1 - c)))
        for cp in cps:
            cp.start()
        for cp in cps:
            cp.wait()

    return pl.pallas_call(
        body, name=name, in_specs=[ANY] * n, out_specs=[ANY] * n,
        out_shape=[jax.ShapeDtypeStruct((N_CHIPS, p.shape[1] // 2, p.shape[2]), p.dtype) for p in ps],
        scratch_shapes=[pltpu.SemaphoreType.DMA((n,)), pltpu.SemaphoreType.DMA((n,))])(*ps)


def pair_sum(p, ra, out_dtype, name):
    _, nr, w = p.shape
    hf = nr // 2
    tr = _pick_rows(hf, cap=max(512, SUM_BLOCK_BYTES // (4 * w)))
    nb = hf // tr

    def body(c_ref, p_ref, r_ref, o_ref):
        o_ref[...] = (p_ref[...] + r_ref[...]).astype(out_dtype)

    c = lax.axis_index("c").astype(jnp.int32).reshape(1)
    return pl.pallas_call(
        body, name=name,
        grid_spec=pltpu.PrefetchScalarGridSpec(
            num_scalar_prefetch=1, grid=(N_CHIPS, nb),
            in_specs=[pl.BlockSpec((1, tr, w), lambda k, i, c_ref: (k, c_ref[0] * nb + i, 0)),
                      pl.BlockSpec((1, tr, w), lambda k, i, c_ref: (k, i, 0))],
            out_specs=pl.BlockSpec((1, tr, w), lambda k, i, c_ref: (k, i, 0))),
        out_shape=jax.ShapeDtypeStruct((N_CHIPS, hf, w), out_dtype),
        compiler_params=_cparams(("parallel", "parallel")))(c, p, ra)


def grads_across_chips(ts):
    n = len(ts)

    def body(*refs):
        t_refs, o_refs, ssem, rsem = refs[:n], refs[n:2 * n], refs[2 * n], refs[2 * n + 1]
        x, y, c = _me()
        k = 2 * x + y
        chips = _other_chips(x, y)
        sends = [_rcopy(t_refs[a].at[2 * cx + cy], o_refs[a].at[k], ssem.at[3 * a + j], rsem.at[3 * a + j], (cx, cy, c))
                 for a in range(n) for j, (cx, cy) in enumerate(chips)]
        for cp in sends:
            cp.start()
        for a in range(n):
            for j, (cx, cy) in enumerate(chips):
                _rcopy(t_refs[a].at[k], o_refs[a].at[2 * cx + cy], ssem.at[3 * a + j], rsem.at[3 * a + j],
                       (cx, cy, c)).wait_recv()
        for cp in sends:
            cp.wait_send()

    return pl.pallas_call(
        body, name="grads_across_chips", in_specs=[ANY] * n, out_specs=[ANY] * n,
        out_shape=[jax.ShapeDtypeStruct(t.shape, t.dtype) for t in ts],
        scratch_shapes=[pltpu.SemaphoreType.DMA((3 * n,)), pltpu.SemaphoreType.DMA((3 * n,))])(*ts)


def chip_sum(t, rb, name):
    _, hf, w = rb.shape
    tr = _pick_rows(hf, cap=max(512, SUM_BLOCK_BYTES // (4 * w)))
    nb = hf // tr

    def body(kc_ref, t_ref, r_ref, o_ref):
        k = kc_ref[0]
        acc = jnp.where(k == 0, t_ref[0], r_ref[0]).astype(F32)
        for j in range(1, N_CHIPS):
            acc = acc + jnp.where(k == j, t_ref[0], r_ref[j]).astype(F32)
        o_ref[...] = acc

    kc = jnp.stack([_chip(), lax.axis_index("c")]).astype(jnp.int32)
    return pl.pallas_call(
        body, name=name,
        grid_spec=pltpu.PrefetchScalarGridSpec(
            num_scalar_prefetch=1, grid=(nb,),
            in_specs=[pl.BlockSpec((1, tr, w), lambda i, kc_ref: (kc_ref[0], i, 0)),
                      pl.BlockSpec((N_CHIPS, tr, w), lambda i, kc_ref: (0, i, 0))],
            out_specs=pl.BlockSpec((tr, w), lambda i, kc_ref: (kc_ref[1] * nb + i, 0))),
        out_shape=jax.ShapeDtypeStruct((2 * hf, w), F32), compiler_params=_cparams(("parallel",)))(kc, t, rb)


def reduced_to_sibling(gs):
    n = len(gs)

    def body(*refs):
        o_refs, ssem, rsem = refs[n:2 * n], refs[2 * n], refs[2 * n + 1]
        x, y, c = _me()
        cps = []
        for a in range(n):
            hf = gs[a].shape[0] // 2
            cps.append(_rcopy(_half(o_refs[a], c, hf), _half(o_refs[a], c, hf), ssem.at[a], rsem.at[a], (x, y, 1 - c)))
        for cp in cps:
            cp.start()
        for a in range(n):
            hf = gs[a].shape[0] // 2
            _rcopy(_half(o_refs[a], c, hf), _half(o_refs[a], 1 - c, hf), ssem.at[a], rsem.at[a],
                   (x, y, 1 - c)).wait_recv()
        for cp in cps:
            cp.wait_send()

    return pl.pallas_call(
        body, name="reduced_to_sibling", in_specs=[ANY] * n, out_specs=[ANY] * n,
        input_output_aliases={a: a for a in range(n)},
        out_shape=[jax.ShapeDtypeStruct(g.shape, g.dtype) for g in gs],
        scratch_shapes=[pltpu.SemaphoreType.DMA((n,)), pltpu.SemaphoreType.DMA((n,))])(*gs)


def _adamw_step(w_ref, g_ref, m_ref, v_ref, d_ref, nm_ref, nv_ref):
    bc1 = 1.0 - ADAM_B1 ** ADAM_STEP
    bc2 = 1.0 - ADAM_B2 ** ADAM_STEP
    gv = g_ref[...]
    nm = ADAM_B1 * m_ref[...] + (1.0 - ADAM_B1) * gv
    nv = ADAM_B2 * v_ref[...] + (1.0 - ADAM_B2) * (gv * gv)
    nm_ref[...] = nm
    nv_ref[...] = nv
    d_ref[...] = -ADAM_LR * ((nm / bc1) / (jnp.sqrt(nv / bc2) + ADAM_EPS) + ADAM_WD * w_ref[...])


def adamw_packed(w, g_buf, r0, m, v, name):
    r, c = w.shape
    tr = _tile_rows(r, r0)

    def body(w_ref, g_ref, m_ref, v_ref, go_ref, d_ref, nm_ref, nv_ref):
        go_ref[...] = g_ref[...]
        _adamw_step(w_ref, g_ref, m_ref, v_ref, d_ref, nm_ref, nv_ref)

    own = pl.BlockSpec((tr, CHUNK_W), lambda i, j: (i, j))
    packed = pl.BlockSpec((tr, CHUNK_W), lambda i, j: ((r0 + j * r) // tr + i, 0))
    st = jax.ShapeDtypeStruct((r, c), F32)
    return pl.pallas_call(body, name=name, grid=(r // tr, c // CHUNK_W), in_specs=[own, packed, own, own],
                          out_specs=[own] * 4, out_shape=[st] * 4,
                          compiler_params=_cparams(("parallel", "parallel")))(w, g_buf, m, v)


def adamw(w, g, m, v, name):
    r, wd = w.shape
    tr = _pick_rows(r, cap=max(16, ADAMW_BLOCK_BYTES // (4 * wd)))
    body = functools.partial(_adamw_step)

    spec = pl.BlockSpec((tr, wd), lambda i: (i, 0))
    st = jax.ShapeDtypeStruct((r, wd), F32)
    return pl.pallas_call(body, name=name, grid=(r // tr,), in_specs=[spec] * 4, out_specs=[spec] * 3,
                          out_shape=[st, st, st], compiler_params=_cparams(("parallel",)))(w, g, m, v)


LAYER_KINDS = ("gmlp", "s5", "mla", "gmlp")
PARAMS = {
    "gmlp": ("norm_g", "w_in", "ln_g", "ln_b", "w_s", "b_s", "w_out"),
    "s5": ("norm_g", "w_in", "a_re", "a_im", "log_step", "b_re", "b_im", "c_re", "c_im", "d_skip", "w_glu", "b_glu", "w_out"),
    "mla": ("norm_g", "w_in", "q_norm_g", "w_uq", "kv_norm_g", "w_ukv", "w_out"),
}
COL_SHARDED = ("w_in", "w_uq", "w_ukv")
ROW_SHARDED = ("w_out", "w_glu")
WEIGHT_NAMES = [("l%d_" % i) + n for i, kind in enumerate(LAYER_KINDS) for n in PARAMS[kind]] + ["final_norm_g"]


def _is_big(name):
    return name.split("_", 1)[1] in COL_SHARDED + ROW_SHARDED


BIG = [n for n in WEIGHT_NAMES if _is_big(n)]
SMALL = [n for n in WEIGHT_NAMES if not _is_big(n)]


def _pack_rows(blocks):
    return jnp.concatenate([b.reshape(-1, PACK_W) for b in blocks], axis=0)


def _shard_major(wn, full, width):
    r, c = full.shape
    if wn in COL_SHARDED:
        t = full.reshape(r, N_CHIPS, c // N_CHIPS).transpose(1, 0, 2)
    else:
        t = full.reshape(N_CHIPS, r // N_CHIPS, c)
    return t.reshape(N_CHIPS, -1, width)


def _from_shard_major(name, t, block_shape):
    r, c = block_shape
    if name.split("_", 1)[1] in COL_SHARDED:
        return t.reshape(N_CHIPS, r, c).transpose(1, 0, 2).reshape(r, N_CHIPS * c)
    return t.reshape(N_CHIPS * r, c)


class BigGradSink:
    ORDER = ("w_out", "w_glu", "w_ukv", "w_uq", "w_in")
    ROW_MAJOR = {2: ("w_uq", "w_in")}

    def __init__(self, layer, block_shapes):
        self.layer = layer
        self.regions = {}
        r0 = 0
        for wn in self.ORDER:
            if wn in block_shapes:
                shape = block_shapes[wn]
                self.regions[wn] = (r0, shape, wn not in self.ROW_MAJOR.get(layer, ()))
                r0 += shape[0] * shape[1] // CHUNK_W
        self.buf = lax.empty((N_CHIPS, r0, CHUNK_W), F32)
        self.flight = None

    def mm(self, wn, a, b, name):
        r0, _, direct = self.regions[wn]
        assert direct
        self.buf = matmul_tn_packed(a, b, self.buf, r0, wn in COL_SHARDED, name)

    def put(self, wn, full):
        r0, _, direct = self.regions[wn]
        assert not direct
        piece = _shard_major(wn, full, CHUNK_W)
        self.buf = lax.dynamic_update_slice(self.buf, piece, (0, r0, 0))

    def send(self):
        i = self.layer
        sib, = grads_to_sibling([self.buf], "grads_to_sibling_l%d" % i)
        t = pair_sum(self.buf, sib, BF16, "pair_sum_l%d" % i)
        self.flight = reduce_start(t, sib, "reduce_l%d_start" % i)
        return self.flight[4][0, 0]


def _small_pack(arrs, total_padded):
    flat = jnp.concatenate([a.reshape(-1) for a in arrs])
    return jnp.pad(flat, (0, total_padded - flat.shape[0]))


def kernel(x, positions, l0_norm_g, l0_w_in, l0_ln_g, l0_ln_b, l0_w_s, l0_b_s, l0_w_out, l1_norm_g, l1_w_in, l1_a_re, l1_a_im, l1_log_step, l1_b_re, l1_b_im, l1_c_re, l1_c_im, l1_d_skip, l1_w_glu, l1_b_glu, l1_w_out, l2_norm_g, l2_w_in, l2_q_norm_g, l2_w_uq, l2_kv_norm_g, l2_w_ukv, l2_w_out, l3_norm_g, l3_w_in, l3_ln_g, l3_ln_b, l3_w_s, l3_b_s, l3_w_out, final_norm_g, loss_target, m_l0_norm_g, m_l0_w_in, m_l0_ln_g, m_l0_ln_b, m_l0_w_s, m_l0_b_s, m_l0_w_out, m_l1_norm_g, m_l1_w_in, m_l1_a_re, m_l1_a_im, m_l1_log_step, m_l1_b_re, m_l1_b_im, m_l1_c_re, m_l1_c_im, m_l1_d_skip, m_l1_w_glu, m_l1_b_glu, m_l1_w_out, m_l2_norm_g, m_l2_w_in, m_l2_q_norm_g, m_l2_w_uq, m_l2_kv_norm_g, m_l2_w_ukv, m_l2_w_out, m_l3_norm_g, m_l3_w_in, m_l3_ln_g, m_l3_ln_b, m_l3_w_s, m_l3_b_s, m_l3_w_out, m_final_norm_g, v_l0_norm_g, v_l0_w_in, v_l0_ln_g, v_l0_ln_b, v_l0_w_s, v_l0_b_s, v_l0_w_out, v_l1_norm_g, v_l1_w_in, v_l1_a_re, v_l1_a_im, v_l1_log_step, v_l1_b_re, v_l1_b_im, v_l1_c_re, v_l1_c_im, v_l1_d_skip, v_l1_w_glu, v_l1_b_glu, v_l1_w_out, v_l2_norm_g, v_l2_w_in, v_l2_q_norm_g, v_l2_w_uq, v_l2_kv_norm_g, v_l2_w_ukv, v_l2_w_out, v_l3_norm_g, v_l3_w_in, v_l3_ln_g, v_l3_ln_b, v_l3_w_s, v_l3_b_s, v_l3_w_out, v_final_norm_g):
    args = locals()
    w = {n: args[n] for n in WEIGHT_NAMES}
    mom_m = {n: args["m_" + n] for n in WEIGHT_NAMES}
    mom_v = {n: args["v_" + n] for n in WEIGHT_NAMES}
    h0 = x[0]
    target = loss_target[0]
    pos = positions.reshape(-1, 1)

    full = {}

    def pack_unit(layers):
        names = [n for n in BIG if int(n[1]) in layers]
        rows = [w[n].size // PACK_W for n in names]
        pad = -sum(rows) % PACK_ROW_ALIGN
        return names, rows, _pack_rows([w[n].astype(BF16) for n in names] + [jnp.zeros((pad, PACK_W), BF16)])

    def unpack_unit(names, rows, gathered):
        r0 = 0
        for n, nr in zip(names, rows):
            full[n] = _from_shard_major(n, gathered[:, r0:r0 + nr, :], w[n].shape)
            r0 += nr

    unit0, unit1, unit2 = pack_unit((0,)), pack_unit((1,)), pack_unit((2, 3))
    wp = dict(w)

    def layer_params(i):
        pre = "l%d_" % i
        p = {k[len(pre):]: v for k, v in wp.items() if k.startswith(pre)}
        wf = {k[len(pre):]: v for k, v in full.items() if k.startswith(pre)}
        return p, wf

    flight = gather_start(unit0[2], unit1[2], "gather_l0_start")
    cos, sins = rope_tables(pos, flight[4][0, 0])
    wp["l1_a_re"] = w["l1_a_re"] + flight[4][0, 0]
    s5_weights = _s5_weights(layer_params(1)[0])
    land = gather_wait(*flight[:4], s5_weights[2], "gather_l0_wait")
    got = gather_handover(land, unit0[2], "gather_l0_handover")
    unpack_unit(unit0[0], unit0[1], got)
    flight = gather_start(unit1[2], got, "gather_l1_start")
    wp["l0_norm_g"] = w["l0_norm_g"] + flight[4][0, 0]

    h = h0
    saved = []
    for i, kind in enumerate(LAYER_KINDS):
        if i == 1:
            land = gather_wait(*flight[:4], h, "gather_l1_wait")
            got = gather_handover(land, unit1[2], "gather_l1_handover")
            unpack_unit(unit1[0], unit1[1], got)
            flight = gather_start(unit2[2], got, "gather_l23_start")
            wp["l1_norm_g"] = w["l1_norm_g"] + flight[4][0, 0]
        if i == 2:
            land = gather_wait(*flight[:4], h, "gather_l23_wait")
            unpack_unit(unit2[0], unit2[1], gather_handover(land, unit2[2], "gather_l23_handover"))
        p, wf = layer_params(i)
        tag = "l%d" % i
        if kind == "gmlp":
            h, s = gmlp_layer_fwd(h, p, wf, tag)
        elif kind == "s5":
            h, s = s5_layer_fwd(h, p, wf, s5_weights, tag)
        else:
            h, s = mla_layer_fwd(h, p, wf, cos, sins, tag)
        saved.append(s)
    loss_part, dh, g_final = loss_head(h, final_norm_g, target)

    grads = {"final_norm_g": g_final.reshape(-1)}
    sinks = {}

    for i in reversed(range(len(LAYER_KINDS))):
        kind = LAYER_KINDS[i]
        p, wf = layer_params(i)
        tag = "l%d" % i
        sink = sinks[i] = BigGradSink(i, {n[3:]: w[n].shape for n in BIG if int(n[1]) == i})
        if kind == "gmlp":
            dh, g = gmlp_layer_bwd(dh, saved[i], p, wf, tag, sink)
        elif kind == "s5":
            dh, g = s5_layer_bwd(dh, saved[i], p, wf, tag, sink)
        else:
            dh, g = mla_layer_bwd(dh, saved[i], p, wf, cos, sins, tag, sink)
        for k, val in g.items():
            grads["l%d_%s" % (i, k)] = val
    grad_x = dh[None]

    n_small = sum(w[n].size for n in SMALL)
    piece = N_CHIPS * 2 * 16 * PACK_W
    n_small_pad = -(-(n_small + 1) // piece) * piece
    nrs = n_small_pad // N_CHIPS // PACK_W
    p_small = _small_pack([grads[n] for n in SMALL] + [loss_part], n_small_pad).reshape(N_CHIPS, nrs, PACK_W)
    sib_small, = grads_to_sibling([p_small], "grads_to_sibling_small")
    t_small = pair_sum(p_small, sib_small, F32, "pair_sum_small")
    rb_small, = grads_across_chips([t_small])
    halves = [chip_sum(t_small, rb_small, "chip_sum_small")]

    after = halves[0]
    for i in reversed(range(len(LAYER_KINDS))):
        t_i, rb_i = reduce_wait(*sinks[i].flight[:4], after, "reduce_l%d_wait" % i)
        halves.append(chip_sum(t_i, rb_i, "chip_sum_l%d" % i))
        after = halves[-1]
    reduced = reduced_to_sibling(halves)
    small_flight = bcast_start(reduced[0], reduced[1], "small_allgather_start")

    g_out, d_out, nm_out, nv_out = {}, {}, {}, {}
    for i, g_i in zip(reversed(range(len(LAYER_KINDS))), reduced[1:]):
        for wn, (r0, shape, direct) in sinks[i].regions.items():
            n = "l%d_%s" % (i, wn)
            if direct:
                g_out[n], d_out[n], nm_out[n], nv_out[n] = adamw_packed(w[n], g_i, r0, mom_m[n], mom_v[n], "adamw_" + n)
            else:
                g_out[n] = g_i[r0:r0 + shape[0] * shape[1] // CHUNK_W].reshape(shape)
                d_out[n], nm_out[n], nv_out[n] = adamw(w[n], g_out[n], mom_m[n], mom_v[n], "adamw_" + n)
    small_all = bcast_wait(*small_flight[:4], nv_out["l0_w_in"], "small_allgather_wait")
    g_small = small_all.reshape(-1, PACK_W)
    sp = lambda d: _small_pack([d[n] for n in SMALL], n_small_pad).reshape(-1, PACK_W)
    d_small, nm_small, nv_small = adamw(sp(w), g_small, sp(mom_m), sp(mom_v), "adamw_small")
    for buf, out in ((g_small, g_out), (d_small, d_out), (nm_small, nm_out), (nv_small, nv_out)):
        flat = buf.reshape(-1)
        o = 0
        for n in SMALL:
            out[n] = flat[o:o + w[n].size].reshape(w[n].shape)
            o += w[n].size
    loss = g_small.reshape(-1)[n_small]
    return (loss, grad_x, *[g_out[n] for n in WEIGHT_NAMES], *[d_out[n] for n in WEIGHT_NAMES],
            *[nm_out[n] for n in WEIGHT_NAMES], *[nv_out[n] for n in WEIGHT_NAMES])
```

```python
import functools
import math

import jax
import jax.numpy as jnp
import numpy as np
from jax import lax
from jax.experimental import pallas as pl
from jax.experimental.pallas import tpu as pltpu

F32 = jnp.float32
BF16 = jnp.bfloat16
MESH = pl.DeviceIdType.MESH
VMEM_LIMIT_BYTES = 56 * 1024 * 1024
LANES = 128
PACK_W = 1024
CHUNK_W = 256
PACK_ROW_ALIGN = 256
ROW_TILE = 256
ROW_TILE_NARROW = 512
SUM_BLOCK_BYTES = 1024 * 1024
ADAMW_BLOCK_BYTES = 1024 * 1024
MM_BLOCK_BYTES = 12 * 1024 * 1024

NORM_EPS = 1e-6
N_CHIPS = 4
GMLP_CHUNK = 128
GMLP_GROUPS = 8
S5_GROUPS = 128
S5_GROUP = 16
S5_STATE = 64
S5_SB = 16
S5_SEG = 8
MLA_HEADS = 16
MLA_NOPE = 128
MLA_ROPE = 64
MLA_Q_RANK = 384
MLA_KV_RANK = 128
MLA_SCALE = (MLA_NOPE + MLA_ROPE) ** -0.5
ROPE_THETA = 10000.0
NEG_INF = -1e30
ADAM_LR, ADAM_B1, ADAM_B2, ADAM_EPS, ADAM_WD, ADAM_STEP = 0.001, 0.9, 0.999, 1e-08, 0.01, 10

DN_NN = (((1,), (0,)), ((), ()))
DN_NT = (((1,), (1,)), ((), ()))
DN_TN = (((0,), (0,)), ((), ()))


def _cparams(sem):
    return pltpu.CompilerParams(dimension_semantics=sem, vmem_limit_bytes=VMEM_LIMIT_BYTES)


def _pick(n, cands=(512, 384, 256, 128)):
    for c in cands:
        if n % c == 0:
            return c
    return n


def _pick_rows(r, cap=512, mult=16):
    return max(t for t in range(mult, cap + 1, mult) if r % t == 0)


def _dot(a, b, dn):
    return lax.dot_general(a.astype(BF16), b.astype(BF16), dn, preferred_element_type=F32)


def _sigmoid(x):
    return 0.5 + 0.5 * jnp.tanh(0.5 * x)


def _gelu(x):
    c = math.sqrt(2.0 / math.pi)
    t = jnp.tanh(c * (x + 0.044715 * x * x * x))
    return 0.5 * x * (1.0 + t)


def _gelu_grad(x):
    c = math.sqrt(2.0 / math.pi)
    t = jnp.tanh(c * (x + 0.044715 * x * x * x))
    return 0.5 * (1.0 + t) + 0.5 * x * (1.0 - t * t) * c * (1.0 + 3.0 * 0.044715 * x * x)


def _gelu_both(x):
    c = math.sqrt(2.0 / math.pi)
    t = jnp.tanh(c * (x + 0.044715 * x * x * x))
    return 0.5 * x * (1.0 + t), 0.5 * (1.0 + t) + 0.5 * x * (1.0 - t * t) * c * (1.0 + 3.0 * 0.044715 * x * x)


def _silu_both(z):
    s = _sigmoid(z)
    return z * s, s * (1.0 + z * (1.0 - s))


def _silu(z):
    return z * _sigmoid(z)


def matmul(a, b, mode, name, out_dtype=F32, add=None):
    if mode == "nn":
        (m, k), n = a.shape, b.shape[1]
    elif mode == "nt":
        (m, k), n = a.shape, b.shape[0]
    else:
        (k, m), n = a.shape, b.shape[1]
    tm = _pick(m, [t for t in (2048, 1024, 512, 384, 256, 128) if t * k * a.dtype.itemsize <= MM_BLOCK_BYTES])
    tn = _pick(n, [t for t in (512, 384, 256, 128) if t * k * b.dtype.itemsize <= MM_BLOCK_BYTES])
    dn = {"nn": DN_NN, "nt": DN_NT, "tn": DN_TN}[mode]

    def body(*refs):
        if add is None:
            a_ref, b_ref, o_ref = refs
        else:
            a_ref, b_ref, add_ref, o_ref = refs
        r = _dot(a_ref[...], b_ref[...], dn)
        if add is not None:
            r = r + add_ref[...].astype(F32)
        o_ref[...] = r.astype(out_dtype)

    a_spec = pl.BlockSpec((k, tm), lambda i, j: (0, i)) if mode == "tn" else pl.BlockSpec((tm, k), lambda i, j: (i, 0))
    b_spec = pl.BlockSpec((tn, k), lambda i, j: (j, 0)) if mode == "nt" else pl.BlockSpec((k, tn), lambda i, j: (0, j))
    o_spec = pl.BlockSpec((tm, tn), lambda i, j: (i, j))
    in_specs = [a_spec, b_spec] + ([o_spec] if add is not None else [])
    args = (a, b) + ((add,) if add is not None else ())
    return pl.pallas_call(
        body, name=name, grid=(m // tm, n // tn), in_specs=in_specs, out_specs=o_spec,
        out_shape=jax.ShapeDtypeStruct((m, n), out_dtype),
        compiler_params=_cparams(("parallel", "arbitrary")))(*args)


def _tile_rows(r, r0, cands=(512, 384, 256, 128)):
    return next(t for t in cands if r % t == 0 and r0 % t == 0)


def matmul_tn_packed(a, b, buf, r0, col_sharded, name):
    k, m = a.shape
    n = b.shape[1]
    if col_sharded:
        chunks = n // N_CHIPS // CHUNK_W
        tm = _tile_rows(m, r0, (1024, 512, 384, 256, 128))
        o_map = lambda i, j: (j // chunks, (r0 + (j % chunks) * m) // tm + i, 0)
    else:
        rs = m // N_CHIPS
        tm = _tile_rows(rs, r0)
        per = rs // tm
        o_map = lambda i, j: (i // per, (r0 + j * rs) // tm + i % per, 0)

    def body(a_ref, b_ref, buf_ref, o_ref):
        o_ref[0] = _dot(a_ref[...], b_ref[...], DN_TN)

    return pl.pallas_call(
        body, name=name, grid=(m // tm, n // CHUNK_W),
        in_specs=[pl.BlockSpec((k, tm), lambda i, j: (0, i)), pl.BlockSpec((k, CHUNK_W), lambda i, j: (0, j)),
                  pl.BlockSpec(memory_space=pl.ANY)],
        out_specs=pl.BlockSpec((1, tm, CHUNK_W), o_map), out_shape=jax.ShapeDtypeStruct(buf.shape, buf.dtype),
        input_output_aliases={2: 0}, compiler_params=_cparams(("parallel", "arbitrary")))(a, b, buf)


def _rows(tl, w, col=0):
    return pl.BlockSpec((tl, w), lambda i: (i, col))


def _full(shape):
    nd = len(shape)
    return pl.BlockSpec(tuple(shape), lambda i: (0,) * nd)


def _rowcall(body, name, n_steps, in_specs, out_specs, out_shape, scratch=()):
    return pl.pallas_call(
        body, name=name, grid=(n_steps,), in_specs=in_specs, out_specs=out_specs, out_shape=out_shape,
        scratch_shapes=list(scratch), compiler_params=_cparams(("arbitrary",)))


def _acc(ref, val, i):
    @pl.when(i == 0)
    def _():
        ref[...] = val

    @pl.when(i != 0)
    def _():
        ref[...] += val


def rms_fwd(h, g, name):
    l, d = h.shape
    tl = ROW_TILE_NARROW

    def body(h_ref, g_ref, o_ref):
        x = h_ref[...]
        r = lax.rsqrt(jnp.mean(x * x, axis=-1, keepdims=True) + NORM_EPS)
        o_ref[...] = (x * r * g_ref[...]).astype(BF16)

    return _rowcall(body, name, l // tl, [_rows(tl, d), _full((1, d))], _rows(tl, d),
                    jax.ShapeDtypeStruct((l, d), BF16))(h, g.reshape(1, d))


def rms_bwd(h, g, dhn, dh_in, name):
    l, d = h.shape
    tl = ROW_TILE_NARROW

    def body(h_ref, g_ref, dhn_ref, dhi_ref, dh_ref, dg_ref):
        i = pl.program_id(0)
        x = h_ref[...]
        r = lax.rsqrt(jnp.mean(x * x, axis=-1, keepdims=True) + NORM_EPS)
        xhat = x * r
        dy = dhn_ref[...]
        dxh = dy * g_ref[...]
        dx = r * (dxh - xhat * jnp.mean(dxh * xhat, axis=-1, keepdims=True))
        dh_ref[...] = dhi_ref[...] + dx
        _acc(dg_ref, jnp.sum(dy * xhat, axis=0, keepdims=True), i)

    return _rowcall(body, name, l // tl, [_rows(tl, d), _full((1, d)), _rows(tl, d), _rows(tl, d)],
                    [_rows(tl, d), _full((1, d))],
                    [jax.ShapeDtypeStruct((l, d), F32), jax.ShapeDtypeStruct((1, d), F32)])(h, g.reshape(1, d), dhn, dh_in)


def loss_head(h, g, target):
    l, d = h.shape
    tl = ROW_TILE_NARROW

    def body(h_ref, g_ref, t_ref, loss_ref, dh_ref, dg_ref):
        i = pl.program_id(0)
        x = h_ref[...]
        gg = g_ref[...]
        r = lax.rsqrt(jnp.mean(x * x, axis=-1, keepdims=True) + NORM_EPS)
        xhat = x * r
        err = xhat * gg - t_ref[...]
        part = 0.5 * jnp.sum(jnp.mean(err * err, axis=-1, keepdims=True), axis=0, keepdims=True)
        _acc(loss_ref, part, i)
        dy = err * (1.0 / d)
        dxh = dy * gg
        dh_ref[...] = r * (dxh - xhat * jnp.mean(dxh * xhat, axis=-1, keepdims=True))
        _acc(dg_ref, jnp.sum(dy * xhat, axis=0, keepdims=True), i)

    return _rowcall(body, "loss_head", l // tl, [_rows(tl, d), _full((1, d)), _rows(tl, d)],
                    [_full((1, 1)), _rows(tl, d), _full((1, d))],
                    [jax.ShapeDtypeStruct((1, 1), F32), jax.ShapeDtypeStruct((l, d), F32),
                     jax.ShapeDtypeStruct((1, d), F32)])(h, g.reshape(1, d), target)


def _gmlp_common(a_ref, lng_ref, lnb_ref):
    di = lng_ref.shape[1]
    u_pre = a_ref[:, 0:di]
    v_pre = a_ref[:, di:2 * di]
    z = a_ref[:, 2 * di:3 * di]
    vg = _gelu(v_pre)
    mu = jnp.mean(vg, axis=-1, keepdims=True)
    xc = vg - mu
    rstd = lax.rsqrt(jnp.mean(xc * xc, axis=-1, keepdims=True) + NORM_EPS)
    vhat = xc * rstd
    vn = vhat * lng_ref[...] + lnb_ref[...]
    return u_pre, v_pre, z, vhat, rstd, vn


def _tril(w):
    r = lax.broadcasted_iota(jnp.int32, w.shape, 0)
    c = lax.broadcasted_iota(jnp.int32, w.shape, 1)
    return jnp.where(c <= r, w, 0.0)


def gmlp_gate_fwd(a, ln_g, ln_b, w_s, b_s, name):
    l, w3 = a.shape
    di = w3 // 3
    dg = di // GMLP_GROUPS
    tl = GMLP_CHUNK

    def body(a_ref, lng_ref, lnb_ref, ws_ref, bs_ref, m_ref):
        u_pre, _, z, _, _, vn = _gmlp_common(a_ref, lng_ref, lnb_ref)
        gate = _gelu(u_pre) * _silu(z)
        for g in range(GMLP_GROUPS):
            sl = slice(g * dg, (g + 1) * dg)
            s = _dot(_tril(ws_ref[g]), vn[:, sl], DN_NN) + bs_ref[g]
            m_ref[:, sl] = (gate[:, sl] * s).astype(BF16)

    return _rowcall(body, name, l // tl,
                    [_rows(tl, w3), _full((1, di)), _full((1, di)), _full(w_s.shape), _full((GMLP_GROUPS, tl, 1))],
                    _rows(tl, di), jax.ShapeDtypeStruct((l, di), BF16))(
        a, ln_g.reshape(1, di), ln_b.reshape(1, di), w_s, b_s.reshape(GMLP_GROUPS, tl, 1))


def gmlp_gate_bwd(a, dm, ln_g, ln_b, w_s, b_s, name):
    l, w3 = a.shape
    di = w3 // 3
    dg = di // GMLP_GROUPS
    tl = GMLP_CHUNK

    def body(a_ref, dm_ref, lng_ref, lnb_ref, ws_ref, bs_ref, da_ref, dlg_ref, dlb_ref, dws_ref, dbs_ref,
             dvn_ref, vh_ref, gv_ref):
        i = pl.program_id(0)
        vg, gv = _gelu_both(a_ref[:, di:2 * di])
        gv_ref[...] = gv
        xc = vg - jnp.mean(vg, axis=-1, keepdims=True)
        rstd = lax.rsqrt(jnp.mean(xc * xc, axis=-1, keepdims=True) + NORM_EPS)
        vh_ref[...] = xc * rstd
        for g in range(GMLP_GROUPS):
            sl = slice(g * dg, (g + 1) * dg)
            wt = _tril(ws_ref[g])
            vn_g = vh_ref[:, sl] * lng_ref[:, sl] + lnb_ref[:, sl]
            s = _dot(wt, vn_g, DN_NN) + bs_ref[g]
            dmg = dm_ref[:, sl]
            u, gu = _gelu_both(a_ref[:, sl])
            sz, gz = _silu_both(a_ref[:, 2 * di + g * dg:2 * di + (g + 1) * dg])
            ds = dmg * u * sz
            da_ref[:, sl] = (dmg * s * sz * gu).astype(BF16)
            da_ref[:, 2 * di + g * dg:2 * di + (g + 1) * dg] = (dmg * u * s * gz).astype(BF16)
            dvn_ref[:, sl] = _dot(wt, ds, DN_TN)
            dw = _tril(_dot(ds, vn_g, DN_NT))
            db = jnp.sum(ds, axis=1, keepdims=True)

            @pl.when(i == 0)
            def _():
                dws_ref[g] = dw
                dbs_ref[g] = db

            @pl.when(i != 0)
            def _():
                dws_ref[g] += dw
                dbs_ref[g] += db

        dvn = dvn_ref[...]
        vhat = vh_ref[...]
        dxh = dvn * lng_ref[...]
        dvg = rstd * (dxh - jnp.mean(dxh, axis=-1, keepdims=True) - vhat * jnp.mean(dxh * vhat, axis=-1, keepdims=True))
        da_ref[:, di:2 * di] = (dvg * gv_ref[...]).astype(BF16)
        _acc(dlg_ref, jnp.sum(dvn * vhat, axis=0, keepdims=True), i)
        _acc(dlb_ref, jnp.sum(dvn, axis=0, keepdims=True), i)

    outs = _rowcall(
        body, name, l // tl,
        [_rows(tl, w3), _rows(tl, di), _full((1, di)), _full((1, di)), _full(w_s.shape), _full((GMLP_GROUPS, tl, 1))],
        [_rows(tl, w3), _full((1, di)), _full((1, di)), _full(w_s.shape), _full((GMLP_GROUPS, tl, 1))],
        [jax.ShapeDtypeStruct((l, w3), BF16), jax.ShapeDtypeStruct((1, di), F32), jax.ShapeDtypeStruct((1, di), F32),
         jax.ShapeDtypeStruct(w_s.shape, F32), jax.ShapeDtypeStruct((GMLP_GROUPS, tl, 1), F32)],
        scratch=[pltpu.VMEM((tl, di), F32)] * 3)(
        a, dm, ln_g.reshape(1, di), ln_b.reshape(1, di), w_s, b_s.reshape(GMLP_GROUPS, tl, 1))
    return outs


def gmlp_layer_fwd(h, p, wf, tag):
    hn = rms_fwd(h, p["norm_g"], tag + "_rms")
    a = matmul(hn, wf["w_in"], "nn", tag + "_mm_in")
    m = gmlp_gate_fwd(a, p["ln_g"], p["ln_b"], p["w_s"], p["b_s"], tag + "_gate")
    h_out = matmul(m, wf["w_out"], "nn", tag + "_mm_out", add=h)
    return h_out, (h, hn, a, m)


def gmlp_layer_bwd(dh_out, saved, p, wf, tag, sink):
    h, hn, a, m = saved
    dm = matmul(dh_out, wf["w_out"], "nt", tag + "_mm_dm")
    sink.mm("w_out", m, dh_out, tag + "_mm_gwout")
    da, dlg, dlb, dws, dbs = gmlp_gate_bwd(a, dm, p["ln_g"], p["ln_b"], p["w_s"], p["b_s"], tag + "_gate_bwd")
    dhn = matmul(da, wf["w_in"], "nt", tag + "_mm_dhn")
    sink.mm("w_in", hn, da, tag + "_mm_gwin")
    zero = sink.send()
    dh, dng = rms_bwd(h, p["norm_g"] + zero, dhn, dh_out, tag + "_rms_bwd")
    grads = {"norm_g": dng.reshape(-1), "ln_g": dlg.reshape(-1), "ln_b": dlb.reshape(-1),
             "w_s": dws, "b_s": dbs.reshape(GMLP_GROUPS, GMLP_CHUNK)}
    return dh, grads


def _cmul(ar, ai, br, bi):
    return ar * br - ai * bi, ar * bi + ai * br


S5_PG = 16


def _gblock(tail):
    return pl.BlockSpec((S5_PG,) + tuple(tail), lambda i: (i, 0, 0))


def s5_params_fwd(a_re, a_im, log_step, b_re, b_im):
    g, p, hh = b_re.shape

    def body(ar_ref, ai_ref, ls_ref, br_ref, bi_ref, lr_ref, li_ref, bbr_ref, bbi_ref):
        ar, ai = ar_ref[...], ai_ref[...]
        step = jnp.exp(ls_ref[...])
        mag = jnp.exp(ar * step)
        lr, li = mag * jnp.cos(ai * step), mag * jnp.sin(ai * step)
        den = 1.0 / (ar * ar + ai * ai)
        fr, fi = _cmul(lr - 1.0, li, ar * den, -ai * den)
        lr_ref[...] = lr
        li_ref[...] = li
        bbr, bbi = _cmul(fr, fi, br_ref[...], bi_ref[...])
        bbr_ref[...] = bbr
        bbi_ref[...] = bbi

    s1 = jax.ShapeDtypeStruct((g, p, 1), F32)
    s3 = jax.ShapeDtypeStruct((g, p, hh), F32)
    b1, b0, b3 = _gblock((p, 1)), _gblock((1, 1)), _gblock((p, hh))
    return pl.pallas_call(body, name="s5_params_fwd", grid=(g // S5_PG,), in_specs=[b1, b1, b0, b3, b3],
                          out_specs=[b1, b1, b3, b3], out_shape=[s1, s1, s3, s3],
                          compiler_params=_cparams(("parallel",)))(
        a_re.reshape(g, p, 1), a_im.reshape(g, p, 1), log_step.reshape(g, 1, 1), b_re, b_im)


def s5_params_bwd(a_re, a_im, log_step, b_re, b_im, dl_re, dl_im, dbb_re, dbb_im):
    g, p, hh = b_re.shape

    def body(ar_ref, ai_ref, ls_ref, br_ref, bi_ref, dlr_ref, dli_ref, dbr_ref, dbi_ref,
             gar_ref, gai_ref, gls_ref, gbr_ref, gbi_ref):
        ar, ai = ar_ref[...], ai_ref[...]
        step = jnp.exp(ls_ref[...])
        mag = jnp.exp(ar * step)
        lr, li = mag * jnp.cos(ai * step), mag * jnp.sin(ai * step)
        den = 1.0 / (ar * ar + ai * ai)
        ir, ii = ar * den, -ai * den
        fr, fi = _cmul(lr - 1.0, li, ir, ii)
        br, bi = br_ref[...], bi_ref[...]
        dbr, dbi = dbr_ref[...], dbi_ref[...]
        gbr, gbi = _cmul(fr, -fi, dbr, dbi)
        gbr_ref[...] = gbr
        gbi_ref[...] = gbi
        pr, pi = _cmul(br, -bi, dbr, dbi)
        gfr = jnp.sum(pr, axis=-1, keepdims=True)
        gfi = jnp.sum(pi, axis=-1, keepdims=True)
        t_r, t_i = _cmul(ir, -ii, gfr, gfi)
        glr, gli = dlr_ref[...] + t_r, dli_ref[...] + t_i
        c1r, c1i = _cmul(step * lr, -step * li, glr, gli)
        qr, qi = _cmul(fr, fi, ir, ii)
        c2r, c2i = _cmul(-qr, qi, gfr, gfi)
        gar_ref[...] = c1r + c2r
        gai_ref[...] = c1i + c2i
        wr, wi = _cmul(ar, ai, lr, li)
        sr, _ = _cmul(wr, -wi, glr, gli)
        gls_ref[...] = jnp.sum(sr, axis=1, keepdims=True) * step

    s1 = jax.ShapeDtypeStruct((g, p, 1), F32)
    s3 = jax.ShapeDtypeStruct((g, p, hh), F32)
    b1, b0, b3 = _gblock((p, 1)), _gblock((1, 1)), _gblock((p, hh))
    return pl.pallas_call(body, name="s5_params_bwd", grid=(g // S5_PG,),
                          in_specs=[b1, b1, b0, b3, b3, b1, b1, b3, b3], out_specs=[b1, b1, b0, b3, b3],
                          out_shape=[s1, s1, jax.ShapeDtypeStruct((g, 1, 1), F32), s3, s3],
                          compiler_params=_cparams(("parallel",)))(
        a_re.reshape(g, p, 1), a_im.reshape(g, p, 1), log_step.reshape(g, 1, 1), b_re, b_im,
        dl_re, dl_im, dbb_re, dbb_im)


def _blockdiag(t):
    sb, n, r, c = t.shape
    eye = jnp.eye(n, dtype=bool)[None, :, None, :, None]
    full = jnp.where(eye, t[:, :, :, None, :], jnp.zeros((), t.dtype))
    return full.reshape(sb, n * r, n * c)


def _blockdiag_extract(m, r, c):
    sb = m.shape[0]
    n = m.shape[1] // r
    m5 = m.reshape(sb, n, r, n, c)
    return jnp.stack([m5[:, i, :, i, :] for i in range(n)], axis=1)


S5_TB = 256
S5_UNROLL = 8


def _lam_power(pr, pi, n):
    for _ in range(int(math.log2(n))):
        pr, pi = _cmul(pr, pi, pr, pi)
    return pr, pi


def _segment_entries(er, ei, pr, pi, reverse):
    seg, ns = er.shape
    row = lax.broadcasted_iota(jnp.int32, (seg, ns), 0)
    cr = jnp.zeros((seg, ns), F32)
    ci = jnp.zeros((seg, ns), F32)
    cur_r = jnp.zeros((1, ns), F32)
    cur_i = jnp.zeros((1, ns), F32)
    for s in (range(seg - 2, -1, -1) if reverse else range(1, seg)):
        src = s + 1 if reverse else s - 1
        mr, mi = _cmul(pr, pi, cur_r, cur_i)
        cur_r = jnp.sum(jnp.where(row == src, er, 0.0), axis=0, keepdims=True) + mr
        cur_i = jnp.sum(jnp.where(row == src, ei, 0.0), axis=0, keepdims=True) + mi
        cr = jnp.where(row == s, cur_r, cr)
        ci = jnp.where(row == s, cur_i, ci)
    return cr, ci


def s5_scan_fused_fwd(a_p, lam_re, lam_im, wb_re, wb_im, wc_re, wc_im, d_skip, name):
    l = a_p.shape[0]
    di = d_skip.shape[1]
    rows = S5_SEG * S5_TB
    nb = l // rows
    ns = wb_re.shape[2]

    def body(u_ref, lr_ref, li_ref, wbr_ref, wbi_ref, wcr_ref, wci_ref, ds_ref, y_ref, yg_ref, ckr_ref, cki_ref,
             bur, bui):
        lr = jnp.broadcast_to(lr_ref[0], (S5_SEG, ns))
        li = jnp.broadcast_to(li_ref[0], (S5_SEG, ns))

        def scan_block(b, carry, keep):
            def step(t, c):
                xr, xi = c
                sl = pl.ds(pl.multiple_of(b * rows + t * S5_SEG, S5_SEG), S5_SEG)
                nr = lr * xr - li * xi + bur[sl, :]
                ni = lr * xi + li * xr + bui[sl, :]
                if keep:
                    bur[sl, :] = nr
                    bui[sl, :] = ni
                return nr, ni

            return lax.fori_loop(0, S5_TB, step, carry, unroll=S5_UNROLL)

        def project(b, carry):
            rs = pl.ds(pl.multiple_of(b * rows, rows), rows)
            u = u_ref[rs, :]
            bur[rs, :] = _dot(u, wbr_ref[0], DN_NN)
            bui[rs, :] = _dot(u, wbi_ref[0], DN_NN)
            return scan_block(b, carry, False)

        zero = jnp.zeros((S5_SEG, ns), F32)
        er, ei = lax.fori_loop(0, nb, project, (zero, zero))
        pr, pi = _lam_power(lr_ref[0], li_ref[0], l // S5_SEG)
        entry = _segment_entries(er, ei, pr, pi, False)

        def emit(b, carry):
            ckr_ref[0, b] = carry[0]
            cki_ref[0, b] = carry[1]
            carry = scan_block(b, carry, True)
            rs = pl.ds(pl.multiple_of(b * rows, rows), rows)
            y = (_dot(bur[rs, :], wcr_ref[0], DN_NN) - _dot(bui[rs, :], wci_ref[0], DN_NN)
                 + ds_ref[...] * u_ref[rs, :])
            y_ref[rs, :] = y
            yg_ref[rs, :] = _gelu(y).astype(BF16)
            return carry

        lax.fori_loop(0, nb, emit, entry)

    sb3 = lambda s: (s, 0, 0)
    st = jax.ShapeDtypeStruct
    return pl.pallas_call(
        body, name=name, grid=(S5_SB,),
        in_specs=[pl.BlockSpec((l, LANES), lambda s: (0, s)),
                  pl.BlockSpec((1, 1, ns), sb3), pl.BlockSpec((1, 1, ns), sb3),
                  pl.BlockSpec((1, LANES, ns), sb3), pl.BlockSpec((1, LANES, ns), sb3),
                  pl.BlockSpec((1, ns, LANES), sb3), pl.BlockSpec((1, ns, LANES), sb3),
                  pl.BlockSpec((1, LANES), lambda s: (0, s))],
        out_specs=[pl.BlockSpec((l, LANES), lambda s: (0, s)), pl.BlockSpec((l, LANES), lambda s: (0, s)),
                   pl.BlockSpec((1, nb, S5_SEG, ns), lambda s: (s, 0, 0, 0)),
                   pl.BlockSpec((1, nb, S5_SEG, ns), lambda s: (s, 0, 0, 0))],
        out_shape=[st((l, di), F32), st((l, di), BF16),
                   st((S5_SB, nb, S5_SEG, ns), F32), st((S5_SB, nb, S5_SEG, ns), F32)],
        scratch_shapes=[pltpu.VMEM((l, ns), F32), pltpu.VMEM((l, ns), F32)],
        compiler_params=_cparams(("parallel",)))(a_p, lam_re, lam_im, wb_re, wb_im, wc_re, wc_im, d_skip)


def s5_scan_fused_bwd(a_p, dy, lam_re, lam_im, wb_re, wb_im, wc_re, wc_im, d_skip, ck_re, ck_im, name):
    l = a_p.shape[0]
    di = d_skip.shape[1]
    rows = S5_SEG * S5_TB
    nb = l // rows
    ns = wb_re.shape[2]

    def body(u_ref, dy_ref, lr_ref, li_ref, wbr_ref, wbi_ref, wcr_ref, wci_ref, ds_ref, ckr_ref, cki_ref,
             du_ref, dwbr_ref, dwbi_ref, dwcr_ref, dwci_ref, dds_ref, dlr_ref, dli_ref, gr, gi, xr_b, xi_b):
        lr = jnp.broadcast_to(lr_ref[0], (S5_SEG, ns))
        li = jnp.broadcast_to(li_ref[0], (S5_SEG, ns))

        def back_project(k, carry):
            b = nb - 1 - k
            rs = pl.ds(pl.multiple_of(b * rows, rows), rows)
            dyv = dy_ref[rs, :]
            gr[rs, :] = _dot(dyv, wcr_ref[0], DN_NT)
            gi[rs, :] = -_dot(dyv, wci_ref[0], DN_NT)

            def step(kk, c):
                ar, ai = c
                sl = pl.ds(pl.multiple_of(b * rows + (S5_TB - 1 - kk) * S5_SEG, S5_SEG), S5_SEG)
                return gr[sl, :] + lr * ar + li * ai, gi[sl, :] + lr * ai - li * ar

            return lax.fori_loop(0, S5_TB, step, carry, unroll=S5_UNROLL)

        zero = jnp.zeros((S5_SEG, ns), F32)
        er, ei = lax.fori_loop(0, nb, back_project, (zero, zero))
        pr, pi = _lam_power(lr_ref[0], -li_ref[0], l // S5_SEG)
        a0r, a0i = _segment_entries(er, ei, pr, pi, True)

        dwbr_ref[...] = jnp.zeros_like(dwbr_ref)
        dwbi_ref[...] = jnp.zeros_like(dwbi_ref)
        dwcr_ref[...] = jnp.zeros_like(dwcr_ref)
        dwci_ref[...] = jnp.zeros_like(dwci_ref)
        dds_ref[...] = jnp.zeros_like(dds_ref)

        def block(k, carry):
            b = nb - 1 - k
            rs = pl.ds(pl.multiple_of(b * rows, rows), rows)
            u = u_ref[rs, :]
            dyv = dy_ref[rs, :]
            body_rows = pl.ds(S5_SEG, rows)
            x0r, x0i = ckr_ref[0, b], cki_ref[0, b]
            xr_b[0:S5_SEG, :] = x0r
            xi_b[0:S5_SEG, :] = x0i
            xr_b[body_rows, :] = _dot(u, wbr_ref[0], DN_NN)
            xi_b[body_rows, :] = _dot(u, wbi_ref[0], DN_NN)

            def fstep(t, c):
                xr, xi = c
                sl = pl.ds(pl.multiple_of((t + 1) * S5_SEG, S5_SEG), S5_SEG)
                nr = lr * xr - li * xi + xr_b[sl, :]
                ni = lr * xi + li * xr + xi_b[sl, :]
                xr_b[sl, :] = nr
                xi_b[sl, :] = ni
                return nr, ni

            lax.fori_loop(0, S5_TB, fstep, (x0r, x0i), unroll=S5_UNROLL)
            dwcr_ref[0] += _dot(xr_b[body_rows, :], dyv, DN_TN)
            dwci_ref[0] -= _dot(xi_b[body_rows, :], dyv, DN_TN)

            def bstep(kk, c):
                ar, ai = c
                sl = pl.ds(pl.multiple_of(b * rows + (S5_TB - 1 - kk) * S5_SEG, S5_SEG), S5_SEG)
                nr = gr[sl, :] + lr * ar + li * ai
                ni = gi[sl, :] + lr * ai - li * ar
                gr[sl, :] = nr
                gi[sl, :] = ni
                return nr, ni

            ar, ai = lax.fori_loop(0, S5_TB, bstep, carry[:2], unroll=S5_UNROLL)
            a_r, a_i = gr[rs, :], gi[rs, :]
            p_r, p_i = xr_b[0:rows, :], xi_b[0:rows, :]
            per_seg = lambda v: jnp.sum(v.reshape(S5_TB, S5_SEG, ns), axis=0)
            carry = (ar, ai, carry[2] + per_seg(a_r * p_r + a_i * p_i), carry[3] + per_seg(a_i * p_r - a_r * p_i))
            du_ref[rs, :] = (_dot(a_r, wbr_ref[0], DN_NT) + _dot(a_i, wbi_ref[0], DN_NT) + ds_ref[...] * dyv).astype(BF16)
            dwbr_ref[0] += _dot(u, a_r, DN_TN)
            dwbi_ref[0] += _dot(u, a_i, DN_TN)
            dds_ref[...] += jnp.sum(dyv * u, axis=0, keepdims=True)
            return carry

        _, _, dlr, dli = lax.fori_loop(0, nb, block, (a0r, a0i, zero, zero))
        dlr_ref[0] = dlr
        dli_ref[0] = dli

    sb3 = lambda s: (s, 0, 0)
    seq = pl.BlockSpec((l, LANES), lambda s: (0, s))
    ck = pl.BlockSpec((1, nb, S5_SEG, ns), lambda s: (s, 0, 0, 0))
    st = jax.ShapeDtypeStruct
    return pl.pallas_call(
        body, name=name, grid=(S5_SB,),
        in_specs=[seq, seq, pl.BlockSpec((1, 1, ns), sb3), pl.BlockSpec((1, 1, ns), sb3),
                  pl.BlockSpec((1, LANES, ns), sb3), pl.BlockSpec((1, LANES, ns), sb3),
                  pl.BlockSpec((1, ns, LANES), sb3), pl.BlockSpec((1, ns, LANES), sb3),
                  pl.BlockSpec((1, LANES), lambda s: (0, s)), ck, ck],
        out_specs=[seq, pl.BlockSpec((1, LANES, ns), sb3), pl.BlockSpec((1, LANES, ns), sb3),
                   pl.BlockSpec((1, ns, LANES), sb3), pl.BlockSpec((1, ns, LANES), sb3),
                   pl.BlockSpec((1, LANES), lambda s: (0, s)),
                   pl.BlockSpec((1, S5_SEG, ns), sb3), pl.BlockSpec((1, S5_SEG, ns), sb3)],
        out_shape=[st((l, di), BF16), st((S5_SB, LANES, ns), F32), st((S5_SB, LANES, ns), F32),
                   st((S5_SB, ns, LANES), F32), st((S5_SB, ns, LANES), F32), st((1, di), F32),
                   st((S5_SB, S5_SEG, ns), F32), st((S5_SB, S5_SEG, ns), F32)],
        scratch_shapes=[pltpu.VMEM((l, ns), F32), pltpu.VMEM((l, ns), F32),
                        pltpu.VMEM((rows + S5_SEG, ns), F32), pltpu.VMEM((rows + S5_SEG, ns), F32)],
        compiler_params=_cparams(("parallel",)))(
        a_p, dy, lam_re, lam_im, wb_re, wb_im, wc_re, wc_im, d_skip, ck_re, ck_im)


def s5_gate_fwd(y, t, b_glu, a_p, name):
    l, d = y.shape
    tl = ROW_TILE

    def body(y_ref, t_ref, b_ref, z_ref, m_ref):
        yg = _gelu(y_ref[...])
        m_ref[...] = (yg * _sigmoid(t_ref[...] + b_ref[...]) * _silu(z_ref[...])).astype(BF16)

    return _rowcall(body, name, l // tl, [_rows(tl, d), _rows(tl, d), _full((1, d)), _rows(tl, d, 1)], _rows(tl, d),
                    jax.ShapeDtypeStruct((l, d), BF16))(y, t, b_glu.reshape(1, d), a_p)


def s5_gate_bwd(dm, y, t, b_glu, a_p, name):
    l, d = y.shape
    tl = ROW_TILE

    def body(dm_ref, y_ref, t_ref, b_ref, z_ref, dt_ref, dyg_ref, dz_ref, db_ref):
        i = pl.program_id(0)
        dmv = dm_ref[...]
        z = z_ref[...]
        yg = _gelu(y_ref[...])
        sg = _sigmoid(t_ref[...] + b_ref[...])
        y2 = yg * sg
        sz, gz = _silu_both(z)
        dy2 = dmv * sz
        dz_ref[...] = (dmv * y2 * gz).astype(BF16)
        dyg_ref[...] = dy2 * sg
        dt = dy2 * yg * sg * (1.0 - sg)
        dt_ref[...] = dt.astype(BF16)
        _acc(db_ref, jnp.sum(dt, axis=0, keepdims=True), i)

    st = jax.ShapeDtypeStruct
    return _rowcall(body, name, l // tl, [_rows(tl, d), _rows(tl, d), _rows(tl, d), _full((1, d)), _rows(tl, d, 1)],
                    [_rows(tl, d), _rows(tl, d), _rows(tl, d), _full((1, d))],
                    [st((l, d), BF16), st((l, d), F32), st((l, d), BF16), st((1, d), F32)])(
        dm, y, t, b_glu.reshape(1, d), a_p)


def s5_act_bwd(y, dyg_a, dyg_b, name):
    l, d = y.shape
    tl = ROW_TILE

    def body(y_ref, a_ref, b_ref, o_ref):
        o_ref[...] = (a_ref[...] + b_ref[...]) * _gelu_grad(y_ref[...])

    return _rowcall(body, name, l // tl, [_rows(tl, d)] * 3, _rows(tl, d), jax.ShapeDtypeStruct((l, d), F32))(y, dyg_a, dyg_b)


def _seg_perm(t):
    l, d = t.shape
    return t.reshape(S5_SEG, l // S5_SEG, d).transpose(1, 0, 2).reshape(l, d)


def _seg_unperm(t):
    l, d = t.shape
    return t.reshape(l // S5_SEG, S5_SEG, d).transpose(1, 0, 2).reshape(l, d)


def _s5_weights(p):
    lr, li, bbr, bbi = s5_params_fwd(p["a_re"], p["a_im"], p["log_step"], p["b_re"], p["b_im"])
    ns = 8 * S5_STATE
    lam_re = lr.reshape(S5_SB, 1, ns)
    lam_im = li.reshape(S5_SB, 1, ns)
    to_bd = lambda t: _blockdiag(t.reshape(S5_SB, 8, t.shape[1], t.shape[2]))
    wb_re = to_bd(bbr.transpose(0, 2, 1)).astype(BF16)
    wb_im = to_bd(bbi.transpose(0, 2, 1)).astype(BF16)
    wc_re = to_bd(p["c_re"].transpose(0, 2, 1)).astype(BF16)
    wc_im = to_bd(p["c_im"].transpose(0, 2, 1)).astype(BF16)
    return lam_re, lam_im, wb_re, wb_im, wc_re, wc_im


def s5_layer_fwd(h, p, wf, sw, tag):
    l = h.shape[0]
    di = p["d_skip"].shape[0]
    hn = rms_fwd(h, p["norm_g"], tag + "_rms")
    hn_p = _seg_perm(hn)
    a_p = matmul(hn_p, wf["w_in"], "nn", tag + "_mm_in")
    dsk = p["d_skip"].reshape(1, di)
    y, yg, ck_re, ck_im = s5_scan_fused_fwd(a_p, *sw, dsk, tag + "_scan")
    t = matmul(yg, wf["w_glu"], "nn", tag + "_mm_glu")
    m = s5_gate_fwd(y, t, p["b_glu"], a_p, tag + "_gate")
    out_p = matmul(m, wf["w_out"], "nn", tag + "_mm_out")
    h_out = residual_add(h, _seg_unperm(out_p), tag + "_res")
    return h_out, (h, hn_p, a_p, sw, ck_re, ck_im, y, yg, t, m)


def residual_add(h, y, name):
    l, d = h.shape
    tl = ROW_TILE_NARROW

    def body(h_ref, y_ref, o_ref):
        o_ref[...] = h_ref[...] + y_ref[...]

    return _rowcall(body, name, l // tl, [_rows(tl, d)] * 2, _rows(tl, d), jax.ShapeDtypeStruct((l, d), F32))(h, y)


def s5_layer_bwd(dh_out, saved, p, wf, tag, sink):
    h, hn_p, a_p, sw, ck_re, ck_im, y, yg, t, m = saved
    l = h.shape[0]
    di = p["d_skip"].shape[0]
    dsk = p["d_skip"].reshape(1, di)
    dout_p = _seg_perm(dh_out)
    dm = matmul(dout_p, wf["w_out"], "nt", tag + "_mm_dm")
    sink.mm("w_out", m, dout_p, tag + "_mm_gwout")
    dt, dyg_a, dz, db_glu = s5_gate_bwd(dm, y, t, p["b_glu"], a_p, tag + "_gate_bwd")
    dyg_b = matmul(dt, wf["w_glu"], "nt", tag + "_mm_dyg")
    sink.mm("w_glu", yg, dt, tag + "_mm_gwglu")
    dy = s5_act_bwd(y, dyg_a, dyg_b, tag + "_act_bwd")
    du, dwbr, dwbi, dwcr, dwci, dds, dlr, dli = s5_scan_fused_bwd(a_p, dy, *sw, dsk, ck_re, ck_im, tag + "_scanb")
    da = jnp.concatenate([du, dz], axis=1)
    dhn_p = matmul(da, wf["w_in"], "nt", tag + "_mm_dhn")
    sink.mm("w_in", hn_p, da, tag + "_mm_gwin")
    zero = sink.send()
    dh, dng = rms_bwd(h, p["norm_g"] + zero, _seg_unperm(dhn_p), dh_out, tag + "_rms_bwd")
    ex = lambda m_, r, c: _blockdiag_extract(m_, r, c).reshape(S5_GROUPS, r, c).transpose(0, 2, 1)
    dbb_re, dbb_im = ex(dwbr, S5_GROUP, S5_STATE), ex(dwbi, S5_GROUP, S5_STATE)
    g_c_re, g_c_im = ex(dwcr, S5_STATE, S5_GROUP), ex(dwci, S5_STATE, S5_GROUP)
    dl_re = lane_sum8(dlr).reshape(S5_GROUPS, S5_STATE, 1)
    dl_im = lane_sum8(dli).reshape(S5_GROUPS, S5_STATE, 1)
    gar, gai, gls, gbr, gbi = s5_params_bwd(p["a_re"], p["a_im"], p["log_step"], p["b_re"], p["b_im"],
                                            dl_re, dl_im, dbb_re, dbb_im)
    grads = {"norm_g": dng.reshape(-1), "a_re": gar.reshape(S5_GROUPS, S5_STATE),
             "a_im": gai.reshape(S5_GROUPS, S5_STATE), "log_step": gls.reshape(-1), "b_re": gbr, "b_im": gbi,
             "c_re": g_c_re, "c_im": g_c_im, "d_skip": dds.reshape(-1), "b_glu": db_glu.reshape(-1)}
    return dh, grads


def lane_sum8(t):
    sb, seg, ns = t.shape

    def body(t_ref, o_ref):
        o_ref[...] = jnp.sum(t_ref[...], axis=1, keepdims=True)

    return pl.pallas_call(body, name="s5_seg_sum", out_shape=jax.ShapeDtypeStruct((sb, 1, ns), F32))(t)


MLA_DI = MLA_HEADS * 128
MLA_CQ0 = MLA_DI
MLA_CKV0 = MLA_CQ0 + MLA_Q_RANK
MLA_KR0 = MLA_CKV0 + MLA_KV_RANK
MLA_AW = MLA_KR0 + LANES


def _rot_half(x):
    w = x.shape[-1]
    lane = lax.broadcasted_iota(jnp.int32, x.shape, x.ndim - 1)
    return jnp.where(lane % MLA_ROPE < MLA_ROPE // 2, pltpu.roll(x, w - MLA_ROPE // 2, x.ndim - 1),
                     pltpu.roll(x, MLA_ROPE // 2, x.ndim - 1))


def rope_tables(pos, zero):
    l = pos.shape[0]
    tl = ROW_TILE
    j = np.arange(LANES) % MLA_ROPE % (MLA_ROPE // 2)
    inv_freq = (ROPE_THETA ** (-(2.0 * j) / MLA_ROPE)).astype(np.float32).reshape(1, LANES)
    sign = np.where(np.arange(LANES) % MLA_ROPE < MLA_ROPE // 2, -1.0, 1.0).astype(np.float32).reshape(1, LANES)

    def body(p_ref, f_ref, s_ref, cos_ref, sin_ref):
        ang = p_ref[...].astype(F32) * f_ref[...]
        cos_ref[...] = jnp.cos(ang)
        sin_ref[...] = jnp.sin(ang) * s_ref[...]

    st = jax.ShapeDtypeStruct((l, LANES), F32)
    return _rowcall(body, "rope_tables", l // tl, [_rows(tl, 1), _full((1, LANES)), _full((1, LANES))],
                    [_rows(tl, LANES)] * 2, [st, st])(pos, jnp.asarray(inv_freq), jnp.asarray(sign) + zero)


def _rope(x, cos, sins):
    return x * cos + _rot_half(x) * sins


def _rope_t(dy, cos, sins):
    return dy * cos - sins * _rot_half(dy)


def _rmsn(x):
    r = lax.rsqrt(jnp.mean(x * x, axis=-1, keepdims=True) + NORM_EPS)
    return x * r, r


def mla_pre(a, q_g, kv_g, cos, sins, name):
    l = a.shape[0]
    tl = ROW_TILE

    def body(a_ref, qg_ref, kg_ref, cos_ref, sin_ref, cq_ref, ckv_ref, krs_ref):
        xq, _ = _rmsn(a_ref[:, MLA_CQ0:MLA_CKV0])
        cq_ref[...] = (xq * qg_ref[...]).astype(BF16)
        xk, _ = _rmsn(a_ref[:, MLA_CKV0:MLA_KR0])
        ckv_ref[...] = (xk * kg_ref[...]).astype(BF16)
        kr = a_ref[:, MLA_KR0:MLA_AW]
        kr2 = kr + pltpu.roll(kr, MLA_ROPE, 1)
        kr2 = _rope(kr2, cos_ref[...], sin_ref[...])
        lane = lax.broadcasted_iota(jnp.int32, kr2.shape, 1)
        krs_ref[0] = jnp.where(lane < MLA_ROPE, kr2, 0.0).astype(BF16)
        krs_ref[1] = jnp.where(lane >= MLA_ROPE, kr2, 0.0).astype(BF16)

    st = jax.ShapeDtypeStruct
    return _rowcall(body, name, l // tl,
                    [_rows(tl, MLA_AW), _full((1, MLA_Q_RANK)), _full((1, MLA_KV_RANK)), _rows(tl, LANES), _rows(tl, LANES)],
                    [_rows(tl, MLA_Q_RANK), _rows(tl, MLA_KV_RANK), pl.BlockSpec((2, tl, LANES), lambda i: (0, i, 0))],
                    [st((l, MLA_Q_RANK), BF16), st((l, MLA_KV_RANK), BF16), st((2, l, LANES), BF16)])(
        a, q_g.reshape(1, -1), kv_g.reshape(1, -1), cos, sins)


def mla_rope_q(qr, cos, sins, name):
    l, w = qr.shape
    tl = ROW_TILE

    def body(q_ref, cos_ref, sin_ref, o_ref):
        c, s = cos_ref[...], sin_ref[...]
        for p in range(w // LANES):
            sl = slice(p * LANES, (p + 1) * LANES)
            o_ref[:, sl] = _rope(q_ref[:, sl], c, s).astype(BF16)

    return _rowcall(body, name, l // tl, [_rows(tl, w), _rows(tl, LANES), _rows(tl, LANES)], _rows(tl, w),
                    jax.ShapeDtypeStruct((l, w), BF16))(qr, cos, sins)


ATT_OUT = 512
ATT_IN = 512
ATT_R = ATT_OUT // ATT_IN


def _scores(qn, qr, kn, kr, mask_off, transposed):
    q2 = jnp.concatenate([qn, qr], axis=1)
    k2 = jnp.concatenate([kn, kr], axis=1)
    s = (_dot(k2, q2, DN_NT) if transposed else _dot(q2, k2, DN_NT)) * MLA_SCALE
    if mask_off is None:
        return s
    r = lax.broadcasted_iota(jnp.int32, s.shape, 0)
    c = lax.broadcasted_iota(jnp.int32, s.shape, 1)
    return jnp.where((r <= c + mask_off) if transposed else (c + mask_off <= r), s, NEG_INF)


def _fold(x, op):
    out = x[:, :LANES]
    for t in range(1, x.shape[1] // LANES):
        out = op(out, x[:, t * LANES:(t + 1) * LANES])
    return out


def flash_fwd(qn, qr, kv, krs, name):
    l = qn.shape[0]
    nq = l // ATT_OUT

    def body(qn_ref, qr_ref, kv_ref, kr_ref, o_ref, lse_ref, s_buf):
        qi = pl.program_id(1)
        q_r = qr_ref[...]
        q_n = [qn_ref[:, hh * LANES:(hh + 1) * LANES] for hh in range(2)]

        def block_scores(j, mx, mask_off):
            sl = pl.ds(pl.multiple_of(j * ATT_IN, ATT_IN), ATT_IN)
            out = []
            for hh in range(2):
                s = _scores(q_n[hh], q_r, kv_ref[sl, 2 * hh * LANES:(2 * hh + 1) * LANES], kr_ref[hh, sl, :],
                            mask_off, False)
                s_buf[hh, j] = s
                out.append(jnp.maximum(mx[hh], _fold(s, jnp.maximum)))
            return tuple(out)

        ninf = jnp.full((ATT_OUT, LANES), NEG_INF, F32)
        mx = lax.fori_loop(0, ATT_R * qi, lambda j, c: block_scores(j, c, None), (ninf, ninf))
        for d in range(ATT_R):
            mx = block_scores(ATT_R * qi + d, mx, d * ATT_IN)
        m = [jnp.max(mx[hh], axis=-1, keepdims=True) for hh in range(2)]

        def block_pv(j, carry):
            sl = pl.ds(pl.multiple_of(j * ATT_IN, ATT_IN), ATT_IN)
            out = []
            for hh in range(2):
                ls, acc = carry[hh]
                p = jnp.exp(s_buf[hh, j] - m[hh])
                out.append((ls + _fold(p, jnp.add),
                            acc + _dot(p, kv_ref[sl, (2 * hh + 1) * LANES:(2 * hh + 2) * LANES], DN_NN)))
            return tuple(out)

        z = jnp.zeros((ATT_OUT, LANES), F32)
        res = lax.fori_loop(0, ATT_R * (qi + 1), block_pv, ((z, z), (z, z)))
        for hh in range(2):
            lsum = jnp.sum(res[hh][0], axis=-1, keepdims=True)
            o_ref[:, hh * LANES:(hh + 1) * LANES] = res[hh][1] / lsum
            lse_ref[hh] = m[hh] + jnp.log(lsum)

    st = jax.ShapeDtypeStruct
    return pl.pallas_call(
        body, name=name, grid=(MLA_HEADS // 2, nq),
        in_specs=[pl.BlockSpec((ATT_OUT, 2 * LANES), lambda p, i: (i, p)),
                  pl.BlockSpec((ATT_OUT, LANES), lambda p, i: (i, p)),
                  pl.BlockSpec((l, 4 * LANES), lambda p, i: (0, p)),
                  pl.BlockSpec((2, l, LANES), lambda p, i: (0, 0, 0))],
        out_specs=[pl.BlockSpec((ATT_OUT, 2 * LANES), lambda p, i: (i, p)),
                   pl.BlockSpec((2, ATT_OUT, 1), lambda p, i: (p, i, 0))],
        out_shape=[st((l, MLA_DI), F32), st((MLA_HEADS, l, 1), F32)],
        scratch_shapes=[pltpu.VMEM((2, l // ATT_IN, ATT_OUT, ATT_IN), F32)],
        compiler_params=_cparams(("parallel", "arbitrary")))(qn, qr, kv, krs)


def flash_dkv(qn, qr, kv, krs, do, lse_row, delta_row, name):
    l = qn.shape[0]
    nk = l // ATT_OUT
    nq = l // ATT_IN

    def body(qn_ref, qr_ref, do_ref, lse_ref, dl_ref, kv_ref, kr_ref, dkv_ref, dkr_ref):
        kj = pl.program_id(1)
        lane = lax.broadcasted_iota(jnp.int32, (ATT_OUT, LANES), 1)
        kn = [kv_ref[:, 2 * hh * LANES:(2 * hh + 1) * LANES] for hh in range(2)]
        v = [kv_ref[:, (2 * hh + 1) * LANES:(2 * hh + 2) * LANES] for hh in range(2)]

        def block(i, carry, mask_off):
            sl = pl.ds(pl.multiple_of(i * ATT_IN, ATT_IN), ATT_IN)
            q_r = qr_ref[sl, :]
            out = []
            for hh in range(2):
                dk2, dv = carry[hh]
                hs = slice(hh * LANES, (hh + 1) * LANES)
                q_n, d_o = qn_ref[sl, hs], do_ref[sl, hs]
                s = _scores(q_n, q_r, kn[hh], kr_ref[hh], mask_off, True)
                pt = jnp.exp(s - lse_ref[hh, i])
                dv = dv + _dot(pt, d_o, DN_NN)
                dpt = _dot(v[hh], d_o, DN_NT)
                dst = (pt * (dpt - dl_ref[hh, i]) * MLA_SCALE).astype(BF16)
                out.append((dk2 + _dot(dst, jnp.concatenate([q_n, q_r], axis=1), DN_NN), dv))
            return tuple(out)

        z = jnp.zeros((ATT_OUT, LANES), F32)
        z2 = jnp.zeros((ATT_OUT, 2 * LANES), F32)
        res = ((z2, z), (z2, z))
        for d in range(ATT_R):
            res = block(ATT_R * kj + d, res, d * ATT_IN)
        res = lax.fori_loop(ATT_R * (kj + 1), nq, lambda i, c: block(i, c, None), res)
        for hh in range(2):
            dkv_ref[:, 2 * hh * LANES:(2 * hh + 1) * LANES] = res[hh][0][:, :LANES].astype(BF16)
            dkv_ref[:, (2 * hh + 1) * LANES:(2 * hh + 2) * LANES] = res[hh][1].astype(BF16)
        dkr_ref[0] = jnp.where(lane < MLA_ROPE, res[0][0][:, LANES:], res[1][0][:, LANES:])

    st = jax.ShapeDtypeStruct
    return pl.pallas_call(
        body, name=name, grid=(MLA_HEADS // 2, nk),
        in_specs=[pl.BlockSpec((l, 2 * LANES), lambda p, j: (0, p)),
                  pl.BlockSpec((l, LANES), lambda p, j: (0, p)),
                  pl.BlockSpec((l, 2 * LANES), lambda p, j: (0, p)),
                  pl.BlockSpec((2, nq, 1, ATT_IN), lambda p, j: (p, 0, 0, 0)),
                  pl.BlockSpec((2, nq, 1, ATT_IN), lambda p, j: (p, 0, 0, 0)),
                  pl.BlockSpec((ATT_OUT, 4 * LANES), lambda p, j: (j, p)),
                  pl.BlockSpec((2, ATT_OUT, LANES), lambda p, j: (0, j, 0))],
        out_specs=[pl.BlockSpec((ATT_OUT, 4 * LANES), lambda p, j: (j, p)),
                   pl.BlockSpec((1, ATT_OUT, LANES), lambda p, j: (p, j, 0))],
        out_shape=[st((l, 2 * MLA_DI), BF16), st((MLA_HEADS // 2, l, LANES), F32)],
        compiler_params=_cparams(("parallel", "arbitrary")))(qn, qr, do, lse_row, delta_row, kv, krs)


def flash_dq(qn, qr, kv, krs, do, lse, delta, cos, sins, name):
    l = qn.shape[0]
    nq = l // ATT_OUT

    def body(qn_ref, qr_ref, do_ref, lse_ref, dl_ref, kv_ref, kr_ref, cos_ref, sin_ref, dqn_ref, dqr_ref):
        qi = pl.program_id(1)
        q_r = qr_ref[...]
        q_n = [qn_ref[:, hh * LANES:(hh + 1) * LANES] for hh in range(2)]
        d_o = [do_ref[:, hh * LANES:(hh + 1) * LANES] for hh in range(2)]
        lse_h = [lse_ref[hh] for hh in range(2)]
        dl_h = [dl_ref[hh] for hh in range(2)]

        def block(j, carry, mask_off):
            sl = pl.ds(pl.multiple_of(j * ATT_IN, ATT_IN), ATT_IN)
            dq2 = list(carry)
            for hh in range(2):
                kn = kv_ref[sl, 2 * hh * LANES:(2 * hh + 1) * LANES]
                v = kv_ref[sl, (2 * hh + 1) * LANES:(2 * hh + 2) * LANES]
                kr = kr_ref[hh, sl, :]
                s = _scores(q_n[hh], q_r, kn, kr, mask_off, False)
                pr = jnp.exp(s - lse_h[hh])
                dp = _dot(d_o[hh], v, DN_NT)
                ds = (pr * (dp - dl_h[hh]) * MLA_SCALE).astype(BF16)
                dq2[hh] = dq2[hh] + _dot(ds, jnp.concatenate([kn, kr], axis=1), DN_NN)
            return tuple(dq2)

        z2 = jnp.zeros((ATT_OUT, 2 * LANES), F32)
        res = lax.fori_loop(0, ATT_R * qi, lambda j, c: block(j, c, None), (z2, z2))
        for d in range(ATT_R):
            res = block(ATT_R * qi + d, res, d * ATT_IN)
        dqn_ref[:, 0:LANES] = res[0][:, :LANES].astype(BF16)
        dqn_ref[:, LANES:2 * LANES] = res[1][:, :LANES].astype(BF16)
        dqr = res[0][:, LANES:] + res[1][:, LANES:]
        dqr_ref[...] = _rope_t(dqr, cos_ref[...], sin_ref[...]).astype(BF16)

    st = jax.ShapeDtypeStruct
    return pl.pallas_call(
        body, name=name, grid=(MLA_HEADS // 2, nq),
        in_specs=[pl.BlockSpec((ATT_OUT, 2 * LANES), lambda p, i: (i, p)),
                  pl.BlockSpec((ATT_OUT, LANES), lambda p, i: (i, p)),
                  pl.BlockSpec((ATT_OUT, 2 * LANES), lambda p, i: (i, p)),
                  pl.BlockSpec((2, ATT_OUT, 1), lambda p, i: (p, i, 0)),
                  pl.BlockSpec((2, ATT_OUT, 1), lambda p, i: (p, i, 0)),
                  pl.BlockSpec((l, 4 * LANES), lambda p, i: (0, p)),
                  pl.BlockSpec((2, l, LANES), lambda p, i: (0, 0, 0)),
                  pl.BlockSpec((ATT_OUT, LANES), lambda p, i: (i, 0)),
                  pl.BlockSpec((ATT_OUT, LANES), lambda p, i: (i, 0))],
        out_specs=[pl.BlockSpec((ATT_OUT, 2 * LANES), lambda p, i: (i, p)),
                   pl.BlockSpec((ATT_OUT, LANES), lambda p, i: (i, p))],
        out_shape=[st((l, MLA_DI), BF16), st((l, MLA_HEADS * MLA_ROPE), BF16)],
        compiler_params=_cparams(("parallel", "arbitrary")))(qn, qr, do, lse, delta, kv, krs, cos, sins)


def mla_gate_fwd(o, a, name):
    l = o.shape[0]
    tl = ROW_TILE

    def body(o_ref, z_ref, m_ref):
        m_ref[...] = (o_ref[...] * _silu(z_ref[...])).astype(BF16)

    return _rowcall(body, name, l // tl, [_rows(tl, MLA_DI), _rows(tl, MLA_DI)], _rows(tl, MLA_DI),
                    jax.ShapeDtypeStruct((l, MLA_DI), BF16))(o, a)


def mla_gate_bwd(dm, o, a, name):
    l = o.shape[0]
    tl = ROW_TILE

    def body(dm_ref, o_ref, z_ref, do_ref, dz_ref, dl_ref):
        dmv, ov, z = dm_ref[...], o_ref[...], z_ref[...]
        sz, gz = _silu_both(z)
        d_o = dmv * sz
        do_ref[...] = d_o.astype(BF16)
        dz_ref[...] = (dmv * ov * gz).astype(BF16)
        pr = d_o * ov
        for h in range(MLA_HEADS):
            dl_ref[h] = jnp.sum(pr[:, h * LANES:(h + 1) * LANES], axis=1, keepdims=True)

    st = jax.ShapeDtypeStruct
    return _rowcall(body, name, l // tl, [_rows(tl, MLA_DI)] * 3,
                    [_rows(tl, MLA_DI), _rows(tl, MLA_DI), pl.BlockSpec((MLA_HEADS, tl, 1), lambda i: (0, i, 0))],
                    [st((l, MLA_DI), BF16), st((l, MLA_DI), BF16), st((MLA_HEADS, l, 1), F32)])(dm, o, a)


def mla_post(a, dcqn, dckvn, dkr_pairs, dz, q_g, kv_g, cos, sins, name):
    l = a.shape[0]
    tl = ROW_TILE
    npair = MLA_HEADS // 2

    def norm_bwd(x, g, dy):
        xhat, r = _rmsn(x)
        dxh = dy * g
        return r * (dxh - xhat * jnp.mean(dxh * xhat, axis=-1, keepdims=True)), jnp.sum(dy * xhat, axis=0, keepdims=True)

    def body(a_ref, dq_ref, dk_ref, dkr_ref, dz_ref, qg_ref, kg_ref, cos_ref, sin_ref, da_ref, dqg_ref, dkg_ref):
        i = pl.program_id(0)
        da_ref[:, 0:MLA_DI] = dz_ref[...]
        dcq, dqg = norm_bwd(a_ref[:, MLA_CQ0:MLA_CKV0], qg_ref[...], dq_ref[...])
        da_ref[:, MLA_CQ0:MLA_CKV0] = dcq.astype(BF16)
        dckv, dkg = norm_bwd(a_ref[:, MLA_CKV0:MLA_KR0], kg_ref[...], dk_ref[...])
        da_ref[:, MLA_CKV0:MLA_KR0] = dckv.astype(BF16)
        dk2 = dkr_ref[0]
        for p in range(1, npair):
            dk2 = dk2 + dkr_ref[p]
        dk2 = _rope_t(dk2, cos_ref[...], sin_ref[...])
        dk2 = dk2 + pltpu.roll(dk2, MLA_ROPE, 1)
        lane = lax.broadcasted_iota(jnp.int32, dk2.shape, 1)
        da_ref[:, MLA_KR0:MLA_AW] = jnp.where(lane < MLA_ROPE, dk2, 0.0).astype(BF16)
        _acc(dqg_ref, dqg, i)
        _acc(dkg_ref, dkg, i)

    st = jax.ShapeDtypeStruct
    return _rowcall(body, name, l // tl,
                    [_rows(tl, MLA_AW), _rows(tl, MLA_Q_RANK), _rows(tl, MLA_KV_RANK),
                     pl.BlockSpec((npair, tl, LANES), lambda i: (0, i, 0)), _rows(tl, MLA_DI),
                     _full((1, MLA_Q_RANK)), _full((1, MLA_KV_RANK)), _rows(tl, LANES), _rows(tl, LANES)],
                    [_rows(tl, MLA_AW), _full((1, MLA_Q_RANK)), _full((1, MLA_KV_RANK))],
                    [st((l, MLA_AW), BF16), st((1, MLA_Q_RANK), F32), st((1, MLA_KV_RANK), F32)])(
        a, dcqn, dckvn, dkr_pairs, dz, q_g.reshape(1, -1), kv_g.reshape(1, -1), cos, sins)


def _mla_w_in_perm(w):
    r = MLA_Q_RANK + MLA_KV_RANK + MLA_ROPE
    pad = jnp.zeros(w.shape[:-1] + (MLA_AW - MLA_KR0 - MLA_ROPE,), w.dtype)
    return jnp.concatenate([w[..., r:], w[..., :r], pad], axis=-1)


def _mla_w_in_unperm(g):
    r = MLA_Q_RANK + MLA_KV_RANK + MLA_ROPE
    return jnp.concatenate([g[..., MLA_DI:MLA_DI + r], g[..., :MLA_DI]], axis=-1)


def _mla_w_uq_split(w):
    k = w.shape[0]
    w3 = w.reshape(k, MLA_HEADS, MLA_NOPE + MLA_ROPE)
    return w3[:, :, :MLA_NOPE].reshape(k, MLA_HEADS * MLA_NOPE), w3[:, :, MLA_NOPE:].reshape(k, MLA_HEADS * MLA_ROPE)


def _mla_w_uq_merge(gn, gr):
    k = gn.shape[0]
    return jnp.concatenate([gn.reshape(k, MLA_HEADS, MLA_NOPE), gr.reshape(k, MLA_HEADS, MLA_ROPE)], axis=2).reshape(k, -1)


def mla_layer_fwd(h, p, wf, cos, sins, tag):
    hn = rms_fwd(h, p["norm_g"], tag + "_rms")
    w_in = _mla_w_in_perm(wf["w_in"])
    w_uq_n, w_uq_r = _mla_w_uq_split(wf["w_uq"])
    a = matmul(hn, w_in, "nn", tag + "_mm_in")
    cqn, ckvn, krs = mla_pre(a, p["q_norm_g"], p["kv_norm_g"], cos, sins, tag + "_pre")
    qn = matmul(cqn, w_uq_n, "nn", tag + "_mm_qn", out_dtype=BF16)
    qr_raw = matmul(cqn, w_uq_r, "nn", tag + "_mm_qr")
    qr = mla_rope_q(qr_raw, cos, sins, tag + "_rope_q")
    kv = matmul(ckvn, wf["w_ukv"], "nn", tag + "_mm_kv", out_dtype=BF16)
    o, lse = flash_fwd(qn, qr, kv, krs, tag + "_flash")
    m = mla_gate_fwd(o, a, tag + "_gate")
    h_out = matmul(m, wf["w_out"], "nn", tag + "_mm_out", add=h)
    return h_out, (h, hn, a, cqn, ckvn, krs, qn, qr, kv, o, lse, m, w_in, w_uq_n, w_uq_r)


def mla_layer_bwd(dh_out, saved, p, wf, cos, sins, tag, sink):
    h, hn, a, cqn, ckvn, krs, qn, qr, kv, o, lse, m, w_in, w_uq_n, w_uq_r = saved
    l = h.shape[0]
    dm = matmul(dh_out, wf["w_out"], "nt", tag + "_mm_dm")
    sink.mm("w_out", m, dh_out, tag + "_mm_gwout")
    do, dz, delta = mla_gate_bwd(dm, o, a, tag + "_gate_bwd")
    lse_row = lse.reshape(MLA_HEADS, l // ATT_IN, 1, ATT_IN)
    delta_row = delta.reshape(MLA_HEADS, l // ATT_IN, 1, ATT_IN)
    dkv, dkr_pairs = flash_dkv(qn, qr, kv, krs, do, lse_row, delta_row, tag + "_flash_dkv")
    dqn, dqr = flash_dq(qn, qr, kv, krs, do, lse, delta, cos, sins, tag + "_flash_dq")
    dcqn = matmul(dqn, w_uq_n, "nt", tag + "_mm_dcq_n")
    dcqn = matmul(dqr, w_uq_r, "nt", tag + "_mm_dcq_r", add=dcqn)
    g_uq_n = matmul(cqn, dqn, "tn", tag + "_mm_guq_n")
    g_uq_r = matmul(cqn, dqr, "tn", tag + "_mm_guq_r")
    dckvn = matmul(dkv, wf["w_ukv"], "nt", tag + "_mm_dckv")
    sink.mm("w_ukv", ckvn, dkv, tag + "_mm_gukv")
    da, dqg, dkg = mla_post(a, dcqn, dckvn, dkr_pairs, dz, p["q_norm_g"], p["kv_norm_g"], cos, sins, tag + "_post")
    dhn = matmul(da, w_in, "nt", tag + "_mm_dhn")
    g_w_in = matmul(hn, da, "tn", tag + "_mm_gwin")
    sink.put("w_uq", _mla_w_uq_merge(g_uq_n, g_uq_r))
    sink.put("w_in", _mla_w_in_unperm(g_w_in))
    zero = sink.send()
    dh, dng = rms_bwd(h, p["norm_g"] + zero, dhn, dh_out, tag + "_rms_bwd")
    grads = {"norm_g": dng.reshape(-1), "q_norm_g": dqg.reshape(-1), "kv_norm_g": dkg.reshape(-1)}
    return dh, grads


ANY = pl.BlockSpec(memory_space=pl.ANY)


def _me():
    return lax.axis_index("x"), lax.axis_index("y"), lax.axis_index("c")


def _chip():
    return 2 * lax.axis_index("x") + lax.axis_index("y")


def _other_chips(x, y):
    return [(1 - x, y), (x, 1 - y), (1 - x, 1 - y)]


def _rcopy(src, dst, ssem, rsem, dev):
    return pltpu.make_async_remote_copy(src_ref=src, dst_ref=dst, send_sem=ssem, recv_sem=rsem,
                                        device_id=dev, device_id_type=MESH)


def _half(ref, c, hf):
    return ref.at[pl.ds(c * hf, hf), :]


HBM = pl.BlockSpec(memory_space=pltpu.HBM)
SEM = pl.BlockSpec(memory_space=pltpu.SEMAPHORE)
SPLIT_EFFECT = pltpu.SideEffectType.DATAFLOW_SIDE_EFFECTING


def gather_start(wb, after, name):
    nr, w = wb.shape
    hf = nr // 2

    def body(w_ref, land_ref, after_ref, ssem, rsem, w_thru, land_thru, token):
        x, y, c = _me()
        k = 2 * x + y
        for j, (cx, cy) in enumerate(_other_chips(x, y)):
            _rcopy(_half(w_ref, c, hf), _half(land_ref.at[k], c, hf), ssem.at[j], rsem.at[j], (cx, cy, c)).start()
        token[...] = jnp.zeros_like(token)

    land = lax.empty((N_CHIPS, nr, w), wb.dtype)
    return pl.pallas_call(
        body, name=name,
        out_shape=(pltpu.SemaphoreType.DMA((3,)), pltpu.SemaphoreType.DMA((3,)), pltpu.HBM(wb.shape, wb.dtype),
                   pltpu.HBM(land.shape, land.dtype), jax.ShapeDtypeStruct((8, LANES), F32)),
        in_specs=(HBM, HBM, ANY), out_specs=(SEM, SEM, HBM, HBM, pl.BlockSpec(memory_space=pltpu.VMEM)),
        input_output_aliases={0: 2, 1: 3},
        compiler_params=pltpu.CompilerParams(has_side_effects=SPLIT_EFFECT))(
        pltpu.with_memory_space_constraint(wb, pltpu.HBM), pltpu.with_memory_space_constraint(land, pltpu.HBM), after)


def gather_wait(ssem, rsem, w_thru, land_thru, after, name):
    nr, w = w_thru.shape
    hf = nr // 2

    def body(w_ref, land_ref, ssem_ref, rsem_ref, after_ref, w_dead, got_ref):
        x, y, c = _me()
        for j, (cx, cy) in enumerate(_other_chips(x, y)):
            cp = _rcopy(_half(w_ref, c, hf), _half(land_ref.at[2 * cx + cy], c, hf), ssem_ref.at[j], rsem_ref.at[j],
                        (cx, cy, c))
            cp.wait_send()
            cp.wait_recv()

    return pl.pallas_call(
        body, name=name, out_shape=(pltpu.HBM(w_thru.shape, w_thru.dtype), pltpu.HBM(land_thru.shape, land_thru.dtype)),
        in_specs=(HBM, HBM, SEM, SEM, ANY), out_specs=(HBM, HBM), input_output_aliases={0: 0, 1: 1},
        compiler_params=pltpu.CompilerParams(has_side_effects=SPLIT_EFFECT))(w_thru, land_thru, ssem, rsem, after)[1]


def gather_handover(land, wb, name):
    _, nr, w = land.shape
    hf = nr // 2

    def body(l_ref, o_ref, ssem, rsem):
        x, y, c = _me()
        chips = _other_chips(x, y)
        sends = []
        for j, (cx, cy) in enumerate(chips):
            region = _half(o_ref.at[2 * cx + cy], c, hf)
            sends.append(_rcopy(region, region, ssem.at[j], rsem.at[j], (x, y, 1 - c)))
            sends[-1].start()
        for j, (cx, cy) in enumerate(chips):
            region = _half(o_ref.at[2 * cx + cy], 1 - c, hf)
            _rcopy(region, region, ssem.at[j], rsem.at[j], (x, y, 1 - c)).wait_recv()
        for cp in sends:
            cp.wait_send()

    out = pl.pallas_call(
        body, name=name, in_specs=[ANY], out_specs=ANY, input_output_aliases={0: 0},
        out_shape=jax.ShapeDtypeStruct(land.shape, land.dtype),
        scratch_shapes=[pltpu.SemaphoreType.DMA((3,)), pltpu.SemaphoreType.DMA((3,))])(land)
    return lax.dynamic_update_slice(out, wb[None], (_chip(), 0, 0))


def reduce_start(t, after, name):
    def body(t_ref, land_ref, after_ref, ssem, rsem, t_thru, land_thru, token):
        x, y, c = _me()
        k = 2 * x + y
        for j, (cx, cy) in enumerate(_other_chips(x, y)):
            _rcopy(t_ref.at[2 * cx + cy], land_ref.at[k], ssem.at[j], rsem.at[j], (cx, cy, c)).start()
        token[...] = jnp.zeros_like(token)

    land = lax.empty(t.shape, t.dtype)
    return pl.pallas_call(
        body, name=name,
        out_shape=(pltpu.SemaphoreType.DMA((3,)), pltpu.SemaphoreType.DMA((3,)), pltpu.HBM(t.shape, t.dtype),
                   pltpu.HBM(t.shape, t.dtype), jax.ShapeDtypeStruct((8, LANES), F32)),
        in_specs=(HBM, HBM, ANY), out_specs=(SEM, SEM, HBM, HBM, pl.BlockSpec(memory_space=pltpu.VMEM)),
        input_output_aliases={0: 2, 1: 3},
        compiler_params=pltpu.CompilerParams(has_side_effects=SPLIT_EFFECT))(
        pltpu.with_memory_space_constraint(t, pltpu.HBM), pltpu.with_memory_space_constraint(land, pltpu.HBM), after)


def bcast_start(g, after, name):
    def body(g_ref, land_ref, after_ref, ssem, rsem, g_thru, land_thru, token):
        x, y, c = _me()
        k = 2 * x + y
        for j, (cx, cy) in enumerate(_other_chips(x, y)):
            _rcopy(g_ref, land_ref.at[k], ssem.at[j], rsem.at[j], (cx, cy, c)).start()
        token[...] = jnp.zeros_like(token)

    land = lax.empty((N_CHIPS,) + g.shape, g.dtype)
    return pl.pallas_call(
        body, name=name,
        out_shape=(pltpu.SemaphoreType.DMA((3,)), pltpu.SemaphoreType.DMA((3,)), pltpu.HBM(g.shape, g.dtype),
                   pltpu.HBM(land.shape, land.dtype), jax.ShapeDtypeStruct((8, LANES), F32)),
        in_specs=(HBM, HBM, ANY), out_specs=(SEM, SEM, HBM, HBM, pl.BlockSpec(memory_space=pltpu.VMEM)),
        input_output_aliases={0: 2, 1: 3},
        compiler_params=pltpu.CompilerParams(has_side_effects=SPLIT_EFFECT))(
        pltpu.with_memory_space_constraint(g, pltpu.HBM), pltpu.with_memory_space_constraint(land, pltpu.HBM), after)


def bcast_wait(ssem, rsem, g_thru, land_thru, after, name):
    def body(g_ref, land_ref, ssem_ref, rsem_ref, after_ref, g_out, got_ref):
        x, y, c = _me()
        for j, (cx, cy) in enumerate(_other_chips(x, y)):
            cp = _rcopy(g_ref, land_ref.at[2 * cx + cy], ssem_ref.at[j], rsem_ref.at[j], (cx, cy, c))
            cp.wait_send()
            cp.wait_recv()

    g, land = pl.pallas_call(
        body, name=name, out_shape=(pltpu.HBM(g_thru.shape, g_thru.dtype), pltpu.HBM(land_thru.shape, land_thru.dtype)),
        in_specs=(HBM, HBM, SEM, SEM, ANY), out_specs=(HBM, HBM), input_output_aliases={0: 0, 1: 1},
        compiler_params=pltpu.CompilerParams(has_side_effects=SPLIT_EFFECT))(g_thru, land_thru, ssem, rsem, after)
    return lax.dynamic_update_slice(land, g[None], (_chip(), 0, 0))


def reduce_wait(ssem, rsem, t_thru, land_thru, after, name):
    def body(t_ref, land_ref, ssem_ref, rsem_ref, after_ref, t_out, got_ref):
        x, y, c = _me()
        k = 2 * x + y
        for j, (cx, cy) in enumerate(_other_chips(x, y)):
            cp = _rcopy(t_ref.at[k], land_ref.at[2 * cx + cy], ssem_ref.at[j], rsem_ref.at[j], (cx, cy, c))
            cp.wait_send()
            cp.wait_recv()

    return pl.pallas_call(
        body, name=name, out_shape=(pltpu.HBM(t_thru.shape, t_thru.dtype), pltpu.HBM(land_thru.shape, land_thru.dtype)),
        in_specs=(HBM, HBM, SEM, SEM, ANY), out_specs=(HBM, HBM), input_output_aliases={0: 0, 1: 1},
        compiler_params=pltpu.CompilerParams(has_side_effects=SPLIT_EFFECT))(t_thru, land_thru, ssem, rsem, after)


def grads_to_sibling(ps, name="grads_to_sibling"):
    n = len(ps)

    def body(*refs):
        p_refs, o_refs, ssem, rsem = refs[:n], refs[n:2 * n], refs[2 * n], refs[2 * n + 1]
        x, y, c = _me()
        cps = []
        for a in range(n):
            hf = ps[a].shape[1] // 2
            cps.append(_rcopy(p_refs[a].at[:, pl.ds((1 - c) * hf, hf), :], o_refs[a], ssem.at[a], rsem.at[a],
                              (x, y, 1 - c)))
        for cp in cps:
            cp.start()
        for cp in cps:
            cp.wait()

    return pl.pallas_call(
        body, name=name, in_specs=[ANY] * n, out_specs=[ANY] * n,
        out_shape=[jax.ShapeDtypeStruct((N_CHIPS, p.shape[1] // 2, p.shape[2]), p.dtype) for p in ps],
        scratch_shapes=[pltpu.SemaphoreType.DMA((n,)), pltpu.SemaphoreType.DMA((n,))])(*ps)


def pair_sum(p, ra, out_dtype, name):
    _, nr, w = p.shape
    hf = nr // 2
    tr = _pick_rows(hf, cap=max(512, SUM_BLOCK_BYTES // (4 * w)))
    nb = hf // tr

    def body(c_ref, p_ref, r_ref, o_ref):
        o_ref[...] = (p_ref[...] + r_ref[...]).astype(out_dtype)

    c = lax.axis_index("c").astype(jnp.int32).reshape(1)
    return pl.pallas_call(
        body, name=name,
        grid_spec=pltpu.PrefetchScalarGridSpec(
            num_scalar_prefetch=1, grid=(N_CHIPS, nb),
            in_specs=[pl.BlockSpec((1, tr, w), lambda k, i, c_ref: (k, c_ref[0] * nb + i, 0)),
                      pl.BlockSpec((1, tr, w), lambda k, i, c_ref: (k, i, 0))],
            out_specs=pl.BlockSpec((1, tr, w), lambda k, i, c_ref: (k, i, 0))),
        out_shape=jax.ShapeDtypeStruct((N_CHIPS, hf, w), out_dtype),
        compiler_params=_cparams(("parallel", "parallel")))(c, p, ra)


def grads_across_chips(ts):
    n = len(ts)

    def body(*refs):
        t_refs, o_refs, ssem, rsem = refs[:n], refs[n:2 * n], refs[2 * n], refs[2 * n + 1]
        x, y, c = _me()
        k = 2 * x + y
        chips = _other_chips(x, y)
        sends = [_rcopy(t_refs[a].at[2 * cx + cy], o_refs[a].at[k], ssem.at[3 * a + j], rsem.at[3 * a + j], (cx, cy, c))
                 for a in range(n) for j, (cx, cy) in enumerate(chips)]
        for cp in sends:
            cp.start()
        for a in range(n):
            for j, (cx, cy) in enumerate(chips):
                _rcopy(t_refs[a].at[k], o_refs[a].at[2 * cx + cy], ssem.at[3 * a + j], rsem.at[3 * a + j],
                       (cx, cy, c)).wait_recv()
        for cp in sends:
            cp.wait_send()

    return pl.pallas_call(
        body, name="grads_across_chips", in_specs=[ANY] * n, out_specs=[ANY] * n,
        out_shape=[jax.ShapeDtypeStruct(t.shape, t.dtype) for t in ts],
        scratch_shapes=[pltpu.SemaphoreType.DMA((3 * n,)), pltpu.SemaphoreType.DMA((3 * n,))])(*ts)


def chip_sum(t, rb, name):
    _, hf, w = rb.shape
    tr = _pick_rows(hf, cap=max(512, SUM_BLOCK_BYTES // (4 * w)))
    nb = hf // tr

    def body(kc_ref, t_ref, r_ref, o_ref):
        k = kc_ref[0]
        acc = jnp.where(k == 0, t_ref[0], r_ref[0]).astype(F32)
        for j in range(1, N_CHIPS):
            acc = acc + jnp.where(k == j, t_ref[0], r_ref[j]).astype(F32)
        o_ref[...] = acc

    kc = jnp.stack([_chip(), lax.axis_index("c")]).astype(jnp.int32)
    return pl.pallas_call(
        body, name=name,
        grid_spec=pltpu.PrefetchScalarGridSpec(
            num_scalar_prefetch=1, grid=(nb,),
            in_specs=[pl.BlockSpec((1, tr, w), lambda i, kc_ref: (kc_ref[0], i, 0)),
                      pl.BlockSpec((N_CHIPS, tr, w), lambda i, kc_ref: (0, i, 0))],
            out_specs=pl.BlockSpec((tr, w), lambda i, kc_ref: (kc_ref[1] * nb + i, 0))),
        out_shape=jax.ShapeDtypeStruct((2 * hf, w), F32), compiler_params=_cparams(("parallel",)))(kc, t, rb)


def reduced_to_sibling(gs):
    n = len(gs)

    def body(*refs):
        o_refs, ssem, rsem = refs[n:2 * n], refs[2 * n], refs[2 * n + 1]
        x, y, c = _me()
        cps = []
        for a in range(n):
            hf = gs[a].shape[0] // 2
            cps.append(_rcopy(_half(o_refs[a], c, hf), _half(o_refs[a], c, hf), ssem.at[a], rsem.at[a], (x, y, 1 - c)))
        for cp in cps:
            cp.start()
        for a in range(n):
            hf = gs[a].shape[0] // 2
            _rcopy(_half(o_refs[a], c, hf), _half(o_refs[a], 1 - c, hf), ssem.at[a], rsem.at[a],
                   (x, y, 1 - c)).wait_recv()
        for cp in cps:
            cp.wait_send()

    return pl.pallas_call(
        body, name="reduced_to_sibling", in_specs=[ANY] * n, out_specs=[ANY] * n,
        input_output_aliases={a: a for a in range(n)},
        out_shape=[jax.ShapeDtypeStruct(g.shape, g.dtype) for g in gs],
        scratch_shapes=[pltpu.SemaphoreType.DMA((n,)), pltpu.SemaphoreType.DMA((n,))])(*gs)


def _adamw_step(w_ref, g_ref, m_ref, v_ref, d_ref, nm_ref, nv_ref):
    bc1 = 1.0 - ADAM_B1 ** ADAM_STEP
    bc2 = 1.0 - ADAM_B2 ** ADAM_STEP
    gv = g_ref[...]
    nm = ADAM_B1 * m_ref[...] + (1.0 - ADAM_B1) * gv
    nv = ADAM_B2 * v_ref[...] + (1.0 - ADAM_B2) * (gv * gv)
    nm_ref[...] = nm
    nv_ref[...] = nv
    d_ref[...] = -ADAM_LR * ((nm / bc1) / (jnp.sqrt(nv / bc2) + ADAM_EPS) + ADAM_WD * w_ref[...])


def adamw_packed(w, g_buf, r0, m, v, name):
    r, c = w.shape
    tr = _tile_rows(r, r0, (1024, 512, 384, 256, 128))

    def body(w_ref, g_ref, m_ref, v_ref, go_ref, d_ref, nm_ref, nv_ref):
        go_ref[...] = g_ref[...]
        _adamw_step(w_ref, g_ref, m_ref, v_ref, d_ref, nm_ref, nv_ref)

    own = pl.BlockSpec((tr, CHUNK_W), lambda i, j: (i, j))
    packed = pl.BlockSpec((tr, CHUNK_W), lambda i, j: ((r0 + j * r) // tr + i, 0))
    st = jax.ShapeDtypeStruct((r, c), F32)
    return pl.pallas_call(body, name=name, grid=(r // tr, c // CHUNK_W), in_specs=[own, packed, own, own],
                          out_specs=[own] * 4, out_shape=[st] * 4,
                          compiler_params=_cparams(("parallel", "parallel")))(w, g_buf, m, v)


def adamw(w, g, m, v, name):
    r, wd = w.shape
    tr = _pick_rows(r, cap=max(16, ADAMW_BLOCK_BYTES // (4 * wd)))
    body = functools.partial(_adamw_step)

    spec = pl.BlockSpec((tr, wd), lambda i: (i, 0))
    st = jax.ShapeDtypeStruct((r, wd), F32)
    return pl.pallas_call(body, name=name, grid=(r // tr,), in_specs=[spec] * 4, out_specs=[spec] * 3,
                          out_shape=[st, st, st], compiler_params=_cparams(("parallel",)))(w, g, m, v)


LAYER_KINDS = ("gmlp", "s5", "mla", "gmlp")
PARAMS = {
    "gmlp": ("norm_g", "w_in", "ln_g", "ln_b", "w_s", "b_s", "w_out"),
    "s5": ("norm_g", "w_in", "a_re", "a_im", "log_step", "b_re", "b_im", "c_re", "c_im", "d_skip", "w_glu", "b_glu", "w_out"),
    "mla": ("norm_g", "w_in", "q_norm_g", "w_uq", "kv_norm_g", "w_ukv", "w_out"),
}
COL_SHARDED = ("w_in", "w_uq", "w_ukv")
ROW_SHARDED = ("w_out", "w_glu")
WEIGHT_NAMES = [("l%d_" % i) + n for i, kind in enumerate(LAYER_KINDS) for n in PARAMS[kind]] + ["final_norm_g"]


def _is_big(name):
    return name.split("_", 1)[1] in COL_SHARDED + ROW_SHARDED


BIG = [n for n in WEIGHT_NAMES if _is_big(n)]
SMALL = [n for n in WEIGHT_NAMES if not _is_big(n)]


def _pack_rows(blocks):
    return jnp.concatenate([b.reshape(-1, PACK_W) for b in blocks], axis=0)


def _shard_major(wn, full, width):
    r, c = full.shape
    if wn in COL_SHARDED:
        t = full.reshape(r, N_CHIPS, c // N_CHIPS).transpose(1, 0, 2)
    else:
        t = full.reshape(N_CHIPS, r // N_CHIPS, c)
    return t.reshape(N_CHIPS, -1, width)


def _from_shard_major(name, t, block_shape):
    r, c = block_shape
    if name.split("_", 1)[1] in COL_SHARDED:
        return t.reshape(N_CHIPS, r, c).transpose(1, 0, 2).reshape(r, N_CHIPS * c)
    return t.reshape(N_CHIPS * r, c)


class BigGradSink:
    ORDER = ("w_out", "w_glu", "w_ukv", "w_uq", "w_in")
    ROW_MAJOR = {2: ("w_uq", "w_in")}

    def __init__(self, layer, block_shapes):
        self.layer = layer
        self.regions = {}
        r0 = 0
        for wn in self.ORDER:
            if wn in block_shapes:
                shape = block_shapes[wn]
                self.regions[wn] = (r0, shape, wn not in self.ROW_MAJOR.get(layer, ()))
                r0 += shape[0] * shape[1] // CHUNK_W
        self.buf = lax.empty((N_CHIPS, r0, CHUNK_W), F32)
        self.flight = None

    def mm(self, wn, a, b, name):
        r0, _, direct = self.regions[wn]
        assert direct
        self.buf = matmul_tn_packed(a, b, self.buf, r0, wn in COL_SHARDED, name)

    def put(self, wn, full):
        r0, _, direct = self.regions[wn]
        assert not direct
        piece = _shard_major(wn, full, CHUNK_W)
        self.buf = lax.dynamic_update_slice(self.buf, piece, (0, r0, 0))

    def send(self):
        i = self.layer
        sib, = grads_to_sibling([self.buf], "grads_to_sibling_l%d" % i)
        t = pair_sum(self.buf, sib, BF16, "pair_sum_l%d" % i)
        self.flight = reduce_start(t, sib, "reduce_l%d_start" % i)
        return self.flight[4][0, 0]


def _small_pack(arrs, total_padded):
    flat = jnp.concatenate([a.reshape(-1) for a in arrs])
    return jnp.pad(flat, (0, total_padded - flat.shape[0]))


def kernel(x, positions, l0_norm_g, l0_w_in, l0_ln_g, l0_ln_b, l0_w_s, l0_b_s, l0_w_out, l1_norm_g, l1_w_in, l1_a_re, l1_a_im, l1_log_step, l1_b_re, l1_b_im, l1_c_re, l1_c_im, l1_d_skip, l1_w_glu, l1_b_glu, l1_w_out, l2_norm_g, l2_w_in, l2_q_norm_g, l2_w_uq, l2_kv_norm_g, l2_w_ukv, l2_w_out, l3_norm_g, l3_w_in, l3_ln_g, l3_ln_b, l3_w_s, l3_b_s, l3_w_out, final_norm_g, loss_target, m_l0_norm_g, m_l0_w_in, m_l0_ln_g, m_l0_ln_b, m_l0_w_s, m_l0_b_s, m_l0_w_out, m_l1_norm_g, m_l1_w_in, m_l1_a_re, m_l1_a_im, m_l1_log_step, m_l1_b_re, m_l1_b_im, m_l1_c_re, m_l1_c_im, m_l1_d_skip, m_l1_w_glu, m_l1_b_glu, m_l1_w_out, m_l2_norm_g, m_l2_w_in, m_l2_q_norm_g, m_l2_w_uq, m_l2_kv_norm_g, m_l2_w_ukv, m_l2_w_out, m_l3_norm_g, m_l3_w_in, m_l3_ln_g, m_l3_ln_b, m_l3_w_s, m_l3_b_s, m_l3_w_out, m_final_norm_g, v_l0_norm_g, v_l0_w_in, v_l0_ln_g, v_l0_ln_b, v_l0_w_s, v_l0_b_s, v_l0_w_out, v_l1_norm_g, v_l1_w_in, v_l1_a_re, v_l1_a_im, v_l1_log_step, v_l1_b_re, v_l1_b_im, v_l1_c_re, v_l1_c_im, v_l1_d_skip, v_l1_w_glu, v_l1_b_glu, v_l1_w_out, v_l2_norm_g, v_l2_w_in, v_l2_q_norm_g, v_l2_w_uq, v_l2_kv_norm_g, v_l2_w_ukv, v_l2_w_out, v_l3_norm_g, v_l3_w_in, v_l3_ln_g, v_l3_ln_b, v_l3_w_s, v_l3_b_s, v_l3_w_out, v_final_norm_g):
    args = locals()
    w = {n: args[n] for n in WEIGHT_NAMES}
    mom_m = {n: args["m_" + n] for n in WEIGHT_NAMES}
    mom_v = {n: args["v_" + n] for n in WEIGHT_NAMES}
    h0 = x[0]
    target = loss_target[0]
    pos = positions.reshape(-1, 1)

    full = {}

    def pack_unit(layers):
        names = [n for n in BIG if int(n[1]) in layers]
        rows = [w[n].size // PACK_W for n in names]
        pad = -sum(rows) % PACK_ROW_ALIGN
        return names, rows, _pack_rows([w[n].astype(BF16) for n in names] + [jnp.zeros((pad, PACK_W), BF16)])

    def unpack_unit(names, rows, gathered):
        r0 = 0
        for n, nr in zip(names, rows):
            full[n] = _from_shard_major(n, gathered[:, r0:r0 + nr, :], w[n].shape)
            r0 += nr

    unit0, unit1, unit2 = pack_unit((0,)), pack_unit((1,)), pack_unit((2, 3))
    wp = dict(w)

    def layer_params(i):
        pre = "l%d_" % i
        p = {k[len(pre):]: v for k, v in wp.items() if k.startswith(pre)}
        wf = {k[len(pre):]: v for k, v in full.items() if k.startswith(pre)}
        return p, wf

    flight = gather_start(unit0[2], unit1[2], "gather_l0_start")
    cos, sins = rope_tables(pos, flight[4][0, 0])
    wp["l1_a_re"] = w["l1_a_re"] + flight[4][0, 0]
    s5_weights = _s5_weights(layer_params(1)[0])
    land = gather_wait(*flight[:4], s5_weights[2], "gather_l0_wait")
    got = gather_handover(land, unit0[2], "gather_l0_handover")
    unpack_unit(unit0[0], unit0[1], got)
    flight = gather_start(unit1[2], got, "gather_l1_start")
    wp["l0_norm_g"] = w["l0_norm_g"] + flight[4][0, 0]

    h = h0
    saved = []
    for i, kind in enumerate(LAYER_KINDS):
        if i == 1:
            land = gather_wait(*flight[:4], h, "gather_l1_wait")
            got = gather_handover(land, unit1[2], "gather_l1_handover")
            unpack_unit(unit1[0], unit1[1], got)
            flight = gather_start(unit2[2], got, "gather_l23_start")
            wp["l1_norm_g"] = w["l1_norm_g"] + flight[4][0, 0]
        if i == 2:
            land = gather_wait(*flight[:4], h, "gather_l23_wait")
            unpack_unit(unit2[0], unit2[1], gather_handover(land, unit2[2], "gather_l23_handover"))
        p, wf = layer_params(i)
        tag = "l%d" % i
        if kind == "gmlp":
            h, s = gmlp_layer_fwd(h, p, wf, tag)
        elif kind == "s5":
            h, s = s5_layer_fwd(h, p, wf, s5_weights, tag)
        else:
            h, s = mla_layer_fwd(h, p, wf, cos, sins, tag)
        saved.append(s)
    loss_part, dh, g_final = loss_head(h, final_norm_g, target)

    grads = {"final_norm_g": g_final.reshape(-1)}
    sinks = {}

    for i in reversed(range(len(LAYER_KINDS))):
        kind = LAYER_KINDS[i]
        p, wf = layer_params(i)
        tag = "l%d" % i
        sink = sinks[i] = BigGradSink(i, {n[3:]: w[n].shape for n in BIG if int(n[1]) == i})
        if kind == "gmlp":
            dh, g = gmlp_layer_bwd(dh, saved[i], p, wf, tag, sink)
        elif kind == "s5":
            dh, g = s5_layer_bwd(dh, saved[i], p, wf, tag, sink)
        else:
            dh, g = mla_layer_bwd(dh, saved[i], p, wf, cos, sins, tag, sink)
        for k, val in g.items():
            grads["l%d_%s" % (i, k)] = val
    grad_x = dh[None]

    n_small = sum(w[n].size for n in SMALL)
    piece = N_CHIPS * 2 * 16 * PACK_W
    n_small_pad = -(-(n_small + 1) // piece) * piece
    nrs = n_small_pad // N_CHIPS // PACK_W
    p_small = _small_pack([grads[n] for n in SMALL] + [loss_part], n_small_pad).reshape(N_CHIPS, nrs, PACK_W)
    sib_small, = grads_to_sibling([p_small], "grads_to_sibling_small")
    t_small = pair_sum(p_small, sib_small, F32, "pair_sum_small")
    rb_small, = grads_across_chips([t_small])
    halves = [chip_sum(t_small, rb_small, "chip_sum_small")]

    after = halves[0]
    for i in reversed(range(len(LAYER_KINDS))):
        t_i, rb_i = reduce_wait(*sinks[i].flight[:4], after, "reduce_l%d_wait" % i)
        halves.append(chip_sum(t_i, rb_i, "chip_sum_l%d" % i))
        after = halves[-1]
    reduced = reduced_to_sibling(halves)
    small_flight = bcast_start(reduced[0], reduced[1], "small_allgather_start")

    g_out, d_out, nm_out, nv_out = {}, {}, {}, {}
    for i, g_i in zip(reversed(range(len(LAYER_KINDS))), reduced[1:]):
        for wn, (r0, shape, direct) in sinks[i].regions.items():
            n = "l%d_%s" % (i, wn)
            if direct:
                g_out[n], d_out[n], nm_out[n], nv_out[n] = adamw_packed(w[n], g_i, r0, mom_m[n], mom_v[n], "adamw_" + n)
            else:
                g_out[n] = g_i[r0:r0 + shape[0] * shape[1] // CHUNK_W].reshape(shape)
                d_out[n], nm_out[n], nv_out[n] = adamw(w[n], g_out[n], mom_m[n], mom_v[n], "adamw_" + n)
    small_all = bcast_wait(*small_flight[:4], nv_out["l0_w_in"], "small_allgather_wait")
    g_small = small_all.reshape(-1, PACK_W)
    sp = lambda d: _small_pack([d[n] for n in SMALL], n_small_pad).reshape(-1, PACK_W)
    d_small, nm_small, nv_small = adamw(sp(w), g_small, sp(mom_m), sp(mom_v), "adamw_small")
    for buf, out in ((g_small, g_out), (d_small, d_out), (nm_small, nm_out), (nv_small, nv_out)):
        flat = buf.reshape(-1)
        o = 0
        for n in SMALL:
            out[n] = flat[o:o + w[n].size].reshape(w[n].shape)
            o += w[n].size
    loss = g_small.reshape(-1)[n_small]
    return (loss, grad_x, *[g_out[n] for n in WEIGHT_NAMES], *[d_out[n] for n in WEIGHT_NAMES],
            *[nm_out[n] for n in WEIGHT_NAMES], *[nv_out[n] for n in WEIGHT_NAMES])
```

```python
import functools
import math

import jax
import jax.numpy as jnp
import numpy as np
from jax import lax
from jax.experimental import pallas as pl
from jax.experimental.pallas import tpu as pltpu

F32 = jnp.float32
BF16 = jnp.bfloat16
MESH = pl.DeviceIdType.MESH
VMEM_LIMIT_BYTES = 56 * 1024 * 1024
LANES = 128
PACK_W = 1024
CHUNK_W = 256
PACK_ROW_ALIGN = 256
ROW_TILE = 256
ROW_TILE_NARROW = 512
SUM_BLOCK_BYTES = 1024 * 1024
ADAMW_BLOCK_BYTES = 1024 * 1024
MM_BLOCK_BYTES = 12 * 1024 * 1024

NORM_EPS = 1e-6
N_CHIPS = 4
GMLP_CHUNK = 128
GMLP_GROUPS = 8
S5_GROUPS = 128
S5_GROUP = 16
S5_STATE = 64
S5_SB = 16
S5_SEG = 8
MLA_HEADS = 16
MLA_NOPE = 128
MLA_ROPE = 64
MLA_Q_RANK = 384
MLA_KV_RANK = 128
MLA_SCALE = (MLA_NOPE + MLA_ROPE) ** -0.5
ROPE_THETA = 10000.0
NEG_INF = -1e30
ADAM_LR, ADAM_B1, ADAM_B2, ADAM_EPS, ADAM_WD, ADAM_STEP = 0.001, 0.9, 0.999, 1e-08, 0.01, 10

DN_NN = (((1,), (0,)), ((), ()))
DN_NT = (((1,), (1,)), ((), ()))
DN_TN = (((0,), (0,)), ((), ()))


def _cparams(sem):
    return pltpu.CompilerParams(dimension_semantics=sem, vmem_limit_bytes=VMEM_LIMIT_BYTES)


def _pick(n, cands=(512, 384, 256, 128)):
    for c in cands:
        if n % c == 0:
            return c
    return n


def _pick_rows(r, cap=512, mult=16):
    return max(t for t in range(mult, cap + 1, mult) if r % t == 0)


def _dot(a, b, dn):
    return lax.dot_general(a.astype(BF16), b.astype(BF16), dn, preferred_element_type=F32)


def _sigmoid(x):
    return 0.5 + 0.5 * jnp.tanh(0.5 * x)


def _gelu(x):
    c = math.sqrt(2.0 / math.pi)
    t = jnp.tanh(c * (x + 0.044715 * x * x * x))
    return 0.5 * x * (1.0 + t)


def _gelu_grad(x):
    c = math.sqrt(2.0 / math.pi)
    t = jnp.tanh(c * (x + 0.044715 * x * x * x))
    return 0.5 * (1.0 + t) + 0.5 * x * (1.0 - t * t) * c * (1.0 + 3.0 * 0.044715 * x * x)


def _gelu_both(x):
    c = math.sqrt(2.0 / math.pi)
    t = jnp.tanh(c * (x + 0.044715 * x * x * x))
    return 0.5 * x * (1.0 + t), 0.5 * (1.0 + t) + 0.5 * x * (1.0 - t * t) * c * (1.0 + 3.0 * 0.044715 * x * x)


def _silu_both(z):
    s = _sigmoid(z)
    return z * s, s * (1.0 + z * (1.0 - s))


def _silu(z):
    return z * _sigmoid(z)


def matmul(a, b, mode, name, out_dtype=F32, add=None):
    if mode == "nn":
        (m, k), n = a.shape, b.shape[1]
    elif mode == "nt":
        (m, k), n = a.shape, b.shape[0]
    else:
        (k, m), n = a.shape, b.shape[1]
    tm = _pick(m, [t for t in (2048, 1024, 512, 384, 256, 128) if t * k * a.dtype.itemsize <= MM_BLOCK_BYTES])
    tn = _pick(n, [t for t in (512, 384, 256, 128) if t * k * b.dtype.itemsize <= MM_BLOCK_BYTES])
    dn = {"nn": DN_NN, "nt": DN_NT, "tn": DN_TN}[mode]

    def body(*refs):
        if add is None:
            a_ref, b_ref, o_ref = refs
        else:
            a_ref, b_ref, add_ref, o_ref = refs
        r = _dot(a_ref[...], b_ref[...], dn)
        if add is not None:
            r = r + add_ref[...].astype(F32)
        o_ref[...] = r.astype(out_dtype)

    a_spec = pl.BlockSpec((k, tm), lambda i, j: (0, i)) if mode == "tn" else pl.BlockSpec((tm, k), lambda i, j: (i, 0))
    b_spec = pl.BlockSpec((tn, k), lambda i, j: (j, 0)) if mode == "nt" else pl.BlockSpec((k, tn), lambda i, j: (0, j))
    o_spec = pl.BlockSpec((tm, tn), lambda i, j: (i, j))
    in_specs = [a_spec, b_spec] + ([o_spec] if add is not None else [])
    args = (a, b) + ((add,) if add is not None else ())
    return pl.pallas_call(
        body, name=name, grid=(m // tm, n // tn), in_specs=in_specs, out_specs=o_spec,
        out_shape=jax.ShapeDtypeStruct((m, n), out_dtype),
        compiler_params=_cparams(("parallel", "arbitrary")))(*args)


def _tile_rows(r, r0, cands=(512, 384, 256, 128)):
    return next(t for t in cands if r % t == 0 and r0 % t == 0)


def matmul_tn_packed(a, b, buf, r0, col_sharded, name):
    k, m = a.shape
    n = b.shape[1]
    if col_sharded:
        chunks = n // N_CHIPS // CHUNK_W
        tm = _tile_rows(m, r0, (1024, 512, 384, 256, 128))
        o_map = lambda i, j: (j // chunks, (r0 + (j % chunks) * m) // tm + i, 0)
    else:
        rs = m // N_CHIPS
        tm = _tile_rows(rs, r0)
        per = rs // tm
        o_map = lambda i, j: (i // per, (r0 + j * rs) // tm + i % per, 0)

    def body(a_ref, b_ref, buf_ref, o_ref):
        o_ref[0] = _dot(a_ref[...], b_ref[...], DN_TN)

    return pl.pallas_call(
        body, name=name, grid=(m // tm, n // CHUNK_W),
        in_specs=[pl.BlockSpec((k, tm), lambda i, j: (0, i)), pl.BlockSpec((k, CHUNK_W), lambda i, j: (0, j)),
                  pl.BlockSpec(memory_space=pl.ANY)],
        out_specs=pl.BlockSpec((1, tm, CHUNK_W), o_map), out_shape=jax.ShapeDtypeStruct(buf.shape, buf.dtype),
        input_output_aliases={2: 0}, compiler_params=_cparams(("parallel", "arbitrary")))(a, b, buf)


def _rows(tl, w, col=0):
    return pl.BlockSpec((tl, w), lambda i: (i, col))


def _full(shape):
    nd = len(shape)
    return pl.BlockSpec(tuple(shape), lambda i: (0,) * nd)


def _rowcall(body, name, n_steps, in_specs, out_specs, out_shape, scratch=()):
    return pl.pallas_call(
        body, name=name, grid=(n_steps,), in_specs=in_specs, out_specs=out_specs, out_shape=out_shape,
        scratch_shapes=list(scratch), compiler_params=_cparams(("arbitrary",)))


def _acc(ref, val, i):
    @pl.when(i == 0)
    def _():
        ref[...] = val

    @pl.when(i != 0)
    def _():
        ref[...] += val


def rms_fwd(h, g, name):
    l, d = h.shape
    tl = ROW_TILE_NARROW

    def body(h_ref, g_ref, o_ref):
        x = h_ref[...]
        r = lax.rsqrt(jnp.mean(x * x, axis=-1, keepdims=True) + NORM_EPS)
        o_ref[...] = (x * r * g_ref[...]).astype(BF16)

    return _rowcall(body, name, l // tl, [_rows(tl, d), _full((1, d))], _rows(tl, d),
                    jax.ShapeDtypeStruct((l, d), BF16))(h, g.reshape(1, d))


def rms_bwd(h, g, dhn, dh_in, name):
    l, d = h.shape
    tl = ROW_TILE_NARROW

    def body(h_ref, g_ref, dhn_ref, dhi_ref, dh_ref, dg_ref):
        i = pl.program_id(0)
        x = h_ref[...]
        r = lax.rsqrt(jnp.mean(x * x, axis=-1, keepdims=True) + NORM_EPS)
        xhat = x * r
        dy = dhn_ref[...]
        dxh = dy * g_ref[...]
        dx = r * (dxh - xhat * jnp.mean(dxh * xhat, axis=-1, keepdims=True))
        dh_ref[...] = dhi_ref[...] + dx
        _acc(dg_ref, jnp.sum(dy * xhat, axis=0, keepdims=True), i)

    return _rowcall(body, name, l // tl, [_rows(tl, d), _full((1, d)), _rows(tl, d), _rows(tl, d)],
                    [_rows(tl, d), _full((1, d))],
                    [jax.ShapeDtypeStruct((l, d), F32), jax.ShapeDtypeStruct((1, d), F32)])(h, g.reshape(1, d), dhn, dh_in)


def loss_head(h, g, target):
    l, d = h.shape
    tl = ROW_TILE_NARROW

    def body(h_ref, g_ref, t_ref, loss_ref, dh_ref, dg_ref):
        i = pl.program_id(0)
        x = h_ref[...]
        gg = g_ref[...]
        r = lax.rsqrt(jnp.mean(x * x, axis=-1, keepdims=True) + NORM_EPS)
        xhat = x * r
        err = xhat * gg - t_ref[...]
        part = 0.5 * jnp.sum(jnp.mean(err * err, axis=-1, keepdims=True), axis=0, keepdims=True)
        _acc(loss_ref, part, i)
        dy = err * (1.0 / d)
        dxh = dy * gg
        dh_ref[...] = r * (dxh - xhat * jnp.mean(dxh * xhat, axis=-1, keepdims=True))
        _acc(dg_ref, jnp.sum(dy * xhat, axis=0, keepdims=True), i)

    return _rowcall(body, "loss_head", l // tl, [_rows(tl, d), _full((1, d)), _rows(tl, d)],
                    [_full((1, 1)), _rows(tl, d), _full((1, d))],
                    [jax.ShapeDtypeStruct((1, 1), F32), jax.ShapeDtypeStruct((l, d), F32),
                     jax.ShapeDtypeStruct((1, d), F32)])(h, g.reshape(1, d), target)


def _gmlp_common(a_ref, lng_ref, lnb_ref):
    di = lng_ref.shape[1]
    u_pre = a_ref[:, 0:di]
    v_pre = a_ref[:, di:2 * di]
    z = a_ref[:, 2 * di:3 * di]
    vg = _gelu(v_pre)
    mu = jnp.mean(vg, axis=-1, keepdims=True)
    xc = vg - mu
    rstd = lax.rsqrt(jnp.mean(xc * xc, axis=-1, keepdims=True) + NORM_EPS)
    vhat = xc * rstd
    vn = vhat * lng_ref[...] + lnb_ref[...]
    return u_pre, v_pre, z, vhat, rstd, vn


def _tril(w):
    r = lax.broadcasted_iota(jnp.int32, w.shape, 0)
    c = lax.broadcasted_iota(jnp.int32, w.shape, 1)
    return jnp.where(c <= r, w, 0.0)


def gmlp_gate_fwd(a, ln_g, ln_b, w_s, b_s, name):
    l, w3 = a.shape
    di = w3 // 3
    dg = di // GMLP_GROUPS
    tl = GMLP_CHUNK

    def body(a_ref, lng_ref, lnb_ref, ws_ref, bs_ref, m_ref):
        u_pre, _, z, _, _, vn = _gmlp_common(a_ref, lng_ref, lnb_ref)
        gate = _gelu(u_pre) * _silu(z)
        for g in range(GMLP_GROUPS):
            sl = slice(g * dg, (g + 1) * dg)
            s = _dot(_tril(ws_ref[g]), vn[:, sl], DN_NN) + bs_ref[g]
            m_ref[:, sl] = (gate[:, sl] * s).astype(BF16)

    return _rowcall(body, name, l // tl,
                    [_rows(tl, w3), _full((1, di)), _full((1, di)), _full(w_s.shape), _full((GMLP_GROUPS, tl, 1))],
                    _rows(tl, di), jax.ShapeDtypeStruct((l, di), BF16))(
        a, ln_g.reshape(1, di), ln_b.reshape(1, di), w_s, b_s.reshape(GMLP_GROUPS, tl, 1))


def gmlp_gate_bwd(a, dm, ln_g, ln_b, w_s, b_s, name):
    l, w3 = a.shape
    di = w3 // 3
    dg = di // GMLP_GROUPS
    tl = GMLP_CHUNK

    def body(a_ref, dm_ref, lng_ref, lnb_ref, ws_ref, bs_ref, da_ref, dlg_ref, dlb_ref, dws_ref, dbs_ref,
             dvn_ref, vh_ref, gv_ref):
        i = pl.program_id(0)
        vg, gv = _gelu_both(a_ref[:, di:2 * di])
        gv_ref[...] = gv
        xc = vg - jnp.mean(vg, axis=-1, keepdims=True)
        rstd = lax.rsqrt(jnp.mean(xc * xc, axis=-1, keepdims=True) + NORM_EPS)
        vh_ref[...] = xc * rstd
        for g in range(GMLP_GROUPS):
            sl = slice(g * dg, (g + 1) * dg)
            wt = _tril(ws_ref[g])
            vn_g = vh_ref[:, sl] * lng_ref[:, sl] + lnb_ref[:, sl]
            s = _dot(wt, vn_g, DN_NN) + bs_ref[g]
            dmg = dm_ref[:, sl]
            u, gu = _gelu_both(a_ref[:, sl])
            sz, gz = _silu_both(a_ref[:, 2 * di + g * dg:2 * di + (g + 1) * dg])
            ds = dmg * u * sz
            da_ref[:, sl] = (dmg * s * sz * gu).astype(BF16)
            da_ref[:, 2 * di + g * dg:2 * di + (g + 1) * dg] = (dmg * u * s * gz).astype(BF16)
            dvn_ref[:, sl] = _dot(wt, ds, DN_TN)
            dw = _tril(_dot(ds, vn_g, DN_NT))
            db = jnp.sum(ds, axis=1, keepdims=True)

            @pl.when(i == 0)
            def _():
                dws_ref[g] = dw
                dbs_ref[g] = db

            @pl.when(i != 0)
            def _():
                dws_ref[g] += dw
                dbs_ref[g] += db

        dvn = dvn_ref[...]
        vhat = vh_ref[...]
        dxh = dvn * lng_ref[...]
        dvg = rstd * (dxh - jnp.mean(dxh, axis=-1, keepdims=True) - vhat * jnp.mean(dxh * vhat, axis=-1, keepdims=True))
        da_ref[:, di:2 * di] = (dvg * gv_ref[...]).astype(BF16)
        _acc(dlg_ref, jnp.sum(dvn * vhat, axis=0, keepdims=True), i)
        _acc(dlb_ref, jnp.sum(dvn, axis=0, keepdims=True), i)

    outs = _rowcall(
        body, name, l // tl,
        [_rows(tl, w3), _rows(tl, di), _full((1, di)), _full((1, di)), _full(w_s.shape), _full((GMLP_GROUPS, tl, 1))],
        [_rows(tl, w3), _full((1, di)), _full((1, di)), _full(w_s.shape), _full((GMLP_GROUPS, tl, 1))],
        [jax.ShapeDtypeStruct((l, w3), BF16), jax.ShapeDtypeStruct((1, di), F32), jax.ShapeDtypeStruct((1, di), F32),
         jax.ShapeDtypeStruct(w_s.shape, F32), jax.ShapeDtypeStruct((GMLP_GROUPS, tl, 1), F32)],
        scratch=[pltpu.VMEM((tl, di), F32)] * 3)(
        a, dm, ln_g.reshape(1, di), ln_b.reshape(1, di), w_s, b_s.reshape(GMLP_GROUPS, tl, 1))
    return outs


def gmlp_layer_fwd(h, p, wf, tag):
    hn = rms_fwd(h, p["norm_g"], tag + "_rms")
    a = matmul(hn, wf["w_in"], "nn", tag + "_mm_in")
    m = gmlp_gate_fwd(a, p["ln_g"], p["ln_b"], p["w_s"], p["b_s"], tag + "_gate")
    h_out = matmul(m, wf["w_out"], "nn", tag + "_mm_out", add=h)
    return h_out, (h, hn, a, m)


def gmlp_layer_bwd(dh_out, saved, p, wf, tag, sink):
    h, hn, a, m = saved
    dm = matmul(dh_out, wf["w_out"], "nt", tag + "_mm_dm")
    sink.mm("w_out", m, dh_out, tag + "_mm_gwout")
    da, dlg, dlb, dws, dbs = gmlp_gate_bwd(a, dm, p["ln_g"], p["ln_b"], p["w_s"], p["b_s"], tag + "_gate_bwd")
    dhn = matmul(da, wf["w_in"], "nt", tag + "_mm_dhn")
    sink.mm("w_in", hn, da, tag + "_mm_gwin")
    zero = sink.send()
    dh, dng = rms_bwd(h, p["norm_g"] + zero, dhn, dh_out, tag + "_rms_bwd")
    grads = {"norm_g": dng.reshape(-1), "ln_g": dlg.reshape(-1), "ln_b": dlb.reshape(-1),
             "w_s": dws, "b_s": dbs.reshape(GMLP_GROUPS, GMLP_CHUNK)}
    return dh, grads


def _cmul(ar, ai, br, bi):
    return ar * br - ai * bi, ar * bi + ai * br


S5_PG = 16


def _gblock(tail):
    return pl.BlockSpec((S5_PG,) + tuple(tail), lambda i: (i, 0, 0))


def s5_params_fwd(a_re, a_im, log_step, b_re, b_im):
    g, p, hh = b_re.shape

    def body(ar_ref, ai_ref, ls_ref, br_ref, bi_ref, lr_ref, li_ref, bbr_ref, bbi_ref):
        ar, ai = ar_ref[...], ai_ref[...]
        step = jnp.exp(ls_ref[...])
        mag = jnp.exp(ar * step)
        lr, li = mag * jnp.cos(ai * step), mag * jnp.sin(ai * step)
        den = 1.0 / (ar * ar + ai * ai)
        fr, fi = _cmul(lr - 1.0, li, ar * den, -ai * den)
        lr_ref[...] = lr
        li_ref[...] = li
        bbr, bbi = _cmul(fr, fi, br_ref[...], bi_ref[...])
        bbr_ref[...] = bbr
        bbi_ref[...] = bbi

    s1 = jax.ShapeDtypeStruct((g, p, 1), F32)
    s3 = jax.ShapeDtypeStruct((g, p, hh), F32)
    b1, b0, b3 = _gblock((p, 1)), _gblock((1, 1)), _gblock((p, hh))
    return pl.pallas_call(body, name="s5_params_fwd", grid=(g // S5_PG,), in_specs=[b1, b1, b0, b3, b3],
                          out_specs=[b1, b1, b3, b3], out_shape=[s1, s1, s3, s3],
                          compiler_params=_cparams(("parallel",)))(
        a_re.reshape(g, p, 1), a_im.reshape(g, p, 1), log_step.reshape(g, 1, 1), b_re, b_im)


def s5_params_bwd(a_re, a_im, log_step, b_re, b_im, dl_re, dl_im, dbb_re, dbb_im):
    g, p, hh = b_re.shape

    def body(ar_ref, ai_ref, ls_ref, br_ref, bi_ref, dlr_ref, dli_ref, dbr_ref, dbi_ref,
             gar_ref, gai_ref, gls_ref, gbr_ref, gbi_ref):
        ar, ai = ar_ref[...], ai_ref[...]
        step = jnp.exp(ls_ref[...])
        mag = jnp.exp(ar * step)
        lr, li = mag * jnp.cos(ai * step), mag * jnp.sin(ai * step)
        den = 1.0 / (ar * ar + ai * ai)
        ir, ii = ar * den, -ai * den
        fr, fi = _cmul(lr - 1.0, li, ir, ii)
        br, bi = br_ref[...], bi_ref[...]
        dbr, dbi = dbr_ref[...], dbi_ref[...]
        gbr, gbi = _cmul(fr, -fi, dbr, dbi)
        gbr_ref[...] = gbr
        gbi_ref[...] = gbi
        pr, pi = _cmul(br, -bi, dbr, dbi)
        gfr = jnp.sum(pr, axis=-1, keepdims=True)
        gfi = jnp.sum(pi, axis=-1, keepdims=True)
        t_r, t_i = _cmul(ir, -ii, gfr, gfi)
        glr, gli = dlr_ref[...] + t_r, dli_ref[...] + t_i
        c1r, c1i = _cmul(step * lr, -step * li, glr, gli)
        qr, qi = _cmul(fr, fi, ir, ii)
        c2r, c2i = _cmul(-qr, qi, gfr, gfi)
        gar_ref[...] = c1r + c2r
        gai_ref[...] = c1i + c2i
        wr, wi = _cmul(ar, ai, lr, li)
        sr, _ = _cmul(wr, -wi, glr, gli)
        gls_ref[...] = jnp.sum(sr, axis=1, keepdims=True) * step

    s1 = jax.ShapeDtypeStruct((g, p, 1), F32)
    s3 = jax.ShapeDtypeStruct((g, p, hh), F32)
    b1, b0, b3 = _gblock((p, 1)), _gblock((1, 1)), _gblock((p, hh))
    return pl.pallas_call(body, name="s5_params_bwd", grid=(g // S5_PG,),
                          in_specs=[b1, b1, b0, b3, b3, b1, b1, b3, b3], out_specs=[b1, b1, b0, b3, b3],
                          out_shape=[s1, s1, jax.ShapeDtypeStruct((g, 1, 1), F32), s3, s3],
                          compiler_params=_cparams(("parallel",)))(
        a_re.reshape(g, p, 1), a_im.reshape(g, p, 1), log_step.reshape(g, 1, 1), b_re, b_im,
        dl_re, dl_im, dbb_re, dbb_im)


def _blockdiag(t):
    sb, n, r, c = t.shape
    eye = jnp.eye(n, dtype=bool)[None, :, None, :, None]
    full = jnp.where(eye, t[:, :, :, None, :], jnp.zeros((), t.dtype))
    return full.reshape(sb, n * r, n * c)


def _blockdiag_extract(m, r, c):
    sb = m.shape[0]
    n = m.shape[1] // r
    m5 = m.reshape(sb, n, r, n, c)
    return jnp.stack([m5[:, i, :, i, :] for i in range(n)], axis=1)


S5_TB = 256
S5_UNROLL = 8


def _lam_power(pr, pi, n):
    for _ in range(int(math.log2(n))):
        pr, pi = _cmul(pr, pi, pr, pi)
    return pr, pi


def _segment_entries(er, ei, pr, pi, reverse):
    seg, ns = er.shape
    row = lax.broadcasted_iota(jnp.int32, (seg, ns), 0)
    cr = jnp.zeros((seg, ns), F32)
    ci = jnp.zeros((seg, ns), F32)
    cur_r = jnp.zeros((1, ns), F32)
    cur_i = jnp.zeros((1, ns), F32)
    for s in (range(seg - 2, -1, -1) if reverse else range(1, seg)):
        src = s + 1 if reverse else s - 1
        mr, mi = _cmul(pr, pi, cur_r, cur_i)
        cur_r = jnp.sum(jnp.where(row == src, er, 0.0), axis=0, keepdims=True) + mr
        cur_i = jnp.sum(jnp.where(row == src, ei, 0.0), axis=0, keepdims=True) + mi
        cr = jnp.where(row == s, cur_r, cr)
        ci = jnp.where(row == s, cur_i, ci)
    return cr, ci


def s5_scan_fused_fwd(a_p, lam_re, lam_im, wb_re, wb_im, wc_re, wc_im, d_skip, name):
    l = a_p.shape[0]
    di = d_skip.shape[1]
    rows = S5_SEG * S5_TB
    nb = l // rows
    ns = wb_re.shape[2]

    def body(u_ref, lr_ref, li_ref, wbr_ref, wbi_ref, wcr_ref, wci_ref, ds_ref, y_ref, yg_ref, ckr_ref, cki_ref,
             bur, bui):
        lr = jnp.broadcast_to(lr_ref[0], (S5_SEG, ns))
        li = jnp.broadcast_to(li_ref[0], (S5_SEG, ns))

        def scan_block(b, carry, keep):
            def step(t, c):
                xr, xi = c
                sl = pl.ds(pl.multiple_of(b * rows + t * S5_SEG, S5_SEG), S5_SEG)
                nr = lr * xr - li * xi + bur[sl, :]
                ni = lr * xi + li * xr + bui[sl, :]
                if keep:
                    bur[sl, :] = nr
                    bui[sl, :] = ni
                return nr, ni

            return lax.fori_loop(0, S5_TB, step, carry, unroll=S5_UNROLL)

        def project(b, carry):
            rs = pl.ds(pl.multiple_of(b * rows, rows), rows)
            u = u_ref[rs, :]
            bur[rs, :] = _dot(u, wbr_ref[0], DN_NN)
            bui[rs, :] = _dot(u, wbi_ref[0], DN_NN)
            return scan_block(b, carry, False)

        zero = jnp.zeros((S5_SEG, ns), F32)
        er, ei = lax.fori_loop(0, nb, project, (zero, zero))
        pr, pi = _lam_power(lr_ref[0], li_ref[0], l // S5_SEG)
        entry = _segment_entries(er, ei, pr, pi, False)

        def emit(b, carry):
            ckr_ref[0, b] = carry[0]
            cki_ref[0, b] = carry[1]
            carry = scan_block(b, carry, True)
            rs = pl.ds(pl.multiple_of(b * rows, rows), rows)
            y = (_dot(bur[rs, :], wcr_ref[0], DN_NN) - _dot(bui[rs, :], wci_ref[0], DN_NN)
                 + ds_ref[...] * u_ref[rs, :])
            y_ref[rs, :] = y
            yg_ref[rs, :] = _gelu(y).astype(BF16)
            return carry

        lax.fori_loop(0, nb, emit, entry)

    sb3 = lambda s: (s, 0, 0)
    st = jax.ShapeDtypeStruct
    return pl.pallas_call(
        body, name=name, grid=(S5_SB,),
        in_specs=[pl.BlockSpec((l, LANES), lambda s: (0, s)),
                  pl.BlockSpec((1, 1, ns), sb3), pl.BlockSpec((1, 1, ns), sb3),
                  pl.BlockSpec((1, LANES, ns), sb3), pl.BlockSpec((1, LANES, ns), sb3),
                  pl.BlockSpec((1, ns, LANES), sb3), pl.BlockSpec((1, ns, LANES), sb3),
                  pl.BlockSpec((1, LANES), lambda s: (0, s))],
        out_specs=[pl.BlockSpec((l, LANES), lambda s: (0, s)), pl.BlockSpec((l, LANES), lambda s: (0, s)),
                   pl.BlockSpec((1, nb, S5_SEG, ns), lambda s: (s, 0, 0, 0)),
                   pl.BlockSpec((1, nb, S5_SEG, ns), lambda s: (s, 0, 0, 0))],
        out_shape=[st((l, di), F32), st((l, di), BF16),
                   st((S5_SB, nb, S5_SEG, ns), F32), st((S5_SB, nb, S5_SEG, ns), F32)],
        scratch_shapes=[pltpu.VMEM((l, ns), F32), pltpu.VMEM((l, ns), F32)],
        compiler_params=_cparams(("parallel",)))(a_p, lam_re, lam_im, wb_re, wb_im, wc_re, wc_im, d_skip)


def s5_scan_fused_bwd(a_p, dy, lam_re, lam_im, wb_re, wb_im, wc_re, wc_im, d_skip, ck_re, ck_im, name):
    l = a_p.shape[0]
    di = d_skip.shape[1]
    rows = S5_SEG * S5_TB
    nb = l // rows
    ns = wb_re.shape[2]

    def body(u_ref, dy_ref, lr_ref, li_ref, wbr_ref, wbi_ref, wcr_ref, wci_ref, ds_ref, ckr_ref, cki_ref,
             du_ref, dwbr_ref, dwbi_ref, dwcr_ref, dwci_ref, dds_ref, dlr_ref, dli_ref, gr, gi, xr_b, xi_b):
        lr = jnp.broadcast_to(lr_ref[0], (S5_SEG, ns))
        li = jnp.broadcast_to(li_ref[0], (S5_SEG, ns))

        def back_project(k, carry):
            b = nb - 1 - k
            rs = pl.ds(pl.multiple_of(b * rows, rows), rows)
            dyv = dy_ref[rs, :]
            gr[rs, :] = _dot(dyv, wcr_ref[0], DN_NT)
            gi[rs, :] = -_dot(dyv, wci_ref[0], DN_NT)

            def step(kk, c):
                ar, ai = c
                sl = pl.ds(pl.multiple_of(b * rows + (S5_TB - 1 - kk) * S5_SEG, S5_SEG), S5_SEG)
                return gr[sl, :] + lr * ar + li * ai, gi[sl, :] + lr * ai - li * ar

            return lax.fori_loop(0, S5_TB, step, carry, unroll=S5_UNROLL)

        zero = jnp.zeros((S5_SEG, ns), F32)
        er, ei = lax.fori_loop(0, nb, back_project, (zero, zero))
        pr, pi = _lam_power(lr_ref[0], -li_ref[0], l // S5_SEG)
        a0r, a0i = _segment_entries(er, ei, pr, pi, True)

        dwbr_ref[...] = jnp.zeros_like(dwbr_ref)
        dwbi_ref[...] = jnp.zeros_like(dwbi_ref)
        dwcr_ref[...] = jnp.zeros_like(dwcr_ref)
        dwci_ref[...] = jnp.zeros_like(dwci_ref)
        dds_ref[...] = jnp.zeros_like(dds_ref)

        def block(k, carry):
            b = nb - 1 - k
            rs = pl.ds(pl.multiple_of(b * rows, rows), rows)
            u = u_ref[rs, :]
            dyv = dy_ref[rs, :]
            body_rows = pl.ds(S5_SEG, rows)
            x0r, x0i = ckr_ref[0, b], cki_ref[0, b]
            xr_b[0:S5_SEG, :] = x0r
            xi_b[0:S5_SEG, :] = x0i
            xr_b[body_rows, :] = _dot(u, wbr_ref[0], DN_NN)
            xi_b[body_rows, :] = _dot(u, wbi_ref[0], DN_NN)

            def fstep(t, c):
                xr, xi = c
                sl = pl.ds(pl.multiple_of((t + 1) * S5_SEG, S5_SEG), S5_SEG)
                nr = lr * xr - li * xi + xr_b[sl, :]
                ni = lr * xi + li * xr + xi_b[sl, :]
                xr_b[sl, :] = nr
                xi_b[sl, :] = ni
                return nr, ni

            lax.fori_loop(0, S5_TB, fstep, (x0r, x0i), unroll=S5_UNROLL)
            dwcr_ref[0] += _dot(xr_b[body_rows, :], dyv, DN_TN)
            dwci_ref[0] -= _dot(xi_b[body_rows, :], dyv, DN_TN)

            def bstep(kk, c):
                ar, ai = c
                sl = pl.ds(pl.multiple_of(b * rows + (S5_TB - 1 - kk) * S5_SEG, S5_SEG), S5_SEG)
                nr = gr[sl, :] + lr * ar + li * ai
                ni = gi[sl, :] + lr * ai - li * ar
                gr[sl, :] = nr
                gi[sl, :] = ni
                return nr, ni

            ar, ai = lax.fori_loop(0, S5_TB, bstep, carry[:2], unroll=S5_UNROLL)
            a_r, a_i = gr[rs, :], gi[rs, :]
            p_r, p_i = xr_b[0:rows, :], xi_b[0:rows, :]
            per_seg = lambda v: jnp.sum(v.reshape(S5_TB, S5_SEG, ns), axis=0)
            carry = (ar, ai, carry[2] + per_seg(a_r * p_r + a_i * p_i), carry[3] + per_seg(a_i * p_r - a_r * p_i))
            du_ref[rs, :] = (_dot(a_r, wbr_ref[0], DN_NT) + _dot(a_i, wbi_ref[0], DN_NT) + ds_ref[...] * dyv).astype(BF16)
            dwbr_ref[0] += _dot(u, a_r, DN_TN)
            dwbi_ref[0] += _dot(u, a_i, DN_TN)
            dds_ref[...] += jnp.sum(dyv * u, axis=0, keepdims=True)
            return carry

        _, _, dlr, dli = lax.fori_loop(0, nb, block, (a0r, a0i, zero, zero))
        dlr_ref[0] = dlr
        dli_ref[0] = dli

    sb3 = lambda s: (s, 0, 0)
    seq = pl.BlockSpec((l, LANES), lambda s: (0, s))
    ck = pl.BlockSpec((1, nb, S5_SEG, ns), lambda s: (s, 0, 0, 0))
    st = jax.ShapeDtypeStruct
    return pl.pallas_call(
        body, name=name, grid=(S5_SB,),
        in_specs=[seq, seq, pl.BlockSpec((1, 1, ns), sb3), pl.BlockSpec((1, 1, ns), sb3),
                  pl.BlockSpec((1, LANES, ns), sb3), pl.BlockSpec((1, LANES, ns), sb3),
                  pl.BlockSpec((1, ns, LANES), sb3), pl.BlockSpec((1, ns, LANES), sb3),
                  pl.BlockSpec((1, LANES), lambda s: (0, s)), ck, ck],
        out_specs=[seq, pl.BlockSpec((1, LANES, ns), sb3), pl.BlockSpec((1, LANES, ns), sb3),
                   pl.BlockSpec((1, ns, LANES), sb3), pl.BlockSpec((1, ns, LANES), sb3),
                   pl.BlockSpec((1, LANES), lambda s: (0, s)),
                   pl.BlockSpec((1, S5_SEG, ns), sb3), pl.BlockSpec((1, S5_SEG, ns), sb3)],
        out_shape=[st((l, di), BF16), st((S5_SB, LANES, ns), F32), st((S5_SB, LANES, ns), F32),
                   st((S5_SB, ns, LANES), F32), st((S5_SB, ns, LANES), F32), st((1, di), F32),
                   st((S5_SB, S5_SEG, ns), F32), st((S5_SB, S5_SEG, ns), F32)],
        scratch_shapes=[pltpu.VMEM((l, ns), F32), pltpu.VMEM((l, ns), F32),
                        pltpu.VMEM((rows + S5_SEG, ns), F32), pltpu.VMEM((rows + S5_SEG, ns), F32)],
        compiler_params=_cparams(("parallel",)))(
        a_p, dy, lam_re, lam_im, wb_re, wb_im, wc_re, wc_im, d_skip, ck_re, ck_im)


def s5_gate_fwd(y, t, b_glu, a_p, name):
    l, d = y.shape
    tl = ROW_TILE

    def body(y_ref, t_ref, b_ref, z_ref, m_ref):
        yg = _gelu(y_ref[...])
        m_ref[...] = (yg * _sigmoid(t_ref[...] + b_ref[...]) * _silu(z_ref[...])).astype(BF16)

    return _rowcall(body, name, l // tl, [_rows(tl, d), _rows(tl, d), _full((1, d)), _rows(tl, d, 1)], _rows(tl, d),
                    jax.ShapeDtypeStruct((l, d), BF16))(y, t, b_glu.reshape(1, d), a_p)


def s5_gate_bwd(dm, y, t, b_glu, a_p, name):
    l, d = y.shape
    tl = ROW_TILE

    def body(dm_ref, y_ref, t_ref, b_ref, z_ref, dt_ref, dyg_ref, dz_ref, db_ref):
        i = pl.program_id(0)
        dmv = dm_ref[...]
        z = z_ref[...]
        yg = _gelu(y_ref[...])
        sg = _sigmoid(t_ref[...] + b_ref[...])
        y2 = yg * sg
        sz, gz = _silu_both(z)
        dy2 = dmv * sz
        dz_ref[...] = (dmv * y2 * gz).astype(BF16)
        dyg_ref[...] = dy2 * sg
        dt = dy2 * yg * sg * (1.0 - sg)
        dt_ref[...] = dt.astype(BF16)
        _acc(db_ref, jnp.sum(dt, axis=0, keepdims=True), i)

    st = jax.ShapeDtypeStruct
    return _rowcall(body, name, l // tl, [_rows(tl, d), _rows(tl, d), _rows(tl, d), _full((1, d)), _rows(tl, d, 1)],
                    [_rows(tl, d), _rows(tl, d), _rows(tl, d), _full((1, d))],
                    [st((l, d), BF16), st((l, d), F32), st((l, d), BF16), st((1, d), F32)])(
        dm, y, t, b_glu.reshape(1, d), a_p)


def s5_act_bwd(y, dyg_a, dyg_b, name):
    l, d = y.shape
    tl = ROW_TILE

    def body(y_ref, a_ref, b_ref, o_ref):
        o_ref[...] = (a_ref[...] + b_ref[...]) * _gelu_grad(y_ref[...])

    return _rowcall(body, name, l // tl, [_rows(tl, d)] * 3, _rows(tl, d), jax.ShapeDtypeStruct((l, d), F32))(y, dyg_a, dyg_b)


def _seg_perm(t):
    l, d = t.shape
    return t.reshape(S5_SEG, l // S5_SEG, d).transpose(1, 0, 2).reshape(l, d)


def _seg_unperm(t):
    l, d = t.shape
    return t.reshape(l // S5_SEG, S5_SEG, d).transpose(1, 0, 2).reshape(l, d)


def _s5_weights(p):
    lr, li, bbr, bbi = s5_params_fwd(p["a_re"], p["a_im"], p["log_step"], p["b_re"], p["b_im"])
    ns = 8 * S5_STATE
    lam_re = lr.reshape(S5_SB, 1, ns)
    lam_im = li.reshape(S5_SB, 1, ns)
    to_bd = lambda t: _blockdiag(t.reshape(S5_SB, 8, t.shape[1], t.shape[2]))
    wb_re = to_bd(bbr.transpose(0, 2, 1)).astype(BF16)
    wb_im = to_bd(bbi.transpose(0, 2, 1)).astype(BF16)
    wc_re = to_bd(p["c_re"].transpose(0, 2, 1)).astype(BF16)
    wc_im = to_bd(p["c_im"].transpose(0, 2, 1)).astype(BF16)
    return lam_re, lam_im, wb_re, wb_im, wc_re, wc_im


def s5_layer_fwd(h, p, wf, sw, tag):
    l = h.shape[0]
    di = p["d_skip"].shape[0]
    hn = rms_fwd(h, p["norm_g"], tag + "_rms")
    hn_p = _seg_perm(hn)
    a_p = matmul(hn_p, wf["w_in"], "nn", tag + "_mm_in")
    dsk = p["d_skip"].reshape(1, di)
    y, yg, ck_re, ck_im = s5_scan_fused_fwd(a_p, *sw, dsk, tag + "_scan")
    t = matmul(yg, wf["w_glu"], "nn", tag + "_mm_glu")
    m = s5_gate_fwd(y, t, p["b_glu"], a_p, tag + "_gate")
    out_p = matmul(m, wf["w_out"], "nn", tag + "_mm_out")
    h_out = residual_add(h, _seg_unperm(out_p), tag + "_res")
    return h_out, (h, hn_p, a_p, sw, ck_re, ck_im, y, yg, t, m)


def residual_add(h, y, name):
    l, d = h.shape
    tl = ROW_TILE_NARROW

    def body(h_ref, y_ref, o_ref):
        o_ref[...] = h_ref[...] + y_ref[...]

    return _rowcall(body, name, l // tl, [_rows(tl, d)] * 2, _rows(tl, d), jax.ShapeDtypeStruct((l, d), F32))(h, y)


def s5_layer_bwd(dh_out, saved, p, wf, tag, sink):
    h, hn_p, a_p, sw, ck_re, ck_im, y, yg, t, m = saved
    l = h.shape[0]
    di = p["d_skip"].shape[0]
    dsk = p["d_skip"].reshape(1, di)
    dout_p = _seg_perm(dh_out)
    dm = matmul(dout_p, wf["w_out"], "nt", tag + "_mm_dm")
    sink.mm("w_out", m, dout_p, tag + "_mm_gwout")
    dt, dyg_a, dz, db_glu = s5_gate_bwd(dm, y, t, p["b_glu"], a_p, tag + "_gate_bwd")
    dyg_b = matmul(dt, wf["w_glu"], "nt", tag + "_mm_dyg")
    sink.mm("w_glu", yg, dt, tag + "_mm_gwglu")
    dy = s5_act_bwd(y, dyg_a, dyg_b, tag + "_act_bwd")
    du, dwbr, dwbi, dwcr, dwci, dds, dlr, dli = s5_scan_fused_bwd(a_p, dy, *sw, dsk, ck_re, ck_im, tag + "_scanb")
    da = jnp.concatenate([du, dz], axis=1)
    dhn_p = matmul(da, wf["w_in"], "nt", tag + "_mm_dhn")
    sink.mm("w_in", hn_p, da, tag + "_mm_gwin")
    zero = sink.send()
    dh, dng = rms_bwd(h, p["norm_g"] + zero, _seg_unperm(dhn_p), dh_out, tag + "_rms_bwd")
    ex = lambda m_, r, c: _blockdiag_extract(m_, r, c).reshape(S5_GROUPS, r, c).transpose(0, 2, 1)
    dbb_re, dbb_im = ex(dwbr, S5_GROUP, S5_STATE), ex(dwbi, S5_GROUP, S5_STATE)
    g_c_re, g_c_im = ex(dwcr, S5_STATE, S5_GROUP), ex(dwci, S5_STATE, S5_GROUP)
    dl_re = lane_sum8(dlr).reshape(S5_GROUPS, S5_STATE, 1)
    dl_im = lane_sum8(dli).reshape(S5_GROUPS, S5_STATE, 1)
    gar, gai, gls, gbr, gbi = s5_params_bwd(p["a_re"], p["a_im"], p["log_step"], p["b_re"], p["b_im"],
                                            dl_re, dl_im, dbb_re, dbb_im)
    grads = {"norm_g": dng.reshape(-1), "a_re": gar.reshape(S5_GROUPS, S5_STATE),
             "a_im": gai.reshape(S5_GROUPS, S5_STATE), "log_step": gls.reshape(-1), "b_re": gbr, "b_im": gbi,
             "c_re": g_c_re, "c_im": g_c_im, "d_skip": dds.reshape(-1), "b_glu": db_glu.reshape(-1)}
    return dh, grads


def lane_sum8(t):
    sb, seg, ns = t.shape

    def body(t_ref, o_ref):
        o_ref[...] = jnp.sum(t_ref[...], axis=1, keepdims=True)

    return pl.pallas_call(body, name="s5_seg_sum", out_shape=jax.ShapeDtypeStruct((sb, 1, ns), F32))(t)


MLA_DI = MLA_HEADS * 128
MLA_CQ0 = MLA_DI
MLA_CKV0 = MLA_CQ0 + MLA_Q_RANK
MLA_KR0 = MLA_CKV0 + MLA_KV_RANK
MLA_AW = MLA_KR0 + LANES


def _rot_half(x):
    w = x.shape[-1]
    lane = lax.broadcasted_iota(jnp.int32, x.shape, x.ndim - 1)
    return jnp.where(lane % MLA_ROPE < MLA_ROPE // 2, pltpu.roll(x, w - MLA_ROPE // 2, x.ndim - 1),
                     pltpu.roll(x, MLA_ROPE // 2, x.ndim - 1))


def rope_tables(pos, zero):
    l = pos.shape[0]
    tl = ROW_TILE
    j = np.arange(LANES) % MLA_ROPE % (MLA_ROPE // 2)
    inv_freq = (ROPE_THETA ** (-(2.0 * j) / MLA_ROPE)).astype(np.float32).reshape(1, LANES)
    sign = np.where(np.arange(LANES) % MLA_ROPE < MLA_ROPE // 2, -1.0, 1.0).astype(np.float32).reshape(1, LANES)

    def body(p_ref, f_ref, s_ref, cos_ref, sin_ref):
        ang = p_ref[...].astype(F32) * f_ref[...]
        cos_ref[...] = jnp.cos(ang)
        sin_ref[...] = jnp.sin(ang) * s_ref[...]

    st = jax.ShapeDtypeStruct((l, LANES), F32)
    return _rowcall(body, "rope_tables", l // tl, [_rows(tl, 1), _full((1, LANES)), _full((1, LANES))],
                    [_rows(tl, LANES)] * 2, [st, st])(pos, jnp.asarray(inv_freq), jnp.asarray(sign) + zero)


def _rope(x, cos, sins):
    return x * cos + _rot_half(x) * sins


def _rope_t(dy, cos, sins):
    return dy * cos - sins * _rot_half(dy)


def _rmsn(x):
    r = lax.rsqrt(jnp.mean(x * x, axis=-1, keepdims=True) + NORM_EPS)
    return x * r, r


def mla_pre(a, q_g, kv_g, cos, sins, name):
    l = a.shape[0]
    tl = ROW_TILE

    def body(a_ref, qg_ref, kg_ref, cos_ref, sin_ref, cq_ref, ckv_ref, krs_ref):
        xq, _ = _rmsn(a_ref[:, MLA_CQ0:MLA_CKV0])
        cq_ref[...] = (xq * qg_ref[...]).astype(BF16)
        xk, _ = _rmsn(a_ref[:, MLA_CKV0:MLA_KR0])
        ckv_ref[...] = (xk * kg_ref[...]).astype(BF16)
        kr = a_ref[:, MLA_KR0:MLA_AW]
        kr2 = kr + pltpu.roll(kr, MLA_ROPE, 1)
        kr2 = _rope(kr2, cos_ref[...], sin_ref[...])
        lane = lax.broadcasted_iota(jnp.int32, kr2.shape, 1)
        krs_ref[0] = jnp.where(lane < MLA_ROPE, kr2, 0.0).astype(BF16)
        krs_ref[1] = jnp.where(lane >= MLA_ROPE, kr2, 0.0).astype(BF16)

    st = jax.ShapeDtypeStruct
    return _rowcall(body, name, l // tl,
                    [_rows(tl, MLA_AW), _full((1, MLA_Q_RANK)), _full((1, MLA_KV_RANK)), _rows(tl, LANES), _rows(tl, LANES)],
                    [_rows(tl, MLA_Q_RANK), _rows(tl, MLA_KV_RANK), pl.BlockSpec((2, tl, LANES), lambda i: (0, i, 0))],
                    [st((l, MLA_Q_RANK), BF16), st((l, MLA_KV_RANK), BF16), st((2, l, LANES), BF16)])(
        a, q_g.reshape(1, -1), kv_g.reshape(1, -1), cos, sins)


def mla_rope_q(qr, cos, sins, name):
    l, w = qr.shape
    tl = ROW_TILE

    def body(q_ref, cos_ref, sin_ref, o_ref):
        c, s = cos_ref[...], sin_ref[...]
        for p in range(w // LANES):
            sl = slice(p * LANES, (p + 1) * LANES)
            o_ref[:, sl] = _rope(q_ref[:, sl], c, s).astype(BF16)

    return _rowcall(body, name, l // tl, [_rows(tl, w), _rows(tl, LANES), _rows(tl, LANES)], _rows(tl, w),
                    jax.ShapeDtypeStruct((l, w), BF16))(qr, cos, sins)


ATT_OUT = 512
ATT_IN = 512
ATT_R = ATT_OUT // ATT_IN


def _scores(qn, qr, kn, kr, mask_off, transposed):
    q2 = jnp.concatenate([qn, qr], axis=1)
    k2 = jnp.concatenate([kn, kr], axis=1)
    s = (_dot(k2, q2, DN_NT) if transposed else _dot(q2, k2, DN_NT)) * MLA_SCALE
    if mask_off is None:
        return s
    r = lax.broadcasted_iota(jnp.int32, s.shape, 0)
    c = lax.broadcasted_iota(jnp.int32, s.shape, 1)
    return jnp.where((r <= c + mask_off) if transposed else (c + mask_off <= r), s, NEG_INF)


def _fold(x, op):
    out = x[:, :LANES]
    for t in range(1, x.shape[1] // LANES):
        out = op(out, x[:, t * LANES:(t + 1) * LANES])
    return out


def flash_fwd(qn, qr, kv, krs, name):
    l = qn.shape[0]
    nq = l // ATT_OUT

    def body(qn_ref, qr_ref, kv_ref, kr_ref, o_ref, lse_ref, s_buf):
        qi = pl.program_id(1)
        q_r = qr_ref[...]
        q_n = [qn_ref[:, hh * LANES:(hh + 1) * LANES] for hh in range(2)]

        def block_scores(j, mx, mask_off):
            sl = pl.ds(pl.multiple_of(j * ATT_IN, ATT_IN), ATT_IN)
            out = []
            for hh in range(2):
                s = _scores(q_n[hh], q_r, kv_ref[sl, 2 * hh * LANES:(2 * hh + 1) * LANES], kr_ref[hh, sl, :],
                            mask_off, False)
                s_buf[hh, j] = s
                out.append(jnp.maximum(mx[hh], _fold(s, jnp.maximum)))
            return tuple(out)

        ninf = jnp.full((ATT_OUT, LANES), NEG_INF, F32)
        mx = lax.fori_loop(0, ATT_R * qi, lambda j, c: block_scores(j, c, None), (ninf, ninf))
        for d in range(ATT_R):
            mx = block_scores(ATT_R * qi + d, mx, d * ATT_IN)
        m = [jnp.max(mx[hh], axis=-1, keepdims=True) for hh in range(2)]

        def block_pv(j, carry):
            sl = pl.ds(pl.multiple_of(j * ATT_IN, ATT_IN), ATT_IN)
            out = []
            for hh in range(2):
                ls, acc = carry[hh]
                p = jnp.exp(s_buf[hh, j] - m[hh])
                out.append((ls + _fold(p, jnp.add),
                            acc + _dot(p, kv_ref[sl, (2 * hh + 1) * LANES:(2 * hh + 2) * LANES], DN_NN)))
            return tuple(out)

        z = jnp.zeros((ATT_OUT, LANES), F32)
        res = lax.fori_loop(0, ATT_R * (qi + 1), block_pv, ((z, z), (z, z)))
        for hh in range(2):
            lsum = jnp.sum(res[hh][0], axis=-1, keepdims=True)
            o_ref[:, hh * LANES:(hh + 1) * LANES] = res[hh][1] / lsum
            lse_ref[hh] = m[hh] + jnp.log(lsum)

    st = jax.ShapeDtypeStruct
    return pl.pallas_call(
        body, name=name, grid=(MLA_HEADS // 2, nq),
        in_specs=[pl.BlockSpec((ATT_OUT, 2 * LANES), lambda p, i: (i, p)),
                  pl.BlockSpec((ATT_OUT, LANES), lambda p, i: (i, p)),
                  pl.BlockSpec((l, 4 * LANES), lambda p, i: (0, p)),
                  pl.BlockSpec((2, l, LANES), lambda p, i: (0, 0, 0))],
        out_specs=[pl.BlockSpec((ATT_OUT, 2 * LANES), lambda p, i: (i, p)),
                   pl.BlockSpec((2, ATT_OUT, 1), lambda p, i: (p, i, 0))],
        out_shape=[st((l, MLA_DI), F32), st((MLA_HEADS, l, 1), F32)],
        scratch_shapes=[pltpu.VMEM((2, l // ATT_IN, ATT_OUT, ATT_IN), F32)],
        compiler_params=_cparams(("parallel", "arbitrary")))(qn, qr, kv, krs)


def flash_dkv(qn, qr, kv, krs, do, lse_row, delta_row, name):
    l = qn.shape[0]
    nk = l // ATT_OUT
    nq = l // ATT_IN

    def body(qn_ref, qr_ref, do_ref, lse_ref, dl_ref, kv_ref, kr_ref, dkv_ref, dkr_ref):
        kj = pl.program_id(1)
        lane = lax.broadcasted_iota(jnp.int32, (ATT_OUT, LANES), 1)
        kn = [kv_ref[:, 2 * hh * LANES:(2 * hh + 1) * LANES] for hh in range(2)]
        v = [kv_ref[:, (2 * hh + 1) * LANES:(2 * hh + 2) * LANES] for hh in range(2)]

        def block(i, carry, mask_off):
            sl = pl.ds(pl.multiple_of(i * ATT_IN, ATT_IN), ATT_IN)
            q_r = qr_ref[sl, :]
            out = []
            for hh in range(2):
                dk2, dv = carry[hh]
                hs = slice(hh * LANES, (hh + 1) * LANES)
                q_n, d_o = qn_ref[sl, hs], do_ref[sl, hs]
                s = _scores(q_n, q_r, kn[hh], kr_ref[hh], mask_off, True)
                pt = jnp.exp(s - lse_ref[hh, i])
                dv = dv + _dot(pt, d_o, DN_NN)
                dpt = _dot(v[hh], d_o, DN_NT)
                dst = (pt * (dpt - dl_ref[hh, i]) * MLA_SCALE).astype(BF16)
                out.append((dk2 + _dot(dst, jnp.concatenate([q_n, q_r], axis=1), DN_NN), dv))
            return tuple(out)

        z = jnp.zeros((ATT_OUT, LANES), F32)
        z2 = jnp.zeros((ATT_OUT, 2 * LANES), F32)
        res = ((z2, z), (z2, z))
        for d in range(ATT_R):
            res = block(ATT_R * kj + d, res, d * ATT_IN)
        res = lax.fori_loop(ATT_R * (kj + 1), nq, lambda i, c: block(i, c, None), res)
        for hh in range(2):
            dkv_ref[:, 2 * hh * LANES:(2 * hh + 1) * LANES] = res[hh][0][:, :LANES].astype(BF16)
            dkv_ref[:, (2 * hh + 1) * LANES:(2 * hh + 2) * LANES] = res[hh][1].astype(BF16)
        dkr_ref[0] = jnp.where(lane < MLA_ROPE, res[0][0][:, LANES:], res[1][0][:, LANES:])

    st = jax.ShapeDtypeStruct
    return pl.pallas_call(
        body, name=name, grid=(MLA_HEADS // 2, nk),
        in_specs=[pl.BlockSpec((l, 2 * LANES), lambda p, j: (0, p)),
                  pl.BlockSpec((l, LANES), lambda p, j: (0, p)),
                  pl.BlockSpec((l, 2 * LANES), lambda p, j: (0, p)),
                  pl.BlockSpec((2, nq, 1, ATT_IN), lambda p, j: (p, 0, 0, 0)),
                  pl.BlockSpec((2, nq, 1, ATT_IN), lambda p, j: (p, 0, 0, 0)),
                  pl.BlockSpec((ATT_OUT, 4 * LANES), lambda p, j: (j, p)),
                  pl.BlockSpec((2, ATT_OUT, LANES), lambda p, j: (0, j, 0))],
        out_specs=[pl.BlockSpec((ATT_OUT, 4 * LANES), lambda p, j: (j, p)),
                   pl.BlockSpec((1, ATT_OUT, LANES), lambda p, j: (p, j, 0))],
        out_shape=[st((l, 2 * MLA_DI), BF16), st((MLA_HEADS // 2, l, LANES), F32)],
        compiler_params=_cparams(("parallel", "arbitrary")))(qn, qr, do, lse_row, delta_row, kv, krs)


def flash_dq(qn, qr, kv, krs, do, lse, delta, cos, sins, name):
    l = qn.shape[0]
    nq = l // ATT_OUT

    def body(qn_ref, qr_ref, do_ref, lse_ref, dl_ref, kv_ref, kr_ref, cos_ref, sin_ref, dqn_ref, dqr_ref):
        qi = pl.program_id(1)
        q_r = qr_ref[...]
        q_n = [qn_ref[:, hh * LANES:(hh + 1) * LANES] for hh in range(2)]
        d_o = [do_ref[:, hh * LANES:(hh + 1) * LANES] for hh in range(2)]
        lse_h = [lse_ref[hh] for hh in range(2)]
        dl_h = [dl_ref[hh] for hh in range(2)]

        def block(j, carry, mask_off):
            sl = pl.ds(pl.multiple_of(j * ATT_IN, ATT_IN), ATT_IN)
            dq2 = list(carry)
            for hh in range(2):
                kn = kv_ref[sl, 2 * hh * LANES:(2 * hh + 1) * LANES]
                v = kv_ref[sl, (2 * hh + 1) * LANES:(2 * hh + 2) * LANES]
                kr = kr_ref[hh, sl, :]
                s = _scores(q_n[hh], q_r, kn, kr, mask_off, False)
                pr = jnp.exp(s - lse_h[hh])
                dp = _dot(d_o[hh], v, DN_NT)
                ds = (pr * (dp - dl_h[hh]) * MLA_SCALE).astype(BF16)
                dq2[hh] = dq2[hh] + _dot(ds, jnp.concatenate([kn, kr], axis=1), DN_NN)
            return tuple(dq2)

        z2 = jnp.zeros((ATT_OUT, 2 * LANES), F32)
        res = lax.fori_loop(0, ATT_R * qi, lambda j, c: block(j, c, None), (z2, z2))
        for d in range(ATT_R):
            res = block(ATT_R * qi + d, res, d * ATT_IN)
        dqn_ref[:, 0:LANES] = res[0][:, :LANES].astype(BF16)
        dqn_ref[:, LANES:2 * LANES] = res[1][:, :LANES].astype(BF16)
        dqr = res[0][:, LANES:] + res[1][:, LANES:]
        dqr_ref[...] = _rope_t(dqr, cos_ref[...], sin_ref[...]).astype(BF16)

    st = jax.ShapeDtypeStruct
    return pl.pallas_call(
        body, name=name, grid=(MLA_HEADS // 2, nq),
        in_specs=[pl.BlockSpec((ATT_OUT, 2 * LANES), lambda p, i: (i, p)),
                  pl.BlockSpec((ATT_OUT, LANES), lambda p, i: (i, p)),
                  pl.BlockSpec((ATT_OUT, 2 * LANES), lambda p, i: (i, p)),
                  pl.BlockSpec((2, ATT_OUT, 1), lambda p, i: (p, i, 0)),
                  pl.BlockSpec((2, ATT_OUT, 1), lambda p, i: (p, i, 0)),
                  pl.BlockSpec((l, 4 * LANES), lambda p, i: (0, p)),
                  pl.BlockSpec((2, l, LANES), lambda p, i: (0, 0, 0)),
                  pl.BlockSpec((ATT_OUT, LANES), lambda p, i: (i, 0)),
                  pl.BlockSpec((ATT_OUT, LANES), lambda p, i: (i, 0))],
        out_specs=[pl.BlockSpec((ATT_OUT, 2 * LANES), lambda p, i: (i, p)),
                   pl.BlockSpec((ATT_OUT, LANES), lambda p, i: (i, p))],
        out_shape=[st((l, MLA_DI), BF16), st((l, MLA_HEADS * MLA_ROPE), BF16)],
        compiler_params=_cparams(("parallel", "arbitrary")))(qn, qr, do, lse, delta, kv, krs, cos, sins)


def mla_gate_fwd(o, a, name):
    l = o.shape[0]
    tl = ROW_TILE

    def body(o_ref, z_ref, m_ref):
        m_ref[...] = (o_ref[...] * _silu(z_ref[...])).astype(BF16)

    return _rowcall(body, name, l // tl, [_rows(tl, MLA_DI), _rows(tl, MLA_DI)], _rows(tl, MLA_DI),
                    jax.ShapeDtypeStruct((l, MLA_DI), BF16))(o, a)


def mla_gate_bwd(dm, o, a, name):
    l = o.shape[0]
    tl = ROW_TILE

    def body(dm_ref, o_ref, z_ref, do_ref, dz_ref, dl_ref):
        dmv, ov, z = dm_ref[...], o_ref[...], z_ref[...]
        sz, gz = _silu_both(z)
        d_o = dmv * sz
        do_ref[...] = d_o.astype(BF16)
        dz_ref[...] = (dmv * ov * gz).astype(BF16)
        pr = d_o * ov
        for h in range(MLA_HEADS):
            dl_ref[h] = jnp.sum(pr[:, h * LANES:(h + 1) * LANES], axis=1, keepdims=True)

    st = jax.ShapeDtypeStruct
    return _rowcall(body, name, l // tl, [_rows(tl, MLA_DI)] * 3,
                    [_rows(tl, MLA_DI), _rows(tl, MLA_DI), pl.BlockSpec((MLA_HEADS, tl, 1), lambda i: (0, i, 0))],
                    [st((l, MLA_DI), BF16), st((l, MLA_DI), BF16), st((MLA_HEADS, l, 1), F32)])(dm, o, a)


def mla_post(a, dcqn, dckvn, dkr_pairs, dz, q_g, kv_g, cos, sins, name):
    l = a.shape[0]
    tl = ROW_TILE
    npair = MLA_HEADS // 2

    def norm_bwd(x, g, dy):
        xhat, r = _rmsn(x)
        dxh = dy * g
        return r * (dxh - xhat * jnp.mean(dxh * xhat, axis=-1, keepdims=True)), jnp.sum(dy * xhat, axis=0, keepdims=True)

    def body(a_ref, dq_ref, dk_ref, dkr_ref, dz_ref, qg_ref, kg_ref, cos_ref, sin_ref, da_ref, dqg_ref, dkg_ref):
        i = pl.program_id(0)
        da_ref[:, 0:MLA_DI] = dz_ref[...]
        dcq, dqg = norm_bwd(a_ref[:, MLA_CQ0:MLA_CKV0], qg_ref[...], dq_ref[...])
        da_ref[:, MLA_CQ0:MLA_CKV0] = dcq.astype(BF16)
        dckv, dkg = norm_bwd(a_ref[:, MLA_CKV0:MLA_KR0], kg_ref[...], dk_ref[...])
        da_ref[:, MLA_CKV0:MLA_KR0] = dckv.astype(BF16)
        dk2 = dkr_ref[0]
        for p in range(1, npair):
            dk2 = dk2 + dkr_ref[p]
        dk2 = _rope_t(dk2, cos_ref[...], sin_ref[...])
        dk2 = dk2 + pltpu.roll(dk2, MLA_ROPE, 1)
        lane = lax.broadcasted_iota(jnp.int32, dk2.shape, 1)
        da_ref[:, MLA_KR0:MLA_AW] = jnp.where(lane < MLA_ROPE, dk2, 0.0).astype(BF16)
        _acc(dqg_ref, dqg, i)
        _acc(dkg_ref, dkg, i)

    st = jax.ShapeDtypeStruct
    return _rowcall(body, name, l // tl,
                    [_rows(tl, MLA_AW), _rows(tl, MLA_Q_RANK), _rows(tl, MLA_KV_RANK),
                     pl.BlockSpec((npair, tl, LANES), lambda i: (0, i, 0)), _rows(tl, MLA_DI),
                     _full((1, MLA_Q_RANK)), _full((1, MLA_KV_RANK)), _rows(tl, LANES), _rows(tl, LANES)],
                    [_rows(tl, MLA_AW), _full((1, MLA_Q_RANK)), _full((1, MLA_KV_RANK))],
                    [st((l, MLA_AW), BF16), st((1, MLA_Q_RANK), F32), st((1, MLA_KV_RANK), F32)])(
        a, dcqn, dckvn, dkr_pairs, dz, q_g.reshape(1, -1), kv_g.reshape(1, -1), cos, sins)


def _mla_w_in_perm(w):
    r = MLA_Q_RANK + MLA_KV_RANK + MLA_ROPE
    pad = jnp.zeros(w.shape[:-1] + (MLA_AW - MLA_KR0 - MLA_ROPE,), w.dtype)
    return jnp.concatenate([w[..., r:], w[..., :r], pad], axis=-1)


def _mla_w_in_unperm(g):
    r = MLA_Q_RANK + MLA_KV_RANK + MLA_ROPE
    return jnp.concatenate([g[..., MLA_DI:MLA_DI + r], g[..., :MLA_DI]], axis=-1)


def _mla_w_uq_split(w):
    k = w.shape[0]
    w3 = w.reshape(k, MLA_HEADS, MLA_NOPE + MLA_ROPE)
    return w3[:, :, :MLA_NOPE].reshape(k, MLA_HEADS * MLA_NOPE), w3[:, :, MLA_NOPE:].reshape(k, MLA_HEADS * MLA_ROPE)


def _mla_w_uq_merge(gn, gr):
    k = gn.shape[0]
    return jnp.concatenate([gn.reshape(k, MLA_HEADS, MLA_NOPE), gr.reshape(k, MLA_HEADS, MLA_ROPE)], axis=2).reshape(k, -1)


def mla_layer_fwd(h, p, wf, cos, sins, tag):
    hn = rms_fwd(h, p["norm_g"], tag + "_rms")
    w_in = _mla_w_in_perm(wf["w_in"])
    w_uq_n, w_uq_r = _mla_w_uq_split(wf["w_uq"])
    a = matmul(hn, w_in, "nn", tag + "_mm_in")
    cqn, ckvn, krs = mla_pre(a, p["q_norm_g"], p["kv_norm_g"], cos, sins, tag + "_pre")
    qn = matmul(cqn, w_uq_n, "nn", tag + "_mm_qn", out_dtype=BF16)
    qr_raw = matmul(cqn, w_uq_r, "nn", tag + "_mm_qr")
    qr = mla_rope_q(qr_raw, cos, sins, tag + "_rope_q")
    kv = matmul(ckvn, wf["w_ukv"], "nn", tag + "_mm_kv", out_dtype=BF16)
    o, lse = flash_fwd(qn, qr, kv, krs, tag + "_flash")
    m = mla_gate_fwd(o, a, tag + "_gate")
    h_out = matmul(m, wf["w_out"], "nn", tag + "_mm_out", add=h)
    return h_out, (h, hn, a, cqn, ckvn, krs, qn, qr, kv, o, lse, m, w_in, w_uq_n, w_uq_r)


def mla_layer_bwd(dh_out, saved, p, wf, cos, sins, tag, sink):
    h, hn, a, cqn, ckvn, krs, qn, qr, kv, o, lse, m, w_in, w_uq_n, w_uq_r = saved
    l = h.shape[0]
    dm = matmul(dh_out, wf["w_out"], "nt", tag + "_mm_dm")
    sink.mm("w_out", m, dh_out, tag + "_mm_gwout")
    do, dz, delta = mla_gate_bwd(dm, o, a, tag + "_gate_bwd")
    lse_row = lse.reshape(MLA_HEADS, l // ATT_IN, 1, ATT_IN)
    delta_row = delta.reshape(MLA_HEADS, l // ATT_IN, 1, ATT_IN)
    dkv, dkr_pairs = flash_dkv(qn, qr, kv, krs, do, lse_row, delta_row, tag + "_flash_dkv")
    dqn, dqr = flash_dq(qn, qr, kv, krs, do, lse, delta, cos, sins, tag + "_flash_dq")
    dcqn = matmul(dqn, w_uq_n, "nt", tag + "_mm_dcq_n")
    dcqn = matmul(dqr, w_uq_r, "nt", tag + "_mm_dcq_r", add=dcqn)
    g_uq_n = matmul(cqn, dqn, "tn", tag + "_mm_guq_n")
    g_uq_r = matmul(cqn, dqr, "tn", tag + "_mm_guq_r")
    dckvn = matmul(dkv, wf["w_ukv"], "nt", tag + "_mm_dckv")
    sink.mm("w_ukv", ckvn, dkv, tag + "_mm_gukv")
    da, dqg, dkg = mla_post(a, dcqn, dckvn, dkr_pairs, dz, p["q_norm_g"], p["kv_norm_g"], cos, sins, tag + "_post")
    dhn = matmul(da, w_in, "nt", tag + "_mm_dhn")
    g_w_in = matmul(hn, da, "tn", tag + "_mm_gwin")
    sink.put("w_uq", _mla_w_uq_merge(g_uq_n, g_uq_r))
    sink.put("w_in", _mla_w_in_unperm(g_w_in))
    zero = sink.send()
    dh, dng = rms_bwd(h, p["norm_g"] + zero, dhn, dh_out, tag + "_rms_bwd")
    grads = {"norm_g": dng.reshape(-1), "q_norm_g": dqg.reshape(-1), "kv_norm_g": dkg.reshape(-1)}
    return dh, grads


ANY = pl.BlockSpec(memory_space=pl.ANY)


def _me():
    return lax.axis_index("x"), lax.axis_index("y"), lax.axis_index("c")


def _chip():
    return 2 * lax.axis_index("x") + lax.axis_index("y")


def _other_chips(x, y):
    return [(1 - x, y), (x, 1 - y), (1 - x, 1 - y)]


def _rcopy(src, dst, ssem, rsem, dev):
    return pltpu.make_async_remote_copy(src_ref=src, dst_ref=dst, send_sem=ssem, recv_sem=rsem,
                                        device_id=dev, device_id_type=MESH)


def _half(ref, c, hf):
    return ref.at[pl.ds(c * hf, hf), :]


HBM = pl.BlockSpec(memory_space=pltpu.HBM)
SEM = pl.BlockSpec(memory_space=pltpu.SEMAPHORE)
SPLIT_EFFECT = pltpu.SideEffectType.DATAFLOW_SIDE_EFFECTING


def gather_start(wb, after, name):
    nr, w = wb.shape
    hf = nr // 2

    def body(w_ref, land_ref, after_ref, ssem, rsem, w_thru, land_thru, token):
        x, y, c = _me()
        k = 2 * x + y
        for j, (cx, cy) in enumerate(_other_chips(x, y)):
            _rcopy(_half(w_ref, c, hf), _half(land_ref.at[k], c, hf), ssem.at[j], rsem.at[j], (cx, cy, c)).start()
        token[...] = jnp.zeros_like(token)

    land = lax.empty((N_CHIPS, nr, w), wb.dtype)
    return pl.pallas_call(
        body, name=name,
        out_shape=(pltpu.SemaphoreType.DMA((3,)), pltpu.SemaphoreType.DMA((3,)), pltpu.HBM(wb.shape, wb.dtype),
                   pltpu.HBM(land.shape, land.dtype), jax.ShapeDtypeStruct((8, LANES), F32)),
        in_specs=(HBM, HBM, ANY), out_specs=(SEM, SEM, HBM, HBM, pl.BlockSpec(memory_space=pltpu.VMEM)),
        input_output_aliases={0: 2, 1: 3},
        compiler_params=pltpu.CompilerParams(has_side_effects=SPLIT_EFFECT))(
        pltpu.with_memory_space_constraint(wb, pltpu.HBM), pltpu.with_memory_space_constraint(land, pltpu.HBM), after)


def gather_wait(ssem, rsem, w_thru, land_thru, after, name):
    nr, w = w_thru.shape
    hf = nr // 2

    def body(w_ref, land_ref, ssem_ref, rsem_ref, after_ref, w_dead, got_ref):
        x, y, c = _me()
        for j, (cx, cy) in enumerate(_other_chips(x, y)):
            cp = _rcopy(_half(w_ref, c, hf), _half(land_ref.at[2 * cx + cy], c, hf), ssem_ref.at[j], rsem_ref.at[j],
                        (cx, cy, c))
            cp.wait_send()
            cp.wait_recv()

    return pl.pallas_call(
        body, name=name, out_shape=(pltpu.HBM(w_thru.shape, w_thru.dtype), pltpu.HBM(land_thru.shape, land_thru.dtype)),
        in_specs=(HBM, HBM, SEM, SEM, ANY), out_specs=(HBM, HBM), input_output_aliases={0: 0, 1: 1},
        compiler_params=pltpu.CompilerParams(has_side_effects=SPLIT_EFFECT))(w_thru, land_thru, ssem, rsem, after)[1]


def gather_handover(land, wb, name):
    _, nr, w = land.shape
    hf = nr // 2

    def body(l_ref, o_ref, ssem, rsem):
        x, y, c = _me()
        chips = _other_chips(x, y)
        sends = []
        for j, (cx, cy) in enumerate(chips):
            region = _half(o_ref.at[2 * cx + cy], c, hf)
            sends.append(_rcopy(region, region, ssem.at[j], rsem.at[j], (x, y, 1 - c)))
            sends[-1].start()
        for j, (cx, cy) in enumerate(chips):
            region = _half(o_ref.at[2 * cx + cy], 1 - c, hf)
            _rcopy(region, region, ssem.at[j], rsem.at[j], (x, y, 1 - c)).wait_recv()
        for cp in sends:
            cp.wait_send()

    out = pl.pallas_call(
        body, name=name, in_specs=[ANY], out_specs=ANY, input_output_aliases={0: 0},
        out_shape=jax.ShapeDtypeStruct(land.shape, land.dtype),
        scratch_shapes=[pltpu.SemaphoreType.DMA((3,)), pltpu.SemaphoreType.DMA((3,))])(land)
    return lax.dynamic_update_slice(out, wb[None], (_chip(), 0, 0))


def pair_start(p, after, name):
    _, nr, w = p.shape
    hf = nr // 2

    def body(p_ref, land_ref, after_ref, ssem, rsem, p_thru, land_thru, token):
        x, y, c = _me()
        _rcopy(p_ref.at[:, pl.ds((1 - c) * hf, hf), :], land_ref, ssem, rsem, (x, y, 1 - c)).start()
        token[...] = jnp.zeros_like(token)

    land = lax.empty((N_CHIPS, hf, w), p.dtype)
    return pl.pallas_call(
        body, name=name,
        out_shape=(pltpu.SemaphoreType.DMA(()), pltpu.SemaphoreType.DMA(()), pltpu.HBM(p.shape, p.dtype),
                   pltpu.HBM(land.shape, land.dtype), jax.ShapeDtypeStruct((8, LANES), F32)),
        in_specs=(HBM, HBM, ANY), out_specs=(SEM, SEM, HBM, HBM, pl.BlockSpec(memory_space=pltpu.VMEM)),
        input_output_aliases={0: 2, 1: 3},
        compiler_params=pltpu.CompilerParams(has_side_effects=SPLIT_EFFECT))(
        pltpu.with_memory_space_constraint(p, pltpu.HBM), pltpu.with_memory_space_constraint(land, pltpu.HBM), after)


def pair_wait(ssem, rsem, p_thru, land_thru, after, name):
    hf = land_thru.shape[1]

    def body(p_ref, land_ref, ssem_ref, rsem_ref, after_ref, p_out, got_ref):
        x, y, c = _me()
        cp = _rcopy(p_ref.at[:, pl.ds((1 - c) * hf, hf), :], land_ref, ssem_ref, rsem_ref, (x, y, 1 - c))
        cp.wait_send()
        cp.wait_recv()

    return pl.pallas_call(
        body, name=name, out_shape=(pltpu.HBM(p_thru.shape, p_thru.dtype), pltpu.HBM(land_thru.shape, land_thru.dtype)),
        in_specs=(HBM, HBM, SEM, SEM, ANY), out_specs=(HBM, HBM), input_output_aliases={0: 0, 1: 1},
        compiler_params=pltpu.CompilerParams(has_side_effects=SPLIT_EFFECT))(p_thru, land_thru, ssem, rsem, after)


def reduce_start(t, after, name):
    def body(t_ref, land_ref, after_ref, ssem, rsem, t_thru, land_thru, token):
        x, y, c = _me()
        k = 2 * x + y
        for j, (cx, cy) in enumerate(_other_chips(x, y)):
            _rcopy(t_ref.at[2 * cx + cy], land_ref.at[k], ssem.at[j], rsem.at[j], (cx, cy, c)).start()
        token[...] = jnp.zeros_like(token)

    land = lax.empty(t.shape, t.dtype)
    return pl.pallas_call(
        body, name=name,
        out_shape=(pltpu.SemaphoreType.DMA((3,)), pltpu.SemaphoreType.DMA((3,)), pltpu.HBM(t.shape, t.dtype),
                   pltpu.HBM(t.shape, t.dtype), jax.ShapeDtypeStruct((8, LANES), F32)),
        in_specs=(HBM, HBM, ANY), out_specs=(SEM, SEM, HBM, HBM, pl.BlockSpec(memory_space=pltpu.VMEM)),
        input_output_aliases={0: 2, 1: 3},
        compiler_params=pltpu.CompilerParams(has_side_effects=SPLIT_EFFECT))(
        pltpu.with_memory_space_constraint(t, pltpu.HBM), pltpu.with_memory_space_constraint(land, pltpu.HBM), after)


def bcast_start(g, after, name):
    def body(g_ref, land_ref, after_ref, ssem, rsem, g_thru, land_thru, token):
        x, y, c = _me()
        k = 2 * x + y
        for j, (cx, cy) in enumerate(_other_chips(x, y)):
            _rcopy(g_ref, land_ref.at[k], ssem.at[j], rsem.at[j], (cx, cy, c)).start()
        token[...] = jnp.zeros_like(token)

    land = lax.empty((N_CHIPS,) + g.shape, g.dtype)
    return pl.pallas_call(
        body, name=name,
        out_shape=(pltpu.SemaphoreType.DMA((3,)), pltpu.SemaphoreType.DMA((3,)), pltpu.HBM(g.shape, g.dtype),
                   pltpu.HBM(land.shape, land.dtype), jax.ShapeDtypeStruct((8, LANES), F32)),
        in_specs=(HBM, HBM, ANY), out_specs=(SEM, SEM, HBM, HBM, pl.BlockSpec(memory_space=pltpu.VMEM)),
        input_output_aliases={0: 2, 1: 3},
        compiler_params=pltpu.CompilerParams(has_side_effects=SPLIT_EFFECT))(
        pltpu.with_memory_space_constraint(g, pltpu.HBM), pltpu.with_memory_space_constraint(land, pltpu.HBM), after)


def bcast_wait(ssem, rsem, g_thru, land_thru, after, name):
    def body(g_ref, land_ref, ssem_ref, rsem_ref, after_ref, g_out, got_ref):
        x, y, c = _me()
        for j, (cx, cy) in enumerate(_other_chips(x, y)):
            cp = _rcopy(g_ref, land_ref.at[2 * cx + cy], ssem_ref.at[j], rsem_ref.at[j], (cx, cy, c))
            cp.wait_send()
            cp.wait_recv()

    g, land = pl.pallas_call(
        body, name=name, out_shape=(pltpu.HBM(g_thru.shape, g_thru.dtype), pltpu.HBM(land_thru.shape, land_thru.dtype)),
        in_specs=(HBM, HBM, SEM, SEM, ANY), out_specs=(HBM, HBM), input_output_aliases={0: 0, 1: 1},
        compiler_params=pltpu.CompilerParams(has_side_effects=SPLIT_EFFECT))(g_thru, land_thru, ssem, rsem, after)
    return lax.dynamic_update_slice(land, g[None], (_chip(), 0, 0))


def reduce_wait(ssem, rsem, t_thru, land_thru, after, name):
    def body(t_ref, land_ref, ssem_ref, rsem_ref, after_ref, t_out, got_ref):
        x, y, c = _me()
        k = 2 * x + y
        for j, (cx, cy) in enumerate(_other_chips(x, y)):
            cp = _rcopy(t_ref.at[k], land_ref.at[2 * cx + cy], ssem_ref.at[j], rsem_ref.at[j], (cx, cy, c))
            cp.wait_send()
            cp.wait_recv()

    return pl.pallas_call(
        body, name=name, out_shape=(pltpu.HBM(t_thru.shape, t_thru.dtype), pltpu.HBM(land_thru.shape, land_thru.dtype)),
        in_specs=(HBM, HBM, SEM, SEM, ANY), out_specs=(HBM, HBM), input_output_aliases={0: 0, 1: 1},
        compiler_params=pltpu.CompilerParams(has_side_effects=SPLIT_EFFECT))(t_thru, land_thru, ssem, rsem, after)


def grads_to_sibling(ps, name="grads_to_sibling"):
    n = len(ps)

    def body(*refs):
        p_refs, o_refs, ssem, rsem = refs[:n], refs[n:2 * n], refs[2 * n], refs[2 * n + 1]
        x, y, c = _me()
        cps = []
        for a in range(n):
            hf = ps[a].shape[1] // 2
            cps.append(_rcopy(p_refs[a].at[:, pl.ds((1 - c) * hf, hf), :], o_refs[a], ssem.at[a], rsem.at[a],
                              (x, y, 1 - c)))
        for cp in cps:
            cp.start()
        for cp in cps:
            cp.wait()

    return pl.pallas_call(
        body, name=name, in_specs=[ANY] * n, out_specs=[ANY] * n,
        out_shape=[jax.ShapeDtypeStruct((N_CHIPS, p.shape[1] // 2, p.shape[2]), p.dtype) for p in ps],
        scratch_shapes=[pltpu.SemaphoreType.DMA((n,)), pltpu.SemaphoreType.DMA((n,))])(*ps)


def pair_sum(p, ra, out_dtype, name):
    _, nr, w = p.shape
    hf = nr // 2
    tr = _pick_rows(hf, cap=max(512, SUM_BLOCK_BYTES // (4 * w)))
    nb = hf // tr

    def body(c_ref, p_ref, r_ref, o_ref):
        o_ref[...] = (p_ref[...] + r_ref[...]).astype(out_dtype)

    c = lax.axis_index("c").astype(jnp.int32).reshape(1)
    return pl.pallas_call(
        body, name=name,
        grid_spec=pltpu.PrefetchScalarGridSpec(
            num_scalar_prefetch=1, grid=(N_CHIPS, nb),
            in_specs=[pl.BlockSpec((1, tr, w), lambda k, i, c_ref: (k, c_ref[0] * nb + i, 0)),
                      pl.BlockSpec((1, tr, w), lambda k, i, c_ref: (k, i, 0))],
            out_specs=pl.BlockSpec((1, tr, w), lambda k, i, c_ref: (k, i, 0))),
        out_shape=jax.ShapeDtypeStruct((N_CHIPS, hf, w), out_dtype),
        compiler_params=_cparams(("parallel", "parallel")))(c, p, ra)


def grads_across_chips(ts):
    n = len(ts)

    def body(*refs):
        t_refs, o_refs, ssem, rsem = refs[:n], refs[n:2 * n], refs[2 * n], refs[2 * n + 1]
        x, y, c = _me()
        k = 2 * x + y
        chips = _other_chips(x, y)
        sends = [_rcopy(t_refs[a].at[2 * cx + cy], o_refs[a].at[k], ssem.at[3 * a + j], rsem.at[3 * a + j], (cx, cy, c))
                 for a in range(n) for j, (cx, cy) in enumerate(chips)]
        for cp in sends:
            cp.start()
        for a in range(n):
            for j, (cx, cy) in enumerate(chips):
                _rcopy(t_refs[a].at[k], o_refs[a].at[2 * cx + cy], ssem.at[3 * a + j], rsem.at[3 * a + j],
                       (cx, cy, c)).wait_recv()
        for cp in sends:
            cp.wait_send()

    return pl.pallas_call(
        body, name="grads_across_chips", in_specs=[ANY] * n, out_specs=[ANY] * n,
        out_shape=[jax.ShapeDtypeStruct(t.shape, t.dtype) for t in ts],
        scratch_shapes=[pltpu.SemaphoreType.DMA((3 * n,)), pltpu.SemaphoreType.DMA((3 * n,))])(*ts)


def chip_sum(t, rb, name):
    _, hf, w = rb.shape
    tr = _pick_rows(hf, cap=max(512, SUM_BLOCK_BYTES // (4 * w)))
    nb = hf // tr

    def body(kc_ref, t_ref, r_ref, o_ref):
        k = kc_ref[0]
        acc = jnp.where(k == 0, t_ref[0], r_ref[0]).astype(F32)
        for j in range(1, N_CHIPS):
            acc = acc + jnp.where(k == j, t_ref[0], r_ref[j]).astype(F32)
        o_ref[...] = acc

    kc = jnp.stack([_chip(), lax.axis_index("c")]).astype(jnp.int32)
    return pl.pallas_call(
        body, name=name,
        grid_spec=pltpu.PrefetchScalarGridSpec(
            num_scalar_prefetch=1, grid=(nb,),
            in_specs=[pl.BlockSpec((1, tr, w), lambda i, kc_ref: (kc_ref[0], i, 0)),
                      pl.BlockSpec((N_CHIPS, tr, w), lambda i, kc_ref: (0, i, 0))],
            out_specs=pl.BlockSpec((tr, w), lambda i, kc_ref: (kc_ref[1] * nb + i, 0))),
        out_shape=jax.ShapeDtypeStruct((2 * hf, w), F32), compiler_params=_cparams(("parallel",)))(kc, t, rb)


def reduced_to_sibling(gs):
    n = len(gs)

    def body(*refs):
        o_refs, ssem, rsem = refs[n:2 * n], refs[2 * n], refs[2 * n + 1]
        x, y, c = _me()
        cps = []
        for a in range(n):
            hf = gs[a].shape[0] // 2
            cps.append(_rcopy(_half(o_refs[a], c, hf), _half(o_refs[a], c, hf), ssem.at[a], rsem.at[a], (x, y, 1 - c)))
        for cp in cps:
            cp.start()
        for a in range(n):
            hf = gs[a].shape[0] // 2
            _rcopy(_half(o_refs[a], c, hf), _half(o_refs[a], 1 - c, hf), ssem.at[a], rsem.at[a],
                   (x, y, 1 - c)).wait_recv()
        for cp in cps:
            cp.wait_send()

    return pl.pallas_call(
        body, name="reduced_to_sibling", in_specs=[ANY] * n, out_specs=[ANY] * n,
        input_output_aliases={a: a for a in range(n)},
        out_shape=[jax.ShapeDtypeStruct(g.shape, g.dtype) for g in gs],
        scratch_shapes=[pltpu.SemaphoreType.DMA((n,)), pltpu.SemaphoreType.DMA((n,))])(*gs)


def _adamw_step(w_ref, g_ref, m_ref, v_ref, d_ref, nm_ref, nv_ref):
    bc1 = 1.0 - ADAM_B1 ** ADAM_STEP
    bc2 = 1.0 - ADAM_B2 ** ADAM_STEP
    gv = g_ref[...]
    nm = ADAM_B1 * m_ref[...] + (1.0 - ADAM_B1) * gv
    nv = ADAM_B2 * v_ref[...] + (1.0 - ADAM_B2) * (gv * gv)
    nm_ref[...] = nm
    nv_ref[...] = nv
    d_ref[...] = -ADAM_LR * ((nm / bc1) / (jnp.sqrt(nv / bc2) + ADAM_EPS) + ADAM_WD * w_ref[...])


def adamw_packed(w, g_buf, r0, m, v, name):
    r, c = w.shape
    tr = _tile_rows(r, r0, (1024, 512, 384, 256, 128))

    def body(w_ref, g_ref, m_ref, v_ref, go_ref, d_ref, nm_ref, nv_ref):
        go_ref[...] = g_ref[...]
        _adamw_step(w_ref, g_ref, m_ref, v_ref, d_ref, nm_ref, nv_ref)

    own = pl.BlockSpec((tr, CHUNK_W), lambda i, j: (i, j))
    packed = pl.BlockSpec((tr, CHUNK_W), lambda i, j: ((r0 + j * r) // tr + i, 0))
    st = jax.ShapeDtypeStruct((r, c), F32)
    return pl.pallas_call(body, name=name, grid=(r // tr, c // CHUNK_W), in_specs=[own, packed, own, own],
                          out_specs=[own] * 4, out_shape=[st] * 4,
                          compiler_params=_cparams(("parallel", "parallel")))(w, g_buf, m, v)


def adamw(w, g, m, v, name):
    r, wd = w.shape
    tr = _pick_rows(r, cap=max(16, ADAMW_BLOCK_BYTES // (4 * wd)))
    body = functools.partial(_adamw_step)

    spec = pl.BlockSpec((tr, wd), lambda i: (i, 0))
    st = jax.ShapeDtypeStruct((r, wd), F32)
    return pl.pallas_call(body, name=name, grid=(r // tr,), in_specs=[spec] * 4, out_specs=[spec] * 3,
                          out_shape=[st, st, st], compiler_params=_cparams(("parallel",)))(w, g, m, v)


LAYER_KINDS = ("gmlp", "s5", "mla", "gmlp")
PARAMS = {
    "gmlp": ("norm_g", "w_in", "ln_g", "ln_b", "w_s", "b_s", "w_out"),
    "s5": ("norm_g", "w_in", "a_re", "a_im", "log_step", "b_re", "b_im", "c_re", "c_im", "d_skip", "w_glu", "b_glu", "w_out"),
    "mla": ("norm_g", "w_in", "q_norm_g", "w_uq", "kv_norm_g", "w_ukv", "w_out"),
}
COL_SHARDED = ("w_in", "w_uq", "w_ukv")
ROW_SHARDED = ("w_out", "w_glu")
WEIGHT_NAMES = [("l%d_" % i) + n for i, kind in enumerate(LAYER_KINDS) for n in PARAMS[kind]] + ["final_norm_g"]


def _is_big(name):
    return name.split("_", 1)[1] in COL_SHARDED + ROW_SHARDED


BIG = [n for n in WEIGHT_NAMES if _is_big(n)]
SMALL = [n for n in WEIGHT_NAMES if not _is_big(n)]


def _pack_rows(blocks):
    return jnp.concatenate([b.reshape(-1, PACK_W) for b in blocks], axis=0)


def _shard_major(wn, full, width):
    r, c = full.shape
    if wn in COL_SHARDED:
        t = full.reshape(r, N_CHIPS, c // N_CHIPS).transpose(1, 0, 2)
    else:
        t = full.reshape(N_CHIPS, r // N_CHIPS, c)
    return t.reshape(N_CHIPS, -1, width)


def _from_shard_major(name, t, block_shape):
    r, c = block_shape
    if name.split("_", 1)[1] in COL_SHARDED:
        return t.reshape(N_CHIPS, r, c).transpose(1, 0, 2).reshape(r, N_CHIPS * c)
    return t.reshape(N_CHIPS * r, c)


class BigGradSink:
    ORDER = ("w_out", "w_glu", "w_ukv", "w_uq", "w_in")
    ROW_MAJOR = {2: ("w_uq", "w_in")}

    def __init__(self, layer, block_shapes):
        self.layer = layer
        self.regions = {}
        r0 = 0
        for wn in self.ORDER:
            if wn in block_shapes:
                shape = block_shapes[wn]
                self.regions[wn] = (r0, shape, wn not in self.ROW_MAJOR.get(layer, ()))
                r0 += shape[0] * shape[1] // CHUNK_W
        self.buf = lax.empty((N_CHIPS, r0, CHUNK_W), F32)
        self.flight = None

    def mm(self, wn, a, b, name):
        r0, _, direct = self.regions[wn]
        assert direct
        self.buf = matmul_tn_packed(a, b, self.buf, r0, wn in COL_SHARDED, name)

    def put(self, wn, full):
        r0, _, direct = self.regions[wn]
        assert not direct
        piece = _shard_major(wn, full, CHUNK_W)
        self.buf = lax.dynamic_update_slice(self.buf, piece, (0, r0, 0))

    def send(self):
        self.pair = pair_start(self.buf, jnp.zeros((8, LANES), F32), "pair_l%d_start" % self.layer)
        return self.pair[4][0, 0]

    def forward(self, after):
        i = self.layer
        p, sib = pair_wait(*self.pair[:4], after, "pair_l%d_wait" % i)
        t = pair_sum(p, sib, BF16, "pair_sum_l%d" % i)
        self.flight = reduce_start(t, sib, "reduce_l%d_start" % i)
        return self.flight[4][0, 0]


def _small_pack(arrs, total_padded):
    flat = jnp.concatenate([a.reshape(-1) for a in arrs])
    return jnp.pad(flat, (0, total_padded - flat.shape[0]))


def kernel(x, positions, l0_norm_g, l0_w_in, l0_ln_g, l0_ln_b, l0_w_s, l0_b_s, l0_w_out, l1_norm_g, l1_w_in, l1_a_re, l1_a_im, l1_log_step, l1_b_re, l1_b_im, l1_c_re, l1_c_im, l1_d_skip, l1_w_glu, l1_b_glu, l1_w_out, l2_norm_g, l2_w_in, l2_q_norm_g, l2_w_uq, l2_kv_norm_g, l2_w_ukv, l2_w_out, l3_norm_g, l3_w_in, l3_ln_g, l3_ln_b, l3_w_s, l3_b_s, l3_w_out, final_norm_g, loss_target, m_l0_norm_g, m_l0_w_in, m_l0_ln_g, m_l0_ln_b, m_l0_w_s, m_l0_b_s, m_l0_w_out, m_l1_norm_g, m_l1_w_in, m_l1_a_re, m_l1_a_im, m_l1_log_step, m_l1_b_re, m_l1_b_im, m_l1_c_re, m_l1_c_im, m_l1_d_skip, m_l1_w_glu, m_l1_b_glu, m_l1_w_out, m_l2_norm_g, m_l2_w_in, m_l2_q_norm_g, m_l2_w_uq, m_l2_kv_norm_g, m_l2_w_ukv, m_l2_w_out, m_l3_norm_g, m_l3_w_in, m_l3_ln_g, m_l3_ln_b, m_l3_w_s, m_l3_b_s, m_l3_w_out, m_final_norm_g, v_l0_norm_g, v_l0_w_in, v_l0_ln_g, v_l0_ln_b, v_l0_w_s, v_l0_b_s, v_l0_w_out, v_l1_norm_g, v_l1_w_in, v_l1_a_re, v_l1_a_im, v_l1_log_step, v_l1_b_re, v_l1_b_im, v_l1_c_re, v_l1_c_im, v_l1_d_skip, v_l1_w_glu, v_l1_b_glu, v_l1_w_out, v_l2_norm_g, v_l2_w_in, v_l2_q_norm_g, v_l2_w_uq, v_l2_kv_norm_g, v_l2_w_ukv, v_l2_w_out, v_l3_norm_g, v_l3_w_in, v_l3_ln_g, v_l3_ln_b, v_l3_w_s, v_l3_b_s, v_l3_w_out, v_final_norm_g):
    args = locals()
    w = {n: args[n] for n in WEIGHT_NAMES}
    mom_m = {n: args["m_" + n] for n in WEIGHT_NAMES}
    mom_v = {n: args["v_" + n] for n in WEIGHT_NAMES}
    h0 = x[0]
    target = loss_target[0]
    pos = positions.reshape(-1, 1)

    full = {}

    def pack_unit(layers):
        names = [n for n in BIG if int(n[1]) in layers]
        rows = [w[n].size // PACK_W for n in names]
        pad = -sum(rows) % PACK_ROW_ALIGN
        return names, rows, _pack_rows([w[n].astype(BF16) for n in names] + [jnp.zeros((pad, PACK_W), BF16)])

    def unpack_unit(names, rows, gathered):
        r0 = 0
        for n, nr in zip(names, rows):
            full[n] = _from_shard_major(n, gathered[:, r0:r0 + nr, :], w[n].shape)
            r0 += nr

    unit0, unit1, unit2 = pack_unit((0,)), pack_unit((1,)), pack_unit((2, 3))
    wp = dict(w)

    def layer_params(i):
        pre = "l%d_" % i
        p = {k[len(pre):]: v for k, v in wp.items() if k.startswith(pre)}
        wf = {k[len(pre):]: v for k, v in full.items() if k.startswith(pre)}
        return p, wf

    flight = gather_start(unit0[2], unit1[2], "gather_l0_start")
    cos, sins = rope_tables(pos, flight[4][0, 0])
    wp["l1_a_re"] = w["l1_a_re"] + flight[4][0, 0]
    s5_weights = _s5_weights(layer_params(1)[0])
    land = gather_wait(*flight[:4], s5_weights[2], "gather_l0_wait")
    got = gather_handover(land, unit0[2], "gather_l0_handover")
    unpack_unit(unit0[0], unit0[1], got)
    flight = gather_start(unit1[2], got, "gather_l1_start")
    wp["l0_norm_g"] = w["l0_norm_g"] + flight[4][0, 0]

    h = h0
    saved = []
    for i, kind in enumerate(LAYER_KINDS):
        if i == 1:
            land = gather_wait(*flight[:4], h, "gather_l1_wait")
            got = gather_handover(land, unit1[2], "gather_l1_handover")
            unpack_unit(unit1[0], unit1[1], got)
            flight = gather_start(unit2[2], got, "gather_l23_start")
            wp["l1_norm_g"] = w["l1_norm_g"] + flight[4][0, 0]
        if i == 2:
            land = gather_wait(*flight[:4], h, "gather_l23_wait")
            unpack_unit(unit2[0], unit2[1], gather_handover(land, unit2[2], "gather_l23_handover"))
        p, wf = layer_params(i)
        tag = "l%d" % i
        if kind == "gmlp":
            h, s = gmlp_layer_fwd(h, p, wf, tag)
        elif kind == "s5":
            h, s = s5_layer_fwd(h, p, wf, s5_weights, tag)
        else:
            h, s = mla_layer_fwd(h, p, wf, cos, sins, tag)
        saved.append(s)
    loss_part, dh, g_final = loss_head(h, final_norm_g, target)

    grads = {"final_norm_g": g_final.reshape(-1)}
    sinks = {}

    for i in reversed(range(len(LAYER_KINDS))):
        kind = LAYER_KINDS[i]
        p, wf = layer_params(i)
        tag = "l%d" % i
        sink = sinks[i] = BigGradSink(i, {n[3:]: w[n].shape for n in BIG if int(n[1]) == i})
        if kind == "gmlp":
            dh, g = gmlp_layer_bwd(dh, saved[i], p, wf, tag, sink)
        elif kind == "s5":
            dh, g = s5_layer_bwd(dh, saved[i], p, wf, tag, sink)
        else:
            dh, g = mla_layer_bwd(dh, saved[i], p, wf, cos, sins, tag, sink)
        for k, val in g.items():
            grads["l%d_%s" % (i, k)] = val
        if i + 1 in sinks:
            zero = sinks[i + 1].forward(dh)
            if i >= 1:
                wp["l%d_norm_g" % (i - 1)] = wp["l%d_norm_g" % (i - 1)] + zero
    sinks[0].forward(dh)
    grad_x = dh[None]

    n_small = sum(w[n].size for n in SMALL)
    piece = N_CHIPS * 2 * 16 * PACK_W
    n_small_pad = -(-(n_small + 1) // piece) * piece
    nrs = n_small_pad // N_CHIPS // PACK_W
    p_small = _small_pack([grads[n] for n in SMALL] + [loss_part], n_small_pad).reshape(N_CHIPS, nrs, PACK_W)
    sib_small, = grads_to_sibling([p_small], "grads_to_sibling_small")
    t_small = pair_sum(p_small, sib_small, F32, "pair_sum_small")
    rb_small, = grads_across_chips([t_small])
    halves = [chip_sum(t_small, rb_small, "chip_sum_small")]

    after = halves[0]
    for i in reversed(range(len(LAYER_KINDS))):
        t_i, rb_i = reduce_wait(*sinks[i].flight[:4], after, "reduce_l%d_wait" % i)
        halves.append(chip_sum(t_i, rb_i, "chip_sum_l%d" % i))
        after = halves[-1]
    reduced = reduced_to_sibling(halves)
    small_flight = bcast_start(reduced[0], reduced[1], "small_allgather_start")

    g_out, d_out, nm_out, nv_out = {}, {}, {}, {}
    for i, g_i in zip(reversed(range(len(LAYER_KINDS))), reduced[1:]):
        for wn, (r0, shape, direct) in sinks[i].regions.items():
            n = "l%d_%s" % (i, wn)
            if direct:
                g_out[n], d_out[n], nm_out[n], nv_out[n] = adamw_packed(w[n], g_i, r0, mom_m[n], mom_v[n], "adamw_" + n)
            else:
                g_out[n] = g_i[r0:r0 + shape[0] * shape[1] // CHUNK_W].reshape(shape)
                d_out[n], nm_out[n], nv_out[n] = adamw(w[n], g_out[n], mom_m[n], mom_v[n], "adamw_" + n)
    small_all = bcast_wait(*small_flight[:4], nv_out["l0_w_in"], "small_allgather_wait")
    g_small = small_all.reshape(-1, PACK_W)
    sp = lambda d: _small_pack([d[n] for n in SMALL], n_small_pad).reshape(-1, PACK_W)
    d_small, nm_small, nv_small = adamw(sp(w), g_small, sp(mom_m), sp(mom_v), "adamw_small")
    for buf, out in ((g_small, g_out), (d_small, d_out), (nm_small, nm_out), (nv_small, nv_out)):
        flat = buf.reshape(-1)
        o = 0
        for n in SMALL:
            out[n] = flat[o:o + w[n].size].reshape(w[n].shape)
            o += w[n].size
    loss = g_small.reshape(-1)[n_small]
    return (loss, grad_x, *[g_out[n] for n in WEIGHT_NAMES], *[d_out[n] for n in WEIGHT_NAMES],
            *[nm_out[n] for n in WEIGHT_NAMES], *[nv_out[n] for n in WEIGHT_NAMES])
```

```python
import functools
import math

import jax
import jax.numpy as jnp
import numpy as np
from jax import lax
from jax.experimental import pallas as pl
from jax.experimental.pallas import tpu as pltpu

F32 = jnp.float32
BF16 = jnp.bfloat16
MESH = pl.DeviceIdType.MESH
VMEM_LIMIT_BYTES = 56 * 1024 * 1024
LANES = 128
PACK_W = 1024
CHUNK_W = 256
PACK_ROW_ALIGN = 256
ROW_TILE = 256
ROW_TILE_NARROW = 512
SUM_BLOCK_BYTES = 1024 * 1024
ADAMW_BLOCK_BYTES = 1024 * 1024
MM_BLOCK_BYTES = 12 * 1024 * 1024

NORM_EPS = 1e-6
N_CHIPS = 4
GMLP_CHUNK = 128
GMLP_GROUPS = 8
S5_GROUPS = 128
S5_GROUP = 16
S5_STATE = 64
S5_SB = 16
S5_SEG = 8
MLA_HEADS = 16
MLA_NOPE = 128
MLA_ROPE = 64
MLA_Q_RANK = 384
MLA_KV_RANK = 128
MLA_SCALE = (MLA_NOPE + MLA_ROPE) ** -0.5
ROPE_THETA = 10000.0
NEG_INF = -1e30
ADAM_LR, ADAM_B1, ADAM_B2, ADAM_EPS, ADAM_WD, ADAM_STEP = 0.001, 0.9, 0.999, 1e-08, 0.01, 10

DN_NN = (((1,), (0,)), ((), ()))
DN_NT = (((1,), (1,)), ((), ()))
DN_TN = (((0,), (0,)), ((), ()))


def _cparams(sem):
    return pltpu.CompilerParams(dimension_semantics=sem, vmem_limit_bytes=VMEM_LIMIT_BYTES)


def _pick(n, cands=(512, 384, 256, 128)):
    for c in cands:
        if n % c == 0:
            return c
    return n


def _pick_rows(r, cap=512, mult=16):
    return max(t for t in range(mult, cap + 1, mult) if r % t == 0)


def _dot(a, b, dn):
    return lax.dot_general(a.astype(BF16), b.astype(BF16), dn, preferred_element_type=F32)


def _sigmoid(x):
    return 0.5 + 0.5 * jnp.tanh(0.5 * x)


def _gelu(x):
    c = math.sqrt(2.0 / math.pi)
    t = jnp.tanh(c * (x + 0.044715 * x * x * x))
    return 0.5 * x * (1.0 + t)


def _gelu_grad(x):
    c = math.sqrt(2.0 / math.pi)
    t = jnp.tanh(c * (x + 0.044715 * x * x * x))
    return 0.5 * (1.0 + t) + 0.5 * x * (1.0 - t * t) * c * (1.0 + 3.0 * 0.044715 * x * x)


def _gelu_both(x):
    c = math.sqrt(2.0 / math.pi)
    t = jnp.tanh(c * (x + 0.044715 * x * x * x))
    return 0.5 * x * (1.0 + t), 0.5 * (1.0 + t) + 0.5 * x * (1.0 - t * t) * c * (1.0 + 3.0 * 0.044715 * x * x)


def _silu_both(z):
    s = _sigmoid(z)
    return z * s, s * (1.0 + z * (1.0 - s))


def _silu(z):
    return z * _sigmoid(z)


def matmul(a, b, mode, name, out_dtype=F32, add=None):
    if mode == "nn":
        (m, k), n = a.shape, b.shape[1]
    elif mode == "nt":
        (m, k), n = a.shape, b.shape[0]
    else:
        (k, m), n = a.shape, b.shape[1]
    tm = _pick(m, [t for t in (2048, 1024, 512, 384, 256, 128) if t * k * a.dtype.itemsize <= MM_BLOCK_BYTES])
    tn = _pick(n, [t for t in (512, 384, 256, 128) if t * k * b.dtype.itemsize <= MM_BLOCK_BYTES])
    dn = {"nn": DN_NN, "nt": DN_NT, "tn": DN_TN}[mode]

    def body(*refs):
        if add is None:
            a_ref, b_ref, o_ref = refs
        else:
            a_ref, b_ref, add_ref, o_ref = refs
        r = _dot(a_ref[...], b_ref[...], dn)
        if add is not None:
            r = r + add_ref[...].astype(F32)
        o_ref[...] = r.astype(out_dtype)

    a_spec = pl.BlockSpec((k, tm), lambda i, j: (0, i)) if mode == "tn" else pl.BlockSpec((tm, k), lambda i, j: (i, 0))
    b_spec = pl.BlockSpec((tn, k), lambda i, j: (j, 0)) if mode == "nt" else pl.BlockSpec((k, tn), lambda i, j: (0, j))
    o_spec = pl.BlockSpec((tm, tn), lambda i, j: (i, j))
    in_specs = [a_spec, b_spec] + ([o_spec] if add is not None else [])
    args = (a, b) + ((add,) if add is not None else ())
    return pl.pallas_call(
        body, name=name, grid=(m // tm, n // tn), in_specs=in_specs, out_specs=o_spec,
        out_shape=jax.ShapeDtypeStruct((m, n), out_dtype),
        compiler_params=_cparams(("parallel", "arbitrary")))(*args)


def _tile_rows(r, r0, cands=(512, 384, 256, 128)):
    return next(t for t in cands if r % t == 0 and r0 % t == 0)


def matmul_tn_packed(a, b, buf, r0, col_sharded, name):
    k, m = a.shape
    n = b.shape[1]
    if col_sharded:
        chunks = n // N_CHIPS // CHUNK_W
        tm = _tile_rows(m, r0, (1024, 512, 384, 256, 128))
        o_map = lambda i, j: (j // chunks, (r0 + (j % chunks) * m) // tm + i, 0)
    else:
        rs = m // N_CHIPS
        tm = _tile_rows(rs, r0)
        per = rs // tm
        o_map = lambda i, j: (i // per, (r0 + j * rs) // tm + i % per, 0)

    def body(a_ref, b_ref, buf_ref, o_ref):
        o_ref[0] = _dot(a_ref[...], b_ref[...], DN_TN)

    return pl.pallas_call(
        body, name=name, grid=(m // tm, n // CHUNK_W),
        in_specs=[pl.BlockSpec((k, tm), lambda i, j: (0, i)), pl.BlockSpec((k, CHUNK_W), lambda i, j: (0, j)),
                  pl.BlockSpec(memory_space=pl.ANY)],
        out_specs=pl.BlockSpec((1, tm, CHUNK_W), o_map), out_shape=jax.ShapeDtypeStruct(buf.shape, buf.dtype),
        input_output_aliases={2: 0}, compiler_params=_cparams(("parallel", "arbitrary")))(a, b, buf)


def _rows(tl, w, col=0):
    return pl.BlockSpec((tl, w), lambda i: (i, col))


def _full(shape):
    nd = len(shape)
    return pl.BlockSpec(tuple(shape), lambda i: (0,) * nd)


def _rowcall(body, name, n_steps, in_specs, out_specs, out_shape, scratch=()):
    return pl.pallas_call(
        body, name=name, grid=(n_steps,), in_specs=in_specs, out_specs=out_specs, out_shape=out_shape,
        scratch_shapes=list(scratch), compiler_params=_cparams(("arbitrary",)))


def _acc(ref, val, i):
    @pl.when(i == 0)
    def _():
        ref[...] = val

    @pl.when(i != 0)
    def _():
        ref[...] += val


def rms_fwd(h, g, name):
    l, d = h.shape
    tl = ROW_TILE_NARROW

    def body(h_ref, g_ref, o_ref):
        x = h_ref[...]
        r = lax.rsqrt(jnp.mean(x * x, axis=-1, keepdims=True) + NORM_EPS)
        o_ref[...] = (x * r * g_ref[...]).astype(BF16)

    return _rowcall(body, name, l // tl, [_rows(tl, d), _full((1, d))], _rows(tl, d),
                    jax.ShapeDtypeStruct((l, d), BF16))(h, g.reshape(1, d))


def rms_bwd(h, g, dhn, dh_in, name):
    l, d = h.shape
    tl = ROW_TILE_NARROW

    def body(h_ref, g_ref, dhn_ref, dhi_ref, dh_ref, dg_ref):
        i = pl.program_id(0)
        x = h_ref[...]
        r = lax.rsqrt(jnp.mean(x * x, axis=-1, keepdims=True) + NORM_EPS)
        xhat = x * r
        dy = dhn_ref[...]
        dxh = dy * g_ref[...]
        dx = r * (dxh - xhat * jnp.mean(dxh * xhat, axis=-1, keepdims=True))
        dh_ref[...] = dhi_ref[...] + dx
        _acc(dg_ref, jnp.sum(dy * xhat, axis=0, keepdims=True), i)

    return _rowcall(body, name, l // tl, [_rows(tl, d), _full((1, d)), _rows(tl, d), _rows(tl, d)],
                    [_rows(tl, d), _full((1, d))],
                    [jax.ShapeDtypeStruct((l, d), F32), jax.ShapeDtypeStruct((1, d), F32)])(h, g.reshape(1, d), dhn, dh_in)


def loss_head(h, g, target):
    l, d = h.shape
    tl = ROW_TILE_NARROW

    def body(h_ref, g_ref, t_ref, loss_ref, dh_ref, dg_ref):
        i = pl.program_id(0)
        x = h_ref[...]
        gg = g_ref[...]
        r = lax.rsqrt(jnp.mean(x * x, axis=-1, keepdims=True) + NORM_EPS)
        xhat = x * r
        err = xhat * gg - t_ref[...]
        part = 0.5 * jnp.sum(jnp.mean(err * err, axis=-1, keepdims=True), axis=0, keepdims=True)
        _acc(loss_ref, part, i)
        dy = err * (1.0 / d)
        dxh = dy * gg
        dh_ref[...] = r * (dxh - xhat * jnp.mean(dxh * xhat, axis=-1, keepdims=True))
        _acc(dg_ref, jnp.sum(dy * xhat, axis=0, keepdims=True), i)

    return _rowcall(body, "loss_head", l // tl, [_rows(tl, d), _full((1, d)), _rows(tl, d)],
                    [_full((1, 1)), _rows(tl, d), _full((1, d))],
                    [jax.ShapeDtypeStruct((1, 1), F32), jax.ShapeDtypeStruct((l, d), F32),
                     jax.ShapeDtypeStruct((1, d), F32)])(h, g.reshape(1, d), target)


def _gmlp_common(a_ref, lng_ref, lnb_ref):
    di = lng_ref.shape[1]
    u_pre = a_ref[:, 0:di]
    v_pre = a_ref[:, di:2 * di]
    z = a_ref[:, 2 * di:3 * di]
    vg = _gelu(v_pre)
    mu = jnp.mean(vg, axis=-1, keepdims=True)
    xc = vg - mu
    rstd = lax.rsqrt(jnp.mean(xc * xc, axis=-1, keepdims=True) + NORM_EPS)
    vhat = xc * rstd
    vn = vhat * lng_ref[...] + lnb_ref[...]
    return u_pre, v_pre, z, vhat, rstd, vn


def _tril(w):
    r = lax.broadcasted_iota(jnp.int32, w.shape, 0)
    c = lax.broadcasted_iota(jnp.int32, w.shape, 1)
    return jnp.where(c <= r, w, 0.0)


def gmlp_gate_fwd(a, ln_g, ln_b, w_s, b_s, name):
    l, w3 = a.shape
    di = w3 // 3
    dg = di // GMLP_GROUPS
    tl = GMLP_CHUNK

    def body(a_ref, lng_ref, lnb_ref, ws_ref, bs_ref, m_ref):
        u_pre, _, z, _, _, vn = _gmlp_common(a_ref, lng_ref, lnb_ref)
        gate = _gelu(u_pre) * _silu(z)
        for g in range(GMLP_GROUPS):
            sl = slice(g * dg, (g + 1) * dg)
            s = _dot(_tril(ws_ref[g]), vn[:, sl], DN_NN) + bs_ref[g]
            m_ref[:, sl] = (gate[:, sl] * s).astype(BF16)

    return _rowcall(body, name, l // tl,
                    [_rows(tl, w3), _full((1, di)), _full((1, di)), _full(w_s.shape), _full((GMLP_GROUPS, tl, 1))],
                    _rows(tl, di), jax.ShapeDtypeStruct((l, di), BF16))(
        a, ln_g.reshape(1, di), ln_b.reshape(1, di), w_s, b_s.reshape(GMLP_GROUPS, tl, 1))


def gmlp_gate_bwd(a, dm, ln_g, ln_b, w_s, b_s, name):
    l, w3 = a.shape
    di = w3 // 3
    dg = di // GMLP_GROUPS
    tl = GMLP_CHUNK

    def body(a_ref, dm_ref, lng_ref, lnb_ref, ws_ref, bs_ref, da_ref, dlg_ref, dlb_ref, dws_ref, dbs_ref,
             dvn_ref, vh_ref, gv_ref):
        i = pl.program_id(0)
        vg, gv = _gelu_both(a_ref[:, di:2 * di])
        gv_ref[...] = gv
        xc = vg - jnp.mean(vg, axis=-1, keepdims=True)
        rstd = lax.rsqrt(jnp.mean(xc * xc, axis=-1, keepdims=True) + NORM_EPS)
        vh_ref[...] = xc * rstd
        for g in range(GMLP_GROUPS):
            sl = slice(g * dg, (g + 1) * dg)
            wt = _tril(ws_ref[g])
            vn_g = vh_ref[:, sl] * lng_ref[:, sl] + lnb_ref[:, sl]
            s = _dot(wt, vn_g, DN_NN) + bs_ref[g]
            dmg = dm_ref[:, sl]
            u, gu = _gelu_both(a_ref[:, sl])
            sz, gz = _silu_both(a_ref[:, 2 * di + g * dg:2 * di + (g + 1) * dg])
            ds = dmg * u * sz
            da_ref[:, sl] = (dmg * s * sz * gu).astype(BF16)
            da_ref[:, 2 * di + g * dg:2 * di + (g + 1) * dg] = (dmg * u * s * gz).astype(BF16)
            dvn_ref[:, sl] = _dot(wt, ds, DN_TN)
            dw = _tril(_dot(ds, vn_g, DN_NT))
            db = jnp.sum(ds, axis=1, keepdims=True)

            @pl.when(i == 0)
            def _():
                dws_ref[g] = dw
                dbs_ref[g] = db

            @pl.when(i != 0)
            def _():
                dws_ref[g] += dw
                dbs_ref[g] += db

        dvn = dvn_ref[...]
        vhat = vh_ref[...]
        dxh = dvn * lng_ref[...]
        dvg = rstd * (dxh - jnp.mean(dxh, axis=-1, keepdims=True) - vhat * jnp.mean(dxh * vhat, axis=-1, keepdims=True))
        da_ref[:, di:2 * di] = (dvg * gv_ref[...]).astype(BF16)
        _acc(dlg_ref, jnp.sum(dvn * vhat, axis=0, keepdims=True), i)
        _acc(dlb_ref, jnp.sum(dvn, axis=0, keepdims=True), i)

    outs = _rowcall(
        body, name, l // tl,
        [_rows(tl, w3), _rows(tl, di), _full((1, di)), _full((1, di)), _full(w_s.shape), _full((GMLP_GROUPS, tl, 1))],
        [_rows(tl, w3), _full((1, di)), _full((1, di)), _full(w_s.shape), _full((GMLP_GROUPS, tl, 1))],
        [jax.ShapeDtypeStruct((l, w3), BF16), jax.ShapeDtypeStruct((1, di), F32), jax.ShapeDtypeStruct((1, di), F32),
         jax.ShapeDtypeStruct(w_s.shape, F32), jax.ShapeDtypeStruct((GMLP_GROUPS, tl, 1), F32)],
        scratch=[pltpu.VMEM((tl, di), F32)] * 3)(
        a, dm, ln_g.reshape(1, di), ln_b.reshape(1, di), w_s, b_s.reshape(GMLP_GROUPS, tl, 1))
    return outs


def gmlp_layer_fwd(h, p, wf, tag):
    hn = rms_fwd(h, p["norm_g"], tag + "_rms")
    a = matmul(hn, wf["w_in"], "nn", tag + "_mm_in")
    m = gmlp_gate_fwd(a, p["ln_g"], p["ln_b"], p["w_s"], p["b_s"], tag + "_gate")
    h_out = matmul(m, wf["w_out"], "nn", tag + "_mm_out", add=h)
    return h_out, (h, hn, a, m)


def gmlp_layer_bwd(dh_out, saved, p, wf, tag, sink):
    h, hn, a, m = saved
    dm = matmul(dh_out, wf["w_out"], "nt", tag + "_mm_dm")
    sink.mm("w_out", m, dh_out, tag + "_mm_gwout")
    sink.forward_above(dm)
    da, dlg, dlb, dws, dbs = gmlp_gate_bwd(a, dm, p["ln_g"], p["ln_b"], p["w_s"], p["b_s"], tag + "_gate_bwd")
    dhn = matmul(da, wf["w_in"], "nt", tag + "_mm_dhn")
    sink.mm("w_in", hn, da, tag + "_mm_gwin")
    zero = sink.send()
    dh, dng = rms_bwd(h, p["norm_g"] + zero, dhn, dh_out, tag + "_rms_bwd")
    grads = {"norm_g": dng.reshape(-1), "ln_g": dlg.reshape(-1), "ln_b": dlb.reshape(-1),
             "w_s": dws, "b_s": dbs.reshape(GMLP_GROUPS, GMLP_CHUNK)}
    return dh, grads


def _cmul(ar, ai, br, bi):
    return ar * br - ai * bi, ar * bi + ai * br


S5_PG = 16


def _gblock(tail):
    return pl.BlockSpec((S5_PG,) + tuple(tail), lambda i: (i, 0, 0))


def s5_params_fwd(a_re, a_im, log_step, b_re, b_im):
    g, p, hh = b_re.shape

    def body(ar_ref, ai_ref, ls_ref, br_ref, bi_ref, lr_ref, li_ref, bbr_ref, bbi_ref):
        ar, ai = ar_ref[...], ai_ref[...]
        step = jnp.exp(ls_ref[...])
        mag = jnp.exp(ar * step)
        lr, li = mag * jnp.cos(ai * step), mag * jnp.sin(ai * step)
        den = 1.0 / (ar * ar + ai * ai)
        fr, fi = _cmul(lr - 1.0, li, ar * den, -ai * den)
        lr_ref[...] = lr
        li_ref[...] = li
        bbr, bbi = _cmul(fr, fi, br_ref[...], bi_ref[...])
        bbr_ref[...] = bbr
        bbi_ref[...] = bbi

    s1 = jax.ShapeDtypeStruct((g, p, 1), F32)
    s3 = jax.ShapeDtypeStruct((g, p, hh), F32)
    b1, b0, b3 = _gblock((p, 1)), _gblock((1, 1)), _gblock((p, hh))
    return pl.pallas_call(body, name="s5_params_fwd", grid=(g // S5_PG,), in_specs=[b1, b1, b0, b3, b3],
                          out_specs=[b1, b1, b3, b3], out_shape=[s1, s1, s3, s3],
                          compiler_params=_cparams(("parallel",)))(
        a_re.reshape(g, p, 1), a_im.reshape(g, p, 1), log_step.reshape(g, 1, 1), b_re, b_im)


def s5_params_bwd(a_re, a_im, log_step, b_re, b_im, dl_re, dl_im, dbb_re, dbb_im):
    g, p, hh = b_re.shape

    def body(ar_ref, ai_ref, ls_ref, br_ref, bi_ref, dlr_ref, dli_ref, dbr_ref, dbi_ref,
             gar_ref, gai_ref, gls_ref, gbr_ref, gbi_ref):
        ar, ai = ar_ref[...], ai_ref[...]
        step = jnp.exp(ls_ref[...])
        mag = jnp.exp(ar * step)
        lr, li = mag * jnp.cos(ai * step), mag * jnp.sin(ai * step)
        den = 1.0 / (ar * ar + ai * ai)
        ir, ii = ar * den, -ai * den
        fr, fi = _cmul(lr - 1.0, li, ir, ii)
        br, bi = br_ref[...], bi_ref[...]
        dbr, dbi = dbr_ref[...], dbi_ref[...]
        gbr, gbi = _cmul(fr, -fi, dbr, dbi)
        gbr_ref[...] = gbr
        gbi_ref[...] = gbi
        pr, pi = _cmul(br, -bi, dbr, dbi)
        gfr = jnp.sum(pr, axis=-1, keepdims=True)
        gfi = jnp.sum(pi, axis=-1, keepdims=True)
        t_r, t_i = _cmul(ir, -ii, gfr, gfi)
        glr, gli = dlr_ref[...] + t_r, dli_ref[...] + t_i
        c1r, c1i = _cmul(step * lr, -step * li, glr, gli)
        qr, qi = _cmul(fr, fi, ir, ii)
        c2r, c2i = _cmul(-qr, qi, gfr, gfi)
        gar_ref[...] = c1r + c2r
        gai_ref[...] = c1i + c2i
        wr, wi = _cmul(ar, ai, lr, li)
        sr, _ = _cmul(wr, -wi, glr, gli)
        gls_ref[...] = jnp.sum(sr, axis=1, keepdims=True) * step

    s1 = jax.ShapeDtypeStruct((g, p, 1), F32)
    s3 = jax.ShapeDtypeStruct((g, p, hh), F32)
    b1, b0, b3 = _gblock((p, 1)), _gblock((1, 1)), _gblock((p, hh))
    return pl.pallas_call(body, name="s5_params_bwd", grid=(g // S5_PG,),
                          in_specs=[b1, b1, b0, b3, b3, b1, b1, b3, b3], out_specs=[b1, b1, b0, b3, b3],
                          out_shape=[s1, s1, jax.ShapeDtypeStruct((g, 1, 1), F32), s3, s3],
                          compiler_params=_cparams(("parallel",)))(
        a_re.reshape(g, p, 1), a_im.reshape(g, p, 1), log_step.reshape(g, 1, 1), b_re, b_im,
        dl_re, dl_im, dbb_re, dbb_im)


def _blockdiag(t):
    sb, n, r, c = t.shape
    eye = jnp.eye(n, dtype=bool)[None, :, None, :, None]
    full = jnp.where(eye, t[:, :, :, None, :], jnp.zeros((), t.dtype))
    return full.reshape(sb, n * r, n * c)


def _blockdiag_extract(m, r, c):
    sb = m.shape[0]
    n = m.shape[1] // r
    m5 = m.reshape(sb, n, r, n, c)
    return jnp.stack([m5[:, i, :, i, :] for i in range(n)], axis=1)


S5_TB = 256
S5_UNROLL = 8


def _lam_power(pr, pi, n):
    for _ in range(int(math.log2(n))):
        pr, pi = _cmul(pr, pi, pr, pi)
    return pr, pi


def _segment_entries(er, ei, pr, pi, reverse):
    seg, ns = er.shape
    row = lax.broadcasted_iota(jnp.int32, (seg, ns), 0)
    cr = jnp.zeros((seg, ns), F32)
    ci = jnp.zeros((seg, ns), F32)
    cur_r = jnp.zeros((1, ns), F32)
    cur_i = jnp.zeros((1, ns), F32)
    for s in (range(seg - 2, -1, -1) if reverse else range(1, seg)):
        src = s + 1 if reverse else s - 1
        mr, mi = _cmul(pr, pi, cur_r, cur_i)
        cur_r = jnp.sum(jnp.where(row == src, er, 0.0), axis=0, keepdims=True) + mr
        cur_i = jnp.sum(jnp.where(row == src, ei, 0.0), axis=0, keepdims=True) + mi
        cr = jnp.where(row == s, cur_r, cr)
        ci = jnp.where(row == s, cur_i, ci)
    return cr, ci


def s5_scan_fused_fwd(a_p, lam_re, lam_im, wb_re, wb_im, wc_re, wc_im, d_skip, name):
    l = a_p.shape[0]
    di = d_skip.shape[1]
    rows = S5_SEG * S5_TB
    nb = l // rows
    ns = wb_re.shape[2]

    def body(u_ref, lr_ref, li_ref, wbr_ref, wbi_ref, wcr_ref, wci_ref, ds_ref, y_ref, yg_ref, ckr_ref, cki_ref,
             bur, bui):
        lr = jnp.broadcast_to(lr_ref[0], (S5_SEG, ns))
        li = jnp.broadcast_to(li_ref[0], (S5_SEG, ns))

        def scan_block(b, carry, keep):
            def step(t, c):
                xr, xi = c
                sl = pl.ds(pl.multiple_of(b * rows + t * S5_SEG, S5_SEG), S5_SEG)
                nr = lr * xr - li * xi + bur[sl, :]
                ni = lr * xi + li * xr + bui[sl, :]
                if keep:
                    bur[sl, :] = nr
                    bui[sl, :] = ni
                return nr, ni

            return lax.fori_loop(0, S5_TB, step, carry, unroll=S5_UNROLL)

        def project(b, carry):
            rs = pl.ds(pl.multiple_of(b * rows, rows), rows)
            u = u_ref[rs, :]
            bur[rs, :] = _dot(u, wbr_ref[0], DN_NN)
            bui[rs, :] = _dot(u, wbi_ref[0], DN_NN)
            return scan_block(b, carry, False)

        zero = jnp.zeros((S5_SEG, ns), F32)
        er, ei = lax.fori_loop(0, nb, project, (zero, zero))
        pr, pi = _lam_power(lr_ref[0], li_ref[0], l // S5_SEG)
        entry = _segment_entries(er, ei, pr, pi, False)

        def emit(b, carry):
            ckr_ref[0, b] = carry[0]
            cki_ref[0, b] = carry[1]
            carry = scan_block(b, carry, True)
            rs = pl.ds(pl.multiple_of(b * rows, rows), rows)
            y = (_dot(bur[rs, :], wcr_ref[0], DN_NN) - _dot(bui[rs, :], wci_ref[0], DN_NN)
                 + ds_ref[...] * u_ref[rs, :])
            y_ref[rs, :] = y
            yg_ref[rs, :] = _gelu(y).astype(BF16)
            return carry

        lax.fori_loop(0, nb, emit, entry)

    sb3 = lambda s: (s, 0, 0)
    st = jax.ShapeDtypeStruct
    return pl.pallas_call(
        body, name=name, grid=(S5_SB,),
        in_specs=[pl.BlockSpec((l, LANES), lambda s: (0, s)),
                  pl.BlockSpec((1, 1, ns), sb3), pl.BlockSpec((1, 1, ns), sb3),
                  pl.BlockSpec((1, LANES, ns), sb3), pl.BlockSpec((1, LANES, ns), sb3),
                  pl.BlockSpec((1, ns, LANES), sb3), pl.BlockSpec((1, ns, LANES), sb3),
                  pl.BlockSpec((1, LANES), lambda s: (0, s))],
        out_specs=[pl.BlockSpec((l, LANES), lambda s: (0, s)), pl.BlockSpec((l, LANES), lambda s: (0, s)),
                   pl.BlockSpec((1, nb, S5_SEG, ns), lambda s: (s, 0, 0, 0)),
                   pl.BlockSpec((1, nb, S5_SEG, ns), lambda s: (s, 0, 0, 0))],
        out_shape=[st((l, di), F32), st((l, di), BF16),
                   st((S5_SB, nb, S5_SEG, ns), F32), st((S5_SB, nb, S5_SEG, ns), F32)],
        scratch_shapes=[pltpu.VMEM((l, ns), F32), pltpu.VMEM((l, ns), F32)],
        compiler_params=_cparams(("parallel",)))(a_p, lam_re, lam_im, wb_re, wb_im, wc_re, wc_im, d_skip)


def s5_scan_fused_bwd(a_p, dy, lam_re, lam_im, wb_re, wb_im, wc_re, wc_im, d_skip, ck_re, ck_im, name):
    l = a_p.shape[0]
    di = d_skip.shape[1]
    rows = S5_SEG * S5_TB
    nb = l // rows
    ns = wb_re.shape[2]

    def body(u_ref, dy_ref, lr_ref, li_ref, wbr_ref, wbi_ref, wcr_ref, wci_ref, ds_ref, ckr_ref, cki_ref,
             du_ref, dwbr_ref, dwbi_ref, dwcr_ref, dwci_ref, dds_ref, dlr_ref, dli_ref, gr, gi, xr_b, xi_b):
        lr = jnp.broadcast_to(lr_ref[0], (S5_SEG, ns))
        li = jnp.broadcast_to(li_ref[0], (S5_SEG, ns))

        def back_project(k, carry):
            b = nb - 1 - k
            rs = pl.ds(pl.multiple_of(b * rows, rows), rows)
            dyv = dy_ref[rs, :]
            gr[rs, :] = _dot(dyv, wcr_ref[0], DN_NT)
            gi[rs, :] = -_dot(dyv, wci_ref[0], DN_NT)

            def step(kk, c):
                ar, ai = c
                sl = pl.ds(pl.multiple_of(b * rows + (S5_TB - 1 - kk) * S5_SEG, S5_SEG), S5_SEG)
                return gr[sl, :] + lr * ar + li * ai, gi[sl, :] + lr * ai - li * ar

            return lax.fori_loop(0, S5_TB, step, carry, unroll=S5_UNROLL)

        zero = jnp.zeros((S5_SEG, ns), F32)
        er, ei = lax.fori_loop(0, nb, back_project, (zero, zero))
        pr, pi = _lam_power(lr_ref[0], -li_ref[0], l // S5_SEG)
        a0r, a0i = _segment_entries(er, ei, pr, pi, True)

        dwbr_ref[...] = jnp.zeros_like(dwbr_ref)
        dwbi_ref[...] = jnp.zeros_like(dwbi_ref)
        dwcr_ref[...] = jnp.zeros_like(dwcr_ref)
        dwci_ref[...] = jnp.zeros_like(dwci_ref)
        dds_ref[...] = jnp.zeros_like(dds_ref)

        def block(k, carry):
            b = nb - 1 - k
            rs = pl.ds(pl.multiple_of(b * rows, rows), rows)
            u = u_ref[rs, :]
            dyv = dy_ref[rs, :]
            body_rows = pl.ds(S5_SEG, rows)
            x0r, x0i = ckr_ref[0, b], cki_ref[0, b]
            xr_b[0:S5_SEG, :] = x0r
            xi_b[0:S5_SEG, :] = x0i
            xr_b[body_rows, :] = _dot(u, wbr_ref[0], DN_NN)
            xi_b[body_rows, :] = _dot(u, wbi_ref[0], DN_NN)

            def fstep(t, c):
                xr, xi = c
                sl = pl.ds(pl.multiple_of((t + 1) * S5_SEG, S5_SEG), S5_SEG)
                nr = lr * xr - li * xi + xr_b[sl, :]
                ni = lr * xi + li * xr + xi_b[sl, :]
                xr_b[sl, :] = nr
                xi_b[sl, :] = ni
                return nr, ni

            lax.fori_loop(0, S5_TB, fstep, (x0r, x0i), unroll=S5_UNROLL)
            dwcr_ref[0] += _dot(xr_b[body_rows, :], dyv, DN_TN)
            dwci_ref[0] -= _dot(xi_b[body_rows, :], dyv, DN_TN)

            def bstep(kk, c):
                ar, ai = c
                sl = pl.ds(pl.multiple_of(b * rows + (S5_TB - 1 - kk) * S5_SEG, S5_SEG), S5_SEG)
                nr = gr[sl, :] + lr * ar + li * ai
                ni = gi[sl, :] + lr * ai - li * ar
                gr[sl, :] = nr
                gi[sl, :] = ni
                return nr, ni

            ar, ai = lax.fori_loop(0, S5_TB, bstep, carry[:2], unroll=S5_UNROLL)
            a_r, a_i = gr[rs, :], gi[rs, :]
            p_r, p_i = xr_b[0:rows, :], xi_b[0:rows, :]
            per_seg = lambda v: jnp.sum(v.reshape(S5_TB, S5_SEG, ns), axis=0)
            carry = (ar, ai, carry[2] + per_seg(a_r * p_r + a_i * p_i), carry[3] + per_seg(a_i * p_r - a_r * p_i))
            du_ref[rs, :] = (_dot(a_r, wbr_ref[0], DN_NT) + _dot(a_i, wbi_ref[0], DN_NT) + ds_ref[...] * dyv).astype(BF16)
            dwbr_ref[0] += _dot(u, a_r, DN_TN)
            dwbi_ref[0] += _dot(u, a_i, DN_TN)
            dds_ref[...] += jnp.sum(dyv * u, axis=0, keepdims=True)
            return carry

        _, _, dlr, dli = lax.fori_loop(0, nb, block, (a0r, a0i, zero, zero))
        dlr_ref[0] = dlr
        dli_ref[0] = dli

    sb3 = lambda s: (s, 0, 0)
    seq = pl.BlockSpec((l, LANES), lambda s: (0, s))
    ck = pl.BlockSpec((1, nb, S5_SEG, ns), lambda s: (s, 0, 0, 0))
    st = jax.ShapeDtypeStruct
    return pl.pallas_call(
        body, name=name, grid=(S5_SB,),
        in_specs=[seq, seq, pl.BlockSpec((1, 1, ns), sb3), pl.BlockSpec((1, 1, ns), sb3),
                  pl.BlockSpec((1, LANES, ns), sb3), pl.BlockSpec((1, LANES, ns), sb3),
                  pl.BlockSpec((1, ns, LANES), sb3), pl.BlockSpec((1, ns, LANES), sb3),
                  pl.BlockSpec((1, LANES), lambda s: (0, s)), ck, ck],
        out_specs=[seq, pl.BlockSpec((1, LANES, ns), sb3), pl.BlockSpec((1, LANES, ns), sb3),
                   pl.BlockSpec((1, ns, LANES), sb3), pl.BlockSpec((1, ns, LANES), sb3),
                   pl.BlockSpec((1, LANES), lambda s: (0, s)),
                   pl.BlockSpec((1, S5_SEG, ns), sb3), pl.BlockSpec((1, S5_SEG, ns), sb3)],
        out_shape=[st((l, di), BF16), st((S5_SB, LANES, ns), F32), st((S5_SB, LANES, ns), F32),
                   st((S5_SB, ns, LANES), F32), st((S5_SB, ns, LANES), F32), st((1, di), F32),
                   st((S5_SB, S5_SEG, ns), F32), st((S5_SB, S5_SEG, ns), F32)],
        scratch_shapes=[pltpu.VMEM((l, ns), F32), pltpu.VMEM((l, ns), F32),
                        pltpu.VMEM((rows + S5_SEG, ns), F32), pltpu.VMEM((rows + S5_SEG, ns), F32)],
        compiler_params=_cparams(("parallel",)))(
        a_p, dy, lam_re, lam_im, wb_re, wb_im, wc_re, wc_im, d_skip, ck_re, ck_im)


def s5_gate_fwd(y, t, b_glu, a_p, name):
    l, d = y.shape
    tl = ROW_TILE

    def body(y_ref, t_ref, b_ref, z_ref, m_ref):
        yg = _gelu(y_ref[...])
        m_ref[...] = (yg * _sigmoid(t_ref[...] + b_ref[...]) * _silu(z_ref[...])).astype(BF16)

    return _rowcall(body, name, l // tl, [_rows(tl, d), _rows(tl, d), _full((1, d)), _rows(tl, d, 1)], _rows(tl, d),
                    jax.ShapeDtypeStruct((l, d), BF16))(y, t, b_glu.reshape(1, d), a_p)


def s5_gate_bwd(dm, y, t, b_glu, a_p, name):
    l, d = y.shape
    tl = ROW_TILE

    def body(dm_ref, y_ref, t_ref, b_ref, z_ref, dt_ref, dyg_ref, dz_ref, db_ref):
        i = pl.program_id(0)
        dmv = dm_ref[...]
        z = z_ref[...]
        yg = _gelu(y_ref[...])
        sg = _sigmoid(t_ref[...] + b_ref[...])
        y2 = yg * sg
        sz, gz = _silu_both(z)
        dy2 = dmv * sz
        dz_ref[...] = (dmv * y2 * gz).astype(BF16)
        dyg_ref[...] = dy2 * sg
        dt = dy2 * yg * sg * (1.0 - sg)
        dt_ref[...] = dt.astype(BF16)
        _acc(db_ref, jnp.sum(dt, axis=0, keepdims=True), i)

    st = jax.ShapeDtypeStruct
    return _rowcall(body, name, l // tl, [_rows(tl, d), _rows(tl, d), _rows(tl, d), _full((1, d)), _rows(tl, d, 1)],
                    [_rows(tl, d), _rows(tl, d), _rows(tl, d), _full((1, d))],
                    [st((l, d), BF16), st((l, d), F32), st((l, d), BF16), st((1, d), F32)])(
        dm, y, t, b_glu.reshape(1, d), a_p)


def s5_act_bwd(y, dyg_a, dyg_b, name):
    l, d = y.shape
    tl = ROW_TILE

    def body(y_ref, a_ref, b_ref, o_ref):
        o_ref[...] = (a_ref[...] + b_ref[...]) * _gelu_grad(y_ref[...])

    return _rowcall(body, name, l // tl, [_rows(tl, d)] * 3, _rows(tl, d), jax.ShapeDtypeStruct((l, d), F32))(y, dyg_a, dyg_b)


def _seg_perm(t):
    l, d = t.shape
    return t.reshape(S5_SEG, l // S5_SEG, d).transpose(1, 0, 2).reshape(l, d)


def _seg_unperm(t):
    l, d = t.shape
    return t.reshape(l // S5_SEG, S5_SEG, d).transpose(1, 0, 2).reshape(l, d)


def _s5_weights(p):
    lr, li, bbr, bbi = s5_params_fwd(p["a_re"], p["a_im"], p["log_step"], p["b_re"], p["b_im"])
    ns = 8 * S5_STATE
    lam_re = lr.reshape(S5_SB, 1, ns)
    lam_im = li.reshape(S5_SB, 1, ns)
    to_bd = lambda t: _blockdiag(t.reshape(S5_SB, 8, t.shape[1], t.shape[2]))
    wb_re = to_bd(bbr.transpose(0, 2, 1)).astype(BF16)
    wb_im = to_bd(bbi.transpose(0, 2, 1)).astype(BF16)
    wc_re = to_bd(p["c_re"].transpose(0, 2, 1)).astype(BF16)
    wc_im = to_bd(p["c_im"].transpose(0, 2, 1)).astype(BF16)
    return lam_re, lam_im, wb_re, wb_im, wc_re, wc_im


def s5_layer_fwd(h, p, wf, sw, tag):
    l = h.shape[0]
    di = p["d_skip"].shape[0]
    hn = rms_fwd(h, p["norm_g"], tag + "_rms")
    hn_p = _seg_perm(hn)
    a_p = matmul(hn_p, wf["w_in"], "nn", tag + "_mm_in")
    dsk = p["d_skip"].reshape(1, di)
    y, yg, ck_re, ck_im = s5_scan_fused_fwd(a_p, *sw, dsk, tag + "_scan")
    t = matmul(yg, wf["w_glu"], "nn", tag + "_mm_glu")
    m = s5_gate_fwd(y, t, p["b_glu"], a_p, tag + "_gate")
    out_p = matmul(m, wf["w_out"], "nn", tag + "_mm_out")
    h_out = residual_add(h, _seg_unperm(out_p), tag + "_res")
    return h_out, (h, hn_p, a_p, sw, ck_re, ck_im, y, yg, t, m)


def residual_add(h, y, name):
    l, d = h.shape
    tl = ROW_TILE_NARROW

    def body(h_ref, y_ref, o_ref):
        o_ref[...] = h_ref[...] + y_ref[...]

    return _rowcall(body, name, l // tl, [_rows(tl, d)] * 2, _rows(tl, d), jax.ShapeDtypeStruct((l, d), F32))(h, y)


def s5_layer_bwd(dh_out, saved, p, wf, tag, sink):
    h, hn_p, a_p, sw, ck_re, ck_im, y, yg, t, m = saved
    l = h.shape[0]
    di = p["d_skip"].shape[0]
    dsk = p["d_skip"].reshape(1, di)
    dout_p = _seg_perm(dh_out)
    dm = matmul(dout_p, wf["w_out"], "nt", tag + "_mm_dm")
    sink.mm("w_out", m, dout_p, tag + "_mm_gwout")
    sink.forward_above(dm)
    dt, dyg_a, dz, db_glu = s5_gate_bwd(dm, y, t, p["b_glu"], a_p, tag + "_gate_bwd")
    dyg_b = matmul(dt, wf["w_glu"], "nt", tag + "_mm_dyg")
    sink.mm("w_glu", yg, dt, tag + "_mm_gwglu")
    dy = s5_act_bwd(y, dyg_a, dyg_b, tag + "_act_bwd")
    du, dwbr, dwbi, dwcr, dwci, dds, dlr, dli = s5_scan_fused_bwd(a_p, dy, *sw, dsk, ck_re, ck_im, tag + "_scanb")
    da = jnp.concatenate([du, dz], axis=1)
    dhn_p = matmul(da, wf["w_in"], "nt", tag + "_mm_dhn")
    sink.mm("w_in", hn_p, da, tag + "_mm_gwin")
    zero = sink.send()
    dh, dng = rms_bwd(h, p["norm_g"] + zero, _seg_unperm(dhn_p), dh_out, tag + "_rms_bwd")
    ex = lambda m_, r, c: _blockdiag_extract(m_, r, c).reshape(S5_GROUPS, r, c).transpose(0, 2, 1)
    dbb_re, dbb_im = ex(dwbr, S5_GROUP, S5_STATE), ex(dwbi, S5_GROUP, S5_STATE)
    g_c_re, g_c_im = ex(dwcr, S5_STATE, S5_GROUP), ex(dwci, S5_STATE, S5_GROUP)
    dl_re = lane_sum8(dlr).reshape(S5_GROUPS, S5_STATE, 1)
    dl_im = lane_sum8(dli).reshape(S5_GROUPS, S5_STATE, 1)
    gar, gai, gls, gbr, gbi = s5_params_bwd(p["a_re"], p["a_im"], p["log_step"], p["b_re"], p["b_im"],
                                            dl_re, dl_im, dbb_re, dbb_im)
    grads = {"norm_g": dng.reshape(-1), "a_re": gar.reshape(S5_GROUPS, S5_STATE),
             "a_im": gai.reshape(S5_GROUPS, S5_STATE), "log_step": gls.reshape(-1), "b_re": gbr, "b_im": gbi,
             "c_re": g_c_re, "c_im": g_c_im, "d_skip": dds.reshape(-1), "b_glu": db_glu.reshape(-1)}
    return dh, grads


def lane_sum8(t):
    sb, seg, ns = t.shape

    def body(t_ref, o_ref):
        o_ref[...] = jnp.sum(t_ref[...], axis=1, keepdims=True)

    return pl.pallas_call(body, name="s5_seg_sum", out_shape=jax.ShapeDtypeStruct((sb, 1, ns), F32))(t)


MLA_DI = MLA_HEADS * 128
MLA_CQ0 = MLA_DI
MLA_CKV0 = MLA_CQ0 + MLA_Q_RANK
MLA_KR0 = MLA_CKV0 + MLA_KV_RANK
MLA_AW = MLA_KR0 + LANES


def _rot_half(x):
    w = x.shape[-1]
    lane = lax.broadcasted_iota(jnp.int32, x.shape, x.ndim - 1)
    return jnp.where(lane % MLA_ROPE < MLA_ROPE // 2, pltpu.roll(x, w - MLA_ROPE // 2, x.ndim - 1),
                     pltpu.roll(x, MLA_ROPE // 2, x.ndim - 1))


def rope_tables(pos, zero):
    l = pos.shape[0]
    tl = ROW_TILE
    j = np.arange(LANES) % MLA_ROPE % (MLA_ROPE // 2)
    inv_freq = (ROPE_THETA ** (-(2.0 * j) / MLA_ROPE)).astype(np.float32).reshape(1, LANES)
    sign = np.where(np.arange(LANES) % MLA_ROPE < MLA_ROPE // 2, -1.0, 1.0).astype(np.float32).reshape(1, LANES)

    def body(p_ref, f_ref, s_ref, cos_ref, sin_ref):
        ang = p_ref[...].astype(F32) * f_ref[...]
        cos_ref[...] = jnp.cos(ang)
        sin_ref[...] = jnp.sin(ang) * s_ref[...]

    st = jax.ShapeDtypeStruct((l, LANES), F32)
    return _rowcall(body, "rope_tables", l // tl, [_rows(tl, 1), _full((1, LANES)), _full((1, LANES))],
                    [_rows(tl, LANES)] * 2, [st, st])(pos, jnp.asarray(inv_freq), jnp.asarray(sign) + zero)


def _rope(x, cos, sins):
    return x * cos + _rot_half(x) * sins


def _rope_t(dy, cos, sins):
    return dy * cos - sins * _rot_half(dy)


def _rmsn(x):
    r = lax.rsqrt(jnp.mean(x * x, axis=-1, keepdims=True) + NORM_EPS)
    return x * r, r


def mla_pre(a, q_g, kv_g, cos, sins, name):
    l = a.shape[0]
    tl = ROW_TILE

    def body(a_ref, qg_ref, kg_ref, cos_ref, sin_ref, cq_ref, ckv_ref, krs_ref):
        xq, _ = _rmsn(a_ref[:, MLA_CQ0:MLA_CKV0])
        cq_ref[...] = (xq * qg_ref[...]).astype(BF16)
        xk, _ = _rmsn(a_ref[:, MLA_CKV0:MLA_KR0])
        ckv_ref[...] = (xk * kg_ref[...]).astype(BF16)
        kr = a_ref[:, MLA_KR0:MLA_AW]
        kr2 = kr + pltpu.roll(kr, MLA_ROPE, 1)
        kr2 = _rope(kr2, cos_ref[...], sin_ref[...])
        lane = lax.broadcasted_iota(jnp.int32, kr2.shape, 1)
        krs_ref[0] = jnp.where(lane < MLA_ROPE, kr2, 0.0).astype(BF16)
        krs_ref[1] = jnp.where(lane >= MLA_ROPE, kr2, 0.0).astype(BF16)

    st = jax.ShapeDtypeStruct
    return _rowcall(body, name, l // tl,
                    [_rows(tl, MLA_AW), _full((1, MLA_Q_RANK)), _full((1, MLA_KV_RANK)), _rows(tl, LANES), _rows(tl, LANES)],
                    [_rows(tl, MLA_Q_RANK), _rows(tl, MLA_KV_RANK), pl.BlockSpec((2, tl, LANES), lambda i: (0, i, 0))],
                    [st((l, MLA_Q_RANK), BF16), st((l, MLA_KV_RANK), BF16), st((2, l, LANES), BF16)])(
        a, q_g.reshape(1, -1), kv_g.reshape(1, -1), cos, sins)


def mla_rope_q(qr, cos, sins, name):
    l, w = qr.shape
    tl = ROW_TILE

    def body(q_ref, cos_ref, sin_ref, o_ref):
        c, s = cos_ref[...], sin_ref[...]
        for p in range(w // LANES):
            sl = slice(p * LANES, (p + 1) * LANES)
            o_ref[:, sl] = _rope(q_ref[:, sl], c, s).astype(BF16)

    return _rowcall(body, name, l // tl, [_rows(tl, w), _rows(tl, LANES), _rows(tl, LANES)], _rows(tl, w),
                    jax.ShapeDtypeStruct((l, w), BF16))(qr, cos, sins)


ATT_OUT = 512
ATT_IN = 512
ATT_R = ATT_OUT // ATT_IN


def _scores(qn, qr, kn, kr, mask_off, transposed):
    q2 = jnp.concatenate([qn, qr], axis=1)
    k2 = jnp.concatenate([kn, kr], axis=1)
    s = (_dot(k2, q2, DN_NT) if transposed else _dot(q2, k2, DN_NT)) * MLA_SCALE
    if mask_off is None:
        return s
    r = lax.broadcasted_iota(jnp.int32, s.shape, 0)
    c = lax.broadcasted_iota(jnp.int32, s.shape, 1)
    return jnp.where((r <= c + mask_off) if transposed else (c + mask_off <= r), s, NEG_INF)


def _fold(x, op):
    out = x[:, :LANES]
    for t in range(1, x.shape[1] // LANES):
        out = op(out, x[:, t * LANES:(t + 1) * LANES])
    return out


def flash_fwd(qn, qr, kv, krs, name):
    l = qn.shape[0]
    nq = l // ATT_OUT

    def body(qn_ref, qr_ref, kv_ref, kr_ref, o_ref, lse_ref, s_buf):
        qi = pl.program_id(1)
        q_r = qr_ref[...]
        q_n = [qn_ref[:, hh * LANES:(hh + 1) * LANES] for hh in range(2)]

        def block_scores(j, mx, mask_off):
            sl = pl.ds(pl.multiple_of(j * ATT_IN, ATT_IN), ATT_IN)
            out = []
            for hh in range(2):
                s = _scores(q_n[hh], q_r, kv_ref[sl, 2 * hh * LANES:(2 * hh + 1) * LANES], kr_ref[hh, sl, :],
                            mask_off, False)
                s_buf[hh, j] = s
                out.append(jnp.maximum(mx[hh], _fold(s, jnp.maximum)))
            return tuple(out)

        ninf = jnp.full((ATT_OUT, LANES), NEG_INF, F32)
        mx = lax.fori_loop(0, ATT_R * qi, lambda j, c: block_scores(j, c, None), (ninf, ninf))
        for d in range(ATT_R):
            mx = block_scores(ATT_R * qi + d, mx, d * ATT_IN)
        m = [jnp.max(mx[hh], axis=-1, keepdims=True) for hh in range(2)]

        def block_pv(j, carry):
            sl = pl.ds(pl.multiple_of(j * ATT_IN, ATT_IN), ATT_IN)
            out = []
            for hh in range(2):
                ls, acc = carry[hh]
                p = jnp.exp(s_buf[hh, j] - m[hh])
                out.append((ls + _fold(p, jnp.add),
                            acc + _dot(p, kv_ref[sl, (2 * hh + 1) * LANES:(2 * hh + 2) * LANES], DN_NN)))
            return tuple(out)

        z = jnp.zeros((ATT_OUT, LANES), F32)
        res = lax.fori_loop(0, ATT_R * (qi + 1), block_pv, ((z, z), (z, z)))
        for hh in range(2):
            lsum = jnp.sum(res[hh][0], axis=-1, keepdims=True)
            o_ref[:, hh * LANES:(hh + 1) * LANES] = res[hh][1] / lsum
            lse_ref[hh] = m[hh] + jnp.log(lsum)

    st = jax.ShapeDtypeStruct
    return pl.pallas_call(
        body, name=name, grid=(MLA_HEADS // 2, nq),
        in_specs=[pl.BlockSpec((ATT_OUT, 2 * LANES), lambda p, i: (i, p)),
                  pl.BlockSpec((ATT_OUT, LANES), lambda p, i: (i, p)),
                  pl.BlockSpec((l, 4 * LANES), lambda p, i: (0, p)),
                  pl.BlockSpec((2, l, LANES), lambda p, i: (0, 0, 0))],
        out_specs=[pl.BlockSpec((ATT_OUT, 2 * LANES), lambda p, i: (i, p)),
                   pl.BlockSpec((2, ATT_OUT, 1), lambda p, i: (p, i, 0))],
        out_shape=[st((l, MLA_DI), F32), st((MLA_HEADS, l, 1), F32)],
        scratch_shapes=[pltpu.VMEM((2, l // ATT_IN, ATT_OUT, ATT_IN), F32)],
        compiler_params=_cparams(("parallel", "arbitrary")))(qn, qr, kv, krs)


def flash_dkv(qn, qr, kv, krs, do, lse_row, delta_row, name):
    l = qn.shape[0]
    nk = l // ATT_OUT
    nq = l // ATT_IN

    def body(qn_ref, qr_ref, do_ref, lse_ref, dl_ref, kv_ref, kr_ref, dkv_ref, dkr_ref):
        kj = pl.program_id(1)
        lane = lax.broadcasted_iota(jnp.int32, (ATT_OUT, LANES), 1)
        kn = [kv_ref[:, 2 * hh * LANES:(2 * hh + 1) * LANES] for hh in range(2)]
        v = [kv_ref[:, (2 * hh + 1) * LANES:(2 * hh + 2) * LANES] for hh in range(2)]

        def block(i, carry, mask_off):
            sl = pl.ds(pl.multiple_of(i * ATT_IN, ATT_IN), ATT_IN)
            q_r = qr_ref[sl, :]
            out = []
            for hh in range(2):
                dk2, dv = carry[hh]
                hs = slice(hh * LANES, (hh + 1) * LANES)
                q_n, d_o = qn_ref[sl, hs], do_ref[sl, hs]
                s = _scores(q_n, q_r, kn[hh], kr_ref[hh], mask_off, True)
                pt = jnp.exp(s - lse_ref[hh, i])
                dv = dv + _dot(pt, d_o, DN_NN)
                dpt = _dot(v[hh], d_o, DN_NT)
                dst = (pt * (dpt - dl_ref[hh, i]) * MLA_SCALE).astype(BF16)
                out.append((dk2 + _dot(dst, jnp.concatenate([q_n, q_r], axis=1), DN_NN), dv))
            return tuple(out)

        z = jnp.zeros((ATT_OUT, LANES), F32)
        z2 = jnp.zeros((ATT_OUT, 2 * LANES), F32)
        res = ((z2, z), (z2, z))
        for d in range(ATT_R):
            res = block(ATT_R * kj + d, res, d * ATT_IN)
        res = lax.fori_loop(ATT_R * (kj + 1), nq, lambda i, c: block(i, c, None), res)
        for hh in range(2):
            dkv_ref[:, 2 * hh * LANES:(2 * hh + 1) * LANES] = res[hh][0][:, :LANES].astype(BF16)
            dkv_ref[:, (2 * hh + 1) * LANES:(2 * hh + 2) * LANES] = res[hh][1].astype(BF16)
        dkr_ref[0] = jnp.where(lane < MLA_ROPE, res[0][0][:, LANES:], res[1][0][:, LANES:])

    st = jax.ShapeDtypeStruct
    return pl.pallas_call(
        body, name=name, grid=(MLA_HEADS // 2, nk),
        in_specs=[pl.BlockSpec((l, 2 * LANES), lambda p, j: (0, p)),
                  pl.BlockSpec((l, LANES), lambda p, j: (0, p)),
                  pl.BlockSpec((l, 2 * LANES), lambda p, j: (0, p)),
                  pl.BlockSpec((2, nq, 1, ATT_IN), lambda p, j: (p, 0, 0, 0)),
                  pl.BlockSpec((2, nq, 1, ATT_IN), lambda p, j: (p, 0, 0, 0)),
                  pl.BlockSpec((ATT_OUT, 4 * LANES), lambda p, j: (j, p)),
                  pl.BlockSpec((2, ATT_OUT, LANES), lambda p, j: (0, j, 0))],
        out_specs=[pl.BlockSpec((ATT_OUT, 4 * LANES), lambda p, j: (j, p)),
                   pl.BlockSpec((1, ATT_OUT, LANES), lambda p, j: (p, j, 0))],
        out_shape=[st((l, 2 * MLA_DI), BF16), st((MLA_HEADS // 2, l, LANES), F32)],
        compiler_params=_cparams(("parallel", "arbitrary")))(qn, qr, do, lse_row, delta_row, kv, krs)


def flash_dq(qn, qr, kv, krs, do, lse, delta, cos, sins, name):
    l = qn.shape[0]
    nq = l // ATT_OUT

    def body(qn_ref, qr_ref, do_ref, lse_ref, dl_ref, kv_ref, kr_ref, cos_ref, sin_ref, dqn_ref, dqr_ref):
        qi = pl.program_id(1)
        q_r = qr_ref[...]
        q_n = [qn_ref[:, hh * LANES:(hh + 1) * LANES] for hh in range(2)]
        d_o = [do_ref[:, hh * LANES:(hh + 1) * LANES] for hh in range(2)]
        lse_h = [lse_ref[hh] for hh in range(2)]
        dl_h = [dl_ref[hh] for hh in range(2)]

        def block(j, carry, mask_off):
            sl = pl.ds(pl.multiple_of(j * ATT_IN, ATT_IN), ATT_IN)
            dq2 = list(carry)
            for hh in range(2):
                kn = kv_ref[sl, 2 * hh * LANES:(2 * hh + 1) * LANES]
                v = kv_ref[sl, (2 * hh + 1) * LANES:(2 * hh + 2) * LANES]
                kr = kr_ref[hh, sl, :]
                s = _scores(q_n[hh], q_r, kn, kr, mask_off, False)
                pr = jnp.exp(s - lse_h[hh])
                dp = _dot(d_o[hh], v, DN_NT)
                ds = (pr * (dp - dl_h[hh]) * MLA_SCALE).astype(BF16)
                dq2[hh] = dq2[hh] + _dot(ds, jnp.concatenate([kn, kr], axis=1), DN_NN)
            return tuple(dq2)

        z2 = jnp.zeros((ATT_OUT, 2 * LANES), F32)
        res = lax.fori_loop(0, ATT_R * qi, lambda j, c: block(j, c, None), (z2, z2))
        for d in range(ATT_R):
            res = block(ATT_R * qi + d, res, d * ATT_IN)
        dqn_ref[:, 0:LANES] = res[0][:, :LANES].astype(BF16)
        dqn_ref[:, LANES:2 * LANES] = res[1][:, :LANES].astype(BF16)
        dqr = res[0][:, LANES:] + res[1][:, LANES:]
        dqr_ref[...] = _rope_t(dqr, cos_ref[...], sin_ref[...]).astype(BF16)

    st = jax.ShapeDtypeStruct
    return pl.pallas_call(
        body, name=name, grid=(MLA_HEADS // 2, nq),
        in_specs=[pl.BlockSpec((ATT_OUT, 2 * LANES), lambda p, i: (i, p)),
                  pl.BlockSpec((ATT_OUT, LANES), lambda p, i: (i, p)),
                  pl.BlockSpec((ATT_OUT, 2 * LANES), lambda p, i: (i, p)),
                  pl.BlockSpec((2, ATT_OUT, 1), lambda p, i: (p, i, 0)),
                  pl.BlockSpec((2, ATT_OUT, 1), lambda p, i: (p, i, 0)),
                  pl.BlockSpec((l, 4 * LANES), lambda p, i: (0, p)),
                  pl.BlockSpec((2, l, LANES), lambda p, i: (0, 0, 0)),
                  pl.BlockSpec((ATT_OUT, LANES), lambda p, i: (i, 0)),
                  pl.BlockSpec((ATT_OUT, LANES), lambda p, i: (i, 0))],
        out_specs=[pl.BlockSpec((ATT_OUT, 2 * LANES), lambda p, i: (i, p)),
                   pl.BlockSpec((ATT_OUT, LANES), lambda p, i: (i, p))],
        out_shape=[st((l, MLA_DI), BF16), st((l, MLA_HEADS * MLA_ROPE), BF16)],
        compiler_params=_cparams(("parallel", "arbitrary")))(qn, qr, do, lse, delta, kv, krs, cos, sins)


def mla_gate_fwd(o, a, name):
    l = o.shape[0]
    tl = ROW_TILE

    def body(o_ref, z_ref, m_ref):
        m_ref[...] = (o_ref[...] * _silu(z_ref[...])).astype(BF16)

    return _rowcall(body, name, l // tl, [_rows(tl, MLA_DI), _rows(tl, MLA_DI)], _rows(tl, MLA_DI),
                    jax.ShapeDtypeStruct((l, MLA_DI), BF16))(o, a)


def mla_gate_bwd(dm, o, a, name):
    l = o.shape[0]
    tl = ROW_TILE

    def body(dm_ref, o_ref, z_ref, do_ref, dz_ref, dl_ref):
        dmv, ov, z = dm_ref[...], o_ref[...], z_ref[...]
        sz, gz = _silu_both(z)
        d_o = dmv * sz
        do_ref[...] = d_o.astype(BF16)
        dz_ref[...] = (dmv * ov * gz).astype(BF16)
        pr = d_o * ov
        for h in range(MLA_HEADS):
            dl_ref[h] = jnp.sum(pr[:, h * LANES:(h + 1) * LANES], axis=1, keepdims=True)

    st = jax.ShapeDtypeStruct
    return _rowcall(body, name, l // tl, [_rows(tl, MLA_DI)] * 3,
                    [_rows(tl, MLA_DI), _rows(tl, MLA_DI), pl.BlockSpec((MLA_HEADS, tl, 1), lambda i: (0, i, 0))],
                    [st((l, MLA_DI), BF16), st((l, MLA_DI), BF16), st((MLA_HEADS, l, 1), F32)])(dm, o, a)


def mla_post(a, dcqn, dckvn, dkr_pairs, dz, q_g, kv_g, cos, sins, name):
    l = a.shape[0]
    tl = ROW_TILE
    npair = MLA_HEADS // 2

    def norm_bwd(x, g, dy):
        xhat, r = _rmsn(x)
        dxh = dy * g
        return r * (dxh - xhat * jnp.mean(dxh * xhat, axis=-1, keepdims=True)), jnp.sum(dy * xhat, axis=0, keepdims=True)

    def body(a_ref, dq_ref, dk_ref, dkr_ref, dz_ref, qg_ref, kg_ref, cos_ref, sin_ref, da_ref, dqg_ref, dkg_ref):
        i = pl.program_id(0)
        da_ref[:, 0:MLA_DI] = dz_ref[...]
        dcq, dqg = norm_bwd(a_ref[:, MLA_CQ0:MLA_CKV0], qg_ref[...], dq_ref[...])
        da_ref[:, MLA_CQ0:MLA_CKV0] = dcq.astype(BF16)
        dckv, dkg = norm_bwd(a_ref[:, MLA_CKV0:MLA_KR0], kg_ref[...], dk_ref[...])
        da_ref[:, MLA_CKV0:MLA_KR0] = dckv.astype(BF16)
        dk2 = dkr_ref[0]
        for p in range(1, npair):
            dk2 = dk2 + dkr_ref[p]
        dk2 = _rope_t(dk2, cos_ref[...], sin_ref[...])
        dk2 = dk2 + pltpu.roll(dk2, MLA_ROPE, 1)
        lane = lax.broadcasted_iota(jnp.int32, dk2.shape, 1)
        da_ref[:, MLA_KR0:MLA_AW] = jnp.where(lane < MLA_ROPE, dk2, 0.0).astype(BF16)
        _acc(dqg_ref, dqg, i)
        _acc(dkg_ref, dkg, i)

    st = jax.ShapeDtypeStruct
    return _rowcall(body, name, l // tl,
                    [_rows(tl, MLA_AW), _rows(tl, MLA_Q_RANK), _rows(tl, MLA_KV_RANK),
                     pl.BlockSpec((npair, tl, LANES), lambda i: (0, i, 0)), _rows(tl, MLA_DI),
                     _full((1, MLA_Q_RANK)), _full((1, MLA_KV_RANK)), _rows(tl, LANES), _rows(tl, LANES)],
                    [_rows(tl, MLA_AW), _full((1, MLA_Q_RANK)), _full((1, MLA_KV_RANK))],
                    [st((l, MLA_AW), BF16), st((1, MLA_Q_RANK), F32), st((1, MLA_KV_RANK), F32)])(
        a, dcqn, dckvn, dkr_pairs, dz, q_g.reshape(1, -1), kv_g.reshape(1, -1), cos, sins)


def _mla_w_in_perm(w):
    r = MLA_Q_RANK + MLA_KV_RANK + MLA_ROPE
    pad = jnp.zeros(w.shape[:-1] + (MLA_AW - MLA_KR0 - MLA_ROPE,), w.dtype)
    return jnp.concatenate([w[..., r:], w[..., :r], pad], axis=-1)


def _mla_w_in_unperm(g):
    r = MLA_Q_RANK + MLA_KV_RANK + MLA_ROPE
    return jnp.concatenate([g[..., MLA_DI:MLA_DI + r], g[..., :MLA_DI]], axis=-1)


def _mla_w_uq_split(w):
    k = w.shape[0]
    w3 = w.reshape(k, MLA_HEADS, MLA_NOPE + MLA_ROPE)
    return w3[:, :, :MLA_NOPE].reshape(k, MLA_HEADS * MLA_NOPE), w3[:, :, MLA_NOPE:].reshape(k, MLA_HEADS * MLA_ROPE)


def _mla_w_uq_merge(gn, gr):
    k = gn.shape[0]
    return jnp.concatenate([gn.reshape(k, MLA_HEADS, MLA_NOPE), gr.reshape(k, MLA_HEADS, MLA_ROPE)], axis=2).reshape(k, -1)


def mla_layer_fwd(h, p, wf, cos, sins, tag):
    hn = rms_fwd(h, p["norm_g"], tag + "_rms")
    w_in = _mla_w_in_perm(wf["w_in"])
    w_uq_n, w_uq_r = _mla_w_uq_split(wf["w_uq"])
    a = matmul(hn, w_in, "nn", tag + "_mm_in")
    cqn, ckvn, krs = mla_pre(a, p["q_norm_g"], p["kv_norm_g"], cos, sins, tag + "_pre")
    qn = matmul(cqn, w_uq_n, "nn", tag + "_mm_qn", out_dtype=BF16)
    qr_raw = matmul(cqn, w_uq_r, "nn", tag + "_mm_qr")
    qr = mla_rope_q(qr_raw, cos, sins, tag + "_rope_q")
    kv = matmul(ckvn, wf["w_ukv"], "nn", tag + "_mm_kv", out_dtype=BF16)
    o, lse = flash_fwd(qn, qr, kv, krs, tag + "_flash")
    m = mla_gate_fwd(o, a, tag + "_gate")
    h_out = matmul(m, wf["w_out"], "nn", tag + "_mm_out", add=h)
    return h_out, (h, hn, a, cqn, ckvn, krs, qn, qr, kv, o, lse, m, w_in, w_uq_n, w_uq_r)


def mla_layer_bwd(dh_out, saved, p, wf, cos, sins, tag, sink):
    h, hn, a, cqn, ckvn, krs, qn, qr, kv, o, lse, m, w_in, w_uq_n, w_uq_r = saved
    l = h.shape[0]
    dm = matmul(dh_out, wf["w_out"], "nt", tag + "_mm_dm")
    sink.mm("w_out", m, dh_out, tag + "_mm_gwout")
    sink.forward_above(dm)
    do, dz, delta = mla_gate_bwd(dm, o, a, tag + "_gate_bwd")
    lse_row = lse.reshape(MLA_HEADS, l // ATT_IN, 1, ATT_IN)
    delta_row = delta.reshape(MLA_HEADS, l // ATT_IN, 1, ATT_IN)
    dkv, dkr_pairs = flash_dkv(qn, qr, kv, krs, do, lse_row, delta_row, tag + "_flash_dkv")
    dqn, dqr = flash_dq(qn, qr, kv, krs, do, lse, delta, cos, sins, tag + "_flash_dq")
    dcqn = matmul(dqn, w_uq_n, "nt", tag + "_mm_dcq_n")
    dcqn = matmul(dqr, w_uq_r, "nt", tag + "_mm_dcq_r", add=dcqn)
    g_uq_n = matmul(cqn, dqn, "tn", tag + "_mm_guq_n")
    g_uq_r = matmul(cqn, dqr, "tn", tag + "_mm_guq_r")
    dckvn = matmul(dkv, wf["w_ukv"], "nt", tag + "_mm_dckv")
    sink.mm("w_ukv", ckvn, dkv, tag + "_mm_gukv")
    da, dqg, dkg = mla_post(a, dcqn, dckvn, dkr_pairs, dz, p["q_norm_g"], p["kv_norm_g"], cos, sins, tag + "_post")
    dhn = matmul(da, w_in, "nt", tag + "_mm_dhn")
    g_w_in = matmul(hn, da, "tn", tag + "_mm_gwin")
    sink.put("w_uq", _mla_w_uq_merge(g_uq_n, g_uq_r))
    sink.put("w_in", _mla_w_in_unperm(g_w_in))
    zero = sink.send()
    dh, dng = rms_bwd(h, p["norm_g"] + zero, dhn, dh_out, tag + "_rms_bwd")
    grads = {"norm_g": dng.reshape(-1), "q_norm_g": dqg.reshape(-1), "kv_norm_g": dkg.reshape(-1)}
    return dh, grads


ANY = pl.BlockSpec(memory_space=pl.ANY)


def _me():
    return lax.axis_index("x"), lax.axis_index("y"), lax.axis_index("c")


def _chip():
    return 2 * lax.axis_index("x") + lax.axis_index("y")


def _other_chips(x, y):
    return [(1 - x, y), (x, 1 - y), (1 - x, 1 - y)]


def _rcopy(src, dst, ssem, rsem, dev):
    return pltpu.make_async_remote_copy(src_ref=src, dst_ref=dst, send_sem=ssem, recv_sem=rsem,
                                        device_id=dev, device_id_type=MESH)


def _half(ref, c, hf):
    return ref.at[pl.ds(c * hf, hf), :]


HBM = pl.BlockSpec(memory_space=pltpu.HBM)
SEM = pl.BlockSpec(memory_space=pltpu.SEMAPHORE)
SPLIT_EFFECT = pltpu.SideEffectType.DATAFLOW_SIDE_EFFECTING


def gather_start(wb, after, name):
    nr, w = wb.shape
    hf = nr // 2

    def body(w_ref, land_ref, after_ref, ssem, rsem, w_thru, land_thru, token):
        x, y, c = _me()
        k = 2 * x + y
        for j, (cx, cy) in enumerate(_other_chips(x, y)):
            _rcopy(_half(w_ref, c, hf), _half(land_ref.at[k], c, hf), ssem.at[j], rsem.at[j], (cx, cy, c)).start()
        token[...] = jnp.zeros_like(token)

    land = lax.empty((N_CHIPS, nr, w), wb.dtype)
    return pl.pallas_call(
        body, name=name,
        out_shape=(pltpu.SemaphoreType.DMA((3,)), pltpu.SemaphoreType.DMA((3,)), pltpu.HBM(wb.shape, wb.dtype),
                   pltpu.HBM(land.shape, land.dtype), jax.ShapeDtypeStruct((8, LANES), F32)),
        in_specs=(HBM, HBM, ANY), out_specs=(SEM, SEM, HBM, HBM, pl.BlockSpec(memory_space=pltpu.VMEM)),
        input_output_aliases={0: 2, 1: 3},
        compiler_params=pltpu.CompilerParams(has_side_effects=SPLIT_EFFECT))(
        pltpu.with_memory_space_constraint(wb, pltpu.HBM), pltpu.with_memory_space_constraint(land, pltpu.HBM), after)


def gather_wait(ssem, rsem, w_thru, land_thru, after, name):
    nr, w = w_thru.shape
    hf = nr // 2

    def body(w_ref, land_ref, ssem_ref, rsem_ref, after_ref, w_dead, got_ref):
        x, y, c = _me()
        for j, (cx, cy) in enumerate(_other_chips(x, y)):
            cp = _rcopy(_half(w_ref, c, hf), _half(land_ref.at[2 * cx + cy], c, hf), ssem_ref.at[j], rsem_ref.at[j],
                        (cx, cy, c))
            cp.wait_send()
            cp.wait_recv()

    return pl.pallas_call(
        body, name=name, out_shape=(pltpu.HBM(w_thru.shape, w_thru.dtype), pltpu.HBM(land_thru.shape, land_thru.dtype)),
        in_specs=(HBM, HBM, SEM, SEM, ANY), out_specs=(HBM, HBM), input_output_aliases={0: 0, 1: 1},
        compiler_params=pltpu.CompilerParams(has_side_effects=SPLIT_EFFECT))(w_thru, land_thru, ssem, rsem, after)[1]


def gather_handover(land, wb, name):
    _, nr, w = land.shape
    hf = nr // 2

    def body(l_ref, o_ref, ssem, rsem):
        x, y, c = _me()
        chips = _other_chips(x, y)
        sends = []
        for j, (cx, cy) in enumerate(chips):
            region = _half(o_ref.at[2 * cx + cy], c, hf)
            sends.append(_rcopy(region, region, ssem.at[j], rsem.at[j], (x, y, 1 - c)))
            sends[-1].start()
        for j, (cx, cy) in enumerate(chips):
            region = _half(o_ref.at[2 * cx + cy], 1 - c, hf)
            _rcopy(region, region, ssem.at[j], rsem.at[j], (x, y, 1 - c)).wait_recv()
        for cp in sends:
            cp.wait_send()

    out = pl.pallas_call(
        body, name=name, in_specs=[ANY], out_specs=ANY, input_output_aliases={0: 0},
        out_shape=jax.ShapeDtypeStruct(land.shape, land.dtype),
        scratch_shapes=[pltpu.SemaphoreType.DMA((3,)), pltpu.SemaphoreType.DMA((3,))])(land)
    return lax.dynamic_update_slice(out, wb[None], (_chip(), 0, 0))


def pair_start(p, after, name):
    _, nr, w = p.shape
    hf = nr // 2

    def body(p_ref, land_ref, after_ref, ssem, rsem, p_thru, land_thru, token):
        x, y, c = _me()
        _rcopy(p_ref.at[:, pl.ds((1 - c) * hf, hf), :], land_ref, ssem, rsem, (x, y, 1 - c)).start()
        token[...] = jnp.zeros_like(token)

    land = lax.empty((N_CHIPS, hf, w), p.dtype)
    return pl.pallas_call(
        body, name=name,
        out_shape=(pltpu.SemaphoreType.DMA(()), pltpu.SemaphoreType.DMA(()), pltpu.HBM(p.shape, p.dtype),
                   pltpu.HBM(land.shape, land.dtype), jax.ShapeDtypeStruct((8, LANES), F32)),
        in_specs=(HBM, HBM, ANY), out_specs=(SEM, SEM, HBM, HBM, pl.BlockSpec(memory_space=pltpu.VMEM)),
        input_output_aliases={0: 2, 1: 3},
        compiler_params=pltpu.CompilerParams(has_side_effects=SPLIT_EFFECT))(
        pltpu.with_memory_space_constraint(p, pltpu.HBM), pltpu.with_memory_space_constraint(land, pltpu.HBM), after)


def pair_wait(ssem, rsem, p_thru, land_thru, after, name):
    hf = land_thru.shape[1]

    def body(p_ref, land_ref, ssem_ref, rsem_ref, after_ref, p_out, got_ref):
        x, y, c = _me()
        cp = _rcopy(p_ref.at[:, pl.ds((1 - c) * hf, hf), :], land_ref, ssem_ref, rsem_ref, (x, y, 1 - c))
        cp.wait_send()
        cp.wait_recv()

    return pl.pallas_call(
        body, name=name, out_shape=(pltpu.HBM(p_thru.shape, p_thru.dtype), pltpu.HBM(land_thru.shape, land_thru.dtype)),
        in_specs=(HBM, HBM, SEM, SEM, ANY), out_specs=(HBM, HBM), input_output_aliases={0: 0, 1: 1},
        compiler_params=pltpu.CompilerParams(has_side_effects=SPLIT_EFFECT))(p_thru, land_thru, ssem, rsem, after)


def reduce_start(t, after, name):
    def body(t_ref, land_ref, after_ref, ssem, rsem, t_thru, land_thru, token):
        x, y, c = _me()
        k = 2 * x + y
        for j, (cx, cy) in enumerate(_other_chips(x, y)):
            _rcopy(t_ref.at[2 * cx + cy], land_ref.at[k], ssem.at[j], rsem.at[j], (cx, cy, c)).start()
        token[...] = jnp.zeros_like(token)

    land = lax.empty(t.shape, t.dtype)
    return pl.pallas_call(
        body, name=name,
        out_shape=(pltpu.SemaphoreType.DMA((3,)), pltpu.SemaphoreType.DMA((3,)), pltpu.HBM(t.shape, t.dtype),
                   pltpu.HBM(t.shape, t.dtype), jax.ShapeDtypeStruct((8, LANES), F32)),
        in_specs=(HBM, HBM, ANY), out_specs=(SEM, SEM, HBM, HBM, pl.BlockSpec(memory_space=pltpu.VMEM)),
        input_output_aliases={0: 2, 1: 3},
        compiler_params=pltpu.CompilerParams(has_side_effects=SPLIT_EFFECT))(
        pltpu.with_memory_space_constraint(t, pltpu.HBM), pltpu.with_memory_space_constraint(land, pltpu.HBM), after)


def bcast_start(g, after, name):
    def body(g_ref, land_ref, after_ref, ssem, rsem, g_thru, land_thru, token):
        x, y, c = _me()
        k = 2 * x + y
        for j, (cx, cy) in enumerate(_other_chips(x, y)):
            _rcopy(g_ref, land_ref.at[k], ssem.at[j], rsem.at[j], (cx, cy, c)).start()
        token[...] = jnp.zeros_like(token)

    land = lax.empty((N_CHIPS,) + g.shape, g.dtype)
    return pl.pallas_call(
        body, name=name,
        out_shape=(pltpu.SemaphoreType.DMA((3,)), pltpu.SemaphoreType.DMA((3,)), pltpu.HBM(g.shape, g.dtype),
                   pltpu.HBM(land.shape, land.dtype), jax.ShapeDtypeStruct((8, LANES), F32)),
        in_specs=(HBM, HBM, ANY), out_specs=(SEM, SEM, HBM, HBM, pl.BlockSpec(memory_space=pltpu.VMEM)),
        input_output_aliases={0: 2, 1: 3},
        compiler_params=pltpu.CompilerParams(has_side_effects=SPLIT_EFFECT))(
        pltpu.with_memory_space_constraint(g, pltpu.HBM), pltpu.with_memory_space_constraint(land, pltpu.HBM), after)


def bcast_wait(ssem, rsem, g_thru, land_thru, after, name):
    def body(g_ref, land_ref, ssem_ref, rsem_ref, after_ref, g_out, got_ref):
        x, y, c = _me()
        for j, (cx, cy) in enumerate(_other_chips(x, y)):
            cp = _rcopy(g_ref, land_ref.at[2 * cx + cy], ssem_ref.at[j], rsem_ref.at[j], (cx, cy, c))
            cp.wait_send()
            cp.wait_recv()

    g, land = pl.pallas_call(
        body, name=name, out_shape=(pltpu.HBM(g_thru.shape, g_thru.dtype), pltpu.HBM(land_thru.shape, land_thru.dtype)),
        in_specs=(HBM, HBM, SEM, SEM, ANY), out_specs=(HBM, HBM), input_output_aliases={0: 0, 1: 1},
        compiler_params=pltpu.CompilerParams(has_side_effects=SPLIT_EFFECT))(g_thru, land_thru, ssem, rsem, after)
    return lax.dynamic_update_slice(land, g[None], (_chip(), 0, 0))


def reduce_wait(ssem, rsem, t_thru, land_thru, after, name):
    def body(t_ref, land_ref, ssem_ref, rsem_ref, after_ref, t_out, got_ref):
        x, y, c = _me()
        k = 2 * x + y
        for j, (cx, cy) in enumerate(_other_chips(x, y)):
            cp = _rcopy(t_ref.at[k], land_ref.at[2 * cx + cy], ssem_ref.at[j], rsem_ref.at[j], (cx, cy, c))
            cp.wait_send()
            cp.wait_recv()

    return pl.pallas_call(
        body, name=name, out_shape=(pltpu.HBM(t_thru.shape, t_thru.dtype), pltpu.HBM(land_thru.shape, land_thru.dtype)),
        in_specs=(HBM, HBM, SEM, SEM, ANY), out_specs=(HBM, HBM), input_output_aliases={0: 0, 1: 1},
        compiler_params=pltpu.CompilerParams(has_side_effects=SPLIT_EFFECT))(t_thru, land_thru, ssem, rsem, after)


def grads_to_sibling(ps, name="grads_to_sibling"):
    n = len(ps)

    def body(*refs):
        p_refs, o_refs, ssem, rsem = refs[:n], refs[n:2 * n], refs[2 * n], refs[2 * n + 1]
        x, y, c = _me()
        cps = []
        for a in range(n):
            hf = ps[a].shape[1] // 2
            cps.append(_rcopy(p_refs[a].at[:, pl.ds((1 - c) * hf, hf), :], o_refs[a], ssem.at[a], rsem.at[a],
                              (x, y, 1 - c)))
        for cp in cps:
            cp.start()
        for cp in cps:
            cp.wait()

    return pl.pallas_call(
        body, name=name, in_specs=[ANY] * n, out_specs=[ANY] * n,
        out_shape=[jax.ShapeDtypeStruct((N_CHIPS, p.shape[1] // 2, p.shape[2]), p.dtype) for p in ps],
        scratch_shapes=[pltpu.SemaphoreType.DMA((n,)), pltpu.SemaphoreType.DMA((n,))])(*ps)


def pair_sum(p, ra, out_dtype, name):
    _, nr, w = p.shape
    hf = nr // 2
    tr = _pick_rows(hf, cap=max(512, SUM_BLOCK_BYTES // (4 * w)))
    nb = hf // tr

    def body(c_ref, p_ref, r_ref, o_ref):
        o_ref[...] = (p_ref[...] + r_ref[...]).astype(out_dtype)

    c = lax.axis_index("c").astype(jnp.int32).reshape(1)
    return pl.pallas_call(
        body, name=name,
        grid_spec=pltpu.PrefetchScalarGridSpec(
            num_scalar_prefetch=1, grid=(N_CHIPS, nb),
            in_specs=[pl.BlockSpec((1, tr, w), lambda k, i, c_ref: (k, c_ref[0] * nb + i, 0)),
                      pl.BlockSpec((1, tr, w), lambda k, i, c_ref: (k, i, 0))],
            out_specs=pl.BlockSpec((1, tr, w), lambda k, i, c_ref: (k, i, 0))),
        out_shape=jax.ShapeDtypeStruct((N_CHIPS, hf, w), out_dtype),
        compiler_params=_cparams(("parallel", "parallel")))(c, p, ra)


def grads_across_chips(ts):
    n = len(ts)

    def body(*refs):
        t_refs, o_refs, ssem, rsem = refs[:n], refs[n:2 * n], refs[2 * n], refs[2 * n + 1]
        x, y, c = _me()
        k = 2 * x + y
        chips = _other_chips(x, y)
        sends = [_rcopy(t_refs[a].at[2 * cx + cy], o_refs[a].at[k], ssem.at[3 * a + j], rsem.at[3 * a + j], (cx, cy, c))
                 for a in range(n) for j, (cx, cy) in enumerate(chips)]
        for cp in sends:
            cp.start()
        for a in range(n):
            for j, (cx, cy) in enumerate(chips):
                _rcopy(t_refs[a].at[k], o_refs[a].at[2 * cx + cy], ssem.at[3 * a + j], rsem.at[3 * a + j],
                       (cx, cy, c)).wait_recv()
        for cp in sends:
            cp.wait_send()

    return pl.pallas_call(
        body, name="grads_across_chips", in_specs=[ANY] * n, out_specs=[ANY] * n,
        out_shape=[jax.ShapeDtypeStruct(t.shape, t.dtype) for t in ts],
        scratch_shapes=[pltpu.SemaphoreType.DMA((3 * n,)), pltpu.SemaphoreType.DMA((3 * n,))])(*ts)


def chip_sum(t, rb, name):
    _, hf, w = rb.shape
    tr = _pick_rows(hf, cap=max(512, SUM_BLOCK_BYTES // (4 * w)))
    nb = hf // tr

    def body(kc_ref, t_ref, r_ref, o_ref):
        k = kc_ref[0]
        acc = jnp.where(k == 0, t_ref[0], r_ref[0]).astype(F32)
        for j in range(1, N_CHIPS):
            acc = acc + jnp.where(k == j, t_ref[0], r_ref[j]).astype(F32)
        o_ref[...] = acc

    kc = jnp.stack([_chip(), lax.axis_index("c")]).astype(jnp.int32)
    return pl.pallas_call(
        body, name=name,
        grid_spec=pltpu.PrefetchScalarGridSpec(
            num_scalar_prefetch=1, grid=(nb,),
            in_specs=[pl.BlockSpec((1, tr, w), lambda i, kc_ref: (kc_ref[0], i, 0)),
                      pl.BlockSpec((N_CHIPS, tr, w), lambda i, kc_ref: (0, i, 0))],
            out_specs=pl.BlockSpec((tr, w), lambda i, kc_ref: (kc_ref[1] * nb + i, 0))),
        out_shape=jax.ShapeDtypeStruct((2 * hf, w), F32), compiler_params=_cparams(("parallel",)))(kc, t, rb)


def reduced_to_sibling(gs):
    n = len(gs)

    def body(*refs):
        o_refs, ssem, rsem = refs[n:2 * n], refs[2 * n], refs[2 * n + 1]
        x, y, c = _me()
        cps = []
        for a in range(n):
            hf = gs[a].shape[0] // 2
            cps.append(_rcopy(_half(o_refs[a], c, hf), _half(o_refs[a], c, hf), ssem.at[a], rsem.at[a], (x, y, 1 - c)))
        for cp in cps:
            cp.start()
        for a in range(n):
            hf = gs[a].shape[0] // 2
            _rcopy(_half(o_refs[a], c, hf), _half(o_refs[a], 1 - c, hf), ssem.at[a], rsem.at[a],
                   (x, y, 1 - c)).wait_recv()
        for cp in cps:
            cp.wait_send()

    return pl.pallas_call(
        body, name="reduced_to_sibling", in_specs=[ANY] * n, out_specs=[ANY] * n,
        input_output_aliases={a: a for a in range(n)},
        out_shape=[jax.ShapeDtypeStruct(g.shape, g.dtype) for g in gs],
        scratch_shapes=[pltpu.SemaphoreType.DMA((n,)), pltpu.SemaphoreType.DMA((n,))])(*gs)


def _adamw_step(w_ref, g_ref, m_ref, v_ref, d_ref, nm_ref, nv_ref):
    bc1 = 1.0 - ADAM_B1 ** ADAM_STEP
    bc2 = 1.0 - ADAM_B2 ** ADAM_STEP
    gv = g_ref[...]
    nm = ADAM_B1 * m_ref[...] + (1.0 - ADAM_B1) * gv
    nv = ADAM_B2 * v_ref[...] + (1.0 - ADAM_B2) * (gv * gv)
    nm_ref[...] = nm
    nv_ref[...] = nv
    d_ref[...] = -ADAM_LR * ((nm / bc1) / (jnp.sqrt(nv / bc2) + ADAM_EPS) + ADAM_WD * w_ref[...])


def adamw_packed(w, g_buf, r0, m, v, name):
    r, c = w.shape
    tr = _tile_rows(r, r0, (1024, 512, 384, 256, 128))

    def body(w_ref, g_ref, m_ref, v_ref, go_ref, d_ref, nm_ref, nv_ref):
        go_ref[...] = g_ref[...]
        _adamw_step(w_ref, g_ref, m_ref, v_ref, d_ref, nm_ref, nv_ref)

    own = pl.BlockSpec((tr, CHUNK_W), lambda i, j: (i, j))
    packed = pl.BlockSpec((tr, CHUNK_W), lambda i, j: ((r0 + j * r) // tr + i, 0))
    st = jax.ShapeDtypeStruct((r, c), F32)
    return pl.pallas_call(body, name=name, grid=(r // tr, c // CHUNK_W), in_specs=[own, packed, own, own],
                          out_specs=[own] * 4, out_shape=[st] * 4,
                          compiler_params=_cparams(("parallel", "parallel")))(w, g_buf, m, v)


def adamw(w, g, m, v, name):
    r, wd = w.shape
    tr = _pick_rows(r, cap=max(16, ADAMW_BLOCK_BYTES // (4 * wd)))
    body = functools.partial(_adamw_step)

    spec = pl.BlockSpec((tr, wd), lambda i: (i, 0))
    st = jax.ShapeDtypeStruct((r, wd), F32)
    return pl.pallas_call(body, name=name, grid=(r // tr,), in_specs=[spec] * 4, out_specs=[spec] * 3,
                          out_shape=[st, st, st], compiler_params=_cparams(("parallel",)))(w, g, m, v)


LAYER_KINDS = ("gmlp", "s5", "mla", "gmlp")
PARAMS = {
    "gmlp": ("norm_g", "w_in", "ln_g", "ln_b", "w_s", "b_s", "w_out"),
    "s5": ("norm_g", "w_in", "a_re", "a_im", "log_step", "b_re", "b_im", "c_re", "c_im", "d_skip", "w_glu", "b_glu", "w_out"),
    "mla": ("norm_g", "w_in", "q_norm_g", "w_uq", "kv_norm_g", "w_ukv", "w_out"),
}
COL_SHARDED = ("w_in", "w_uq", "w_ukv")
ROW_SHARDED = ("w_out", "w_glu")
WEIGHT_NAMES = [("l%d_" % i) + n for i, kind in enumerate(LAYER_KINDS) for n in PARAMS[kind]] + ["final_norm_g"]


def _is_big(name):
    return name.split("_", 1)[1] in COL_SHARDED + ROW_SHARDED


BIG = [n for n in WEIGHT_NAMES if _is_big(n)]
SMALL = [n for n in WEIGHT_NAMES if not _is_big(n)]


def _pack_rows(blocks):
    return jnp.concatenate([b.reshape(-1, PACK_W) for b in blocks], axis=0)


def _shard_major(wn, full, width):
    r, c = full.shape
    if wn in COL_SHARDED:
        t = full.reshape(r, N_CHIPS, c // N_CHIPS).transpose(1, 0, 2)
    else:
        t = full.reshape(N_CHIPS, r // N_CHIPS, c)
    return t.reshape(N_CHIPS, -1, width)


def _from_shard_major(name, t, block_shape):
    r, c = block_shape
    if name.split("_", 1)[1] in COL_SHARDED:
        return t.reshape(N_CHIPS, r, c).transpose(1, 0, 2).reshape(r, N_CHIPS * c)
    return t.reshape(N_CHIPS * r, c)


class BigGradSink:
    ORDER = ("w_out", "w_glu", "w_ukv", "w_uq", "w_in")
    ROW_MAJOR = {2: ("w_uq", "w_in")}

    def __init__(self, layer, block_shapes, above):
        self.layer = layer
        self.above = above
        self.zero_above = 0.0
        self.regions = {}
        r0 = 0
        for wn in self.ORDER:
            if wn in block_shapes:
                shape = block_shapes[wn]
                self.regions[wn] = (r0, shape, wn not in self.ROW_MAJOR.get(layer, ()))
                r0 += shape[0] * shape[1] // CHUNK_W
        self.buf = lax.empty((N_CHIPS, r0, CHUNK_W), F32)
        self.flight = None

    def mm(self, wn, a, b, name):
        r0, _, direct = self.regions[wn]
        assert direct
        self.buf = matmul_tn_packed(a, b, self.buf, r0, wn in COL_SHARDED, name)

    def put(self, wn, full):
        r0, _, direct = self.regions[wn]
        assert not direct
        piece = _shard_major(wn, full, CHUNK_W)
        self.buf = lax.dynamic_update_slice(self.buf, piece, (0, r0, 0))

    def forward_above(self, after):
        if self.above is not None:
            self.zero_above = self.above.forward(after)

    def send(self):
        self.pair = pair_start(self.buf, jnp.zeros((8, LANES), F32), "pair_l%d_start" % self.layer)
        return self.pair[4][0, 0] + self.zero_above

    def forward(self, after):
        i = self.layer
        p, sib = pair_wait(*self.pair[:4], after, "pair_l%d_wait" % i)
        t = pair_sum(p, sib, BF16, "pair_sum_l%d" % i)
        self.flight = reduce_start(t, sib, "reduce_l%d_start" % i)
        return self.flight[4][0, 0]


def _small_pack(arrs, total_padded):
    flat = jnp.concatenate([a.reshape(-1) for a in arrs])
    return jnp.pad(flat, (0, total_padded - flat.shape[0]))


def kernel(x, positions, l0_norm_g, l0_w_in, l0_ln_g, l0_ln_b, l0_w_s, l0_b_s, l0_w_out, l1_norm_g, l1_w_in, l1_a_re, l1_a_im, l1_log_step, l1_b_re, l1_b_im, l1_c_re, l1_c_im, l1_d_skip, l1_w_glu, l1_b_glu, l1_w_out, l2_norm_g, l2_w_in, l2_q_norm_g, l2_w_uq, l2_kv_norm_g, l2_w_ukv, l2_w_out, l3_norm_g, l3_w_in, l3_ln_g, l3_ln_b, l3_w_s, l3_b_s, l3_w_out, final_norm_g, loss_target, m_l0_norm_g, m_l0_w_in, m_l0_ln_g, m_l0_ln_b, m_l0_w_s, m_l0_b_s, m_l0_w_out, m_l1_norm_g, m_l1_w_in, m_l1_a_re, m_l1_a_im, m_l1_log_step, m_l1_b_re, m_l1_b_im, m_l1_c_re, m_l1_c_im, m_l1_d_skip, m_l1_w_glu, m_l1_b_glu, m_l1_w_out, m_l2_norm_g, m_l2_w_in, m_l2_q_norm_g, m_l2_w_uq, m_l2_kv_norm_g, m_l2_w_ukv, m_l2_w_out, m_l3_norm_g, m_l3_w_in, m_l3_ln_g, m_l3_ln_b, m_l3_w_s, m_l3_b_s, m_l3_w_out, m_final_norm_g, v_l0_norm_g, v_l0_w_in, v_l0_ln_g, v_l0_ln_b, v_l0_w_s, v_l0_b_s, v_l0_w_out, v_l1_norm_g, v_l1_w_in, v_l1_a_re, v_l1_a_im, v_l1_log_step, v_l1_b_re, v_l1_b_im, v_l1_c_re, v_l1_c_im, v_l1_d_skip, v_l1_w_glu, v_l1_b_glu, v_l1_w_out, v_l2_norm_g, v_l2_w_in, v_l2_q_norm_g, v_l2_w_uq, v_l2_kv_norm_g, v_l2_w_ukv, v_l2_w_out, v_l3_norm_g, v_l3_w_in, v_l3_ln_g, v_l3_ln_b, v_l3_w_s, v_l3_b_s, v_l3_w_out, v_final_norm_g):
    args = locals()
    w = {n: args[n] for n in WEIGHT_NAMES}
    mom_m = {n: args["m_" + n] for n in WEIGHT_NAMES}
    mom_v = {n: args["v_" + n] for n in WEIGHT_NAMES}
    h0 = x[0]
    target = loss_target[0]
    pos = positions.reshape(-1, 1)

    full = {}

    def pack_unit(layers):
        names = [n for n in BIG if int(n[1]) in layers]
        rows = [w[n].size // PACK_W for n in names]
        pad = -sum(rows) % PACK_ROW_ALIGN
        return names, rows, _pack_rows([w[n].astype(BF16) for n in names] + [jnp.zeros((pad, PACK_W), BF16)])

    def unpack_unit(names, rows, gathered):
        r0 = 0
        for n, nr in zip(names, rows):
            full[n] = _from_shard_major(n, gathered[:, r0:r0 + nr, :], w[n].shape)
            r0 += nr

    unit0, unit1, unit2 = pack_unit((0,)), pack_unit((1,)), pack_unit((2, 3))
    wp = dict(w)

    def layer_params(i):
        pre = "l%d_" % i
        p = {k[len(pre):]: v for k, v in wp.items() if k.startswith(pre)}
        wf = {k[len(pre):]: v for k, v in full.items() if k.startswith(pre)}
        return p, wf

    flight = gather_start(unit0[2], unit1[2], "gather_l0_start")
    cos, sins = rope_tables(pos, flight[4][0, 0])
    wp["l1_a_re"] = w["l1_a_re"] + flight[4][0, 0]
    s5_weights = _s5_weights(layer_params(1)[0])
    land = gather_wait(*flight[:4], s5_weights[2], "gather_l0_wait")
    got = gather_handover(land, unit0[2], "gather_l0_handover")
    unpack_unit(unit0[0], unit0[1], got)
    flight = gather_start(unit1[2], got, "gather_l1_start")
    wp["l0_norm_g"] = w["l0_norm_g"] + flight[4][0, 0]

    h = h0
    saved = []
    for i, kind in enumerate(LAYER_KINDS):
        if i == 1:
            land = gather_wait(*flight[:4], h, "gather_l1_wait")
            got = gather_handover(land, unit1[2], "gather_l1_handover")
            unpack_unit(unit1[0], unit1[1], got)
            flight = gather_start(unit2[2], got, "gather_l23_start")
            wp["l1_norm_g"] = w["l1_norm_g"] + flight[4][0, 0]
        if i == 2:
            land = gather_wait(*flight[:4], h, "gather_l23_wait")
            unpack_unit(unit2[0], unit2[1], gather_handover(land, unit2[2], "gather_l23_handover"))
        p, wf = layer_params(i)
        tag = "l%d" % i
        if kind == "gmlp":
            h, s = gmlp_layer_fwd(h, p, wf, tag)
        elif kind == "s5":
            h, s = s5_layer_fwd(h, p, wf, s5_weights, tag)
        else:
            h, s = mla_layer_fwd(h, p, wf, cos, sins, tag)
        saved.append(s)
    loss_part, dh, g_final = loss_head(h, final_norm_g, target)

    grads = {"final_norm_g": g_final.reshape(-1)}
    sinks = {}

    for i in reversed(range(len(LAYER_KINDS))):
        kind = LAYER_KINDS[i]
        p, wf = layer_params(i)
        tag = "l%d" % i
        sink = sinks[i] = BigGradSink(i, {n[3:]: w[n].shape for n in BIG if int(n[1]) == i}, sinks.get(i + 1))
        if kind == "gmlp":
            dh, g = gmlp_layer_bwd(dh, saved[i], p, wf, tag, sink)
        elif kind == "s5":
            dh, g = s5_layer_bwd(dh, saved[i], p, wf, tag, sink)
        else:
            dh, g = mla_layer_bwd(dh, saved[i], p, wf, cos, sins, tag, sink)
        for k, val in g.items():
            grads["l%d_%s" % (i, k)] = val
    sinks[0].forward(dh)
    grad_x = dh[None]

    n_small = sum(w[n].size for n in SMALL)
    piece = N_CHIPS * 2 * 16 * PACK_W
    n_small_pad = -(-(n_small + 1) // piece) * piece
    nrs = n_small_pad // N_CHIPS // PACK_W
    p_small = _small_pack([grads[n] for n in SMALL] + [loss_part], n_small_pad).reshape(N_CHIPS, nrs, PACK_W)
    sib_small, = grads_to_sibling([p_small], "grads_to_sibling_small")
    t_small = pair_sum(p_small, sib_small, F32, "pair_sum_small")
    rb_small, = grads_across_chips([t_small])
    halves = [chip_sum(t_small, rb_small, "chip_sum_small")]

    after = halves[0]
    for i in reversed(range(len(LAYER_KINDS))):
        t_i, rb_i = reduce_wait(*sinks[i].flight[:4], after, "reduce_l%d_wait" % i)
        halves.append(chip_sum(t_i, rb_i, "chip_sum_l%d" % i))
        after = halves[-1]
    reduced = reduced_to_sibling(halves)
    small_flight = bcast_start(reduced[0], reduced[1], "small_allgather_start")

    g_out, d_out, nm_out, nv_out = {}, {}, {}, {}
    for i, g_i in zip(reversed(range(len(LAYER_KINDS))), reduced[1:]):
        for wn, (r0, shape, direct) in sinks[i].regions.items():
            n = "l%d_%s" % (i, wn)
            if direct:
                g_out[n], d_out[n], nm_out[n], nv_out[n] = adamw_packed(w[n], g_i, r0, mom_m[n], mom_v[n], "adamw_" + n)
            else:
                g_out[n] = g_i[r0:r0 + shape[0] * shape[1] // CHUNK_W].reshape(shape)
                d_out[n], nm_out[n], nv_out[n] = adamw(w[n], g_out[n], mom_m[n], mom_v[n], "adamw_" + n)
    small_all = bcast_wait(*small_flight[:4], nv_out["l0_w_in"], "small_allgather_wait")
    g_small = small_all.reshape(-1, PACK_W)
    sp = lambda d: _small_pack([d[n] for n in SMALL], n_small_pad).reshape(-1, PACK_W)
    d_small, nm_small, nv_small = adamw(sp(w), g_small, sp(mom_m), sp(mom_v), "adamw_small")
    for buf, out in ((g_small, g_out), (d_small, d_out), (nm_small, nm_out), (nv_small, nv_out)):
        flat = buf.reshape(-1)
        o = 0
        for n in SMALL:
            out[n] = flat[o:o + w[n].size].reshape(w[n].shape)
            o += w[n].size
    loss = g_small.reshape(-1)[n_small]
    return (loss, grad_x, *[g_out[n] for n in WEIGHT_NAMES], *[d_out[n] for n in WEIGHT_NAMES],
            *[nm_out[n] for n in WEIGHT_NAMES], *[nv_out[n] for n in WEIGHT_NAMES])
```

```python
import functools
import math

import jax
import jax.numpy as jnp
import numpy as np
from jax import lax
from jax.experimental import pallas as pl
from jax.experimental.pallas import tpu as pltpu

F32 = jnp.float32
BF16 = jnp.bfloat16
MESH = pl.DeviceIdType.MESH
VMEM_LIMIT_BYTES = 56 * 1024 * 1024
LANES = 128
PACK_W = 1024
CHUNK_W = 256
PACK_ROW_ALIGN = 256
ROW_TILE = 512
ROW_TILE_HEAVY = 256
ROW_TILE_NARROW = 512
SUM_BLOCK_BYTES = 1024 * 1024
ADAMW_BLOCK_BYTES = 1024 * 1024
MM_BLOCK_BYTES = 12 * 1024 * 1024

NORM_EPS = 1e-6
N_CHIPS = 4
GMLP_CHUNK = 128
GMLP_GROUPS = 8
S5_GROUPS = 128
S5_GROUP = 16
S5_STATE = 64
S5_SB = 16
S5_SEG = 8
MLA_HEADS = 16
MLA_NOPE = 128
MLA_ROPE = 64
MLA_Q_RANK = 384
MLA_KV_RANK = 128
MLA_SCALE = (MLA_NOPE + MLA_ROPE) ** -0.5
ROPE_THETA = 10000.0
NEG_INF = -1e30
ADAM_LR, ADAM_B1, ADAM_B2, ADAM_EPS, ADAM_WD, ADAM_STEP = 0.001, 0.9, 0.999, 1e-08, 0.01, 10

DN_NN = (((1,), (0,)), ((), ()))
DN_NT = (((1,), (1,)), ((), ()))
DN_TN = (((0,), (0,)), ((), ()))


def _cparams(sem):
    return pltpu.CompilerParams(dimension_semantics=sem, vmem_limit_bytes=VMEM_LIMIT_BYTES)


def _pick(n, cands=(512, 384, 256, 128)):
    for c in cands:
        if n % c == 0:
            return c
    return n


def _pick_rows(r, cap=512, mult=16):
    return max(t for t in range(mult, cap + 1, mult) if r % t == 0)


def _dot(a, b, dn):
    return lax.dot_general(a.astype(BF16), b.astype(BF16), dn, preferred_element_type=F32)


def _sigmoid(x):
    return 0.5 + 0.5 * jnp.tanh(0.5 * x)


def _gelu(x):
    c = math.sqrt(2.0 / math.pi)
    t = jnp.tanh(c * (x + 0.044715 * x * x * x))
    return 0.5 * x * (1.0 + t)


def _gelu_grad(x):
    c = math.sqrt(2.0 / math.pi)
    t = jnp.tanh(c * (x + 0.044715 * x * x * x))
    return 0.5 * (1.0 + t) + 0.5 * x * (1.0 - t * t) * c * (1.0 + 3.0 * 0.044715 * x * x)


def _gelu_both(x):
    c = math.sqrt(2.0 / math.pi)
    t = jnp.tanh(c * (x + 0.044715 * x * x * x))
    return 0.5 * x * (1.0 + t), 0.5 * (1.0 + t) + 0.5 * x * (1.0 - t * t) * c * (1.0 + 3.0 * 0.044715 * x * x)


def _silu_both(z):
    s = _sigmoid(z)
    return z * s, s * (1.0 + z * (1.0 - s))


def _silu(z):
    return z * _sigmoid(z)


def matmul(a, b, mode, name, out_dtype=F32, add=None):
    if mode == "nn":
        (m, k), n = a.shape, b.shape[1]
    elif mode == "nt":
        (m, k), n = a.shape, b.shape[0]
    else:
        (k, m), n = a.shape, b.shape[1]
    tm = _pick(m, [t for t in (2048, 1024, 512, 384, 256, 128) if t * k * a.dtype.itemsize <= MM_BLOCK_BYTES])
    tn = _pick(n, [t for t in (512, 384, 256, 128) if t * k * b.dtype.itemsize <= MM_BLOCK_BYTES])
    dn = {"nn": DN_NN, "nt": DN_NT, "tn": DN_TN}[mode]

    def body(*refs):
        if add is None:
            a_ref, b_ref, o_ref = refs
        else:
            a_ref, b_ref, add_ref, o_ref = refs
        r = _dot(a_ref[...], b_ref[...], dn)
        if add is not None:
            r = r + add_ref[...].astype(F32)
        o_ref[...] = r.astype(out_dtype)

    a_spec = pl.BlockSpec((k, tm), lambda i, j: (0, i)) if mode == "tn" else pl.BlockSpec((tm, k), lambda i, j: (i, 0))
    b_spec = pl.BlockSpec((tn, k), lambda i, j: (j, 0)) if mode == "nt" else pl.BlockSpec((k, tn), lambda i, j: (0, j))
    o_spec = pl.BlockSpec((tm, tn), lambda i, j: (i, j))
    in_specs = [a_spec, b_spec] + ([o_spec] if add is not None else [])
    args = (a, b) + ((add,) if add is not None else ())
    return pl.pallas_call(
        body, name=name, grid=(m // tm, n // tn), in_specs=in_specs, out_specs=o_spec,
        out_shape=jax.ShapeDtypeStruct((m, n), out_dtype),
        compiler_params=_cparams(("parallel", "arbitrary")))(*args)


def _tile_rows(r, r0, cands=(512, 384, 256, 128)):
    return next(t for t in cands if r % t == 0 and r0 % t == 0)


def matmul_tn_packed(a, b, buf, r0, col_sharded, name):
    k, m = a.shape
    n = b.shape[1]
    if col_sharded:
        chunks = n // N_CHIPS // CHUNK_W
        tm = _tile_rows(m, r0, (1024, 512, 384, 256, 128))
        o_map = lambda i, j: (j // chunks, (r0 + (j % chunks) * m) // tm + i, 0)
    else:
        rs = m // N_CHIPS
        tm = _tile_rows(rs, r0)
        per = rs // tm
        o_map = lambda i, j: (i // per, (r0 + j * rs) // tm + i % per, 0)

    def body(a_ref, b_ref, buf_ref, o_ref):
        o_ref[0] = _dot(a_ref[...], b_ref[...], DN_TN)

    return pl.pallas_call(
        body, name=name, grid=(m // tm, n // CHUNK_W),
        in_specs=[pl.BlockSpec((k, tm), lambda i, j: (0, i)), pl.BlockSpec((k, CHUNK_W), lambda i, j: (0, j)),
                  pl.BlockSpec(memory_space=pl.ANY)],
        out_specs=pl.BlockSpec((1, tm, CHUNK_W), o_map), out_shape=jax.ShapeDtypeStruct(buf.shape, buf.dtype),
        input_output_aliases={2: 0}, compiler_params=_cparams(("parallel", "arbitrary")))(a, b, buf)


def _rows(tl, w, col=0):
    return pl.BlockSpec((tl, w), lambda i: (i, col))


def _full(shape):
    nd = len(shape)
    return pl.BlockSpec(tuple(shape), lambda i: (0,) * nd)


def _rowcall(body, name, n_steps, in_specs, out_specs, out_shape, scratch=()):
    return pl.pallas_call(
        body, name=name, grid=(n_steps,), in_specs=in_specs, out_specs=out_specs, out_shape=out_shape,
        scratch_shapes=list(scratch), compiler_params=_cparams(("arbitrary",)))


def _acc(ref, val, i):
    @pl.when(i == 0)
    def _():
        ref[...] = val

    @pl.when(i != 0)
    def _():
        ref[...] += val


def rms_fwd(h, g, name):
    l, d = h.shape
    tl = ROW_TILE_NARROW

    def body(h_ref, g_ref, o_ref):
        x = h_ref[...]
        r = lax.rsqrt(jnp.mean(x * x, axis=-1, keepdims=True) + NORM_EPS)
        o_ref[...] = (x * r * g_ref[...]).astype(BF16)

    return _rowcall(body, name, l // tl, [_rows(tl, d), _full((1, d))], _rows(tl, d),
                    jax.ShapeDtypeStruct((l, d), BF16))(h, g.reshape(1, d))


def rms_bwd(h, g, dhn, dh_in, name):
    l, d = h.shape
    tl = ROW_TILE_NARROW

    def body(h_ref, g_ref, dhn_ref, dhi_ref, dh_ref, dg_ref):
        i = pl.program_id(0)
        x = h_ref[...]
        r = lax.rsqrt(jnp.mean(x * x, axis=-1, keepdims=True) + NORM_EPS)
        xhat = x * r
        dy = dhn_ref[...]
        dxh = dy * g_ref[...]
        dx = r * (dxh - xhat * jnp.mean(dxh * xhat, axis=-1, keepdims=True))
        dh_ref[...] = dhi_ref[...] + dx
        _acc(dg_ref, jnp.sum(dy * xhat, axis=0, keepdims=True), i)

    return _rowcall(body, name, l // tl, [_rows(tl, d), _full((1, d)), _rows(tl, d), _rows(tl, d)],
                    [_rows(tl, d), _full((1, d))],
                    [jax.ShapeDtypeStruct((l, d), F32), jax.ShapeDtypeStruct((1, d), F32)])(h, g.reshape(1, d), dhn, dh_in)


def loss_head(h, g, target):
    l, d = h.shape
    tl = ROW_TILE_NARROW

    def body(h_ref, g_ref, t_ref, loss_ref, dh_ref, dg_ref):
        i = pl.program_id(0)
        x = h_ref[...]
        gg = g_ref[...]
        r = lax.rsqrt(jnp.mean(x * x, axis=-1, keepdims=True) + NORM_EPS)
        xhat = x * r
        err = xhat * gg - t_ref[...]
        part = 0.5 * jnp.sum(jnp.mean(err * err, axis=-1, keepdims=True), axis=0, keepdims=True)
        _acc(loss_ref, part, i)
        dy = err * (1.0 / d)
        dxh = dy * gg
        dh_ref[...] = r * (dxh - xhat * jnp.mean(dxh * xhat, axis=-1, keepdims=True))
        _acc(dg_ref, jnp.sum(dy * xhat, axis=0, keepdims=True), i)

    return _rowcall(body, "loss_head", l // tl, [_rows(tl, d), _full((1, d)), _rows(tl, d)],
                    [_full((1, 1)), _rows(tl, d), _full((1, d))],
                    [jax.ShapeDtypeStruct((1, 1), F32), jax.ShapeDtypeStruct((l, d), F32),
                     jax.ShapeDtypeStruct((1, d), F32)])(h, g.reshape(1, d), target)


def _gmlp_common(a_ref, lng_ref, lnb_ref):
    di = lng_ref.shape[1]
    u_pre = a_ref[:, 0:di]
    v_pre = a_ref[:, di:2 * di]
    z = a_ref[:, 2 * di:3 * di]
    vg = _gelu(v_pre)
    mu = jnp.mean(vg, axis=-1, keepdims=True)
    xc = vg - mu
    rstd = lax.rsqrt(jnp.mean(xc * xc, axis=-1, keepdims=True) + NORM_EPS)
    vhat = xc * rstd
    vn = vhat * lng_ref[...] + lnb_ref[...]
    return u_pre, v_pre, z, vhat, rstd, vn


def _tril(w):
    r = lax.broadcasted_iota(jnp.int32, w.shape, 0)
    c = lax.broadcasted_iota(jnp.int32, w.shape, 1)
    return jnp.where(c <= r, w, 0.0)


def gmlp_gate_fwd(a, ln_g, ln_b, w_s, b_s, name):
    l, w3 = a.shape
    di = w3 // 3
    dg = di // GMLP_GROUPS
    tl = GMLP_CHUNK

    def body(a_ref, lng_ref, lnb_ref, ws_ref, bs_ref, m_ref):
        u_pre, _, z, _, _, vn = _gmlp_common(a_ref, lng_ref, lnb_ref)
        gate = _gelu(u_pre) * _silu(z)
        for g in range(GMLP_GROUPS):
            sl = slice(g * dg, (g + 1) * dg)
            s = _dot(_tril(ws_ref[g]), vn[:, sl], DN_NN) + bs_ref[g]
            m_ref[:, sl] = (gate[:, sl] * s).astype(BF16)

    return _rowcall(body, name, l // tl,
                    [_rows(tl, w3), _full((1, di)), _full((1, di)), _full(w_s.shape), _full((GMLP_GROUPS, tl, 1))],
                    _rows(tl, di), jax.ShapeDtypeStruct((l, di), BF16))(
        a, ln_g.reshape(1, di), ln_b.reshape(1, di), w_s, b_s.reshape(GMLP_GROUPS, tl, 1))


def gmlp_gate_bwd(a, dm, ln_g, ln_b, w_s, b_s, name):
    l, w3 = a.shape
    di = w3 // 3
    dg = di // GMLP_GROUPS
    tl = GMLP_CHUNK

    def body(a_ref, dm_ref, lng_ref, lnb_ref, ws_ref, bs_ref, da_ref, dlg_ref, dlb_ref, dws_ref, dbs_ref,
             dvn_ref, vh_ref, gv_ref):
        i = pl.program_id(0)
        vg, gv = _gelu_both(a_ref[:, di:2 * di])
        gv_ref[...] = gv
        xc = vg - jnp.mean(vg, axis=-1, keepdims=True)
        rstd = lax.rsqrt(jnp.mean(xc * xc, axis=-1, keepdims=True) + NORM_EPS)
        vh_ref[...] = xc * rstd
        for g in range(GMLP_GROUPS):
            sl = slice(g * dg, (g + 1) * dg)
            wt = _tril(ws_ref[g])
            vn_g = vh_ref[:, sl] * lng_ref[:, sl] + lnb_ref[:, sl]
            s = _dot(wt, vn_g, DN_NN) + bs_ref[g]
            dmg = dm_ref[:, sl]
            u, gu = _gelu_both(a_ref[:, sl])
            sz, gz = _silu_both(a_ref[:, 2 * di + g * dg:2 * di + (g + 1) * dg])
            ds = dmg * u * sz
            da_ref[:, sl] = (dmg * s * sz * gu).astype(BF16)
            da_ref[:, 2 * di + g * dg:2 * di + (g + 1) * dg] = (dmg * u * s * gz).astype(BF16)
            dvn_ref[:, sl] = _dot(wt, ds, DN_TN)
            dw = _tril(_dot(ds, vn_g, DN_NT))
            db = jnp.sum(ds, axis=1, keepdims=True)

            @pl.when(i == 0)
            def _():
                dws_ref[g] = dw
                dbs_ref[g] = db

            @pl.when(i != 0)
            def _():
                dws_ref[g] += dw
                dbs_ref[g] += db

        dvn = dvn_ref[...]
        vhat = vh_ref[...]
        dxh = dvn * lng_ref[...]
        dvg = rstd * (dxh - jnp.mean(dxh, axis=-1, keepdims=True) - vhat * jnp.mean(dxh * vhat, axis=-1, keepdims=True))
        da_ref[:, di:2 * di] = (dvg * gv_ref[...]).astype(BF16)
        _acc(dlg_ref, jnp.sum(dvn * vhat, axis=0, keepdims=True), i)
        _acc(dlb_ref, jnp.sum(dvn, axis=0, keepdims=True), i)

    outs = _rowcall(
        body, name, l // tl,
        [_rows(tl, w3), _rows(tl, di), _full((1, di)), _full((1, di)), _full(w_s.shape), _full((GMLP_GROUPS, tl, 1))],
        [_rows(tl, w3), _full((1, di)), _full((1, di)), _full(w_s.shape), _full((GMLP_GROUPS, tl, 1))],
        [jax.ShapeDtypeStruct((l, w3), BF16), jax.ShapeDtypeStruct((1, di), F32), jax.ShapeDtypeStruct((1, di), F32),
         jax.ShapeDtypeStruct(w_s.shape, F32), jax.ShapeDtypeStruct((GMLP_GROUPS, tl, 1), F32)],
        scratch=[pltpu.VMEM((tl, di), F32)] * 3)(
        a, dm, ln_g.reshape(1, di), ln_b.reshape(1, di), w_s, b_s.reshape(GMLP_GROUPS, tl, 1))
    return outs


def gmlp_layer_fwd(h, p, wf, tag):
    hn = rms_fwd(h, p["norm_g"], tag + "_rms")
    a = matmul(hn, wf["w_in"], "nn", tag + "_mm_in")
    m = gmlp_gate_fwd(a, p["ln_g"], p["ln_b"], p["w_s"], p["b_s"], tag + "_gate")
    h_out = matmul(m, wf["w_out"], "nn", tag + "_mm_out", add=h)
    return h_out, (h, hn, a, m)


def gmlp_layer_bwd(dh_out, saved, p, wf, tag, sink):
    h, hn, a, m = saved
    dm = matmul(dh_out, wf["w_out"], "nt", tag + "_mm_dm")
    sink.mm("w_out", m, dh_out, tag + "_mm_gwout")
    da, dlg, dlb, dws, dbs = gmlp_gate_bwd(a, dm, p["ln_g"], p["ln_b"], p["w_s"], p["b_s"], tag + "_gate_bwd")
    dhn = matmul(da, wf["w_in"], "nt", tag + "_mm_dhn")
    sink.mm("w_in", hn, da, tag + "_mm_gwin")
    zero = sink.send()
    dh, dng = rms_bwd(h, p["norm_g"] + zero, dhn, dh_out, tag + "_rms_bwd")
    grads = {"norm_g": dng.reshape(-1), "ln_g": dlg.reshape(-1), "ln_b": dlb.reshape(-1),
             "w_s": dws, "b_s": dbs.reshape(GMLP_GROUPS, GMLP_CHUNK)}
    return dh, grads


def _cmul(ar, ai, br, bi):
    return ar * br - ai * bi, ar * bi + ai * br


S5_PG = 16


def _gblock(tail):
    return pl.BlockSpec((S5_PG,) + tuple(tail), lambda i: (i, 0, 0))


def s5_params_fwd(a_re, a_im, log_step, b_re, b_im):
    g, p, hh = b_re.shape

    def body(ar_ref, ai_ref, ls_ref, br_ref, bi_ref, lr_ref, li_ref, bbr_ref, bbi_ref):
        ar, ai = ar_ref[...], ai_ref[...]
        step = jnp.exp(ls_ref[...])
        mag = jnp.exp(ar * step)
        lr, li = mag * jnp.cos(ai * step), mag * jnp.sin(ai * step)
        den = 1.0 / (ar * ar + ai * ai)
        fr, fi = _cmul(lr - 1.0, li, ar * den, -ai * den)
        lr_ref[...] = lr
        li_ref[...] = li
        bbr, bbi = _cmul(fr, fi, br_ref[...], bi_ref[...])
        bbr_ref[...] = bbr
        bbi_ref[...] = bbi

    s1 = jax.ShapeDtypeStruct((g, p, 1), F32)
    s3 = jax.ShapeDtypeStruct((g, p, hh), F32)
    b1, b0, b3 = _gblock((p, 1)), _gblock((1, 1)), _gblock((p, hh))
    return pl.pallas_call(body, name="s5_params_fwd", grid=(g // S5_PG,), in_specs=[b1, b1, b0, b3, b3],
                          out_specs=[b1, b1, b3, b3], out_shape=[s1, s1, s3, s3],
                          compiler_params=_cparams(("parallel",)))(
        a_re.reshape(g, p, 1), a_im.reshape(g, p, 1), log_step.reshape(g, 1, 1), b_re, b_im)


def s5_params_bwd(a_re, a_im, log_step, b_re, b_im, dl_re, dl_im, dbb_re, dbb_im):
    g, p, hh = b_re.shape

    def body(ar_ref, ai_ref, ls_ref, br_ref, bi_ref, dlr_ref, dli_ref, dbr_ref, dbi_ref,
             gar_ref, gai_ref, gls_ref, gbr_ref, gbi_ref):
        ar, ai = ar_ref[...], ai_ref[...]
        step = jnp.exp(ls_ref[...])
        mag = jnp.exp(ar * step)
        lr, li = mag * jnp.cos(ai * step), mag * jnp.sin(ai * step)
        den = 1.0 / (ar * ar + ai * ai)
        ir, ii = ar * den, -ai * den
        fr, fi = _cmul(lr - 1.0, li, ir, ii)
        br, bi = br_ref[...], bi_ref[...]
        dbr, dbi = dbr_ref[...], dbi_ref[...]
        gbr, gbi = _cmul(fr, -fi, dbr, dbi)
        gbr_ref[...] = gbr
        gbi_ref[...] = gbi
        pr, pi = _cmul(br, -bi, dbr, dbi)
        gfr = jnp.sum(pr, axis=-1, keepdims=True)
        gfi = jnp.sum(pi, axis=-1, keepdims=True)
        t_r, t_i = _cmul(ir, -ii, gfr, gfi)
        glr, gli = dlr_ref[...] + t_r, dli_ref[...] + t_i
        c1r, c1i = _cmul(step * lr, -step * li, glr, gli)
        qr, qi = _cmul(fr, fi, ir, ii)
        c2r, c2i = _cmul(-qr, qi, gfr, gfi)
        gar_ref[...] = c1r + c2r
        gai_ref[...] = c1i + c2i
        wr, wi = _cmul(ar, ai, lr, li)
        sr, _ = _cmul(wr, -wi, glr, gli)
        gls_ref[...] = jnp.sum(sr, axis=1, keepdims=True) * step

    s1 = jax.ShapeDtypeStruct((g, p, 1), F32)
    s3 = jax.ShapeDtypeStruct((g, p, hh), F32)
    b1, b0, b3 = _gblock((p, 1)), _gblock((1, 1)), _gblock((p, hh))
    return pl.pallas_call(body, name="s5_params_bwd", grid=(g // S5_PG,),
                          in_specs=[b1, b1, b0, b3, b3, b1, b1, b3, b3], out_specs=[b1, b1, b0, b3, b3],
                          out_shape=[s1, s1, jax.ShapeDtypeStruct((g, 1, 1), F32), s3, s3],
                          compiler_params=_cparams(("parallel",)))(
        a_re.reshape(g, p, 1), a_im.reshape(g, p, 1), log_step.reshape(g, 1, 1), b_re, b_im,
        dl_re, dl_im, dbb_re, dbb_im)


def _blockdiag(t):
    sb, n, r, c = t.shape
    eye = jnp.eye(n, dtype=bool)[None, :, None, :, None]
    full = jnp.where(eye, t[:, :, :, None, :], jnp.zeros((), t.dtype))
    return full.reshape(sb, n * r, n * c)


def _blockdiag_extract(m, r, c):
    sb = m.shape[0]
    n = m.shape[1] // r
    m5 = m.reshape(sb, n, r, n, c)
    return jnp.stack([m5[:, i, :, i, :] for i in range(n)], axis=1)


S5_TB = 256
S5_UNROLL = 8


def _lam_power(pr, pi, n):
    for _ in range(int(math.log2(n))):
        pr, pi = _cmul(pr, pi, pr, pi)
    return pr, pi


def _segment_entries(er, ei, pr, pi, reverse):
    seg, ns = er.shape
    row = lax.broadcasted_iota(jnp.int32, (seg, ns), 0)
    cr = jnp.zeros((seg, ns), F32)
    ci = jnp.zeros((seg, ns), F32)
    cur_r = jnp.zeros((1, ns), F32)
    cur_i = jnp.zeros((1, ns), F32)
    for s in (range(seg - 2, -1, -1) if reverse else range(1, seg)):
        src = s + 1 if reverse else s - 1
        mr, mi = _cmul(pr, pi, cur_r, cur_i)
        cur_r = jnp.sum(jnp.where(row == src, er, 0.0), axis=0, keepdims=True) + mr
        cur_i = jnp.sum(jnp.where(row == src, ei, 0.0), axis=0, keepdims=True) + mi
        cr = jnp.where(row == s, cur_r, cr)
        ci = jnp.where(row == s, cur_i, ci)
    return cr, ci


def s5_scan_fused_fwd(a_p, lam_re, lam_im, wb_re, wb_im, wc_re, wc_im, d_skip, name):
    l = a_p.shape[0]
    di = d_skip.shape[1]
    rows = S5_SEG * S5_TB
    nb = l // rows
    ns = wb_re.shape[2]

    def body(u_ref, lr_ref, li_ref, wbr_ref, wbi_ref, wcr_ref, wci_ref, ds_ref, y_ref, yg_ref, ckr_ref, cki_ref,
             bur, bui):
        lr = jnp.broadcast_to(lr_ref[0], (S5_SEG, ns))
        li = jnp.broadcast_to(li_ref[0], (S5_SEG, ns))

        def scan_block(b, carry, keep):
            def step(t, c):
                xr, xi = c
                sl = pl.ds(pl.multiple_of(b * rows + t * S5_SEG, S5_SEG), S5_SEG)
                nr = lr * xr - li * xi + bur[sl, :]
                ni = lr * xi + li * xr + bui[sl, :]
                if keep:
                    bur[sl, :] = nr
                    bui[sl, :] = ni
                return nr, ni

            return lax.fori_loop(0, S5_TB, step, carry, unroll=S5_UNROLL)

        def project(b, carry):
            rs = pl.ds(pl.multiple_of(b * rows, rows), rows)
            u = u_ref[rs, :]
            bur[rs, :] = _dot(u, wbr_ref[0], DN_NN)
            bui[rs, :] = _dot(u, wbi_ref[0], DN_NN)
            return scan_block(b, carry, False)

        zero = jnp.zeros((S5_SEG, ns), F32)
        er, ei = lax.fori_loop(0, nb, project, (zero, zero))
        pr, pi = _lam_power(lr_ref[0], li_ref[0], l // S5_SEG)
        entry = _segment_entries(er, ei, pr, pi, False)

        def emit(b, carry):
            ckr_ref[0, b] = carry[0]
            cki_ref[0, b] = carry[1]
            carry = scan_block(b, carry, True)
            rs = pl.ds(pl.multiple_of(b * rows, rows), rows)
            y = (_dot(bur[rs, :], wcr_ref[0], DN_NN) - _dot(bui[rs, :], wci_ref[0], DN_NN)
                 + ds_ref[...] * u_ref[rs, :])
            y_ref[rs, :] = y
            yg_ref[rs, :] = _gelu(y).astype(BF16)
            return carry

        lax.fori_loop(0, nb, emit, entry)

    sb3 = lambda s: (s, 0, 0)
    st = jax.ShapeDtypeStruct
    return pl.pallas_call(
        body, name=name, grid=(S5_SB,),
        in_specs=[pl.BlockSpec((l, LANES), lambda s: (0, s)),
                  pl.BlockSpec((1, 1, ns), sb3), pl.BlockSpec((1, 1, ns), sb3),
                  pl.BlockSpec((1, LANES, ns), sb3), pl.BlockSpec((1, LANES, ns), sb3),
                  pl.BlockSpec((1, ns, LANES), sb3), pl.BlockSpec((1, ns, LANES), sb3),
                  pl.BlockSpec((1, LANES), lambda s: (0, s))],
        out_specs=[pl.BlockSpec((l, LANES), lambda s: (0, s)), pl.BlockSpec((l, LANES), lambda s: (0, s)),
                   pl.BlockSpec((1, nb, S5_SEG, ns), lambda s: (s, 0, 0, 0)),
                   pl.BlockSpec((1, nb, S5_SEG, ns), lambda s: (s, 0, 0, 0))],
        out_shape=[st((l, di), F32), st((l, di), BF16),
                   st((S5_SB, nb, S5_SEG, ns), F32), st((S5_SB, nb, S5_SEG, ns), F32)],
        scratch_shapes=[pltpu.VMEM((l, ns), F32), pltpu.VMEM((l, ns), F32)],
        compiler_params=_cparams(("parallel",)))(a_p, lam_re, lam_im, wb_re, wb_im, wc_re, wc_im, d_skip)


def s5_scan_fused_bwd(a_p, dy, lam_re, lam_im, wb_re, wb_im, wc_re, wc_im, d_skip, ck_re, ck_im, name):
    l = a_p.shape[0]
    di = d_skip.shape[1]
    rows = S5_SEG * S5_TB
    nb = l // rows
    ns = wb_re.shape[2]

    def body(u_ref, dy_ref, lr_ref, li_ref, wbr_ref, wbi_ref, wcr_ref, wci_ref, ds_ref, ckr_ref, cki_ref,
             du_ref, dwbr_ref, dwbi_ref, dwcr_ref, dwci_ref, dds_ref, dlr_ref, dli_ref, gr, gi, xr_b, xi_b):
        lr = jnp.broadcast_to(lr_ref[0], (S5_SEG, ns))
        li = jnp.broadcast_to(li_ref[0], (S5_SEG, ns))

        def back_project(k, carry):
            b = nb - 1 - k
            rs = pl.ds(pl.multiple_of(b * rows, rows), rows)
            dyv = dy_ref[rs, :]
            gr[rs, :] = _dot(dyv, wcr_ref[0], DN_NT)
            gi[rs, :] = -_dot(dyv, wci_ref[0], DN_NT)

            def step(kk, c):
                ar, ai = c
                sl = pl.ds(pl.multiple_of(b * rows + (S5_TB - 1 - kk) * S5_SEG, S5_SEG), S5_SEG)
                return gr[sl, :] + lr * ar + li * ai, gi[sl, :] + lr * ai - li * ar

            return lax.fori_loop(0, S5_TB, step, carry, unroll=S5_UNROLL)

        zero = jnp.zeros((S5_SEG, ns), F32)
        er, ei = lax.fori_loop(0, nb, back_project, (zero, zero))
        pr, pi = _lam_power(lr_ref[0], -li_ref[0], l // S5_SEG)
        a0r, a0i = _segment_entries(er, ei, pr, pi, True)

        dwbr_ref[...] = jnp.zeros_like(dwbr_ref)
        dwbi_ref[...] = jnp.zeros_like(dwbi_ref)
        dwcr_ref[...] = jnp.zeros_like(dwcr_ref)
        dwci_ref[...] = jnp.zeros_like(dwci_ref)
        dds_ref[...] = jnp.zeros_like(dds_ref)

        def block(k, carry):
            b = nb - 1 - k
            rs = pl.ds(pl.multiple_of(b * rows, rows), rows)
            u = u_ref[rs, :]
            dyv = dy_ref[rs, :]
            body_rows = pl.ds(S5_SEG, rows)
            x0r, x0i = ckr_ref[0, b], cki_ref[0, b]
            xr_b[0:S5_SEG, :] = x0r
            xi_b[0:S5_SEG, :] = x0i
            xr_b[body_rows, :] = _dot(u, wbr_ref[0], DN_NN)
            xi_b[body_rows, :] = _dot(u, wbi_ref[0], DN_NN)

            def fstep(t, c):
                xr, xi = c
                sl = pl.ds(pl.multiple_of((t + 1) * S5_SEG, S5_SEG), S5_SEG)
                nr = lr * xr - li * xi + xr_b[sl, :]
                ni = lr * xi + li * xr + xi_b[sl, :]
                xr_b[sl, :] = nr
                xi_b[sl, :] = ni
                return nr, ni

            lax.fori_loop(0, S5_TB, fstep, (x0r, x0i), unroll=S5_UNROLL)
            dwcr_ref[0] += _dot(xr_b[body_rows, :], dyv, DN_TN)
            dwci_ref[0] -= _dot(xi_b[body_rows, :], dyv, DN_TN)

            def bstep(kk, c):
                ar, ai = c
                sl = pl.ds(pl.multiple_of(b * rows + (S5_TB - 1 - kk) * S5_SEG, S5_SEG), S5_SEG)
                nr = gr[sl, :] + lr * ar + li * ai
                ni = gi[sl, :] + lr * ai - li * ar
                gr[sl, :] = nr
                gi[sl, :] = ni
                return nr, ni

            ar, ai = lax.fori_loop(0, S5_TB, bstep, carry[:2], unroll=S5_UNROLL)
            a_r, a_i = gr[rs, :], gi[rs, :]
            p_r, p_i = xr_b[0:rows, :], xi_b[0:rows, :]
            per_seg = lambda v: jnp.sum(v.reshape(S5_TB, S5_SEG, ns), axis=0)
            carry = (ar, ai, carry[2] + per_seg(a_r * p_r + a_i * p_i), carry[3] + per_seg(a_i * p_r - a_r * p_i))
            du_ref[rs, :] = (_dot(a_r, wbr_ref[0], DN_NT) + _dot(a_i, wbi_ref[0], DN_NT) + ds_ref[...] * dyv).astype(BF16)
            dwbr_ref[0] += _dot(u, a_r, DN_TN)
            dwbi_ref[0] += _dot(u, a_i, DN_TN)
            dds_ref[...] += jnp.sum(dyv * u, axis=0, keepdims=True)
            return carry

        _, _, dlr, dli = lax.fori_loop(0, nb, block, (a0r, a0i, zero, zero))
        dlr_ref[0] = dlr
        dli_ref[0] = dli

    sb3 = lambda s: (s, 0, 0)
    seq = pl.BlockSpec((l, LANES), lambda s: (0, s))
    ck = pl.BlockSpec((1, nb, S5_SEG, ns), lambda s: (s, 0, 0, 0))
    st = jax.ShapeDtypeStruct
    return pl.pallas_call(
        body, name=name, grid=(S5_SB,),
        in_specs=[seq, seq, pl.BlockSpec((1, 1, ns), sb3), pl.BlockSpec((1, 1, ns), sb3),
                  pl.BlockSpec((1, LANES, ns), sb3), pl.BlockSpec((1, LANES, ns), sb3),
                  pl.BlockSpec((1, ns, LANES), sb3), pl.BlockSpec((1, ns, LANES), sb3),
                  pl.BlockSpec((1, LANES), lambda s: (0, s)), ck, ck],
        out_specs=[seq, pl.BlockSpec((1, LANES, ns), sb3), pl.BlockSpec((1, LANES, ns), sb3),
                   pl.BlockSpec((1, ns, LANES), sb3), pl.BlockSpec((1, ns, LANES), sb3),
                   pl.BlockSpec((1, LANES), lambda s: (0, s)),
                   pl.BlockSpec((1, S5_SEG, ns), sb3), pl.BlockSpec((1, S5_SEG, ns), sb3)],
        out_shape=[st((l, di), BF16), st((S5_SB, LANES, ns), F32), st((S5_SB, LANES, ns), F32),
                   st((S5_SB, ns, LANES), F32), st((S5_SB, ns, LANES), F32), st((1, di), F32),
                   st((S5_SB, S5_SEG, ns), F32), st((S5_SB, S5_SEG, ns), F32)],
        scratch_shapes=[pltpu.VMEM((l, ns), F32), pltpu.VMEM((l, ns), F32),
                        pltpu.VMEM((rows + S5_SEG, ns), F32), pltpu.VMEM((rows + S5_SEG, ns), F32)],
        compiler_params=_cparams(("parallel",)))(
        a_p, dy, lam_re, lam_im, wb_re, wb_im, wc_re, wc_im, d_skip, ck_re, ck_im)


def s5_gate_fwd(y, t, b_glu, a_p, name):
    l, d = y.shape
    tl = ROW_TILE

    def body(y_ref, t_ref, b_ref, z_ref, m_ref):
        yg = _gelu(y_ref[...])
        m_ref[...] = (yg * _sigmoid(t_ref[...] + b_ref[...]) * _silu(z_ref[...])).astype(BF16)

    return _rowcall(body, name, l // tl, [_rows(tl, d), _rows(tl, d), _full((1, d)), _rows(tl, d, 1)], _rows(tl, d),
                    jax.ShapeDtypeStruct((l, d), BF16))(y, t, b_glu.reshape(1, d), a_p)


def s5_gate_bwd(dm, y, t, b_glu, a_p, name):
    l, d = y.shape
    tl = ROW_TILE_HEAVY

    def body(dm_ref, y_ref, t_ref, b_ref, z_ref, dt_ref, dyg_ref, dz_ref, db_ref):
        i = pl.program_id(0)
        dmv = dm_ref[...]
        z = z_ref[...]
        yg = _gelu(y_ref[...])
        sg = _sigmoid(t_ref[...] + b_ref[...])
        y2 = yg * sg
        sz, gz = _silu_both(z)
        dy2 = dmv * sz
        dz_ref[...] = (dmv * y2 * gz).astype(BF16)
        dyg_ref[...] = dy2 * sg
        dt = dy2 * yg * sg * (1.0 - sg)
        dt_ref[...] = dt.astype(BF16)
        _acc(db_ref, jnp.sum(dt, axis=0, keepdims=True), i)

    st = jax.ShapeDtypeStruct
    return _rowcall(body, name, l // tl, [_rows(tl, d), _rows(tl, d), _rows(tl, d), _full((1, d)), _rows(tl, d, 1)],
                    [_rows(tl, d), _rows(tl, d), _rows(tl, d), _full((1, d))],
                    [st((l, d), BF16), st((l, d), F32), st((l, d), BF16), st((1, d), F32)])(
        dm, y, t, b_glu.reshape(1, d), a_p)


def s5_act_bwd(y, dyg_a, dyg_b, name):
    l, d = y.shape
    tl = ROW_TILE

    def body(y_ref, a_ref, b_ref, o_ref):
        o_ref[...] = (a_ref[...] + b_ref[...]) * _gelu_grad(y_ref[...])

    return _rowcall(body, name, l // tl, [_rows(tl, d)] * 3, _rows(tl, d), jax.ShapeDtypeStruct((l, d), F32))(y, dyg_a, dyg_b)


def _seg_perm(t):
    l, d = t.shape
    return t.reshape(S5_SEG, l // S5_SEG, d).transpose(1, 0, 2).reshape(l, d)


def _seg_unperm(t):
    l, d = t.shape
    return t.reshape(l // S5_SEG, S5_SEG, d).transpose(1, 0, 2).reshape(l, d)


def _s5_weights(p):
    lr, li, bbr, bbi = s5_params_fwd(p["a_re"], p["a_im"], p["log_step"], p["b_re"], p["b_im"])
    ns = 8 * S5_STATE
    lam_re = lr.reshape(S5_SB, 1, ns)
    lam_im = li.reshape(S5_SB, 1, ns)
    to_bd = lambda t: _blockdiag(t.reshape(S5_SB, 8, t.shape[1], t.shape[2]))
    wb_re = to_bd(bbr.transpose(0, 2, 1)).astype(BF16)
    wb_im = to_bd(bbi.transpose(0, 2, 1)).astype(BF16)
    wc_re = to_bd(p["c_re"].transpose(0, 2, 1)).astype(BF16)
    wc_im = to_bd(p["c_im"].transpose(0, 2, 1)).astype(BF16)
    return lam_re, lam_im, wb_re, wb_im, wc_re, wc_im


def s5_layer_fwd(h, p, wf, sw, tag):
    l = h.shape[0]
    di = p["d_skip"].shape[0]
    hn = rms_fwd(h, p["norm_g"], tag + "_rms")
    hn_p = _seg_perm(hn)
    a_p = matmul(hn_p, wf["w_in"], "nn", tag + "_mm_in")
    dsk = p["d_skip"].reshape(1, di)
    y, yg, ck_re, ck_im = s5_scan_fused_fwd(a_p, *sw, dsk, tag + "_scan")
    t = matmul(yg, wf["w_glu"], "nn", tag + "_mm_glu")
    m = s5_gate_fwd(y, t, p["b_glu"], a_p, tag + "_gate")
    out_p = matmul(m, wf["w_out"], "nn", tag + "_mm_out")
    h_out = residual_add(h, _seg_unperm(out_p), tag + "_res")
    return h_out, (h, hn_p, a_p, sw, ck_re, ck_im, y, yg, t, m)


def residual_add(h, y, name):
    l, d = h.shape
    tl = ROW_TILE_NARROW

    def body(h_ref, y_ref, o_ref):
        o_ref[...] = h_ref[...] + y_ref[...]

    return _rowcall(body, name, l // tl, [_rows(tl, d)] * 2, _rows(tl, d), jax.ShapeDtypeStruct((l, d), F32))(h, y)


def s5_layer_bwd(dh_out, saved, p, wf, tag, sink):
    h, hn_p, a_p, sw, ck_re, ck_im, y, yg, t, m = saved
    l = h.shape[0]
    di = p["d_skip"].shape[0]
    dsk = p["d_skip"].reshape(1, di)
    dout_p = _seg_perm(dh_out)
    dm = matmul(dout_p, wf["w_out"], "nt", tag + "_mm_dm")
    sink.mm("w_out", m, dout_p, tag + "_mm_gwout")
    dt, dyg_a, dz, db_glu = s5_gate_bwd(dm, y, t, p["b_glu"], a_p, tag + "_gate_bwd")
    dyg_b = matmul(dt, wf["w_glu"], "nt", tag + "_mm_dyg")
    sink.mm("w_glu", yg, dt, tag + "_mm_gwglu")
    dy = s5_act_bwd(y, dyg_a, dyg_b, tag + "_act_bwd")
    du, dwbr, dwbi, dwcr, dwci, dds, dlr, dli = s5_scan_fused_bwd(a_p, dy, *sw, dsk, ck_re, ck_im, tag + "_scanb")
    da = jnp.concatenate([du, dz], axis=1)
    dhn_p = matmul(da, wf["w_in"], "nt", tag + "_mm_dhn")
    sink.mm("w_in", hn_p, da, tag + "_mm_gwin")
    zero = sink.send()
    dh, dng = rms_bwd(h, p["norm_g"] + zero, _seg_unperm(dhn_p), dh_out, tag + "_rms_bwd")
    ex = lambda m_, r, c: _blockdiag_extract(m_, r, c).reshape(S5_GROUPS, r, c).transpose(0, 2, 1)
    dbb_re, dbb_im = ex(dwbr, S5_GROUP, S5_STATE), ex(dwbi, S5_GROUP, S5_STATE)
    g_c_re, g_c_im = ex(dwcr, S5_STATE, S5_GROUP), ex(dwci, S5_STATE, S5_GROUP)
    dl_re = lane_sum8(dlr).reshape(S5_GROUPS, S5_STATE, 1)
    dl_im = lane_sum8(dli).reshape(S5_GROUPS, S5_STATE, 1)
    gar, gai, gls, gbr, gbi = s5_params_bwd(p["a_re"], p["a_im"], p["log_step"], p["b_re"], p["b_im"],
                                            dl_re, dl_im, dbb_re, dbb_im)
    grads = {"norm_g": dng.reshape(-1), "a_re": gar.reshape(S5_GROUPS, S5_STATE),
             "a_im": gai.reshape(S5_GROUPS, S5_STATE), "log_step": gls.reshape(-1), "b_re": gbr, "b_im": gbi,
             "c_re": g_c_re, "c_im": g_c_im, "d_skip": dds.reshape(-1), "b_glu": db_glu.reshape(-1)}
    return dh, grads


def lane_sum8(t):
    sb, seg, ns = t.shape

    def body(t_ref, o_ref):
        o_ref[...] = jnp.sum(t_ref[...], axis=1, keepdims=True)

    return pl.pallas_call(body, name="s5_seg_sum", out_shape=jax.ShapeDtypeStruct((sb, 1, ns), F32))(t)


MLA_DI = MLA_HEADS * 128
MLA_CQ0 = MLA_DI
MLA_CKV0 = MLA_CQ0 + MLA_Q_RANK
MLA_KR0 = MLA_CKV0 + MLA_KV_RANK
MLA_AW = MLA_KR0 + LANES


def _rot_half(x):
    w = x.shape[-1]
    lane = lax.broadcasted_iota(jnp.int32, x.shape, x.ndim - 1)
    return jnp.where(lane % MLA_ROPE < MLA_ROPE // 2, pltpu.roll(x, w - MLA_ROPE // 2, x.ndim - 1),
                     pltpu.roll(x, MLA_ROPE // 2, x.ndim - 1))


def rope_tables(pos, zero):
    l = pos.shape[0]
    tl = ROW_TILE
    j = np.arange(LANES) % MLA_ROPE % (MLA_ROPE // 2)
    inv_freq = (ROPE_THETA ** (-(2.0 * j) / MLA_ROPE)).astype(np.float32).reshape(1, LANES)
    sign = np.where(np.arange(LANES) % MLA_ROPE < MLA_ROPE // 2, -1.0, 1.0).astype(np.float32).reshape(1, LANES)

    def body(p_ref, f_ref, s_ref, cos_ref, sin_ref):
        ang = p_ref[...].astype(F32) * f_ref[...]
        cos_ref[...] = jnp.cos(ang)
        sin_ref[...] = jnp.sin(ang) * s_ref[...]

    st = jax.ShapeDtypeStruct((l, LANES), F32)
    return _rowcall(body, "rope_tables", l // tl, [_rows(tl, 1), _full((1, LANES)), _full((1, LANES))],
                    [_rows(tl, LANES)] * 2, [st, st])(pos, jnp.asarray(inv_freq), jnp.asarray(sign) + zero)


def _rope(x, cos, sins):
    return x * cos + _rot_half(x) * sins


def _rope_t(dy, cos, sins):
    return dy * cos - sins * _rot_half(dy)


def _rmsn(x):
    r = lax.rsqrt(jnp.mean(x * x, axis=-1, keepdims=True) + NORM_EPS)
    return x * r, r


def mla_pre(a, q_g, kv_g, cos, sins, name):
    l = a.shape[0]
    tl = ROW_TILE

    def body(a_ref, qg_ref, kg_ref, cos_ref, sin_ref, cq_ref, ckv_ref, krs_ref):
        xq, _ = _rmsn(a_ref[:, MLA_CQ0:MLA_CKV0])
        cq_ref[...] = (xq * qg_ref[...]).astype(BF16)
        xk, _ = _rmsn(a_ref[:, MLA_CKV0:MLA_KR0])
        ckv_ref[...] = (xk * kg_ref[...]).astype(BF16)
        kr = a_ref[:, MLA_KR0:MLA_AW]
        kr2 = kr + pltpu.roll(kr, MLA_ROPE, 1)
        kr2 = _rope(kr2, cos_ref[...], sin_ref[...])
        lane = lax.broadcasted_iota(jnp.int32, kr2.shape, 1)
        krs_ref[0] = jnp.where(lane < MLA_ROPE, kr2, 0.0).astype(BF16)
        krs_ref[1] = jnp.where(lane >= MLA_ROPE, kr2, 0.0).astype(BF16)

    st = jax.ShapeDtypeStruct
    return _rowcall(body, name, l // tl,
                    [_rows(tl, MLA_AW), _full((1, MLA_Q_RANK)), _full((1, MLA_KV_RANK)), _rows(tl, LANES), _rows(tl, LANES)],
                    [_rows(tl, MLA_Q_RANK), _rows(tl, MLA_KV_RANK), pl.BlockSpec((2, tl, LANES), lambda i: (0, i, 0))],
                    [st((l, MLA_Q_RANK), BF16), st((l, MLA_KV_RANK), BF16), st((2, l, LANES), BF16)])(
        a, q_g.reshape(1, -1), kv_g.reshape(1, -1), cos, sins)


def mla_rope_q(qr, cos, sins, name):
    l, w = qr.shape
    tl = ROW_TILE

    def body(q_ref, cos_ref, sin_ref, o_ref):
        c, s = cos_ref[...], sin_ref[...]
        for p in range(w // LANES):
            sl = slice(p * LANES, (p + 1) * LANES)
            o_ref[:, sl] = _rope(q_ref[:, sl], c, s).astype(BF16)

    return _rowcall(body, name, l // tl, [_rows(tl, w), _rows(tl, LANES), _rows(tl, LANES)], _rows(tl, w),
                    jax.ShapeDtypeStruct((l, w), BF16))(qr, cos, sins)


ATT_OUT = 512
ATT_IN = 512
ATT_R = ATT_OUT // ATT_IN


def _scores(qn, qr, kn, kr, mask_off, transposed):
    q2 = jnp.concatenate([qn, qr], axis=1)
    k2 = jnp.concatenate([kn, kr], axis=1)
    s = (_dot(k2, q2, DN_NT) if transposed else _dot(q2, k2, DN_NT)) * MLA_SCALE
    if mask_off is None:
        return s
    r = lax.broadcasted_iota(jnp.int32, s.shape, 0)
    c = lax.broadcasted_iota(jnp.int32, s.shape, 1)
    return jnp.where((r <= c + mask_off) if transposed else (c + mask_off <= r), s, NEG_INF)


def _fold(x, op):
    out = x[:, :LANES]
    for t in range(1, x.shape[1] // LANES):
        out = op(out, x[:, t * LANES:(t + 1) * LANES])
    return out


def flash_fwd(qn, qr, kv, krs, name):
    l = qn.shape[0]
    nq = l // ATT_OUT

    def body(qn_ref, qr_ref, kv_ref, kr_ref, o_ref, lse_ref, s_buf):
        qi = pl.program_id(1)
        q_r = qr_ref[...]
        q_n = [qn_ref[:, hh * LANES:(hh + 1) * LANES] for hh in range(2)]

        def block_scores(j, mx, mask_off):
            sl = pl.ds(pl.multiple_of(j * ATT_IN, ATT_IN), ATT_IN)
            out = []
            for hh in range(2):
                s = _scores(q_n[hh], q_r, kv_ref[sl, 2 * hh * LANES:(2 * hh + 1) * LANES], kr_ref[hh, sl, :],
                            mask_off, False)
                s_buf[hh, j] = s
                out.append(jnp.maximum(mx[hh], _fold(s, jnp.maximum)))
            return tuple(out)

        ninf = jnp.full((ATT_OUT, LANES), NEG_INF, F32)
        mx = lax.fori_loop(0, ATT_R * qi, lambda j, c: block_scores(j, c, None), (ninf, ninf))
        for d in range(ATT_R):
            mx = block_scores(ATT_R * qi + d, mx, d * ATT_IN)
        m = [jnp.max(mx[hh], axis=-1, keepdims=True) for hh in range(2)]

        def block_pv(j, carry):
            sl = pl.ds(pl.multiple_of(j * ATT_IN, ATT_IN), ATT_IN)
            out = []
            for hh in range(2):
                ls, acc = carry[hh]
                p = jnp.exp(s_buf[hh, j] - m[hh])
                out.append((ls + _fold(p, jnp.add),
                            acc + _dot(p, kv_ref[sl, (2 * hh + 1) * LANES:(2 * hh + 2) * LANES], DN_NN)))
            return tuple(out)

        z = jnp.zeros((ATT_OUT, LANES), F32)
        res = lax.fori_loop(0, ATT_R * (qi + 1), block_pv, ((z, z), (z, z)))
        for hh in range(2):
            lsum = jnp.sum(res[hh][0], axis=-1, keepdims=True)
            o_ref[:, hh * LANES:(hh + 1) * LANES] = res[hh][1] / lsum
            lse_ref[hh] = m[hh] + jnp.log(lsum)

    st = jax.ShapeDtypeStruct
    return pl.pallas_call(
        body, name=name, grid=(MLA_HEADS // 2, nq),
        in_specs=[pl.BlockSpec((ATT_OUT, 2 * LANES), lambda p, i: (i, p)),
                  pl.BlockSpec((ATT_OUT, LANES), lambda p, i: (i, p)),
                  pl.BlockSpec((l, 4 * LANES), lambda p, i: (0, p)),
                  pl.BlockSpec((2, l, LANES), lambda p, i: (0, 0, 0))],
        out_specs=[pl.BlockSpec((ATT_OUT, 2 * LANES), lambda p, i: (i, p)),
                   pl.BlockSpec((2, ATT_OUT, 1), lambda p, i: (p, i, 0))],
        out_shape=[st((l, MLA_DI), F32), st((MLA_HEADS, l, 1), F32)],
        scratch_shapes=[pltpu.VMEM((2, l // ATT_IN, ATT_OUT, ATT_IN), F32)],
        compiler_params=_cparams(("parallel", "arbitrary")))(qn, qr, kv, krs)


def flash_dkv(qn, qr, kv, krs, do, lse_row, delta_row, name):
    l = qn.shape[0]
    nk = l // ATT_OUT
    nq = l // ATT_IN

    def body(qn_ref, qr_ref, do_ref, lse_ref, dl_ref, kv_ref, kr_ref, dkv_ref, dkr_ref):
        kj = pl.program_id(1)
        lane = lax.broadcasted_iota(jnp.int32, (ATT_OUT, LANES), 1)
        kn = [kv_ref[:, 2 * hh * LANES:(2 * hh + 1) * LANES] for hh in range(2)]
        v = [kv_ref[:, (2 * hh + 1) * LANES:(2 * hh + 2) * LANES] for hh in range(2)]

        def block(i, carry, mask_off):
            sl = pl.ds(pl.multiple_of(i * ATT_IN, ATT_IN), ATT_IN)
            q_r = qr_ref[sl, :]
            out = []
            for hh in range(2):
                dk2, dv = carry[hh]
                hs = slice(hh * LANES, (hh + 1) * LANES)
                q_n, d_o = qn_ref[sl, hs], do_ref[sl, hs]
                s = _scores(q_n, q_r, kn[hh], kr_ref[hh], mask_off, True)
                pt = jnp.exp(s - lse_ref[hh, i])
                dv = dv + _dot(pt, d_o, DN_NN)
                dpt = _dot(v[hh], d_o, DN_NT)
                dst = (pt * (dpt - dl_ref[hh, i]) * MLA_SCALE).astype(BF16)
                out.append((dk2 + _dot(dst, jnp.concatenate([q_n, q_r], axis=1), DN_NN), dv))
            return tuple(out)

        z = jnp.zeros((ATT_OUT, LANES), F32)
        z2 = jnp.zeros((ATT_OUT, 2 * LANES), F32)
        res = ((z2, z), (z2, z))
        for d in range(ATT_R):
            res = block(ATT_R * kj + d, res, d * ATT_IN)
        res = lax.fori_loop(ATT_R * (kj + 1), nq, lambda i, c: block(i, c, None), res)
        for hh in range(2):
            dkv_ref[:, 2 * hh * LANES:(2 * hh + 1) * LANES] = res[hh][0][:, :LANES].astype(BF16)
            dkv_ref[:, (2 * hh + 1) * LANES:(2 * hh + 2) * LANES] = res[hh][1].astype(BF16)
        dkr_ref[0] = jnp.where(lane < MLA_ROPE, res[0][0][:, LANES:], res[1][0][:, LANES:])

    st = jax.ShapeDtypeStruct
    return pl.pallas_call(
        body, name=name, grid=(MLA_HEADS // 2, nk),
        in_specs=[pl.BlockSpec((l, 2 * LANES), lambda p, j: (0, p)),
                  pl.BlockSpec((l, LANES), lambda p, j: (0, p)),
                  pl.BlockSpec((l, 2 * LANES), lambda p, j: (0, p)),
                  pl.BlockSpec((2, nq, 1, ATT_IN), lambda p, j: (p, 0, 0, 0)),
                  pl.BlockSpec((2, nq, 1, ATT_IN), lambda p, j: (p, 0, 0, 0)),
                  pl.BlockSpec((ATT_OUT, 4 * LANES), lambda p, j: (j, p)),
                  pl.BlockSpec((2, ATT_OUT, LANES), lambda p, j: (0, j, 0))],
        out_specs=[pl.BlockSpec((ATT_OUT, 4 * LANES), lambda p, j: (j, p)),
                   pl.BlockSpec((1, ATT_OUT, LANES), lambda p, j: (p, j, 0))],
        out_shape=[st((l, 2 * MLA_DI), BF16), st((MLA_HEADS // 2, l, LANES), F32)],
        compiler_params=_cparams(("parallel", "arbitrary")))(qn, qr, do, lse_row, delta_row, kv, krs)


def flash_dq(qn, qr, kv, krs, do, lse, delta, cos, sins, name):
    l = qn.shape[0]
    nq = l // ATT_OUT

    def body(qn_ref, qr_ref, do_ref, lse_ref, dl_ref, kv_ref, kr_ref, cos_ref, sin_ref, dqn_ref, dqr_ref):
        qi = pl.program_id(1)
        q_r = qr_ref[...]
        q_n = [qn_ref[:, hh * LANES:(hh + 1) * LANES] for hh in range(2)]
        d_o = [do_ref[:, hh * LANES:(hh + 1) * LANES] for hh in range(2)]
        lse_h = [lse_ref[hh] for hh in range(2)]
        dl_h = [dl_ref[hh] for hh in range(2)]

        def block(j, carry, mask_off):
            sl = pl.ds(pl.multiple_of(j * ATT_IN, ATT_IN), ATT_IN)
            dq2 = list(carry)
            for hh in range(2):
                kn = kv_ref[sl, 2 * hh * LANES:(2 * hh + 1) * LANES]
                v = kv_ref[sl, (2 * hh + 1) * LANES:(2 * hh + 2) * LANES]
                kr = kr_ref[hh, sl, :]
                s = _scores(q_n[hh], q_r, kn, kr, mask_off, False)
                pr = jnp.exp(s - lse_h[hh])
                dp = _dot(d_o[hh], v, DN_NT)
                ds = (pr * (dp - dl_h[hh]) * MLA_SCALE).astype(BF16)
                dq2[hh] = dq2[hh] + _dot(ds, jnp.concatenate([kn, kr], axis=1), DN_NN)
            return tuple(dq2)

        z2 = jnp.zeros((ATT_OUT, 2 * LANES), F32)
        res = lax.fori_loop(0, ATT_R * qi, lambda j, c: block(j, c, None), (z2, z2))
        for d in range(ATT_R):
            res = block(ATT_R * qi + d, res, d * ATT_IN)
        dqn_ref[:, 0:LANES] = res[0][:, :LANES].astype(BF16)
        dqn_ref[:, LANES:2 * LANES] = res[1][:, :LANES].astype(BF16)
        dqr = res[0][:, LANES:] + res[1][:, LANES:]
        dqr_ref[...] = _rope_t(dqr, cos_ref[...], sin_ref[...]).astype(BF16)

    st = jax.ShapeDtypeStruct
    return pl.pallas_call(
        body, name=name, grid=(MLA_HEADS // 2, nq),
        in_specs=[pl.BlockSpec((ATT_OUT, 2 * LANES), lambda p, i: (i, p)),
                  pl.BlockSpec((ATT_OUT, LANES), lambda p, i: (i, p)),
                  pl.BlockSpec((ATT_OUT, 2 * LANES), lambda p, i: (i, p)),
                  pl.BlockSpec((2, ATT_OUT, 1), lambda p, i: (p, i, 0)),
                  pl.BlockSpec((2, ATT_OUT, 1), lambda p, i: (p, i, 0)),
                  pl.BlockSpec((l, 4 * LANES), lambda p, i: (0, p)),
                  pl.BlockSpec((2, l, LANES), lambda p, i: (0, 0, 0)),
                  pl.BlockSpec((ATT_OUT, LANES), lambda p, i: (i, 0)),
                  pl.BlockSpec((ATT_OUT, LANES), lambda p, i: (i, 0))],
        out_specs=[pl.BlockSpec((ATT_OUT, 2 * LANES), lambda p, i: (i, p)),
                   pl.BlockSpec((ATT_OUT, LANES), lambda p, i: (i, p))],
        out_shape=[st((l, MLA_DI), BF16), st((l, MLA_HEADS * MLA_ROPE), BF16)],
        compiler_params=_cparams(("parallel", "arbitrary")))(qn, qr, do, lse, delta, kv, krs, cos, sins)


def mla_gate_fwd(o, a, name):
    l = o.shape[0]
    tl = ROW_TILE

    def body(o_ref, z_ref, m_ref):
        m_ref[...] = (o_ref[...] * _silu(z_ref[...])).astype(BF16)

    return _rowcall(body, name, l // tl, [_rows(tl, MLA_DI), _rows(tl, MLA_DI)], _rows(tl, MLA_DI),
                    jax.ShapeDtypeStruct((l, MLA_DI), BF16))(o, a)


def mla_gate_bwd(dm, o, a, name):
    l = o.shape[0]
    tl = ROW_TILE

    def body(dm_ref, o_ref, z_ref, do_ref, dz_ref, dl_ref):
        dmv, ov, z = dm_ref[...], o_ref[...], z_ref[...]
        sz, gz = _silu_both(z)
        d_o = dmv * sz
        do_ref[...] = d_o.astype(BF16)
        dz_ref[...] = (dmv * ov * gz).astype(BF16)
        pr = d_o * ov
        for h in range(MLA_HEADS):
            dl_ref[h] = jnp.sum(pr[:, h * LANES:(h + 1) * LANES], axis=1, keepdims=True)

    st = jax.ShapeDtypeStruct
    return _rowcall(body, name, l // tl, [_rows(tl, MLA_DI)] * 3,
                    [_rows(tl, MLA_DI), _rows(tl, MLA_DI), pl.BlockSpec((MLA_HEADS, tl, 1), lambda i: (0, i, 0))],
                    [st((l, MLA_DI), BF16), st((l, MLA_DI), BF16), st((MLA_HEADS, l, 1), F32)])(dm, o, a)


def mla_post(a, dcqn, dckvn, dkr_pairs, dz, q_g, kv_g, cos, sins, name):
    l = a.shape[0]
    tl = ROW_TILE
    npair = MLA_HEADS // 2

    def norm_bwd(x, g, dy):
        xhat, r = _rmsn(x)
        dxh = dy * g
        return r * (dxh - xhat * jnp.mean(dxh * xhat, axis=-1, keepdims=True)), jnp.sum(dy * xhat, axis=0, keepdims=True)

    def body(a_ref, dq_ref, dk_ref, dkr_ref, dz_ref, qg_ref, kg_ref, cos_ref, sin_ref, da_ref, dqg_ref, dkg_ref):
        i = pl.program_id(0)
        da_ref[:, 0:MLA_DI] = dz_ref[...]
        dcq, dqg = norm_bwd(a_ref[:, MLA_CQ0:MLA_CKV0], qg_ref[...], dq_ref[...])
        da_ref[:, MLA_CQ0:MLA_CKV0] = dcq.astype(BF16)
        dckv, dkg = norm_bwd(a_ref[:, MLA_CKV0:MLA_KR0], kg_ref[...], dk_ref[...])
        da_ref[:, MLA_CKV0:MLA_KR0] = dckv.astype(BF16)
        dk2 = dkr_ref[0]
        for p in range(1, npair):
            dk2 = dk2 + dkr_ref[p]
        dk2 = _rope_t(dk2, cos_ref[...], sin_ref[...])
        dk2 = dk2 + pltpu.roll(dk2, MLA_ROPE, 1)
        lane = lax.broadcasted_iota(jnp.int32, dk2.shape, 1)
        da_ref[:, MLA_KR0:MLA_AW] = jnp.where(lane < MLA_ROPE, dk2, 0.0).astype(BF16)
        _acc(dqg_ref, dqg, i)
        _acc(dkg_ref, dkg, i)

    st = jax.ShapeDtypeStruct
    return _rowcall(body, name, l // tl,
                    [_rows(tl, MLA_AW), _rows(tl, MLA_Q_RANK), _rows(tl, MLA_KV_RANK),
                     pl.BlockSpec((npair, tl, LANES), lambda i: (0, i, 0)), _rows(tl, MLA_DI),
                     _full((1, MLA_Q_RANK)), _full((1, MLA_KV_RANK)), _rows(tl, LANES), _rows(tl, LANES)],
                    [_rows(tl, MLA_AW), _full((1, MLA_Q_RANK)), _full((1, MLA_KV_RANK))],
                    [st((l, MLA_AW), BF16), st((1, MLA_Q_RANK), F32), st((1, MLA_KV_RANK), F32)])(
        a, dcqn, dckvn, dkr_pairs, dz, q_g.reshape(1, -1), kv_g.reshape(1, -1), cos, sins)


def _mla_w_in_perm(w):
    r = MLA_Q_RANK + MLA_KV_RANK + MLA_ROPE
    pad = jnp.zeros(w.shape[:-1] + (MLA_AW - MLA_KR0 - MLA_ROPE,), w.dtype)
    return jnp.concatenate([w[..., r:], w[..., :r], pad], axis=-1)


def _mla_w_in_unperm(g):
    r = MLA_Q_RANK + MLA_KV_RANK + MLA_ROPE
    return jnp.concatenate([g[..., MLA_DI:MLA_DI + r], g[..., :MLA_DI]], axis=-1)


def _mla_w_uq_split(w):
    k = w.shape[0]
    w3 = w.reshape(k, MLA_HEADS, MLA_NOPE + MLA_ROPE)
    return w3[:, :, :MLA_NOPE].reshape(k, MLA_HEADS * MLA_NOPE), w3[:, :, MLA_NOPE:].reshape(k, MLA_HEADS * MLA_ROPE)


def _mla_w_uq_merge(gn, gr):
    k = gn.shape[0]
    return jnp.concatenate([gn.reshape(k, MLA_HEADS, MLA_NOPE), gr.reshape(k, MLA_HEADS, MLA_ROPE)], axis=2).reshape(k, -1)


def mla_layer_fwd(h, p, wf, cos, sins, tag):
    hn = rms_fwd(h, p["norm_g"], tag + "_rms")
    w_in = _mla_w_in_perm(wf["w_in"])
    w_uq_n, w_uq_r = _mla_w_uq_split(wf["w_uq"])
    a = matmul(hn, w_in, "nn", tag + "_mm_in")
    cqn, ckvn, krs = mla_pre(a, p["q_norm_g"], p["kv_norm_g"], cos, sins, tag + "_pre")
    qn = matmul(cqn, w_uq_n, "nn", tag + "_mm_qn", out_dtype=BF16)
    qr_raw = matmul(cqn, w_uq_r, "nn", tag + "_mm_qr")
    qr = mla_rope_q(qr_raw, cos, sins, tag + "_rope_q")
    kv = matmul(ckvn, wf["w_ukv"], "nn", tag + "_mm_kv", out_dtype=BF16)
    o, lse = flash_fwd(qn, qr, kv, krs, tag + "_flash")
    m = mla_gate_fwd(o, a, tag + "_gate")
    h_out = matmul(m, wf["w_out"], "nn", tag + "_mm_out", add=h)
    return h_out, (h, hn, a, cqn, ckvn, krs, qn, qr, kv, o, lse, m, w_in, w_uq_n, w_uq_r)


def mla_layer_bwd(dh_out, saved, p, wf, cos, sins, tag, sink):
    h, hn, a, cqn, ckvn, krs, qn, qr, kv, o, lse, m, w_in, w_uq_n, w_uq_r = saved
    l = h.shape[0]
    dm = matmul(dh_out, wf["w_out"], "nt", tag + "_mm_dm")
    sink.mm("w_out", m, dh_out, tag + "_mm_gwout")
    do, dz, delta = mla_gate_bwd(dm, o, a, tag + "_gate_bwd")
    lse_row = lse.reshape(MLA_HEADS, l // ATT_IN, 1, ATT_IN)
    delta_row = delta.reshape(MLA_HEADS, l // ATT_IN, 1, ATT_IN)
    dkv, dkr_pairs = flash_dkv(qn, qr, kv, krs, do, lse_row, delta_row, tag + "_flash_dkv")
    dqn, dqr = flash_dq(qn, qr, kv, krs, do, lse, delta, cos, sins, tag + "_flash_dq")
    dcqn = matmul(dqn, w_uq_n, "nt", tag + "_mm_dcq_n")
    dcqn = matmul(dqr, w_uq_r, "nt", tag + "_mm_dcq_r", add=dcqn)
    g_uq_n = matmul(cqn, dqn, "tn", tag + "_mm_guq_n")
    g_uq_r = matmul(cqn, dqr, "tn", tag + "_mm_guq_r")
    dckvn = matmul(dkv, wf["w_ukv"], "nt", tag + "_mm_dckv")
    sink.mm("w_ukv", ckvn, dkv, tag + "_mm_gukv")
    da, dqg, dkg = mla_post(a, dcqn, dckvn, dkr_pairs, dz, p["q_norm_g"], p["kv_norm_g"], cos, sins, tag + "_post")
    dhn = matmul(da, w_in, "nt", tag + "_mm_dhn")
    g_w_in = matmul(hn, da, "tn", tag + "_mm_gwin")
    sink.put("w_uq", _mla_w_uq_merge(g_uq_n, g_uq_r))
    sink.put("w_in", _mla_w_in_unperm(g_w_in))
    zero = sink.send()
    dh, dng = rms_bwd(h, p["norm_g"] + zero, dhn, dh_out, tag + "_rms_bwd")
    grads = {"norm_g": dng.reshape(-1), "q_norm_g": dqg.reshape(-1), "kv_norm_g": dkg.reshape(-1)}
    return dh, grads


ANY = pl.BlockSpec(memory_space=pl.ANY)


def _me():
    return lax.axis_index("x"), lax.axis_index("y"), lax.axis_index("c")


def _chip():
    return 2 * lax.axis_index("x") + lax.axis_index("y")


def _other_chips(x, y):
    return [(1 - x, y), (x, 1 - y), (1 - x, 1 - y)]


def _rcopy(src, dst, ssem, rsem, dev):
    return pltpu.make_async_remote_copy(src_ref=src, dst_ref=dst, send_sem=ssem, recv_sem=rsem,
                                        device_id=dev, device_id_type=MESH)


def _half(ref, c, hf):
    return ref.at[pl.ds(c * hf, hf), :]


HBM = pl.BlockSpec(memory_space=pltpu.HBM)
SEM = pl.BlockSpec(memory_space=pltpu.SEMAPHORE)
SPLIT_EFFECT = pltpu.SideEffectType.DATAFLOW_SIDE_EFFECTING


def gather_start(wb, after, name):
    nr, w = wb.shape
    hf = nr // 2

    def body(w_ref, land_ref, after_ref, ssem, rsem, w_thru, land_thru, token):
        x, y, c = _me()
        k = 2 * x + y
        for j, (cx, cy) in enumerate(_other_chips(x, y)):
            _rcopy(_half(w_ref, c, hf), _half(land_ref.at[k], c, hf), ssem.at[j], rsem.at[j], (cx, cy, c)).start()
        token[...] = jnp.zeros_like(token)

    land = lax.empty((N_CHIPS, nr, w), wb.dtype)
    return pl.pallas_call(
        body, name=name,
        out_shape=(pltpu.SemaphoreType.DMA((3,)), pltpu.SemaphoreType.DMA((3,)), pltpu.HBM(wb.shape, wb.dtype),
                   pltpu.HBM(land.shape, land.dtype), jax.ShapeDtypeStruct((8, LANES), F32)),
        in_specs=(HBM, HBM, ANY), out_specs=(SEM, SEM, HBM, HBM, pl.BlockSpec(memory_space=pltpu.VMEM)),
        input_output_aliases={0: 2, 1: 3},
        compiler_params=pltpu.CompilerParams(has_side_effects=SPLIT_EFFECT))(
        pltpu.with_memory_space_constraint(wb, pltpu.HBM), pltpu.with_memory_space_constraint(land, pltpu.HBM), after)


def gather_wait(ssem, rsem, w_thru, land_thru, after, name):
    nr, w = w_thru.shape
    hf = nr // 2

    def body(w_ref, land_ref, ssem_ref, rsem_ref, after_ref, w_dead, got_ref):
        x, y, c = _me()
        for j, (cx, cy) in enumerate(_other_chips(x, y)):
            cp = _rcopy(_half(w_ref, c, hf), _half(land_ref.at[2 * cx + cy], c, hf), ssem_ref.at[j], rsem_ref.at[j],
                        (cx, cy, c))
            cp.wait_send()
            cp.wait_recv()

    return pl.pallas_call(
        body, name=name, out_shape=(pltpu.HBM(w_thru.shape, w_thru.dtype), pltpu.HBM(land_thru.shape, land_thru.dtype)),
        in_specs=(HBM, HBM, SEM, SEM, ANY), out_specs=(HBM, HBM), input_output_aliases={0: 0, 1: 1},
        compiler_params=pltpu.CompilerParams(has_side_effects=SPLIT_EFFECT))(w_thru, land_thru, ssem, rsem, after)[1]


def gather_handover(land, wb, name):
    _, nr, w = land.shape
    hf = nr // 2

    def body(l_ref, o_ref, ssem, rsem):
        x, y, c = _me()
        chips = _other_chips(x, y)
        sends = []
        for j, (cx, cy) in enumerate(chips):
            region = _half(o_ref.at[2 * cx + cy], c, hf)
            sends.append(_rcopy(region, region, ssem.at[j], rsem.at[j], (x, y, 1 - c)))
            sends[-1].start()
        for j, (cx, cy) in enumerate(chips):
            region = _half(o_ref.at[2 * cx + cy], 1 - c, hf)
            _rcopy(region, region, ssem.at[j], rsem.at[j], (x, y, 1 - c)).wait_recv()
        for cp in sends:
            cp.wait_send()

    out = pl.pallas_call(
        body, name=name, in_specs=[ANY], out_specs=ANY, input_output_aliases={0: 0},
        out_shape=jax.ShapeDtypeStruct(land.shape, land.dtype),
        scratch_shapes=[pltpu.SemaphoreType.DMA((3,)), pltpu.SemaphoreType.DMA((3,))])(land)
    return lax.dynamic_update_slice(out, wb[None], (_chip(), 0, 0))


def reduce_start(t, after, name):
    def body(t_ref, land_ref, after_ref, ssem, rsem, t_thru, land_thru, token):
        x, y, c = _me()
        k = 2 * x + y
        for j, (cx, cy) in enumerate(_other_chips(x, y)):
            _rcopy(t_ref.at[2 * cx + cy], land_ref.at[k], ssem.at[j], rsem.at[j], (cx, cy, c)).start()
        token[...] = jnp.zeros_like(token)

    land = lax.empty(t.shape, t.dtype)
    return pl.pallas_call(
        body, name=name,
        out_shape=(pltpu.SemaphoreType.DMA((3,)), pltpu.SemaphoreType.DMA((3,)), pltpu.HBM(t.shape, t.dtype),
                   pltpu.HBM(t.shape, t.dtype), jax.ShapeDtypeStruct((8, LANES), F32)),
        in_specs=(HBM, HBM, ANY), out_specs=(SEM, SEM, HBM, HBM, pl.BlockSpec(memory_space=pltpu.VMEM)),
        input_output_aliases={0: 2, 1: 3},
        compiler_params=pltpu.CompilerParams(has_side_effects=SPLIT_EFFECT))(
        pltpu.with_memory_space_constraint(t, pltpu.HBM), pltpu.with_memory_space_constraint(land, pltpu.HBM), after)


def bcast_start(g, after, name):
    def body(g_ref, land_ref, after_ref, ssem, rsem, g_thru, land_thru, token):
        x, y, c = _me()
        k = 2 * x + y
        for j, (cx, cy) in enumerate(_other_chips(x, y)):
            _rcopy(g_ref, land_ref.at[k], ssem.at[j], rsem.at[j], (cx, cy, c)).start()
        token[...] = jnp.zeros_like(token)

    land = lax.empty((N_CHIPS,) + g.shape, g.dtype)
    return pl.pallas_call(
        body, name=name,
        out_shape=(pltpu.SemaphoreType.DMA((3,)), pltpu.SemaphoreType.DMA((3,)), pltpu.HBM(g.shape, g.dtype),
                   pltpu.HBM(land.shape, land.dtype), jax.ShapeDtypeStruct((8, LANES), F32)),
        in_specs=(HBM, HBM, ANY), out_specs=(SEM, SEM, HBM, HBM, pl.BlockSpec(memory_space=pltpu.VMEM)),
        input_output_aliases={0: 2, 1: 3},
        compiler_params=pltpu.CompilerParams(has_side_effects=SPLIT_EFFECT))(
        pltpu.with_memory_space_constraint(g, pltpu.HBM), pltpu.with_memory_space_constraint(land, pltpu.HBM), after)


def bcast_wait(ssem, rsem, g_thru, land_thru, after, name):
    def body(g_ref, land_ref, ssem_ref, rsem_ref, after_ref, g_out, got_ref):
        x, y, c = _me()
        for j, (cx, cy) in enumerate(_other_chips(x, y)):
            cp = _rcopy(g_ref, land_ref.at[2 * cx + cy], ssem_ref.at[j], rsem_ref.at[j], (cx, cy, c))
            cp.wait_send()
            cp.wait_recv()

    g, land = pl.pallas_call(
        body, name=name, out_shape=(pltpu.HBM(g_thru.shape, g_thru.dtype), pltpu.HBM(land_thru.shape, land_thru.dtype)),
        in_specs=(HBM, HBM, SEM, SEM, ANY), out_specs=(HBM, HBM), input_output_aliases={0: 0, 1: 1},
        compiler_params=pltpu.CompilerParams(has_side_effects=SPLIT_EFFECT))(g_thru, land_thru, ssem, rsem, after)
    return lax.dynamic_update_slice(land, g[None], (_chip(), 0, 0))


def reduce_wait(ssem, rsem, t_thru, land_thru, after, name):
    def body(t_ref, land_ref, ssem_ref, rsem_ref, after_ref, t_out, got_ref):
        x, y, c = _me()
        k = 2 * x + y
        for j, (cx, cy) in enumerate(_other_chips(x, y)):
            cp = _rcopy(t_ref.at[k], land_ref.at[2 * cx + cy], ssem_ref.at[j], rsem_ref.at[j], (cx, cy, c))
            cp.wait_send()
            cp.wait_recv()

    return pl.pallas_call(
        body, name=name, out_shape=(pltpu.HBM(t_thru.shape, t_thru.dtype), pltpu.HBM(land_thru.shape, land_thru.dtype)),
        in_specs=(HBM, HBM, SEM, SEM, ANY), out_specs=(HBM, HBM), input_output_aliases={0: 0, 1: 1},
        compiler_params=pltpu.CompilerParams(has_side_effects=SPLIT_EFFECT))(t_thru, land_thru, ssem, rsem, after)


def grads_to_sibling(ps, name="grads_to_sibling"):
    n = len(ps)

    def body(*refs):
        p_refs, o_refs, ssem, rsem = refs[:n], refs[n:2 * n], refs[2 * n], refs[2 * n + 1]
        x, y, c = _me()
        cps = []
        for a in range(n):
            hf = ps[a].shape[1] // 2
            cps.append(_rcopy(p_refs[a].at[:, pl.ds((1 - c) * hf, hf), :], o_refs[a], ssem.at[a], rsem.at[a],
                              (x, y, 1 - c)))
        for cp in cps:
            cp.start()
        for cp in cps:
            cp.wait()

    return pl.pallas_call(
        body, name=name, in_specs=[ANY] * n, out_specs=[ANY] * n,
        out_shape=[jax.ShapeDtypeStruct((N_CHIPS, p.shape[1] // 2, p.shape[2]), p.dtype) for p in ps],
        scratch_shapes=[pltpu.SemaphoreType.DMA((n,)), pltpu.SemaphoreType.DMA((n,))])(*ps)


def pair_sum(p, ra, out_dtype, name):
    _, nr, w = p.shape
    hf = nr // 2
    tr = _pick_rows(hf, cap=max(512, SUM_BLOCK_BYTES // (4 * w)))
    nb = hf // tr

    def body(c_ref, p_ref, r_ref, o_ref):
        o_ref[...] = (p_ref[...] + r_ref[...]).astype(out_dtype)

    c = lax.axis_index("c").astype(jnp.int32).reshape(1)
    return pl.pallas_call(
        body, name=name,
        grid_spec=pltpu.PrefetchScalarGridSpec(
            num_scalar_prefetch=1, grid=(N_CHIPS, nb),
            in_specs=[pl.BlockSpec((1, tr, w), lambda k, i, c_ref: (k, c_ref[0] * nb + i, 0)),
                      pl.BlockSpec((1, tr, w), lambda k, i, c_ref: (k, i, 0))],
            out_specs=pl.BlockSpec((1, tr, w), lambda k, i, c_ref: (k, i, 0))),
        out_shape=jax.ShapeDtypeStruct((N_CHIPS, hf, w), out_dtype),
        compiler_params=_cparams(("parallel", "parallel")))(c, p, ra)


def grads_across_chips(ts):
    n = len(ts)

    def body(*refs):
        t_refs, o_refs, ssem, rsem = refs[:n], refs[n:2 * n], refs[2 * n], refs[2 * n + 1]
        x, y, c = _me()
        k = 2 * x + y
        chips = _other_chips(x, y)
        sends = [_rcopy(t_refs[a].at[2 * cx + cy], o_refs[a].at[k], ssem.at[3 * a + j], rsem.at[3 * a + j], (cx, cy, c))
                 for a in range(n) for j, (cx, cy) in enumerate(chips)]
        for cp in sends:
            cp.start()
        for a in range(n):
            for j, (cx, cy) in enumerate(chips):
                _rcopy(t_refs[a].at[k], o_refs[a].at[2 * cx + cy], ssem.at[3 * a + j], rsem.at[3 * a + j],
                       (cx, cy, c)).wait_recv()
        for cp in sends:
            cp.wait_send()

    return pl.pallas_call(
        body, name="grads_across_chips", in_specs=[ANY] * n, out_specs=[ANY] * n,
        out_shape=[jax.ShapeDtypeStruct(t.shape, t.dtype) for t in ts],
        scratch_shapes=[pltpu.SemaphoreType.DMA((3 * n,)), pltpu.SemaphoreType.DMA((3 * n,))])(*ts)


def chip_sum(t, rb, name):
    _, hf, w = rb.shape
    tr = _pick_rows(hf, cap=max(512, SUM_BLOCK_BYTES // (4 * w)))
    nb = hf // tr

    def body(kc_ref, t_ref, r_ref, o_ref):
        k = kc_ref[0]
        acc = jnp.where(k == 0, t_ref[0], r_ref[0]).astype(F32)
        for j in range(1, N_CHIPS):
            acc = acc + jnp.where(k == j, t_ref[0], r_ref[j]).astype(F32)
        o_ref[...] = acc

    kc = jnp.stack([_chip(), lax.axis_index("c")]).astype(jnp.int32)
    return pl.pallas_call(
        body, name=name,
        grid_spec=pltpu.PrefetchScalarGridSpec(
            num_scalar_prefetch=1, grid=(nb,),
            in_specs=[pl.BlockSpec((1, tr, w), lambda i, kc_ref: (kc_ref[0], i, 0)),
                      pl.BlockSpec((N_CHIPS, tr, w), lambda i, kc_ref: (0, i, 0))],
            out_specs=pl.BlockSpec((tr, w), lambda i, kc_ref: (kc_ref[1] * nb + i, 0))),
        out_shape=jax.ShapeDtypeStruct((2 * hf, w), F32), compiler_params=_cparams(("parallel",)))(kc, t, rb)


def reduced_to_sibling(gs):
    n = len(gs)

    def body(*refs):
        o_refs, ssem, rsem = refs[n:2 * n], refs[2 * n], refs[2 * n + 1]
        x, y, c = _me()
        cps = []
        for a in range(n):
            hf = gs[a].shape[0] // 2
            cps.append(_rcopy(_half(o_refs[a], c, hf), _half(o_refs[a], c, hf), ssem.at[a], rsem.at[a], (x, y, 1 - c)))
        for cp in cps:
            cp.start()
        for a in range(n):
            hf = gs[a].shape[0] // 2
            _rcopy(_half(o_refs[a], c, hf), _half(o_refs[a], 1 - c, hf), ssem.at[a], rsem.at[a],
                   (x, y, 1 - c)).wait_recv()
        for cp in cps:
            cp.wait_send()

    return pl.pallas_call(
        body, name="reduced_to_sibling", in_specs=[ANY] * n, out_specs=[ANY] * n,
        input_output_aliases={a: a for a in range(n)},
        out_shape=[jax.ShapeDtypeStruct(g.shape, g.dtype) for g in gs],
        scratch_shapes=[pltpu.SemaphoreType.DMA((n,)), pltpu.SemaphoreType.DMA((n,))])(*gs)


def _adamw_step(w_ref, g_ref, m_ref, v_ref, d_ref, nm_ref, nv_ref):
    bc1 = 1.0 - ADAM_B1 ** ADAM_STEP
    bc2 = 1.0 - ADAM_B2 ** ADAM_STEP
    gv = g_ref[...]
    nm = ADAM_B1 * m_ref[...] + (1.0 - ADAM_B1) * gv
    nv = ADAM_B2 * v_ref[...] + (1.0 - ADAM_B2) * (gv * gv)
    nm_ref[...] = nm
    nv_ref[...] = nv
    d_ref[...] = -ADAM_LR * ((nm / bc1) / (jnp.sqrt(nv / bc2) + ADAM_EPS) + ADAM_WD * w_ref[...])


def adamw_packed(w, g_buf, r0, m, v, name):
    r, c = w.shape
    tr = _tile_rows(r, r0, (1024, 512, 384, 256, 128))

    def body(w_ref, g_ref, m_ref, v_ref, go_ref, d_ref, nm_ref, nv_ref):
        go_ref[...] = g_ref[...]
        _adamw_step(w_ref, g_ref, m_ref, v_ref, d_ref, nm_ref, nv_ref)

    own = pl.BlockSpec((tr, CHUNK_W), lambda i, j: (i, j))
    packed = pl.BlockSpec((tr, CHUNK_W), lambda i, j: ((r0 + j * r) // tr + i, 0))
    st = jax.ShapeDtypeStruct((r, c), F32)
    return pl.pallas_call(body, name=name, grid=(r // tr, c // CHUNK_W), in_specs=[own, packed, own, own],
                          out_specs=[own] * 4, out_shape=[st] * 4,
                          compiler_params=_cparams(("parallel", "parallel")))(w, g_buf, m, v)


def adamw(w, g, m, v, name):
    r, wd = w.shape
    tr = _pick_rows(r, cap=max(16, ADAMW_BLOCK_BYTES // (4 * wd)))
    body = functools.partial(_adamw_step)

    spec = pl.BlockSpec((tr, wd), lambda i: (i, 0))
    st = jax.ShapeDtypeStruct((r, wd), F32)
    return pl.pallas_call(body, name=name, grid=(r // tr,), in_specs=[spec] * 4, out_specs=[spec] * 3,
                          out_shape=[st, st, st], compiler_params=_cparams(("parallel",)))(w, g, m, v)


LAYER_KINDS = ("gmlp", "s5", "mla", "gmlp")
PARAMS = {
    "gmlp": ("norm_g", "w_in", "ln_g", "ln_b", "w_s", "b_s", "w_out"),
    "s5": ("norm_g", "w_in", "a_re", "a_im", "log_step", "b_re", "b_im", "c_re", "c_im", "d_skip", "w_glu", "b_glu", "w_out"),
    "mla": ("norm_g", "w_in", "q_norm_g", "w_uq", "kv_norm_g", "w_ukv", "w_out"),
}
COL_SHARDED = ("w_in", "w_uq", "w_ukv")
ROW_SHARDED = ("w_out", "w_glu")
WEIGHT_NAMES = [("l%d_" % i) + n for i, kind in enumerate(LAYER_KINDS) for n in PARAMS[kind]] + ["final_norm_g"]


def _is_big(name):
    return name.split("_", 1)[1] in COL_SHARDED + ROW_SHARDED


BIG = [n for n in WEIGHT_NAMES if _is_big(n)]
SMALL = [n for n in WEIGHT_NAMES if not _is_big(n)]


def _pack_rows(blocks):
    return jnp.concatenate([b.reshape(-1, PACK_W) for b in blocks], axis=0)


def _shard_major(wn, full, width):
    r, c = full.shape
    if wn in COL_SHARDED:
        t = full.reshape(r, N_CHIPS, c // N_CHIPS).transpose(1, 0, 2)
    else:
        t = full.reshape(N_CHIPS, r // N_CHIPS, c)
    return t.reshape(N_CHIPS, -1, width)


def _from_shard_major(name, t, block_shape):
    r, c = block_shape
    if name.split("_", 1)[1] in COL_SHARDED:
        return t.reshape(N_CHIPS, r, c).transpose(1, 0, 2).reshape(r, N_CHIPS * c)
    return t.reshape(N_CHIPS * r, c)


class BigGradSink:
    ORDER = ("w_out", "w_glu", "w_ukv", "w_uq", "w_in")
    ROW_MAJOR = {2: ("w_uq", "w_in")}

    def __init__(self, layer, block_shapes):
        self.layer = layer
        self.regions = {}
        r0 = 0
        for wn in self.ORDER:
            if wn in block_shapes:
                shape = block_shapes[wn]
                self.regions[wn] = (r0, shape, wn not in self.ROW_MAJOR.get(layer, ()))
                r0 += shape[0] * shape[1] // CHUNK_W
        self.buf = lax.empty((N_CHIPS, r0, CHUNK_W), F32)
        self.flight = None

    def mm(self, wn, a, b, name):
        r0, _, direct = self.regions[wn]
        assert direct
        self.buf = matmul_tn_packed(a, b, self.buf, r0, wn in COL_SHARDED, name)

    def put(self, wn, full):
        r0, _, direct = self.regions[wn]
        assert not direct
        piece = _shard_major(wn, full, CHUNK_W)
        self.buf = lax.dynamic_update_slice(self.buf, piece, (0, r0, 0))

    def send(self):
        i = self.layer
        sib, = grads_to_sibling([self.buf], "grads_to_sibling_l%d" % i)
        t = pair_sum(self.buf, sib, BF16, "pair_sum_l%d" % i)
        self.flight = reduce_start(t, sib, "reduce_l%d_start" % i)
        return self.flight[4][0, 0]


def _small_pack(arrs, total_padded):
    flat = jnp.concatenate([a.reshape(-1) for a in arrs])
    return jnp.pad(flat, (0, total_padded - flat.shape[0]))


def kernel(x, positions, l0_norm_g, l0_w_in, l0_ln_g, l0_ln_b, l0_w_s, l0_b_s, l0_w_out, l1_norm_g, l1_w_in, l1_a_re, l1_a_im, l1_log_step, l1_b_re, l1_b_im, l1_c_re, l1_c_im, l1_d_skip, l1_w_glu, l1_b_glu, l1_w_out, l2_norm_g, l2_w_in, l2_q_norm_g, l2_w_uq, l2_kv_norm_g, l2_w_ukv, l2_w_out, l3_norm_g, l3_w_in, l3_ln_g, l3_ln_b, l3_w_s, l3_b_s, l3_w_out, final_norm_g, loss_target, m_l0_norm_g, m_l0_w_in, m_l0_ln_g, m_l0_ln_b, m_l0_w_s, m_l0_b_s, m_l0_w_out, m_l1_norm_g, m_l1_w_in, m_l1_a_re, m_l1_a_im, m_l1_log_step, m_l1_b_re, m_l1_b_im, m_l1_c_re, m_l1_c_im, m_l1_d_skip, m_l1_w_glu, m_l1_b_glu, m_l1_w_out, m_l2_norm_g, m_l2_w_in, m_l2_q_norm_g, m_l2_w_uq, m_l2_kv_norm_g, m_l2_w_ukv, m_l2_w_out, m_l3_norm_g, m_l3_w_in, m_l3_ln_g, m_l3_ln_b, m_l3_w_s, m_l3_b_s, m_l3_w_out, m_final_norm_g, v_l0_norm_g, v_l0_w_in, v_l0_ln_g, v_l0_ln_b, v_l0_w_s, v_l0_b_s, v_l0_w_out, v_l1_norm_g, v_l1_w_in, v_l1_a_re, v_l1_a_im, v_l1_log_step, v_l1_b_re, v_l1_b_im, v_l1_c_re, v_l1_c_im, v_l1_d_skip, v_l1_w_glu, v_l1_b_glu, v_l1_w_out, v_l2_norm_g, v_l2_w_in, v_l2_q_norm_g, v_l2_w_uq, v_l2_kv_norm_g, v_l2_w_ukv, v_l2_w_out, v_l3_norm_g, v_l3_w_in, v_l3_ln_g, v_l3_ln_b, v_l3_w_s, v_l3_b_s, v_l3_w_out, v_final_norm_g):
    args = locals()
    w = {n: args[n] for n in WEIGHT_NAMES}
    mom_m = {n: args["m_" + n] for n in WEIGHT_NAMES}
    mom_v = {n: args["v_" + n] for n in WEIGHT_NAMES}
    h0 = x[0]
    target = loss_target[0]
    pos = positions.reshape(-1, 1)

    full = {}

    def pack_unit(layers):
        names = [n for n in BIG if int(n[1]) in layers]
        rows = [w[n].size // PACK_W for n in names]
        pad = -sum(rows) % PACK_ROW_ALIGN
        return names, rows, _pack_rows([w[n].astype(BF16) for n in names] + [jnp.zeros((pad, PACK_W), BF16)])

    def unpack_unit(names, rows, gathered):
        r0 = 0
        for n, nr in zip(names, rows):
            full[n] = _from_shard_major(n, gathered[:, r0:r0 + nr, :], w[n].shape)
            r0 += nr

    unit0, unit1, unit2 = pack_unit((0,)), pack_unit((1,)), pack_unit((2, 3))
    wp = dict(w)

    def layer_params(i):
        pre = "l%d_" % i
        p = {k[len(pre):]: v for k, v in wp.items() if k.startswith(pre)}
        wf = {k[len(pre):]: v for k, v in full.items() if k.startswith(pre)}
        return p, wf

    flight = gather_start(unit0[2], unit1[2], "gather_l0_start")
    cos, sins = rope_tables(pos, flight[4][0, 0])
    wp["l1_a_re"] = w["l1_a_re"] + flight[4][0, 0]
    s5_weights = _s5_weights(layer_params(1)[0])
    land = gather_wait(*flight[:4], s5_weights[2], "gather_l0_wait")
    got = gather_handover(land, unit0[2], "gather_l0_handover")
    unpack_unit(unit0[0], unit0[1], got)
    flight = gather_start(unit1[2], got, "gather_l1_start")
    wp["l0_norm_g"] = w["l0_norm_g"] + flight[4][0, 0]

    h = h0
    saved = []
    for i, kind in enumerate(LAYER_KINDS):
        if i == 1:
            land = gather_wait(*flight[:4], h, "gather_l1_wait")
            got = gather_handover(land, unit1[2], "gather_l1_handover")
            unpack_unit(unit1[0], unit1[1], got)
            flight = gather_start(unit2[2], got, "gather_l23_start")
            wp["l1_norm_g"] = w["l1_norm_g"] + flight[4][0, 0]
        if i == 2:
            land = gather_wait(*flight[:4], h, "gather_l23_wait")
            unpack_unit(unit2[0], unit2[1], gather_handover(land, unit2[2], "gather_l23_handover"))
        p, wf = layer_params(i)
        tag = "l%d" % i
        if kind == "gmlp":
            h, s = gmlp_layer_fwd(h, p, wf, tag)
        elif kind == "s5":
            h, s = s5_layer_fwd(h, p, wf, s5_weights, tag)
        else:
            h, s = mla_layer_fwd(h, p, wf, cos, sins, tag)
        saved.append(s)
    loss_part, dh, g_final = loss_head(h, final_norm_g, target)

    grads = {"final_norm_g": g_final.reshape(-1)}
    sinks = {}

    for i in reversed(range(len(LAYER_KINDS))):
        kind = LAYER_KINDS[i]
        p, wf = layer_params(i)
        tag = "l%d" % i
        sink = sinks[i] = BigGradSink(i, {n[3:]: w[n].shape for n in BIG if int(n[1]) == i})
        if kind == "gmlp":
            dh, g = gmlp_layer_bwd(dh, saved[i], p, wf, tag, sink)
        elif kind == "s5":
            dh, g = s5_layer_bwd(dh, saved[i], p, wf, tag, sink)
        else:
            dh, g = mla_layer_bwd(dh, saved[i], p, wf, cos, sins, tag, sink)
        for k, val in g.items():
            grads["l%d_%s" % (i, k)] = val
    grad_x = dh[None]

    n_small = sum(w[n].size for n in SMALL)
    piece = N_CHIPS * 2 * 16 * PACK_W
    n_small_pad = -(-(n_small + 1) // piece) * piece
    nrs = n_small_pad // N_CHIPS // PACK_W
    p_small = _small_pack([grads[n] for n in SMALL] + [loss_part], n_small_pad).reshape(N_CHIPS, nrs, PACK_W)
    sib_small, = grads_to_sibling([p_small], "grads_to_sibling_small")
    t_small = pair_sum(p_small, sib_small, F32, "pair_sum_small")
    rb_small, = grads_across_chips([t_small])
    halves = [chip_sum(t_small, rb_small, "chip_sum_small")]

    after = halves[0]
    for i in reversed(range(len(LAYER_KINDS))):
        t_i, rb_i = reduce_wait(*sinks[i].flight[:4], after, "reduce_l%d_wait" % i)
        halves.append(chip_sum(t_i, rb_i, "chip_sum_l%d" % i))
        after = halves[-1]
    reduced = reduced_to_sibling(halves)
    small_flight = bcast_start(reduced[0], reduced[1], "small_allgather_start")

    g_out, d_out, nm_out, nv_out = {}, {}, {}, {}
    for i, g_i in zip(reversed(range(len(LAYER_KINDS))), reduced[1:]):
        for wn, (r0, shape, direct) in sinks[i].regions.items():
            n = "l%d_%s" % (i, wn)
            if direct:
                g_out[n], d_out[n], nm_out[n], nv_out[n] = adamw_packed(w[n], g_i, r0, mom_m[n], mom_v[n], "adamw_" + n)
            else:
                g_out[n] = g_i[r0:r0 + shape[0] * shape[1] // CHUNK_W].reshape(shape)
                d_out[n], nm_out[n], nv_out[n] = adamw(w[n], g_out[n], mom_m[n], mom_v[n], "adamw_" + n)
    small_all = bcast_wait(*small_flight[:4], nv_out["l0_w_in"], "small_allgather_wait")
    g_small = small_all.reshape(-1, PACK_W)
    sp = lambda d: _small_pack([d[n] for n in SMALL], n_small_pad).reshape(-1, PACK_W)
    d_small, nm_small, nv_small = adamw(sp(w), g_small, sp(mom_m), sp(mom_v), "adamw_small")
    for buf, out in ((g_small, g_out), (d_small, d_out), (nm_small, nm_out), (nv_small, nv_out)):
        flat = buf.reshape(-1)
        o = 0
        for n in SMALL:
            out[n] = flat[o:o + w[n].size].reshape(w[n].shape)
            o += w[n].size
    loss = g_small.reshape(-1)[n_small]
    return (loss, grad_x, *[g_out[n] for n in WEIGHT_NAMES], *[d_out[n] for n in WEIGHT_NAMES],
            *[nm_out[n] for n in WEIGHT_NAMES], *[nv_out[n] for n in WEIGHT_NAMES])
```

```python
import functools
import math

import jax
import jax.numpy as jnp
import numpy as np
from jax import lax
from jax.experimental import pallas as pl
from jax.experimental.pallas import tpu as pltpu

F32 = jnp.float32
BF16 = jnp.bfloat16
MESH = pl.DeviceIdType.MESH
VMEM_LIMIT_BYTES = 56 * 1024 * 1024
LANES = 128
PACK_W = 1024
CHUNK_W = 256
PACK_ROW_ALIGN = 256
ROW_TILE = 512
ROW_TILE_HEAVY = 256
ROW_TILE_NARROW = 512
SUM_BLOCK_BYTES = 1024 * 1024
ADAMW_BLOCK_BYTES = 1024 * 1024
MM_BLOCK_BYTES = 12 * 1024 * 1024

NORM_EPS = 1e-6
N_CHIPS = 4
GMLP_CHUNK = 128
GMLP_GROUPS = 8
S5_GROUPS = 128
S5_GROUP = 16
S5_STATE = 64
S5_SB = 16
S5_SEG = 8
MLA_HEADS = 16
MLA_NOPE = 128
MLA_ROPE = 64
MLA_Q_RANK = 384
MLA_KV_RANK = 128
MLA_SCALE = (MLA_NOPE + MLA_ROPE) ** -0.5
ROPE_THETA = 10000.0
NEG_INF = -1e30
ADAM_LR, ADAM_B1, ADAM_B2, ADAM_EPS, ADAM_WD, ADAM_STEP = 0.001, 0.9, 0.999, 1e-08, 0.01, 10

DN_NN = (((1,), (0,)), ((), ()))
DN_NT = (((1,), (1,)), ((), ()))
DN_TN = (((0,), (0,)), ((), ()))


def _cparams(sem):
    return pltpu.CompilerParams(dimension_semantics=sem, vmem_limit_bytes=VMEM_LIMIT_BYTES)


def _pick(n, cands=(512, 384, 256, 128)):
    for c in cands:
        if n % c == 0:
            return c
    return n


def _pick_rows(r, cap=512, mult=16):
    return max(t for t in range(mult, cap + 1, mult) if r % t == 0)


def _dot(a, b, dn):
    return lax.dot_general(a.astype(BF16), b.astype(BF16), dn, preferred_element_type=F32)


def _sigmoid(x):
    return 0.5 + 0.5 * jnp.tanh(0.5 * x)


def _gelu(x):
    c = math.sqrt(2.0 / math.pi)
    t = jnp.tanh(c * (x + 0.044715 * x * x * x))
    return 0.5 * x * (1.0 + t)


def _gelu_grad(x):
    c = math.sqrt(2.0 / math.pi)
    t = jnp.tanh(c * (x + 0.044715 * x * x * x))
    return 0.5 * (1.0 + t) + 0.5 * x * (1.0 - t * t) * c * (1.0 + 3.0 * 0.044715 * x * x)


def _gelu_both(x):
    c = math.sqrt(2.0 / math.pi)
    t = jnp.tanh(c * (x + 0.044715 * x * x * x))
    return 0.5 * x * (1.0 + t), 0.5 * (1.0 + t) + 0.5 * x * (1.0 - t * t) * c * (1.0 + 3.0 * 0.044715 * x * x)


def _silu_both(z):
    s = _sigmoid(z)
    return z * s, s * (1.0 + z * (1.0 - s))


def _silu(z):
    return z * _sigmoid(z)


def matmul(a, b, mode, name, out_dtype=F32, add=None):
    if mode == "nn":
        (m, k), n = a.shape, b.shape[1]
    elif mode == "nt":
        (m, k), n = a.shape, b.shape[0]
    else:
        (k, m), n = a.shape, b.shape[1]
    tm = _pick(m, [t for t in (2048, 1024, 512, 384, 256, 128) if t * k * a.dtype.itemsize <= MM_BLOCK_BYTES])
    tn = _pick(n, [t for t in (512, 384, 256, 128) if t * k * b.dtype.itemsize <= MM_BLOCK_BYTES])
    dn = {"nn": DN_NN, "nt": DN_NT, "tn": DN_TN}[mode]

    def body(*refs):
        if add is None:
            a_ref, b_ref, o_ref = refs
        else:
            a_ref, b_ref, add_ref, o_ref = refs
        r = _dot(a_ref[...], b_ref[...], dn)
        if add is not None:
            r = r + add_ref[...].astype(F32)
        o_ref[...] = r.astype(out_dtype)

    a_spec = pl.BlockSpec((k, tm), lambda i, j: (0, i)) if mode == "tn" else pl.BlockSpec((tm, k), lambda i, j: (i, 0))
    b_spec = pl.BlockSpec((tn, k), lambda i, j: (j, 0)) if mode == "nt" else pl.BlockSpec((k, tn), lambda i, j: (0, j))
    o_spec = pl.BlockSpec((tm, tn), lambda i, j: (i, j))
    in_specs = [a_spec, b_spec] + ([o_spec] if add is not None else [])
    args = (a, b) + ((add,) if add is not None else ())
    return pl.pallas_call(
        body, name=name, grid=(m // tm, n // tn), in_specs=in_specs, out_specs=o_spec,
        out_shape=jax.ShapeDtypeStruct((m, n), out_dtype),
        compiler_params=_cparams(("parallel", "arbitrary")))(*args)


def _tile_rows(r, r0, cands=(512, 384, 256, 128)):
    return next(t for t in cands if r % t == 0 and r0 % t == 0)


def matmul_tn_packed(a, b, buf, r0, col_sharded, name):
    k, m = a.shape
    n = b.shape[1]
    if col_sharded:
        chunks = n // N_CHIPS // CHUNK_W
        tm = _tile_rows(m, r0, (1024, 512, 384, 256, 128))
        o_map = lambda i, j: (j // chunks, (r0 + (j % chunks) * m) // tm + i, 0)
    else:
        rs = m // N_CHIPS
        tm = _tile_rows(rs, r0)
        per = rs // tm
        o_map = lambda i, j: (i // per, (r0 + j * rs) // tm + i % per, 0)

    def body(a_ref, b_ref, buf_ref, o_ref):
        o_ref[0] = _dot(a_ref[...], b_ref[...], DN_TN)

    return pl.pallas_call(
        body, name=name, grid=(m // tm, n // CHUNK_W),
        in_specs=[pl.BlockSpec((k, tm), lambda i, j: (0, i)), pl.BlockSpec((k, CHUNK_W), lambda i, j: (0, j)),
                  pl.BlockSpec(memory_space=pl.ANY)],
        out_specs=pl.BlockSpec((1, tm, CHUNK_W), o_map), out_shape=jax.ShapeDtypeStruct(buf.shape, buf.dtype),
        input_output_aliases={2: 0}, compiler_params=_cparams(("parallel", "arbitrary")))(a, b, buf)


def _rows(tl, w, col=0):
    return pl.BlockSpec((tl, w), lambda i: (i, col))


def _full(shape):
    nd = len(shape)
    return pl.BlockSpec(tuple(shape), lambda i: (0,) * nd)


def _rowcall(body, name, n_steps, in_specs, out_specs, out_shape, scratch=()):
    return pl.pallas_call(
        body, name=name, grid=(n_steps,), in_specs=in_specs, out_specs=out_specs, out_shape=out_shape,
        scratch_shapes=list(scratch), compiler_params=_cparams(("arbitrary",)))


def _acc(ref, val, i):
    @pl.when(i == 0)
    def _():
        ref[...] = val

    @pl.when(i != 0)
    def _():
        ref[...] += val


def rms_fwd(h, g, name):
    l, d = h.shape
    tl = ROW_TILE_NARROW

    def body(h_ref, g_ref, o_ref):
        x = h_ref[...]
        r = lax.rsqrt(jnp.mean(x * x, axis=-1, keepdims=True) + NORM_EPS)
        o_ref[...] = (x * r * g_ref[...]).astype(BF16)

    return _rowcall(body, name, l // tl, [_rows(tl, d), _full((1, d))], _rows(tl, d),
                    jax.ShapeDtypeStruct((l, d), BF16))(h, g.reshape(1, d))


def rms_bwd(h, g, dhn, dh_in, name):
    l, d = h.shape
    tl = ROW_TILE_NARROW

    def body(h_ref, g_ref, dhn_ref, dhi_ref, dh_ref, dg_ref):
        i = pl.program_id(0)
        x = h_ref[...]
        r = lax.rsqrt(jnp.mean(x * x, axis=-1, keepdims=True) + NORM_EPS)
        xhat = x * r
        dy = dhn_ref[...]
        dxh = dy * g_ref[...]
        dx = r * (dxh - xhat * jnp.mean(dxh * xhat, axis=-1, keepdims=True))
        dh_ref[...] = dhi_ref[...] + dx
        _acc(dg_ref, jnp.sum(dy * xhat, axis=0, keepdims=True), i)

    return _rowcall(body, name, l // tl, [_rows(tl, d), _full((1, d)), _rows(tl, d), _rows(tl, d)],
                    [_rows(tl, d), _full((1, d))],
                    [jax.ShapeDtypeStruct((l, d), F32), jax.ShapeDtypeStruct((1, d), F32)])(h, g.reshape(1, d), dhn, dh_in)


def loss_head(h, g, target):
    l, d = h.shape
    tl = ROW_TILE_NARROW

    def body(h_ref, g_ref, t_ref, loss_ref, dh_ref, dg_ref):
        i = pl.program_id(0)
        x = h_ref[...]
        gg = g_ref[...]
        r = lax.rsqrt(jnp.mean(x * x, axis=-1, keepdims=True) + NORM_EPS)
        xhat = x * r
        err = xhat * gg - t_ref[...]
        part = 0.5 * jnp.sum(jnp.mean(err * err, axis=-1, keepdims=True), axis=0, keepdims=True)
        _acc(loss_ref, part, i)
        dy = err * (1.0 / d)
        dxh = dy * gg
        dh_ref[...] = r * (dxh - xhat * jnp.mean(dxh * xhat, axis=-1, keepdims=True))
        _acc(dg_ref, jnp.sum(dy * xhat, axis=0, keepdims=True), i)

    return _rowcall(body, "loss_head", l // tl, [_rows(tl, d), _full((1, d)), _rows(tl, d)],
                    [_full((1, 1)), _rows(tl, d), _full((1, d))],
                    [jax.ShapeDtypeStruct((1, 1), F32), jax.ShapeDtypeStruct((l, d), F32),
                     jax.ShapeDtypeStruct((1, d), F32)])(h, g.reshape(1, d), target)


def _gmlp_common(a_ref, lng_ref, lnb_ref):
    di = lng_ref.shape[1]
    u_pre = a_ref[:, 0:di]
    v_pre = a_ref[:, di:2 * di]
    z = a_ref[:, 2 * di:3 * di]
    vg = _gelu(v_pre)
    mu = jnp.mean(vg, axis=-1, keepdims=True)
    xc = vg - mu
    rstd = lax.rsqrt(jnp.mean(xc * xc, axis=-1, keepdims=True) + NORM_EPS)
    vhat = xc * rstd
    vn = vhat * lng_ref[...] + lnb_ref[...]
    return u_pre, v_pre, z, vhat, rstd, vn


def _tril(w):
    r = lax.broadcasted_iota(jnp.int32, w.shape, 0)
    c = lax.broadcasted_iota(jnp.int32, w.shape, 1)
    return jnp.where(c <= r, w, 0.0)


def gmlp_gate_fwd(a, ln_g, ln_b, w_s, b_s, name):
    l, w3 = a.shape
    di = w3 // 3
    dg = di // GMLP_GROUPS
    tl = GMLP_CHUNK

    def body(a_ref, lng_ref, lnb_ref, ws_ref, bs_ref, m_ref):
        u_pre, _, z, _, _, vn = _gmlp_common(a_ref, lng_ref, lnb_ref)
        gate = _gelu(u_pre) * _silu(z)
        for g in range(GMLP_GROUPS):
            sl = slice(g * dg, (g + 1) * dg)
            s = _dot(_tril(ws_ref[g]), vn[:, sl], DN_NN) + bs_ref[g]
            m_ref[:, sl] = (gate[:, sl] * s).astype(BF16)

    return _rowcall(body, name, l // tl,
                    [_rows(tl, w3), _full((1, di)), _full((1, di)), _full(w_s.shape), _full((GMLP_GROUPS, tl, 1))],
                    _rows(tl, di), jax.ShapeDtypeStruct((l, di), BF16))(
        a, ln_g.reshape(1, di), ln_b.reshape(1, di), w_s, b_s.reshape(GMLP_GROUPS, tl, 1))


def gmlp_gate_bwd(a, dm, ln_g, ln_b, w_s, b_s, name):
    l, w3 = a.shape
    di = w3 // 3
    dg = di // GMLP_GROUPS
    tl = GMLP_CHUNK

    def body(a_ref, dm_ref, lng_ref, lnb_ref, ws_ref, bs_ref, da_ref, dlg_ref, dlb_ref, dws_ref, dbs_ref,
             dvn_ref, vh_ref, gv_ref):
        i = pl.program_id(0)
        vg, gv = _gelu_both(a_ref[:, di:2 * di])
        gv_ref[...] = gv
        xc = vg - jnp.mean(vg, axis=-1, keepdims=True)
        rstd = lax.rsqrt(jnp.mean(xc * xc, axis=-1, keepdims=True) + NORM_EPS)
        vh_ref[...] = xc * rstd
        for g in range(GMLP_GROUPS):
            sl = slice(g * dg, (g + 1) * dg)
            wt = _tril(ws_ref[g])
            vn_g = vh_ref[:, sl] * lng_ref[:, sl] + lnb_ref[:, sl]
            s = _dot(wt, vn_g, DN_NN) + bs_ref[g]
            dmg = dm_ref[:, sl]
            u, gu = _gelu_both(a_ref[:, sl])
            sz, gz = _silu_both(a_ref[:, 2 * di + g * dg:2 * di + (g + 1) * dg])
            ds = dmg * u * sz
            da_ref[:, sl] = (dmg * s * sz * gu).astype(BF16)
            da_ref[:, 2 * di + g * dg:2 * di + (g + 1) * dg] = (dmg * u * s * gz).astype(BF16)
            dvn_ref[:, sl] = _dot(wt, ds, DN_TN)
            dw = _tril(_dot(ds, vn_g, DN_NT))
            db = jnp.sum(ds, axis=1, keepdims=True)

            @pl.when(i == 0)
            def _():
                dws_ref[g] = dw
                dbs_ref[g] = db

            @pl.when(i != 0)
            def _():
                dws_ref[g] += dw
                dbs_ref[g] += db

        dvn = dvn_ref[...]
        vhat = vh_ref[...]
        dxh = dvn * lng_ref[...]
        dvg = rstd * (dxh - jnp.mean(dxh, axis=-1, keepdims=True) - vhat * jnp.mean(dxh * vhat, axis=-1, keepdims=True))
        da_ref[:, di:2 * di] = (dvg * gv_ref[...]).astype(BF16)
        _acc(dlg_ref, jnp.sum(dvn * vhat, axis=0, keepdims=True), i)
        _acc(dlb_ref, jnp.sum(dvn, axis=0, keepdims=True), i)

    outs = _rowcall(
        body, name, l // tl,
        [_rows(tl, w3), _rows(tl, di), _full((1, di)), _full((1, di)), _full(w_s.shape), _full((GMLP_GROUPS, tl, 1))],
        [_rows(tl, w3), _full((1, di)), _full((1, di)), _full(w_s.shape), _full((GMLP_GROUPS, tl, 1))],
        [jax.ShapeDtypeStruct((l, w3), BF16), jax.ShapeDtypeStruct((1, di), F32), jax.ShapeDtypeStruct((1, di), F32),
         jax.ShapeDtypeStruct(w_s.shape, F32), jax.ShapeDtypeStruct((GMLP_GROUPS, tl, 1), F32)],
        scratch=[pltpu.VMEM((tl, di), F32)] * 3)(
        a, dm, ln_g.reshape(1, di), ln_b.reshape(1, di), w_s, b_s.reshape(GMLP_GROUPS, tl, 1))
    return outs


def gmlp_layer_fwd(h, p, wf, tag):
    hn = rms_fwd(h, p["norm_g"], tag + "_rms")
    a = matmul(hn, wf["w_in"], "nn", tag + "_mm_in")
    m = gmlp_gate_fwd(a, p["ln_g"], p["ln_b"], p["w_s"], p["b_s"], tag + "_gate")
    h_out = matmul(m, wf["w_out"], "nn", tag + "_mm_out", add=h)
    return h_out, (h, hn, a, m)


def gmlp_layer_bwd(dh_out, saved, p, wf, tag, sink):
    h, hn, a, m = saved
    dm = matmul(dh_out, wf["w_out"], "nt", tag + "_mm_dm")
    sink.mm("w_out", m, dh_out, tag + "_mm_gwout")
    da, dlg, dlb, dws, dbs = gmlp_gate_bwd(a, dm, p["ln_g"], p["ln_b"], p["w_s"], p["b_s"], tag + "_gate_bwd")
    dhn = matmul(da, wf["w_in"], "nt", tag + "_mm_dhn")
    sink.mm("w_in", hn, da, tag + "_mm_gwin")
    zero = sink.send()
    dh, dng = rms_bwd(h, p["norm_g"] + zero, dhn, dh_out, tag + "_rms_bwd")
    grads = {"norm_g": dng.reshape(-1), "ln_g": dlg.reshape(-1), "ln_b": dlb.reshape(-1),
             "w_s": dws, "b_s": dbs.reshape(GMLP_GROUPS, GMLP_CHUNK)}
    return dh, grads


def _cmul(ar, ai, br, bi):
    return ar * br - ai * bi, ar * bi + ai * br


S5_PG = 16


def _gblock(tail):
    return pl.BlockSpec((S5_PG,) + tuple(tail), lambda i: (i, 0, 0))


def s5_params_fwd(a_re, a_im, log_step, b_re, b_im):
    g, p, hh = b_re.shape

    def body(ar_ref, ai_ref, ls_ref, br_ref, bi_ref, lr_ref, li_ref, bbr_ref, bbi_ref):
        ar, ai = ar_ref[...], ai_ref[...]
        step = jnp.exp(ls_ref[...])
        mag = jnp.exp(ar * step)
        lr, li = mag * jnp.cos(ai * step), mag * jnp.sin(ai * step)
        den = 1.0 / (ar * ar + ai * ai)
        fr, fi = _cmul(lr - 1.0, li, ar * den, -ai * den)
        lr_ref[...] = lr
        li_ref[...] = li
        bbr, bbi = _cmul(fr, fi, br_ref[...], bi_ref[...])
        bbr_ref[...] = bbr
        bbi_ref[...] = bbi

    s1 = jax.ShapeDtypeStruct((g, p, 1), F32)
    s3 = jax.ShapeDtypeStruct((g, p, hh), F32)
    b1, b0, b3 = _gblock((p, 1)), _gblock((1, 1)), _gblock((p, hh))
    return pl.pallas_call(body, name="s5_params_fwd", grid=(g // S5_PG,), in_specs=[b1, b1, b0, b3, b3],
                          out_specs=[b1, b1, b3, b3], out_shape=[s1, s1, s3, s3],
                          compiler_params=_cparams(("parallel",)))(
        a_re.reshape(g, p, 1), a_im.reshape(g, p, 1), log_step.reshape(g, 1, 1), b_re, b_im)


def s5_params_bwd(a_re, a_im, log_step, b_re, b_im, dl_re, dl_im, dbb_re, dbb_im):
    g, p, hh = b_re.shape

    def body(ar_ref, ai_ref, ls_ref, br_ref, bi_ref, dlr_ref, dli_ref, dbr_ref, dbi_ref,
             gar_ref, gai_ref, gls_ref, gbr_ref, gbi_ref):
        ar, ai = ar_ref[...], ai_ref[...]
        step = jnp.exp(ls_ref[...])
        mag = jnp.exp(ar * step)
        lr, li = mag * jnp.cos(ai * step), mag * jnp.sin(ai * step)
        den = 1.0 / (ar * ar + ai * ai)
        ir, ii = ar * den, -ai * den
        fr, fi = _cmul(lr - 1.0, li, ir, ii)
        br, bi = br_ref[...], bi_ref[...]
        dbr, dbi = dbr_ref[...], dbi_ref[...]
        gbr, gbi = _cmul(fr, -fi, dbr, dbi)
        gbr_ref[...] = gbr
        gbi_ref[...] = gbi
        pr, pi = _cmul(br, -bi, dbr, dbi)
        gfr = jnp.sum(pr, axis=-1, keepdims=True)
        gfi = jnp.sum(pi, axis=-1, keepdims=True)
        t_r, t_i = _cmul(ir, -ii, gfr, gfi)
        glr, gli = dlr_ref[...] + t_r, dli_ref[...] + t_i
        c1r, c1i = _cmul(step * lr, -step * li, glr, gli)
        qr, qi = _cmul(fr, fi, ir, ii)
        c2r, c2i = _cmul(-qr, qi, gfr, gfi)
        gar_ref[...] = c1r + c2r
        gai_ref[...] = c1i + c2i
        wr, wi = _cmul(ar, ai, lr, li)
        sr, _ = _cmul(wr, -wi, glr, gli)
        gls_ref[...] = jnp.sum(sr, axis=1, keepdims=True) * step

    s1 = jax.ShapeDtypeStruct((g, p, 1), F32)
    s3 = jax.ShapeDtypeStruct((g, p, hh), F32)
    b1, b0, b3 = _gblock((p, 1)), _gblock((1, 1)), _gblock((p, hh))
    return pl.pallas_call(body, name="s5_params_bwd", grid=(g // S5_PG,),
                          in_specs=[b1, b1, b0, b3, b3, b1, b1, b3, b3], out_specs=[b1, b1, b0, b3, b3],
                          out_shape=[s1, s1, jax.ShapeDtypeStruct((g, 1, 1), F32), s3, s3],
                          compiler_params=_cparams(("parallel",)))(
        a_re.reshape(g, p, 1), a_im.reshape(g, p, 1), log_step.reshape(g, 1, 1), b_re, b_im,
        dl_re, dl_im, dbb_re, dbb_im)


def _blockdiag(t):
    sb, n, r, c = t.shape
    eye = jnp.eye(n, dtype=bool)[None, :, None, :, None]
    full = jnp.where(eye, t[:, :, :, None, :], jnp.zeros((), t.dtype))
    return full.reshape(sb, n * r, n * c)


def _blockdiag_extract(m, r, c):
    sb = m.shape[0]
    n = m.shape[1] // r
    m5 = m.reshape(sb, n, r, n, c)
    return jnp.stack([m5[:, i, :, i, :] for i in range(n)], axis=1)


S5_TB = 256
S5_UNROLL = 8


def _lam_power(pr, pi, n):
    for _ in range(int(math.log2(n))):
        pr, pi = _cmul(pr, pi, pr, pi)
    return pr, pi


def _segment_entries(er, ei, pr, pi, reverse):
    seg, ns = er.shape
    row = lax.broadcasted_iota(jnp.int32, (seg, ns), 0)
    cr = jnp.zeros((seg, ns), F32)
    ci = jnp.zeros((seg, ns), F32)
    cur_r = jnp.zeros((1, ns), F32)
    cur_i = jnp.zeros((1, ns), F32)
    for s in (range(seg - 2, -1, -1) if reverse else range(1, seg)):
        src = s + 1 if reverse else s - 1
        mr, mi = _cmul(pr, pi, cur_r, cur_i)
        cur_r = jnp.sum(jnp.where(row == src, er, 0.0), axis=0, keepdims=True) + mr
        cur_i = jnp.sum(jnp.where(row == src, ei, 0.0), axis=0, keepdims=True) + mi
        cr = jnp.where(row == s, cur_r, cr)
        ci = jnp.where(row == s, cur_i, ci)
    return cr, ci


def s5_scan_fused_fwd(a_p, lam_re, lam_im, wb_re, wb_im, wc_re, wc_im, d_skip, name):
    l = a_p.shape[0]
    di = d_skip.shape[1]
    rows = S5_SEG * S5_TB
    nb = l // rows
    ns = wb_re.shape[2]

    def body(u_ref, lr_ref, li_ref, wbr_ref, wbi_ref, wcr_ref, wci_ref, ds_ref, y_ref, yg_ref, ckr_ref, cki_ref,
             bur, bui):
        lr = jnp.broadcast_to(lr_ref[0], (S5_SEG, ns))
        li = jnp.broadcast_to(li_ref[0], (S5_SEG, ns))

        def scan_block(b, carry, keep):
            def step(t, c):
                xr, xi = c
                sl = pl.ds(pl.multiple_of(b * rows + t * S5_SEG, S5_SEG), S5_SEG)
                nr = lr * xr - li * xi + bur[sl, :]
                ni = lr * xi + li * xr + bui[sl, :]
                if keep:
                    bur[sl, :] = nr
                    bui[sl, :] = ni
                return nr, ni

            return lax.fori_loop(0, S5_TB, step, carry, unroll=S5_UNROLL)

        def project(b, carry):
            rs = pl.ds(pl.multiple_of(b * rows, rows), rows)
            u = u_ref[rs, :]
            bur[rs, :] = _dot(u, wbr_ref[0], DN_NN)
            bui[rs, :] = _dot(u, wbi_ref[0], DN_NN)
            return scan_block(b, carry, False)

        zero = jnp.zeros((S5_SEG, ns), F32)
        er, ei = lax.fori_loop(0, nb, project, (zero, zero))
        pr, pi = _lam_power(lr_ref[0], li_ref[0], l // S5_SEG)
        entry = _segment_entries(er, ei, pr, pi, False)

        def emit(b, carry):
            ckr_ref[0, b] = carry[0]
            cki_ref[0, b] = carry[1]
            carry = scan_block(b, carry, True)
            rs = pl.ds(pl.multiple_of(b * rows, rows), rows)
            y = (_dot(bur[rs, :], wcr_ref[0], DN_NN) - _dot(bui[rs, :], wci_ref[0], DN_NN)
                 + ds_ref[...] * u_ref[rs, :])
            y_ref[rs, :] = y
            yg_ref[rs, :] = _gelu(y).astype(BF16)
            return carry

        lax.fori_loop(0, nb, emit, entry)

    sb3 = lambda s: (s, 0, 0)
    st = jax.ShapeDtypeStruct
    return pl.pallas_call(
        body, name=name, grid=(S5_SB,),
        in_specs=[pl.BlockSpec((l, LANES), lambda s: (0, s)),
                  pl.BlockSpec((1, 1, ns), sb3), pl.BlockSpec((1, 1, ns), sb3),
                  pl.BlockSpec((1, LANES, ns), sb3), pl.BlockSpec((1, LANES, ns), sb3),
                  pl.BlockSpec((1, ns, LANES), sb3), pl.BlockSpec((1, ns, LANES), sb3),
                  pl.BlockSpec((1, LANES), lambda s: (0, s))],
        out_specs=[pl.BlockSpec((l, LANES), lambda s: (0, s)), pl.BlockSpec((l, LANES), lambda s: (0, s)),
                   pl.BlockSpec((1, nb, S5_SEG, ns), lambda s: (s, 0, 0, 0)),
                   pl.BlockSpec((1, nb, S5_SEG, ns), lambda s: (s, 0, 0, 0))],
        out_shape=[st((l, di), F32), st((l, di), BF16),
                   st((S5_SB, nb, S5_SEG, ns), F32), st((S5_SB, nb, S5_SEG, ns), F32)],
        scratch_shapes=[pltpu.VMEM((l, ns), F32), pltpu.VMEM((l, ns), F32)],
        compiler_params=_cparams(("parallel",)))(a_p, lam_re, lam_im, wb_re, wb_im, wc_re, wc_im, d_skip)


def s5_scan_fused_bwd(a_p, dy, lam_re, lam_im, wb_re, wb_im, wc_re, wc_im, d_skip, ck_re, ck_im, name):
    l = a_p.shape[0]
    di = d_skip.shape[1]
    rows = S5_SEG * S5_TB
    nb = l // rows
    ns = wb_re.shape[2]

    def body(u_ref, dy_ref, lr_ref, li_ref, wbr_ref, wbi_ref, wcr_ref, wci_ref, ds_ref, ckr_ref, cki_ref,
             du_ref, dwbr_ref, dwbi_ref, dwcr_ref, dwci_ref, dds_ref, dlr_ref, dli_ref, gr, gi, xr_b, xi_b):
        lr = jnp.broadcast_to(lr_ref[0], (S5_SEG, ns))
        li = jnp.broadcast_to(li_ref[0], (S5_SEG, ns))

        def back_project(k, carry):
            b = nb - 1 - k
            rs = pl.ds(pl.multiple_of(b * rows, rows), rows)
            dyv = dy_ref[rs, :]
            gr[rs, :] = _dot(dyv, wcr_ref[0], DN_NT)
            gi[rs, :] = -_dot(dyv, wci_ref[0], DN_NT)

            def step(kk, c):
                ar, ai = c
                sl = pl.ds(pl.multiple_of(b * rows + (S5_TB - 1 - kk) * S5_SEG, S5_SEG), S5_SEG)
                return gr[sl, :] + lr * ar + li * ai, gi[sl, :] + lr * ai - li * ar

            return lax.fori_loop(0, S5_TB, step, carry, unroll=S5_UNROLL)

        zero = jnp.zeros((S5_SEG, ns), F32)
        er, ei = lax.fori_loop(0, nb, back_project, (zero, zero))
        pr, pi = _lam_power(lr_ref[0], -li_ref[0], l // S5_SEG)
        a0r, a0i = _segment_entries(er, ei, pr, pi, True)

        dwbr_ref[...] = jnp.zeros_like(dwbr_ref)
        dwbi_ref[...] = jnp.zeros_like(dwbi_ref)
        dwcr_ref[...] = jnp.zeros_like(dwcr_ref)
        dwci_ref[...] = jnp.zeros_like(dwci_ref)
        dds_ref[...] = jnp.zeros_like(dds_ref)

        def block(k, carry):
            b = nb - 1 - k
            rs = pl.ds(pl.multiple_of(b * rows, rows), rows)
            u = u_ref[rs, :]
            dyv = dy_ref[rs, :]
            body_rows = pl.ds(S5_SEG, rows)
            x0r, x0i = ckr_ref[0, b], cki_ref[0, b]
            xr_b[0:S5_SEG, :] = x0r
            xi_b[0:S5_SEG, :] = x0i
            xr_b[body_rows, :] = _dot(u, wbr_ref[0], DN_NN)
            xi_b[body_rows, :] = _dot(u, wbi_ref[0], DN_NN)

            def fstep(t, c):
                xr, xi = c
                sl = pl.ds(pl.multiple_of((t + 1) * S5_SEG, S5_SEG), S5_SEG)
                nr = lr * xr - li * xi + xr_b[sl, :]
                ni = lr * xi + li * xr + xi_b[sl, :]
                xr_b[sl, :] = nr
                xi_b[sl, :] = ni
                return nr, ni

            lax.fori_loop(0, S5_TB, fstep, (x0r, x0i), unroll=S5_UNROLL)
            dwcr_ref[0] += _dot(xr_b[body_rows, :], dyv, DN_TN)
            dwci_ref[0] -= _dot(xi_b[body_rows, :], dyv, DN_TN)

            def bstep(kk, c):
                ar, ai = c
                sl = pl.ds(pl.multiple_of(b * rows + (S5_TB - 1 - kk) * S5_SEG, S5_SEG), S5_SEG)
                nr = gr[sl, :] + lr * ar + li * ai
                ni = gi[sl, :] + lr * ai - li * ar
                gr[sl, :] = nr
                gi[sl, :] = ni
                return nr, ni

            ar, ai = lax.fori_loop(0, S5_TB, bstep, carry[:2], unroll=S5_UNROLL)
            a_r, a_i = gr[rs, :], gi[rs, :]
            p_r, p_i = xr_b[0:rows, :], xi_b[0:rows, :]
            per_seg = lambda v: jnp.sum(v.reshape(S5_TB, S5_SEG, ns), axis=0)
            carry = (ar, ai, carry[2] + per_seg(a_r * p_r + a_i * p_i), carry[3] + per_seg(a_i * p_r - a_r * p_i))
            du_ref[rs, :] = (_dot(a_r, wbr_ref[0], DN_NT) + _dot(a_i, wbi_ref[0], DN_NT) + ds_ref[...] * dyv).astype(BF16)
            dwbr_ref[0] += _dot(u, a_r, DN_TN)
            dwbi_ref[0] += _dot(u, a_i, DN_TN)
            dds_ref[...] += jnp.sum(dyv * u, axis=0, keepdims=True)
            return carry

        _, _, dlr, dli = lax.fori_loop(0, nb, block, (a0r, a0i, zero, zero))
        dlr_ref[0] = dlr
        dli_ref[0] = dli

    sb3 = lambda s: (s, 0, 0)
    seq = pl.BlockSpec((l, LANES), lambda s: (0, s))
    ck = pl.BlockSpec((1, nb, S5_SEG, ns), lambda s: (s, 0, 0, 0))
    st = jax.ShapeDtypeStruct
    return pl.pallas_call(
        body, name=name, grid=(S5_SB,),
        in_specs=[seq, seq, pl.BlockSpec((1, 1, ns), sb3), pl.BlockSpec((1, 1, ns), sb3),
                  pl.BlockSpec((1, LANES, ns), sb3), pl.BlockSpec((1, LANES, ns), sb3),
                  pl.BlockSpec((1, ns, LANES), sb3), pl.BlockSpec((1, ns, LANES), sb3),
                  pl.BlockSpec((1, LANES), lambda s: (0, s)), ck, ck],
        out_specs=[seq, pl.BlockSpec((1, LANES, ns), sb3), pl.BlockSpec((1, LANES, ns), sb3),
                   pl.BlockSpec((1, ns, LANES), sb3), pl.BlockSpec((1, ns, LANES), sb3),
                   pl.BlockSpec((1, LANES), lambda s: (0, s)),
                   pl.BlockSpec((1, S5_SEG, ns), sb3), pl.BlockSpec((1, S5_SEG, ns), sb3)],
        out_shape=[st((l, di), BF16), st((S5_SB, LANES, ns), F32), st((S5_SB, LANES, ns), F32),
                   st((S5_SB, ns, LANES), F32), st((S5_SB, ns, LANES), F32), st((1, di), F32),
                   st((S5_SB, S5_SEG, ns), F32), st((S5_SB, S5_SEG, ns), F32)],
        scratch_shapes=[pltpu.VMEM((l, ns), F32), pltpu.VMEM((l, ns), F32),
                        pltpu.VMEM((rows + S5_SEG, ns), F32), pltpu.VMEM((rows + S5_SEG, ns), F32)],
        compiler_params=_cparams(("parallel",)))(
        a_p, dy, lam_re, lam_im, wb_re, wb_im, wc_re, wc_im, d_skip, ck_re, ck_im)


def s5_gate_fwd(y, t, b_glu, a_p, name):
    l, d = y.shape
    tl = ROW_TILE

    def body(y_ref, t_ref, b_ref, z_ref, m_ref):
        yg = _gelu(y_ref[...])
        m_ref[...] = (yg * _sigmoid(t_ref[...] + b_ref[...]) * _silu(z_ref[...])).astype(BF16)

    return _rowcall(body, name, l // tl, [_rows(tl, d), _rows(tl, d), _full((1, d)), _rows(tl, d, 1)], _rows(tl, d),
                    jax.ShapeDtypeStruct((l, d), BF16))(y, t, b_glu.reshape(1, d), a_p)


def s5_gate_bwd(dm, y, t, b_glu, a_p, name):
    l, d = y.shape
    tl = ROW_TILE_HEAVY

    def body(dm_ref, y_ref, t_ref, b_ref, z_ref, dt_ref, dyg_ref, dz_ref, db_ref):
        i = pl.program_id(0)
        dmv = dm_ref[...]
        z = z_ref[...]
        yg = _gelu(y_ref[...])
        sg = _sigmoid(t_ref[...] + b_ref[...])
        y2 = yg * sg
        sz, gz = _silu_both(z)
        dy2 = dmv * sz
        dz_ref[...] = (dmv * y2 * gz).astype(BF16)
        dyg_ref[...] = dy2 * sg
        dt = dy2 * yg * sg * (1.0 - sg)
        dt_ref[...] = dt.astype(BF16)
        _acc(db_ref, jnp.sum(dt, axis=0, keepdims=True), i)

    st = jax.ShapeDtypeStruct
    return _rowcall(body, name, l // tl, [_rows(tl, d), _rows(tl, d), _rows(tl, d), _full((1, d)), _rows(tl, d, 1)],
                    [_rows(tl, d), _rows(tl, d), _rows(tl, d), _full((1, d))],
                    [st((l, d), BF16), st((l, d), F32), st((l, d), BF16), st((1, d), F32)])(
        dm, y, t, b_glu.reshape(1, d), a_p)


def s5_act_bwd(y, dyg_a, dyg_b, name):
    l, d = y.shape
    tl = ROW_TILE

    def body(y_ref, a_ref, b_ref, o_ref):
        o_ref[...] = (a_ref[...] + b_ref[...]) * _gelu_grad(y_ref[...])

    return _rowcall(body, name, l // tl, [_rows(tl, d)] * 3, _rows(tl, d), jax.ShapeDtypeStruct((l, d), F32))(y, dyg_a, dyg_b)


def _seg_perm(t):
    l, d = t.shape
    return t.reshape(S5_SEG, l // S5_SEG, d).transpose(1, 0, 2).reshape(l, d)


def _seg_unperm(t):
    l, d = t.shape
    return t.reshape(l // S5_SEG, S5_SEG, d).transpose(1, 0, 2).reshape(l, d)


def _s5_weights(p):
    lr, li, bbr, bbi = s5_params_fwd(p["a_re"], p["a_im"], p["log_step"], p["b_re"], p["b_im"])
    ns = 8 * S5_STATE
    lam_re = lr.reshape(S5_SB, 1, ns)
    lam_im = li.reshape(S5_SB, 1, ns)
    to_bd = lambda t: _blockdiag(t.reshape(S5_SB, 8, t.shape[1], t.shape[2]))
    wb_re = to_bd(bbr.transpose(0, 2, 1)).astype(BF16)
    wb_im = to_bd(bbi.transpose(0, 2, 1)).astype(BF16)
    wc_re = to_bd(p["c_re"].transpose(0, 2, 1)).astype(BF16)
    wc_im = to_bd(p["c_im"].transpose(0, 2, 1)).astype(BF16)
    return lam_re, lam_im, wb_re, wb_im, wc_re, wc_im


def s5_layer_fwd(h, p, wf, sw, tag):
    l = h.shape[0]
    di = p["d_skip"].shape[0]
    hn = rms_fwd(h, p["norm_g"], tag + "_rms")
    hn_p = _seg_perm(hn)
    a_p = matmul(hn_p, wf["w_in"], "nn", tag + "_mm_in")
    dsk = p["d_skip"].reshape(1, di)
    y, yg, ck_re, ck_im = s5_scan_fused_fwd(a_p, *sw, dsk, tag + "_scan")
    t = matmul(yg, wf["w_glu"], "nn", tag + "_mm_glu")
    m = s5_gate_fwd(y, t, p["b_glu"], a_p, tag + "_gate")
    out_p = matmul(m, wf["w_out"], "nn", tag + "_mm_out")
    h_out = residual_add(h, _seg_unperm(out_p), tag + "_res")
    return h_out, (h, hn_p, a_p, sw, ck_re, ck_im, y, yg, t, m)


def residual_add(h, y, name):
    l, d = h.shape
    tl = ROW_TILE_NARROW

    def body(h_ref, y_ref, o_ref):
        o_ref[...] = h_ref[...] + y_ref[...]

    return _rowcall(body, name, l // tl, [_rows(tl, d)] * 2, _rows(tl, d), jax.ShapeDtypeStruct((l, d), F32))(h, y)


def s5_layer_bwd(dh_out, saved, p, wf, tag, sink):
    h, hn_p, a_p, sw, ck_re, ck_im, y, yg, t, m = saved
    l = h.shape[0]
    di = p["d_skip"].shape[0]
    dsk = p["d_skip"].reshape(1, di)
    dout_p = _seg_perm(dh_out)
    dm = matmul(dout_p, wf["w_out"], "nt", tag + "_mm_dm")
    sink.mm("w_out", m, dout_p, tag + "_mm_gwout")
    dt, dyg_a, dz, db_glu = s5_gate_bwd(dm, y, t, p["b_glu"], a_p, tag + "_gate_bwd")
    dyg_b = matmul(dt, wf["w_glu"], "nt", tag + "_mm_dyg")
    sink.mm("w_glu", yg, dt, tag + "_mm_gwglu")
    dy = s5_act_bwd(y, dyg_a, dyg_b, tag + "_act_bwd")
    du, dwbr, dwbi, dwcr, dwci, dds, dlr, dli = s5_scan_fused_bwd(a_p, dy, *sw, dsk, ck_re, ck_im, tag + "_scanb")
    da = jnp.concatenate([du, dz], axis=1)
    dhn_p = matmul(da, wf["w_in"], "nt", tag + "_mm_dhn")
    sink.mm("w_in", hn_p, da, tag + "_mm_gwin")
    zero = sink.send()
    dh, dng = rms_bwd(h, p["norm_g"] + zero, _seg_unperm(dhn_p), dh_out, tag + "_rms_bwd")
    ex = lambda m_, r, c: _blockdiag_extract(m_, r, c).reshape(S5_GROUPS, r, c).transpose(0, 2, 1)
    dbb_re, dbb_im = ex(dwbr, S5_GROUP, S5_STATE), ex(dwbi, S5_GROUP, S5_STATE)
    g_c_re, g_c_im = ex(dwcr, S5_STATE, S5_GROUP), ex(dwci, S5_STATE, S5_GROUP)
    dl_re = lane_sum8(dlr).reshape(S5_GROUPS, S5_STATE, 1)
    dl_im = lane_sum8(dli).reshape(S5_GROUPS, S5_STATE, 1)
    gar, gai, gls, gbr, gbi = s5_params_bwd(p["a_re"], p["a_im"], p["log_step"], p["b_re"], p["b_im"],
                                            dl_re, dl_im, dbb_re, dbb_im)
    grads = {"norm_g": dng.reshape(-1), "a_re": gar.reshape(S5_GROUPS, S5_STATE),
             "a_im": gai.reshape(S5_GROUPS, S5_STATE), "log_step": gls.reshape(-1), "b_re": gbr, "b_im": gbi,
             "c_re": g_c_re, "c_im": g_c_im, "d_skip": dds.reshape(-1), "b_glu": db_glu.reshape(-1)}
    return dh, grads


def lane_sum8(t):
    sb, seg, ns = t.shape

    def body(t_ref, o_ref):
        o_ref[...] = jnp.sum(t_ref[...], axis=1, keepdims=True)

    return pl.pallas_call(body, name="s5_seg_sum", out_shape=jax.ShapeDtypeStruct((sb, 1, ns), F32))(t)


MLA_DI = MLA_HEADS * 128
MLA_CQ0 = MLA_DI
MLA_CKV0 = MLA_CQ0 + MLA_Q_RANK
MLA_KR0 = MLA_CKV0 + MLA_KV_RANK
MLA_AW = MLA_KR0 + LANES


def _rot_half(x):
    w = x.shape[-1]
    lane = lax.broadcasted_iota(jnp.int32, x.shape, x.ndim - 1)
    return jnp.where(lane % MLA_ROPE < MLA_ROPE // 2, pltpu.roll(x, w - MLA_ROPE // 2, x.ndim - 1),
                     pltpu.roll(x, MLA_ROPE // 2, x.ndim - 1))


def rope_tables(pos, zero):
    l = pos.shape[0]
    tl = ROW_TILE
    j = np.arange(LANES) % MLA_ROPE % (MLA_ROPE // 2)
    inv_freq = (ROPE_THETA ** (-(2.0 * j) / MLA_ROPE)).astype(np.float32).reshape(1, LANES)
    sign = np.where(np.arange(LANES) % MLA_ROPE < MLA_ROPE // 2, -1.0, 1.0).astype(np.float32).reshape(1, LANES)

    def body(p_ref, f_ref, s_ref, cos_ref, sin_ref):
        ang = p_ref[...].astype(F32) * f_ref[...]
        cos_ref[...] = jnp.cos(ang)
        sin_ref[...] = jnp.sin(ang) * s_ref[...]

    st = jax.ShapeDtypeStruct((l, LANES), F32)
    return _rowcall(body, "rope_tables", l // tl, [_rows(tl, 1), _full((1, LANES)), _full((1, LANES))],
                    [_rows(tl, LANES)] * 2, [st, st])(pos, jnp.asarray(inv_freq), jnp.asarray(sign) + zero)


def _rope(x, cos, sins):
    return x * cos + _rot_half(x) * sins


def _rope_t(dy, cos, sins):
    return dy * cos - sins * _rot_half(dy)


def _rmsn(x):
    r = lax.rsqrt(jnp.mean(x * x, axis=-1, keepdims=True) + NORM_EPS)
    return x * r, r


def mla_pre(a, q_g, kv_g, cos, sins, name):
    l = a.shape[0]
    tl = ROW_TILE

    def body(a_ref, qg_ref, kg_ref, cos_ref, sin_ref, cq_ref, ckv_ref, krs_ref):
        xq, _ = _rmsn(a_ref[:, MLA_CQ0:MLA_CKV0])
        cq_ref[...] = (xq * qg_ref[...]).astype(BF16)
        xk, _ = _rmsn(a_ref[:, MLA_CKV0:MLA_KR0])
        ckv_ref[...] = (xk * kg_ref[...]).astype(BF16)
        kr = a_ref[:, MLA_KR0:MLA_AW]
        kr2 = kr + pltpu.roll(kr, MLA_ROPE, 1)
        kr2 = _rope(kr2, cos_ref[...], sin_ref[...])
        lane = lax.broadcasted_iota(jnp.int32, kr2.shape, 1)
        krs_ref[0] = jnp.where(lane < MLA_ROPE, kr2, 0.0).astype(BF16)
        krs_ref[1] = jnp.where(lane >= MLA_ROPE, kr2, 0.0).astype(BF16)

    st = jax.ShapeDtypeStruct
    return _rowcall(body, name, l // tl,
                    [_rows(tl, MLA_AW), _full((1, MLA_Q_RANK)), _full((1, MLA_KV_RANK)), _rows(tl, LANES), _rows(tl, LANES)],
                    [_rows(tl, MLA_Q_RANK), _rows(tl, MLA_KV_RANK), pl.BlockSpec((2, tl, LANES), lambda i: (0, i, 0))],
                    [st((l, MLA_Q_RANK), BF16), st((l, MLA_KV_RANK), BF16), st((2, l, LANES), BF16)])(
        a, q_g.reshape(1, -1), kv_g.reshape(1, -1), cos, sins)


def mla_rope_q(qr, cos, sins, name):
    l, w = qr.shape
    tl = ROW_TILE

    def body(q_ref, cos_ref, sin_ref, o_ref):
        c, s = cos_ref[...], sin_ref[...]
        for p in range(w // LANES):
            sl = slice(p * LANES, (p + 1) * LANES)
            o_ref[:, sl] = _rope(q_ref[:, sl], c, s).astype(BF16)

    return _rowcall(body, name, l // tl, [_rows(tl, w), _rows(tl, LANES), _rows(tl, LANES)], _rows(tl, w),
                    jax.ShapeDtypeStruct((l, w), BF16))(qr, cos, sins)


ATT_OUT = 512
ATT_IN = 512
ATT_R = ATT_OUT // ATT_IN


def _scores(qn, qr, kn, kr, mask_off, transposed):
    q2 = jnp.concatenate([qn, qr], axis=1)
    k2 = jnp.concatenate([kn, kr], axis=1)
    s = (_dot(k2, q2, DN_NT) if transposed else _dot(q2, k2, DN_NT)) * MLA_SCALE
    if mask_off is None:
        return s
    r = lax.broadcasted_iota(jnp.int32, s.shape, 0)
    c = lax.broadcasted_iota(jnp.int32, s.shape, 1)
    return jnp.where((r <= c + mask_off) if transposed else (c + mask_off <= r), s, NEG_INF)


def _fold(x, op):
    out = x[:, :LANES]
    for t in range(1, x.shape[1] // LANES):
        out = op(out, x[:, t * LANES:(t + 1) * LANES])
    return out


def flash_fwd(qn, qr, kv, krs, name):
    l = qn.shape[0]
    nq = l // ATT_OUT

    def body(qn_ref, qr_ref, kv_ref, kr_ref, o_ref, lse_ref, s_buf):
        qi = pl.program_id(1)
        q_r = qr_ref[...]
        q_n = [qn_ref[:, hh * LANES:(hh + 1) * LANES] for hh in range(2)]

        def block_scores(j, mx, mask_off):
            sl = pl.ds(pl.multiple_of(j * ATT_IN, ATT_IN), ATT_IN)
            out = []
            for hh in range(2):
                s = _scores(q_n[hh], q_r, kv_ref[sl, 2 * hh * LANES:(2 * hh + 1) * LANES], kr_ref[hh, sl, :],
                            mask_off, False)
                s_buf[hh, j] = s
                out.append(jnp.maximum(mx[hh], _fold(s, jnp.maximum)))
            return tuple(out)

        ninf = jnp.full((ATT_OUT, LANES), NEG_INF, F32)
        mx = lax.fori_loop(0, ATT_R * qi, lambda j, c: block_scores(j, c, None), (ninf, ninf))
        for d in range(ATT_R):
            mx = block_scores(ATT_R * qi + d, mx, d * ATT_IN)
        m = [jnp.max(mx[hh], axis=-1, keepdims=True) for hh in range(2)]

        def block_pv(j, carry):
            sl = pl.ds(pl.multiple_of(j * ATT_IN, ATT_IN), ATT_IN)
            out = []
            for hh in range(2):
                ls, acc = carry[hh]
                p = jnp.exp(s_buf[hh, j] - m[hh])
                out.append((ls + _fold(p, jnp.add),
                            acc + _dot(p, kv_ref[sl, (2 * hh + 1) * LANES:(2 * hh + 2) * LANES], DN_NN)))
            return tuple(out)

        z = jnp.zeros((ATT_OUT, LANES), F32)
        res = lax.fori_loop(0, ATT_R * (qi + 1), block_pv, ((z, z), (z, z)))
        for hh in range(2):
            lsum = jnp.sum(res[hh][0], axis=-1, keepdims=True)
            o_ref[:, hh * LANES:(hh + 1) * LANES] = res[hh][1] / lsum
            lse_ref[hh] = m[hh] + jnp.log(lsum)

    st = jax.ShapeDtypeStruct
    return pl.pallas_call(
        body, name=name, grid=(MLA_HEADS // 2, nq),
        in_specs=[pl.BlockSpec((ATT_OUT, 2 * LANES), lambda p, i: (i, p)),
                  pl.BlockSpec((ATT_OUT, LANES), lambda p, i: (i, p)),
                  pl.BlockSpec((l, 4 * LANES), lambda p, i: (0, p)),
                  pl.BlockSpec((2, l, LANES), lambda p, i: (0, 0, 0))],
        out_specs=[pl.BlockSpec((ATT_OUT, 2 * LANES), lambda p, i: (i, p)),
                   pl.BlockSpec((2, ATT_OUT, 1), lambda p, i: (p, i, 0))],
        out_shape=[st((l, MLA_DI), F32), st((MLA_HEADS, l, 1), F32)],
        scratch_shapes=[pltpu.VMEM((2, l // ATT_IN, ATT_OUT, ATT_IN), F32)],
        compiler_params=_cparams(("parallel", "arbitrary")))(qn, qr, kv, krs)


def flash_dkv(qn, qr, kv, krs, do, lse_row, delta_row, name):
    l = qn.shape[0]
    nk = l // ATT_OUT
    nq = l // ATT_IN

    def body(qn_ref, qr_ref, do_ref, lse_ref, dl_ref, kv_ref, kr_ref, dkv_ref, dkr_ref):
        kj = pl.program_id(1)
        lane = lax.broadcasted_iota(jnp.int32, (ATT_OUT, LANES), 1)
        kn = [kv_ref[:, 2 * hh * LANES:(2 * hh + 1) * LANES] for hh in range(2)]
        v = [kv_ref[:, (2 * hh + 1) * LANES:(2 * hh + 2) * LANES] for hh in range(2)]

        def block(i, carry, mask_off):
            sl = pl.ds(pl.multiple_of(i * ATT_IN, ATT_IN), ATT_IN)
            q_r = qr_ref[sl, :]
            out = []
            for hh in range(2):
                dk2, dv = carry[hh]
                hs = slice(hh * LANES, (hh + 1) * LANES)
                q_n, d_o = qn_ref[sl, hs], do_ref[sl, hs]
                s = _scores(q_n, q_r, kn[hh], kr_ref[hh], mask_off, True)
                pt = jnp.exp(s - lse_ref[hh, i])
                dv = dv + _dot(pt, d_o, DN_NN)
                dpt = _dot(v[hh], d_o, DN_NT)
                dst = (pt * (dpt - dl_ref[hh, i]) * MLA_SCALE).astype(BF16)
                out.append((dk2 + _dot(dst, jnp.concatenate([q_n, q_r], axis=1), DN_NN), dv))
            return tuple(out)

        z = jnp.zeros((ATT_OUT, LANES), F32)
        z2 = jnp.zeros((ATT_OUT, 2 * LANES), F32)
        res = ((z2, z), (z2, z))
        for d in range(ATT_R):
            res = block(ATT_R * kj + d, res, d * ATT_IN)
        res = lax.fori_loop(ATT_R * (kj + 1), nq, lambda i, c: block(i, c, None), res)
        for hh in range(2):
            dkv_ref[:, 2 * hh * LANES:(2 * hh + 1) * LANES] = res[hh][0][:, :LANES].astype(BF16)
            dkv_ref[:, (2 * hh + 1) * LANES:(2 * hh + 2) * LANES] = res[hh][1].astype(BF16)
        dkr_ref[0] = jnp.where(lane < MLA_ROPE, res[0][0][:, LANES:], res[1][0][:, LANES:])

    st = jax.ShapeDtypeStruct
    return pl.pallas_call(
        body, name=name, grid=(MLA_HEADS // 2, nk),
        in_specs=[pl.BlockSpec((l, 2 * LANES), lambda p, j: (0, p)),
                  pl.BlockSpec((l, LANES), lambda p, j: (0, p)),
                  pl.BlockSpec((l, 2 * LANES), lambda p, j: (0, p)),
                  pl.BlockSpec((2, nq, 1, ATT_IN), lambda p, j: (p, 0, 0, 0)),
                  pl.BlockSpec((2, nq, 1, ATT_IN), lambda p, j: (p, 0, 0, 0)),
                  pl.BlockSpec((ATT_OUT, 4 * LANES), lambda p, j: (j, p)),
                  pl.BlockSpec((2, ATT_OUT, LANES), lambda p, j: (0, j, 0))],
        out_specs=[pl.BlockSpec((ATT_OUT, 4 * LANES), lambda p, j: (j, p)),
                   pl.BlockSpec((1, ATT_OUT, LANES), lambda p, j: (p, j, 0))],
        out_shape=[st((l, 2 * MLA_DI), BF16), st((MLA_HEADS // 2, l, LANES), F32)],
        compiler_params=_cparams(("parallel", "arbitrary")))(qn, qr, do, lse_row, delta_row, kv, krs)


def flash_dq(qn, qr, kv, krs, do, lse, delta, cos, sins, name):
    l = qn.shape[0]
    nq = l // ATT_OUT

    def body(qn_ref, qr_ref, do_ref, lse_ref, dl_ref, kv_ref, kr_ref, cos_ref, sin_ref, dqn_ref, dqr_ref):
        qi = pl.program_id(1)
        q_r = qr_ref[...]
        q_n = [qn_ref[:, hh * LANES:(hh + 1) * LANES] for hh in range(2)]
        d_o = [do_ref[:, hh * LANES:(hh + 1) * LANES] for hh in range(2)]
        lse_h = [lse_ref[hh] for hh in range(2)]
        dl_h = [dl_ref[hh] for hh in range(2)]

        def block(j, carry, mask_off):
            sl = pl.ds(pl.multiple_of(j * ATT_IN, ATT_IN), ATT_IN)
            dq2 = list(carry)
            for hh in range(2):
                kn = kv_ref[sl, 2 * hh * LANES:(2 * hh + 1) * LANES]
                v = kv_ref[sl, (2 * hh + 1) * LANES:(2 * hh + 2) * LANES]
                kr = kr_ref[hh, sl, :]
                s = _scores(q_n[hh], q_r, kn, kr, mask_off, False)
                pr = jnp.exp(s - lse_h[hh])
                dp = _dot(d_o[hh], v, DN_NT)
                ds = (pr * (dp - dl_h[hh]) * MLA_SCALE).astype(BF16)
                dq2[hh] = dq2[hh] + _dot(ds, jnp.concatenate([kn, kr], axis=1), DN_NN)
            return tuple(dq2)

        z2 = jnp.zeros((ATT_OUT, 2 * LANES), F32)
        res = lax.fori_loop(0, ATT_R * qi, lambda j, c: block(j, c, None), (z2, z2))
        for d in range(ATT_R):
            res = block(ATT_R * qi + d, res, d * ATT_IN)
        dqn_ref[:, 0:LANES] = res[0][:, :LANES].astype(BF16)
        dqn_ref[:, LANES:2 * LANES] = res[1][:, :LANES].astype(BF16)
        dqr = res[0][:, LANES:] + res[1][:, LANES:]
        dqr_ref[...] = _rope_t(dqr, cos_ref[...], sin_ref[...]).astype(BF16)

    st = jax.ShapeDtypeStruct
    return pl.pallas_call(
        body, name=name, grid=(MLA_HEADS // 2, nq),
        in_specs=[pl.BlockSpec((ATT_OUT, 2 * LANES), lambda p, i: (i, p)),
                  pl.BlockSpec((ATT_OUT, LANES), lambda p, i: (i, p)),
                  pl.BlockSpec((ATT_OUT, 2 * LANES), lambda p, i: (i, p)),
                  pl.BlockSpec((2, ATT_OUT, 1), lambda p, i: (p, i, 0)),
                  pl.BlockSpec((2, ATT_OUT, 1), lambda p, i: (p, i, 0)),
                  pl.BlockSpec((l, 4 * LANES), lambda p, i: (0, p)),
                  pl.BlockSpec((2, l, LANES), lambda p, i: (0, 0, 0)),
                  pl.BlockSpec((ATT_OUT, LANES), lambda p, i: (i, 0)),
                  pl.BlockSpec((ATT_OUT, LANES), lambda p, i: (i, 0))],
        out_specs=[pl.BlockSpec((ATT_OUT, 2 * LANES), lambda p, i: (i, p)),
                   pl.BlockSpec((ATT_OUT, LANES), lambda p, i: (i, p))],
        out_shape=[st((l, MLA_DI), BF16), st((l, MLA_HEADS * MLA_ROPE), BF16)],
        compiler_params=_cparams(("parallel", "arbitrary")))(qn, qr, do, lse, delta, kv, krs, cos, sins)


def mla_gate_fwd(o, a, name):
    l = o.shape[0]
    tl = ROW_TILE

    def body(o_ref, z_ref, m_ref):
        m_ref[...] = (o_ref[...] * _silu(z_ref[...])).astype(BF16)

    return _rowcall(body, name, l // tl, [_rows(tl, MLA_DI), _rows(tl, MLA_DI)], _rows(tl, MLA_DI),
                    jax.ShapeDtypeStruct((l, MLA_DI), BF16))(o, a)


def mla_gate_bwd(dm, o, a, name):
    l = o.shape[0]
    tl = ROW_TILE

    def body(dm_ref, o_ref, z_ref, do_ref, dz_ref, dl_ref):
        dmv, ov, z = dm_ref[...], o_ref[...], z_ref[...]
        sz, gz = _silu_both(z)
        d_o = dmv * sz
        do_ref[...] = d_o.astype(BF16)
        dz_ref[...] = (dmv * ov * gz).astype(BF16)
        pr = d_o * ov
        for h in range(MLA_HEADS):
            dl_ref[h] = jnp.sum(pr[:, h * LANES:(h + 1) * LANES], axis=1, keepdims=True)

    st = jax.ShapeDtypeStruct
    return _rowcall(body, name, l // tl, [_rows(tl, MLA_DI)] * 3,
                    [_rows(tl, MLA_DI), _rows(tl, MLA_DI), pl.BlockSpec((MLA_HEADS, tl, 1), lambda i: (0, i, 0))],
                    [st((l, MLA_DI), BF16), st((l, MLA_DI), BF16), st((MLA_HEADS, l, 1), F32)])(dm, o, a)


def mla_post(a, dcqn, dckvn, dkr_pairs, dz, q_g, kv_g, cos, sins, name):
    l = a.shape[0]
    tl = ROW_TILE
    npair = MLA_HEADS // 2

    def norm_bwd(x, g, dy):
        xhat, r = _rmsn(x)
        dxh = dy * g
        return r * (dxh - xhat * jnp.mean(dxh * xhat, axis=-1, keepdims=True)), jnp.sum(dy * xhat, axis=0, keepdims=True)

    def body(a_ref, dq_ref, dk_ref, dkr_ref, dz_ref, qg_ref, kg_ref, cos_ref, sin_ref, da_ref, dqg_ref, dkg_ref):
        i = pl.program_id(0)
        da_ref[:, 0:MLA_DI] = dz_ref[...]
        dcq, dqg = norm_bwd(a_ref[:, MLA_CQ0:MLA_CKV0], qg_ref[...], dq_ref[...])
        da_ref[:, MLA_CQ0:MLA_CKV0] = dcq.astype(BF16)
        dckv, dkg = norm_bwd(a_ref[:, MLA_CKV0:MLA_KR0], kg_ref[...], dk_ref[...])
        da_ref[:, MLA_CKV0:MLA_KR0] = dckv.astype(BF16)
        dk2 = dkr_ref[0]
        for p in range(1, npair):
            dk2 = dk2 + dkr_ref[p]
        dk2 = _rope_t(dk2, cos_ref[...], sin_ref[...])
        dk2 = dk2 + pltpu.roll(dk2, MLA_ROPE, 1)
        lane = lax.broadcasted_iota(jnp.int32, dk2.shape, 1)
        da_ref[:, MLA_KR0:MLA_AW] = jnp.where(lane < MLA_ROPE, dk2, 0.0).astype(BF16)
        _acc(dqg_ref, dqg, i)
        _acc(dkg_ref, dkg, i)

    st = jax.ShapeDtypeStruct
    return _rowcall(body, name, l // tl,
                    [_rows(tl, MLA_AW), _rows(tl, MLA_Q_RANK), _rows(tl, MLA_KV_RANK),
                     pl.BlockSpec((npair, tl, LANES), lambda i: (0, i, 0)), _rows(tl, MLA_DI),
                     _full((1, MLA_Q_RANK)), _full((1, MLA_KV_RANK)), _rows(tl, LANES), _rows(tl, LANES)],
                    [_rows(tl, MLA_AW), _full((1, MLA_Q_RANK)), _full((1, MLA_KV_RANK))],
                    [st((l, MLA_AW), BF16), st((1, MLA_Q_RANK), F32), st((1, MLA_KV_RANK), F32)])(
        a, dcqn, dckvn, dkr_pairs, dz, q_g.reshape(1, -1), kv_g.reshape(1, -1), cos, sins)


def _mla_w_in_perm(w):
    r = MLA_Q_RANK + MLA_KV_RANK + MLA_ROPE
    pad = jnp.zeros(w.shape[:-1] + (MLA_AW - MLA_KR0 - MLA_ROPE,), w.dtype)
    return jnp.concatenate([w[..., r:], w[..., :r], pad], axis=-1)


def _mla_w_in_unperm(g):
    r = MLA_Q_RANK + MLA_KV_RANK + MLA_ROPE
    return jnp.concatenate([g[..., MLA_DI:MLA_DI + r], g[..., :MLA_DI]], axis=-1)


def _mla_w_uq_split(w):
    k = w.shape[0]
    w3 = w.reshape(k, MLA_HEADS, MLA_NOPE + MLA_ROPE)
    return w3[:, :, :MLA_NOPE].reshape(k, MLA_HEADS * MLA_NOPE), w3[:, :, MLA_NOPE:].reshape(k, MLA_HEADS * MLA_ROPE)


def _mla_w_uq_merge(gn, gr):
    k = gn.shape[0]
    return jnp.concatenate([gn.reshape(k, MLA_HEADS, MLA_NOPE), gr.reshape(k, MLA_HEADS, MLA_ROPE)], axis=2).reshape(k, -1)


def mla_layer_fwd(h, p, wf, cos, sins, tag):
    hn = rms_fwd(h, p["norm_g"], tag + "_rms")
    w_in = _mla_w_in_perm(wf["w_in"])
    w_uq_n, w_uq_r = _mla_w_uq_split(wf["w_uq"])
    a = matmul(hn, w_in, "nn", tag + "_mm_in")
    cqn, ckvn, krs = mla_pre(a, p["q_norm_g"], p["kv_norm_g"], cos, sins, tag + "_pre")
    qn = matmul(cqn, w_uq_n, "nn", tag + "_mm_qn", out_dtype=BF16)
    qr_raw = matmul(cqn, w_uq_r, "nn", tag + "_mm_qr")
    qr = mla_rope_q(qr_raw, cos, sins, tag + "_rope_q")
    kv = matmul(ckvn, wf["w_ukv"], "nn", tag + "_mm_kv", out_dtype=BF16)
    o, lse = flash_fwd(qn, qr, kv, krs, tag + "_flash")
    m = mla_gate_fwd(o, a, tag + "_gate")
    h_out = matmul(m, wf["w_out"], "nn", tag + "_mm_out", add=h)
    return h_out, (h, hn, a, cqn, ckvn, krs, qn, qr, kv, o, lse, m, w_in, w_uq_n, w_uq_r)


def mla_layer_bwd(dh_out, saved, p, wf, cos, sins, tag, sink):
    h, hn, a, cqn, ckvn, krs, qn, qr, kv, o, lse, m, w_in, w_uq_n, w_uq_r = saved
    l = h.shape[0]
    dm = matmul(dh_out, wf["w_out"], "nt", tag + "_mm_dm")
    sink.mm("w_out", m, dh_out, tag + "_mm_gwout")
    do, dz, delta = mla_gate_bwd(dm, o, a, tag + "_gate_bwd")
    lse_row = lse.reshape(MLA_HEADS, l // ATT_IN, 1, ATT_IN)
    delta_row = delta.reshape(MLA_HEADS, l // ATT_IN, 1, ATT_IN)
    dkv, dkr_pairs = flash_dkv(qn, qr, kv, krs, do, lse_row, delta_row, tag + "_flash_dkv")
    dqn, dqr = flash_dq(qn, qr, kv, krs, do, lse, delta, cos, sins, tag + "_flash_dq")
    dcqn = matmul(dqn, w_uq_n, "nt", tag + "_mm_dcq_n")
    dcqn = matmul(dqr, w_uq_r, "nt", tag + "_mm_dcq_r", add=dcqn)
    g_uq_n = matmul(cqn, dqn, "tn", tag + "_mm_guq_n")
    g_uq_r = matmul(cqn, dqr, "tn", tag + "_mm_guq_r")
    dckvn = matmul(dkv, wf["w_ukv"], "nt", tag + "_mm_dckv")
    sink.mm("w_ukv", ckvn, dkv, tag + "_mm_gukv")
    da, dqg, dkg = mla_post(a, dcqn, dckvn, dkr_pairs, dz, p["q_norm_g"], p["kv_norm_g"], cos, sins, tag + "_post")
    dhn = matmul(da, w_in, "nt", tag + "_mm_dhn")
    g_w_in = matmul(hn, da, "tn", tag + "_mm_gwin")
    sink.put("w_uq", _mla_w_uq_merge(g_uq_n, g_uq_r))
    sink.put("w_in", _mla_w_in_unperm(g_w_in))
    zero = sink.send()
    dh, dng = rms_bwd(h, p["norm_g"] + zero, dhn, dh_out, tag + "_rms_bwd")
    grads = {"norm_g": dng.reshape(-1), "q_norm_g": dqg.reshape(-1), "kv_norm_g": dkg.reshape(-1)}
    return dh, grads


ANY = pl.BlockSpec(memory_space=pl.ANY)


def _me():
    return lax.axis_index("x"), lax.axis_index("y"), lax.axis_index("c")


def _chip():
    return 2 * lax.axis_index("x") + lax.axis_index("y")


def _other_chips(x, y):
    return [(1 - x, y), (x, 1 - y), (1 - x, 1 - y)]


def _rcopy(src, dst, ssem, rsem, dev):
    return pltpu.make_async_remote_copy(src_ref=src, dst_ref=dst, send_sem=ssem, recv_sem=rsem,
                                        device_id=dev, device_id_type=MESH)


def _half(ref, c, hf):
    return ref.at[pl.ds(c * hf, hf), :]


HBM = pl.BlockSpec(memory_space=pltpu.HBM)
SEM = pl.BlockSpec(memory_space=pltpu.SEMAPHORE)
SPLIT_EFFECT = pltpu.SideEffectType.DATAFLOW_SIDE_EFFECTING


def gather_start(wb, after, name):
    nr, w = wb.shape
    hf = nr // 2

    def body(w_ref, land_ref, after_ref, ssem, rsem, w_thru, land_thru, token):
        x, y, c = _me()
        k = 2 * x + y
        for j, (cx, cy) in enumerate(_other_chips(x, y)):
            _rcopy(_half(w_ref, c, hf), _half(land_ref.at[k], c, hf), ssem.at[j], rsem.at[j], (cx, cy, c)).start()
        token[...] = jnp.zeros_like(token)

    land = lax.empty((N_CHIPS, nr, w), wb.dtype)
    return pl.pallas_call(
        body, name=name,
        out_shape=(pltpu.SemaphoreType.DMA((3,)), pltpu.SemaphoreType.DMA((3,)), pltpu.HBM(wb.shape, wb.dtype),
                   pltpu.HBM(land.shape, land.dtype), jax.ShapeDtypeStruct((8, LANES), F32)),
        in_specs=(HBM, HBM, ANY), out_specs=(SEM, SEM, HBM, HBM, pl.BlockSpec(memory_space=pltpu.VMEM)),
        input_output_aliases={0: 2, 1: 3},
        compiler_params=pltpu.CompilerParams(has_side_effects=SPLIT_EFFECT))(
        pltpu.with_memory_space_constraint(wb, pltpu.HBM), pltpu.with_memory_space_constraint(land, pltpu.HBM), after)


def gather_wait(ssem, rsem, w_thru, land_thru, after, name):
    nr, w = w_thru.shape
    hf = nr // 2

    def body(w_ref, land_ref, ssem_ref, rsem_ref, after_ref, w_dead, got_ref):
        x, y, c = _me()
        for j, (cx, cy) in enumerate(_other_chips(x, y)):
            cp = _rcopy(_half(w_ref, c, hf), _half(land_ref.at[2 * cx + cy], c, hf), ssem_ref.at[j], rsem_ref.at[j],
                        (cx, cy, c))
            cp.wait_send()
            cp.wait_recv()

    return pl.pallas_call(
        body, name=name, out_shape=(pltpu.HBM(w_thru.shape, w_thru.dtype), pltpu.HBM(land_thru.shape, land_thru.dtype)),
        in_specs=(HBM, HBM, SEM, SEM, ANY), out_specs=(HBM, HBM), input_output_aliases={0: 0, 1: 1},
        compiler_params=pltpu.CompilerParams(has_side_effects=SPLIT_EFFECT))(w_thru, land_thru, ssem, rsem, after)[1]


def gather_handover(land, wb, name):
    _, nr, w = land.shape
    hf = nr // 2

    def body(l_ref, o_ref, ssem, rsem):
        x, y, c = _me()
        chips = _other_chips(x, y)
        sends = []
        for j, (cx, cy) in enumerate(chips):
            region = _half(o_ref.at[2 * cx + cy], c, hf)
            sends.append(_rcopy(region, region, ssem.at[j], rsem.at[j], (x, y, 1 - c)))
            sends[-1].start()
        for j, (cx, cy) in enumerate(chips):
            region = _half(o_ref.at[2 * cx + cy], 1 - c, hf)
            _rcopy(region, region, ssem.at[j], rsem.at[j], (x, y, 1 - c)).wait_recv()
        for cp in sends:
            cp.wait_send()

    out = pl.pallas_call(
        body, name=name, in_specs=[ANY], out_specs=ANY, input_output_aliases={0: 0},
        out_shape=jax.ShapeDtypeStruct(land.shape, land.dtype),
        scratch_shapes=[pltpu.SemaphoreType.DMA((3,)), pltpu.SemaphoreType.DMA((3,))])(land)
    return lax.dynamic_update_slice(out, wb[None], (_chip(), 0, 0))


def reduce_start(t, after, name):
    def body(t_ref, land_ref, after_ref, ssem, rsem, t_thru, land_thru, token):
        x, y, c = _me()
        k = 2 * x + y
        for j, (cx, cy) in enumerate(_other_chips(x, y)):
            _rcopy(t_ref.at[2 * cx + cy], land_ref.at[k], ssem.at[j], rsem.at[j], (cx, cy, c)).start()
        token[...] = jnp.zeros_like(token)

    land = lax.empty(t.shape, t.dtype)
    return pl.pallas_call(
        body, name=name,
        out_shape=(pltpu.SemaphoreType.DMA((3,)), pltpu.SemaphoreType.DMA((3,)), pltpu.HBM(t.shape, t.dtype),
                   pltpu.HBM(t.shape, t.dtype), jax.ShapeDtypeStruct((8, LANES), F32)),
        in_specs=(HBM, HBM, ANY), out_specs=(SEM, SEM, HBM, HBM, pl.BlockSpec(memory_space=pltpu.VMEM)),
        input_output_aliases={0: 2, 1: 3},
        compiler_params=pltpu.CompilerParams(has_side_effects=SPLIT_EFFECT))(
        pltpu.with_memory_space_constraint(t, pltpu.HBM), pltpu.with_memory_space_constraint(land, pltpu.HBM), after)


def bcast_start(g, after, name):
    def body(g_ref, land_ref, after_ref, ssem, rsem, g_thru, land_thru, token):
        x, y, c = _me()
        k = 2 * x + y
        for j, (cx, cy) in enumerate(_other_chips(x, y)):
            _rcopy(g_ref, land_ref.at[k], ssem.at[j], rsem.at[j], (cx, cy, c)).start()
        token[...] = jnp.zeros_like(token)

    land = lax.empty((N_CHIPS,) + g.shape, g.dtype)
    return pl.pallas_call(
        body, name=name,
        out_shape=(pltpu.SemaphoreType.DMA((3,)), pltpu.SemaphoreType.DMA((3,)), pltpu.HBM(g.shape, g.dtype),
                   pltpu.HBM(land.shape, land.dtype), jax.ShapeDtypeStruct((8, LANES), F32)),
        in_specs=(HBM, HBM, ANY), out_specs=(SEM, SEM, HBM, HBM, pl.BlockSpec(memory_space=pltpu.VMEM)),
        input_output_aliases={0: 2, 1: 3},
        compiler_params=pltpu.CompilerParams(has_side_effects=SPLIT_EFFECT))(
        pltpu.with_memory_space_constraint(g, pltpu.HBM), pltpu.with_memory_space_constraint(land, pltpu.HBM), after)


def bcast_wait(ssem, rsem, g_thru, land_thru, after, name):
    def body(g_ref, land_ref, ssem_ref, rsem_ref, after_ref, g_out, got_ref):
        x, y, c = _me()
        for j, (cx, cy) in enumerate(_other_chips(x, y)):
            cp = _rcopy(g_ref, land_ref.at[2 * cx + cy], ssem_ref.at[j], rsem_ref.at[j], (cx, cy, c))
            cp.wait_send()
            cp.wait_recv()

    g, land = pl.pallas_call(
        body, name=name, out_shape=(pltpu.HBM(g_thru.shape, g_thru.dtype), pltpu.HBM(land_thru.shape, land_thru.dtype)),
        in_specs=(HBM, HBM, SEM, SEM, ANY), out_specs=(HBM, HBM), input_output_aliases={0: 0, 1: 1},
        compiler_params=pltpu.CompilerParams(has_side_effects=SPLIT_EFFECT))(g_thru, land_thru, ssem, rsem, after)
    return lax.dynamic_update_slice(land, g[None], (_chip(), 0, 0))


def reduce_wait(ssem, rsem, t_thru, land_thru, after, name):
    def body(t_ref, land_ref, ssem_ref, rsem_ref, after_ref, t_out, got_ref):
        x, y, c = _me()
        k = 2 * x + y
        for j, (cx, cy) in enumerate(_other_chips(x, y)):
            cp = _rcopy(t_ref.at[k], land_ref.at[2 * cx + cy], ssem_ref.at[j], rsem_ref.at[j], (cx, cy, c))
            cp.wait_send()
            cp.wait_recv()

    return pl.pallas_call(
        body, name=name, out_shape=(pltpu.HBM(t_thru.shape, t_thru.dtype), pltpu.HBM(land_thru.shape, land_thru.dtype)),
        in_specs=(HBM, HBM, SEM, SEM, ANY), out_specs=(HBM, HBM), input_output_aliases={0: 0, 1: 1},
        compiler_params=pltpu.CompilerParams(has_side_effects=SPLIT_EFFECT))(t_thru, land_thru, ssem, rsem, after)


def grads_to_sibling(ps, name="grads_to_sibling"):
    n = len(ps)

    def body(*refs):
        p_refs, o_refs, ssem, rsem = refs[:n], refs[n:2 * n], refs[2 * n], refs[2 * n + 1]
        x, y, c = _me()
        cps = []
        for a in range(n):
            hf = ps[a].shape[1] // 2
            cps.append(_rcopy(p_refs[a].at[:, pl.ds((1 - c) * hf, hf), :], o_refs[a], ssem.at[a], rsem.at[a],
                              (x, y, 1 - c)))
        for cp in cps:
            cp.start()
        for cp in cps:
            cp.wait()

    return pl.pallas_call(
        body, name=name, in_specs=[ANY] * n, out_specs=[ANY] * n,
        out_shape=[jax.ShapeDtypeStruct((N_CHIPS, p.shape[1] // 2, p.shape[2]), p.dtype) for p in ps],
        scratch_shapes=[pltpu.SemaphoreType.DMA((n,)), pltpu.SemaphoreType.DMA((n,))])(*ps)


def pair_sum(p, ra, out_dtype, name):
    _, nr, w = p.shape
    hf = nr // 2
    tr = _pick_rows(hf, cap=max(512, 2 * SUM_BLOCK_BYTES // (4 * w)))
    nb = hf // tr

    def body(c_ref, p_ref, r_ref, o_ref):
        o_ref[...] = (p_ref[...] + r_ref[...]).astype(out_dtype)

    c = lax.axis_index("c").astype(jnp.int32).reshape(1)
    return pl.pallas_call(
        body, name=name,
        grid_spec=pltpu.PrefetchScalarGridSpec(
            num_scalar_prefetch=1, grid=(N_CHIPS, nb),
            in_specs=[pl.BlockSpec((1, tr, w), lambda k, i, c_ref: (k, c_ref[0] * nb + i, 0)),
                      pl.BlockSpec((1, tr, w), lambda k, i, c_ref: (k, i, 0))],
            out_specs=pl.BlockSpec((1, tr, w), lambda k, i, c_ref: (k, i, 0))),
        out_shape=jax.ShapeDtypeStruct((N_CHIPS, hf, w), out_dtype),
        compiler_params=_cparams(("parallel", "parallel")))(c, p, ra)


def grads_across_chips(ts):
    n = len(ts)

    def body(*refs):
        t_refs, o_refs, ssem, rsem = refs[:n], refs[n:2 * n], refs[2 * n], refs[2 * n + 1]
        x, y, c = _me()
        k = 2 * x + y
        chips = _other_chips(x, y)
        sends = [_rcopy(t_refs[a].at[2 * cx + cy], o_refs[a].at[k], ssem.at[3 * a + j], rsem.at[3 * a + j], (cx, cy, c))
                 for a in range(n) for j, (cx, cy) in enumerate(chips)]
        for cp in sends:
            cp.start()
        for a in range(n):
            for j, (cx, cy) in enumerate(chips):
                _rcopy(t_refs[a].at[k], o_refs[a].at[2 * cx + cy], ssem.at[3 * a + j], rsem.at[3 * a + j],
                       (cx, cy, c)).wait_recv()
        for cp in sends:
            cp.wait_send()

    return pl.pallas_call(
        body, name="grads_across_chips", in_specs=[ANY] * n, out_specs=[ANY] * n,
        out_shape=[jax.ShapeDtypeStruct(t.shape, t.dtype) for t in ts],
        scratch_shapes=[pltpu.SemaphoreType.DMA((3 * n,)), pltpu.SemaphoreType.DMA((3 * n,))])(*ts)


def chip_sum(t, rb, name):
    _, hf, w = rb.shape
    tr = _pick_rows(hf, cap=max(512, SUM_BLOCK_BYTES // (4 * w)))
    nb = hf // tr

    def body(kc_ref, t_ref, r_ref, o_ref):
        k = kc_ref[0]
        acc = jnp.where(k == 0, t_ref[0], r_ref[0]).astype(F32)
        for j in range(1, N_CHIPS):
            acc = acc + jnp.where(k == j, t_ref[0], r_ref[j]).astype(F32)
        o_ref[...] = acc

    kc = jnp.stack([_chip(), lax.axis_index("c")]).astype(jnp.int32)
    return pl.pallas_call(
        body, name=name,
        grid_spec=pltpu.PrefetchScalarGridSpec(
            num_scalar_prefetch=1, grid=(nb,),
            in_specs=[pl.BlockSpec((1, tr, w), lambda i, kc_ref: (kc_ref[0], i, 0)),
                      pl.BlockSpec((N_CHIPS, tr, w), lambda i, kc_ref: (0, i, 0))],
            out_specs=pl.BlockSpec((tr, w), lambda i, kc_ref: (kc_ref[1] * nb + i, 0))),
        out_shape=jax.ShapeDtypeStruct((2 * hf, w), F32), compiler_params=_cparams(("parallel",)))(kc, t, rb)


def reduced_to_sibling(gs):
    n = len(gs)

    def body(*refs):
        o_refs, ssem, rsem = refs[n:2 * n], refs[2 * n], refs[2 * n + 1]
        x, y, c = _me()
        cps = []
        for a in range(n):
            hf = gs[a].shape[0] // 2
            cps.append(_rcopy(_half(o_refs[a], c, hf), _half(o_refs[a], c, hf), ssem.at[a], rsem.at[a], (x, y, 1 - c)))
        for cp in cps:
            cp.start()
        for a in range(n):
            hf = gs[a].shape[0] // 2
            _rcopy(_half(o_refs[a], c, hf), _half(o_refs[a], 1 - c, hf), ssem.at[a], rsem.at[a],
                   (x, y, 1 - c)).wait_recv()
        for cp in cps:
            cp.wait_send()

    return pl.pallas_call(
        body, name="reduced_to_sibling", in_specs=[ANY] * n, out_specs=[ANY] * n,
        input_output_aliases={a: a for a in range(n)},
        out_shape=[jax.ShapeDtypeStruct(g.shape, g.dtype) for g in gs],
        scratch_shapes=[pltpu.SemaphoreType.DMA((n,)), pltpu.SemaphoreType.DMA((n,))])(*gs)


def _adamw_step(w_ref, g_ref, m_ref, v_ref, d_ref, nm_ref, nv_ref):
    bc1 = 1.0 - ADAM_B1 ** ADAM_STEP
    bc2 = 1.0 - ADAM_B2 ** ADAM_STEP
    gv = g_ref[...]
    nm = ADAM_B1 * m_ref[...] + (1.0 - ADAM_B1) * gv
    nv = ADAM_B2 * v_ref[...] + (1.0 - ADAM_B2) * (gv * gv)
    nm_ref[...] = nm
    nv_ref[...] = nv
    d_ref[...] = -ADAM_LR * ((nm / bc1) / (jnp.sqrt(nv / bc2) + ADAM_EPS) + ADAM_WD * w_ref[...])


def adamw_packed(w, g_buf, r0, m, v, name):
    r, c = w.shape
    tr = _tile_rows(r, r0, (1024, 512, 384, 256, 128))

    def body(w_ref, g_ref, m_ref, v_ref, go_ref, d_ref, nm_ref, nv_ref):
        go_ref[...] = g_ref[...]
        _adamw_step(w_ref, g_ref, m_ref, v_ref, d_ref, nm_ref, nv_ref)

    own = pl.BlockSpec((tr, CHUNK_W), lambda i, j: (i, j))
    packed = pl.BlockSpec((tr, CHUNK_W), lambda i, j: ((r0 + j * r) // tr + i, 0))
    st = jax.ShapeDtypeStruct((r, c), F32)
    return pl.pallas_call(body, name=name, grid=(r // tr, c // CHUNK_W), in_specs=[own, packed, own, own],
                          out_specs=[own] * 4, out_shape=[st] * 4,
                          compiler_params=_cparams(("parallel", "parallel")))(w, g_buf, m, v)


def adamw(w, g, m, v, name):
    r, wd = w.shape
    tr = _pick_rows(r, cap=max(16, ADAMW_BLOCK_BYTES // (4 * wd)))
    body = functools.partial(_adamw_step)

    spec = pl.BlockSpec((tr, wd), lambda i: (i, 0))
    st = jax.ShapeDtypeStruct((r, wd), F32)
    return pl.pallas_call(body, name=name, grid=(r // tr,), in_specs=[spec] * 4, out_specs=[spec] * 3,
                          out_shape=[st, st, st], compiler_params=_cparams(("parallel",)))(w, g, m, v)


LAYER_KINDS = ("gmlp", "s5", "mla", "gmlp")
PARAMS = {
    "gmlp": ("norm_g", "w_in", "ln_g", "ln_b", "w_s", "b_s", "w_out"),
    "s5": ("norm_g", "w_in", "a_re", "a_im", "log_step", "b_re", "b_im", "c_re", "c_im", "d_skip", "w_glu", "b_glu", "w_out"),
    "mla": ("norm_g", "w_in", "q_norm_g", "w_uq", "kv_norm_g", "w_ukv", "w_out"),
}
COL_SHARDED = ("w_in", "w_uq", "w_ukv")
ROW_SHARDED = ("w_out", "w_glu")
WEIGHT_NAMES = [("l%d_" % i) + n for i, kind in enumerate(LAYER_KINDS) for n in PARAMS[kind]] + ["final_norm_g"]


def _is_big(name):
    return name.split("_", 1)[1] in COL_SHARDED + ROW_SHARDED


BIG = [n for n in WEIGHT_NAMES if _is_big(n)]
SMALL = [n for n in WEIGHT_NAMES if not _is_big(n)]


def _pack_rows(blocks):
    return jnp.concatenate([b.reshape(-1, PACK_W) for b in blocks], axis=0)


def _shard_major(wn, full, width):
    r, c = full.shape
    if wn in COL_SHARDED:
        t = full.reshape(r, N_CHIPS, c // N_CHIPS).transpose(1, 0, 2)
    else:
        t = full.reshape(N_CHIPS, r // N_CHIPS, c)
    return t.reshape(N_CHIPS, -1, width)


def _from_shard_major(name, t, block_shape):
    r, c = block_shape
    if name.split("_", 1)[1] in COL_SHARDED:
        return t.reshape(N_CHIPS, r, c).transpose(1, 0, 2).reshape(r, N_CHIPS * c)
    return t.reshape(N_CHIPS * r, c)


class BigGradSink:
    ORDER = ("w_out", "w_glu", "w_ukv", "w_uq", "w_in")
    ROW_MAJOR = {2: ("w_uq", "w_in")}

    def __init__(self, layer, block_shapes):
        self.layer = layer
        self.regions = {}
        r0 = 0
        for wn in self.ORDER:
            if wn in block_shapes:
                shape = block_shapes[wn]
                self.regions[wn] = (r0, shape, wn not in self.ROW_MAJOR.get(layer, ()))
                r0 += shape[0] * shape[1] // CHUNK_W
        self.buf = lax.empty((N_CHIPS, r0, CHUNK_W), F32)
        self.flight = None

    def mm(self, wn, a, b, name):
        r0, _, direct = self.regions[wn]
        assert direct
        self.buf = matmul_tn_packed(a, b, self.buf, r0, wn in COL_SHARDED, name)

    def put(self, wn, full):
        r0, _, direct = self.regions[wn]
        assert not direct
        piece = _shard_major(wn, full, CHUNK_W)
        self.buf = lax.dynamic_update_slice(self.buf, piece, (0, r0, 0))

    def send(self):
        i = self.layer
        sib, = grads_to_sibling([self.buf], "grads_to_sibling_l%d" % i)
        t = pair_sum(self.buf, sib, BF16, "pair_sum_l%d" % i)
        self.flight = reduce_start(t, sib, "reduce_l%d_start" % i)
        return self.flight[4][0, 0]


def _small_pack(arrs, total_padded):
    flat = jnp.concatenate([a.reshape(-1) for a in arrs])
    return jnp.pad(flat, (0, total_padded - flat.shape[0]))


def kernel(x, positions, l0_norm_g, l0_w_in, l0_ln_g, l0_ln_b, l0_w_s, l0_b_s, l0_w_out, l1_norm_g, l1_w_in, l1_a_re, l1_a_im, l1_log_step, l1_b_re, l1_b_im, l1_c_re, l1_c_im, l1_d_skip, l1_w_glu, l1_b_glu, l1_w_out, l2_norm_g, l2_w_in, l2_q_norm_g, l2_w_uq, l2_kv_norm_g, l2_w_ukv, l2_w_out, l3_norm_g, l3_w_in, l3_ln_g, l3_ln_b, l3_w_s, l3_b_s, l3_w_out, final_norm_g, loss_target, m_l0_norm_g, m_l0_w_in, m_l0_ln_g, m_l0_ln_b, m_l0_w_s, m_l0_b_s, m_l0_w_out, m_l1_norm_g, m_l1_w_in, m_l1_a_re, m_l1_a_im, m_l1_log_step, m_l1_b_re, m_l1_b_im, m_l1_c_re, m_l1_c_im, m_l1_d_skip, m_l1_w_glu, m_l1_b_glu, m_l1_w_out, m_l2_norm_g, m_l2_w_in, m_l2_q_norm_g, m_l2_w_uq, m_l2_kv_norm_g, m_l2_w_ukv, m_l2_w_out, m_l3_norm_g, m_l3_w_in, m_l3_ln_g, m_l3_ln_b, m_l3_w_s, m_l3_b_s, m_l3_w_out, m_final_norm_g, v_l0_norm_g, v_l0_w_in, v_l0_ln_g, v_l0_ln_b, v_l0_w_s, v_l0_b_s, v_l0_w_out, v_l1_norm_g, v_l1_w_in, v_l1_a_re, v_l1_a_im, v_l1_log_step, v_l1_b_re, v_l1_b_im, v_l1_c_re, v_l1_c_im, v_l1_d_skip, v_l1_w_glu, v_l1_b_glu, v_l1_w_out, v_l2_norm_g, v_l2_w_in, v_l2_q_norm_g, v_l2_w_uq, v_l2_kv_norm_g, v_l2_w_ukv, v_l2_w_out, v_l3_norm_g, v_l3_w_in, v_l3_ln_g, v_l3_ln_b, v_l3_w_s, v_l3_b_s, v_l3_w_out, v_final_norm_g):
    args = locals()
    w = {n: args[n] for n in WEIGHT_NAMES}
    mom_m = {n: args["m_" + n] for n in WEIGHT_NAMES}
    mom_v = {n: args["v_" + n] for n in WEIGHT_NAMES}
    h0 = x[0]
    target = loss_target[0]
    pos = positions.reshape(-1, 1)

    full = {}

    def pack_unit(layers):
        names = [n for n in BIG if int(n[1]) in layers]
        rows = [w[n].size // PACK_W for n in names]
        pad = -sum(rows) % PACK_ROW_ALIGN
        return names, rows, _pack_rows([w[n].astype(BF16) for n in names] + [jnp.zeros((pad, PACK_W), BF16)])

    def unpack_unit(names, rows, gathered):
        r0 = 0
        for n, nr in zip(names, rows):
            full[n] = _from_shard_major(n, gathered[:, r0:r0 + nr, :], w[n].shape)
            r0 += nr

    unit0, unit1, unit2 = pack_unit((0,)), pack_unit((1,)), pack_unit((2, 3))
    wp = dict(w)

    def layer_params(i):
        pre = "l%d_" % i
        p = {k[len(pre):]: v for k, v in wp.items() if k.startswith(pre)}
        wf = {k[len(pre):]: v for k, v in full.items() if k.startswith(pre)}
        return p, wf

    flight = gather_start(unit0[2], unit1[2], "gather_l0_start")
    cos, sins = rope_tables(pos, flight[4][0, 0])
    wp["l1_a_re"] = w["l1_a_re"] + flight[4][0, 0]
    s5_weights = _s5_weights(layer_params(1)[0])
    land = gather_wait(*flight[:4], s5_weights[2], "gather_l0_wait")
    got = gather_handover(land, unit0[2], "gather_l0_handover")
    unpack_unit(unit0[0], unit0[1], got)
    flight = gather_start(unit1[2], got, "gather_l1_start")
    wp["l0_norm_g"] = w["l0_norm_g"] + flight[4][0, 0]

    h = h0
    saved = []
    for i, kind in enumerate(LAYER_KINDS):
        if i == 1:
            land = gather_wait(*flight[:4], h, "gather_l1_wait")
            got = gather_handover(land, unit1[2], "gather_l1_handover")
            unpack_unit(unit1[0], unit1[1], got)
            flight = gather_start(unit2[2], got, "gather_l23_start")
            wp["l1_norm_g"] = w["l1_norm_g"] + flight[4][0, 0]
        if i == 2:
            land = gather_wait(*flight[:4], h, "gather_l23_wait")
            unpack_unit(unit2[0], unit2[1], gather_handover(land, unit2[2], "gather_l23_handover"))
        p, wf = layer_params(i)
        tag = "l%d" % i
        if kind == "gmlp":
            h, s = gmlp_layer_fwd(h, p, wf, tag)
        elif kind == "s5":
            h, s = s5_layer_fwd(h, p, wf, s5_weights, tag)
        else:
            h, s = mla_layer_fwd(h, p, wf, cos, sins, tag)
        saved.append(s)
    loss_part, dh, g_final = loss_head(h, final_norm_g, target)

    grads = {"final_norm_g": g_final.reshape(-1)}
    sinks = {}

    for i in reversed(range(len(LAYER_KINDS))):
        kind = LAYER_KINDS[i]
        p, wf = layer_params(i)
        tag = "l%d" % i
        sink = sinks[i] = BigGradSink(i, {n[3:]: w[n].shape for n in BIG if int(n[1]) == i})
        if kind == "gmlp":
            dh, g = gmlp_layer_bwd(dh, saved[i], p, wf, tag, sink)
        elif kind == "s5":
            dh, g = s5_layer_bwd(dh, saved[i], p, wf, tag, sink)
        else:
            dh, g = mla_layer_bwd(dh, saved[i], p, wf, cos, sins, tag, sink)
        for k, val in g.items():
            grads["l%d_%s" % (i, k)] = val
    grad_x = dh[None]

    n_small = sum(w[n].size for n in SMALL)
    piece = N_CHIPS * 2 * 16 * PACK_W
    n_small_pad = -(-(n_small + 1) // piece) * piece
    nrs = n_small_pad // N_CHIPS // PACK_W
    p_small = _small_pack([grads[n] for n in SMALL] + [loss_part], n_small_pad).reshape(N_CHIPS, nrs, PACK_W)
    sib_small, = grads_to_sibling([p_small], "grads_to_sibling_small")
    t_small = pair_sum(p_small, sib_small, F32, "pair_sum_small")
    rb_small, = grads_across_chips([t_small])
    halves = [chip_sum(t_small, rb_small, "chip_sum_small")]

    after = halves[0]
    for i in reversed(range(len(LAYER_KINDS))):
        t_i, rb_i = reduce_wait(*sinks[i].flight[:4], after, "reduce_l%d_wait" % i)
        halves.append(chip_sum(t_i, rb_i, "chip_sum_l%d" % i))
        after = halves[-1]
    reduced = reduced_to_sibling(halves)
    small_flight = bcast_start(reduced[0], reduced[1], "small_allgather_start")

    g_out, d_out, nm_out, nv_out = {}, {}, {}, {}
    for i, g_i in zip(reversed(range(len(LAYER_KINDS))), reduced[1:]):
        for wn, (r0, shape, direct) in sinks[i].regions.items():
            n = "l%d_%s" % (i, wn)
            if direct:
                g_out[n], d_out[n], nm_out[n], nv_out[n] = adamw_packed(w[n], g_i, r0, mom_m[n], mom_v[n], "adamw_" + n)
            else:
                g_out[n] = g_i[r0:r0 + shape[0] * shape[1] // CHUNK_W].reshape(shape)
                d_out[n], nm_out[n], nv_out[n] = adamw(w[n], g_out[n], mom_m[n], mom_v[n], "adamw_" + n)
    small_all = bcast_wait(*small_flight[:4], nv_out["l0_w_in"], "small_allgather_wait")
    g_small = small_all.reshape(-1, PACK_W)
    sp = lambda d: _small_pack([d[n] for n in SMALL], n_small_pad).reshape(-1, PACK_W)
    d_small, nm_small, nv_small = adamw(sp(w), g_small, sp(mom_m), sp(mom_v), "adamw_small")
    for buf, out in ((g_small, g_out), (d_small, d_out), (nm_small, nm_out), (nv_small, nv_out)):
        flat = buf.reshape(-1)
        o = 0
        for n in SMALL:
            out[n] = flat[o:o + w[n].size].reshape(w[n].shape)
            o += w[n].size
    loss = g_small.reshape(-1)[n_small]
    return (loss, grad_x, *[g_out[n] for n in WEIGHT_NAMES], *[d_out[n] for n in WEIGHT_NAMES],
            *[nm_out[n] for n in WEIGHT_NAMES], *[nv_out[n] for n in WEIGHT_NAMES])
```

```python
import functools
import math

import jax
import jax.numpy as jnp
import numpy as np
from jax import lax
from jax.experimental import pallas as pl
from jax.experimental.pallas import tpu as pltpu

F32 = jnp.float32
BF16 = jnp.bfloat16
MESH = pl.DeviceIdType.MESH
VMEM_LIMIT_BYTES = 56 * 1024 * 1024
LANES = 128
PACK_W = 1024
CHUNK_W = 256
PACK_ROW_ALIGN = 256
ROW_TILE = 512
ROW_TILE_HEAVY = 256
ROW_TILE_NARROW = 512
SUM_BLOCK_BYTES = 2 * 1024 * 1024
ADAMW_BLOCK_BYTES = 1024 * 1024
MM_BLOCK_BYTES = 12 * 1024 * 1024

NORM_EPS = 1e-6
N_CHIPS = 4
GMLP_CHUNK = 128
GMLP_GROUPS = 8
S5_GROUPS = 128
S5_GROUP = 16
S5_STATE = 64
S5_SB = 16
S5_SEG = 8
MLA_HEADS = 16
MLA_NOPE = 128
MLA_ROPE = 64
MLA_Q_RANK = 384
MLA_KV_RANK = 128
MLA_SCALE = (MLA_NOPE + MLA_ROPE) ** -0.5
ROPE_THETA = 10000.0
NEG_INF = -1e30
ADAM_LR, ADAM_B1, ADAM_B2, ADAM_EPS, ADAM_WD, ADAM_STEP = 0.001, 0.9, 0.999, 1e-08, 0.01, 10

DN_NN = (((1,), (0,)), ((), ()))
DN_NT = (((1,), (1,)), ((), ()))
DN_TN = (((0,), (0,)), ((), ()))


def _cparams(sem):
    return pltpu.CompilerParams(dimension_semantics=sem, vmem_limit_bytes=VMEM_LIMIT_BYTES)


def _pick(n, cands=(512, 384, 256, 128)):
    for c in cands:
        if n % c == 0:
            return c
    return n


def _pick_rows(r, cap=512, mult=16):
    return max(t for t in range(mult, cap + 1, mult) if r % t == 0)


def _dot(a, b, dn):
    return lax.dot_general(a.astype(BF16), b.astype(BF16), dn, preferred_element_type=F32)


def _sigmoid(x):
    return 0.5 + 0.5 * jnp.tanh(0.5 * x)


def _gelu(x):
    c = math.sqrt(2.0 / math.pi)
    t = jnp.tanh(c * (x + 0.044715 * x * x * x))
    return 0.5 * x * (1.0 + t)


def _gelu_grad(x):
    c = math.sqrt(2.0 / math.pi)
    t = jnp.tanh(c * (x + 0.044715 * x * x * x))
    return 0.5 * (1.0 + t) + 0.5 * x * (1.0 - t * t) * c * (1.0 + 3.0 * 0.044715 * x * x)


def _gelu_both(x):
    c = math.sqrt(2.0 / math.pi)
    t = jnp.tanh(c * (x + 0.044715 * x * x * x))
    return 0.5 * x * (1.0 + t), 0.5 * (1.0 + t) + 0.5 * x * (1.0 - t * t) * c * (1.0 + 3.0 * 0.044715 * x * x)


def _silu_both(z):
    s = _sigmoid(z)
    return z * s, s * (1.0 + z * (1.0 - s))


def _silu(z):
    return z * _sigmoid(z)


def matmul(a, b, mode, name, out_dtype=F32, add=None):
    if mode == "nn":
        (m, k), n = a.shape, b.shape[1]
    elif mode == "nt":
        (m, k), n = a.shape, b.shape[0]
    else:
        (k, m), n = a.shape, b.shape[1]
    tm = _pick(m, [t for t in (2048, 1024, 512, 384, 256, 128) if t * k * a.dtype.itemsize <= MM_BLOCK_BYTES])
    tn = _pick(n, [t for t in (512, 384, 256, 128) if t * k * b.dtype.itemsize <= MM_BLOCK_BYTES])
    dn = {"nn": DN_NN, "nt": DN_NT, "tn": DN_TN}[mode]

    def body(*refs):
        if add is None:
            a_ref, b_ref, o_ref = refs
        else:
            a_ref, b_ref, add_ref, o_ref = refs
        r = _dot(a_ref[...], b_ref[...], dn)
        if add is not None:
            r = r + add_ref[...].astype(F32)
        o_ref[...] = r.astype(out_dtype)

    a_spec = pl.BlockSpec((k, tm), lambda i, j: (0, i)) if mode == "tn" else pl.BlockSpec((tm, k), lambda i, j: (i, 0))
    b_spec = pl.BlockSpec((tn, k), lambda i, j: (j, 0)) if mode == "nt" else pl.BlockSpec((k, tn), lambda i, j: (0, j))
    o_spec = pl.BlockSpec((tm, tn), lambda i, j: (i, j))
    in_specs = [a_spec, b_spec] + ([o_spec] if add is not None else [])
    args = (a, b) + ((add,) if add is not None else ())
    return pl.pallas_call(
        body, name=name, grid=(m // tm, n // tn), in_specs=in_specs, out_specs=o_spec,
        out_shape=jax.ShapeDtypeStruct((m, n), out_dtype),
        compiler_params=_cparams(("parallel", "arbitrary")))(*args)


def _tile_rows(r, r0, cands=(512, 384, 256, 128)):
    return next(t for t in cands if r % t == 0 and r0 % t == 0)


def matmul_tn_packed(a, b, buf, r0, col_sharded, name):
    k, m = a.shape
    n = b.shape[1]
    if col_sharded:
        chunks = n // N_CHIPS // CHUNK_W
        tm = _tile_rows(m, r0, (1024, 512, 384, 256, 128))
        o_map = lambda i, j: (j // chunks, (r0 + (j % chunks) * m) // tm + i, 0)
    else:
        rs = m // N_CHIPS
        tm = _tile_rows(rs, r0)
        per = rs // tm
        o_map = lambda i, j: (i // per, (r0 + j * rs) // tm + i % per, 0)

    def body(a_ref, b_ref, buf_ref, o_ref):
        o_ref[0] = _dot(a_ref[...], b_ref[...], DN_TN)

    return pl.pallas_call(
        body, name=name, grid=(m // tm, n // CHUNK_W),
        in_specs=[pl.BlockSpec((k, tm), lambda i, j: (0, i)), pl.BlockSpec((k, CHUNK_W), lambda i, j: (0, j)),
                  pl.BlockSpec(memory_space=pl.ANY)],
        out_specs=pl.BlockSpec((1, tm, CHUNK_W), o_map), out_shape=jax.ShapeDtypeStruct(buf.shape, buf.dtype),
        input_output_aliases={2: 0}, compiler_params=_cparams(("parallel", "arbitrary")))(a, b, buf)


def _rows(tl, w, col=0):
    return pl.BlockSpec((tl, w), lambda i: (i, col))


def _full(shape):
    nd = len(shape)
    return pl.BlockSpec(tuple(shape), lambda i: (0,) * nd)


def _rowcall(body, name, n_steps, in_specs, out_specs, out_shape, scratch=()):
    return pl.pallas_call(
        body, name=name, grid=(n_steps,), in_specs=in_specs, out_specs=out_specs, out_shape=out_shape,
        scratch_shapes=list(scratch), compiler_params=_cparams(("arbitrary",)))


def _acc(ref, val, i):
    @pl.when(i == 0)
    def _():
        ref[...] = val

    @pl.when(i != 0)
    def _():
        ref[...] += val


def rms_fwd(h, g, name):
    l, d = h.shape
    tl = ROW_TILE_NARROW

    def body(h_ref, g_ref, o_ref):
        x = h_ref[...]
        r = lax.rsqrt(jnp.mean(x * x, axis=-1, keepdims=True) + NORM_EPS)
        o_ref[...] = (x * r * g_ref[...]).astype(BF16)

    return _rowcall(body, name, l // tl, [_rows(tl, d), _full((1, d))], _rows(tl, d),
                    jax.ShapeDtypeStruct((l, d), BF16))(h, g.reshape(1, d))


def rms_bwd(h, g, dhn, dh_in, name):
    l, d = h.shape
    tl = ROW_TILE_NARROW

    def body(h_ref, g_ref, dhn_ref, dhi_ref, dh_ref, dg_ref):
        i = pl.program_id(0)
        x = h_ref[...]
        r = lax.rsqrt(jnp.mean(x * x, axis=-1, keepdims=True) + NORM_EPS)
        xhat = x * r
        dy = dhn_ref[...]
        dxh = dy * g_ref[...]
        dx = r * (dxh - xhat * jnp.mean(dxh * xhat, axis=-1, keepdims=True))
        dh_ref[...] = dhi_ref[...] + dx
        _acc(dg_ref, jnp.sum(dy * xhat, axis=0, keepdims=True), i)

    return _rowcall(body, name, l // tl, [_rows(tl, d), _full((1, d)), _rows(tl, d), _rows(tl, d)],
                    [_rows(tl, d), _full((1, d))],
                    [jax.ShapeDtypeStruct((l, d), F32), jax.ShapeDtypeStruct((1, d), F32)])(h, g.reshape(1, d), dhn, dh_in)


def loss_head(h, g, target):
    l, d = h.shape
    tl = ROW_TILE_NARROW

    def body(h_ref, g_ref, t_ref, loss_ref, dh_ref, dg_ref):
        i = pl.program_id(0)
        x = h_ref[...]
        gg = g_ref[...]
        r = lax.rsqrt(jnp.mean(x * x, axis=-1, keepdims=True) + NORM_EPS)
        xhat = x * r
        err = xhat * gg - t_ref[...]
        part = 0.5 * jnp.sum(jnp.mean(err * err, axis=-1, keepdims=True), axis=0, keepdims=True)
        _acc(loss_ref, part, i)
        dy = err * (1.0 / d)
        dxh = dy * gg
        dh_ref[...] = r * (dxh - xhat * jnp.mean(dxh * xhat, axis=-1, keepdims=True))
        _acc(dg_ref, jnp.sum(dy * xhat, axis=0, keepdims=True), i)

    return _rowcall(body, "loss_head", l // tl, [_rows(tl, d), _full((1, d)), _rows(tl, d)],
                    [_full((1, 1)), _rows(tl, d), _full((1, d))],
                    [jax.ShapeDtypeStruct((1, 1), F32), jax.ShapeDtypeStruct((l, d), F32),
                     jax.ShapeDtypeStruct((1, d), F32)])(h, g.reshape(1, d), target)


def _gmlp_common(a_ref, lng_ref, lnb_ref):
    di = lng_ref.shape[1]
    u_pre = a_ref[:, 0:di]
    v_pre = a_ref[:, di:2 * di]
    z = a_ref[:, 2 * di:3 * di]
    vg = _gelu(v_pre)
    mu = jnp.mean(vg, axis=-1, keepdims=True)
    xc = vg - mu
    rstd = lax.rsqrt(jnp.mean(xc * xc, axis=-1, keepdims=True) + NORM_EPS)
    vhat = xc * rstd
    vn = vhat * lng_ref[...] + lnb_ref[...]
    return u_pre, v_pre, z, vhat, rstd, vn


def _tril(w):
    r = lax.broadcasted_iota(jnp.int32, w.shape, 0)
    c = lax.broadcasted_iota(jnp.int32, w.shape, 1)
    return jnp.where(c <= r, w, 0.0)


def gmlp_gate_fwd(a, ln_g, ln_b, w_s, b_s, name):
    l, w3 = a.shape
    di = w3 // 3
    dg = di // GMLP_GROUPS
    tl = GMLP_CHUNK

    def body(a_ref, lng_ref, lnb_ref, ws_ref, bs_ref, m_ref):
        u_pre, _, z, _, _, vn = _gmlp_common(a_ref, lng_ref, lnb_ref)
        gate = _gelu(u_pre) * _silu(z)
        for g in range(GMLP_GROUPS):
            sl = slice(g * dg, (g + 1) * dg)
            s = _dot(_tril(ws_ref[g]), vn[:, sl], DN_NN) + bs_ref[g]
            m_ref[:, sl] = (gate[:, sl] * s).astype(BF16)

    return _rowcall(body, name, l // tl,
                    [_rows(tl, w3), _full((1, di)), _full((1, di)), _full(w_s.shape), _full((GMLP_GROUPS, tl, 1))],
                    _rows(tl, di), jax.ShapeDtypeStruct((l, di), BF16))(
        a, ln_g.reshape(1, di), ln_b.reshape(1, di), w_s, b_s.reshape(GMLP_GROUPS, tl, 1))


def gmlp_gate_bwd(a, dm, ln_g, ln_b, w_s, b_s, name):
    l, w3 = a.shape
    di = w3 // 3
    dg = di // GMLP_GROUPS
    tl = GMLP_CHUNK

    def body(a_ref, dm_ref, lng_ref, lnb_ref, ws_ref, bs_ref, da_ref, dlg_ref, dlb_ref, dws_ref, dbs_ref,
             dvn_ref, vh_ref, gv_ref):
        i = pl.program_id(0)
        vg, gv = _gelu_both(a_ref[:, di:2 * di])
        gv_ref[...] = gv
        xc = vg - jnp.mean(vg, axis=-1, keepdims=True)
        rstd = lax.rsqrt(jnp.mean(xc * xc, axis=-1, keepdims=True) + NORM_EPS)
        vh_ref[...] = xc * rstd
        for g in range(GMLP_GROUPS):
            sl = slice(g * dg, (g + 1) * dg)
            wt = _tril(ws_ref[g])
            vn_g = vh_ref[:, sl] * lng_ref[:, sl] + lnb_ref[:, sl]
            s = _dot(wt, vn_g, DN_NN) + bs_ref[g]
            dmg = dm_ref[:, sl]
            u, gu = _gelu_both(a_ref[:, sl])
            sz, gz = _silu_both(a_ref[:, 2 * di + g * dg:2 * di + (g + 1) * dg])
            ds = dmg * u * sz
            da_ref[:, sl] = (dmg * s * sz * gu).astype(BF16)
            da_ref[:, 2 * di + g * dg:2 * di + (g + 1) * dg] = (dmg * u * s * gz).astype(BF16)
            dvn_ref[:, sl] = _dot(wt, ds, DN_TN)
            dw = _tril(_dot(ds, vn_g, DN_NT))
            db = jnp.sum(ds, axis=1, keepdims=True)

            @pl.when(i == 0)
            def _():
                dws_ref[g] = dw
                dbs_ref[g] = db

            @pl.when(i != 0)
            def _():
                dws_ref[g] += dw
                dbs_ref[g] += db

        dvn = dvn_ref[...]
        vhat = vh_ref[...]
        dxh = dvn * lng_ref[...]
        dvg = rstd * (dxh - jnp.mean(dxh, axis=-1, keepdims=True) - vhat * jnp.mean(dxh * vhat, axis=-1, keepdims=True))
        da_ref[:, di:2 * di] = (dvg * gv_ref[...]).astype(BF16)
        _acc(dlg_ref, jnp.sum(dvn * vhat, axis=0, keepdims=True), i)
        _acc(dlb_ref, jnp.sum(dvn, axis=0, keepdims=True), i)

    outs = _rowcall(
        body, name, l // tl,
        [_rows(tl, w3), _rows(tl, di), _full((1, di)), _full((1, di)), _full(w_s.shape), _full((GMLP_GROUPS, tl, 1))],
        [_rows(tl, w3), _full((1, di)), _full((1, di)), _full(w_s.shape), _full((GMLP_GROUPS, tl, 1))],
        [jax.ShapeDtypeStruct((l, w3), BF16), jax.ShapeDtypeStruct((1, di), F32), jax.ShapeDtypeStruct((1, di), F32),
         jax.ShapeDtypeStruct(w_s.shape, F32), jax.ShapeDtypeStruct((GMLP_GROUPS, tl, 1), F32)],
        scratch=[pltpu.VMEM((tl, di), F32)] * 3)(
        a, dm, ln_g.reshape(1, di), ln_b.reshape(1, di), w_s, b_s.reshape(GMLP_GROUPS, tl, 1))
    return outs


def gmlp_layer_fwd(h, p, wf, tag):
    hn = rms_fwd(h, p["norm_g"], tag + "_rms")
    a = matmul(hn, wf["w_in"], "nn", tag + "_mm_in")
    m = gmlp_gate_fwd(a, p["ln_g"], p["ln_b"], p["w_s"], p["b_s"], tag + "_gate")
    h_out = matmul(m, wf["w_out"], "nn", tag + "_mm_out", add=h)
    return h_out, (h, hn, a, m)


def gmlp_layer_bwd(dh_out, saved, p, wf, tag, sink):
    h, hn, a, m = saved
    dm = matmul(dh_out, wf["w_out"], "nt", tag + "_mm_dm")
    sink.mm("w_out", m, dh_out, tag + "_mm_gwout")
    da, dlg, dlb, dws, dbs = gmlp_gate_bwd(a, dm, p["ln_g"], p["ln_b"], p["w_s"], p["b_s"], tag + "_gate_bwd")
    dhn = matmul(da, wf["w_in"], "nt", tag + "_mm_dhn")
    sink.mm("w_in", hn, da, tag + "_mm_gwin")
    zero = sink.send()
    dh, dng = rms_bwd(h, p["norm_g"] + zero, dhn, dh_out, tag + "_rms_bwd")
    grads = {"norm_g": dng.reshape(-1), "ln_g": dlg.reshape(-1), "ln_b": dlb.reshape(-1),
             "w_s": dws, "b_s": dbs.reshape(GMLP_GROUPS, GMLP_CHUNK)}
    return dh, grads


def _cmul(ar, ai, br, bi):
    return ar * br - ai * bi, ar * bi + ai * br


S5_PG = 16


def _gblock(tail):
    return pl.BlockSpec((S5_PG,) + tuple(tail), lambda i: (i, 0, 0))


def s5_params_fwd(a_re, a_im, log_step, b_re, b_im):
    g, p, hh = b_re.shape

    def body(ar_ref, ai_ref, ls_ref, br_ref, bi_ref, lr_ref, li_ref, bbr_ref, bbi_ref):
        ar, ai = ar_ref[...], ai_ref[...]
        step = jnp.exp(ls_ref[...])
        mag = jnp.exp(ar * step)
        lr, li = mag * jnp.cos(ai * step), mag * jnp.sin(ai * step)
        den = 1.0 / (ar * ar + ai * ai)
        fr, fi = _cmul(lr - 1.0, li, ar * den, -ai * den)
        lr_ref[...] = lr
        li_ref[...] = li
        bbr, bbi = _cmul(fr, fi, br_ref[...], bi_ref[...])
        bbr_ref[...] = bbr
        bbi_ref[...] = bbi

    s1 = jax.ShapeDtypeStruct((g, p, 1), F32)
    s3 = jax.ShapeDtypeStruct((g, p, hh), F32)
    b1, b0, b3 = _gblock((p, 1)), _gblock((1, 1)), _gblock((p, hh))
    return pl.pallas_call(body, name="s5_params_fwd", grid=(g // S5_PG,), in_specs=[b1, b1, b0, b3, b3],
                          out_specs=[b1, b1, b3, b3], out_shape=[s1, s1, s3, s3],
                          compiler_params=_cparams(("parallel",)))(
        a_re.reshape(g, p, 1), a_im.reshape(g, p, 1), log_step.reshape(g, 1, 1), b_re, b_im)


def s5_params_bwd(a_re, a_im, log_step, b_re, b_im, dl_re, dl_im, dbb_re, dbb_im):
    g, p, hh = b_re.shape

    def body(ar_ref, ai_ref, ls_ref, br_ref, bi_ref, dlr_ref, dli_ref, dbr_ref, dbi_ref,
             gar_ref, gai_ref, gls_ref, gbr_ref, gbi_ref):
        ar, ai = ar_ref[...], ai_ref[...]
        step = jnp.exp(ls_ref[...])
        mag = jnp.exp(ar * step)
        lr, li = mag * jnp.cos(ai * step), mag * jnp.sin(ai * step)
        den = 1.0 / (ar * ar + ai * ai)
        ir, ii = ar * den, -ai * den
        fr, fi = _cmul(lr - 1.0, li, ir, ii)
        br, bi = br_ref[...], bi_ref[...]
        dbr, dbi = dbr_ref[...], dbi_ref[...]
        gbr, gbi = _cmul(fr, -fi, dbr, dbi)
        gbr_ref[...] = gbr
        gbi_ref[...] = gbi
        pr, pi = _cmul(br, -bi, dbr, dbi)
        gfr = jnp.sum(pr, axis=-1, keepdims=True)
        gfi = jnp.sum(pi, axis=-1, keepdims=True)
        t_r, t_i = _cmul(ir, -ii, gfr, gfi)
        glr, gli = dlr_ref[...] + t_r, dli_ref[...] + t_i
        c1r, c1i = _cmul(step * lr, -step * li, glr, gli)
        qr, qi = _cmul(fr, fi, ir, ii)
        c2r, c2i = _cmul(-qr, qi, gfr, gfi)
        gar_ref[...] = c1r + c2r
        gai_ref[...] = c1i + c2i
        wr, wi = _cmul(ar, ai, lr, li)
        sr, _ = _cmul(wr, -wi, glr, gli)
        gls_ref[...] = jnp.sum(sr, axis=1, keepdims=True) * step

    s1 = jax.ShapeDtypeStruct((g, p, 1), F32)
    s3 = jax.ShapeDtypeStruct((g, p, hh), F32)
    b1, b0, b3 = _gblock((p, 1)), _gblock((1, 1)), _gblock((p, hh))
    return pl.pallas_call(body, name="s5_params_bwd", grid=(g // S5_PG,),
                          in_specs=[b1, b1, b0, b3, b3, b1, b1, b3, b3], out_specs=[b1, b1, b0, b3, b3],
                          out_shape=[s1, s1, jax.ShapeDtypeStruct((g, 1, 1), F32), s3, s3],
                          compiler_params=_cparams(("parallel",)))(
        a_re.reshape(g, p, 1), a_im.reshape(g, p, 1), log_step.reshape(g, 1, 1), b_re, b_im,
        dl_re, dl_im, dbb_re, dbb_im)


def _blockdiag(t):
    sb, n, r, c = t.shape
    eye = jnp.eye(n, dtype=bool)[None, :, None, :, None]
    full = jnp.where(eye, t[:, :, :, None, :], jnp.zeros((), t.dtype))
    return full.reshape(sb, n * r, n * c)


def _blockdiag_extract(m, r, c):
    sb = m.shape[0]
    n = m.shape[1] // r
    m5 = m.reshape(sb, n, r, n, c)
    return jnp.stack([m5[:, i, :, i, :] for i in range(n)], axis=1)


S5_TB = 256
S5_UNROLL = 8


def _lam_power(pr, pi, n):
    for _ in range(int(math.log2(n))):
        pr, pi = _cmul(pr, pi, pr, pi)
    return pr, pi


def _segment_entries(er, ei, pr, pi, reverse):
    seg, ns = er.shape
    row = lax.broadcasted_iota(jnp.int32, (seg, ns), 0)
    cr = jnp.zeros((seg, ns), F32)
    ci = jnp.zeros((seg, ns), F32)
    cur_r = jnp.zeros((1, ns), F32)
    cur_i = jnp.zeros((1, ns), F32)
    for s in (range(seg - 2, -1, -1) if reverse else range(1, seg)):
        src = s + 1 if reverse else s - 1
        mr, mi = _cmul(pr, pi, cur_r, cur_i)
        cur_r = jnp.sum(jnp.where(row == src, er, 0.0), axis=0, keepdims=True) + mr
        cur_i = jnp.sum(jnp.where(row == src, ei, 0.0), axis=0, keepdims=True) + mi
        cr = jnp.where(row == s, cur_r, cr)
        ci = jnp.where(row == s, cur_i, ci)
    return cr, ci


def s5_scan_fused_fwd(a_p, lam_re, lam_im, wb_re, wb_im, wc_re, wc_im, d_skip, name):
    l = a_p.shape[0]
    di = d_skip.shape[1]
    rows = S5_SEG * S5_TB
    nb = l // rows
    ns = wb_re.shape[2]

    def body(u_ref, lr_ref, li_ref, wbr_ref, wbi_ref, wcr_ref, wci_ref, ds_ref, y_ref, yg_ref, ckr_ref, cki_ref,
             bur, bui):
        lr = jnp.broadcast_to(lr_ref[0], (S5_SEG, ns))
        li = jnp.broadcast_to(li_ref[0], (S5_SEG, ns))

        def scan_block(b, carry, keep):
            def step(t, c):
                xr, xi = c
                sl = pl.ds(pl.multiple_of(b * rows + t * S5_SEG, S5_SEG), S5_SEG)
                nr = lr * xr - li * xi + bur[sl, :]
                ni = lr * xi + li * xr + bui[sl, :]
                if keep:
                    bur[sl, :] = nr
                    bui[sl, :] = ni
                return nr, ni

            return lax.fori_loop(0, S5_TB, step, carry, unroll=S5_UNROLL)

        def project(b, carry):
            rs = pl.ds(pl.multiple_of(b * rows, rows), rows)
            u = u_ref[rs, :]
            bur[rs, :] = _dot(u, wbr_ref[0], DN_NN)
            bui[rs, :] = _dot(u, wbi_ref[0], DN_NN)
            return scan_block(b, carry, False)

        zero = jnp.zeros((S5_SEG, ns), F32)
        er, ei = lax.fori_loop(0, nb, project, (zero, zero))
        pr, pi = _lam_power(lr_ref[0], li_ref[0], l // S5_SEG)
        entry = _segment_entries(er, ei, pr, pi, False)

        def emit(b, carry):
            ckr_ref[0, b] = carry[0]
            cki_ref[0, b] = carry[1]
            carry = scan_block(b, carry, True)
            rs = pl.ds(pl.multiple_of(b * rows, rows), rows)
            y = (_dot(bur[rs, :], wcr_ref[0], DN_NN) - _dot(bui[rs, :], wci_ref[0], DN_NN)
                 + ds_ref[...] * u_ref[rs, :])
            y_ref[rs, :] = y
            yg_ref[rs, :] = _gelu(y).astype(BF16)
            return carry

        lax.fori_loop(0, nb, emit, entry)

    sb3 = lambda s: (s, 0, 0)
    st = jax.ShapeDtypeStruct
    return pl.pallas_call(
        body, name=name, grid=(S5_SB,),
        in_specs=[pl.BlockSpec((l, LANES), lambda s: (0, s)),
                  pl.BlockSpec((1, 1, ns), sb3), pl.BlockSpec((1, 1, ns), sb3),
                  pl.BlockSpec((1, LANES, ns), sb3), pl.BlockSpec((1, LANES, ns), sb3),
                  pl.BlockSpec((1, ns, LANES), sb3), pl.BlockSpec((1, ns, LANES), sb3),
                  pl.BlockSpec((1, LANES), lambda s: (0, s))],
        out_specs=[pl.BlockSpec((l, LANES), lambda s: (0, s)), pl.BlockSpec((l, LANES), lambda s: (0, s)),
                   pl.BlockSpec((1, nb, S5_SEG, ns), lambda s: (s, 0, 0, 0)),
                   pl.BlockSpec((1, nb, S5_SEG, ns), lambda s: (s, 0, 0, 0))],
        out_shape=[st((l, di), F32), st((l, di), BF16),
                   st((S5_SB, nb, S5_SEG, ns), F32), st((S5_SB, nb, S5_SEG, ns), F32)],
        scratch_shapes=[pltpu.VMEM((l, ns), F32), pltpu.VMEM((l, ns), F32)],
        compiler_params=_cparams(("parallel",)))(a_p, lam_re, lam_im, wb_re, wb_im, wc_re, wc_im, d_skip)


def s5_scan_fused_bwd(a_p, dy, lam_re, lam_im, wb_re, wb_im, wc_re, wc_im, d_skip, ck_re, ck_im, name):
    l = a_p.shape[0]
    di = d_skip.shape[1]
    rows = S5_SEG * S5_TB
    nb = l // rows
    ns = wb_re.shape[2]

    def body(u_ref, dy_ref, lr_ref, li_ref, wbr_ref, wbi_ref, wcr_ref, wci_ref, ds_ref, ckr_ref, cki_ref,
             du_ref, dwbr_ref, dwbi_ref, dwcr_ref, dwci_ref, dds_ref, dlr_ref, dli_ref, gr, gi, xr_b, xi_b):
        lr = jnp.broadcast_to(lr_ref[0], (S5_SEG, ns))
        li = jnp.broadcast_to(li_ref[0], (S5_SEG, ns))

        def back_project(k, carry):
            b = nb - 1 - k
            rs = pl.ds(pl.multiple_of(b * rows, rows), rows)
            dyv = dy_ref[rs, :]
            gr[rs, :] = _dot(dyv, wcr_ref[0], DN_NT)
            gi[rs, :] = -_dot(dyv, wci_ref[0], DN_NT)

            def step(kk, c):
                ar, ai = c
                sl = pl.ds(pl.multiple_of(b * rows + (S5_TB - 1 - kk) * S5_SEG, S5_SEG), S5_SEG)
                return gr[sl, :] + lr * ar + li * ai, gi[sl, :] + lr * ai - li * ar

            return lax.fori_loop(0, S5_TB, step, carry, unroll=S5_UNROLL)

        zero = jnp.zeros((S5_SEG, ns), F32)
        er, ei = lax.fori_loop(0, nb, back_project, (zero, zero))
        pr, pi = _lam_power(lr_ref[0], -li_ref[0], l // S5_SEG)
        a0r, a0i = _segment_entries(er, ei, pr, pi, True)

        dwbr_ref[...] = jnp.zeros_like(dwbr_ref)
        dwbi_ref[...] = jnp.zeros_like(dwbi_ref)
        dwcr_ref[...] = jnp.zeros_like(dwcr_ref)
        dwci_ref[...] = jnp.zeros_like(dwci_ref)
        dds_ref[...] = jnp.zeros_like(dds_ref)

        def block(k, carry):
            b = nb - 1 - k
            rs = pl.ds(pl.multiple_of(b * rows, rows), rows)
            u = u_ref[rs, :]
            dyv = dy_ref[rs, :]
            body_rows = pl.ds(S5_SEG, rows)
            x0r, x0i = ckr_ref[0, b], cki_ref[0, b]
            xr_b[0:S5_SEG, :] = x0r
            xi_b[0:S5_SEG, :] = x0i
            xr_b[body_rows, :] = _dot(u, wbr_ref[0], DN_NN)
            xi_b[body_rows, :] = _dot(u, wbi_ref[0], DN_NN)

            def fstep(t, c):
                xr, xi = c
                sl = pl.ds(pl.multiple_of((t + 1) * S5_SEG, S5_SEG), S5_SEG)
                nr = lr * xr - li * xi + xr_b[sl, :]
                ni = lr * xi + li * xr + xi_b[sl, :]
                xr_b[sl, :] = nr
                xi_b[sl, :] = ni
                return nr, ni

            lax.fori_loop(0, S5_TB, fstep, (x0r, x0i), unroll=S5_UNROLL)
            dwcr_ref[0] += _dot(xr_b[body_rows, :], dyv, DN_TN)
            dwci_ref[0] -= _dot(xi_b[body_rows, :], dyv, DN_TN)

            def bstep(kk, c):
                ar, ai = c
                sl = pl.ds(pl.multiple_of(b * rows + (S5_TB - 1 - kk) * S5_SEG, S5_SEG), S5_SEG)
                nr = gr[sl, :] + lr * ar + li * ai
                ni = gi[sl, :] + lr * ai - li * ar
                gr[sl, :] = nr
                gi[sl, :] = ni
                return nr, ni

            ar, ai = lax.fori_loop(0, S5_TB, bstep, carry[:2], unroll=S5_UNROLL)
            a_r, a_i = gr[rs, :], gi[rs, :]
            p_r, p_i = xr_b[0:rows, :], xi_b[0:rows, :]
            per_seg = lambda v: jnp.sum(v.reshape(S5_TB, S5_SEG, ns), axis=0)
            carry = (ar, ai, carry[2] + per_seg(a_r * p_r + a_i * p_i), carry[3] + per_seg(a_i * p_r - a_r * p_i))
            du_ref[rs, :] = (_dot(a_r, wbr_ref[0], DN_NT) + _dot(a_i, wbi_ref[0], DN_NT) + ds_ref[...] * dyv).astype(BF16)
            dwbr_ref[0] += _dot(u, a_r, DN_TN)
            dwbi_ref[0] += _dot(u, a_i, DN_TN)
            dds_ref[...] += jnp.sum(dyv * u, axis=0, keepdims=True)
            return carry

        _, _, dlr, dli = lax.fori_loop(0, nb, block, (a0r, a0i, zero, zero))
        dlr_ref[0] = dlr
        dli_ref[0] = dli

    sb3 = lambda s: (s, 0, 0)
    seq = pl.BlockSpec((l, LANES), lambda s: (0, s))
    ck = pl.BlockSpec((1, nb, S5_SEG, ns), lambda s: (s, 0, 0, 0))
    st = jax.ShapeDtypeStruct
    return pl.pallas_call(
        body, name=name, grid=(S5_SB,),
        in_specs=[seq, seq, pl.BlockSpec((1, 1, ns), sb3), pl.BlockSpec((1, 1, ns), sb3),
                  pl.BlockSpec((1, LANES, ns), sb3), pl.BlockSpec((1, LANES, ns), sb3),
                  pl.BlockSpec((1, ns, LANES), sb3), pl.BlockSpec((1, ns, LANES), sb3),
                  pl.BlockSpec((1, LANES), lambda s: (0, s)), ck, ck],
        out_specs=[seq, pl.BlockSpec((1, LANES, ns), sb3), pl.BlockSpec((1, LANES, ns), sb3),
                   pl.BlockSpec((1, ns, LANES), sb3), pl.BlockSpec((1, ns, LANES), sb3),
                   pl.BlockSpec((1, LANES), lambda s: (0, s)),
                   pl.BlockSpec((1, S5_SEG, ns), sb3), pl.BlockSpec((1, S5_SEG, ns), sb3)],
        out_shape=[st((l, di), BF16), st((S5_SB, LANES, ns), F32), st((S5_SB, LANES, ns), F32),
                   st((S5_SB, ns, LANES), F32), st((S5_SB, ns, LANES), F32), st((1, di), F32),
                   st((S5_SB, S5_SEG, ns), F32), st((S5_SB, S5_SEG, ns), F32)],
        scratch_shapes=[pltpu.VMEM((l, ns), F32), pltpu.VMEM((l, ns), F32),
                        pltpu.VMEM((rows + S5_SEG, ns), F32), pltpu.VMEM((rows + S5_SEG, ns), F32)],
        compiler_params=_cparams(("parallel",)))(
        a_p, dy, lam_re, lam_im, wb_re, wb_im, wc_re, wc_im, d_skip, ck_re, ck_im)


def s5_gate_fwd(y, t, b_glu, a_p, name):
    l, d = y.shape
    tl = ROW_TILE

    def body(y_ref, t_ref, b_ref, z_ref, m_ref):
        yg = _gelu(y_ref[...])
        m_ref[...] = (yg * _sigmoid(t_ref[...] + b_ref[...]) * _silu(z_ref[...])).astype(BF16)

    return _rowcall(body, name, l // tl, [_rows(tl, d), _rows(tl, d), _full((1, d)), _rows(tl, d, 1)], _rows(tl, d),
                    jax.ShapeDtypeStruct((l, d), BF16))(y, t, b_glu.reshape(1, d), a_p)


def s5_gate_bwd(dm, y, t, b_glu, a_p, name):
    l, d = y.shape
    tl = ROW_TILE_HEAVY

    def body(dm_ref, y_ref, t_ref, b_ref, z_ref, dt_ref, dyg_ref, dz_ref, db_ref):
        i = pl.program_id(0)
        dmv = dm_ref[...]
        z = z_ref[...]
        yg = _gelu(y_ref[...])
        sg = _sigmoid(t_ref[...] + b_ref[...])
        y2 = yg * sg
        sz, gz = _silu_both(z)
        dy2 = dmv * sz
        dz_ref[...] = (dmv * y2 * gz).astype(BF16)
        dyg_ref[...] = dy2 * sg
        dt = dy2 * yg * sg * (1.0 - sg)
        dt_ref[...] = dt.astype(BF16)
        _acc(db_ref, jnp.sum(dt, axis=0, keepdims=True), i)

    st = jax.ShapeDtypeStruct
    return _rowcall(body, name, l // tl, [_rows(tl, d), _rows(tl, d), _rows(tl, d), _full((1, d)), _rows(tl, d, 1)],
                    [_rows(tl, d), _rows(tl, d), _rows(tl, d), _full((1, d))],
                    [st((l, d), BF16), st((l, d), F32), st((l, d), BF16), st((1, d), F32)])(
        dm, y, t, b_glu.reshape(1, d), a_p)


def s5_act_bwd(y, dyg_a, dyg_b, name):
    l, d = y.shape
    tl = ROW_TILE

    def body(y_ref, a_ref, b_ref, o_ref):
        o_ref[...] = (a_ref[...] + b_ref[...]) * _gelu_grad(y_ref[...])

    return _rowcall(body, name, l // tl, [_rows(tl, d)] * 3, _rows(tl, d), jax.ShapeDtypeStruct((l, d), F32))(y, dyg_a, dyg_b)


def _seg_perm(t):
    l, d = t.shape
    return t.reshape(S5_SEG, l // S5_SEG, d).transpose(1, 0, 2).reshape(l, d)


def _seg_unperm(t):
    l, d = t.shape
    return t.reshape(l // S5_SEG, S5_SEG, d).transpose(1, 0, 2).reshape(l, d)


def _s5_weights(p):
    lr, li, bbr, bbi = s5_params_fwd(p["a_re"], p["a_im"], p["log_step"], p["b_re"], p["b_im"])
    ns = 8 * S5_STATE
    lam_re = lr.reshape(S5_SB, 1, ns)
    lam_im = li.reshape(S5_SB, 1, ns)
    to_bd = lambda t: _blockdiag(t.reshape(S5_SB, 8, t.shape[1], t.shape[2]))
    wb_re = to_bd(bbr.transpose(0, 2, 1)).astype(BF16)
    wb_im = to_bd(bbi.transpose(0, 2, 1)).astype(BF16)
    wc_re = to_bd(p["c_re"].transpose(0, 2, 1)).astype(BF16)
    wc_im = to_bd(p["c_im"].transpose(0, 2, 1)).astype(BF16)
    return lam_re, lam_im, wb_re, wb_im, wc_re, wc_im


def s5_layer_fwd(h, p, wf, sw, tag):
    l = h.shape[0]
    di = p["d_skip"].shape[0]
    hn = rms_fwd(h, p["norm_g"], tag + "_rms")
    hn_p = _seg_perm(hn)
    a_p = matmul(hn_p, wf["w_in"], "nn", tag + "_mm_in")
    dsk = p["d_skip"].reshape(1, di)
    y, yg, ck_re, ck_im = s5_scan_fused_fwd(a_p, *sw, dsk, tag + "_scan")
    t = matmul(yg, wf["w_glu"], "nn", tag + "_mm_glu")
    m = s5_gate_fwd(y, t, p["b_glu"], a_p, tag + "_gate")
    out_p = matmul(m, wf["w_out"], "nn", tag + "_mm_out")
    h_out = residual_add(h, _seg_unperm(out_p), tag + "_res")
    return h_out, (h, hn_p, a_p, sw, ck_re, ck_im, y, yg, t, m)


def residual_add(h, y, name):
    l, d = h.shape
    tl = ROW_TILE_NARROW

    def body(h_ref, y_ref, o_ref):
        o_ref[...] = h_ref[...] + y_ref[...]

    return _rowcall(body, name, l // tl, [_rows(tl, d)] * 2, _rows(tl, d), jax.ShapeDtypeStruct((l, d), F32))(h, y)


def s5_layer_bwd(dh_out, saved, p, wf, tag, sink):
    h, hn_p, a_p, sw, ck_re, ck_im, y, yg, t, m = saved
    l = h.shape[0]
    di = p["d_skip"].shape[0]
    dsk = p["d_skip"].reshape(1, di)
    dout_p = _seg_perm(dh_out)
    dm = matmul(dout_p, wf["w_out"], "nt", tag + "_mm_dm")
    sink.mm("w_out", m, dout_p, tag + "_mm_gwout")
    dt, dyg_a, dz, db_glu = s5_gate_bwd(dm, y, t, p["b_glu"], a_p, tag + "_gate_bwd")
    dyg_b = matmul(dt, wf["w_glu"], "nt", tag + "_mm_dyg")
    sink.mm("w_glu", yg, dt, tag + "_mm_gwglu")
    dy = s5_act_bwd(y, dyg_a, dyg_b, tag + "_act_bwd")
    du, dwbr, dwbi, dwcr, dwci, dds, dlr, dli = s5_scan_fused_bwd(a_p, dy, *sw, dsk, ck_re, ck_im, tag + "_scanb")
    da = jnp.concatenate([du, dz], axis=1)
    dhn_p = matmul(da, wf["w_in"], "nt", tag + "_mm_dhn")
    sink.mm("w_in", hn_p, da, tag + "_mm_gwin")
    zero = sink.send()
    dh, dng = rms_bwd(h, p["norm_g"] + zero, _seg_unperm(dhn_p), dh_out, tag + "_rms_bwd")
    ex = lambda m_, r, c: _blockdiag_extract(m_, r, c).reshape(S5_GROUPS, r, c).transpose(0, 2, 1)
    dbb_re, dbb_im = ex(dwbr, S5_GROUP, S5_STATE), ex(dwbi, S5_GROUP, S5_STATE)
    g_c_re, g_c_im = ex(dwcr, S5_STATE, S5_GROUP), ex(dwci, S5_STATE, S5_GROUP)
    dl_re = lane_sum8(dlr).reshape(S5_GROUPS, S5_STATE, 1)
    dl_im = lane_sum8(dli).reshape(S5_GROUPS, S5_STATE, 1)
    gar, gai, gls, gbr, gbi = s5_params_bwd(p["a_re"], p["a_im"], p["log_step"], p["b_re"], p["b_im"],
                                            dl_re, dl_im, dbb_re, dbb_im)
    grads = {"norm_g": dng.reshape(-1), "a_re": gar.reshape(S5_GROUPS, S5_STATE),
             "a_im": gai.reshape(S5_GROUPS, S5_STATE), "log_step": gls.reshape(-1), "b_re": gbr, "b_im": gbi,
             "c_re": g_c_re, "c_im": g_c_im, "d_skip": dds.reshape(-1), "b_glu": db_glu.reshape(-1)}
    return dh, grads


def lane_sum8(t):
    sb, seg, ns = t.shape

    def body(t_ref, o_ref):
        o_ref[...] = jnp.sum(t_ref[...], axis=1, keepdims=True)

    return pl.pallas_call(body, name="s5_seg_sum", out_shape=jax.ShapeDtypeStruct((sb, 1, ns), F32))(t)


MLA_DI = MLA_HEADS * 128
MLA_CQ0 = MLA_DI
MLA_CKV0 = MLA_CQ0 + MLA_Q_RANK
MLA_KR0 = MLA_CKV0 + MLA_KV_RANK
MLA_AW = MLA_KR0 + LANES


def _rot_half(x):
    w = x.shape[-1]
    lane = lax.broadcasted_iota(jnp.int32, x.shape, x.ndim - 1)
    return jnp.where(lane % MLA_ROPE < MLA_ROPE // 2, pltpu.roll(x, w - MLA_ROPE // 2, x.ndim - 1),
                     pltpu.roll(x, MLA_ROPE // 2, x.ndim - 1))


def rope_tables(pos, zero):
    l = pos.shape[0]
    tl = ROW_TILE
    j = np.arange(LANES) % MLA_ROPE % (MLA_ROPE // 2)
    inv_freq = (ROPE_THETA ** (-(2.0 * j) / MLA_ROPE)).astype(np.float32).reshape(1, LANES)
    sign = np.where(np.arange(LANES) % MLA_ROPE < MLA_ROPE // 2, -1.0, 1.0).astype(np.float32).reshape(1, LANES)

    def body(p_ref, f_ref, s_ref, cos_ref, sin_ref):
        ang = p_ref[...].astype(F32) * f_ref[...]
        cos_ref[...] = jnp.cos(ang)
        sin_ref[...] = jnp.sin(ang) * s_ref[...]

    st = jax.ShapeDtypeStruct((l, LANES), F32)
    return _rowcall(body, "rope_tables", l // tl, [_rows(tl, 1), _full((1, LANES)), _full((1, LANES))],
                    [_rows(tl, LANES)] * 2, [st, st])(pos, jnp.asarray(inv_freq), jnp.asarray(sign) + zero)


def _rope(x, cos, sins):
    return x * cos + _rot_half(x) * sins


def _rope_t(dy, cos, sins):
    return dy * cos - sins * _rot_half(dy)


def _rmsn(x):
    r = lax.rsqrt(jnp.mean(x * x, axis=-1, keepdims=True) + NORM_EPS)
    return x * r, r


def mla_pre(a, q_g, kv_g, cos, sins, name):
    l = a.shape[0]
    tl = ROW_TILE

    def body(a_ref, qg_ref, kg_ref, cos_ref, sin_ref, cq_ref, ckv_ref, krs_ref):
        xq, _ = _rmsn(a_ref[:, MLA_CQ0:MLA_CKV0])
        cq_ref[...] = (xq * qg_ref[...]).astype(BF16)
        xk, _ = _rmsn(a_ref[:, MLA_CKV0:MLA_KR0])
        ckv_ref[...] = (xk * kg_ref[...]).astype(BF16)
        kr = a_ref[:, MLA_KR0:MLA_AW]
        kr2 = kr + pltpu.roll(kr, MLA_ROPE, 1)
        kr2 = _rope(kr2, cos_ref[...], sin_ref[...])
        lane = lax.broadcasted_iota(jnp.int32, kr2.shape, 1)
        krs_ref[0] = jnp.where(lane < MLA_ROPE, kr2, 0.0).astype(BF16)
        krs_ref[1] = jnp.where(lane >= MLA_ROPE, kr2, 0.0).astype(BF16)

    st = jax.ShapeDtypeStruct
    return _rowcall(body, name, l // tl,
                    [_rows(tl, MLA_AW), _full((1, MLA_Q_RANK)), _full((1, MLA_KV_RANK)), _rows(tl, LANES), _rows(tl, LANES)],
                    [_rows(tl, MLA_Q_RANK), _rows(tl, MLA_KV_RANK), pl.BlockSpec((2, tl, LANES), lambda i: (0, i, 0))],
                    [st((l, MLA_Q_RANK), BF16), st((l, MLA_KV_RANK), BF16), st((2, l, LANES), BF16)])(
        a, q_g.reshape(1, -1), kv_g.reshape(1, -1), cos, sins)


def mla_rope_q(qr, cos, sins, name):
    l, w = qr.shape
    tl = ROW_TILE

    def body(q_ref, cos_ref, sin_ref, o_ref):
        c, s = cos_ref[...], sin_ref[...]
        for p in range(w // LANES):
            sl = slice(p * LANES, (p + 1) * LANES)
            o_ref[:, sl] = _rope(q_ref[:, sl], c, s).astype(BF16)

    return _rowcall(body, name, l // tl, [_rows(tl, w), _rows(tl, LANES), _rows(tl, LANES)], _rows(tl, w),
                    jax.ShapeDtypeStruct((l, w), BF16))(qr, cos, sins)


ATT_OUT = 512
ATT_IN = 512
ATT_R = ATT_OUT // ATT_IN


def _scores(qn, qr, kn, kr, mask_off, transposed):
    q2 = jnp.concatenate([qn, qr], axis=1)
    k2 = jnp.concatenate([kn, kr], axis=1)
    s = (_dot(k2, q2, DN_NT) if transposed else _dot(q2, k2, DN_NT)) * MLA_SCALE
    if mask_off is None:
        return s
    r = lax.broadcasted_iota(jnp.int32, s.shape, 0)
    c = lax.broadcasted_iota(jnp.int32, s.shape, 1)
    return jnp.where((r <= c + mask_off) if transposed else (c + mask_off <= r), s, NEG_INF)


def _fold(x, op):
    out = x[:, :LANES]
    for t in range(1, x.shape[1] // LANES):
        out = op(out, x[:, t * LANES:(t + 1) * LANES])
    return out


def flash_fwd(qn, qr, kv, krs, name):
    l = qn.shape[0]
    nq = l // ATT_OUT

    def body(qn_ref, qr_ref, kv_ref, kr_ref, o_ref, lse_ref, s_buf):
        qi = pl.program_id(1)
        q_r = qr_ref[...]
        q_n = [qn_ref[:, hh * LANES:(hh + 1) * LANES] for hh in range(2)]

        def block_scores(j, mx, mask_off):
            sl = pl.ds(pl.multiple_of(j * ATT_IN, ATT_IN), ATT_IN)
            out = []
            for hh in range(2):
                s = _scores(q_n[hh], q_r, kv_ref[sl, 2 * hh * LANES:(2 * hh + 1) * LANES], kr_ref[hh, sl, :],
                            mask_off, False)
                s_buf[hh, j] = s
                out.append(jnp.maximum(mx[hh], _fold(s, jnp.maximum)))
            return tuple(out)

        ninf = jnp.full((ATT_OUT, LANES), NEG_INF, F32)
        mx = lax.fori_loop(0, ATT_R * qi, lambda j, c: block_scores(j, c, None), (ninf, ninf))
        for d in range(ATT_R):
            mx = block_scores(ATT_R * qi + d, mx, d * ATT_IN)
        m = [jnp.max(mx[hh], axis=-1, keepdims=True) for hh in range(2)]

        def block_pv(j, carry):
            sl = pl.ds(pl.multiple_of(j * ATT_IN, ATT_IN), ATT_IN)
            out = []
            for hh in range(2):
                ls, acc = carry[hh]
                p = jnp.exp(s_buf[hh, j] - m[hh])
                out.append((ls + _fold(p, jnp.add),
                            acc + _dot(p, kv_ref[sl, (2 * hh + 1) * LANES:(2 * hh + 2) * LANES], DN_NN)))
            return tuple(out)

        z = jnp.zeros((ATT_OUT, LANES), F32)
        res = lax.fori_loop(0, ATT_R * (qi + 1), block_pv, ((z, z), (z, z)))
        for hh in range(2):
            lsum = jnp.sum(res[hh][0], axis=-1, keepdims=True)
            o_ref[:, hh * LANES:(hh + 1) * LANES] = res[hh][1] / lsum
            lse_ref[hh] = m[hh] + jnp.log(lsum)

    st = jax.ShapeDtypeStruct
    return pl.pallas_call(
        body, name=name, grid=(MLA_HEADS // 2, nq),
        in_specs=[pl.BlockSpec((ATT_OUT, 2 * LANES), lambda p, i: (i, p)),
                  pl.BlockSpec((ATT_OUT, LANES), lambda p, i: (i, p)),
                  pl.BlockSpec((l, 4 * LANES), lambda p, i: (0, p)),
                  pl.BlockSpec((2, l, LANES), lambda p, i: (0, 0, 0))],
        out_specs=[pl.BlockSpec((ATT_OUT, 2 * LANES), lambda p, i: (i, p)),
                   pl.BlockSpec((2, ATT_OUT, 1), lambda p, i: (p, i, 0))],
        out_shape=[st((l, MLA_DI), F32), st((MLA_HEADS, l, 1), F32)],
        scratch_shapes=[pltpu.VMEM((2, l // ATT_IN, ATT_OUT, ATT_IN), F32)],
        compiler_params=_cparams(("parallel", "arbitrary")))(qn, qr, kv, krs)


def flash_dkv(qn, qr, kv, krs, do, lse_row, delta_row, name):
    l = qn.shape[0]
    nk = l // ATT_OUT
    nq = l // ATT_IN

    def body(qn_ref, qr_ref, do_ref, lse_ref, dl_ref, kv_ref, kr_ref, dkv_ref, dkr_ref):
        kj = pl.program_id(1)
        lane = lax.broadcasted_iota(jnp.int32, (ATT_OUT, LANES), 1)
        kn = [kv_ref[:, 2 * hh * LANES:(2 * hh + 1) * LANES] for hh in range(2)]
        v = [kv_ref[:, (2 * hh + 1) * LANES:(2 * hh + 2) * LANES] for hh in range(2)]

        def block(i, carry, mask_off):
            sl = pl.ds(pl.multiple_of(i * ATT_IN, ATT_IN), ATT_IN)
            q_r = qr_ref[sl, :]
            out = []
            for hh in range(2):
                dk2, dv = carry[hh]
                hs = slice(hh * LANES, (hh + 1) * LANES)
                q_n, d_o = qn_ref[sl, hs], do_ref[sl, hs]
                s = _scores(q_n, q_r, kn[hh], kr_ref[hh], mask_off, True)
                pt = jnp.exp(s - lse_ref[hh, i])
                dv = dv + _dot(pt, d_o, DN_NN)
                dpt = _dot(v[hh], d_o, DN_NT)
                dst = (pt * (dpt - dl_ref[hh, i]) * MLA_SCALE).astype(BF16)
                out.append((dk2 + _dot(dst, jnp.concatenate([q_n, q_r], axis=1), DN_NN), dv))
            return tuple(out)

        z = jnp.zeros((ATT_OUT, LANES), F32)
        z2 = jnp.zeros((ATT_OUT, 2 * LANES), F32)
        res = ((z2, z), (z2, z))
        for d in range(ATT_R):
            res = block(ATT_R * kj + d, res, d * ATT_IN)
        res = lax.fori_loop(ATT_R * (kj + 1), nq, lambda i, c: block(i, c, None), res)
        for hh in range(2):
            dkv_ref[:, 2 * hh * LANES:(2 * hh + 1) * LANES] = res[hh][0][:, :LANES].astype(BF16)
            dkv_ref[:, (2 * hh + 1) * LANES:(2 * hh + 2) * LANES] = res[hh][1].astype(BF16)
        dkr_ref[0] = jnp.where(lane < MLA_ROPE, res[0][0][:, LANES:], res[1][0][:, LANES:])

    st = jax.ShapeDtypeStruct
    return pl.pallas_call(
        body, name=name, grid=(MLA_HEADS // 2, nk),
        in_specs=[pl.BlockSpec((l, 2 * LANES), lambda p, j: (0, p)),
                  pl.BlockSpec((l, LANES), lambda p, j: (0, p)),
                  pl.BlockSpec((l, 2 * LANES), lambda p, j: (0, p)),
                  pl.BlockSpec((2, nq, 1, ATT_IN), lambda p, j: (p, 0, 0, 0)),
                  pl.BlockSpec((2, nq, 1, ATT_IN), lambda p, j: (p, 0, 0, 0)),
                  pl.BlockSpec((ATT_OUT, 4 * LANES), lambda p, j: (j, p)),
                  pl.BlockSpec((2, ATT_OUT, LANES), lambda p, j: (0, j, 0))],
        out_specs=[pl.BlockSpec((ATT_OUT, 4 * LANES), lambda p, j: (j, p)),
                   pl.BlockSpec((1, ATT_OUT, LANES), lambda p, j: (p, j, 0))],
        out_shape=[st((l, 2 * MLA_DI), BF16), st((MLA_HEADS // 2, l, LANES), F32)],
        compiler_params=_cparams(("parallel", "arbitrary")))(qn, qr, do, lse_row, delta_row, kv, krs)


def flash_dq(qn, qr, kv, krs, do, lse, delta, cos, sins, name):
    l = qn.shape[0]
    nq = l // ATT_OUT

    def body(qn_ref, qr_ref, do_ref, lse_ref, dl_ref, kv_ref, kr_ref, cos_ref, sin_ref, dqn_ref, dqr_ref):
        qi = pl.program_id(1)
        q_r = qr_ref[...]
        q_n = [qn_ref[:, hh * LANES:(hh + 1) * LANES] for hh in range(2)]
        d_o = [do_ref[:, hh * LANES:(hh + 1) * LANES] for hh in range(2)]
        lse_h = [lse_ref[hh] for hh in range(2)]
        dl_h = [dl_ref[hh] for hh in range(2)]

        def block(j, carry, mask_off):
            sl = pl.ds(pl.multiple_of(j * ATT_IN, ATT_IN), ATT_IN)
            dq2 = list(carry)
            for hh in range(2):
                kn = kv_ref[sl, 2 * hh * LANES:(2 * hh + 1) * LANES]
                v = kv_ref[sl, (2 * hh + 1) * LANES:(2 * hh + 2) * LANES]
                kr = kr_ref[hh, sl, :]
                s = _scores(q_n[hh], q_r, kn, kr, mask_off, False)
                pr = jnp.exp(s - lse_h[hh])
                dp = _dot(d_o[hh], v, DN_NT)
                ds = (pr * (dp - dl_h[hh]) * MLA_SCALE).astype(BF16)
                dq2[hh] = dq2[hh] + _dot(ds, jnp.concatenate([kn, kr], axis=1), DN_NN)
            return tuple(dq2)

        z2 = jnp.zeros((ATT_OUT, 2 * LANES), F32)
        res = lax.fori_loop(0, ATT_R * qi, lambda j, c: block(j, c, None), (z2, z2))
        for d in range(ATT_R):
            res = block(ATT_R * qi + d, res, d * ATT_IN)
        dqn_ref[:, 0:LANES] = res[0][:, :LANES].astype(BF16)
        dqn_ref[:, LANES:2 * LANES] = res[1][:, :LANES].astype(BF16)
        dqr = res[0][:, LANES:] + res[1][:, LANES:]
        dqr_ref[...] = _rope_t(dqr, cos_ref[...], sin_ref[...]).astype(BF16)

    st = jax.ShapeDtypeStruct
    return pl.pallas_call(
        body, name=name, grid=(MLA_HEADS // 2, nq),
        in_specs=[pl.BlockSpec((ATT_OUT, 2 * LANES), lambda p, i: (i, p)),
                  pl.BlockSpec((ATT_OUT, LANES), lambda p, i: (i, p)),
                  pl.BlockSpec((ATT_OUT, 2 * LANES), lambda p, i: (i, p)),
                  pl.BlockSpec((2, ATT_OUT, 1), lambda p, i: (p, i, 0)),
                  pl.BlockSpec((2, ATT_OUT, 1), lambda p, i: (p, i, 0)),
                  pl.BlockSpec((l, 4 * LANES), lambda p, i: (0, p)),
                  pl.BlockSpec((2, l, LANES), lambda p, i: (0, 0, 0)),
                  pl.BlockSpec((ATT_OUT, LANES), lambda p, i: (i, 0)),
                  pl.BlockSpec((ATT_OUT, LANES), lambda p, i: (i, 0))],
        out_specs=[pl.BlockSpec((ATT_OUT, 2 * LANES), lambda p, i: (i, p)),
                   pl.BlockSpec((ATT_OUT, LANES), lambda p, i: (i, p))],
        out_shape=[st((l, MLA_DI), BF16), st((l, MLA_HEADS * MLA_ROPE), BF16)],
        compiler_params=_cparams(("parallel", "arbitrary")))(qn, qr, do, lse, delta, kv, krs, cos, sins)


def mla_gate_fwd(o, a, name):
    l = o.shape[0]
    tl = ROW_TILE

    def body(o_ref, z_ref, m_ref):
        m_ref[...] = (o_ref[...] * _silu(z_ref[...])).astype(BF16)

    return _rowcall(body, name, l // tl, [_rows(tl, MLA_DI), _rows(tl, MLA_DI)], _rows(tl, MLA_DI),
                    jax.ShapeDtypeStruct((l, MLA_DI), BF16))(o, a)


def mla_gate_bwd(dm, o, a, name):
    l = o.shape[0]
    tl = ROW_TILE

    def body(dm_ref, o_ref, z_ref, do_ref, dz_ref, dl_ref):
        dmv, ov, z = dm_ref[...], o_ref[...], z_ref[...]
        sz, gz = _silu_both(z)
        d_o = dmv * sz
        do_ref[...] = d_o.astype(BF16)
        dz_ref[...] = (dmv * ov * gz).astype(BF16)
        pr = d_o * ov
        for h in range(MLA_HEADS):
            dl_ref[h] = jnp.sum(pr[:, h * LANES:(h + 1) * LANES], axis=1, keepdims=True)

    st = jax.ShapeDtypeStruct
    return _rowcall(body, name, l // tl, [_rows(tl, MLA_DI)] * 3,
                    [_rows(tl, MLA_DI), _rows(tl, MLA_DI), pl.BlockSpec((MLA_HEADS, tl, 1), lambda i: (0, i, 0))],
                    [st((l, MLA_DI), BF16), st((l, MLA_DI), BF16), st((MLA_HEADS, l, 1), F32)])(dm, o, a)


def mla_post(a, dcqn, dckvn, dkr_pairs, dz, q_g, kv_g, cos, sins, name):
    l = a.shape[0]
    tl = ROW_TILE
    npair = MLA_HEADS // 2

    def norm_bwd(x, g, dy):
        xhat, r = _rmsn(x)
        dxh = dy * g
        return r * (dxh - xhat * jnp.mean(dxh * xhat, axis=-1, keepdims=True)), jnp.sum(dy * xhat, axis=0, keepdims=True)

    def body(a_ref, dq_ref, dk_ref, dkr_ref, dz_ref, qg_ref, kg_ref, cos_ref, sin_ref, da_ref, dqg_ref, dkg_ref):
        i = pl.program_id(0)
        da_ref[:, 0:MLA_DI] = dz_ref[...]
        dcq, dqg = norm_bwd(a_ref[:, MLA_CQ0:MLA_CKV0], qg_ref[...], dq_ref[...])
        da_ref[:, MLA_CQ0:MLA_CKV0] = dcq.astype(BF16)
        dckv, dkg = norm_bwd(a_ref[:, MLA_CKV0:MLA_KR0], kg_ref[...], dk_ref[...])
        da_ref[:, MLA_CKV0:MLA_KR0] = dckv.astype(BF16)
        dk2 = dkr_ref[0]
        for p in range(1, npair):
            dk2 = dk2 + dkr_ref[p]
        dk2 = _rope_t(dk2, cos_ref[...], sin_ref[...])
        dk2 = dk2 + pltpu.roll(dk2, MLA_ROPE, 1)
        lane = lax.broadcasted_iota(jnp.int32, dk2.shape, 1)
        da_ref[:, MLA_KR0:MLA_AW] = jnp.where(lane < MLA_ROPE, dk2, 0.0).astype(BF16)
        _acc(dqg_ref, dqg, i)
        _acc(dkg_ref, dkg, i)

    st = jax.ShapeDtypeStruct
    return _rowcall(body, name, l // tl,
                    [_rows(tl, MLA_AW), _rows(tl, MLA_Q_RANK), _rows(tl, MLA_KV_RANK),
                     pl.BlockSpec((npair, tl, LANES), lambda i: (0, i, 0)), _rows(tl, MLA_DI),
                     _full((1, MLA_Q_RANK)), _full((1, MLA_KV_RANK)), _rows(tl, LANES), _rows(tl, LANES)],
                    [_rows(tl, MLA_AW), _full((1, MLA_Q_RANK)), _full((1, MLA_KV_RANK))],
                    [st((l, MLA_AW), BF16), st((1, MLA_Q_RANK), F32), st((1, MLA_KV_RANK), F32)])(
        a, dcqn, dckvn, dkr_pairs, dz, q_g.reshape(1, -1), kv_g.reshape(1, -1), cos, sins)


def _mla_w_in_perm(w):
    r = MLA_Q_RANK + MLA_KV_RANK + MLA_ROPE
    pad = jnp.zeros(w.shape[:-1] + (MLA_AW - MLA_KR0 - MLA_ROPE,), w.dtype)
    return jnp.concatenate([w[..., r:], w[..., :r], pad], axis=-1)


def _mla_w_in_unperm(g):
    r = MLA_Q_RANK + MLA_KV_RANK + MLA_ROPE
    return jnp.concatenate([g[..., MLA_DI:MLA_DI + r], g[..., :MLA_DI]], axis=-1)


def _mla_w_uq_split(w):
    k = w.shape[0]
    w3 = w.reshape(k, MLA_HEADS, MLA_NOPE + MLA_ROPE)
    return w3[:, :, :MLA_NOPE].reshape(k, MLA_HEADS * MLA_NOPE), w3[:, :, MLA_NOPE:].reshape(k, MLA_HEADS * MLA_ROPE)


def _mla_w_uq_merge(gn, gr):
    k = gn.shape[0]
    return jnp.concatenate([gn.reshape(k, MLA_HEADS, MLA_NOPE), gr.reshape(k, MLA_HEADS, MLA_ROPE)], axis=2).reshape(k, -1)


def mla_layer_fwd(h, p, wf, cos, sins, tag):
    hn = rms_fwd(h, p["norm_g"], tag + "_rms")
    w_in = _mla_w_in_perm(wf["w_in"])
    w_uq_n, w_uq_r = _mla_w_uq_split(wf["w_uq"])
    a = matmul(hn, w_in, "nn", tag + "_mm_in")
    cqn, ckvn, krs = mla_pre(a, p["q_norm_g"], p["kv_norm_g"], cos, sins, tag + "_pre")
    qn = matmul(cqn, w_uq_n, "nn", tag + "_mm_qn", out_dtype=BF16)
    qr_raw = matmul(cqn, w_uq_r, "nn", tag + "_mm_qr")
    qr = mla_rope_q(qr_raw, cos, sins, tag + "_rope_q")
    kv = matmul(ckvn, wf["w_ukv"], "nn", tag + "_mm_kv", out_dtype=BF16)
    o, lse = flash_fwd(qn, qr, kv, krs, tag + "_flash")
    m = mla_gate_fwd(o, a, tag + "_gate")
    h_out = matmul(m, wf["w_out"], "nn", tag + "_mm_out", add=h)
    return h_out, (h, hn, a, cqn, ckvn, krs, qn, qr, kv, o, lse, m, w_in, w_uq_n, w_uq_r)


def mla_layer_bwd(dh_out, saved, p, wf, cos, sins, tag, sink):
    h, hn, a, cqn, ckvn, krs, qn, qr, kv, o, lse, m, w_in, w_uq_n, w_uq_r = saved
    l = h.shape[0]
    dm = matmul(dh_out, wf["w_out"], "nt", tag + "_mm_dm")
    sink.mm("w_out", m, dh_out, tag + "_mm_gwout")
    do, dz, delta = mla_gate_bwd(dm, o, a, tag + "_gate_bwd")
    lse_row = lse.reshape(MLA_HEADS, l // ATT_IN, 1, ATT_IN)
    delta_row = delta.reshape(MLA_HEADS, l // ATT_IN, 1, ATT_IN)
    dkv, dkr_pairs = flash_dkv(qn, qr, kv, krs, do, lse_row, delta_row, tag + "_flash_dkv")
    dqn, dqr = flash_dq(qn, qr, kv, krs, do, lse, delta, cos, sins, tag + "_flash_dq")
    dcqn = matmul(dqn, w_uq_n, "nt", tag + "_mm_dcq_n")
    dcqn = matmul(dqr, w_uq_r, "nt", tag + "_mm_dcq_r", add=dcqn)
    g_uq_n = matmul(cqn, dqn, "tn", tag + "_mm_guq_n")
    g_uq_r = matmul(cqn, dqr, "tn", tag + "_mm_guq_r")
    dckvn = matmul(dkv, wf["w_ukv"], "nt", tag + "_mm_dckv")
    sink.mm("w_ukv", ckvn, dkv, tag + "_mm_gukv")
    da, dqg, dkg = mla_post(a, dcqn, dckvn, dkr_pairs, dz, p["q_norm_g"], p["kv_norm_g"], cos, sins, tag + "_post")
    dhn = matmul(da, w_in, "nt", tag + "_mm_dhn")
    g_w_in = matmul(hn, da, "tn", tag + "_mm_gwin")
    sink.put("w_uq", _mla_w_uq_merge(g_uq_n, g_uq_r))
    sink.put("w_in", _mla_w_in_unperm(g_w_in))
    zero = sink.send()
    dh, dng = rms_bwd(h, p["norm_g"] + zero, dhn, dh_out, tag + "_rms_bwd")
    grads = {"norm_g": dng.reshape(-1), "q_norm_g": dqg.reshape(-1), "kv_norm_g": dkg.reshape(-1)}
    return dh, grads


ANY = pl.BlockSpec(memory_space=pl.ANY)


def _me():
    return lax.axis_index("x"), lax.axis_index("y"), lax.axis_index("c")


def _chip():
    return 2 * lax.axis_index("x") + lax.axis_index("y")


def _other_chips(x, y):
    return [(1 - x, y), (x, 1 - y), (1 - x, 1 - y)]


def _rcopy(src, dst, ssem, rsem, dev):
    return pltpu.make_async_remote_copy(src_ref=src, dst_ref=dst, send_sem=ssem, recv_sem=rsem,
                                        device_id=dev, device_id_type=MESH)


def _half(ref, c, hf):
    return ref.at[pl.ds(c * hf, hf), :]


HBM = pl.BlockSpec(memory_space=pltpu.HBM)
SEM = pl.BlockSpec(memory_space=pltpu.SEMAPHORE)
SPLIT_EFFECT = pltpu.SideEffectType.DATAFLOW_SIDE_EFFECTING


def gather_start(wb, after, name):
    nr, w = wb.shape
    hf = nr // 2

    def body(w_ref, land_ref, after_ref, ssem, rsem, w_thru, land_thru, token):
        x, y, c = _me()
        k = 2 * x + y
        for j, (cx, cy) in enumerate(_other_chips(x, y)):
            _rcopy(_half(w_ref, c, hf), _half(land_ref.at[k], c, hf), ssem.at[j], rsem.at[j], (cx, cy, c)).start()
        token[...] = jnp.zeros_like(token)

    land = lax.empty((N_CHIPS, nr, w), wb.dtype)
    return pl.pallas_call(
        body, name=name,
        out_shape=(pltpu.SemaphoreType.DMA((3,)), pltpu.SemaphoreType.DMA((3,)), pltpu.HBM(wb.shape, wb.dtype),
                   pltpu.HBM(land.shape, land.dtype), jax.ShapeDtypeStruct((8, LANES), F32)),
        in_specs=(HBM, HBM, ANY), out_specs=(SEM, SEM, HBM, HBM, pl.BlockSpec(memory_space=pltpu.VMEM)),
        input_output_aliases={0: 2, 1: 3},
        compiler_params=pltpu.CompilerParams(has_side_effects=SPLIT_EFFECT))(
        pltpu.with_memory_space_constraint(wb, pltpu.HBM), pltpu.with_memory_space_constraint(land, pltpu.HBM), after)


def gather_wait(ssem, rsem, w_thru, land_thru, after, name):
    nr, w = w_thru.shape
    hf = nr // 2

    def body(w_ref, land_ref, ssem_ref, rsem_ref, after_ref, w_dead, got_ref):
        x, y, c = _me()
        for j, (cx, cy) in enumerate(_other_chips(x, y)):
            cp = _rcopy(_half(w_ref, c, hf), _half(land_ref.at[2 * cx + cy], c, hf), ssem_ref.at[j], rsem_ref.at[j],
                        (cx, cy, c))
            cp.wait_send()
            cp.wait_recv()

    return pl.pallas_call(
        body, name=name, out_shape=(pltpu.HBM(w_thru.shape, w_thru.dtype), pltpu.HBM(land_thru.shape, land_thru.dtype)),
        in_specs=(HBM, HBM, SEM, SEM, ANY), out_specs=(HBM, HBM), input_output_aliases={0: 0, 1: 1},
        compiler_params=pltpu.CompilerParams(has_side_effects=SPLIT_EFFECT))(w_thru, land_thru, ssem, rsem, after)[1]


def gather_handover(land, wb, name):
    _, nr, w = land.shape
    hf = nr // 2

    def body(l_ref, o_ref, ssem, rsem):
        x, y, c = _me()
        chips = _other_chips(x, y)
        sends = []
        for j, (cx, cy) in enumerate(chips):
            region = _half(o_ref.at[2 * cx + cy], c, hf)
            sends.append(_rcopy(region, region, ssem.at[j], rsem.at[j], (x, y, 1 - c)))
            sends[-1].start()
        for j, (cx, cy) in enumerate(chips):
            region = _half(o_ref.at[2 * cx + cy], 1 - c, hf)
            _rcopy(region, region, ssem.at[j], rsem.at[j], (x, y, 1 - c)).wait_recv()
        for cp in sends:
            cp.wait_send()

    out = pl.pallas_call(
        body, name=name, in_specs=[ANY], out_specs=ANY, input_output_aliases={0: 0},
        out_shape=jax.ShapeDtypeStruct(land.shape, land.dtype),
        scratch_shapes=[pltpu.SemaphoreType.DMA((3,)), pltpu.SemaphoreType.DMA((3,))])(land)
    return lax.dynamic_update_slice(out, wb[None], (_chip(), 0, 0))


def reduce_start(t, after, name):
    def body(t_ref, land_ref, after_ref, ssem, rsem, t_thru, land_thru, token):
        x, y, c = _me()
        k = 2 * x + y
        for j, (cx, cy) in enumerate(_other_chips(x, y)):
            _rcopy(t_ref.at[2 * cx + cy], land_ref.at[k], ssem.at[j], rsem.at[j], (cx, cy, c)).start()
        token[...] = jnp.zeros_like(token)

    land = lax.empty(t.shape, t.dtype)
    return pl.pallas_call(
        body, name=name,
        out_shape=(pltpu.SemaphoreType.DMA((3,)), pltpu.SemaphoreType.DMA((3,)), pltpu.HBM(t.shape, t.dtype),
                   pltpu.HBM(t.shape, t.dtype), jax.ShapeDtypeStruct((8, LANES), F32)),
        in_specs=(HBM, HBM, ANY), out_specs=(SEM, SEM, HBM, HBM, pl.BlockSpec(memory_space=pltpu.VMEM)),
        input_output_aliases={0: 2, 1: 3},
        compiler_params=pltpu.CompilerParams(has_side_effects=SPLIT_EFFECT))(
        pltpu.with_memory_space_constraint(t, pltpu.HBM), pltpu.with_memory_space_constraint(land, pltpu.HBM), after)


def bcast_start(g, after, name):
    def body(g_ref, land_ref, after_ref, ssem, rsem, g_thru, land_thru, token):
        x, y, c = _me()
        k = 2 * x + y
        for j, (cx, cy) in enumerate(_other_chips(x, y)):
            _rcopy(g_ref, land_ref.at[k], ssem.at[j], rsem.at[j], (cx, cy, c)).start()
        token[...] = jnp.zeros_like(token)

    land = lax.empty((N_CHIPS,) + g.shape, g.dtype)
    return pl.pallas_call(
        body, name=name,
        out_shape=(pltpu.SemaphoreType.DMA((3,)), pltpu.SemaphoreType.DMA((3,)), pltpu.HBM(g.shape, g.dtype),
                   pltpu.HBM(land.shape, land.dtype), jax.ShapeDtypeStruct((8, LANES), F32)),
        in_specs=(HBM, HBM, ANY), out_specs=(SEM, SEM, HBM, HBM, pl.BlockSpec(memory_space=pltpu.VMEM)),
        input_output_aliases={0: 2, 1: 3},
        compiler_params=pltpu.CompilerParams(has_side_effects=SPLIT_EFFECT))(
        pltpu.with_memory_space_constraint(g, pltpu.HBM), pltpu.with_memory_space_constraint(land, pltpu.HBM), after)


def bcast_wait(ssem, rsem, g_thru, land_thru, after, name):
    def body(g_ref, land_ref, ssem_ref, rsem_ref, after_ref, g_out, got_ref):
        x, y, c = _me()
        for j, (cx, cy) in enumerate(_other_chips(x, y)):
            cp = _rcopy(g_ref, land_ref.at[2 * cx + cy], ssem_ref.at[j], rsem_ref.at[j], (cx, cy, c))
            cp.wait_send()
            cp.wait_recv()

    g, land = pl.pallas_call(
        body, name=name, out_shape=(pltpu.HBM(g_thru.shape, g_thru.dtype), pltpu.HBM(land_thru.shape, land_thru.dtype)),
        in_specs=(HBM, HBM, SEM, SEM, ANY), out_specs=(HBM, HBM), input_output_aliases={0: 0, 1: 1},
        compiler_params=pltpu.CompilerParams(has_side_effects=SPLIT_EFFECT))(g_thru, land_thru, ssem, rsem, after)
    return lax.dynamic_update_slice(land, g[None], (_chip(), 0, 0))


def reduce_wait(ssem, rsem, t_thru, land_thru, after, name):
    def body(t_ref, land_ref, ssem_ref, rsem_ref, after_ref, t_out, got_ref):
        x, y, c = _me()
        k = 2 * x + y
        for j, (cx, cy) in enumerate(_other_chips(x, y)):
            cp = _rcopy(t_ref.at[k], land_ref.at[2 * cx + cy], ssem_ref.at[j], rsem_ref.at[j], (cx, cy, c))
            cp.wait_send()
            cp.wait_recv()

    return pl.pallas_call(
        body, name=name, out_shape=(pltpu.HBM(t_thru.shape, t_thru.dtype), pltpu.HBM(land_thru.shape, land_thru.dtype)),
        in_specs=(HBM, HBM, SEM, SEM, ANY), out_specs=(HBM, HBM), input_output_aliases={0: 0, 1: 1},
        compiler_params=pltpu.CompilerParams(has_side_effects=SPLIT_EFFECT))(t_thru, land_thru, ssem, rsem, after)


def grads_to_sibling(ps, name="grads_to_sibling"):
    n = len(ps)

    def body(*refs):
        p_refs, o_refs, ssem, rsem = refs[:n], refs[n:2 * n], refs[2 * n], refs[2 * n + 1]
        x, y, c = _me()
        cps = []
        for a in range(n):
            hf = ps[a].shape[1] // 2
            cps.append(_rcopy(p_refs[a].at[:, pl.ds((1 - c) * hf, hf), :], o_refs[a], ssem.at[a], rsem.at[a],
                              (x, y, 1 - c)))
        for cp in cps:
            cp.start()
        for cp in cps:
            cp.wait()

    return pl.pallas_call(
        body, name=name, in_specs=[ANY] * n, out_specs=[ANY] * n,
        out_shape=[jax.ShapeDtypeStruct((N_CHIPS, p.shape[1] // 2, p.shape[2]), p.dtype) for p in ps],
        scratch_shapes=[pltpu.SemaphoreType.DMA((n,)), pltpu.SemaphoreType.DMA((n,))])(*ps)


def pair_sum(p, ra, out_dtype, name):
    _, nr, w = p.shape
    hf = nr // 2
    tr = _pick_rows(hf, cap=max(512, 2 * SUM_BLOCK_BYTES // (4 * w)))
    nb = hf // tr

    def body(c_ref, p_ref, r_ref, o_ref):
        o_ref[...] = (p_ref[...] + r_ref[...]).astype(out_dtype)

    c = lax.axis_index("c").astype(jnp.int32).reshape(1)
    return pl.pallas_call(
        body, name=name,
        grid_spec=pltpu.PrefetchScalarGridSpec(
            num_scalar_prefetch=1, grid=(N_CHIPS, nb),
            in_specs=[pl.BlockSpec((1, tr, w), lambda k, i, c_ref: (k, c_ref[0] * nb + i, 0)),
                      pl.BlockSpec((1, tr, w), lambda k, i, c_ref: (k, i, 0))],
            out_specs=pl.BlockSpec((1, tr, w), lambda k, i, c_ref: (k, i, 0))),
        out_shape=jax.ShapeDtypeStruct((N_CHIPS, hf, w), out_dtype),
        compiler_params=_cparams(("parallel", "parallel")))(c, p, ra)


def grads_across_chips(ts):
    n = len(ts)

    def body(*refs):
        t_refs, o_refs, ssem, rsem = refs[:n], refs[n:2 * n], refs[2 * n], refs[2 * n + 1]
        x, y, c = _me()
        k = 2 * x + y
        chips = _other_chips(x, y)
        sends = [_rcopy(t_refs[a].at[2 * cx + cy], o_refs[a].at[k], ssem.at[3 * a + j], rsem.at[3 * a + j], (cx, cy, c))
                 for a in range(n) for j, (cx, cy) in enumerate(chips)]
        for cp in sends:
            cp.start()
        for a in range(n):
            for j, (cx, cy) in enumerate(chips):
                _rcopy(t_refs[a].at[k], o_refs[a].at[2 * cx + cy], ssem.at[3 * a + j], rsem.at[3 * a + j],
                       (cx, cy, c)).wait_recv()
        for cp in sends:
            cp.wait_send()

    return pl.pallas_call(
        body, name="grads_across_chips", in_specs=[ANY] * n, out_specs=[ANY] * n,
        out_shape=[jax.ShapeDtypeStruct(t.shape, t.dtype) for t in ts],
        scratch_shapes=[pltpu.SemaphoreType.DMA((3 * n,)), pltpu.SemaphoreType.DMA((3 * n,))])(*ts)


def chip_sum(t, rb, name):
    _, hf, w = rb.shape
    tr = _pick_rows(hf, cap=max(512, SUM_BLOCK_BYTES // (4 * w)))
    nb = hf // tr

    def body(kc_ref, t_ref, r_ref, o_ref):
        k = kc_ref[0]
        acc = jnp.where(k == 0, t_ref[0], r_ref[0]).astype(F32)
        for j in range(1, N_CHIPS):
            acc = acc + jnp.where(k == j, t_ref[0], r_ref[j]).astype(F32)
        o_ref[...] = acc

    kc = jnp.stack([_chip(), lax.axis_index("c")]).astype(jnp.int32)
    return pl.pallas_call(
        body, name=name,
        grid_spec=pltpu.PrefetchScalarGridSpec(
            num_scalar_prefetch=1, grid=(nb,),
            in_specs=[pl.BlockSpec((1, tr, w), lambda i, kc_ref: (kc_ref[0], i, 0)),
                      pl.BlockSpec((N_CHIPS, tr, w), lambda i, kc_ref: (0, i, 0))],
            out_specs=pl.BlockSpec((tr, w), lambda i, kc_ref: (kc_ref[1] * nb + i, 0))),
        out_shape=jax.ShapeDtypeStruct((2 * hf, w), F32), compiler_params=_cparams(("parallel",)))(kc, t, rb)


def reduced_to_sibling(gs):
    n = len(gs)

    def body(*refs):
        o_refs, ssem, rsem = refs[n:2 * n], refs[2 * n], refs[2 * n + 1]
        x, y, c = _me()
        cps = []
        for a in range(n):
            hf = gs[a].shape[0] // 2
            cps.append(_rcopy(_half(o_refs[a], c, hf), _half(o_refs[a], c, hf), ssem.at[a], rsem.at[a], (x, y, 1 - c)))
        for cp in cps:
            cp.start()
        for a in range(n):
            hf = gs[a].shape[0] // 2
            _rcopy(_half(o_refs[a], c, hf), _half(o_refs[a], 1 - c, hf), ssem.at[a], rsem.at[a],
                   (x, y, 1 - c)).wait_recv()
        for cp in cps:
            cp.wait_send()

    return pl.pallas_call(
        body, name="reduced_to_sibling", in_specs=[ANY] * n, out_specs=[ANY] * n,
        input_output_aliases={a: a for a in range(n)},
        out_shape=[jax.ShapeDtypeStruct(g.shape, g.dtype) for g in gs],
        scratch_shapes=[pltpu.SemaphoreType.DMA((n,)), pltpu.SemaphoreType.DMA((n,))])(*gs)


def _adamw_step(w_ref, g_ref, m_ref, v_ref, d_ref, nm_ref, nv_ref):
    bc1 = 1.0 - ADAM_B1 ** ADAM_STEP
    bc2 = 1.0 - ADAM_B2 ** ADAM_STEP
    gv = g_ref[...]
    nm = ADAM_B1 * m_ref[...] + (1.0 - ADAM_B1) * gv
    nv = ADAM_B2 * v_ref[...] + (1.0 - ADAM_B2) * (gv * gv)
    nm_ref[...] = nm
    nv_ref[...] = nv
    d_ref[...] = -ADAM_LR * ((nm / bc1) / (jnp.sqrt(nv / bc2) + ADAM_EPS) + ADAM_WD * w_ref[...])


def adamw_packed(w, g_buf, r0, m, v, name):
    r, c = w.shape
    tr = _tile_rows(r, r0, (1024, 512, 384, 256, 128))

    def body(w_ref, g_ref, m_ref, v_ref, go_ref, d_ref, nm_ref, nv_ref):
        go_ref[...] = g_ref[...]
        _adamw_step(w_ref, g_ref, m_ref, v_ref, d_ref, nm_ref, nv_ref)

    own = pl.BlockSpec((tr, CHUNK_W), lambda i, j: (i, j))
    packed = pl.BlockSpec((tr, CHUNK_W), lambda i, j: ((r0 + j * r) // tr + i, 0))
    st = jax.ShapeDtypeStruct((r, c), F32)
    return pl.pallas_call(body, name=name, grid=(r // tr, c // CHUNK_W), in_specs=[own, packed, own, own],
                          out_specs=[own] * 4, out_shape=[st] * 4,
                          compiler_params=_cparams(("parallel", "parallel")))(w, g_buf, m, v)


def adamw(w, g, m, v, name):
    r, wd = w.shape
    tr = _pick_rows(r, cap=max(16, ADAMW_BLOCK_BYTES // (4 * wd)))
    body = functools.partial(_adamw_step)

    spec = pl.BlockSpec((tr, wd), lambda i: (i, 0))
    st = jax.ShapeDtypeStruct((r, wd), F32)
    return pl.pallas_call(body, name=name, grid=(r // tr,), in_specs=[spec] * 4, out_specs=[spec] * 3,
                          out_shape=[st, st, st], compiler_params=_cparams(("parallel",)))(w, g, m, v)


LAYER_KINDS = ("gmlp", "s5", "mla", "gmlp")
PARAMS = {
    "gmlp": ("norm_g", "w_in", "ln_g", "ln_b", "w_s", "b_s", "w_out"),
    "s5": ("norm_g", "w_in", "a_re", "a_im", "log_step", "b_re", "b_im", "c_re", "c_im", "d_skip", "w_glu", "b_glu", "w_out"),
    "mla": ("norm_g", "w_in", "q_norm_g", "w_uq", "kv_norm_g", "w_ukv", "w_out"),
}
COL_SHARDED = ("w_in", "w_uq", "w_ukv")
ROW_SHARDED = ("w_out", "w_glu")
WEIGHT_NAMES = [("l%d_" % i) + n for i, kind in enumerate(LAYER_KINDS) for n in PARAMS[kind]] + ["final_norm_g"]


def _is_big(name):
    return name.split("_", 1)[1] in COL_SHARDED + ROW_SHARDED


BIG = [n for n in WEIGHT_NAMES if _is_big(n)]
SMALL = [n for n in WEIGHT_NAMES if not _is_big(n)]


def _pack_rows(blocks):
    return jnp.concatenate([b.reshape(-1, PACK_W) for b in blocks], axis=0)


def _shard_major(wn, full, width):
    r, c = full.shape
    if wn in COL_SHARDED:
        t = full.reshape(r, N_CHIPS, c // N_CHIPS).transpose(1, 0, 2)
    else:
        t = full.reshape(N_CHIPS, r // N_CHIPS, c)
    return t.reshape(N_CHIPS, -1, width)


def _from_shard_major(name, t, block_shape):
    r, c = block_shape
    if name.split("_", 1)[1] in COL_SHARDED:
        return t.reshape(N_CHIPS, r, c).transpose(1, 0, 2).reshape(r, N_CHIPS * c)
    return t.reshape(N_CHIPS * r, c)


class BigGradSink:
    ORDER = ("w_out", "w_glu", "w_ukv", "w_uq", "w_in")
    ROW_MAJOR = {2: ("w_uq", "w_in")}

    def __init__(self, layer, block_shapes):
        self.layer = layer
        self.regions = {}
        r0 = 0
        for wn in self.ORDER:
            if wn in block_shapes:
                shape = block_shapes[wn]
                self.regions[wn] = (r0, shape, wn not in self.ROW_MAJOR.get(layer, ()))
                r0 += shape[0] * shape[1] // CHUNK_W
        self.buf = lax.empty((N_CHIPS, r0, CHUNK_W), F32)
        self.flight = None

    def mm(self, wn, a, b, name):
        r0, _, direct = self.regions[wn]
        assert direct
        self.buf = matmul_tn_packed(a, b, self.buf, r0, wn in COL_SHARDED, name)

    def put(self, wn, full):
        r0, _, direct = self.regions[wn]
        assert not direct
        piece = _shard_major(wn, full, CHUNK_W)
        self.buf = lax.dynamic_update_slice(self.buf, piece, (0, r0, 0))

    def send(self):
        i = self.layer
        sib, = grads_to_sibling([self.buf], "grads_to_sibling_l%d" % i)
        t = pair_sum(self.buf, sib, BF16, "pair_sum_l%d" % i)
        self.flight = reduce_start(t, sib, "reduce_l%d_start" % i)
        return self.flight[4][0, 0]


def _small_pack(arrs, total_padded):
    flat = jnp.concatenate([a.reshape(-1) for a in arrs])
    return jnp.pad(flat, (0, total_padded - flat.shape[0]))


def kernel(x, positions, l0_norm_g, l0_w_in, l0_ln_g, l0_ln_b, l0_w_s, l0_b_s, l0_w_out, l1_norm_g, l1_w_in, l1_a_re, l1_a_im, l1_log_step, l1_b_re, l1_b_im, l1_c_re, l1_c_im, l1_d_skip, l1_w_glu, l1_b_glu, l1_w_out, l2_norm_g, l2_w_in, l2_q_norm_g, l2_w_uq, l2_kv_norm_g, l2_w_ukv, l2_w_out, l3_norm_g, l3_w_in, l3_ln_g, l3_ln_b, l3_w_s, l3_b_s, l3_w_out, final_norm_g, loss_target, m_l0_norm_g, m_l0_w_in, m_l0_ln_g, m_l0_ln_b, m_l0_w_s, m_l0_b_s, m_l0_w_out, m_l1_norm_g, m_l1_w_in, m_l1_a_re, m_l1_a_im, m_l1_log_step, m_l1_b_re, m_l1_b_im, m_l1_c_re, m_l1_c_im, m_l1_d_skip, m_l1_w_glu, m_l1_b_glu, m_l1_w_out, m_l2_norm_g, m_l2_w_in, m_l2_q_norm_g, m_l2_w_uq, m_l2_kv_norm_g, m_l2_w_ukv, m_l2_w_out, m_l3_norm_g, m_l3_w_in, m_l3_ln_g, m_l3_ln_b, m_l3_w_s, m_l3_b_s, m_l3_w_out, m_final_norm_g, v_l0_norm_g, v_l0_w_in, v_l0_ln_g, v_l0_ln_b, v_l0_w_s, v_l0_b_s, v_l0_w_out, v_l1_norm_g, v_l1_w_in, v_l1_a_re, v_l1_a_im, v_l1_log_step, v_l1_b_re, v_l1_b_im, v_l1_c_re, v_l1_c_im, v_l1_d_skip, v_l1_w_glu, v_l1_b_glu, v_l1_w_out, v_l2_norm_g, v_l2_w_in, v_l2_q_norm_g, v_l2_w_uq, v_l2_kv_norm_g, v_l2_w_ukv, v_l2_w_out, v_l3_norm_g, v_l3_w_in, v_l3_ln_g, v_l3_ln_b, v_l3_w_s, v_l3_b_s, v_l3_w_out, v_final_norm_g):
    args = locals()
    w = {n: args[n] for n in WEIGHT_NAMES}
    mom_m = {n: args["m_" + n] for n in WEIGHT_NAMES}
    mom_v = {n: args["v_" + n] for n in WEIGHT_NAMES}
    h0 = x[0]
    target = loss_target[0]
    pos = positions.reshape(-1, 1)

    full = {}

    def pack_unit(layers):
        names = [n for n in BIG if int(n[1]) in layers]
        rows = [w[n].size // PACK_W for n in names]
        pad = -sum(rows) % PACK_ROW_ALIGN
        return names, rows, _pack_rows([w[n].astype(BF16) for n in names] + [jnp.zeros((pad, PACK_W), BF16)])

    def unpack_unit(names, rows, gathered):
        r0 = 0
        for n, nr in zip(names, rows):
            full[n] = _from_shard_major(n, gathered[:, r0:r0 + nr, :], w[n].shape)
            r0 += nr

    unit0, unit1, unit2 = pack_unit((0,)), pack_unit((1,)), pack_unit((2, 3))
    wp = dict(w)

    def layer_params(i):
        pre = "l%d_" % i
        p = {k[len(pre):]: v for k, v in wp.items() if k.startswith(pre)}
        wf = {k[len(pre):]: v for k, v in full.items() if k.startswith(pre)}
        return p, wf

    flight = gather_start(unit0[2], unit1[2], "gather_l0_start")
    cos, sins = rope_tables(pos, flight[4][0, 0])
    wp["l1_a_re"] = w["l1_a_re"] + flight[4][0, 0]
    s5_weights = _s5_weights(layer_params(1)[0])
    land = gather_wait(*flight[:4], s5_weights[2], "gather_l0_wait")
    got = gather_handover(land, unit0[2], "gather_l0_handover")
    unpack_unit(unit0[0], unit0[1], got)
    flight = gather_start(unit1[2], got, "gather_l1_start")
    wp["l0_norm_g"] = w["l0_norm_g"] + flight[4][0, 0]

    h = h0
    saved = []
    for i, kind in enumerate(LAYER_KINDS):
        if i == 1:
            land = gather_wait(*flight[:4], h, "gather_l1_wait")
            got = gather_handover(land, unit1[2], "gather_l1_handover")
            unpack_unit(unit1[0], unit1[1], got)
            flight = gather_start(unit2[2], got, "gather_l23_start")
            wp["l1_norm_g"] = w["l1_norm_g"] + flight[4][0, 0]
        if i == 2:
            land = gather_wait(*flight[:4], h, "gather_l23_wait")
            unpack_unit(unit2[0], unit2[1], gather_handover(land, unit2[2], "gather_l23_handover"))
        p, wf = layer_params(i)
        tag = "l%d" % i
        if kind == "gmlp":
            h, s = gmlp_layer_fwd(h, p, wf, tag)
        elif kind == "s5":
            h, s = s5_layer_fwd(h, p, wf, s5_weights, tag)
        else:
            h, s = mla_layer_fwd(h, p, wf, cos, sins, tag)
        saved.append(s)
    loss_part, dh, g_final = loss_head(h, final_norm_g, target)

    grads = {"final_norm_g": g_final.reshape(-1)}
    sinks = {}

    for i in reversed(range(len(LAYER_KINDS))):
        kind = LAYER_KINDS[i]
        p, wf = layer_params(i)
        tag = "l%d" % i
        sink = sinks[i] = BigGradSink(i, {n[3:]: w[n].shape for n in BIG if int(n[1]) == i})
        if kind == "gmlp":
            dh, g = gmlp_layer_bwd(dh, saved[i], p, wf, tag, sink)
        elif kind == "s5":
            dh, g = s5_layer_bwd(dh, saved[i], p, wf, tag, sink)
        else:
            dh, g = mla_layer_bwd(dh, saved[i], p, wf, cos, sins, tag, sink)
        for k, val in g.items():
            grads["l%d_%s" % (i, k)] = val
    grad_x = dh[None]

    n_small = sum(w[n].size for n in SMALL)
    piece = N_CHIPS * 2 * 16 * PACK_W
    n_small_pad = -(-(n_small + 1) // piece) * piece
    nrs = n_small_pad // N_CHIPS // PACK_W
    p_small = _small_pack([grads[n] for n in SMALL] + [loss_part], n_small_pad).reshape(N_CHIPS, nrs, PACK_W)
    sib_small, = grads_to_sibling([p_small], "grads_to_sibling_small")
    t_small = pair_sum(p_small, sib_small, F32, "pair_sum_small")
    rb_small, = grads_across_chips([t_small])
    halves = [chip_sum(t_small, rb_small, "chip_sum_small")]

    after = halves[0]
    for i in reversed(range(len(LAYER_KINDS))):
        t_i, rb_i = reduce_wait(*sinks[i].flight[:4], after, "reduce_l%d_wait" % i)
        halves.append(chip_sum(t_i, rb_i, "chip_sum_l%d" % i))
        after = halves[-1]
    reduced = reduced_to_sibling(halves)
    small_flight = bcast_start(reduced[0], reduced[1], "small_allgather_start")

    g_out, d_out, nm_out, nv_out = {}, {}, {}, {}
    for i, g_i in zip(reversed(range(len(LAYER_KINDS))), reduced[1:]):
        for wn, (r0, shape, direct) in sinks[i].regions.items():
            n = "l%d_%s" % (i, wn)
            if direct:
                g_out[n], d_out[n], nm_out[n], nv_out[n] = adamw_packed(w[n], g_i, r0, mom_m[n], mom_v[n], "adamw_" + n)
            else:
                g_out[n] = g_i[r0:r0 + shape[0] * shape[1] // CHUNK_W].reshape(shape)
                d_out[n], nm_out[n], nv_out[n] = adamw(w[n], g_out[n], mom_m[n], mom_v[n], "adamw_" + n)
    small_all = bcast_wait(*small_flight[:4], nv_out["l0_w_in"], "small_allgather_wait")
    g_small = small_all.reshape(-1, PACK_W)
    sp = lambda d: _small_pack([d[n] for n in SMALL], n_small_pad).reshape(-1, PACK_W)
    d_small, nm_small, nv_small = adamw(sp(w), g_small, sp(mom_m), sp(mom_v), "adamw_small")
    for buf, out in ((g_small, g_out), (d_small, d_out), (nm_small, nm_out), (nv_small, nv_out)):
        flat = buf.reshape(-1)
        o = 0
        for n in SMALL:
            out[n] = flat[o:o + w[n].size].reshape(w[n].shape)
            o += w[n].size
    loss = g_small.reshape(-1)[n_small]
    return (loss, grad_x, *[g_out[n] for n in WEIGHT_NAMES], *[d_out[n] for n in WEIGHT_NAMES],
            *[nm_out[n] for n in WEIGHT_NAMES], *[nv_out[n] for n in WEIGHT_NAMES])
```

```python
import functools
import math

import jax
import jax.numpy as jnp
import numpy as np
from jax import lax
from jax.experimental import pallas as pl
from jax.experimental.pallas import tpu as pltpu

F32 = jnp.float32
BF16 = jnp.bfloat16
MESH = pl.DeviceIdType.MESH
VMEM_LIMIT_BYTES = 56 * 1024 * 1024
LANES = 128
PACK_W = 1024
CHUNK_W = 256
PACK_ROW_ALIGN = 256
ROW_TILE = 512
ROW_TILE_HEAVY = 256
ROW_TILE_NARROW = 512
SUM_BLOCK_BYTES = 2 * 1024 * 1024
ADAMW_BLOCK_BYTES = 1024 * 1024
MM_BLOCK_BYTES = 12 * 1024 * 1024

NORM_EPS = 1e-6
N_CHIPS = 4
GMLP_CHUNK = 128
GMLP_GROUPS = 8
S5_GROUPS = 128
S5_GROUP = 16
S5_STATE = 64
S5_SB = 16
S5_SEG = 8
MLA_HEADS = 16
MLA_NOPE = 128
MLA_ROPE = 64
MLA_Q_RANK = 384
MLA_KV_RANK = 128
MLA_SCALE = (MLA_NOPE + MLA_ROPE) ** -0.5
ROPE_THETA = 10000.0
NEG_INF = -1e30
ADAM_LR, ADAM_B1, ADAM_B2, ADAM_EPS, ADAM_WD, ADAM_STEP = 0.001, 0.9, 0.999, 1e-08, 0.01, 10

DN_NN = (((1,), (0,)), ((), ()))
DN_NT = (((1,), (1,)), ((), ()))
DN_TN = (((0,), (0,)), ((), ()))


def _cparams(sem):
    return pltpu.CompilerParams(dimension_semantics=sem, vmem_limit_bytes=VMEM_LIMIT_BYTES)


def _pick(n, cands=(512, 384, 256, 128)):
    for c in cands:
        if n % c == 0:
            return c
    return n


def _pick_rows(r, cap=512, mult=16):
    return max(t for t in range(mult, cap + 1, mult) if r % t == 0)


def _dot(a, b, dn):
    return lax.dot_general(a.astype(BF16), b.astype(BF16), dn, preferred_element_type=F32)


def _sigmoid(x):
    return 0.5 + 0.5 * jnp.tanh(0.5 * x)


def _gelu(x):
    c = math.sqrt(2.0 / math.pi)
    t = jnp.tanh(c * (x + 0.044715 * x * x * x))
    return 0.5 * x * (1.0 + t)


def _gelu_grad(x):
    c = math.sqrt(2.0 / math.pi)
    t = jnp.tanh(c * (x + 0.044715 * x * x * x))
    return 0.5 * (1.0 + t) + 0.5 * x * (1.0 - t * t) * c * (1.0 + 3.0 * 0.044715 * x * x)


def _gelu_both(x):
    c = math.sqrt(2.0 / math.pi)
    t = jnp.tanh(c * (x + 0.044715 * x * x * x))
    return 0.5 * x * (1.0 + t), 0.5 * (1.0 + t) + 0.5 * x * (1.0 - t * t) * c * (1.0 + 3.0 * 0.044715 * x * x)


def _silu_both(z):
    s = _sigmoid(z)
    return z * s, s * (1.0 + z * (1.0 - s))


def _silu(z):
    return z * _sigmoid(z)


def matmul(a, b, mode, name, out_dtype=F32, add=None):
    if mode == "nn":
        (m, k), n = a.shape, b.shape[1]
    elif mode == "nt":
        (m, k), n = a.shape, b.shape[0]
    else:
        (k, m), n = a.shape, b.shape[1]
    tm = _pick(m, [t for t in (2048, 1024, 512, 384, 256, 128) if t * k * a.dtype.itemsize <= MM_BLOCK_BYTES])
    tn = _pick(n, [t for t in (512, 384, 256, 128) if t * k * b.dtype.itemsize <= MM_BLOCK_BYTES])
    dn = {"nn": DN_NN, "nt": DN_NT, "tn": DN_TN}[mode]

    def body(*refs):
        if add is None:
            a_ref, b_ref, o_ref = refs
        else:
            a_ref, b_ref, add_ref, o_ref = refs
        r = _dot(a_ref[...], b_ref[...], dn)
        if add is not None:
            r = r + add_ref[...].astype(F32)
        o_ref[...] = r.astype(out_dtype)

    a_spec = pl.BlockSpec((k, tm), lambda i, j: (0, i)) if mode == "tn" else pl.BlockSpec((tm, k), lambda i, j: (i, 0))
    b_spec = pl.BlockSpec((tn, k), lambda i, j: (j, 0)) if mode == "nt" else pl.BlockSpec((k, tn), lambda i, j: (0, j))
    o_spec = pl.BlockSpec((tm, tn), lambda i, j: (i, j))
    in_specs = [a_spec, b_spec] + ([o_spec] if add is not None else [])
    args = (a, b) + ((add,) if add is not None else ())
    return pl.pallas_call(
        body, name=name, grid=(m // tm, n // tn), in_specs=in_specs, out_specs=o_spec,
        out_shape=jax.ShapeDtypeStruct((m, n), out_dtype),
        compiler_params=_cparams(("parallel", "arbitrary")))(*args)


def _tile_rows(r, r0, cands=(512, 384, 256, 128)):
    return next(t for t in cands if r % t == 0 and r0 % t == 0)


def matmul_tn_packed(a, b, buf, r0, col_sharded, name):
    k, m = a.shape
    n = b.shape[1]
    if col_sharded:
        chunks = n // N_CHIPS // CHUNK_W
        tm = _tile_rows(m, r0, (1024, 512, 384, 256, 128))
        o_map = lambda i, j: (j // chunks, (r0 + (j % chunks) * m) // tm + i, 0)
    else:
        rs = m // N_CHIPS
        tm = _tile_rows(rs, r0)
        per = rs // tm
        o_map = lambda i, j: (i // per, (r0 + j * rs) // tm + i % per, 0)

    def body(a_ref, b_ref, buf_ref, o_ref):
        o_ref[0] = _dot(a_ref[...], b_ref[...], DN_TN)

    return pl.pallas_call(
        body, name=name, grid=(m // tm, n // CHUNK_W),
        in_specs=[pl.BlockSpec((k, tm), lambda i, j: (0, i)), pl.BlockSpec((k, CHUNK_W), lambda i, j: (0, j)),
                  pl.BlockSpec(memory_space=pl.ANY)],
        out_specs=pl.BlockSpec((1, tm, CHUNK_W), o_map), out_shape=jax.ShapeDtypeStruct(buf.shape, buf.dtype),
        input_output_aliases={2: 0}, compiler_params=_cparams(("parallel", "arbitrary")))(a, b, buf)


def _rows(tl, w, col=0):
    return pl.BlockSpec((tl, w), lambda i: (i, col))


def _full(shape):
    nd = len(shape)
    return pl.BlockSpec(tuple(shape), lambda i: (0,) * nd)


def _rowcall(body, name, n_steps, in_specs, out_specs, out_shape, scratch=()):
    return pl.pallas_call(
        body, name=name, grid=(n_steps,), in_specs=in_specs, out_specs=out_specs, out_shape=out_shape,
        scratch_shapes=list(scratch), compiler_params=_cparams(("arbitrary",)))


def _acc(ref, val, i):
    @pl.when(i == 0)
    def _():
        ref[...] = val

    @pl.when(i != 0)
    def _():
        ref[...] += val


def rms_fwd(h, g, name):
    l, d = h.shape
    tl = ROW_TILE_NARROW

    def body(h_ref, g_ref, o_ref):
        x = h_ref[...]
        r = lax.rsqrt(jnp.mean(x * x, axis=-1, keepdims=True) + NORM_EPS)
        o_ref[...] = (x * r * g_ref[...]).astype(BF16)

    return _rowcall(body, name, l // tl, [_rows(tl, d), _full((1, d))], _rows(tl, d),
                    jax.ShapeDtypeStruct((l, d), BF16))(h, g.reshape(1, d))


def rms_matmul(h, g, b, name):
    m, k = h.shape
    n = b.shape[1]
    tm = _pick(m, [t for t in (2048, 1024, 512, 256, 128) if t * k * 4 <= MM_BLOCK_BYTES])
    tn = _pick(n, [t for t in (512, 384, 256, 128) if t * k * b.dtype.itemsize <= MM_BLOCK_BYTES])

    def body(h_ref, g_ref, b_ref, hn_ref, a_ref):
        x = h_ref[...]
        r = lax.rsqrt(jnp.mean(x * x, axis=-1, keepdims=True) + NORM_EPS)
        hn = (x * r * g_ref[...]).astype(BF16)

        @pl.when(pl.program_id(1) == 0)
        def _():
            hn_ref[...] = hn

        a_ref[...] = lax.dot_general(hn, b_ref[...].astype(BF16), DN_NN, preferred_element_type=F32)

    return pl.pallas_call(
        body, name=name, grid=(m // tm, n // tn),
        in_specs=[pl.BlockSpec((tm, k), lambda i, j: (i, 0)), pl.BlockSpec((1, k), lambda i, j: (0, 0)),
                  pl.BlockSpec((k, tn), lambda i, j: (0, j))],
        out_specs=[pl.BlockSpec((tm, k), lambda i, j: (i, 0)), pl.BlockSpec((tm, tn), lambda i, j: (i, j))],
        out_shape=[jax.ShapeDtypeStruct((m, k), BF16), jax.ShapeDtypeStruct((m, n), F32)],
        compiler_params=_cparams(("parallel", "arbitrary")))(h, g.reshape(1, k), b)


def rms_bwd(h, g, dhn, dh_in, name):
    l, d = h.shape
    tl = ROW_TILE_NARROW

    def body(h_ref, g_ref, dhn_ref, dhi_ref, dh_ref, dg_ref):
        i = pl.program_id(0)
        x = h_ref[...]
        r = lax.rsqrt(jnp.mean(x * x, axis=-1, keepdims=True) + NORM_EPS)
        xhat = x * r
        dy = dhn_ref[...]
        dxh = dy * g_ref[...]
        dx = r * (dxh - xhat * jnp.mean(dxh * xhat, axis=-1, keepdims=True))
        dh_ref[...] = dhi_ref[...] + dx
        _acc(dg_ref, jnp.sum(dy * xhat, axis=0, keepdims=True), i)

    return _rowcall(body, name, l // tl, [_rows(tl, d), _full((1, d)), _rows(tl, d), _rows(tl, d)],
                    [_rows(tl, d), _full((1, d))],
                    [jax.ShapeDtypeStruct((l, d), F32), jax.ShapeDtypeStruct((1, d), F32)])(h, g.reshape(1, d), dhn, dh_in)


def loss_head(h, g, target):
    l, d = h.shape
    tl = ROW_TILE_NARROW

    def body(h_ref, g_ref, t_ref, loss_ref, dh_ref, dg_ref):
        i = pl.program_id(0)
        x = h_ref[...]
        gg = g_ref[...]
        r = lax.rsqrt(jnp.mean(x * x, axis=-1, keepdims=True) + NORM_EPS)
        xhat = x * r
        err = xhat * gg - t_ref[...]
        part = 0.5 * jnp.sum(jnp.mean(err * err, axis=-1, keepdims=True), axis=0, keepdims=True)
        _acc(loss_ref, part, i)
        dy = err * (1.0 / d)
        dxh = dy * gg
        dh_ref[...] = r * (dxh - xhat * jnp.mean(dxh * xhat, axis=-1, keepdims=True))
        _acc(dg_ref, jnp.sum(dy * xhat, axis=0, keepdims=True), i)

    return _rowcall(body, "loss_head", l // tl, [_rows(tl, d), _full((1, d)), _rows(tl, d)],
                    [_full((1, 1)), _rows(tl, d), _full((1, d))],
                    [jax.ShapeDtypeStruct((1, 1), F32), jax.ShapeDtypeStruct((l, d), F32),
                     jax.ShapeDtypeStruct((1, d), F32)])(h, g.reshape(1, d), target)


def _gmlp_common(a_ref, lng_ref, lnb_ref):
    di = lng_ref.shape[1]
    u_pre = a_ref[:, 0:di]
    v_pre = a_ref[:, di:2 * di]
    z = a_ref[:, 2 * di:3 * di]
    vg = _gelu(v_pre)
    mu = jnp.mean(vg, axis=-1, keepdims=True)
    xc = vg - mu
    rstd = lax.rsqrt(jnp.mean(xc * xc, axis=-1, keepdims=True) + NORM_EPS)
    vhat = xc * rstd
    vn = vhat * lng_ref[...] + lnb_ref[...]
    return u_pre, v_pre, z, vhat, rstd, vn


def _tril(w):
    r = lax.broadcasted_iota(jnp.int32, w.shape, 0)
    c = lax.broadcasted_iota(jnp.int32, w.shape, 1)
    return jnp.where(c <= r, w, 0.0)


def gmlp_gate_fwd(a, ln_g, ln_b, w_s, b_s, name):
    l, w3 = a.shape
    di = w3 // 3
    dg = di // GMLP_GROUPS
    tl = GMLP_CHUNK

    def body(a_ref, lng_ref, lnb_ref, ws_ref, bs_ref, m_ref):
        u_pre, _, z, _, _, vn = _gmlp_common(a_ref, lng_ref, lnb_ref)
        gate = _gelu(u_pre) * _silu(z)
        for g in range(GMLP_GROUPS):
            sl = slice(g * dg, (g + 1) * dg)
            s = _dot(_tril(ws_ref[g]), vn[:, sl], DN_NN) + bs_ref[g]
            m_ref[:, sl] = (gate[:, sl] * s).astype(BF16)

    return _rowcall(body, name, l // tl,
                    [_rows(tl, w3), _full((1, di)), _full((1, di)), _full(w_s.shape), _full((GMLP_GROUPS, tl, 1))],
                    _rows(tl, di), jax.ShapeDtypeStruct((l, di), BF16))(
        a, ln_g.reshape(1, di), ln_b.reshape(1, di), w_s, b_s.reshape(GMLP_GROUPS, tl, 1))


def gmlp_gate_bwd(a, dm, ln_g, ln_b, w_s, b_s, name):
    l, w3 = a.shape
    di = w3 // 3
    dg = di // GMLP_GROUPS
    tl = GMLP_CHUNK

    def body(a_ref, dm_ref, lng_ref, lnb_ref, ws_ref, bs_ref, da_ref, dlg_ref, dlb_ref, dws_ref, dbs_ref,
             dvn_ref, vh_ref, gv_ref):
        i = pl.program_id(0)
        vg, gv = _gelu_both(a_ref[:, di:2 * di])
        gv_ref[...] = gv
        xc = vg - jnp.mean(vg, axis=-1, keepdims=True)
        rstd = lax.rsqrt(jnp.mean(xc * xc, axis=-1, keepdims=True) + NORM_EPS)
        vh_ref[...] = xc * rstd
        for g in range(GMLP_GROUPS):
            sl = slice(g * dg, (g + 1) * dg)
            wt = _tril(ws_ref[g])
            vn_g = vh_ref[:, sl] * lng_ref[:, sl] + lnb_ref[:, sl]
            s = _dot(wt, vn_g, DN_NN) + bs_ref[g]
            dmg = dm_ref[:, sl]
            u, gu = _gelu_both(a_ref[:, sl])
            sz, gz = _silu_both(a_ref[:, 2 * di + g * dg:2 * di + (g + 1) * dg])
            ds = dmg * u * sz
            da_ref[:, sl] = (dmg * s * sz * gu).astype(BF16)
            da_ref[:, 2 * di + g * dg:2 * di + (g + 1) * dg] = (dmg * u * s * gz).astype(BF16)
            dvn_ref[:, sl] = _dot(wt, ds, DN_TN)
            dw = _tril(_dot(ds, vn_g, DN_NT))
            db = jnp.sum(ds, axis=1, keepdims=True)

            @pl.when(i == 0)
            def _():
                dws_ref[g] = dw
                dbs_ref[g] = db

            @pl.when(i != 0)
            def _():
                dws_ref[g] += dw
                dbs_ref[g] += db

        dvn = dvn_ref[...]
        vhat = vh_ref[...]
        dxh = dvn * lng_ref[...]
        dvg = rstd * (dxh - jnp.mean(dxh, axis=-1, keepdims=True) - vhat * jnp.mean(dxh * vhat, axis=-1, keepdims=True))
        da_ref[:, di:2 * di] = (dvg * gv_ref[...]).astype(BF16)
        _acc(dlg_ref, jnp.sum(dvn * vhat, axis=0, keepdims=True), i)
        _acc(dlb_ref, jnp.sum(dvn, axis=0, keepdims=True), i)

    outs = _rowcall(
        body, name, l // tl,
        [_rows(tl, w3), _rows(tl, di), _full((1, di)), _full((1, di)), _full(w_s.shape), _full((GMLP_GROUPS, tl, 1))],
        [_rows(tl, w3), _full((1, di)), _full((1, di)), _full(w_s.shape), _full((GMLP_GROUPS, tl, 1))],
        [jax.ShapeDtypeStruct((l, w3), BF16), jax.ShapeDtypeStruct((1, di), F32), jax.ShapeDtypeStruct((1, di), F32),
         jax.ShapeDtypeStruct(w_s.shape, F32), jax.ShapeDtypeStruct((GMLP_GROUPS, tl, 1), F32)],
        scratch=[pltpu.VMEM((tl, di), F32)] * 3)(
        a, dm, ln_g.reshape(1, di), ln_b.reshape(1, di), w_s, b_s.reshape(GMLP_GROUPS, tl, 1))
    return outs


def gmlp_layer_fwd(h, p, wf, tag):
    hn, a = rms_matmul(h, p["norm_g"], wf["w_in"], tag + "_mm_in")
    m = gmlp_gate_fwd(a, p["ln_g"], p["ln_b"], p["w_s"], p["b_s"], tag + "_gate")
    h_out = matmul(m, wf["w_out"], "nn", tag + "_mm_out", add=h)
    return h_out, (h, hn, a, m)


def gmlp_layer_bwd(dh_out, saved, p, wf, tag, sink):
    h, hn, a, m = saved
    dm = matmul(dh_out, wf["w_out"], "nt", tag + "_mm_dm")
    sink.mm("w_out", m, dh_out, tag + "_mm_gwout")
    da, dlg, dlb, dws, dbs = gmlp_gate_bwd(a, dm, p["ln_g"], p["ln_b"], p["w_s"], p["b_s"], tag + "_gate_bwd")
    dhn = matmul(da, wf["w_in"], "nt", tag + "_mm_dhn")
    sink.mm("w_in", hn, da, tag + "_mm_gwin")
    zero = sink.send()
    dh, dng = rms_bwd(h, p["norm_g"] + zero, dhn, dh_out, tag + "_rms_bwd")
    grads = {"norm_g": dng.reshape(-1), "ln_g": dlg.reshape(-1), "ln_b": dlb.reshape(-1),
             "w_s": dws, "b_s": dbs.reshape(GMLP_GROUPS, GMLP_CHUNK)}
    return dh, grads


def _cmul(ar, ai, br, bi):
    return ar * br - ai * bi, ar * bi + ai * br


S5_PG = 16


def _gblock(tail):
    return pl.BlockSpec((S5_PG,) + tuple(tail), lambda i: (i, 0, 0))


def s5_params_fwd(a_re, a_im, log_step, b_re, b_im):
    g, p, hh = b_re.shape

    def body(ar_ref, ai_ref, ls_ref, br_ref, bi_ref, lr_ref, li_ref, bbr_ref, bbi_ref):
        ar, ai = ar_ref[...], ai_ref[...]
        step = jnp.exp(ls_ref[...])
        mag = jnp.exp(ar * step)
        lr, li = mag * jnp.cos(ai * step), mag * jnp.sin(ai * step)
        den = 1.0 / (ar * ar + ai * ai)
        fr, fi = _cmul(lr - 1.0, li, ar * den, -ai * den)
        lr_ref[...] = lr
        li_ref[...] = li
        bbr, bbi = _cmul(fr, fi, br_ref[...], bi_ref[...])
        bbr_ref[...] = bbr
        bbi_ref[...] = bbi

    s1 = jax.ShapeDtypeStruct((g, p, 1), F32)
    s3 = jax.ShapeDtypeStruct((g, p, hh), F32)
    b1, b0, b3 = _gblock((p, 1)), _gblock((1, 1)), _gblock((p, hh))
    return pl.pallas_call(body, name="s5_params_fwd", grid=(g // S5_PG,), in_specs=[b1, b1, b0, b3, b3],
                          out_specs=[b1, b1, b3, b3], out_shape=[s1, s1, s3, s3],
                          compiler_params=_cparams(("parallel",)))(
        a_re.reshape(g, p, 1), a_im.reshape(g, p, 1), log_step.reshape(g, 1, 1), b_re, b_im)


def s5_params_bwd(a_re, a_im, log_step, b_re, b_im, dl_re, dl_im, dbb_re, dbb_im):
    g, p, hh = b_re.shape

    def body(ar_ref, ai_ref, ls_ref, br_ref, bi_ref, dlr_ref, dli_ref, dbr_ref, dbi_ref,
             gar_ref, gai_ref, gls_ref, gbr_ref, gbi_ref):
        ar, ai = ar_ref[...], ai_ref[...]
        step = jnp.exp(ls_ref[...])
        mag = jnp.exp(ar * step)
        lr, li = mag * jnp.cos(ai * step), mag * jnp.sin(ai * step)
        den = 1.0 / (ar * ar + ai * ai)
        ir, ii = ar * den, -ai * den
        fr, fi = _cmul(lr - 1.0, li, ir, ii)
        br, bi = br_ref[...], bi_ref[...]
        dbr, dbi = dbr_ref[...], dbi_ref[...]
        gbr, gbi = _cmul(fr, -fi, dbr, dbi)
        gbr_ref[...] = gbr
        gbi_ref[...] = gbi
        pr, pi = _cmul(br, -bi, dbr, dbi)
        gfr = jnp.sum(pr, axis=-1, keepdims=True)
        gfi = jnp.sum(pi, axis=-1, keepdims=True)
        t_r, t_i = _cmul(ir, -ii, gfr, gfi)
        glr, gli = dlr_ref[...] + t_r, dli_ref[...] + t_i
        c1r, c1i = _cmul(step * lr, -step * li, glr, gli)
        qr, qi = _cmul(fr, fi, ir, ii)
        c2r, c2i = _cmul(-qr, qi, gfr, gfi)
        gar_ref[...] = c1r + c2r
        gai_ref[...] = c1i + c2i
        wr, wi = _cmul(ar, ai, lr, li)
        sr, _ = _cmul(wr, -wi, glr, gli)
        gls_ref[...] = jnp.sum(sr, axis=1, keepdims=True) * step

    s1 = jax.ShapeDtypeStruct((g, p, 1), F32)
    s3 = jax.ShapeDtypeStruct((g, p, hh), F32)
    b1, b0, b3 = _gblock((p, 1)), _gblock((1, 1)), _gblock((p, hh))
    return pl.pallas_call(body, name="s5_params_bwd", grid=(g // S5_PG,),
                          in_specs=[b1, b1, b0, b3, b3, b1, b1, b3, b3], out_specs=[b1, b1, b0, b3, b3],
                          out_shape=[s1, s1, jax.ShapeDtypeStruct((g, 1, 1), F32), s3, s3],
                          compiler_params=_cparams(("parallel",)))(
        a_re.reshape(g, p, 1), a_im.reshape(g, p, 1), log_step.reshape(g, 1, 1), b_re, b_im,
        dl_re, dl_im, dbb_re, dbb_im)


def _blockdiag(t):
    sb, n, r, c = t.shape
    eye = jnp.eye(n, dtype=bool)[None, :, None, :, None]
    full = jnp.where(eye, t[:, :, :, None, :], jnp.zeros((), t.dtype))
    return full.reshape(sb, n * r, n * c)


def _blockdiag_extract(m, r, c):
    sb = m.shape[0]
    n = m.shape[1] // r
    m5 = m.reshape(sb, n, r, n, c)
    return jnp.stack([m5[:, i, :, i, :] for i in range(n)], axis=1)


S5_TB = 256
S5_UNROLL = 8


def _lam_power(pr, pi, n):
    for _ in range(int(math.log2(n))):
        pr, pi = _cmul(pr, pi, pr, pi)
    return pr, pi


def _segment_entries(er, ei, pr, pi, reverse):
    seg, ns = er.shape
    row = lax.broadcasted_iota(jnp.int32, (seg, ns), 0)
    cr = jnp.zeros((seg, ns), F32)
    ci = jnp.zeros((seg, ns), F32)
    cur_r = jnp.zeros((1, ns), F32)
    cur_i = jnp.zeros((1, ns), F32)
    for s in (range(seg - 2, -1, -1) if reverse else range(1, seg)):
        src = s + 1 if reverse else s - 1
        mr, mi = _cmul(pr, pi, cur_r, cur_i)
        cur_r = jnp.sum(jnp.where(row == src, er, 0.0), axis=0, keepdims=True) + mr
        cur_i = jnp.sum(jnp.where(row == src, ei, 0.0), axis=0, keepdims=True) + mi
        cr = jnp.where(row == s, cur_r, cr)
        ci = jnp.where(row == s, cur_i, ci)
    return cr, ci


def s5_scan_fused_fwd(a_p, lam_re, lam_im, wb_re, wb_im, wc_re, wc_im, d_skip, name):
    l = a_p.shape[0]
    di = d_skip.shape[1]
    rows = S5_SEG * S5_TB
    nb = l // rows
    ns = wb_re.shape[2]

    def body(u_ref, lr_ref, li_ref, wbr_ref, wbi_ref, wcr_ref, wci_ref, ds_ref, y_ref, yg_ref, ckr_ref, cki_ref,
             bur, bui):
        lr = jnp.broadcast_to(lr_ref[0], (S5_SEG, ns))
        li = jnp.broadcast_to(li_ref[0], (S5_SEG, ns))

        def scan_block(b, carry, keep):
            def step(t, c):
                xr, xi = c
                sl = pl.ds(pl.multiple_of(b * rows + t * S5_SEG, S5_SEG), S5_SEG)
                nr = lr * xr - li * xi + bur[sl, :]
                ni = lr * xi + li * xr + bui[sl, :]
                if keep:
                    bur[sl, :] = nr
                    bui[sl, :] = ni
                return nr, ni

            return lax.fori_loop(0, S5_TB, step, carry, unroll=S5_UNROLL)

        def project(b, carry):
            rs = pl.ds(pl.multiple_of(b * rows, rows), rows)
            u = u_ref[rs, :]
            bur[rs, :] = _dot(u, wbr_ref[0], DN_NN)
            bui[rs, :] = _dot(u, wbi_ref[0], DN_NN)
            return scan_block(b, carry, False)

        zero = jnp.zeros((S5_SEG, ns), F32)
        er, ei = lax.fori_loop(0, nb, project, (zero, zero))
        pr, pi = _lam_power(lr_ref[0], li_ref[0], l // S5_SEG)
        entry = _segment_entries(er, ei, pr, pi, False)

        def emit(b, carry):
            ckr_ref[0, b] = carry[0]
            cki_ref[0, b] = carry[1]
            carry = scan_block(b, carry, True)
            rs = pl.ds(pl.multiple_of(b * rows, rows), rows)
            y = (_dot(bur[rs, :], wcr_ref[0], DN_NN) - _dot(bui[rs, :], wci_ref[0], DN_NN)
                 + ds_ref[...] * u_ref[rs, :])
            y_ref[rs, :] = y
            yg_ref[rs, :] = _gelu(y).astype(BF16)
            return carry

        lax.fori_loop(0, nb, emit, entry)

    sb3 = lambda s: (s, 0, 0)
    st = jax.ShapeDtypeStruct
    return pl.pallas_call(
        body, name=name, grid=(S5_SB,),
        in_specs=[pl.BlockSpec((l, LANES), lambda s: (0, s)),
                  pl.BlockSpec((1, 1, ns), sb3), pl.BlockSpec((1, 1, ns), sb3),
                  pl.BlockSpec((1, LANES, ns), sb3), pl.BlockSpec((1, LANES, ns), sb3),
                  pl.BlockSpec((1, ns, LANES), sb3), pl.BlockSpec((1, ns, LANES), sb3),
                  pl.BlockSpec((1, LANES), lambda s: (0, s))],
        out_specs=[pl.BlockSpec((l, LANES), lambda s: (0, s)), pl.BlockSpec((l, LANES), lambda s: (0, s)),
                   pl.BlockSpec((1, nb, S5_SEG, ns), lambda s: (s, 0, 0, 0)),
                   pl.BlockSpec((1, nb, S5_SEG, ns), lambda s: (s, 0, 0, 0))],
        out_shape=[st((l, di), F32), st((l, di), BF16),
                   st((S5_SB, nb, S5_SEG, ns), F32), st((S5_SB, nb, S5_SEG, ns), F32)],
        scratch_shapes=[pltpu.VMEM((l, ns), F32), pltpu.VMEM((l, ns), F32)],
        compiler_params=_cparams(("parallel",)))(a_p, lam_re, lam_im, wb_re, wb_im, wc_re, wc_im, d_skip)


def s5_scan_fused_bwd(a_p, dy, lam_re, lam_im, wb_re, wb_im, wc_re, wc_im, d_skip, ck_re, ck_im, name):
    l = a_p.shape[0]
    di = d_skip.shape[1]
    rows = S5_SEG * S5_TB
    nb = l // rows
    ns = wb_re.shape[2]

    def body(u_ref, dy_ref, lr_ref, li_ref, wbr_ref, wbi_ref, wcr_ref, wci_ref, ds_ref, ckr_ref, cki_ref,
             du_ref, dwbr_ref, dwbi_ref, dwcr_ref, dwci_ref, dds_ref, dlr_ref, dli_ref, gr, gi, xr_b, xi_b):
        lr = jnp.broadcast_to(lr_ref[0], (S5_SEG, ns))
        li = jnp.broadcast_to(li_ref[0], (S5_SEG, ns))

        def back_project(k, carry):
            b = nb - 1 - k
            rs = pl.ds(pl.multiple_of(b * rows, rows), rows)
            dyv = dy_ref[rs, :]
            gr[rs, :] = _dot(dyv, wcr_ref[0], DN_NT)
            gi[rs, :] = -_dot(dyv, wci_ref[0], DN_NT)

            def step(kk, c):
                ar, ai = c
                sl = pl.ds(pl.multiple_of(b * rows + (S5_TB - 1 - kk) * S5_SEG, S5_SEG), S5_SEG)
                return gr[sl, :] + lr * ar + li * ai, gi[sl, :] + lr * ai - li * ar

            return lax.fori_loop(0, S5_TB, step, carry, unroll=S5_UNROLL)

        zero = jnp.zeros((S5_SEG, ns), F32)
        er, ei = lax.fori_loop(0, nb, back_project, (zero, zero))
        pr, pi = _lam_power(lr_ref[0], -li_ref[0], l // S5_SEG)
        a0r, a0i = _segment_entries(er, ei, pr, pi, True)

        dwbr_ref[...] = jnp.zeros_like(dwbr_ref)
        dwbi_ref[...] = jnp.zeros_like(dwbi_ref)
        dwcr_ref[...] = jnp.zeros_like(dwcr_ref)
        dwci_ref[...] = jnp.zeros_like(dwci_ref)
        dds_ref[...] = jnp.zeros_like(dds_ref)

        def block(k, carry):
            b = nb - 1 - k
            rs = pl.ds(pl.multiple_of(b * rows, rows), rows)
            u = u_ref[rs, :]
            dyv = dy_ref[rs, :]
            body_rows = pl.ds(S5_SEG, rows)
            x0r, x0i = ckr_ref[0, b], cki_ref[0, b]
            xr_b[0:S5_SEG, :] = x0r
            xi_b[0:S5_SEG, :] = x0i
            xr_b[body_rows, :] = _dot(u, wbr_ref[0], DN_NN)
            xi_b[body_rows, :] = _dot(u, wbi_ref[0], DN_NN)

            def fstep(t, c):
                xr, xi = c
                sl = pl.ds(pl.multiple_of((t + 1) * S5_SEG, S5_SEG), S5_SEG)
                nr = lr * xr - li * xi + xr_b[sl, :]
                ni = lr * xi + li * xr + xi_b[sl, :]
                xr_b[sl, :] = nr
                xi_b[sl, :] = ni
                return nr, ni

            lax.fori_loop(0, S5_TB, fstep, (x0r, x0i), unroll=S5_UNROLL)
            dwcr_ref[0] += _dot(xr_b[body_rows, :], dyv, DN_TN)
            dwci_ref[0] -= _dot(xi_b[body_rows, :], dyv, DN_TN)

            def bstep(kk, c):
                ar, ai = c
                sl = pl.ds(pl.multiple_of(b * rows + (S5_TB - 1 - kk) * S5_SEG, S5_SEG), S5_SEG)
                nr = gr[sl, :] + lr * ar + li * ai
                ni = gi[sl, :] + lr * ai - li * ar
                gr[sl, :] = nr
                gi[sl, :] = ni
                return nr, ni

            ar, ai = lax.fori_loop(0, S5_TB, bstep, carry[:2], unroll=S5_UNROLL)
            a_r, a_i = gr[rs, :], gi[rs, :]
            p_r, p_i = xr_b[0:rows, :], xi_b[0:rows, :]
            per_seg = lambda v: jnp.sum(v.reshape(S5_TB, S5_SEG, ns), axis=0)
            carry = (ar, ai, carry[2] + per_seg(a_r * p_r + a_i * p_i), carry[3] + per_seg(a_i * p_r - a_r * p_i))
            du_ref[rs, :] = (_dot(a_r, wbr_ref[0], DN_NT) + _dot(a_i, wbi_ref[0], DN_NT) + ds_ref[...] * dyv).astype(BF16)
            dwbr_ref[0] += _dot(u, a_r, DN_TN)
            dwbi_ref[0] += _dot(u, a_i, DN_TN)
            dds_ref[...] += jnp.sum(dyv * u, axis=0, keepdims=True)
            return carry

        _, _, dlr, dli = lax.fori_loop(0, nb, block, (a0r, a0i, zero, zero))
        dlr_ref[0] = dlr
        dli_ref[0] = dli

    sb3 = lambda s: (s, 0, 0)
    seq = pl.BlockSpec((l, LANES), lambda s: (0, s))
    ck = pl.BlockSpec((1, nb, S5_SEG, ns), lambda s: (s, 0, 0, 0))
    st = jax.ShapeDtypeStruct
    return pl.pallas_call(
        body, name=name, grid=(S5_SB,),
        in_specs=[seq, seq, pl.BlockSpec((1, 1, ns), sb3), pl.BlockSpec((1, 1, ns), sb3),
                  pl.BlockSpec((1, LANES, ns), sb3), pl.BlockSpec((1, LANES, ns), sb3),
                  pl.BlockSpec((1, ns, LANES), sb3), pl.BlockSpec((1, ns, LANES), sb3),
                  pl.BlockSpec((1, LANES), lambda s: (0, s)), ck, ck],
        out_specs=[seq, pl.BlockSpec((1, LANES, ns), sb3), pl.BlockSpec((1, LANES, ns), sb3),
                   pl.BlockSpec((1, ns, LANES), sb3), pl.BlockSpec((1, ns, LANES), sb3),
                   pl.BlockSpec((1, LANES), lambda s: (0, s)),
                   pl.BlockSpec((1, S5_SEG, ns), sb3), pl.BlockSpec((1, S5_SEG, ns), sb3)],
        out_shape=[st((l, di), BF16), st((S5_SB, LANES, ns), F32), st((S5_SB, LANES, ns), F32),
                   st((S5_SB, ns, LANES), F32), st((S5_SB, ns, LANES), F32), st((1, di), F32),
                   st((S5_SB, S5_SEG, ns), F32), st((S5_SB, S5_SEG, ns), F32)],
        scratch_shapes=[pltpu.VMEM((l, ns), F32), pltpu.VMEM((l, ns), F32),
                        pltpu.VMEM((rows + S5_SEG, ns), F32), pltpu.VMEM((rows + S5_SEG, ns), F32)],
        compiler_params=_cparams(("parallel",)))(
        a_p, dy, lam_re, lam_im, wb_re, wb_im, wc_re, wc_im, d_skip, ck_re, ck_im)


def s5_gate_fwd(y, t, b_glu, a_p, name):
    l, d = y.shape
    tl = ROW_TILE

    def body(y_ref, t_ref, b_ref, z_ref, m_ref):
        yg = _gelu(y_ref[...])
        m_ref[...] = (yg * _sigmoid(t_ref[...] + b_ref[...]) * _silu(z_ref[...])).astype(BF16)

    return _rowcall(body, name, l // tl, [_rows(tl, d), _rows(tl, d), _full((1, d)), _rows(tl, d, 1)], _rows(tl, d),
                    jax.ShapeDtypeStruct((l, d), BF16))(y, t, b_glu.reshape(1, d), a_p)


def s5_gate_bwd(dm, y, t, b_glu, a_p, name):
    l, d = y.shape
    tl = ROW_TILE_HEAVY

    def body(dm_ref, y_ref, t_ref, b_ref, z_ref, dt_ref, dyg_ref, dz_ref, db_ref):
        i = pl.program_id(0)
        dmv = dm_ref[...]
        z = z_ref[...]
        yg = _gelu(y_ref[...])
        sg = _sigmoid(t_ref[...] + b_ref[...])
        y2 = yg * sg
        sz, gz = _silu_both(z)
        dy2 = dmv * sz
        dz_ref[...] = (dmv * y2 * gz).astype(BF16)
        dyg_ref[...] = dy2 * sg
        dt = dy2 * yg * sg * (1.0 - sg)
        dt_ref[...] = dt.astype(BF16)
        _acc(db_ref, jnp.sum(dt, axis=0, keepdims=True), i)

    st = jax.ShapeDtypeStruct
    return _rowcall(body, name, l // tl, [_rows(tl, d), _rows(tl, d), _rows(tl, d), _full((1, d)), _rows(tl, d, 1)],
                    [_rows(tl, d), _rows(tl, d), _rows(tl, d), _full((1, d))],
                    [st((l, d), BF16), st((l, d), F32), st((l, d), BF16), st((1, d), F32)])(
        dm, y, t, b_glu.reshape(1, d), a_p)


def s5_act_bwd(y, dyg_a, dyg_b, name):
    l, d = y.shape
    tl = ROW_TILE

    def body(y_ref, a_ref, b_ref, o_ref):
        o_ref[...] = (a_ref[...] + b_ref[...]) * _gelu_grad(y_ref[...])

    return _rowcall(body, name, l // tl, [_rows(tl, d)] * 3, _rows(tl, d), jax.ShapeDtypeStruct((l, d), F32))(y, dyg_a, dyg_b)


def _seg_perm(t):
    l, d = t.shape
    return t.reshape(S5_SEG, l // S5_SEG, d).transpose(1, 0, 2).reshape(l, d)


def _seg_unperm(t):
    l, d = t.shape
    return t.reshape(l // S5_SEG, S5_SEG, d).transpose(1, 0, 2).reshape(l, d)


def _s5_weights(p):
    lr, li, bbr, bbi = s5_params_fwd(p["a_re"], p["a_im"], p["log_step"], p["b_re"], p["b_im"])
    ns = 8 * S5_STATE
    lam_re = lr.reshape(S5_SB, 1, ns)
    lam_im = li.reshape(S5_SB, 1, ns)
    to_bd = lambda t: _blockdiag(t.reshape(S5_SB, 8, t.shape[1], t.shape[2]))
    wb_re = to_bd(bbr.transpose(0, 2, 1)).astype(BF16)
    wb_im = to_bd(bbi.transpose(0, 2, 1)).astype(BF16)
    wc_re = to_bd(p["c_re"].transpose(0, 2, 1)).astype(BF16)
    wc_im = to_bd(p["c_im"].transpose(0, 2, 1)).astype(BF16)
    return lam_re, lam_im, wb_re, wb_im, wc_re, wc_im


def s5_layer_fwd(h, p, wf, sw, tag):
    l = h.shape[0]
    di = p["d_skip"].shape[0]
    hn = rms_fwd(h, p["norm_g"], tag + "_rms")
    hn_p = _seg_perm(hn)
    a_p = matmul(hn_p, wf["w_in"], "nn", tag + "_mm_in")
    dsk = p["d_skip"].reshape(1, di)
    y, yg, ck_re, ck_im = s5_scan_fused_fwd(a_p, *sw, dsk, tag + "_scan")
    t = matmul(yg, wf["w_glu"], "nn", tag + "_mm_glu")
    m = s5_gate_fwd(y, t, p["b_glu"], a_p, tag + "_gate")
    out_p = matmul(m, wf["w_out"], "nn", tag + "_mm_out")
    h_out = residual_add(h, _seg_unperm(out_p), tag + "_res")
    return h_out, (h, hn_p, a_p, sw, ck_re, ck_im, y, yg, t, m)


def residual_add(h, y, name):
    l, d = h.shape
    tl = ROW_TILE_NARROW

    def body(h_ref, y_ref, o_ref):
        o_ref[...] = h_ref[...] + y_ref[...]

    return _rowcall(body, name, l // tl, [_rows(tl, d)] * 2, _rows(tl, d), jax.ShapeDtypeStruct((l, d), F32))(h, y)


def s5_layer_bwd(dh_out, saved, p, wf, tag, sink):
    h, hn_p, a_p, sw, ck_re, ck_im, y, yg, t, m = saved
    l = h.shape[0]
    di = p["d_skip"].shape[0]
    dsk = p["d_skip"].reshape(1, di)
    dout_p = _seg_perm(dh_out)
    dm = matmul(dout_p, wf["w_out"], "nt", tag + "_mm_dm")
    sink.mm("w_out", m, dout_p, tag + "_mm_gwout")
    dt, dyg_a, dz, db_glu = s5_gate_bwd(dm, y, t, p["b_glu"], a_p, tag + "_gate_bwd")
    dyg_b = matmul(dt, wf["w_glu"], "nt", tag + "_mm_dyg")
    sink.mm("w_glu", yg, dt, tag + "_mm_gwglu")
    dy = s5_act_bwd(y, dyg_a, dyg_b, tag + "_act_bwd")
    du, dwbr, dwbi, dwcr, dwci, dds, dlr, dli = s5_scan_fused_bwd(a_p, dy, *sw, dsk, ck_re, ck_im, tag + "_scanb")
    da = jnp.concatenate([du, dz], axis=1)
    dhn_p = matmul(da, wf["w_in"], "nt", tag + "_mm_dhn")
    sink.mm("w_in", hn_p, da, tag + "_mm_gwin")
    zero = sink.send()
    dh, dng = rms_bwd(h, p["norm_g"] + zero, _seg_unperm(dhn_p), dh_out, tag + "_rms_bwd")
    ex = lambda m_, r, c: _blockdiag_extract(m_, r, c).reshape(S5_GROUPS, r, c).transpose(0, 2, 1)
    dbb_re, dbb_im = ex(dwbr, S5_GROUP, S5_STATE), ex(dwbi, S5_GROUP, S5_STATE)
    g_c_re, g_c_im = ex(dwcr, S5_STATE, S5_GROUP), ex(dwci, S5_STATE, S5_GROUP)
    dl_re = lane_sum8(dlr).reshape(S5_GROUPS, S5_STATE, 1)
    dl_im = lane_sum8(dli).reshape(S5_GROUPS, S5_STATE, 1)
    gar, gai, gls, gbr, gbi = s5_params_bwd(p["a_re"], p["a_im"], p["log_step"], p["b_re"], p["b_im"],
                                            dl_re, dl_im, dbb_re, dbb_im)
    grads = {"norm_g": dng.reshape(-1), "a_re": gar.reshape(S5_GROUPS, S5_STATE),
             "a_im": gai.reshape(S5_GROUPS, S5_STATE), "log_step": gls.reshape(-1), "b_re": gbr, "b_im": gbi,
             "c_re": g_c_re, "c_im": g_c_im, "d_skip": dds.reshape(-1), "b_glu": db_glu.reshape(-1)}
    return dh, grads


def lane_sum8(t):
    sb, seg, ns = t.shape

    def body(t_ref, o_ref):
        o_ref[...] = jnp.sum(t_ref[...], axis=1, keepdims=True)

    return pl.pallas_call(body, name="s5_seg_sum", out_shape=jax.ShapeDtypeStruct((sb, 1, ns), F32))(t)


MLA_DI = MLA_HEADS * 128
MLA_CQ0 = MLA_DI
MLA_CKV0 = MLA_CQ0 + MLA_Q_RANK
MLA_KR0 = MLA_CKV0 + MLA_KV_RANK
MLA_AW = MLA_KR0 + LANES


def _rot_half(x):
    w = x.shape[-1]
    lane = lax.broadcasted_iota(jnp.int32, x.shape, x.ndim - 1)
    return jnp.where(lane % MLA_ROPE < MLA_ROPE // 2, pltpu.roll(x, w - MLA_ROPE // 2, x.ndim - 1),
                     pltpu.roll(x, MLA_ROPE // 2, x.ndim - 1))


def rope_tables(pos, zero):
    l = pos.shape[0]
    tl = ROW_TILE
    j = np.arange(LANES) % MLA_ROPE % (MLA_ROPE // 2)
    inv_freq = (ROPE_THETA ** (-(2.0 * j) / MLA_ROPE)).astype(np.float32).reshape(1, LANES)
    sign = np.where(np.arange(LANES) % MLA_ROPE < MLA_ROPE // 2, -1.0, 1.0).astype(np.float32).reshape(1, LANES)

    def body(p_ref, f_ref, s_ref, cos_ref, sin_ref):
        ang = p_ref[...].astype(F32) * f_ref[...]
        cos_ref[...] = jnp.cos(ang)
        sin_ref[...] = jnp.sin(ang) * s_ref[...]

    st = jax.ShapeDtypeStruct((l, LANES), F32)
    return _rowcall(body, "rope_tables", l // tl, [_rows(tl, 1), _full((1, LANES)), _full((1, LANES))],
                    [_rows(tl, LANES)] * 2, [st, st])(pos, jnp.asarray(inv_freq), jnp.asarray(sign) + zero)


def _rope(x, cos, sins):
    return x * cos + _rot_half(x) * sins


def _rope_t(dy, cos, sins):
    return dy * cos - sins * _rot_half(dy)


def _rmsn(x):
    r = lax.rsqrt(jnp.mean(x * x, axis=-1, keepdims=True) + NORM_EPS)
    return x * r, r


def mla_pre(a, q_g, kv_g, cos, sins, name):
    l = a.shape[0]
    tl = ROW_TILE

    def body(a_ref, qg_ref, kg_ref, cos_ref, sin_ref, cq_ref, ckv_ref, krs_ref):
        xq, _ = _rmsn(a_ref[:, MLA_CQ0:MLA_CKV0])
        cq_ref[...] = (xq * qg_ref[...]).astype(BF16)
        xk, _ = _rmsn(a_ref[:, MLA_CKV0:MLA_KR0])
        ckv_ref[...] = (xk * kg_ref[...]).astype(BF16)
        kr = a_ref[:, MLA_KR0:MLA_AW]
        kr2 = kr + pltpu.roll(kr, MLA_ROPE, 1)
        kr2 = _rope(kr2, cos_ref[...], sin_ref[...])
        lane = lax.broadcasted_iota(jnp.int32, kr2.shape, 1)
        krs_ref[0] = jnp.where(lane < MLA_ROPE, kr2, 0.0).astype(BF16)
        krs_ref[1] = jnp.where(lane >= MLA_ROPE, kr2, 0.0).astype(BF16)

    st = jax.ShapeDtypeStruct
    return _rowcall(body, name, l // tl,
                    [_rows(tl, MLA_AW), _full((1, MLA_Q_RANK)), _full((1, MLA_KV_RANK)), _rows(tl, LANES), _rows(tl, LANES)],
                    [_rows(tl, MLA_Q_RANK), _rows(tl, MLA_KV_RANK), pl.BlockSpec((2, tl, LANES), lambda i: (0, i, 0))],
                    [st((l, MLA_Q_RANK), BF16), st((l, MLA_KV_RANK), BF16), st((2, l, LANES), BF16)])(
        a, q_g.reshape(1, -1), kv_g.reshape(1, -1), cos, sins)


def mla_rope_q(qr, cos, sins, name):
    l, w = qr.shape
    tl = ROW_TILE

    def body(q_ref, cos_ref, sin_ref, o_ref):
        c, s = cos_ref[...], sin_ref[...]
        for p in range(w // LANES):
            sl = slice(p * LANES, (p + 1) * LANES)
            o_ref[:, sl] = _rope(q_ref[:, sl], c, s).astype(BF16)

    return _rowcall(body, name, l // tl, [_rows(tl, w), _rows(tl, LANES), _rows(tl, LANES)], _rows(tl, w),
                    jax.ShapeDtypeStruct((l, w), BF16))(qr, cos, sins)


ATT_OUT = 512
ATT_IN = 512
ATT_R = ATT_OUT // ATT_IN


def _scores(qn, qr, kn, kr, mask_off, transposed):
    q2 = jnp.concatenate([qn, qr], axis=1)
    k2 = jnp.concatenate([kn, kr], axis=1)
    s = (_dot(k2, q2, DN_NT) if transposed else _dot(q2, k2, DN_NT)) * MLA_SCALE
    if mask_off is None:
        return s
    r = lax.broadcasted_iota(jnp.int32, s.shape, 0)
    c = lax.broadcasted_iota(jnp.int32, s.shape, 1)
    return jnp.where((r <= c + mask_off) if transposed else (c + mask_off <= r), s, NEG_INF)


def _fold(x, op):
    out = x[:, :LANES]
    for t in range(1, x.shape[1] // LANES):
        out = op(out, x[:, t * LANES:(t + 1) * LANES])
    return out


def flash_fwd(qn, qr, kv, krs, name):
    l = qn.shape[0]
    nq = l // ATT_OUT

    def body(qn_ref, qr_ref, kv_ref, kr_ref, o_ref, lse_ref, s_buf):
        qi = pl.program_id(1)
        q_r = qr_ref[...]
        q_n = [qn_ref[:, hh * LANES:(hh + 1) * LANES] for hh in range(2)]

        def block_scores(j, mx, mask_off):
            sl = pl.ds(pl.multiple_of(j * ATT_IN, ATT_IN), ATT_IN)
            out = []
            for hh in range(2):
                s = _scores(q_n[hh], q_r, kv_ref[sl, 2 * hh * LANES:(2 * hh + 1) * LANES], kr_ref[hh, sl, :],
                            mask_off, False)
                s_buf[hh, j] = s
                out.append(jnp.maximum(mx[hh], _fold(s, jnp.maximum)))
            return tuple(out)

        ninf = jnp.full((ATT_OUT, LANES), NEG_INF, F32)
        mx = lax.fori_loop(0, ATT_R * qi, lambda j, c: block_scores(j, c, None), (ninf, ninf))
        for d in range(ATT_R):
            mx = block_scores(ATT_R * qi + d, mx, d * ATT_IN)
        m = [jnp.max(mx[hh], axis=-1, keepdims=True) for hh in range(2)]

        def block_pv(j, carry):
            sl = pl.ds(pl.multiple_of(j * ATT_IN, ATT_IN), ATT_IN)
            out = []
            for hh in range(2):
                ls, acc = carry[hh]
                p = jnp.exp(s_buf[hh, j] - m[hh])
                out.append((ls + _fold(p, jnp.add),
                            acc + _dot(p, kv_ref[sl, (2 * hh + 1) * LANES:(2 * hh + 2) * LANES], DN_NN)))
            return tuple(out)

        z = jnp.zeros((ATT_OUT, LANES), F32)
        res = lax.fori_loop(0, ATT_R * (qi + 1), block_pv, ((z, z), (z, z)))
        for hh in range(2):
            lsum = jnp.sum(res[hh][0], axis=-1, keepdims=True)
            o_ref[:, hh * LANES:(hh + 1) * LANES] = res[hh][1] / lsum
            lse_ref[hh] = m[hh] + jnp.log(lsum)

    st = jax.ShapeDtypeStruct
    return pl.pallas_call(
        body, name=name, grid=(MLA_HEADS // 2, nq),
        in_specs=[pl.BlockSpec((ATT_OUT, 2 * LANES), lambda p, i: (i, p)),
                  pl.BlockSpec((ATT_OUT, LANES), lambda p, i: (i, p)),
                  pl.BlockSpec((l, 4 * LANES), lambda p, i: (0, p)),
                  pl.BlockSpec((2, l, LANES), lambda p, i: (0, 0, 0))],
        out_specs=[pl.BlockSpec((ATT_OUT, 2 * LANES), lambda p, i: (i, p)),
                   pl.BlockSpec((2, ATT_OUT, 1), lambda p, i: (p, i, 0))],
        out_shape=[st((l, MLA_DI), F32), st((MLA_HEADS, l, 1), F32)],
        scratch_shapes=[pltpu.VMEM((2, l // ATT_IN, ATT_OUT, ATT_IN), F32)],
        compiler_params=_cparams(("parallel", "arbitrary")))(qn, qr, kv, krs)


def flash_dkv(qn, qr, kv, krs, do, lse_row, delta_row, name):
    l = qn.shape[0]
    nk = l // ATT_OUT
    nq = l // ATT_IN

    def body(qn_ref, qr_ref, do_ref, lse_ref, dl_ref, kv_ref, kr_ref, dkv_ref, dkr_ref):
        kj = pl.program_id(1)
        lane = lax.broadcasted_iota(jnp.int32, (ATT_OUT, LANES), 1)
        kn = [kv_ref[:, 2 * hh * LANES:(2 * hh + 1) * LANES] for hh in range(2)]
        v = [kv_ref[:, (2 * hh + 1) * LANES:(2 * hh + 2) * LANES] for hh in range(2)]

        def block(i, carry, mask_off):
            sl = pl.ds(pl.multiple_of(i * ATT_IN, ATT_IN), ATT_IN)
            q_r = qr_ref[sl, :]
            out = []
            for hh in range(2):
                dk2, dv = carry[hh]
                hs = slice(hh * LANES, (hh + 1) * LANES)
                q_n, d_o = qn_ref[sl, hs], do_ref[sl, hs]
                s = _scores(q_n, q_r, kn[hh], kr_ref[hh], mask_off, True)
                pt = jnp.exp(s - lse_ref[hh, i])
                dv = dv + _dot(pt, d_o, DN_NN)
                dpt = _dot(v[hh], d_o, DN_NT)
                dst = (pt * (dpt - dl_ref[hh, i]) * MLA_SCALE).astype(BF16)
                out.append((dk2 + _dot(dst, jnp.concatenate([q_n, q_r], axis=1), DN_NN), dv))
            return tuple(out)

        z = jnp.zeros((ATT_OUT, LANES), F32)
        z2 = jnp.zeros((ATT_OUT, 2 * LANES), F32)
        res = ((z2, z), (z2, z))
        for d in range(ATT_R):
            res = block(ATT_R * kj + d, res, d * ATT_IN)
        res = lax.fori_loop(ATT_R * (kj + 1), nq, lambda i, c: block(i, c, None), res)
        for hh in range(2):
            dkv_ref[:, 2 * hh * LANES:(2 * hh + 1) * LANES] = res[hh][0][:, :LANES].astype(BF16)
            dkv_ref[:, (2 * hh + 1) * LANES:(2 * hh + 2) * LANES] = res[hh][1].astype(BF16)
        dkr_ref[0] = jnp.where(lane < MLA_ROPE, res[0][0][:, LANES:], res[1][0][:, LANES:])

    st = jax.ShapeDtypeStruct
    return pl.pallas_call(
        body, name=name, grid=(MLA_HEADS // 2, nk),
        in_specs=[pl.BlockSpec((l, 2 * LANES), lambda p, j: (0, p)),
                  pl.BlockSpec((l, LANES), lambda p, j: (0, p)),
                  pl.BlockSpec((l, 2 * LANES), lambda p, j: (0, p)),
                  pl.BlockSpec((2, nq, 1, ATT_IN), lambda p, j: (p, 0, 0, 0)),
                  pl.BlockSpec((2, nq, 1, ATT_IN), lambda p, j: (p, 0, 0, 0)),
                  pl.BlockSpec((ATT_OUT, 4 * LANES), lambda p, j: (j, p)),
                  pl.BlockSpec((2, ATT_OUT, LANES), lambda p, j: (0, j, 0))],
        out_specs=[pl.BlockSpec((ATT_OUT, 4 * LANES), lambda p, j: (j, p)),
                   pl.BlockSpec((1, ATT_OUT, LANES), lambda p, j: (p, j, 0))],
        out_shape=[st((l, 2 * MLA_DI), BF16), st((MLA_HEADS // 2, l, LANES), F32)],
        compiler_params=_cparams(("parallel", "arbitrary")))(qn, qr, do, lse_row, delta_row, kv, krs)


def flash_dq(qn, qr, kv, krs, do, lse, delta, cos, sins, name):
    l = qn.shape[0]
    nq = l // ATT_OUT

    def body(qn_ref, qr_ref, do_ref, lse_ref, dl_ref, kv_ref, kr_ref, cos_ref, sin_ref, dqn_ref, dqr_ref):
        qi = pl.program_id(1)
        q_r = qr_ref[...]
        q_n = [qn_ref[:, hh * LANES:(hh + 1) * LANES] for hh in range(2)]
        d_o = [do_ref[:, hh * LANES:(hh + 1) * LANES] for hh in range(2)]
        lse_h = [lse_ref[hh] for hh in range(2)]
        dl_h = [dl_ref[hh] for hh in range(2)]

        def block(j, carry, mask_off):
            sl = pl.ds(pl.multiple_of(j * ATT_IN, ATT_IN), ATT_IN)
            dq2 = list(carry)
            for hh in range(2):
                kn = kv_ref[sl, 2 * hh * LANES:(2 * hh + 1) * LANES]
                v = kv_ref[sl, (2 * hh + 1) * LANES:(2 * hh + 2) * LANES]
                kr = kr_ref[hh, sl, :]
                s = _scores(q_n[hh], q_r, kn, kr, mask_off, False)
                pr = jnp.exp(s - lse_h[hh])
                dp = _dot(d_o[hh], v, DN_NT)
                ds = (pr * (dp - dl_h[hh]) * MLA_SCALE).astype(BF16)
                dq2[hh] = dq2[hh] + _dot(ds, jnp.concatenate([kn, kr], axis=1), DN_NN)
            return tuple(dq2)

        z2 = jnp.zeros((ATT_OUT, 2 * LANES), F32)
        res = lax.fori_loop(0, ATT_R * qi, lambda j, c: block(j, c, None), (z2, z2))
        for d in range(ATT_R):
            res = block(ATT_R * qi + d, res, d * ATT_IN)
        dqn_ref[:, 0:LANES] = res[0][:, :LANES].astype(BF16)
        dqn_ref[:, LANES:2 * LANES] = res[1][:, :LANES].astype(BF16)
        dqr = res[0][:, LANES:] + res[1][:, LANES:]
        dqr_ref[...] = _rope_t(dqr, cos_ref[...], sin_ref[...]).astype(BF16)

    st = jax.ShapeDtypeStruct
    return pl.pallas_call(
        body, name=name, grid=(MLA_HEADS // 2, nq),
        in_specs=[pl.BlockSpec((ATT_OUT, 2 * LANES), lambda p, i: (i, p)),
                  pl.BlockSpec((ATT_OUT, LANES), lambda p, i: (i, p)),
                  pl.BlockSpec((ATT_OUT, 2 * LANES), lambda p, i: (i, p)),
                  pl.BlockSpec((2, ATT_OUT, 1), lambda p, i: (p, i, 0)),
                  pl.BlockSpec((2, ATT_OUT, 1), lambda p, i: (p, i, 0)),
                  pl.BlockSpec((l, 4 * LANES), lambda p, i: (0, p)),
                  pl.BlockSpec((2, l, LANES), lambda p, i: (0, 0, 0)),
                  pl.BlockSpec((ATT_OUT, LANES), lambda p, i: (i, 0)),
                  pl.BlockSpec((ATT_OUT, LANES), lambda p, i: (i, 0))],
        out_specs=[pl.BlockSpec((ATT_OUT, 2 * LANES), lambda p, i: (i, p)),
                   pl.BlockSpec((ATT_OUT, LANES), lambda p, i: (i, p))],
        out_shape=[st((l, MLA_DI), BF16), st((l, MLA_HEADS * MLA_ROPE), BF16)],
        compiler_params=_cparams(("parallel", "arbitrary")))(qn, qr, do, lse, delta, kv, krs, cos, sins)


def mla_gate_fwd(o, a, name):
    l = o.shape[0]
    tl = ROW_TILE

    def body(o_ref, z_ref, m_ref):
        m_ref[...] = (o_ref[...] * _silu(z_ref[...])).astype(BF16)

    return _rowcall(body, name, l // tl, [_rows(tl, MLA_DI), _rows(tl, MLA_DI)], _rows(tl, MLA_DI),
                    jax.ShapeDtypeStruct((l, MLA_DI), BF16))(o, a)


def mla_gate_bwd(dm, o, a, name):
    l = o.shape[0]
    tl = ROW_TILE

    def body(dm_ref, o_ref, z_ref, do_ref, dz_ref, dl_ref):
        dmv, ov, z = dm_ref[...], o_ref[...], z_ref[...]
        sz, gz = _silu_both(z)
        d_o = dmv * sz
        do_ref[...] = d_o.astype(BF16)
        dz_ref[...] = (dmv * ov * gz).astype(BF16)
        pr = d_o * ov
        for h in range(MLA_HEADS):
            dl_ref[h] = jnp.sum(pr[:, h * LANES:(h + 1) * LANES], axis=1, keepdims=True)

    st = jax.ShapeDtypeStruct
    return _rowcall(body, name, l // tl, [_rows(tl, MLA_DI)] * 3,
                    [_rows(tl, MLA_DI), _rows(tl, MLA_DI), pl.BlockSpec((MLA_HEADS, tl, 1), lambda i: (0, i, 0))],
                    [st((l, MLA_DI), BF16), st((l, MLA_DI), BF16), st((MLA_HEADS, l, 1), F32)])(dm, o, a)


def mla_post(a, dcqn, dckvn, dkr_pairs, dz, q_g, kv_g, cos, sins, name):
    l = a.shape[0]
    tl = ROW_TILE
    npair = MLA_HEADS // 2

    def norm_bwd(x, g, dy):
        xhat, r = _rmsn(x)
        dxh = dy * g
        return r * (dxh - xhat * jnp.mean(dxh * xhat, axis=-1, keepdims=True)), jnp.sum(dy * xhat, axis=0, keepdims=True)

    def body(a_ref, dq_ref, dk_ref, dkr_ref, dz_ref, qg_ref, kg_ref, cos_ref, sin_ref, da_ref, dqg_ref, dkg_ref):
        i = pl.program_id(0)
        da_ref[:, 0:MLA_DI] = dz_ref[...]
        dcq, dqg = norm_bwd(a_ref[:, MLA_CQ0:MLA_CKV0], qg_ref[...], dq_ref[...])
        da_ref[:, MLA_CQ0:MLA_CKV0] = dcq.astype(BF16)
        dckv, dkg = norm_bwd(a_ref[:, MLA_CKV0:MLA_KR0], kg_ref[...], dk_ref[...])
        da_ref[:, MLA_CKV0:MLA_KR0] = dckv.astype(BF16)
        dk2 = dkr_ref[0]
        for p in range(1, npair):
            dk2 = dk2 + dkr_ref[p]
        dk2 = _rope_t(dk2, cos_ref[...], sin_ref[...])
        dk2 = dk2 + pltpu.roll(dk2, MLA_ROPE, 1)
        lane = lax.broadcasted_iota(jnp.int32, dk2.shape, 1)
        da_ref[:, MLA_KR0:MLA_AW] = jnp.where(lane < MLA_ROPE, dk2, 0.0).astype(BF16)
        _acc(dqg_ref, dqg, i)
        _acc(dkg_ref, dkg, i)

    st = jax.ShapeDtypeStruct
    return _rowcall(body, name, l // tl,
                    [_rows(tl, MLA_AW), _rows(tl, MLA_Q_RANK), _rows(tl, MLA_KV_RANK),
                     pl.BlockSpec((npair, tl, LANES), lambda i: (0, i, 0)), _rows(tl, MLA_DI),
                     _full((1, MLA_Q_RANK)), _full((1, MLA_KV_RANK)), _rows(tl, LANES), _rows(tl, LANES)],
                    [_rows(tl, MLA_AW), _full((1, MLA_Q_RANK)), _full((1, MLA_KV_RANK))],
                    [st((l, MLA_AW), BF16), st((1, MLA_Q_RANK), F32), st((1, MLA_KV_RANK), F32)])(
        a, dcqn, dckvn, dkr_pairs, dz, q_g.reshape(1, -1), kv_g.reshape(1, -1), cos, sins)


def _mla_w_in_perm(w):
    r = MLA_Q_RANK + MLA_KV_RANK + MLA_ROPE
    pad = jnp.zeros(w.shape[:-1] + (MLA_AW - MLA_KR0 - MLA_ROPE,), w.dtype)
    return jnp.concatenate([w[..., r:], w[..., :r], pad], axis=-1)


def _mla_w_in_unperm(g):
    r = MLA_Q_RANK + MLA_KV_RANK + MLA_ROPE
    return jnp.concatenate([g[..., MLA_DI:MLA_DI + r], g[..., :MLA_DI]], axis=-1)


def _mla_w_uq_split(w):
    k = w.shape[0]
    w3 = w.reshape(k, MLA_HEADS, MLA_NOPE + MLA_ROPE)
    return w3[:, :, :MLA_NOPE].reshape(k, MLA_HEADS * MLA_NOPE), w3[:, :, MLA_NOPE:].reshape(k, MLA_HEADS * MLA_ROPE)


def _mla_w_uq_merge(gn, gr):
    k = gn.shape[0]
    return jnp.concatenate([gn.reshape(k, MLA_HEADS, MLA_NOPE), gr.reshape(k, MLA_HEADS, MLA_ROPE)], axis=2).reshape(k, -1)


def mla_layer_fwd(h, p, wf, cos, sins, tag):
    w_in = _mla_w_in_perm(wf["w_in"])
    w_uq_n, w_uq_r = _mla_w_uq_split(wf["w_uq"])
    hn, a = rms_matmul(h, p["norm_g"], w_in, tag + "_mm_in")
    cqn, ckvn, krs = mla_pre(a, p["q_norm_g"], p["kv_norm_g"], cos, sins, tag + "_pre")
    qn = matmul(cqn, w_uq_n, "nn", tag + "_mm_qn", out_dtype=BF16)
    qr_raw = matmul(cqn, w_uq_r, "nn", tag + "_mm_qr")
    qr = mla_rope_q(qr_raw, cos, sins, tag + "_rope_q")
    kv = matmul(ckvn, wf["w_ukv"], "nn", tag + "_mm_kv", out_dtype=BF16)
    o, lse = flash_fwd(qn, qr, kv, krs, tag + "_flash")
    m = mla_gate_fwd(o, a, tag + "_gate")
    h_out = matmul(m, wf["w_out"], "nn", tag + "_mm_out", add=h)
    return h_out, (h, hn, a, cqn, ckvn, krs, qn, qr, kv, o, lse, m, w_in, w_uq_n, w_uq_r)


def mla_layer_bwd(dh_out, saved, p, wf, cos, sins, tag, sink):
    h, hn, a, cqn, ckvn, krs, qn, qr, kv, o, lse, m, w_in, w_uq_n, w_uq_r = saved
    l = h.shape[0]
    dm = matmul(dh_out, wf["w_out"], "nt", tag + "_mm_dm")
    sink.mm("w_out", m, dh_out, tag + "_mm_gwout")
    do, dz, delta = mla_gate_bwd(dm, o, a, tag + "_gate_bwd")
    lse_row = lse.reshape(MLA_HEADS, l // ATT_IN, 1, ATT_IN)
    delta_row = delta.reshape(MLA_HEADS, l // ATT_IN, 1, ATT_IN)
    dkv, dkr_pairs = flash_dkv(qn, qr, kv, krs, do, lse_row, delta_row, tag + "_flash_dkv")
    dqn, dqr = flash_dq(qn, qr, kv, krs, do, lse, delta, cos, sins, tag + "_flash_dq")
    dcqn = matmul(dqn, w_uq_n, "nt", tag + "_mm_dcq_n")
    dcqn = matmul(dqr, w_uq_r, "nt", tag + "_mm_dcq_r", add=dcqn)
    g_uq_n = matmul(cqn, dqn, "tn", tag + "_mm_guq_n")
    g_uq_r = matmul(cqn, dqr, "tn", tag + "_mm_guq_r")
    dckvn = matmul(dkv, wf["w_ukv"], "nt", tag + "_mm_dckv")
    sink.mm("w_ukv", ckvn, dkv, tag + "_mm_gukv")
    da, dqg, dkg = mla_post(a, dcqn, dckvn, dkr_pairs, dz, p["q_norm_g"], p["kv_norm_g"], cos, sins, tag + "_post")
    dhn = matmul(da, w_in, "nt", tag + "_mm_dhn")
    g_w_in = matmul(hn, da, "tn", tag + "_mm_gwin")
    sink.put("w_uq", _mla_w_uq_merge(g_uq_n, g_uq_r))
    sink.put("w_in", _mla_w_in_unperm(g_w_in))
    zero = sink.send()
    dh, dng = rms_bwd(h, p["norm_g"] + zero, dhn, dh_out, tag + "_rms_bwd")
    grads = {"norm_g": dng.reshape(-1), "q_norm_g": dqg.reshape(-1), "kv_norm_g": dkg.reshape(-1)}
    return dh, grads


ANY = pl.BlockSpec(memory_space=pl.ANY)


def _me():
    return lax.axis_index("x"), lax.axis_index("y"), lax.axis_index("c")


def _chip():
    return 2 * lax.axis_index("x") + lax.axis_index("y")


def _other_chips(x, y):
    return [(1 - x, y), (x, 1 - y), (1 - x, 1 - y)]


def _rcopy(src, dst, ssem, rsem, dev):
    return pltpu.make_async_remote_copy(src_ref=src, dst_ref=dst, send_sem=ssem, recv_sem=rsem,
                                        device_id=dev, device_id_type=MESH)


def _half(ref, c, hf):
    return ref.at[pl.ds(c * hf, hf), :]


HBM = pl.BlockSpec(memory_space=pltpu.HBM)
SEM = pl.BlockSpec(memory_space=pltpu.SEMAPHORE)
SPLIT_EFFECT = pltpu.SideEffectType.DATAFLOW_SIDE_EFFECTING


def gather_start(wb, after, name):
    nr, w = wb.shape
    hf = nr // 2

    def body(w_ref, land_ref, after_ref, ssem, rsem, w_thru, land_thru, token):
        x, y, c = _me()
        k = 2 * x + y
        for j, (cx, cy) in enumerate(_other_chips(x, y)):
            _rcopy(_half(w_ref, c, hf), _half(land_ref.at[k], c, hf), ssem.at[j], rsem.at[j], (cx, cy, c)).start()
        token[...] = jnp.zeros_like(token)

    land = lax.empty((N_CHIPS, nr, w), wb.dtype)
    return pl.pallas_call(
        body, name=name,
        out_shape=(pltpu.SemaphoreType.DMA((3,)), pltpu.SemaphoreType.DMA((3,)), pltpu.HBM(wb.shape, wb.dtype),
                   pltpu.HBM(land.shape, land.dtype), jax.ShapeDtypeStruct((8, LANES), F32)),
        in_specs=(HBM, HBM, ANY), out_specs=(SEM, SEM, HBM, HBM, pl.BlockSpec(memory_space=pltpu.VMEM)),
        input_output_aliases={0: 2, 1: 3},
        compiler_params=pltpu.CompilerParams(has_side_effects=SPLIT_EFFECT))(
        pltpu.with_memory_space_constraint(wb, pltpu.HBM), pltpu.with_memory_space_constraint(land, pltpu.HBM), after)


def gather_wait(ssem, rsem, w_thru, land_thru, after, name):
    nr, w = w_thru.shape
    hf = nr // 2

    def body(w_ref, land_ref, ssem_ref, rsem_ref, after_ref, w_dead, got_ref):
        x, y, c = _me()
        for j, (cx, cy) in enumerate(_other_chips(x, y)):
            cp = _rcopy(_half(w_ref, c, hf), _half(land_ref.at[2 * cx + cy], c, hf), ssem_ref.at[j], rsem_ref.at[j],
                        (cx, cy, c))
            cp.wait_send()
            cp.wait_recv()

    return pl.pallas_call(
        body, name=name, out_shape=(pltpu.HBM(w_thru.shape, w_thru.dtype), pltpu.HBM(land_thru.shape, land_thru.dtype)),
        in_specs=(HBM, HBM, SEM, SEM, ANY), out_specs=(HBM, HBM), input_output_aliases={0: 0, 1: 1},
        compiler_params=pltpu.CompilerParams(has_side_effects=SPLIT_EFFECT))(w_thru, land_thru, ssem, rsem, after)[1]


def gather_handover(land, wb, name):
    _, nr, w = land.shape
    hf = nr // 2

    def body(l_ref, o_ref, ssem, rsem):
        x, y, c = _me()
        chips = _other_chips(x, y)
        sends = []
        for j, (cx, cy) in enumerate(chips):
            region = _half(o_ref.at[2 * cx + cy], c, hf)
            sends.append(_rcopy(region, region, ssem.at[j], rsem.at[j], (x, y, 1 - c)))
            sends[-1].start()
        for j, (cx, cy) in enumerate(chips):
            region = _half(o_ref.at[2 * cx + cy], 1 - c, hf)
            _rcopy(region, region, ssem.at[j], rsem.at[j], (x, y, 1 - c)).wait_recv()
        for cp in sends:
            cp.wait_send()

    out = pl.pallas_call(
        body, name=name, in_specs=[ANY], out_specs=ANY, input_output_aliases={0: 0},
        out_shape=jax.ShapeDtypeStruct(land.shape, land.dtype),
        scratch_shapes=[pltpu.SemaphoreType.DMA((3,)), pltpu.SemaphoreType.DMA((3,))])(land)
    return lax.dynamic_update_slice(out, wb[None], (_chip(), 0, 0))


def reduce_start(t, after, name):
    def body(t_ref, land_ref, after_ref, ssem, rsem, t_thru, land_thru, token):
        x, y, c = _me()
        k = 2 * x + y
        for j, (cx, cy) in enumerate(_other_chips(x, y)):
            _rcopy(t_ref.at[2 * cx + cy], land_ref.at[k], ssem.at[j], rsem.at[j], (cx, cy, c)).start()
        token[...] = jnp.zeros_like(token)

    land = lax.empty(t.shape, t.dtype)
    return pl.pallas_call(
        body, name=name,
        out_shape=(pltpu.SemaphoreType.DMA((3,)), pltpu.SemaphoreType.DMA((3,)), pltpu.HBM(t.shape, t.dtype),
                   pltpu.HBM(t.shape, t.dtype), jax.ShapeDtypeStruct((8, LANES), F32)),
        in_specs=(HBM, HBM, ANY), out_specs=(SEM, SEM, HBM, HBM, pl.BlockSpec(memory_space=pltpu.VMEM)),
        input_output_aliases={0: 2, 1: 3},
        compiler_params=pltpu.CompilerParams(has_side_effects=SPLIT_EFFECT))(
        pltpu.with_memory_space_constraint(t, pltpu.HBM), pltpu.with_memory_space_constraint(land, pltpu.HBM), after)


def bcast_start(g, after, name):
    def body(g_ref, land_ref, after_ref, ssem, rsem, g_thru, land_thru, token):
        x, y, c = _me()
        k = 2 * x + y
        for j, (cx, cy) in enumerate(_other_chips(x, y)):
            _rcopy(g_ref, land_ref.at[k], ssem.at[j], rsem.at[j], (cx, cy, c)).start()
        token[...] = jnp.zeros_like(token)

    land = lax.empty((N_CHIPS,) + g.shape, g.dtype)
    return pl.pallas_call(
        body, name=name,
        out_shape=(pltpu.SemaphoreType.DMA((3,)), pltpu.SemaphoreType.DMA((3,)), pltpu.HBM(g.shape, g.dtype),
                   pltpu.HBM(land.shape, land.dtype), jax.ShapeDtypeStruct((8, LANES), F32)),
        in_specs=(HBM, HBM, ANY), out_specs=(SEM, SEM, HBM, HBM, pl.BlockSpec(memory_space=pltpu.VMEM)),
        input_output_aliases={0: 2, 1: 3},
        compiler_params=pltpu.CompilerParams(has_side_effects=SPLIT_EFFECT))(
        pltpu.with_memory_space_constraint(g, pltpu.HBM), pltpu.with_memory_space_constraint(land, pltpu.HBM), after)


def bcast_wait(ssem, rsem, g_thru, land_thru, after, name):
    def body(g_ref, land_ref, ssem_ref, rsem_ref, after_ref, g_out, got_ref):
        x, y, c = _me()
        for j, (cx, cy) in enumerate(_other_chips(x, y)):
            cp = _rcopy(g_ref, land_ref.at[2 * cx + cy], ssem_ref.at[j], rsem_ref.at[j], (cx, cy, c))
            cp.wait_send()
            cp.wait_recv()

    g, land = pl.pallas_call(
        body, name=name, out_shape=(pltpu.HBM(g_thru.shape, g_thru.dtype), pltpu.HBM(land_thru.shape, land_thru.dtype)),
        in_specs=(HBM, HBM, SEM, SEM, ANY), out_specs=(HBM, HBM), input_output_aliases={0: 0, 1: 1},
        compiler_params=pltpu.CompilerParams(has_side_effects=SPLIT_EFFECT))(g_thru, land_thru, ssem, rsem, after)
    return lax.dynamic_update_slice(land, g[None], (_chip(), 0, 0))


def reduce_wait(ssem, rsem, t_thru, land_thru, after, name):
    def body(t_ref, land_ref, ssem_ref, rsem_ref, after_ref, t_out, got_ref):
        x, y, c = _me()
        k = 2 * x + y
        for j, (cx, cy) in enumerate(_other_chips(x, y)):
            cp = _rcopy(t_ref.at[k], land_ref.at[2 * cx + cy], ssem_ref.at[j], rsem_ref.at[j], (cx, cy, c))
            cp.wait_send()
            cp.wait_recv()

    return pl.pallas_call(
        body, name=name, out_shape=(pltpu.HBM(t_thru.shape, t_thru.dtype), pltpu.HBM(land_thru.shape, land_thru.dtype)),
        in_specs=(HBM, HBM, SEM, SEM, ANY), out_specs=(HBM, HBM), input_output_aliases={0: 0, 1: 1},
        compiler_params=pltpu.CompilerParams(has_side_effects=SPLIT_EFFECT))(t_thru, land_thru, ssem, rsem, after)


def grads_to_sibling(ps, name="grads_to_sibling"):
    n = len(ps)

    def body(*refs):
        p_refs, o_refs, ssem, rsem = refs[:n], refs[n:2 * n], refs[2 * n], refs[2 * n + 1]
        x, y, c = _me()
        cps = []
        for a in range(n):
            hf = ps[a].shape[1] // 2
            cps.append(_rcopy(p_refs[a].at[:, pl.ds((1 - c) * hf, hf), :], o_refs[a], ssem.at[a], rsem.at[a],
                              (x, y, 1 - c)))
        for cp in cps:
            cp.start()
        for cp in cps:
            cp.wait()

    return pl.pallas_call(
        body, name=name, in_specs=[ANY] * n, out_specs=[ANY] * n,
        out_shape=[jax.ShapeDtypeStruct((N_CHIPS, p.shape[1] // 2, p.shape[2]), p.dtype) for p in ps],
        scratch_shapes=[pltpu.SemaphoreType.DMA((n,)), pltpu.SemaphoreType.DMA((n,))])(*ps)


def pair_sum(p, ra, out_dtype, name):
    _, nr, w = p.shape
    hf = nr // 2
    tr = _pick_rows(hf, cap=max(512, 2 * SUM_BLOCK_BYTES // (4 * w)))
    nb = hf // tr

    def body(c_ref, p_ref, r_ref, o_ref):
        o_ref[...] = (p_ref[...] + r_ref[...]).astype(out_dtype)

    c = lax.axis_index("c").astype(jnp.int32).reshape(1)
    return pl.pallas_call(
        body, name=name,
        grid_spec=pltpu.PrefetchScalarGridSpec(
            num_scalar_prefetch=1, grid=(N_CHIPS, nb),
            in_specs=[pl.BlockSpec((1, tr, w), lambda k, i, c_ref: (k, c_ref[0] * nb + i, 0)),
                      pl.BlockSpec((1, tr, w), lambda k, i, c_ref: (k, i, 0))],
            out_specs=pl.BlockSpec((1, tr, w), lambda k, i, c_ref: (k, i, 0))),
        out_shape=jax.ShapeDtypeStruct((N_CHIPS, hf, w), out_dtype),
        compiler_params=_cparams(("parallel", "parallel")))(c, p, ra)


def grads_across_chips(ts):
    n = len(ts)

    def body(*refs):
        t_refs, o_refs, ssem, rsem = refs[:n], refs[n:2 * n], refs[2 * n], refs[2 * n + 1]
        x, y, c = _me()
        k = 2 * x + y
        chips = _other_chips(x, y)
        sends = [_rcopy(t_refs[a].at[2 * cx + cy], o_refs[a].at[k], ssem.at[3 * a + j], rsem.at[3 * a + j], (cx, cy, c))
                 for a in range(n) for j, (cx, cy) in enumerate(chips)]
        for cp in sends:
            cp.start()
        for a in range(n):
            for j, (cx, cy) in enumerate(chips):
                _rcopy(t_refs[a].at[k], o_refs[a].at[2 * cx + cy], ssem.at[3 * a + j], rsem.at[3 * a + j],
                       (cx, cy, c)).wait_recv()
        for cp in sends:
            cp.wait_send()

    return pl.pallas_call(
        body, name="grads_across_chips", in_specs=[ANY] * n, out_specs=[ANY] * n,
        out_shape=[jax.ShapeDtypeStruct(t.shape, t.dtype) for t in ts],
        scratch_shapes=[pltpu.SemaphoreType.DMA((3 * n,)), pltpu.SemaphoreType.DMA((3 * n,))])(*ts)


def chip_sum(t, rb, name):
    _, hf, w = rb.shape
    tr = _pick_rows(hf, cap=max(512, SUM_BLOCK_BYTES // (4 * w)))
    nb = hf // tr

    def body(kc_ref, t_ref, r_ref, o_ref):
        k = kc_ref[0]
        acc = jnp.where(k == 0, t_ref[0], r_ref[0]).astype(F32)
        for j in range(1, N_CHIPS):
            acc = acc + jnp.where(k == j, t_ref[0], r_ref[j]).astype(F32)
        o_ref[...] = acc

    kc = jnp.stack([_chip(), lax.axis_index("c")]).astype(jnp.int32)
    return pl.pallas_call(
        body, name=name,
        grid_spec=pltpu.PrefetchScalarGridSpec(
            num_scalar_prefetch=1, grid=(nb,),
            in_specs=[pl.BlockSpec((1, tr, w), lambda i, kc_ref: (kc_ref[0], i, 0)),
                      pl.BlockSpec((N_CHIPS, tr, w), lambda i, kc_ref: (0, i, 0))],
            out_specs=pl.BlockSpec((tr, w), lambda i, kc_ref: (kc_ref[1] * nb + i, 0))),
        out_shape=jax.ShapeDtypeStruct((2 * hf, w), F32), compiler_params=_cparams(("parallel",)))(kc, t, rb)


def reduced_to_sibling(gs):
    n = len(gs)

    def body(*refs):
        o_refs, ssem, rsem = refs[n:2 * n], refs[2 * n], refs[2 * n + 1]
        x, y, c = _me()
        cps = []
        for a in range(n):
            hf = gs[a].shape[0] // 2
            cps.append(_rcopy(_half(o_refs[a], c, hf), _half(o_refs[a], c, hf), ssem.at[a], rsem.at[a], (x, y, 1 - c)))
        for cp in cps:
            cp.start()
        for a in range(n):
            hf = gs[a].shape[0] // 2
            _rcopy(_half(o_refs[a], c, hf), _half(o_refs[a], 1 - c, hf), ssem.at[a], rsem.at[a],
                   (x, y, 1 - c)).wait_recv()
        for cp in cps:
            cp.wait_send()

    return pl.pallas_call(
        body, name="reduced_to_sibling", in_specs=[ANY] * n, out_specs=[ANY] * n,
        input_output_aliases={a: a for a in range(n)},
        out_shape=[jax.ShapeDtypeStruct(g.shape, g.dtype) for g in gs],
        scratch_shapes=[pltpu.SemaphoreType.DMA((n,)), pltpu.SemaphoreType.DMA((n,))])(*gs)


def _adamw_step(w_ref, g_ref, m_ref, v_ref, d_ref, nm_ref, nv_ref):
    bc1 = 1.0 - ADAM_B1 ** ADAM_STEP
    bc2 = 1.0 - ADAM_B2 ** ADAM_STEP
    gv = g_ref[...]
    nm = ADAM_B1 * m_ref[...] + (1.0 - ADAM_B1) * gv
    nv = ADAM_B2 * v_ref[...] + (1.0 - ADAM_B2) * (gv * gv)
    nm_ref[...] = nm
    nv_ref[...] = nv
    d_ref[...] = -ADAM_LR * ((nm / bc1) / (jnp.sqrt(nv / bc2) + ADAM_EPS) + ADAM_WD * w_ref[...])


def adamw_packed(w, g_buf, r0, m, v, name):
    r, c = w.shape
    tr = _tile_rows(r, r0, (1024, 512, 384, 256, 128))

    def body(w_ref, g_ref, m_ref, v_ref, go_ref, d_ref, nm_ref, nv_ref):
        go_ref[...] = g_ref[...]
        _adamw_step(w_ref, g_ref, m_ref, v_ref, d_ref, nm_ref, nv_ref)

    own = pl.BlockSpec((tr, CHUNK_W), lambda i, j: (i, j))
    packed = pl.BlockSpec((tr, CHUNK_W), lambda i, j: ((r0 + j * r) // tr + i, 0))
    st = jax.ShapeDtypeStruct((r, c), F32)
    return pl.pallas_call(body, name=name, grid=(r // tr, c // CHUNK_W), in_specs=[own, packed, own, own],
                          out_specs=[own] * 4, out_shape=[st] * 4,
                          compiler_params=_cparams(("parallel", "parallel")))(w, g_buf, m, v)


def adamw(w, g, m, v, name):
    r, wd = w.shape
    tr = _pick_rows(r, cap=max(16, ADAMW_BLOCK_BYTES // (4 * wd)))
    body = functools.partial(_adamw_step)

    spec = pl.BlockSpec((tr, wd), lambda i: (i, 0))
    st = jax.ShapeDtypeStruct((r, wd), F32)
    return pl.pallas_call(body, name=name, grid=(r // tr,), in_specs=[spec] * 4, out_specs=[spec] * 3,
                          out_shape=[st, st, st], compiler_params=_cparams(("parallel",)))(w, g, m, v)


LAYER_KINDS = ("gmlp", "s5", "mla", "gmlp")
PARAMS = {
    "gmlp": ("norm_g", "w_in", "ln_g", "ln_b", "w_s", "b_s", "w_out"),
    "s5": ("norm_g", "w_in", "a_re", "a_im", "log_step", "b_re", "b_im", "c_re", "c_im", "d_skip", "w_glu", "b_glu", "w_out"),
    "mla": ("norm_g", "w_in", "q_norm_g", "w_uq", "kv_norm_g", "w_ukv", "w_out"),
}
COL_SHARDED = ("w_in", "w_uq", "w_ukv")
ROW_SHARDED = ("w_out", "w_glu")
WEIGHT_NAMES = [("l%d_" % i) + n for i, kind in enumerate(LAYER_KINDS) for n in PARAMS[kind]] + ["final_norm_g"]


def _is_big(name):
    return name.split("_", 1)[1] in COL_SHARDED + ROW_SHARDED


BIG = [n for n in WEIGHT_NAMES if _is_big(n)]
SMALL = [n for n in WEIGHT_NAMES if not _is_big(n)]


def _pack_rows(blocks):
    return jnp.concatenate([b.reshape(-1, PACK_W) for b in blocks], axis=0)


def _shard_major(wn, full, width):
    r, c = full.shape
    if wn in COL_SHARDED:
        t = full.reshape(r, N_CHIPS, c // N_CHIPS).transpose(1, 0, 2)
    else:
        t = full.reshape(N_CHIPS, r // N_CHIPS, c)
    return t.reshape(N_CHIPS, -1, width)


def _from_shard_major(name, t, block_shape):
    r, c = block_shape
    if name.split("_", 1)[1] in COL_SHARDED:
        return t.reshape(N_CHIPS, r, c).transpose(1, 0, 2).reshape(r, N_CHIPS * c)
    return t.reshape(N_CHIPS * r, c)


class BigGradSink:
    ORDER = ("w_out", "w_glu", "w_ukv", "w_uq", "w_in")
    ROW_MAJOR = {2: ("w_uq", "w_in")}

    def __init__(self, layer, block_shapes):
        self.layer = layer
        self.regions = {}
        r0 = 0
        for wn in self.ORDER:
            if wn in block_shapes:
                shape = block_shapes[wn]
                self.regions[wn] = (r0, shape, wn not in self.ROW_MAJOR.get(layer, ()))
                r0 += shape[0] * shape[1] // CHUNK_W
        self.buf = lax.empty((N_CHIPS, r0, CHUNK_W), F32)
        self.flight = None

    def mm(self, wn, a, b, name):
        r0, _, direct = self.regions[wn]
        assert direct
        self.buf = matmul_tn_packed(a, b, self.buf, r0, wn in COL_SHARDED, name)

    def put(self, wn, full):
        r0, _, direct = self.regions[wn]
        assert not direct
        piece = _shard_major(wn, full, CHUNK_W)
        self.buf = lax.dynamic_update_slice(self.buf, piece, (0, r0, 0))

    def send(self):
        i = self.layer
        sib, = grads_to_sibling([self.buf], "grads_to_sibling_l%d" % i)
        t = pair_sum(self.buf, sib, BF16, "pair_sum_l%d" % i)
        self.flight = reduce_start(t, sib, "reduce_l%d_start" % i)
        return self.flight[4][0, 0]


def _small_pack(arrs, total_padded):
    flat = jnp.concatenate([a.reshape(-1) for a in arrs])
    return jnp.pad(flat, (0, total_padded - flat.shape[0]))


def kernel(x, positions, l0_norm_g, l0_w_in, l0_ln_g, l0_ln_b, l0_w_s, l0_b_s, l0_w_out, l1_norm_g, l1_w_in, l1_a_re, l1_a_im, l1_log_step, l1_b_re, l1_b_im, l1_c_re, l1_c_im, l1_d_skip, l1_w_glu, l1_b_glu, l1_w_out, l2_norm_g, l2_w_in, l2_q_norm_g, l2_w_uq, l2_kv_norm_g, l2_w_ukv, l2_w_out, l3_norm_g, l3_w_in, l3_ln_g, l3_ln_b, l3_w_s, l3_b_s, l3_w_out, final_norm_g, loss_target, m_l0_norm_g, m_l0_w_in, m_l0_ln_g, m_l0_ln_b, m_l0_w_s, m_l0_b_s, m_l0_w_out, m_l1_norm_g, m_l1_w_in, m_l1_a_re, m_l1_a_im, m_l1_log_step, m_l1_b_re, m_l1_b_im, m_l1_c_re, m_l1_c_im, m_l1_d_skip, m_l1_w_glu, m_l1_b_glu, m_l1_w_out, m_l2_norm_g, m_l2_w_in, m_l2_q_norm_g, m_l2_w_uq, m_l2_kv_norm_g, m_l2_w_ukv, m_l2_w_out, m_l3_norm_g, m_l3_w_in, m_l3_ln_g, m_l3_ln_b, m_l3_w_s, m_l3_b_s, m_l3_w_out, m_final_norm_g, v_l0_norm_g, v_l0_w_in, v_l0_ln_g, v_l0_ln_b, v_l0_w_s, v_l0_b_s, v_l0_w_out, v_l1_norm_g, v_l1_w_in, v_l1_a_re, v_l1_a_im, v_l1_log_step, v_l1_b_re, v_l1_b_im, v_l1_c_re, v_l1_c_im, v_l1_d_skip, v_l1_w_glu, v_l1_b_glu, v_l1_w_out, v_l2_norm_g, v_l2_w_in, v_l2_q_norm_g, v_l2_w_uq, v_l2_kv_norm_g, v_l2_w_ukv, v_l2_w_out, v_l3_norm_g, v_l3_w_in, v_l3_ln_g, v_l3_ln_b, v_l3_w_s, v_l3_b_s, v_l3_w_out, v_final_norm_g):
    args = locals()
    w = {n: args[n] for n in WEIGHT_NAMES}
    mom_m = {n: args["m_" + n] for n in WEIGHT_NAMES}
    mom_v = {n: args["v_" + n] for n in WEIGHT_NAMES}
    h0 = x[0]
    target = loss_target[0]
    pos = positions.reshape(-1, 1)

    full = {}

    def pack_unit(layers):
        names = [n for n in BIG if int(n[1]) in layers]
        rows = [w[n].size // PACK_W for n in names]
        pad = -sum(rows) % PACK_ROW_ALIGN
        return names, rows, _pack_rows([w[n].astype(BF16) for n in names] + [jnp.zeros((pad, PACK_W), BF16)])

    def unpack_unit(names, rows, gathered):
        r0 = 0
        for n, nr in zip(names, rows):
            full[n] = _from_shard_major(n, gathered[:, r0:r0 + nr, :], w[n].shape)
            r0 += nr

    unit0, unit1, unit2 = pack_unit((0,)), pack_unit((1,)), pack_unit((2, 3))
    wp = dict(w)

    def layer_params(i):
        pre = "l%d_" % i
        p = {k[len(pre):]: v for k, v in wp.items() if k.startswith(pre)}
        wf = {k[len(pre):]: v for k, v in full.items() if k.startswith(pre)}
        return p, wf

    flight = gather_start(unit0[2], unit1[2], "gather_l0_start")
    cos, sins = rope_tables(pos, flight[4][0, 0])
    wp["l1_a_re"] = w["l1_a_re"] + flight[4][0, 0]
    s5_weights = _s5_weights(layer_params(1)[0])
    land = gather_wait(*flight[:4], s5_weights[2], "gather_l0_wait")
    got = gather_handover(land, unit0[2], "gather_l0_handover")
    unpack_unit(unit0[0], unit0[1], got)
    flight = gather_start(unit1[2], got, "gather_l1_start")
    wp["l0_norm_g"] = w["l0_norm_g"] + flight[4][0, 0]

    h = h0
    saved = []
    for i, kind in enumerate(LAYER_KINDS):
        if i == 1:
            land = gather_wait(*flight[:4], h, "gather_l1_wait")
            got = gather_handover(land, unit1[2], "gather_l1_handover")
            unpack_unit(unit1[0], unit1[1], got)
            flight = gather_start(unit2[2], got, "gather_l23_start")
            wp["l1_norm_g"] = w["l1_norm_g"] + flight[4][0, 0]
        if i == 2:
            land = gather_wait(*flight[:4], h, "gather_l23_wait")
            unpack_unit(unit2[0], unit2[1], gather_handover(land, unit2[2], "gather_l23_handover"))
        p, wf = layer_params(i)
        tag = "l%d" % i
        if kind == "gmlp":
            h, s = gmlp_layer_fwd(h, p, wf, tag)
        elif kind == "s5":
            h, s = s5_layer_fwd(h, p, wf, s5_weights, tag)
        else:
            h, s = mla_layer_fwd(h, p, wf, cos, sins, tag)
        saved.append(s)
    loss_part, dh, g_final = loss_head(h, final_norm_g, target)

    grads = {"final_norm_g": g_final.reshape(-1)}
    sinks = {}

    for i in reversed(range(len(LAYER_KINDS))):
        kind = LAYER_KINDS[i]
        p, wf = layer_params(i)
        tag = "l%d" % i
        sink = sinks[i] = BigGradSink(i, {n[3:]: w[n].shape for n in BIG if int(n[1]) == i})
        if kind == "gmlp":
            dh, g = gmlp_layer_bwd(dh, saved[i], p, wf, tag, sink)
        elif kind == "s5":
            dh, g = s5_layer_bwd(dh, saved[i], p, wf, tag, sink)
        else:
            dh, g = mla_layer_bwd(dh, saved[i], p, wf, cos, sins, tag, sink)
        for k, val in g.items():
            grads["l%d_%s" % (i, k)] = val
    grad_x = dh[None]

    n_small = sum(w[n].size for n in SMALL)
    piece = N_CHIPS * 2 * 16 * PACK_W
    n_small_pad = -(-(n_small + 1) // piece) * piece
    nrs = n_small_pad // N_CHIPS // PACK_W
    p_small = _small_pack([grads[n] for n in SMALL] + [loss_part], n_small_pad).reshape(N_CHIPS, nrs, PACK_W)
    sib_small, = grads_to_sibling([p_small], "grads_to_sibling_small")
    t_small = pair_sum(p_small, sib_small, F32, "pair_sum_small")
    rb_small, = grads_across_chips([t_small])
    halves = [chip_sum(t_small, rb_small, "chip_sum_small")]

    after = halves[0]
    for i in reversed(range(len(LAYER_KINDS))):
        t_i, rb_i = reduce_wait(*sinks[i].flight[:4], after, "reduce_l%d_wait" % i)
        halves.append(chip_sum(t_i, rb_i, "chip_sum_l%d" % i))
        after = halves[-1]
    reduced = reduced_to_sibling(halves)
    small_flight = bcast_start(reduced[0], reduced[1], "small_allgather_start")

    g_out, d_out, nm_out, nv_out = {}, {}, {}, {}
    for i, g_i in zip(reversed(range(len(LAYER_KINDS))), reduced[1:]):
        for wn, (r0, shape, direct) in sinks[i].regions.items():
            n = "l%d_%s" % (i, wn)
            if direct:
                g_out[n], d_out[n], nm_out[n], nv_out[n] = adamw_packed(w[n], g_i, r0, mom_m[n], mom_v[n], "adamw_" + n)
            else:
                g_out[n] = g_i[r0:r0 + shape[0] * shape[1] // CHUNK_W].reshape(shape)
                d_out[n], nm_out[n], nv_out[n] = adamw(w[n], g_out[n], mom_m[n], mom_v[n], "adamw_" + n)
    small_all = bcast_wait(*small_flight[:4], nv_out["l0_w_in"], "small_allgather_wait")
    g_small = small_all.reshape(-1, PACK_W)
    sp = lambda d: _small_pack([d[n] for n in SMALL], n_small_pad).reshape(-1, PACK_W)
    d_small, nm_small, nv_small = adamw(sp(w), g_small, sp(mom_m), sp(mom_v), "adamw_small")
    for buf, out in ((g_small, g_out), (d_small, d_out), (nm_small, nm_out), (nv_small, nv_out)):
        flat = buf.reshape(-1)
        o = 0
        for n in SMALL:
            out[n] = flat[o:o + w[n].size].reshape(w[n].shape)
            o += w[n].size
    loss = g_small.reshape(-1)[n_small]
    return (loss, grad_x, *[g_out[n] for n in WEIGHT_NAMES], *[d_out[n] for n in WEIGHT_NAMES],
            *[nm_out[n] for n in WEIGHT_NAMES], *[nv_out[n] for n in WEIGHT_NAMES])
```
